```python
import jax, jax.numpy as jnp
from jax import lax
import numpy as np


D_MODEL = 1024
BATCH = 8
SEQ = 8192
DEPTH = 1

N_Q_HEADS = 16
N_KV_HEADS = 4
HEAD_DIM = 64
WINDOW = 128
ATTN_BLOCK = 128
ROPE_THETA = 500000.0
ROPE_DIM = HEAD_DIM // 4
HGRN_HEADS = 8
HGRN_DK = 128
HGRN_DV = 128
HGRN_CHUNK = 64
D_FF = 2816
ATTN_WIDTH = N_Q_HEADS * HEAD_DIM
KV_WIDTH = N_KV_HEADS * HEAD_DIM
HGRN_KWIDTH = HGRN_HEADS * HGRN_DK
HGRN_VWIDTH = HGRN_HEADS * HGRN_DV
IN_WIDTHS = (ATTN_WIDTH, KV_WIDTH, KV_WIDTH, HGRN_KWIDTH, HGRN_KWIDTH, HGRN_VWIDTH, HGRN_VWIDTH, D_MODEL, D_MODEL)
D_IN = sum(IN_WIDTHS)
DEEPNORM_ALPHA = (2 * DEPTH) ** 0.25
DEEPNORM_BETA = (8 * DEPTH) ** -0.25
LN_EPS = 1e-5
RMS_EPS = 1e-6
NEG_INF = -1e30

kernel_name = 'hybrid_swa_sink_hgrn2_macaron_deepnorm'


def layer_norm(x, g, b):
    xf = x.astype(jnp.float32)
    mu = jnp.mean(xf, axis=-1, keepdims=True)
    var = jnp.mean(jnp.square(xf - mu), axis=-1, keepdims=True)
    y = (xf - mu) * lax.rsqrt(var + LN_EPS) * g.astype(jnp.float32) + b.astype(jnp.float32)
    return y.astype(x.dtype)


def swiglu(x, w1, w3, w2):
    return (jax.nn.silu(x @ w1) * (x @ w3)) @ w2


def rope_tables(seq_len):
    pos = jnp.arange(seq_len, dtype=jnp.float32)
    inv_freq = ROPE_THETA ** (-jnp.arange(0, ROPE_DIM, 2, dtype=jnp.float32) / ROPE_DIM)
    ang = pos[:, None] * inv_freq[None, :]
    return jnp.cos(ang)[None, :, None, :], jnp.sin(ang)[None, :, None, :]


def partial_rope(t, cos, sin):
    tf = t.astype(jnp.float32)
    half = ROPE_DIM // 2
    t1, t2, rest = tf[..., :half], tf[..., half:ROPE_DIM], tf[..., ROPE_DIM:]
    rot = jnp.concatenate([t1 * cos - t2 * sin, t2 * cos + t1 * sin, rest], axis=-1)
    return rot.astype(t.dtype)


def sliding_window_attention(q, k, v, sinks):
    B, S = q.shape[0], q.shape[1]
    nb = S // ATTN_BLOCK
    grp = N_Q_HEADS // N_KV_HEADS
    qb = q.reshape(B, nb, ATTN_BLOCK, N_KV_HEADS, grp, HEAD_DIM)

    def band(t):
        tp = jnp.pad(t, ((0, 0), (ATTN_BLOCK, 0), (0, 0), (0, 0)))
        tp = tp.reshape(B, nb + 1, ATTN_BLOCK, N_KV_HEADS, HEAD_DIM)
        return jnp.concatenate([tp[:, :-1], tp[:, 1:]], axis=2)

    kb, vb = band(k), band(v)
    scores = jnp.einsum('bnqkgd,bnskd->bnkgqs', qb, kb).astype(jnp.float32) * (HEAD_DIM ** -0.5)
    qi = jnp.arange(ATTN_BLOCK)[:, None]
    kj = jnp.arange(2 * ATTN_BLOCK)[None, :]
    blk = jnp.arange(nb)[:, None, None]
    dist = qi + ATTN_BLOCK - kj
    mask = (dist >= 0) & (dist < WINDOW) & (blk * ATTN_BLOCK + kj - ATTN_BLOCK >= 0)
    scores = jnp.where(mask[None, :, None, None], scores, NEG_INF)
    sink = sinks.astype(jnp.float32).reshape(1, 1, N_KV_HEADS, grp, 1, 1)
    m = jnp.maximum(jnp.max(scores, axis=-1, keepdims=True), sink)
    p = jnp.exp(scores - m)
    denom = jnp.sum(p, axis=-1, keepdims=True) + jnp.exp(sink - m)
    probs = (p / denom).astype(v.dtype)
    out = jnp.einsum('bnkgqs,bnskd->bnqkgd', probs, vb)
    return out.reshape(B, S, ATTN_WIDTH)


def hgrn2_recurrence(q, f_logit, v, lb):
    B, S = q.shape[0], q.shape[1]
    nc = S // HGRN_CHUNK
    lb = lb.reshape(HGRN_HEADS, HGRN_DK)
    f = lb + (1.0 - lb) * jax.nn.sigmoid(f_logit.astype(jnp.float32))
    k = 1.0 - f

    def chunks(t):
        return t.reshape(B, nc, HGRN_CHUNK, HGRN_HEADS, t.shape[-1]).transpose(0, 3, 1, 2, 4)

    qc, kc, vc = chunks(q.astype(jnp.float32)), chunks(k), chunks(v.astype(jnp.float32))
    gc = jnp.cumsum(chunks(jnp.log(f)), axis=3)
    g_last = gc[:, :, :, -1:, :]
    q_dec = qc * jnp.exp(gc)
    k_inv = kc * jnp.exp(-gc)
    k_end = kc * jnp.exp(g_last - gc)
    causal = jnp.tril(jnp.ones((HGRN_CHUNK, HGRN_CHUNK), dtype=bool))
    scores = jnp.where(causal, jnp.einsum('bhntd,bhnsd->bhnts', q_dec, k_inv), 0.0)
    o_intra = jnp.einsum('bhnts,bhnse->bhnte', scores, vc)
    upd = jnp.einsum('bhnsd,bhnse->bhnde', k_end, vc)
    decay = jnp.exp(g_last[:, :, :, 0, :])

    def step(state, inp):
        a_n, u_n = inp
        return state * a_n[..., None] + u_n, state

    s0 = jnp.zeros((B, HGRN_HEADS, HGRN_DK, HGRN_DV), jnp.float32)
    _, s_start = lax.scan(step, s0, (jnp.moveaxis(decay, 2, 0), jnp.moveaxis(upd, 2, 0)))
    s_start = jnp.moveaxis(s_start, 0, 2)
    o = o_intra + jnp.einsum('bhntd,bhnde->bhnte', q_dec, s_start)
    return o.transpose(0, 2, 3, 1, 4).reshape(B, S, HGRN_HEADS, HGRN_DV)


def token_mixer(h, w_in, b_in, sinks, lb, norm_g, w_pa, w_ph, w_out, cos, sin):
    B, S = h.shape[0], h.shape[1]
    proj = h @ w_in + b_in
    splits = np.cumsum(IN_WIDTHS)[:-1].tolist()
    q_a, k_a, v_a, f_h, q_h, i_h, og_h, gate_a, gate_h = jnp.split(proj, splits, axis=-1)
    q_a = partial_rope(q_a.reshape(B, S, N_Q_HEADS, HEAD_DIM), cos, sin)
    k_a = partial_rope(k_a.reshape(B, S, N_KV_HEADS, HEAD_DIM), cos, sin)
    v_a = v_a.reshape(B, S, N_KV_HEADS, HEAD_DIM)
    y_attn = sliding_window_attention(q_a, k_a, v_a, sinks)
    q_h = jax.nn.silu(q_h).reshape(B, S, HGRN_HEADS, HGRN_DK)
    o_h = hgrn2_recurrence(q_h, f_h.reshape(B, S, HGRN_HEADS, HGRN_DK),
                           i_h.reshape(B, S, HGRN_HEADS, HGRN_DV), lb)
    o_h = o_h * lax.rsqrt(jnp.mean(jnp.square(o_h), axis=-1, keepdims=True) + RMS_EPS) * norm_g.astype(jnp.float32)
    y_hgrn = o_h.reshape(B, S, HGRN_VWIDTH).astype(h.dtype) * jax.nn.silu(og_h)
    merged = jax.nn.sigmoid(gate_a) * (y_attn @ w_pa) + jax.nn.sigmoid(gate_h) * (y_hgrn @ w_ph)
    return merged @ w_out


def _fwd_setup_inputs(seed: int = 0) -> dict:
    key = jax.random.key(seed)
    ks = jax.random.split(key, 24)

    def nrm(k, shape, scale):
        return jax.random.normal(k, shape, jnp.float32) * scale

    D = D_MODEL
    return {
        'x': nrm(ks[0], (BATCH, SEQ, D), 1.0),
        'ln1_g': 1.0 + nrm(ks[1], (DEPTH, D), 0.02),
        'ln1_b': nrm(ks[2], (DEPTH, D), 0.02),
        'ffn1_w1': nrm(ks[3], (DEPTH, D, D_FF), D ** -0.5),
        'ffn1_w3': nrm(ks[4], (DEPTH, D, D_FF), D ** -0.5),
        'ffn1_w2': nrm(ks[5], (DEPTH, D_FF, D), D_FF ** -0.5 * DEEPNORM_BETA),
        'ln2_g': 1.0 + nrm(ks[6], (DEPTH, D), 0.02),
        'ln2_b': nrm(ks[7], (DEPTH, D), 0.02),
        'w_in': nrm(ks[8], (DEPTH, D, D_IN), D ** -0.5),
        'b_in': nrm(ks[9], (DEPTH, D_IN), 0.02),
        'attn_sinks': nrm(ks[10], (DEPTH, N_Q_HEADS), 0.5),
        'hgrn_lb_logits': nrm(ks[11], (DEPTH + 1, HGRN_KWIDTH), 0.1),
        'hgrn_norm_g': 1.0 + nrm(ks[12], (DEPTH, HGRN_DV), 0.02),
        'w_proj_attn': nrm(ks[13], (DEPTH, ATTN_WIDTH, D), ATTN_WIDTH ** -0.5 * DEEPNORM_BETA),
        'w_proj_hgrn': nrm(ks[14], (DEPTH, HGRN_VWIDTH, D), HGRN_VWIDTH ** -0.5 * DEEPNORM_BETA),
        'w_out': nrm(ks[15], (DEPTH, D, D), D ** -0.5 * DEEPNORM_BETA),
        'ln3_g': 1.0 + nrm(ks[16], (DEPTH, D), 0.02),
        'ln3_b': nrm(ks[17], (DEPTH, D), 0.02),
        'ffn2_w1': nrm(ks[18], (DEPTH, D, D_FF), D ** -0.5),
        'ffn2_w3': nrm(ks[19], (DEPTH, D, D_FF), D ** -0.5),
        'ffn2_w2': nrm(ks[20], (DEPTH, D_FF, D), D_FF ** -0.5 * DEEPNORM_BETA),
    }


def _fwd_reference(x, ln1_g, ln1_b, ffn1_w1, ffn1_w3, ffn1_w2, ln2_g, ln2_b, w_in, b_in,
              attn_sinks, hgrn_lb_logits, hgrn_norm_g, w_proj_attn, w_proj_hgrn, w_out,
              ln3_g, ln3_b, ffn2_w1, ffn2_w3, ffn2_w2):
    cos, sin = rope_tables(x.shape[1])
    lb_all = jnp.cumsum(jax.nn.softmax(hgrn_lb_logits.astype(jnp.float32), axis=0), axis=0)
    for l in range(DEPTH):
        x = layer_norm(DEEPNORM_ALPHA * x + 0.5 * swiglu(x, ffn1_w1[l], ffn1_w3[l], ffn1_w2[l]),
                       ln1_g[l], ln1_b[l])
        mix = token_mixer(x, w_in[l], b_in[l], attn_sinks[l], lb_all[l], hgrn_norm_g[l],
                          w_proj_attn[l], w_proj_hgrn[l], w_out[l], cos, sin)
        x = layer_norm(DEEPNORM_ALPHA * x + mix, ln2_g[l], ln2_b[l])
        x = layer_norm(DEEPNORM_ALPHA * x + 0.5 * swiglu(x, ffn2_w1[l], ffn2_w3[l], ffn2_w2[l]),
                       ln3_g[l], ln3_b[l])
    return x


import jax as _jax
import jax.numpy as _jnp

TWIN_FORMAT = 'train_step'
FWD_PARAMS = ['x', 'ln1_g', 'ln1_b', 'ffn1_w1', 'ffn1_w3', 'ffn1_w2', 'ln2_g', 'ln2_b', 'w_in', 'b_in', 'attn_sinks', 'hgrn_lb_logits', 'hgrn_norm_g', 'w_proj_attn', 'w_proj_hgrn', 'w_out', 'ln3_g', 'ln3_b', 'ffn2_w1', 'ffn2_w3', 'ffn2_w2']
TWIN_WEIGHTS = ['ln1_g', 'ln1_b', 'ffn1_w1', 'ffn1_w3', 'ffn1_w2', 'ln2_g', 'ln2_b', 'w_in', 'b_in', 'attn_sinks', 'hgrn_lb_logits', 'hgrn_norm_g', 'w_proj_attn', 'w_proj_hgrn', 'w_out', 'ln3_g', 'ln3_b', 'ffn2_w1', 'ffn2_w3', 'ffn2_w2']
TWIN_DIFF_INPUT = 'x'
TWIN_INPUTS = ['x', 'ln1_g', 'ln1_b', 'ffn1_w1', 'ffn1_w3', 'ffn1_w2', 'ln2_g', 'ln2_b', 'w_in', 'b_in', 'attn_sinks', 'hgrn_lb_logits', 'hgrn_norm_g', 'w_proj_attn', 'w_proj_hgrn', 'w_out', 'ln3_g', 'ln3_b', 'ffn2_w1', 'ffn2_w3', 'ffn2_w2', 'loss_target', 'm_ln1_g', 'm_ln1_b', 'm_ffn1_w1', 'm_ffn1_w3', 'm_ffn1_w2', 'm_ln2_g', 'm_ln2_b', 'm_w_in', 'm_b_in', 'm_attn_sinks', 'm_hgrn_lb_logits', 'm_hgrn_norm_g', 'm_w_proj_attn', 'm_w_proj_hgrn', 'm_w_out', 'm_ln3_g', 'm_ln3_b', 'm_ffn2_w1', 'm_ffn2_w3', 'm_ffn2_w2', 'v_ln1_g', 'v_ln1_b', 'v_ffn1_w1', 'v_ffn1_w3', 'v_ffn1_w2', 'v_ln2_g', 'v_ln2_b', 'v_w_in', 'v_b_in', 'v_attn_sinks', 'v_hgrn_lb_logits', 'v_hgrn_norm_g', 'v_w_proj_attn', 'v_w_proj_hgrn', 'v_w_out', 'v_ln3_g', 'v_ln3_b', 'v_ffn2_w1', 'v_ffn2_w3', 'v_ffn2_w2']
TWIN_OUTPUTS = ['loss', 'grad_x', 'grad_ln1_g', 'grad_ln1_b', 'grad_ffn1_w1', 'grad_ffn1_w3', 'grad_ffn1_w2', 'grad_ln2_g', 'grad_ln2_b', 'grad_w_in', 'grad_b_in', 'grad_attn_sinks', 'grad_hgrn_lb_logits', 'grad_hgrn_norm_g', 'grad_w_proj_attn', 'grad_w_proj_hgrn', 'grad_w_out', 'grad_ln3_g', 'grad_ln3_b', 'grad_ffn2_w1', 'grad_ffn2_w3', 'grad_ffn2_w2', 'delta_ln1_g', 'delta_ln1_b', 'delta_ffn1_w1', 'delta_ffn1_w3', 'delta_ffn1_w2', 'delta_ln2_g', 'delta_ln2_b', 'delta_w_in', 'delta_b_in', 'delta_attn_sinks', 'delta_hgrn_lb_logits', 'delta_hgrn_norm_g', 'delta_w_proj_attn', 'delta_w_proj_hgrn', 'delta_w_out', 'delta_ln3_g', 'delta_ln3_b', 'delta_ffn2_w1', 'delta_ffn2_w3', 'delta_ffn2_w2', 'new_m_ln1_g', 'new_m_ln1_b', 'new_m_ffn1_w1', 'new_m_ffn1_w3', 'new_m_ffn1_w2', 'new_m_ln2_g', 'new_m_ln2_b', 'new_m_w_in', 'new_m_b_in', 'new_m_attn_sinks', 'new_m_hgrn_lb_logits', 'new_m_hgrn_norm_g', 'new_m_w_proj_attn', 'new_m_w_proj_hgrn', 'new_m_w_out', 'new_m_ln3_g', 'new_m_ln3_b', 'new_m_ffn2_w1', 'new_m_ffn2_w3', 'new_m_ffn2_w2', 'new_v_ln1_g', 'new_v_ln1_b', 'new_v_ffn1_w1', 'new_v_ffn1_w3', 'new_v_ffn1_w2', 'new_v_ln2_g', 'new_v_ln2_b', 'new_v_w_in', 'new_v_b_in', 'new_v_attn_sinks', 'new_v_hgrn_lb_logits', 'new_v_hgrn_norm_g', 'new_v_w_proj_attn', 'new_v_w_proj_hgrn', 'new_v_w_out', 'new_v_ln3_g', 'new_v_ln3_b', 'new_v_ffn2_w1', 'new_v_ffn2_w3', 'new_v_ffn2_w2']
TWIN_LEAF_KINDS = {'loss': 'loss', 'grad_x': 'grad_x', 'grad_ln1_g': 'grad_w', 'grad_ln1_b': 'grad_w', 'grad_ffn1_w1': 'grad_w', 'grad_ffn1_w3': 'grad_w', 'grad_ffn1_w2': 'grad_w', 'grad_ln2_g': 'grad_w', 'grad_ln2_b': 'grad_w', 'grad_w_in': 'grad_w', 'grad_b_in': 'grad_w', 'grad_attn_sinks': 'grad_w', 'grad_hgrn_lb_logits': 'grad_w', 'grad_hgrn_norm_g': 'grad_w', 'grad_w_proj_attn': 'grad_w', 'grad_w_proj_hgrn': 'grad_w', 'grad_w_out': 'grad_w', 'grad_ln3_g': 'grad_w', 'grad_ln3_b': 'grad_w', 'grad_ffn2_w1': 'grad_w', 'grad_ffn2_w3': 'grad_w', 'grad_ffn2_w2': 'grad_w', 'delta_ln1_g': 'delta_w', 'delta_ln1_b': 'delta_w', 'delta_ffn1_w1': 'delta_w', 'delta_ffn1_w3': 'delta_w', 'delta_ffn1_w2': 'delta_w', 'delta_ln2_g': 'delta_w', 'delta_ln2_b': 'delta_w', 'delta_w_in': 'delta_w', 'delta_b_in': 'delta_w', 'delta_attn_sinks': 'delta_w', 'delta_hgrn_lb_logits': 'delta_w', 'delta_hgrn_norm_g': 'delta_w', 'delta_w_proj_attn': 'delta_w', 'delta_w_proj_hgrn': 'delta_w', 'delta_w_out': 'delta_w', 'delta_ln3_g': 'delta_w', 'delta_ln3_b': 'delta_w', 'delta_ffn2_w1': 'delta_w', 'delta_ffn2_w3': 'delta_w', 'delta_ffn2_w2': 'delta_w', 'new_m_ln1_g': 'new_m', 'new_m_ln1_b': 'new_m', 'new_m_ffn1_w1': 'new_m', 'new_m_ffn1_w3': 'new_m', 'new_m_ffn1_w2': 'new_m', 'new_m_ln2_g': 'new_m', 'new_m_ln2_b': 'new_m', 'new_m_w_in': 'new_m', 'new_m_b_in': 'new_m', 'new_m_attn_sinks': 'new_m', 'new_m_hgrn_lb_logits': 'new_m', 'new_m_hgrn_norm_g': 'new_m', 'new_m_w_proj_attn': 'new_m', 'new_m_w_proj_hgrn': 'new_m', 'new_m_w_out': 'new_m', 'new_m_ln3_g': 'new_m', 'new_m_ln3_b': 'new_m', 'new_m_ffn2_w1': 'new_m', 'new_m_ffn2_w3': 'new_m', 'new_m_ffn2_w2': 'new_m', 'new_v_ln1_g': 'new_v', 'new_v_ln1_b': 'new_v', 'new_v_ffn1_w1': 'new_v', 'new_v_ffn1_w3': 'new_v', 'new_v_ffn1_w2': 'new_v', 'new_v_ln2_g': 'new_v', 'new_v_ln2_b': 'new_v', 'new_v_w_in': 'new_v', 'new_v_b_in': 'new_v', 'new_v_attn_sinks': 'new_v', 'new_v_hgrn_lb_logits': 'new_v', 'new_v_hgrn_norm_g': 'new_v', 'new_v_w_proj_attn': 'new_v', 'new_v_w_proj_hgrn': 'new_v', 'new_v_w_out': 'new_v', 'new_v_ln3_g': 'new_v', 'new_v_ln3_b': 'new_v', 'new_v_ffn2_w1': 'new_v', 'new_v_ffn2_w3': 'new_v', 'new_v_ffn2_w2': 'new_v'}


def _forward(args):
    return _fwd_reference(*[args[k] for k in FWD_PARAMS])


def _output_shape():
    out = _jax.eval_shape(lambda: _forward(_fwd_setup_inputs(0)))
    return out.shape, out.dtype

N_MICROBATCH = 1
ADAM_LR = 0.001
ADAM_B1 = 0.9
ADAM_B2 = 0.999
ADAM_EPS = 1e-08
ADAM_WD = 0.01
ADAM_STEP = 10
PER_EXAMPLE_BATCH_AXIS = {'x': 0, 'loss_target': 0}
SHARED_INPUTS = []
_WEIGHT_DTYPES = {'ln1_g': _jnp.float32, 'ln1_b': _jnp.float32, 'ffn1_w1': _jnp.float32, 'ffn1_w3': _jnp.float32, 'ffn1_w2': _jnp.float32, 'ln2_g': _jnp.float32, 'ln2_b': _jnp.float32, 'w_in': _jnp.float32, 'b_in': _jnp.float32, 'attn_sinks': _jnp.float32, 'hgrn_lb_logits': _jnp.float32, 'hgrn_norm_g': _jnp.float32, 'w_proj_attn': _jnp.float32, 'w_proj_hgrn': _jnp.float32, 'w_out': _jnp.float32, 'ln3_g': _jnp.float32, 'ln3_b': _jnp.float32, 'ffn2_w1': _jnp.float32, 'ffn2_w3': _jnp.float32, 'ffn2_w2': _jnp.float32}
MOMENT_SCALE = {'ln1_g': 1.940681e+00, 'ln1_b': 7.176994e-01, 'ffn1_w1': 2.340945e-02, 'ffn1_w3': 2.275077e-02, 'ffn1_w2': 6.324974e-02, 'ln2_g': 1.978312e+00, 'ln2_b': 7.318569e-01, 'w_in': 1.423607e-02, 'b_in': 3.632105e-02, 'attn_sinks': 4.608703e-03, 'hgrn_lb_logits': 2.416888e-03, 'hgrn_norm_g': 7.444128e-02, 'w_proj_attn': 1.421682e-02, 'w_proj_hgrn': 4.084535e-02, 'w_out': 4.325433e-02, 'ln3_g': 6.405518e+01, 'ln3_b': 1.429588e+00, 'ffn2_w1': 2.302583e-02, 'ffn2_w3': 2.238892e-02, 'ffn2_w2': 6.245136e-02}


def _to_microbatches(a, axis):
    t = _jnp.moveaxis(a, axis, 0)
    t = t.reshape((N_MICROBATCH, t.shape[0] // N_MICROBATCH) + t.shape[1:])
    return _jnp.moveaxis(t, 1, axis + 1)


def setup_inputs(seed: int = 0) -> dict:
    inp = _fwd_setup_inputs(seed)
    key = _jax.random.fold_in(_jax.random.key(seed), 7919)
    shape, _ = _output_shape()
    out = dict(inp)
    out["loss_target"] = _jax.random.normal(_jax.random.fold_in(key, 0), shape, _jnp.float32)
    for i, name in enumerate(TWIN_WEIGHTS):
        w = inp[name].astype(_jnp.float32)
        if MOMENT_SCALE is None:
            s = _jnp.sqrt(_jnp.mean(_jnp.square(w)) + 1e-30)
        else:
            s = MOMENT_SCALE[name]
        km, kv = _jax.random.split(_jax.random.fold_in(key, i + 1))
        out[name] = w
        out["m_" + name] = s * _jax.random.normal(km, w.shape, _jnp.float32)
        out["v_" + name] = (s * s) * _jax.random.uniform(kv, w.shape, _jnp.float32, 0.5, 1.5)
    if N_MICROBATCH > 1:
        for name, axis in PER_EXAMPLE_BATCH_AXIS.items():
            out[name] = _to_microbatches(out[name], axis)
    return {'x': out['x'], 'ln1_g': out['ln1_g'], 'ln1_b': out['ln1_b'], 'ffn1_w1': out['ffn1_w1'], 'ffn1_w3': out['ffn1_w3'], 'ffn1_w2': out['ffn1_w2'], 'ln2_g': out['ln2_g'], 'ln2_b': out['ln2_b'], 'w_in': out['w_in'], 'b_in': out['b_in'], 'attn_sinks': out['attn_sinks'], 'hgrn_lb_logits': out['hgrn_lb_logits'], 'hgrn_norm_g': out['hgrn_norm_g'], 'w_proj_attn': out['w_proj_attn'], 'w_proj_hgrn': out['w_proj_hgrn'], 'w_out': out['w_out'], 'ln3_g': out['ln3_g'], 'ln3_b': out['ln3_b'], 'ffn2_w1': out['ffn2_w1'], 'ffn2_w3': out['ffn2_w3'], 'ffn2_w2': out['ffn2_w2'], 'loss_target': out['loss_target'], 'm_ln1_g': out['m_ln1_g'], 'm_ln1_b': out['m_ln1_b'], 'm_ffn1_w1': out['m_ffn1_w1'], 'm_ffn1_w3': out['m_ffn1_w3'], 'm_ffn1_w2': out['m_ffn1_w2'], 'm_ln2_g': out['m_ln2_g'], 'm_ln2_b': out['m_ln2_b'], 'm_w_in': out['m_w_in'], 'm_b_in': out['m_b_in'], 'm_attn_sinks': out['m_attn_sinks'], 'm_hgrn_lb_logits': out['m_hgrn_lb_logits'], 'm_hgrn_norm_g': out['m_hgrn_norm_g'], 'm_w_proj_attn': out['m_w_proj_attn'], 'm_w_proj_hgrn': out['m_w_proj_hgrn'], 'm_w_out': out['m_w_out'], 'm_ln3_g': out['m_ln3_g'], 'm_ln3_b': out['m_ln3_b'], 'm_ffn2_w1': out['m_ffn2_w1'], 'm_ffn2_w3': out['m_ffn2_w3'], 'm_ffn2_w2': out['m_ffn2_w2'], 'v_ln1_g': out['v_ln1_g'], 'v_ln1_b': out['v_ln1_b'], 'v_ffn1_w1': out['v_ffn1_w1'], 'v_ffn1_w3': out['v_ffn1_w3'], 'v_ffn1_w2': out['v_ffn1_w2'], 'v_ln2_g': out['v_ln2_g'], 'v_ln2_b': out['v_ln2_b'], 'v_w_in': out['v_w_in'], 'v_b_in': out['v_b_in'], 'v_attn_sinks': out['v_attn_sinks'], 'v_hgrn_lb_logits': out['v_hgrn_lb_logits'], 'v_hgrn_norm_g': out['v_hgrn_norm_g'], 'v_w_proj_attn': out['v_w_proj_attn'], 'v_w_proj_hgrn': out['v_w_proj_hgrn'], 'v_w_out': out['v_w_out'], 'v_ln3_g': out['v_ln3_g'], 'v_ln3_b': out['v_ln3_b'], 'v_ffn2_w1': out['v_ffn2_w1'], 'v_ffn2_w3': out['v_ffn2_w3'], 'v_ffn2_w2': out['v_ffn2_w2']}


def _loss(weights, diff, rest, loss_target):
    with _jax.named_scope("forward"):
        args = {**rest, TWIN_DIFF_INPUT: diff, **{k: w.astype(_WEIGHT_DTYPES[k]) for k, w in weights.items()}}
        y = _forward(args)
    with _jax.named_scope("loss_head"):
        err = _jnp.square(y.astype(_jnp.float32) - loss_target)
        return 0.5 * _jnp.sum(_jnp.mean(err, axis=-1)) if err.ndim else 0.5 * err


def _adamw(w, g, m, v):
    m = ADAM_B1 * m + (1.0 - ADAM_B1) * g
    v = ADAM_B2 * v + (1.0 - ADAM_B2) * _jnp.square(g)
    m_hat = m / (1.0 - ADAM_B1 ** ADAM_STEP)
    v_hat = v / (1.0 - ADAM_B2 ** ADAM_STEP)
    delta = -ADAM_LR * (m_hat / (_jnp.sqrt(v_hat) + ADAM_EPS) + ADAM_WD * w)
    return delta, m, v


def reference(x, ln1_g, ln1_b, ffn1_w1, ffn1_w3, ffn1_w2, ln2_g, ln2_b, w_in, b_in, attn_sinks, hgrn_lb_logits, hgrn_norm_g, w_proj_attn, w_proj_hgrn, w_out, ln3_g, ln3_b, ffn2_w1, ffn2_w3, ffn2_w2, loss_target, m_ln1_g, m_ln1_b, m_ffn1_w1, m_ffn1_w3, m_ffn1_w2, m_ln2_g, m_ln2_b, m_w_in, m_b_in, m_attn_sinks, m_hgrn_lb_logits, m_hgrn_norm_g, m_w_proj_attn, m_w_proj_hgrn, m_w_out, m_ln3_g, m_ln3_b, m_ffn2_w1, m_ffn2_w3, m_ffn2_w2, v_ln1_g, v_ln1_b, v_ffn1_w1, v_ffn1_w3, v_ffn1_w2, v_ln2_g, v_ln2_b, v_w_in, v_b_in, v_attn_sinks, v_hgrn_lb_logits, v_hgrn_norm_g, v_w_proj_attn, v_w_proj_hgrn, v_w_out, v_ln3_g, v_ln3_b, v_ffn2_w1, v_ffn2_w3, v_ffn2_w2):
    given = dict(x=x, ln1_g=ln1_g, ln1_b=ln1_b, ffn1_w1=ffn1_w1, ffn1_w3=ffn1_w3, ffn1_w2=ffn1_w2, ln2_g=ln2_g, ln2_b=ln2_b, w_in=w_in, b_in=b_in, attn_sinks=attn_sinks, hgrn_lb_logits=hgrn_lb_logits, hgrn_norm_g=hgrn_norm_g, w_proj_attn=w_proj_attn, w_proj_hgrn=w_proj_hgrn, w_out=w_out, ln3_g=ln3_g, ln3_b=ln3_b, ffn2_w1=ffn2_w1, ffn2_w3=ffn2_w3, ffn2_w2=ffn2_w2, loss_target=loss_target, m_ln1_g=m_ln1_g, m_ln1_b=m_ln1_b, m_ffn1_w1=m_ffn1_w1, m_ffn1_w3=m_ffn1_w3, m_ffn1_w2=m_ffn1_w2, m_ln2_g=m_ln2_g, m_ln2_b=m_ln2_b, m_w_in=m_w_in, m_b_in=m_b_in, m_attn_sinks=m_attn_sinks, m_hgrn_lb_logits=m_hgrn_lb_logits, m_hgrn_norm_g=m_hgrn_norm_g, m_w_proj_attn=m_w_proj_attn, m_w_proj_hgrn=m_w_proj_hgrn, m_w_out=m_w_out, m_ln3_g=m_ln3_g, m_ln3_b=m_ln3_b, m_ffn2_w1=m_ffn2_w1, m_ffn2_w3=m_ffn2_w3, m_ffn2_w2=m_ffn2_w2, v_ln1_g=v_ln1_g, v_ln1_b=v_ln1_b, v_ffn1_w1=v_ffn1_w1, v_ffn1_w3=v_ffn1_w3, v_ffn1_w2=v_ffn1_w2, v_ln2_g=v_ln2_g, v_ln2_b=v_ln2_b, v_w_in=v_w_in, v_b_in=v_b_in, v_attn_sinks=v_attn_sinks, v_hgrn_lb_logits=v_hgrn_lb_logits, v_hgrn_norm_g=v_hgrn_norm_g, v_w_proj_attn=v_w_proj_attn, v_w_proj_hgrn=v_w_proj_hgrn, v_w_out=v_w_out, v_ln3_g=v_ln3_g, v_ln3_b=v_ln3_b, v_ffn2_w1=v_ffn2_w1, v_ffn2_w3=v_ffn2_w3, v_ffn2_w2=v_ffn2_w2)
    weights = {n: given[n] for n in TWIN_WEIGHTS}
    shared = {n: given[n] for n in SHARED_INPUTS}
    per_example = {n: given[n] for n in ['x']}
    grad_fn = _jax.value_and_grad(_loss, argnums=(0, 1))

    def one_microbatch(ex, loss_target):
        ex = dict(ex)
        diff = ex.pop(TWIN_DIFF_INPUT)
        return grad_fn(weights, diff, {**shared, **ex}, loss_target)

    if N_MICROBATCH == 1:
        loss, (grad_w, grad_x) = one_microbatch(per_example, given["loss_target"])
    else:
        def body(carry, xs):
            loss_sum, grad_sum = carry
            l_k, (gw_k, gx_k) = one_microbatch(xs[0], xs[1])
            with _jax.named_scope("update"):
                return (loss_sum + l_k, _jax.tree.map(_jnp.add, grad_sum, gw_k)), gx_k

        init = (_jnp.zeros((), _jnp.float32), _jax.tree.map(_jnp.zeros_like, weights))
        (loss, grad_w), grad_x = _jax.lax.scan(body, init, (per_example, given["loss_target"]))
    with _jax.named_scope("update"):
        delta_w, new_m, new_v = {}, {}, {}
        for n in TWIN_WEIGHTS:
            delta_w[n], new_m[n], new_v[n] = _adamw(weights[n], grad_w[n], given["m_" + n], given["v_" + n])
    return (loss, grad_x, *[grad_w[n] for n in TWIN_WEIGHTS], *[delta_w[n] for n in TWIN_WEIGHTS],
            *[new_m[n] for n in TWIN_WEIGHTS], *[new_v[n] for n in TWIN_WEIGHTS])
```

```python
import jax
import jax.numpy as jnp
from jax import lax
from jax.experimental import pallas as pl
from jax.experimental.pallas import tpu as pltpu

F32 = jnp.float32
BF16 = jnp.bfloat16

NDEV = 8
D = 1024
DFF = 2816
RF = DFF // NDEV
DIN = 7680
RIN = DIN // NDEV
RP = D // NDEV
N_Q_HEADS = 16
N_KV_HEADS = 4
HEAD_DIM = 64
ATTN_BLOCK = 128
ROPE_THETA = 500000.0
ROPE_DIM = HEAD_DIM // 4
HGRN_HEADS = 8
HGRN_DK = 128
HGRN_CHUNK = 64
ALPHA = 2.0 ** 0.25
LN_EPS = 1e-5
RMS_EPS = 1e-6
NEG_INF = -1e30
ADAM_LR = 0.001
ADAM_B1 = 0.9
ADAM_B2 = 0.999
ADAM_EPS = 1e-08
ADAM_WD = 0.01
ADAM_STEP = 10

QKV_W = 1536
HG_W = 4096
GATE_W = 2048
VMEM_LIMIT = 60 * 2 ** 20

MESH = pl.DeviceIdType.MESH
HBM_SPEC = pl.BlockSpec(memory_space=pltpu.HBM)


def _params(*sem):
    return pltpu.CompilerParams(dimension_semantics=sem, vmem_limit_bytes=VMEM_LIMIT)


def _dot(a, b):
    return jnp.dot(a, b, preferred_element_type=F32)


def _dot_nt(a, b):
    return lax.dot_general(a, b, (((1,), (1,)), ((), ())), preferred_element_type=F32)


def _dot_tn(a, b):
    return lax.dot_general(a, b, (((0,), (0,)), ((), ())), preferred_element_type=F32)


def _sigmoid(x):
    return 1.0 / (1.0 + jnp.exp(-x))


def _ln_fwd(z):
    mu = jnp.mean(z, axis=-1, keepdims=True)
    zc = z - mu
    var = jnp.mean(zc * zc, axis=-1, keepdims=True)
    r = lax.rsqrt(var + LN_EPS)
    return zc * r, r


def _ln_bwd(dy, xh, r, g):
    dxh = dy * g
    m1 = jnp.mean(dxh, axis=-1, keepdims=True)
    m2 = jnp.mean(dxh * xh, axis=-1, keepdims=True)
    dz = r * (dxh - m1 - xh * m2)
    return dz, jnp.sum(dy * xh, axis=0, keepdims=True), jnp.sum(dy, axis=0, keepdims=True)


def _load_rows(wall_ref, pieces, sem):
    copies = []
    for dst, off, r in pieces:
        for k in range(NDEV):
            c = pltpu.make_async_copy(wall_ref.at[k, pl.ds(off, r), :], dst.at[pl.ds(k * r, r), :], sem.at[len(copies)])
            c.start()
            copies.append(c)
    for c in copies:
        c.wait()


def _all_gather(name, shard):
    rows, cols = shard.shape

    def body(x_ref, out_ref, send_sems, recv_sems, local_sem):
        x, y, c = lax.axis_index("x"), lax.axis_index("y"), lax.axis_index("c")
        me, sibling = (x, y, c), (x, y, 1 - c)
        chips = [(1 - x, y), (x, 1 - y), (1 - x, 1 - y)]

        def slot(px, py, pc):
            return out_ref.at[4 * px + 2 * py + pc]

        def copy(k, block, to, src=None):
            return pltpu.make_async_remote_copy(
                src_ref=slot(*block) if src is None else src, dst_ref=slot(*block),
                send_sem=send_sems.at[k], recv_sem=recv_sems.at[k], device_id=to, device_id_type=MESH)

        mine = pltpu.make_async_copy(x_ref, slot(*me), local_sem)
        mine.start()
        first = [copy(0, me, sibling, src=x_ref)]
        first += [copy(1 + j, me, (*chip, c), src=x_ref) for j, chip in enumerate(chips)]
        for cp in first:
            cp.start()
        passed = [copy(4 + j, (*chip, c), sibling) for j, chip in enumerate(chips)]
        for j, chip in enumerate(chips):
            copy(1 + j, (*chip, c), me).wait_recv()
            passed[j].start()
        copy(0, sibling, me).wait_recv()
        for j, chip in enumerate(chips):
            copy(4 + j, (*chip, 1 - c), me).wait_recv()
        for cp in first + passed:
            cp.wait_send()
        mine.wait()

    return pl.pallas_call(
        body, name=name,
        out_shape=jax.ShapeDtypeStruct((NDEV, rows, cols), shard.dtype),
        in_specs=[HBM_SPEC], out_specs=HBM_SPEC,
        scratch_shapes=[pltpu.SemaphoreType.DMA((7,)), pltpu.SemaphoreType.DMA((7,)), pltpu.SemaphoreType.DMA],
    )(shard)


def _exchange_grads(name, grads, rows):
    n = len(grads)

    def body(*refs):
        g_refs, out_refs = refs[:n], refs[n:2 * n]
        send_sems, recv_sems, local_sems = refs[2 * n:]
        x, y, c = lax.axis_index("x"), lax.axis_index("y"), lax.axis_index("c")
        me = 4 * x + 2 * y + c
        copies = []
        for i in range(n):
            r = rows[i]
            own = pltpu.make_async_copy(g_refs[i].at[pl.ds(pl.multiple_of(me * r, 16), r), :], out_refs[i].at[me], local_sems.at[i])
            own.start()
            copies.append(own)
            for j in range(1, NDEV):
                px, py, pc = x ^ (j >> 2), y ^ ((j >> 1) & 1), c ^ (j & 1)
                peer = 4 * px + 2 * py + pc
                cp = pltpu.make_async_remote_copy(
                    src_ref=g_refs[i].at[pl.ds(pl.multiple_of(peer * r, 16), r), :], dst_ref=out_refs[i].at[me],
                    send_sem=send_sems.at[i, j - 1], recv_sem=recv_sems.at[i, j - 1],
                    device_id=(px, py, pc), device_id_type=MESH)
                cp.start()
                copies.append(cp)
        for cp in copies:
            cp.wait()

    return pl.pallas_call(
        body, name=name,
        out_shape=[jax.ShapeDtypeStruct((NDEV, r, D), g.dtype) for g, r in zip(grads, rows)],
        in_specs=[HBM_SPEC] * n, out_specs=[HBM_SPEC] * n,
        scratch_shapes=[pltpu.SemaphoreType.DMA((n, NDEV - 1)), pltpu.SemaphoreType.DMA((n, NDEV - 1)),
                        pltpu.SemaphoreType.DMA((n,))],
    )(*grads)


def _ffn_fwd(name, xin, affine, wall, offs, tm, loss=None):
    T = xin.shape[0]
    nt = T // tm
    fc = DFF // 2

    def body(*refs):
        it = iter(refs)
        x_ref = next(it)
        if affine is not None:
            g_ref, b_ref = next(it), next(it)
        wall_ref = next(it)
        if loss is not None:
            go_ref, bo_ref, tgt_ref = next(it), next(it), next(it)
        xh_ref, r_ref, a_ref, b2_ref, yb_ref = (next(it) for _ in range(5))
        if loss is not None:
            loss_ref = next(it)
        w1, w3, w2, sem = (next(it) for _ in range(4))

        @pl.when(pl.program_id(0) == 0)
        def _():
            _load_rows(wall_ref, [(w1, offs[0], RF), (w3, offs[1], RF), (w2, offs[2], RF)], sem)
            if loss is not None:
                loss_ref[...] = jnp.zeros_like(loss_ref)

        y = x_ref[...]
        if affine is not None:
            y = y * g_ref[...] + b_ref[...]
        yb = y.astype(BF16)
        yb_ref[...] = yb
        f = jnp.zeros((tm, D), F32)
        for ci in range(DFF // fc):
            cs = slice(ci * fc, (ci + 1) * fc)
            a = _dot_nt(yb, w1[cs, :]).astype(BF16)
            b = _dot_nt(yb, w3[cs, :]).astype(BF16)
            a_ref[:, cs] = a
            b2_ref[:, cs] = b
            af, bf = a.astype(F32), b.astype(F32)
            u = (af * _sigmoid(af) * bf).astype(BF16)
            f = f + _dot(u, w2[cs, :])
        xh, r = _ln_fwd(ALPHA * y + 0.5 * f)
        xh_ref[...] = xh
        r_ref[...] = jnp.broadcast_to(r, (tm, 128))
        if loss is not None:
            e = xh * go_ref[...] + bo_ref[...] - tgt_ref[...]
            loss_ref[...] += jnp.sum(e * e) * (0.5 / D)

    row = lambda w: pl.BlockSpec((tm, w), lambda i: (i, 0))
    vec = pl.BlockSpec((1, D), lambda i: (0, 0))
    ins, in_specs = [xin], [row(D)]
    if affine is not None:
        ins += list(affine)
        in_specs += [vec, vec]
    ins.append(wall)
    in_specs.append(HBM_SPEC)
    if loss is not None:
        ins += list(loss)
        in_specs += [vec, vec, row(D)]
    out_shape = [jax.ShapeDtypeStruct((T, D), F32), jax.ShapeDtypeStruct((T, 128), F32),
                 jax.ShapeDtypeStruct((T, DFF), BF16), jax.ShapeDtypeStruct((T, DFF), BF16),
                 jax.ShapeDtypeStruct((T, D), BF16)]
    out_specs = [row(D), row(128), row(DFF), row(DFF), row(D)]
    if loss is not None:
        out_shape.append(jax.ShapeDtypeStruct((8, 128), F32))
        out_specs.append(pl.BlockSpec((8, 128), lambda i: (0, 0)))
    return pl.pallas_call(
        body, name=name, grid=(nt,), in_specs=in_specs, out_specs=out_specs, out_shape=out_shape,
        scratch_shapes=[pltpu.VMEM((DFF, D), BF16)] * 3 + [pltpu.SemaphoreType.DMA((3 * NDEV,))],
        compiler_params=_params("arbitrary"),
    )(*ins)


def _ffn_bwd(name, dy_src, xh, r, g, a, b, wall, offs, tm):
    T = xh.shape[0]
    nt = T // tm
    fc = DFF // 2
    from_loss = dy_src[0] == "loss"

    def body(*refs):
        it = iter(refs)
        if from_loss:
            bo_ref, tgt_ref = next(it), next(it)
        else:
            dy_ref = next(it)
        xh_ref, r_ref, g_ref, a_ref, b_ref, wall_ref = (next(it) for _ in range(6))
        dyin_ref, dab_ref, u_ref, df_ref, dg_ref, db_ref = (next(it) for _ in range(6))
        w1, w3, w2, sem = (next(it) for _ in range(4))

        @pl.when(pl.program_id(0) == 0)
        def _():
            _load_rows(wall_ref, [(w1, offs[0], RF), (w3, offs[1], RF), (w2, offs[2], RF)], sem)
            dg_ref[...] = jnp.zeros_like(dg_ref)
            db_ref[...] = jnp.zeros_like(db_ref)

        xhv = xh_ref[...]
        gv = g_ref[...]
        if from_loss:
            dy = (xhv * gv + bo_ref[...] - tgt_ref[...]) * (1.0 / D)
        else:
            dy = dy_ref[...]
        dz, dgp, dbp = _ln_bwd(dy, xhv, r_ref[:, :1], gv)
        dg_ref[...] += dgp
        db_ref[...] += dbp
        df = (0.5 * dz).astype(BF16)
        df_ref[...] = df
        dx = ALPHA * dz
        for ci in range(DFF // fc):
            cs = slice(ci * fc, (ci + 1) * fc)
            du = _dot_nt(df, w2[cs, :])
            af, bf = a_ref[:, cs].astype(F32), b_ref[:, cs].astype(F32)
            s = _sigmoid(af)
            sl = af * s
            u_ref[:, cs] = (sl * bf).astype(BF16)
            da = (du * bf * (s * (1.0 + af * (1.0 - s)))).astype(BF16)
            dbb = (du * sl).astype(BF16)
            dab_ref[:, cs] = da
            dab_ref[:, DFF + ci * fc:DFF + (ci + 1) * fc] = dbb
            dx = dx + _dot(da, w1[cs, :]) + _dot(dbb, w3[cs, :])
        dyin_ref[...] = dx

    row = lambda w: pl.BlockSpec((tm, w), lambda i: (i, 0))
    vec = pl.BlockSpec((1, D), lambda i: (0, 0))
    if from_loss:
        ins, in_specs = [dy_src[1], dy_src[2]], [vec, row(D)]
    else:
        ins, in_specs = [dy_src[1]], [row(D)]
    ins += [xh, r, g, a, b, wall]
    in_specs += [row(D), row(128), vec, row(DFF), row(DFF), HBM_SPEC]
    return pl.pallas_call(
        body, name=name, grid=(nt,), in_specs=in_specs,
        out_specs=[row(D), row(2 * DFF), row(DFF), row(D), vec, vec],
        out_shape=[jax.ShapeDtypeStruct((T, D), F32), jax.ShapeDtypeStruct((T, 2 * DFF), BF16),
                   jax.ShapeDtypeStruct((T, DFF), BF16), jax.ShapeDtypeStruct((T, D), BF16),
                   jax.ShapeDtypeStruct((1, D), F32), jax.ShapeDtypeStruct((1, D), F32)],
        scratch_shapes=[pltpu.VMEM((DFF, D), BF16)] * 3 + [pltpu.SemaphoreType.DMA((3 * NDEV,))],
        compiler_params=_params("arbitrary"),
    )(*ins)


def _wgrad(name, a, b, bn, tk):
    T, N = a.shape
    nk = T // tk

    def body(a_ref, b_ref, o_ref, acc):
        k = pl.program_id(1)

        @pl.when(k == 0)
        def _():
            acc[...] = jnp.zeros_like(acc)

        acc[...] += _dot_tn(a_ref[...], b_ref[...])

        @pl.when(k == nk - 1)
        def _():
            o_ref[...] = acc[...].astype(BF16)

    return pl.pallas_call(
        body, name=name, grid=(N // bn, nk),
        in_specs=[pl.BlockSpec((tk, bn), lambda n, k: (k, n)), pl.BlockSpec((tk, D), lambda n, k: (k, 0))],
        out_specs=pl.BlockSpec((bn, D), lambda n, k: (n, 0)),
        out_shape=jax.ShapeDtypeStruct((N, D), BF16),
        scratch_shapes=[pltpu.VMEM((bn, D), F32)],
        compiler_params=_params("parallel", "arbitrary"),
    )(a, b)


def _rope_tables(T):
    pos = jnp.arange(T, dtype=F32)
    inv_freq = ROPE_THETA ** (-jnp.arange(0, ROPE_DIM, 2, dtype=F32) / ROPE_DIM)
    ang = pos[:, None] * inv_freq[None, :]
    cos, sin = jnp.cos(ang), jnp.sin(ang)
    half = ROPE_DIM // 2
    ones = jnp.ones((T, HEAD_DIM - ROPE_DIM), F32)
    zeros = jnp.zeros((T, HEAD_DIM - ROPE_DIM), F32)
    zh = jnp.zeros((T, half), F32)
    c1 = jnp.concatenate([cos, cos, ones], axis=1)
    s1 = jnp.concatenate([-sin, zh, zeros], axis=1)
    s2 = jnp.concatenate([zh, sin, zeros], axis=1)
    two = lambda t: jnp.concatenate([t, t], axis=1)
    return two(c1), two(s1), two(s2)


def _rope(t, c, s1, s2):
    n = t.shape[1] // 128
    ct, s1t, s2t = (jnp.tile(v, (1, n)) for v in (c, s1, s2))
    w = t.shape[1]
    return t * ct + pltpu.roll(t, w - 8, 1) * s1t + pltpu.roll(t, 8, 1) * s2t


def _rope_t(dr, c, s1, s2):
    n = dr.shape[1] // 128
    ct, s1t, s2t = (jnp.tile(v, (1, n)) for v in (c, s1, s2))
    w = dr.shape[1]
    return dr * ct + pltpu.roll(dr * s1t, 8, 1) + pltpu.roll(dr * s2t, w - 8, 1)


_Q, _K, _V = (0, 1024), (1024, 256), (1280, 256)
_HG = (1536, HG_W)
_GATES = (5632, GATE_W)


def _inproj_fwd(name, xh, g, b, wall, b_in, ropes, tm):
    T = xh.shape[0]

    def body(xh_ref, g_ref, b_ref, wall_ref, bin_ref, c_ref, s1_ref, s2_ref,
             qkv_ref, hg_ref, gate_ref, yb_ref, w, sem):
        @pl.when(pl.program_id(0) == 0)
        def _():
            _load_rows(wall_ref, [(w, 0, RIN)], sem)

        yb = (xh_ref[...] * g_ref[...] + b_ref[...]).astype(BF16)
        yb_ref[...] = yb
        c, s1, s2 = c_ref[...], s1_ref[...], s2_ref[...]

        def piece(start, width):
            return _dot_nt(yb, w[start:start + width, :]) + bin_ref[:, start:start + width]

        qkv_ref[:, 0:1024] = _rope(piece(*_Q), c, s1, s2)
        qkv_ref[:, 1024:1280] = _rope(piece(*_K), c, s1, s2)
        qkv_ref[:, 1280:1536] = piece(*_V)
        for j in range(4):
            hg_ref[:, 1024 * j:1024 * (j + 1)] = piece(_HG[0] + 1024 * j, 1024)
        for j in range(2):
            gate_ref[:, 1024 * j:1024 * (j + 1)] = piece(_GATES[0] + 1024 * j, 1024)

    row = lambda wd: pl.BlockSpec((tm, wd), lambda i: (i, 0))
    vec = lambda wd: pl.BlockSpec((1, wd), lambda i: (0, 0))
    return pl.pallas_call(
        body, name=name, grid=(T // tm,),
        in_specs=[row(D), vec(D), vec(D), HBM_SPEC, vec(DIN), row(128), row(128), row(128)],
        out_specs=[row(QKV_W), row(HG_W), row(GATE_W), row(D)],
        out_shape=[jax.ShapeDtypeStruct((T, QKV_W), F32), jax.ShapeDtypeStruct((T, HG_W), F32),
                   jax.ShapeDtypeStruct((T, GATE_W), F32), jax.ShapeDtypeStruct((T, D), BF16)],
        scratch_shapes=[pltpu.VMEM((DIN, D), BF16), pltpu.SemaphoreType.DMA((NDEV,))],
        compiler_params=_params("arbitrary"),
    )(xh, g, b, wall, b_in, *ropes)


def _inproj_bwd(name, dq, dk, dv, dhg, dgates, dz2, wall, ropes, tm):
    T = dq.shape[0]

    def body(dq_ref, dk_ref, dv_ref, d0, d1, d2, d3, dga_ref, dgh_ref, dz_ref, wall_ref, c_ref, s1_ref, s2_ref,
             dy_ref, dproj_ref, dbin_ref, w, sem):
        @pl.when(pl.program_id(0) == 0)
        def _():
            _load_rows(wall_ref, [(w, 0, RIN)], sem)
            dbin_ref[...] = jnp.zeros_like(dbin_ref)

        c, s1, s2 = c_ref[...], s1_ref[...], s2_ref[...]
        acc = ALPHA * dz_ref[...]
        pieces = [(_Q[0], _rope_t(dq_ref[...], c, s1, s2)), (_K[0], _rope_t(dk_ref[...], c, s1, s2)),
                  (_V[0], dv_ref[...])]
        pieces += [(_HG[0] + 1024 * j, r[...]) for j, r in enumerate((d0, d1, d2, d3))]
        pieces += [(_GATES[0], dga_ref[...]), (_GATES[0] + 1024, dgh_ref[...])]
        for start, val in pieces:
            width = val.shape[1]
            dbin_ref[:, start:start + width] += jnp.sum(val, axis=0, keepdims=True)
            vb = val.astype(BF16)
            dproj_ref[:, start:start + width] = vb
            acc = acc + _dot(vb, w[start:start + width, :])
        dy_ref[...] = acc

    row = lambda wd: pl.BlockSpec((tm, wd), lambda i: (i, 0))
    return pl.pallas_call(
        body, name=name, grid=(T // tm,),
        in_specs=[row(D), row(256), row(256)] + [row(D)] * 4 + [row(D), row(D), row(D), HBM_SPEC] + [row(128)] * 3,
        out_specs=[row(D), row(DIN), pl.BlockSpec((1, DIN), lambda i: (0, 0))],
        out_shape=[jax.ShapeDtypeStruct((T, D), F32), jax.ShapeDtypeStruct((T, DIN), BF16),
                   jax.ShapeDtypeStruct((1, DIN), F32)],
        scratch_shapes=[pltpu.VMEM((DIN, D), BF16), pltpu.SemaphoreType.DMA((NDEV,))],
        compiler_params=_params("arbitrary"),
    )(dq, dk, dv, *dhg, *dgates, dz2, wall, *ropes)


def _halves(t):
    lane = lax.broadcasted_iota(jnp.int32, t.shape, 1)
    low = lane < HEAD_DIM
    sw = pltpu.roll(t, HEAD_DIM, 1)
    zero = jnp.zeros_like(t)
    h0 = (jnp.where(low, t, zero), jnp.where(low, zero, sw))
    h1 = (jnp.where(low, sw, zero), jnp.where(low, zero, t))
    return h0, h1


def _attn_mask(first):
    qi = lax.broadcasted_iota(jnp.int32, (ATTN_BLOCK, 2 * ATTN_BLOCK), 0)
    kj = lax.broadcasted_iota(jnp.int32, (ATTN_BLOCK, 2 * ATTN_BLOCK), 1)
    dist = qi + ATTN_BLOCK - kj
    return (dist >= 0) & (dist < ATTN_BLOCK) & jnp.logical_or(kj >= ATTN_BLOCK, jnp.logical_not(first))


def _attn_probs(qp, kz, sink, mask):
    s = _dot_nt(qp, kz) * (HEAD_DIM ** -0.5)
    s = jnp.where(mask, s, NEG_INF)
    m = jnp.maximum(jnp.max(s, axis=-1, keepdims=True), sink)
    p = jnp.exp(s - m)
    es = jnp.exp(sink - m)
    denom = jnp.sum(p, axis=-1, keepdims=True) + es
    return p / denom, es / denom


def _kv_pairs(kp_ref, kc_ref):
    band = jnp.concatenate([kp_ref[...], kc_ref[...]], axis=0)
    out = []
    for gp in range(N_KV_HEADS // 2):
        h0, h1 = _halves(band[:, 128 * gp:128 * (gp + 1)])
        out += [tuple(v.astype(BF16) for v in h0), tuple(v.astype(BF16) for v in h1)]
    return out


def _attn_fwd(name, qkv, sinks):
    T = qkv.shape[0]
    nb = T // ATTN_BLOCK

    def body(q_ref, kp_ref, kc_ref, vp_ref, vc_ref, sink_ref, o_ref):
        mask = _attn_mask(pl.program_id(0) == 0)
        kz, vz = _kv_pairs(kp_ref, kc_ref), _kv_pairs(vp_ref, vc_ref)
        for j in range(N_Q_HEADS // 2):
            g = j // 2
            qp = q_ref[:, 128 * j:128 * (j + 1)].astype(BF16)
            o = jnp.zeros((ATTN_BLOCK, 128), F32)
            for hh in range(2):
                h = 2 * j + hh
                p, _ = _attn_probs(qp, kz[g][hh], sink_ref[:, h:h + 1], mask)
                o = o + _dot(p.astype(BF16), vz[g][hh])
            o_ref[:, 128 * j:128 * (j + 1)] = o.astype(BF16)

    prev = lambda i: jnp.maximum(i - 1, 0)
    return pl.pallas_call(
        body, name=name, grid=(nb,),
        in_specs=[pl.BlockSpec((ATTN_BLOCK, D), lambda i: (i, 0)),
                  pl.BlockSpec((ATTN_BLOCK, 256), lambda i: (prev(i), 4)),
                  pl.BlockSpec((ATTN_BLOCK, 256), lambda i: (i, 4)),
                  pl.BlockSpec((ATTN_BLOCK, 256), lambda i: (prev(i), 5)),
                  pl.BlockSpec((ATTN_BLOCK, 256), lambda i: (i, 5)),
                  pl.BlockSpec((1, N_Q_HEADS), lambda i: (0, 0))],
        out_specs=pl.BlockSpec((ATTN_BLOCK, D), lambda i: (i, 0)),
        out_shape=jax.ShapeDtypeStruct((T, D), BF16),
        compiler_params=_params("arbitrary"),
    )(qkv, qkv, qkv, qkv, qkv, sinks)


def _attn_bwd(name, qkv, sinks, dya):
    T = qkv.shape[0]
    nb = T // ATTN_BLOCK

    def body(q_ref, kp_ref, kc_ref, vp_ref, vc_ref, sink_ref, do_ref, dq_ref, dk_ref, dv_ref, ds_ref):
        i = pl.program_id(0)

        @pl.when(i == 0)
        def _():
            ds_ref[...] = jnp.zeros_like(ds_ref)

        mask = _attn_mask(i == 0)
        kz, vz = _kv_pairs(kp_ref, kc_ref), _kv_pairs(vp_ref, vc_ref)
        lane = lax.broadcasted_iota(jnp.int32, (2 * ATTN_BLOCK, 128), 1)
        low = lane < HEAD_DIM
        slane = lax.broadcasted_iota(jnp.int32, (1, 128), 1)
        dkz = [jnp.zeros((2 * ATTN_BLOCK, 128), F32) for _ in range(N_KV_HEADS)]
        dvz = [jnp.zeros((2 * ATTN_BLOCK, 128), F32) for _ in range(N_KV_HEADS)]
        dsink = jnp.zeros((1, 128), F32)
        for j in range(N_Q_HEADS // 2):
            g = j // 2
            qp = q_ref[:, 128 * j:128 * (j + 1)].astype(BF16)
            dop = do_ref[:, 128 * j:128 * (j + 1)].astype(BF16)
            dqp = jnp.zeros((ATTN_BLOCK, 128), F32)
            for hh in range(2):
                h = 2 * j + hh
                p, ps = _attn_probs(qp, kz[g][hh], sink_ref[:, h:h + 1], mask)
                pb = p.astype(BF16)
                dp = _dot_nt(dop, vz[g][hh])
                delta = jnp.sum(p * dp, axis=-1, keepdims=True)
                dsb = (p * (dp - delta) * (HEAD_DIM ** -0.5)).astype(BF16)
                dsink = dsink + jnp.where(slane == h, -jnp.sum(ps * delta), 0.0)
                dqp = dqp + _dot(dsb, kz[g][hh])
                dk_h = _dot_tn(dsb, qp)
                dv_h = _dot_tn(pb, dop)
                own = low if hh == 0 else jnp.logical_not(low)
                dk_h = jnp.where(own, dk_h, 0.0)
                dv_h = jnp.where(own, dv_h, 0.0)
                if hh != g % 2:
                    dk_h = pltpu.roll(dk_h, HEAD_DIM, 1)
                    dv_h = pltpu.roll(dv_h, HEAD_DIM, 1)
                dkz[g] = dkz[g] + dk_h
                dvz[g] = dvz[g] + dv_h
            dq_ref[:, 128 * j:128 * (j + 1)] = dqp
        ds_ref[...] += dsink
        cur = pl.ds(pl.multiple_of(i * ATTN_BLOCK, ATTN_BLOCK), ATTN_BLOCK)
        prv = pl.ds(pl.multiple_of(jnp.maximum(i - 1, 0) * ATTN_BLOCK, ATTN_BLOCK), ATTN_BLOCK)
        for gp in range(N_KV_HEADS // 2):
            cols = slice(128 * gp, 128 * (gp + 1))
            dkb = dkz[2 * gp] + dkz[2 * gp + 1]
            dvb = dvz[2 * gp] + dvz[2 * gp + 1]
            dk_ref[cur, cols] = dkb[ATTN_BLOCK:, :]
            dv_ref[cur, cols] = dvb[ATTN_BLOCK:, :]

            @pl.when(i > 0)
            def _():
                dk_ref[prv, cols] += dkb[:ATTN_BLOCK, :]
                dv_ref[prv, cols] += dvb[:ATTN_BLOCK, :]

    prev = lambda i: jnp.maximum(i - 1, 0)
    whole = lambda w: pl.BlockSpec((T, w), lambda i: (0, 0))
    return pl.pallas_call(
        body, name=name, grid=(nb,),
        in_specs=[pl.BlockSpec((ATTN_BLOCK, D), lambda i: (i, 0)),
                  pl.BlockSpec((ATTN_BLOCK, 256), lambda i: (prev(i), 4)),
                  pl.BlockSpec((ATTN_BLOCK, 256), lambda i: (i, 4)),
                  pl.BlockSpec((ATTN_BLOCK, 256), lambda i: (prev(i), 5)),
                  pl.BlockSpec((ATTN_BLOCK, 256), lambda i: (i, 5)),
                  pl.BlockSpec((1, N_Q_HEADS), lambda i: (0, 0)),
                  pl.BlockSpec((ATTN_BLOCK, D), lambda i: (i, 0))],
        out_specs=[pl.BlockSpec((ATTN_BLOCK, D), lambda i: (i, 0)), whole(256), whole(256),
                   pl.BlockSpec((1, 128), lambda i: (0, 0))],
        out_shape=[jax.ShapeDtypeStruct((T, D), F32), jax.ShapeDtypeStruct((T, 256), F32),
                   jax.ShapeDtypeStruct((T, 256), F32), jax.ShapeDtypeStruct((1, 128), F32)],
        compiler_params=_params("arbitrary"),
    )(qkv, qkv, qkv, qkv, qkv, sinks, dya)


def _split3(x):
    h1 = x.astype(BF16)
    r1 = x - h1.astype(F32)
    h2 = r1.astype(BF16)
    h3 = (r1 - h2.astype(F32)).astype(BF16)
    return h1, h2, h3


def _tri_sum(tri, x):
    h1, h2, h3 = _split3(x)
    return _dot(tri, h1) + _dot(tri, h2) + _dot(tri, h3)


def _lower_bound(lbl_ref):
    l0, l1 = lbl_ref[0:1, :], lbl_ref[1:2, :]
    m = jnp.maximum(l0, l1)
    e0, e1 = jnp.exp(l0 - m), jnp.exp(l1 - m)
    return e0 / (e0 + e1)


def _hgrn_chunk(fl, qh, vv, st, lb, tril_b, causal):
    sg = _sigmoid(fl)
    f = lb + (1.0 - lb) * sg
    k = 1.0 - f
    gc = _tri_sum(tril_b, jnp.log(f))
    gl = gc[HGRN_CHUNK - 1:HGRN_CHUNK, :]
    sq = _sigmoid(qh)
    eg = jnp.exp(gc)
    eng = jnp.exp(-gc)
    elg = jnp.exp(gl - gc)
    qd = qh * sq * eg
    ki = k * eng
    ke = k * elg
    qd_b, ki_b, ke_b, v_b, st_b = (t.astype(BF16) for t in (qd, ki, ke, vv, st))
    am = jnp.where(causal, _dot_nt(qd_b, ki_b), 0.0)
    o = _dot(am.astype(BF16), v_b) + _dot_nt(qd_b, st_b)
    egl = jnp.exp(gl)
    st_new = st * egl + _dot_tn(v_b, ke_b)
    return dict(sg=sg, f=f, sq=sq, eg=eg, eng=eng, elg=elg, qd=qd, ki=ki, ke=ke, egl=egl, am=am, o=o,
                qd_b=qd_b, ki_b=ki_b, ke_b=ke_b, v_b=v_b, st_b=st_b), st_new


def _chunk_consts():
    ri = lax.broadcasted_iota(jnp.int32, (HGRN_CHUNK, HGRN_CHUNK), 0)
    ci = lax.broadcasted_iota(jnp.int32, (HGRN_CHUNK, HGRN_CHUNK), 1)
    causal = ri >= ci
    return causal, jnp.where(causal, 1.0, 0.0).astype(BF16), jnp.where(ri <= ci, 1.0, 0.0).astype(BF16)


def _hgrn_fwd(name, hg, lb_logits, norm_g, th):
    T = hg.shape[0]
    nc = th // HGRN_CHUNK

    def body(fl_ref, qh_ref, v_ref, og_ref, lbl_ref, ng_ref, y_ref, ss_ref, st):
        @pl.when(pl.program_id(1) == 0)
        def _():
            st[...] = jnp.zeros_like(st)

        lb = _lower_bound(lbl_ref)
        causal, tril_b, _ = _chunk_consts()
        ng = ng_ref[...]

        def chunk(ci, carry):
            rows = pl.ds(pl.multiple_of(ci * HGRN_CHUNK, HGRN_CHUNK), HGRN_CHUNK)
            s_in = st[...]
            ss_ref[0, ci] = s_in
            q, s_out = _hgrn_chunk(fl_ref[rows, :], qh_ref[rows, :], v_ref[rows, :], s_in, lb, tril_b, causal)
            st[...] = s_out
            o = q["o"]
            rr = lax.rsqrt(jnp.mean(o * o, axis=-1, keepdims=True) + RMS_EPS)
            og = og_ref[rows, :]
            y_ref[rows, :] = (o * rr * ng * (og * _sigmoid(og))).astype(BF16)
            return carry

        lax.fori_loop(0, nc, chunk, 0)

    col = lambda j: pl.BlockSpec((th, 128), lambda h, t: (t, 8 * j + h))
    return pl.pallas_call(
        body, name=name, grid=(HGRN_HEADS, T // th),
        in_specs=[col(0), col(1), col(2), col(3), pl.BlockSpec((2, 128), lambda h, t: (0, h)),
                  pl.BlockSpec((1, 128), lambda h, t: (0, 0))],
        out_specs=[pl.BlockSpec((th, 128), lambda h, t: (t, h)),
                   pl.BlockSpec((1, nc, 128, 128), lambda h, t: (h, t, 0, 0))],
        out_shape=[jax.ShapeDtypeStruct((T, D), BF16),
                   jax.ShapeDtypeStruct((HGRN_HEADS, T // HGRN_CHUNK, 128, 128), F32)],
        scratch_shapes=[pltpu.VMEM((128, 128), F32)],
        compiler_params=_params("parallel", "arbitrary"),
    )(hg, hg, hg, hg, lb_logits, norm_g)


def _hgrn_bwd(name, hg, lb_logits, norm_g, sstart, dyh, th):
    T = hg.shape[0]
    nc = th // HGRN_CHUNK
    nt = T // th

    def body(fl_ref, qh_ref, v_ref, og_ref, lbl_ref, ng_ref, ss_ref, dy_ref,
             dfl_ref, dqh_ref, dv_ref, dog_ref, dlb_ref, dng_ref, dst):
        @pl.when(pl.program_id(1) == 0)
        def _():
            dst[...] = jnp.zeros_like(dst)
            dlb_ref[...] = jnp.zeros_like(dlb_ref)
            dng_ref[...] = jnp.zeros_like(dng_ref)

        lb = _lower_bound(lbl_ref)
        causal, tril_b, triu_b = _chunk_consts()
        ng = ng_ref[...]
        last = lax.broadcasted_iota(jnp.int32, (HGRN_CHUNK, 128), 0) == HGRN_CHUNK - 1

        def chunk(step, carry):
            ci = nc - 1 - step
            rows = pl.ds(pl.multiple_of(ci * HGRN_CHUNK, HGRN_CHUNK), HGRN_CHUNK)
            fl, qh, vv, og, dy = fl_ref[rows, :], qh_ref[rows, :], v_ref[rows, :], og_ref[rows, :], dy_ref[rows, :]
            s_in = ss_ref[0, ci]
            q, _ = _hgrn_chunk(fl, qh, vv, s_in, lb, tril_b, causal)
            o = q["o"]
            rr = lax.rsqrt(jnp.mean(o * o, axis=-1, keepdims=True) + RMS_EPS)
            oh = o * rr
            sog = _sigmoid(og)
            dog_ref[rows, :] = dy * oh * ng * (sog * (1.0 + og * (1.0 - sog)))
            don = dy * (og * sog)
            dng_ref[0] += jnp.sum(don * oh, axis=0, keepdims=True)
            dd = don * ng
            do = rr * (dd - oh * jnp.mean(dd * oh, axis=-1, keepdims=True))
            do_b = do.astype(BF16)
            dsp = dst[...]
            dsp_b = dsp.astype(BF16)
            da = jnp.where(causal, _dot_nt(do_b, q["v_b"]), 0.0)
            da_b = da.astype(BF16)
            dv_ref[rows, :] = _dot_tn(q["am"].astype(BF16), do_b) + _dot_nt(q["ke_b"], dsp_b)
            dqd = _dot(da_b, q["ki_b"]) + _dot(do_b, q["st_b"])
            dki = _dot_tn(da_b, q["qd_b"])
            dke = _dot(q["v_b"], dsp_b)
            dgl = jnp.sum(dke * q["ke"], axis=0, keepdims=True) + jnp.sum(dsp * s_in, axis=0, keepdims=True) * q["egl"]
            dst[...] = dsp * q["egl"] + _dot_tn(do_b, q["qd_b"])
            dgc = dqd * q["qd"] - dki * q["ki"] - dke * q["ke"]
            dgc = dgc + jnp.where(last, dgl, 0.0)
            dlf = _tri_sum(triu_b, dgc)
            df = dlf / q["f"] - (dki * q["eng"] + dke * q["elg"])
            sg = q["sg"]
            dlb_ref[...] += jnp.sum(df * (1.0 - sg), axis=0, keepdims=True)
            dfl_ref[rows, :] = df * (1.0 - lb) * sg * (1.0 - sg)
            sq = q["sq"]
            dqh_ref[rows, :] = dqd * q["eg"] * (sq * (1.0 + qh * (1.0 - sq)))
            return carry

        lax.fori_loop(0, nc, chunk, 0)

    col = lambda j: pl.BlockSpec((th, 128), lambda h, t: (nt - 1 - t, 8 * j + h))
    out = pl.BlockSpec((th, 128), lambda h, t: (nt - 1 - t, h))
    return pl.pallas_call(
        body, name=name, grid=(HGRN_HEADS, nt),
        in_specs=[col(0), col(1), col(2), col(3), pl.BlockSpec((2, 128), lambda h, t: (0, h)),
                  pl.BlockSpec((1, 128), lambda h, t: (0, 0)),
                  pl.BlockSpec((1, nc, 128, 128), lambda h, t: (h, nt - 1 - t, 0, 0)), out],
        out_specs=[out, out, out, out, pl.BlockSpec((1, 128), lambda h, t: (0, h)),
                   pl.BlockSpec((1, 1, 128), lambda h, t: (h, 0, 0))],
        out_shape=[jax.ShapeDtypeStruct((T, D), F32)] * 4 + [jax.ShapeDtypeStruct((1, D), F32),
                                                              jax.ShapeDtypeStruct((HGRN_HEADS, 1, 128), F32)],
        scratch_shapes=[pltpu.VMEM((128, 128), F32)],
        compiler_params=_params("parallel", "arbitrary"),
    )(hg, hg, hg, hg, lb_logits, norm_g, sstart, dyh)


def _mix_fwd(name, ya, yh, gates, xh1, g1, b1, wall, tm):
    T = ya.shape[0]

    def body(ya_ref, yh_ref, ga_ref, gh_ref, xh_ref, g_ref, b_ref, wall_ref,
             xo_ref, r_ref, pa_ref, ph_ref, mg_ref, wpa, wph, wo, sem):
        @pl.when(pl.program_id(0) == 0)
        def _():
            _load_rows(wall_ref, [(wpa, RIN, RP), (wph, RIN + RP, RP), (wo, RIN + 2 * RP, RP)], sem)

        pa = _dot(ya_ref[...], wpa[...]).astype(BF16)
        ph = _dot(yh_ref[...], wph[...]).astype(BF16)
        pa_ref[...] = pa
        ph_ref[...] = ph
        mg = (_sigmoid(ga_ref[...]) * pa.astype(F32) + _sigmoid(gh_ref[...]) * ph.astype(F32)).astype(BF16)
        mg_ref[...] = mg
        y1 = xh_ref[...] * g_ref[...] + b_ref[...]
        xh2, r = _ln_fwd(ALPHA * y1 + _dot(mg, wo[...]))
        xo_ref[...] = xh2
        r_ref[...] = jnp.broadcast_to(r, (tm, 128))

    row = lambda w: pl.BlockSpec((tm, w), lambda i: (i, 0))
    vec = pl.BlockSpec((1, D), lambda i: (0, 0))
    return pl.pallas_call(
        body, name=name, grid=(T // tm,),
        in_specs=[row(D), row(D), pl.BlockSpec((tm, D), lambda i: (i, 0)), pl.BlockSpec((tm, D), lambda i: (i, 1)),
                  row(D), vec, vec, HBM_SPEC],
        out_specs=[row(D), row(128), row(D), row(D), row(D)],
        out_shape=[jax.ShapeDtypeStruct((T, D), F32), jax.ShapeDtypeStruct((T, 128), F32)]
        + [jax.ShapeDtypeStruct((T, D), BF16)] * 3,
        scratch_shapes=[pltpu.VMEM((D, D), BF16)] * 3 + [pltpu.SemaphoreType.DMA((3 * NDEV,))],
        compiler_params=_params("arbitrary"),
    )(ya, yh, gates, gates, xh1, g1, b1, wall)


def _mix_bwd(name, dy2, xh2, r2, g2, gates, pa, ph, wall, tm):
    T = dy2.shape[0]

    def body(dy_ref, xh_ref, r_ref, g_ref, ga_ref, gh_ref, pa_ref, ph_ref, wall_ref,
             dz_ref, dmix_ref, dpa_ref, dph_ref, dga_ref, dgh_ref, dya_ref, dyh_ref, dg_ref, db_ref,
             wpa, wph, wo, sem):
        @pl.when(pl.program_id(0) == 0)
        def _():
            _load_rows(wall_ref, [(wpa, RIN, RP), (wph, RIN + RP, RP), (wo, RIN + 2 * RP, RP)], sem)
            dg_ref[...] = jnp.zeros_like(dg_ref)
            db_ref[...] = jnp.zeros_like(db_ref)

        dz, dgp, dbp = _ln_bwd(dy_ref[...], xh_ref[...], r_ref[:, :1], g_ref[...])
        dg_ref[...] += dgp
        db_ref[...] += dbp
        dz_ref[...] = dz
        dmix = dz.astype(BF16)
        dmix_ref[...] = dmix
        dmg = _dot_nt(dmix, wo[...])
        sa, sh = _sigmoid(ga_ref[...]), _sigmoid(gh_ref[...])
        dga_ref[...] = dmg * pa_ref[...].astype(F32) * sa * (1.0 - sa)
        dgh_ref[...] = dmg * ph_ref[...].astype(F32) * sh * (1.0 - sh)
        dpa = (dmg * sa).astype(BF16)
        dph = (dmg * sh).astype(BF16)
        dpa_ref[...] = dpa
        dph_ref[...] = dph
        dya_ref[...] = _dot_nt(dpa, wpa[...])
        dyh_ref[...] = _dot_nt(dph, wph[...])

    row = lambda w: pl.BlockSpec((tm, w), lambda i: (i, 0))
    vec = pl.BlockSpec((1, D), lambda i: (0, 0))
    return pl.pallas_call(
        body, name=name, grid=(T // tm,),
        in_specs=[row(D), row(D), row(128), vec, pl.BlockSpec((tm, D), lambda i: (i, 0)),
                  pl.BlockSpec((tm, D), lambda i: (i, 1)), row(D), row(D), HBM_SPEC],
        out_specs=[row(D)] * 8 + [vec, vec],
        out_shape=[jax.ShapeDtypeStruct((T, D), F32)] + [jax.ShapeDtypeStruct((T, D), BF16)] * 3
        + [jax.ShapeDtypeStruct((T, D), F32)] * 4 + [jax.ShapeDtypeStruct((1, D), F32)] * 2,
        scratch_shapes=[pltpu.VMEM((D, D), BF16)] * 3 + [pltpu.SemaphoreType.DMA((3 * NDEV,))],
        compiler_params=_params("arbitrary"),
    )(dy2, xh2, r2, g2, gates, gates, pa, ph, wall)


def _adam(w, g, m, v):
    m = ADAM_B1 * m + (1.0 - ADAM_B1) * g
    v = ADAM_B2 * v + (1.0 - ADAM_B2) * (g * g)
    m_hat = m / (1.0 - ADAM_B1 ** ADAM_STEP)
    v_hat = v / (1.0 - ADAM_B2 ** ADAM_STEP)
    delta = -ADAM_LR * (m_hat / (jnp.sqrt(v_hat) + ADAM_EPS) + ADAM_WD * w)
    return delta, m, v


def _sum_slots(name, recv):
    _, rows, _ = recv.shape

    def body(r_ref, o_ref):
        acc = r_ref[0].astype(F32)
        for k in range(1, NDEV):
            acc = acc + r_ref[k].astype(F32)
        o_ref[...] = acc

    return pl.pallas_call(
        body, name=name, out_shape=jax.ShapeDtypeStruct((rows, D), F32),
        compiler_params=pltpu.CompilerParams(vmem_limit_bytes=VMEM_LIMIT),
    )(recv)


def _adam_call(name, w, g, m, v):
    def body(w_ref, g_ref, m_ref, v_ref, d_ref, mo_ref, vo_ref):
        d, mm, vv = _adam(w_ref[...], g_ref[...], m_ref[...], v_ref[...])
        d_ref[...] = d
        mo_ref[...] = mm
        vo_ref[...] = vv

    return pl.pallas_call(
        body, name=name, out_shape=[jax.ShapeDtypeStruct(w.shape, F32)] * 3,
        compiler_params=pltpu.CompilerParams(vmem_limit_bytes=VMEM_LIMIT),
    )(w, g, m, v)


_SMALL = [("ln1_g", D), ("ln1_b", D), ("ln2_g", D), ("ln2_b", D), ("ln3_g", D), ("ln3_b", D),
          ("b_in", DIN), ("lb", D), ("attn_sinks", 128), ("hgrn_norm_g", 128), ("loss", 128)]
_SMALL_OFF = {}
_o = 0
for _n, _w in _SMALL:
    _SMALL_OFF[_n] = (_o, _w)
    _o += _w
PACK = _o


def _small_step(name, packed, small_w, small_m, small_v):
    names = ["ln1_g", "ln1_b", "ln2_g", "ln2_b", "ln3_g", "ln3_b", "b_in", "attn_sinks", "hgrn_lb_logits", "hgrn_norm_g"]
    np_ = len(names)

    def body(*refs):
        p_ref = refs[0]
        w_refs = refs[1:1 + np_]
        m_refs = refs[1 + np_:1 + 2 * np_]
        v_refs = refs[1 + 2 * np_:1 + 3 * np_]
        outs = refs[1 + 3 * np_:2 + 7 * np_]
        buf, send_sems, recv_sems = refs[2 + 7 * np_:]
        loss_ref, outs = outs[0], outs[1:]
        x, y, c = lax.axis_index("x"), lax.axis_index("y"), lax.axis_index("c")
        me = 4 * x + 2 * y + c
        buf[me] = p_ref[...]
        copies = []
        for j in range(1, NDEV):
            px, py, pc = x ^ (j >> 2), y ^ ((j >> 1) & 1), c ^ (j & 1)
            cp = pltpu.make_async_remote_copy(
                src_ref=p_ref, dst_ref=buf.at[me], send_sem=send_sems.at[j - 1], recv_sem=recv_sems.at[j - 1],
                device_id=(px, py, pc), device_id_type=MESH)
            cp.start()
            copies.append(cp)
        for cp in copies:
            cp.wait()
        tot = buf[0]
        for k in range(1, NDEV):
            tot = tot + buf[k]

        def part(n):
            o, w = _SMALL_OFF[n]
            return tot[:, o:o + w]

        loss_ref[...] = part("loss")
        for i, n in enumerate(names):
            w = w_refs[i][...]
            if n == "hgrn_lb_logits":
                m_ = jnp.maximum(w[0:1, :], w[1:2, :])
                e0, e1 = jnp.exp(w[0:1, :] - m_), jnp.exp(w[1:2, :] - m_)
                p0 = e0 / (e0 + e1)
                t = p0 * (1.0 - p0) * part("lb")
                g = jnp.concatenate([t, -t], axis=0)
            elif n == "attn_sinks":
                g = part(n)[:, :N_Q_HEADS]
            else:
                g = part(n)
            d, mm, vv = _adam(w, g, m_refs[i][...], v_refs[i][...])
            outs[4 * i][...] = g
            outs[4 * i + 1][...] = d
            outs[4 * i + 2][...] = mm
            outs[4 * i + 3][...] = vv

    out_shape = [jax.ShapeDtypeStruct((1, 128), F32)]
    for n in names:
        out_shape += [jax.ShapeDtypeStruct(small_w[n].shape, F32)] * 4
    return pl.pallas_call(
        body, name=name, out_shape=out_shape,
        scratch_shapes=[pltpu.VMEM((NDEV, 1, PACK), F32), pltpu.SemaphoreType.DMA((NDEV - 1,)),
                        pltpu.SemaphoreType.DMA((NDEV - 1,))],
        compiler_params=pltpu.CompilerParams(vmem_limit_bytes=VMEM_LIMIT),
    )(packed, *[small_w[n] for n in names], *[small_m[n] for n in names], *[small_v[n] for n in names]), names


def _tile(T, pref):
    return min(T, pref)


def kernel(x, ln1_g, ln1_b, ffn1_w1, ffn1_w3, ffn1_w2, ln2_g, ln2_b, w_in, b_in, attn_sinks, hgrn_lb_logits, hgrn_norm_g, w_proj_attn, w_proj_hgrn, w_out, ln3_g, ln3_b, ffn2_w1, ffn2_w3, ffn2_w2, loss_target, m_ln1_g, m_ln1_b, m_ffn1_w1, m_ffn1_w3, m_ffn1_w2, m_ln2_g, m_ln2_b, m_w_in, m_b_in, m_attn_sinks, m_hgrn_lb_logits, m_hgrn_norm_g, m_w_proj_attn, m_w_proj_hgrn, m_w_out, m_ln3_g, m_ln3_b, m_ffn2_w1, m_ffn2_w3, m_ffn2_w2, v_ln1_g, v_ln1_b, v_ffn1_w1, v_ffn1_w3, v_ffn1_w2, v_ln2_g, v_ln2_b, v_w_in, v_b_in, v_attn_sinks, v_hgrn_lb_logits, v_hgrn_norm_g, v_w_proj_attn, v_w_proj_hgrn, v_w_out, v_ln3_g, v_ln3_b, v_ffn2_w1, v_ffn2_w3, v_ffn2_w2):
    T = x.shape[1]
    xs = x[0]
    tgt = loss_target[0]
    tm = _tile(T, 256)
    tk = _tile(T, 512)
    th = _tile(T, 512)

    t_bf = lambda w: w[0].T.astype(BF16)
    n_bf = lambda w: w[0].astype(BF16)
    ffn1_all = _all_gather("gather_ffn1", jnp.concatenate([t_bf(ffn1_w1), t_bf(ffn1_w3), n_bf(ffn1_w2)], axis=0))
    mix_all = _all_gather("gather_mixer", jnp.concatenate(
        [t_bf(w_in), n_bf(w_proj_attn), n_bf(w_proj_hgrn), n_bf(w_out)], axis=0))
    ffn2_all = _all_gather("gather_ffn2", jnp.concatenate([t_bf(ffn2_w1), t_bf(ffn2_w3), n_bf(ffn2_w2)], axis=0))
    ffn_offs = (0, RF, 2 * RF)
    ropes = _rope_tables(T)

    xh1, r1, a1, b1, xb0 = _ffn_fwd("ffn1_fwd", xs, None, ffn1_all, ffn_offs, tm)
    qkv, hg, gates, y1b = _inproj_fwd("inproj_fwd", xh1, ln1_g, ln1_b, mix_all, b_in, ropes, tm)
    ya = _attn_fwd("attn_fwd", qkv, attn_sinks)
    yh, sstart = _hgrn_fwd("hgrn_fwd", hg, hgrn_lb_logits, hgrn_norm_g, th)
    xh2, r2, pa, ph, merged = _mix_fwd("mix_fwd", ya, yh, gates, xh1, ln1_g, ln1_b, mix_all, tm)
    xh3, r3, a2, b2, y2b, loss_part = _ffn_fwd("ffn2_fwd", xh2, (ln2_g, ln2_b), ffn2_all, ffn_offs, tm,
                                               loss=(ln3_g, ln3_b, tgt))

    dy2, dab2, u2, df2, dg3, db3 = _ffn_bwd("ffn2_bwd", ("loss", ln3_b, tgt), xh3, r3, ln3_g, a2, b2, ffn2_all,
                                            ffn_offs, tm)
    g_ffn2_13 = _wgrad("wgrad_ffn2_w13", dab2, y2b, DFF // 2, tk)
    g_ffn2_2 = _wgrad("wgrad_ffn2_w2", u2, df2, DFF // 2, tk)
    dz2, dmix, dpa, dph, dga, dgh, dya, dyh, dg2, db2 = _mix_bwd("mix_bwd", dy2, xh2, r2, ln2_g, gates, pa, ph,
                                                                 mix_all, tm)
    g_wo = _wgrad("wgrad_w_out", merged, dmix, D, tk)
    g_pa = _wgrad("wgrad_w_pa", ya, dpa, D, tk)
    g_ph = _wgrad("wgrad_w_ph", yh, dph, D, tk)
    dq, dk, dv, dsink = _attn_bwd("attn_bwd", qkv, attn_sinks, dya)
    dfl, dqh, dih, dog, dlb, dng = _hgrn_bwd("hgrn_bwd", hg, hgrn_lb_logits, hgrn_norm_g, sstart, dyh, th)
    dy1, dproj, dbin = _inproj_bwd("inproj_bwd", dq, dk, dv, (dfl, dqh, dih, dog), (dga, dgh), dz2, mix_all, ropes, tm)
    g_win = _wgrad("wgrad_w_in", dproj, y1b, DIN // 4, tk)
    gx, dab1, u1, df1, dg1, db1 = _ffn_bwd("ffn1_bwd", ("dy", dy1), xh1, r1, ln1_g, a1, b1, ffn1_all, ffn_offs, tm)
    g_ffn1_13 = _wgrad("wgrad_ffn1_w13", dab1, xb0, DFF // 2, tk)
    g_ffn1_2 = _wgrad("wgrad_ffn1_w2", u1, df1, DFF // 2, tk)

    grads = [g_ffn1_13[:DFF], g_ffn1_13[DFF:], g_ffn1_2, g_win, g_pa, g_ph, g_wo, g_ffn2_13[:DFF], g_ffn2_13[DFF:], g_ffn2_2]
    rows = [RF, RF, RF, RIN, RP, RP, RP, RF, RF, RF]
    recv = _exchange_grads("exchange_grads", grads, rows)
    big = [("ffn1_w1", ffn1_w1, m_ffn1_w1, v_ffn1_w1, True), ("ffn1_w3", ffn1_w3, m_ffn1_w3, v_ffn1_w3, True),
           ("ffn1_w2", ffn1_w2, m_ffn1_w2, v_ffn1_w2, False), ("w_in", w_in, m_w_in, v_w_in, True),
           ("w_proj_attn", w_proj_attn, m_w_proj_attn, v_w_proj_attn, False),
           ("w_proj_hgrn", w_proj_hgrn, m_w_proj_hgrn, v_w_proj_hgrn, False),
           ("w_out", w_out, m_w_out, v_w_out, False),
           ("ffn2_w1", ffn2_w1, m_ffn2_w1, v_ffn2_w1, True), ("ffn2_w3", ffn2_w3, m_ffn2_w3, v_ffn2_w3, True),
           ("ffn2_w2", ffn2_w2, m_ffn2_w2, v_ffn2_w2, False)]
    res = {}
    for (n, w, m, v, transposed), rc in zip(big, recv):
        g = _sum_slots("sum_" + n, rc)
        if transposed:
            g = g.T
        d, mm, vv = _adam_call("adam_" + n, w[0], g, m[0], v[0])
        res[n] = tuple(t[None] for t in (g, d, mm, vv))

    parts = {"ln1_g": dg1, "ln1_b": db1, "ln2_g": dg2, "ln2_b": db2, "ln3_g": dg3, "ln3_b": db3, "b_in": dbin,
             "lb": dlb, "attn_sinks": dsink, "hgrn_norm_g": jnp.sum(dng, axis=0), "loss": loss_part[0:1, :]}
    packed = jnp.concatenate([parts[n] for n, _ in _SMALL], axis=1)
    small_w = dict(ln1_g=ln1_g, ln1_b=ln1_b, ln2_g=ln2_g, ln2_b=ln2_b, ln3_g=ln3_g, ln3_b=ln3_b, b_in=b_in,
                   attn_sinks=attn_sinks, hgrn_lb_logits=hgrn_lb_logits, hgrn_norm_g=hgrn_norm_g)
    small_m = dict(ln1_g=m_ln1_g, ln1_b=m_ln1_b, ln2_g=m_ln2_g, ln2_b=m_ln2_b, ln3_g=m_ln3_g, ln3_b=m_ln3_b,
                   b_in=m_b_in, attn_sinks=m_attn_sinks, hgrn_lb_logits=m_hgrn_lb_logits, hgrn_norm_g=m_hgrn_norm_g)
    small_v = dict(ln1_g=v_ln1_g, ln1_b=v_ln1_b, ln2_g=v_ln2_g, ln2_b=v_ln2_b, ln3_g=v_ln3_g, ln3_b=v_ln3_b,
                   b_in=v_b_in, attn_sinks=v_attn_sinks, hgrn_lb_logits=v_hgrn_lb_logits, hgrn_norm_g=v_hgrn_norm_g)
    outs, names = _small_step("small_step", packed, small_w, small_m, small_v)
    loss = outs[0][0, 0]
    for i, n in enumerate(names):
        res[n] = tuple(outs[1 + 4 * i:5 + 4 * i])

    order = ["ln1_g", "ln1_b", "ffn1_w1", "ffn1_w3", "ffn1_w2", "ln2_g", "ln2_b", "w_in", "b_in", "attn_sinks",
             "hgrn_lb_logits", "hgrn_norm_g", "w_proj_attn", "w_proj_hgrn", "w_out", "ln3_g", "ln3_b",
             "ffn2_w1", "ffn2_w3", "ffn2_w2"]
    return (loss, gx[None], *[res[n][0] for n in order], *[res[n][1] for n in order],
            *[res[n][2] for n in order], *[res[n][3] for n in order])
```

```python
import jax
import jax.numpy as jnp
from jax import lax
from jax.experimental import pallas as pl
from jax.experimental.pallas import tpu as pltpu

F32 = jnp.float32
BF16 = jnp.bfloat16

NDEV = 8
D = 1024
DFF = 2816
RF = DFF // NDEV
DIN = 7680
RIN = DIN // NDEV
RP = D // NDEV
N_Q_HEADS = 16
N_KV_HEADS = 4
HEAD_DIM = 64
ATTN_BLOCK = 128
ROPE_THETA = 500000.0
ROPE_DIM = HEAD_DIM // 4
HGRN_HEADS = 8
HGRN_DK = 128
HGRN_CHUNK = 64
ALPHA = 2.0 ** 0.25
LN_EPS = 1e-5
RMS_EPS = 1e-6
NEG_INF = -1e30
ADAM_LR = 0.001
ADAM_B1 = 0.9
ADAM_B2 = 0.999
ADAM_EPS = 1e-08
ADAM_WD = 0.01
ADAM_STEP = 10

QKV_W = 1536
HG_W = 4096
GATE_W = 2048
VMEM_LIMIT = 60 * 2 ** 20

MESH = pl.DeviceIdType.MESH
HBM_SPEC = pl.BlockSpec(memory_space=pltpu.HBM)


def _params(*sem):
    return pltpu.CompilerParams(dimension_semantics=sem, vmem_limit_bytes=VMEM_LIMIT)


def _dot(a, b):
    return jnp.dot(a, b, preferred_element_type=F32)


def _dot_nt(a, b):
    return lax.dot_general(a, b, (((1,), (1,)), ((), ())), preferred_element_type=F32)


def _dot_tn(a, b):
    return lax.dot_general(a, b, (((0,), (0,)), ((), ())), preferred_element_type=F32)


def _sigmoid(x):
    return 1.0 / (1.0 + jnp.exp(-x))


def _ln_fwd(z):
    mu = jnp.mean(z, axis=-1, keepdims=True)
    zc = z - mu
    var = jnp.mean(zc * zc, axis=-1, keepdims=True)
    r = lax.rsqrt(var + LN_EPS)
    return zc * r, r


def _ln_bwd(dy, xh, r, g):
    dxh = dy * g
    m1 = jnp.mean(dxh, axis=-1, keepdims=True)
    m2 = jnp.mean(dxh * xh, axis=-1, keepdims=True)
    dz = r * (dxh - m1 - xh * m2)
    return dz, jnp.sum(dy * xh, axis=0, keepdims=True), jnp.sum(dy, axis=0, keepdims=True)


def _load_rows(wall_ref, pieces, sem):
    copies = []
    for dst, off, r in pieces:
        for k in range(NDEV):
            c = pltpu.make_async_copy(wall_ref.at[k, pl.ds(off, r), :], dst.at[pl.ds(k * r, r), :], sem.at[len(copies)])
            c.start()
            copies.append(c)
    for c in copies:
        c.wait()


def _all_gather(name, shard):
    rows, cols = shard.shape

    def body(x_ref, out_ref, send_sems, recv_sems, local_sem):
        x, y, c = lax.axis_index("x"), lax.axis_index("y"), lax.axis_index("c")
        me, sibling = (x, y, c), (x, y, 1 - c)
        chips = [(1 - x, y), (x, 1 - y), (1 - x, 1 - y)]

        def slot(px, py, pc):
            return out_ref.at[4 * px + 2 * py + pc]

        def copy(k, block, to, src=None):
            return pltpu.make_async_remote_copy(
                src_ref=slot(*block) if src is None else src, dst_ref=slot(*block),
                send_sem=send_sems.at[k], recv_sem=recv_sems.at[k], device_id=to, device_id_type=MESH)

        mine = pltpu.make_async_copy(x_ref, slot(*me), local_sem)
        mine.start()
        first = [copy(0, me, sibling, src=x_ref)]
        first += [copy(1 + j, me, (*chip, c), src=x_ref) for j, chip in enumerate(chips)]
        for cp in first:
            cp.start()
        passed = [copy(4 + j, (*chip, c), sibling) for j, chip in enumerate(chips)]
        for j, chip in enumerate(chips):
            copy(1 + j, (*chip, c), me).wait_recv()
            passed[j].start()
        copy(0, sibling, me).wait_recv()
        for j, chip in enumerate(chips):
            copy(4 + j, (*chip, 1 - c), me).wait_recv()
        for cp in first + passed:
            cp.wait_send()
        mine.wait()

    return pl.pallas_call(
        body, name=name,
        out_shape=jax.ShapeDtypeStruct((NDEV, rows, cols), shard.dtype),
        in_specs=[HBM_SPEC], out_specs=HBM_SPEC,
        scratch_shapes=[pltpu.SemaphoreType.DMA((7,)), pltpu.SemaphoreType.DMA((7,)), pltpu.SemaphoreType.DMA],
    )(shard)


def _exchange_grads(name, grads, rows):
    n = len(grads)

    def body(*refs):
        g_refs, out_refs = refs[:n], refs[n:2 * n]
        send_sems, recv_sems, local_sems = refs[2 * n:]
        x, y, c = lax.axis_index("x"), lax.axis_index("y"), lax.axis_index("c")
        me = 4 * x + 2 * y + c
        copies = []
        for i in range(n):
            r = rows[i]
            own = pltpu.make_async_copy(g_refs[i].at[pl.ds(pl.multiple_of(me * r, 16), r), :], out_refs[i].at[me], local_sems.at[i])
            own.start()
            copies.append(own)
            for j in range(1, NDEV):
                px, py, pc = x ^ (j >> 2), y ^ ((j >> 1) & 1), c ^ (j & 1)
                peer = 4 * px + 2 * py + pc
                cp = pltpu.make_async_remote_copy(
                    src_ref=g_refs[i].at[pl.ds(pl.multiple_of(peer * r, 16), r), :], dst_ref=out_refs[i].at[me],
                    send_sem=send_sems.at[i, j - 1], recv_sem=recv_sems.at[i, j - 1],
                    device_id=(px, py, pc), device_id_type=MESH)
                cp.start()
                copies.append(cp)
        for cp in copies:
            cp.wait()

    return pl.pallas_call(
        body, name=name,
        out_shape=[jax.ShapeDtypeStruct((NDEV, r, D), g.dtype) for g, r in zip(grads, rows)],
        in_specs=[HBM_SPEC] * n, out_specs=[HBM_SPEC] * n,
        scratch_shapes=[pltpu.SemaphoreType.DMA((n, NDEV - 1)), pltpu.SemaphoreType.DMA((n, NDEV - 1)),
                        pltpu.SemaphoreType.DMA((n,))],
    )(*grads)


def _ffn_fwd(name, xin, affine, wall, offs, tm, loss=None):
    T = xin.shape[0]
    nt = T // tm
    fc = DFF // 2

    def body(*refs):
        it = iter(refs)
        x_ref = next(it)
        if affine is not None:
            g_ref, b_ref = next(it), next(it)
        wall_ref = next(it)
        if loss is not None:
            go_ref, bo_ref, tgt_ref = next(it), next(it), next(it)
        xh_ref, r_ref, a_ref, b2_ref, yb_ref = (next(it) for _ in range(5))
        if loss is not None:
            loss_ref = next(it)
        w1, w3, w2, sem = (next(it) for _ in range(4))

        @pl.when(pl.program_id(0) == 0)
        def _():
            _load_rows(wall_ref, [(w1, offs[0], RF), (w3, offs[1], RF), (w2, offs[2], RF)], sem)
            if loss is not None:
                loss_ref[...] = jnp.zeros_like(loss_ref)

        y = x_ref[...]
        if affine is not None:
            y = y * g_ref[...] + b_ref[...]
        yb = y.astype(BF16)
        yb_ref[...] = yb
        f = jnp.zeros((tm, D), F32)
        for ci in range(DFF // fc):
            cs = slice(ci * fc, (ci + 1) * fc)
            a = _dot_nt(yb, w1[cs, :]).astype(BF16)
            b = _dot_nt(yb, w3[cs, :]).astype(BF16)
            a_ref[:, cs] = a
            b2_ref[:, cs] = b
            af, bf = a.astype(F32), b.astype(F32)
            u = (af * _sigmoid(af) * bf).astype(BF16)
            f = f + _dot(u, w2[cs, :])
        xh, r = _ln_fwd(ALPHA * y + 0.5 * f)
        xh_ref[...] = xh
        r_ref[...] = jnp.broadcast_to(r, (tm, 128))
        if loss is not None:
            e = xh * go_ref[...] + bo_ref[...] - tgt_ref[...]
            loss_ref[...] += jnp.sum(e * e) * (0.5 / D)

    row = lambda w: pl.BlockSpec((tm, w), lambda i: (i, 0))
    vec = pl.BlockSpec((1, D), lambda i: (0, 0))
    ins, in_specs = [xin], [row(D)]
    if affine is not None:
        ins += list(affine)
        in_specs += [vec, vec]
    ins.append(wall)
    in_specs.append(HBM_SPEC)
    if loss is not None:
        ins += list(loss)
        in_specs += [vec, vec, row(D)]
    out_shape = [jax.ShapeDtypeStruct((T, D), F32), jax.ShapeDtypeStruct((T, 128), F32),
                 jax.ShapeDtypeStruct((T, DFF), BF16), jax.ShapeDtypeStruct((T, DFF), BF16),
                 jax.ShapeDtypeStruct((T, D), BF16)]
    out_specs = [row(D), row(128), row(DFF), row(DFF), row(D)]
    if loss is not None:
        out_shape.append(jax.ShapeDtypeStruct((8, 128), F32))
        out_specs.append(pl.BlockSpec((8, 128), lambda i: (0, 0)))
    return pl.pallas_call(
        body, name=name, grid=(nt,), in_specs=in_specs, out_specs=out_specs, out_shape=out_shape,
        scratch_shapes=[pltpu.VMEM((DFF, D), BF16)] * 3 + [pltpu.SemaphoreType.DMA((3 * NDEV,))],
        compiler_params=_params("arbitrary"),
    )(*ins)


def _ffn_bwd(name, dy_src, xh, r, g, a, b, wall, offs, tm):
    T = xh.shape[0]
    nt = T // tm
    fc = DFF // 2
    from_loss = dy_src[0] == "loss"

    def body(*refs):
        it = iter(refs)
        if from_loss:
            bo_ref, tgt_ref = next(it), next(it)
        else:
            dy_ref = next(it)
        xh_ref, r_ref, g_ref, a_ref, b_ref, wall_ref = (next(it) for _ in range(6))
        dyin_ref, dab_ref, u_ref, df_ref, dg_ref, db_ref = (next(it) for _ in range(6))
        w1, w3, w2, sem = (next(it) for _ in range(4))

        @pl.when(pl.program_id(0) == 0)
        def _():
            _load_rows(wall_ref, [(w1, offs[0], RF), (w3, offs[1], RF), (w2, offs[2], RF)], sem)
            dg_ref[...] = jnp.zeros_like(dg_ref)
            db_ref[...] = jnp.zeros_like(db_ref)

        xhv = xh_ref[...]
        gv = g_ref[...]
        if from_loss:
            dy = (xhv * gv + bo_ref[...] - tgt_ref[...]) * (1.0 / D)
        else:
            dy = dy_ref[...]
        dz, dgp, dbp = _ln_bwd(dy, xhv, r_ref[:, :1], gv)
        dg_ref[...] += dgp
        db_ref[...] += dbp
        df = (0.5 * dz).astype(BF16)
        df_ref[...] = df
        dx = ALPHA * dz
        for ci in range(DFF // fc):
            cs = slice(ci * fc, (ci + 1) * fc)
            du = _dot_nt(df, w2[cs, :])
            af, bf = a_ref[:, cs].astype(F32), b_ref[:, cs].astype(F32)
            s = _sigmoid(af)
            sl = af * s
            u_ref[:, cs] = (sl * bf).astype(BF16)
            da = (du * bf * (s * (1.0 + af * (1.0 - s)))).astype(BF16)
            dbb = (du * sl).astype(BF16)
            dab_ref[:, cs] = da
            dab_ref[:, DFF + ci * fc:DFF + (ci + 1) * fc] = dbb
            dx = dx + _dot(da, w1[cs, :]) + _dot(dbb, w3[cs, :])
        dyin_ref[...] = dx

    row = lambda w: pl.BlockSpec((tm, w), lambda i: (i, 0))
    vec = pl.BlockSpec((1, D), lambda i: (0, 0))
    if from_loss:
        ins, in_specs = [dy_src[1], dy_src[2]], [vec, row(D)]
    else:
        ins, in_specs = [dy_src[1]], [row(D)]
    ins += [xh, r, g, a, b, wall]
    in_specs += [row(D), row(128), vec, row(DFF), row(DFF), HBM_SPEC]
    return pl.pallas_call(
        body, name=name, grid=(nt,), in_specs=in_specs,
        out_specs=[row(D), row(2 * DFF), row(DFF), row(D), vec, vec],
        out_shape=[jax.ShapeDtypeStruct((T, D), F32), jax.ShapeDtypeStruct((T, 2 * DFF), BF16),
                   jax.ShapeDtypeStruct((T, DFF), BF16), jax.ShapeDtypeStruct((T, D), BF16),
                   jax.ShapeDtypeStruct((1, D), F32), jax.ShapeDtypeStruct((1, D), F32)],
        scratch_shapes=[pltpu.VMEM((DFF, D), BF16)] * 3 + [pltpu.SemaphoreType.DMA((3 * NDEV,))],
        compiler_params=_params("arbitrary"),
    )(*ins)


def _wgrad(name, a, b, bn, tk):
    T, N = a.shape
    nk = T // tk

    def body(a_ref, b_ref, o_ref, acc):
        k = pl.program_id(1)

        @pl.when(k == 0)
        def _():
            acc[...] = jnp.zeros_like(acc)

        acc[...] += _dot_tn(a_ref[...], b_ref[...])

        @pl.when(k == nk - 1)
        def _():
            o_ref[...] = acc[...].astype(BF16)

    return pl.pallas_call(
        body, name=name, grid=(N // bn, nk),
        in_specs=[pl.BlockSpec((tk, bn), lambda n, k: (k, n)), pl.BlockSpec((tk, D), lambda n, k: (k, 0))],
        out_specs=pl.BlockSpec((bn, D), lambda n, k: (n, 0)),
        out_shape=jax.ShapeDtypeStruct((N, D), BF16),
        scratch_shapes=[pltpu.VMEM((bn, D), F32)],
        compiler_params=_params("parallel", "arbitrary"),
    )(a, b)


def _rope_tables(T):
    pos = jnp.arange(T, dtype=F32)
    inv_freq = ROPE_THETA ** (-jnp.arange(0, ROPE_DIM, 2, dtype=F32) / ROPE_DIM)
    ang = pos[:, None] * inv_freq[None, :]
    cos, sin = jnp.cos(ang), jnp.sin(ang)
    half = ROPE_DIM // 2
    ones = jnp.ones((T, HEAD_DIM - ROPE_DIM), F32)
    zeros = jnp.zeros((T, HEAD_DIM - ROPE_DIM), F32)
    zh = jnp.zeros((T, half), F32)
    c1 = jnp.concatenate([cos, cos, ones], axis=1)
    s1 = jnp.concatenate([-sin, zh, zeros], axis=1)
    s2 = jnp.concatenate([zh, sin, zeros], axis=1)
    two = lambda t: jnp.concatenate([t, t], axis=1)
    return two(c1), two(s1), two(s2)


def _rope(t, c, s1, s2):
    n = t.shape[1] // 128
    ct, s1t, s2t = (jnp.tile(v, (1, n)) for v in (c, s1, s2))
    w = t.shape[1]
    return t * ct + pltpu.roll(t, w - 8, 1) * s1t + pltpu.roll(t, 8, 1) * s2t


def _rope_t(dr, c, s1, s2):
    n = dr.shape[1] // 128
    ct, s1t, s2t = (jnp.tile(v, (1, n)) for v in (c, s1, s2))
    w = dr.shape[1]
    return dr * ct + pltpu.roll(dr * s1t, 8, 1) + pltpu.roll(dr * s2t, w - 8, 1)


_Q, _K, _V = (0, 1024), (1024, 256), (1280, 256)
_HG = (1536, HG_W)
_GATES = (5632, GATE_W)


def _inproj_fwd(name, xh, g, b, wall, b_in, ropes, tm):
    T = xh.shape[0]

    def body(xh_ref, g_ref, b_ref, wall_ref, bin_ref, c_ref, s1_ref, s2_ref,
             qkv_ref, kt_ref, vt_ref, hg_ref, gate_ref, yb_ref, w, sem):
        @pl.when(pl.program_id(0) == 0)
        def _():
            _load_rows(wall_ref, [(w, 0, RIN)], sem)

        yb = (xh_ref[...] * g_ref[...] + b_ref[...]).astype(BF16)
        yb_ref[...] = yb
        c, s1, s2 = c_ref[...], s1_ref[...], s2_ref[...]

        def piece(start, width):
            return _dot_nt(yb, w[start:start + width, :]) + bin_ref[:, start:start + width]

        k = _rope(piece(*_K), c, s1, s2)
        v = piece(*_V)
        qkv_ref[:, 0:1024] = _rope(piece(*_Q), c, s1, s2).astype(BF16)
        qkv_ref[:, 1024:1280] = k.astype(BF16)
        qkv_ref[:, 1280:1536] = v.astype(BF16)
        kt_ref[...] = k.T.astype(BF16)
        vt_ref[...] = v.T.astype(BF16)
        for j in range(4):
            hg_ref[:, 1024 * j:1024 * (j + 1)] = piece(_HG[0] + 1024 * j, 1024)
        for j in range(2):
            gate_ref[:, 1024 * j:1024 * (j + 1)] = piece(_GATES[0] + 1024 * j, 1024)

    row = lambda wd: pl.BlockSpec((tm, wd), lambda i: (i, 0))
    vec = lambda wd: pl.BlockSpec((1, wd), lambda i: (0, 0))
    colt = pl.BlockSpec((256, tm), lambda i: (0, i))
    return pl.pallas_call(
        body, name=name, grid=(T // tm,),
        in_specs=[row(D), vec(D), vec(D), HBM_SPEC, vec(DIN), row(128), row(128), row(128)],
        out_specs=[row(QKV_W), colt, colt, row(HG_W), row(GATE_W), row(D)],
        out_shape=[jax.ShapeDtypeStruct((T, QKV_W), BF16), jax.ShapeDtypeStruct((256, T), BF16),
                   jax.ShapeDtypeStruct((256, T), BF16), jax.ShapeDtypeStruct((T, HG_W), F32),
                   jax.ShapeDtypeStruct((T, GATE_W), F32), jax.ShapeDtypeStruct((T, D), BF16)],
        scratch_shapes=[pltpu.VMEM((DIN, D), BF16), pltpu.SemaphoreType.DMA((NDEV,))],
        compiler_params=_params("arbitrary"),
    )(xh, g, b, wall, b_in, *ropes)


def _inproj_bwd(name, dq, dk, dv, dhg, dgates, dz2, wall, ropes, tm):
    T = dq.shape[0]

    def body(dq_ref, dk_ref, dv_ref, d0, d1, d2, d3, dga_ref, dgh_ref, dz_ref, wall_ref, c_ref, s1_ref, s2_ref,
             dy_ref, dproj_ref, dbin_ref, w, sem):
        @pl.when(pl.program_id(0) == 0)
        def _():
            _load_rows(wall_ref, [(w, 0, RIN)], sem)
            dbin_ref[...] = jnp.zeros_like(dbin_ref)

        c, s1, s2 = c_ref[...], s1_ref[...], s2_ref[...]
        acc = ALPHA * dz_ref[...]
        pieces = [(_Q[0], _rope_t(dq_ref[...], c, s1, s2)), (_K[0], _rope_t(dk_ref[...], c, s1, s2)),
                  (_V[0], dv_ref[...])]
        pieces += [(_HG[0] + 1024 * j, r[...]) for j, r in enumerate((d0, d1, d2, d3))]
        pieces += [(_GATES[0], dga_ref[...]), (_GATES[0] + 1024, dgh_ref[...])]
        for start, val in pieces:
            width = val.shape[1]
            dbin_ref[:, start:start + width] += jnp.sum(val, axis=0, keepdims=True)
            vb = val.astype(BF16)
            dproj_ref[:, start:start + width] = vb
            acc = acc + _dot(vb, w[start:start + width, :])
        dy_ref[...] = acc

    row = lambda wd: pl.BlockSpec((tm, wd), lambda i: (i, 0))
    return pl.pallas_call(
        body, name=name, grid=(T // tm,),
        in_specs=[row(D), row(256), row(256)] + [row(D)] * 4 + [row(D), row(D), row(D), HBM_SPEC] + [row(128)] * 3,
        out_specs=[row(D), row(DIN), pl.BlockSpec((1, DIN), lambda i: (0, 0))],
        out_shape=[jax.ShapeDtypeStruct((T, D), F32), jax.ShapeDtypeStruct((T, DIN), BF16),
                   jax.ShapeDtypeStruct((1, DIN), F32)],
        scratch_shapes=[pltpu.VMEM((DIN, D), BF16), pltpu.SemaphoreType.DMA((NDEV,))],
        compiler_params=_params("arbitrary"),
    )(dq, dk, dv, *dhg, *dgates, dz2, wall, *ropes)


def _halves(t):
    lane = lax.broadcasted_iota(jnp.int32, t.shape, 1)
    low = lane < HEAD_DIM
    sw = pltpu.roll(t, HEAD_DIM, 1)
    zero = jnp.zeros_like(t)
    h0 = (jnp.where(low, t, zero), jnp.where(low, zero, sw))
    h1 = (jnp.where(low, sw, zero), jnp.where(low, zero, t))
    return h0, h1


def _lane_stack(p_ref, c_ref, gp):
    sl = slice(128 * gp, 128 * (gp + 1))
    hp, hc = _halves(p_ref[:, sl].astype(F32)), _halves(c_ref[:, sl].astype(F32))
    return [jnp.concatenate([hp[gg][0], hc[gg][0], hp[gg][1], hc[gg][1]], axis=0).astype(BF16) for gg in range(2)]


def _row_stack(tp_ref, tc_ref, g):
    band = jnp.concatenate([tp_ref[64 * g:64 * (g + 1), :], tc_ref[64 * g:64 * (g + 1), :]], axis=1)
    z = jnp.zeros_like(band)
    return [jnp.concatenate([band, z], axis=0), jnp.concatenate([z, band], axis=0)]


def _fold(prevm, t4, hh):
    return jnp.where(prevm, t4[256 * hh:256 * hh + 128, :], t4[256 * hh + 128:256 * hh + 256, :])


def _unfold(prevm, t):
    return jnp.concatenate([jnp.where(prevm, t, 0.0), jnp.where(prevm, 0.0, t)], axis=0).astype(BF16)


def _attn_softmax(s, kill, sink):
    s = jnp.where(kill, NEG_INF, s)
    m = jnp.maximum(jnp.max(s, axis=0, keepdims=True), sink)
    p = jnp.exp(s - m)
    es = jnp.exp(sink - m)
    denom = jnp.sum(p, axis=0, keepdims=True) + es
    return p / denom, es / denom


def _attn_masks(first):
    row = lax.broadcasted_iota(jnp.int32, (ATTN_BLOCK, ATTN_BLOCK), 0)
    lane = lax.broadcasted_iota(jnp.int32, (ATTN_BLOCK, ATTN_BLOCK), 1)
    prevm = row > lane
    return prevm, jnp.logical_and(first, prevm)


def _attn_fwd(name, qkv, vt, sinks):
    T = qkv.shape[0]
    nb = T // ATTN_BLOCK
    scale = HEAD_DIM ** -0.5

    def body(q_ref, kp_ref, kc_ref, vtp_ref, vtc_ref, sink_ref, o_ref):
        prevm, kill = _attn_masks(pl.program_id(0) == 0)
        kst = _lane_stack(kp_ref, kc_ref, 0) + _lane_stack(kp_ref, kc_ref, 1)
        vts = [_row_stack(vtp_ref, vtc_ref, g) for g in range(N_KV_HEADS)]
        for j in range(N_Q_HEADS // 2):
            g = j // 2
            qp = q_ref[:, 128 * j:128 * (j + 1)]
            s4 = _dot_nt(kst[g], qp) * scale
            ot = jnp.zeros((128, ATTN_BLOCK), F32)
            for hh in range(2):
                h = 2 * j + hh
                pn, _ = _attn_softmax(_fold(prevm, s4, hh), kill, sink_ref[:, h:h + 1])
                ot = ot + _dot(vts[g][hh], _unfold(prevm, pn))
            o_ref[:, 128 * j:128 * (j + 1)] = ot.T.astype(BF16)

    prev = lambda i: jnp.maximum(i - 1, 0)
    return pl.pallas_call(
        body, name=name, grid=(nb,),
        in_specs=[pl.BlockSpec((ATTN_BLOCK, D), lambda i: (i, 0)),
                  pl.BlockSpec((ATTN_BLOCK, 256), lambda i: (prev(i), 4)),
                  pl.BlockSpec((ATTN_BLOCK, 256), lambda i: (i, 4)),
                  pl.BlockSpec((256, ATTN_BLOCK), lambda i: (0, prev(i))),
                  pl.BlockSpec((256, ATTN_BLOCK), lambda i: (0, i)),
                  pl.BlockSpec((1, N_Q_HEADS), lambda i: (0, 0))],
        out_specs=pl.BlockSpec((ATTN_BLOCK, D), lambda i: (i, 0)),
        out_shape=jax.ShapeDtypeStruct((T, D), BF16),
        compiler_params=_params("arbitrary"),
    )(qkv, qkv, qkv, vt, vt, sinks)


def _attn_bwd(name, qkv, kt, sinks, dya):
    T = qkv.shape[0]
    nb = T // ATTN_BLOCK
    scale = HEAD_DIM ** -0.5

    def body(q_ref, kp_ref, kc_ref, vp_ref, vc_ref, ktp_ref, ktc_ref, sink_ref, do_ref,
             dq_ref, dk_ref, dv_ref, ds_ref):
        i = pl.program_id(0)

        @pl.when(i == 0)
        def _():
            ds_ref[...] = jnp.zeros_like(ds_ref)

        prevm, kill = _attn_masks(i == 0)
        kst = _lane_stack(kp_ref, kc_ref, 0) + _lane_stack(kp_ref, kc_ref, 1)
        vst = _lane_stack(vp_ref, vc_ref, 0) + _lane_stack(vp_ref, vc_ref, 1)
        kts = [_row_stack(ktp_ref, ktc_ref, g) for g in range(N_KV_HEADS)]
        lane = lax.broadcasted_iota(jnp.int32, (2 * ATTN_BLOCK, 128), 1)
        low = lane < HEAD_DIM
        slane = lax.broadcasted_iota(jnp.int32, (1, 128), 1)
        dkz = [jnp.zeros((2 * ATTN_BLOCK, 128), F32) for _ in range(N_KV_HEADS)]
        dvz = [jnp.zeros((2 * ATTN_BLOCK, 128), F32) for _ in range(N_KV_HEADS)]
        dsink = jnp.zeros((1, 128), F32)
        for j in range(N_Q_HEADS // 2):
            g = j // 2
            qp = q_ref[:, 128 * j:128 * (j + 1)]
            dop = do_ref[:, 128 * j:128 * (j + 1)]
            s4 = _dot_nt(kst[g], qp) * scale
            dp4 = _dot_nt(vst[g], dop)
            dqt = jnp.zeros((128, ATTN_BLOCK), F32)
            ds2, p2 = [], []
            for hh in range(2):
                h = 2 * j + hh
                pn, ps = _attn_softmax(_fold(prevm, s4, hh), kill, sink_ref[:, h:h + 1])
                dp = _fold(prevm, dp4, hh)
                delta = jnp.sum(pn * dp, axis=0, keepdims=True)
                dsink = dsink + jnp.where(slane == h, -jnp.sum(ps * delta), 0.0)
                ds2.append(_unfold(prevm, pn * (dp - delta) * scale))
                p2.append(_unfold(prevm, pn))
                dqt = dqt + _dot(kts[g][hh], ds2[hh])
            dq_ref[:, 128 * j:128 * (j + 1)] = dqt.T
            dk4 = _dot(jnp.concatenate(ds2, axis=0), qp)
            dv4 = _dot(jnp.concatenate(p2, axis=0), dop)
            for hh in range(2):
                own = low if hh == 0 else jnp.logical_not(low)
                dk_h = jnp.where(own, dk4[256 * hh:256 * (hh + 1), :], 0.0)
                dv_h = jnp.where(own, dv4[256 * hh:256 * (hh + 1), :], 0.0)
                if hh != g % 2:
                    dk_h = pltpu.roll(dk_h, HEAD_DIM, 1)
                    dv_h = pltpu.roll(dv_h, HEAD_DIM, 1)
                dkz[g] = dkz[g] + dk_h
                dvz[g] = dvz[g] + dv_h
        ds_ref[...] += dsink
        cur = pl.ds(pl.multiple_of(i * ATTN_BLOCK, ATTN_BLOCK), ATTN_BLOCK)
        prv = pl.ds(pl.multiple_of(jnp.maximum(i - 1, 0) * ATTN_BLOCK, ATTN_BLOCK), ATTN_BLOCK)
        for gp in range(N_KV_HEADS // 2):
            cols = slice(128 * gp, 128 * (gp + 1))
            dkb = dkz[2 * gp] + dkz[2 * gp + 1]
            dvb = dvz[2 * gp] + dvz[2 * gp + 1]
            dk_ref[cur, cols] = dkb[ATTN_BLOCK:, :]
            dv_ref[cur, cols] = dvb[ATTN_BLOCK:, :]

            @pl.when(i > 0)
            def _():
                dk_ref[prv, cols] += dkb[:ATTN_BLOCK, :]
                dv_ref[prv, cols] += dvb[:ATTN_BLOCK, :]

    prev = lambda i: jnp.maximum(i - 1, 0)
    whole = lambda w: pl.BlockSpec((T, w), lambda i: (0, 0))
    return pl.pallas_call(
        body, name=name, grid=(nb,),
        in_specs=[pl.BlockSpec((ATTN_BLOCK, D), lambda i: (i, 0)),
                  pl.BlockSpec((ATTN_BLOCK, 256), lambda i: (prev(i), 4)),
                  pl.BlockSpec((ATTN_BLOCK, 256), lambda i: (i, 4)),
                  pl.BlockSpec((ATTN_BLOCK, 256), lambda i: (prev(i), 5)),
                  pl.BlockSpec((ATTN_BLOCK, 256), lambda i: (i, 5)),
                  pl.BlockSpec((256, ATTN_BLOCK), lambda i: (0, prev(i))),
                  pl.BlockSpec((256, ATTN_BLOCK), lambda i: (0, i)),
                  pl.BlockSpec((1, N_Q_HEADS), lambda i: (0, 0)),
                  pl.BlockSpec((ATTN_BLOCK, D), lambda i: (i, 0))],
        out_specs=[pl.BlockSpec((ATTN_BLOCK, D), lambda i: (i, 0)), whole(256), whole(256),
                   pl.BlockSpec((1, 128), lambda i: (0, 0))],
        out_shape=[jax.ShapeDtypeStruct((T, D), F32), jax.ShapeDtypeStruct((T, 256), F32),
                   jax.ShapeDtypeStruct((T, 256), F32), jax.ShapeDtypeStruct((1, 128), F32)],
        compiler_params=_params("arbitrary"),
    )(qkv, qkv, qkv, qkv, qkv, kt, kt, sinks, dya)


HG_SUB = 256
HG_TILE = 1024
CPS = HG_SUB // HGRN_CHUNK


def _chunk_sum(tri, x):
    h1 = x.astype(BF16)
    h2 = (x - h1.astype(F32)).astype(BF16)
    r = _dot(tri, jnp.concatenate([h1, h2], axis=1))
    return r[:, :128] + r[:, 128:]


def _chunk_tri():
    ri = lax.broadcasted_iota(jnp.int32, (HG_SUB, HG_SUB), 0)
    ci = lax.broadcasted_iota(jnp.int32, (HG_SUB, HG_SUB), 1)
    return jnp.where((ri >= ci) & (ri // HGRN_CHUNK == ci // HGRN_CHUNK), 1.0, 0.0).astype(BF16)


def _chunks(t):
    return [t[HGRN_CHUNK * c:HGRN_CHUNK * (c + 1), :] for c in range(CPS)]


def _lower_bound(lbl_ref):
    l0, l1 = lbl_ref[0:1, :], lbl_ref[1:2, :]
    m = jnp.maximum(l0, l1)
    e0, e1 = jnp.exp(l0 - m), jnp.exp(l1 - m)
    return e0 / (e0 + e1)


def _hgrn_sub(fl, qh, vv, lb, tril_b):
    sg = _sigmoid(fl)
    f = lb + (1.0 - lb) * sg
    k = 1.0 - f
    gc = _chunk_sum(tril_b, jnp.log(f))
    gl_c = [t[HGRN_CHUNK - 1:HGRN_CHUNK, :] for t in _chunks(gc)]
    gl = jnp.concatenate([jnp.broadcast_to(g, (HGRN_CHUNK, 128)) for g in gl_c], axis=0)
    sq = _sigmoid(qh)
    eg = jnp.exp(gc)
    eng = jnp.exp(-gc)
    elg = jnp.exp(gl - gc)
    qd = qh * sq * eg
    ki = k * eng
    ke = k * elg
    qd_b, ki_b, ke_b, v_b = (t.astype(BF16) for t in (qd, ki, ke, vv))
    am_b = jnp.where(tril_b > 0, _dot_nt(qd_b, ki_b), 0.0).astype(BF16)
    return dict(sg=sg, f=f, sq=sq, eg=eg, eng=eng, elg=elg, qd=qd, ki=ki, ke=ke, egl=[jnp.exp(g) for g in gl_c],
                am_b=am_b, qd_b=qd_b, ki_b=ki_b, ke_b=ke_b, v_b=v_b)


def _hgrn_out(q, st_b):
    inter = [_dot_nt(qc, s) for qc, s in zip(_chunks(q["qd_b"]), st_b)]
    return _dot(q["am_b"], q["v_b"]) + jnp.concatenate(inter, axis=0)


def _hgrn_fwd(name, hg, lb_logits, norm_g, tri, th):
    T = hg.shape[0]
    nc = th // HGRN_CHUNK

    def body(fl_ref, qh_ref, v_ref, og_ref, lbl_ref, ng_ref, tri_ref, y_ref, ss_ref, st):
        @pl.when(pl.program_id(1) == 0)
        def _():
            st[...] = jnp.zeros_like(st)

        lb = _lower_bound(lbl_ref)
        ng = ng_ref[...]

        def sub(si, carry):
            rows = pl.ds(pl.multiple_of(si * HG_SUB, HG_SUB), HG_SUB)
            q = _hgrn_sub(fl_ref[rows, :], qh_ref[rows, :], v_ref[rows, :], lb, tri_ref[...])
            st_b, s = [], st[...]
            for c, (vc, kc) in enumerate(zip(_chunks(q["v_b"]), _chunks(q["ke_b"]))):
                ss_ref[0, si * CPS + c] = s
                st_b.append(s.astype(BF16))
                s = s * q["egl"][c] + _dot_tn(vc, kc)
            st[...] = s
            o = _hgrn_out(q, st_b)
            rr = lax.rsqrt(jnp.mean(o * o, axis=-1, keepdims=True) + RMS_EPS)
            og = og_ref[rows, :]
            y_ref[rows, :] = (o * rr * ng * (og * _sigmoid(og))).astype(BF16)
            return carry

        lax.fori_loop(0, th // HG_SUB, sub, 0)

    col = lambda j: pl.BlockSpec((th, 128), lambda h, t: (t, 8 * j + h))
    return pl.pallas_call(
        body, name=name, grid=(HGRN_HEADS, T // th),
        in_specs=[col(0), col(1), col(2), col(3), pl.BlockSpec((2, 128), lambda h, t: (0, h)),
                  pl.BlockSpec((1, 128), lambda h, t: (0, 0)), pl.BlockSpec((HG_SUB, HG_SUB), lambda h, t: (0, 0))],
        out_specs=[pl.BlockSpec((th, 128), lambda h, t: (t, h)),
                   pl.BlockSpec((1, nc, 128, 128), lambda h, t: (h, t, 0, 0))],
        out_shape=[jax.ShapeDtypeStruct((T, D), BF16),
                   jax.ShapeDtypeStruct((HGRN_HEADS, T // HGRN_CHUNK, 128, 128), F32)],
        scratch_shapes=[pltpu.VMEM((128, 128), F32)],
        compiler_params=_params("parallel", "arbitrary"),
    )(hg, hg, hg, hg, lb_logits, norm_g, tri)


def _hgrn_bwd(name, hg, lb_logits, norm_g, tri, tri_t, sstart, dyh, th):
    T = hg.shape[0]
    nc = th // HGRN_CHUNK
    nt = T // th

    def body(fl_ref, qh_ref, v_ref, og_ref, lbl_ref, ng_ref, tri_ref, trit_ref, ss_ref, dy_ref,
             dfl_ref, dqh_ref, dv_ref, dog_ref, dlb_ref, dng_ref, dst):
        @pl.when(pl.program_id(1) == 0)
        def _():
            dst[...] = jnp.zeros_like(dst)
            dlb_ref[...] = jnp.zeros_like(dlb_ref)
            dng_ref[...] = jnp.zeros_like(dng_ref)

        lb = _lower_bound(lbl_ref)
        ng = ng_ref[...]
        last = lax.broadcasted_iota(jnp.int32, (HGRN_CHUNK, 128), 0) == HGRN_CHUNK - 1
        nsub = th // HG_SUB

        def sub(step, carry):
            si = nsub - 1 - step
            rows = pl.ds(pl.multiple_of(si * HG_SUB, HG_SUB), HG_SUB)
            qh, og, dy = qh_ref[rows, :], og_ref[rows, :], dy_ref[rows, :]
            tril_b = tri_ref[...]
            q = _hgrn_sub(fl_ref[rows, :], qh, v_ref[rows, :], lb, tril_b)
            s_in = [ss_ref[0, si * CPS + c] for c in range(CPS)]
            st_b = [s.astype(BF16) for s in s_in]
            o = _hgrn_out(q, st_b)
            rr = lax.rsqrt(jnp.mean(o * o, axis=-1, keepdims=True) + RMS_EPS)
            oh = o * rr
            sog = _sigmoid(og)
            dog_ref[rows, :] = dy * oh * ng * (sog * (1.0 + og * (1.0 - sog)))
            don = dy * (og * sog)
            dng_ref[0] += jnp.sum(don * oh, axis=0, keepdims=True)
            dd = don * ng
            do_b = (rr * (dd - oh * jnp.mean(dd * oh, axis=-1, keepdims=True))).astype(BF16)
            da_b = jnp.where(tril_b > 0, _dot_nt(do_b, q["v_b"]), 0.0).astype(BF16)
            do_c, qd_c, ke_c, v_c = (_chunks(t) for t in (do_b, q["qd_b"], q["ke_b"], q["v_b"]))
            dsp, dgl_dec, d = [None] * CPS, [None] * CPS, dst[...]
            for c in reversed(range(CPS)):
                dsp[c] = d
                dgl_dec[c] = jnp.sum(d * s_in[c], axis=0, keepdims=True) * q["egl"][c]
                d = d * q["egl"][c] + _dot_tn(do_c[c], qd_c[c])
            dst[...] = d
            dsp_b = [t.astype(BF16) for t in dsp]
            dv_ref[rows, :] = _dot_tn(q["am_b"], do_b) + jnp.concatenate(
                [_dot_nt(ke_c[c], dsp_b[c]) for c in range(CPS)], axis=0)
            dqd = _dot(da_b, q["ki_b"]) + jnp.concatenate([_dot(do_c[c], st_b[c]) for c in range(CPS)], axis=0)
            dki = _dot_tn(da_b, q["qd_b"])
            dke = jnp.concatenate([_dot(v_c[c], dsp_b[c]) for c in range(CPS)], axis=0)
            dkk = dke * q["ke"]
            dgl = [jnp.sum(t, axis=0, keepdims=True) + dgl_dec[c] for c, t in enumerate(_chunks(dkk))]
            dgc = dqd * q["qd"] - dki * q["ki"] - dkk + jnp.concatenate(
                [jnp.where(last, g, 0.0) for g in dgl], axis=0)
            dlf = _chunk_sum(trit_ref[...], dgc)
            df = dlf / q["f"] - (dki * q["eng"] + dke * q["elg"])
            sg = q["sg"]
            dlb_ref[...] += jnp.sum(df * (1.0 - sg), axis=0, keepdims=True)
            dfl_ref[rows, :] = df * (1.0 - lb) * sg * (1.0 - sg)
            sq = q["sq"]
            dqh_ref[rows, :] = dqd * q["eg"] * (sq * (1.0 + qh * (1.0 - sq)))
            return carry

        lax.fori_loop(0, nsub, sub, 0)

    col = lambda j: pl.BlockSpec((th, 128), lambda h, t: (nt - 1 - t, 8 * j + h))
    out = pl.BlockSpec((th, 128), lambda h, t: (nt - 1 - t, h))
    tri_spec = pl.BlockSpec((HG_SUB, HG_SUB), lambda h, t: (0, 0))
    return pl.pallas_call(
        body, name=name, grid=(HGRN_HEADS, nt),
        in_specs=[col(0), col(1), col(2), col(3), pl.BlockSpec((2, 128), lambda h, t: (0, h)),
                  pl.BlockSpec((1, 128), lambda h, t: (0, 0)), tri_spec, tri_spec,
                  pl.BlockSpec((1, nc, 128, 128), lambda h, t: (h, nt - 1 - t, 0, 0)), out],
        out_specs=[out, out, out, out, pl.BlockSpec((1, 128), lambda h, t: (0, h)),
                   pl.BlockSpec((1, 1, 128), lambda h, t: (h, 0, 0))],
        out_shape=[jax.ShapeDtypeStruct((T, D), F32)] * 4 + [jax.ShapeDtypeStruct((1, D), F32),
                                                              jax.ShapeDtypeStruct((HGRN_HEADS, 1, 128), F32)],
        scratch_shapes=[pltpu.VMEM((128, 128), F32)],
        compiler_params=_params("parallel", "arbitrary"),
    )(hg, hg, hg, hg, lb_logits, norm_g, tri, tri_t, sstart, dyh)


def _mix_fwd(name, ya, yh, gates, xh1, g1, b1, wall, tm):
    T = ya.shape[0]

    def body(ya_ref, yh_ref, ga_ref, gh_ref, xh_ref, g_ref, b_ref, wall_ref,
             xo_ref, r_ref, pa_ref, ph_ref, mg_ref, wpa, wph, wo, sem):
        @pl.when(pl.program_id(0) == 0)
        def _():
            _load_rows(wall_ref, [(wpa, RIN, RP), (wph, RIN + RP, RP), (wo, RIN + 2 * RP, RP)], sem)

        pa = _dot(ya_ref[...], wpa[...]).astype(BF16)
        ph = _dot(yh_ref[...], wph[...]).astype(BF16)
        pa_ref[...] = pa
        ph_ref[...] = ph
        mg = (_sigmoid(ga_ref[...]) * pa.astype(F32) + _sigmoid(gh_ref[...]) * ph.astype(F32)).astype(BF16)
        mg_ref[...] = mg
        y1 = xh_ref[...] * g_ref[...] + b_ref[...]
        xh2, r = _ln_fwd(ALPHA * y1 + _dot(mg, wo[...]))
        xo_ref[...] = xh2
        r_ref[...] = jnp.broadcast_to(r, (tm, 128))

    row = lambda w: pl.BlockSpec((tm, w), lambda i: (i, 0))
    vec = pl.BlockSpec((1, D), lambda i: (0, 0))
    return pl.pallas_call(
        body, name=name, grid=(T // tm,),
        in_specs=[row(D), row(D), pl.BlockSpec((tm, D), lambda i: (i, 0)), pl.BlockSpec((tm, D), lambda i: (i, 1)),
                  row(D), vec, vec, HBM_SPEC],
        out_specs=[row(D), row(128), row(D), row(D), row(D)],
        out_shape=[jax.ShapeDtypeStruct((T, D), F32), jax.ShapeDtypeStruct((T, 128), F32)]
        + [jax.ShapeDtypeStruct((T, D), BF16)] * 3,
        scratch_shapes=[pltpu.VMEM((D, D), BF16)] * 3 + [pltpu.SemaphoreType.DMA((3 * NDEV,))],
        compiler_params=_params("arbitrary"),
    )(ya, yh, gates, gates, xh1, g1, b1, wall)


def _mix_bwd(name, dy2, xh2, r2, g2, gates, pa, ph, wall, tm):
    T = dy2.shape[0]

    def body(dy_ref, xh_ref, r_ref, g_ref, ga_ref, gh_ref, pa_ref, ph_ref, wall_ref,
             dz_ref, dmix_ref, dpa_ref, dph_ref, dga_ref, dgh_ref, dya_ref, dyh_ref, dg_ref, db_ref,
             wpa, wph, wo, sem):
        @pl.when(pl.program_id(0) == 0)
        def _():
            _load_rows(wall_ref, [(wpa, RIN, RP), (wph, RIN + RP, RP), (wo, RIN + 2 * RP, RP)], sem)
            dg_ref[...] = jnp.zeros_like(dg_ref)
            db_ref[...] = jnp.zeros_like(db_ref)

        dz, dgp, dbp = _ln_bwd(dy_ref[...], xh_ref[...], r_ref[:, :1], g_ref[...])
        dg_ref[...] += dgp
        db_ref[...] += dbp
        dz_ref[...] = dz
        dmix = dz.astype(BF16)
        dmix_ref[...] = dmix
        dmg = _dot_nt(dmix, wo[...])
        sa, sh = _sigmoid(ga_ref[...]), _sigmoid(gh_ref[...])
        dga_ref[...] = dmg * pa_ref[...].astype(F32) * sa * (1.0 - sa)
        dgh_ref[...] = dmg * ph_ref[...].astype(F32) * sh * (1.0 - sh)
        dpa = (dmg * sa).astype(BF16)
        dph = (dmg * sh).astype(BF16)
        dpa_ref[...] = dpa
        dph_ref[...] = dph
        dya_ref[...] = _dot_nt(dpa, wpa[...]).astype(BF16)
        dyh_ref[...] = _dot_nt(dph, wph[...])

    row = lambda w: pl.BlockSpec((tm, w), lambda i: (i, 0))
    vec = pl.BlockSpec((1, D), lambda i: (0, 0))
    return pl.pallas_call(
        body, name=name, grid=(T // tm,),
        in_specs=[row(D), row(D), row(128), vec, pl.BlockSpec((tm, D), lambda i: (i, 0)),
                  pl.BlockSpec((tm, D), lambda i: (i, 1)), row(D), row(D), HBM_SPEC],
        out_specs=[row(D)] * 8 + [vec, vec],
        out_shape=[jax.ShapeDtypeStruct((T, D), F32)] + [jax.ShapeDtypeStruct((T, D), BF16)] * 3
        + [jax.ShapeDtypeStruct((T, D), F32)] * 2 + [jax.ShapeDtypeStruct((T, D), BF16), jax.ShapeDtypeStruct((T, D), F32)]
        + [jax.ShapeDtypeStruct((1, D), F32)] * 2,
        scratch_shapes=[pltpu.VMEM((D, D), BF16)] * 3 + [pltpu.SemaphoreType.DMA((3 * NDEV,))],
        compiler_params=_params("arbitrary"),
    )(dy2, xh2, r2, g2, gates, gates, pa, ph, wall)


def _adam(w, g, m, v):
    m = ADAM_B1 * m + (1.0 - ADAM_B1) * g
    v = ADAM_B2 * v + (1.0 - ADAM_B2) * (g * g)
    m_hat = m / (1.0 - ADAM_B1 ** ADAM_STEP)
    v_hat = v / (1.0 - ADAM_B2 ** ADAM_STEP)
    delta = -ADAM_LR * (m_hat / (jnp.sqrt(v_hat) + ADAM_EPS) + ADAM_WD * w)
    return delta, m, v


def _sum_slots(name, recv):
    _, rows, _ = recv.shape

    def body(r_ref, o_ref):
        acc = r_ref[0].astype(F32)
        for k in range(1, NDEV):
            acc = acc + r_ref[k].astype(F32)
        o_ref[...] = acc

    return pl.pallas_call(
        body, name=name, out_shape=jax.ShapeDtypeStruct((rows, D), F32),
        compiler_params=pltpu.CompilerParams(vmem_limit_bytes=VMEM_LIMIT),
    )(recv)


def _adam_call(name, w, g, m, v):
    def body(w_ref, g_ref, m_ref, v_ref, d_ref, mo_ref, vo_ref):
        d, mm, vv = _adam(w_ref[...], g_ref[...], m_ref[...], v_ref[...])
        d_ref[...] = d
        mo_ref[...] = mm
        vo_ref[...] = vv

    return pl.pallas_call(
        body, name=name, out_shape=[jax.ShapeDtypeStruct(w.shape, F32)] * 3,
        compiler_params=pltpu.CompilerParams(vmem_limit_bytes=VMEM_LIMIT),
    )(w, g, m, v)


_SMALL = [("ln1_g", D), ("ln1_b", D), ("ln2_g", D), ("ln2_b", D), ("ln3_g", D), ("ln3_b", D),
          ("b_in", DIN), ("lb", D), ("attn_sinks", 128), ("hgrn_norm_g", 128), ("loss", 128)]
_SMALL_OFF = {}
_o = 0
for _n, _w in _SMALL:
    _SMALL_OFF[_n] = (_o, _w)
    _o += _w
PACK = _o


def _small_step(name, packed, small_w, small_m, small_v):
    names = ["ln1_g", "ln1_b", "ln2_g", "ln2_b", "ln3_g", "ln3_b", "b_in", "attn_sinks", "hgrn_lb_logits", "hgrn_norm_g"]
    np_ = len(names)

    def body(*refs):
        p_ref = refs[0]
        w_refs = refs[1:1 + np_]
        m_refs = refs[1 + np_:1 + 2 * np_]
        v_refs = refs[1 + 2 * np_:1 + 3 * np_]
        outs = refs[1 + 3 * np_:2 + 7 * np_]
        buf, send_sems, recv_sems = refs[2 + 7 * np_:]
        loss_ref, outs = outs[0], outs[1:]
        x, y, c = lax.axis_index("x"), lax.axis_index("y"), lax.axis_index("c")
        me = 4 * x + 2 * y + c
        buf[me] = p_ref[...]
        copies = []
        for j in range(1, NDEV):
            px, py, pc = x ^ (j >> 2), y ^ ((j >> 1) & 1), c ^ (j & 1)
            cp = pltpu.make_async_remote_copy(
                src_ref=p_ref, dst_ref=buf.at[me], send_sem=send_sems.at[j - 1], recv_sem=recv_sems.at[j - 1],
                device_id=(px, py, pc), device_id_type=MESH)
            cp.start()
            copies.append(cp)
        for cp in copies:
            cp.wait()
        tot = buf[0]
        for k in range(1, NDEV):
            tot = tot + buf[k]

        def part(n):
            o, w = _SMALL_OFF[n]
            return tot[:, o:o + w]

        loss_ref[...] = part("loss")
        for i, n in enumerate(names):
            w = w_refs[i][...]
            if n == "hgrn_lb_logits":
                m_ = jnp.maximum(w[0:1, :], w[1:2, :])
                e0, e1 = jnp.exp(w[0:1, :] - m_), jnp.exp(w[1:2, :] - m_)
                p0 = e0 / (e0 + e1)
                t = p0 * (1.0 - p0) * part("lb")
                g = jnp.concatenate([t, -t], axis=0)
            elif n == "attn_sinks":
                g = part(n)[:, :N_Q_HEADS]
            else:
                g = part(n)
            d, mm, vv = _adam(w, g, m_refs[i][...], v_refs[i][...])
            outs[4 * i][...] = g
            outs[4 * i + 1][...] = d
            outs[4 * i + 2][...] = mm
            outs[4 * i + 3][...] = vv

    out_shape = [jax.ShapeDtypeStruct((1, 128), F32)]
    for n in names:
        out_shape += [jax.ShapeDtypeStruct(small_w[n].shape, F32)] * 4
    return pl.pallas_call(
        body, name=name, out_shape=out_shape,
        scratch_shapes=[pltpu.VMEM((NDEV, 1, PACK), F32), pltpu.SemaphoreType.DMA((NDEV - 1,)),
                        pltpu.SemaphoreType.DMA((NDEV - 1,))],
        compiler_params=pltpu.CompilerParams(vmem_limit_bytes=VMEM_LIMIT),
    )(packed, *[small_w[n] for n in names], *[small_m[n] for n in names], *[small_v[n] for n in names]), names


def _tile(T, pref):
    return min(T, pref)


def kernel(x, ln1_g, ln1_b, ffn1_w1, ffn1_w3, ffn1_w2, ln2_g, ln2_b, w_in, b_in, attn_sinks, hgrn_lb_logits, hgrn_norm_g, w_proj_attn, w_proj_hgrn, w_out, ln3_g, ln3_b, ffn2_w1, ffn2_w3, ffn2_w2, loss_target, m_ln1_g, m_ln1_b, m_ffn1_w1, m_ffn1_w3, m_ffn1_w2, m_ln2_g, m_ln2_b, m_w_in, m_b_in, m_attn_sinks, m_hgrn_lb_logits, m_hgrn_norm_g, m_w_proj_attn, m_w_proj_hgrn, m_w_out, m_ln3_g, m_ln3_b, m_ffn2_w1, m_ffn2_w3, m_ffn2_w2, v_ln1_g, v_ln1_b, v_ffn1_w1, v_ffn1_w3, v_ffn1_w2, v_ln2_g, v_ln2_b, v_w_in, v_b_in, v_attn_sinks, v_hgrn_lb_logits, v_hgrn_norm_g, v_w_proj_attn, v_w_proj_hgrn, v_w_out, v_ln3_g, v_ln3_b, v_ffn2_w1, v_ffn2_w3, v_ffn2_w2):
    T = x.shape[1]
    xs = x[0]
    tgt = loss_target[0]
    tm = _tile(T, 256)
    tk = _tile(T, 512)
    th = _tile(T, HG_TILE)

    t_bf = lambda w: w[0].T.astype(BF16)
    n_bf = lambda w: w[0].astype(BF16)
    ffn1_all = _all_gather("gather_ffn1", jnp.concatenate([t_bf(ffn1_w1), t_bf(ffn1_w3), n_bf(ffn1_w2)], axis=0))
    mix_all = _all_gather("gather_mixer", jnp.concatenate(
        [t_bf(w_in), n_bf(w_proj_attn), n_bf(w_proj_hgrn), n_bf(w_out)], axis=0))
    ffn2_all = _all_gather("gather_ffn2", jnp.concatenate([t_bf(ffn2_w1), t_bf(ffn2_w3), n_bf(ffn2_w2)], axis=0))
    ffn_offs = (0, RF, 2 * RF)
    ropes = _rope_tables(T)
    tri = _chunk_tri()
    tri_t = tri.T

    xh1, r1, a1, b1, xb0 = _ffn_fwd("ffn1_fwd", xs, None, ffn1_all, ffn_offs, tm)
    qkv, kt, vt, hg, gates, y1b = _inproj_fwd("inproj_fwd", xh1, ln1_g, ln1_b, mix_all, b_in, ropes, tm)
    ya = _attn_fwd("attn_fwd", qkv, vt, attn_sinks)
    yh, sstart = _hgrn_fwd("hgrn_fwd", hg, hgrn_lb_logits, hgrn_norm_g, tri, th)
    xh2, r2, pa, ph, merged = _mix_fwd("mix_fwd", ya, yh, gates, xh1, ln1_g, ln1_b, mix_all, tm)
    xh3, r3, a2, b2, y2b, loss_part = _ffn_fwd("ffn2_fwd", xh2, (ln2_g, ln2_b), ffn2_all, ffn_offs, tm,
                                               loss=(ln3_g, ln3_b, tgt))

    dy2, dab2, u2, df2, dg3, db3 = _ffn_bwd("ffn2_bwd", ("loss", ln3_b, tgt), xh3, r3, ln3_g, a2, b2, ffn2_all,
                                            ffn_offs, tm)
    g_ffn2_13 = _wgrad("wgrad_ffn2_w13", dab2, y2b, DFF // 2, tk)
    g_ffn2_2 = _wgrad("wgrad_ffn2_w2", u2, df2, DFF // 2, tk)
    dz2, dmix, dpa, dph, dga, dgh, dya, dyh, dg2, db2 = _mix_bwd("mix_bwd", dy2, xh2, r2, ln2_g, gates, pa, ph,
                                                                 mix_all, tm)
    g_wo = _wgrad("wgrad_w_out", merged, dmix, D, tk)
    g_pa = _wgrad("wgrad_w_pa", ya, dpa, D, tk)
    g_ph = _wgrad("wgrad_w_ph", yh, dph, D, tk)
    dq, dk, dv, dsink = _attn_bwd("attn_bwd", qkv, kt, attn_sinks, dya)
    dfl, dqh, dih, dog, dlb, dng = _hgrn_bwd("hgrn_bwd", hg, hgrn_lb_logits, hgrn_norm_g, tri, tri_t, sstart, dyh, th)
    dy1, dproj, dbin = _inproj_bwd("inproj_bwd", dq, dk, dv, (dfl, dqh, dih, dog), (dga, dgh), dz2, mix_all, ropes, tm)
    g_win = _wgrad("wgrad_w_in", dproj, y1b, DIN // 4, tk)
    gx, dab1, u1, df1, dg1, db1 = _ffn_bwd("ffn1_bwd", ("dy", dy1), xh1, r1, ln1_g, a1, b1, ffn1_all, ffn_offs, tm)
    g_ffn1_13 = _wgrad("wgrad_ffn1_w13", dab1, xb0, DFF // 2, tk)
    g_ffn1_2 = _wgrad("wgrad_ffn1_w2", u1, df1, DFF // 2, tk)

    grads = [g_ffn1_13[:DFF], g_ffn1_13[DFF:], g_ffn1_2, g_win, g_pa, g_ph, g_wo, g_ffn2_13[:DFF], g_ffn2_13[DFF:], g_ffn2_2]
    rows = [RF, RF, RF, RIN, RP, RP, RP, RF, RF, RF]
    recv = _exchange_grads("exchange_grads", grads, rows)
    big = [("ffn1_w1", ffn1_w1, m_ffn1_w1, v_ffn1_w1, True), ("ffn1_w3", ffn1_w3, m_ffn1_w3, v_ffn1_w3, True),
           ("ffn1_w2", ffn1_w2, m_ffn1_w2, v_ffn1_w2, False), ("w_in", w_in, m_w_in, v_w_in, True),
           ("w_proj_attn", w_proj_attn, m_w_proj_attn, v_w_proj_attn, False),
           ("w_proj_hgrn", w_proj_hgrn, m_w_proj_hgrn, v_w_proj_hgrn, False),
           ("w_out", w_out, m_w_out, v_w_out, False),
           ("ffn2_w1", ffn2_w1, m_ffn2_w1, v_ffn2_w1, True), ("ffn2_w3", ffn2_w3, m_ffn2_w3, v_ffn2_w3, True),
           ("ffn2_w2", ffn2_w2, m_ffn2_w2, v_ffn2_w2, False)]
    res = {}
    for (n, w, m, v, transposed), rc in zip(big, recv):
        g = _sum_slots("sum_" + n, rc)
        if transposed:
            g = g.T
        d, mm, vv = _adam_call("adam_" + n, w[0], g, m[0], v[0])
        res[n] = tuple(t[None] for t in (g, d, mm, vv))

    parts = {"ln1_g": dg1, "ln1_b": db1, "ln2_g": dg2, "ln2_b": db2, "ln3_g": dg3, "ln3_b": db3, "b_in": dbin,
             "lb": dlb, "attn_sinks": dsink, "hgrn_norm_g": jnp.sum(dng, axis=0), "loss": loss_part[0:1, :]}
    packed = jnp.concatenate([parts[n] for n, _ in _SMALL], axis=1)
    small_w = dict(ln1_g=ln1_g, ln1_b=ln1_b, ln2_g=ln2_g, ln2_b=ln2_b, ln3_g=ln3_g, ln3_b=ln3_b, b_in=b_in,
                   attn_sinks=attn_sinks, hgrn_lb_logits=hgrn_lb_logits, hgrn_norm_g=hgrn_norm_g)
    small_m = dict(ln1_g=m_ln1_g, ln1_b=m_ln1_b, ln2_g=m_ln2_g, ln2_b=m_ln2_b, ln3_g=m_ln3_g, ln3_b=m_ln3_b,
                   b_in=m_b_in, attn_sinks=m_attn_sinks, hgrn_lb_logits=m_hgrn_lb_logits, hgrn_norm_g=m_hgrn_norm_g)
    small_v = dict(ln1_g=v_ln1_g, ln1_b=v_ln1_b, ln2_g=v_ln2_g, ln2_b=v_ln2_b, ln3_g=v_ln3_g, ln3_b=v_ln3_b,
                   b_in=v_b_in, attn_sinks=v_attn_sinks, hgrn_lb_logits=v_hgrn_lb_logits, hgrn_norm_g=v_hgrn_norm_g)
    outs, names = _small_step("small_step", packed, small_w, small_m, small_v)
    loss = outs[0][0, 0]
    for i, n in enumerate(names):
        res[n] = tuple(outs[1 + 4 * i:5 + 4 * i])

    order = ["ln1_g", "ln1_b", "ffn1_w1", "ffn1_w3", "ffn1_w2", "ln2_g", "ln2_b", "w_in", "b_in", "attn_sinks",
             "hgrn_lb_logits", "hgrn_norm_g", "w_proj_attn", "w_proj_hgrn", "w_out", "ln3_g", "ln3_b",
             "ffn2_w1", "ffn2_w3", "ffn2_w2"]
    return (loss, gx[None], *[res[n][0] for n in order], *[res[n][1] for n in order],
            *[res[n][2] for n in order], *[res[n][3] for n in order])
```

```python
import jax
import jax.numpy as jnp
from jax import lax
from jax.experimental import pallas as pl
from jax.experimental.pallas import tpu as pltpu

F32 = jnp.float32
BF16 = jnp.bfloat16

NDEV = 8
D = 1024
DFF = 2816
RF = DFF // NDEV
DIN = 7680
RIN = DIN // NDEV
RP = D // NDEV
N_Q_HEADS = 16
N_KV_HEADS = 4
HEAD_DIM = 64
ATTN_BLOCK = 128
ROPE_THETA = 500000.0
ROPE_DIM = HEAD_DIM // 4
HGRN_HEADS = 8
HGRN_DK = 128
HGRN_CHUNK = 64
ALPHA = 2.0 ** 0.25
LN_EPS = 1e-5
RMS_EPS = 1e-6
NEG_INF = -1e30
ADAM_LR = 0.001
ADAM_B1 = 0.9
ADAM_B2 = 0.999
ADAM_EPS = 1e-08
ADAM_WD = 0.01
ADAM_STEP = 10

QKV_W = 1536
HG_W = 4096
GATE_W = 2048
VMEM_LIMIT = 60 * 2 ** 20

MESH = pl.DeviceIdType.MESH
HBM_SPEC = pl.BlockSpec(memory_space=pltpu.HBM)


def _params(*sem):
    return pltpu.CompilerParams(dimension_semantics=sem, vmem_limit_bytes=VMEM_LIMIT)


def _dot(a, b):
    return jnp.dot(a, b, preferred_element_type=F32)


def _dot_nt(a, b):
    return lax.dot_general(a, b, (((1,), (1,)), ((), ())), preferred_element_type=F32)


def _dot_tn(a, b):
    return lax.dot_general(a, b, (((0,), (0,)), ((), ())), preferred_element_type=F32)


def _sigmoid(x):
    return 1.0 / (1.0 + jnp.exp(-x))


def _ln_fwd(z):
    mu = jnp.mean(z, axis=-1, keepdims=True)
    zc = z - mu
    var = jnp.mean(zc * zc, axis=-1, keepdims=True)
    r = lax.rsqrt(var + LN_EPS)
    return zc * r, r


def _ln_bwd(dy, xh, r, g):
    dxh = dy * g
    m1 = jnp.mean(dxh, axis=-1, keepdims=True)
    m2 = jnp.mean(dxh * xh, axis=-1, keepdims=True)
    dz = r * (dxh - m1 - xh * m2)
    return dz, jnp.sum(dy * xh, axis=0, keepdims=True), jnp.sum(dy, axis=0, keepdims=True)


def _load_rows(wall_ref, pieces, sem):
    copies = []
    for dst, off, r in pieces:
        for k in range(NDEV):
            c = pltpu.make_async_copy(wall_ref.at[k, pl.ds(off, r), :], dst.at[pl.ds(k * r, r), :], sem.at[len(copies)])
            c.start()
            copies.append(c)
    for c in copies:
        c.wait()


class _Exchange:
    def __init__(self, inputs, out_shape, scratch, begin, middle, end):
        self.inputs, self.out_shape, self.scratch = inputs, out_shape, scratch
        self.begin, self.middle, self.end = begin, middle, end


def _gather_exchange(shard):
    rows, cols = shard.shape

    def ops(ins, outs, scr):
        (x_ref,), (out_ref,), (send_sems, recv_sems, local_sem) = ins, outs, scr
        x, y, c = lax.axis_index("x"), lax.axis_index("y"), lax.axis_index("c")
        me, sibling = (x, y, c), (x, y, 1 - c)
        chips = [(1 - x, y), (x, 1 - y), (1 - x, 1 - y)]

        def slot(px, py, pc):
            return out_ref.at[4 * px + 2 * py + pc]

        def copy(k, block, to, src=None):
            return pltpu.make_async_remote_copy(
                src_ref=slot(*block) if src is None else src, dst_ref=slot(*block),
                send_sem=send_sems.at[k], recv_sem=recv_sems.at[k], device_id=to, device_id_type=MESH)

        mine = lambda: pltpu.make_async_copy(x_ref, slot(*me), local_sem)
        first = lambda: [copy(0, me, sibling, src=x_ref)] + [
            copy(1 + j, me, (*chip, c), src=x_ref) for j, chip in enumerate(chips)]
        passed = lambda: [copy(4 + j, (*chip, c), sibling) for j, chip in enumerate(chips)]
        return c, me, sibling, chips, copy, mine, first, passed

    def begin(*refs):
        _, _, _, _, _, mine, first, _ = ops(*refs)
        mine().start()
        for cp in first():
            cp.start()

    def middle(*refs):
        c, me, _, chips, copy, _, _, passed = ops(*refs)
        for (j, chip), fwd in zip(enumerate(chips), passed()):
            copy(1 + j, (*chip, c), me).wait_recv()
            fwd.start()

    def end(*refs):
        c, me, sibling, chips, copy, mine, first, passed = ops(*refs)
        copy(0, sibling, me).wait_recv()
        for j, chip in enumerate(chips):
            copy(4 + j, (*chip, 1 - c), me).wait_recv()
        for cp in first() + passed():
            cp.wait_send()
        mine().wait()

    return _Exchange([shard], [jax.ShapeDtypeStruct((NDEV, rows, cols), shard.dtype)],
                     [pltpu.SemaphoreType.DMA((7,)), pltpu.SemaphoreType.DMA((7,)), pltpu.SemaphoreType.DMA],
                     begin, middle, end)


def _grad_exchange(grads, bases, rows):
    n = len(grads)

    def copies(g_refs, out_refs, scr):
        send_sems, recv_sems, local_sems = scr
        x, y, c = lax.axis_index("x"), lax.axis_index("y"), lax.axis_index("c")
        me = 4 * x + 2 * y + c
        out = []
        for i in range(n):
            r = rows[i]
            src = lambda k: g_refs[i].at[pl.ds(pl.multiple_of(bases[i] + k * r, 16), r), :]
            out.append(pltpu.make_async_copy(src(me), out_refs[i].at[me], local_sems.at[i]))
            for j in range(1, NDEV):
                px, py, pc = x ^ (j >> 2), y ^ ((j >> 1) & 1), c ^ (j & 1)
                out.append(pltpu.make_async_remote_copy(
                    src_ref=src(4 * px + 2 * py + pc), dst_ref=out_refs[i].at[me],
                    send_sem=send_sems.at[i, j - 1], recv_sem=recv_sems.at[i, j - 1],
                    device_id=(px, py, pc), device_id_type=MESH))
        return out

    def begin(*refs):
        for cp in copies(*refs):
            cp.start()

    def end(*refs):
        for cp in copies(*refs):
            cp.wait()

    return _Exchange(list(grads), [jax.ShapeDtypeStruct((NDEV, r, D), g.dtype) for g, r in zip(grads, rows)],
                     [pltpu.SemaphoreType.DMA((n, NDEV - 1)), pltpu.SemaphoreType.DMA((n, NDEV - 1)),
                      pltpu.SemaphoreType.DMA((n,))], begin, None, end)


def _exchange_call(name, ex):
    ni, no = len(ex.inputs), len(ex.out_shape)

    def body(*refs):
        parts = (refs[:ni], refs[ni:ni + no], refs[ni + no:])
        ex.begin(*parts)
        if ex.middle is not None:
            ex.middle(*parts)
        ex.end(*parts)

    return pl.pallas_call(body, name=name, out_shape=ex.out_shape, in_specs=[HBM_SPEC] * ni, out_specs=[HBM_SPEC] * no,
                          scratch_shapes=ex.scratch)(*ex.inputs)


def _call(body, *, name, grid, in_specs, out_specs, out_shape, scratch_shapes, ins, carry=None):
    sem = ("arbitrary",) * len(grid)
    if carry is None:
        outs = pl.pallas_call(body, name=name, grid=grid, in_specs=in_specs, out_specs=out_specs, out_shape=out_shape,
                              scratch_shapes=scratch_shapes, compiler_params=_params(*sem))(*ins)
        return outs, None
    n_in, n_out, n_scr = len(ins), len(out_shape), len(scratch_shapes)
    ci, co = len(carry.inputs), len(carry.out_shape)
    total = 1
    for g in grid:
        total *= g

    def wrapped(*refs):
        own_in, ex_in = refs[:n_in], refs[n_in:n_in + ci]
        o0 = n_in + ci
        own_out, ex_out = refs[o0:o0 + n_out], refs[o0 + n_out:o0 + n_out + co]
        s0 = o0 + n_out + co
        own_scr, ex_scr = refs[s0:s0 + n_scr], refs[s0 + n_scr:]
        step = pl.program_id(0)
        for d in range(1, len(grid)):
            step = step * grid[d] + pl.program_id(d)
        parts = (ex_in, ex_out, ex_scr)
        pl.when(step == 0)(lambda: carry.begin(*parts))
        body(*own_in, *own_out, *own_scr)
        if carry.middle is not None:
            pl.when(step == total // 2)(lambda: carry.middle(*parts))
        pl.when(step == total - 1)(lambda: carry.end(*parts))

    outs = pl.pallas_call(
        wrapped, name=name, grid=grid, in_specs=list(in_specs) + [HBM_SPEC] * ci,
        out_specs=list(out_specs) + [HBM_SPEC] * co, out_shape=list(out_shape) + list(carry.out_shape),
        scratch_shapes=list(scratch_shapes) + list(carry.scratch), compiler_params=_params(*sem),
    )(*ins, *carry.inputs)
    return outs[:n_out], outs[n_out:]


def _ffn_fwd(name, xin, affine, wall, offs, tm, loss=None, carry=None):
    T = xin.shape[0]
    nt = T // tm
    fc = DFF // 2

    def body(*refs):
        it = iter(refs)
        x_ref = next(it)
        if affine is not None:
            g_ref, b_ref = next(it), next(it)
        wall_ref = next(it)
        if loss is not None:
            go_ref, bo_ref, tgt_ref = next(it), next(it), next(it)
        xh_ref, r_ref, a_ref, b2_ref, yb_ref = (next(it) for _ in range(5))
        if loss is not None:
            loss_ref = next(it)
        w1, w3, w2, sem = (next(it) for _ in range(4))

        @pl.when(pl.program_id(0) == 0)
        def _():
            _load_rows(wall_ref, [(w1, offs[0], RF), (w3, offs[1], RF), (w2, offs[2], RF)], sem)
            if loss is not None:
                loss_ref[...] = jnp.zeros_like(loss_ref)

        y = x_ref[...]
        if affine is not None:
            y = y * g_ref[...] + b_ref[...]
        yb = y.astype(BF16)
        yb_ref[...] = yb
        f = jnp.zeros((tm, D), F32)
        for ci in range(DFF // fc):
            cs = slice(ci * fc, (ci + 1) * fc)
            a = _dot_nt(yb, w1[cs, :]).astype(BF16)
            b = _dot_nt(yb, w3[cs, :]).astype(BF16)
            a_ref[:, cs] = a
            b2_ref[:, cs] = b
            af, bf = a.astype(F32), b.astype(F32)
            u = (af * _sigmoid(af) * bf).astype(BF16)
            f = f + _dot(u, w2[cs, :])
        xh, r = _ln_fwd(ALPHA * y + 0.5 * f)
        xh_ref[...] = xh
        r_ref[...] = jnp.broadcast_to(r, (tm, 128))
        if loss is not None:
            e = xh * go_ref[...] + bo_ref[...] - tgt_ref[...]
            loss_ref[...] += jnp.sum(e * e) * (0.5 / D)

    row = lambda w: pl.BlockSpec((tm, w), lambda i: (i, 0))
    vec = pl.BlockSpec((1, D), lambda i: (0, 0))
    ins, in_specs = [xin], [row(D)]
    if affine is not None:
        ins += list(affine)
        in_specs += [vec, vec]
    ins.append(wall)
    in_specs.append(HBM_SPEC)
    if loss is not None:
        ins += list(loss)
        in_specs += [vec, vec, row(D)]
    out_shape = [jax.ShapeDtypeStruct((T, D), F32), jax.ShapeDtypeStruct((T, 128), F32),
                 jax.ShapeDtypeStruct((T, DFF), BF16), jax.ShapeDtypeStruct((T, DFF), BF16),
                 jax.ShapeDtypeStruct((T, D), BF16)]
    out_specs = [row(D), row(128), row(DFF), row(DFF), row(D)]
    if loss is not None:
        out_shape.append(jax.ShapeDtypeStruct((8, 128), F32))
        out_specs.append(pl.BlockSpec((8, 128), lambda i: (0, 0)))
    return _call(body, name=name, grid=(nt,), in_specs=in_specs, out_specs=out_specs, out_shape=out_shape,
                 scratch_shapes=[pltpu.VMEM((DFF, D), BF16)] * 3 + [pltpu.SemaphoreType.DMA((3 * NDEV,))],
                 ins=ins, carry=carry)


def _ffn_bwd(name, dy_src, xh, r, g, a, b, wall, offs, tm, carry=None):
    T = xh.shape[0]
    nt = T // tm
    fc = DFF // 2
    from_loss = dy_src[0] == "loss"

    def body(*refs):
        it = iter(refs)
        if from_loss:
            bo_ref, tgt_ref = next(it), next(it)
        else:
            dy_ref = next(it)
        xh_ref, r_ref, g_ref, a_ref, b_ref, wall_ref = (next(it) for _ in range(6))
        dyin_ref, dab_ref, u_ref, df_ref, dg_ref, db_ref = (next(it) for _ in range(6))
        w1, w3, w2, sem = (next(it) for _ in range(4))

        @pl.when(pl.program_id(0) == 0)
        def _():
            _load_rows(wall_ref, [(w1, offs[0], RF), (w3, offs[1], RF), (w2, offs[2], RF)], sem)
            dg_ref[...] = jnp.zeros_like(dg_ref)
            db_ref[...] = jnp.zeros_like(db_ref)

        xhv = xh_ref[...]
        gv = g_ref[...]
        if from_loss:
            dy = (xhv * gv + bo_ref[...] - tgt_ref[...]) * (1.0 / D)
        else:
            dy = dy_ref[...]
        dz, dgp, dbp = _ln_bwd(dy, xhv, r_ref[:, :1], gv)
        dg_ref[...] += dgp
        db_ref[...] += dbp
        df = (0.5 * dz).astype(BF16)
        df_ref[...] = df
        dx = ALPHA * dz
        for ci in range(DFF // fc):
            cs = slice(ci * fc, (ci + 1) * fc)
            du = _dot_nt(df, w2[cs, :])
            af, bf = a_ref[:, cs].astype(F32), b_ref[:, cs].astype(F32)
            s = _sigmoid(af)
            sl = af * s
            u_ref[:, cs] = (sl * bf).astype(BF16)
            da = (du * bf * (s * (1.0 + af * (1.0 - s)))).astype(BF16)
            dbb = (du * sl).astype(BF16)
            dab_ref[:, cs] = da
            dab_ref[:, DFF + ci * fc:DFF + (ci + 1) * fc] = dbb
            dx = dx + _dot(da, w1[cs, :]) + _dot(dbb, w3[cs, :])
        dyin_ref[...] = dx

    row = lambda w: pl.BlockSpec((tm, w), lambda i: (i, 0))
    vec = pl.BlockSpec((1, D), lambda i: (0, 0))
    if from_loss:
        ins, in_specs = [dy_src[1], dy_src[2]], [vec, row(D)]
    else:
        ins, in_specs = [dy_src[1]], [row(D)]
    ins += [xh, r, g, a, b, wall]
    in_specs += [row(D), row(128), vec, row(DFF), row(DFF), HBM_SPEC]
    return _call(
        body, name=name, grid=(nt,), in_specs=in_specs,
        out_specs=[row(D), row(2 * DFF), row(DFF), row(D), vec, vec],
        out_shape=[jax.ShapeDtypeStruct((T, D), F32), jax.ShapeDtypeStruct((T, 2 * DFF), BF16),
                   jax.ShapeDtypeStruct((T, DFF), BF16), jax.ShapeDtypeStruct((T, D), BF16),
                   jax.ShapeDtypeStruct((1, D), F32), jax.ShapeDtypeStruct((1, D), F32)],
        scratch_shapes=[pltpu.VMEM((DFF, D), BF16)] * 3 + [pltpu.SemaphoreType.DMA((3 * NDEV,))],
        ins=ins, carry=carry)


def _wgrad(name, a, b, bn, tk, col0=0, ncols=None, carry=None):
    T = a.shape[0]
    N = a.shape[1] if ncols is None else ncols
    nk = T // tk
    c0 = col0 // bn

    def body(a_ref, b_ref, o_ref, acc):
        k = pl.program_id(1)

        @pl.when(k == 0)
        def _():
            acc[...] = jnp.zeros_like(acc)

        acc[...] += _dot_tn(a_ref[...], b_ref[...])

        @pl.when(k == nk - 1)
        def _():
            o_ref[...] = acc[...].astype(BF16)

    (out,), ex = _call(
        body, name=name, grid=(N // bn, nk),
        in_specs=[pl.BlockSpec((tk, bn), lambda n, k: (k, n + c0)), pl.BlockSpec((tk, D), lambda n, k: (k, 0))],
        out_specs=[pl.BlockSpec((bn, D), lambda n, k: (n, 0))],
        out_shape=[jax.ShapeDtypeStruct((N, D), BF16)],
        scratch_shapes=[pltpu.VMEM((bn, D), F32)], ins=[a, b], carry=carry)
    return out, ex


def _rope_tables(T):
    pos = jnp.arange(T, dtype=F32)
    inv_freq = ROPE_THETA ** (-jnp.arange(0, ROPE_DIM, 2, dtype=F32) / ROPE_DIM)
    ang = pos[:, None] * inv_freq[None, :]
    cos, sin = lax.optimization_barrier((jnp.cos(ang), jnp.sin(ang)))
    half = ROPE_DIM // 2
    ones = jnp.ones((T, HEAD_DIM - ROPE_DIM), F32)
    zeros = jnp.zeros((T, HEAD_DIM - ROPE_DIM), F32)
    zh = jnp.zeros((T, half), F32)
    c1 = jnp.concatenate([cos, cos, ones], axis=1)
    s1 = jnp.concatenate([-sin, zh, zeros], axis=1)
    s2 = jnp.concatenate([zh, sin, zeros], axis=1)
    two = lambda t: jnp.concatenate([t, t], axis=1)
    return two(c1), two(s1), two(s2)


def _rope(t, c, s1, s2):
    n = t.shape[1] // 128
    ct, s1t, s2t = (jnp.tile(v, (1, n)) for v in (c, s1, s2))
    w = t.shape[1]
    return t * ct + pltpu.roll(t, w - 8, 1) * s1t + pltpu.roll(t, 8, 1) * s2t


def _rope_t(dr, c, s1, s2):
    n = dr.shape[1] // 128
    ct, s1t, s2t = (jnp.tile(v, (1, n)) for v in (c, s1, s2))
    w = dr.shape[1]
    return dr * ct + pltpu.roll(dr * s1t, 8, 1) + pltpu.roll(dr * s2t, w - 8, 1)


_Q, _K, _V = (0, 1024), (1024, 256), (1280, 256)
_HG = (1536, HG_W)
_GATES = (5632, GATE_W)


def _inproj_fwd(name, xh, g, b, wall, b_in, ropes, tm, carry=None):
    T = xh.shape[0]

    def body(xh_ref, g_ref, b_ref, wall_ref, bin_ref, c_ref, s1_ref, s2_ref,
             qkv_ref, kt_ref, vt_ref, hg_ref, gate_ref, yb_ref, w, sem):
        @pl.when(pl.program_id(0) == 0)
        def _():
            _load_rows(wall_ref, [(w, 0, RIN)], sem)

        yb = (xh_ref[...] * g_ref[...] + b_ref[...]).astype(BF16)
        yb_ref[...] = yb
        c, s1, s2 = c_ref[...], s1_ref[...], s2_ref[...]

        def piece(start, width):
            return _dot_nt(yb, w[start:start + width, :]) + bin_ref[:, start:start + width]

        k = _rope(piece(*_K), c, s1, s2)
        v = piece(*_V)
        qkv_ref[:, 0:1024] = _rope(piece(*_Q), c, s1, s2).astype(BF16)
        qkv_ref[:, 1024:1280] = k.astype(BF16)
        qkv_ref[:, 1280:1536] = v.astype(BF16)
        kt_ref[...] = k.T.astype(BF16)
        vt_ref[...] = v.T.astype(BF16)
        for j in range(4):
            hg_ref[:, 1024 * j:1024 * (j + 1)] = piece(_HG[0] + 1024 * j, 1024)
        for j in range(2):
            gate_ref[:, 1024 * j:1024 * (j + 1)] = piece(_GATES[0] + 1024 * j, 1024)

    row = lambda wd: pl.BlockSpec((tm, wd), lambda i: (i, 0))
    vec = lambda wd: pl.BlockSpec((1, wd), lambda i: (0, 0))
    colt = pl.BlockSpec((256, tm), lambda i: (0, i))
    return _call(
        body, name=name, grid=(T // tm,),
        in_specs=[row(D), vec(D), vec(D), HBM_SPEC, vec(DIN), row(128), row(128), row(128)],
        out_specs=[row(QKV_W), colt, colt, row(HG_W), row(GATE_W), row(D)],
        out_shape=[jax.ShapeDtypeStruct((T, QKV_W), BF16), jax.ShapeDtypeStruct((256, T), BF16),
                   jax.ShapeDtypeStruct((256, T), BF16), jax.ShapeDtypeStruct((T, HG_W), F32),
                   jax.ShapeDtypeStruct((T, GATE_W), F32), jax.ShapeDtypeStruct((T, D), BF16)],
        scratch_shapes=[pltpu.VMEM((DIN, D), BF16), pltpu.SemaphoreType.DMA((NDEV,))],
        ins=[xh, g, b, wall, b_in, *ropes], carry=carry)


def _inproj_bwd(name, dq, dk, dv, dhg, dgates, dz2, wall, ropes, tm):
    T = dq.shape[0]

    def body(dq_ref, dk_ref, dv_ref, d0, d1, d2, d3, dga_ref, dgh_ref, dz_ref, wall_ref, c_ref, s1_ref, s2_ref,
             dy_ref, dproj_ref, dbin_ref, w, sem):
        @pl.when(pl.program_id(0) == 0)
        def _():
            _load_rows(wall_ref, [(w, 0, RIN)], sem)
            dbin_ref[...] = jnp.zeros_like(dbin_ref)

        c, s1, s2 = c_ref[...], s1_ref[...], s2_ref[...]
        acc = ALPHA * dz_ref[...]
        pieces = [(_Q[0], _rope_t(dq_ref[...], c, s1, s2)), (_K[0], _rope_t(dk_ref[...], c, s1, s2)),
                  (_V[0], dv_ref[...])]
        pieces += [(_HG[0] + 1024 * j, r[...]) for j, r in enumerate((d0, d1, d2, d3))]
        pieces += [(_GATES[0], dga_ref[...]), (_GATES[0] + 1024, dgh_ref[...])]
        for start, val in pieces:
            width = val.shape[1]
            dbin_ref[:, start:start + width] += jnp.sum(val, axis=0, keepdims=True)
            vb = val.astype(BF16)
            dproj_ref[:, start:start + width] = vb
            acc = acc + _dot(vb, w[start:start + width, :])
        dy_ref[...] = acc

    row = lambda wd: pl.BlockSpec((tm, wd), lambda i: (i, 0))
    return pl.pallas_call(
        body, name=name, grid=(T // tm,),
        in_specs=[row(D), row(256), row(256)] + [row(D)] * 4 + [row(D), row(D), row(D), HBM_SPEC] + [row(128)] * 3,
        out_specs=[row(D), row(DIN), pl.BlockSpec((1, DIN), lambda i: (0, 0))],
        out_shape=[jax.ShapeDtypeStruct((T, D), F32), jax.ShapeDtypeStruct((T, DIN), BF16),
                   jax.ShapeDtypeStruct((1, DIN), F32)],
        scratch_shapes=[pltpu.VMEM((DIN, D), BF16), pltpu.SemaphoreType.DMA((NDEV,))],
        compiler_params=_params("arbitrary"),
    )(dq, dk, dv, *dhg, *dgates, dz2, wall, *ropes)


def _halves(t):
    lane = lax.broadcasted_iota(jnp.int32, t.shape, 1)
    low = lane < HEAD_DIM
    sw = pltpu.roll(t, HEAD_DIM, 1)
    zero = jnp.zeros_like(t)
    h0 = (jnp.where(low, t, zero), jnp.where(low, zero, sw))
    h1 = (jnp.where(low, sw, zero), jnp.where(low, zero, t))
    return h0, h1


def _lane_stack(p_ref, c_ref, gp):
    sl = slice(128 * gp, 128 * (gp + 1))
    hp, hc = _halves(p_ref[:, sl].astype(F32)), _halves(c_ref[:, sl].astype(F32))
    return [jnp.concatenate([hp[gg][0], hc[gg][0], hp[gg][1], hc[gg][1]], axis=0).astype(BF16) for gg in range(2)]


def _row_stack(tp_ref, tc_ref, g):
    band = jnp.concatenate([tp_ref[64 * g:64 * (g + 1), :], tc_ref[64 * g:64 * (g + 1), :]], axis=1)
    z = jnp.zeros_like(band)
    return [jnp.concatenate([band, z], axis=0), jnp.concatenate([z, band], axis=0)]


def _fold(prevm, t4, hh):
    return jnp.where(prevm, t4[256 * hh:256 * hh + 128, :], t4[256 * hh + 128:256 * hh + 256, :])


def _unfold(prevm, t):
    return jnp.concatenate([jnp.where(prevm, t, 0.0), jnp.where(prevm, 0.0, t)], axis=0).astype(BF16)


def _attn_softmax(s, kill, sink):
    s = jnp.where(kill, NEG_INF, s)
    m = jnp.maximum(jnp.max(s, axis=0, keepdims=True), sink)
    p = jnp.exp(s - m)
    es = jnp.exp(sink - m)
    denom = jnp.sum(p, axis=0, keepdims=True) + es
    return p / denom, es / denom


def _attn_masks(first):
    row = lax.broadcasted_iota(jnp.int32, (ATTN_BLOCK, ATTN_BLOCK), 0)
    lane = lax.broadcasted_iota(jnp.int32, (ATTN_BLOCK, ATTN_BLOCK), 1)
    prevm = row > lane
    return prevm, jnp.logical_and(first, prevm)


def _attn_fwd(name, qkv, vt, sinks):
    T = qkv.shape[0]
    nb = T // ATTN_BLOCK
    scale = HEAD_DIM ** -0.5

    def body(q_ref, kp_ref, kc_ref, vtp_ref, vtc_ref, sink_ref, o_ref):
        prevm, kill = _attn_masks(pl.program_id(0) == 0)
        kst = _lane_stack(kp_ref, kc_ref, 0) + _lane_stack(kp_ref, kc_ref, 1)
        vts = [_row_stack(vtp_ref, vtc_ref, g) for g in range(N_KV_HEADS)]
        for j in range(N_Q_HEADS // 2):
            g = j // 2
            qp = q_ref[:, 128 * j:128 * (j + 1)]
            s4 = _dot_nt(kst[g], qp) * scale
            ot = jnp.zeros((128, ATTN_BLOCK), F32)
            for hh in range(2):
                h = 2 * j + hh
                pn, _ = _attn_softmax(_fold(prevm, s4, hh), kill, sink_ref[:, h:h + 1])
                ot = ot + _dot(vts[g][hh], _unfold(prevm, pn))
            o_ref[:, 128 * j:128 * (j + 1)] = ot.T.astype(BF16)

    prev = lambda i: jnp.maximum(i - 1, 0)
    return pl.pallas_call(
        body, name=name, grid=(nb,),
        in_specs=[pl.BlockSpec((ATTN_BLOCK, D), lambda i: (i, 0)),
                  pl.BlockSpec((ATTN_BLOCK, 256), lambda i: (prev(i), 4)),
                  pl.BlockSpec((ATTN_BLOCK, 256), lambda i: (i, 4)),
                  pl.BlockSpec((256, ATTN_BLOCK), lambda i: (0, prev(i))),
                  pl.BlockSpec((256, ATTN_BLOCK), lambda i: (0, i)),
                  pl.BlockSpec((1, N_Q_HEADS), lambda i: (0, 0))],
        out_specs=pl.BlockSpec((ATTN_BLOCK, D), lambda i: (i, 0)),
        out_shape=jax.ShapeDtypeStruct((T, D), BF16),
        compiler_params=_params("arbitrary"),
    )(qkv, qkv, qkv, vt, vt, sinks)


def _attn_bwd(name, qkv, kt, sinks, dya, carry=None):
    T = qkv.shape[0]
    nb = T // ATTN_BLOCK
    scale = HEAD_DIM ** -0.5

    def body(q_ref, kp_ref, kc_ref, vp_ref, vc_ref, ktp_ref, ktc_ref, sink_ref, do_ref,
             dq_ref, dk_ref, dv_ref, ds_ref):
        i = pl.program_id(0)

        @pl.when(i == 0)
        def _():
            ds_ref[...] = jnp.zeros_like(ds_ref)

        prevm, kill = _attn_masks(i == 0)
        kst = _lane_stack(kp_ref, kc_ref, 0) + _lane_stack(kp_ref, kc_ref, 1)
        vst = _lane_stack(vp_ref, vc_ref, 0) + _lane_stack(vp_ref, vc_ref, 1)
        kts = [_row_stack(ktp_ref, ktc_ref, g) for g in range(N_KV_HEADS)]
        lane = lax.broadcasted_iota(jnp.int32, (2 * ATTN_BLOCK, 128), 1)
        low = lane < HEAD_DIM
        slane = lax.broadcasted_iota(jnp.int32, (1, 128), 1)
        dkz = [jnp.zeros((2 * ATTN_BLOCK, 128), F32) for _ in range(N_KV_HEADS)]
        dvz = [jnp.zeros((2 * ATTN_BLOCK, 128), F32) for _ in range(N_KV_HEADS)]
        dsink = jnp.zeros((1, 128), F32)
        for j in range(N_Q_HEADS // 2):
            g = j // 2
            qp = q_ref[:, 128 * j:128 * (j + 1)]
            dop = do_ref[:, 128 * j:128 * (j + 1)]
            s4 = _dot_nt(kst[g], qp) * scale
            dp4 = _dot_nt(vst[g], dop)
            dqt = jnp.zeros((128, ATTN_BLOCK), F32)
            ds2, p2 = [], []
            for hh in range(2):
                h = 2 * j + hh
                pn, ps = _attn_softmax(_fold(prevm, s4, hh), kill, sink_ref[:, h:h + 1])
                dp = _fold(prevm, dp4, hh)
                delta = jnp.sum(pn * dp, axis=0, keepdims=True)
                dsink = dsink + jnp.where(slane == h, -jnp.sum(ps * delta), 0.0)
                ds2.append(_unfold(prevm, pn * (dp - delta) * scale))
                p2.append(_unfold(prevm, pn))
                dqt = dqt + _dot(kts[g][hh], ds2[hh])
            dq_ref[:, 128 * j:128 * (j + 1)] = dqt.T
            dk4 = _dot(jnp.concatenate(ds2, axis=0), qp)
            dv4 = _dot(jnp.concatenate(p2, axis=0), dop)
            for hh in range(2):
                own = low if hh == 0 else jnp.logical_not(low)
                dk_h = jnp.where(own, dk4[256 * hh:256 * (hh + 1), :], 0.0)
                dv_h = jnp.where(own, dv4[256 * hh:256 * (hh + 1), :], 0.0)
                if hh != g % 2:
                    dk_h = pltpu.roll(dk_h, HEAD_DIM, 1)
                    dv_h = pltpu.roll(dv_h, HEAD_DIM, 1)
                dkz[g] = dkz[g] + dk_h
                dvz[g] = dvz[g] + dv_h
        ds_ref[...] += dsink
        cur = pl.ds(pl.multiple_of(i * ATTN_BLOCK, ATTN_BLOCK), ATTN_BLOCK)
        prv = pl.ds(pl.multiple_of(jnp.maximum(i - 1, 0) * ATTN_BLOCK, ATTN_BLOCK), ATTN_BLOCK)
        for gp in range(N_KV_HEADS // 2):
            cols = slice(128 * gp, 128 * (gp + 1))
            dkb = dkz[2 * gp] + dkz[2 * gp + 1]
            dvb = dvz[2 * gp] + dvz[2 * gp + 1]
            dk_ref[cur, cols] = dkb[ATTN_BLOCK:, :]
            dv_ref[cur, cols] = dvb[ATTN_BLOCK:, :]

            @pl.when(i > 0)
            def _():
                dk_ref[prv, cols] += dkb[:ATTN_BLOCK, :]
                dv_ref[prv, cols] += dvb[:ATTN_BLOCK, :]

    prev = lambda i: jnp.maximum(i - 1, 0)
    whole = lambda w: pl.BlockSpec((T, w), lambda i: (0, 0))
    return _call(
        body, name=name, grid=(nb,), scratch_shapes=[], ins=[qkv, qkv, qkv, qkv, qkv, kt, kt, sinks, dya], carry=carry,
        in_specs=[pl.BlockSpec((ATTN_BLOCK, D), lambda i: (i, 0)),
                  pl.BlockSpec((ATTN_BLOCK, 256), lambda i: (prev(i), 4)),
                  pl.BlockSpec((ATTN_BLOCK, 256), lambda i: (i, 4)),
                  pl.BlockSpec((ATTN_BLOCK, 256), lambda i: (prev(i), 5)),
                  pl.BlockSpec((ATTN_BLOCK, 256), lambda i: (i, 5)),
                  pl.BlockSpec((256, ATTN_BLOCK), lambda i: (0, prev(i))),
                  pl.BlockSpec((256, ATTN_BLOCK), lambda i: (0, i)),
                  pl.BlockSpec((1, N_Q_HEADS), lambda i: (0, 0)),
                  pl.BlockSpec((ATTN_BLOCK, D), lambda i: (i, 0))],
        out_specs=[pl.BlockSpec((ATTN_BLOCK, D), lambda i: (i, 0)), whole(256), whole(256),
                   pl.BlockSpec((1, 128), lambda i: (0, 0))],
        out_shape=[jax.ShapeDtypeStruct((T, D), F32), jax.ShapeDtypeStruct((T, 256), F32),
                   jax.ShapeDtypeStruct((T, 256), F32), jax.ShapeDtypeStruct((1, 128), F32)])


HG_SUB = 256
HG_TILE = 1024
CPS = HG_SUB // HGRN_CHUNK


def _chunk_sum(tri, x):
    h1 = x.astype(BF16)
    h2 = (x - h1.astype(F32)).astype(BF16)
    r = _dot(tri, jnp.concatenate([h1, h2], axis=1))
    return r[:, :128] + r[:, 128:]


def _chunk_tri():
    ri = lax.broadcasted_iota(jnp.int32, (HG_SUB, HG_SUB), 0)
    ci = lax.broadcasted_iota(jnp.int32, (HG_SUB, HG_SUB), 1)
    return jnp.where((ri >= ci) & (ri // HGRN_CHUNK == ci // HGRN_CHUNK), 1.0, 0.0).astype(BF16)


def _chunks(t):
    return [t[HGRN_CHUNK * c:HGRN_CHUNK * (c + 1), :] for c in range(CPS)]


def _lower_bound(lbl_ref):
    l0, l1 = lbl_ref[0:1, :], lbl_ref[1:2, :]
    m = jnp.maximum(l0, l1)
    e0, e1 = jnp.exp(l0 - m), jnp.exp(l1 - m)
    return e0 / (e0 + e1)


def _hgrn_sub(fl, qh, vv, lb, tril_b):
    sg = _sigmoid(fl)
    f = lb + (1.0 - lb) * sg
    k = 1.0 - f
    gc = _chunk_sum(tril_b, jnp.log(f))
    gl_c = [t[HGRN_CHUNK - 1:HGRN_CHUNK, :] for t in _chunks(gc)]
    gl = jnp.concatenate([jnp.broadcast_to(g, (HGRN_CHUNK, 128)) for g in gl_c], axis=0)
    sq = _sigmoid(qh)
    eg = jnp.exp(gc)
    eng = jnp.exp(-gc)
    elg = jnp.exp(gl - gc)
    qd = qh * sq * eg
    ki = k * eng
    ke = k * elg
    qd_b, ki_b, ke_b, v_b = (t.astype(BF16) for t in (qd, ki, ke, vv))
    am_b = jnp.where(tril_b > 0, _dot_nt(qd_b, ki_b), 0.0).astype(BF16)
    return dict(sg=sg, f=f, sq=sq, eg=eg, eng=eng, elg=elg, qd=qd, ki=ki, ke=ke, egl=[jnp.exp(g) for g in gl_c],
                am_b=am_b, qd_b=qd_b, ki_b=ki_b, ke_b=ke_b, v_b=v_b)


def _hgrn_out(q, st_b):
    inter = [_dot_nt(qc, s) for qc, s in zip(_chunks(q["qd_b"]), st_b)]
    return _dot(q["am_b"], q["v_b"]) + jnp.concatenate(inter, axis=0)


def _hgrn_fwd(name, hg, lb_logits, norm_g, tri, th):
    T = hg.shape[0]
    nc = th // HGRN_CHUNK

    def body(fl_ref, qh_ref, v_ref, og_ref, lbl_ref, ng_ref, tri_ref, y_ref, ss_ref, st):
        @pl.when(pl.program_id(1) == 0)
        def _():
            st[...] = jnp.zeros_like(st)

        lb = _lower_bound(lbl_ref)
        ng = ng_ref[...]

        def sub(si, carry):
            rows = pl.ds(pl.multiple_of(si * HG_SUB, HG_SUB), HG_SUB)
            q = _hgrn_sub(fl_ref[rows, :], qh_ref[rows, :], v_ref[rows, :], lb, tri_ref[...])
            st_b, s = [], st[...]
            for c, (vc, kc) in enumerate(zip(_chunks(q["v_b"]), _chunks(q["ke_b"]))):
                ss_ref[0, si * CPS + c] = s
                st_b.append(s.astype(BF16))
                s = s * q["egl"][c] + _dot_tn(vc, kc)
            st[...] = s
            o = _hgrn_out(q, st_b)
            rr = lax.rsqrt(jnp.mean(o * o, axis=-1, keepdims=True) + RMS_EPS)
            og = og_ref[rows, :]
            y_ref[rows, :] = (o * rr * ng * (og * _sigmoid(og))).astype(BF16)
            return carry

        lax.fori_loop(0, th // HG_SUB, sub, 0)

    col = lambda j: pl.BlockSpec((th, 128), lambda h, t: (t, 8 * j + h))
    return pl.pallas_call(
        body, name=name, grid=(HGRN_HEADS, T // th),
        in_specs=[col(0), col(1), col(2), col(3), pl.BlockSpec((2, 128), lambda h, t: (0, h)),
                  pl.BlockSpec((1, 128), lambda h, t: (0, 0)), pl.BlockSpec((HG_SUB, HG_SUB), lambda h, t: (0, 0))],
        out_specs=[pl.BlockSpec((th, 128), lambda h, t: (t, h)),
                   pl.BlockSpec((1, nc, 128, 128), lambda h, t: (h, t, 0, 0))],
        out_shape=[jax.ShapeDtypeStruct((T, D), BF16),
                   jax.ShapeDtypeStruct((HGRN_HEADS, T // HGRN_CHUNK, 128, 128), F32)],
        scratch_shapes=[pltpu.VMEM((128, 128), F32)],
        compiler_params=_params("parallel", "arbitrary"),
    )(hg, hg, hg, hg, lb_logits, norm_g, tri)


def _hgrn_bwd(name, hg, lb_logits, norm_g, tri, tri_t, sstart, dyh, th, carry=None):
    T = hg.shape[0]
    nc = th // HGRN_CHUNK
    nt = T // th

    def body(fl_ref, qh_ref, v_ref, og_ref, lbl_ref, ng_ref, tri_ref, trit_ref, ss_ref, dy_ref,
             dfl_ref, dqh_ref, dv_ref, dog_ref, dlb_ref, dng_ref, dst):
        @pl.when(pl.program_id(1) == 0)
        def _():
            dst[...] = jnp.zeros_like(dst)
            dlb_ref[...] = jnp.zeros_like(dlb_ref)
            dng_ref[...] = jnp.zeros_like(dng_ref)

        lb = _lower_bound(lbl_ref)
        ng = ng_ref[...]
        last = lax.broadcasted_iota(jnp.int32, (HGRN_CHUNK, 128), 0) == HGRN_CHUNK - 1
        nsub = th // HG_SUB

        def sub(step, carry):
            si = nsub - 1 - step
            rows = pl.ds(pl.multiple_of(si * HG_SUB, HG_SUB), HG_SUB)
            qh, og, dy = qh_ref[rows, :], og_ref[rows, :], dy_ref[rows, :]
            tril_b = tri_ref[...]
            q = _hgrn_sub(fl_ref[rows, :], qh, v_ref[rows, :], lb, tril_b)
            s_in = [ss_ref[0, si * CPS + c] for c in range(CPS)]
            st_b = [s.astype(BF16) for s in s_in]
            o = _hgrn_out(q, st_b)
            rr = lax.rsqrt(jnp.mean(o * o, axis=-1, keepdims=True) + RMS_EPS)
            oh = o * rr
            sog = _sigmoid(og)
            dog_ref[rows, :] = dy * oh * ng * (sog * (1.0 + og * (1.0 - sog)))
            don = dy * (og * sog)
            dng_ref[0] += jnp.sum(don * oh, axis=0, keepdims=True)
            dd = don * ng
            do_b = (rr * (dd - oh * jnp.mean(dd * oh, axis=-1, keepdims=True))).astype(BF16)
            da_b = jnp.where(tril_b > 0, _dot_nt(do_b, q["v_b"]), 0.0).astype(BF16)
            do_c, qd_c, ke_c, v_c = (_chunks(t) for t in (do_b, q["qd_b"], q["ke_b"], q["v_b"]))
            dsp, dgl_dec, d = [None] * CPS, [None] * CPS, dst[...]
            for c in reversed(range(CPS)):
                dsp[c] = d
                dgl_dec[c] = jnp.sum(d * s_in[c], axis=0, keepdims=True) * q["egl"][c]
                d = d * q["egl"][c] + _dot_tn(do_c[c], qd_c[c])
            dst[...] = d
            dsp_b = [t.astype(BF16) for t in dsp]
            dv_ref[rows, :] = _dot_tn(q["am_b"], do_b) + jnp.concatenate(
                [_dot_nt(ke_c[c], dsp_b[c]) for c in range(CPS)], axis=0)
            dqd = _dot(da_b, q["ki_b"]) + jnp.concatenate([_dot(do_c[c], st_b[c]) for c in range(CPS)], axis=0)
            dki = _dot_tn(da_b, q["qd_b"])
            dke = jnp.concatenate([_dot(v_c[c], dsp_b[c]) for c in range(CPS)], axis=0)
            dkk = dke * q["ke"]
            dgl = [jnp.sum(t, axis=0, keepdims=True) + dgl_dec[c] for c, t in enumerate(_chunks(dkk))]
            dgc = dqd * q["qd"] - dki * q["ki"] - dkk + jnp.concatenate(
                [jnp.where(last, g, 0.0) for g in dgl], axis=0)
            dlf = _chunk_sum(trit_ref[...], dgc)
            df = dlf / q["f"] - (dki * q["eng"] + dke * q["elg"])
            sg = q["sg"]
            dlb_ref[...] += jnp.sum(df * (1.0 - sg), axis=0, keepdims=True)
            dfl_ref[rows, :] = df * (1.0 - lb) * sg * (1.0 - sg)
            sq = q["sq"]
            dqh_ref[rows, :] = dqd * q["eg"] * (sq * (1.0 + qh * (1.0 - sq)))
            return carry

        lax.fori_loop(0, nsub, sub, 0)

    col = lambda j: pl.BlockSpec((th, 128), lambda h, t: (nt - 1 - t, 8 * j + h))
    out = pl.BlockSpec((th, 128), lambda h, t: (nt - 1 - t, h))
    tri_spec = pl.BlockSpec((HG_SUB, HG_SUB), lambda h, t: (0, 0))
    return _call(
        body, name=name, grid=(HGRN_HEADS, nt),
        in_specs=[col(0), col(1), col(2), col(3), pl.BlockSpec((2, 128), lambda h, t: (0, h)),
                  pl.BlockSpec((1, 128), lambda h, t: (0, 0)), tri_spec, tri_spec,
                  pl.BlockSpec((1, nc, 128, 128), lambda h, t: (h, nt - 1 - t, 0, 0)), out],
        out_specs=[out, out, out, out, pl.BlockSpec((1, 128), lambda h, t: (0, h)),
                   pl.BlockSpec((1, 1, 128), lambda h, t: (h, 0, 0))],
        out_shape=[jax.ShapeDtypeStruct((T, D), F32)] * 4 + [jax.ShapeDtypeStruct((1, D), F32),
                                                              jax.ShapeDtypeStruct((HGRN_HEADS, 1, 128), F32)],
        scratch_shapes=[pltpu.VMEM((128, 128), F32)],
        ins=[hg, hg, hg, hg, lb_logits, norm_g, tri, tri_t, sstart, dyh], carry=carry)


def _mix_fwd(name, ya, yh, gates, xh1, g1, b1, wall, tm):
    T = ya.shape[0]

    def body(ya_ref, yh_ref, ga_ref, gh_ref, xh_ref, g_ref, b_ref, wall_ref,
             xo_ref, r_ref, pa_ref, ph_ref, mg_ref, wpa, wph, wo, sem):
        @pl.when(pl.program_id(0) == 0)
        def _():
            _load_rows(wall_ref, [(wpa, RIN, RP), (wph, RIN + RP, RP), (wo, RIN + 2 * RP, RP)], sem)

        pa = _dot(ya_ref[...], wpa[...]).astype(BF16)
        ph = _dot(yh_ref[...], wph[...]).astype(BF16)
        pa_ref[...] = pa
        ph_ref[...] = ph
        mg = (_sigmoid(ga_ref[...]) * pa.astype(F32) + _sigmoid(gh_ref[...]) * ph.astype(F32)).astype(BF16)
        mg_ref[...] = mg
        y1 = xh_ref[...] * g_ref[...] + b_ref[...]
        xh2, r = _ln_fwd(ALPHA * y1 + _dot(mg, wo[...]))
        xo_ref[...] = xh2
        r_ref[...] = jnp.broadcast_to(r, (tm, 128))

    row = lambda w: pl.BlockSpec((tm, w), lambda i: (i, 0))
    vec = pl.BlockSpec((1, D), lambda i: (0, 0))
    return pl.pallas_call(
        body, name=name, grid=(T // tm,),
        in_specs=[row(D), row(D), pl.BlockSpec((tm, D), lambda i: (i, 0)), pl.BlockSpec((tm, D), lambda i: (i, 1)),
                  row(D), vec, vec, HBM_SPEC],
        out_specs=[row(D), row(128), row(D), row(D), row(D)],
        out_shape=[jax.ShapeDtypeStruct((T, D), F32), jax.ShapeDtypeStruct((T, 128), F32)]
        + [jax.ShapeDtypeStruct((T, D), BF16)] * 3,
        scratch_shapes=[pltpu.VMEM((D, D), BF16)] * 3 + [pltpu.SemaphoreType.DMA((3 * NDEV,))],
        compiler_params=_params("arbitrary"),
    )(ya, yh, gates, gates, xh1, g1, b1, wall)


def _mix_bwd(name, dy2, xh2, r2, g2, gates, pa, ph, wall, tm):
    T = dy2.shape[0]

    def body(dy_ref, xh_ref, r_ref, g_ref, ga_ref, gh_ref, pa_ref, ph_ref, wall_ref,
             dz_ref, dmix_ref, dpa_ref, dph_ref, dga_ref, dgh_ref, dya_ref, dyh_ref, dg_ref, db_ref,
             wpa, wph, wo, sem):
        @pl.when(pl.program_id(0) == 0)
        def _():
            _load_rows(wall_ref, [(wpa, RIN, RP), (wph, RIN + RP, RP), (wo, RIN + 2 * RP, RP)], sem)
            dg_ref[...] = jnp.zeros_like(dg_ref)
            db_ref[...] = jnp.zeros_like(db_ref)

        dz, dgp, dbp = _ln_bwd(dy_ref[...], xh_ref[...], r_ref[:, :1], g_ref[...])
        dg_ref[...] += dgp
        db_ref[...] += dbp
        dz_ref[...] = dz
        dmix = dz.astype(BF16)
        dmix_ref[...] = dmix
        dmg = _dot_nt(dmix, wo[...])
        sa, sh = _sigmoid(ga_ref[...]), _sigmoid(gh_ref[...])
        dga_ref[...] = dmg * pa_ref[...].astype(F32) * sa * (1.0 - sa)
        dgh_ref[...] = dmg * ph_ref[...].astype(F32) * sh * (1.0 - sh)
        dpa = (dmg * sa).astype(BF16)
        dph = (dmg * sh).astype(BF16)
        dpa_ref[...] = dpa
        dph_ref[...] = dph
        dya_ref[...] = _dot_nt(dpa, wpa[...]).astype(BF16)
        dyh_ref[...] = _dot_nt(dph, wph[...])

    row = lambda w: pl.BlockSpec((tm, w), lambda i: (i, 0))
    vec = pl.BlockSpec((1, D), lambda i: (0, 0))
    return pl.pallas_call(
        body, name=name, grid=(T // tm,),
        in_specs=[row(D), row(D), row(128), vec, pl.BlockSpec((tm, D), lambda i: (i, 0)),
                  pl.BlockSpec((tm, D), lambda i: (i, 1)), row(D), row(D), HBM_SPEC],
        out_specs=[row(D)] * 8 + [vec, vec],
        out_shape=[jax.ShapeDtypeStruct((T, D), F32)] + [jax.ShapeDtypeStruct((T, D), BF16)] * 3
        + [jax.ShapeDtypeStruct((T, D), F32)] * 2 + [jax.ShapeDtypeStruct((T, D), BF16), jax.ShapeDtypeStruct((T, D), F32)]
        + [jax.ShapeDtypeStruct((1, D), F32)] * 2,
        scratch_shapes=[pltpu.VMEM((D, D), BF16)] * 3 + [pltpu.SemaphoreType.DMA((3 * NDEV,))],
        compiler_params=_params("arbitrary"),
    )(dy2, xh2, r2, g2, gates, gates, pa, ph, wall)


def _adam(w, g, m, v):
    m = ADAM_B1 * m + (1.0 - ADAM_B1) * g
    v = ADAM_B2 * v + (1.0 - ADAM_B2) * (g * g)
    m_hat = m / (1.0 - ADAM_B1 ** ADAM_STEP)
    v_hat = v / (1.0 - ADAM_B2 ** ADAM_STEP)
    delta = -ADAM_LR * (m_hat / (jnp.sqrt(v_hat) + ADAM_EPS) + ADAM_WD * w)
    return delta, m, v


def _grad_step(name, recv, w, m, v, transposed):
    _, rows, _ = recv.shape
    cb = 256
    pad = (-rows) % 128

    def body(r_ref, w_ref, m_ref, v_ref, g_ref, d_ref, mo_ref, vo_ref):
        acc = r_ref[0].astype(F32)
        for k in range(1, NDEV):
            acc = acc + r_ref[k].astype(F32)
        if transposed:
            if pad:
                acc = jnp.concatenate([acc, jnp.zeros((pad, cb), F32)], axis=0)
            acc = acc.T[:, :rows]
        g_ref[...] = acc
        d, mm, vv = _adam(w_ref[...], acc, m_ref[...], v_ref[...])
        d_ref[...] = d
        mo_ref[...] = mm
        vo_ref[...] = vv

    blk = pl.BlockSpec((cb, rows), lambda j: (j, 0)) if transposed else pl.BlockSpec((rows, cb), lambda j: (0, j))
    return pl.pallas_call(
        body, name=name, grid=(D // cb,),
        in_specs=[pl.BlockSpec((NDEV, rows, cb), lambda j: (0, 0, j)), blk, blk, blk],
        out_specs=[blk] * 4, out_shape=[jax.ShapeDtypeStruct(w.shape, F32)] * 4,
        compiler_params=_params("parallel"),
    )(recv, w, m, v)


_SMALL = [("ln1_g", D), ("ln1_b", D), ("ln2_g", D), ("ln2_b", D), ("ln3_g", D), ("ln3_b", D),
          ("b_in", DIN), ("lb", D), ("attn_sinks", 128), ("hgrn_norm_g", 128), ("loss", 128)]
_SMALL_OFF = {}
_o = 0
for _n, _w in _SMALL:
    _SMALL_OFF[_n] = (_o, _w)
    _o += _w
PACK = _o


def _small_reduce(name, packed):
    def body(p_ref, o_ref, buf, send_sems, recv_sems):
        x, y, c = lax.axis_index("x"), lax.axis_index("y"), lax.axis_index("c")
        me = 4 * x + 2 * y + c
        buf[me] = p_ref[...]
        copies = []
        for j in range(1, NDEV):
            px, py, pc = x ^ (j >> 2), y ^ ((j >> 1) & 1), c ^ (j & 1)
            cp = pltpu.make_async_remote_copy(
                src_ref=p_ref, dst_ref=buf.at[me], send_sem=send_sems.at[j - 1], recv_sem=recv_sems.at[j - 1],
                device_id=(px, py, pc), device_id_type=MESH)
            cp.start()
            copies.append(cp)
        for cp in copies:
            cp.wait()
        tot = buf[0]
        for k in range(1, NDEV):
            tot = tot + buf[k]
        o_ref[...] = tot

    return pl.pallas_call(
        body, name=name, out_shape=jax.ShapeDtypeStruct((1, PACK), F32),
        scratch_shapes=[pltpu.VMEM((NDEV, 1, PACK), F32), pltpu.SemaphoreType.DMA((NDEV - 1,)),
                        pltpu.SemaphoreType.DMA((NDEV - 1,))],
        compiler_params=pltpu.CompilerParams(vmem_limit_bytes=VMEM_LIMIT),
    )(packed)


def _small_step(name, total, small_w, small_m, small_v):
    names = ["ln1_g", "ln1_b", "ln2_g", "ln2_b", "ln3_g", "ln3_b", "b_in", "attn_sinks", "hgrn_lb_logits", "hgrn_norm_g"]
    np_ = len(names)

    def body(*refs):
        t_ref = refs[0]
        w_refs = refs[1:1 + np_]
        m_refs = refs[1 + np_:1 + 2 * np_]
        v_refs = refs[1 + 2 * np_:1 + 3 * np_]
        loss_ref, outs = refs[1 + 3 * np_], refs[2 + 3 * np_:]

        def part(n):
            o, w = _SMALL_OFF[n]
            return t_ref[:, o:o + w]

        loss_ref[...] = part("loss")
        for i, n in enumerate(names):
            w = w_refs[i][...]
            if n == "hgrn_lb_logits":
                m_ = jnp.maximum(w[0:1, :], w[1:2, :])
                e0, e1 = jnp.exp(w[0:1, :] - m_), jnp.exp(w[1:2, :] - m_)
                p0 = e0 / (e0 + e1)
                t = p0 * (1.0 - p0) * part("lb")
                g = jnp.concatenate([t, -t], axis=0)
            elif n == "attn_sinks":
                g = part(n)[:, :N_Q_HEADS]
            else:
                g = part(n)
            d, mm, vv = _adam(w, g, m_refs[i][...], v_refs[i][...])
            outs[4 * i][...] = g
            outs[4 * i + 1][...] = d
            outs[4 * i + 2][...] = mm
            outs[4 * i + 3][...] = vv

    out_shape = [jax.ShapeDtypeStruct((1, 128), F32)]
    for n in names:
        out_shape += [jax.ShapeDtypeStruct(small_w[n].shape, F32)] * 4
    return pl.pallas_call(
        body, name=name, out_shape=out_shape,
        compiler_params=pltpu.CompilerParams(vmem_limit_bytes=VMEM_LIMIT),
    )(total, *[small_w[n] for n in names], *[small_m[n] for n in names], *[small_v[n] for n in names]), names


def _tile(T, pref):
    return min(T, pref)


def kernel(x, ln1_g, ln1_b, ffn1_w1, ffn1_w3, ffn1_w2, ln2_g, ln2_b, w_in, b_in, attn_sinks, hgrn_lb_logits, hgrn_norm_g, w_proj_attn, w_proj_hgrn, w_out, ln3_g, ln3_b, ffn2_w1, ffn2_w3, ffn2_w2, loss_target, m_ln1_g, m_ln1_b, m_ffn1_w1, m_ffn1_w3, m_ffn1_w2, m_ln2_g, m_ln2_b, m_w_in, m_b_in, m_attn_sinks, m_hgrn_lb_logits, m_hgrn_norm_g, m_w_proj_attn, m_w_proj_hgrn, m_w_out, m_ln3_g, m_ln3_b, m_ffn2_w1, m_ffn2_w3, m_ffn2_w2, v_ln1_g, v_ln1_b, v_ffn1_w1, v_ffn1_w3, v_ffn1_w2, v_ln2_g, v_ln2_b, v_w_in, v_b_in, v_attn_sinks, v_hgrn_lb_logits, v_hgrn_norm_g, v_w_proj_attn, v_w_proj_hgrn, v_w_out, v_ln3_g, v_ln3_b, v_ffn2_w1, v_ffn2_w3, v_ffn2_w2):
    T = x.shape[1]
    xs = x[0]
    tgt = loss_target[0]
    tm = _tile(T, 256)
    tk = _tile(T, 512)
    th = _tile(T, HG_TILE)

    t_bf = lambda w: w[0].T.astype(BF16)
    n_bf = lambda w: w[0].astype(BF16)
    ffn_shard = lambda w1, w3, w2: jnp.concatenate([t_bf(w1), t_bf(w3), n_bf(w2)], axis=0)
    mix_shard = jnp.concatenate([t_bf(w_in), n_bf(w_proj_attn), n_bf(w_proj_hgrn), n_bf(w_out)], axis=0)
    (ffn1_all,) = _exchange_call("gather_ffn1", _gather_exchange(ffn_shard(ffn1_w1, ffn1_w3, ffn1_w2)))
    ffn_offs = (0, RF, 2 * RF)
    ropes = _rope_tables(T)
    tri = _chunk_tri()
    tri_t = tri.T

    (xh1, r1, a1, b1, xb0), (mix_all,) = _ffn_fwd("ffn1_fwd", xs, None, ffn1_all, ffn_offs, tm,
                                                  carry=_gather_exchange(mix_shard))
    (qkv, kt, vt, hg, gates, y1b), (ffn2_all,) = _inproj_fwd(
        "inproj_fwd", xh1, ln1_g, ln1_b, mix_all, b_in, ropes, tm,
        carry=_gather_exchange(ffn_shard(ffn2_w1, ffn2_w3, ffn2_w2)))
    ya = _attn_fwd("attn_fwd", qkv, vt, attn_sinks)
    yh, sstart = _hgrn_fwd("hgrn_fwd", hg, hgrn_lb_logits, hgrn_norm_g, tri, th)
    xh2, r2, pa, ph, merged = _mix_fwd("mix_fwd", ya, yh, gates, xh1, ln1_g, ln1_b, mix_all, tm)
    (xh3, r3, a2, b2, y2b, loss_part), _ = _ffn_fwd("ffn2_fwd", xh2, (ln2_g, ln2_b), ffn2_all, ffn_offs, tm,
                                                    loss=(ln3_g, ln3_b, tgt))

    (dy2, dab2, u2, df2, dg3, db3), _ = _ffn_bwd("ffn2_bwd", ("loss", ln3_b, tgt), xh3, r3, ln3_g, a2, b2, ffn2_all,
                                                 ffn_offs, tm)
    g_ffn2_13, _ = _wgrad("wgrad_ffn2_w13", dab2, y2b, DFF // 2, tk)
    g_ffn2_2, _ = _wgrad("wgrad_ffn2_w2", u2, df2, DFF // 2, tk)
    dz2, dmix, dpa, dph, dga, dgh, dya, dyh, dg2, db2 = _mix_bwd("mix_bwd", dy2, xh2, r2, ln2_g, gates, pa, ph,
                                                                 mix_all, tm)
    g_wo, _ = _wgrad("wgrad_w_out", merged, dmix, D, tk)
    g_pa, _ = _wgrad("wgrad_w_pa", ya, dpa, D, tk)
    g_ph, _ = _wgrad("wgrad_w_ph", yh, dph, D, tk)
    recv = {}
    (dq, dk, dv, dsink), (recv["ffn2_w1"], recv["ffn2_w3"], recv["ffn2_w2"]) = _attn_bwd(
        "attn_bwd", qkv, kt, attn_sinks, dya,
        carry=_grad_exchange([g_ffn2_13, g_ffn2_13, g_ffn2_2], [0, DFF, 0], [RF, RF, RF]))
    (dfl, dqh, dih, dog, dlb, dng), (recv["w_proj_attn"], recv["w_proj_hgrn"], recv["w_out"]) = _hgrn_bwd(
        "hgrn_bwd", hg, hgrn_lb_logits, hgrn_norm_g, tri, tri_t, sstart, dyh, th,
        carry=_grad_exchange([g_pa, g_ph, g_wo], [0, 0, 0], [RP, RP, RP]))
    dy1, dproj, dbin = _inproj_bwd("inproj_bwd", dq, dk, dv, (dfl, dqh, dih, dog), (dga, dgh), dz2, mix_all, ropes, tm)
    g_win, _ = _wgrad("wgrad_w_in", dproj, y1b, DIN // 4, tk)
    (gx, dab1, u1, df1, dg1, db1), (recv["w_in"],) = _ffn_bwd(
        "ffn1_bwd", ("dy", dy1), xh1, r1, ln1_g, a1, b1, ffn1_all, ffn_offs, tm,
        carry=_grad_exchange([g_win], [0], [RIN]))
    g_ffn1_2, _ = _wgrad("wgrad_ffn1_w2", u1, df1, DFF // 2, tk)
    g_ffn1_1, (recv["ffn1_w2"],) = _wgrad("wgrad_ffn1_w1", dab1, xb0, DFF // 2, tk, 0, DFF,
                                          carry=_grad_exchange([g_ffn1_2], [0], [RF]))
    g_ffn1_3, (recv["ffn1_w1"],) = _wgrad("wgrad_ffn1_w3", dab1, xb0, DFF // 2, tk, DFF, DFF,
                                          carry=_grad_exchange([g_ffn1_1], [0], [RF]))
    (recv["ffn1_w3"],) = _exchange_call("exchange_ffn1_w3", _grad_exchange([g_ffn1_3], [0], [RF]))

    big = [("ffn1_w1", ffn1_w1, m_ffn1_w1, v_ffn1_w1, True), ("ffn1_w3", ffn1_w3, m_ffn1_w3, v_ffn1_w3, True),
           ("ffn1_w2", ffn1_w2, m_ffn1_w2, v_ffn1_w2, False), ("w_in", w_in, m_w_in, v_w_in, True),
           ("w_proj_attn", w_proj_attn, m_w_proj_attn, v_w_proj_attn, False),
           ("w_proj_hgrn", w_proj_hgrn, m_w_proj_hgrn, v_w_proj_hgrn, False),
           ("w_out", w_out, m_w_out, v_w_out, False),
           ("ffn2_w1", ffn2_w1, m_ffn2_w1, v_ffn2_w1, True), ("ffn2_w3", ffn2_w3, m_ffn2_w3, v_ffn2_w3, True),
           ("ffn2_w2", ffn2_w2, m_ffn2_w2, v_ffn2_w2, False)]
    res = {}
    for n, w, m, v, transposed in big:
        res[n] = tuple(t[None] for t in _grad_step("step_" + n, recv[n], w[0], m[0], v[0], transposed))

    parts = {"ln1_g": dg1, "ln1_b": db1, "ln2_g": dg2, "ln2_b": db2, "ln3_g": dg3, "ln3_b": db3, "b_in": dbin,
             "lb": dlb, "attn_sinks": dsink, "hgrn_norm_g": jnp.sum(dng, axis=0), "loss": loss_part[0:1, :]}
    packed = jnp.concatenate([parts[n] for n, _ in _SMALL], axis=1)
    small_w = dict(ln1_g=ln1_g, ln1_b=ln1_b, ln2_g=ln2_g, ln2_b=ln2_b, ln3_g=ln3_g, ln3_b=ln3_b, b_in=b_in,
                   attn_sinks=attn_sinks, hgrn_lb_logits=hgrn_lb_logits, hgrn_norm_g=hgrn_norm_g)
    small_m = dict(ln1_g=m_ln1_g, ln1_b=m_ln1_b, ln2_g=m_ln2_g, ln2_b=m_ln2_b, ln3_g=m_ln3_g, ln3_b=m_ln3_b,
                   b_in=m_b_in, attn_sinks=m_attn_sinks, hgrn_lb_logits=m_hgrn_lb_logits, hgrn_norm_g=m_hgrn_norm_g)
    small_v = dict(ln1_g=v_ln1_g, ln1_b=v_ln1_b, ln2_g=v_ln2_g, ln2_b=v_ln2_b, ln3_g=v_ln3_g, ln3_b=v_ln3_b,
                   b_in=v_b_in, attn_sinks=v_attn_sinks, hgrn_lb_logits=v_hgrn_lb_logits, hgrn_norm_g=v_hgrn_norm_g)
    outs, names = _small_step("small_step", _small_reduce("small_reduce", packed), small_w, small_m, small_v)
    loss = outs[0][0, 0]
    for i, n in enumerate(names):
        res[n] = tuple(outs[1 + 4 * i:5 + 4 * i])

    order = ["ln1_g", "ln1_b", "ffn1_w1", "ffn1_w3", "ffn1_w2", "ln2_g", "ln2_b", "w_in", "b_in", "attn_sinks",
             "hgrn_lb_logits", "hgrn_norm_g", "w_proj_attn", "w_proj_hgrn", "w_out", "ln3_g", "ln3_b",
             "ffn2_w1", "ffn2_w3", "ffn2_w2"]
    return (loss, gx[None], *[res[n][0] for n in order], *[res[n][1] for n in order],
            *[res[n][2] for n in order], *[res[n][3] for n in order])
```

```python
import jax
import jax.numpy as jnp
from jax import lax
from jax.experimental import pallas as pl
from jax.experimental.pallas import tpu as pltpu

F32 = jnp.float32
BF16 = jnp.bfloat16

NDEV = 8
D = 1024
DFF = 2816
RF = DFF // NDEV
DIN = 7680
RIN = DIN // NDEV
RP = D // NDEV
N_Q_HEADS = 16
N_KV_HEADS = 4
HEAD_DIM = 64
ATTN_BLOCK = 128
ROPE_THETA = 500000.0
ROPE_DIM = HEAD_DIM // 4
HGRN_HEADS = 8
HGRN_DK = 128
HGRN_CHUNK = 64
ALPHA = 2.0 ** 0.25
LN_EPS = 1e-5
RMS_EPS = 1e-6
NEG_INF = -1e30
ADAM_LR = 0.001
ADAM_B1 = 0.9
ADAM_B2 = 0.999
ADAM_EPS = 1e-08
ADAM_WD = 0.01
ADAM_STEP = 10

QKV_W = 1536
HG_W = 4096
GATE_W = 2048
VMEM_LIMIT = 60 * 2 ** 20
FFN_CHUNKS = 2

MESH = pl.DeviceIdType.MESH
HBM_SPEC = pl.BlockSpec(memory_space=pltpu.HBM)


def _params(*sem):
    return pltpu.CompilerParams(dimension_semantics=sem, vmem_limit_bytes=VMEM_LIMIT)


def _dot(a, b):
    return jnp.dot(a, b, preferred_element_type=F32)


def _dot_nt(a, b):
    return lax.dot_general(a, b, (((1,), (1,)), ((), ())), preferred_element_type=F32)


def _dot_tn(a, b):
    return lax.dot_general(a, b, (((0,), (0,)), ((), ())), preferred_element_type=F32)


def _sigmoid(x):
    return 0.5 * jnp.tanh(0.5 * x) + 0.5


def _ln_fwd(z):
    mu = jnp.mean(z, axis=-1, keepdims=True)
    zc = z - mu
    var = jnp.mean(zc * zc, axis=-1, keepdims=True)
    r = lax.rsqrt(var + LN_EPS)
    return zc * r, r


def _ln_bwd(dy, xh, r, g):
    dxh = dy * g
    m1 = jnp.mean(dxh, axis=-1, keepdims=True)
    m2 = jnp.mean(dxh * xh, axis=-1, keepdims=True)
    dz = r * (dxh - m1 - xh * m2)
    return dz, jnp.sum(dy * xh, axis=0, keepdims=True), jnp.sum(dy, axis=0, keepdims=True)


def _load_rows(wall_ref, pieces, sem):
    copies = []
    for dst, off, r in pieces:
        for k in range(NDEV):
            c = pltpu.make_async_copy(wall_ref.at[k, pl.ds(off, r), :], dst.at[pl.ds(k * r, r), :], sem.at[len(copies)])
            c.start()
            copies.append(c)
    for c in copies:
        c.wait()


class _Exchange:
    def __init__(self, inputs, out_shape, scratch, begin, middle, end):
        self.inputs, self.out_shape, self.scratch = inputs, out_shape, scratch
        self.begin, self.middle, self.end = begin, middle, end


def _gather_exchange(shard):
    rows, cols = shard.shape

    def ops(ins, outs, scr):
        (x_ref,), (out_ref,), (send_sems, recv_sems, local_sem) = ins, outs, scr
        x, y, c = lax.axis_index("x"), lax.axis_index("y"), lax.axis_index("c")
        me, sibling = (x, y, c), (x, y, 1 - c)
        chips = [(1 - x, y), (x, 1 - y), (1 - x, 1 - y)]

        def slot(px, py, pc):
            return out_ref.at[4 * px + 2 * py + pc]

        def copy(k, block, to, src=None):
            return pltpu.make_async_remote_copy(
                src_ref=slot(*block) if src is None else src, dst_ref=slot(*block),
                send_sem=send_sems.at[k], recv_sem=recv_sems.at[k], device_id=to, device_id_type=MESH)

        mine = lambda: pltpu.make_async_copy(x_ref, slot(*me), local_sem)
        first = lambda: [copy(0, me, sibling, src=x_ref)] + [
            copy(1 + j, me, (*chip, c), src=x_ref) for j, chip in enumerate(chips)]
        passed = lambda: [copy(4 + j, (*chip, c), sibling) for j, chip in enumerate(chips)]
        return c, me, sibling, chips, copy, mine, first, passed

    def begin(*refs):
        _, _, _, _, _, mine, first, _ = ops(*refs)
        mine().start()
        for cp in first():
            cp.start()

    def middle(*refs):
        c, me, _, chips, copy, _, _, passed = ops(*refs)
        for (j, chip), fwd in zip(enumerate(chips), passed()):
            copy(1 + j, (*chip, c), me).wait_recv()
            fwd.start()

    def end(*refs):
        c, me, sibling, chips, copy, mine, first, passed = ops(*refs)
        copy(0, sibling, me).wait_recv()
        for j, chip in enumerate(chips):
            copy(4 + j, (*chip, 1 - c), me).wait_recv()
        for cp in first() + passed():
            cp.wait_send()
        mine().wait()

    return _Exchange([shard], [jax.ShapeDtypeStruct((NDEV, rows, cols), shard.dtype)],
                     [pltpu.SemaphoreType.DMA((7,)), pltpu.SemaphoreType.DMA((7,)), pltpu.SemaphoreType.DMA],
                     begin, middle, end)


def _grad_exchange(grads, bases, rows):
    n = len(grads)

    def copies(g_refs, out_refs, scr):
        send_sems, recv_sems, local_sems = scr
        x, y, c = lax.axis_index("x"), lax.axis_index("y"), lax.axis_index("c")
        me = 4 * x + 2 * y + c
        out = []
        for i in range(n):
            r = rows[i]
            src = lambda k: g_refs[i].at[pl.ds(pl.multiple_of(bases[i] + k * r, 16), r), :]
            out.append(pltpu.make_async_copy(src(me), out_refs[i].at[me], local_sems.at[i]))
            for j in range(1, NDEV):
                px, py, pc = x ^ (j >> 2), y ^ ((j >> 1) & 1), c ^ (j & 1)
                out.append(pltpu.make_async_remote_copy(
                    src_ref=src(4 * px + 2 * py + pc), dst_ref=out_refs[i].at[me],
                    send_sem=send_sems.at[i, j - 1], recv_sem=recv_sems.at[i, j - 1],
                    device_id=(px, py, pc), device_id_type=MESH))
        return out

    def begin(*refs):
        for cp in copies(*refs):
            cp.start()

    def end(*refs):
        for cp in copies(*refs):
            cp.wait()

    return _Exchange(list(grads), [jax.ShapeDtypeStruct((NDEV, r, D), g.dtype) for g, r in zip(grads, rows)],
                     [pltpu.SemaphoreType.DMA((n, NDEV - 1)), pltpu.SemaphoreType.DMA((n, NDEV - 1)),
                      pltpu.SemaphoreType.DMA((n,))], begin, None, end)


def _exchange_call(name, ex):
    ni, no = len(ex.inputs), len(ex.out_shape)

    def body(*refs):
        parts = (refs[:ni], refs[ni:ni + no], refs[ni + no:])
        ex.begin(*parts)
        if ex.middle is not None:
            ex.middle(*parts)
        ex.end(*parts)

    return pl.pallas_call(body, name=name, out_shape=ex.out_shape, in_specs=[HBM_SPEC] * ni, out_specs=[HBM_SPEC] * no,
                          scratch_shapes=ex.scratch)(*ex.inputs)


def _call(body, *, name, grid, in_specs, out_specs, out_shape, scratch_shapes, ins, carry=None):
    sem = ("arbitrary",) * len(grid)
    if carry is None:
        outs = pl.pallas_call(body, name=name, grid=grid, in_specs=in_specs, out_specs=out_specs, out_shape=out_shape,
                              scratch_shapes=scratch_shapes, compiler_params=_params(*sem))(*ins)
        return outs, None
    n_in, n_out, n_scr = len(ins), len(out_shape), len(scratch_shapes)
    ci, co = len(carry.inputs), len(carry.out_shape)
    total = 1
    for g in grid:
        total *= g

    def wrapped(*refs):
        own_in, ex_in = refs[:n_in], refs[n_in:n_in + ci]
        o0 = n_in + ci
        own_out, ex_out = refs[o0:o0 + n_out], refs[o0 + n_out:o0 + n_out + co]
        s0 = o0 + n_out + co
        own_scr, ex_scr = refs[s0:s0 + n_scr], refs[s0 + n_scr:]
        step = pl.program_id(0)
        for d in range(1, len(grid)):
            step = step * grid[d] + pl.program_id(d)
        parts = (ex_in, ex_out, ex_scr)
        pl.when(step == 0)(lambda: carry.begin(*parts))
        body(*own_in, *own_out, *own_scr)
        if carry.middle is not None:
            pl.when(step == total // 2)(lambda: carry.middle(*parts))
        pl.when(step == total - 1)(lambda: carry.end(*parts))

    outs = pl.pallas_call(
        wrapped, name=name, grid=grid, in_specs=list(in_specs) + [HBM_SPEC] * ci,
        out_specs=list(out_specs) + [HBM_SPEC] * co, out_shape=list(out_shape) + list(carry.out_shape),
        scratch_shapes=list(scratch_shapes) + list(carry.scratch), compiler_params=_params(*sem),
    )(*ins, *carry.inputs)
    return outs[:n_out], outs[n_out:]


def _ffn_fwd(name, xin, affine, wall, offs, tm, loss=None, carry=None):
    T = xin.shape[0]
    nt = T // tm
    fc = DFF // FFN_CHUNKS

    def body(*refs):
        it = iter(refs)
        x_ref = next(it)
        if affine is not None:
            g_ref, b_ref = next(it), next(it)
        wall_ref = next(it)
        if loss is not None:
            go_ref, bo_ref, tgt_ref = next(it), next(it), next(it)
        xh_ref, r_ref, a_ref, b2_ref, yb_ref = (next(it) for _ in range(5))
        if loss is not None:
            loss_ref = next(it)
        w1, w3, w2, sem = (next(it) for _ in range(4))

        @pl.when(pl.program_id(0) == 0)
        def _():
            _load_rows(wall_ref, [(w1, offs[0], RF), (w3, offs[1], RF), (w2, offs[2], RF)], sem)
            if loss is not None:
                loss_ref[...] = jnp.zeros_like(loss_ref)

        y = x_ref[...]
        if affine is not None:
            y = y * g_ref[...] + b_ref[...]
        yb = y.astype(BF16)
        yb_ref[...] = yb
        f = jnp.zeros((tm, D), F32)
        for ci in range(DFF // fc):
            cs = slice(ci * fc, (ci + 1) * fc)
            a = _dot_nt(yb, w1[cs, :]).astype(BF16)
            b = _dot_nt(yb, w3[cs, :]).astype(BF16)
            a_ref[:, cs] = a
            b2_ref[:, cs] = b
            af, bf = a.astype(F32), b.astype(F32)
            u = (af * _sigmoid(af) * bf).astype(BF16)
            f = f + _dot(u, w2[cs, :])
        xh, r = _ln_fwd(ALPHA * y + 0.5 * f)
        xh_ref[...] = xh
        r_ref[...] = jnp.broadcast_to(r, (tm, 128))
        if loss is not None:
            e = xh * go_ref[...] + bo_ref[...] - tgt_ref[...]
            loss_ref[...] += jnp.sum(e * e) * (0.5 / D)

    row = lambda w: pl.BlockSpec((tm, w), lambda i: (i, 0))
    vec = pl.BlockSpec((1, D), lambda i: (0, 0))
    ins, in_specs = [xin], [row(D)]
    if affine is not None:
        ins += list(affine)
        in_specs += [vec, vec]
    ins.append(wall)
    in_specs.append(HBM_SPEC)
    if loss is not None:
        ins += list(loss)
        in_specs += [vec, vec, row(D)]
    out_shape = [jax.ShapeDtypeStruct((T, D), F32), jax.ShapeDtypeStruct((T, 128), F32),
                 jax.ShapeDtypeStruct((T, DFF), BF16), jax.ShapeDtypeStruct((T, DFF), BF16),
                 jax.ShapeDtypeStruct((T, D), BF16)]
    out_specs = [row(D), row(128), row(DFF), row(DFF), row(D)]
    if loss is not None:
        out_shape.append(jax.ShapeDtypeStruct((8, 128), F32))
        out_specs.append(pl.BlockSpec((8, 128), lambda i: (0, 0)))
    return _call(body, name=name, grid=(nt,), in_specs=in_specs, out_specs=out_specs, out_shape=out_shape,
                 scratch_shapes=[pltpu.VMEM((DFF, D), BF16)] * 3 + [pltpu.SemaphoreType.DMA((3 * NDEV,))],
                 ins=ins, carry=carry)


def _ffn_bwd(name, dy_src, xh, r, g, a, b, wall, offs, tm, carry=None):
    T = xh.shape[0]
    nt = T // tm
    fc = DFF // FFN_CHUNKS
    from_loss = dy_src[0] == "loss"

    def body(*refs):
        it = iter(refs)
        if from_loss:
            bo_ref, tgt_ref = next(it), next(it)
        else:
            dy_ref = next(it)
        xh_ref, r_ref, g_ref, a_ref, b_ref, wall_ref = (next(it) for _ in range(6))
        dyin_ref, dab_ref, u_ref, df_ref, dg_ref, db_ref = (next(it) for _ in range(6))
        w1, w3, w2, sem = (next(it) for _ in range(4))

        @pl.when(pl.program_id(0) == 0)
        def _():
            _load_rows(wall_ref, [(w1, offs[0], RF), (w3, offs[1], RF), (w2, offs[2], RF)], sem)
            dg_ref[...] = jnp.zeros_like(dg_ref)
            db_ref[...] = jnp.zeros_like(db_ref)

        xhv = xh_ref[...]
        gv = g_ref[...]
        if from_loss:
            dy = (xhv * gv + bo_ref[...] - tgt_ref[...]) * (1.0 / D)
        else:
            dy = dy_ref[...]
        dz, dgp, dbp = _ln_bwd(dy, xhv, r_ref[:, :1], gv)
        dg_ref[...] += dgp
        db_ref[...] += dbp
        df = (0.5 * dz).astype(BF16)
        df_ref[...] = df
        dx = ALPHA * dz
        for ci in range(DFF // fc):
            cs = slice(ci * fc, (ci + 1) * fc)
            du = _dot_nt(df, w2[cs, :])
            af, bf = a_ref[:, cs].astype(F32), b_ref[:, cs].astype(F32)
            s = _sigmoid(af)
            sl = af * s
            u_ref[:, cs] = (sl * bf).astype(BF16)
            da = (du * bf * (s * (1.0 + af * (1.0 - s)))).astype(BF16)
            dbb = (du * sl).astype(BF16)
            dab_ref[:, cs] = da
            dab_ref[:, DFF + ci * fc:DFF + (ci + 1) * fc] = dbb
            dx = dx + _dot(da, w1[cs, :]) + _dot(dbb, w3[cs, :])
        dyin_ref[...] = dx

    row = lambda w: pl.BlockSpec((tm, w), lambda i: (i, 0))
    vec = pl.BlockSpec((1, D), lambda i: (0, 0))
    if from_loss:
        ins, in_specs = [dy_src[1], dy_src[2]], [vec, row(D)]
    else:
        ins, in_specs = [dy_src[1]], [row(D)]
    ins += [xh, r, g, a, b, wall]
    in_specs += [row(D), row(128), vec, row(DFF), row(DFF), HBM_SPEC]
    return _call(
        body, name=name, grid=(nt,), in_specs=in_specs,
        out_specs=[row(D), row(2 * DFF), row(DFF), row(D), vec, vec],
        out_shape=[jax.ShapeDtypeStruct((T, D), F32), jax.ShapeDtypeStruct((T, 2 * DFF), BF16),
                   jax.ShapeDtypeStruct((T, DFF), BF16), jax.ShapeDtypeStruct((T, D), BF16),
                   jax.ShapeDtypeStruct((1, D), F32), jax.ShapeDtypeStruct((1, D), F32)],
        scratch_shapes=[pltpu.VMEM((DFF, D), BF16)] * 3 + [pltpu.SemaphoreType.DMA((3 * NDEV,))],
        ins=ins, carry=carry)


def _wgrad(name, a, b, bn, tk, col0=0, ncols=None, carry=None):
    T = a.shape[0]
    N = a.shape[1] if ncols is None else ncols
    nk = T // tk
    c0 = col0 // bn

    def body(a_ref, b_ref, o_ref, acc):
        k = pl.program_id(1)

        @pl.when(k == 0)
        def _():
            acc[...] = jnp.zeros_like(acc)

        acc[...] += _dot_tn(a_ref[...], b_ref[...])

        @pl.when(k == nk - 1)
        def _():
            o_ref[...] = acc[...].astype(BF16)

    (out,), ex = _call(
        body, name=name, grid=(N // bn, nk),
        in_specs=[pl.BlockSpec((tk, bn), lambda n, k: (k, n + c0)), pl.BlockSpec((tk, D), lambda n, k: (k, 0))],
        out_specs=[pl.BlockSpec((bn, D), lambda n, k: (n, 0))],
        out_shape=[jax.ShapeDtypeStruct((N, D), BF16)],
        scratch_shapes=[pltpu.VMEM((bn, D), F32)], ins=[a, b], carry=carry)
    return out, ex


def _rope_tables(T):
    pos = jnp.arange(T, dtype=F32)
    inv_freq = ROPE_THETA ** (-jnp.arange(0, ROPE_DIM, 2, dtype=F32) / ROPE_DIM)
    half = ROPE_DIM // 2
    ch = jnp.arange(128) % HEAD_DIM
    ang = pos[:, None] * inv_freq[ch % half][None, :]
    cos, sin = jnp.cos(ang), jnp.sin(ang)
    first, second = (ch < half)[None, :], ((ch >= half) & (ch < ROPE_DIM))[None, :]
    return (jnp.where(first | second, cos, 1.0), jnp.where(first, -sin, 0.0), jnp.where(second, sin, 0.0))


def _rope(t, c, s1, s2):
    n = t.shape[1] // 128
    ct, s1t, s2t = (jnp.tile(v, (1, n)) for v in (c, s1, s2))
    w = t.shape[1]
    return t * ct + pltpu.roll(t, w - 8, 1) * s1t + pltpu.roll(t, 8, 1) * s2t


def _rope_t(dr, c, s1, s2):
    n = dr.shape[1] // 128
    ct, s1t, s2t = (jnp.tile(v, (1, n)) for v in (c, s1, s2))
    w = dr.shape[1]
    return dr * ct + pltpu.roll(dr * s1t, 8, 1) + pltpu.roll(dr * s2t, w - 8, 1)


_Q, _K, _V = (0, 1024), (1024, 256), (1280, 256)
_HG = (1536, HG_W)
_GATES = (5632, GATE_W)


def _inproj_fwd(name, xh, g, b, wall, b_in, ropes, tm, carry=None):
    T = xh.shape[0]

    def body(xh_ref, g_ref, b_ref, wall_ref, bin_ref, c_ref, s1_ref, s2_ref,
             qkv_ref, kt_ref, vt_ref, hg_ref, gate_ref, yb_ref, w, sem):
        @pl.when(pl.program_id(0) == 0)
        def _():
            _load_rows(wall_ref, [(w, 0, RIN)], sem)

        yb = (xh_ref[...] * g_ref[...] + b_ref[...]).astype(BF16)
        yb_ref[...] = yb
        c, s1, s2 = c_ref[...], s1_ref[...], s2_ref[...]

        def piece(start, width):
            return _dot_nt(yb, w[start:start + width, :]) + bin_ref[:, start:start + width]

        k = _rope(piece(*_K), c, s1, s2)
        v = piece(*_V)
        qkv_ref[:, 0:1024] = _rope(piece(*_Q), c, s1, s2).astype(BF16)
        qkv_ref[:, 1024:1280] = k.astype(BF16)
        qkv_ref[:, 1280:1536] = v.astype(BF16)
        kt_ref[...] = k.T.astype(BF16)
        vt_ref[...] = v.T.astype(BF16)
        for j in range(4):
            hg_ref[:, 1024 * j:1024 * (j + 1)] = piece(_HG[0] + 1024 * j, 1024)
        for j in range(2):
            gate_ref[:, 1024 * j:1024 * (j + 1)] = piece(_GATES[0] + 1024 * j, 1024)

    row = lambda wd: pl.BlockSpec((tm, wd), lambda i: (i, 0))
    vec = lambda wd: pl.BlockSpec((1, wd), lambda i: (0, 0))
    colt = pl.BlockSpec((256, tm), lambda i: (0, i))
    return _call(
        body, name=name, grid=(T // tm,),
        in_specs=[row(D), vec(D), vec(D), HBM_SPEC, vec(DIN), row(128), row(128), row(128)],
        out_specs=[row(QKV_W), colt, colt, row(HG_W), row(GATE_W), row(D)],
        out_shape=[jax.ShapeDtypeStruct((T, QKV_W), BF16), jax.ShapeDtypeStruct((256, T), BF16),
                   jax.ShapeDtypeStruct((256, T), BF16), jax.ShapeDtypeStruct((T, HG_W), F32),
                   jax.ShapeDtypeStruct((T, GATE_W), F32), jax.ShapeDtypeStruct((T, D), BF16)],
        scratch_shapes=[pltpu.VMEM((DIN, D), BF16), pltpu.SemaphoreType.DMA((NDEV,))],
        ins=[xh, g, b, wall, b_in, *ropes], carry=carry)


def _inproj_bwd(name, dq, dk, dv, dhg, dgates, dz2, wall, ropes, tm):
    T = dq.shape[0]

    def body(dq_ref, dk_ref, dv_ref, d0, d1, d2, d3, dga_ref, dgh_ref, dz_ref, wall_ref, c_ref, s1_ref, s2_ref,
             dy_ref, dproj_ref, dbin_ref, w, sem):
        @pl.when(pl.program_id(0) == 0)
        def _():
            _load_rows(wall_ref, [(w, 0, RIN)], sem)
            dbin_ref[...] = jnp.zeros_like(dbin_ref)

        c, s1, s2 = c_ref[...], s1_ref[...], s2_ref[...]
        acc = ALPHA * dz_ref[...]
        pieces = [(_Q[0], _rope_t(dq_ref[...], c, s1, s2)), (_K[0], _rope_t(dk_ref[...], c, s1, s2)),
                  (_V[0], dv_ref[...])]
        pieces += [(_HG[0] + 1024 * j, r[...]) for j, r in enumerate((d0, d1, d2, d3))]
        pieces += [(_GATES[0], dga_ref[...]), (_GATES[0] + 1024, dgh_ref[...])]
        for start, val in pieces:
            width = val.shape[1]
            dbin_ref[:, start:start + width] += jnp.sum(val, axis=0, keepdims=True)
            vb = val.astype(BF16)
            dproj_ref[:, start:start + width] = vb
            acc = acc + _dot(vb, w[start:start + width, :])
        dy_ref[...] = acc

    row = lambda wd: pl.BlockSpec((tm, wd), lambda i: (i, 0))
    return pl.pallas_call(
        body, name=name, grid=(T // tm,),
        in_specs=[row(D), row(256), row(256)] + [row(D)] * 4 + [row(D), row(D), row(D), HBM_SPEC] + [row(128)] * 3,
        out_specs=[row(D), row(DIN), pl.BlockSpec((1, DIN), lambda i: (0, 0))],
        out_shape=[jax.ShapeDtypeStruct((T, D), F32), jax.ShapeDtypeStruct((T, DIN), BF16),
                   jax.ShapeDtypeStruct((1, DIN), F32)],
        scratch_shapes=[pltpu.VMEM((DIN, D), BF16), pltpu.SemaphoreType.DMA((NDEV,))],
        compiler_params=_params("arbitrary"),
    )(dq, dk, dv, *dhg, *dgates, dz2, wall, *ropes)


def _halves(t):
    lane = lax.broadcasted_iota(jnp.int32, t.shape, 1)
    low = lane < HEAD_DIM
    sw = pltpu.roll(t, HEAD_DIM, 1)
    zero = jnp.zeros_like(t)
    h0 = (jnp.where(low, t, zero), jnp.where(low, zero, sw))
    h1 = (jnp.where(low, sw, zero), jnp.where(low, zero, t))
    return h0, h1


def _lane_stack(p_ref, c_ref, gp):
    sl = slice(128 * gp, 128 * (gp + 1))
    hp, hc = _halves(p_ref[:, sl].astype(F32)), _halves(c_ref[:, sl].astype(F32))
    return [jnp.concatenate([hp[gg][0], hc[gg][0], hp[gg][1], hc[gg][1]], axis=0).astype(BF16) for gg in range(2)]


def _row_stack(tp_ref, tc_ref, g):
    band = jnp.concatenate([tp_ref[64 * g:64 * (g + 1), :], tc_ref[64 * g:64 * (g + 1), :]], axis=1)
    z = jnp.zeros_like(band)
    return [jnp.concatenate([band, z], axis=0), jnp.concatenate([z, band], axis=0)]


def _fold(prevm, t4, hh):
    return jnp.where(prevm, t4[256 * hh:256 * hh + 128, :], t4[256 * hh + 128:256 * hh + 256, :])


def _unfold(prevm, t):
    return jnp.concatenate([jnp.where(prevm, t, 0.0), jnp.where(prevm, 0.0, t)], axis=0).astype(BF16)


def _attn_softmax(s, kill, sink):
    s = jnp.where(kill, NEG_INF, s)
    m = jnp.maximum(jnp.max(s, axis=0, keepdims=True), sink)
    p = jnp.exp(s - m)
    es = jnp.exp(sink - m)
    denom = jnp.sum(p, axis=0, keepdims=True) + es
    return p / denom, es / denom


def _attn_masks(first):
    row = lax.broadcasted_iota(jnp.int32, (ATTN_BLOCK, ATTN_BLOCK), 0)
    lane = lax.broadcasted_iota(jnp.int32, (ATTN_BLOCK, ATTN_BLOCK), 1)
    prevm = row > lane
    return prevm, jnp.logical_and(first, prevm)


def _attn_fwd(name, qkv, vt, sinks):
    T = qkv.shape[0]
    nb = T // ATTN_BLOCK
    scale = HEAD_DIM ** -0.5

    def body(q_ref, kp_ref, kc_ref, vtp_ref, vtc_ref, sink_ref, o_ref):
        prevm, kill = _attn_masks(pl.program_id(0) == 0)
        kst = _lane_stack(kp_ref, kc_ref, 0) + _lane_stack(kp_ref, kc_ref, 1)
        vts = [_row_stack(vtp_ref, vtc_ref, g) for g in range(N_KV_HEADS)]
        for j in range(N_Q_HEADS // 2):
            g = j // 2
            qp = q_ref[:, 128 * j:128 * (j + 1)]
            s4 = _dot_nt(kst[g], qp) * scale
            ot = jnp.zeros((128, ATTN_BLOCK), F32)
            for hh in range(2):
                h = 2 * j + hh
                pn, _ = _attn_softmax(_fold(prevm, s4, hh), kill, sink_ref[:, h:h + 1])
                ot = ot + _dot(vts[g][hh], _unfold(prevm, pn))
            o_ref[:, 128 * j:128 * (j + 1)] = ot.T.astype(BF16)

    prev = lambda i: jnp.maximum(i - 1, 0)
    return pl.pallas_call(
        body, name=name, grid=(nb,),
        in_specs=[pl.BlockSpec((ATTN_BLOCK, D), lambda i: (i, 0)),
                  pl.BlockSpec((ATTN_BLOCK, 256), lambda i: (prev(i), 4)),
                  pl.BlockSpec((ATTN_BLOCK, 256), lambda i: (i, 4)),
                  pl.BlockSpec((256, ATTN_BLOCK), lambda i: (0, prev(i))),
                  pl.BlockSpec((256, ATTN_BLOCK), lambda i: (0, i)),
                  pl.BlockSpec((1, N_Q_HEADS), lambda i: (0, 0))],
        out_specs=pl.BlockSpec((ATTN_BLOCK, D), lambda i: (i, 0)),
        out_shape=jax.ShapeDtypeStruct((T, D), BF16),
        compiler_params=_params("arbitrary"),
    )(qkv, qkv, qkv, vt, vt, sinks)


def _attn_bwd(name, qkv, kt, sinks, dya, carry=None):
    T = qkv.shape[0]
    nb = T // ATTN_BLOCK
    scale = HEAD_DIM ** -0.5

    def body(q_ref, kp_ref, kc_ref, vp_ref, vc_ref, ktp_ref, ktc_ref, sink_ref, do_ref,
             dq_ref, dk_ref, dv_ref, ds_ref):
        i = pl.program_id(0)

        @pl.when(i == 0)
        def _():
            ds_ref[...] = jnp.zeros_like(ds_ref)

        prevm, kill = _attn_masks(i == 0)
        kst = _lane_stack(kp_ref, kc_ref, 0) + _lane_stack(kp_ref, kc_ref, 1)
        vst = _lane_stack(vp_ref, vc_ref, 0) + _lane_stack(vp_ref, vc_ref, 1)
        kts = [_row_stack(ktp_ref, ktc_ref, g) for g in range(N_KV_HEADS)]
        lane = lax.broadcasted_iota(jnp.int32, (2 * ATTN_BLOCK, 128), 1)
        low = lane < HEAD_DIM
        slane = lax.broadcasted_iota(jnp.int32, (1, 128), 1)
        dkz = [jnp.zeros((2 * ATTN_BLOCK, 128), F32) for _ in range(N_KV_HEADS)]
        dvz = [jnp.zeros((2 * ATTN_BLOCK, 128), F32) for _ in range(N_KV_HEADS)]
        dsink = jnp.zeros((1, 128), F32)
        for j in range(N_Q_HEADS // 2):
            g = j // 2
            qp = q_ref[:, 128 * j:128 * (j + 1)]
            dop = do_ref[:, 128 * j:128 * (j + 1)]
            s4 = _dot_nt(kst[g], qp) * scale
            dp4 = _dot_nt(vst[g], dop)
            dqt = jnp.zeros((128, ATTN_BLOCK), F32)
            ds2, p2 = [], []
            for hh in range(2):
                h = 2 * j + hh
                pn, ps = _attn_softmax(_fold(prevm, s4, hh), kill, sink_ref[:, h:h + 1])
                dp = _fold(prevm, dp4, hh)
                delta = jnp.sum(pn * dp, axis=0, keepdims=True)
                dsink = dsink + jnp.where(slane == h, -jnp.sum(ps * delta), 0.0)
                ds2.append(_unfold(prevm, pn * (dp - delta) * scale))
                p2.append(_unfold(prevm, pn))
                dqt = dqt + _dot(kts[g][hh], ds2[hh])
            dq_ref[:, 128 * j:128 * (j + 1)] = dqt.T
            dk4 = _dot(jnp.concatenate(ds2, axis=0), qp)
            dv4 = _dot(jnp.concatenate(p2, axis=0), dop)
            for hh in range(2):
                own = low if hh == 0 else jnp.logical_not(low)
                dk_h = jnp.where(own, dk4[256 * hh:256 * (hh + 1), :], 0.0)
                dv_h = jnp.where(own, dv4[256 * hh:256 * (hh + 1), :], 0.0)
                if hh != g % 2:
                    dk_h = pltpu.roll(dk_h, HEAD_DIM, 1)
                    dv_h = pltpu.roll(dv_h, HEAD_DIM, 1)
                dkz[g] = dkz[g] + dk_h
                dvz[g] = dvz[g] + dv_h
        ds_ref[...] += dsink
        cur = pl.ds(pl.multiple_of(i * ATTN_BLOCK, ATTN_BLOCK), ATTN_BLOCK)
        prv = pl.ds(pl.multiple_of(jnp.maximum(i - 1, 0) * ATTN_BLOCK, ATTN_BLOCK), ATTN_BLOCK)
        for gp in range(N_KV_HEADS // 2):
            cols = slice(128 * gp, 128 * (gp + 1))
            dkb = dkz[2 * gp] + dkz[2 * gp + 1]
            dvb = dvz[2 * gp] + dvz[2 * gp + 1]
            dk_ref[cur, cols] = dkb[ATTN_BLOCK:, :]
            dv_ref[cur, cols] = dvb[ATTN_BLOCK:, :]

            @pl.when(i > 0)
            def _():
                dk_ref[prv, cols] += dkb[:ATTN_BLOCK, :]
                dv_ref[prv, cols] += dvb[:ATTN_BLOCK, :]

    prev = lambda i: jnp.maximum(i - 1, 0)
    whole = lambda w: pl.BlockSpec((T, w), lambda i: (0, 0))
    return _call(
        body, name=name, grid=(nb,), scratch_shapes=[], ins=[qkv, qkv, qkv, qkv, qkv, kt, kt, sinks, dya], carry=carry,
        in_specs=[pl.BlockSpec((ATTN_BLOCK, D), lambda i: (i, 0)),
                  pl.BlockSpec((ATTN_BLOCK, 256), lambda i: (prev(i), 4)),
                  pl.BlockSpec((ATTN_BLOCK, 256), lambda i: (i, 4)),
                  pl.BlockSpec((ATTN_BLOCK, 256), lambda i: (prev(i), 5)),
                  pl.BlockSpec((ATTN_BLOCK, 256), lambda i: (i, 5)),
                  pl.BlockSpec((256, ATTN_BLOCK), lambda i: (0, prev(i))),
                  pl.BlockSpec((256, ATTN_BLOCK), lambda i: (0, i)),
                  pl.BlockSpec((1, N_Q_HEADS), lambda i: (0, 0)),
                  pl.BlockSpec((ATTN_BLOCK, D), lambda i: (i, 0))],
        out_specs=[pl.BlockSpec((ATTN_BLOCK, D), lambda i: (i, 0)), whole(256), whole(256),
                   pl.BlockSpec((1, 128), lambda i: (0, 0))],
        out_shape=[jax.ShapeDtypeStruct((T, D), F32), jax.ShapeDtypeStruct((T, 256), F32),
                   jax.ShapeDtypeStruct((T, 256), F32), jax.ShapeDtypeStruct((1, 128), F32)])


HG_SUB = 256
HG_TILE = 1024
CPS = HG_SUB // HGRN_CHUNK
HG_HEADS_PER_STEP = 4


def _chunk_sum(tri, x):
    w = x.shape[1]
    h1 = x.astype(BF16)
    h2 = (x - h1.astype(F32)).astype(BF16)
    r = _dot(tri, jnp.concatenate([h1, h2], axis=1))
    return r[:, :w] + r[:, w:]


def _chunk_tri():
    ri = lax.broadcasted_iota(jnp.int32, (HG_SUB, HG_SUB), 0)
    ci = lax.broadcasted_iota(jnp.int32, (HG_SUB, HG_SUB), 1)
    return jnp.where((ri >= ci) & (ri // HGRN_CHUNK == ci // HGRN_CHUNK), 1.0, 0.0).astype(BF16)


def _chunks(t):
    return [t[HGRN_CHUNK * c:HGRN_CHUNK * (c + 1), :] for c in range(CPS)]


def _lower_bound(lbl_ref):
    l0, l1 = lbl_ref[0:1, :], lbl_ref[1:2, :]
    m = jnp.maximum(l0, l1)
    e0, e1 = jnp.exp(l0 - m), jnp.exp(l1 - m)
    return e0 / (e0 + e1)


def _heads(t):
    return [t[:, 128 * h:128 * (h + 1)] for h in range(t.shape[1] // 128)]


def _per_head(fn, *wide):
    return jnp.concatenate([fn(*parts) for parts in zip(*[_heads(t) for t in wide])], axis=1)


def _hgrn_sub(fl, qh, vv, lb, tril_b):
    w = fl.shape[1]
    sg = _sigmoid(fl)
    f = lb + (1.0 - lb) * sg
    k = 1.0 - f
    gc = _chunk_sum(tril_b, jnp.log(f))
    gl_c = [t[HGRN_CHUNK - 1:HGRN_CHUNK, :] for t in _chunks(gc)]
    gl = jnp.concatenate([jnp.broadcast_to(g, (HGRN_CHUNK, w)) for g in gl_c], axis=0)
    sq = _sigmoid(qh)
    eg = jnp.exp(gc)
    eng = jnp.exp(-gc)
    elg = jnp.exp(gl - gc)
    qd = qh * sq * eg
    ki = k * eng
    ke = k * elg
    qd_b, ki_b, ke_b, v_b = (t.astype(BF16) for t in (qd, ki, ke, vv))
    am_b = [jnp.where(tril_b > 0, _dot_nt(qq, kk), 0.0).astype(BF16) for qq, kk in zip(_heads(qd_b), _heads(ki_b))]
    return dict(sg=sg, f=f, sq=sq, eg=eg, eng=eng, elg=elg, qd=qd, ki=ki, ke=ke, egl=[jnp.exp(g) for g in gl_c],
                am_b=am_b, qd_b=qd_b, ki_b=ki_b, ke_b=ke_b, v_b=v_b)


def _hgrn_out(q, st_b):
    outs = []
    for h, (qd_h, v_h) in enumerate(zip(_heads(q["qd_b"]), _heads(q["v_b"]))):
        inter = [_dot_nt(qc, s) for qc, s in zip(_chunks(qd_h), st_b[h])]
        outs.append(_dot(q["am_b"][h], v_h) + jnp.concatenate(inter, axis=0))
    return jnp.concatenate(outs, axis=1)


def _head_rms(o):
    return _per_head(lambda t: jnp.broadcast_to(
        lax.rsqrt(jnp.mean(t * t, axis=-1, keepdims=True) + RMS_EPS), t.shape), o)


def _hgrn_fwd(name, hg, lb_logits, norm_g, tri, th):
    T = hg.shape[0]
    nc = th // HGRN_CHUNK
    hps = HG_HEADS_PER_STEP

    def body(fl_ref, qh_ref, v_ref, og_ref, lbl_ref, ng_ref, tri_ref, y_ref, ss_ref, st):
        @pl.when(pl.program_id(1) == 0)
        def _():
            st[...] = jnp.zeros_like(st)

        lbs = _lower_bound(lbl_ref)
        ng = jnp.tile(ng_ref[...], (1, hps))

        def sub(si, carry):
            rows = pl.ds(pl.multiple_of(si * HG_SUB, HG_SUB), HG_SUB)
            q = _hgrn_sub(fl_ref[rows, :], qh_ref[rows, :], v_ref[rows, :], lbs, tri_ref[...])
            v_hc = [_chunks(t) for t in _heads(q["v_b"])]
            k_hc = [_chunks(t) for t in _heads(q["ke_b"])]
            s = [st[h] for h in range(hps)]
            st_b = [[] for _ in range(hps)]
            for c in range(CPS):
                egl = _heads(q["egl"][c])
                for h in range(hps):
                    ss_ref[h, si * CPS + c] = s[h]
                    st_b[h].append(s[h].astype(BF16))
                    s[h] = s[h] * egl[h] + _dot_tn(v_hc[h][c], k_hc[h][c])
            for h in range(hps):
                st[h] = s[h]
            o = _hgrn_out(q, st_b)
            og = og_ref[rows, :]
            y_ref[rows, :] = (o * _head_rms(o) * ng * (og * _sigmoid(og))).astype(BF16)
            return carry

        lax.fori_loop(0, th // HG_SUB, sub, 0)

    wd = 128 * hps
    col = lambda j: pl.BlockSpec((th, wd), lambda h, t: (t, (8 // hps) * j + h))
    return pl.pallas_call(
        body, name=name, grid=(HGRN_HEADS // hps, T // th),
        in_specs=[col(0), col(1), col(2), col(3), pl.BlockSpec((2, wd), lambda h, t: (0, h)),
                  pl.BlockSpec((1, 128), lambda h, t: (0, 0)), pl.BlockSpec((HG_SUB, HG_SUB), lambda h, t: (0, 0))],
        out_specs=[pl.BlockSpec((th, wd), lambda h, t: (t, h)),
                   pl.BlockSpec((hps, nc, 128, 128), lambda h, t: (h, t, 0, 0))],
        out_shape=[jax.ShapeDtypeStruct((T, D), BF16),
                   jax.ShapeDtypeStruct((HGRN_HEADS, T // HGRN_CHUNK, 128, 128), F32)],
        scratch_shapes=[pltpu.VMEM((hps, 128, 128), F32)],
        compiler_params=_params("parallel", "arbitrary"),
    )(hg, hg, hg, hg, lb_logits, norm_g, tri)


def _hgrn_bwd(name, hg, lb_logits, norm_g, tri, tri_t, sstart, dyh, th, carry=None):
    T = hg.shape[0]
    nc = th // HGRN_CHUNK
    nt = T // th
    hps = HG_HEADS_PER_STEP
    wd = 128 * hps

    def body(fl_ref, qh_ref, v_ref, og_ref, lbl_ref, ng_ref, tri_ref, trit_ref, ss_ref, dy_ref,
             dfl_ref, dqh_ref, dv_ref, dog_ref, dlb_ref, dng_ref, dst):
        @pl.when(pl.program_id(1) == 0)
        def _():
            dst[...] = jnp.zeros_like(dst)
            dlb_ref[...] = jnp.zeros_like(dlb_ref)
            dng_ref[...] = jnp.zeros_like(dng_ref)

        lb = _lower_bound(lbl_ref)
        ng = jnp.tile(ng_ref[...], (1, hps))
        last = lax.broadcasted_iota(jnp.int32, (HGRN_CHUNK, wd), 0) == HGRN_CHUNK - 1
        nsub = th // HG_SUB
        cat0 = lambda parts: jnp.concatenate(parts, axis=0)

        def sub(step, carry):
            si = nsub - 1 - step
            rows = pl.ds(pl.multiple_of(si * HG_SUB, HG_SUB), HG_SUB)
            qh, og, dy = qh_ref[rows, :], og_ref[rows, :], dy_ref[rows, :]
            tril_b = tri_ref[...]
            q = _hgrn_sub(fl_ref[rows, :], qh, v_ref[rows, :], lb, tril_b)
            s_in = [[ss_ref[h, si * CPS + c] for c in range(CPS)] for h in range(hps)]
            st_b = [[s.astype(BF16) for s in row] for row in s_in]
            o = _hgrn_out(q, st_b)
            rr = _head_rms(o)
            oh = o * rr
            sog = _sigmoid(og)
            dog_ref[rows, :] = dy * oh * ng * (sog * (1.0 + og * (1.0 - sog)))
            don = dy * (og * sog)
            dng_w = jnp.sum(don * oh, axis=0, keepdims=True)
            dd = don * ng
            mean_h = _per_head(lambda t: jnp.broadcast_to(jnp.mean(t, axis=-1, keepdims=True), t.shape), dd * oh)
            do_b = (rr * (dd - oh * mean_h)).astype(BF16)
            do_h, qd_h, ki_h, ke_h, v_h = (_heads(t) for t in (do_b, q["qd_b"], q["ki_b"], q["ke_b"], q["v_b"]))
            da_b = [jnp.where(tril_b > 0, _dot_nt(do_h[h], v_h[h]), 0.0).astype(BF16) for h in range(hps)]
            do_c, qd_c, ke_c, v_c = ([_chunks(t) for t in hs] for hs in (do_h, qd_h, ke_h, v_h))
            dsp = [[None] * CPS for _ in range(hps)]
            dgl_dec = [[None] * CPS for _ in range(hps)]
            d = [dst[h] for h in range(hps)]
            for c in reversed(range(CPS)):
                egl = _heads(q["egl"][c])
                for h in range(hps):
                    dsp[h][c] = d[h]
                    dgl_dec[h][c] = jnp.sum(d[h] * s_in[h][c], axis=0, keepdims=True) * egl[h]
                    d[h] = d[h] * egl[h] + _dot_tn(do_c[h][c], qd_c[h][c])
            for h in range(hps):
                dst[h] = d[h]
                dng_ref[h] += dng_w[:, 128 * h:128 * (h + 1)]
            dsp_b = [[t.astype(BF16) for t in row] for row in dsp]
            dv_ref[rows, :] = jnp.concatenate(
                [_dot_tn(q["am_b"][h], do_h[h]) + cat0([_dot_nt(ke_c[h][c], dsp_b[h][c]) for c in range(CPS)])
                 for h in range(hps)], axis=1)
            dqd = jnp.concatenate(
                [_dot(da_b[h], ki_h[h]) + cat0([_dot(do_c[h][c], st_b[h][c]) for c in range(CPS)])
                 for h in range(hps)], axis=1)
            dki = jnp.concatenate([_dot_tn(da_b[h], qd_h[h]) for h in range(hps)], axis=1)
            dke = jnp.concatenate([cat0([_dot(v_c[h][c], dsp_b[h][c]) for c in range(CPS)]) for h in range(hps)], axis=1)
            dkk = dke * q["ke"]
            dgl = [jnp.sum(t, axis=0, keepdims=True) + jnp.concatenate([dgl_dec[h][c] for h in range(hps)], axis=1)
                   for c, t in enumerate(_chunks(dkk))]
            dgc = dqd * q["qd"] - dki * q["ki"] - dkk + cat0([jnp.where(last, g, 0.0) for g in dgl])
            dlf = _chunk_sum(trit_ref[...], dgc)
            df = dlf / q["f"] - (dki * q["eng"] + dke * q["elg"])
            sg = q["sg"]
            dlb_ref[...] += jnp.sum(df * (1.0 - sg), axis=0, keepdims=True)
            dfl_ref[rows, :] = df * (1.0 - lb) * sg * (1.0 - sg)
            sq = q["sq"]
            dqh_ref[rows, :] = dqd * q["eg"] * (sq * (1.0 + qh * (1.0 - sq)))
            return carry

        lax.fori_loop(0, nsub, sub, 0)
    col = lambda j: pl.BlockSpec((th, wd), lambda h, t: (nt - 1 - t, (8 // hps) * j + h))
    out = pl.BlockSpec((th, wd), lambda h, t: (nt - 1 - t, h))
    tri_spec = pl.BlockSpec((HG_SUB, HG_SUB), lambda h, t: (0, 0))
    return _call(
        body, name=name, grid=(HGRN_HEADS // hps, nt),
        in_specs=[col(0), col(1), col(2), col(3), pl.BlockSpec((2, wd), lambda h, t: (0, h)),
                  pl.BlockSpec((1, 128), lambda h, t: (0, 0)), tri_spec, tri_spec,
                  pl.BlockSpec((hps, nc, 128, 128), lambda h, t: (h, nt - 1 - t, 0, 0)), out],
        out_specs=[out, out, out, out, pl.BlockSpec((1, wd), lambda h, t: (0, h)),
                   pl.BlockSpec((hps, 1, 128), lambda h, t: (h, 0, 0))],
        out_shape=[jax.ShapeDtypeStruct((T, D), F32)] * 4 + [jax.ShapeDtypeStruct((1, D), F32),
                                                              jax.ShapeDtypeStruct((HGRN_HEADS, 1, 128), F32)],
        scratch_shapes=[pltpu.VMEM((hps, 128, 128), F32)],
        ins=[hg, hg, hg, hg, lb_logits, norm_g, tri, tri_t, sstart, dyh], carry=carry)


def _mix_fwd(name, ya, yh, gates, xh1, g1, b1, wall, tm):
    T = ya.shape[0]

    def body(ya_ref, yh_ref, ga_ref, gh_ref, xh_ref, g_ref, b_ref, wall_ref,
             xo_ref, r_ref, pa_ref, ph_ref, mg_ref, wpa, wph, wo, sem):
        @pl.when(pl.program_id(0) == 0)
        def _():
            _load_rows(wall_ref, [(wpa, RIN, RP), (wph, RIN + RP, RP), (wo, RIN + 2 * RP, RP)], sem)

        pa = _dot(ya_ref[...], wpa[...]).astype(BF16)
        ph = _dot(yh_ref[...], wph[...]).astype(BF16)
        pa_ref[...] = pa
        ph_ref[...] = ph
        mg = (_sigmoid(ga_ref[...]) * pa.astype(F32) + _sigmoid(gh_ref[...]) * ph.astype(F32)).astype(BF16)
        mg_ref[...] = mg
        y1 = xh_ref[...] * g_ref[...] + b_ref[...]
        xh2, r = _ln_fwd(ALPHA * y1 + _dot(mg, wo[...]))
        xo_ref[...] = xh2
        r_ref[...] = jnp.broadcast_to(r, (tm, 128))

    row = lambda w: pl.BlockSpec((tm, w), lambda i: (i, 0))
    vec = pl.BlockSpec((1, D), lambda i: (0, 0))
    return pl.pallas_call(
        body, name=name, grid=(T // tm,),
        in_specs=[row(D), row(D), pl.BlockSpec((tm, D), lambda i: (i, 0)), pl.BlockSpec((tm, D), lambda i: (i, 1)),
                  row(D), vec, vec, HBM_SPEC],
        out_specs=[row(D), row(128), row(D), row(D), row(D)],
        out_shape=[jax.ShapeDtypeStruct((T, D), F32), jax.ShapeDtypeStruct((T, 128), F32)]
        + [jax.ShapeDtypeStruct((T, D), BF16)] * 3,
        scratch_shapes=[pltpu.VMEM((D, D), BF16)] * 3 + [pltpu.SemaphoreType.DMA((3 * NDEV,))],
        compiler_params=_params("arbitrary"),
    )(ya, yh, gates, gates, xh1, g1, b1, wall)


def _mix_bwd(name, dy2, xh2, r2, g2, gates, pa, ph, wall, tm):
    T = dy2.shape[0]

    def body(dy_ref, xh_ref, r_ref, g_ref, ga_ref, gh_ref, pa_ref, ph_ref, wall_ref,
             dz_ref, dmix_ref, dpa_ref, dph_ref, dga_ref, dgh_ref, dya_ref, dyh_ref, dg_ref, db_ref,
             wpa, wph, wo, sem):
        @pl.when(pl.program_id(0) == 0)
        def _():
            _load_rows(wall_ref, [(wpa, RIN, RP), (wph, RIN + RP, RP), (wo, RIN + 2 * RP, RP)], sem)
            dg_ref[...] = jnp.zeros_like(dg_ref)
            db_ref[...] = jnp.zeros_like(db_ref)

        dz, dgp, dbp = _ln_bwd(dy_ref[...], xh_ref[...], r_ref[:, :1], g_ref[...])
        dg_ref[...] += dgp
        db_ref[...] += dbp
        dz_ref[...] = dz
        dmix = dz.astype(BF16)
        dmix_ref[...] = dmix
        dmg = _dot_nt(dmix, wo[...])
        sa, sh = _sigmoid(ga_ref[...]), _sigmoid(gh_ref[...])
        dga_ref[...] = dmg * pa_ref[...].astype(F32) * sa * (1.0 - sa)
        dgh_ref[...] = dmg * ph_ref[...].astype(F32) * sh * (1.0 - sh)
        dpa = (dmg * sa).astype(BF16)
        dph = (dmg * sh).astype(BF16)
        dpa_ref[...] = dpa
        dph_ref[...] = dph
        dya_ref[...] = _dot_nt(dpa, wpa[...]).astype(BF16)
        dyh_ref[...] = _dot_nt(dph, wph[...])

    row = lambda w: pl.BlockSpec((tm, w), lambda i: (i, 0))
    vec = pl.BlockSpec((1, D), lambda i: (0, 0))
    return pl.pallas_call(
        body, name=name, grid=(T // tm,),
        in_specs=[row(D), row(D), row(128), vec, pl.BlockSpec((tm, D), lambda i: (i, 0)),
                  pl.BlockSpec((tm, D), lambda i: (i, 1)), row(D), row(D), HBM_SPEC],
        out_specs=[row(D)] * 8 + [vec, vec],
        out_shape=[jax.ShapeDtypeStruct((T, D), F32)] + [jax.ShapeDtypeStruct((T, D), BF16)] * 3
        + [jax.ShapeDtypeStruct((T, D), F32)] * 2 + [jax.ShapeDtypeStruct((T, D), BF16), jax.ShapeDtypeStruct((T, D), F32)]
        + [jax.ShapeDtypeStruct((1, D), F32)] * 2,
        scratch_shapes=[pltpu.VMEM((D, D), BF16)] * 3 + [pltpu.SemaphoreType.DMA((3 * NDEV,))],
        compiler_params=_params("arbitrary"),
    )(dy2, xh2, r2, g2, gates, gates, pa, ph, wall)


def _adam(w, g, m, v):
    m = ADAM_B1 * m + (1.0 - ADAM_B1) * g
    v = ADAM_B2 * v + (1.0 - ADAM_B2) * (g * g)
    m_hat = m / (1.0 - ADAM_B1 ** ADAM_STEP)
    v_hat = v / (1.0 - ADAM_B2 ** ADAM_STEP)
    delta = -ADAM_LR * (m_hat / (jnp.sqrt(v_hat) + ADAM_EPS) + ADAM_WD * w)
    return delta, m, v


def _grad_step(name, recv, w, m, v, transposed):
    _, rows, _ = recv.shape
    cb = 256
    pad = (-rows) % 128

    def body(r_ref, w_ref, m_ref, v_ref, g_ref, d_ref, mo_ref, vo_ref):
        acc = r_ref[0].astype(F32)
        for k in range(1, NDEV):
            acc = acc + r_ref[k].astype(F32)
        if transposed:
            if pad:
                acc = jnp.concatenate([acc, jnp.zeros((pad, cb), F32)], axis=0)
            acc = acc.T[:, :rows]
        g_ref[...] = acc
        d, mm, vv = _adam(w_ref[...], acc, m_ref[...], v_ref[...])
        d_ref[...] = d
        mo_ref[...] = mm
        vo_ref[...] = vv

    blk = pl.BlockSpec((cb, rows), lambda j: (j, 0)) if transposed else pl.BlockSpec((rows, cb), lambda j: (0, j))
    return pl.pallas_call(
        body, name=name, grid=(D // cb,),
        in_specs=[pl.BlockSpec((NDEV, rows, cb), lambda j: (0, 0, j)), blk, blk, blk],
        out_specs=[blk] * 4, out_shape=[jax.ShapeDtypeStruct(w.shape, F32)] * 4,
        compiler_params=_params("parallel"),
    )(recv, w, m, v)


_SMALL = [("ln1_g", D), ("ln1_b", D), ("ln2_g", D), ("ln2_b", D), ("ln3_g", D), ("ln3_b", D),
          ("b_in", DIN), ("lb", D), ("attn_sinks", 128), ("hgrn_norm_g", 128), ("loss", 128)]
_SMALL_OFF = {}
_o = 0
for _n, _w in _SMALL:
    _SMALL_OFF[_n] = (_o, _w)
    _o += _w
PACK = _o


def _small_reduce(name, packed):
    def body(p_ref, o_ref, buf, send_sems, recv_sems):
        x, y, c = lax.axis_index("x"), lax.axis_index("y"), lax.axis_index("c")
        me = 4 * x + 2 * y + c
        buf[me] = p_ref[...]
        copies = []
        for j in range(1, NDEV):
            px, py, pc = x ^ (j >> 2), y ^ ((j >> 1) & 1), c ^ (j & 1)
            cp = pltpu.make_async_remote_copy(
                src_ref=p_ref, dst_ref=buf.at[me], send_sem=send_sems.at[j - 1], recv_sem=recv_sems.at[j - 1],
                device_id=(px, py, pc), device_id_type=MESH)
            cp.start()
            copies.append(cp)
        for cp in copies:
            cp.wait()
        tot = buf[0]
        for k in range(1, NDEV):
            tot = tot + buf[k]
        o_ref[...] = tot

    return pl.pallas_call(
        body, name=name, out_shape=jax.ShapeDtypeStruct((1, PACK), F32),
        scratch_shapes=[pltpu.VMEM((NDEV, 1, PACK), F32), pltpu.SemaphoreType.DMA((NDEV - 1,)),
                        pltpu.SemaphoreType.DMA((NDEV - 1,))],
        compiler_params=pltpu.CompilerParams(vmem_limit_bytes=VMEM_LIMIT),
    )(packed)


def _small_step(name, total, small_w, small_m, small_v):
    names = ["ln1_g", "ln1_b", "ln2_g", "ln2_b", "ln3_g", "ln3_b", "b_in", "attn_sinks", "hgrn_lb_logits", "hgrn_norm_g"]
    np_ = len(names)

    def body(*refs):
        t_ref = refs[0]
        w_refs = refs[1:1 + np_]
        m_refs = refs[1 + np_:1 + 2 * np_]
        v_refs = refs[1 + 2 * np_:1 + 3 * np_]
        loss_ref, outs = refs[1 + 3 * np_], refs[2 + 3 * np_:]

        def part(n):
            o, w = _SMALL_OFF[n]
            return t_ref[:, o:o + w]

        loss_ref[...] = part("loss")
        for i, n in enumerate(names):
            w = w_refs[i][...]
            if n == "hgrn_lb_logits":
                m_ = jnp.maximum(w[0:1, :], w[1:2, :])
                e0, e1 = jnp.exp(w[0:1, :] - m_), jnp.exp(w[1:2, :] - m_)
                p0 = e0 / (e0 + e1)
                t = p0 * (1.0 - p0) * part("lb")
                g = jnp.concatenate([t, -t], axis=0)
            elif n == "attn_sinks":
                g = part(n)[:, :N_Q_HEADS]
            else:
                g = part(n)
            d, mm, vv = _adam(w, g, m_refs[i][...], v_refs[i][...])
            outs[4 * i][...] = g
            outs[4 * i + 1][...] = d
            outs[4 * i + 2][...] = mm
            outs[4 * i + 3][...] = vv

    out_shape = [jax.ShapeDtypeStruct((1, 128), F32)]
    for n in names:
        out_shape += [jax.ShapeDtypeStruct(small_w[n].shape, F32)] * 4
    return pl.pallas_call(
        body, name=name, out_shape=out_shape,
        compiler_params=pltpu.CompilerParams(vmem_limit_bytes=VMEM_LIMIT),
    )(total, *[small_w[n] for n in names], *[small_m[n] for n in names], *[small_v[n] for n in names]), names


def _tile(T, pref):
    return min(T, pref)


def kernel(x, ln1_g, ln1_b, ffn1_w1, ffn1_w3, ffn1_w2, ln2_g, ln2_b, w_in, b_in, attn_sinks, hgrn_lb_logits, hgrn_norm_g, w_proj_attn, w_proj_hgrn, w_out, ln3_g, ln3_b, ffn2_w1, ffn2_w3, ffn2_w2, loss_target, m_ln1_g, m_ln1_b, m_ffn1_w1, m_ffn1_w3, m_ffn1_w2, m_ln2_g, m_ln2_b, m_w_in, m_b_in, m_attn_sinks, m_hgrn_lb_logits, m_hgrn_norm_g, m_w_proj_attn, m_w_proj_hgrn, m_w_out, m_ln3_g, m_ln3_b, m_ffn2_w1, m_ffn2_w3, m_ffn2_w2, v_ln1_g, v_ln1_b, v_ffn1_w1, v_ffn1_w3, v_ffn1_w2, v_ln2_g, v_ln2_b, v_w_in, v_b_in, v_attn_sinks, v_hgrn_lb_logits, v_hgrn_norm_g, v_w_proj_attn, v_w_proj_hgrn, v_w_out, v_ln3_g, v_ln3_b, v_ffn2_w1, v_ffn2_w3, v_ffn2_w2):
    T = x.shape[1]
    xs = x[0]
    tgt = loss_target[0]
    tm = _tile(T, 256)
    tk = _tile(T, 512)
    th = _tile(T, HG_TILE)

    t_bf = lambda w: w[0].T.astype(BF16)
    n_bf = lambda w: w[0].astype(BF16)
    ffn_shard = lambda w1, w3, w2: jnp.concatenate([t_bf(w1), t_bf(w3), n_bf(w2)], axis=0)
    mix_shard = jnp.concatenate([t_bf(w_in), n_bf(w_proj_attn), n_bf(w_proj_hgrn), n_bf(w_out)], axis=0)
    (ffn1_all,) = _exchange_call("gather_ffn1", _gather_exchange(ffn_shard(ffn1_w1, ffn1_w3, ffn1_w2)))
    ffn_offs = (0, RF, 2 * RF)
    ropes = _rope_tables(T)
    tri = _chunk_tri()
    tri_t = tri.T

    (xh1, r1, a1, b1, xb0), (mix_all,) = _ffn_fwd("ffn1_fwd", xs, None, ffn1_all, ffn_offs, tm,
                                                  carry=_gather_exchange(mix_shard))
    (qkv, kt, vt, hg, gates, y1b), (ffn2_all,) = _inproj_fwd(
        "inproj_fwd", xh1, ln1_g, ln1_b, mix_all, b_in, ropes, tm,
        carry=_gather_exchange(ffn_shard(ffn2_w1, ffn2_w3, ffn2_w2)))
    ya = _attn_fwd("attn_fwd", qkv, vt, attn_sinks)
    yh, sstart = _hgrn_fwd("hgrn_fwd", hg, hgrn_lb_logits, hgrn_norm_g, tri, th)
    xh2, r2, pa, ph, merged = _mix_fwd("mix_fwd", ya, yh, gates, xh1, ln1_g, ln1_b, mix_all, tm)
    (xh3, r3, a2, b2, y2b, loss_part), _ = _ffn_fwd("ffn2_fwd", xh2, (ln2_g, ln2_b), ffn2_all, ffn_offs, tm,
                                                    loss=(ln3_g, ln3_b, tgt))

    (dy2, dab2, u2, df2, dg3, db3), _ = _ffn_bwd("ffn2_bwd", ("loss", ln3_b, tgt), xh3, r3, ln3_g, a2, b2, ffn2_all,
                                                 ffn_offs, tm)
    g_ffn2_13, _ = _wgrad("wgrad_ffn2_w13", dab2, y2b, DFF // 2, tk)
    g_ffn2_2, _ = _wgrad("wgrad_ffn2_w2", u2, df2, DFF // 2, tk)
    dz2, dmix, dpa, dph, dga, dgh, dya, dyh, dg2, db2 = _mix_bwd("mix_bwd", dy2, xh2, r2, ln2_g, gates, pa, ph,
                                                                 mix_all, tm)
    g_wo, _ = _wgrad("wgrad_w_out", merged, dmix, D, tk)
    g_pa, _ = _wgrad("wgrad_w_pa", ya, dpa, D, tk)
    g_ph, _ = _wgrad("wgrad_w_ph", yh, dph, D, tk)
    recv = {}
    (dq, dk, dv, dsink), (recv["ffn2_w1"], recv["ffn2_w3"], recv["ffn2_w2"]) = _attn_bwd(
        "attn_bwd", qkv, kt, attn_sinks, dya,
        carry=_grad_exchange([g_ffn2_13, g_ffn2_13, g_ffn2_2], [0, DFF, 0], [RF, RF, RF]))
    (dfl, dqh, dih, dog, dlb, dng), (recv["w_proj_attn"], recv["w_proj_hgrn"], recv["w_out"]) = _hgrn_bwd(
        "hgrn_bwd", hg, hgrn_lb_logits, hgrn_norm_g, tri, tri_t, sstart, dyh, th,
        carry=_grad_exchange([g_pa, g_ph, g_wo], [0, 0, 0], [RP, RP, RP]))
    dy1, dproj, dbin = _inproj_bwd("inproj_bwd", dq, dk, dv, (dfl, dqh, dih, dog), (dga, dgh), dz2, mix_all, ropes, tm)
    g_win, _ = _wgrad("wgrad_w_in", dproj, y1b, DIN // 4, tk)
    (gx, dab1, u1, df1, dg1, db1), (recv["w_in"],) = _ffn_bwd(
        "ffn1_bwd", ("dy", dy1), xh1, r1, ln1_g, a1, b1, ffn1_all, ffn_offs, tm,
        carry=_grad_exchange([g_win], [0], [RIN]))
    g_ffn1_2, _ = _wgrad("wgrad_ffn1_w2", u1, df1, DFF // 2, tk)
    g_ffn1_1, (recv["ffn1_w2"],) = _wgrad("wgrad_ffn1_w1", dab1, xb0, DFF // 2, tk, 0, DFF,
                                          carry=_grad_exchange([g_ffn1_2], [0], [RF]))
    g_ffn1_3, (recv["ffn1_w1"],) = _wgrad("wgrad_ffn1_w3", dab1, xb0, DFF // 2, tk, DFF, DFF,
                                          carry=_grad_exchange([g_ffn1_1], [0], [RF]))
    (recv["ffn1_w3"],) = _exchange_call("exchange_ffn1_w3", _grad_exchange([g_ffn1_3], [0], [RF]))

    big = [("ffn1_w1", ffn1_w1, m_ffn1_w1, v_ffn1_w1, True), ("ffn1_w3", ffn1_w3, m_ffn1_w3, v_ffn1_w3, True),
           ("ffn1_w2", ffn1_w2, m_ffn1_w2, v_ffn1_w2, False), ("w_in", w_in, m_w_in, v_w_in, True),
           ("w_proj_attn", w_proj_attn, m_w_proj_attn, v_w_proj_attn, False),
           ("w_proj_hgrn", w_proj_hgrn, m_w_proj_hgrn, v_w_proj_hgrn, False),
           ("w_out", w_out, m_w_out, v_w_out, False),
           ("ffn2_w1", ffn2_w1, m_ffn2_w1, v_ffn2_w1, True), ("ffn2_w3", ffn2_w3, m_ffn2_w3, v_ffn2_w3, True),
           ("ffn2_w2", ffn2_w2, m_ffn2_w2, v_ffn2_w2, False)]
    res = {}
    for n, w, m, v, transposed in big:
        res[n] = tuple(t[None] for t in _grad_step("step_" + n, recv[n], w[0], m[0], v[0], transposed))

    parts = {"ln1_g": dg1, "ln1_b": db1, "ln2_g": dg2, "ln2_b": db2, "ln3_g": dg3, "ln3_b": db3, "b_in": dbin,
             "lb": dlb, "attn_sinks": dsink, "hgrn_norm_g": jnp.sum(dng, axis=0), "loss": loss_part[0:1, :]}
    packed = jnp.concatenate([parts[n] for n, _ in _SMALL], axis=1)
    small_w = dict(ln1_g=ln1_g, ln1_b=ln1_b, ln2_g=ln2_g, ln2_b=ln2_b, ln3_g=ln3_g, ln3_b=ln3_b, b_in=b_in,
                   attn_sinks=attn_sinks, hgrn_lb_logits=hgrn_lb_logits, hgrn_norm_g=hgrn_norm_g)
    small_m = dict(ln1_g=m_ln1_g, ln1_b=m_ln1_b, ln2_g=m_ln2_g, ln2_b=m_ln2_b, ln3_g=m_ln3_g, ln3_b=m_ln3_b,
                   b_in=m_b_in, attn_sinks=m_attn_sinks, hgrn_lb_logits=m_hgrn_lb_logits, hgrn_norm_g=m_hgrn_norm_g)
    small_v = dict(ln1_g=v_ln1_g, ln1_b=v_ln1_b, ln2_g=v_ln2_g, ln2_b=v_ln2_b, ln3_g=v_ln3_g, ln3_b=v_ln3_b,
                   b_in=v_b_in, attn_sinks=v_attn_sinks, hgrn_lb_logits=v_hgrn_lb_logits, hgrn_norm_g=v_hgrn_norm_g)
    outs, names = _small_step("small_step", _small_reduce("small_reduce", packed), small_w, small_m, small_v)
    loss = outs[0][0, 0]
    for i, n in enumerate(names):
        res[n] = tuple(outs[1 + 4 * i:5 + 4 * i])

    order = ["ln1_g", "ln1_b", "ffn1_w1", "ffn1_w3", "ffn1_w2", "ln2_g", "ln2_b", "w_in", "b_in", "attn_sinks",
             "hgrn_lb_logits", "hgrn_norm_g", "w_proj_attn", "w_proj_hgrn", "w_out", "ln3_g", "ln3_b",
             "ffn2_w1", "ffn2_w3", "ffn2_w2"]
    return (loss, gx[None], *[res[n][0] for n in order], *[res[n][1] for n in order],
            *[res[n][2] for n in order], *[res[n][3] for n in order])
```

```python
import jax
import jax.numpy as jnp
from jax import lax
from jax.experimental import pallas as pl
from jax.experimental.pallas import tpu as pltpu

F32 = jnp.float32
BF16 = jnp.bfloat16

NDEV = 8
D = 1024
DFF = 2816
RF = DFF // NDEV
DIN = 7680
RIN = DIN // NDEV
RP = D // NDEV
N_Q_HEADS = 16
N_KV_HEADS = 4
HEAD_DIM = 64
ATTN_BLOCK = 128
ROPE_THETA = 500000.0
ROPE_DIM = HEAD_DIM // 4
HGRN_HEADS = 8
HGRN_DK = 128
HGRN_CHUNK = 64
ALPHA = 2.0 ** 0.25
LN_EPS = 1e-5
RMS_EPS = 1e-6
NEG_INF = -1e30
ADAM_LR = 0.001
ADAM_B1 = 0.9
ADAM_B2 = 0.999
ADAM_EPS = 1e-08
ADAM_WD = 0.01
ADAM_STEP = 10

QKV_W = 1536
HG_W = 4096
GATE_W = 2048
VMEM_LIMIT = 60 * 2 ** 20
FFN_CHUNKS = 2

MESH = pl.DeviceIdType.MESH
HBM_SPEC = pl.BlockSpec(memory_space=pltpu.HBM)


def _params(*sem):
    return pltpu.CompilerParams(dimension_semantics=sem, vmem_limit_bytes=VMEM_LIMIT)


def _dot(a, b):
    return jnp.dot(a, b, preferred_element_type=F32)


def _dot_nt(a, b):
    return lax.dot_general(a, b, (((1,), (1,)), ((), ())), preferred_element_type=F32)


def _dot_tn(a, b):
    return lax.dot_general(a, b, (((0,), (0,)), ((), ())), preferred_element_type=F32)


def _sigmoid(x):
    return 0.5 * jnp.tanh(0.5 * x) + 0.5


def _ln_fwd(z):
    mu = jnp.mean(z, axis=-1, keepdims=True)
    zc = z - mu
    var = jnp.mean(zc * zc, axis=-1, keepdims=True)
    r = lax.rsqrt(var + LN_EPS)
    return zc * r, r


def _ln_bwd(dy, xh, r, g):
    dxh = dy * g
    m1 = jnp.mean(dxh, axis=-1, keepdims=True)
    m2 = jnp.mean(dxh * xh, axis=-1, keepdims=True)
    dz = r * (dxh - m1 - xh * m2)
    return dz, jnp.sum(dy * xh, axis=0, keepdims=True), jnp.sum(dy, axis=0, keepdims=True)


def _load_rows(wall_ref, pieces, sem):
    copies = []
    for dst, off, r in pieces:
        for k in range(NDEV):
            c = pltpu.make_async_copy(wall_ref.at[k, pl.ds(off, r), :], dst.at[pl.ds(k * r, r), :], sem.at[len(copies)])
            c.start()
            copies.append(c)
    for c in copies:
        c.wait()


class _Exchange:
    def __init__(self, inputs, out_shape, scratch, begin, middle, end):
        self.inputs, self.out_shape, self.scratch = inputs, out_shape, scratch
        self.begin, self.middle, self.end = begin, middle, end


def _gather_exchange(shard):
    rows, cols = shard.shape

    def ops(ins, outs, scr):
        (x_ref,), (out_ref,), (send_sems, recv_sems, local_sem) = ins, outs, scr
        x, y, c = lax.axis_index("x"), lax.axis_index("y"), lax.axis_index("c")
        me, sibling = (x, y, c), (x, y, 1 - c)
        chips = [(1 - x, y), (x, 1 - y), (1 - x, 1 - y)]

        def slot(px, py, pc):
            return out_ref.at[4 * px + 2 * py + pc]

        def copy(k, block, to, src=None):
            return pltpu.make_async_remote_copy(
                src_ref=slot(*block) if src is None else src, dst_ref=slot(*block),
                send_sem=send_sems.at[k], recv_sem=recv_sems.at[k], device_id=to, device_id_type=MESH)

        mine = lambda: pltpu.make_async_copy(x_ref, slot(*me), local_sem)
        first = lambda: [copy(0, me, sibling, src=x_ref)] + [
            copy(1 + j, me, (*chip, c), src=x_ref) for j, chip in enumerate(chips)]
        passed = lambda: [copy(4 + j, (*chip, c), sibling) for j, chip in enumerate(chips)]
        return c, me, sibling, chips, copy, mine, first, passed

    def begin(*refs):
        _, _, _, _, _, mine, first, _ = ops(*refs)
        mine().start()
        for cp in first():
            cp.start()

    def middle(*refs):
        c, me, _, chips, copy, _, _, passed = ops(*refs)
        for (j, chip), fwd in zip(enumerate(chips), passed()):
            copy(1 + j, (*chip, c), me).wait_recv()
            fwd.start()

    def end(*refs):
        c, me, sibling, chips, copy, mine, first, passed = ops(*refs)
        copy(0, sibling, me).wait_recv()
        for j, chip in enumerate(chips):
            copy(4 + j, (*chip, 1 - c), me).wait_recv()
        for cp in first() + passed():
            cp.wait_send()
        mine().wait()

    return _Exchange([shard], [jax.ShapeDtypeStruct((NDEV, rows, cols), shard.dtype)],
                     [pltpu.SemaphoreType.DMA((7,)), pltpu.SemaphoreType.DMA((7,)), pltpu.SemaphoreType.DMA],
                     begin, middle, end)


def _grad_exchange(grads, bases, rows):
    n = len(grads)

    def copies(g_refs, out_refs, scr):
        send_sems, recv_sems, local_sems = scr
        x, y, c = lax.axis_index("x"), lax.axis_index("y"), lax.axis_index("c")
        me = 4 * x + 2 * y + c
        out = []
        for i in range(n):
            r = rows[i]
            src = lambda k: g_refs[i].at[pl.ds(pl.multiple_of(bases[i] + k * r, 16), r), :]
            out.append(pltpu.make_async_copy(src(me), out_refs[i].at[me], local_sems.at[i]))
            for j in range(1, NDEV):
                px, py, pc = x ^ (j >> 2), y ^ ((j >> 1) & 1), c ^ (j & 1)
                out.append(pltpu.make_async_remote_copy(
                    src_ref=src(4 * px + 2 * py + pc), dst_ref=out_refs[i].at[me],
                    send_sem=send_sems.at[i, j - 1], recv_sem=recv_sems.at[i, j - 1],
                    device_id=(px, py, pc), device_id_type=MESH))
        return out

    def begin(*refs):
        for cp in copies(*refs):
            cp.start()

    def end(*refs):
        for cp in copies(*refs):
            cp.wait()

    return _Exchange(list(grads), [jax.ShapeDtypeStruct((NDEV, r, D), g.dtype) for g, r in zip(grads, rows)],
                     [pltpu.SemaphoreType.DMA((n, NDEV - 1)), pltpu.SemaphoreType.DMA((n, NDEV - 1)),
                      pltpu.SemaphoreType.DMA((n,))], begin, None, end)


def _exchange_call(name, ex):
    ni, no = len(ex.inputs), len(ex.out_shape)

    def body(*refs):
        parts = (refs[:ni], refs[ni:ni + no], refs[ni + no:])
        ex.begin(*parts)
        if ex.middle is not None:
            ex.middle(*parts)
        ex.end(*parts)

    return pl.pallas_call(body, name=name, out_shape=ex.out_shape, in_specs=[HBM_SPEC] * ni, out_specs=[HBM_SPEC] * no,
                          scratch_shapes=ex.scratch)(*ex.inputs)


def _call(body, *, name, grid, in_specs, out_specs, out_shape, scratch_shapes, ins, carry=None):
    sem = ("arbitrary",) * len(grid)
    if carry is None:
        outs = pl.pallas_call(body, name=name, grid=grid, in_specs=in_specs, out_specs=out_specs, out_shape=out_shape,
                              scratch_shapes=scratch_shapes, compiler_params=_params(*sem))(*ins)
        return outs, None
    n_in, n_out, n_scr = len(ins), len(out_shape), len(scratch_shapes)
    ci, co = len(carry.inputs), len(carry.out_shape)
    total = 1
    for g in grid:
        total *= g

    def wrapped(*refs):
        own_in, ex_in = refs[:n_in], refs[n_in:n_in + ci]
        o0 = n_in + ci
        own_out, ex_out = refs[o0:o0 + n_out], refs[o0 + n_out:o0 + n_out + co]
        s0 = o0 + n_out + co
        own_scr, ex_scr = refs[s0:s0 + n_scr], refs[s0 + n_scr:]
        step = pl.program_id(0)
        for d in range(1, len(grid)):
            step = step * grid[d] + pl.program_id(d)
        parts = (ex_in, ex_out, ex_scr)
        pl.when(step == 0)(lambda: carry.begin(*parts))
        body(*own_in, *own_out, *own_scr)
        if carry.middle is not None:
            pl.when(step == total // 2)(lambda: carry.middle(*parts))
        pl.when(step == total - 1)(lambda: carry.end(*parts))

    outs = pl.pallas_call(
        wrapped, name=name, grid=grid, in_specs=list(in_specs) + [HBM_SPEC] * ci,
        out_specs=list(out_specs) + [HBM_SPEC] * co, out_shape=list(out_shape) + list(carry.out_shape),
        scratch_shapes=list(scratch_shapes) + list(carry.scratch), compiler_params=_params(*sem),
    )(*ins, *carry.inputs)
    return outs[:n_out], outs[n_out:]


def _ffn_fwd(name, xin, affine, wall, offs, tm, loss=None, carry=None):
    T = xin.shape[0]
    nt = T // tm
    fc = DFF // FFN_CHUNKS

    def body(*refs):
        it = iter(refs)
        x_ref = next(it)
        if affine is not None:
            g_ref, b_ref = next(it), next(it)
        wall_ref = next(it)
        if loss is not None:
            go_ref, bo_ref, tgt_ref = next(it), next(it), next(it)
        xh_ref, r_ref, a_ref, b2_ref, yb_ref = (next(it) for _ in range(5))
        if loss is not None:
            loss_ref = next(it)
        w1, w3, w2, sem = (next(it) for _ in range(4))

        @pl.when(pl.program_id(0) == 0)
        def _():
            _load_rows(wall_ref, [(w1, offs[0], RF), (w3, offs[1], RF), (w2, offs[2], RF)], sem)
            if loss is not None:
                loss_ref[...] = jnp.zeros_like(loss_ref)

        y = x_ref[...]
        if affine is not None:
            y = y * g_ref[...] + b_ref[...]
        yb = y.astype(BF16)
        yb_ref[...] = yb
        f = jnp.zeros((tm, D), F32)
        for ci in range(DFF // fc):
            cs = slice(ci * fc, (ci + 1) * fc)
            a = _dot_nt(yb, w1[cs, :]).astype(BF16)
            b = _dot_nt(yb, w3[cs, :]).astype(BF16)
            a_ref[:, cs] = a
            b2_ref[:, cs] = b
            af, bf = a.astype(F32), b.astype(F32)
            u = (af * _sigmoid(af) * bf).astype(BF16)
            f = f + _dot(u, w2[cs, :])
        xh, r = _ln_fwd(ALPHA * y + 0.5 * f)
        xh_ref[...] = xh
        r_ref[...] = jnp.broadcast_to(r, (tm, 128))
        if loss is not None:
            e = xh * go_ref[...] + bo_ref[...] - tgt_ref[...]
            loss_ref[...] += jnp.sum(e * e) * (0.5 / D)

    row = lambda w: pl.BlockSpec((tm, w), lambda i: (i, 0))
    vec = pl.BlockSpec((1, D), lambda i: (0, 0))
    ins, in_specs = [xin], [row(D)]
    if affine is not None:
        ins += list(affine)
        in_specs += [vec, vec]
    ins.append(wall)
    in_specs.append(HBM_SPEC)
    if loss is not None:
        ins += list(loss)
        in_specs += [vec, vec, row(D)]
    out_shape = [jax.ShapeDtypeStruct((T, D), F32), jax.ShapeDtypeStruct((T, 128), F32),
                 jax.ShapeDtypeStruct((T, DFF), BF16), jax.ShapeDtypeStruct((T, DFF), BF16),
                 jax.ShapeDtypeStruct((T, D), BF16)]
    out_specs = [row(D), row(128), row(DFF), row(DFF), row(D)]
    if loss is not None:
        out_shape.append(jax.ShapeDtypeStruct((8, 128), F32))
        out_specs.append(pl.BlockSpec((8, 128), lambda i: (0, 0)))
    return _call(body, name=name, grid=(nt,), in_specs=in_specs, out_specs=out_specs, out_shape=out_shape,
                 scratch_shapes=[pltpu.VMEM((DFF, D), BF16)] * 3 + [pltpu.SemaphoreType.DMA((3 * NDEV,))],
                 ins=ins, carry=carry)


def _ffn_bwd(name, dy_src, xh, r, g, a, b, wall, offs, tm, carry=None):
    T = xh.shape[0]
    nt = T // tm
    fc = DFF // FFN_CHUNKS
    from_loss = dy_src[0] == "loss"

    def body(*refs):
        it = iter(refs)
        if from_loss:
            bo_ref, tgt_ref = next(it), next(it)
        else:
            dy_ref = next(it)
        xh_ref, r_ref, g_ref, a_ref, b_ref, wall_ref = (next(it) for _ in range(6))
        dyin_ref, dab_ref, u_ref, df_ref, dg_ref, db_ref = (next(it) for _ in range(6))
        w1, w3, w2, sem = (next(it) for _ in range(4))

        @pl.when(pl.program_id(0) == 0)
        def _():
            _load_rows(wall_ref, [(w1, offs[0], RF), (w3, offs[1], RF), (w2, offs[2], RF)], sem)
            dg_ref[...] = jnp.zeros_like(dg_ref)
            db_ref[...] = jnp.zeros_like(db_ref)

        xhv = xh_ref[...]
        gv = g_ref[...]
        if from_loss:
            dy = (xhv * gv + bo_ref[...] - tgt_ref[...]) * (1.0 / D)
        else:
            dy = dy_ref[...]
        dz, dgp, dbp = _ln_bwd(dy, xhv, r_ref[:, :1], gv)
        dg_ref[...] += dgp
        db_ref[...] += dbp
        df = (0.5 * dz).astype(BF16)
        df_ref[...] = df
        dx = ALPHA * dz
        for ci in range(DFF // fc):
            cs = slice(ci * fc, (ci + 1) * fc)
            du = _dot_nt(df, w2[cs, :])
            af, bf = a_ref[:, cs].astype(F32), b_ref[:, cs].astype(F32)
            s = _sigmoid(af)
            sl = af * s
            u_ref[:, cs] = (sl * bf).astype(BF16)
            da = (du * bf * (s * (1.0 + af * (1.0 - s)))).astype(BF16)
            dbb = (du * sl).astype(BF16)
            dab_ref[:, cs] = da
            dab_ref[:, DFF + ci * fc:DFF + (ci + 1) * fc] = dbb
            dx = dx + _dot(da, w1[cs, :]) + _dot(dbb, w3[cs, :])
        dyin_ref[...] = dx

    row = lambda w: pl.BlockSpec((tm, w), lambda i: (i, 0))
    vec = pl.BlockSpec((1, D), lambda i: (0, 0))
    if from_loss:
        ins, in_specs = [dy_src[1], dy_src[2]], [vec, row(D)]
    else:
        ins, in_specs = [dy_src[1]], [row(D)]
    ins += [xh, r, g, a, b, wall]
    in_specs += [row(D), row(128), vec, row(DFF), row(DFF), HBM_SPEC]
    return _call(
        body, name=name, grid=(nt,), in_specs=in_specs,
        out_specs=[row(D), row(2 * DFF), row(DFF), row(D), vec, vec],
        out_shape=[jax.ShapeDtypeStruct((T, D), F32), jax.ShapeDtypeStruct((T, 2 * DFF), BF16),
                   jax.ShapeDtypeStruct((T, DFF), BF16), jax.ShapeDtypeStruct((T, D), BF16),
                   jax.ShapeDtypeStruct((1, D), F32), jax.ShapeDtypeStruct((1, D), F32)],
        scratch_shapes=[pltpu.VMEM((DFF, D), BF16)] * 3 + [pltpu.SemaphoreType.DMA((3 * NDEV,))],
        ins=ins, carry=carry)


def _wgrad(name, a, b, bn, tk, col0=0, ncols=None, carry=None):
    T = a.shape[0]
    N = a.shape[1] if ncols is None else ncols
    nk = T // tk
    c0 = col0 // bn

    def body(a_ref, b_ref, o_ref, acc):
        k = pl.program_id(1)

        @pl.when(k == 0)
        def _():
            acc[...] = jnp.zeros_like(acc)

        acc[...] += _dot_tn(a_ref[...], b_ref[...])

        @pl.when(k == nk - 1)
        def _():
            o_ref[...] = acc[...].astype(BF16)

    (out,), ex = _call(
        body, name=name, grid=(N // bn, nk),
        in_specs=[pl.BlockSpec((tk, bn), lambda n, k: (k, n + c0)), pl.BlockSpec((tk, D), lambda n, k: (k, 0))],
        out_specs=[pl.BlockSpec((bn, D), lambda n, k: (n, 0))],
        out_shape=[jax.ShapeDtypeStruct((N, D), BF16)],
        scratch_shapes=[pltpu.VMEM((bn, D), F32)], ins=[a, b], carry=carry)
    return out, ex


def _rope_tables(T):
    pos = jnp.arange(T, dtype=F32)
    inv_freq = ROPE_THETA ** (-jnp.arange(0, ROPE_DIM, 2, dtype=F32) / ROPE_DIM)
    half = ROPE_DIM // 2
    ch = jnp.arange(128) % HEAD_DIM
    ang = pos[:, None] * inv_freq[ch % half][None, :]
    cos, sin = jnp.cos(ang), jnp.sin(ang)
    first, second = (ch < half)[None, :], ((ch >= half) & (ch < ROPE_DIM))[None, :]
    return (jnp.where(first | second, cos, 1.0), jnp.where(first, -sin, 0.0), jnp.where(second, sin, 0.0))


def _rope(t, c, s1, s2):
    n = t.shape[1] // 128
    ct, s1t, s2t = (jnp.tile(v, (1, n)) for v in (c, s1, s2))
    w = t.shape[1]
    return t * ct + pltpu.roll(t, w - 8, 1) * s1t + pltpu.roll(t, 8, 1) * s2t


def _rope_t(dr, c, s1, s2):
    n = dr.shape[1] // 128
    ct, s1t, s2t = (jnp.tile(v, (1, n)) for v in (c, s1, s2))
    w = dr.shape[1]
    return dr * ct + pltpu.roll(dr * s1t, 8, 1) + pltpu.roll(dr * s2t, w - 8, 1)


_Q, _K, _V = (0, 1024), (1024, 256), (1280, 256)
_HG = (1536, HG_W)
_GATES = (5632, GATE_W)


def _inproj_fwd(name, xh, g, b, wall, b_in, ropes, tm, carry=None):
    T = xh.shape[0]

    def body(xh_ref, g_ref, b_ref, wall_ref, bin_ref, c_ref, s1_ref, s2_ref,
             qkv_ref, kt_ref, vt_ref, hg_ref, gate_ref, yb_ref, w, sem):
        @pl.when(pl.program_id(0) == 0)
        def _():
            _load_rows(wall_ref, [(w, 0, RIN)], sem)

        yb = (xh_ref[...] * g_ref[...] + b_ref[...]).astype(BF16)
        yb_ref[...] = yb
        c, s1, s2 = c_ref[...], s1_ref[...], s2_ref[...]

        def piece(start, width):
            return _dot_nt(yb, w[start:start + width, :]) + bin_ref[:, start:start + width]

        k = _rope(piece(*_K), c, s1, s2)
        v = piece(*_V)
        qkv_ref[:, 0:1024] = _rope(piece(*_Q), c, s1, s2).astype(BF16)
        qkv_ref[:, 1024:1280] = k.astype(BF16)
        qkv_ref[:, 1280:1536] = v.astype(BF16)
        kt_ref[...] = k.T.astype(BF16)
        vt_ref[...] = v.T.astype(BF16)
        for j in range(4):
            hg_ref[:, 1024 * j:1024 * (j + 1)] = piece(_HG[0] + 1024 * j, 1024)
        for j in range(2):
            gate_ref[:, 1024 * j:1024 * (j + 1)] = piece(_GATES[0] + 1024 * j, 1024)

    row = lambda wd: pl.BlockSpec((tm, wd), lambda i: (i, 0))
    vec = lambda wd: pl.BlockSpec((1, wd), lambda i: (0, 0))
    colt = pl.BlockSpec((256, tm), lambda i: (0, i))
    return _call(
        body, name=name, grid=(T // tm,),
        in_specs=[row(D), vec(D), vec(D), HBM_SPEC, vec(DIN), row(128), row(128), row(128)],
        out_specs=[row(QKV_W), colt, colt, row(HG_W), row(GATE_W), row(D)],
        out_shape=[jax.ShapeDtypeStruct((T, QKV_W), BF16), jax.ShapeDtypeStruct((256, T), BF16),
                   jax.ShapeDtypeStruct((256, T), BF16), jax.ShapeDtypeStruct((T, HG_W), F32),
                   jax.ShapeDtypeStruct((T, GATE_W), F32), jax.ShapeDtypeStruct((T, D), BF16)],
        scratch_shapes=[pltpu.VMEM((DIN, D), BF16), pltpu.SemaphoreType.DMA((NDEV,))],
        ins=[xh, g, b, wall, b_in, *ropes], carry=carry)


def _inproj_bwd(name, dq, dk, dv, dhg, dgates, dz2, wall, ropes, tm):
    T = dq.shape[0]

    def body(dq_ref, dk_ref, dv_ref, d0, d1, d2, d3, dga_ref, dgh_ref, dz_ref, wall_ref, c_ref, s1_ref, s2_ref,
             dy_ref, dproj_ref, dbin_ref, w, sem):
        @pl.when(pl.program_id(0) == 0)
        def _():
            _load_rows(wall_ref, [(w, 0, RIN)], sem)
            dbin_ref[...] = jnp.zeros_like(dbin_ref)

        c, s1, s2 = c_ref[...], s1_ref[...], s2_ref[...]
        acc = ALPHA * dz_ref[...]
        pieces = [(_Q[0], _rope_t(dq_ref[...], c, s1, s2)), (_K[0], _rope_t(dk_ref[...], c, s1, s2)),
                  (_V[0], dv_ref[...])]
        pieces += [(_HG[0] + 1024 * j, r[...]) for j, r in enumerate((d0, d1, d2, d3))]
        pieces += [(_GATES[0], dga_ref[...]), (_GATES[0] + 1024, dgh_ref[...])]
        for start, val in pieces:
            width = val.shape[1]
            dbin_ref[:, start:start + width] += jnp.sum(val, axis=0, keepdims=True)
            vb = val.astype(BF16)
            dproj_ref[:, start:start + width] = vb
            acc = acc + _dot(vb, w[start:start + width, :])
        dy_ref[...] = acc

    row = lambda wd: pl.BlockSpec((tm, wd), lambda i: (i, 0))
    return pl.pallas_call(
        body, name=name, grid=(T // tm,),
        in_specs=[row(D), row(256), row(256)] + [row(D)] * 4 + [row(D), row(D), row(D), HBM_SPEC] + [row(128)] * 3,
        out_specs=[row(D), row(DIN), pl.BlockSpec((1, DIN), lambda i: (0, 0))],
        out_shape=[jax.ShapeDtypeStruct((T, D), F32), jax.ShapeDtypeStruct((T, DIN), BF16),
                   jax.ShapeDtypeStruct((1, DIN), F32)],
        scratch_shapes=[pltpu.VMEM((DIN, D), BF16), pltpu.SemaphoreType.DMA((NDEV,))],
        compiler_params=_params("arbitrary"),
    )(dq, dk, dv, *dhg, *dgates, dz2, wall, *ropes)


def _halves(t):
    lane = lax.broadcasted_iota(jnp.int32, t.shape, 1)
    low = lane < HEAD_DIM
    sw = pltpu.roll(t, HEAD_DIM, 1)
    zero = jnp.zeros_like(t)
    h0 = (jnp.where(low, t, zero), jnp.where(low, zero, sw))
    h1 = (jnp.where(low, sw, zero), jnp.where(low, zero, t))
    return h0, h1


def _lane_stack(p_ref, c_ref, gp):
    sl = slice(128 * gp, 128 * (gp + 1))
    hp, hc = _halves(p_ref[:, sl].astype(F32)), _halves(c_ref[:, sl].astype(F32))
    return [jnp.concatenate([hp[gg][0], hc[gg][0], hp[gg][1], hc[gg][1]], axis=0).astype(BF16) for gg in range(2)]


def _row_stack(tp_ref, tc_ref, g):
    band = jnp.concatenate([tp_ref[64 * g:64 * (g + 1), :], tc_ref[64 * g:64 * (g + 1), :]], axis=1)
    z = jnp.zeros_like(band)
    return [jnp.concatenate([band, z], axis=0), jnp.concatenate([z, band], axis=0)]


def _fold(prevm, t4, hh):
    return jnp.where(prevm, t4[256 * hh:256 * hh + 128, :], t4[256 * hh + 128:256 * hh + 256, :])


def _unfold(prevm, t):
    return jnp.concatenate([jnp.where(prevm, t, 0.0), jnp.where(prevm, 0.0, t)], axis=0).astype(BF16)


def _attn_softmax(s, kill, sink):
    s = jnp.where(kill, NEG_INF, s)
    m = jnp.maximum(jnp.max(s, axis=0, keepdims=True), sink)
    p = jnp.exp(s - m)
    es = jnp.exp(sink - m)
    denom = jnp.sum(p, axis=0, keepdims=True) + es
    return p / denom, es / denom


def _attn_masks(first):
    row = lax.broadcasted_iota(jnp.int32, (ATTN_BLOCK, 2 * ATTN_BLOCK), 0)
    lane = lax.broadcasted_iota(jnp.int32, (ATTN_BLOCK, 2 * ATTN_BLOCK), 1)
    prevm = row > lane % ATTN_BLOCK
    return prevm, jnp.logical_and(first, prevm)


def _pair_rows(ref, g):
    return jnp.concatenate([ref[:, 256 * g:256 * g + 128], ref[:, 256 * g + 128:256 * (g + 1)]], axis=0)


def _pair_sinks(sink_ref, g, hh):
    h0, h1 = 4 * g + hh, 4 * g + 2 + hh
    return jnp.concatenate([jnp.broadcast_to(sink_ref[:, h0:h0 + 1], (1, ATTN_BLOCK)),
                            jnp.broadcast_to(sink_ref[:, h1:h1 + 1], (1, ATTN_BLOCK))], axis=1)


def _attn_fwd(name, qkv, vt, sinks):
    T = qkv.shape[0]
    nb = T // ATTN_BLOCK
    scale = HEAD_DIM ** -0.5

    def body(q_ref, kp_ref, kc_ref, vtp_ref, vtc_ref, sink_ref, o_ref):
        prevm, kill = _attn_masks(pl.program_id(0) == 0)
        kst = _lane_stack(kp_ref, kc_ref, 0) + _lane_stack(kp_ref, kc_ref, 1)
        vts = [_row_stack(vtp_ref, vtc_ref, g) for g in range(N_KV_HEADS)]
        s8 = [_dot_nt(kst[g], _pair_rows(q_ref, g)) * scale for g in range(N_KV_HEADS)]
        pu = [[_unfold(prevm, _attn_softmax(_fold(prevm, s8[g], hh), kill, _pair_sinks(sink_ref, g, hh))[0])
               for hh in range(2)] for g in range(N_KV_HEADS)]
        for g in range(N_KV_HEADS):
            ot = _dot(vts[g][0], pu[g][0]) + _dot(vts[g][1], pu[g][1])
            o_ref[:, 256 * g:256 * g + 128] = ot[:, :ATTN_BLOCK].T.astype(BF16)
            o_ref[:, 256 * g + 128:256 * (g + 1)] = ot[:, ATTN_BLOCK:].T.astype(BF16)

    prev = lambda i: jnp.maximum(i - 1, 0)
    return pl.pallas_call(
        body, name=name, grid=(nb,),
        in_specs=[pl.BlockSpec((ATTN_BLOCK, D), lambda i: (i, 0)),
                  pl.BlockSpec((ATTN_BLOCK, 256), lambda i: (prev(i), 4)),
                  pl.BlockSpec((ATTN_BLOCK, 256), lambda i: (i, 4)),
                  pl.BlockSpec((256, ATTN_BLOCK), lambda i: (0, prev(i))),
                  pl.BlockSpec((256, ATTN_BLOCK), lambda i: (0, i)),
                  pl.BlockSpec((1, N_Q_HEADS), lambda i: (0, 0))],
        out_specs=pl.BlockSpec((ATTN_BLOCK, D), lambda i: (i, 0)),
        out_shape=jax.ShapeDtypeStruct((T, D), BF16),
        compiler_params=_params("arbitrary"),
    )(qkv, qkv, qkv, vt, vt, sinks)


def _attn_bwd(name, qkv, kt, sinks, dya, carry=None):
    T = qkv.shape[0]
    nb = T // ATTN_BLOCK
    scale = HEAD_DIM ** -0.5

    def body(q_ref, kp_ref, kc_ref, vp_ref, vc_ref, ktp_ref, ktc_ref, sink_ref, do_ref,
             dq_ref, dk_ref, dv_ref, ds_ref):
        i = pl.program_id(0)

        @pl.when(i == 0)
        def _():
            ds_ref[...] = jnp.zeros_like(ds_ref)

        prevm, kill = _attn_masks(i == 0)
        kst = _lane_stack(kp_ref, kc_ref, 0) + _lane_stack(kp_ref, kc_ref, 1)
        vst = _lane_stack(vp_ref, vc_ref, 0) + _lane_stack(vp_ref, vc_ref, 1)
        kts = [_row_stack(ktp_ref, ktc_ref, g) for g in range(N_KV_HEADS)]
        lane = lax.broadcasted_iota(jnp.int32, (2 * ATTN_BLOCK, 128), 1)
        low = lane < HEAD_DIM
        slane = lax.broadcasted_iota(jnp.int32, (1, 128), 1)
        dkz = [jnp.zeros((2 * ATTN_BLOCK, 128), F32) for _ in range(N_KV_HEADS)]
        dvz = [jnp.zeros((2 * ATTN_BLOCK, 128), F32) for _ in range(N_KV_HEADS)]
        dsink = jnp.zeros((1, 128), F32)
        groups = range(N_KV_HEADS)
        qcat = [_pair_rows(q_ref, g) for g in groups]
        docat = [_pair_rows(do_ref, g) for g in groups]
        s8 = [_dot_nt(kst[g], qcat[g]) * scale for g in groups]
        dp8 = [_dot_nt(vst[g], docat[g]) for g in groups]
        ds_u = [[None, None] for _ in groups]
        p_u = [[None, None] for _ in groups]
        for g in groups:
            for hh in range(2):
                pn, ps = _attn_softmax(_fold(prevm, s8[g], hh), kill, _pair_sinks(sink_ref, g, hh))
                dp = _fold(prevm, dp8[g], hh)
                delta = jnp.sum(pn * dp, axis=0, keepdims=True)
                sd = ps * delta
                dsink = dsink + jnp.where(slane == 4 * g + hh, -jnp.sum(sd[:, :ATTN_BLOCK]), 0.0) \
                    + jnp.where(slane == 4 * g + 2 + hh, -jnp.sum(sd[:, ATTN_BLOCK:]), 0.0)
                ds_u[g][hh] = _unfold(prevm, pn * (dp - delta) * scale)
                p_u[g][hh] = _unfold(prevm, pn)
        for g in groups:
            dqt = _dot(kts[g][0], ds_u[g][0]) + _dot(kts[g][1], ds_u[g][1])
            dq_ref[:, 256 * g:256 * g + 128] = dqt[:, :ATTN_BLOCK].T
            dq_ref[:, 256 * g + 128:256 * (g + 1)] = dqt[:, ATTN_BLOCK:].T
            for hh in range(2):
                own = low if hh == 0 else jnp.logical_not(low)
                dk_h = jnp.where(own, _dot(ds_u[g][hh], qcat[g]), 0.0)
                dv_h = jnp.where(own, _dot(p_u[g][hh], docat[g]), 0.0)
                if hh != g % 2:
                    dk_h = pltpu.roll(dk_h, HEAD_DIM, 1)
                    dv_h = pltpu.roll(dv_h, HEAD_DIM, 1)
                dkz[g] = dkz[g] + dk_h
                dvz[g] = dvz[g] + dv_h
        ds_ref[...] += dsink
        cur = pl.ds(pl.multiple_of(i * ATTN_BLOCK, ATTN_BLOCK), ATTN_BLOCK)
        prv = pl.ds(pl.multiple_of(jnp.maximum(i - 1, 0) * ATTN_BLOCK, ATTN_BLOCK), ATTN_BLOCK)
        for gp in range(N_KV_HEADS // 2):
            cols = slice(128 * gp, 128 * (gp + 1))
            dkb = dkz[2 * gp] + dkz[2 * gp + 1]
            dvb = dvz[2 * gp] + dvz[2 * gp + 1]
            dk_ref[cur, cols] = dkb[ATTN_BLOCK:, :]
            dv_ref[cur, cols] = dvb[ATTN_BLOCK:, :]

            @pl.when(i > 0)
            def _():
                dk_ref[prv, cols] += dkb[:ATTN_BLOCK, :]
                dv_ref[prv, cols] += dvb[:ATTN_BLOCK, :]

    prev = lambda i: jnp.maximum(i - 1, 0)
    whole = lambda w: pl.BlockSpec((T, w), lambda i: (0, 0))
    return _call(
        body, name=name, grid=(nb,), scratch_shapes=[], ins=[qkv, qkv, qkv, qkv, qkv, kt, kt, sinks, dya], carry=carry,
        in_specs=[pl.BlockSpec((ATTN_BLOCK, D), lambda i: (i, 0)),
                  pl.BlockSpec((ATTN_BLOCK, 256), lambda i: (prev(i), 4)),
                  pl.BlockSpec((ATTN_BLOCK, 256), lambda i: (i, 4)),
                  pl.BlockSpec((ATTN_BLOCK, 256), lambda i: (prev(i), 5)),
                  pl.BlockSpec((ATTN_BLOCK, 256), lambda i: (i, 5)),
                  pl.BlockSpec((256, ATTN_BLOCK), lambda i: (0, prev(i))),
                  pl.BlockSpec((256, ATTN_BLOCK), lambda i: (0, i)),
                  pl.BlockSpec((1, N_Q_HEADS), lambda i: (0, 0)),
                  pl.BlockSpec((ATTN_BLOCK, D), lambda i: (i, 0))],
        out_specs=[pl.BlockSpec((ATTN_BLOCK, D), lambda i: (i, 0)), whole(256), whole(256),
                   pl.BlockSpec((1, 128), lambda i: (0, 0))],
        out_shape=[jax.ShapeDtypeStruct((T, D), F32), jax.ShapeDtypeStruct((T, 256), F32),
                   jax.ShapeDtypeStruct((T, 256), F32), jax.ShapeDtypeStruct((1, 128), F32)])


HG_SUB = 256
HG_TILE = 1024
CPS = HG_SUB // HGRN_CHUNK
HG_HEADS_PER_STEP = 4


def _chunk_sum(tri, x):
    w = x.shape[1]
    h1 = x.astype(BF16)
    h2 = (x - h1.astype(F32)).astype(BF16)
    r = _dot(tri, jnp.concatenate([h1, h2], axis=1))
    return r[:, :w] + r[:, w:]


def _chunk_tri():
    ri = lax.broadcasted_iota(jnp.int32, (HG_SUB, HG_SUB), 0)
    ci = lax.broadcasted_iota(jnp.int32, (HG_SUB, HG_SUB), 1)
    return jnp.where((ri >= ci) & (ri // HGRN_CHUNK == ci // HGRN_CHUNK), 1.0, 0.0).astype(BF16)


def _chunks(t):
    return [t[HGRN_CHUNK * c:HGRN_CHUNK * (c + 1), :] for c in range(CPS)]


def _lower_bound(lbl_ref):
    l0, l1 = lbl_ref[0:1, :], lbl_ref[1:2, :]
    m = jnp.maximum(l0, l1)
    e0, e1 = jnp.exp(l0 - m), jnp.exp(l1 - m)
    return e0 / (e0 + e1)


def _heads(t):
    return [t[:, 128 * h:128 * (h + 1)] for h in range(t.shape[1] // 128)]


def _per_head(fn, *wide):
    return jnp.concatenate([fn(*parts) for parts in zip(*[_heads(t) for t in wide])], axis=1)


def _hgrn_sub(fl, qh, vv, lb, tril_b):
    w = fl.shape[1]
    sg = _sigmoid(fl)
    f = lb + (1.0 - lb) * sg
    k = 1.0 - f
    gc = _chunk_sum(tril_b, jnp.log(f))
    gl_c = [t[HGRN_CHUNK - 1:HGRN_CHUNK, :] for t in _chunks(gc)]
    gl = jnp.concatenate([jnp.broadcast_to(g, (HGRN_CHUNK, w)) for g in gl_c], axis=0)
    sq = _sigmoid(qh)
    eg = jnp.exp(gc)
    eng = jnp.exp(-gc)
    elg = jnp.exp(gl - gc)
    qd = qh * sq * eg
    ki = k * eng
    ke = k * elg
    qd_b, ki_b, ke_b, v_b = (t.astype(BF16) for t in (qd, ki, ke, vv))
    am_b = [jnp.where(tril_b > 0, _dot_nt(qq, kk), 0.0).astype(BF16) for qq, kk in zip(_heads(qd_b), _heads(ki_b))]
    return dict(sg=sg, f=f, sq=sq, eg=eg, eng=eng, elg=elg, qd=qd, ki=ki, ke=ke, egl=[jnp.exp(g) for g in gl_c],
                am_b=am_b, qd_b=qd_b, ki_b=ki_b, ke_b=ke_b, v_b=v_b)


def _hgrn_out(q, st_b):
    outs = []
    for h, (qd_h, v_h) in enumerate(zip(_heads(q["qd_b"]), _heads(q["v_b"]))):
        inter = [_dot_nt(qc, s) for qc, s in zip(_chunks(qd_h), st_b[h])]
        outs.append(_dot(q["am_b"][h], v_h) + jnp.concatenate(inter, axis=0))
    return jnp.concatenate(outs, axis=1)


def _head_rms(o):
    return _per_head(lambda t: jnp.broadcast_to(
        lax.rsqrt(jnp.mean(t * t, axis=-1, keepdims=True) + RMS_EPS), t.shape), o)


def _hgrn_fwd(name, hg, lb_logits, norm_g, tri, th):
    T = hg.shape[0]
    nc = th // HGRN_CHUNK
    hps = HG_HEADS_PER_STEP

    def body(fl_ref, qh_ref, v_ref, og_ref, lbl_ref, ng_ref, tri_ref, y_ref, ss_ref, st):
        @pl.when(pl.program_id(1) == 0)
        def _():
            st[...] = jnp.zeros_like(st)

        lbs = _lower_bound(lbl_ref)
        ng = jnp.tile(ng_ref[...], (1, hps))

        def sub(si, carry):
            rows = pl.ds(pl.multiple_of(si * HG_SUB, HG_SUB), HG_SUB)
            q = _hgrn_sub(fl_ref[rows, :], qh_ref[rows, :], v_ref[rows, :], lbs, tri_ref[...])
            v_hc = [_chunks(t) for t in _heads(q["v_b"])]
            k_hc = [_chunks(t) for t in _heads(q["ke_b"])]
            s = [st[h] for h in range(hps)]
            st_b = [[] for _ in range(hps)]
            for c in range(CPS):
                egl = _heads(q["egl"][c])
                for h in range(hps):
                    ss_ref[h, si * CPS + c] = s[h]
                    st_b[h].append(s[h].astype(BF16))
                    s[h] = s[h] * egl[h] + _dot_tn(v_hc[h][c], k_hc[h][c])
            for h in range(hps):
                st[h] = s[h]
            o = _hgrn_out(q, st_b)
            og = og_ref[rows, :]
            y_ref[rows, :] = (o * _head_rms(o) * ng * (og * _sigmoid(og))).astype(BF16)
            return carry

        lax.fori_loop(0, th // HG_SUB, sub, 0)

    wd = 128 * hps
    col = lambda j: pl.BlockSpec((th, wd), lambda h, t: (t, (8 // hps) * j + h))
    return pl.pallas_call(
        body, name=name, grid=(HGRN_HEADS // hps, T // th),
        in_specs=[col(0), col(1), col(2), col(3), pl.BlockSpec((2, wd), lambda h, t: (0, h)),
                  pl.BlockSpec((1, 128), lambda h, t: (0, 0)), pl.BlockSpec((HG_SUB, HG_SUB), lambda h, t: (0, 0))],
        out_specs=[pl.BlockSpec((th, wd), lambda h, t: (t, h)),
                   pl.BlockSpec((hps, nc, 128, 128), lambda h, t: (h, t, 0, 0))],
        out_shape=[jax.ShapeDtypeStruct((T, D), BF16),
                   jax.ShapeDtypeStruct((HGRN_HEADS, T // HGRN_CHUNK, 128, 128), F32)],
        scratch_shapes=[pltpu.VMEM((hps, 128, 128), F32)],
        compiler_params=_params("parallel", "arbitrary"),
    )(hg, hg, hg, hg, lb_logits, norm_g, tri)


def _hgrn_bwd(name, hg, lb_logits, norm_g, tri, tri_t, sstart, dyh, th, carry=None):
    T = hg.shape[0]
    nc = th // HGRN_CHUNK
    nt = T // th
    hps = HG_HEADS_PER_STEP
    wd = 128 * hps

    def body(fl_ref, qh_ref, v_ref, og_ref, lbl_ref, ng_ref, tri_ref, trit_ref, ss_ref, dy_ref,
             dfl_ref, dqh_ref, dv_ref, dog_ref, dlb_ref, dng_ref, dst):
        @pl.when(pl.program_id(1) == 0)
        def _():
            dst[...] = jnp.zeros_like(dst)
            dlb_ref[...] = jnp.zeros_like(dlb_ref)
            dng_ref[...] = jnp.zeros_like(dng_ref)

        lb = _lower_bound(lbl_ref)
        ng = jnp.tile(ng_ref[...], (1, hps))
        last = lax.broadcasted_iota(jnp.int32, (HGRN_CHUNK, wd), 0) == HGRN_CHUNK - 1
        nsub = th // HG_SUB
        cat0 = lambda parts: jnp.concatenate(parts, axis=0)

        def sub(step, carry):
            si = nsub - 1 - step
            rows = pl.ds(pl.multiple_of(si * HG_SUB, HG_SUB), HG_SUB)
            qh, og, dy = qh_ref[rows, :], og_ref[rows, :], dy_ref[rows, :]
            tril_b = tri_ref[...]
            q = _hgrn_sub(fl_ref[rows, :], qh, v_ref[rows, :], lb, tril_b)
            s_in = [[ss_ref[h, si * CPS + c] for c in range(CPS)] for h in range(hps)]
            st_b = [[s.astype(BF16) for s in row] for row in s_in]
            o = _hgrn_out(q, st_b)
            rr = _head_rms(o)
            oh = o * rr
            sog = _sigmoid(og)
            dog_ref[rows, :] = dy * oh * ng * (sog * (1.0 + og * (1.0 - sog)))
            don = dy * (og * sog)
            dng_w = jnp.sum(don * oh, axis=0, keepdims=True)
            dd = don * ng
            mean_h = _per_head(lambda t: jnp.broadcast_to(jnp.mean(t, axis=-1, keepdims=True), t.shape), dd * oh)
            do_b = (rr * (dd - oh * mean_h)).astype(BF16)
            do_h, qd_h, ki_h, ke_h, v_h = (_heads(t) for t in (do_b, q["qd_b"], q["ki_b"], q["ke_b"], q["v_b"]))
            da_b = [jnp.where(tril_b > 0, _dot_nt(do_h[h], v_h[h]), 0.0).astype(BF16) for h in range(hps)]
            do_c, qd_c, ke_c, v_c = ([_chunks(t) for t in hs] for hs in (do_h, qd_h, ke_h, v_h))
            dsp = [[None] * CPS for _ in range(hps)]
            dgl_dec = [[None] * CPS for _ in range(hps)]
            d = [dst[h] for h in range(hps)]
            for c in reversed(range(CPS)):
                egl = _heads(q["egl"][c])
                for h in range(hps):
                    dsp[h][c] = d[h]
                    dgl_dec[h][c] = jnp.sum(d[h] * s_in[h][c], axis=0, keepdims=True) * egl[h]
                    d[h] = d[h] * egl[h] + _dot_tn(do_c[h][c], qd_c[h][c])
            for h in range(hps):
                dst[h] = d[h]
                dng_ref[h] += dng_w[:, 128 * h:128 * (h + 1)]
            dsp_b = [[t.astype(BF16) for t in row] for row in dsp]
            dv_ref[rows, :] = jnp.concatenate(
                [_dot_tn(q["am_b"][h], do_h[h]) + cat0([_dot_nt(ke_c[h][c], dsp_b[h][c]) for c in range(CPS)])
                 for h in range(hps)], axis=1)
            dqd = jnp.concatenate(
                [_dot(da_b[h], ki_h[h]) + cat0([_dot(do_c[h][c], st_b[h][c]) for c in range(CPS)])
                 for h in range(hps)], axis=1)
            dki = jnp.concatenate([_dot_tn(da_b[h], qd_h[h]) for h in range(hps)], axis=1)
            dke = jnp.concatenate([cat0([_dot(v_c[h][c], dsp_b[h][c]) for c in range(CPS)]) for h in range(hps)], axis=1)
            dkk = dke * q["ke"]
            dgl = [jnp.sum(t, axis=0, keepdims=True) + jnp.concatenate([dgl_dec[h][c] for h in range(hps)], axis=1)
                   for c, t in enumerate(_chunks(dkk))]
            dgc = dqd * q["qd"] - dki * q["ki"] - dkk + cat0([jnp.where(last, g, 0.0) for g in dgl])
            dlf = _chunk_sum(trit_ref[...], dgc)
            df = dlf / q["f"] - (dki * q["eng"] + dke * q["elg"])
            sg = q["sg"]
            dlb_ref[...] += jnp.sum(df * (1.0 - sg), axis=0, keepdims=True)
            dfl_ref[rows, :] = df * (1.0 - lb) * sg * (1.0 - sg)
            sq = q["sq"]
            dqh_ref[rows, :] = dqd * q["eg"] * (sq * (1.0 + qh * (1.0 - sq)))
            return carry

        lax.fori_loop(0, nsub, sub, 0)
    col = lambda j: pl.BlockSpec((th, wd), lambda h, t: (nt - 1 - t, (8 // hps) * j + h))
    out = pl.BlockSpec((th, wd), lambda h, t: (nt - 1 - t, h))
    tri_spec = pl.BlockSpec((HG_SUB, HG_SUB), lambda h, t: (0, 0))
    return _call(
        body, name=name, grid=(HGRN_HEADS // hps, nt),
        in_specs=[col(0), col(1), col(2), col(3), pl.BlockSpec((2, wd), lambda h, t: (0, h)),
                  pl.BlockSpec((1, 128), lambda h, t: (0, 0)), tri_spec, tri_spec,
                  pl.BlockSpec((hps, nc, 128, 128), lambda h, t: (h, nt - 1 - t, 0, 0)), out],
        out_specs=[out, out, out, out, pl.BlockSpec((1, wd), lambda h, t: (0, h)),
                   pl.BlockSpec((hps, 1, 128), lambda h, t: (h, 0, 0))],
        out_shape=[jax.ShapeDtypeStruct((T, D), F32)] * 4 + [jax.ShapeDtypeStruct((1, D), F32),
                                                              jax.ShapeDtypeStruct((HGRN_HEADS, 1, 128), F32)],
        scratch_shapes=[pltpu.VMEM((hps, 128, 128), F32)],
        ins=[hg, hg, hg, hg, lb_logits, norm_g, tri, tri_t, sstart, dyh], carry=carry)


def _mix_fwd(name, ya, yh, gates, xh1, g1, b1, wall, tm):
    T = ya.shape[0]

    def body(ya_ref, yh_ref, ga_ref, gh_ref, xh_ref, g_ref, b_ref, wall_ref,
             xo_ref, r_ref, pa_ref, ph_ref, mg_ref, wpa, wph, wo, sem):
        @pl.when(pl.program_id(0) == 0)
        def _():
            _load_rows(wall_ref, [(wpa, RIN, RP), (wph, RIN + RP, RP), (wo, RIN + 2 * RP, RP)], sem)

        pa = _dot(ya_ref[...], wpa[...]).astype(BF16)
        ph = _dot(yh_ref[...], wph[...]).astype(BF16)
        pa_ref[...] = pa
        ph_ref[...] = ph
        mg = (_sigmoid(ga_ref[...]) * pa.astype(F32) + _sigmoid(gh_ref[...]) * ph.astype(F32)).astype(BF16)
        mg_ref[...] = mg
        y1 = xh_ref[...] * g_ref[...] + b_ref[...]
        xh2, r = _ln_fwd(ALPHA * y1 + _dot(mg, wo[...]))
        xo_ref[...] = xh2
        r_ref[...] = jnp.broadcast_to(r, (tm, 128))

    row = lambda w: pl.BlockSpec((tm, w), lambda i: (i, 0))
    vec = pl.BlockSpec((1, D), lambda i: (0, 0))
    return pl.pallas_call(
        body, name=name, grid=(T // tm,),
        in_specs=[row(D), row(D), pl.BlockSpec((tm, D), lambda i: (i, 0)), pl.BlockSpec((tm, D), lambda i: (i, 1)),
                  row(D), vec, vec, HBM_SPEC],
        out_specs=[row(D), row(128), row(D), row(D), row(D)],
        out_shape=[jax.ShapeDtypeStruct((T, D), F32), jax.ShapeDtypeStruct((T, 128), F32)]
        + [jax.ShapeDtypeStruct((T, D), BF16)] * 3,
        scratch_shapes=[pltpu.VMEM((D, D), BF16)] * 3 + [pltpu.SemaphoreType.DMA((3 * NDEV,))],
        compiler_params=_params("arbitrary"),
    )(ya, yh, gates, gates, xh1, g1, b1, wall)


def _mix_bwd(name, dy2, xh2, r2, g2, gates, pa, ph, wall, tm):
    T = dy2.shape[0]

    def body(dy_ref, xh_ref, r_ref, g_ref, ga_ref, gh_ref, pa_ref, ph_ref, wall_ref,
             dz_ref, dmix_ref, dpa_ref, dph_ref, dga_ref, dgh_ref, dya_ref, dyh_ref, dg_ref, db_ref,
             wpa, wph, wo, sem):
        @pl.when(pl.program_id(0) == 0)
        def _():
            _load_rows(wall_ref, [(wpa, RIN, RP), (wph, RIN + RP, RP), (wo, RIN + 2 * RP, RP)], sem)
            dg_ref[...] = jnp.zeros_like(dg_ref)
            db_ref[...] = jnp.zeros_like(db_ref)

        dz, dgp, dbp = _ln_bwd(dy_ref[...], xh_ref[...], r_ref[:, :1], g_ref[...])
        dg_ref[...] += dgp
        db_ref[...] += dbp
        dz_ref[...] = dz
        dmix = dz.astype(BF16)
        dmix_ref[...] = dmix
        dmg = _dot_nt(dmix, wo[...])
        sa, sh = _sigmoid(ga_ref[...]), _sigmoid(gh_ref[...])
        dga_ref[...] = dmg * pa_ref[...].astype(F32) * sa * (1.0 - sa)
        dgh_ref[...] = dmg * ph_ref[...].astype(F32) * sh * (1.0 - sh)
        dpa = (dmg * sa).astype(BF16)
        dph = (dmg * sh).astype(BF16)
        dpa_ref[...] = dpa
        dph_ref[...] = dph
        dya_ref[...] = _dot_nt(dpa, wpa[...]).astype(BF16)
        dyh_ref[...] = _dot_nt(dph, wph[...])

    row = lambda w: pl.BlockSpec((tm, w), lambda i: (i, 0))
    vec = pl.BlockSpec((1, D), lambda i: (0, 0))
    return pl.pallas_call(
        body, name=name, grid=(T // tm,),
        in_specs=[row(D), row(D), row(128), vec, pl.BlockSpec((tm, D), lambda i: (i, 0)),
                  pl.BlockSpec((tm, D), lambda i: (i, 1)), row(D), row(D), HBM_SPEC],
        out_specs=[row(D)] * 8 + [vec, vec],
        out_shape=[jax.ShapeDtypeStruct((T, D), F32)] + [jax.ShapeDtypeStruct((T, D), BF16)] * 3
        + [jax.ShapeDtypeStruct((T, D), F32)] * 2 + [jax.ShapeDtypeStruct((T, D), BF16), jax.ShapeDtypeStruct((T, D), F32)]
        + [jax.ShapeDtypeStruct((1, D), F32)] * 2,
        scratch_shapes=[pltpu.VMEM((D, D), BF16)] * 3 + [pltpu.SemaphoreType.DMA((3 * NDEV,))],
        compiler_params=_params("arbitrary"),
    )(dy2, xh2, r2, g2, gates, gates, pa, ph, wall)


def _adam(w, g, m, v):
    m = ADAM_B1 * m + (1.0 - ADAM_B1) * g
    v = ADAM_B2 * v + (1.0 - ADAM_B2) * (g * g)
    m_hat = m / (1.0 - ADAM_B1 ** ADAM_STEP)
    v_hat = v / (1.0 - ADAM_B2 ** ADAM_STEP)
    delta = -ADAM_LR * (m_hat / (jnp.sqrt(v_hat) + ADAM_EPS) + ADAM_WD * w)
    return delta, m, v


def _grad_step(name, recv, w, m, v):
    _, rows, _ = recv.shape
    cb = 256

    def body(r_ref, w_ref, m_ref, v_ref, g_ref, d_ref, mo_ref, vo_ref):
        acc = r_ref[0].astype(F32)
        for k in range(1, NDEV):
            acc = acc + r_ref[k].astype(F32)
        g_ref[...] = acc
        d, mm, vv = _adam(w_ref[...], acc, m_ref[...], v_ref[...])
        d_ref[...] = d
        mo_ref[...] = mm
        vo_ref[...] = vv

    blk = pl.BlockSpec((rows, cb), lambda j: (0, j))
    return pl.pallas_call(
        body, name=name, grid=(D // cb,),
        in_specs=[pl.BlockSpec((NDEV, rows, cb), lambda j: (0, 0, j)), blk, blk, blk],
        out_specs=[blk] * 4, out_shape=[jax.ShapeDtypeStruct(w.shape, F32)] * 4,
        compiler_params=_params("parallel"),
    )(recv, w, m, v)


_SMALL = [("ln1_g", D), ("ln1_b", D), ("ln2_g", D), ("ln2_b", D), ("ln3_g", D), ("ln3_b", D),
          ("b_in", DIN), ("lb", D), ("attn_sinks", 128), ("hgrn_norm_g", 128), ("loss", 128)]
_SMALL_OFF = {}
_o = 0
for _n, _w in _SMALL:
    _SMALL_OFF[_n] = (_o, _w)
    _o += _w
PACK = _o


def _small_reduce(name, packed):
    def body(p_ref, o_ref, buf, send_sems, recv_sems):
        x, y, c = lax.axis_index("x"), lax.axis_index("y"), lax.axis_index("c")
        me = 4 * x + 2 * y + c
        buf[me] = p_ref[...]
        copies = []
        for j in range(1, NDEV):
            px, py, pc = x ^ (j >> 2), y ^ ((j >> 1) & 1), c ^ (j & 1)
            cp = pltpu.make_async_remote_copy(
                src_ref=p_ref, dst_ref=buf.at[me], send_sem=send_sems.at[j - 1], recv_sem=recv_sems.at[j - 1],
                device_id=(px, py, pc), device_id_type=MESH)
            cp.start()
            copies.append(cp)
        for cp in copies:
            cp.wait()
        tot = buf[0]
        for k in range(1, NDEV):
            tot = tot + buf[k]
        o_ref[...] = tot

    return pl.pallas_call(
        body, name=name, out_shape=jax.ShapeDtypeStruct((1, PACK), F32),
        scratch_shapes=[pltpu.VMEM((NDEV, 1, PACK), F32), pltpu.SemaphoreType.DMA((NDEV - 1,)),
                        pltpu.SemaphoreType.DMA((NDEV - 1,))],
        compiler_params=pltpu.CompilerParams(vmem_limit_bytes=VMEM_LIMIT),
    )(packed)


def _small_step(name, total, small_w, small_m, small_v):
    names = ["ln1_g", "ln1_b", "ln2_g", "ln2_b", "ln3_g", "ln3_b", "b_in", "attn_sinks", "hgrn_lb_logits", "hgrn_norm_g"]
    np_ = len(names)

    def body(*refs):
        t_ref = refs[0]
        w_refs = refs[1:1 + np_]
        m_refs = refs[1 + np_:1 + 2 * np_]
        v_refs = refs[1 + 2 * np_:1 + 3 * np_]
        loss_ref, outs = refs[1 + 3 * np_], refs[2 + 3 * np_:]

        def part(n):
            o, w = _SMALL_OFF[n]
            return t_ref[:, o:o + w]

        loss_ref[...] = part("loss")
        for i, n in enumerate(names):
            w = w_refs[i][...]
            if n == "hgrn_lb_logits":
                m_ = jnp.maximum(w[0:1, :], w[1:2, :])
                e0, e1 = jnp.exp(w[0:1, :] - m_), jnp.exp(w[1:2, :] - m_)
                p0 = e0 / (e0 + e1)
                t = p0 * (1.0 - p0) * part("lb")
                g = jnp.concatenate([t, -t], axis=0)
            elif n == "attn_sinks":
                g = part(n)[:, :N_Q_HEADS]
            else:
                g = part(n)
            d, mm, vv = _adam(w, g, m_refs[i][...], v_refs[i][...])
            outs[4 * i][...] = g
            outs[4 * i + 1][...] = d
            outs[4 * i + 2][...] = mm
            outs[4 * i + 3][...] = vv

    out_shape = [jax.ShapeDtypeStruct((1, 128), F32)]
    for n in names:
        out_shape += [jax.ShapeDtypeStruct(small_w[n].shape, F32)] * 4
    return pl.pallas_call(
        body, name=name, out_shape=out_shape,
        compiler_params=pltpu.CompilerParams(vmem_limit_bytes=VMEM_LIMIT),
    )(total, *[small_w[n] for n in names], *[small_m[n] for n in names], *[small_v[n] for n in names]), names


def _tile(T, pref):
    return min(T, pref)


def kernel(x, ln1_g, ln1_b, ffn1_w1, ffn1_w3, ffn1_w2, ln2_g, ln2_b, w_in, b_in, attn_sinks, hgrn_lb_logits, hgrn_norm_g, w_proj_attn, w_proj_hgrn, w_out, ln3_g, ln3_b, ffn2_w1, ffn2_w3, ffn2_w2, loss_target, m_ln1_g, m_ln1_b, m_ffn1_w1, m_ffn1_w3, m_ffn1_w2, m_ln2_g, m_ln2_b, m_w_in, m_b_in, m_attn_sinks, m_hgrn_lb_logits, m_hgrn_norm_g, m_w_proj_attn, m_w_proj_hgrn, m_w_out, m_ln3_g, m_ln3_b, m_ffn2_w1, m_ffn2_w3, m_ffn2_w2, v_ln1_g, v_ln1_b, v_ffn1_w1, v_ffn1_w3, v_ffn1_w2, v_ln2_g, v_ln2_b, v_w_in, v_b_in, v_attn_sinks, v_hgrn_lb_logits, v_hgrn_norm_g, v_w_proj_attn, v_w_proj_hgrn, v_w_out, v_ln3_g, v_ln3_b, v_ffn2_w1, v_ffn2_w3, v_ffn2_w2):
    T = x.shape[1]
    xs = x[0]
    tgt = loss_target[0]
    tm = _tile(T, 256)
    tm2 = _tile(T, 512)
    tk = _tile(T, 2048)
    th = _tile(T, HG_TILE)

    t_bf = lambda w: w[0].T.astype(BF16)
    n_bf = lambda w: w[0].astype(BF16)
    ffn_shard = lambda w1, w3, w2: jnp.concatenate([t_bf(w1), t_bf(w3), n_bf(w2)], axis=0)
    mix_shard = jnp.concatenate([t_bf(w_in), n_bf(w_proj_attn), n_bf(w_proj_hgrn), n_bf(w_out)], axis=0)
    (ffn1_all,) = _exchange_call("gather_ffn1", _gather_exchange(ffn_shard(ffn1_w1, ffn1_w3, ffn1_w2)))
    ffn_offs = (0, RF, 2 * RF)
    ropes = _rope_tables(T)
    tri = _chunk_tri()
    tri_t = tri.T

    (xh1, r1, a1, b1, xb0), (mix_all,) = _ffn_fwd("ffn1_fwd", xs, None, ffn1_all, ffn_offs, tm,
                                                  carry=_gather_exchange(mix_shard))
    (qkv, kt, vt, hg, gates, y1b), (ffn2_all,) = _inproj_fwd(
        "inproj_fwd", xh1, ln1_g, ln1_b, mix_all, b_in, ropes, tm2,
        carry=_gather_exchange(ffn_shard(ffn2_w1, ffn2_w3, ffn2_w2)))
    ya = _attn_fwd("attn_fwd", qkv, vt, attn_sinks)
    yh, sstart = _hgrn_fwd("hgrn_fwd", hg, hgrn_lb_logits, hgrn_norm_g, tri, th)
    xh2, r2, pa, ph, merged = _mix_fwd("mix_fwd", ya, yh, gates, xh1, ln1_g, ln1_b, mix_all, tm2)
    (xh3, r3, a2, b2, y2b, loss_part), _ = _ffn_fwd("ffn2_fwd", xh2, (ln2_g, ln2_b), ffn2_all, ffn_offs, tm,
                                                    loss=(ln3_g, ln3_b, tgt))

    (dy2, dab2, u2, df2, dg3, db3), _ = _ffn_bwd("ffn2_bwd", ("loss", ln3_b, tgt), xh3, r3, ln3_g, a2, b2, ffn2_all,
                                                 ffn_offs, tm)
    g_ffn2_13, _ = _wgrad("wgrad_ffn2_w13", dab2, y2b, DFF // 2, tk)
    g_ffn2_2, _ = _wgrad("wgrad_ffn2_w2", u2, df2, DFF // 2, tk)
    dz2, dmix, dpa, dph, dga, dgh, dya, dyh, dg2, db2 = _mix_bwd("mix_bwd", dy2, xh2, r2, ln2_g, gates, pa, ph,
                                                                 mix_all, tm2)
    g_wo, _ = _wgrad("wgrad_w_out", merged, dmix, D, tk)
    g_pa, _ = _wgrad("wgrad_w_pa", ya, dpa, D, tk)
    g_ph, _ = _wgrad("wgrad_w_ph", yh, dph, D, tk)
    recv = {}
    (dq, dk, dv, dsink), (recv["ffn2_w1"], recv["ffn2_w3"], recv["ffn2_w2"]) = _attn_bwd(
        "attn_bwd", qkv, kt, attn_sinks, dya,
        carry=_grad_exchange([g_ffn2_13, g_ffn2_13, g_ffn2_2], [0, DFF, 0], [RF, RF, RF]))
    (dfl, dqh, dih, dog, dlb, dng), (recv["w_proj_attn"], recv["w_proj_hgrn"], recv["w_out"]) = _hgrn_bwd(
        "hgrn_bwd", hg, hgrn_lb_logits, hgrn_norm_g, tri, tri_t, sstart, dyh, th,
        carry=_grad_exchange([g_pa, g_ph, g_wo], [0, 0, 0], [RP, RP, RP]))
    dy1, dproj, dbin = _inproj_bwd("inproj_bwd", dq, dk, dv, (dfl, dqh, dih, dog), (dga, dgh), dz2, mix_all, ropes, tm)
    g_win, _ = _wgrad("wgrad_w_in", dproj, y1b, DIN // 4, tk)
    (gx, dab1, u1, df1, dg1, db1), (recv["w_in"],) = _ffn_bwd(
        "ffn1_bwd", ("dy", dy1), xh1, r1, ln1_g, a1, b1, ffn1_all, ffn_offs, tm,
        carry=_grad_exchange([g_win], [0], [RIN]))
    g_ffn1_2, _ = _wgrad("wgrad_ffn1_w2", u1, df1, DFF // 2, tk)
    g_ffn1_1, (recv["ffn1_w2"],) = _wgrad("wgrad_ffn1_w1", dab1, xb0, DFF // 2, tk, 0, DFF,
                                          carry=_grad_exchange([g_ffn1_2], [0], [RF]))
    g_ffn1_3, (recv["ffn1_w1"],) = _wgrad("wgrad_ffn1_w3", dab1, xb0, DFF // 2, tk, DFF, DFF,
                                          carry=_grad_exchange([g_ffn1_1], [0], [RF]))
    (recv["ffn1_w3"],) = _exchange_call("exchange_ffn1_w3", _grad_exchange([g_ffn1_3], [0], [RF]))

    big = [("ffn1_w1", ffn1_w1, m_ffn1_w1, v_ffn1_w1, True), ("ffn1_w3", ffn1_w3, m_ffn1_w3, v_ffn1_w3, True),
           ("ffn1_w2", ffn1_w2, m_ffn1_w2, v_ffn1_w2, False), ("w_in", w_in, m_w_in, v_w_in, True),
           ("w_proj_attn", w_proj_attn, m_w_proj_attn, v_w_proj_attn, False),
           ("w_proj_hgrn", w_proj_hgrn, m_w_proj_hgrn, v_w_proj_hgrn, False),
           ("w_out", w_out, m_w_out, v_w_out, False),
           ("ffn2_w1", ffn2_w1, m_ffn2_w1, v_ffn2_w1, True), ("ffn2_w3", ffn2_w3, m_ffn2_w3, v_ffn2_w3, True),
           ("ffn2_w2", ffn2_w2, m_ffn2_w2, v_ffn2_w2, False)]
    res = {}
    for n, w, m, v, transposed in big:
        view = (lambda t: t[0].T) if transposed else (lambda t: t[0])
        back = (lambda t: t.T[None]) if transposed else (lambda t: t[None])
        res[n] = tuple(back(t) for t in _grad_step("step_" + n, recv[n], view(w), view(m), view(v)))

    parts = {"ln1_g": dg1, "ln1_b": db1, "ln2_g": dg2, "ln2_b": db2, "ln3_g": dg3, "ln3_b": db3, "b_in": dbin,
             "lb": dlb, "attn_sinks": dsink, "hgrn_norm_g": jnp.sum(dng, axis=0), "loss": loss_part[0:1, :]}
    packed = jnp.concatenate([parts[n] for n, _ in _SMALL], axis=1)
    small_w = dict(ln1_g=ln1_g, ln1_b=ln1_b, ln2_g=ln2_g, ln2_b=ln2_b, ln3_g=ln3_g, ln3_b=ln3_b, b_in=b_in,
                   attn_sinks=attn_sinks, hgrn_lb_logits=hgrn_lb_logits, hgrn_norm_g=hgrn_norm_g)
    small_m = dict(ln1_g=m_ln1_g, ln1_b=m_ln1_b, ln2_g=m_ln2_g, ln2_b=m_ln2_b, ln3_g=m_ln3_g, ln3_b=m_ln3_b,
                   b_in=m_b_in, attn_sinks=m_attn_sinks, hgrn_lb_logits=m_hgrn_lb_logits, hgrn_norm_g=m_hgrn_norm_g)
    small_v = dict(ln1_g=v_ln1_g, ln1_b=v_ln1_b, ln2_g=v_ln2_g, ln2_b=v_ln2_b, ln3_g=v_ln3_g, ln3_b=v_ln3_b,
                   b_in=v_b_in, attn_sinks=v_attn_sinks, hgrn_lb_logits=v_hgrn_lb_logits, hgrn_norm_g=v_hgrn_norm_g)
    outs, names = _small_step("small_step", _small_reduce("small_reduce", packed), small_w, small_m, small_v)
    loss = outs[0][0, 0]
    for i, n in enumerate(names):
        res[n] = tuple(outs[1 + 4 * i:5 + 4 * i])

    order = ["ln1_g", "ln1_b", "ffn1_w1", "ffn1_w3", "ffn1_w2", "ln2_g", "ln2_b", "w_in", "b_in", "attn_sinks",
             "hgrn_lb_logits", "hgrn_norm_g", "w_proj_attn", "w_proj_hgrn", "w_out", "ln3_g", "ln3_b",
             "ffn2_w1", "ffn2_w3", "ffn2_w2"]
    return (loss, gx[None], *[res[n][0] for n in order], *[res[n][1] for n in order],
            *[res[n][2] for n in order], *[res[n][3] for n in order])
```

```python
import jax
import jax.numpy as jnp
from jax import lax
from jax.experimental import pallas as pl
from jax.experimental.pallas import tpu as pltpu

F32 = jnp.float32
BF16 = jnp.bfloat16

NDEV = 8
D = 1024
DFF = 2816
RF = DFF // NDEV
DIN = 7680
RIN = DIN // NDEV
RP = D // NDEV
N_Q_HEADS = 16
N_KV_HEADS = 4
HEAD_DIM = 64
ATTN_BLOCK = 128
ROPE_THETA = 500000.0
ROPE_DIM = HEAD_DIM // 4
HGRN_HEADS = 8
HGRN_DK = 128
HGRN_CHUNK = 64
ALPHA = 2.0 ** 0.25
LN_EPS = 1e-5
RMS_EPS = 1e-6
NEG_INF = -1e30
ADAM_LR = 0.001
ADAM_B1 = 0.9
ADAM_B2 = 0.999
ADAM_EPS = 1e-08
ADAM_WD = 0.01
ADAM_STEP = 10

QKV_W = 1536
HG_W = 4096
GATE_W = 2048
VMEM_LIMIT = 60 * 2 ** 20
FFN_CHUNKS = 1

MESH = pl.DeviceIdType.MESH
HBM_SPEC = pl.BlockSpec(memory_space=pltpu.HBM)


def _params(*sem):
    return pltpu.CompilerParams(dimension_semantics=sem, vmem_limit_bytes=VMEM_LIMIT)


def _dot(a, b):
    return jnp.dot(a, b, preferred_element_type=F32)


def _dot_nt(a, b):
    return lax.dot_general(a, b, (((1,), (1,)), ((), ())), preferred_element_type=F32)


def _dot_tn(a, b):
    return lax.dot_general(a, b, (((0,), (0,)), ((), ())), preferred_element_type=F32)


def _sigmoid(x):
    return 0.5 * jnp.tanh(0.5 * x) + 0.5


def _ln_fwd(z):
    mu = jnp.mean(z, axis=-1, keepdims=True)
    zc = z - mu
    var = jnp.mean(zc * zc, axis=-1, keepdims=True)
    r = lax.rsqrt(var + LN_EPS)
    return zc * r, r


def _ln_bwd(dy, xh, r, g):
    dxh = dy * g
    m1 = jnp.mean(dxh, axis=-1, keepdims=True)
    m2 = jnp.mean(dxh * xh, axis=-1, keepdims=True)
    dz = r * (dxh - m1 - xh * m2)
    return dz, jnp.sum(dy * xh, axis=0, keepdims=True), jnp.sum(dy, axis=0, keepdims=True)


def _load_rows(wall_ref, pieces, sem):
    copies = []
    for dst, off, r in pieces:
        for k in range(NDEV):
            c = pltpu.make_async_copy(wall_ref.at[k, pl.ds(off, r), :], dst.at[pl.ds(k * r, r), :], sem.at[len(copies)])
            c.start()
            copies.append(c)
    for c in copies:
        c.wait()


class _Exchange:
    def __init__(self, inputs, out_shape, scratch, begin, middle, end):
        self.inputs, self.out_shape, self.scratch = inputs, out_shape, scratch
        self.begin, self.middle, self.end = begin, middle, end


def _gather_exchange(shard):
    rows, cols = shard.shape

    def ops(ins, outs, scr):
        (x_ref,), (out_ref,), (send_sems, recv_sems, local_sem) = ins, outs, scr
        x, y, c = lax.axis_index("x"), lax.axis_index("y"), lax.axis_index("c")
        me, sibling = (x, y, c), (x, y, 1 - c)
        chips = [(1 - x, y), (x, 1 - y), (1 - x, 1 - y)]

        def slot(px, py, pc):
            return out_ref.at[4 * px + 2 * py + pc]

        def copy(k, block, to, src=None):
            return pltpu.make_async_remote_copy(
                src_ref=slot(*block) if src is None else src, dst_ref=slot(*block),
                send_sem=send_sems.at[k], recv_sem=recv_sems.at[k], device_id=to, device_id_type=MESH)

        mine = lambda: pltpu.make_async_copy(x_ref, slot(*me), local_sem)
        first = lambda: [copy(0, me, sibling, src=x_ref)] + [
            copy(1 + j, me, (*chip, c), src=x_ref) for j, chip in enumerate(chips)]
        passed = lambda: [copy(4 + j, (*chip, c), sibling) for j, chip in enumerate(chips)]
        return c, me, sibling, chips, copy, mine, first, passed

    def begin(*refs):
        _, _, _, _, _, mine, first, _ = ops(*refs)
        mine().start()
        for cp in first():
            cp.start()

    def middle(*refs):
        c, me, _, chips, copy, _, _, passed = ops(*refs)
        for (j, chip), fwd in zip(enumerate(chips), passed()):
            copy(1 + j, (*chip, c), me).wait_recv()
            fwd.start()

    def end(*refs):
        c, me, sibling, chips, copy, mine, first, passed = ops(*refs)
        copy(0, sibling, me).wait_recv()
        for j, chip in enumerate(chips):
            copy(4 + j, (*chip, 1 - c), me).wait_recv()
        for cp in first() + passed():
            cp.wait_send()
        mine().wait()

    return _Exchange([shard], [jax.ShapeDtypeStruct((NDEV, rows, cols), shard.dtype)],
                     [pltpu.SemaphoreType.DMA((7,)), pltpu.SemaphoreType.DMA((7,)), pltpu.SemaphoreType.DMA],
                     begin, middle, end)


def _grad_exchange(grads, bases, rows):
    n = len(grads)

    def copies(g_refs, out_refs, scr):
        send_sems, recv_sems, local_sems = scr
        x, y, c = lax.axis_index("x"), lax.axis_index("y"), lax.axis_index("c")
        me = 4 * x + 2 * y + c
        out = []
        for i in range(n):
            r = rows[i]
            src = lambda k: g_refs[i].at[pl.ds(pl.multiple_of(bases[i] + k * r, 16), r), :]
            out.append(pltpu.make_async_copy(src(me), out_refs[i].at[me], local_sems.at[i]))
            for j in range(1, NDEV):
                px, py, pc = x ^ (j >> 2), y ^ ((j >> 1) & 1), c ^ (j & 1)
                out.append(pltpu.make_async_remote_copy(
                    src_ref=src(4 * px + 2 * py + pc), dst_ref=out_refs[i].at[me],
                    send_sem=send_sems.at[i, j - 1], recv_sem=recv_sems.at[i, j - 1],
                    device_id=(px, py, pc), device_id_type=MESH))
        return out

    def begin(*refs):
        for cp in copies(*refs):
            cp.start()

    def end(*refs):
        for cp in copies(*refs):
            cp.wait()

    return _Exchange(list(grads), [jax.ShapeDtypeStruct((NDEV, r, D), g.dtype) for g, r in zip(grads, rows)],
                     [pltpu.SemaphoreType.DMA((n, NDEV - 1)), pltpu.SemaphoreType.DMA((n, NDEV - 1)),
                      pltpu.SemaphoreType.DMA((n,))], begin, None, end)


def _exchange_call(name, ex):
    ni, no = len(ex.inputs), len(ex.out_shape)

    def body(*refs):
        parts = (refs[:ni], refs[ni:ni + no], refs[ni + no:])
        ex.begin(*parts)
        if ex.middle is not None:
            ex.middle(*parts)
        ex.end(*parts)

    return pl.pallas_call(body, name=name, out_shape=ex.out_shape, in_specs=[HBM_SPEC] * ni, out_specs=[HBM_SPEC] * no,
                          scratch_shapes=ex.scratch)(*ex.inputs)


def _call(body, *, name, grid, in_specs, out_specs, out_shape, scratch_shapes, ins, carry=None):
    sem = ("arbitrary",) * len(grid)
    if carry is None:
        outs = pl.pallas_call(body, name=name, grid=grid, in_specs=in_specs, out_specs=out_specs, out_shape=out_shape,
                              scratch_shapes=scratch_shapes, compiler_params=_params(*sem))(*ins)
        return outs, None
    n_in, n_out, n_scr = len(ins), len(out_shape), len(scratch_shapes)
    ci, co = len(carry.inputs), len(carry.out_shape)
    total = 1
    for g in grid:
        total *= g

    def wrapped(*refs):
        own_in, ex_in = refs[:n_in], refs[n_in:n_in + ci]
        o0 = n_in + ci
        own_out, ex_out = refs[o0:o0 + n_out], refs[o0 + n_out:o0 + n_out + co]
        s0 = o0 + n_out + co
        own_scr, ex_scr = refs[s0:s0 + n_scr], refs[s0 + n_scr:]
        step = pl.program_id(0)
        for d in range(1, len(grid)):
            step = step * grid[d] + pl.program_id(d)
        parts = (ex_in, ex_out, ex_scr)
        pl.when(step == 0)(lambda: carry.begin(*parts))
        body(*own_in, *own_out, *own_scr)
        if carry.middle is not None:
            pl.when(step == total // 2)(lambda: carry.middle(*parts))
        pl.when(step == total - 1)(lambda: carry.end(*parts))

    outs = pl.pallas_call(
        wrapped, name=name, grid=grid, in_specs=list(in_specs) + [HBM_SPEC] * ci,
        out_specs=list(out_specs) + [HBM_SPEC] * co, out_shape=list(out_shape) + list(carry.out_shape),
        scratch_shapes=list(scratch_shapes) + list(carry.scratch), compiler_params=_params(*sem),
    )(*ins, *carry.inputs)
    return outs[:n_out], outs[n_out:]


def _ffn_fwd(name, xin, affine, wall, offs, tm, loss=None, carry=None):
    T = xin.shape[0]
    nt = T // tm
    fc = DFF // FFN_CHUNKS

    def body(*refs):
        it = iter(refs)
        x_ref = next(it)
        if affine is not None:
            g_ref, b_ref = next(it), next(it)
        wall_ref = next(it)
        if loss is not None:
            go_ref, bo_ref, tgt_ref = next(it), next(it), next(it)
        xh_ref, r_ref, a_ref, b2_ref, yb_ref = (next(it) for _ in range(5))
        if loss is not None:
            loss_ref = next(it)
        w1, w3, w2, sem = (next(it) for _ in range(4))

        @pl.when(pl.program_id(0) == 0)
        def _():
            _load_rows(wall_ref, [(w1, offs[0], RF), (w3, offs[1], RF), (w2, offs[2], RF)], sem)
            if loss is not None:
                loss_ref[...] = jnp.zeros_like(loss_ref)

        y = x_ref[...]
        if affine is not None:
            y = y * g_ref[...] + b_ref[...]
        yb = y.astype(BF16)
        yb_ref[...] = yb
        f = jnp.zeros((tm, D), F32)
        for ci in range(DFF // fc):
            cs = slice(ci * fc, (ci + 1) * fc)
            a = _dot_nt(yb, w1[cs, :]).astype(BF16)
            b = _dot_nt(yb, w3[cs, :]).astype(BF16)
            a_ref[:, cs] = a
            b2_ref[:, cs] = b
            af, bf = a.astype(F32), b.astype(F32)
            u = (af * _sigmoid(af) * bf).astype(BF16)
            f = f + _dot(u, w2[cs, :])
        xh, r = _ln_fwd(ALPHA * y + 0.5 * f)
        xh_ref[...] = xh
        r_ref[...] = jnp.broadcast_to(r, (tm, 128))
        if loss is not None:
            e = xh * go_ref[...] + bo_ref[...] - tgt_ref[...]
            loss_ref[...] += jnp.sum(e * e) * (0.5 / D)

    row = lambda w: pl.BlockSpec((tm, w), lambda i: (i, 0))
    vec = pl.BlockSpec((1, D), lambda i: (0, 0))
    ins, in_specs = [xin], [row(D)]
    if affine is not None:
        ins += list(affine)
        in_specs += [vec, vec]
    ins.append(wall)
    in_specs.append(HBM_SPEC)
    if loss is not None:
        ins += list(loss)
        in_specs += [vec, vec, row(D)]
    out_shape = [jax.ShapeDtypeStruct((T, D), F32), jax.ShapeDtypeStruct((T, 128), F32),
                 jax.ShapeDtypeStruct((T, DFF), BF16), jax.ShapeDtypeStruct((T, DFF), BF16),
                 jax.ShapeDtypeStruct((T, D), BF16)]
    out_specs = [row(D), row(128), row(DFF), row(DFF), row(D)]
    if loss is not None:
        out_shape.append(jax.ShapeDtypeStruct((8, 128), F32))
        out_specs.append(pl.BlockSpec((8, 128), lambda i: (0, 0)))
    return _call(body, name=name, grid=(nt,), in_specs=in_specs, out_specs=out_specs, out_shape=out_shape,
                 scratch_shapes=[pltpu.VMEM((DFF, D), BF16)] * 3 + [pltpu.SemaphoreType.DMA((3 * NDEV,))],
                 ins=ins, carry=carry)


def _ffn_bwd(name, dy_src, xh, r, g, a, b, wall, offs, tm, carry=None):
    T = xh.shape[0]
    nt = T // tm
    fc = DFF // FFN_CHUNKS
    from_loss = dy_src[0] == "loss"

    def body(*refs):
        it = iter(refs)
        if from_loss:
            bo_ref, tgt_ref = next(it), next(it)
        else:
            dy_ref = next(it)
        xh_ref, r_ref, g_ref, a_ref, b_ref, wall_ref = (next(it) for _ in range(6))
        dyin_ref, dab_ref, u_ref, df_ref, dg_ref, db_ref = (next(it) for _ in range(6))
        w1, w3, w2, sem = (next(it) for _ in range(4))

        @pl.when(pl.program_id(0) == 0)
        def _():
            _load_rows(wall_ref, [(w1, offs[0], RF), (w3, offs[1], RF), (w2, offs[2], RF)], sem)
            dg_ref[...] = jnp.zeros_like(dg_ref)
            db_ref[...] = jnp.zeros_like(db_ref)

        xhv = xh_ref[...]
        gv = g_ref[...]
        if from_loss:
            dy = (xhv * gv + bo_ref[...] - tgt_ref[...]) * (1.0 / D)
        else:
            dy = dy_ref[...]
        dz, dgp, dbp = _ln_bwd(dy, xhv, r_ref[:, :1], gv)
        dg_ref[...] += dgp
        db_ref[...] += dbp
        df = (0.5 * dz).astype(BF16)
        df_ref[...] = df
        dx = ALPHA * dz
        for ci in range(DFF // fc):
            cs = slice(ci * fc, (ci + 1) * fc)
            du = _dot_nt(df, w2[cs, :])
            af, bf = a_ref[:, cs].astype(F32), b_ref[:, cs].astype(F32)
            s = _sigmoid(af)
            sl = af * s
            u_ref[:, cs] = (sl * bf).astype(BF16)
            da = (du * bf * (s * (1.0 + af * (1.0 - s)))).astype(BF16)
            dbb = (du * sl).astype(BF16)
            dab_ref[:, cs] = da
            dab_ref[:, DFF + ci * fc:DFF + (ci + 1) * fc] = dbb
            dx = dx + _dot(da, w1[cs, :]) + _dot(dbb, w3[cs, :])
        dyin_ref[...] = dx

    row = lambda w: pl.BlockSpec((tm, w), lambda i: (i, 0))
    vec = pl.BlockSpec((1, D), lambda i: (0, 0))
    if from_loss:
        ins, in_specs = [dy_src[1], dy_src[2]], [vec, row(D)]
    else:
        ins, in_specs = [dy_src[1]], [row(D)]
    ins += [xh, r, g, a, b, wall]
    in_specs += [row(D), row(128), vec, row(DFF), row(DFF), HBM_SPEC]
    return _call(
        body, name=name, grid=(nt,), in_specs=in_specs,
        out_specs=[row(D), row(2 * DFF), row(DFF), row(D), vec, vec],
        out_shape=[jax.ShapeDtypeStruct((T, D), F32), jax.ShapeDtypeStruct((T, 2 * DFF), BF16),
                   jax.ShapeDtypeStruct((T, DFF), BF16), jax.ShapeDtypeStruct((T, D), BF16),
                   jax.ShapeDtypeStruct((1, D), F32), jax.ShapeDtypeStruct((1, D), F32)],
        scratch_shapes=[pltpu.VMEM((DFF, D), BF16)] * 3 + [pltpu.SemaphoreType.DMA((3 * NDEV,))],
        ins=ins, carry=carry)


def _wgrad(name, a, b, bn, tk, col0=0, ncols=None, carry=None):
    T = a.shape[0]
    N = a.shape[1] if ncols is None else ncols
    nk = T // tk
    c0 = col0 // bn

    def body(a_ref, b_ref, o_ref, acc):
        k = pl.program_id(1)

        @pl.when(k == 0)
        def _():
            acc[...] = jnp.zeros_like(acc)

        acc[...] += _dot_tn(a_ref[...], b_ref[...])

        @pl.when(k == nk - 1)
        def _():
            o_ref[...] = acc[...].astype(BF16)

    (out,), ex = _call(
        body, name=name, grid=(N // bn, nk),
        in_specs=[pl.BlockSpec((tk, bn), lambda n, k: (k, n + c0)), pl.BlockSpec((tk, D), lambda n, k: (k, 0))],
        out_specs=[pl.BlockSpec((bn, D), lambda n, k: (n, 0))],
        out_shape=[jax.ShapeDtypeStruct((N, D), BF16)],
        scratch_shapes=[pltpu.VMEM((bn, D), F32)], ins=[a, b], carry=carry)
    return out, ex


def _rope_tables(T):
    pos = jnp.arange(T, dtype=F32)
    inv_freq = ROPE_THETA ** (-jnp.arange(0, ROPE_DIM, 2, dtype=F32) / ROPE_DIM)
    half = ROPE_DIM // 2
    ch = jnp.arange(128) % HEAD_DIM
    ang = pos[:, None] * inv_freq[ch % half][None, :]
    cos, sin = jnp.cos(ang), jnp.sin(ang)
    first, second = (ch < half)[None, :], ((ch >= half) & (ch < ROPE_DIM))[None, :]
    return (jnp.where(first | second, cos, 1.0), jnp.where(first, -sin, 0.0), jnp.where(second, sin, 0.0))


def _rope(t, c, s1, s2):
    n = t.shape[1] // 128
    ct, s1t, s2t = (jnp.tile(v, (1, n)) for v in (c, s1, s2))
    w = t.shape[1]
    return t * ct + pltpu.roll(t, w - 8, 1) * s1t + pltpu.roll(t, 8, 1) * s2t


def _rope_t(dr, c, s1, s2):
    n = dr.shape[1] // 128
    ct, s1t, s2t = (jnp.tile(v, (1, n)) for v in (c, s1, s2))
    w = dr.shape[1]
    return dr * ct + pltpu.roll(dr * s1t, 8, 1) + pltpu.roll(dr * s2t, w - 8, 1)


_Q, _K, _V = (0, 1024), (1024, 256), (1280, 256)
_HG = (1536, HG_W)
_GATES = (5632, GATE_W)


def _inproj_fwd(name, xh, g, b, wall, b_in, ropes, tm, carry=None):
    T = xh.shape[0]

    def body(xh_ref, g_ref, b_ref, wall_ref, bin_ref, c_ref, s1_ref, s2_ref,
             qkv_ref, kt_ref, vt_ref, hg_ref, gate_ref, yb_ref, w, sem):
        @pl.when(pl.program_id(0) == 0)
        def _():
            _load_rows(wall_ref, [(w, 0, RIN)], sem)

        yb = (xh_ref[...] * g_ref[...] + b_ref[...]).astype(BF16)
        yb_ref[...] = yb
        c, s1, s2 = c_ref[...], s1_ref[...], s2_ref[...]

        def piece(start, width):
            return _dot_nt(yb, w[start:start + width, :]) + bin_ref[:, start:start + width]

        k = _rope(piece(*_K), c, s1, s2)
        v = piece(*_V)
        qkv_ref[:, 0:1024] = _rope(piece(*_Q), c, s1, s2).astype(BF16)
        qkv_ref[:, 1024:1280] = k.astype(BF16)
        qkv_ref[:, 1280:1536] = v.astype(BF16)
        kt_ref[...] = k.T.astype(BF16)
        vt_ref[...] = v.T.astype(BF16)
        for j in range(4):
            hg_ref[:, 1024 * j:1024 * (j + 1)] = piece(_HG[0] + 1024 * j, 1024)
        for j in range(2):
            gate_ref[:, 1024 * j:1024 * (j + 1)] = piece(_GATES[0] + 1024 * j, 1024)

    row = lambda wd: pl.BlockSpec((tm, wd), lambda i: (i, 0))
    vec = lambda wd: pl.BlockSpec((1, wd), lambda i: (0, 0))
    colt = pl.BlockSpec((256, tm), lambda i: (0, i))
    return _call(
        body, name=name, grid=(T // tm,),
        in_specs=[row(D), vec(D), vec(D), HBM_SPEC, vec(DIN), row(128), row(128), row(128)],
        out_specs=[row(QKV_W), colt, colt, row(HG_W), row(GATE_W), row(D)],
        out_shape=[jax.ShapeDtypeStruct((T, QKV_W), BF16), jax.ShapeDtypeStruct((256, T), BF16),
                   jax.ShapeDtypeStruct((256, T), BF16), jax.ShapeDtypeStruct((T, HG_W), F32),
                   jax.ShapeDtypeStruct((T, GATE_W), F32), jax.ShapeDtypeStruct((T, D), BF16)],
        scratch_shapes=[pltpu.VMEM((DIN, D), BF16), pltpu.SemaphoreType.DMA((NDEV,))],
        ins=[xh, g, b, wall, b_in, *ropes], carry=carry)


def _inproj_bwd(name, dq, dk, dv, dhg, dgates, dz2, wall, ropes, tm):
    T = dq.shape[0]

    def body(dq_ref, dk_ref, dv_ref, d0, d1, d2, d3, dga_ref, dgh_ref, dz_ref, wall_ref, c_ref, s1_ref, s2_ref,
             dy_ref, dproj_ref, dbin_ref, w, sem):
        @pl.when(pl.program_id(0) == 0)
        def _():
            _load_rows(wall_ref, [(w, 0, RIN)], sem)
            dbin_ref[...] = jnp.zeros_like(dbin_ref)

        c, s1, s2 = c_ref[...], s1_ref[...], s2_ref[...]
        acc = ALPHA * dz_ref[...]
        pieces = [(_Q[0], _rope_t(dq_ref[...], c, s1, s2)), (_K[0], _rope_t(dk_ref[...], c, s1, s2)),
                  (_V[0], dv_ref[...])]
        pieces += [(_HG[0] + 1024 * j, r[...]) for j, r in enumerate((d0, d1, d2, d3))]
        pieces += [(_GATES[0], dga_ref[...]), (_GATES[0] + 1024, dgh_ref[...])]
        for start, val in pieces:
            width = val.shape[1]
            dbin_ref[:, start:start + width] += jnp.sum(val, axis=0, keepdims=True)
            vb = val.astype(BF16)
            dproj_ref[:, start:start + width] = vb
            acc = acc + _dot(vb, w[start:start + width, :])
        dy_ref[...] = acc

    row = lambda wd: pl.BlockSpec((tm, wd), lambda i: (i, 0))
    return pl.pallas_call(
        body, name=name, grid=(T // tm,),
        in_specs=[row(D), row(256), row(256)] + [row(D)] * 4 + [row(D), row(D), row(D), HBM_SPEC] + [row(128)] * 3,
        out_specs=[row(D), row(DIN), pl.BlockSpec((1, DIN), lambda i: (0, 0))],
        out_shape=[jax.ShapeDtypeStruct((T, D), F32), jax.ShapeDtypeStruct((T, DIN), BF16),
                   jax.ShapeDtypeStruct((1, DIN), F32)],
        scratch_shapes=[pltpu.VMEM((DIN, D), BF16), pltpu.SemaphoreType.DMA((NDEV,))],
        compiler_params=_params("arbitrary"),
    )(dq, dk, dv, *dhg, *dgates, dz2, wall, *ropes)


def _halves(t):
    lane = lax.broadcasted_iota(jnp.int32, t.shape, 1)
    low = lane < HEAD_DIM
    sw = pltpu.roll(t, HEAD_DIM, 1)
    zero = jnp.zeros_like(t)
    h0 = (jnp.where(low, t, zero), jnp.where(low, zero, sw))
    h1 = (jnp.where(low, sw, zero), jnp.where(low, zero, t))
    return h0, h1


def _lane_stack(p_ref, c_ref, gp):
    sl = slice(128 * gp, 128 * (gp + 1))
    hp, hc = _halves(p_ref[:, sl].astype(F32)), _halves(c_ref[:, sl].astype(F32))
    return [jnp.concatenate([hp[gg][0], hc[gg][0], hp[gg][1], hc[gg][1]], axis=0).astype(BF16) for gg in range(2)]


def _row_stack(tp_ref, tc_ref, g):
    band = jnp.concatenate([tp_ref[64 * g:64 * (g + 1), :], tc_ref[64 * g:64 * (g + 1), :]], axis=1)
    z = jnp.zeros_like(band)
    return [jnp.concatenate([band, z], axis=0), jnp.concatenate([z, band], axis=0)]


def _fold(prevm, t4, hh):
    return jnp.where(prevm, t4[256 * hh:256 * hh + 128, :], t4[256 * hh + 128:256 * hh + 256, :])


def _unfold(prevm, t):
    return jnp.concatenate([jnp.where(prevm, t, 0.0), jnp.where(prevm, 0.0, t)], axis=0).astype(BF16)


def _attn_softmax(s, kill, sink):
    s = jnp.where(kill, NEG_INF, s)
    m = jnp.maximum(jnp.max(s, axis=0, keepdims=True), sink)
    p = jnp.exp(s - m)
    es = jnp.exp(sink - m)
    denom = jnp.sum(p, axis=0, keepdims=True) + es
    return p / denom, es / denom


def _attn_masks(first):
    row = lax.broadcasted_iota(jnp.int32, (ATTN_BLOCK, 2 * ATTN_BLOCK), 0)
    lane = lax.broadcasted_iota(jnp.int32, (ATTN_BLOCK, 2 * ATTN_BLOCK), 1)
    prevm = row > lane % ATTN_BLOCK
    return prevm, jnp.logical_and(first, prevm)


def _pair_rows(ref, g):
    return jnp.concatenate([ref[:, 256 * g:256 * g + 128], ref[:, 256 * g + 128:256 * (g + 1)]], axis=0)


def _pair_sinks(sink_ref, g, hh):
    h0, h1 = 4 * g + hh, 4 * g + 2 + hh
    return jnp.concatenate([jnp.broadcast_to(sink_ref[:, h0:h0 + 1], (1, ATTN_BLOCK)),
                            jnp.broadcast_to(sink_ref[:, h1:h1 + 1], (1, ATTN_BLOCK))], axis=1)


def _attn_fwd(name, qkv, vt, sinks):
    T = qkv.shape[0]
    nb = T // ATTN_BLOCK
    scale = HEAD_DIM ** -0.5

    def body(q_ref, kp_ref, kc_ref, vtp_ref, vtc_ref, sink_ref, o_ref):
        prevm, kill = _attn_masks(pl.program_id(0) == 0)
        kst = _lane_stack(kp_ref, kc_ref, 0) + _lane_stack(kp_ref, kc_ref, 1)
        vts = [_row_stack(vtp_ref, vtc_ref, g) for g in range(N_KV_HEADS)]
        s8 = [_dot_nt(kst[g], _pair_rows(q_ref, g)) * scale for g in range(N_KV_HEADS)]
        pu = [[_unfold(prevm, _attn_softmax(_fold(prevm, s8[g], hh), kill, _pair_sinks(sink_ref, g, hh))[0])
               for hh in range(2)] for g in range(N_KV_HEADS)]
        for g in range(N_KV_HEADS):
            ot = _dot(vts[g][0], pu[g][0]) + _dot(vts[g][1], pu[g][1])
            o_ref[:, 256 * g:256 * g + 128] = ot[:, :ATTN_BLOCK].T.astype(BF16)
            o_ref[:, 256 * g + 128:256 * (g + 1)] = ot[:, ATTN_BLOCK:].T.astype(BF16)

    prev = lambda i: jnp.maximum(i - 1, 0)
    return pl.pallas_call(
        body, name=name, grid=(nb,),
        in_specs=[pl.BlockSpec((ATTN_BLOCK, D), lambda i: (i, 0)),
                  pl.BlockSpec((ATTN_BLOCK, 256), lambda i: (prev(i), 4)),
                  pl.BlockSpec((ATTN_BLOCK, 256), lambda i: (i, 4)),
                  pl.BlockSpec((256, ATTN_BLOCK), lambda i: (0, prev(i))),
                  pl.BlockSpec((256, ATTN_BLOCK), lambda i: (0, i)),
                  pl.BlockSpec((1, N_Q_HEADS), lambda i: (0, 0))],
        out_specs=pl.BlockSpec((ATTN_BLOCK, D), lambda i: (i, 0)),
        out_shape=jax.ShapeDtypeStruct((T, D), BF16),
        compiler_params=_params("arbitrary"),
    )(qkv, qkv, qkv, vt, vt, sinks)


def _attn_bwd(name, qkv, kt, sinks, dya, carry=None):
    T = qkv.shape[0]
    nb = T // ATTN_BLOCK
    scale = HEAD_DIM ** -0.5

    def body(q_ref, kp_ref, kc_ref, vp_ref, vc_ref, ktp_ref, ktc_ref, sink_ref, do_ref,
             dq_ref, dk_ref, dv_ref, ds_ref):
        i = pl.program_id(0)

        @pl.when(i == 0)
        def _():
            ds_ref[...] = jnp.zeros_like(ds_ref)

        prevm, kill = _attn_masks(i == 0)
        kst = _lane_stack(kp_ref, kc_ref, 0) + _lane_stack(kp_ref, kc_ref, 1)
        vst = _lane_stack(vp_ref, vc_ref, 0) + _lane_stack(vp_ref, vc_ref, 1)
        kts = [_row_stack(ktp_ref, ktc_ref, g) for g in range(N_KV_HEADS)]
        lane = lax.broadcasted_iota(jnp.int32, (2 * ATTN_BLOCK, 128), 1)
        low = lane < HEAD_DIM
        slane = lax.broadcasted_iota(jnp.int32, (1, 128), 1)
        dkz = [jnp.zeros((2 * ATTN_BLOCK, 128), F32) for _ in range(N_KV_HEADS)]
        dvz = [jnp.zeros((2 * ATTN_BLOCK, 128), F32) for _ in range(N_KV_HEADS)]
        dsink = jnp.zeros((1, 128), F32)
        groups = range(N_KV_HEADS)
        qcat = [_pair_rows(q_ref, g) for g in groups]
        docat = [_pair_rows(do_ref, g) for g in groups]
        s8 = [_dot_nt(kst[g], qcat[g]) * scale for g in groups]
        dp8 = [_dot_nt(vst[g], docat[g]) for g in groups]
        ds_u = [[None, None] for _ in groups]
        p_u = [[None, None] for _ in groups]
        for g in groups:
            for hh in range(2):
                pn, ps = _attn_softmax(_fold(prevm, s8[g], hh), kill, _pair_sinks(sink_ref, g, hh))
                dp = _fold(prevm, dp8[g], hh)
                delta = jnp.sum(pn * dp, axis=0, keepdims=True)
                sd = ps * delta
                dsink = dsink + jnp.where(slane == 4 * g + hh, -jnp.sum(sd[:, :ATTN_BLOCK]), 0.0) \
                    + jnp.where(slane == 4 * g + 2 + hh, -jnp.sum(sd[:, ATTN_BLOCK:]), 0.0)
                ds_u[g][hh] = _unfold(prevm, pn * (dp - delta) * scale)
                p_u[g][hh] = _unfold(prevm, pn)
        for g in groups:
            dqt = _dot(kts[g][0], ds_u[g][0]) + _dot(kts[g][1], ds_u[g][1])
            dq_ref[:, 256 * g:256 * g + 128] = dqt[:, :ATTN_BLOCK].T
            dq_ref[:, 256 * g + 128:256 * (g + 1)] = dqt[:, ATTN_BLOCK:].T
            for hh in range(2):
                own = low if hh == 0 else jnp.logical_not(low)
                dk_h = jnp.where(own, _dot(ds_u[g][hh], qcat[g]), 0.0)
                dv_h = jnp.where(own, _dot(p_u[g][hh], docat[g]), 0.0)
                if hh != g % 2:
                    dk_h = pltpu.roll(dk_h, HEAD_DIM, 1)
                    dv_h = pltpu.roll(dv_h, HEAD_DIM, 1)
                dkz[g] = dkz[g] + dk_h
                dvz[g] = dvz[g] + dv_h
        ds_ref[...] += dsink
        cur = pl.ds(pl.multiple_of(i * ATTN_BLOCK, ATTN_BLOCK), ATTN_BLOCK)
        prv = pl.ds(pl.multiple_of(jnp.maximum(i - 1, 0) * ATTN_BLOCK, ATTN_BLOCK), ATTN_BLOCK)
        for gp in range(N_KV_HEADS // 2):
            cols = slice(128 * gp, 128 * (gp + 1))
            dkb = dkz[2 * gp] + dkz[2 * gp + 1]
            dvb = dvz[2 * gp] + dvz[2 * gp + 1]
            dk_ref[cur, cols] = dkb[ATTN_BLOCK:, :]
            dv_ref[cur, cols] = dvb[ATTN_BLOCK:, :]

            @pl.when(i > 0)
            def _():
                dk_ref[prv, cols] += dkb[:ATTN_BLOCK, :]
                dv_ref[prv, cols] += dvb[:ATTN_BLOCK, :]

    prev = lambda i: jnp.maximum(i - 1, 0)
    whole = lambda w: pl.BlockSpec((T, w), lambda i: (0, 0))
    return _call(
        body, name=name, grid=(nb,), scratch_shapes=[], ins=[qkv, qkv, qkv, qkv, qkv, kt, kt, sinks, dya], carry=carry,
        in_specs=[pl.BlockSpec((ATTN_BLOCK, D), lambda i: (i, 0)),
                  pl.BlockSpec((ATTN_BLOCK, 256), lambda i: (prev(i), 4)),
                  pl.BlockSpec((ATTN_BLOCK, 256), lambda i: (i, 4)),
                  pl.BlockSpec((ATTN_BLOCK, 256), lambda i: (prev(i), 5)),
                  pl.BlockSpec((ATTN_BLOCK, 256), lambda i: (i, 5)),
                  pl.BlockSpec((256, ATTN_BLOCK), lambda i: (0, prev(i))),
                  pl.BlockSpec((256, ATTN_BLOCK), lambda i: (0, i)),
                  pl.BlockSpec((1, N_Q_HEADS), lambda i: (0, 0)),
                  pl.BlockSpec((ATTN_BLOCK, D), lambda i: (i, 0))],
        out_specs=[pl.BlockSpec((ATTN_BLOCK, D), lambda i: (i, 0)), whole(256), whole(256),
                   pl.BlockSpec((1, 128), lambda i: (0, 0))],
        out_shape=[jax.ShapeDtypeStruct((T, D), F32), jax.ShapeDtypeStruct((T, 256), F32),
                   jax.ShapeDtypeStruct((T, 256), F32), jax.ShapeDtypeStruct((1, 128), F32)])


HG_FWD = (256, 8, 512)
HG_BWD = (128, 8, 512)


def _chunk_sum(tri, x):
    w = x.shape[1]
    h1 = x.astype(BF16)
    h2 = (x - h1.astype(F32)).astype(BF16)
    r = _dot(tri, jnp.concatenate([h1, h2], axis=1))
    return r[:, :w] + r[:, w:]


def _chunk_tri(n):
    ri = lax.broadcasted_iota(jnp.int32, (n, n), 0)
    ci = lax.broadcasted_iota(jnp.int32, (n, n), 1)
    return jnp.where((ri >= ci) & (ri // HGRN_CHUNK == ci // HGRN_CHUNK), 1.0, 0.0).astype(BF16)


def _chunks(t):
    return [t[HGRN_CHUNK * c:HGRN_CHUNK * (c + 1), :] for c in range(t.shape[0] // HGRN_CHUNK)]


def _lower_bound(lbl_ref):
    l0, l1 = lbl_ref[0:1, :], lbl_ref[1:2, :]
    m = jnp.maximum(l0, l1)
    e0, e1 = jnp.exp(l0 - m), jnp.exp(l1 - m)
    return e0 / (e0 + e1)


def _heads(t):
    return [t[:, 128 * h:128 * (h + 1)] for h in range(t.shape[1] // 128)]


def _per_head(fn, *wide):
    return jnp.concatenate([fn(*parts) for parts in zip(*[_heads(t) for t in wide])], axis=1)


def _hgrn_sub(fl, qh, vv, lb, tril_b):
    w = fl.shape[1]
    sg = _sigmoid(fl)
    f = lb + (1.0 - lb) * sg
    k = 1.0 - f
    gc = _chunk_sum(tril_b, jnp.log(f))
    gl_c = [t[HGRN_CHUNK - 1:HGRN_CHUNK, :] for t in _chunks(gc)]
    gl = jnp.concatenate([jnp.broadcast_to(g, (HGRN_CHUNK, w)) for g in gl_c], axis=0)
    sq = _sigmoid(qh)
    eg = jnp.exp(gc)
    eng = jnp.exp(-gc)
    elg = jnp.exp(gl - gc)
    qd = qh * sq * eg
    ki = k * eng
    ke = k * elg
    qd_b, ki_b, ke_b, v_b = (t.astype(BF16) for t in (qd, ki, ke, vv))
    am_b = [jnp.where(tril_b > 0, _dot_nt(qq, kk), 0.0).astype(BF16) for qq, kk in zip(_heads(qd_b), _heads(ki_b))]
    return dict(sg=sg, f=f, sq=sq, eg=eg, eng=eng, elg=elg, qd=qd, ki=ki, ke=ke, egl=[jnp.exp(g) for g in gl_c],
                am_b=am_b, qd_b=qd_b, ki_b=ki_b, ke_b=ke_b, v_b=v_b)


def _hgrn_out(q, st_b):
    outs = []
    for h, (qd_h, v_h) in enumerate(zip(_heads(q["qd_b"]), _heads(q["v_b"]))):
        inter = [_dot_nt(qc, s) for qc, s in zip(_chunks(qd_h), st_b[h])]
        outs.append(_dot(q["am_b"][h], v_h) + jnp.concatenate(inter, axis=0))
    return jnp.concatenate(outs, axis=1)


def _head_rms(o):
    return _per_head(lambda t: jnp.broadcast_to(
        lax.rsqrt(jnp.mean(t * t, axis=-1, keepdims=True) + RMS_EPS), t.shape), o)


def _hgrn_fwd(name, hg, lb_logits, norm_g, cfg):
    T = hg.shape[0]
    sub_rows, hps, th = min(cfg[0], T), cfg[1], min(cfg[2], T)
    cps = sub_rows // HGRN_CHUNK
    nc = th // HGRN_CHUNK
    tri = _chunk_tri(sub_rows)

    def body(fl_ref, qh_ref, v_ref, og_ref, lbl_ref, ng_ref, tri_ref, y_ref, ss_ref, st):
        @pl.when(pl.program_id(1) == 0)
        def _():
            st[...] = jnp.zeros_like(st)

        lbs = _lower_bound(lbl_ref)
        ng = jnp.tile(ng_ref[...], (1, hps))

        def sub(si, carry):
            rows = pl.ds(pl.multiple_of(si * sub_rows, sub_rows), sub_rows)
            q = _hgrn_sub(fl_ref[rows, :], qh_ref[rows, :], v_ref[rows, :], lbs, tri_ref[...])
            v_hc = [_chunks(t) for t in _heads(q["v_b"])]
            k_hc = [_chunks(t) for t in _heads(q["ke_b"])]
            s = [st[h] for h in range(hps)]
            st_b = [[] for _ in range(hps)]
            for c in range(cps):
                egl = _heads(q["egl"][c])
                for h in range(hps):
                    ss_ref[h, si * cps + c] = s[h]
                    st_b[h].append(s[h].astype(BF16))
                    s[h] = s[h] * egl[h] + _dot_tn(v_hc[h][c], k_hc[h][c])
            for h in range(hps):
                st[h] = s[h]
            o = _hgrn_out(q, st_b)
            og = og_ref[rows, :]
            y_ref[rows, :] = (o * _head_rms(o) * ng * (og * _sigmoid(og))).astype(BF16)
            return carry

        lax.fori_loop(0, th // sub_rows, sub, 0)

    wd = 128 * hps
    col = lambda j: pl.BlockSpec((th, wd), lambda h, t: (t, (8 // hps) * j + h))
    return pl.pallas_call(
        body, name=name, grid=(HGRN_HEADS // hps, T // th),
        in_specs=[col(0), col(1), col(2), col(3), pl.BlockSpec((2, wd), lambda h, t: (0, h)),
                  pl.BlockSpec((1, 128), lambda h, t: (0, 0)), pl.BlockSpec((sub_rows, sub_rows), lambda h, t: (0, 0))],
        out_specs=[pl.BlockSpec((th, wd), lambda h, t: (t, h)),
                   pl.BlockSpec((hps, nc, 128, 128), lambda h, t: (h, t, 0, 0))],
        out_shape=[jax.ShapeDtypeStruct((T, D), BF16),
                   jax.ShapeDtypeStruct((HGRN_HEADS, T // HGRN_CHUNK, 128, 128), F32)],
        scratch_shapes=[pltpu.VMEM((hps, 128, 128), F32)],
        compiler_params=_params("parallel", "arbitrary"),
    )(hg, hg, hg, hg, lb_logits, norm_g, tri)


def _hgrn_bwd(name, hg, lb_logits, norm_g, sstart, dyh, cfg, carry=None):
    T = hg.shape[0]
    sub_rows, hps, th = min(cfg[0], T), cfg[1], min(cfg[2], T)
    cps = sub_rows // HGRN_CHUNK
    nc = th // HGRN_CHUNK
    nt = T // th
    wd = 128 * hps
    tri = _chunk_tri(sub_rows)
    tri_t = tri.T

    def body(fl_ref, qh_ref, v_ref, og_ref, lbl_ref, ng_ref, tri_ref, trit_ref, ss_ref, dy_ref,
             dfl_ref, dqh_ref, dv_ref, dog_ref, dlb_ref, dng_ref, dst):
        @pl.when(pl.program_id(1) == 0)
        def _():
            dst[...] = jnp.zeros_like(dst)
            dlb_ref[...] = jnp.zeros_like(dlb_ref)
            dng_ref[...] = jnp.zeros_like(dng_ref)

        lb = _lower_bound(lbl_ref)
        ng = jnp.tile(ng_ref[...], (1, hps))
        last = lax.broadcasted_iota(jnp.int32, (HGRN_CHUNK, wd), 0) == HGRN_CHUNK - 1
        nsub = th // sub_rows
        cat0 = lambda parts: jnp.concatenate(parts, axis=0)

        def sub(step, carry):
            si = nsub - 1 - step
            rows = pl.ds(pl.multiple_of(si * sub_rows, sub_rows), sub_rows)
            qh, og, dy = qh_ref[rows, :], og_ref[rows, :], dy_ref[rows, :]
            tril_b = tri_ref[...]
            q = _hgrn_sub(fl_ref[rows, :], qh, v_ref[rows, :], lb, tril_b)
            s_in = [[ss_ref[h, si * cps + c] for c in range(cps)] for h in range(hps)]
            st_b = [[s.astype(BF16) for s in row] for row in s_in]
            o = _hgrn_out(q, st_b)
            rr = _head_rms(o)
            oh = o * rr
            sog = _sigmoid(og)
            dog_ref[rows, :] = dy * oh * ng * (sog * (1.0 + og * (1.0 - sog)))
            don = dy * (og * sog)
            dng_w = jnp.sum(don * oh, axis=0, keepdims=True)
            dd = don * ng
            mean_h = _per_head(lambda t: jnp.broadcast_to(jnp.mean(t, axis=-1, keepdims=True), t.shape), dd * oh)
            do_b = (rr * (dd - oh * mean_h)).astype(BF16)
            do_h, qd_h, ki_h, ke_h, v_h = (_heads(t) for t in (do_b, q["qd_b"], q["ki_b"], q["ke_b"], q["v_b"]))
            da_b = [jnp.where(tril_b > 0, _dot_nt(do_h[h], v_h[h]), 0.0).astype(BF16) for h in range(hps)]
            do_c, qd_c, ke_c, v_c = ([_chunks(t) for t in hs] for hs in (do_h, qd_h, ke_h, v_h))
            dsp = [[None] * cps for _ in range(hps)]
            dgl_dec = [[None] * cps for _ in range(hps)]
            d = [dst[h] for h in range(hps)]
            for c in reversed(range(cps)):
                egl = _heads(q["egl"][c])
                for h in range(hps):
                    dsp[h][c] = d[h]
                    dgl_dec[h][c] = jnp.sum(d[h] * s_in[h][c], axis=0, keepdims=True) * egl[h]
                    d[h] = d[h] * egl[h] + _dot_tn(do_c[h][c], qd_c[h][c])
            for h in range(hps):
                dst[h] = d[h]
                dng_ref[h] += dng_w[:, 128 * h:128 * (h + 1)]
            dsp_b = [[t.astype(BF16) for t in row] for row in dsp]
            dv_ref[rows, :] = jnp.concatenate(
                [_dot_tn(q["am_b"][h], do_h[h]) + cat0([_dot_nt(ke_c[h][c], dsp_b[h][c]) for c in range(cps)])
                 for h in range(hps)], axis=1)
            dqd = jnp.concatenate(
                [_dot(da_b[h], ki_h[h]) + cat0([_dot(do_c[h][c], st_b[h][c]) for c in range(cps)])
                 for h in range(hps)], axis=1)
            dki = jnp.concatenate([_dot_tn(da_b[h], qd_h[h]) for h in range(hps)], axis=1)
            dke = jnp.concatenate([cat0([_dot(v_c[h][c], dsp_b[h][c]) for c in range(cps)]) for h in range(hps)], axis=1)
            dkk = dke * q["ke"]
            dgl = [jnp.sum(t, axis=0, keepdims=True) + jnp.concatenate([dgl_dec[h][c] for h in range(hps)], axis=1)
                   for c, t in enumerate(_chunks(dkk))]
            dgc = dqd * q["qd"] - dki * q["ki"] - dkk + cat0([jnp.where(last, g, 0.0) for g in dgl])
            dlf = _chunk_sum(trit_ref[...], dgc)
            df = dlf / q["f"] - (dki * q["eng"] + dke * q["elg"])
            sg = q["sg"]
            dlb_ref[...] += jnp.sum(df * (1.0 - sg), axis=0, keepdims=True)
            dfl_ref[rows, :] = df * (1.0 - lb) * sg * (1.0 - sg)
            sq = q["sq"]
            dqh_ref[rows, :] = dqd * q["eg"] * (sq * (1.0 + qh * (1.0 - sq)))
            return carry

        lax.fori_loop(0, nsub, sub, 0)
    col = lambda j: pl.BlockSpec((th, wd), lambda h, t: (nt - 1 - t, (8 // hps) * j + h))
    out = pl.BlockSpec((th, wd), lambda h, t: (nt - 1 - t, h))
    tri_spec = pl.BlockSpec((sub_rows, sub_rows), lambda h, t: (0, 0))
    return _call(
        body, name=name, grid=(HGRN_HEADS // hps, nt),
        in_specs=[col(0), col(1), col(2), col(3), pl.BlockSpec((2, wd), lambda h, t: (0, h)),
                  pl.BlockSpec((1, 128), lambda h, t: (0, 0)), tri_spec, tri_spec,
                  pl.BlockSpec((hps, nc, 128, 128), lambda h, t: (h, nt - 1 - t, 0, 0)), out],
        out_specs=[out, out, out, out, pl.BlockSpec((1, wd), lambda h, t: (0, h)),
                   pl.BlockSpec((hps, 1, 128), lambda h, t: (h, 0, 0))],
        out_shape=[jax.ShapeDtypeStruct((T, D), F32)] * 4 + [jax.ShapeDtypeStruct((1, D), F32),
                                                              jax.ShapeDtypeStruct((HGRN_HEADS, 1, 128), F32)],
        scratch_shapes=[pltpu.VMEM((hps, 128, 128), F32)],
        ins=[hg, hg, hg, hg, lb_logits, norm_g, tri, tri_t, sstart, dyh], carry=carry)


def _mix_fwd(name, ya, yh, gates, xh1, g1, b1, wall, tm):
    T = ya.shape[0]

    def body(ya_ref, yh_ref, ga_ref, gh_ref, xh_ref, g_ref, b_ref, wall_ref,
             xo_ref, r_ref, pa_ref, ph_ref, mg_ref, wpa, wph, wo, sem):
        @pl.when(pl.program_id(0) == 0)
        def _():
            _load_rows(wall_ref, [(wpa, RIN, RP), (wph, RIN + RP, RP), (wo, RIN + 2 * RP, RP)], sem)

        pa = _dot(ya_ref[...], wpa[...]).astype(BF16)
        ph = _dot(yh_ref[...], wph[...]).astype(BF16)
        pa_ref[...] = pa
        ph_ref[...] = ph
        mg = (_sigmoid(ga_ref[...]) * pa.astype(F32) + _sigmoid(gh_ref[...]) * ph.astype(F32)).astype(BF16)
        mg_ref[...] = mg
        y1 = xh_ref[...] * g_ref[...] + b_ref[...]
        xh2, r = _ln_fwd(ALPHA * y1 + _dot(mg, wo[...]))
        xo_ref[...] = xh2
        r_ref[...] = jnp.broadcast_to(r, (tm, 128))

    row = lambda w: pl.BlockSpec((tm, w), lambda i: (i, 0))
    vec = pl.BlockSpec((1, D), lambda i: (0, 0))
    return pl.pallas_call(
        body, name=name, grid=(T // tm,),
        in_specs=[row(D), row(D), pl.BlockSpec((tm, D), lambda i: (i, 0)), pl.BlockSpec((tm, D), lambda i: (i, 1)),
                  row(D), vec, vec, HBM_SPEC],
        out_specs=[row(D), row(128), row(D), row(D), row(D)],
        out_shape=[jax.ShapeDtypeStruct((T, D), F32), jax.ShapeDtypeStruct((T, 128), F32)]
        + [jax.ShapeDtypeStruct((T, D), BF16)] * 3,
        scratch_shapes=[pltpu.VMEM((D, D), BF16)] * 3 + [pltpu.SemaphoreType.DMA((3 * NDEV,))],
        compiler_params=_params("arbitrary"),
    )(ya, yh, gates, gates, xh1, g1, b1, wall)


def _mix_bwd(name, dy2, xh2, r2, g2, gates, pa, ph, wall, tm):
    T = dy2.shape[0]

    def body(dy_ref, xh_ref, r_ref, g_ref, ga_ref, gh_ref, pa_ref, ph_ref, wall_ref,
             dz_ref, dmix_ref, dpa_ref, dph_ref, dga_ref, dgh_ref, dya_ref, dyh_ref, dg_ref, db_ref,
             wpa, wph, wo, sem):
        @pl.when(pl.program_id(0) == 0)
        def _():
            _load_rows(wall_ref, [(wpa, RIN, RP), (wph, RIN + RP, RP), (wo, RIN + 2 * RP, RP)], sem)
            dg_ref[...] = jnp.zeros_like(dg_ref)
            db_ref[...] = jnp.zeros_like(db_ref)

        dz, dgp, dbp = _ln_bwd(dy_ref[...], xh_ref[...], r_ref[:, :1], g_ref[...])
        dg_ref[...] += dgp
        db_ref[...] += dbp
        dz_ref[...] = dz
        dmix = dz.astype(BF16)
        dmix_ref[...] = dmix
        dmg = _dot_nt(dmix, wo[...])
        sa, sh = _sigmoid(ga_ref[...]), _sigmoid(gh_ref[...])
        dga_ref[...] = dmg * pa_ref[...].astype(F32) * sa * (1.0 - sa)
        dgh_ref[...] = dmg * ph_ref[...].astype(F32) * sh * (1.0 - sh)
        dpa = (dmg * sa).astype(BF16)
        dph = (dmg * sh).astype(BF16)
        dpa_ref[...] = dpa
        dph_ref[...] = dph
        dya_ref[...] = _dot_nt(dpa, wpa[...]).astype(BF16)
        dyh_ref[...] = _dot_nt(dph, wph[...])

    row = lambda w: pl.BlockSpec((tm, w), lambda i: (i, 0))
    vec = pl.BlockSpec((1, D), lambda i: (0, 0))
    return pl.pallas_call(
        body, name=name, grid=(T // tm,),
        in_specs=[row(D), row(D), row(128), vec, pl.BlockSpec((tm, D), lambda i: (i, 0)),
                  pl.BlockSpec((tm, D), lambda i: (i, 1)), row(D), row(D), HBM_SPEC],
        out_specs=[row(D)] * 8 + [vec, vec],
        out_shape=[jax.ShapeDtypeStruct((T, D), F32)] + [jax.ShapeDtypeStruct((T, D), BF16)] * 3
        + [jax.ShapeDtypeStruct((T, D), F32)] * 2 + [jax.ShapeDtypeStruct((T, D), BF16), jax.ShapeDtypeStruct((T, D), F32)]
        + [jax.ShapeDtypeStruct((1, D), F32)] * 2,
        scratch_shapes=[pltpu.VMEM((D, D), BF16)] * 3 + [pltpu.SemaphoreType.DMA((3 * NDEV,))],
        compiler_params=_params("arbitrary"),
    )(dy2, xh2, r2, g2, gates, gates, pa, ph, wall)


def _adam(w, g, m, v):
    m = ADAM_B1 * m + (1.0 - ADAM_B1) * g
    v = ADAM_B2 * v + (1.0 - ADAM_B2) * (g * g)
    m_hat = m / (1.0 - ADAM_B1 ** ADAM_STEP)
    v_hat = v / (1.0 - ADAM_B2 ** ADAM_STEP)
    delta = -ADAM_LR * (m_hat / (jnp.sqrt(v_hat) + ADAM_EPS) + ADAM_WD * w)
    return delta, m, v


def _grad_step(name, recv, w, m, v):
    _, rows, _ = recv.shape
    cb = 256

    def body(r_ref, w_ref, m_ref, v_ref, g_ref, d_ref, mo_ref, vo_ref):
        acc = r_ref[0].astype(F32)
        for k in range(1, NDEV):
            acc = acc + r_ref[k].astype(F32)
        g_ref[...] = acc
        d, mm, vv = _adam(w_ref[...], acc, m_ref[...], v_ref[...])
        d_ref[...] = d
        mo_ref[...] = mm
        vo_ref[...] = vv

    blk = pl.BlockSpec((rows, cb), lambda j: (0, j))
    return pl.pallas_call(
        body, name=name, grid=(D // cb,),
        in_specs=[pl.BlockSpec((NDEV, rows, cb), lambda j: (0, 0, j)), blk, blk, blk],
        out_specs=[blk] * 4, out_shape=[jax.ShapeDtypeStruct(w.shape, F32)] * 4,
        compiler_params=_params("parallel"),
    )(recv, w, m, v)


_SMALL = [("ln1_g", D), ("ln1_b", D), ("ln2_g", D), ("ln2_b", D), ("ln3_g", D), ("ln3_b", D),
          ("b_in", DIN), ("lb", D), ("attn_sinks", 128), ("hgrn_norm_g", 128), ("loss", 128)]
_SMALL_OFF = {}
_o = 0
for _n, _w in _SMALL:
    _SMALL_OFF[_n] = (_o, _w)
    _o += _w
PACK = _o


def _small_reduce(name, packed):
    def body(p_ref, o_ref, buf, send_sems, recv_sems):
        x, y, c = lax.axis_index("x"), lax.axis_index("y"), lax.axis_index("c")
        me = 4 * x + 2 * y + c
        buf[me] = p_ref[...]
        copies = []
        for j in range(1, NDEV):
            px, py, pc = x ^ (j >> 2), y ^ ((j >> 1) & 1), c ^ (j & 1)
            cp = pltpu.make_async_remote_copy(
                src_ref=p_ref, dst_ref=buf.at[me], send_sem=send_sems.at[j - 1], recv_sem=recv_sems.at[j - 1],
                device_id=(px, py, pc), device_id_type=MESH)
            cp.start()
            copies.append(cp)
        for cp in copies:
            cp.wait()
        tot = buf[0]
        for k in range(1, NDEV):
            tot = tot + buf[k]
        o_ref[...] = tot

    return pl.pallas_call(
        body, name=name, out_shape=jax.ShapeDtypeStruct((1, PACK), F32),
        scratch_shapes=[pltpu.VMEM((NDEV, 1, PACK), F32), pltpu.SemaphoreType.DMA((NDEV - 1,)),
                        pltpu.SemaphoreType.DMA((NDEV - 1,))],
        compiler_params=pltpu.CompilerParams(vmem_limit_bytes=VMEM_LIMIT),
    )(packed)


def _small_step(name, total, small_w, small_m, small_v):
    names = ["ln1_g", "ln1_b", "ln2_g", "ln2_b", "ln3_g", "ln3_b", "b_in", "attn_sinks", "hgrn_lb_logits", "hgrn_norm_g"]
    np_ = len(names)

    def body(*refs):
        t_ref = refs[0]
        w_refs = refs[1:1 + np_]
        m_refs = refs[1 + np_:1 + 2 * np_]
        v_refs = refs[1 + 2 * np_:1 + 3 * np_]
        loss_ref, outs = refs[1 + 3 * np_], refs[2 + 3 * np_:]

        def part(n):
            o, w = _SMALL_OFF[n]
            return t_ref[:, o:o + w]

        loss_ref[...] = part("loss")
        for i, n in enumerate(names):
            w = w_refs[i][...]
            if n == "hgrn_lb_logits":
                m_ = jnp.maximum(w[0:1, :], w[1:2, :])
                e0, e1 = jnp.exp(w[0:1, :] - m_), jnp.exp(w[1:2, :] - m_)
                p0 = e0 / (e0 + e1)
                t = p0 * (1.0 - p0) * part("lb")
                g = jnp.concatenate([t, -t], axis=0)
            elif n == "attn_sinks":
                g = part(n)[:, :N_Q_HEADS]
            else:
                g = part(n)
            d, mm, vv = _adam(w, g, m_refs[i][...], v_refs[i][...])
            outs[4 * i][...] = g
            outs[4 * i + 1][...] = d
            outs[4 * i + 2][...] = mm
            outs[4 * i + 3][...] = vv

    out_shape = [jax.ShapeDtypeStruct((1, 128), F32)]
    for n in names:
        out_shape += [jax.ShapeDtypeStruct(small_w[n].shape, F32)] * 4
    return pl.pallas_call(
        body, name=name, out_shape=out_shape,
        compiler_params=pltpu.CompilerParams(vmem_limit_bytes=VMEM_LIMIT),
    )(total, *[small_w[n] for n in names], *[small_m[n] for n in names], *[small_v[n] for n in names]), names


def _tile(T, pref):
    return min(T, pref)


def kernel(x, ln1_g, ln1_b, ffn1_w1, ffn1_w3, ffn1_w2, ln2_g, ln2_b, w_in, b_in, attn_sinks, hgrn_lb_logits, hgrn_norm_g, w_proj_attn, w_proj_hgrn, w_out, ln3_g, ln3_b, ffn2_w1, ffn2_w3, ffn2_w2, loss_target, m_ln1_g, m_ln1_b, m_ffn1_w1, m_ffn1_w3, m_ffn1_w2, m_ln2_g, m_ln2_b, m_w_in, m_b_in, m_attn_sinks, m_hgrn_lb_logits, m_hgrn_norm_g, m_w_proj_attn, m_w_proj_hgrn, m_w_out, m_ln3_g, m_ln3_b, m_ffn2_w1, m_ffn2_w3, m_ffn2_w2, v_ln1_g, v_ln1_b, v_ffn1_w1, v_ffn1_w3, v_ffn1_w2, v_ln2_g, v_ln2_b, v_w_in, v_b_in, v_attn_sinks, v_hgrn_lb_logits, v_hgrn_norm_g, v_w_proj_attn, v_w_proj_hgrn, v_w_out, v_ln3_g, v_ln3_b, v_ffn2_w1, v_ffn2_w3, v_ffn2_w2):
    T = x.shape[1]
    xs = x[0]
    tgt = loss_target[0]
    tm = _tile(T, 256)
    tm2 = _tile(T, 512)
    tk = _tile(T, 2048)

    t_bf = lambda w: w[0].T.astype(BF16)
    n_bf = lambda w: w[0].astype(BF16)
    ffn_shard = lambda w1, w3, w2: jnp.concatenate([t_bf(w1), t_bf(w3), n_bf(w2)], axis=0)
    mix_shard = jnp.concatenate([t_bf(w_in), n_bf(w_proj_attn), n_bf(w_proj_hgrn), n_bf(w_out)], axis=0)
    (ffn1_all,) = _exchange_call("gather_ffn1", _gather_exchange(ffn_shard(ffn1_w1, ffn1_w3, ffn1_w2)))
    ffn_offs = (0, RF, 2 * RF)
    ropes = _rope_tables(T)

    (xh1, r1, a1, b1, xb0), (mix_all,) = _ffn_fwd("ffn1_fwd", xs, None, ffn1_all, ffn_offs, tm,
                                                  carry=_gather_exchange(mix_shard))
    (qkv, kt, vt, hg, gates, y1b), (ffn2_all,) = _inproj_fwd(
        "inproj_fwd", xh1, ln1_g, ln1_b, mix_all, b_in, ropes, tm2,
        carry=_gather_exchange(ffn_shard(ffn2_w1, ffn2_w3, ffn2_w2)))
    ya = _attn_fwd("attn_fwd", qkv, vt, attn_sinks)
    yh, sstart = _hgrn_fwd("hgrn_fwd", hg, hgrn_lb_logits, hgrn_norm_g, HG_FWD)
    xh2, r2, pa, ph, merged = _mix_fwd("mix_fwd", ya, yh, gates, xh1, ln1_g, ln1_b, mix_all, tm2)
    (xh3, r3, a2, b2, y2b, loss_part), _ = _ffn_fwd("ffn2_fwd", xh2, (ln2_g, ln2_b), ffn2_all, ffn_offs, tm,
                                                    loss=(ln3_g, ln3_b, tgt))

    (dy2, dab2, u2, df2, dg3, db3), _ = _ffn_bwd("ffn2_bwd", ("loss", ln3_b, tgt), xh3, r3, ln3_g, a2, b2, ffn2_all,
                                                 ffn_offs, tm)
    g_ffn2_13, _ = _wgrad("wgrad_ffn2_w13", dab2, y2b, DFF // 2, tk)
    g_ffn2_2, _ = _wgrad("wgrad_ffn2_w2", u2, df2, DFF // 2, tk)
    dz2, dmix, dpa, dph, dga, dgh, dya, dyh, dg2, db2 = _mix_bwd("mix_bwd", dy2, xh2, r2, ln2_g, gates, pa, ph,
                                                                 mix_all, tm2)
    g_wo, _ = _wgrad("wgrad_w_out", merged, dmix, D, tk)
    g_pa, _ = _wgrad("wgrad_w_pa", ya, dpa, D, tk)
    g_ph, _ = _wgrad("wgrad_w_ph", yh, dph, D, tk)
    recv = {}
    (dq, dk, dv, dsink), (recv["ffn2_w1"], recv["ffn2_w3"], recv["ffn2_w2"]) = _attn_bwd(
        "attn_bwd", qkv, kt, attn_sinks, dya,
        carry=_grad_exchange([g_ffn2_13, g_ffn2_13, g_ffn2_2], [0, DFF, 0], [RF, RF, RF]))
    (dfl, dqh, dih, dog, dlb, dng), (recv["w_proj_attn"], recv["w_proj_hgrn"], recv["w_out"]) = _hgrn_bwd(
        "hgrn_bwd", hg, hgrn_lb_logits, hgrn_norm_g, sstart, dyh, HG_BWD,
        carry=_grad_exchange([g_pa, g_ph, g_wo], [0, 0, 0], [RP, RP, RP]))
    dy1, dproj, dbin = _inproj_bwd("inproj_bwd", dq, dk, dv, (dfl, dqh, dih, dog), (dga, dgh), dz2, mix_all, ropes, tm)
    g_win, _ = _wgrad("wgrad_w_in", dproj, y1b, DIN // 4, tk)
    (gx, dab1, u1, df1, dg1, db1), (recv["w_in"],) = _ffn_bwd(
        "ffn1_bwd", ("dy", dy1), xh1, r1, ln1_g, a1, b1, ffn1_all, ffn_offs, tm,
        carry=_grad_exchange([g_win], [0], [RIN]))
    g_ffn1_2, _ = _wgrad("wgrad_ffn1_w2", u1, df1, DFF // 2, tk)
    g_ffn1_1, (recv["ffn1_w2"],) = _wgrad("wgrad_ffn1_w1", dab1, xb0, DFF // 2, tk, 0, DFF,
                                          carry=_grad_exchange([g_ffn1_2], [0], [RF]))
    g_ffn1_3, (recv["ffn1_w1"],) = _wgrad("wgrad_ffn1_w3", dab1, xb0, DFF // 2, tk, DFF, DFF,
                                          carry=_grad_exchange([g_ffn1_1], [0], [RF]))
    (recv["ffn1_w3"],) = _exchange_call("exchange_ffn1_w3", _grad_exchange([g_ffn1_3], [0], [RF]))

    big = [("ffn1_w1", ffn1_w1, m_ffn1_w1, v_ffn1_w1, True), ("ffn1_w3", ffn1_w3, m_ffn1_w3, v_ffn1_w3, True),
           ("ffn1_w2", ffn1_w2, m_ffn1_w2, v_ffn1_w2, False), ("w_in", w_in, m_w_in, v_w_in, True),
           ("w_proj_attn", w_proj_attn, m_w_proj_attn, v_w_proj_attn, False),
           ("w_proj_hgrn", w_proj_hgrn, m_w_proj_hgrn, v_w_proj_hgrn, False),
           ("w_out", w_out, m_w_out, v_w_out, False),
           ("ffn2_w1", ffn2_w1, m_ffn2_w1, v_ffn2_w1, True), ("ffn2_w3", ffn2_w3, m_ffn2_w3, v_ffn2_w3, True),
           ("ffn2_w2", ffn2_w2, m_ffn2_w2, v_ffn2_w2, False)]
    res = {}
    for n, w, m, v, transposed in big:
        view = (lambda t: t[0].T) if transposed else (lambda t: t[0])
        back = (lambda t: t.T[None]) if transposed else (lambda t: t[None])
        res[n] = tuple(back(t) for t in _grad_step("step_" + n, recv[n], view(w), view(m), view(v)))

    parts = {"ln1_g": dg1, "ln1_b": db1, "ln2_g": dg2, "ln2_b": db2, "ln3_g": dg3, "ln3_b": db3, "b_in": dbin,
             "lb": dlb, "attn_sinks": dsink, "hgrn_norm_g": jnp.sum(dng, axis=0), "loss": loss_part[0:1, :]}
    packed = jnp.concatenate([parts[n] for n, _ in _SMALL], axis=1)
    small_w = dict(ln1_g=ln1_g, ln1_b=ln1_b, ln2_g=ln2_g, ln2_b=ln2_b, ln3_g=ln3_g, ln3_b=ln3_b, b_in=b_in,
                   attn_sinks=attn_sinks, hgrn_lb_logits=hgrn_lb_logits, hgrn_norm_g=hgrn_norm_g)
    small_m = dict(ln1_g=m_ln1_g, ln1_b=m_ln1_b, ln2_g=m_ln2_g, ln2_b=m_ln2_b, ln3_g=m_ln3_g, ln3_b=m_ln3_b,
                   b_in=m_b_in, attn_sinks=m_attn_sinks, hgrn_lb_logits=m_hgrn_lb_logits, hgrn_norm_g=m_hgrn_norm_g)
    small_v = dict(ln1_g=v_ln1_g, ln1_b=v_ln1_b, ln2_g=v_ln2_g, ln2_b=v_ln2_b, ln3_g=v_ln3_g, ln3_b=v_ln3_b,
                   b_in=v_b_in, attn_sinks=v_attn_sinks, hgrn_lb_logits=v_hgrn_lb_logits, hgrn_norm_g=v_hgrn_norm_g)
    outs, names = _small_step("small_step", _small_reduce("small_reduce", packed), small_w, small_m, small_v)
    loss = outs[0][0, 0]
    for i, n in enumerate(names):
        res[n] = tuple(outs[1 + 4 * i:5 + 4 * i])

    order = ["ln1_g", "ln1_b", "ffn1_w1", "ffn1_w3", "ffn1_w2", "ln2_g", "ln2_b", "w_in", "b_in", "attn_sinks",
             "hgrn_lb_logits", "hgrn_norm_g", "w_proj_attn", "w_proj_hgrn", "w_out", "ln3_g", "ln3_b",
             "ffn2_w1", "ffn2_w3", "ffn2_w2"]
    return (loss, gx[None], *[res[n][0] for n in order], *[res[n][1] for n in order],
            *[res[n][2] for n in order], *[res[n][3] for n in order])
```

```python
import jax
import jax.numpy as jnp
from jax import lax
from jax.experimental import pallas as pl
from jax.experimental.pallas import tpu as pltpu

F32 = jnp.float32
BF16 = jnp.bfloat16

NDEV = 8
D = 1024
DFF = 2816
RF = DFF // NDEV
DIN = 7680
RIN = DIN // NDEV
RP = D // NDEV
N_Q_HEADS = 16
N_KV_HEADS = 4
HEAD_DIM = 64
ATTN_BLOCK = 128
ROPE_THETA = 500000.0
ROPE_DIM = HEAD_DIM // 4
HGRN_HEADS = 8
HGRN_DK = 128
HGRN_CHUNK = 64
ALPHA = 2.0 ** 0.25
LN_EPS = 1e-5
RMS_EPS = 1e-6
NEG_INF = -1e30
ADAM_LR = 0.001
ADAM_B1 = 0.9
ADAM_B2 = 0.999
ADAM_EPS = 1e-08
ADAM_WD = 0.01
ADAM_STEP = 10

QKV_W = 1536
HG_W = 4096
GATE_W = 2048
VMEM_LIMIT = 60 * 2 ** 20
PART_ROWS = 256

MESH = pl.DeviceIdType.MESH
HBM_SPEC = pl.BlockSpec(memory_space=pltpu.HBM)


def _params(*sem):
    return pltpu.CompilerParams(dimension_semantics=sem, vmem_limit_bytes=VMEM_LIMIT)


def _dot(a, b):
    return jnp.dot(a, b, preferred_element_type=F32)


def _dot_nt(a, b):
    return lax.dot_general(a, b, (((1,), (1,)), ((), ())), preferred_element_type=F32)


def _dot_tn(a, b):
    return lax.dot_general(a, b, (((0,), (0,)), ((), ())), preferred_element_type=F32)


def _sigmoid(x):
    return 0.5 * jnp.tanh(0.5 * x) + 0.5


def _ln_fwd(z):
    mu = jnp.mean(z, axis=-1, keepdims=True)
    zc = z - mu
    var = jnp.mean(zc * zc, axis=-1, keepdims=True)
    r = lax.rsqrt(var + LN_EPS)
    return zc * r, r


def _ln_bwd(dy, xh, r, g):
    dxh = dy * g
    m1 = jnp.mean(dxh, axis=-1, keepdims=True)
    m2 = jnp.mean(dxh * xh, axis=-1, keepdims=True)
    dz = r * (dxh - m1 - xh * m2)
    return dz, jnp.sum(dy * xh, axis=0, keepdims=True), jnp.sum(dy, axis=0, keepdims=True)


def _load_rows(wall_ref, pieces, sem):
    copies = []
    for dst, off, r in pieces:
        for k in range(NDEV):
            c = pltpu.make_async_copy(wall_ref.at[k, pl.ds(off, r), :], dst.at[pl.ds(k * r, r), :], sem.at[len(copies)])
            c.start()
            copies.append(c)
    for c in copies:
        c.wait()


class _Exchange:
    def __init__(self, inputs, out_shape, scratch, begin, middle, end):
        self.inputs, self.out_shape, self.scratch = inputs, out_shape, scratch
        self.begin, self.middle, self.end = begin, middle, end


def _gather_exchange(shard):
    rows, cols = shard.shape

    def ops(ins, outs, scr):
        (x_ref,), (out_ref,), (send_sems, recv_sems, local_sem) = ins, outs, scr
        x, y, c = lax.axis_index("x"), lax.axis_index("y"), lax.axis_index("c")
        me, sibling = (x, y, c), (x, y, 1 - c)
        chips = [(1 - x, y), (x, 1 - y), (1 - x, 1 - y)]

        def slot(px, py, pc):
            return out_ref.at[4 * px + 2 * py + pc]

        def copy(k, block, to, src=None):
            return pltpu.make_async_remote_copy(
                src_ref=slot(*block) if src is None else src, dst_ref=slot(*block),
                send_sem=send_sems.at[k], recv_sem=recv_sems.at[k], device_id=to, device_id_type=MESH)

        mine = lambda: pltpu.make_async_copy(x_ref, slot(*me), local_sem)
        first = lambda: [copy(0, me, sibling, src=x_ref)] + [
            copy(1 + j, me, (*chip, c), src=x_ref) for j, chip in enumerate(chips)]
        passed = lambda: [copy(4 + j, (*chip, c), sibling) for j, chip in enumerate(chips)]
        return c, me, sibling, chips, copy, mine, first, passed

    def begin(*refs):
        _, _, _, _, _, mine, first, _ = ops(*refs)
        mine().start()
        for cp in first():
            cp.start()

    def middle(*refs):
        c, me, _, chips, copy, _, _, passed = ops(*refs)
        for (j, chip), fwd in zip(enumerate(chips), passed()):
            copy(1 + j, (*chip, c), me).wait_recv()
            fwd.start()

    def end(*refs):
        c, me, sibling, chips, copy, mine, first, passed = ops(*refs)
        copy(0, sibling, me).wait_recv()
        for j, chip in enumerate(chips):
            copy(4 + j, (*chip, 1 - c), me).wait_recv()
        for cp in first() + passed():
            cp.wait_send()
        mine().wait()

    return _Exchange([shard], [jax.ShapeDtypeStruct((NDEV, rows, cols), shard.dtype)],
                     [pltpu.SemaphoreType.DMA((7,)), pltpu.SemaphoreType.DMA((7,)), pltpu.SemaphoreType.DMA],
                     begin, middle, end)


def _grad_exchange(grads, bases, rows):
    n = len(grads)

    def copies(g_refs, out_refs, scr):
        send_sems, recv_sems, local_sems = scr
        x, y, c = lax.axis_index("x"), lax.axis_index("y"), lax.axis_index("c")
        me = 4 * x + 2 * y + c
        out = []
        for i in range(n):
            r = rows[i]
            src = lambda k: g_refs[i].at[pl.ds(pl.multiple_of(bases[i] + k * r, 16), r), :]
            out.append(pltpu.make_async_copy(src(me), out_refs[i].at[me], local_sems.at[i]))
            for j in range(1, NDEV):
                px, py, pc = x ^ (j >> 2), y ^ ((j >> 1) & 1), c ^ (j & 1)
                out.append(pltpu.make_async_remote_copy(
                    src_ref=src(4 * px + 2 * py + pc), dst_ref=out_refs[i].at[me],
                    send_sem=send_sems.at[i, j - 1], recv_sem=recv_sems.at[i, j - 1],
                    device_id=(px, py, pc), device_id_type=MESH))
        return out

    def begin(*refs):
        for cp in copies(*refs):
            cp.start()

    def end(*refs):
        for cp in copies(*refs):
            cp.wait()

    return _Exchange(list(grads), [jax.ShapeDtypeStruct((NDEV, r, D), g.dtype) for g, r in zip(grads, rows)],
                     [pltpu.SemaphoreType.DMA((n, NDEV - 1)), pltpu.SemaphoreType.DMA((n, NDEV - 1)),
                      pltpu.SemaphoreType.DMA((n,))], begin, None, end)


def _exchange_call(name, ex):
    ni, no = len(ex.inputs), len(ex.out_shape)

    def body(*refs):
        parts = (refs[:ni], refs[ni:ni + no], refs[ni + no:])
        ex.begin(*parts)
        if ex.middle is not None:
            ex.middle(*parts)
        ex.end(*parts)

    return pl.pallas_call(body, name=name, out_shape=ex.out_shape, in_specs=[HBM_SPEC] * ni, out_specs=[HBM_SPEC] * no,
                          scratch_shapes=ex.scratch)(*ex.inputs)


def _call(body, *, name, grid, in_specs, out_specs, out_shape, scratch_shapes, ins, carry=None):
    sem = ("arbitrary",) * len(grid)
    if carry is None:
        outs = pl.pallas_call(body, name=name, grid=grid, in_specs=in_specs, out_specs=out_specs, out_shape=out_shape,
                              scratch_shapes=scratch_shapes, compiler_params=_params(*sem))(*ins)
        return outs, None
    n_in, n_out, n_scr = len(ins), len(out_shape), len(scratch_shapes)
    ci, co = len(carry.inputs), len(carry.out_shape)
    total = 1
    for g in grid:
        total *= g

    def wrapped(*refs):
        own_in, ex_in = refs[:n_in], refs[n_in:n_in + ci]
        o0 = n_in + ci
        own_out, ex_out = refs[o0:o0 + n_out], refs[o0 + n_out:o0 + n_out + co]
        s0 = o0 + n_out + co
        own_scr, ex_scr = refs[s0:s0 + n_scr], refs[s0 + n_scr:]
        step = pl.program_id(0)
        for d in range(1, len(grid)):
            step = step * grid[d] + pl.program_id(d)
        parts = (ex_in, ex_out, ex_scr)
        pl.when(step == 0)(lambda: carry.begin(*parts))
        body(*own_in, *own_out, *own_scr)
        if carry.middle is not None:
            pl.when(step == total // 2)(lambda: carry.middle(*parts))
        pl.when(step == total - 1)(lambda: carry.end(*parts))

    outs = pl.pallas_call(
        wrapped, name=name, grid=grid, in_specs=list(in_specs) + [HBM_SPEC] * ci,
        out_specs=list(out_specs) + [HBM_SPEC] * co, out_shape=list(out_shape) + list(carry.out_shape),
        scratch_shapes=list(scratch_shapes) + list(carry.scratch), compiler_params=_params(*sem),
    )(*ins, *carry.inputs)
    return outs[:n_out], outs[n_out:]


def _ffn_fwd(name, xin, affine, wall, offs, tm, loss=None, carry=None):
    T = xin.shape[0]
    nt = T // tm

    def body(*refs):
        it = iter(refs)
        x_ref = next(it)
        if affine is not None:
            g_ref, b_ref = next(it), next(it)
        wall_ref = next(it)
        if loss is not None:
            go_ref, bo_ref, tgt_ref = next(it), next(it), next(it)
        xh_ref, r_ref, a_ref, b2_ref, yb_ref = (next(it) for _ in range(5))
        if loss is not None:
            loss_ref = next(it)
        w1, w3, w2, sem = (next(it) for _ in range(4))

        @pl.when(pl.program_id(0) == 0)
        def _():
            _load_rows(wall_ref, [(w1, offs[0], RF), (w3, offs[1], RF), (w2, offs[2], RF)], sem)
            if loss is not None:
                loss_ref[...] = jnp.zeros_like(loss_ref)

        parts = [slice(p, min(p + PART_ROWS, tm)) for p in range(0, tm, PART_ROWS)]
        ys, ybs = [], []
        for rs in parts:
            y = x_ref[rs, :]
            if affine is not None:
                y = y * g_ref[...] + b_ref[...]
            ys.append(y)
            ybs.append(y.astype(BF16))
            yb_ref[rs, :] = ybs[-1]
        ab = [(_dot_nt(yb, w1[...]).astype(BF16), _dot_nt(yb, w3[...]).astype(BF16)) for yb in ybs]
        us = []
        for rs, (a, b) in zip(parts, ab):
            a_ref[rs, :] = a
            b2_ref[rs, :] = b
            af, bf = a.astype(F32), b.astype(F32)
            us.append((af * _sigmoid(af) * bf).astype(BF16))
        fs = [_dot(u, w2[...]) for u in us]
        for rs, y, f in zip(parts, ys, fs):
            xh, r = _ln_fwd(ALPHA * y + 0.5 * f)
            xh_ref[rs, :] = xh
            r_ref[rs, :] = jnp.broadcast_to(r, (xh.shape[0], 128))
            if loss is not None:
                e = xh * go_ref[...] + bo_ref[...] - tgt_ref[rs, :]
                loss_ref[...] += jnp.sum(e * e) * (0.5 / D)

    row = lambda w: pl.BlockSpec((tm, w), lambda i: (i, 0))
    vec = pl.BlockSpec((1, D), lambda i: (0, 0))
    ins, in_specs = [xin], [row(D)]
    if affine is not None:
        ins += list(affine)
        in_specs += [vec, vec]
    ins.append(wall)
    in_specs.append(HBM_SPEC)
    if loss is not None:
        ins += list(loss)
        in_specs += [vec, vec, row(D)]
    out_shape = [jax.ShapeDtypeStruct((T, D), F32), jax.ShapeDtypeStruct((T, 128), F32),
                 jax.ShapeDtypeStruct((T, DFF), BF16), jax.ShapeDtypeStruct((T, DFF), BF16),
                 jax.ShapeDtypeStruct((T, D), BF16)]
    out_specs = [row(D), row(128), row(DFF), row(DFF), row(D)]
    if loss is not None:
        out_shape.append(jax.ShapeDtypeStruct((8, 128), F32))
        out_specs.append(pl.BlockSpec((8, 128), lambda i: (0, 0)))
    return _call(body, name=name, grid=(nt,), in_specs=in_specs, out_specs=out_specs, out_shape=out_shape,
                 scratch_shapes=[pltpu.VMEM((DFF, D), BF16)] * 3 + [pltpu.SemaphoreType.DMA((3 * NDEV,))],
                 ins=ins, carry=carry)


def _ffn_bwd(name, dy_src, xh, r, g, a, b, wall, offs, tm, carry=None):
    T = xh.shape[0]
    nt = T // tm
    from_loss = dy_src[0] == "loss"

    def body(*refs):
        it = iter(refs)
        if from_loss:
            bo_ref, tgt_ref = next(it), next(it)
        else:
            dy_ref = next(it)
        xh_ref, r_ref, g_ref, a_ref, b_ref, wall_ref = (next(it) for _ in range(6))
        dyin_ref, dab_ref, u_ref, df_ref, dg_ref, db_ref = (next(it) for _ in range(6))
        w1, w3, w2, sem = (next(it) for _ in range(4))

        @pl.when(pl.program_id(0) == 0)
        def _():
            _load_rows(wall_ref, [(w1, offs[0], RF), (w3, offs[1], RF), (w2, offs[2], RF)], sem)
            dg_ref[...] = jnp.zeros_like(dg_ref)
            db_ref[...] = jnp.zeros_like(db_ref)

        parts = [slice(p, min(p + PART_ROWS, tm)) for p in range(0, tm, PART_ROWS)]
        gv = g_ref[...]
        dz, df = [], []
        for rs in parts:
            xhv = xh_ref[rs, :]
            if from_loss:
                dy = (xhv * gv + bo_ref[...] - tgt_ref[rs, :]) * (1.0 / D)
            else:
                dy = dy_ref[rs, :]
            dzp, dgp, dbp = _ln_bwd(dy, xhv, r_ref[rs, :1], gv)
            dg_ref[...] += dgp
            db_ref[...] += dbp
            dz.append(dzp)
            df.append((0.5 * dzp).astype(BF16))
            df_ref[rs, :] = df[-1]
        du = [_dot_nt(d, w2[...]) for d in df]
        da, dbb = [], []
        for rs, dup in zip(parts, du):
            af, bf = a_ref[rs, :].astype(F32), b_ref[rs, :].astype(F32)
            s = _sigmoid(af)
            sl = af * s
            u_ref[rs, :] = (sl * bf).astype(BF16)
            da.append((dup * bf * (s * (1.0 + af * (1.0 - s)))).astype(BF16))
            dbb.append((dup * sl).astype(BF16))
            dab_ref[rs, :DFF] = da[-1]
            dab_ref[rs, DFF:] = dbb[-1]
        for rs, dzp, dap, dbp in zip(parts, dz, da, dbb):
            dyin_ref[rs, :] = ALPHA * dzp + _dot(dap, w1[...]) + _dot(dbp, w3[...])

    row = lambda w: pl.BlockSpec((tm, w), lambda i: (i, 0))
    vec = pl.BlockSpec((1, D), lambda i: (0, 0))
    if from_loss:
        ins, in_specs = [dy_src[1], dy_src[2]], [vec, row(D)]
    else:
        ins, in_specs = [dy_src[1]], [row(D)]
    ins += [xh, r, g, a, b, wall]
    in_specs += [row(D), row(128), vec, row(DFF), row(DFF), HBM_SPEC]
    return _call(
        body, name=name, grid=(nt,), in_specs=in_specs,
        out_specs=[row(D), row(2 * DFF), row(DFF), row(D), vec, vec],
        out_shape=[jax.ShapeDtypeStruct((T, D), F32), jax.ShapeDtypeStruct((T, 2 * DFF), BF16),
                   jax.ShapeDtypeStruct((T, DFF), BF16), jax.ShapeDtypeStruct((T, D), BF16),
                   jax.ShapeDtypeStruct((1, D), F32), jax.ShapeDtypeStruct((1, D), F32)],
        scratch_shapes=[pltpu.VMEM((DFF, D), BF16)] * 3 + [pltpu.SemaphoreType.DMA((3 * NDEV,))],
        ins=ins, carry=carry)


def _wgrad(name, a, b, bn, tk, col0=0, ncols=None, carry=None):
    T = a.shape[0]
    N = a.shape[1] if ncols is None else ncols
    nk = T // tk
    c0 = col0 // bn

    def body(a_ref, b_ref, o_ref, acc):
        k = pl.program_id(1)

        @pl.when(k == 0)
        def _():
            acc[...] = jnp.zeros_like(acc)

        acc[...] += _dot_tn(a_ref[...], b_ref[...])

        @pl.when(k == nk - 1)
        def _():
            o_ref[...] = acc[...].astype(BF16)

    (out,), ex = _call(
        body, name=name, grid=(N // bn, nk),
        in_specs=[pl.BlockSpec((tk, bn), lambda n, k: (k, n + c0)), pl.BlockSpec((tk, D), lambda n, k: (k, 0))],
        out_specs=[pl.BlockSpec((bn, D), lambda n, k: (n, 0))],
        out_shape=[jax.ShapeDtypeStruct((N, D), BF16)],
        scratch_shapes=[pltpu.VMEM((bn, D), F32)], ins=[a, b], carry=carry)
    return out, ex


def _rope_tables(T):
    pos = jnp.arange(T, dtype=F32)
    inv_freq = ROPE_THETA ** (-jnp.arange(0, ROPE_DIM, 2, dtype=F32) / ROPE_DIM)
    half = ROPE_DIM // 2
    ch = jnp.arange(128) % HEAD_DIM
    ang = pos[:, None] * inv_freq[ch % half][None, :]
    cos, sin = jnp.cos(ang), jnp.sin(ang)
    first, second = (ch < half)[None, :], ((ch >= half) & (ch < ROPE_DIM))[None, :]
    return (jnp.where(first | second, cos, 1.0), jnp.where(first, -sin, 0.0), jnp.where(second, sin, 0.0))


def _rope(t, c, s1, s2):
    n = t.shape[1] // 128
    ct, s1t, s2t = (jnp.tile(v, (1, n)) for v in (c, s1, s2))
    w = t.shape[1]
    return t * ct + pltpu.roll(t, w - 8, 1) * s1t + pltpu.roll(t, 8, 1) * s2t


def _rope_t(dr, c, s1, s2):
    n = dr.shape[1] // 128
    ct, s1t, s2t = (jnp.tile(v, (1, n)) for v in (c, s1, s2))
    w = dr.shape[1]
    return dr * ct + pltpu.roll(dr * s1t, 8, 1) + pltpu.roll(dr * s2t, w - 8, 1)


_Q, _K, _V = (0, 1024), (1024, 256), (1280, 256)
_HG = (1536, HG_W)
_GATES = (5632, GATE_W)


def _inproj_fwd(name, xh, g, b, wall, b_in, ropes, tm, carry=None):
    T = xh.shape[0]

    def body(xh_ref, g_ref, b_ref, wall_ref, bin_ref, c_ref, s1_ref, s2_ref,
             qkv_ref, kt_ref, vt_ref, hg_ref, gate_ref, yb_ref, w, sem):
        @pl.when(pl.program_id(0) == 0)
        def _():
            _load_rows(wall_ref, [(w, 0, RIN)], sem)

        yb = (xh_ref[...] * g_ref[...] + b_ref[...]).astype(BF16)
        yb_ref[...] = yb
        c, s1, s2 = c_ref[...], s1_ref[...], s2_ref[...]

        def piece(start, width):
            return _dot_nt(yb, w[start:start + width, :]) + bin_ref[:, start:start + width]

        k = _rope(piece(*_K), c, s1, s2)
        v = piece(*_V)
        qkv_ref[:, 0:1024] = _rope(piece(*_Q), c, s1, s2).astype(BF16)
        qkv_ref[:, 1024:1280] = k.astype(BF16)
        qkv_ref[:, 1280:1536] = v.astype(BF16)
        kt_ref[...] = k.T.astype(BF16)
        vt_ref[...] = v.T.astype(BF16)
        for j in range(4):
            hg_ref[:, 1024 * j:1024 * (j + 1)] = piece(_HG[0] + 1024 * j, 1024)
        for j in range(2):
            gate_ref[:, 1024 * j:1024 * (j + 1)] = piece(_GATES[0] + 1024 * j, 1024)

    row = lambda wd: pl.BlockSpec((tm, wd), lambda i: (i, 0))
    vec = lambda wd: pl.BlockSpec((1, wd), lambda i: (0, 0))
    colt = pl.BlockSpec((256, tm), lambda i: (0, i))
    return _call(
        body, name=name, grid=(T // tm,),
        in_specs=[row(D), vec(D), vec(D), HBM_SPEC, vec(DIN), row(128), row(128), row(128)],
        out_specs=[row(QKV_W), colt, colt, row(HG_W), row(GATE_W), row(D)],
        out_shape=[jax.ShapeDtypeStruct((T, QKV_W), BF16), jax.ShapeDtypeStruct((256, T), BF16),
                   jax.ShapeDtypeStruct((256, T), BF16), jax.ShapeDtypeStruct((T, HG_W), F32),
                   jax.ShapeDtypeStruct((T, GATE_W), F32), jax.ShapeDtypeStruct((T, D), BF16)],
        scratch_shapes=[pltpu.VMEM((DIN, D), BF16), pltpu.SemaphoreType.DMA((NDEV,))],
        ins=[xh, g, b, wall, b_in, *ropes], carry=carry)


def _inproj_bwd(name, dq, dk, dv, dhg, dgates, dz2, wall, ropes, tm):
    T = dq.shape[0]

    def body(dq_ref, dk_ref, dv_ref, d0, d1, d2, d3, dga_ref, dgh_ref, dz_ref, wall_ref, c_ref, s1_ref, s2_ref,
             dy_ref, dproj_ref, dbin_ref, w, sem):
        @pl.when(pl.program_id(0) == 0)
        def _():
            _load_rows(wall_ref, [(w, 0, RIN)], sem)
            dbin_ref[...] = jnp.zeros_like(dbin_ref)

        c, s1, s2 = c_ref[...], s1_ref[...], s2_ref[...]
        acc = ALPHA * dz_ref[...]
        pieces = [(_Q[0], _rope_t(dq_ref[...], c, s1, s2)), (_K[0], _rope_t(dk_ref[...], c, s1, s2)),
                  (_V[0], dv_ref[...])]
        pieces += [(_HG[0] + 1024 * j, r[...]) for j, r in enumerate((d0, d1, d2, d3))]
        pieces += [(_GATES[0], dga_ref[...]), (_GATES[0] + 1024, dgh_ref[...])]
        for start, val in pieces:
            width = val.shape[1]
            dbin_ref[:, start:start + width] += jnp.sum(val.astype(F32), axis=0, keepdims=True)
            vb = val.astype(BF16)
            dproj_ref[:, start:start + width] = vb
            acc = acc + _dot(vb, w[start:start + width, :])
        dy_ref[...] = acc

    row = lambda wd: pl.BlockSpec((tm, wd), lambda i: (i, 0))
    return pl.pallas_call(
        body, name=name, grid=(T // tm,),
        in_specs=[row(D), row(256), row(256)] + [row(D)] * 4 + [row(D), row(D), row(D), HBM_SPEC] + [row(128)] * 3,
        out_specs=[row(D), row(DIN), pl.BlockSpec((1, DIN), lambda i: (0, 0))],
        out_shape=[jax.ShapeDtypeStruct((T, D), F32), jax.ShapeDtypeStruct((T, DIN), BF16),
                   jax.ShapeDtypeStruct((1, DIN), F32)],
        scratch_shapes=[pltpu.VMEM((DIN, D), BF16), pltpu.SemaphoreType.DMA((NDEV,))],
        compiler_params=_params("arbitrary"),
    )(dq, dk, dv, *dhg, *dgates, dz2, wall, *ropes)


def _halves(t):
    lane = lax.broadcasted_iota(jnp.int32, t.shape, 1)
    low = lane < HEAD_DIM
    sw = pltpu.roll(t, HEAD_DIM, 1)
    zero = jnp.zeros_like(t)
    h0 = (jnp.where(low, t, zero), jnp.where(low, zero, sw))
    h1 = (jnp.where(low, sw, zero), jnp.where(low, zero, t))
    return h0, h1


def _lane_stack(p_ref, c_ref, gp):
    sl = slice(128 * gp, 128 * (gp + 1))
    hp, hc = _halves(p_ref[:, sl].astype(F32)), _halves(c_ref[:, sl].astype(F32))
    return [jnp.concatenate([hp[gg][0], hc[gg][0], hp[gg][1], hc[gg][1]], axis=0).astype(BF16) for gg in range(2)]


def _row_stack(tp_ref, tc_ref, g):
    band = jnp.concatenate([tp_ref[64 * g:64 * (g + 1), :], tc_ref[64 * g:64 * (g + 1), :]], axis=1)
    z = jnp.zeros_like(band)
    return [jnp.concatenate([band, z], axis=0), jnp.concatenate([z, band], axis=0)]


def _fold(prevm, t4, hh):
    return jnp.where(prevm, t4[256 * hh:256 * hh + 128, :], t4[256 * hh + 128:256 * hh + 256, :])


def _unfold(prevm, t):
    return jnp.concatenate([jnp.where(prevm, t, 0.0), jnp.where(prevm, 0.0, t)], axis=0).astype(BF16)


def _attn_softmax(s, kill, sink):
    s = jnp.where(kill, NEG_INF, s)
    m = jnp.maximum(jnp.max(s, axis=0, keepdims=True), sink)
    p = jnp.exp(s - m)
    es = jnp.exp(sink - m)
    denom = jnp.sum(p, axis=0, keepdims=True) + es
    return p / denom, es / denom


def _attn_masks(first):
    row = lax.broadcasted_iota(jnp.int32, (ATTN_BLOCK, 2 * ATTN_BLOCK), 0)
    lane = lax.broadcasted_iota(jnp.int32, (ATTN_BLOCK, 2 * ATTN_BLOCK), 1)
    prevm = row > lane % ATTN_BLOCK
    return prevm, jnp.logical_and(first, prevm)


def _pair_rows(ref, g):
    return jnp.concatenate([ref[:, 256 * g:256 * g + 128], ref[:, 256 * g + 128:256 * (g + 1)]], axis=0)


def _pair_sinks(sink_ref, g, hh):
    h0, h1 = 4 * g + hh, 4 * g + 2 + hh
    return jnp.concatenate([jnp.broadcast_to(sink_ref[:, h0:h0 + 1], (1, ATTN_BLOCK)),
                            jnp.broadcast_to(sink_ref[:, h1:h1 + 1], (1, ATTN_BLOCK))], axis=1)


def _attn_fwd(name, qkv, vt, sinks):
    T = qkv.shape[0]
    nb = T // ATTN_BLOCK
    scale = HEAD_DIM ** -0.5

    def body(q_ref, kp_ref, kc_ref, vtp_ref, vtc_ref, sink_ref, o_ref):
        prevm, kill = _attn_masks(pl.program_id(0) == 0)
        kst = _lane_stack(kp_ref, kc_ref, 0) + _lane_stack(kp_ref, kc_ref, 1)
        vts = [_row_stack(vtp_ref, vtc_ref, g) for g in range(N_KV_HEADS)]
        s8 = [_dot_nt(kst[g], _pair_rows(q_ref, g)) * scale for g in range(N_KV_HEADS)]
        pu = [[_unfold(prevm, _attn_softmax(_fold(prevm, s8[g], hh), kill, _pair_sinks(sink_ref, g, hh))[0])
               for hh in range(2)] for g in range(N_KV_HEADS)]
        for g in range(N_KV_HEADS):
            ot = _dot(vts[g][0], pu[g][0]) + _dot(vts[g][1], pu[g][1])
            o_ref[:, 256 * g:256 * g + 128] = ot[:, :ATTN_BLOCK].T.astype(BF16)
            o_ref[:, 256 * g + 128:256 * (g + 1)] = ot[:, ATTN_BLOCK:].T.astype(BF16)

    prev = lambda i: jnp.maximum(i - 1, 0)
    return pl.pallas_call(
        body, name=name, grid=(nb,),
        in_specs=[pl.BlockSpec((ATTN_BLOCK, D), lambda i: (i, 0)),
                  pl.BlockSpec((ATTN_BLOCK, 256), lambda i: (prev(i), 4)),
                  pl.BlockSpec((ATTN_BLOCK, 256), lambda i: (i, 4)),
                  pl.BlockSpec((256, ATTN_BLOCK), lambda i: (0, prev(i))),
                  pl.BlockSpec((256, ATTN_BLOCK), lambda i: (0, i)),
                  pl.BlockSpec((1, N_Q_HEADS), lambda i: (0, 0))],
        out_specs=pl.BlockSpec((ATTN_BLOCK, D), lambda i: (i, 0)),
        out_shape=jax.ShapeDtypeStruct((T, D), BF16),
        compiler_params=_params("arbitrary"),
    )(qkv, qkv, qkv, vt, vt, sinks)


def _attn_bwd(name, qkv, kt, sinks, dya, carry=None):
    T = qkv.shape[0]
    nb = T // ATTN_BLOCK
    scale = HEAD_DIM ** -0.5

    def body(q_ref, kp_ref, kc_ref, vp_ref, vc_ref, ktp_ref, ktc_ref, sink_ref, do_ref,
             dq_ref, dk_ref, dv_ref, ds_ref):
        i = pl.program_id(0)

        @pl.when(i == 0)
        def _():
            ds_ref[...] = jnp.zeros_like(ds_ref)

        prevm, kill = _attn_masks(i == 0)
        kst = _lane_stack(kp_ref, kc_ref, 0) + _lane_stack(kp_ref, kc_ref, 1)
        vst = _lane_stack(vp_ref, vc_ref, 0) + _lane_stack(vp_ref, vc_ref, 1)
        kts = [_row_stack(ktp_ref, ktc_ref, g) for g in range(N_KV_HEADS)]
        lane = lax.broadcasted_iota(jnp.int32, (2 * ATTN_BLOCK, 128), 1)
        low = lane < HEAD_DIM
        slane = lax.broadcasted_iota(jnp.int32, (1, 128), 1)
        dkz = [jnp.zeros((2 * ATTN_BLOCK, 128), F32) for _ in range(N_KV_HEADS)]
        dvz = [jnp.zeros((2 * ATTN_BLOCK, 128), F32) for _ in range(N_KV_HEADS)]
        dsink = jnp.zeros((1, 128), F32)
        groups = range(N_KV_HEADS)
        qcat = [_pair_rows(q_ref, g) for g in groups]
        docat = [_pair_rows(do_ref, g) for g in groups]
        s8 = [_dot_nt(kst[g], qcat[g]) * scale for g in groups]
        dp8 = [_dot_nt(vst[g], docat[g]) for g in groups]
        ds_u = [[None, None] for _ in groups]
        p_u = [[None, None] for _ in groups]
        for g in groups:
            for hh in range(2):
                pn, ps = _attn_softmax(_fold(prevm, s8[g], hh), kill, _pair_sinks(sink_ref, g, hh))
                dp = _fold(prevm, dp8[g], hh)
                delta = jnp.sum(pn * dp, axis=0, keepdims=True)
                sd = ps * delta
                dsink = dsink + jnp.where(slane == 4 * g + hh, -jnp.sum(sd[:, :ATTN_BLOCK]), 0.0) \
                    + jnp.where(slane == 4 * g + 2 + hh, -jnp.sum(sd[:, ATTN_BLOCK:]), 0.0)
                ds_u[g][hh] = _unfold(prevm, pn * (dp - delta) * scale)
                p_u[g][hh] = _unfold(prevm, pn)
        for g in groups:
            dqt = _dot(kts[g][0], ds_u[g][0]) + _dot(kts[g][1], ds_u[g][1])
            dq_ref[:, 256 * g:256 * g + 128] = dqt[:, :ATTN_BLOCK].T
            dq_ref[:, 256 * g + 128:256 * (g + 1)] = dqt[:, ATTN_BLOCK:].T
            for hh in range(2):
                own = low if hh == 0 else jnp.logical_not(low)
                dk_h = jnp.where(own, _dot(ds_u[g][hh], qcat[g]), 0.0)
                dv_h = jnp.where(own, _dot(p_u[g][hh], docat[g]), 0.0)
                if hh != g % 2:
                    dk_h = pltpu.roll(dk_h, HEAD_DIM, 1)
                    dv_h = pltpu.roll(dv_h, HEAD_DIM, 1)
                dkz[g] = dkz[g] + dk_h
                dvz[g] = dvz[g] + dv_h
        ds_ref[...] += dsink
        cur = pl.ds(pl.multiple_of(i * ATTN_BLOCK, ATTN_BLOCK), ATTN_BLOCK)
        prv = pl.ds(pl.multiple_of(jnp.maximum(i - 1, 0) * ATTN_BLOCK, ATTN_BLOCK), ATTN_BLOCK)
        for gp in range(N_KV_HEADS // 2):
            cols = slice(128 * gp, 128 * (gp + 1))
            dkb = dkz[2 * gp] + dkz[2 * gp + 1]
            dvb = dvz[2 * gp] + dvz[2 * gp + 1]
            dk_ref[cur, cols] = dkb[ATTN_BLOCK:, :]
            dv_ref[cur, cols] = dvb[ATTN_BLOCK:, :]

            @pl.when(i > 0)
            def _():
                dk_ref[prv, cols] += dkb[:ATTN_BLOCK, :]
                dv_ref[prv, cols] += dvb[:ATTN_BLOCK, :]

    prev = lambda i: jnp.maximum(i - 1, 0)
    whole = lambda w: pl.BlockSpec((T, w), lambda i: (0, 0))
    return _call(
        body, name=name, grid=(nb,), scratch_shapes=[], ins=[qkv, qkv, qkv, qkv, qkv, kt, kt, sinks, dya], carry=carry,
        in_specs=[pl.BlockSpec((ATTN_BLOCK, D), lambda i: (i, 0)),
                  pl.BlockSpec((ATTN_BLOCK, 256), lambda i: (prev(i), 4)),
                  pl.BlockSpec((ATTN_BLOCK, 256), lambda i: (i, 4)),
                  pl.BlockSpec((ATTN_BLOCK, 256), lambda i: (prev(i), 5)),
                  pl.BlockSpec((ATTN_BLOCK, 256), lambda i: (i, 5)),
                  pl.BlockSpec((256, ATTN_BLOCK), lambda i: (0, prev(i))),
                  pl.BlockSpec((256, ATTN_BLOCK), lambda i: (0, i)),
                  pl.BlockSpec((1, N_Q_HEADS), lambda i: (0, 0)),
                  pl.BlockSpec((ATTN_BLOCK, D), lambda i: (i, 0))],
        out_specs=[pl.BlockSpec((ATTN_BLOCK, D), lambda i: (i, 0)), whole(256), whole(256),
                   pl.BlockSpec((1, 128), lambda i: (0, 0))],
        out_shape=[jax.ShapeDtypeStruct((T, D), F32), jax.ShapeDtypeStruct((T, 256), F32),
                   jax.ShapeDtypeStruct((T, 256), F32), jax.ShapeDtypeStruct((1, 128), F32)])


HG_FWD = (256, 8, 512)
HG_BWD = (128, 8, 512)


def _chunk_sum(tri, x):
    w = x.shape[1]
    h1 = x.astype(BF16)
    h2 = (x - h1.astype(F32)).astype(BF16)
    r = _dot(tri, jnp.concatenate([h1, h2], axis=1))
    return r[:, :w] + r[:, w:]


def _chunk_tri(n):
    ri = lax.broadcasted_iota(jnp.int32, (n, n), 0)
    ci = lax.broadcasted_iota(jnp.int32, (n, n), 1)
    return jnp.where((ri >= ci) & (ri // HGRN_CHUNK == ci // HGRN_CHUNK), 1.0, 0.0).astype(BF16)


def _chunks(t):
    return [t[HGRN_CHUNK * c:HGRN_CHUNK * (c + 1), :] for c in range(t.shape[0] // HGRN_CHUNK)]


def _lower_bound(lbl_ref):
    l0, l1 = lbl_ref[0:1, :], lbl_ref[1:2, :]
    m = jnp.maximum(l0, l1)
    e0, e1 = jnp.exp(l0 - m), jnp.exp(l1 - m)
    return e0 / (e0 + e1)


def _heads(t):
    return [t[:, 128 * h:128 * (h + 1)] for h in range(t.shape[1] // 128)]


def _per_head(fn, *wide):
    return jnp.concatenate([fn(*parts) for parts in zip(*[_heads(t) for t in wide])], axis=1)


def _hgrn_sub(fl, qh, vv, lb, tril_b):
    w = fl.shape[1]
    sg = _sigmoid(fl)
    f = lb + (1.0 - lb) * sg
    k = 1.0 - f
    gc = _chunk_sum(tril_b, jnp.log(f))
    gl_c = [t[HGRN_CHUNK - 1:HGRN_CHUNK, :] for t in _chunks(gc)]
    gl = jnp.concatenate([jnp.broadcast_to(g, (HGRN_CHUNK, w)) for g in gl_c], axis=0)
    sq = _sigmoid(qh)
    eg = jnp.exp(gc)
    eng = jnp.exp(-gc)
    elg = jnp.exp(gl - gc)
    qd = qh * sq * eg
    ki = k * eng
    ke = k * elg
    qd_b, ki_b, ke_b, v_b = (t.astype(BF16) for t in (qd, ki, ke, vv))
    am_b = [jnp.where(tril_b > 0, _dot_nt(qq, kk), 0.0).astype(BF16) for qq, kk in zip(_heads(qd_b), _heads(ki_b))]
    return dict(sg=sg, f=f, sq=sq, eg=eg, eng=eng, elg=elg, qd=qd, ki=ki, ke=ke, egl=[jnp.exp(g) for g in gl_c],
                am_b=am_b, qd_b=qd_b, ki_b=ki_b, ke_b=ke_b, v_b=v_b)


def _hgrn_out(q, st_b):
    outs = []
    for h, (qd_h, v_h) in enumerate(zip(_heads(q["qd_b"]), _heads(q["v_b"]))):
        inter = [_dot_nt(qc, s) for qc, s in zip(_chunks(qd_h), st_b[h])]
        outs.append(_dot(q["am_b"][h], v_h) + jnp.concatenate(inter, axis=0))
    return jnp.concatenate(outs, axis=1)


def _head_rms(o):
    return _per_head(lambda t: jnp.broadcast_to(
        lax.rsqrt(jnp.mean(t * t, axis=-1, keepdims=True) + RMS_EPS), t.shape), o)


def _hgrn_fwd(name, hg, lb_logits, norm_g, cfg):
    T = hg.shape[0]
    sub_rows, hps, th = min(cfg[0], T), cfg[1], min(cfg[2], T)
    cps = sub_rows // HGRN_CHUNK
    nc = th // HGRN_CHUNK
    tri = _chunk_tri(sub_rows)

    def body(fl_ref, qh_ref, v_ref, og_ref, lbl_ref, ng_ref, tri_ref, y_ref, ss_ref, st):
        @pl.when(pl.program_id(1) == 0)
        def _():
            st[...] = jnp.zeros_like(st)

        lbs = _lower_bound(lbl_ref)
        ng = jnp.tile(ng_ref[...], (1, hps))

        def sub(si, carry):
            rows = pl.ds(pl.multiple_of(si * sub_rows, sub_rows), sub_rows)
            q = _hgrn_sub(fl_ref[rows, :], qh_ref[rows, :], v_ref[rows, :], lbs, tri_ref[...])
            v_hc = [_chunks(t) for t in _heads(q["v_b"])]
            k_hc = [_chunks(t) for t in _heads(q["ke_b"])]
            s = [st[h] for h in range(hps)]
            st_b = [[] for _ in range(hps)]
            for c in range(cps):
                egl = _heads(q["egl"][c])
                for h in range(hps):
                    ss_ref[h, si * cps + c] = s[h]
                    st_b[h].append(s[h].astype(BF16))
                    s[h] = s[h] * egl[h] + _dot_tn(v_hc[h][c], k_hc[h][c])
            for h in range(hps):
                st[h] = s[h]
            o = _hgrn_out(q, st_b)
            og = og_ref[rows, :]
            y_ref[rows, :] = (o * _head_rms(o) * ng * (og * _sigmoid(og))).astype(BF16)
            return carry

        lax.fori_loop(0, th // sub_rows, sub, 0)

    wd = 128 * hps
    col = lambda j: pl.BlockSpec((th, wd), lambda h, t: (t, (8 // hps) * j + h))
    return pl.pallas_call(
        body, name=name, grid=(HGRN_HEADS // hps, T // th),
        in_specs=[col(0), col(1), col(2), col(3), pl.BlockSpec((2, wd), lambda h, t: (0, h)),
                  pl.BlockSpec((1, 128), lambda h, t: (0, 0)), pl.BlockSpec((sub_rows, sub_rows), lambda h, t: (0, 0))],
        out_specs=[pl.BlockSpec((th, wd), lambda h, t: (t, h)),
                   pl.BlockSpec((hps, nc, 128, 128), lambda h, t: (h, t, 0, 0))],
        out_shape=[jax.ShapeDtypeStruct((T, D), BF16),
                   jax.ShapeDtypeStruct((HGRN_HEADS, T // HGRN_CHUNK, 128, 128), F32)],
        scratch_shapes=[pltpu.VMEM((hps, 128, 128), F32)],
        compiler_params=_params("parallel", "arbitrary"),
    )(hg, hg, hg, hg, lb_logits, norm_g, tri)


def _hgrn_bwd(name, hg, lb_logits, norm_g, sstart, dyh, cfg, carry=None):
    T = hg.shape[0]
    sub_rows, hps, th = min(cfg[0], T), cfg[1], min(cfg[2], T)
    cps = sub_rows // HGRN_CHUNK
    nc = th // HGRN_CHUNK
    nt = T // th
    wd = 128 * hps
    tri = _chunk_tri(sub_rows)
    tri_t = tri.T

    def body(fl_ref, qh_ref, v_ref, og_ref, lbl_ref, ng_ref, tri_ref, trit_ref, ss_ref, dy_ref,
             dfl_ref, dqh_ref, dv_ref, dog_ref, dlb_ref, dng_ref, dst):
        @pl.when(pl.program_id(1) == 0)
        def _():
            dst[...] = jnp.zeros_like(dst)
            dlb_ref[...] = jnp.zeros_like(dlb_ref)
            dng_ref[...] = jnp.zeros_like(dng_ref)

        lb = _lower_bound(lbl_ref)
        ng = jnp.tile(ng_ref[...], (1, hps))
        last = lax.broadcasted_iota(jnp.int32, (HGRN_CHUNK, wd), 0) == HGRN_CHUNK - 1
        nsub = th // sub_rows
        cat0 = lambda parts: jnp.concatenate(parts, axis=0)

        def sub(step, carry):
            si = nsub - 1 - step
            rows = pl.ds(pl.multiple_of(si * sub_rows, sub_rows), sub_rows)
            qh, og, dy = qh_ref[rows, :], og_ref[rows, :], dy_ref[rows, :]
            tril_b = tri_ref[...]
            q = _hgrn_sub(fl_ref[rows, :], qh, v_ref[rows, :], lb, tril_b)
            s_in = [[ss_ref[h, si * cps + c] for c in range(cps)] for h in range(hps)]
            st_b = [[s.astype(BF16) for s in row] for row in s_in]
            o = _hgrn_out(q, st_b)
            rr = _head_rms(o)
            oh = o * rr
            sog = _sigmoid(og)
            dog_ref[rows, :] = (dy * oh * ng * (sog * (1.0 + og * (1.0 - sog)))).astype(BF16)
            don = dy * (og * sog)
            dng_w = jnp.sum(don * oh, axis=0, keepdims=True)
            dd = don * ng
            mean_h = _per_head(lambda t: jnp.broadcast_to(jnp.mean(t, axis=-1, keepdims=True), t.shape), dd * oh)
            do_b = (rr * (dd - oh * mean_h)).astype(BF16)
            do_h, qd_h, ki_h, ke_h, v_h = (_heads(t) for t in (do_b, q["qd_b"], q["ki_b"], q["ke_b"], q["v_b"]))
            da_b = [jnp.where(tril_b > 0, _dot_nt(do_h[h], v_h[h]), 0.0).astype(BF16) for h in range(hps)]
            do_c, qd_c, ke_c, v_c = ([_chunks(t) for t in hs] for hs in (do_h, qd_h, ke_h, v_h))
            dsp = [[None] * cps for _ in range(hps)]
            dgl_dec = [[None] * cps for _ in range(hps)]
            d = [dst[h] for h in range(hps)]
            for c in reversed(range(cps)):
                egl = _heads(q["egl"][c])
                for h in range(hps):
                    dsp[h][c] = d[h]
                    dgl_dec[h][c] = jnp.sum(d[h] * s_in[h][c], axis=0, keepdims=True) * egl[h]
                    d[h] = d[h] * egl[h] + _dot_tn(do_c[h][c], qd_c[h][c])
            for h in range(hps):
                dst[h] = d[h]
                dng_ref[h] += dng_w[:, 128 * h:128 * (h + 1)]
            dsp_b = [[t.astype(BF16) for t in row] for row in dsp]
            dv_ref[rows, :] = jnp.concatenate(
                [_dot_tn(q["am_b"][h], do_h[h]) + cat0([_dot_nt(ke_c[h][c], dsp_b[h][c]) for c in range(cps)])
                 for h in range(hps)], axis=1).astype(BF16)
            dqd = jnp.concatenate(
                [_dot(da_b[h], ki_h[h]) + cat0([_dot(do_c[h][c], st_b[h][c]) for c in range(cps)])
                 for h in range(hps)], axis=1)
            dki = jnp.concatenate([_dot_tn(da_b[h], qd_h[h]) for h in range(hps)], axis=1)
            dke = jnp.concatenate([cat0([_dot(v_c[h][c], dsp_b[h][c]) for c in range(cps)]) for h in range(hps)], axis=1)
            dkk = dke * q["ke"]
            dgl = [jnp.sum(t, axis=0, keepdims=True) + jnp.concatenate([dgl_dec[h][c] for h in range(hps)], axis=1)
                   for c, t in enumerate(_chunks(dkk))]
            dgc = dqd * q["qd"] - dki * q["ki"] - dkk + cat0([jnp.where(last, g, 0.0) for g in dgl])
            dlf = _chunk_sum(trit_ref[...], dgc)
            df = dlf / q["f"] - (dki * q["eng"] + dke * q["elg"])
            sg = q["sg"]
            dlb_ref[...] += jnp.sum(df * (1.0 - sg), axis=0, keepdims=True)
            dfl_ref[rows, :] = (df * (1.0 - lb) * sg * (1.0 - sg)).astype(BF16)
            sq = q["sq"]
            dqh_ref[rows, :] = (dqd * q["eg"] * (sq * (1.0 + qh * (1.0 - sq)))).astype(BF16)
            return carry

        lax.fori_loop(0, nsub, sub, 0)
    col = lambda j: pl.BlockSpec((th, wd), lambda h, t: (nt - 1 - t, (8 // hps) * j + h))
    out = pl.BlockSpec((th, wd), lambda h, t: (nt - 1 - t, h))
    tri_spec = pl.BlockSpec((sub_rows, sub_rows), lambda h, t: (0, 0))
    return _call(
        body, name=name, grid=(HGRN_HEADS // hps, nt),
        in_specs=[col(0), col(1), col(2), col(3), pl.BlockSpec((2, wd), lambda h, t: (0, h)),
                  pl.BlockSpec((1, 128), lambda h, t: (0, 0)), tri_spec, tri_spec,
                  pl.BlockSpec((hps, nc, 128, 128), lambda h, t: (h, nt - 1 - t, 0, 0)), out],
        out_specs=[out, out, out, out, pl.BlockSpec((1, wd), lambda h, t: (0, h)),
                   pl.BlockSpec((hps, 1, 128), lambda h, t: (h, 0, 0))],
        out_shape=[jax.ShapeDtypeStruct((T, D), BF16)] * 4 + [jax.ShapeDtypeStruct((1, D), F32),
                                                              jax.ShapeDtypeStruct((HGRN_HEADS, 1, 128), F32)],
        scratch_shapes=[pltpu.VMEM((hps, 128, 128), F32)],
        ins=[hg, hg, hg, hg, lb_logits, norm_g, tri, tri_t, sstart, dyh], carry=carry)


def _mix_fwd(name, ya, yh, gates, xh1, g1, b1, wall, tm):
    T = ya.shape[0]

    def body(ya_ref, yh_ref, ga_ref, gh_ref, xh_ref, g_ref, b_ref, wall_ref,
             xo_ref, r_ref, pa_ref, ph_ref, mg_ref, wpa, wph, wo, sem):
        @pl.when(pl.program_id(0) == 0)
        def _():
            _load_rows(wall_ref, [(wpa, RIN, RP), (wph, RIN + RP, RP), (wo, RIN + 2 * RP, RP)], sem)

        parts = [slice(p, min(p + PART_ROWS, tm)) for p in range(0, tm, PART_ROWS)]
        pas = [_dot(ya_ref[rs, :], wpa[...]).astype(BF16) for rs in parts]
        phs = [_dot(yh_ref[rs, :], wph[...]).astype(BF16) for rs in parts]
        mgs = []
        for rs, pa, ph in zip(parts, pas, phs):
            pa_ref[rs, :] = pa
            ph_ref[rs, :] = ph
            mgs.append((_sigmoid(ga_ref[rs, :]) * pa.astype(F32) + _sigmoid(gh_ref[rs, :]) * ph.astype(F32)).astype(BF16))
            mg_ref[rs, :] = mgs[-1]
        mixes = [_dot(mg, wo[...]) for mg in mgs]
        for rs, mix in zip(parts, mixes):
            y1 = xh_ref[rs, :] * g_ref[...] + b_ref[...]
            xh2, r = _ln_fwd(ALPHA * y1 + mix)
            xo_ref[rs, :] = xh2
            r_ref[rs, :] = jnp.broadcast_to(r, (xh2.shape[0], 128))

    row = lambda w: pl.BlockSpec((tm, w), lambda i: (i, 0))
    vec = pl.BlockSpec((1, D), lambda i: (0, 0))
    return pl.pallas_call(
        body, name=name, grid=(T // tm,),
        in_specs=[row(D), row(D), pl.BlockSpec((tm, D), lambda i: (i, 0)), pl.BlockSpec((tm, D), lambda i: (i, 1)),
                  row(D), vec, vec, HBM_SPEC],
        out_specs=[row(D), row(128), row(D), row(D), row(D)],
        out_shape=[jax.ShapeDtypeStruct((T, D), F32), jax.ShapeDtypeStruct((T, 128), F32)]
        + [jax.ShapeDtypeStruct((T, D), BF16)] * 3,
        scratch_shapes=[pltpu.VMEM((D, D), BF16)] * 3 + [pltpu.SemaphoreType.DMA((3 * NDEV,))],
        compiler_params=_params("arbitrary"),
    )(ya, yh, gates, gates, xh1, g1, b1, wall)


def _mix_bwd(name, dy2, xh2, r2, g2, gates, pa, ph, wall, tm):
    T = dy2.shape[0]

    def body(dy_ref, xh_ref, r_ref, g_ref, ga_ref, gh_ref, pa_ref, ph_ref, wall_ref,
             dz_ref, dmix_ref, dpa_ref, dph_ref, dga_ref, dgh_ref, dya_ref, dyh_ref, dg_ref, db_ref,
             wpa, wph, wo, sem):
        @pl.when(pl.program_id(0) == 0)
        def _():
            _load_rows(wall_ref, [(wpa, RIN, RP), (wph, RIN + RP, RP), (wo, RIN + 2 * RP, RP)], sem)
            dg_ref[...] = jnp.zeros_like(dg_ref)
            db_ref[...] = jnp.zeros_like(db_ref)

        parts = [slice(p, min(p + PART_ROWS, tm)) for p in range(0, tm, PART_ROWS)]
        dmix = []
        for rs in parts:
            dz, dgp, dbp = _ln_bwd(dy_ref[rs, :], xh_ref[rs, :], r_ref[rs, :1], g_ref[...])
            dg_ref[...] += dgp
            db_ref[...] += dbp
            dz_ref[rs, :] = dz
            dmix.append(dz.astype(BF16))
            dmix_ref[rs, :] = dmix[-1]
        dmgs = [_dot_nt(d, wo[...]) for d in dmix]
        dpas, dphs = [], []
        for rs, dmg in zip(parts, dmgs):
            sa, sh = _sigmoid(ga_ref[rs, :]), _sigmoid(gh_ref[rs, :])
            dga_ref[rs, :] = (dmg * pa_ref[rs, :].astype(F32) * sa * (1.0 - sa)).astype(BF16)
            dgh_ref[rs, :] = (dmg * ph_ref[rs, :].astype(F32) * sh * (1.0 - sh)).astype(BF16)
            dpas.append((dmg * sa).astype(BF16))
            dphs.append((dmg * sh).astype(BF16))
            dpa_ref[rs, :] = dpas[-1]
            dph_ref[rs, :] = dphs[-1]
        for rs, dpa, dph in zip(parts, dpas, dphs):
            dya_ref[rs, :] = _dot_nt(dpa, wpa[...]).astype(BF16)
            dyh_ref[rs, :] = _dot_nt(dph, wph[...])

    row = lambda w: pl.BlockSpec((tm, w), lambda i: (i, 0))
    vec = pl.BlockSpec((1, D), lambda i: (0, 0))
    return pl.pallas_call(
        body, name=name, grid=(T // tm,),
        in_specs=[row(D), row(D), row(128), vec, pl.BlockSpec((tm, D), lambda i: (i, 0)),
                  pl.BlockSpec((tm, D), lambda i: (i, 1)), row(D), row(D), HBM_SPEC],
        out_specs=[row(D)] * 8 + [vec, vec],
        out_shape=[jax.ShapeDtypeStruct((T, D), F32)] + [jax.ShapeDtypeStruct((T, D), BF16)] * 3
        + [jax.ShapeDtypeStruct((T, D), BF16)] * 3 + [jax.ShapeDtypeStruct((T, D), F32)]
        + [jax.ShapeDtypeStruct((1, D), F32)] * 2,
        scratch_shapes=[pltpu.VMEM((D, D), BF16)] * 3 + [pltpu.SemaphoreType.DMA((3 * NDEV,))],
        compiler_params=_params("arbitrary"),
    )(dy2, xh2, r2, g2, gates, gates, pa, ph, wall)


def _adam(w, g, m, v):
    m = ADAM_B1 * m + (1.0 - ADAM_B1) * g
    v = ADAM_B2 * v + (1.0 - ADAM_B2) * (g * g)
    m_hat = m / (1.0 - ADAM_B1 ** ADAM_STEP)
    v_hat = v / (1.0 - ADAM_B2 ** ADAM_STEP)
    delta = -ADAM_LR * (m_hat / (jnp.sqrt(v_hat) + ADAM_EPS) + ADAM_WD * w)
    return delta, m, v


def _grad_steps(name, items, cb, carry=None):
    n = len(items)

    def body(*refs):
        ins, outs = refs[:4 * n], refs[4 * n:]
        for i in range(n):
            r_ref, w_ref, m_ref, v_ref = ins[4 * i:4 * i + 4]
            g_ref, d_ref, mo_ref, vo_ref = outs[4 * i:4 * i + 4]
            acc = r_ref[0].astype(F32)
            for k in range(1, NDEV):
                acc = acc + r_ref[k].astype(F32)
            g_ref[...] = acc
            d, mm, vv = _adam(w_ref[...], acc, m_ref[...], v_ref[...])
            d_ref[...] = d
            mo_ref[...] = mm
            vo_ref[...] = vv

    in_specs, out_specs, out_shape, ins = [], [], [], []
    for recv, w, m, v in items:
        rows = recv.shape[1]
        blk = pl.BlockSpec((rows, cb), lambda j: (0, j))
        in_specs += [pl.BlockSpec((NDEV, rows, cb), lambda j: (0, 0, j)), blk, blk, blk]
        out_specs += [blk] * 4
        out_shape += [jax.ShapeDtypeStruct(w.shape, F32)] * 4
        ins += [recv, w, m, v]
    outs, ex = _call(body, name=name, grid=(D // cb,), in_specs=in_specs, out_specs=out_specs, out_shape=out_shape,
                     scratch_shapes=[], ins=ins, carry=carry)
    return [tuple(outs[4 * i:4 * i + 4]) for i in range(n)], ex


_SMALL = [("ln1_g", D), ("ln1_b", D), ("ln2_g", D), ("ln2_b", D), ("ln3_g", D), ("ln3_b", D),
          ("b_in", DIN), ("lb", D), ("attn_sinks", 128), ("hgrn_norm_g", 128), ("loss", 128)]
_SMALL_OFF = {}
_o = 0
for _n, _w in _SMALL:
    _SMALL_OFF[_n] = (_o, _w)
    _o += _w
PACK = _o


def _small_reduce(name, packed):
    def body(p_ref, o_ref, buf, send_sems, recv_sems):
        x, y, c = lax.axis_index("x"), lax.axis_index("y"), lax.axis_index("c")
        me = 4 * x + 2 * y + c
        buf[me] = p_ref[...]
        copies = []
        for j in range(1, NDEV):
            px, py, pc = x ^ (j >> 2), y ^ ((j >> 1) & 1), c ^ (j & 1)
            cp = pltpu.make_async_remote_copy(
                src_ref=p_ref, dst_ref=buf.at[me], send_sem=send_sems.at[j - 1], recv_sem=recv_sems.at[j - 1],
                device_id=(px, py, pc), device_id_type=MESH)
            cp.start()
            copies.append(cp)
        for cp in copies:
            cp.wait()
        tot = buf[0]
        for k in range(1, NDEV):
            tot = tot + buf[k]
        o_ref[...] = tot

    return pl.pallas_call(
        body, name=name, out_shape=jax.ShapeDtypeStruct((1, PACK), F32),
        scratch_shapes=[pltpu.VMEM((NDEV, 1, PACK), F32), pltpu.SemaphoreType.DMA((NDEV - 1,)),
                        pltpu.SemaphoreType.DMA((NDEV - 1,))],
        compiler_params=pltpu.CompilerParams(vmem_limit_bytes=VMEM_LIMIT),
    )(packed)


def _small_step(name, total, small_w, small_m, small_v):
    names = ["ln1_g", "ln1_b", "ln2_g", "ln2_b", "ln3_g", "ln3_b", "b_in", "attn_sinks", "hgrn_lb_logits", "hgrn_norm_g"]
    np_ = len(names)

    def body(*refs):
        t_ref = refs[0]
        w_refs = refs[1:1 + np_]
        m_refs = refs[1 + np_:1 + 2 * np_]
        v_refs = refs[1 + 2 * np_:1 + 3 * np_]
        loss_ref, outs = refs[1 + 3 * np_], refs[2 + 3 * np_:]

        def part(n):
            o, w = _SMALL_OFF[n]
            return t_ref[:, o:o + w]

        loss_ref[...] = part("loss")
        for i, n in enumerate(names):
            w = w_refs[i][...]
            if n == "hgrn_lb_logits":
                m_ = jnp.maximum(w[0:1, :], w[1:2, :])
                e0, e1 = jnp.exp(w[0:1, :] - m_), jnp.exp(w[1:2, :] - m_)
                p0 = e0 / (e0 + e1)
                t = p0 * (1.0 - p0) * part("lb")
                g = jnp.concatenate([t, -t], axis=0)
            elif n == "attn_sinks":
                g = part(n)[:, :N_Q_HEADS]
            else:
                g = part(n)
            d, mm, vv = _adam(w, g, m_refs[i][...], v_refs[i][...])
            outs[4 * i][...] = g
            outs[4 * i + 1][...] = d
            outs[4 * i + 2][...] = mm
            outs[4 * i + 3][...] = vv

    out_shape = [jax.ShapeDtypeStruct((1, 128), F32)]
    for n in names:
        out_shape += [jax.ShapeDtypeStruct(small_w[n].shape, F32)] * 4
    return pl.pallas_call(
        body, name=name, out_shape=out_shape,
        compiler_params=pltpu.CompilerParams(vmem_limit_bytes=VMEM_LIMIT),
    )(total, *[small_w[n] for n in names], *[small_m[n] for n in names], *[small_v[n] for n in names]), names


def _tile(T, pref):
    return min(T, pref)


def kernel(x, ln1_g, ln1_b, ffn1_w1, ffn1_w3, ffn1_w2, ln2_g, ln2_b, w_in, b_in, attn_sinks, hgrn_lb_logits, hgrn_norm_g, w_proj_attn, w_proj_hgrn, w_out, ln3_g, ln3_b, ffn2_w1, ffn2_w3, ffn2_w2, loss_target, m_ln1_g, m_ln1_b, m_ffn1_w1, m_ffn1_w3, m_ffn1_w2, m_ln2_g, m_ln2_b, m_w_in, m_b_in, m_attn_sinks, m_hgrn_lb_logits, m_hgrn_norm_g, m_w_proj_attn, m_w_proj_hgrn, m_w_out, m_ln3_g, m_ln3_b, m_ffn2_w1, m_ffn2_w3, m_ffn2_w2, v_ln1_g, v_ln1_b, v_ffn1_w1, v_ffn1_w3, v_ffn1_w2, v_ln2_g, v_ln2_b, v_w_in, v_b_in, v_attn_sinks, v_hgrn_lb_logits, v_hgrn_norm_g, v_w_proj_attn, v_w_proj_hgrn, v_w_out, v_ln3_g, v_ln3_b, v_ffn2_w1, v_ffn2_w3, v_ffn2_w2):
    T = x.shape[1]
    xs = x[0]
    tgt = loss_target[0]
    tm = _tile(T, 256)
    tm2 = _tile(T, 512)
    tk = _tile(T, 2048)

    t_bf = lambda w: w[0].T.astype(BF16)
    n_bf = lambda w: w[0].astype(BF16)
    ffn_shard = lambda w1, w3, w2: jnp.concatenate([t_bf(w1), t_bf(w3), n_bf(w2)], axis=0)
    mix_shard = jnp.concatenate([t_bf(w_in), n_bf(w_proj_attn), n_bf(w_proj_hgrn), n_bf(w_out)], axis=0)
    (ffn1_all,) = _exchange_call("gather_ffn1", _gather_exchange(ffn_shard(ffn1_w1, ffn1_w3, ffn1_w2)))
    ffn_offs = (0, RF, 2 * RF)
    ropes = _rope_tables(T)

    (xh1, r1, a1, b1, xb0), (mix_all,) = _ffn_fwd("ffn1_fwd", xs, None, ffn1_all, ffn_offs, tm2,
                                                  carry=_gather_exchange(mix_shard))
    (qkv, kt, vt, hg, gates, y1b), (ffn2_all,) = _inproj_fwd(
        "inproj_fwd", xh1, ln1_g, ln1_b, mix_all, b_in, ropes, tm2,
        carry=_gather_exchange(ffn_shard(ffn2_w1, ffn2_w3, ffn2_w2)))
    ya = _attn_fwd("attn_fwd", qkv, vt, attn_sinks)
    yh, sstart = _hgrn_fwd("hgrn_fwd", hg, hgrn_lb_logits, hgrn_norm_g, HG_FWD)
    xh2, r2, pa, ph, merged = _mix_fwd("mix_fwd", ya, yh, gates, xh1, ln1_g, ln1_b, mix_all, tm2)
    (xh3, r3, a2, b2, y2b, loss_part), _ = _ffn_fwd("ffn2_fwd", xh2, (ln2_g, ln2_b), ffn2_all, ffn_offs, tm2,
                                                    loss=(ln3_g, ln3_b, tgt))

    (dy2, dab2, u2, df2, dg3, db3), _ = _ffn_bwd("ffn2_bwd", ("loss", ln3_b, tgt), xh3, r3, ln3_g, a2, b2, ffn2_all,
                                                 ffn_offs, tm)
    g_ffn2_13, _ = _wgrad("wgrad_ffn2_w13", dab2, y2b, DFF // 2, tk)
    g_ffn2_2, _ = _wgrad("wgrad_ffn2_w2", u2, df2, DFF // 2, tk)
    dz2, dmix, dpa, dph, dga, dgh, dya, dyh, dg2, db2 = _mix_bwd("mix_bwd", dy2, xh2, r2, ln2_g, gates, pa, ph,
                                                                 mix_all, tm2)
    g_wo, _ = _wgrad("wgrad_w_out", merged, dmix, D, tk)
    g_pa, _ = _wgrad("wgrad_w_pa", ya, dpa, D, tk)
    g_ph, _ = _wgrad("wgrad_w_ph", yh, dph, D, tk)
    recv = {}
    (dq, dk, dv, dsink), (recv["ffn2_w1"], recv["ffn2_w3"], recv["ffn2_w2"]) = _attn_bwd(
        "attn_bwd", qkv, kt, attn_sinks, dya,
        carry=_grad_exchange([g_ffn2_13, g_ffn2_13, g_ffn2_2], [0, DFF, 0], [RF, RF, RF]))
    (dfl, dqh, dih, dog, dlb, dng), (recv["w_proj_attn"], recv["w_proj_hgrn"], recv["w_out"]) = _hgrn_bwd(
        "hgrn_bwd", hg, hgrn_lb_logits, hgrn_norm_g, sstart, dyh, HG_BWD,
        carry=_grad_exchange([g_pa, g_ph, g_wo], [0, 0, 0], [RP, RP, RP]))
    dy1, dproj, dbin = _inproj_bwd("inproj_bwd", dq, dk, dv, (dfl, dqh, dih, dog), (dga, dgh), dz2, mix_all, ropes, tm)
    g_win, _ = _wgrad("wgrad_w_in", dproj, y1b, DIN // 4, tk)
    (gx, dab1, u1, df1, dg1, db1), (recv["w_in"],) = _ffn_bwd(
        "ffn1_bwd", ("dy", dy1), xh1, r1, ln1_g, a1, b1, ffn1_all, ffn_offs, tm,
        carry=_grad_exchange([g_win], [0], [RIN]))
    g_ffn1_2, _ = _wgrad("wgrad_ffn1_w2", u1, df1, DFF // 2, tk)
    g_ffn1_1, (recv["ffn1_w2"],) = _wgrad("wgrad_ffn1_w1", dab1, xb0, DFF // 2, tk, 0, DFF,
                                          carry=_grad_exchange([g_ffn1_2], [0], [RF]))
    g_ffn1_3, (recv["ffn1_w1"],) = _wgrad("wgrad_ffn1_w3", dab1, xb0, DFF // 2, tk, DFF, DFF,
                                          carry=_grad_exchange([g_ffn1_1], [0], [RF]))
    (recv["ffn1_w3"],) = _exchange_call("exchange_ffn1_w3", _grad_exchange([g_ffn1_3], [0], [RF]))

    big = [("ffn1_w1", ffn1_w1, m_ffn1_w1, v_ffn1_w1, True), ("ffn1_w3", ffn1_w3, m_ffn1_w3, v_ffn1_w3, True),
           ("ffn1_w2", ffn1_w2, m_ffn1_w2, v_ffn1_w2, False), ("w_in", w_in, m_w_in, v_w_in, True),
           ("w_proj_attn", w_proj_attn, m_w_proj_attn, v_w_proj_attn, False),
           ("w_proj_hgrn", w_proj_hgrn, m_w_proj_hgrn, v_w_proj_hgrn, False),
           ("w_out", w_out, m_w_out, v_w_out, False),
           ("ffn2_w1", ffn2_w1, m_ffn2_w1, v_ffn2_w1, True), ("ffn2_w3", ffn2_w3, m_ffn2_w3, v_ffn2_w3, True),
           ("ffn2_w2", ffn2_w2, m_ffn2_w2, v_ffn2_w2, False)]
    view = lambda t, transposed: t[0].T if transposed else t[0]
    back = lambda t, transposed: t.T[None] if transposed else t[None]
    res = {}
    for n, w, m, v, tr in big:
        (outs,), _ = _grad_steps("step_" + n, [(recv[n], view(w, tr), view(m, tr), view(v, tr))], 256)
        res[n] = tuple(back(t, tr) for t in outs)

    parts = {"ln1_g": dg1, "ln1_b": db1, "ln2_g": dg2, "ln2_b": db2, "ln3_g": dg3, "ln3_b": db3, "b_in": dbin,
             "lb": dlb, "attn_sinks": dsink, "hgrn_norm_g": jnp.sum(dng, axis=0), "loss": loss_part[0:1, :]}
    packed = jnp.concatenate([parts[n] for n, _ in _SMALL], axis=1)
    small_w = dict(ln1_g=ln1_g, ln1_b=ln1_b, ln2_g=ln2_g, ln2_b=ln2_b, ln3_g=ln3_g, ln3_b=ln3_b, b_in=b_in,
                   attn_sinks=attn_sinks, hgrn_lb_logits=hgrn_lb_logits, hgrn_norm_g=hgrn_norm_g)
    small_m = dict(ln1_g=m_ln1_g, ln1_b=m_ln1_b, ln2_g=m_ln2_g, ln2_b=m_ln2_b, ln3_g=m_ln3_g, ln3_b=m_ln3_b,
                   b_in=m_b_in, attn_sinks=m_attn_sinks, hgrn_lb_logits=m_hgrn_lb_logits, hgrn_norm_g=m_hgrn_norm_g)
    small_v = dict(ln1_g=v_ln1_g, ln1_b=v_ln1_b, ln2_g=v_ln2_g, ln2_b=v_ln2_b, ln3_g=v_ln3_g, ln3_b=v_ln3_b,
                   b_in=v_b_in, attn_sinks=v_attn_sinks, hgrn_lb_logits=v_hgrn_lb_logits, hgrn_norm_g=v_hgrn_norm_g)
    outs, names = _small_step("small_step", _small_reduce("small_reduce", packed), small_w, small_m, small_v)
    loss = outs[0][0, 0]
    for i, n in enumerate(names):
        res[n] = tuple(outs[1 + 4 * i:5 + 4 * i])

    order = ["ln1_g", "ln1_b", "ffn1_w1", "ffn1_w3", "ffn1_w2", "ln2_g", "ln2_b", "w_in", "b_in", "attn_sinks",
             "hgrn_lb_logits", "hgrn_norm_g", "w_proj_attn", "w_proj_hgrn", "w_out", "ln3_g", "ln3_b",
             "ffn2_w1", "ffn2_w3", "ffn2_w2"]
    return (loss, gx[None], *[res[n][0] for n in order], *[res[n][1] for n in order],
            *[res[n][2] for n in order], *[res[n][3] for n in order])
```

```python
import jax
import jax.numpy as jnp
from jax import lax
from jax.experimental import pallas as pl
from jax.experimental.pallas import tpu as pltpu

F32 = jnp.float32
BF16 = jnp.bfloat16

NDEV = 8
D = 1024
DFF = 2816
RF = DFF // NDEV
DIN = 7680
RIN = DIN // NDEV
RP = D // NDEV
N_Q_HEADS = 16
N_KV_HEADS = 4
HEAD_DIM = 64
ATTN_BLOCK = 128
ROPE_THETA = 500000.0
ROPE_DIM = HEAD_DIM // 4
HGRN_HEADS = 8
HGRN_DK = 128
HGRN_CHUNK = 64
ALPHA = 2.0 ** 0.25
LN_EPS = 1e-5
RMS_EPS = 1e-6
NEG_INF = -1e30
ADAM_LR = 0.001
ADAM_B1 = 0.9
ADAM_B2 = 0.999
ADAM_EPS = 1e-08
ADAM_WD = 0.01
ADAM_STEP = 10

QKV_W = 1536
HG_W = 4096
GATE_W = 2048
VMEM_LIMIT = 60 * 2 ** 20
PART_ROWS = 256

MESH = pl.DeviceIdType.MESH
HBM_SPEC = pl.BlockSpec(memory_space=pltpu.HBM)


def _params(*sem):
    return pltpu.CompilerParams(dimension_semantics=sem, vmem_limit_bytes=VMEM_LIMIT)


def _dot(a, b):
    return jnp.dot(a, b, preferred_element_type=F32)


def _dot_nt(a, b):
    return lax.dot_general(a, b, (((1,), (1,)), ((), ())), preferred_element_type=F32)


def _dot_tn(a, b):
    return lax.dot_general(a, b, (((0,), (0,)), ((), ())), preferred_element_type=F32)


def _sigmoid(x):
    return 0.5 * jnp.tanh(0.5 * x) + 0.5


def _ln_fwd(z):
    mu = jnp.mean(z, axis=-1, keepdims=True)
    zc = z - mu
    var = jnp.mean(zc * zc, axis=-1, keepdims=True)
    r = lax.rsqrt(var + LN_EPS)
    return zc * r, r


def _ln_bwd(dy, xh, r, g):
    dxh = dy * g
    m1 = jnp.mean(dxh, axis=-1, keepdims=True)
    m2 = jnp.mean(dxh * xh, axis=-1, keepdims=True)
    dz = r * (dxh - m1 - xh * m2)
    return dz, jnp.sum(dy * xh, axis=0, keepdims=True), jnp.sum(dy, axis=0, keepdims=True)


def _load_rows(wall_ref, pieces, sem):
    copies = []
    for dst, off, r in pieces:
        for k in range(NDEV):
            c = pltpu.make_async_copy(wall_ref.at[k, pl.ds(off, r), :], dst.at[pl.ds(k * r, r), :], sem.at[len(copies)])
            c.start()
            copies.append(c)
    for c in copies:
        c.wait()


class _Exchange:
    def __init__(self, inputs, out_shape, scratch, begin, middle, end):
        self.inputs, self.out_shape, self.scratch = inputs, out_shape, scratch
        self.begin, self.middle, self.end = begin, middle, end


def _gather_exchange(shard):
    rows, cols = shard.shape

    def ops(ins, outs, scr):
        (x_ref,), (out_ref,), (send_sems, recv_sems, local_sem) = ins, outs, scr
        x, y, c = lax.axis_index("x"), lax.axis_index("y"), lax.axis_index("c")
        me, sibling = (x, y, c), (x, y, 1 - c)
        chips = [(1 - x, y), (x, 1 - y), (1 - x, 1 - y)]

        def slot(px, py, pc):
            return out_ref.at[4 * px + 2 * py + pc]

        def copy(k, block, to, src=None):
            return pltpu.make_async_remote_copy(
                src_ref=slot(*block) if src is None else src, dst_ref=slot(*block),
                send_sem=send_sems.at[k], recv_sem=recv_sems.at[k], device_id=to, device_id_type=MESH)

        mine = lambda: pltpu.make_async_copy(x_ref, slot(*me), local_sem)
        first = lambda: [copy(0, me, sibling, src=x_ref)] + [
            copy(1 + j, me, (*chip, c), src=x_ref) for j, chip in enumerate(chips)]
        passed = lambda: [copy(4 + j, (*chip, c), sibling) for j, chip in enumerate(chips)]
        return c, me, sibling, chips, copy, mine, first, passed

    def begin(*refs):
        _, _, _, _, _, mine, first, _ = ops(*refs)
        mine().start()
        for cp in first():
            cp.start()

    def middle(*refs):
        c, me, _, chips, copy, _, _, passed = ops(*refs)
        for (j, chip), fwd in zip(enumerate(chips), passed()):
            copy(1 + j, (*chip, c), me).wait_recv()
            fwd.start()

    def end(*refs):
        c, me, sibling, chips, copy, mine, first, passed = ops(*refs)
        copy(0, sibling, me).wait_recv()
        for j, chip in enumerate(chips):
            copy(4 + j, (*chip, 1 - c), me).wait_recv()
        for cp in first() + passed():
            cp.wait_send()
        mine().wait()

    return _Exchange([shard], [jax.ShapeDtypeStruct((NDEV, rows, cols), shard.dtype)],
                     [pltpu.SemaphoreType.DMA((7,)), pltpu.SemaphoreType.DMA((7,)), pltpu.SemaphoreType.DMA],
                     begin, middle, end)


def _grad_exchange(grads, bases, rows):
    n = len(grads)

    def copies(g_refs, out_refs, scr):
        send_sems, recv_sems, local_sems = scr
        x, y, c = lax.axis_index("x"), lax.axis_index("y"), lax.axis_index("c")
        me = 4 * x + 2 * y + c
        out = []
        for i in range(n):
            r = rows[i]
            src = lambda k: g_refs[i].at[pl.ds(pl.multiple_of(bases[i] + k * r, 16), r), :]
            out.append(pltpu.make_async_copy(src(me), out_refs[i].at[me], local_sems.at[i]))
            for j in range(1, NDEV):
                px, py, pc = x ^ (j >> 2), y ^ ((j >> 1) & 1), c ^ (j & 1)
                out.append(pltpu.make_async_remote_copy(
                    src_ref=src(4 * px + 2 * py + pc), dst_ref=out_refs[i].at[me],
                    send_sem=send_sems.at[i, j - 1], recv_sem=recv_sems.at[i, j - 1],
                    device_id=(px, py, pc), device_id_type=MESH))
        return out

    def begin(*refs):
        for cp in copies(*refs):
            cp.start()

    def end(*refs):
        for cp in copies(*refs):
            cp.wait()

    return _Exchange(list(grads), [jax.ShapeDtypeStruct((NDEV, r, D), g.dtype) for g, r in zip(grads, rows)],
                     [pltpu.SemaphoreType.DMA((n, NDEV - 1)), pltpu.SemaphoreType.DMA((n, NDEV - 1)),
                      pltpu.SemaphoreType.DMA((n,))], begin, None, end)


def _exchange_call(name, ex):
    ni, no = len(ex.inputs), len(ex.out_shape)

    def body(*refs):
        parts = (refs[:ni], refs[ni:ni + no], refs[ni + no:])
        ex.begin(*parts)
        if ex.middle is not None:
            ex.middle(*parts)
        ex.end(*parts)

    return pl.pallas_call(body, name=name, out_shape=ex.out_shape, in_specs=[HBM_SPEC] * ni, out_specs=[HBM_SPEC] * no,
                          scratch_shapes=ex.scratch)(*ex.inputs)


def _call(body, *, name, grid, in_specs, out_specs, out_shape, scratch_shapes, ins, carry=None):
    sem = ("arbitrary",) * len(grid)
    if carry is None:
        outs = pl.pallas_call(body, name=name, grid=grid, in_specs=in_specs, out_specs=out_specs, out_shape=out_shape,
                              scratch_shapes=scratch_shapes, compiler_params=_params(*sem))(*ins)
        return outs, None
    n_in, n_out, n_scr = len(ins), len(out_shape), len(scratch_shapes)
    ci, co = len(carry.inputs), len(carry.out_shape)
    total = 1
    for g in grid:
        total *= g

    def wrapped(*refs):
        own_in, ex_in = refs[:n_in], refs[n_in:n_in + ci]
        o0 = n_in + ci
        own_out, ex_out = refs[o0:o0 + n_out], refs[o0 + n_out:o0 + n_out + co]
        s0 = o0 + n_out + co
        own_scr, ex_scr = refs[s0:s0 + n_scr], refs[s0 + n_scr:]
        step = pl.program_id(0)
        for d in range(1, len(grid)):
            step = step * grid[d] + pl.program_id(d)
        parts = (ex_in, ex_out, ex_scr)
        pl.when(step == 0)(lambda: carry.begin(*parts))
        body(*own_in, *own_out, *own_scr)
        if carry.middle is not None:
            pl.when(step == (3 * total) // 4)(lambda: carry.middle(*parts))
        pl.when(step == total - 1)(lambda: carry.end(*parts))

    outs = pl.pallas_call(
        wrapped, name=name, grid=grid, in_specs=list(in_specs) + [HBM_SPEC] * ci,
        out_specs=list(out_specs) + [HBM_SPEC] * co, out_shape=list(out_shape) + list(carry.out_shape),
        scratch_shapes=list(scratch_shapes) + list(carry.scratch), compiler_params=_params(*sem),
    )(*ins, *carry.inputs)
    return outs[:n_out], outs[n_out:]


def _ffn_fwd(name, xin, affine, wall, offs, tm, loss=None, carry=None):
    T = xin.shape[0]
    nt = T // tm

    def body(*refs):
        it = iter(refs)
        x_ref = next(it)
        if affine is not None:
            g_ref, b_ref = next(it), next(it)
        wall_ref = next(it)
        if loss is not None:
            go_ref, bo_ref, tgt_ref = next(it), next(it), next(it)
        xh_ref, r_ref, a_ref, b2_ref, yb_ref = (next(it) for _ in range(5))
        if loss is not None:
            loss_ref = next(it)
        w1, w3, w2, sem = (next(it) for _ in range(4))

        @pl.when(pl.program_id(0) == 0)
        def _():
            _load_rows(wall_ref, [(w1, offs[0], RF), (w3, offs[1], RF), (w2, offs[2], RF)], sem)
            if loss is not None:
                loss_ref[...] = jnp.zeros_like(loss_ref)

        parts = [slice(p, min(p + PART_ROWS, tm)) for p in range(0, tm, PART_ROWS)]
        ys, ybs = [], []
        for rs in parts:
            y = x_ref[rs, :]
            if affine is not None:
                y = y * g_ref[...] + b_ref[...]
            ys.append(y)
            ybs.append(y.astype(BF16))
            yb_ref[rs, :] = ybs[-1]
        ab = [(_dot_nt(yb, w1[...]).astype(BF16), _dot_nt(yb, w3[...]).astype(BF16)) for yb in ybs]
        us = []
        for rs, (a, b) in zip(parts, ab):
            a_ref[rs, :] = a
            b2_ref[rs, :] = b
            af, bf = a.astype(F32), b.astype(F32)
            us.append((af * _sigmoid(af) * bf).astype(BF16))
        fs = [_dot(u, w2[...]) for u in us]
        for rs, y, f in zip(parts, ys, fs):
            xh, r = _ln_fwd(ALPHA * y + 0.5 * f)
            xh_ref[rs, :] = xh
            r_ref[rs, :] = jnp.broadcast_to(r, (xh.shape[0], 128))
            if loss is not None:
                e = xh * go_ref[...] + bo_ref[...] - tgt_ref[rs, :]
                loss_ref[...] += jnp.sum(e * e) * (0.5 / D)

    row = lambda w: pl.BlockSpec((tm, w), lambda i: (i, 0))
    vec = pl.BlockSpec((1, D), lambda i: (0, 0))
    ins, in_specs = [xin], [row(D)]
    if affine is not None:
        ins += list(affine)
        in_specs += [vec, vec]
    ins.append(wall)
    in_specs.append(HBM_SPEC)
    if loss is not None:
        ins += list(loss)
        in_specs += [vec, vec, row(D)]
    out_shape = [jax.ShapeDtypeStruct((T, D), F32), jax.ShapeDtypeStruct((T, 128), F32),
                 jax.ShapeDtypeStruct((T, DFF), BF16), jax.ShapeDtypeStruct((T, DFF), BF16),
                 jax.ShapeDtypeStruct((T, D), BF16)]
    out_specs = [row(D), row(128), row(DFF), row(DFF), row(D)]
    if loss is not None:
        out_shape.append(jax.ShapeDtypeStruct((8, 128), F32))
        out_specs.append(pl.BlockSpec((8, 128), lambda i: (0, 0)))
    return _call(body, name=name, grid=(nt,), in_specs=in_specs, out_specs=out_specs, out_shape=out_shape,
                 scratch_shapes=[pltpu.VMEM((DFF, D), BF16)] * 3 + [pltpu.SemaphoreType.DMA((3 * NDEV,))],
                 ins=ins, carry=carry)


def _ffn_bwd(name, dy_src, xh, r, g, a, b, wall, offs, tm, carry=None):
    T = xh.shape[0]
    nt = T // tm
    from_loss = dy_src[0] == "loss"

    def body(*refs):
        it = iter(refs)
        if from_loss:
            bo_ref, tgt_ref = next(it), next(it)
        else:
            dy_ref = next(it)
        xh_ref, r_ref, g_ref, a_ref, b_ref, wall_ref = (next(it) for _ in range(6))
        dyin_ref, dab_ref, u_ref, df_ref, dg_ref, db_ref = (next(it) for _ in range(6))
        w1, w3, w2, sem = (next(it) for _ in range(4))

        @pl.when(pl.program_id(0) == 0)
        def _():
            _load_rows(wall_ref, [(w1, offs[0], RF), (w3, offs[1], RF), (w2, offs[2], RF)], sem)
            dg_ref[...] = jnp.zeros_like(dg_ref)
            db_ref[...] = jnp.zeros_like(db_ref)

        parts = [slice(p, min(p + PART_ROWS, tm)) for p in range(0, tm, PART_ROWS)]
        gv = g_ref[...]
        dz, df = [], []
        for rs in parts:
            xhv = xh_ref[rs, :]
            if from_loss:
                dy = (xhv * gv + bo_ref[...] - tgt_ref[rs, :]) * (1.0 / D)
            else:
                dy = dy_ref[rs, :]
            dzp, dgp, dbp = _ln_bwd(dy, xhv, r_ref[rs, :1], gv)
            dg_ref[...] += dgp
            db_ref[...] += dbp
            dz.append(dzp)
            df.append((0.5 * dzp).astype(BF16))
            df_ref[rs, :] = df[-1]
        du = [_dot_nt(d, w2[...]) for d in df]
        da, dbb = [], []
        for rs, dup in zip(parts, du):
            af, bf = a_ref[rs, :].astype(F32), b_ref[rs, :].astype(F32)
            s = _sigmoid(af)
            sl = af * s
            u_ref[rs, :] = (sl * bf).astype(BF16)
            da.append((dup * bf * (s * (1.0 + af * (1.0 - s)))).astype(BF16))
            dbb.append((dup * sl).astype(BF16))
            dab_ref[rs, :DFF] = da[-1]
            dab_ref[rs, DFF:] = dbb[-1]
        for rs, dzp, dap, dbp in zip(parts, dz, da, dbb):
            dyin_ref[rs, :] = ALPHA * dzp + _dot(dap, w1[...]) + _dot(dbp, w3[...])

    row = lambda w: pl.BlockSpec((tm, w), lambda i: (i, 0))
    vec = pl.BlockSpec((1, D), lambda i: (0, 0))
    if from_loss:
        ins, in_specs = [dy_src[1], dy_src[2]], [vec, row(D)]
    else:
        ins, in_specs = [dy_src[1]], [row(D)]
    ins += [xh, r, g, a, b, wall]
    in_specs += [row(D), row(128), vec, row(DFF), row(DFF), HBM_SPEC]
    return _call(
        body, name=name, grid=(nt,), in_specs=in_specs,
        out_specs=[row(D), row(2 * DFF), row(DFF), row(D), vec, vec],
        out_shape=[jax.ShapeDtypeStruct((T, D), F32), jax.ShapeDtypeStruct((T, 2 * DFF), BF16),
                   jax.ShapeDtypeStruct((T, DFF), BF16), jax.ShapeDtypeStruct((T, D), BF16),
                   jax.ShapeDtypeStruct((1, D), F32), jax.ShapeDtypeStruct((1, D), F32)],
        scratch_shapes=[pltpu.VMEM((DFF, D), BF16)] * 3 + [pltpu.SemaphoreType.DMA((3 * NDEV,))],
        ins=ins, carry=carry)


def _wgrad(name, a, b, bn, tk, col0=0, ncols=None, carry=None):
    T = a.shape[0]
    N = a.shape[1] if ncols is None else ncols
    nk = T // tk
    c0 = col0 // bn

    def body(a_ref, b_ref, o_ref, acc):
        k = pl.program_id(1)

        @pl.when(k == 0)
        def _():
            acc[...] = jnp.zeros_like(acc)

        acc[...] += _dot_tn(a_ref[...], b_ref[...])

        @pl.when(k == nk - 1)
        def _():
            o_ref[...] = acc[...].astype(BF16)

    (out,), ex = _call(
        body, name=name, grid=(N // bn, nk),
        in_specs=[pl.BlockSpec((tk, bn), lambda n, k: (k, n + c0)), pl.BlockSpec((tk, D), lambda n, k: (k, 0))],
        out_specs=[pl.BlockSpec((bn, D), lambda n, k: (n, 0))],
        out_shape=[jax.ShapeDtypeStruct((N, D), BF16)],
        scratch_shapes=[pltpu.VMEM((bn, D), F32)], ins=[a, b], carry=carry)
    return out, ex


def _rope_tables(T):
    pos = jnp.arange(T, dtype=F32)
    inv_freq = ROPE_THETA ** (-jnp.arange(0, ROPE_DIM, 2, dtype=F32) / ROPE_DIM)
    half = ROPE_DIM // 2
    ch = jnp.arange(128) % HEAD_DIM
    ang = pos[:, None] * inv_freq[ch % half][None, :]
    cos, sin = jnp.cos(ang), jnp.sin(ang)
    first, second = (ch < half)[None, :], ((ch >= half) & (ch < ROPE_DIM))[None, :]
    return (jnp.where(first | second, cos, 1.0), jnp.where(first, -sin, 0.0), jnp.where(second, sin, 0.0))


def _rope(t, c, s1, s2):
    n = t.shape[1] // 128
    ct, s1t, s2t = (jnp.tile(v, (1, n)) for v in (c, s1, s2))
    w = t.shape[1]
    return t * ct + pltpu.roll(t, w - 8, 1) * s1t + pltpu.roll(t, 8, 1) * s2t


def _rope_t(dr, c, s1, s2):
    n = dr.shape[1] // 128
    ct, s1t, s2t = (jnp.tile(v, (1, n)) for v in (c, s1, s2))
    w = dr.shape[1]
    return dr * ct + pltpu.roll(dr * s1t, 8, 1) + pltpu.roll(dr * s2t, w - 8, 1)


_Q, _K, _V = (0, 1024), (1024, 256), (1280, 256)
_HG = (1536, HG_W)
_GATES = (5632, GATE_W)


def _inproj_fwd(name, xh, g, b, wall, b_in, ropes, tm, carry=None):
    T = xh.shape[0]

    def body(xh_ref, g_ref, b_ref, wall_ref, bin_ref, c_ref, s1_ref, s2_ref,
             qkv_ref, kt_ref, vt_ref, hg_ref, gate_ref, yb_ref, w, sem):
        @pl.when(pl.program_id(0) == 0)
        def _():
            _load_rows(wall_ref, [(w, 0, RIN)], sem)

        yb = (xh_ref[...] * g_ref[...] + b_ref[...]).astype(BF16)
        yb_ref[...] = yb
        c, s1, s2 = c_ref[...], s1_ref[...], s2_ref[...]

        def piece(start, width):
            return _dot_nt(yb, w[start:start + width, :]) + bin_ref[:, start:start + width]

        k = _rope(piece(*_K), c, s1, s2)
        v = piece(*_V)
        qkv_ref[:, 0:1024] = _rope(piece(*_Q), c, s1, s2).astype(BF16)
        qkv_ref[:, 1024:1280] = k.astype(BF16)
        qkv_ref[:, 1280:1536] = v.astype(BF16)
        kt_ref[...] = k.T.astype(BF16)
        vt_ref[...] = v.T.astype(BF16)
        for j in range(4):
            hg_ref[:, 1024 * j:1024 * (j + 1)] = piece(_HG[0] + 1024 * j, 1024)
        for j in range(2):
            gate_ref[:, 1024 * j:1024 * (j + 1)] = piece(_GATES[0] + 1024 * j, 1024)

    row = lambda wd: pl.BlockSpec((tm, wd), lambda i: (i, 0))
    vec = lambda wd: pl.BlockSpec((1, wd), lambda i: (0, 0))
    colt = pl.BlockSpec((256, tm), lambda i: (0, i))
    return _call(
        body, name=name, grid=(T // tm,),
        in_specs=[row(D), vec(D), vec(D), HBM_SPEC, vec(DIN), row(128), row(128), row(128)],
        out_specs=[row(QKV_W), colt, colt, row(HG_W), row(GATE_W), row(D)],
        out_shape=[jax.ShapeDtypeStruct((T, QKV_W), BF16), jax.ShapeDtypeStruct((256, T), BF16),
                   jax.ShapeDtypeStruct((256, T), BF16), jax.ShapeDtypeStruct((T, HG_W), F32),
                   jax.ShapeDtypeStruct((T, GATE_W), F32), jax.ShapeDtypeStruct((T, D), BF16)],
        scratch_shapes=[pltpu.VMEM((DIN, D), BF16), pltpu.SemaphoreType.DMA((NDEV,))],
        ins=[xh, g, b, wall, b_in, *ropes], carry=carry)


def _inproj_bwd(name, dq, dk, dv, dhg, dgates, dz2, wall, ropes, tm):
    T = dq.shape[0]

    def body(dq_ref, dk_ref, dv_ref, d0, d1, d2, d3, dga_ref, dgh_ref, dz_ref, wall_ref, c_ref, s1_ref, s2_ref,
             dy_ref, dproj_ref, dbin_ref, w, sem):
        @pl.when(pl.program_id(0) == 0)
        def _():
            _load_rows(wall_ref, [(w, 0, RIN)], sem)
            dbin_ref[...] = jnp.zeros_like(dbin_ref)

        c, s1, s2 = c_ref[...], s1_ref[...], s2_ref[...]
        acc = ALPHA * dz_ref[...]
        pieces = [(_Q[0], _rope_t(dq_ref[...], c, s1, s2)), (_K[0], _rope_t(dk_ref[...], c, s1, s2)),
                  (_V[0], dv_ref[...])]
        pieces += [(_HG[0] + 1024 * j, r[...]) for j, r in enumerate((d0, d1, d2, d3))]
        pieces += [(_GATES[0], dga_ref[...]), (_GATES[0] + 1024, dgh_ref[...])]
        for start, val in pieces:
            width = val.shape[1]
            dbin_ref[:, start:start + width] += jnp.sum(val.astype(F32), axis=0, keepdims=True)
            vb = val.astype(BF16)
            dproj_ref[:, start:start + width] = vb
            acc = acc + _dot(vb, w[start:start + width, :])
        dy_ref[...] = acc

    row = lambda wd: pl.BlockSpec((tm, wd), lambda i: (i, 0))
    return pl.pallas_call(
        body, name=name, grid=(T // tm,),
        in_specs=[row(D), row(256), row(256)] + [row(D)] * 4 + [row(D), row(D), row(D), HBM_SPEC] + [row(128)] * 3,
        out_specs=[row(D), row(DIN), pl.BlockSpec((1, DIN), lambda i: (0, 0))],
        out_shape=[jax.ShapeDtypeStruct((T, D), F32), jax.ShapeDtypeStruct((T, DIN), BF16),
                   jax.ShapeDtypeStruct((1, DIN), F32)],
        scratch_shapes=[pltpu.VMEM((DIN, D), BF16), pltpu.SemaphoreType.DMA((NDEV,))],
        compiler_params=_params("arbitrary"),
    )(dq, dk, dv, *dhg, *dgates, dz2, wall, *ropes)


def _halves(t):
    lane = lax.broadcasted_iota(jnp.int32, t.shape, 1)
    low = lane < HEAD_DIM
    sw = pltpu.roll(t, HEAD_DIM, 1)
    zero = jnp.zeros_like(t)
    h0 = (jnp.where(low, t, zero), jnp.where(low, zero, sw))
    h1 = (jnp.where(low, sw, zero), jnp.where(low, zero, t))
    return h0, h1


def _lane_stack(p_ref, c_ref, gp):
    sl = slice(128 * gp, 128 * (gp + 1))
    hp, hc = _halves(p_ref[:, sl].astype(F32)), _halves(c_ref[:, sl].astype(F32))
    return [jnp.concatenate([hp[gg][0], hc[gg][0], hp[gg][1], hc[gg][1]], axis=0).astype(BF16) for gg in range(2)]


def _row_stack(tp_ref, tc_ref, g):
    band = jnp.concatenate([tp_ref[64 * g:64 * (g + 1), :], tc_ref[64 * g:64 * (g + 1), :]], axis=1)
    z = jnp.zeros_like(band)
    return [jnp.concatenate([band, z], axis=0), jnp.concatenate([z, band], axis=0)]


def _fold(prevm, t4, hh):
    return jnp.where(prevm, t4[256 * hh:256 * hh + 128, :], t4[256 * hh + 128:256 * hh + 256, :])


def _unfold(prevm, t):
    return jnp.concatenate([jnp.where(prevm, t, 0.0), jnp.where(prevm, 0.0, t)], axis=0).astype(BF16)


def _attn_softmax(s, kill, sink):
    s = jnp.where(kill, NEG_INF, s)
    m = jnp.maximum(jnp.max(s, axis=0, keepdims=True), sink)
    p = jnp.exp(s - m)
    es = jnp.exp(sink - m)
    denom = jnp.sum(p, axis=0, keepdims=True) + es
    return p / denom, es / denom


def _attn_masks(first):
    row = lax.broadcasted_iota(jnp.int32, (ATTN_BLOCK, 2 * ATTN_BLOCK), 0)
    lane = lax.broadcasted_iota(jnp.int32, (ATTN_BLOCK, 2 * ATTN_BLOCK), 1)
    prevm = row > lane % ATTN_BLOCK
    return prevm, jnp.logical_and(first, prevm)


def _pair_rows(ref, g):
    return jnp.concatenate([ref[:, 256 * g:256 * g + 128], ref[:, 256 * g + 128:256 * (g + 1)]], axis=0)


def _pair_sinks(sink_ref, g, hh):
    h0, h1 = 4 * g + hh, 4 * g + 2 + hh
    return jnp.concatenate([jnp.broadcast_to(sink_ref[:, h0:h0 + 1], (1, ATTN_BLOCK)),
                            jnp.broadcast_to(sink_ref[:, h1:h1 + 1], (1, ATTN_BLOCK))], axis=1)


def _attn_fwd(name, qkv, vt, sinks):
    T = qkv.shape[0]
    nb = T // ATTN_BLOCK
    scale = HEAD_DIM ** -0.5

    def body(q_ref, kp_ref, kc_ref, vtp_ref, vtc_ref, sink_ref, o_ref):
        prevm, kill = _attn_masks(pl.program_id(0) == 0)
        kst = _lane_stack(kp_ref, kc_ref, 0) + _lane_stack(kp_ref, kc_ref, 1)
        vts = [_row_stack(vtp_ref, vtc_ref, g) for g in range(N_KV_HEADS)]
        s8 = [_dot_nt(kst[g], _pair_rows(q_ref, g)) * scale for g in range(N_KV_HEADS)]
        pu = [[_unfold(prevm, _attn_softmax(_fold(prevm, s8[g], hh), kill, _pair_sinks(sink_ref, g, hh))[0])
               for hh in range(2)] for g in range(N_KV_HEADS)]
        for g in range(N_KV_HEADS):
            ot = _dot(vts[g][0], pu[g][0]) + _dot(vts[g][1], pu[g][1])
            o_ref[:, 256 * g:256 * g + 128] = ot[:, :ATTN_BLOCK].T.astype(BF16)
            o_ref[:, 256 * g + 128:256 * (g + 1)] = ot[:, ATTN_BLOCK:].T.astype(BF16)

    prev = lambda i: jnp.maximum(i - 1, 0)
    return pl.pallas_call(
        body, name=name, grid=(nb,),
        in_specs=[pl.BlockSpec((ATTN_BLOCK, D), lambda i: (i, 0)),
                  pl.BlockSpec((ATTN_BLOCK, 256), lambda i: (prev(i), 4)),
                  pl.BlockSpec((ATTN_BLOCK, 256), lambda i: (i, 4)),
                  pl.BlockSpec((256, ATTN_BLOCK), lambda i: (0, prev(i))),
                  pl.BlockSpec((256, ATTN_BLOCK), lambda i: (0, i)),
                  pl.BlockSpec((1, N_Q_HEADS), lambda i: (0, 0))],
        out_specs=pl.BlockSpec((ATTN_BLOCK, D), lambda i: (i, 0)),
        out_shape=jax.ShapeDtypeStruct((T, D), BF16),
        compiler_params=_params("arbitrary"),
    )(qkv, qkv, qkv, vt, vt, sinks)


def _attn_bwd(name, qkv, kt, sinks, dya, carry=None):
    T = qkv.shape[0]
    nb = T // ATTN_BLOCK
    scale = HEAD_DIM ** -0.5

    def body(q_ref, kp_ref, kc_ref, vp_ref, vc_ref, ktp_ref, ktc_ref, sink_ref, do_ref,
             dq_ref, dk_ref, dv_ref, ds_ref):
        i = pl.program_id(0)

        @pl.when(i == 0)
        def _():
            ds_ref[...] = jnp.zeros_like(ds_ref)

        prevm, kill = _attn_masks(i == 0)
        kst = _lane_stack(kp_ref, kc_ref, 0) + _lane_stack(kp_ref, kc_ref, 1)
        vst = _lane_stack(vp_ref, vc_ref, 0) + _lane_stack(vp_ref, vc_ref, 1)
        kts = [_row_stack(ktp_ref, ktc_ref, g) for g in range(N_KV_HEADS)]
        lane = lax.broadcasted_iota(jnp.int32, (2 * ATTN_BLOCK, 128), 1)
        low = lane < HEAD_DIM
        slane = lax.broadcasted_iota(jnp.int32, (1, 128), 1)
        dkz = [jnp.zeros((2 * ATTN_BLOCK, 128), F32) for _ in range(N_KV_HEADS)]
        dvz = [jnp.zeros((2 * ATTN_BLOCK, 128), F32) for _ in range(N_KV_HEADS)]
        dsink = jnp.zeros((1, 128), F32)
        groups = range(N_KV_HEADS)
        qcat = [_pair_rows(q_ref, g) for g in groups]
        docat = [_pair_rows(do_ref, g) for g in groups]
        s8 = [_dot_nt(kst[g], qcat[g]) * scale for g in groups]
        dp8 = [_dot_nt(vst[g], docat[g]) for g in groups]
        ds_u = [[None, None] for _ in groups]
        p_u = [[None, None] for _ in groups]
        for g in groups:
            for hh in range(2):
                pn, ps = _attn_softmax(_fold(prevm, s8[g], hh), kill, _pair_sinks(sink_ref, g, hh))
                dp = _fold(prevm, dp8[g], hh)
                delta = jnp.sum(pn * dp, axis=0, keepdims=True)
                sd = ps * delta
                dsink = dsink + jnp.where(slane == 4 * g + hh, -jnp.sum(sd[:, :ATTN_BLOCK]), 0.0) \
                    + jnp.where(slane == 4 * g + 2 + hh, -jnp.sum(sd[:, ATTN_BLOCK:]), 0.0)
                ds_u[g][hh] = _unfold(prevm, pn * (dp - delta) * scale)
                p_u[g][hh] = _unfold(prevm, pn)
        for g in groups:
            dqt = _dot(kts[g][0], ds_u[g][0]) + _dot(kts[g][1], ds_u[g][1])
            dq_ref[:, 256 * g:256 * g + 128] = dqt[:, :ATTN_BLOCK].T
            dq_ref[:, 256 * g + 128:256 * (g + 1)] = dqt[:, ATTN_BLOCK:].T
            for hh in range(2):
                own = low if hh == 0 else jnp.logical_not(low)
                dk_h = jnp.where(own, _dot(ds_u[g][hh], qcat[g]), 0.0)
                dv_h = jnp.where(own, _dot(p_u[g][hh], docat[g]), 0.0)
                if hh != g % 2:
                    dk_h = pltpu.roll(dk_h, HEAD_DIM, 1)
                    dv_h = pltpu.roll(dv_h, HEAD_DIM, 1)
                dkz[g] = dkz[g] + dk_h
                dvz[g] = dvz[g] + dv_h
        ds_ref[...] += dsink
        cur = pl.ds(pl.multiple_of(i * ATTN_BLOCK, ATTN_BLOCK), ATTN_BLOCK)
        prv = pl.ds(pl.multiple_of(jnp.maximum(i - 1, 0) * ATTN_BLOCK, ATTN_BLOCK), ATTN_BLOCK)
        for gp in range(N_KV_HEADS // 2):
            cols = slice(128 * gp, 128 * (gp + 1))
            dkb = dkz[2 * gp] + dkz[2 * gp + 1]
            dvb = dvz[2 * gp] + dvz[2 * gp + 1]
            dk_ref[cur, cols] = dkb[ATTN_BLOCK:, :]
            dv_ref[cur, cols] = dvb[ATTN_BLOCK:, :]

            @pl.when(i > 0)
            def _():
                dk_ref[prv, cols] += dkb[:ATTN_BLOCK, :]
                dv_ref[prv, cols] += dvb[:ATTN_BLOCK, :]

    prev = lambda i: jnp.maximum(i - 1, 0)
    whole = lambda w: pl.BlockSpec((T, w), lambda i: (0, 0))
    return _call(
        body, name=name, grid=(nb,), scratch_shapes=[], ins=[qkv, qkv, qkv, qkv, qkv, kt, kt, sinks, dya], carry=carry,
        in_specs=[pl.BlockSpec((ATTN_BLOCK, D), lambda i: (i, 0)),
                  pl.BlockSpec((ATTN_BLOCK, 256), lambda i: (prev(i), 4)),
                  pl.BlockSpec((ATTN_BLOCK, 256), lambda i: (i, 4)),
                  pl.BlockSpec((ATTN_BLOCK, 256), lambda i: (prev(i), 5)),
                  pl.BlockSpec((ATTN_BLOCK, 256), lambda i: (i, 5)),
                  pl.BlockSpec((256, ATTN_BLOCK), lambda i: (0, prev(i))),
                  pl.BlockSpec((256, ATTN_BLOCK), lambda i: (0, i)),
                  pl.BlockSpec((1, N_Q_HEADS), lambda i: (0, 0)),
                  pl.BlockSpec((ATTN_BLOCK, D), lambda i: (i, 0))],
        out_specs=[pl.BlockSpec((ATTN_BLOCK, D), lambda i: (i, 0)), whole(256), whole(256),
                   pl.BlockSpec((1, 128), lambda i: (0, 0))],
        out_shape=[jax.ShapeDtypeStruct((T, D), F32), jax.ShapeDtypeStruct((T, 256), F32),
                   jax.ShapeDtypeStruct((T, 256), F32), jax.ShapeDtypeStruct((1, 128), F32)])


HG_FWD = (256, 8, 512)
HG_BWD = (128, 8, 512)


def _chunk_sum(tri, x):
    w = x.shape[1]
    h1 = x.astype(BF16)
    h2 = (x - h1.astype(F32)).astype(BF16)
    r = _dot(tri, jnp.concatenate([h1, h2], axis=1))
    return r[:, :w] + r[:, w:]


def _chunk_tri(n):
    ri = lax.broadcasted_iota(jnp.int32, (n, n), 0)
    ci = lax.broadcasted_iota(jnp.int32, (n, n), 1)
    return jnp.where((ri >= ci) & (ri // HGRN_CHUNK == ci // HGRN_CHUNK), 1.0, 0.0).astype(BF16)


def _chunks(t):
    return [t[HGRN_CHUNK * c:HGRN_CHUNK * (c + 1), :] for c in range(t.shape[0] // HGRN_CHUNK)]


def _lower_bound(lbl_ref):
    l0, l1 = lbl_ref[0:1, :], lbl_ref[1:2, :]
    m = jnp.maximum(l0, l1)
    e0, e1 = jnp.exp(l0 - m), jnp.exp(l1 - m)
    return e0 / (e0 + e1)


def _heads(t):
    return [t[:, 128 * h:128 * (h + 1)] for h in range(t.shape[1] // 128)]


def _per_head(fn, *wide):
    return jnp.concatenate([fn(*parts) for parts in zip(*[_heads(t) for t in wide])], axis=1)


def _hgrn_sub(fl, qh, vv, lb, tril_b):
    w = fl.shape[1]
    sg = _sigmoid(fl)
    f = lb + (1.0 - lb) * sg
    k = 1.0 - f
    gc = _chunk_sum(tril_b, jnp.log(f))
    gl_c = [t[HGRN_CHUNK - 1:HGRN_CHUNK, :] for t in _chunks(gc)]
    gl = jnp.concatenate([jnp.broadcast_to(g, (HGRN_CHUNK, w)) for g in gl_c], axis=0)
    sq = _sigmoid(qh)
    eg = jnp.exp(gc)
    eng = jnp.exp(-gc)
    elg = jnp.exp(gl - gc)
    qd = qh * sq * eg
    ki = k * eng
    ke = k * elg
    qd_b, ki_b, ke_b, v_b = (t.astype(BF16) for t in (qd, ki, ke, vv))
    am_b = [jnp.where(tril_b > 0, _dot_nt(qq, kk), 0.0).astype(BF16) for qq, kk in zip(_heads(qd_b), _heads(ki_b))]
    return dict(sg=sg, f=f, sq=sq, eg=eg, eng=eng, elg=elg, qd=qd, ki=ki, ke=ke, egl=[jnp.exp(g) for g in gl_c],
                am_b=am_b, qd_b=qd_b, ki_b=ki_b, ke_b=ke_b, v_b=v_b)


def _hgrn_out(q, st_b):
    outs = []
    for h, (qd_h, v_h) in enumerate(zip(_heads(q["qd_b"]), _heads(q["v_b"]))):
        inter = [_dot_nt(qc, s) for qc, s in zip(_chunks(qd_h), st_b[h])]
        outs.append(_dot(q["am_b"][h], v_h) + jnp.concatenate(inter, axis=0))
    return jnp.concatenate(outs, axis=1)


def _head_rms(o):
    return _per_head(lambda t: jnp.broadcast_to(
        lax.rsqrt(jnp.mean(t * t, axis=-1, keepdims=True) + RMS_EPS), t.shape), o)


def _hgrn_fwd(name, hg, lb_logits, norm_g, cfg):
    T = hg.shape[0]
    sub_rows, hps, th = min(cfg[0], T), cfg[1], min(cfg[2], T)
    cps = sub_rows // HGRN_CHUNK
    nc = th // HGRN_CHUNK
    tri = _chunk_tri(sub_rows)

    def body(fl_ref, qh_ref, v_ref, og_ref, lbl_ref, ng_ref, tri_ref, y_ref, ss_ref, st):
        @pl.when(pl.program_id(1) == 0)
        def _():
            st[...] = jnp.zeros_like(st)

        lbs = _lower_bound(lbl_ref)
        ng = jnp.tile(ng_ref[...], (1, hps))

        def sub(si, carry):
            rows = pl.ds(pl.multiple_of(si * sub_rows, sub_rows), sub_rows)
            q = _hgrn_sub(fl_ref[rows, :], qh_ref[rows, :], v_ref[rows, :], lbs, tri_ref[...])
            v_hc = [_chunks(t) for t in _heads(q["v_b"])]
            k_hc = [_chunks(t) for t in _heads(q["ke_b"])]
            s = [st[h] for h in range(hps)]
            st_b = [[] for _ in range(hps)]
            for c in range(cps):
                egl = _heads(q["egl"][c])
                for h in range(hps):
                    ss_ref[h, si * cps + c] = s[h]
                    st_b[h].append(s[h].astype(BF16))
                    s[h] = s[h] * egl[h] + _dot_tn(v_hc[h][c], k_hc[h][c])
            for h in range(hps):
                st[h] = s[h]
            o = _hgrn_out(q, st_b)
            og = og_ref[rows, :]
            y_ref[rows, :] = (o * _head_rms(o) * ng * (og * _sigmoid(og))).astype(BF16)
            return carry

        lax.fori_loop(0, th // sub_rows, sub, 0)

    wd = 128 * hps
    col = lambda j: pl.BlockSpec((th, wd), lambda h, t: (t, (8 // hps) * j + h))
    return pl.pallas_call(
        body, name=name, grid=(HGRN_HEADS // hps, T // th),
        in_specs=[col(0), col(1), col(2), col(3), pl.BlockSpec((2, wd), lambda h, t: (0, h)),
                  pl.BlockSpec((1, 128), lambda h, t: (0, 0)), pl.BlockSpec((sub_rows, sub_rows), lambda h, t: (0, 0))],
        out_specs=[pl.BlockSpec((th, wd), lambda h, t: (t, h)),
                   pl.BlockSpec((hps, nc, 128, 128), lambda h, t: (h, t, 0, 0))],
        out_shape=[jax.ShapeDtypeStruct((T, D), BF16),
                   jax.ShapeDtypeStruct((HGRN_HEADS, T // HGRN_CHUNK, 128, 128), F32)],
        scratch_shapes=[pltpu.VMEM((hps, 128, 128), F32)],
        compiler_params=_params("parallel", "arbitrary"),
    )(hg, hg, hg, hg, lb_logits, norm_g, tri)


def _hgrn_bwd(name, hg, lb_logits, norm_g, sstart, dyh, cfg, carry=None):
    T = hg.shape[0]
    sub_rows, hps, th = min(cfg[0], T), cfg[1], min(cfg[2], T)
    cps = sub_rows // HGRN_CHUNK
    nc = th // HGRN_CHUNK
    nt = T // th
    wd = 128 * hps
    tri = _chunk_tri(sub_rows)
    tri_t = tri.T

    def body(fl_ref, qh_ref, v_ref, og_ref, lbl_ref, ng_ref, tri_ref, trit_ref, ss_ref, dy_ref,
             dfl_ref, dqh_ref, dv_ref, dog_ref, dlb_ref, dng_ref, dst):
        @pl.when(pl.program_id(1) == 0)
        def _():
            dst[...] = jnp.zeros_like(dst)
            dlb_ref[...] = jnp.zeros_like(dlb_ref)
            dng_ref[...] = jnp.zeros_like(dng_ref)

        lb = _lower_bound(lbl_ref)
        ng = jnp.tile(ng_ref[...], (1, hps))
        last = lax.broadcasted_iota(jnp.int32, (HGRN_CHUNK, wd), 0) == HGRN_CHUNK - 1
        nsub = th // sub_rows
        cat0 = lambda parts: jnp.concatenate(parts, axis=0)

        def sub(step, carry):
            si = nsub - 1 - step
            rows = pl.ds(pl.multiple_of(si * sub_rows, sub_rows), sub_rows)
            qh, og, dy = qh_ref[rows, :], og_ref[rows, :], dy_ref[rows, :]
            tril_b = tri_ref[...]
            q = _hgrn_sub(fl_ref[rows, :], qh, v_ref[rows, :], lb, tril_b)
            s_in = [[ss_ref[h, si * cps + c] for c in range(cps)] for h in range(hps)]
            st_b = [[s.astype(BF16) for s in row] for row in s_in]
            o = _hgrn_out(q, st_b)
            rr = _head_rms(o)
            oh = o * rr
            sog = _sigmoid(og)
            dog_ref[rows, :] = (dy * oh * ng * (sog * (1.0 + og * (1.0 - sog)))).astype(BF16)
            don = dy * (og * sog)
            dng_w = jnp.sum(don * oh, axis=0, keepdims=True)
            dd = don * ng
            mean_h = _per_head(lambda t: jnp.broadcast_to(jnp.mean(t, axis=-1, keepdims=True), t.shape), dd * oh)
            do_b = (rr * (dd - oh * mean_h)).astype(BF16)
            do_h, qd_h, ki_h, ke_h, v_h = (_heads(t) for t in (do_b, q["qd_b"], q["ki_b"], q["ke_b"], q["v_b"]))
            da_b = [jnp.where(tril_b > 0, _dot_nt(do_h[h], v_h[h]), 0.0).astype(BF16) for h in range(hps)]
            do_c, qd_c, ke_c, v_c = ([_chunks(t) for t in hs] for hs in (do_h, qd_h, ke_h, v_h))
            dsp = [[None] * cps for _ in range(hps)]
            dgl_dec = [[None] * cps for _ in range(hps)]
            d = [dst[h] for h in range(hps)]
            for c in reversed(range(cps)):
                egl = _heads(q["egl"][c])
                for h in range(hps):
                    dsp[h][c] = d[h]
                    dgl_dec[h][c] = jnp.sum(d[h] * s_in[h][c], axis=0, keepdims=True) * egl[h]
                    d[h] = d[h] * egl[h] + _dot_tn(do_c[h][c], qd_c[h][c])
            for h in range(hps):
                dst[h] = d[h]
                dng_ref[h] += dng_w[:, 128 * h:128 * (h + 1)]
            dsp_b = [[t.astype(BF16) for t in row] for row in dsp]
            dv_ref[rows, :] = jnp.concatenate(
                [_dot_tn(q["am_b"][h], do_h[h]) + cat0([_dot_nt(ke_c[h][c], dsp_b[h][c]) for c in range(cps)])
                 for h in range(hps)], axis=1).astype(BF16)
            dqd = jnp.concatenate(
                [_dot(da_b[h], ki_h[h]) + cat0([_dot(do_c[h][c], st_b[h][c]) for c in range(cps)])
                 for h in range(hps)], axis=1)
            dki = jnp.concatenate([_dot_tn(da_b[h], qd_h[h]) for h in range(hps)], axis=1)
            dke = jnp.concatenate([cat0([_dot(v_c[h][c], dsp_b[h][c]) for c in range(cps)]) for h in range(hps)], axis=1)
            dkk = dke * q["ke"]
            dgl = [jnp.sum(t, axis=0, keepdims=True) + jnp.concatenate([dgl_dec[h][c] for h in range(hps)], axis=1)
                   for c, t in enumerate(_chunks(dkk))]
            dgc = dqd * q["qd"] - dki * q["ki"] - dkk + cat0([jnp.where(last, g, 0.0) for g in dgl])
            dlf = _chunk_sum(trit_ref[...], dgc)
            df = dlf / q["f"] - (dki * q["eng"] + dke * q["elg"])
            sg = q["sg"]
            dlb_ref[...] += jnp.sum(df * (1.0 - sg), axis=0, keepdims=True)
            dfl_ref[rows, :] = (df * (1.0 - lb) * sg * (1.0 - sg)).astype(BF16)
            sq = q["sq"]
            dqh_ref[rows, :] = (dqd * q["eg"] * (sq * (1.0 + qh * (1.0 - sq)))).astype(BF16)
            return carry

        lax.fori_loop(0, nsub, sub, 0)
    col = lambda j: pl.BlockSpec((th, wd), lambda h, t: (nt - 1 - t, (8 // hps) * j + h))
    out = pl.BlockSpec((th, wd), lambda h, t: (nt - 1 - t, h))
    tri_spec = pl.BlockSpec((sub_rows, sub_rows), lambda h, t: (0, 0))
    return _call(
        body, name=name, grid=(HGRN_HEADS // hps, nt),
        in_specs=[col(0), col(1), col(2), col(3), pl.BlockSpec((2, wd), lambda h, t: (0, h)),
                  pl.BlockSpec((1, 128), lambda h, t: (0, 0)), tri_spec, tri_spec,
                  pl.BlockSpec((hps, nc, 128, 128), lambda h, t: (h, nt - 1 - t, 0, 0)), out],
        out_specs=[out, out, out, out, pl.BlockSpec((1, wd), lambda h, t: (0, h)),
                   pl.BlockSpec((hps, 1, 128), lambda h, t: (h, 0, 0))],
        out_shape=[jax.ShapeDtypeStruct((T, D), BF16)] * 4 + [jax.ShapeDtypeStruct((1, D), F32),
                                                              jax.ShapeDtypeStruct((HGRN_HEADS, 1, 128), F32)],
        scratch_shapes=[pltpu.VMEM((hps, 128, 128), F32)],
        ins=[hg, hg, hg, hg, lb_logits, norm_g, tri, tri_t, sstart, dyh], carry=carry)


def _mix_fwd(name, ya, yh, gates, xh1, g1, b1, wall, tm):
    T = ya.shape[0]

    def body(ya_ref, yh_ref, ga_ref, gh_ref, xh_ref, g_ref, b_ref, wall_ref,
             xo_ref, r_ref, pa_ref, ph_ref, mg_ref, wpa, wph, wo, sem):
        @pl.when(pl.program_id(0) == 0)
        def _():
            _load_rows(wall_ref, [(wpa, RIN, RP), (wph, RIN + RP, RP), (wo, RIN + 2 * RP, RP)], sem)

        parts = [slice(p, min(p + PART_ROWS, tm)) for p in range(0, tm, PART_ROWS)]
        pas = [_dot(ya_ref[rs, :], wpa[...]).astype(BF16) for rs in parts]
        phs = [_dot(yh_ref[rs, :], wph[...]).astype(BF16) for rs in parts]
        mgs = []
        for rs, pa, ph in zip(parts, pas, phs):
            pa_ref[rs, :] = pa
            ph_ref[rs, :] = ph
            mgs.append((_sigmoid(ga_ref[rs, :]) * pa.astype(F32) + _sigmoid(gh_ref[rs, :]) * ph.astype(F32)).astype(BF16))
            mg_ref[rs, :] = mgs[-1]
        mixes = [_dot(mg, wo[...]) for mg in mgs]
        for rs, mix in zip(parts, mixes):
            y1 = xh_ref[rs, :] * g_ref[...] + b_ref[...]
            xh2, r = _ln_fwd(ALPHA * y1 + mix)
            xo_ref[rs, :] = xh2
            r_ref[rs, :] = jnp.broadcast_to(r, (xh2.shape[0], 128))

    row = lambda w: pl.BlockSpec((tm, w), lambda i: (i, 0))
    vec = pl.BlockSpec((1, D), lambda i: (0, 0))
    return pl.pallas_call(
        body, name=name, grid=(T // tm,),
        in_specs=[row(D), row(D), pl.BlockSpec((tm, D), lambda i: (i, 0)), pl.BlockSpec((tm, D), lambda i: (i, 1)),
                  row(D), vec, vec, HBM_SPEC],
        out_specs=[row(D), row(128), row(D), row(D), row(D)],
        out_shape=[jax.ShapeDtypeStruct((T, D), F32), jax.ShapeDtypeStruct((T, 128), F32)]
        + [jax.ShapeDtypeStruct((T, D), BF16)] * 3,
        scratch_shapes=[pltpu.VMEM((D, D), BF16)] * 3 + [pltpu.SemaphoreType.DMA((3 * NDEV,))],
        compiler_params=_params("arbitrary"),
    )(ya, yh, gates, gates, xh1, g1, b1, wall)


def _mix_bwd(name, dy2, xh2, r2, g2, gates, pa, ph, wall, tm):
    T = dy2.shape[0]

    def body(dy_ref, xh_ref, r_ref, g_ref, ga_ref, gh_ref, pa_ref, ph_ref, wall_ref,
             dz_ref, dmix_ref, dpa_ref, dph_ref, dga_ref, dgh_ref, dya_ref, dyh_ref, dg_ref, db_ref,
             wpa, wph, wo, sem):
        @pl.when(pl.program_id(0) == 0)
        def _():
            _load_rows(wall_ref, [(wpa, RIN, RP), (wph, RIN + RP, RP), (wo, RIN + 2 * RP, RP)], sem)
            dg_ref[...] = jnp.zeros_like(dg_ref)
            db_ref[...] = jnp.zeros_like(db_ref)

        parts = [slice(p, min(p + PART_ROWS, tm)) for p in range(0, tm, PART_ROWS)]
        dmix = []
        for rs in parts:
            dz, dgp, dbp = _ln_bwd(dy_ref[rs, :], xh_ref[rs, :], r_ref[rs, :1], g_ref[...])
            dg_ref[...] += dgp
            db_ref[...] += dbp
            dz_ref[rs, :] = dz
            dmix.append(dz.astype(BF16))
            dmix_ref[rs, :] = dmix[-1]
        dmgs = [_dot_nt(d, wo[...]) for d in dmix]
        dpas, dphs = [], []
        for rs, dmg in zip(parts, dmgs):
            sa, sh = _sigmoid(ga_ref[rs, :]), _sigmoid(gh_ref[rs, :])
            dga_ref[rs, :] = (dmg * pa_ref[rs, :].astype(F32) * sa * (1.0 - sa)).astype(BF16)
            dgh_ref[rs, :] = (dmg * ph_ref[rs, :].astype(F32) * sh * (1.0 - sh)).astype(BF16)
            dpas.append((dmg * sa).astype(BF16))
            dphs.append((dmg * sh).astype(BF16))
            dpa_ref[rs, :] = dpas[-1]
            dph_ref[rs, :] = dphs[-1]
        for rs, dpa, dph in zip(parts, dpas, dphs):
            dya_ref[rs, :] = _dot_nt(dpa, wpa[...]).astype(BF16)
            dyh_ref[rs, :] = _dot_nt(dph, wph[...])

    row = lambda w: pl.BlockSpec((tm, w), lambda i: (i, 0))
    vec = pl.BlockSpec((1, D), lambda i: (0, 0))
    return pl.pallas_call(
        body, name=name, grid=(T // tm,),
        in_specs=[row(D), row(D), row(128), vec, pl.BlockSpec((tm, D), lambda i: (i, 0)),
                  pl.BlockSpec((tm, D), lambda i: (i, 1)), row(D), row(D), HBM_SPEC],
        out_specs=[row(D)] * 8 + [vec, vec],
        out_shape=[jax.ShapeDtypeStruct((T, D), F32)] + [jax.ShapeDtypeStruct((T, D), BF16)] * 3
        + [jax.ShapeDtypeStruct((T, D), BF16)] * 3 + [jax.ShapeDtypeStruct((T, D), F32)]
        + [jax.ShapeDtypeStruct((1, D), F32)] * 2,
        scratch_shapes=[pltpu.VMEM((D, D), BF16)] * 3 + [pltpu.SemaphoreType.DMA((3 * NDEV,))],
        compiler_params=_params("arbitrary"),
    )(dy2, xh2, r2, g2, gates, gates, pa, ph, wall)


def _adam(w, g, m, v):
    m = ADAM_B1 * m + (1.0 - ADAM_B1) * g
    v = ADAM_B2 * v + (1.0 - ADAM_B2) * (g * g)
    m_hat = m / (1.0 - ADAM_B1 ** ADAM_STEP)
    v_hat = v / (1.0 - ADAM_B2 ** ADAM_STEP)
    delta = -ADAM_LR * (m_hat / (jnp.sqrt(v_hat) + ADAM_EPS) + ADAM_WD * w)
    return delta, m, v


def _grad_steps(name, items, cb, carry=None):
    n = len(items)

    def body(*refs):
        ins, outs = refs[:4 * n], refs[4 * n:]
        for i in range(n):
            r_ref, w_ref, m_ref, v_ref = ins[4 * i:4 * i + 4]
            g_ref, d_ref, mo_ref, vo_ref = outs[4 * i:4 * i + 4]
            acc = r_ref[0].astype(F32)
            for k in range(1, NDEV):
                acc = acc + r_ref[k].astype(F32)
            g_ref[...] = acc
            d, mm, vv = _adam(w_ref[...], acc, m_ref[...], v_ref[...])
            d_ref[...] = d
            mo_ref[...] = mm
            vo_ref[...] = vv

    in_specs, out_specs, out_shape, ins = [], [], [], []
    for recv, w, m, v in items:
        rows = recv.shape[1]
        blk = pl.BlockSpec((rows, cb), lambda j: (0, j))
        in_specs += [pl.BlockSpec((NDEV, rows, cb), lambda j: (0, 0, j)), blk, blk, blk]
        out_specs += [blk] * 4
        out_shape += [jax.ShapeDtypeStruct(w.shape, F32)] * 4
        ins += [recv, w, m, v]
    outs, ex = _call(body, name=name, grid=(D // cb,), in_specs=in_specs, out_specs=out_specs, out_shape=out_shape,
                     scratch_shapes=[], ins=ins, carry=carry)
    return [tuple(outs[4 * i:4 * i + 4]) for i in range(n)], ex


_SMALL = [("ln1_g", D), ("ln1_b", D), ("ln2_g", D), ("ln2_b", D), ("ln3_g", D), ("ln3_b", D),
          ("b_in", DIN), ("lb", D), ("attn_sinks", 128), ("hgrn_norm_g", 128), ("loss", 128)]
_SMALL_OFF = {}
_o = 0
for _n, _w in _SMALL:
    _SMALL_OFF[_n] = (_o, _w)
    _o += _w
PACK = _o


def _small_reduce(name, packed):
    def body(p_ref, o_ref, buf, send_sems, recv_sems):
        x, y, c = lax.axis_index("x"), lax.axis_index("y"), lax.axis_index("c")
        me = 4 * x + 2 * y + c
        buf[me] = p_ref[...]
        copies = []
        for j in range(1, NDEV):
            px, py, pc = x ^ (j >> 2), y ^ ((j >> 1) & 1), c ^ (j & 1)
            cp = pltpu.make_async_remote_copy(
                src_ref=p_ref, dst_ref=buf.at[me], send_sem=send_sems.at[j - 1], recv_sem=recv_sems.at[j - 1],
                device_id=(px, py, pc), device_id_type=MESH)
            cp.start()
            copies.append(cp)
        for cp in copies:
            cp.wait()
        tot = buf[0]
        for k in range(1, NDEV):
            tot = tot + buf[k]
        o_ref[...] = tot

    return pl.pallas_call(
        body, name=name, out_shape=jax.ShapeDtypeStruct((1, PACK), F32),
        scratch_shapes=[pltpu.VMEM((NDEV, 1, PACK), F32), pltpu.SemaphoreType.DMA((NDEV - 1,)),
                        pltpu.SemaphoreType.DMA((NDEV - 1,))],
        compiler_params=pltpu.CompilerParams(vmem_limit_bytes=VMEM_LIMIT),
    )(packed)


def _small_step(name, total, small_w, small_m, small_v):
    names = ["ln1_g", "ln1_b", "ln2_g", "ln2_b", "ln3_g", "ln3_b", "b_in", "attn_sinks", "hgrn_lb_logits", "hgrn_norm_g"]
    np_ = len(names)

    def body(*refs):
        t_ref = refs[0]
        w_refs = refs[1:1 + np_]
        m_refs = refs[1 + np_:1 + 2 * np_]
        v_refs = refs[1 + 2 * np_:1 + 3 * np_]
        loss_ref, outs = refs[1 + 3 * np_], refs[2 + 3 * np_:]

        def part(n):
            o, w = _SMALL_OFF[n]
            return t_ref[:, o:o + w]

        loss_ref[...] = part("loss")
        for i, n in enumerate(names):
            w = w_refs[i][...]
            if n == "hgrn_lb_logits":
                m_ = jnp.maximum(w[0:1, :], w[1:2, :])
                e0, e1 = jnp.exp(w[0:1, :] - m_), jnp.exp(w[1:2, :] - m_)
                p0 = e0 / (e0 + e1)
                t = p0 * (1.0 - p0) * part("lb")
                g = jnp.concatenate([t, -t], axis=0)
            elif n == "attn_sinks":
                g = part(n)[:, :N_Q_HEADS]
            else:
                g = part(n)
            d, mm, vv = _adam(w, g, m_refs[i][...], v_refs[i][...])
            outs[4 * i][...] = g
            outs[4 * i + 1][...] = d
            outs[4 * i + 2][...] = mm
            outs[4 * i + 3][...] = vv

    out_shape = [jax.ShapeDtypeStruct((1, 128), F32)]
    for n in names:
        out_shape += [jax.ShapeDtypeStruct(small_w[n].shape, F32)] * 4
    return pl.pallas_call(
        body, name=name, out_shape=out_shape,
        compiler_params=pltpu.CompilerParams(vmem_limit_bytes=VMEM_LIMIT),
    )(total, *[small_w[n] for n in names], *[small_m[n] for n in names], *[small_v[n] for n in names]), names


def _tile(T, pref):
    return min(T, pref)


def kernel(x, ln1_g, ln1_b, ffn1_w1, ffn1_w3, ffn1_w2, ln2_g, ln2_b, w_in, b_in, attn_sinks, hgrn_lb_logits, hgrn_norm_g, w_proj_attn, w_proj_hgrn, w_out, ln3_g, ln3_b, ffn2_w1, ffn2_w3, ffn2_w2, loss_target, m_ln1_g, m_ln1_b, m_ffn1_w1, m_ffn1_w3, m_ffn1_w2, m_ln2_g, m_ln2_b, m_w_in, m_b_in, m_attn_sinks, m_hgrn_lb_logits, m_hgrn_norm_g, m_w_proj_attn, m_w_proj_hgrn, m_w_out, m_ln3_g, m_ln3_b, m_ffn2_w1, m_ffn2_w3, m_ffn2_w2, v_ln1_g, v_ln1_b, v_ffn1_w1, v_ffn1_w3, v_ffn1_w2, v_ln2_g, v_ln2_b, v_w_in, v_b_in, v_attn_sinks, v_hgrn_lb_logits, v_hgrn_norm_g, v_w_proj_attn, v_w_proj_hgrn, v_w_out, v_ln3_g, v_ln3_b, v_ffn2_w1, v_ffn2_w3, v_ffn2_w2):
    T = x.shape[1]
    xs = x[0]
    tgt = loss_target[0]
    tm = _tile(T, 256)
    tm2 = _tile(T, 512)
    tk = _tile(T, 2048)

    t_bf = lambda w: w[0].T.astype(BF16)
    n_bf = lambda w: w[0].astype(BF16)
    ffn_shard = lambda w1, w3, w2: jnp.concatenate([t_bf(w1), t_bf(w3), n_bf(w2)], axis=0)
    mix_shard = jnp.concatenate([t_bf(w_in), n_bf(w_proj_attn), n_bf(w_proj_hgrn), n_bf(w_out)], axis=0)
    (ffn1_all,) = _exchange_call("gather_ffn1", _gather_exchange(ffn_shard(ffn1_w1, ffn1_w3, ffn1_w2)))
    ffn_offs = (0, RF, 2 * RF)
    ropes = _rope_tables(T)

    (xh1, r1, a1, b1, xb0), (mix_all,) = _ffn_fwd("ffn1_fwd", xs, None, ffn1_all, ffn_offs, tm2,
                                                  carry=_gather_exchange(mix_shard))
    (qkv, kt, vt, hg, gates, y1b), (ffn2_all,) = _inproj_fwd(
        "inproj_fwd", xh1, ln1_g, ln1_b, mix_all, b_in, ropes, tm2,
        carry=_gather_exchange(ffn_shard(ffn2_w1, ffn2_w3, ffn2_w2)))
    ya = _attn_fwd("attn_fwd", qkv, vt, attn_sinks)
    yh, sstart = _hgrn_fwd("hgrn_fwd", hg, hgrn_lb_logits, hgrn_norm_g, HG_FWD)
    xh2, r2, pa, ph, merged = _mix_fwd("mix_fwd", ya, yh, gates, xh1, ln1_g, ln1_b, mix_all, tm2)
    (xh3, r3, a2, b2, y2b, loss_part), _ = _ffn_fwd("ffn2_fwd", xh2, (ln2_g, ln2_b), ffn2_all, ffn_offs, tm2,
                                                    loss=(ln3_g, ln3_b, tgt))

    (dy2, dab2, u2, df2, dg3, db3), _ = _ffn_bwd("ffn2_bwd", ("loss", ln3_b, tgt), xh3, r3, ln3_g, a2, b2, ffn2_all,
                                                 ffn_offs, tm)
    g_ffn2_13, _ = _wgrad("wgrad_ffn2_w13", dab2, y2b, DFF // 2, tk)
    g_ffn2_2, _ = _wgrad("wgrad_ffn2_w2", u2, df2, DFF // 2, tk)
    dz2, dmix, dpa, dph, dga, dgh, dya, dyh, dg2, db2 = _mix_bwd("mix_bwd", dy2, xh2, r2, ln2_g, gates, pa, ph,
                                                                 mix_all, tm2)
    g_wo, _ = _wgrad("wgrad_w_out", merged, dmix, D, tk)
    g_pa, _ = _wgrad("wgrad_w_pa", ya, dpa, D, tk)
    g_ph, _ = _wgrad("wgrad_w_ph", yh, dph, D, tk)
    recv = {}
    (dq, dk, dv, dsink), (recv["ffn2_w1"], recv["ffn2_w3"], recv["ffn2_w2"]) = _attn_bwd(
        "attn_bwd", qkv, kt, attn_sinks, dya,
        carry=_grad_exchange([g_ffn2_13, g_ffn2_13, g_ffn2_2], [0, DFF, 0], [RF, RF, RF]))
    (dfl, dqh, dih, dog, dlb, dng), (recv["w_proj_attn"], recv["w_proj_hgrn"], recv["w_out"]) = _hgrn_bwd(
        "hgrn_bwd", hg, hgrn_lb_logits, hgrn_norm_g, sstart, dyh, HG_BWD,
        carry=_grad_exchange([g_pa, g_ph, g_wo], [0, 0, 0], [RP, RP, RP]))
    dy1, dproj, dbin = _inproj_bwd("inproj_bwd", dq, dk, dv, (dfl, dqh, dih, dog), (dga, dgh), dz2, mix_all, ropes, tm)
    g_win, _ = _wgrad("wgrad_w_in", dproj, y1b, DIN // 4, tk)
    (gx, dab1, u1, df1, dg1, db1), (recv["w_in"],) = _ffn_bwd(
        "ffn1_bwd", ("dy", dy1), xh1, r1, ln1_g, a1, b1, ffn1_all, ffn_offs, tm,
        carry=_grad_exchange([g_win], [0], [RIN]))
    g_ffn1_2, _ = _wgrad("wgrad_ffn1_w2", u1, df1, DFF // 2, tk)
    g_ffn1_1, (recv["ffn1_w2"],) = _wgrad("wgrad_ffn1_w1", dab1, xb0, DFF // 2, tk, 0, DFF,
                                          carry=_grad_exchange([g_ffn1_2], [0], [RF]))
    g_ffn1_3, (recv["ffn1_w1"],) = _wgrad("wgrad_ffn1_w3", dab1, xb0, DFF // 2, tk, DFF, DFF,
                                          carry=_grad_exchange([g_ffn1_1], [0], [RF]))
    (recv["ffn1_w3"],) = _exchange_call("exchange_ffn1_w3", _grad_exchange([g_ffn1_3], [0], [RF]))

    big = [("ffn1_w1", ffn1_w1, m_ffn1_w1, v_ffn1_w1, True), ("ffn1_w3", ffn1_w3, m_ffn1_w3, v_ffn1_w3, True),
           ("ffn1_w2", ffn1_w2, m_ffn1_w2, v_ffn1_w2, False), ("w_in", w_in, m_w_in, v_w_in, True),
           ("w_proj_attn", w_proj_attn, m_w_proj_attn, v_w_proj_attn, False),
           ("w_proj_hgrn", w_proj_hgrn, m_w_proj_hgrn, v_w_proj_hgrn, False),
           ("w_out", w_out, m_w_out, v_w_out, False),
           ("ffn2_w1", ffn2_w1, m_ffn2_w1, v_ffn2_w1, True), ("ffn2_w3", ffn2_w3, m_ffn2_w3, v_ffn2_w3, True),
           ("ffn2_w2", ffn2_w2, m_ffn2_w2, v_ffn2_w2, False)]
    view = lambda t, transposed: t[0].T if transposed else t[0]
    back = lambda t, transposed: t.T[None] if transposed else t[None]
    stepped, _ = _grad_steps("steps", [(recv[n], view(w, tr), view(m, tr), view(v, tr)) for n, w, m, v, tr in big], 128)
    res = {n: tuple(back(t, tr) for t in outs) for (n, _, _, _, tr), outs in zip(big, stepped)}

    parts = {"ln1_g": dg1, "ln1_b": db1, "ln2_g": dg2, "ln2_b": db2, "ln3_g": dg3, "ln3_b": db3, "b_in": dbin,
             "lb": dlb, "attn_sinks": dsink, "hgrn_norm_g": jnp.sum(dng, axis=0), "loss": loss_part[0:1, :]}
    packed = jnp.concatenate([parts[n] for n, _ in _SMALL], axis=1)
    small_w = dict(ln1_g=ln1_g, ln1_b=ln1_b, ln2_g=ln2_g, ln2_b=ln2_b, ln3_g=ln3_g, ln3_b=ln3_b, b_in=b_in,
                   attn_sinks=attn_sinks, hgrn_lb_logits=hgrn_lb_logits, hgrn_norm_g=hgrn_norm_g)
    small_m = dict(ln1_g=m_ln1_g, ln1_b=m_ln1_b, ln2_g=m_ln2_g, ln2_b=m_ln2_b, ln3_g=m_ln3_g, ln3_b=m_ln3_b,
                   b_in=m_b_in, attn_sinks=m_attn_sinks, hgrn_lb_logits=m_hgrn_lb_logits, hgrn_norm_g=m_hgrn_norm_g)
    small_v = dict(ln1_g=v_ln1_g, ln1_b=v_ln1_b, ln2_g=v_ln2_g, ln2_b=v_ln2_b, ln3_g=v_ln3_g, ln3_b=v_ln3_b,
                   b_in=v_b_in, attn_sinks=v_attn_sinks, hgrn_lb_logits=v_hgrn_lb_logits, hgrn_norm_g=v_hgrn_norm_g)
    outs, names = _small_step("small_step", _small_reduce("small_reduce", packed), small_w, small_m, small_v)
    loss = outs[0][0, 0]
    for i, n in enumerate(names):
        res[n] = tuple(outs[1 + 4 * i:5 + 4 * i])

    order = ["ln1_g", "ln1_b", "ffn1_w1", "ffn1_w3", "ffn1_w2", "ln2_g", "ln2_b", "w_in", "b_in", "attn_sinks",
             "hgrn_lb_logits", "hgrn_norm_g", "w_proj_attn", "w_proj_hgrn", "w_out", "ln3_g", "ln3_b",
             "ffn2_w1", "ffn2_w3", "ffn2_w2"]
    return (loss, gx[None], *[res[n][0] for n in order], *[res[n][1] for n in order],
            *[res[n][2] for n in order], *[res[n][3] for n in order])
```

```python
import jax
import jax.numpy as jnp
from jax import lax
from jax.experimental import pallas as pl
from jax.experimental.pallas import tpu as pltpu

F32 = jnp.float32
BF16 = jnp.bfloat16

NDEV = 8
D = 1024
DFF = 2816
RF = DFF // NDEV
DIN = 7680
RIN = DIN // NDEV
RP = D // NDEV
N_Q_HEADS = 16
N_KV_HEADS = 4
HEAD_DIM = 64
ATTN_BLOCK = 128
ROPE_THETA = 500000.0
ROPE_DIM = HEAD_DIM // 4
HGRN_HEADS = 8
HGRN_DK = 128
HGRN_CHUNK = 64
ALPHA = 2.0 ** 0.25
LN_EPS = 1e-5
RMS_EPS = 1e-6
NEG_INF = -1e30
ADAM_LR = 0.001
ADAM_B1 = 0.9
ADAM_B2 = 0.999
ADAM_EPS = 1e-08
ADAM_WD = 0.01
ADAM_STEP = 10

QKV_W = 1536
HG_W = 4096
GATE_W = 2048
VMEM_LIMIT = 60 * 2 ** 20
PART_ROWS = 256

MESH = pl.DeviceIdType.MESH
HBM_SPEC = pl.BlockSpec(memory_space=pltpu.HBM)


def _params(*sem):
    return pltpu.CompilerParams(dimension_semantics=sem, vmem_limit_bytes=VMEM_LIMIT)


def _dot(a, b):
    return jnp.dot(a, b, preferred_element_type=F32)


def _dot_nt(a, b):
    return lax.dot_general(a, b, (((1,), (1,)), ((), ())), preferred_element_type=F32)


def _dot_tn(a, b):
    return lax.dot_general(a, b, (((0,), (0,)), ((), ())), preferred_element_type=F32)


def _sigmoid(x):
    return 0.5 * jnp.tanh(0.5 * x) + 0.5


def _ln_fwd(z):
    mu = jnp.mean(z, axis=-1, keepdims=True)
    zc = z - mu
    var = jnp.mean(zc * zc, axis=-1, keepdims=True)
    r = lax.rsqrt(var + LN_EPS)
    return zc * r, r


def _ln_bwd(dy, xh, r, g):
    dxh = dy * g
    m1 = jnp.mean(dxh, axis=-1, keepdims=True)
    m2 = jnp.mean(dxh * xh, axis=-1, keepdims=True)
    dz = r * (dxh - m1 - xh * m2)
    return dz, jnp.sum(dy * xh, axis=0, keepdims=True), jnp.sum(dy, axis=0, keepdims=True)


def _load_rows(wall_ref, pieces, sem):
    copies = []
    for dst, off, r in pieces:
        for k in range(NDEV):
            c = pltpu.make_async_copy(wall_ref.at[k, pl.ds(off, r), :], dst.at[pl.ds(k * r, r), :], sem.at[len(copies)])
            c.start()
            copies.append(c)
    for c in copies:
        c.wait()


class _Exchange:
    def __init__(self, inputs, out_shape, scratch, begin, middle, end):
        self.inputs, self.out_shape, self.scratch = inputs, out_shape, scratch
        self.begin, self.middle, self.end = begin, middle, end


def _gather_exchange(shard):
    rows, cols = shard.shape

    def ops(ins, outs, scr):
        (x_ref,), (out_ref,), (send_sems, recv_sems, local_sem) = ins, outs, scr
        x, y, c = lax.axis_index("x"), lax.axis_index("y"), lax.axis_index("c")
        me, sibling = (x, y, c), (x, y, 1 - c)
        chips = [(1 - x, y), (x, 1 - y), (1 - x, 1 - y)]

        def slot(px, py, pc):
            return out_ref.at[4 * px + 2 * py + pc]

        def copy(k, block, to, src=None):
            return pltpu.make_async_remote_copy(
                src_ref=slot(*block) if src is None else src, dst_ref=slot(*block),
                send_sem=send_sems.at[k], recv_sem=recv_sems.at[k], device_id=to, device_id_type=MESH)

        mine = lambda: pltpu.make_async_copy(x_ref, slot(*me), local_sem)
        first = lambda: [copy(0, me, sibling, src=x_ref)] + [
            copy(1 + j, me, (*chip, c), src=x_ref) for j, chip in enumerate(chips)]
        passed = lambda: [copy(4 + j, (*chip, c), sibling) for j, chip in enumerate(chips)]
        return c, me, sibling, chips, copy, mine, first, passed

    def begin(*refs):
        _, _, _, _, _, mine, first, _ = ops(*refs)
        mine().start()
        for cp in first():
            cp.start()

    def middle(*refs):
        c, me, _, chips, copy, _, _, passed = ops(*refs)
        for (j, chip), fwd in zip(enumerate(chips), passed()):
            copy(1 + j, (*chip, c), me).wait_recv()
            fwd.start()

    def end(*refs):
        c, me, sibling, chips, copy, mine, first, passed = ops(*refs)
        copy(0, sibling, me).wait_recv()
        for j, chip in enumerate(chips):
            copy(4 + j, (*chip, 1 - c), me).wait_recv()
        for cp in first() + passed():
            cp.wait_send()
        mine().wait()

    return _Exchange([shard], [jax.ShapeDtypeStruct((NDEV, rows, cols), shard.dtype)],
                     [pltpu.SemaphoreType.DMA((7,)), pltpu.SemaphoreType.DMA((7,)), pltpu.SemaphoreType.DMA],
                     begin, middle, end)


def _grad_exchange(grads, bases, rows):
    n = len(grads)

    def copies(g_refs, out_refs, scr):
        send_sems, recv_sems, local_sems = scr
        x, y, c = lax.axis_index("x"), lax.axis_index("y"), lax.axis_index("c")
        me = 4 * x + 2 * y + c
        out = []
        for i in range(n):
            r = rows[i]
            src = lambda k: g_refs[i].at[pl.ds(pl.multiple_of(bases[i] + k * r, 16), r), :]
            out.append(pltpu.make_async_copy(src(me), out_refs[i].at[me], local_sems.at[i]))
            for j in range(1, NDEV):
                px, py, pc = x ^ (j >> 2), y ^ ((j >> 1) & 1), c ^ (j & 1)
                out.append(pltpu.make_async_remote_copy(
                    src_ref=src(4 * px + 2 * py + pc), dst_ref=out_refs[i].at[me],
                    send_sem=send_sems.at[i, j - 1], recv_sem=recv_sems.at[i, j - 1],
                    device_id=(px, py, pc), device_id_type=MESH))
        return out

    def begin(*refs):
        for cp in copies(*refs):
            cp.start()

    def end(*refs):
        for cp in copies(*refs):
            cp.wait()

    return _Exchange(list(grads), [jax.ShapeDtypeStruct((NDEV, r, D), g.dtype) for g, r in zip(grads, rows)],
                     [pltpu.SemaphoreType.DMA((n, NDEV - 1)), pltpu.SemaphoreType.DMA((n, NDEV - 1)),
                      pltpu.SemaphoreType.DMA((n,))], begin, None, end)


def _exchange_call(name, ex):
    ni, no = len(ex.inputs), len(ex.out_shape)

    def body(*refs):
        parts = (refs[:ni], refs[ni:ni + no], refs[ni + no:])
        ex.begin(*parts)
        if ex.middle is not None:
            ex.middle(*parts)
        ex.end(*parts)

    return pl.pallas_call(body, name=name, out_shape=ex.out_shape, in_specs=[HBM_SPEC] * ni, out_specs=[HBM_SPEC] * no,
                          scratch_shapes=ex.scratch)(*ex.inputs)


def _call(body, *, name, grid, in_specs, out_specs, out_shape, scratch_shapes, ins, carry=None):
    sem = ("arbitrary",) * len(grid)
    if carry is None:
        outs = pl.pallas_call(body, name=name, grid=grid, in_specs=in_specs, out_specs=out_specs, out_shape=out_shape,
                              scratch_shapes=scratch_shapes, compiler_params=_params(*sem))(*ins)
        return outs, None
    n_in, n_out, n_scr = len(ins), len(out_shape), len(scratch_shapes)
    ci, co = len(carry.inputs), len(carry.out_shape)
    total = 1
    for g in grid:
        total *= g

    def wrapped(*refs):
        own_in, ex_in = refs[:n_in], refs[n_in:n_in + ci]
        o0 = n_in + ci
        own_out, ex_out = refs[o0:o0 + n_out], refs[o0 + n_out:o0 + n_out + co]
        s0 = o0 + n_out + co
        own_scr, ex_scr = refs[s0:s0 + n_scr], refs[s0 + n_scr:]
        step = pl.program_id(0)
        for d in range(1, len(grid)):
            step = step * grid[d] + pl.program_id(d)
        parts = (ex_in, ex_out, ex_scr)
        pl.when(step == 0)(lambda: carry.begin(*parts))
        body(*own_in, *own_out, *own_scr)
        if carry.middle is not None:
            pl.when(step == (3 * total) // 4)(lambda: carry.middle(*parts))
        pl.when(step == total - 1)(lambda: carry.end(*parts))

    outs = pl.pallas_call(
        wrapped, name=name, grid=grid, in_specs=list(in_specs) + [HBM_SPEC] * ci,
        out_specs=list(out_specs) + [HBM_SPEC] * co, out_shape=list(out_shape) + list(carry.out_shape),
        scratch_shapes=list(scratch_shapes) + list(carry.scratch), compiler_params=_params(*sem),
    )(*ins, *carry.inputs)
    return outs[:n_out], outs[n_out:]


def _ffn_fwd(name, xin, affine, wall, offs, tm, loss=None, carry=None):
    T = xin.shape[0]
    nt = T // tm

    def body(*refs):
        it = iter(refs)
        x_ref = next(it)
        if affine is not None:
            g_ref, b_ref = next(it), next(it)
        wall_ref = next(it)
        if loss is not None:
            go_ref, bo_ref, tgt_ref = next(it), next(it), next(it)
        xh_ref, r_ref, a_ref, b2_ref, yb_ref = (next(it) for _ in range(5))
        if loss is not None:
            loss_ref = next(it)
        w1, w3, w2, sem = (next(it) for _ in range(4))

        @pl.when(pl.program_id(0) == 0)
        def _():
            _load_rows(wall_ref, [(w1, offs[0], RF), (w3, offs[1], RF), (w2, offs[2], RF)], sem)
            if loss is not None:
                loss_ref[...] = jnp.zeros_like(loss_ref)

        parts = [slice(p, min(p + PART_ROWS, tm)) for p in range(0, tm, PART_ROWS)]
        ys, ybs = [], []
        for rs in parts:
            y = x_ref[rs, :]
            if affine is not None:
                y = y * g_ref[...] + b_ref[...]
            ys.append(y)
            ybs.append(y.astype(BF16))
            yb_ref[rs, :] = ybs[-1]
        ab = [(_dot_nt(yb, w1[...]).astype(BF16), _dot_nt(yb, w3[...]).astype(BF16)) for yb in ybs]
        us = []
        for rs, (a, b) in zip(parts, ab):
            a_ref[rs, :] = a
            b2_ref[rs, :] = b
            af, bf = a.astype(F32), b.astype(F32)
            us.append((af * _sigmoid(af) * bf).astype(BF16))
        fs = [_dot(u, w2[...]) for u in us]
        for rs, y, f in zip(parts, ys, fs):
            xh, r = _ln_fwd(ALPHA * y + 0.5 * f)
            xh_ref[rs, :] = xh
            r_ref[rs, :] = jnp.broadcast_to(r, (xh.shape[0], 128))
            if loss is not None:
                e = xh * go_ref[...] + bo_ref[...] - tgt_ref[rs, :]
                loss_ref[...] += jnp.sum(e * e) * (0.5 / D)

    row = lambda w: pl.BlockSpec((tm, w), lambda i: (i, 0))
    vec = pl.BlockSpec((1, D), lambda i: (0, 0))
    ins, in_specs = [xin], [row(D)]
    if affine is not None:
        ins += list(affine)
        in_specs += [vec, vec]
    ins.append(wall)
    in_specs.append(HBM_SPEC)
    if loss is not None:
        ins += list(loss)
        in_specs += [vec, vec, row(D)]
    out_shape = [jax.ShapeDtypeStruct((T, D), F32), jax.ShapeDtypeStruct((T, 128), F32),
                 jax.ShapeDtypeStruct((T, DFF), BF16), jax.ShapeDtypeStruct((T, DFF), BF16),
                 jax.ShapeDtypeStruct((T, D), BF16)]
    out_specs = [row(D), row(128), row(DFF), row(DFF), row(D)]
    if loss is not None:
        out_shape.append(jax.ShapeDtypeStruct((8, 128), F32))
        out_specs.append(pl.BlockSpec((8, 128), lambda i: (0, 0)))
    return _call(body, name=name, grid=(nt,), in_specs=in_specs, out_specs=out_specs, out_shape=out_shape,
                 scratch_shapes=[pltpu.VMEM((DFF, D), BF16)] * 3 + [pltpu.SemaphoreType.DMA((3 * NDEV,))],
                 ins=ins, carry=carry)


def _ffn_bwd(name, dy_src, xh, r, g, a, b, wall, offs, tm, carry=None):
    T = xh.shape[0]
    nt = T // tm
    from_loss = dy_src[0] == "loss"

    def body(*refs):
        it = iter(refs)
        if from_loss:
            bo_ref, tgt_ref = next(it), next(it)
        else:
            dy_ref = next(it)
        xh_ref, r_ref, g_ref, a_ref, b_ref, wall_ref = (next(it) for _ in range(6))
        dyin_ref, dab_ref, u_ref, df_ref, dg_ref, db_ref = (next(it) for _ in range(6))
        w1, w3, w2, sem = (next(it) for _ in range(4))

        @pl.when(pl.program_id(0) == 0)
        def _():
            _load_rows(wall_ref, [(w1, offs[0], RF), (w3, offs[1], RF), (w2, offs[2], RF)], sem)
            dg_ref[...] = jnp.zeros_like(dg_ref)
            db_ref[...] = jnp.zeros_like(db_ref)

        parts = [slice(p, min(p + PART_ROWS, tm)) for p in range(0, tm, PART_ROWS)]
        gv = g_ref[...]
        dz, df = [], []
        for rs in parts:
            xhv = xh_ref[rs, :]
            if from_loss:
                dy = (xhv * gv + bo_ref[...] - tgt_ref[rs, :]) * (1.0 / D)
            else:
                dy = dy_ref[rs, :]
            dzp, dgp, dbp = _ln_bwd(dy, xhv, r_ref[rs, :1], gv)
            dg_ref[...] += dgp
            db_ref[...] += dbp
            dz.append(dzp)
            df.append((0.5 * dzp).astype(BF16))
            df_ref[rs, :] = df[-1]
        du = [_dot_nt(d, w2[...]) for d in df]
        da, dbb = [], []
        for rs, dup in zip(parts, du):
            af, bf = a_ref[rs, :].astype(F32), b_ref[rs, :].astype(F32)
            s = _sigmoid(af)
            sl = af * s
            u_ref[rs, :] = (sl * bf).astype(BF16)
            da.append((dup * bf * (s * (1.0 + af * (1.0 - s)))).astype(BF16))
            dbb.append((dup * sl).astype(BF16))
            dab_ref[rs, :DFF] = da[-1]
            dab_ref[rs, DFF:] = dbb[-1]
        for rs, dzp, dap, dbp in zip(parts, dz, da, dbb):
            dyin_ref[rs, :] = ALPHA * dzp + _dot(dap, w1[...]) + _dot(dbp, w3[...])

    row = lambda w: pl.BlockSpec((tm, w), lambda i: (i, 0))
    vec = pl.BlockSpec((1, D), lambda i: (0, 0))
    if from_loss:
        ins, in_specs = [dy_src[1], dy_src[2]], [vec, row(D)]
    else:
        ins, in_specs = [dy_src[1]], [row(D)]
    ins += [xh, r, g, a, b, wall]
    in_specs += [row(D), row(128), vec, row(DFF), row(DFF), HBM_SPEC]
    return _call(
        body, name=name, grid=(nt,), in_specs=in_specs,
        out_specs=[row(D), row(2 * DFF), row(DFF), row(D), vec, vec],
        out_shape=[jax.ShapeDtypeStruct((T, D), F32), jax.ShapeDtypeStruct((T, 2 * DFF), BF16),
                   jax.ShapeDtypeStruct((T, DFF), BF16), jax.ShapeDtypeStruct((T, D), BF16),
                   jax.ShapeDtypeStruct((1, D), F32), jax.ShapeDtypeStruct((1, D), F32)],
        scratch_shapes=[pltpu.VMEM((DFF, D), BF16)] * 3 + [pltpu.SemaphoreType.DMA((3 * NDEV,))],
        ins=ins, carry=carry)


def _wgrad(name, a, b, bn, tk, col0=0, ncols=None, carry=None):
    T = a.shape[0]
    N = a.shape[1] if ncols is None else ncols
    nk = T // tk
    c0 = col0 // bn

    def body(a_ref, b_ref, o_ref, acc):
        k = pl.program_id(1)

        @pl.when(k == 0)
        def _():
            acc[...] = jnp.zeros_like(acc)

        acc[...] += _dot_tn(a_ref[...], b_ref[...])

        @pl.when(k == nk - 1)
        def _():
            o_ref[...] = acc[...].astype(BF16)

    (out,), ex = _call(
        body, name=name, grid=(N // bn, nk),
        in_specs=[pl.BlockSpec((tk, bn), lambda n, k: (k, n + c0)), pl.BlockSpec((tk, D), lambda n, k: (k, 0))],
        out_specs=[pl.BlockSpec((bn, D), lambda n, k: (n, 0))],
        out_shape=[jax.ShapeDtypeStruct((N, D), BF16)],
        scratch_shapes=[pltpu.VMEM((bn, D), F32)], ins=[a, b], carry=carry)
    return out, ex


def _rope_tables(T):
    pos = jnp.arange(T, dtype=F32)
    inv_freq = ROPE_THETA ** (-jnp.arange(0, ROPE_DIM, 2, dtype=F32) / ROPE_DIM)
    half = ROPE_DIM // 2
    ch = jnp.arange(128) % HEAD_DIM
    ang = pos[:, None] * inv_freq[ch % half][None, :]
    cos, sin = jnp.cos(ang), jnp.sin(ang)
    first, second = (ch < half)[None, :], ((ch >= half) & (ch < ROPE_DIM))[None, :]
    return (jnp.where(first | second, cos, 1.0), jnp.where(first, -sin, 0.0), jnp.where(second, sin, 0.0))


def _rope(t, c, s1, s2):
    n = t.shape[1] // 128
    ct, s1t, s2t = (jnp.tile(v, (1, n)) for v in (c, s1, s2))
    w = t.shape[1]
    return t * ct + pltpu.roll(t, w - 8, 1) * s1t + pltpu.roll(t, 8, 1) * s2t


def _rope_t(dr, c, s1, s2):
    n = dr.shape[1] // 128
    ct, s1t, s2t = (jnp.tile(v, (1, n)) for v in (c, s1, s2))
    w = dr.shape[1]
    return dr * ct + pltpu.roll(dr * s1t, 8, 1) + pltpu.roll(dr * s2t, w - 8, 1)


_Q, _K, _V = (0, 1024), (1024, 256), (1280, 256)
_HG = (1536, HG_W)
_GATES = (5632, GATE_W)


def _inproj_fwd(name, xh, g, b, wall, b_in, ropes, tm, carry=None):
    T = xh.shape[0]

    def body(xh_ref, g_ref, b_ref, wall_ref, bin_ref, c_ref, s1_ref, s2_ref,
             qkv_ref, kt_ref, vt_ref, hg_ref, gate_ref, yb_ref, w, sem):
        @pl.when(pl.program_id(0) == 0)
        def _():
            _load_rows(wall_ref, [(w, 0, RIN)], sem)

        yb = (xh_ref[...] * g_ref[...] + b_ref[...]).astype(BF16)
        yb_ref[...] = yb
        c, s1, s2 = c_ref[...], s1_ref[...], s2_ref[...]

        def piece(start, width):
            return _dot_nt(yb, w[start:start + width, :]) + bin_ref[:, start:start + width]

        k = _rope(piece(*_K), c, s1, s2)
        v = piece(*_V)
        qkv_ref[:, 0:1024] = _rope(piece(*_Q), c, s1, s2).astype(BF16)
        qkv_ref[:, 1024:1280] = k.astype(BF16)
        qkv_ref[:, 1280:1536] = v.astype(BF16)
        kt_ref[...] = k.T.astype(BF16)
        vt_ref[...] = v.T.astype(BF16)
        for j in range(4):
            hg_ref[:, 1024 * j:1024 * (j + 1)] = piece(_HG[0] + 1024 * j, 1024)
        for j in range(2):
            gate_ref[:, 1024 * j:1024 * (j + 1)] = piece(_GATES[0] + 1024 * j, 1024)

    row = lambda wd: pl.BlockSpec((tm, wd), lambda i: (i, 0))
    vec = lambda wd: pl.BlockSpec((1, wd), lambda i: (0, 0))
    colt = pl.BlockSpec((256, tm), lambda i: (0, i))
    return _call(
        body, name=name, grid=(T // tm,),
        in_specs=[row(D), vec(D), vec(D), HBM_SPEC, vec(DIN), row(128), row(128), row(128)],
        out_specs=[row(QKV_W), colt, colt, row(HG_W), row(GATE_W), row(D)],
        out_shape=[jax.ShapeDtypeStruct((T, QKV_W), BF16), jax.ShapeDtypeStruct((256, T), BF16),
                   jax.ShapeDtypeStruct((256, T), BF16), jax.ShapeDtypeStruct((T, HG_W), F32),
                   jax.ShapeDtypeStruct((T, GATE_W), F32), jax.ShapeDtypeStruct((T, D), BF16)],
        scratch_shapes=[pltpu.VMEM((DIN, D), BF16), pltpu.SemaphoreType.DMA((NDEV,))],
        ins=[xh, g, b, wall, b_in, *ropes], carry=carry)


def _inproj_bwd(name, dq, dk, dv, dhg, dgates, dz2, wall, ropes, tm):
    T = dq.shape[0]

    def body(dq_ref, dk_ref, dv_ref, d0, d1, d2, d3, dga_ref, dgh_ref, dz_ref, wall_ref, c_ref, s1_ref, s2_ref,
             dy_ref, dproj_ref, dbin_ref, w, sem):
        @pl.when(pl.program_id(0) == 0)
        def _():
            _load_rows(wall_ref, [(w, 0, RIN)], sem)
            dbin_ref[...] = jnp.zeros_like(dbin_ref)

        c, s1, s2 = c_ref[...], s1_ref[...], s2_ref[...]
        acc = ALPHA * dz_ref[...]
        pieces = [(_Q[0], _rope_t(dq_ref[...], c, s1, s2)), (_K[0], _rope_t(dk_ref[...], c, s1, s2)),
                  (_V[0], dv_ref[...])]
        pieces += [(_HG[0] + 1024 * j, r[...]) for j, r in enumerate((d0, d1, d2, d3))]
        pieces += [(_GATES[0], dga_ref[...]), (_GATES[0] + 1024, dgh_ref[...])]
        for start, val in pieces:
            width = val.shape[1]
            dbin_ref[:, start:start + width] += jnp.sum(val.astype(F32), axis=0, keepdims=True)
            vb = val.astype(BF16)
            dproj_ref[:, start:start + width] = vb
            acc = acc + _dot(vb, w[start:start + width, :])
        dy_ref[...] = acc

    row = lambda wd: pl.BlockSpec((tm, wd), lambda i: (i, 0))
    return pl.pallas_call(
        body, name=name, grid=(T // tm,),
        in_specs=[row(D), row(256), row(256)] + [row(D)] * 4 + [row(D), row(D), row(D), HBM_SPEC] + [row(128)] * 3,
        out_specs=[row(D), row(DIN), pl.BlockSpec((1, DIN), lambda i: (0, 0))],
        out_shape=[jax.ShapeDtypeStruct((T, D), F32), jax.ShapeDtypeStruct((T, DIN), BF16),
                   jax.ShapeDtypeStruct((1, DIN), F32)],
        scratch_shapes=[pltpu.VMEM((DIN, D), BF16), pltpu.SemaphoreType.DMA((NDEV,))],
        compiler_params=_params("arbitrary"),
    )(dq, dk, dv, *dhg, *dgates, dz2, wall, *ropes)


def _halves(t):
    lane = lax.broadcasted_iota(jnp.int32, t.shape, 1)
    low = lane < HEAD_DIM
    sw = pltpu.roll(t, HEAD_DIM, 1)
    zero = jnp.zeros_like(t)
    h0 = (jnp.where(low, t, zero), jnp.where(low, zero, sw))
    h1 = (jnp.where(low, sw, zero), jnp.where(low, zero, t))
    return h0, h1


def _lane_stack(p_ref, c_ref, gp):
    sl = slice(128 * gp, 128 * (gp + 1))
    hp, hc = _halves(p_ref[:, sl].astype(F32)), _halves(c_ref[:, sl].astype(F32))
    return [jnp.concatenate([hp[gg][0], hc[gg][0], hp[gg][1], hc[gg][1]], axis=0).astype(BF16) for gg in range(2)]


def _row_stack(tp_ref, tc_ref, g):
    band = jnp.concatenate([tp_ref[64 * g:64 * (g + 1), :], tc_ref[64 * g:64 * (g + 1), :]], axis=1)
    z = jnp.zeros_like(band)
    return [jnp.concatenate([band, z], axis=0), jnp.concatenate([z, band], axis=0)]


def _fold(prevm, t4, hh):
    return jnp.where(prevm, t4[256 * hh:256 * hh + 128, :], t4[256 * hh + 128:256 * hh + 256, :])


def _unfold(prevm, t):
    return jnp.concatenate([jnp.where(prevm, t, 0.0), jnp.where(prevm, 0.0, t)], axis=0).astype(BF16)


def _attn_softmax(s, kill, sink):
    s = jnp.where(kill, NEG_INF, s)
    m = jnp.maximum(jnp.max(s, axis=0, keepdims=True), sink)
    p = jnp.exp(s - m)
    es = jnp.exp(sink - m)
    denom = jnp.sum(p, axis=0, keepdims=True) + es
    return p / denom, es / denom


def _attn_masks(first):
    row = lax.broadcasted_iota(jnp.int32, (ATTN_BLOCK, 2 * ATTN_BLOCK), 0)
    lane = lax.broadcasted_iota(jnp.int32, (ATTN_BLOCK, 2 * ATTN_BLOCK), 1)
    prevm = row > lane % ATTN_BLOCK
    return prevm, jnp.logical_and(first, prevm)


def _pair_rows(ref, g):
    return jnp.concatenate([ref[:, 256 * g:256 * g + 128], ref[:, 256 * g + 128:256 * (g + 1)]], axis=0)


def _pair_sinks(sink_ref, g, hh):
    h0, h1 = 4 * g + hh, 4 * g + 2 + hh
    return jnp.concatenate([jnp.broadcast_to(sink_ref[:, h0:h0 + 1], (1, ATTN_BLOCK)),
                            jnp.broadcast_to(sink_ref[:, h1:h1 + 1], (1, ATTN_BLOCK))], axis=1)


def _attn_fwd(name, qkv, vt, sinks):
    T = qkv.shape[0]
    nb = T // ATTN_BLOCK
    scale = HEAD_DIM ** -0.5

    def body(q_ref, kp_ref, kc_ref, vtp_ref, vtc_ref, sink_ref, o_ref):
        prevm, kill = _attn_masks(pl.program_id(0) == 0)
        kst = _lane_stack(kp_ref, kc_ref, 0) + _lane_stack(kp_ref, kc_ref, 1)
        vts = [_row_stack(vtp_ref, vtc_ref, g) for g in range(N_KV_HEADS)]
        s8 = [_dot_nt(kst[g], _pair_rows(q_ref, g)) * scale for g in range(N_KV_HEADS)]
        pu = [[_unfold(prevm, _attn_softmax(_fold(prevm, s8[g], hh), kill, _pair_sinks(sink_ref, g, hh))[0])
               for hh in range(2)] for g in range(N_KV_HEADS)]
        for g in range(N_KV_HEADS):
            ot = _dot(vts[g][0], pu[g][0]) + _dot(vts[g][1], pu[g][1])
            o_ref[:, 256 * g:256 * g + 128] = ot[:, :ATTN_BLOCK].T.astype(BF16)
            o_ref[:, 256 * g + 128:256 * (g + 1)] = ot[:, ATTN_BLOCK:].T.astype(BF16)

    prev = lambda i: jnp.maximum(i - 1, 0)
    return pl.pallas_call(
        body, name=name, grid=(nb,),
        in_specs=[pl.BlockSpec((ATTN_BLOCK, D), lambda i: (i, 0)),
                  pl.BlockSpec((ATTN_BLOCK, 256), lambda i: (prev(i), 4)),
                  pl.BlockSpec((ATTN_BLOCK, 256), lambda i: (i, 4)),
                  pl.BlockSpec((256, ATTN_BLOCK), lambda i: (0, prev(i))),
                  pl.BlockSpec((256, ATTN_BLOCK), lambda i: (0, i)),
                  pl.BlockSpec((1, N_Q_HEADS), lambda i: (0, 0))],
        out_specs=pl.BlockSpec((ATTN_BLOCK, D), lambda i: (i, 0)),
        out_shape=jax.ShapeDtypeStruct((T, D), BF16),
        compiler_params=_params("arbitrary"),
    )(qkv, qkv, qkv, vt, vt, sinks)


def _attn_bwd(name, qkv, kt, sinks, dya, carry=None):
    T = qkv.shape[0]
    nb = T // ATTN_BLOCK
    scale = HEAD_DIM ** -0.5

    def body(q_ref, kp_ref, kc_ref, vp_ref, vc_ref, ktp_ref, ktc_ref, sink_ref, do_ref,
             dq_ref, dk_ref, dv_ref, ds_ref):
        i = pl.program_id(0)

        @pl.when(i == 0)
        def _():
            ds_ref[...] = jnp.zeros_like(ds_ref)

        prevm, kill = _attn_masks(i == 0)
        kst = _lane_stack(kp_ref, kc_ref, 0) + _lane_stack(kp_ref, kc_ref, 1)
        vst = _lane_stack(vp_ref, vc_ref, 0) + _lane_stack(vp_ref, vc_ref, 1)
        kts = [_row_stack(ktp_ref, ktc_ref, g) for g in range(N_KV_HEADS)]
        lane = lax.broadcasted_iota(jnp.int32, (2 * ATTN_BLOCK, 128), 1)
        low = lane < HEAD_DIM
        slane = lax.broadcasted_iota(jnp.int32, (1, 128), 1)
        dkz = [jnp.zeros((2 * ATTN_BLOCK, 128), F32) for _ in range(N_KV_HEADS)]
        dvz = [jnp.zeros((2 * ATTN_BLOCK, 128), F32) for _ in range(N_KV_HEADS)]
        dsink = jnp.zeros((1, 128), F32)
        groups = range(N_KV_HEADS)
        qcat = [_pair_rows(q_ref, g) for g in groups]
        docat = [_pair_rows(do_ref, g) for g in groups]
        s8 = [_dot_nt(kst[g], qcat[g]) * scale for g in groups]
        dp8 = [_dot_nt(vst[g], docat[g]) for g in groups]
        ds_u = [[None, None] for _ in groups]
        p_u = [[None, None] for _ in groups]
        for g in groups:
            for hh in range(2):
                pn, ps = _attn_softmax(_fold(prevm, s8[g], hh), kill, _pair_sinks(sink_ref, g, hh))
                dp = _fold(prevm, dp8[g], hh)
                delta = jnp.sum(pn * dp, axis=0, keepdims=True)
                sd = ps * delta
                dsink = dsink + jnp.where(slane == 4 * g + hh, -jnp.sum(sd[:, :ATTN_BLOCK]), 0.0) \
                    + jnp.where(slane == 4 * g + 2 + hh, -jnp.sum(sd[:, ATTN_BLOCK:]), 0.0)
                ds_u[g][hh] = _unfold(prevm, pn * (dp - delta) * scale)
                p_u[g][hh] = _unfold(prevm, pn)
        for g in groups:
            dqt = _dot(kts[g][0], ds_u[g][0]) + _dot(kts[g][1], ds_u[g][1])
            dq_ref[:, 256 * g:256 * g + 128] = dqt[:, :ATTN_BLOCK].T
            dq_ref[:, 256 * g + 128:256 * (g + 1)] = dqt[:, ATTN_BLOCK:].T
            for hh in range(2):
                own = low if hh == 0 else jnp.logical_not(low)
                dk_h = jnp.where(own, _dot(ds_u[g][hh], qcat[g]), 0.0)
                dv_h = jnp.where(own, _dot(p_u[g][hh], docat[g]), 0.0)
                if hh != g % 2:
                    dk_h = pltpu.roll(dk_h, HEAD_DIM, 1)
                    dv_h = pltpu.roll(dv_h, HEAD_DIM, 1)
                dkz[g] = dkz[g] + dk_h
                dvz[g] = dvz[g] + dv_h
        ds_ref[...] += dsink
        cur = pl.ds(pl.multiple_of(i * ATTN_BLOCK, ATTN_BLOCK), ATTN_BLOCK)
        prv = pl.ds(pl.multiple_of(jnp.maximum(i - 1, 0) * ATTN_BLOCK, ATTN_BLOCK), ATTN_BLOCK)
        for gp in range(N_KV_HEADS // 2):
            cols = slice(128 * gp, 128 * (gp + 1))
            dkb = dkz[2 * gp] + dkz[2 * gp + 1]
            dvb = dvz[2 * gp] + dvz[2 * gp + 1]
            dk_ref[cur, cols] = dkb[ATTN_BLOCK:, :]
            dv_ref[cur, cols] = dvb[ATTN_BLOCK:, :]

            @pl.when(i > 0)
            def _():
                dk_ref[prv, cols] += dkb[:ATTN_BLOCK, :]
                dv_ref[prv, cols] += dvb[:ATTN_BLOCK, :]

    prev = lambda i: jnp.maximum(i - 1, 0)
    whole = lambda w: pl.BlockSpec((T, w), lambda i: (0, 0))
    return _call(
        body, name=name, grid=(nb,), scratch_shapes=[], ins=[qkv, qkv, qkv, qkv, qkv, kt, kt, sinks, dya], carry=carry,
        in_specs=[pl.BlockSpec((ATTN_BLOCK, D), lambda i: (i, 0)),
                  pl.BlockSpec((ATTN_BLOCK, 256), lambda i: (prev(i), 4)),
                  pl.BlockSpec((ATTN_BLOCK, 256), lambda i: (i, 4)),
                  pl.BlockSpec((ATTN_BLOCK, 256), lambda i: (prev(i), 5)),
                  pl.BlockSpec((ATTN_BLOCK, 256), lambda i: (i, 5)),
                  pl.BlockSpec((256, ATTN_BLOCK), lambda i: (0, prev(i))),
                  pl.BlockSpec((256, ATTN_BLOCK), lambda i: (0, i)),
                  pl.BlockSpec((1, N_Q_HEADS), lambda i: (0, 0)),
                  pl.BlockSpec((ATTN_BLOCK, D), lambda i: (i, 0))],
        out_specs=[pl.BlockSpec((ATTN_BLOCK, D), lambda i: (i, 0)), whole(256), whole(256),
                   pl.BlockSpec((1, 128), lambda i: (0, 0))],
        out_shape=[jax.ShapeDtypeStruct((T, D), F32), jax.ShapeDtypeStruct((T, 256), F32),
                   jax.ShapeDtypeStruct((T, 256), F32), jax.ShapeDtypeStruct((1, 128), F32)])


HG_FWD = (256, 8, 512)
HG_BWD = (128, 8, 512)


def _chunk_sum(tri, x):
    w = x.shape[1]
    h1 = x.astype(BF16)
    h2 = (x - h1.astype(F32)).astype(BF16)
    r = _dot(tri, jnp.concatenate([h1, h2], axis=1))
    return r[:, :w] + r[:, w:]


def _chunk_tri(n):
    ri = lax.broadcasted_iota(jnp.int32, (n, n), 0)
    ci = lax.broadcasted_iota(jnp.int32, (n, n), 1)
    return jnp.where((ri >= ci) & (ri // HGRN_CHUNK == ci // HGRN_CHUNK), 1.0, 0.0).astype(BF16)


def _chunks(t):
    return [t[HGRN_CHUNK * c:HGRN_CHUNK * (c + 1), :] for c in range(t.shape[0] // HGRN_CHUNK)]


def _lower_bound(lbl_ref):
    l0, l1 = lbl_ref[0:1, :], lbl_ref[1:2, :]
    m = jnp.maximum(l0, l1)
    e0, e1 = jnp.exp(l0 - m), jnp.exp(l1 - m)
    return e0 / (e0 + e1)


def _heads(t):
    return [t[:, 128 * h:128 * (h + 1)] for h in range(t.shape[1] // 128)]


def _per_head(fn, *wide):
    return jnp.concatenate([fn(*parts) for parts in zip(*[_heads(t) for t in wide])], axis=1)


def _hgrn_sub(fl, qh, vv, lb, tril_b):
    w = fl.shape[1]
    sg = _sigmoid(fl)
    f = lb + (1.0 - lb) * sg
    k = 1.0 - f
    gc = _chunk_sum(tril_b, jnp.log(f))
    gl_c = [t[HGRN_CHUNK - 1:HGRN_CHUNK, :] for t in _chunks(gc)]
    gl = jnp.concatenate([jnp.broadcast_to(g, (HGRN_CHUNK, w)) for g in gl_c], axis=0)
    sq = _sigmoid(qh)
    eg = jnp.exp(gc)
    eng = jnp.exp(-gc)
    elg = jnp.exp(gl - gc)
    qd = qh * sq * eg
    ki = k * eng
    ke = k * elg
    qd_b, ki_b, ke_b, v_b = (t.astype(BF16) for t in (qd, ki, ke, vv))
    am_b = [jnp.where(tril_b > 0, _dot_nt(qq, kk), 0.0).astype(BF16) for qq, kk in zip(_heads(qd_b), _heads(ki_b))]
    return dict(sg=sg, f=f, sq=sq, eg=eg, eng=eng, elg=elg, qd=qd, ki=ki, ke=ke, egl=[jnp.exp(g) for g in gl_c],
                am_b=am_b, qd_b=qd_b, ki_b=ki_b, ke_b=ke_b, v_b=v_b)


def _hgrn_out(q, st_b):
    outs = []
    for h, (qd_h, v_h) in enumerate(zip(_heads(q["qd_b"]), _heads(q["v_b"]))):
        inter = [_dot_nt(qc, s) for qc, s in zip(_chunks(qd_h), st_b[h])]
        outs.append(_dot(q["am_b"][h], v_h) + jnp.concatenate(inter, axis=0))
    return jnp.concatenate(outs, axis=1)


def _head_rms(o):
    return _per_head(lambda t: jnp.broadcast_to(
        lax.rsqrt(jnp.mean(t * t, axis=-1, keepdims=True) + RMS_EPS), t.shape), o)


def _hgrn_fwd(name, hg, lb_logits, norm_g, cfg):
    T = hg.shape[0]
    sub_rows, hps, th = min(cfg[0], T), cfg[1], min(cfg[2], T)
    cps = sub_rows // HGRN_CHUNK
    nc = th // HGRN_CHUNK
    tri = _chunk_tri(sub_rows)

    def body(fl_ref, qh_ref, v_ref, og_ref, lbl_ref, ng_ref, tri_ref, y_ref, ss_ref, st):
        @pl.when(pl.program_id(1) == 0)
        def _():
            st[...] = jnp.zeros_like(st)

        lbs = _lower_bound(lbl_ref)
        ng = jnp.tile(ng_ref[...], (1, hps))

        def sub(si, carry):
            rows = pl.ds(pl.multiple_of(si * sub_rows, sub_rows), sub_rows)
            q = _hgrn_sub(fl_ref[rows, :], qh_ref[rows, :], v_ref[rows, :], lbs, tri_ref[...])
            v_hc = [_chunks(t) for t in _heads(q["v_b"])]
            k_hc = [_chunks(t) for t in _heads(q["ke_b"])]
            s = [st[h] for h in range(hps)]
            st_b = [[] for _ in range(hps)]
            for c in range(cps):
                egl = _heads(q["egl"][c])
                for h in range(hps):
                    ss_ref[h, si * cps + c] = s[h]
                    st_b[h].append(s[h].astype(BF16))
                    s[h] = s[h] * egl[h] + _dot_tn(v_hc[h][c], k_hc[h][c])
            for h in range(hps):
                st[h] = s[h]
            o = _hgrn_out(q, st_b)
            og = og_ref[rows, :]
            y_ref[rows, :] = (o * _head_rms(o) * ng * (og * _sigmoid(og))).astype(BF16)
            return carry

        lax.fori_loop(0, th // sub_rows, sub, 0)

    wd = 128 * hps
    col = lambda j: pl.BlockSpec((th, wd), lambda h, t: (t, (8 // hps) * j + h))
    return pl.pallas_call(
        body, name=name, grid=(HGRN_HEADS // hps, T // th),
        in_specs=[col(0), col(1), col(2), col(3), pl.BlockSpec((2, wd), lambda h, t: (0, h)),
                  pl.BlockSpec((1, 128), lambda h, t: (0, 0)), pl.BlockSpec((sub_rows, sub_rows), lambda h, t: (0, 0))],
        out_specs=[pl.BlockSpec((th, wd), lambda h, t: (t, h)),
                   pl.BlockSpec((hps, nc, 128, 128), lambda h, t: (h, t, 0, 0))],
        out_shape=[jax.ShapeDtypeStruct((T, D), BF16),
                   jax.ShapeDtypeStruct((HGRN_HEADS, T // HGRN_CHUNK, 128, 128), F32)],
        scratch_shapes=[pltpu.VMEM((hps, 128, 128), F32)],
        compiler_params=_params("parallel", "arbitrary"),
    )(hg, hg, hg, hg, lb_logits, norm_g, tri)


def _hgrn_bwd(name, hg, lb_logits, norm_g, sstart, dyh, cfg, carry=None):
    T = hg.shape[0]
    sub_rows, hps, th = min(cfg[0], T), cfg[1], min(cfg[2], T)
    cps = sub_rows // HGRN_CHUNK
    nc = th // HGRN_CHUNK
    nt = T // th
    wd = 128 * hps
    tri = _chunk_tri(sub_rows)
    tri_t = tri.T

    def body(fl_ref, qh_ref, v_ref, og_ref, lbl_ref, ng_ref, tri_ref, trit_ref, ss_ref, dy_ref,
             dfl_ref, dqh_ref, dv_ref, dog_ref, dlb_ref, dng_ref, dst):
        @pl.when(pl.program_id(1) == 0)
        def _():
            dst[...] = jnp.zeros_like(dst)
            dlb_ref[...] = jnp.zeros_like(dlb_ref)
            dng_ref[...] = jnp.zeros_like(dng_ref)

        lb = _lower_bound(lbl_ref)
        ng = jnp.tile(ng_ref[...], (1, hps))
        last = lax.broadcasted_iota(jnp.int32, (HGRN_CHUNK, wd), 0) == HGRN_CHUNK - 1
        nsub = th // sub_rows
        cat0 = lambda parts: jnp.concatenate(parts, axis=0)

        def sub(step, carry):
            si = nsub - 1 - step
            rows = pl.ds(pl.multiple_of(si * sub_rows, sub_rows), sub_rows)
            qh, og, dy = qh_ref[rows, :], og_ref[rows, :], dy_ref[rows, :]
            tril_b = tri_ref[...]
            q = _hgrn_sub(fl_ref[rows, :], qh, v_ref[rows, :], lb, tril_b)
            s_in = [[ss_ref[h, si * cps + c] for c in range(cps)] for h in range(hps)]
            st_b = [[s.astype(BF16) for s in row] for row in s_in]
            o = _hgrn_out(q, st_b)
            rr = _head_rms(o)
            oh = o * rr
            sog = _sigmoid(og)
            dog_ref[rows, :] = (dy * oh * ng * (sog * (1.0 + og * (1.0 - sog)))).astype(BF16)
            don = dy * (og * sog)
            dng_w = jnp.sum(don * oh, axis=0, keepdims=True)
            dd = don * ng
            mean_h = _per_head(lambda t: jnp.broadcast_to(jnp.mean(t, axis=-1, keepdims=True), t.shape), dd * oh)
            do_b = (rr * (dd - oh * mean_h)).astype(BF16)
            do_h, qd_h, ki_h, ke_h, v_h = (_heads(t) for t in (do_b, q["qd_b"], q["ki_b"], q["ke_b"], q["v_b"]))
            da_b = [jnp.where(tril_b > 0, _dot_nt(do_h[h], v_h[h]), 0.0).astype(BF16) for h in range(hps)]
            do_c, qd_c, ke_c, v_c = ([_chunks(t) for t in hs] for hs in (do_h, qd_h, ke_h, v_h))
            dsp = [[None] * cps for _ in range(hps)]
            dgl_dec = [[None] * cps for _ in range(hps)]
            d = [dst[h] for h in range(hps)]
            for c in reversed(range(cps)):
                egl = _heads(q["egl"][c])
                for h in range(hps):
                    dsp[h][c] = d[h]
                    dgl_dec[h][c] = jnp.sum(d[h] * s_in[h][c], axis=0, keepdims=True) * egl[h]
                    d[h] = d[h] * egl[h] + _dot_tn(do_c[h][c], qd_c[h][c])
            for h in range(hps):
                dst[h] = d[h]
                dng_ref[h] += dng_w[:, 128 * h:128 * (h + 1)]
            dsp_b = [[t.astype(BF16) for t in row] for row in dsp]
            dv_ref[rows, :] = jnp.concatenate(
                [_dot_tn(q["am_b"][h], do_h[h]) + cat0([_dot_nt(ke_c[h][c], dsp_b[h][c]) for c in range(cps)])
                 for h in range(hps)], axis=1).astype(BF16)
            dqd = jnp.concatenate(
                [_dot(da_b[h], ki_h[h]) + cat0([_dot(do_c[h][c], st_b[h][c]) for c in range(cps)])
                 for h in range(hps)], axis=1)
            dki = jnp.concatenate([_dot_tn(da_b[h], qd_h[h]) for h in range(hps)], axis=1)
            dke = jnp.concatenate([cat0([_dot(v_c[h][c], dsp_b[h][c]) for c in range(cps)]) for h in range(hps)], axis=1)
            dkk = dke * q["ke"]
            dgl = [jnp.sum(t, axis=0, keepdims=True) + jnp.concatenate([dgl_dec[h][c] for h in range(hps)], axis=1)
                   for c, t in enumerate(_chunks(dkk))]
            dgc = dqd * q["qd"] - dki * q["ki"] - dkk + cat0([jnp.where(last, g, 0.0) for g in dgl])
            dlf = _chunk_sum(trit_ref[...], dgc)
            df = dlf / q["f"] - (dki * q["eng"] + dke * q["elg"])
            sg = q["sg"]
            dlb_ref[...] += jnp.sum(df * (1.0 - sg), axis=0, keepdims=True)
            dfl_ref[rows, :] = (df * (1.0 - lb) * sg * (1.0 - sg)).astype(BF16)
            sq = q["sq"]
            dqh_ref[rows, :] = (dqd * q["eg"] * (sq * (1.0 + qh * (1.0 - sq)))).astype(BF16)
            return carry

        lax.fori_loop(0, nsub, sub, 0)
    col = lambda j: pl.BlockSpec((th, wd), lambda h, t: (nt - 1 - t, (8 // hps) * j + h))
    out = pl.BlockSpec((th, wd), lambda h, t: (nt - 1 - t, h))
    tri_spec = pl.BlockSpec((sub_rows, sub_rows), lambda h, t: (0, 0))
    return _call(
        body, name=name, grid=(HGRN_HEADS // hps, nt),
        in_specs=[col(0), col(1), col(2), col(3), pl.BlockSpec((2, wd), lambda h, t: (0, h)),
                  pl.BlockSpec((1, 128), lambda h, t: (0, 0)), tri_spec, tri_spec,
                  pl.BlockSpec((hps, nc, 128, 128), lambda h, t: (h, nt - 1 - t, 0, 0)), out],
        out_specs=[out, out, out, out, pl.BlockSpec((1, wd), lambda h, t: (0, h)),
                   pl.BlockSpec((hps, 1, 128), lambda h, t: (h, 0, 0))],
        out_shape=[jax.ShapeDtypeStruct((T, D), BF16)] * 4 + [jax.ShapeDtypeStruct((1, D), F32),
                                                              jax.ShapeDtypeStruct((HGRN_HEADS, 1, 128), F32)],
        scratch_shapes=[pltpu.VMEM((hps, 128, 128), F32)],
        ins=[hg, hg, hg, hg, lb_logits, norm_g, tri, tri_t, sstart, dyh], carry=carry)


def _mix_fwd(name, ya, yh, gates, xh1, g1, b1, wall, tm):
    T = ya.shape[0]

    def body(ya_ref, yh_ref, ga_ref, gh_ref, xh_ref, g_ref, b_ref, wall_ref,
             xo_ref, r_ref, pa_ref, ph_ref, mg_ref, wpa, wph, wo, sem):
        @pl.when(pl.program_id(0) == 0)
        def _():
            _load_rows(wall_ref, [(wpa, RIN, RP), (wph, RIN + RP, RP), (wo, RIN + 2 * RP, RP)], sem)

        parts = [slice(p, min(p + PART_ROWS, tm)) for p in range(0, tm, PART_ROWS)]
        pas = [_dot(ya_ref[rs, :], wpa[...]).astype(BF16) for rs in parts]
        phs = [_dot(yh_ref[rs, :], wph[...]).astype(BF16) for rs in parts]
        mgs = []
        for rs, pa, ph in zip(parts, pas, phs):
            pa_ref[rs, :] = pa
            ph_ref[rs, :] = ph
            mgs.append((_sigmoid(ga_ref[rs, :]) * pa.astype(F32) + _sigmoid(gh_ref[rs, :]) * ph.astype(F32)).astype(BF16))
            mg_ref[rs, :] = mgs[-1]
        mixes = [_dot(mg, wo[...]) for mg in mgs]
        for rs, mix in zip(parts, mixes):
            y1 = xh_ref[rs, :] * g_ref[...] + b_ref[...]
            xh2, r = _ln_fwd(ALPHA * y1 + mix)
            xo_ref[rs, :] = xh2
            r_ref[rs, :] = jnp.broadcast_to(r, (xh2.shape[0], 128))

    row = lambda w: pl.BlockSpec((tm, w), lambda i: (i, 0))
    vec = pl.BlockSpec((1, D), lambda i: (0, 0))
    return pl.pallas_call(
        body, name=name, grid=(T // tm,),
        in_specs=[row(D), row(D), pl.BlockSpec((tm, D), lambda i: (i, 0)), pl.BlockSpec((tm, D), lambda i: (i, 1)),
                  row(D), vec, vec, HBM_SPEC],
        out_specs=[row(D), row(128), row(D), row(D), row(D)],
        out_shape=[jax.ShapeDtypeStruct((T, D), F32), jax.ShapeDtypeStruct((T, 128), F32)]
        + [jax.ShapeDtypeStruct((T, D), BF16)] * 3,
        scratch_shapes=[pltpu.VMEM((D, D), BF16)] * 3 + [pltpu.SemaphoreType.DMA((3 * NDEV,))],
        compiler_params=_params("arbitrary"),
    )(ya, yh, gates, gates, xh1, g1, b1, wall)


def _mix_bwd(name, dy2, xh2, r2, g2, gates, pa, ph, wall, tm, carry=None):
    T = dy2.shape[0]

    def body(dy_ref, xh_ref, r_ref, g_ref, ga_ref, gh_ref, pa_ref, ph_ref, wall_ref,
             dz_ref, dmix_ref, dpa_ref, dph_ref, dga_ref, dgh_ref, dya_ref, dyh_ref, dg_ref, db_ref,
             wpa, wph, wo, sem):
        @pl.when(pl.program_id(0) == 0)
        def _():
            _load_rows(wall_ref, [(wpa, RIN, RP), (wph, RIN + RP, RP), (wo, RIN + 2 * RP, RP)], sem)
            dg_ref[...] = jnp.zeros_like(dg_ref)
            db_ref[...] = jnp.zeros_like(db_ref)

        parts = [slice(p, min(p + PART_ROWS, tm)) for p in range(0, tm, PART_ROWS)]
        dmix = []
        for rs in parts:
            dz, dgp, dbp = _ln_bwd(dy_ref[rs, :], xh_ref[rs, :], r_ref[rs, :1], g_ref[...])
            dg_ref[...] += dgp
            db_ref[...] += dbp
            dz_ref[rs, :] = dz
            dmix.append(dz.astype(BF16))
            dmix_ref[rs, :] = dmix[-1]
        dmgs = [_dot_nt(d, wo[...]) for d in dmix]
        dpas, dphs = [], []
        for rs, dmg in zip(parts, dmgs):
            sa, sh = _sigmoid(ga_ref[rs, :]), _sigmoid(gh_ref[rs, :])
            dga_ref[rs, :] = (dmg * pa_ref[rs, :].astype(F32) * sa * (1.0 - sa)).astype(BF16)
            dgh_ref[rs, :] = (dmg * ph_ref[rs, :].astype(F32) * sh * (1.0 - sh)).astype(BF16)
            dpas.append((dmg * sa).astype(BF16))
            dphs.append((dmg * sh).astype(BF16))
            dpa_ref[rs, :] = dpas[-1]
            dph_ref[rs, :] = dphs[-1]
        for rs, dpa, dph in zip(parts, dpas, dphs):
            dya_ref[rs, :] = _dot_nt(dpa, wpa[...]).astype(BF16)
            dyh_ref[rs, :] = _dot_nt(dph, wph[...])

    row = lambda w: pl.BlockSpec((tm, w), lambda i: (i, 0))
    vec = pl.BlockSpec((1, D), lambda i: (0, 0))
    return _call(
        body, name=name, grid=(T // tm,),
        in_specs=[row(D), row(D), row(128), vec, pl.BlockSpec((tm, D), lambda i: (i, 0)),
                  pl.BlockSpec((tm, D), lambda i: (i, 1)), row(D), row(D), HBM_SPEC],
        out_specs=[row(D)] * 8 + [vec, vec],
        out_shape=[jax.ShapeDtypeStruct((T, D), F32)] + [jax.ShapeDtypeStruct((T, D), BF16)] * 6
        + [jax.ShapeDtypeStruct((T, D), F32)] + [jax.ShapeDtypeStruct((1, D), F32)] * 2,
        scratch_shapes=[pltpu.VMEM((D, D), BF16)] * 3 + [pltpu.SemaphoreType.DMA((3 * NDEV,))],
        ins=[dy2, xh2, r2, g2, gates, gates, pa, ph, wall], carry=carry)


def _adam(w, g, m, v):
    m = ADAM_B1 * m + (1.0 - ADAM_B1) * g
    v = ADAM_B2 * v + (1.0 - ADAM_B2) * (g * g)
    m_hat = m / (1.0 - ADAM_B1 ** ADAM_STEP)
    v_hat = v / (1.0 - ADAM_B2 ** ADAM_STEP)
    delta = -ADAM_LR * (m_hat / (jnp.sqrt(v_hat) + ADAM_EPS) + ADAM_WD * w)
    return delta, m, v


def _grad_steps(name, items, cb, carry=None):
    n = len(items)

    def body(*refs):
        ins, outs = refs[:4 * n], refs[4 * n:]
        for i in range(n):
            r_ref, w_ref, m_ref, v_ref = ins[4 * i:4 * i + 4]
            g_ref, d_ref, mo_ref, vo_ref = outs[4 * i:4 * i + 4]
            acc = r_ref[0].astype(F32)
            for k in range(1, NDEV):
                acc = acc + r_ref[k].astype(F32)
            g_ref[...] = acc
            d, mm, vv = _adam(w_ref[...], acc, m_ref[...], v_ref[...])
            d_ref[...] = d
            mo_ref[...] = mm
            vo_ref[...] = vv

    in_specs, out_specs, out_shape, ins = [], [], [], []
    for recv, w, m, v in items:
        rows = recv.shape[1]
        blk = pl.BlockSpec((rows, cb), lambda j: (0, j))
        in_specs += [pl.BlockSpec((NDEV, rows, cb), lambda j: (0, 0, j)), blk, blk, blk]
        out_specs += [blk] * 4
        out_shape += [jax.ShapeDtypeStruct(w.shape, F32)] * 4
        ins += [recv, w, m, v]
    outs, ex = _call(body, name=name, grid=(D // cb,), in_specs=in_specs, out_specs=out_specs, out_shape=out_shape,
                     scratch_shapes=[], ins=ins, carry=carry)
    return [tuple(outs[4 * i:4 * i + 4]) for i in range(n)], ex


_SMALL = [("ln1_g", D), ("ln1_b", D), ("ln2_g", D), ("ln2_b", D), ("ln3_g", D), ("ln3_b", D),
          ("b_in", DIN), ("lb", D), ("attn_sinks", 128), ("hgrn_norm_g", 128), ("loss", 128)]
_SMALL_OFF = {}
_o = 0
for _n, _w in _SMALL:
    _SMALL_OFF[_n] = (_o, _w)
    _o += _w
PACK = _o


def _small_reduce(name, packed):
    def body(p_ref, o_ref, buf, send_sems, recv_sems):
        x, y, c = lax.axis_index("x"), lax.axis_index("y"), lax.axis_index("c")
        me = 4 * x + 2 * y + c
        buf[me] = p_ref[...]
        copies = []
        for j in range(1, NDEV):
            px, py, pc = x ^ (j >> 2), y ^ ((j >> 1) & 1), c ^ (j & 1)
            cp = pltpu.make_async_remote_copy(
                src_ref=p_ref, dst_ref=buf.at[me], send_sem=send_sems.at[j - 1], recv_sem=recv_sems.at[j - 1],
                device_id=(px, py, pc), device_id_type=MESH)
            cp.start()
            copies.append(cp)
        for cp in copies:
            cp.wait()
        tot = buf[0]
        for k in range(1, NDEV):
            tot = tot + buf[k]
        o_ref[...] = tot

    return pl.pallas_call(
        body, name=name, out_shape=jax.ShapeDtypeStruct((1, PACK), F32),
        scratch_shapes=[pltpu.VMEM((NDEV, 1, PACK), F32), pltpu.SemaphoreType.DMA((NDEV - 1,)),
                        pltpu.SemaphoreType.DMA((NDEV - 1,))],
        compiler_params=pltpu.CompilerParams(vmem_limit_bytes=VMEM_LIMIT),
    )(packed)


def _small_step(name, total, small_w, small_m, small_v):
    names = ["ln1_g", "ln1_b", "ln2_g", "ln2_b", "ln3_g", "ln3_b", "b_in", "attn_sinks", "hgrn_lb_logits", "hgrn_norm_g"]
    np_ = len(names)

    def body(*refs):
        t_ref = refs[0]
        w_refs = refs[1:1 + np_]
        m_refs = refs[1 + np_:1 + 2 * np_]
        v_refs = refs[1 + 2 * np_:1 + 3 * np_]
        loss_ref, outs = refs[1 + 3 * np_], refs[2 + 3 * np_:]

        def part(n):
            o, w = _SMALL_OFF[n]
            return t_ref[:, o:o + w]

        loss_ref[...] = part("loss")
        for i, n in enumerate(names):
            w = w_refs[i][...]
            if n == "hgrn_lb_logits":
                m_ = jnp.maximum(w[0:1, :], w[1:2, :])
                e0, e1 = jnp.exp(w[0:1, :] - m_), jnp.exp(w[1:2, :] - m_)
                p0 = e0 / (e0 + e1)
                t = p0 * (1.0 - p0) * part("lb")
                g = jnp.concatenate([t, -t], axis=0)
            elif n == "attn_sinks":
                g = part(n)[:, :N_Q_HEADS]
            else:
                g = part(n)
            d, mm, vv = _adam(w, g, m_refs[i][...], v_refs[i][...])
            outs[4 * i][...] = g
            outs[4 * i + 1][...] = d
            outs[4 * i + 2][...] = mm
            outs[4 * i + 3][...] = vv

    out_shape = [jax.ShapeDtypeStruct((1, 128), F32)]
    for n in names:
        out_shape += [jax.ShapeDtypeStruct(small_w[n].shape, F32)] * 4
    return pl.pallas_call(
        body, name=name, out_shape=out_shape,
        compiler_params=pltpu.CompilerParams(vmem_limit_bytes=VMEM_LIMIT),
    )(total, *[small_w[n] for n in names], *[small_m[n] for n in names], *[small_v[n] for n in names]), names


def _tile(T, pref):
    return min(T, pref)


def kernel(x, ln1_g, ln1_b, ffn1_w1, ffn1_w3, ffn1_w2, ln2_g, ln2_b, w_in, b_in, attn_sinks, hgrn_lb_logits, hgrn_norm_g, w_proj_attn, w_proj_hgrn, w_out, ln3_g, ln3_b, ffn2_w1, ffn2_w3, ffn2_w2, loss_target, m_ln1_g, m_ln1_b, m_ffn1_w1, m_ffn1_w3, m_ffn1_w2, m_ln2_g, m_ln2_b, m_w_in, m_b_in, m_attn_sinks, m_hgrn_lb_logits, m_hgrn_norm_g, m_w_proj_attn, m_w_proj_hgrn, m_w_out, m_ln3_g, m_ln3_b, m_ffn2_w1, m_ffn2_w3, m_ffn2_w2, v_ln1_g, v_ln1_b, v_ffn1_w1, v_ffn1_w3, v_ffn1_w2, v_ln2_g, v_ln2_b, v_w_in, v_b_in, v_attn_sinks, v_hgrn_lb_logits, v_hgrn_norm_g, v_w_proj_attn, v_w_proj_hgrn, v_w_out, v_ln3_g, v_ln3_b, v_ffn2_w1, v_ffn2_w3, v_ffn2_w2):
    T = x.shape[1]
    xs = x[0]
    tgt = loss_target[0]
    tm = _tile(T, 256)
    tm2 = _tile(T, 512)
    tk = _tile(T, 2048)

    t_bf = lambda w: w[0].T.astype(BF16)
    n_bf = lambda w: w[0].astype(BF16)
    ffn_shard = lambda w1, w3, w2: jnp.concatenate([t_bf(w1), t_bf(w3), n_bf(w2)], axis=0)
    mix_shard = jnp.concatenate([t_bf(w_in), n_bf(w_proj_attn), n_bf(w_proj_hgrn), n_bf(w_out)], axis=0)
    (ffn1_all,) = _exchange_call("gather_ffn1", _gather_exchange(ffn_shard(ffn1_w1, ffn1_w3, ffn1_w2)))
    ffn_offs = (0, RF, 2 * RF)
    ropes = _rope_tables(T)

    (xh1, r1, a1, b1, xb0), (mix_all,) = _ffn_fwd("ffn1_fwd", xs, None, ffn1_all, ffn_offs, tm2,
                                                  carry=_gather_exchange(mix_shard))
    (qkv, kt, vt, hg, gates, y1b), (ffn2_all,) = _inproj_fwd(
        "inproj_fwd", xh1, ln1_g, ln1_b, mix_all, b_in, ropes, tm2,
        carry=_gather_exchange(ffn_shard(ffn2_w1, ffn2_w3, ffn2_w2)))
    ya = _attn_fwd("attn_fwd", qkv, vt, attn_sinks)
    yh, sstart = _hgrn_fwd("hgrn_fwd", hg, hgrn_lb_logits, hgrn_norm_g, HG_FWD)
    xh2, r2, pa, ph, merged = _mix_fwd("mix_fwd", ya, yh, gates, xh1, ln1_g, ln1_b, mix_all, tm2)
    (xh3, r3, a2, b2, y2b, loss_part), _ = _ffn_fwd("ffn2_fwd", xh2, (ln2_g, ln2_b), ffn2_all, ffn_offs, tm2,
                                                    loss=(ln3_g, ln3_b, tgt))

    (dy2, dab2, u2, df2, dg3, db3), _ = _ffn_bwd("ffn2_bwd", ("loss", ln3_b, tgt), xh3, r3, ln3_g, a2, b2, ffn2_all,
                                                 ffn_offs, tm)
    recv = {}
    g_ffn2_13, _ = _wgrad("wgrad_ffn2_w13", dab2, y2b, DFF // 2, tk)
    g_ffn2_2, (recv["ffn2_w1"],) = _wgrad("wgrad_ffn2_w2", u2, df2, DFF // 2, tk,
                                          carry=_grad_exchange([g_ffn2_13], [0], [RF]))
    (dz2, dmix, dpa, dph, dga, dgh, dya, dyh, dg2, db2), (recv["ffn2_w3"], recv["ffn2_w2"]) = _mix_bwd(
        "mix_bwd", dy2, xh2, r2, ln2_g, gates, pa, ph, mix_all, tm2,
        carry=_grad_exchange([g_ffn2_13, g_ffn2_2], [DFF, 0], [RF, RF]))
    g_wo, _ = _wgrad("wgrad_w_out", merged, dmix, D, tk)
    g_pa, _ = _wgrad("wgrad_w_pa", ya, dpa, D, tk)
    g_ph, _ = _wgrad("wgrad_w_ph", yh, dph, D, tk)
    (dq, dk, dv, dsink), (recv["w_proj_attn"], recv["w_proj_hgrn"], recv["w_out"]) = _attn_bwd(
        "attn_bwd", qkv, kt, attn_sinks, dya, carry=_grad_exchange([g_pa, g_ph, g_wo], [0, 0, 0], [RP, RP, RP]))
    (dfl, dqh, dih, dog, dlb, dng), _ = _hgrn_bwd("hgrn_bwd", hg, hgrn_lb_logits, hgrn_norm_g, sstart, dyh, HG_BWD)
    dy1, dproj, dbin = _inproj_bwd("inproj_bwd", dq, dk, dv, (dfl, dqh, dih, dog), (dga, dgh), dz2, mix_all, ropes, tm)
    g_win, _ = _wgrad("wgrad_w_in", dproj, y1b, DIN // 4, tk)
    (gx, dab1, u1, df1, dg1, db1), (recv["w_in"],) = _ffn_bwd(
        "ffn1_bwd", ("dy", dy1), xh1, r1, ln1_g, a1, b1, ffn1_all, ffn_offs, tm,
        carry=_grad_exchange([g_win], [0], [RIN]))
    g_ffn1_2, _ = _wgrad("wgrad_ffn1_w2", u1, df1, DFF // 2, tk)
    g_ffn1_1, (recv["ffn1_w2"],) = _wgrad("wgrad_ffn1_w1", dab1, xb0, DFF // 2, tk, 0, DFF,
                                          carry=_grad_exchange([g_ffn1_2], [0], [RF]))
    g_ffn1_3, (recv["ffn1_w1"],) = _wgrad("wgrad_ffn1_w3", dab1, xb0, DFF // 2, tk, DFF, DFF,
                                          carry=_grad_exchange([g_ffn1_1], [0], [RF]))
    (recv["ffn1_w3"],) = _exchange_call("exchange_ffn1_w3", _grad_exchange([g_ffn1_3], [0], [RF]))

    big = [("ffn1_w1", ffn1_w1, m_ffn1_w1, v_ffn1_w1, True), ("ffn1_w3", ffn1_w3, m_ffn1_w3, v_ffn1_w3, True),
           ("ffn1_w2", ffn1_w2, m_ffn1_w2, v_ffn1_w2, False), ("w_in", w_in, m_w_in, v_w_in, True),
           ("w_proj_attn", w_proj_attn, m_w_proj_attn, v_w_proj_attn, False),
           ("w_proj_hgrn", w_proj_hgrn, m_w_proj_hgrn, v_w_proj_hgrn, False),
           ("w_out", w_out, m_w_out, v_w_out, False),
           ("ffn2_w1", ffn2_w1, m_ffn2_w1, v_ffn2_w1, True), ("ffn2_w3", ffn2_w3, m_ffn2_w3, v_ffn2_w3, True),
           ("ffn2_w2", ffn2_w2, m_ffn2_w2, v_ffn2_w2, False)]
    view = lambda t, transposed: t[0].T if transposed else t[0]
    back = lambda t, transposed: t.T[None] if transposed else t[None]
    stepped, _ = _grad_steps("steps", [(recv[n], view(w, tr), view(m, tr), view(v, tr)) for n, w, m, v, tr in big], 128)
    res = {n: tuple(back(t, tr) for t in outs) for (n, _, _, _, tr), outs in zip(big, stepped)}

    parts = {"ln1_g": dg1, "ln1_b": db1, "ln2_g": dg2, "ln2_b": db2, "ln3_g": dg3, "ln3_b": db3, "b_in": dbin,
             "lb": dlb, "attn_sinks": dsink, "hgrn_norm_g": jnp.sum(dng, axis=0), "loss": loss_part[0:1, :]}
    packed = jnp.concatenate([parts[n] for n, _ in _SMALL], axis=1)
    small_w = dict(ln1_g=ln1_g, ln1_b=ln1_b, ln2_g=ln2_g, ln2_b=ln2_b, ln3_g=ln3_g, ln3_b=ln3_b, b_in=b_in,
                   attn_sinks=attn_sinks, hgrn_lb_logits=hgrn_lb_logits, hgrn_norm_g=hgrn_norm_g)
    small_m = dict(ln1_g=m_ln1_g, ln1_b=m_ln1_b, ln2_g=m_ln2_g, ln2_b=m_ln2_b, ln3_g=m_ln3_g, ln3_b=m_ln3_b,
                   b_in=m_b_in, attn_sinks=m_attn_sinks, hgrn_lb_logits=m_hgrn_lb_logits, hgrn_norm_g=m_hgrn_norm_g)
    small_v = dict(ln1_g=v_ln1_g, ln1_b=v_ln1_b, ln2_g=v_ln2_g, ln2_b=v_ln2_b, ln3_g=v_ln3_g, ln3_b=v_ln3_b,
                   b_in=v_b_in, attn_sinks=v_attn_sinks, hgrn_lb_logits=v_hgrn_lb_logits, hgrn_norm_g=v_hgrn_norm_g)
    outs, names = _small_step("small_step", _small_reduce("small_reduce", packed), small_w, small_m, small_v)
    loss = outs[0][0, 0]
    for i, n in enumerate(names):
        res[n] = tuple(outs[1 + 4 * i:5 + 4 * i])

    order = ["ln1_g", "ln1_b", "ffn1_w1", "ffn1_w3", "ffn1_w2", "ln2_g", "ln2_b", "w_in", "b_in", "attn_sinks",
             "hgrn_lb_logits", "hgrn_norm_g", "w_proj_attn", "w_proj_hgrn", "w_out", "ln3_g", "ln3_b",
             "ffn2_w1", "ffn2_w3", "ffn2_w2"]
    return (loss, gx[None], *[res[n][0] for n in order], *[res[n][1] for n in order],
            *[res[n][2] for n in order], *[res[n][3] for n in order])
```

```python
import jax
import jax.numpy as jnp
from jax import lax
from jax.experimental import pallas as pl
from jax.experimental.pallas import tpu as pltpu

F32 = jnp.float32
BF16 = jnp.bfloat16

NDEV = 8
D = 1024
DFF = 2816
RF = DFF // NDEV
DIN = 7680
RIN = DIN // NDEV
RP = D // NDEV
N_Q_HEADS = 16
N_KV_HEADS = 4
HEAD_DIM = 64
ATTN_BLOCK = 128
ROPE_THETA = 500000.0
ROPE_DIM = HEAD_DIM // 4
HGRN_HEADS = 8
HGRN_DK = 128
HGRN_CHUNK = 64
ALPHA = 2.0 ** 0.25
LN_EPS = 1e-5
RMS_EPS = 1e-6
NEG_INF = -1e30
ADAM_LR = 0.001
ADAM_B1 = 0.9
ADAM_B2 = 0.999
ADAM_EPS = 1e-08
ADAM_WD = 0.01
ADAM_STEP = 10

QKV_W = 1536
HG_W = 4096
GATE_W = 2048
VMEM_LIMIT = 60 * 2 ** 20
PART_ROWS = 256
MESH = pl.DeviceIdType.MESH
HBM_SPEC = pl.BlockSpec(memory_space=pltpu.HBM)


def _params(*sem):
    return pltpu.CompilerParams(dimension_semantics=sem, vmem_limit_bytes=VMEM_LIMIT)


def _dot(a, b):
    return jnp.dot(a, b, preferred_element_type=F32)


def _dot_nt(a, b):
    return lax.dot_general(a, b, (((1,), (1,)), ((), ())), preferred_element_type=F32)


def _dot_tn(a, b):
    return lax.dot_general(a, b, (((0,), (0,)), ((), ())), preferred_element_type=F32)


def _sigmoid(x):
    return 0.5 * jnp.tanh(0.5 * x) + 0.5


def _ln_fwd(z):
    mu = jnp.mean(z, axis=-1, keepdims=True)
    zc = z - mu
    var = jnp.mean(zc * zc, axis=-1, keepdims=True)
    r = lax.rsqrt(var + LN_EPS)
    return zc * r, r


def _ln_bwd(dy, xh, r, g):
    dxh = dy * g
    m1 = jnp.mean(dxh, axis=-1, keepdims=True)
    m2 = jnp.mean(dxh * xh, axis=-1, keepdims=True)
    dz = r * (dxh - m1 - xh * m2)
    return dz, jnp.sum(dy * xh, axis=0, keepdims=True), jnp.sum(dy, axis=0, keepdims=True)


def _load_rows(wall_ref, pieces, sem):
    copies = []
    for dst, off, r in pieces:
        for k in range(NDEV):
            c = pltpu.make_async_copy(wall_ref.at[k, pl.ds(off, r), :], dst.at[pl.ds(k * r, r), :], sem.at[len(copies)])
            c.start()
            copies.append(c)
    for c in copies:
        c.wait()


class _Exchange:
    def __init__(self, inputs, out_shape, scratch, begin, middle, end):
        self.inputs, self.out_shape, self.scratch = inputs, out_shape, scratch
        self.begin, self.middle, self.end = begin, middle, end


def _gather_exchange(shard):
    rows, cols = shard.shape

    def ops(ins, outs, scr):
        (x_ref,), (out_ref,), (send_sems, recv_sems, local_sem) = ins, outs, scr
        x, y, c = lax.axis_index("x"), lax.axis_index("y"), lax.axis_index("c")
        me, sibling = (x, y, c), (x, y, 1 - c)
        chips = [(1 - x, y), (x, 1 - y), (1 - x, 1 - y)]

        def slot(px, py, pc):
            return out_ref.at[4 * px + 2 * py + pc]

        def copy(k, block, to, src=None):
            return pltpu.make_async_remote_copy(
                src_ref=slot(*block) if src is None else src, dst_ref=slot(*block),
                send_sem=send_sems.at[k], recv_sem=recv_sems.at[k], device_id=to, device_id_type=MESH)

        mine = lambda: pltpu.make_async_copy(x_ref, slot(*me), local_sem)
        first = lambda: [copy(0, me, sibling, src=x_ref)] + [
            copy(1 + j, me, (*chip, c), src=x_ref) for j, chip in enumerate(chips)]
        passed = lambda: [copy(4 + j, (*chip, c), sibling) for j, chip in enumerate(chips)]
        return c, me, sibling, chips, copy, mine, first, passed

    def begin(*refs):
        _, _, _, _, _, mine, first, _ = ops(*refs)
        mine().start()
        for cp in first():
            cp.start()

    def middle(*refs):
        c, me, _, chips, copy, _, _, passed = ops(*refs)
        for (j, chip), fwd in zip(enumerate(chips), passed()):
            copy(1 + j, (*chip, c), me).wait_recv()
            fwd.start()

    def end(*refs):
        c, me, sibling, chips, copy, mine, first, passed = ops(*refs)
        copy(0, sibling, me).wait_recv()
        for j, chip in enumerate(chips):
            copy(4 + j, (*chip, 1 - c), me).wait_recv()
        for cp in first() + passed():
            cp.wait_send()
        mine().wait()

    return _Exchange([shard], [jax.ShapeDtypeStruct((NDEV, rows, cols), shard.dtype)],
                     [pltpu.SemaphoreType.DMA((7,)), pltpu.SemaphoreType.DMA((7,)), pltpu.SemaphoreType.DMA],
                     begin, middle, end)


def _grad_exchange(grads, bases, rows):
    n = len(grads)

    def copies(g_refs, out_refs, scr):
        send_sems, recv_sems, local_sems = scr
        x, y, c = lax.axis_index("x"), lax.axis_index("y"), lax.axis_index("c")
        me = 4 * x + 2 * y + c
        out = []
        for i in range(n):
            r = rows[i]
            src = lambda k: g_refs[i].at[pl.ds(pl.multiple_of(bases[i] + k * r, 16), r), :]
            out.append(pltpu.make_async_copy(src(me), out_refs[i].at[me], local_sems.at[i]))
            for j in range(1, NDEV):
                px, py, pc = x ^ (j >> 2), y ^ ((j >> 1) & 1), c ^ (j & 1)
                out.append(pltpu.make_async_remote_copy(
                    src_ref=src(4 * px + 2 * py + pc), dst_ref=out_refs[i].at[me],
                    send_sem=send_sems.at[i, j - 1], recv_sem=recv_sems.at[i, j - 1],
                    device_id=(px, py, pc), device_id_type=MESH))
        return out

    def begin(*refs):
        for cp in copies(*refs):
            cp.start()

    def end(*refs):
        for cp in copies(*refs):
            cp.wait()

    return _Exchange(list(grads), [jax.ShapeDtypeStruct((NDEV, r, D), g.dtype) for g, r in zip(grads, rows)],
                     [pltpu.SemaphoreType.DMA((n, NDEV - 1)), pltpu.SemaphoreType.DMA((n, NDEV - 1)),
                      pltpu.SemaphoreType.DMA((n,))], begin, None, end)


def _exchange_call(name, ex):
    ni, no = len(ex.inputs), len(ex.out_shape)

    def body(*refs):
        parts = (refs[:ni], refs[ni:ni + no], refs[ni + no:])
        ex.begin(*parts)
        if ex.middle is not None:
            ex.middle(*parts)
        ex.end(*parts)

    return pl.pallas_call(body, name=name, out_shape=ex.out_shape, in_specs=[HBM_SPEC] * ni, out_specs=[HBM_SPEC] * no,
                          scratch_shapes=ex.scratch)(*ex.inputs)


def _call(body, *, name, grid, in_specs, out_specs, out_shape, scratch_shapes, ins, carry=None):
    sem = ("arbitrary",) * len(grid)
    if carry is None:
        outs = pl.pallas_call(body, name=name, grid=grid, in_specs=in_specs, out_specs=out_specs, out_shape=out_shape,
                              scratch_shapes=scratch_shapes, compiler_params=_params(*sem))(*ins)
        return outs, None
    n_in, n_out, n_scr = len(ins), len(out_shape), len(scratch_shapes)
    ci, co = len(carry.inputs), len(carry.out_shape)
    total = 1
    for g in grid:
        total *= g

    def wrapped(*refs):
        own_in, ex_in = refs[:n_in], refs[n_in:n_in + ci]
        o0 = n_in + ci
        own_out, ex_out = refs[o0:o0 + n_out], refs[o0 + n_out:o0 + n_out + co]
        s0 = o0 + n_out + co
        own_scr, ex_scr = refs[s0:s0 + n_scr], refs[s0 + n_scr:]
        step = pl.program_id(0)
        for d in range(1, len(grid)):
            step = step * grid[d] + pl.program_id(d)
        parts = (ex_in, ex_out, ex_scr)
        pl.when(step == 0)(lambda: carry.begin(*parts))
        body(*own_in, *own_out, *own_scr)
        if carry.middle is not None:
            pl.when(step == (3 * total) // 4)(lambda: carry.middle(*parts))
        pl.when(step == total - 1)(lambda: carry.end(*parts))

    outs = pl.pallas_call(
        wrapped, name=name, grid=grid, in_specs=list(in_specs) + [HBM_SPEC] * ci,
        out_specs=list(out_specs) + [HBM_SPEC] * co, out_shape=list(out_shape) + list(carry.out_shape),
        scratch_shapes=list(scratch_shapes) + list(carry.scratch), compiler_params=_params(*sem),
    )(*ins, *carry.inputs)
    return outs[:n_out], outs[n_out:]


def _ffn_fwd(name, xin, affine, wall, offs, tm, loss=None, carry=None):
    T = xin.shape[0]
    nt = T // tm

    def body(*refs):
        it = iter(refs)
        x_ref = next(it)
        if affine is not None:
            g_ref, b_ref = next(it), next(it)
        wall_ref = next(it)
        if loss is not None:
            go_ref, bo_ref, tgt_ref = next(it), next(it), next(it)
        xh_ref, r_ref, a_ref, b2_ref, yb_ref = (next(it) for _ in range(5))
        if loss is not None:
            loss_ref = next(it)
        w1, w3, w2, sem = (next(it) for _ in range(4))

        @pl.when(pl.program_id(0) == 0)
        def _():
            _load_rows(wall_ref, [(w1, offs[0], RF), (w3, offs[1], RF), (w2, offs[2], RF)], sem)
            if loss is not None:
                loss_ref[...] = jnp.zeros_like(loss_ref)

        parts = [slice(p, min(p + PART_ROWS, tm)) for p in range(0, tm, PART_ROWS)]
        ys, ybs = [], []
        for rs in parts:
            y = x_ref[rs, :]
            if affine is not None:
                y = y * g_ref[...] + b_ref[...]
            ys.append(y)
            ybs.append(y.astype(BF16))
            yb_ref[rs, :] = ybs[-1]
        ab = [(_dot_nt(yb, w1[...]).astype(BF16), _dot_nt(yb, w3[...]).astype(BF16)) for yb in ybs]
        us = []
        for rs, (a, b) in zip(parts, ab):
            a_ref[rs, :] = a
            b2_ref[rs, :] = b
            af, bf = a.astype(F32), b.astype(F32)
            us.append((af * _sigmoid(af) * bf).astype(BF16))
        fs = [_dot(u, w2[...]) for u in us]
        for rs, y, f in zip(parts, ys, fs):
            xh, r = _ln_fwd(ALPHA * y + 0.5 * f)
            xh_ref[rs, :] = xh
            r_ref[rs, :] = jnp.broadcast_to(r, (xh.shape[0], 128))
            if loss is not None:
                e = xh * go_ref[...] + bo_ref[...] - tgt_ref[rs, :]
                loss_ref[...] += jnp.sum(e * e) * (0.5 / D)

    row = lambda w: pl.BlockSpec((tm, w), lambda i: (i, 0))
    vec = pl.BlockSpec((1, D), lambda i: (0, 0))
    ins, in_specs = [xin], [row(D)]
    if affine is not None:
        ins += list(affine)
        in_specs += [vec, vec]
    ins.append(wall)
    in_specs.append(HBM_SPEC)
    if loss is not None:
        ins += list(loss)
        in_specs += [vec, vec, row(D)]
    out_shape = [jax.ShapeDtypeStruct((T, D), F32), jax.ShapeDtypeStruct((T, 128), F32),
                 jax.ShapeDtypeStruct((T, DFF), BF16), jax.ShapeDtypeStruct((T, DFF), BF16),
                 jax.ShapeDtypeStruct((T, D), BF16)]
    out_specs = [row(D), row(128), row(DFF), row(DFF), row(D)]
    if loss is not None:
        out_shape.append(jax.ShapeDtypeStruct((8, 128), F32))
        out_specs.append(pl.BlockSpec((8, 128), lambda i: (0, 0)))
    return _call(body, name=name, grid=(nt,), in_specs=in_specs, out_specs=out_specs, out_shape=out_shape,
                 scratch_shapes=[pltpu.VMEM((DFF, D), BF16)] * 3 + [pltpu.SemaphoreType.DMA((3 * NDEV,))],
                 ins=ins, carry=carry)


def _ffn_bwd(name, dy_src, xh, r, g, a, b, wall, offs, tm, carry=None):
    T = xh.shape[0]
    nt = T // tm
    from_loss = dy_src[0] == "loss"

    def body(*refs):
        it = iter(refs)
        if from_loss:
            bo_ref, tgt_ref = next(it), next(it)
        else:
            dy_ref = next(it)
        xh_ref, r_ref, g_ref, a_ref, b_ref, wall_ref = (next(it) for _ in range(6))
        dyin_ref, dab_ref, u_ref, df_ref, dg_ref, db_ref = (next(it) for _ in range(6))
        w1, w3, w2, sem = (next(it) for _ in range(4))

        @pl.when(pl.program_id(0) == 0)
        def _():
            _load_rows(wall_ref, [(w1, offs[0], RF), (w3, offs[1], RF), (w2, offs[2], RF)], sem)
            dg_ref[...] = jnp.zeros_like(dg_ref)
            db_ref[...] = jnp.zeros_like(db_ref)

        parts = [slice(p, min(p + PART_ROWS, tm)) for p in range(0, tm, PART_ROWS)]
        gv = g_ref[...]
        dz, df = [], []
        for rs in parts:
            xhv = xh_ref[rs, :]
            if from_loss:
                dy = (xhv * gv + bo_ref[...] - tgt_ref[rs, :]) * (1.0 / D)
            else:
                dy = dy_ref[rs, :]
            dzp, dgp, dbp = _ln_bwd(dy, xhv, r_ref[rs, :1], gv)
            dg_ref[...] += dgp
            db_ref[...] += dbp
            dz.append(dzp)
            df.append((0.5 * dzp).astype(BF16))
            df_ref[rs, :] = df[-1]
        du = [_dot_nt(d, w2[...]) for d in df]
        da, dbb = [], []
        for rs, dup in zip(parts, du):
            af, bf = a_ref[rs, :].astype(F32), b_ref[rs, :].astype(F32)
            s = _sigmoid(af)
            sl = af * s
            u_ref[rs, :] = (sl * bf).astype(BF16)
            da.append((dup * bf * (s * (1.0 + af * (1.0 - s)))).astype(BF16))
            dbb.append((dup * sl).astype(BF16))
            dab_ref[rs, :DFF] = da[-1]
            dab_ref[rs, DFF:] = dbb[-1]
        for rs, dzp, dap, dbp in zip(parts, dz, da, dbb):
            dyin_ref[rs, :] = ALPHA * dzp + _dot(dap, w1[...]) + _dot(dbp, w3[...])

    row = lambda w: pl.BlockSpec((tm, w), lambda i: (i, 0))
    vec = pl.BlockSpec((1, D), lambda i: (0, 0))
    if from_loss:
        ins, in_specs = [dy_src[1], dy_src[2]], [vec, row(D)]
    else:
        ins, in_specs = [dy_src[1]], [row(D)]
    ins += [xh, r, g, a, b, wall]
    in_specs += [row(D), row(128), vec, row(DFF), row(DFF), HBM_SPEC]
    return _call(
        body, name=name, grid=(nt,), in_specs=in_specs,
        out_specs=[row(D), row(2 * DFF), row(DFF), row(D), vec, vec],
        out_shape=[jax.ShapeDtypeStruct((T, D), F32), jax.ShapeDtypeStruct((T, 2 * DFF), BF16),
                   jax.ShapeDtypeStruct((T, DFF), BF16), jax.ShapeDtypeStruct((T, D), BF16),
                   jax.ShapeDtypeStruct((1, D), F32), jax.ShapeDtypeStruct((1, D), F32)],
        scratch_shapes=[pltpu.VMEM((DFF, D), BF16)] * 3 + [pltpu.SemaphoreType.DMA((3 * NDEV,))],
        ins=ins, carry=carry)


def _wgrad(name, a, b, bn, tk, col0=0, ncols=None, carry=None):
    T = a.shape[0]
    N = a.shape[1] if ncols is None else ncols
    nk = T // tk
    c0 = col0 // bn

    def body(a_ref, b_ref, o_ref, acc):
        k = pl.program_id(1)

        @pl.when(k == 0)
        def _():
            acc[...] = jnp.zeros_like(acc)

        acc[...] += _dot_tn(a_ref[...], b_ref[...])

        @pl.when(k == nk - 1)
        def _():
            o_ref[...] = acc[...].astype(BF16)

    (out,), ex = _call(
        body, name=name, grid=(N // bn, nk),
        in_specs=[pl.BlockSpec((tk, bn), lambda n, k: (k, n + c0)), pl.BlockSpec((tk, D), lambda n, k: (k, 0))],
        out_specs=[pl.BlockSpec((bn, D), lambda n, k: (n, 0))],
        out_shape=[jax.ShapeDtypeStruct((N, D), BF16)],
        scratch_shapes=[pltpu.VMEM((bn, D), F32)], ins=[a, b], carry=carry)
    return out, ex


def _rope_tables(T):
    pos = jnp.arange(T, dtype=F32)
    inv_freq = ROPE_THETA ** (-jnp.arange(0, ROPE_DIM, 2, dtype=F32) / ROPE_DIM)
    half = ROPE_DIM // 2
    ch = jnp.arange(128) % HEAD_DIM
    ang = pos[:, None] * inv_freq[ch % half][None, :]
    cos, sin = jnp.cos(ang), jnp.sin(ang)
    first, second = (ch < half)[None, :], ((ch >= half) & (ch < ROPE_DIM))[None, :]
    return (jnp.where(first | second, cos, 1.0), jnp.where(first, -sin, 0.0), jnp.where(second, sin, 0.0))


def _rope(t, c, s1, s2):
    n = t.shape[1] // 128
    ct, s1t, s2t = (jnp.tile(v, (1, n)) for v in (c, s1, s2))
    w = t.shape[1]
    return t * ct + pltpu.roll(t, w - 8, 1) * s1t + pltpu.roll(t, 8, 1) * s2t


def _rope_t(dr, c, s1, s2):
    n = dr.shape[1] // 128
    ct, s1t, s2t = (jnp.tile(v, (1, n)) for v in (c, s1, s2))
    w = dr.shape[1]
    return dr * ct + pltpu.roll(dr * s1t, 8, 1) + pltpu.roll(dr * s2t, w - 8, 1)


_Q, _K, _V = (0, 1024), (1024, 256), (1280, 256)
_HG = (1536, HG_W)
_GATES = (5632, GATE_W)


def _inproj_fwd(name, xh, g, b, wall, b_in, ropes, tm, carry=None):
    T = xh.shape[0]

    def body(xh_ref, g_ref, b_ref, wall_ref, bin_ref, c_ref, s1_ref, s2_ref,
             qkv_ref, kt_ref, vt_ref, hg_ref, gate_ref, yb_ref, w, sem):
        @pl.when(pl.program_id(0) == 0)
        def _():
            _load_rows(wall_ref, [(w, 0, RIN)], sem)

        yb = (xh_ref[...] * g_ref[...] + b_ref[...]).astype(BF16)
        yb_ref[...] = yb
        c, s1, s2 = c_ref[...], s1_ref[...], s2_ref[...]

        def piece(start, width):
            return _dot_nt(yb, w[start:start + width, :]) + bin_ref[:, start:start + width]

        k = _rope(piece(*_K), c, s1, s2)
        v = piece(*_V)
        qkv_ref[:, 0:1024] = _rope(piece(*_Q), c, s1, s2).astype(BF16)
        qkv_ref[:, 1024:1280] = k.astype(BF16)
        qkv_ref[:, 1280:1536] = v.astype(BF16)
        kt_ref[...] = k.T.astype(BF16)
        vt_ref[...] = v.T.astype(BF16)
        for j in range(4):
            hg_ref[:, 1024 * j:1024 * (j + 1)] = piece(_HG[0] + 1024 * j, 1024)
        for j in range(2):
            gate_ref[:, 1024 * j:1024 * (j + 1)] = piece(_GATES[0] + 1024 * j, 1024)

    row = lambda wd: pl.BlockSpec((tm, wd), lambda i: (i, 0))
    vec = lambda wd: pl.BlockSpec((1, wd), lambda i: (0, 0))
    colt = pl.BlockSpec((256, tm), lambda i: (0, i))
    return _call(
        body, name=name, grid=(T // tm,),
        in_specs=[row(D), vec(D), vec(D), HBM_SPEC, vec(DIN), row(128), row(128), row(128)],
        out_specs=[row(QKV_W), colt, colt, row(HG_W), row(GATE_W), row(D)],
        out_shape=[jax.ShapeDtypeStruct((T, QKV_W), BF16), jax.ShapeDtypeStruct((256, T), BF16),
                   jax.ShapeDtypeStruct((256, T), BF16), jax.ShapeDtypeStruct((T, HG_W), F32),
                   jax.ShapeDtypeStruct((T, GATE_W), F32), jax.ShapeDtypeStruct((T, D), BF16)],
        scratch_shapes=[pltpu.VMEM((DIN, D), BF16), pltpu.SemaphoreType.DMA((NDEV,))],
        ins=[xh, g, b, wall, b_in, *ropes], carry=carry)


def _inproj_bwd(name, dq, dk, dv, dhg, dgates, dz2, wall, ropes, tm):
    T = dq.shape[0]

    def body(dq_ref, dk_ref, dv_ref, d0, d1, d2, d3, dga_ref, dgh_ref, dz_ref, wall_ref, c_ref, s1_ref, s2_ref,
             dy_ref, dproj_ref, dbin_ref, w, sem):
        @pl.when(pl.program_id(0) == 0)
        def _():
            _load_rows(wall_ref, [(w, 0, RIN)], sem)
            dbin_ref[...] = jnp.zeros_like(dbin_ref)

        c, s1, s2 = c_ref[...], s1_ref[...], s2_ref[...]
        acc = ALPHA * dz_ref[...]
        pieces = [(_Q[0], _rope_t(dq_ref[...], c, s1, s2)), (_K[0], _rope_t(dk_ref[...], c, s1, s2)),
                  (_V[0], dv_ref[...])]
        pieces += [(_HG[0] + 1024 * j, r[...]) for j, r in enumerate((d0, d1, d2, d3))]
        pieces += [(_GATES[0], dga_ref[...]), (_GATES[0] + 1024, dgh_ref[...])]
        for start, val in pieces:
            width = val.shape[1]
            dbin_ref[:, start:start + width] += jnp.sum(val.astype(F32), axis=0, keepdims=True)
            vb = val.astype(BF16)
            dproj_ref[:, start:start + width] = vb
            acc = acc + _dot(vb, w[start:start + width, :])
        dy_ref[...] = acc

    row = lambda wd: pl.BlockSpec((tm, wd), lambda i: (i, 0))
    return pl.pallas_call(
        body, name=name, grid=(T // tm,),
        in_specs=[row(D), row(256), row(256)] + [row(D)] * 4 + [row(D), row(D), row(D), HBM_SPEC] + [row(128)] * 3,
        out_specs=[row(D), row(DIN), pl.BlockSpec((1, DIN), lambda i: (0, 0))],
        out_shape=[jax.ShapeDtypeStruct((T, D), F32), jax.ShapeDtypeStruct((T, DIN), BF16),
                   jax.ShapeDtypeStruct((1, DIN), F32)],
        scratch_shapes=[pltpu.VMEM((DIN, D), BF16), pltpu.SemaphoreType.DMA((NDEV,))],
        compiler_params=_params("arbitrary"),
    )(dq, dk, dv, *dhg, *dgates, dz2, wall, *ropes)


def _halves(t):
    lane = lax.broadcasted_iota(jnp.int32, t.shape, 1)
    low = lane < HEAD_DIM
    sw = pltpu.roll(t, HEAD_DIM, 1)
    zero = jnp.zeros_like(t)
    h0 = (jnp.where(low, t, zero), jnp.where(low, zero, sw))
    h1 = (jnp.where(low, sw, zero), jnp.where(low, zero, t))
    return h0, h1


def _lane_stack(p_ref, c_ref, gp):
    sl = slice(128 * gp, 128 * (gp + 1))
    hp, hc = _halves(p_ref[:, sl].astype(F32)), _halves(c_ref[:, sl].astype(F32))
    return [jnp.concatenate([hp[gg][0], hc[gg][0], hp[gg][1], hc[gg][1]], axis=0).astype(BF16) for gg in range(2)]


def _row_stack(tp_ref, tc_ref, g):
    band = jnp.concatenate([tp_ref[64 * g:64 * (g + 1), :], tc_ref[64 * g:64 * (g + 1), :]], axis=1)
    z = jnp.zeros_like(band)
    return [jnp.concatenate([band, z], axis=0), jnp.concatenate([z, band], axis=0)]


def _fold(prevm, t4, hh):
    return jnp.where(prevm, t4[256 * hh:256 * hh + 128, :], t4[256 * hh + 128:256 * hh + 256, :])


def _unfold(prevm, t):
    return jnp.concatenate([jnp.where(prevm, t, 0.0), jnp.where(prevm, 0.0, t)], axis=0).astype(BF16)


def _attn_softmax(s, kill, sink):
    s = jnp.where(kill, NEG_INF, s)
    m = jnp.maximum(jnp.max(s, axis=0, keepdims=True), sink)
    p = jnp.exp(s - m)
    es = jnp.exp(sink - m)
    inv = 1.0 / (jnp.sum(p, axis=0, keepdims=True) + es)
    return p * inv, es * inv


def _attn_masks(first):
    row = lax.broadcasted_iota(jnp.int32, (ATTN_BLOCK, 2 * ATTN_BLOCK), 0)
    lane = lax.broadcasted_iota(jnp.int32, (ATTN_BLOCK, 2 * ATTN_BLOCK), 1)
    prevm = row > lane % ATTN_BLOCK
    return prevm, jnp.logical_and(first, prevm)


def _pair_rows(ref, g):
    return jnp.concatenate([ref[:, 256 * g:256 * g + 128], ref[:, 256 * g + 128:256 * (g + 1)]], axis=0)


def _pair_sinks(sink_ref, g, hh):
    h0, h1 = 4 * g + hh, 4 * g + 2 + hh
    return jnp.concatenate([jnp.broadcast_to(sink_ref[:, h0:h0 + 1], (1, ATTN_BLOCK)),
                            jnp.broadcast_to(sink_ref[:, h1:h1 + 1], (1, ATTN_BLOCK))], axis=1)


def _attn_fwd(name, qkv, vt, sinks):
    T = qkv.shape[0]
    nb = T // ATTN_BLOCK
    scale = HEAD_DIM ** -0.5

    def body(q_ref, kp_ref, kc_ref, vtp_ref, vtc_ref, sink_ref, o_ref):
        prevm, kill = _attn_masks(pl.program_id(0) == 0)
        kst = _lane_stack(kp_ref, kc_ref, 0) + _lane_stack(kp_ref, kc_ref, 1)
        vts = [_row_stack(vtp_ref, vtc_ref, g) for g in range(N_KV_HEADS)]
        s8 = [_dot_nt(kst[g], _pair_rows(q_ref, g)) * scale for g in range(N_KV_HEADS)]
        pu = [[_unfold(prevm, _attn_softmax(_fold(prevm, s8[g], hh), kill, _pair_sinks(sink_ref, g, hh))[0])
               for hh in range(2)] for g in range(N_KV_HEADS)]
        for g in range(N_KV_HEADS):
            ot = _dot(vts[g][0], pu[g][0]) + _dot(vts[g][1], pu[g][1])
            o_ref[:, 256 * g:256 * g + 128] = ot[:, :ATTN_BLOCK].T.astype(BF16)
            o_ref[:, 256 * g + 128:256 * (g + 1)] = ot[:, ATTN_BLOCK:].T.astype(BF16)

    prev = lambda i: jnp.maximum(i - 1, 0)
    return pl.pallas_call(
        body, name=name, grid=(nb,),
        in_specs=[pl.BlockSpec((ATTN_BLOCK, D), lambda i: (i, 0)),
                  pl.BlockSpec((ATTN_BLOCK, 256), lambda i: (prev(i), 4)),
                  pl.BlockSpec((ATTN_BLOCK, 256), lambda i: (i, 4)),
                  pl.BlockSpec((256, ATTN_BLOCK), lambda i: (0, prev(i))),
                  pl.BlockSpec((256, ATTN_BLOCK), lambda i: (0, i)),
                  pl.BlockSpec((1, N_Q_HEADS), lambda i: (0, 0))],
        out_specs=pl.BlockSpec((ATTN_BLOCK, D), lambda i: (i, 0)),
        out_shape=jax.ShapeDtypeStruct((T, D), BF16),
        compiler_params=_params("arbitrary"),
    )(qkv, qkv, qkv, vt, vt, sinks)


ATTN_BWD_ORDER = ((0, 0), (0, 1), (1, 0), (0, 2), (1, 1), (2, 0), (0, 3), (1, 2), (2, 1), (1, 3), (2, 2), (2, 3))


def _attn_bwd(name, qkv, kt, sinks, dya, carry=None):
    T = qkv.shape[0]
    nb = T // ATTN_BLOCK
    scale = HEAD_DIM ** -0.5

    def body(q_ref, kp_ref, kc_ref, vp_ref, vc_ref, ktp_ref, ktc_ref, sink_ref, do_ref,
             dq_ref, dk_ref, dv_ref, ds_ref):
        i = pl.program_id(0)

        @pl.when(i == 0)
        def _():
            ds_ref[...] = jnp.zeros_like(ds_ref)

        prevm, kill = _attn_masks(i == 0)
        kst = _lane_stack(kp_ref, kc_ref, 0) + _lane_stack(kp_ref, kc_ref, 1)
        vst = _lane_stack(vp_ref, vc_ref, 0) + _lane_stack(vp_ref, vc_ref, 1)
        kts = [_row_stack(ktp_ref, ktc_ref, g) for g in range(N_KV_HEADS)]
        lane = lax.broadcasted_iota(jnp.int32, (2 * ATTN_BLOCK, 128), 1)
        low = lane < HEAD_DIM
        slane = lax.broadcasted_iota(jnp.int32, (1, 128), 1)
        dkz = [jnp.zeros((2 * ATTN_BLOCK, 128), F32) for _ in range(N_KV_HEADS)]
        dvz = [jnp.zeros((2 * ATTN_BLOCK, 128), F32) for _ in range(N_KV_HEADS)]
        dsink = jnp.zeros((1, 128), F32)
        groups = range(N_KV_HEADS)
        qcat, docat, s8, dp8 = {}, {}, {}, {}
        ds_u, p_u, dsinks = {}, {}, []

        def scores(g):
            qcat[g], docat[g] = _pair_rows(q_ref, g), _pair_rows(do_ref, g)
            s8[g] = _dot_nt(kst[g], qcat[g]) * scale
            dp8[g] = _dot_nt(vst[g], docat[g])

        def algebra(g):
            for hh in range(2):
                pn, ps = _attn_softmax(_fold(prevm, s8[g], hh), kill, _pair_sinks(sink_ref, g, hh))
                dp = _fold(prevm, dp8[g], hh)
                delta = jnp.sum(pn * dp, axis=0, keepdims=True)
                sd = ps * delta
                dsinks.append(jnp.where(slane == 4 * g + hh, -jnp.sum(sd[:, :ATTN_BLOCK]), 0.0)
                              + jnp.where(slane == 4 * g + 2 + hh, -jnp.sum(sd[:, ATTN_BLOCK:]), 0.0))
                ds_u[g, hh] = _unfold(prevm, pn * (dp - delta) * scale)
                p_u[g, hh] = _unfold(prevm, pn)

        def grads(g):
            dqt = _dot(kts[g][0], ds_u[g, 0]) + _dot(kts[g][1], ds_u[g, 1])
            dq_ref[:, 256 * g:256 * g + 128] = dqt[:, :ATTN_BLOCK].T
            dq_ref[:, 256 * g + 128:256 * (g + 1)] = dqt[:, ATTN_BLOCK:].T
            for hh in range(2):
                own = low if hh == 0 else jnp.logical_not(low)
                dk_h = jnp.where(own, _dot(ds_u[g, hh], qcat[g]), 0.0)
                dv_h = jnp.where(own, _dot(p_u[g, hh], docat[g]), 0.0)
                if hh != g % 2:
                    dk_h = pltpu.roll(dk_h, HEAD_DIM, 1)
                    dv_h = pltpu.roll(dv_h, HEAD_DIM, 1)
                dkz[g] = dkz[g] + dk_h
                dvz[g] = dvz[g] + dv_h

        for stage, g in ATTN_BWD_ORDER:
            (scores, algebra, grads)[stage](g)
        for t in dsinks:
            dsink = dsink + t
        ds_ref[...] += dsink
        cur = pl.ds(pl.multiple_of(i * ATTN_BLOCK, ATTN_BLOCK), ATTN_BLOCK)
        prv = pl.ds(pl.multiple_of(jnp.maximum(i - 1, 0) * ATTN_BLOCK, ATTN_BLOCK), ATTN_BLOCK)
        for gp in range(N_KV_HEADS // 2):
            cols = slice(128 * gp, 128 * (gp + 1))
            dkb = dkz[2 * gp] + dkz[2 * gp + 1]
            dvb = dvz[2 * gp] + dvz[2 * gp + 1]
            dk_ref[cur, cols] = dkb[ATTN_BLOCK:, :]
            dv_ref[cur, cols] = dvb[ATTN_BLOCK:, :]

            @pl.when(i > 0)
            def _():
                dk_ref[prv, cols] += dkb[:ATTN_BLOCK, :]
                dv_ref[prv, cols] += dvb[:ATTN_BLOCK, :]

    prev = lambda i: jnp.maximum(i - 1, 0)
    whole = lambda w: pl.BlockSpec((T, w), lambda i: (0, 0))
    return _call(
        body, name=name, grid=(nb,), scratch_shapes=[], ins=[qkv, qkv, qkv, qkv, qkv, kt, kt, sinks, dya], carry=carry,
        in_specs=[pl.BlockSpec((ATTN_BLOCK, D), lambda i: (i, 0)),
                  pl.BlockSpec((ATTN_BLOCK, 256), lambda i: (prev(i), 4)),
                  pl.BlockSpec((ATTN_BLOCK, 256), lambda i: (i, 4)),
                  pl.BlockSpec((ATTN_BLOCK, 256), lambda i: (prev(i), 5)),
                  pl.BlockSpec((ATTN_BLOCK, 256), lambda i: (i, 5)),
                  pl.BlockSpec((256, ATTN_BLOCK), lambda i: (0, prev(i))),
                  pl.BlockSpec((256, ATTN_BLOCK), lambda i: (0, i)),
                  pl.BlockSpec((1, N_Q_HEADS), lambda i: (0, 0)),
                  pl.BlockSpec((ATTN_BLOCK, D), lambda i: (i, 0))],
        out_specs=[pl.BlockSpec((ATTN_BLOCK, D), lambda i: (i, 0)), whole(256), whole(256),
                   pl.BlockSpec((1, 128), lambda i: (0, 0))],
        out_shape=[jax.ShapeDtypeStruct((T, D), F32), jax.ShapeDtypeStruct((T, 256), F32),
                   jax.ShapeDtypeStruct((T, 256), F32), jax.ShapeDtypeStruct((1, 128), F32)])


HG_FWD = (256, 8, 512)
HG_BWD = (128, 8, 512)


def _chunk_sum(tri, x):
    w = x.shape[1]
    h1 = x.astype(BF16)
    h2 = (x - h1.astype(F32)).astype(BF16)
    r = _dot(tri, jnp.concatenate([h1, h2], axis=1))
    return r[:, :w] + r[:, w:]


def _chunk_tri(n):
    ri = lax.broadcasted_iota(jnp.int32, (n, n), 0)
    ci = lax.broadcasted_iota(jnp.int32, (n, n), 1)
    return jnp.where((ri >= ci) & (ri // HGRN_CHUNK == ci // HGRN_CHUNK), 1.0, 0.0).astype(BF16)


def _chunks(t):
    return [t[HGRN_CHUNK * c:HGRN_CHUNK * (c + 1), :] for c in range(t.shape[0] // HGRN_CHUNK)]


def _lower_bound(lbl_ref):
    l0, l1 = lbl_ref[0:1, :], lbl_ref[1:2, :]
    m = jnp.maximum(l0, l1)
    e0, e1 = jnp.exp(l0 - m), jnp.exp(l1 - m)
    return e0 / (e0 + e1)


def _heads(t):
    return [t[:, 128 * h:128 * (h + 1)] for h in range(t.shape[1] // 128)]


def _per_head(fn, *wide):
    return jnp.concatenate([fn(*parts) for parts in zip(*[_heads(t) for t in wide])], axis=1)


def _hgrn_sub(fl, qh, vv, lb, tril_b):
    w = fl.shape[1]
    sg = _sigmoid(fl)
    f = lb + (1.0 - lb) * sg
    k = 1.0 - f
    gc = _chunk_sum(tril_b, jnp.log(f))
    gl_c = [t[HGRN_CHUNK - 1:HGRN_CHUNK, :] for t in _chunks(gc)]
    gl = jnp.concatenate([jnp.broadcast_to(g, (HGRN_CHUNK, w)) for g in gl_c], axis=0)
    sq = _sigmoid(qh)
    eg = jnp.exp(gc)
    eng = jnp.exp(-gc)
    elg = jnp.exp(gl - gc)
    qd = qh * sq * eg
    ki = k * eng
    ke = k * elg
    qd_b, ki_b, ke_b, v_b = (t.astype(BF16) for t in (qd, ki, ke, vv))
    am_b = [jnp.where(tril_b > 0, _dot_nt(qq, kk), 0.0).astype(BF16) for qq, kk in zip(_heads(qd_b), _heads(ki_b))]
    return dict(sg=sg, f=f, sq=sq, eg=eg, eng=eng, elg=elg, qd=qd, ki=ki, ke=ke, egl=[jnp.exp(g) for g in gl_c],
                am_b=am_b, qd_b=qd_b, ki_b=ki_b, ke_b=ke_b, v_b=v_b)


def _hgrn_out(q, st_b):
    outs = []
    for h, (qd_h, v_h) in enumerate(zip(_heads(q["qd_b"]), _heads(q["v_b"]))):
        inter = [_dot_nt(qc, s) for qc, s in zip(_chunks(qd_h), st_b[h])]
        outs.append(_dot(q["am_b"][h], v_h) + jnp.concatenate(inter, axis=0))
    return jnp.concatenate(outs, axis=1)


def _head_rms(o):
    return _per_head(lambda t: jnp.broadcast_to(
        lax.rsqrt(jnp.mean(t * t, axis=-1, keepdims=True) + RMS_EPS), t.shape), o)


def _hgrn_fwd(name, hg, lb_logits, norm_g, cfg):
    T = hg.shape[0]
    sub_rows, hps, th = min(cfg[0], T), cfg[1], min(cfg[2], T)
    cps = sub_rows // HGRN_CHUNK
    nc = th // HGRN_CHUNK
    tri = _chunk_tri(sub_rows)

    def body(fl_ref, qh_ref, v_ref, og_ref, lbl_ref, ng_ref, tri_ref, y_ref, ss_ref, st):
        @pl.when(pl.program_id(1) == 0)
        def _():
            st[...] = jnp.zeros_like(st)

        lbs = _lower_bound(lbl_ref)
        ng = jnp.tile(ng_ref[...], (1, hps))

        def sub(si, carry):
            rows = pl.ds(pl.multiple_of(si * sub_rows, sub_rows), sub_rows)
            q = _hgrn_sub(fl_ref[rows, :], qh_ref[rows, :], v_ref[rows, :], lbs, tri_ref[...])
            v_hc = [_chunks(t) for t in _heads(q["v_b"])]
            k_hc = [_chunks(t) for t in _heads(q["ke_b"])]
            s = [st[h] for h in range(hps)]
            st_b = [[] for _ in range(hps)]
            for c in range(cps):
                egl = _heads(q["egl"][c])
                for h in range(hps):
                    ss_ref[h, si * cps + c] = s[h]
                    st_b[h].append(s[h].astype(BF16))
                    s[h] = s[h] * egl[h] + _dot_tn(v_hc[h][c], k_hc[h][c])
            for h in range(hps):
                st[h] = s[h]
            o = _hgrn_out(q, st_b)
            og = og_ref[rows, :]
            y_ref[rows, :] = (o * _head_rms(o) * ng * (og * _sigmoid(og))).astype(BF16)
            return carry

        lax.fori_loop(0, th // sub_rows, sub, 0)

    wd = 128 * hps
    col = lambda j: pl.BlockSpec((th, wd), lambda h, t: (t, (8 // hps) * j + h))
    return pl.pallas_call(
        body, name=name, grid=(HGRN_HEADS // hps, T // th),
        in_specs=[col(0), col(1), col(2), col(3), pl.BlockSpec((2, wd), lambda h, t: (0, h)),
                  pl.BlockSpec((1, 128), lambda h, t: (0, 0)), pl.BlockSpec((sub_rows, sub_rows), lambda h, t: (0, 0))],
        out_specs=[pl.BlockSpec((th, wd), lambda h, t: (t, h)),
                   pl.BlockSpec((hps, nc, 128, 128), lambda h, t: (h, t, 0, 0))],
        out_shape=[jax.ShapeDtypeStruct((T, D), BF16),
                   jax.ShapeDtypeStruct((HGRN_HEADS, T // HGRN_CHUNK, 128, 128), F32)],
        scratch_shapes=[pltpu.VMEM((hps, 128, 128), F32)],
        compiler_params=_params("parallel", "arbitrary"),
    )(hg, hg, hg, hg, lb_logits, norm_g, tri)


def _hgrn_bwd(name, hg, lb_logits, norm_g, sstart, dyh, cfg, carry=None):
    T = hg.shape[0]
    sub_rows, hps, th = min(cfg[0], T), cfg[1], min(cfg[2], T)
    cps = sub_rows // HGRN_CHUNK
    nc = th // HGRN_CHUNK
    nt = T // th
    wd = 128 * hps
    tri = _chunk_tri(sub_rows)
    tri_t = tri.T

    def body(fl_ref, qh_ref, v_ref, og_ref, lbl_ref, ng_ref, tri_ref, trit_ref, ss_ref, dy_ref,
             dfl_ref, dqh_ref, dv_ref, dog_ref, dlb_ref, dng_ref, dst):
        @pl.when(pl.program_id(1) == 0)
        def _():
            dst[...] = jnp.zeros_like(dst)
            dlb_ref[...] = jnp.zeros_like(dlb_ref)
            dng_ref[...] = jnp.zeros_like(dng_ref)

        lb = _lower_bound(lbl_ref)
        ng = jnp.tile(ng_ref[...], (1, hps))
        last = lax.broadcasted_iota(jnp.int32, (HGRN_CHUNK, wd), 0) == HGRN_CHUNK - 1
        nsub = th // sub_rows
        cat0 = lambda parts: jnp.concatenate(parts, axis=0)

        def sub(step, carry):
            si = nsub - 1 - step
            rows = pl.ds(pl.multiple_of(si * sub_rows, sub_rows), sub_rows)
            qh, og, dy = qh_ref[rows, :], og_ref[rows, :], dy_ref[rows, :]
            tril_b = tri_ref[...]
            q = _hgrn_sub(fl_ref[rows, :], qh, v_ref[rows, :], lb, tril_b)
            s_in = [[ss_ref[h, si * cps + c] for c in range(cps)] for h in range(hps)]
            st_b = [[s.astype(BF16) for s in row] for row in s_in]
            o = _hgrn_out(q, st_b)
            rr = _head_rms(o)
            oh = o * rr
            sog = _sigmoid(og)
            dog_ref[rows, :] = (dy * oh * ng * (sog * (1.0 + og * (1.0 - sog)))).astype(BF16)
            don = dy * (og * sog)
            dng_w = jnp.sum(don * oh, axis=0, keepdims=True)
            dd = don * ng
            mean_h = _per_head(lambda t: jnp.broadcast_to(jnp.mean(t, axis=-1, keepdims=True), t.shape), dd * oh)
            do_b = (rr * (dd - oh * mean_h)).astype(BF16)
            do_h, qd_h, ki_h, ke_h, v_h = (_heads(t) for t in (do_b, q["qd_b"], q["ki_b"], q["ke_b"], q["v_b"]))
            da_b = [jnp.where(tril_b > 0, _dot_nt(do_h[h], v_h[h]), 0.0).astype(BF16) for h in range(hps)]
            do_c, qd_c, ke_c, v_c = ([_chunks(t) for t in hs] for hs in (do_h, qd_h, ke_h, v_h))
            dsp = [[None] * cps for _ in range(hps)]
            dgl_dec = [[None] * cps for _ in range(hps)]
            d = [dst[h] for h in range(hps)]
            for c in reversed(range(cps)):
                egl = _heads(q["egl"][c])
                for h in range(hps):
                    dsp[h][c] = d[h]
                    dgl_dec[h][c] = jnp.sum(d[h] * s_in[h][c], axis=0, keepdims=True) * egl[h]
                    d[h] = d[h] * egl[h] + _dot_tn(do_c[h][c], qd_c[h][c])
            for h in range(hps):
                dst[h] = d[h]
                dng_ref[h] += dng_w[:, 128 * h:128 * (h + 1)]
            dsp_b = [[t.astype(BF16) for t in row] for row in dsp]
            dv_ref[rows, :] = jnp.concatenate(
                [_dot_tn(q["am_b"][h], do_h[h]) + cat0([_dot_nt(ke_c[h][c], dsp_b[h][c]) for c in range(cps)])
                 for h in range(hps)], axis=1).astype(BF16)
            dqd = jnp.concatenate(
                [_dot(da_b[h], ki_h[h]) + cat0([_dot(do_c[h][c], st_b[h][c]) for c in range(cps)])
                 for h in range(hps)], axis=1)
            dki = jnp.concatenate([_dot_tn(da_b[h], qd_h[h]) for h in range(hps)], axis=1)
            dke = jnp.concatenate([cat0([_dot(v_c[h][c], dsp_b[h][c]) for c in range(cps)]) for h in range(hps)], axis=1)
            dkk = dke * q["ke"]
            dgl = [jnp.sum(t, axis=0, keepdims=True) + jnp.concatenate([dgl_dec[h][c] for h in range(hps)], axis=1)
                   for c, t in enumerate(_chunks(dkk))]
            dgc = dqd * q["qd"] - dki * q["ki"] - dkk + cat0([jnp.where(last, g, 0.0) for g in dgl])
            dlf = _chunk_sum(trit_ref[...], dgc)
            df = dlf / q["f"] - (dki * q["eng"] + dke * q["elg"])
            sg = q["sg"]
            dlb_ref[...] += jnp.sum(df * (1.0 - sg), axis=0, keepdims=True)
            dfl_ref[rows, :] = (df * (1.0 - lb) * sg * (1.0 - sg)).astype(BF16)
            sq = q["sq"]
            dqh_ref[rows, :] = (dqd * q["eg"] * (sq * (1.0 + qh * (1.0 - sq)))).astype(BF16)
            return carry

        lax.fori_loop(0, nsub, sub, 0)
    col = lambda j: pl.BlockSpec((th, wd), lambda h, t: (nt - 1 - t, (8 // hps) * j + h))
    out = pl.BlockSpec((th, wd), lambda h, t: (nt - 1 - t, h))
    tri_spec = pl.BlockSpec((sub_rows, sub_rows), lambda h, t: (0, 0))
    return _call(
        body, name=name, grid=(HGRN_HEADS // hps, nt),
        in_specs=[col(0), col(1), col(2), col(3), pl.BlockSpec((2, wd), lambda h, t: (0, h)),
                  pl.BlockSpec((1, 128), lambda h, t: (0, 0)), tri_spec, tri_spec,
                  pl.BlockSpec((hps, nc, 128, 128), lambda h, t: (h, nt - 1 - t, 0, 0)), out],
        out_specs=[out, out, out, out, pl.BlockSpec((1, wd), lambda h, t: (0, h)),
                   pl.BlockSpec((hps, 1, 128), lambda h, t: (h, 0, 0))],
        out_shape=[jax.ShapeDtypeStruct((T, D), BF16)] * 4 + [jax.ShapeDtypeStruct((1, D), F32),
                                                              jax.ShapeDtypeStruct((HGRN_HEADS, 1, 128), F32)],
        scratch_shapes=[pltpu.VMEM((hps, 128, 128), F32)],
        ins=[hg, hg, hg, hg, lb_logits, norm_g, tri, tri_t, sstart, dyh], carry=carry)


def _mix_fwd(name, ya, yh, gates, xh1, g1, b1, wall, tm):
    T = ya.shape[0]

    def body(ya_ref, yh_ref, ga_ref, gh_ref, xh_ref, g_ref, b_ref, wall_ref,
             xo_ref, r_ref, pa_ref, ph_ref, mg_ref, wpa, wph, wo, sem):
        @pl.when(pl.program_id(0) == 0)
        def _():
            _load_rows(wall_ref, [(wpa, RIN, RP), (wph, RIN + RP, RP), (wo, RIN + 2 * RP, RP)], sem)

        parts = [slice(p, min(p + PART_ROWS, tm)) for p in range(0, tm, PART_ROWS)]
        pas = [_dot(ya_ref[rs, :], wpa[...]).astype(BF16) for rs in parts]
        phs = [_dot(yh_ref[rs, :], wph[...]).astype(BF16) for rs in parts]
        mgs = []
        for rs, pa, ph in zip(parts, pas, phs):
            pa_ref[rs, :] = pa
            ph_ref[rs, :] = ph
            mgs.append((_sigmoid(ga_ref[rs, :]) * pa.astype(F32) + _sigmoid(gh_ref[rs, :]) * ph.astype(F32)).astype(BF16))
            mg_ref[rs, :] = mgs[-1]
        mixes = [_dot(mg, wo[...]) for mg in mgs]
        for rs, mix in zip(parts, mixes):
            y1 = xh_ref[rs, :] * g_ref[...] + b_ref[...]
            xh2, r = _ln_fwd(ALPHA * y1 + mix)
            xo_ref[rs, :] = xh2
            r_ref[rs, :] = jnp.broadcast_to(r, (xh2.shape[0], 128))

    row = lambda w: pl.BlockSpec((tm, w), lambda i: (i, 0))
    vec = pl.BlockSpec((1, D), lambda i: (0, 0))
    return pl.pallas_call(
        body, name=name, grid=(T // tm,),
        in_specs=[row(D), row(D), pl.BlockSpec((tm, D), lambda i: (i, 0)), pl.BlockSpec((tm, D), lambda i: (i, 1)),
                  row(D), vec, vec, HBM_SPEC],
        out_specs=[row(D), row(128), row(D), row(D), row(D)],
        out_shape=[jax.ShapeDtypeStruct((T, D), F32), jax.ShapeDtypeStruct((T, 128), F32)]
        + [jax.ShapeDtypeStruct((T, D), BF16)] * 3,
        scratch_shapes=[pltpu.VMEM((D, D), BF16)] * 3 + [pltpu.SemaphoreType.DMA((3 * NDEV,))],
        compiler_params=_params("arbitrary"),
    )(ya, yh, gates, gates, xh1, g1, b1, wall)


def _mix_bwd(name, dy2, xh2, r2, g2, gates, pa, ph, wall, tm, carry=None):
    T = dy2.shape[0]

    def body(dy_ref, xh_ref, r_ref, g_ref, ga_ref, gh_ref, pa_ref, ph_ref, wall_ref,
             dz_ref, dmix_ref, dpa_ref, dph_ref, dga_ref, dgh_ref, dya_ref, dyh_ref, dg_ref, db_ref,
             wpa, wph, wo, sem):
        @pl.when(pl.program_id(0) == 0)
        def _():
            _load_rows(wall_ref, [(wpa, RIN, RP), (wph, RIN + RP, RP), (wo, RIN + 2 * RP, RP)], sem)
            dg_ref[...] = jnp.zeros_like(dg_ref)
            db_ref[...] = jnp.zeros_like(db_ref)

        parts = [slice(p, min(p + PART_ROWS, tm)) for p in range(0, tm, PART_ROWS)]
        dmix = []
        for rs in parts:
            dz, dgp, dbp = _ln_bwd(dy_ref[rs, :], xh_ref[rs, :], r_ref[rs, :1], g_ref[...])
            dg_ref[...] += dgp
            db_ref[...] += dbp
            dz_ref[rs, :] = dz
            dmix.append(dz.astype(BF16))
            dmix_ref[rs, :] = dmix[-1]
        dmgs = [_dot_nt(d, wo[...]) for d in dmix]
        dpas, dphs = [], []
        for rs, dmg in zip(parts, dmgs):
            sa, sh = _sigmoid(ga_ref[rs, :]), _sigmoid(gh_ref[rs, :])
            dga_ref[rs, :] = (dmg * pa_ref[rs, :].astype(F32) * sa * (1.0 - sa)).astype(BF16)
            dgh_ref[rs, :] = (dmg * ph_ref[rs, :].astype(F32) * sh * (1.0 - sh)).astype(BF16)
            dpas.append((dmg * sa).astype(BF16))
            dphs.append((dmg * sh).astype(BF16))
            dpa_ref[rs, :] = dpas[-1]
            dph_ref[rs, :] = dphs[-1]
        for rs, dpa, dph in zip(parts, dpas, dphs):
            dya_ref[rs, :] = _dot_nt(dpa, wpa[...]).astype(BF16)
            dyh_ref[rs, :] = _dot_nt(dph, wph[...])

    row = lambda w: pl.BlockSpec((tm, w), lambda i: (i, 0))
    vec = pl.BlockSpec((1, D), lambda i: (0, 0))
    return _call(
        body, name=name, grid=(T // tm,),
        in_specs=[row(D), row(D), row(128), vec, pl.BlockSpec((tm, D), lambda i: (i, 0)),
                  pl.BlockSpec((tm, D), lambda i: (i, 1)), row(D), row(D), HBM_SPEC],
        out_specs=[row(D)] * 8 + [vec, vec],
        out_shape=[jax.ShapeDtypeStruct((T, D), F32)] + [jax.ShapeDtypeStruct((T, D), BF16)] * 6
        + [jax.ShapeDtypeStruct((T, D), F32)] + [jax.ShapeDtypeStruct((1, D), F32)] * 2,
        scratch_shapes=[pltpu.VMEM((D, D), BF16)] * 3 + [pltpu.SemaphoreType.DMA((3 * NDEV,))],
        ins=[dy2, xh2, r2, g2, gates, gates, pa, ph, wall], carry=carry)


def _adam(w, g, m, v):
    m = ADAM_B1 * m + (1.0 - ADAM_B1) * g
    v = ADAM_B2 * v + (1.0 - ADAM_B2) * (g * g)
    m_hat = m / (1.0 - ADAM_B1 ** ADAM_STEP)
    v_hat = v / (1.0 - ADAM_B2 ** ADAM_STEP)
    delta = -ADAM_LR * (m_hat / (jnp.sqrt(v_hat) + ADAM_EPS) + ADAM_WD * w)
    return delta, m, v


def _grad_steps(name, items, cb, carry=None):
    n = len(items)

    def body(*refs):
        ins, outs = refs[:4 * n], refs[4 * n:]
        for i in range(n):
            r_ref, w_ref, m_ref, v_ref = ins[4 * i:4 * i + 4]
            g_ref, d_ref, mo_ref, vo_ref = outs[4 * i:4 * i + 4]
            acc = r_ref[0].astype(F32)
            for k in range(1, NDEV):
                acc = acc + r_ref[k].astype(F32)
            g_ref[...] = acc
            d, mm, vv = _adam(w_ref[...], acc, m_ref[...], v_ref[...])
            d_ref[...] = d
            mo_ref[...] = mm
            vo_ref[...] = vv

    in_specs, out_specs, out_shape, ins = [], [], [], []
    for recv, w, m, v in items:
        rows = recv.shape[1]
        blk = pl.BlockSpec((rows, cb), lambda j: (0, j))
        in_specs += [pl.BlockSpec((NDEV, rows, cb), lambda j: (0, 0, j)), blk, blk, blk]
        out_specs += [blk] * 4
        out_shape += [jax.ShapeDtypeStruct(w.shape, F32)] * 4
        ins += [recv, w, m, v]
    outs, ex = _call(body, name=name, grid=(D // cb,), in_specs=in_specs, out_specs=out_specs, out_shape=out_shape,
                     scratch_shapes=[], ins=ins, carry=carry)
    return [tuple(outs[4 * i:4 * i + 4]) for i in range(n)], ex


_SMALL = [("ln1_g", D), ("ln1_b", D), ("ln2_g", D), ("ln2_b", D), ("ln3_g", D), ("ln3_b", D),
          ("b_in", DIN), ("lb", D), ("attn_sinks", 128), ("hgrn_norm_g", 128), ("loss", 128)]
_SMALL_OFF = {}
_o = 0
for _n, _w in _SMALL:
    _SMALL_OFF[_n] = (_o, _w)
    _o += _w
PACK = _o


def _small_reduce(name, packed):
    def body(p_ref, o_ref, buf, send_sems, recv_sems):
        x, y, c = lax.axis_index("x"), lax.axis_index("y"), lax.axis_index("c")
        me = 4 * x + 2 * y + c
        buf[me] = p_ref[...]
        copies = []
        for j in range(1, NDEV):
            px, py, pc = x ^ (j >> 2), y ^ ((j >> 1) & 1), c ^ (j & 1)
            cp = pltpu.make_async_remote_copy(
                src_ref=p_ref, dst_ref=buf.at[me], send_sem=send_sems.at[j - 1], recv_sem=recv_sems.at[j - 1],
                device_id=(px, py, pc), device_id_type=MESH)
            cp.start()
            copies.append(cp)
        for cp in copies:
            cp.wait()
        tot = buf[0]
        for k in range(1, NDEV):
            tot = tot + buf[k]
        o_ref[...] = tot

    return pl.pallas_call(
        body, name=name, out_shape=jax.ShapeDtypeStruct((1, PACK), F32),
        scratch_shapes=[pltpu.VMEM((NDEV, 1, PACK), F32), pltpu.SemaphoreType.DMA((NDEV - 1,)),
                        pltpu.SemaphoreType.DMA((NDEV - 1,))],
        compiler_params=pltpu.CompilerParams(vmem_limit_bytes=VMEM_LIMIT),
    )(packed)


def _small_step(name, total, small_w, small_m, small_v):
    names = ["ln1_g", "ln1_b", "ln2_g", "ln2_b", "ln3_g", "ln3_b", "b_in", "attn_sinks", "hgrn_lb_logits", "hgrn_norm_g"]
    np_ = len(names)

    def body(*refs):
        t_ref = refs[0]
        w_refs = refs[1:1 + np_]
        m_refs = refs[1 + np_:1 + 2 * np_]
        v_refs = refs[1 + 2 * np_:1 + 3 * np_]
        loss_ref, outs = refs[1 + 3 * np_], refs[2 + 3 * np_:]

        def part(n):
            o, w = _SMALL_OFF[n]
            return t_ref[:, o:o + w]

        loss_ref[...] = part("loss")
        for i, n in enumerate(names):
            w = w_refs[i][...]
            if n == "hgrn_lb_logits":
                m_ = jnp.maximum(w[0:1, :], w[1:2, :])
                e0, e1 = jnp.exp(w[0:1, :] - m_), jnp.exp(w[1:2, :] - m_)
                p0 = e0 / (e0 + e1)
                t = p0 * (1.0 - p0) * part("lb")
                g = jnp.concatenate([t, -t], axis=0)
            elif n == "attn_sinks":
                g = part(n)[:, :N_Q_HEADS]
            else:
                g = part(n)
            d, mm, vv = _adam(w, g, m_refs[i][...], v_refs[i][...])
            outs[4 * i][...] = g
            outs[4 * i + 1][...] = d
            outs[4 * i + 2][...] = mm
            outs[4 * i + 3][...] = vv

    out_shape = [jax.ShapeDtypeStruct((1, 128), F32)]
    for n in names:
        out_shape += [jax.ShapeDtypeStruct(small_w[n].shape, F32)] * 4
    return pl.pallas_call(
        body, name=name, out_shape=out_shape,
        compiler_params=pltpu.CompilerParams(vmem_limit_bytes=VMEM_LIMIT),
    )(total, *[small_w[n] for n in names], *[small_m[n] for n in names], *[small_v[n] for n in names]), names


def _tile(T, pref):
    return min(T, pref)


def kernel(x, ln1_g, ln1_b, ffn1_w1, ffn1_w3, ffn1_w2, ln2_g, ln2_b, w_in, b_in, attn_sinks, hgrn_lb_logits, hgrn_norm_g, w_proj_attn, w_proj_hgrn, w_out, ln3_g, ln3_b, ffn2_w1, ffn2_w3, ffn2_w2, loss_target, m_ln1_g, m_ln1_b, m_ffn1_w1, m_ffn1_w3, m_ffn1_w2, m_ln2_g, m_ln2_b, m_w_in, m_b_in, m_attn_sinks, m_hgrn_lb_logits, m_hgrn_norm_g, m_w_proj_attn, m_w_proj_hgrn, m_w_out, m_ln3_g, m_ln3_b, m_ffn2_w1, m_ffn2_w3, m_ffn2_w2, v_ln1_g, v_ln1_b, v_ffn1_w1, v_ffn1_w3, v_ffn1_w2, v_ln2_g, v_ln2_b, v_w_in, v_b_in, v_attn_sinks, v_hgrn_lb_logits, v_hgrn_norm_g, v_w_proj_attn, v_w_proj_hgrn, v_w_out, v_ln3_g, v_ln3_b, v_ffn2_w1, v_ffn2_w3, v_ffn2_w2):
    T = x.shape[1]
    xs = x[0]
    tgt = loss_target[0]
    tm = _tile(T, 256)
    tm2 = _tile(T, 512)
    tk = _tile(T, 2048)

    t_bf = lambda w: w[0].T.astype(BF16)
    n_bf = lambda w: w[0].astype(BF16)
    ffn_shard = lambda w1, w3, w2: jnp.concatenate([t_bf(w1), t_bf(w3), n_bf(w2)], axis=0)
    mix_shard = jnp.concatenate([t_bf(w_in), n_bf(w_proj_attn), n_bf(w_proj_hgrn), n_bf(w_out)], axis=0)
    (ffn1_all,) = _exchange_call("gather_ffn1", _gather_exchange(ffn_shard(ffn1_w1, ffn1_w3, ffn1_w2)))
    ffn_offs = (0, RF, 2 * RF)
    ropes = _rope_tables(T)

    (xh1, r1, a1, b1, xb0), (mix_all,) = _ffn_fwd("ffn1_fwd", xs, None, ffn1_all, ffn_offs, tm2,
                                                  carry=_gather_exchange(mix_shard))
    (qkv, kt, vt, hg, gates, y1b), (ffn2_all,) = _inproj_fwd(
        "inproj_fwd", xh1, ln1_g, ln1_b, mix_all, b_in, ropes, tm2,
        carry=_gather_exchange(ffn_shard(ffn2_w1, ffn2_w3, ffn2_w2)))
    ya = _attn_fwd("attn_fwd", qkv, vt, attn_sinks)
    yh, sstart = _hgrn_fwd("hgrn_fwd", hg, hgrn_lb_logits, hgrn_norm_g, HG_FWD)
    xh2, r2, pa, ph, merged = _mix_fwd("mix_fwd", ya, yh, gates, xh1, ln1_g, ln1_b, mix_all, tm2)
    (xh3, r3, a2, b2, y2b, loss_part), _ = _ffn_fwd("ffn2_fwd", xh2, (ln2_g, ln2_b), ffn2_all, ffn_offs, tm2,
                                                    loss=(ln3_g, ln3_b, tgt))

    (dy2, dab2, u2, df2, dg3, db3), _ = _ffn_bwd("ffn2_bwd", ("loss", ln3_b, tgt), xh3, r3, ln3_g, a2, b2, ffn2_all,
                                                 ffn_offs, tm)
    recv = {}
    g_ffn2_13, _ = _wgrad("wgrad_ffn2_w13", dab2, y2b, DFF // 2, tk)
    g_ffn2_2, (recv["ffn2_w1"],) = _wgrad("wgrad_ffn2_w2", u2, df2, DFF // 2, tk,
                                          carry=_grad_exchange([g_ffn2_13], [0], [RF]))
    (dz2, dmix, dpa, dph, dga, dgh, dya, dyh, dg2, db2), (recv["ffn2_w3"],) = _mix_bwd(
        "mix_bwd", dy2, xh2, r2, ln2_g, gates, pa, ph, mix_all, tm2, carry=_grad_exchange([g_ffn2_13], [DFF], [RF]))
    g_wo, _ = _wgrad("wgrad_w_out", merged, dmix, D, tk)
    g_pa, _ = _wgrad("wgrad_w_pa", ya, dpa, D, tk)
    g_ph, _ = _wgrad("wgrad_w_ph", yh, dph, D, tk)
    (dq, dk, dv, dsink), (recv["ffn2_w2"],) = _attn_bwd(
        "attn_bwd", qkv, kt, attn_sinks, dya, carry=_grad_exchange([g_ffn2_2], [0], [RF]))
    (dfl, dqh, dih, dog, dlb, dng), (recv["w_proj_attn"], recv["w_proj_hgrn"], recv["w_out"]) = _hgrn_bwd(
        "hgrn_bwd", hg, hgrn_lb_logits, hgrn_norm_g, sstart, dyh, HG_BWD,
        carry=_grad_exchange([g_pa, g_ph, g_wo], [0, 0, 0], [RP, RP, RP]))
    dy1, dproj, dbin = _inproj_bwd("inproj_bwd", dq, dk, dv, (dfl, dqh, dih, dog), (dga, dgh), dz2, mix_all, ropes, tm)
    g_win, _ = _wgrad("wgrad_w_in", dproj, y1b, DIN // 4, tk)
    (gx, dab1, u1, df1, dg1, db1), (recv["w_in"],) = _ffn_bwd(
        "ffn1_bwd", ("dy", dy1), xh1, r1, ln1_g, a1, b1, ffn1_all, ffn_offs, tm,
        carry=_grad_exchange([g_win], [0], [RIN]))
    g_ffn1_2, _ = _wgrad("wgrad_ffn1_w2", u1, df1, DFF // 2, tk)
    g_ffn1_1, (recv["ffn1_w2"],) = _wgrad("wgrad_ffn1_w1", dab1, xb0, DFF // 2, tk, 0, DFF,
                                          carry=_grad_exchange([g_ffn1_2], [0], [RF]))
    g_ffn1_3, (recv["ffn1_w1"],) = _wgrad("wgrad_ffn1_w3", dab1, xb0, DFF // 2, tk, DFF, DFF,
                                          carry=_grad_exchange([g_ffn1_1], [0], [RF]))
    (recv["ffn1_w3"],) = _exchange_call("exchange_ffn1_w3", _grad_exchange([g_ffn1_3], [0], [RF]))

    big = [("ffn1_w1", ffn1_w1, m_ffn1_w1, v_ffn1_w1, True), ("ffn1_w3", ffn1_w3, m_ffn1_w3, v_ffn1_w3, True),
           ("ffn1_w2", ffn1_w2, m_ffn1_w2, v_ffn1_w2, False), ("w_in", w_in, m_w_in, v_w_in, True),
           ("w_proj_attn", w_proj_attn, m_w_proj_attn, v_w_proj_attn, False),
           ("w_proj_hgrn", w_proj_hgrn, m_w_proj_hgrn, v_w_proj_hgrn, False),
           ("w_out", w_out, m_w_out, v_w_out, False),
           ("ffn2_w1", ffn2_w1, m_ffn2_w1, v_ffn2_w1, True), ("ffn2_w3", ffn2_w3, m_ffn2_w3, v_ffn2_w3, True),
           ("ffn2_w2", ffn2_w2, m_ffn2_w2, v_ffn2_w2, False)]
    view = lambda t, transposed: t[0].T if transposed else t[0]
    back = lambda t, transposed: t.T[None] if transposed else t[None]
    stepped, _ = _grad_steps("steps", [(recv[n], view(w, tr), view(m, tr), view(v, tr)) for n, w, m, v, tr in big], 128)
    res = {n: tuple(back(t, tr) for t in outs) for (n, _, _, _, tr), outs in zip(big, stepped)}

    parts = {"ln1_g": dg1, "ln1_b": db1, "ln2_g": dg2, "ln2_b": db2, "ln3_g": dg3, "ln3_b": db3, "b_in": dbin,
             "lb": dlb, "attn_sinks": dsink, "hgrn_norm_g": jnp.sum(dng, axis=0), "loss": loss_part[0:1, :]}
    packed = jnp.concatenate([parts[n] for n, _ in _SMALL], axis=1)
    small_w = dict(ln1_g=ln1_g, ln1_b=ln1_b, ln2_g=ln2_g, ln2_b=ln2_b, ln3_g=ln3_g, ln3_b=ln3_b, b_in=b_in,
                   attn_sinks=attn_sinks, hgrn_lb_logits=hgrn_lb_logits, hgrn_norm_g=hgrn_norm_g)
    small_m = dict(ln1_g=m_ln1_g, ln1_b=m_ln1_b, ln2_g=m_ln2_g, ln2_b=m_ln2_b, ln3_g=m_ln3_g, ln3_b=m_ln3_b,
                   b_in=m_b_in, attn_sinks=m_attn_sinks, hgrn_lb_logits=m_hgrn_lb_logits, hgrn_norm_g=m_hgrn_norm_g)
    small_v = dict(ln1_g=v_ln1_g, ln1_b=v_ln1_b, ln2_g=v_ln2_g, ln2_b=v_ln2_b, ln3_g=v_ln3_g, ln3_b=v_ln3_b,
                   b_in=v_b_in, attn_sinks=v_attn_sinks, hgrn_lb_logits=v_hgrn_lb_logits, hgrn_norm_g=v_hgrn_norm_g)
    outs, names = _small_step("small_step", _small_reduce("small_reduce", packed), small_w, small_m, small_v)
    loss = outs[0][0, 0]
    for i, n in enumerate(names):
        res[n] = tuple(outs[1 + 4 * i:5 + 4 * i])

    order = ["ln1_g", "ln1_b", "ffn1_w1", "ffn1_w3", "ffn1_w2", "ln2_g", "ln2_b", "w_in", "b_in", "attn_sinks",
             "hgrn_lb_logits", "hgrn_norm_g", "w_proj_attn", "w_proj_hgrn", "w_out", "ln3_g", "ln3_b",
             "ffn2_w1", "ffn2_w3", "ffn2_w2"]
    return (loss, gx[None], *[res[n][0] for n in order], *[res[n][1] for n in order],
            *[res[n][2] for n in order], *[res[n][3] for n in order])
```

```python
import jax
import jax.numpy as jnp
from jax import lax
from jax.experimental import pallas as pl
from jax.experimental.pallas import tpu as pltpu

F32 = jnp.float32
BF16 = jnp.bfloat16

NDEV = 8
D = 1024
DFF = 2816
RF = DFF // NDEV
DIN = 7680
RIN = DIN // NDEV
RP = D // NDEV
N_Q_HEADS = 16
N_KV_HEADS = 4
HEAD_DIM = 64
ATTN_BLOCK = 128
ROPE_THETA = 500000.0
ROPE_DIM = HEAD_DIM // 4
HGRN_HEADS = 8
HGRN_DK = 128
HGRN_CHUNK = 64
ALPHA = 2.0 ** 0.25
LN_EPS = 1e-5
RMS_EPS = 1e-6
NEG_INF = -1e30
ADAM_LR = 0.001
ADAM_B1 = 0.9
ADAM_B2 = 0.999
ADAM_EPS = 1e-08
ADAM_WD = 0.01
ADAM_STEP = 10

QKV_W = 1536
HG_W = 4096
GATE_W = 2048
VMEM_LIMIT = 60 * 2 ** 20
PART_ROWS = 256
MESH = pl.DeviceIdType.MESH
HBM_SPEC = pl.BlockSpec(memory_space=pltpu.HBM)


def _params(*sem):
    return pltpu.CompilerParams(dimension_semantics=sem, vmem_limit_bytes=VMEM_LIMIT)


def _dot(a, b):
    return jnp.dot(a, b, preferred_element_type=F32)


def _dot_nt(a, b):
    return lax.dot_general(a, b, (((1,), (1,)), ((), ())), preferred_element_type=F32)


def _dot_tn(a, b):
    return lax.dot_general(a, b, (((0,), (0,)), ((), ())), preferred_element_type=F32)


def _sigmoid(x):
    return 0.5 * jnp.tanh(0.5 * x) + 0.5


def _ln_fwd(z):
    mu = jnp.mean(z, axis=-1, keepdims=True)
    zc = z - mu
    var = jnp.mean(zc * zc, axis=-1, keepdims=True)
    r = lax.rsqrt(var + LN_EPS)
    return zc * r, r


def _ln_bwd(dy, xh, r, g):
    dxh = dy * g
    m1 = jnp.mean(dxh, axis=-1, keepdims=True)
    m2 = jnp.mean(dxh * xh, axis=-1, keepdims=True)
    dz = r * (dxh - m1 - xh * m2)
    return dz, jnp.sum(dy * xh, axis=0, keepdims=True), jnp.sum(dy, axis=0, keepdims=True)


def _load_rows(wall_ref, pieces, sem):
    copies = []
    for dst, off, r in pieces:
        for k in range(NDEV):
            c = pltpu.make_async_copy(wall_ref.at[k, pl.ds(off, r), :], dst.at[pl.ds(k * r, r), :], sem.at[len(copies)])
            c.start()
            copies.append(c)
    for c in copies:
        c.wait()


class _Exchange:
    def __init__(self, inputs, out_shape, scratch, begin, middle, end):
        self.inputs, self.out_shape, self.scratch = inputs, out_shape, scratch
        self.begin, self.middle, self.end = begin, middle, end


def _gather_exchange(shard):
    rows, cols = shard.shape

    def ops(ins, outs, scr):
        (x_ref,), (out_ref,), (send_sems, recv_sems, local_sem) = ins, outs, scr
        x, y, c = lax.axis_index("x"), lax.axis_index("y"), lax.axis_index("c")
        me, sibling = (x, y, c), (x, y, 1 - c)
        chips = [(1 - x, y), (x, 1 - y), (1 - x, 1 - y)]

        def slot(px, py, pc):
            return out_ref.at[4 * px + 2 * py + pc]

        def copy(k, block, to, src=None):
            return pltpu.make_async_remote_copy(
                src_ref=slot(*block) if src is None else src, dst_ref=slot(*block),
                send_sem=send_sems.at[k], recv_sem=recv_sems.at[k], device_id=to, device_id_type=MESH)

        mine = lambda: pltpu.make_async_copy(x_ref, slot(*me), local_sem)
        first = lambda: [copy(0, me, sibling, src=x_ref)] + [
            copy(1 + j, me, (*chip, c), src=x_ref) for j, chip in enumerate(chips)]
        passed = lambda: [copy(4 + j, (*chip, c), sibling) for j, chip in enumerate(chips)]
        return c, me, sibling, chips, copy, mine, first, passed

    def begin(*refs):
        _, _, _, _, _, mine, first, _ = ops(*refs)
        mine().start()
        for cp in first():
            cp.start()

    def middle(*refs):
        c, me, _, chips, copy, _, _, passed = ops(*refs)
        for (j, chip), fwd in zip(enumerate(chips), passed()):
            copy(1 + j, (*chip, c), me).wait_recv()
            fwd.start()

    def end(*refs):
        c, me, sibling, chips, copy, mine, first, passed = ops(*refs)
        copy(0, sibling, me).wait_recv()
        for j, chip in enumerate(chips):
            copy(4 + j, (*chip, 1 - c), me).wait_recv()
        for cp in first() + passed():
            cp.wait_send()
        mine().wait()

    return _Exchange([shard], [jax.ShapeDtypeStruct((NDEV, rows, cols), shard.dtype)],
                     [pltpu.SemaphoreType.DMA((7,)), pltpu.SemaphoreType.DMA((7,)), pltpu.SemaphoreType.DMA],
                     begin, middle, end)


def _grad_exchange(grads, bases, rows, strides=None):
    n = len(grads)
    strides = rows if strides is None else strides

    def copies(g_refs, out_refs, scr):
        send_sems, recv_sems, local_sems = scr
        x, y, c = lax.axis_index("x"), lax.axis_index("y"), lax.axis_index("c")
        me = 4 * x + 2 * y + c
        out = []
        for i in range(n):
            r = rows[i]
            src = lambda k: g_refs[i].at[pl.ds(pl.multiple_of(bases[i] + k * strides[i], 16), r), :]
            out.append(pltpu.make_async_copy(src(me), out_refs[i].at[me], local_sems.at[i]))
            for j in range(1, NDEV):
                px, py, pc = x ^ (j >> 2), y ^ ((j >> 1) & 1), c ^ (j & 1)
                out.append(pltpu.make_async_remote_copy(
                    src_ref=src(4 * px + 2 * py + pc), dst_ref=out_refs[i].at[me],
                    send_sem=send_sems.at[i, j - 1], recv_sem=recv_sems.at[i, j - 1],
                    device_id=(px, py, pc), device_id_type=MESH))
        return out

    def begin(*refs):
        for cp in copies(*refs):
            cp.start()

    def end(*refs):
        for cp in copies(*refs):
            cp.wait()

    return _Exchange(list(grads), [jax.ShapeDtypeStruct((NDEV, r, g.shape[1]), g.dtype) for g, r in zip(grads, rows)],
                     [pltpu.SemaphoreType.DMA((n, NDEV - 1)), pltpu.SemaphoreType.DMA((n, NDEV - 1)),
                      pltpu.SemaphoreType.DMA((n,))], begin, None, end)


def _exchange_call(name, ex):
    ni, no = len(ex.inputs), len(ex.out_shape)

    def body(*refs):
        parts = (refs[:ni], refs[ni:ni + no], refs[ni + no:])
        ex.begin(*parts)
        if ex.middle is not None:
            ex.middle(*parts)
        ex.end(*parts)

    return pl.pallas_call(body, name=name, out_shape=ex.out_shape, in_specs=[HBM_SPEC] * ni, out_specs=[HBM_SPEC] * no,
                          scratch_shapes=ex.scratch)(*ex.inputs)


def _call(body, *, name, grid, in_specs, out_specs, out_shape, scratch_shapes, ins, carry=None):
    sem = ("arbitrary",) * len(grid)
    if carry is None:
        outs = pl.pallas_call(body, name=name, grid=grid, in_specs=in_specs, out_specs=out_specs, out_shape=out_shape,
                              scratch_shapes=scratch_shapes, compiler_params=_params(*sem))(*ins)
        return outs, None
    n_in, n_out, n_scr = len(ins), len(out_shape), len(scratch_shapes)
    ci, co = len(carry.inputs), len(carry.out_shape)
    total = 1
    for g in grid:
        total *= g

    def wrapped(*refs):
        own_in, ex_in = refs[:n_in], refs[n_in:n_in + ci]
        o0 = n_in + ci
        own_out, ex_out = refs[o0:o0 + n_out], refs[o0 + n_out:o0 + n_out + co]
        s0 = o0 + n_out + co
        own_scr, ex_scr = refs[s0:s0 + n_scr], refs[s0 + n_scr:]
        step = pl.program_id(0)
        for d in range(1, len(grid)):
            step = step * grid[d] + pl.program_id(d)
        parts = (ex_in, ex_out, ex_scr)
        pl.when(step == 0)(lambda: carry.begin(*parts))
        body(*own_in, *own_out, *own_scr)
        if carry.middle is not None:
            pl.when(step == (3 * total) // 4)(lambda: carry.middle(*parts))
        pl.when(step == total - 1)(lambda: carry.end(*parts))

    outs = pl.pallas_call(
        wrapped, name=name, grid=grid, in_specs=list(in_specs) + [HBM_SPEC] * ci,
        out_specs=list(out_specs) + [HBM_SPEC] * co, out_shape=list(out_shape) + list(carry.out_shape),
        scratch_shapes=list(scratch_shapes) + list(carry.scratch), compiler_params=_params(*sem),
    )(*ins, *carry.inputs)
    return outs[:n_out], outs[n_out:]


def _ffn_fwd(name, xin, affine, wall, offs, tm, loss=None, carry=None):
    T = xin.shape[0]
    nt = T // tm

    def body(*refs):
        it = iter(refs)
        x_ref = next(it)
        if affine is not None:
            g_ref, b_ref = next(it), next(it)
        wall_ref = next(it)
        if loss is not None:
            go_ref, bo_ref, tgt_ref = next(it), next(it), next(it)
        xh_ref, r_ref, a_ref, b2_ref, yb_ref = (next(it) for _ in range(5))
        if loss is not None:
            loss_ref = next(it)
        w1, w3, w2, sem = (next(it) for _ in range(4))

        @pl.when(pl.program_id(0) == 0)
        def _():
            _load_rows(wall_ref, [(w1, offs[0], RF), (w3, offs[1], RF), (w2, offs[2], RF)], sem)
            if loss is not None:
                loss_ref[...] = jnp.zeros_like(loss_ref)

        parts = [slice(p, min(p + PART_ROWS, tm)) for p in range(0, tm, PART_ROWS)]
        ys, ybs = [], []
        for rs in parts:
            y = x_ref[rs, :]
            if affine is not None:
                y = y * g_ref[...] + b_ref[...]
            ys.append(y)
            ybs.append(y.astype(BF16))
            yb_ref[rs, :] = ybs[-1]
        ab = [(_dot_nt(yb, w1[...]).astype(BF16), _dot_nt(yb, w3[...]).astype(BF16)) for yb in ybs]
        us = []
        for rs, (a, b) in zip(parts, ab):
            a_ref[rs, :] = a
            b2_ref[rs, :] = b
            af, bf = a.astype(F32), b.astype(F32)
            us.append((af * _sigmoid(af) * bf).astype(BF16))
        fs = [_dot(u, w2[...]) for u in us]
        for rs, y, f in zip(parts, ys, fs):
            xh, r = _ln_fwd(ALPHA * y + 0.5 * f)
            xh_ref[rs, :] = xh
            r_ref[rs, :] = jnp.broadcast_to(r, (xh.shape[0], 128))
            if loss is not None:
                e = xh * go_ref[...] + bo_ref[...] - tgt_ref[rs, :]
                loss_ref[...] += jnp.sum(e * e) * (0.5 / D)

    row = lambda w: pl.BlockSpec((tm, w), lambda i: (i, 0))
    vec = pl.BlockSpec((1, D), lambda i: (0, 0))
    ins, in_specs = [xin], [row(D)]
    if affine is not None:
        ins += list(affine)
        in_specs += [vec, vec]
    ins.append(wall)
    in_specs.append(HBM_SPEC)
    if loss is not None:
        ins += list(loss)
        in_specs += [vec, vec, row(D)]
    out_shape = [jax.ShapeDtypeStruct((T, D), F32), jax.ShapeDtypeStruct((T, 128), F32),
                 jax.ShapeDtypeStruct((T, DFF), BF16), jax.ShapeDtypeStruct((T, DFF), BF16),
                 jax.ShapeDtypeStruct((T, D), BF16)]
    out_specs = [row(D), row(128), row(DFF), row(DFF), row(D)]
    if loss is not None:
        out_shape.append(jax.ShapeDtypeStruct((8, 128), F32))
        out_specs.append(pl.BlockSpec((8, 128), lambda i: (0, 0)))
    return _call(body, name=name, grid=(nt,), in_specs=in_specs, out_specs=out_specs, out_shape=out_shape,
                 scratch_shapes=[pltpu.VMEM((DFF, D), BF16)] * 3 + [pltpu.SemaphoreType.DMA((3 * NDEV,))],
                 ins=ins, carry=carry)


def _ffn_bwd(name, dy_src, xh, r, g, a, b, wall, offs, tm, with_dx=True, carry=None):
    T = xh.shape[0]
    nt = T // tm
    from_loss = dy_src[0] == "loss"

    def body(*refs):
        it = iter(refs)
        if from_loss:
            bo_ref, tgt_ref = next(it), next(it)
        else:
            dy_ref = next(it)
        xh_ref, r_ref, g_ref, a_ref, b_ref, wall_ref = (next(it) for _ in range(6))
        dyin_ref, dab_ref, u_ref, df_ref, dg_ref, db_ref = (next(it) for _ in range(6))
        if with_dx:
            w1, w3 = next(it), next(it)
        w2, sem = next(it), next(it)

        @pl.when(pl.program_id(0) == 0)
        def _():
            _load_rows(wall_ref, ([(w1, offs[0], RF), (w3, offs[1], RF)] if with_dx else []) + [(w2, offs[2], RF)], sem)
            dg_ref[...] = jnp.zeros_like(dg_ref)
            db_ref[...] = jnp.zeros_like(db_ref)

        parts = [slice(p, min(p + PART_ROWS, tm)) for p in range(0, tm, PART_ROWS)]
        gv = g_ref[...]
        dz, df = [], []
        for rs in parts:
            xhv = xh_ref[rs, :]
            if from_loss:
                dy = (xhv * gv + bo_ref[...] - tgt_ref[rs, :]) * (1.0 / D)
            else:
                dy = dy_ref[rs, :]
            dzp, dgp, dbp = _ln_bwd(dy, xhv, r_ref[rs, :1], gv)
            dg_ref[...] += dgp
            db_ref[...] += dbp
            dz.append(dzp)
            df.append((0.5 * dzp).astype(BF16))
            df_ref[rs, :] = df[-1]
        du = [_dot_nt(d, w2[...]) for d in df]
        da, dbb = [], []
        for rs, dup in zip(parts, du):
            af, bf = a_ref[rs, :].astype(F32), b_ref[rs, :].astype(F32)
            s = _sigmoid(af)
            sl = af * s
            u_ref[rs, :] = (sl * bf).astype(BF16)
            da.append((dup * bf * (s * (1.0 + af * (1.0 - s)))).astype(BF16))
            dbb.append((dup * sl).astype(BF16))
            dab_ref[rs, :DFF] = da[-1]
            dab_ref[rs, DFF:] = dbb[-1]
        for rs, dzp, dap, dbp in zip(parts, dz, da, dbb):
            dyin_ref[rs, :] = ALPHA * dzp + _dot(dap, w1[...]) + _dot(dbp, w3[...]) if with_dx else dzp

    row = lambda w: pl.BlockSpec((tm, w), lambda i: (i, 0))
    vec = pl.BlockSpec((1, D), lambda i: (0, 0))
    if from_loss:
        ins, in_specs = [dy_src[1], dy_src[2]], [vec, row(D)]
    else:
        ins, in_specs = [dy_src[1]], [row(D)]
    ins += [xh, r, g, a, b, wall]
    in_specs += [row(D), row(128), vec, row(DFF), row(DFF), HBM_SPEC]
    return _call(
        body, name=name, grid=(nt,), in_specs=in_specs,
        out_specs=[row(D), row(2 * DFF), row(DFF), row(D), vec, vec],
        out_shape=[jax.ShapeDtypeStruct((T, D), F32), jax.ShapeDtypeStruct((T, 2 * DFF), BF16),
                   jax.ShapeDtypeStruct((T, DFF), BF16), jax.ShapeDtypeStruct((T, D), BF16),
                   jax.ShapeDtypeStruct((1, D), F32), jax.ShapeDtypeStruct((1, D), F32)],
        scratch_shapes=[pltpu.VMEM((DFF, D), BF16)] * (3 if with_dx else 1) + [pltpu.SemaphoreType.DMA((3 * NDEV,))],
        ins=ins, carry=carry)


def _ffn_dx(name, dab, dz, wall, offs, tm, carry=None):
    T = dz.shape[0]

    def body(dab_ref, dz_ref, wall_ref, o_ref, w13, sem):
        @pl.when(pl.program_id(0) == 0)
        def _():
            _load_rows(wall_ref, [(w13.at[pl.ds(0, DFF), :], offs[0], RF), (w13.at[pl.ds(DFF, DFF), :], offs[1], RF)], sem)

        for p in range(0, tm, PART_ROWS):
            rs = slice(p, min(p + PART_ROWS, tm))
            o_ref[rs, :] = ALPHA * dz_ref[rs, :] + _dot(dab_ref[rs, :], w13[...])

    row = lambda w: pl.BlockSpec((tm, w), lambda i: (i, 0))
    (out,), ex = _call(
        body, name=name, grid=(T // tm,), in_specs=[row(2 * DFF), row(D), HBM_SPEC], out_specs=[row(D)],
        out_shape=[jax.ShapeDtypeStruct((T, D), F32)],
        scratch_shapes=[pltpu.VMEM((2 * DFF, D), BF16), pltpu.SemaphoreType.DMA((2 * NDEV,))],
        ins=[dab, dz, wall], carry=carry)
    return out, ex


def _wgrad(name, a, b, bn, tk, col0=0, ncols=None, carry=None):
    T = a.shape[0]
    N = a.shape[1] if ncols is None else ncols
    nk = T // tk
    c0 = col0 // bn

    def body(a_ref, b_ref, o_ref, acc):
        k = pl.program_id(1)

        @pl.when(k == 0)
        def _():
            acc[...] = jnp.zeros_like(acc)

        acc[...] += _dot_tn(a_ref[...], b_ref[...])

        @pl.when(k == nk - 1)
        def _():
            o_ref[...] = acc[...].astype(BF16)

    (out,), ex = _call(
        body, name=name, grid=(N // bn, nk),
        in_specs=[pl.BlockSpec((tk, bn), lambda n, k: (k, n + c0)), pl.BlockSpec((tk, D), lambda n, k: (k, 0))],
        out_specs=[pl.BlockSpec((bn, D), lambda n, k: (n, 0))],
        out_shape=[jax.ShapeDtypeStruct((N, D), BF16)],
        scratch_shapes=[pltpu.VMEM((bn, D), F32)], ins=[a, b], carry=carry)
    return out, ex


def _rope_tables(T):
    pos = jnp.arange(T, dtype=F32)
    inv_freq = ROPE_THETA ** (-jnp.arange(0, ROPE_DIM, 2, dtype=F32) / ROPE_DIM)
    half = ROPE_DIM // 2
    ch = jnp.arange(128) % HEAD_DIM
    ang = pos[:, None] * inv_freq[ch % half][None, :]
    cos, sin = jnp.cos(ang), jnp.sin(ang)
    first, second = (ch < half)[None, :], ((ch >= half) & (ch < ROPE_DIM))[None, :]
    return (jnp.where(first | second, cos, 1.0), jnp.where(first, -sin, 0.0), jnp.where(second, sin, 0.0))


def _rope(t, c, s1, s2):
    n = t.shape[1] // 128
    ct, s1t, s2t = (jnp.tile(v, (1, n)) for v in (c, s1, s2))
    w = t.shape[1]
    return t * ct + pltpu.roll(t, w - 8, 1) * s1t + pltpu.roll(t, 8, 1) * s2t


def _rope_t(dr, c, s1, s2):
    n = dr.shape[1] // 128
    ct, s1t, s2t = (jnp.tile(v, (1, n)) for v in (c, s1, s2))
    w = dr.shape[1]
    return dr * ct + pltpu.roll(dr * s1t, 8, 1) + pltpu.roll(dr * s2t, w - 8, 1)


_Q, _K, _V = (0, 1024), (1024, 256), (1280, 256)
_HG = (1536, HG_W)
_GATES = (5632, GATE_W)


def _inproj_fwd(name, xh, g, b, wall, b_in, ropes, tm, carry=None):
    T = xh.shape[0]

    def body(xh_ref, g_ref, b_ref, wall_ref, bin_ref, c_ref, s1_ref, s2_ref,
             qkv_ref, kt_ref, vt_ref, hg_ref, gate_ref, yb_ref, w, sem):
        @pl.when(pl.program_id(0) == 0)
        def _():
            _load_rows(wall_ref, [(w, 0, RIN)], sem)

        yb = (xh_ref[...] * g_ref[...] + b_ref[...]).astype(BF16)
        yb_ref[...] = yb
        c, s1, s2 = c_ref[...], s1_ref[...], s2_ref[...]

        def piece(start, width):
            return _dot_nt(yb, w[start:start + width, :]) + bin_ref[:, start:start + width]

        k = _rope(piece(*_K), c, s1, s2)
        v = piece(*_V)
        qkv_ref[:, 0:1024] = _rope(piece(*_Q), c, s1, s2).astype(BF16)
        qkv_ref[:, 1024:1280] = k.astype(BF16)
        qkv_ref[:, 1280:1536] = v.astype(BF16)
        kt_ref[...] = k.T.astype(BF16)
        vt_ref[...] = v.T.astype(BF16)
        for j in range(4):
            hg_ref[:, 1024 * j:1024 * (j + 1)] = piece(_HG[0] + 1024 * j, 1024)
        for j in range(2):
            gate_ref[:, 1024 * j:1024 * (j + 1)] = piece(_GATES[0] + 1024 * j, 1024)

    row = lambda wd: pl.BlockSpec((tm, wd), lambda i: (i, 0))
    vec = lambda wd: pl.BlockSpec((1, wd), lambda i: (0, 0))
    colt = pl.BlockSpec((256, tm), lambda i: (0, i))
    return _call(
        body, name=name, grid=(T // tm,),
        in_specs=[row(D), vec(D), vec(D), HBM_SPEC, vec(DIN), row(128), row(128), row(128)],
        out_specs=[row(QKV_W), colt, colt, row(HG_W), row(GATE_W), row(D)],
        out_shape=[jax.ShapeDtypeStruct((T, QKV_W), BF16), jax.ShapeDtypeStruct((256, T), BF16),
                   jax.ShapeDtypeStruct((256, T), BF16), jax.ShapeDtypeStruct((T, HG_W), F32),
                   jax.ShapeDtypeStruct((T, GATE_W), F32), jax.ShapeDtypeStruct((T, D), BF16)],
        scratch_shapes=[pltpu.VMEM((DIN, D), BF16), pltpu.SemaphoreType.DMA((NDEV,))],
        ins=[xh, g, b, wall, b_in, *ropes], carry=carry)


def _inproj_bwd(name, dq, dk, dv, dhg, dgates, dz2, wall, ropes, tm):
    T = dq.shape[0]

    def body(dq_ref, dk_ref, dv_ref, d0, d1, d2, d3, dga_ref, dgh_ref, dz_ref, wall_ref, c_ref, s1_ref, s2_ref,
             dy_ref, dproj_ref, dbin_ref, w, sem):
        @pl.when(pl.program_id(0) == 0)
        def _():
            _load_rows(wall_ref, [(w, 0, RIN)], sem)
            dbin_ref[...] = jnp.zeros_like(dbin_ref)

        c, s1, s2 = c_ref[...], s1_ref[...], s2_ref[...]
        acc = ALPHA * dz_ref[...]
        pieces = [(_Q[0], _rope_t(dq_ref[...], c, s1, s2)), (_K[0], _rope_t(dk_ref[...], c, s1, s2)),
                  (_V[0], dv_ref[...])]
        pieces += [(_HG[0] + 1024 * j, r[...]) for j, r in enumerate((d0, d1, d2, d3))]
        pieces += [(_GATES[0], dga_ref[...]), (_GATES[0] + 1024, dgh_ref[...])]
        for start, val in pieces:
            width = val.shape[1]
            dbin_ref[:, start:start + width] += jnp.sum(val.astype(F32), axis=0, keepdims=True)
            vb = val.astype(BF16)
            dproj_ref[:, start:start + width] = vb
            acc = acc + _dot(vb, w[start:start + width, :])
        dy_ref[...] = acc

    row = lambda wd: pl.BlockSpec((tm, wd), lambda i: (i, 0))
    return pl.pallas_call(
        body, name=name, grid=(T // tm,),
        in_specs=[row(D), row(256), row(256)] + [row(D)] * 4 + [row(D), row(D), row(D), HBM_SPEC] + [row(128)] * 3,
        out_specs=[row(D), row(DIN), pl.BlockSpec((1, DIN), lambda i: (0, 0))],
        out_shape=[jax.ShapeDtypeStruct((T, D), F32), jax.ShapeDtypeStruct((T, DIN), BF16),
                   jax.ShapeDtypeStruct((1, DIN), F32)],
        scratch_shapes=[pltpu.VMEM((DIN, D), BF16), pltpu.SemaphoreType.DMA((NDEV,))],
        compiler_params=_params("arbitrary"),
    )(dq, dk, dv, *dhg, *dgates, dz2, wall, *ropes)


def _halves(t):
    lane = lax.broadcasted_iota(jnp.int32, t.shape, 1)
    low = lane < HEAD_DIM
    sw = pltpu.roll(t, HEAD_DIM, 1)
    zero = jnp.zeros_like(t)
    h0 = (jnp.where(low, t, zero), jnp.where(low, zero, sw))
    h1 = (jnp.where(low, sw, zero), jnp.where(low, zero, t))
    return h0, h1


def _lane_stack(p_ref, c_ref, gp):
    sl = slice(128 * gp, 128 * (gp + 1))
    hp, hc = _halves(p_ref[:, sl].astype(F32)), _halves(c_ref[:, sl].astype(F32))
    return [jnp.concatenate([hp[gg][0], hc[gg][0], hp[gg][1], hc[gg][1]], axis=0).astype(BF16) for gg in range(2)]


def _row_stack(tp_ref, tc_ref, g):
    band = jnp.concatenate([tp_ref[64 * g:64 * (g + 1), :], tc_ref[64 * g:64 * (g + 1), :]], axis=1)
    z = jnp.zeros_like(band)
    return [jnp.concatenate([band, z], axis=0), jnp.concatenate([z, band], axis=0)]


def _fold(prevm, t4, hh):
    return jnp.where(prevm, t4[256 * hh:256 * hh + 128, :], t4[256 * hh + 128:256 * hh + 256, :])


def _unfold(prevm, t):
    return jnp.concatenate([jnp.where(prevm, t, 0.0), jnp.where(prevm, 0.0, t)], axis=0).astype(BF16)


def _attn_softmax(s, kill, sink):
    s = jnp.where(kill, NEG_INF, s)
    m = jnp.maximum(jnp.max(s, axis=0, keepdims=True), sink)
    p = jnp.exp(s - m)
    es = jnp.exp(sink - m)
    inv = 1.0 / (jnp.sum(p, axis=0, keepdims=True) + es)
    return p * inv, es * inv


def _attn_masks(first):
    row = lax.broadcasted_iota(jnp.int32, (ATTN_BLOCK, 2 * ATTN_BLOCK), 0)
    lane = lax.broadcasted_iota(jnp.int32, (ATTN_BLOCK, 2 * ATTN_BLOCK), 1)
    prevm = row > lane % ATTN_BLOCK
    return prevm, jnp.logical_and(first, prevm)


def _pair_rows(ref, g):
    return jnp.concatenate([ref[:, 256 * g:256 * g + 128], ref[:, 256 * g + 128:256 * (g + 1)]], axis=0)


def _pair_sinks(sink_ref, g, hh):
    h0, h1 = 4 * g + hh, 4 * g + 2 + hh
    return jnp.concatenate([jnp.broadcast_to(sink_ref[:, h0:h0 + 1], (1, ATTN_BLOCK)),
                            jnp.broadcast_to(sink_ref[:, h1:h1 + 1], (1, ATTN_BLOCK))], axis=1)


def _attn_fwd(name, qkv, vt, sinks):
    T = qkv.shape[0]
    nb = T // ATTN_BLOCK
    scale = HEAD_DIM ** -0.5

    def body(q_ref, kp_ref, kc_ref, vtp_ref, vtc_ref, sink_ref, o_ref):
        prevm, kill = _attn_masks(pl.program_id(0) == 0)
        kst = _lane_stack(kp_ref, kc_ref, 0) + _lane_stack(kp_ref, kc_ref, 1)
        vts = [_row_stack(vtp_ref, vtc_ref, g) for g in range(N_KV_HEADS)]
        s8 = [_dot_nt(kst[g], _pair_rows(q_ref, g)) * scale for g in range(N_KV_HEADS)]
        pu = [[_unfold(prevm, _attn_softmax(_fold(prevm, s8[g], hh), kill, _pair_sinks(sink_ref, g, hh))[0])
               for hh in range(2)] for g in range(N_KV_HEADS)]
        for g in range(N_KV_HEADS):
            ot = _dot(vts[g][0], pu[g][0]) + _dot(vts[g][1], pu[g][1])
            o_ref[:, 256 * g:256 * g + 128] = ot[:, :ATTN_BLOCK].T.astype(BF16)
            o_ref[:, 256 * g + 128:256 * (g + 1)] = ot[:, ATTN_BLOCK:].T.astype(BF16)

    prev = lambda i: jnp.maximum(i - 1, 0)
    return pl.pallas_call(
        body, name=name, grid=(nb,),
        in_specs=[pl.BlockSpec((ATTN_BLOCK, D), lambda i: (i, 0)),
                  pl.BlockSpec((ATTN_BLOCK, 256), lambda i: (prev(i), 4)),
                  pl.BlockSpec((ATTN_BLOCK, 256), lambda i: (i, 4)),
                  pl.BlockSpec((256, ATTN_BLOCK), lambda i: (0, prev(i))),
                  pl.BlockSpec((256, ATTN_BLOCK), lambda i: (0, i)),
                  pl.BlockSpec((1, N_Q_HEADS), lambda i: (0, 0))],
        out_specs=pl.BlockSpec((ATTN_BLOCK, D), lambda i: (i, 0)),
        out_shape=jax.ShapeDtypeStruct((T, D), BF16),
        compiler_params=_params("arbitrary"),
    )(qkv, qkv, qkv, vt, vt, sinks)


ATTN_BWD_ORDER = ((0, 0), (0, 1), (1, 0), (0, 2), (1, 1), (2, 0), (0, 3), (1, 2), (2, 1), (1, 3), (2, 2), (2, 3))


def _attn_bwd(name, qkv, kt, sinks, dya, carry=None):
    T = qkv.shape[0]
    nb = T // ATTN_BLOCK
    scale = HEAD_DIM ** -0.5

    def body(q_ref, kp_ref, kc_ref, vp_ref, vc_ref, ktp_ref, ktc_ref, sink_ref, do_ref,
             dq_ref, dk_ref, dv_ref, ds_ref):
        i = pl.program_id(0)

        @pl.when(i == 0)
        def _():
            ds_ref[...] = jnp.zeros_like(ds_ref)

        prevm, kill = _attn_masks(i == 0)
        kst = _lane_stack(kp_ref, kc_ref, 0) + _lane_stack(kp_ref, kc_ref, 1)
        vst = _lane_stack(vp_ref, vc_ref, 0) + _lane_stack(vp_ref, vc_ref, 1)
        kts = [_row_stack(ktp_ref, ktc_ref, g) for g in range(N_KV_HEADS)]
        lane = lax.broadcasted_iota(jnp.int32, (2 * ATTN_BLOCK, 128), 1)
        low = lane < HEAD_DIM
        slane = lax.broadcasted_iota(jnp.int32, (1, 128), 1)
        dkz = [jnp.zeros((2 * ATTN_BLOCK, 128), F32) for _ in range(N_KV_HEADS)]
        dvz = [jnp.zeros((2 * ATTN_BLOCK, 128), F32) for _ in range(N_KV_HEADS)]
        dsink = jnp.zeros((1, 128), F32)
        groups = range(N_KV_HEADS)
        qcat, docat, s8, dp8 = {}, {}, {}, {}
        ds_u, p_u, dsinks = {}, {}, []

        def scores(g):
            qcat[g], docat[g] = _pair_rows(q_ref, g), _pair_rows(do_ref, g)
            s8[g] = _dot_nt(kst[g], qcat[g]) * scale
            dp8[g] = _dot_nt(vst[g], docat[g])

        def algebra(g):
            for hh in range(2):
                pn, ps = _attn_softmax(_fold(prevm, s8[g], hh), kill, _pair_sinks(sink_ref, g, hh))
                dp = _fold(prevm, dp8[g], hh)
                delta = jnp.sum(pn * dp, axis=0, keepdims=True)
                sd = ps * delta
                dsinks.append(jnp.where(slane == 4 * g + hh, -jnp.sum(sd[:, :ATTN_BLOCK]), 0.0)
                              + jnp.where(slane == 4 * g + 2 + hh, -jnp.sum(sd[:, ATTN_BLOCK:]), 0.0))
                ds_u[g, hh] = _unfold(prevm, pn * (dp - delta) * scale)
                p_u[g, hh] = _unfold(prevm, pn)

        def grads(g):
            dqt = _dot(kts[g][0], ds_u[g, 0]) + _dot(kts[g][1], ds_u[g, 1])
            dq_ref[:, 256 * g:256 * g + 128] = dqt[:, :ATTN_BLOCK].T
            dq_ref[:, 256 * g + 128:256 * (g + 1)] = dqt[:, ATTN_BLOCK:].T
            for hh in range(2):
                own = low if hh == 0 else jnp.logical_not(low)
                dk_h = jnp.where(own, _dot(ds_u[g, hh], qcat[g]), 0.0)
                dv_h = jnp.where(own, _dot(p_u[g, hh], docat[g]), 0.0)
                if hh != g % 2:
                    dk_h = pltpu.roll(dk_h, HEAD_DIM, 1)
                    dv_h = pltpu.roll(dv_h, HEAD_DIM, 1)
                dkz[g] = dkz[g] + dk_h
                dvz[g] = dvz[g] + dv_h

        for stage, g in ATTN_BWD_ORDER:
            (scores, algebra, grads)[stage](g)
        for t in dsinks:
            dsink = dsink + t
        ds_ref[...] += dsink
        cur = pl.ds(pl.multiple_of(i * ATTN_BLOCK, ATTN_BLOCK), ATTN_BLOCK)
        prv = pl.ds(pl.multiple_of(jnp.maximum(i - 1, 0) * ATTN_BLOCK, ATTN_BLOCK), ATTN_BLOCK)
        for gp in range(N_KV_HEADS // 2):
            cols = slice(128 * gp, 128 * (gp + 1))
            dkb = dkz[2 * gp] + dkz[2 * gp + 1]
            dvb = dvz[2 * gp] + dvz[2 * gp + 1]
            dk_ref[cur, cols] = dkb[ATTN_BLOCK:, :]
            dv_ref[cur, cols] = dvb[ATTN_BLOCK:, :]

            @pl.when(i > 0)
            def _():
                dk_ref[prv, cols] += dkb[:ATTN_BLOCK, :]
                dv_ref[prv, cols] += dvb[:ATTN_BLOCK, :]

    prev = lambda i: jnp.maximum(i - 1, 0)
    whole = lambda w: pl.BlockSpec((T, w), lambda i: (0, 0))
    return _call(
        body, name=name, grid=(nb,), scratch_shapes=[], ins=[qkv, qkv, qkv, qkv, qkv, kt, kt, sinks, dya], carry=carry,
        in_specs=[pl.BlockSpec((ATTN_BLOCK, D), lambda i: (i, 0)),
                  pl.BlockSpec((ATTN_BLOCK, 256), lambda i: (prev(i), 4)),
                  pl.BlockSpec((ATTN_BLOCK, 256), lambda i: (i, 4)),
                  pl.BlockSpec((ATTN_BLOCK, 256), lambda i: (prev(i), 5)),
                  pl.BlockSpec((ATTN_BLOCK, 256), lambda i: (i, 5)),
                  pl.BlockSpec((256, ATTN_BLOCK), lambda i: (0, prev(i))),
                  pl.BlockSpec((256, ATTN_BLOCK), lambda i: (0, i)),
                  pl.BlockSpec((1, N_Q_HEADS), lambda i: (0, 0)),
                  pl.BlockSpec((ATTN_BLOCK, D), lambda i: (i, 0))],
        out_specs=[pl.BlockSpec((ATTN_BLOCK, D), lambda i: (i, 0)), whole(256), whole(256),
                   pl.BlockSpec((1, 128), lambda i: (0, 0))],
        out_shape=[jax.ShapeDtypeStruct((T, D), F32), jax.ShapeDtypeStruct((T, 256), F32),
                   jax.ShapeDtypeStruct((T, 256), F32), jax.ShapeDtypeStruct((1, 128), F32)])


HG_FWD = (256, 8, 512)
HG_BWD = (128, 8, 512)


def _chunk_sum(tri, x):
    w = x.shape[1]
    h1 = x.astype(BF16)
    h2 = (x - h1.astype(F32)).astype(BF16)
    r = _dot(tri, jnp.concatenate([h1, h2], axis=1))
    return r[:, :w] + r[:, w:]


def _chunk_tri(n):
    ri = lax.broadcasted_iota(jnp.int32, (n, n), 0)
    ci = lax.broadcasted_iota(jnp.int32, (n, n), 1)
    return jnp.where((ri >= ci) & (ri // HGRN_CHUNK == ci // HGRN_CHUNK), 1.0, 0.0).astype(BF16)


def _chunks(t):
    return [t[HGRN_CHUNK * c:HGRN_CHUNK * (c + 1), :] for c in range(t.shape[0] // HGRN_CHUNK)]


def _lower_bound(lbl_ref):
    l0, l1 = lbl_ref[0:1, :], lbl_ref[1:2, :]
    m = jnp.maximum(l0, l1)
    e0, e1 = jnp.exp(l0 - m), jnp.exp(l1 - m)
    return e0 / (e0 + e1)


def _heads(t):
    return [t[:, 128 * h:128 * (h + 1)] for h in range(t.shape[1] // 128)]


def _per_head(fn, *wide):
    return jnp.concatenate([fn(*parts) for parts in zip(*[_heads(t) for t in wide])], axis=1)


def _hgrn_sub(fl, qh, vv, lb, tril_b):
    w = fl.shape[1]
    sg = _sigmoid(fl)
    f = lb + (1.0 - lb) * sg
    k = 1.0 - f
    gc = _chunk_sum(tril_b, jnp.log(f))
    gl_c = [t[HGRN_CHUNK - 1:HGRN_CHUNK, :] for t in _chunks(gc)]
    gl = jnp.concatenate([jnp.broadcast_to(g, (HGRN_CHUNK, w)) for g in gl_c], axis=0)
    sq = _sigmoid(qh)
    eg = jnp.exp(gc)
    eng = jnp.exp(-gc)
    elg = jnp.exp(gl - gc)
    qd = qh * sq * eg
    ki = k * eng
    ke = k * elg
    qd_b, ki_b, ke_b, v_b = (t.astype(BF16) for t in (qd, ki, ke, vv))
    am_b = [jnp.where(tril_b > 0, _dot_nt(qq, kk), 0.0).astype(BF16) for qq, kk in zip(_heads(qd_b), _heads(ki_b))]
    return dict(sg=sg, f=f, sq=sq, eg=eg, eng=eng, elg=elg, qd=qd, ki=ki, ke=ke, egl=[jnp.exp(g) for g in gl_c],
                am_b=am_b, qd_b=qd_b, ki_b=ki_b, ke_b=ke_b, v_b=v_b)


def _hgrn_out(q, st_b):
    outs = []
    for h, (qd_h, v_h) in enumerate(zip(_heads(q["qd_b"]), _heads(q["v_b"]))):
        inter = [_dot_nt(qc, s) for qc, s in zip(_chunks(qd_h), st_b[h])]
        outs.append(_dot(q["am_b"][h], v_h) + jnp.concatenate(inter, axis=0))
    return jnp.concatenate(outs, axis=1)


def _head_rms(o):
    return _per_head(lambda t: jnp.broadcast_to(
        lax.rsqrt(jnp.mean(t * t, axis=-1, keepdims=True) + RMS_EPS), t.shape), o)


def _hgrn_fwd(name, hg, lb_logits, norm_g, cfg):
    T = hg.shape[0]
    sub_rows, hps, th = min(cfg[0], T), cfg[1], min(cfg[2], T)
    cps = sub_rows // HGRN_CHUNK
    nc = th // HGRN_CHUNK
    tri = _chunk_tri(sub_rows)

    def body(fl_ref, qh_ref, v_ref, og_ref, lbl_ref, ng_ref, tri_ref, y_ref, ss_ref, st):
        @pl.when(pl.program_id(1) == 0)
        def _():
            st[...] = jnp.zeros_like(st)

        lbs = _lower_bound(lbl_ref)
        ng = jnp.tile(ng_ref[...], (1, hps))

        def sub(si, carry):
            rows = pl.ds(pl.multiple_of(si * sub_rows, sub_rows), sub_rows)
            q = _hgrn_sub(fl_ref[rows, :], qh_ref[rows, :], v_ref[rows, :], lbs, tri_ref[...])
            v_hc = [_chunks(t) for t in _heads(q["v_b"])]
            k_hc = [_chunks(t) for t in _heads(q["ke_b"])]
            s = [st[h] for h in range(hps)]
            st_b = [[] for _ in range(hps)]
            for c in range(cps):
                egl = _heads(q["egl"][c])
                for h in range(hps):
                    ss_ref[h, si * cps + c] = s[h]
                    st_b[h].append(s[h].astype(BF16))
                    s[h] = s[h] * egl[h] + _dot_tn(v_hc[h][c], k_hc[h][c])
            for h in range(hps):
                st[h] = s[h]
            o = _hgrn_out(q, st_b)
            og = og_ref[rows, :]
            y_ref[rows, :] = (o * _head_rms(o) * ng * (og * _sigmoid(og))).astype(BF16)
            return carry

        lax.fori_loop(0, th // sub_rows, sub, 0)

    wd = 128 * hps
    col = lambda j: pl.BlockSpec((th, wd), lambda h, t: (t, (8 // hps) * j + h))
    return pl.pallas_call(
        body, name=name, grid=(HGRN_HEADS // hps, T // th),
        in_specs=[col(0), col(1), col(2), col(3), pl.BlockSpec((2, wd), lambda h, t: (0, h)),
                  pl.BlockSpec((1, 128), lambda h, t: (0, 0)), pl.BlockSpec((sub_rows, sub_rows), lambda h, t: (0, 0))],
        out_specs=[pl.BlockSpec((th, wd), lambda h, t: (t, h)),
                   pl.BlockSpec((hps, nc, 128, 128), lambda h, t: (h, t, 0, 0))],
        out_shape=[jax.ShapeDtypeStruct((T, D), BF16),
                   jax.ShapeDtypeStruct((HGRN_HEADS, T // HGRN_CHUNK, 128, 128), F32)],
        scratch_shapes=[pltpu.VMEM((hps, 128, 128), F32)],
        compiler_params=_params("parallel", "arbitrary"),
    )(hg, hg, hg, hg, lb_logits, norm_g, tri)


def _hgrn_bwd(name, hg, lb_logits, norm_g, sstart, dyh, cfg, carry=None):
    T = hg.shape[0]
    sub_rows, hps, th = min(cfg[0], T), cfg[1], min(cfg[2], T)
    cps = sub_rows // HGRN_CHUNK
    nc = th // HGRN_CHUNK
    nt = T // th
    wd = 128 * hps
    tri = _chunk_tri(sub_rows)
    tri_t = tri.T

    def body(fl_ref, qh_ref, v_ref, og_ref, lbl_ref, ng_ref, tri_ref, trit_ref, ss_ref, dy_ref,
             dfl_ref, dqh_ref, dv_ref, dog_ref, dlb_ref, dng_ref, dst):
        @pl.when(pl.program_id(1) == 0)
        def _():
            dst[...] = jnp.zeros_like(dst)
            dlb_ref[...] = jnp.zeros_like(dlb_ref)
            dng_ref[...] = jnp.zeros_like(dng_ref)

        lb = _lower_bound(lbl_ref)
        ng = jnp.tile(ng_ref[...], (1, hps))
        last = lax.broadcasted_iota(jnp.int32, (HGRN_CHUNK, wd), 0) == HGRN_CHUNK - 1
        nsub = th // sub_rows
        cat0 = lambda parts: jnp.concatenate(parts, axis=0)

        def sub(step, carry):
            si = nsub - 1 - step
            rows = pl.ds(pl.multiple_of(si * sub_rows, sub_rows), sub_rows)
            qh, og, dy = qh_ref[rows, :], og_ref[rows, :], dy_ref[rows, :]
            tril_b = tri_ref[...]
            q = _hgrn_sub(fl_ref[rows, :], qh, v_ref[rows, :], lb, tril_b)
            s_in = [[ss_ref[h, si * cps + c] for c in range(cps)] for h in range(hps)]
            st_b = [[s.astype(BF16) for s in row] for row in s_in]
            o = _hgrn_out(q, st_b)
            rr = _head_rms(o)
            oh = o * rr
            sog = _sigmoid(og)
            dog_ref[rows, :] = (dy * oh * ng * (sog * (1.0 + og * (1.0 - sog)))).astype(BF16)
            don = dy * (og * sog)
            dng_w = jnp.sum(don * oh, axis=0, keepdims=True)
            dd = don * ng
            mean_h = _per_head(lambda t: jnp.broadcast_to(jnp.mean(t, axis=-1, keepdims=True), t.shape), dd * oh)
            do_b = (rr * (dd - oh * mean_h)).astype(BF16)
            do_h, qd_h, ki_h, ke_h, v_h = (_heads(t) for t in (do_b, q["qd_b"], q["ki_b"], q["ke_b"], q["v_b"]))
            da_b = [jnp.where(tril_b > 0, _dot_nt(do_h[h], v_h[h]), 0.0).astype(BF16) for h in range(hps)]
            do_c, qd_c, ke_c, v_c = ([_chunks(t) for t in hs] for hs in (do_h, qd_h, ke_h, v_h))
            dsp = [[None] * cps for _ in range(hps)]
            dgl_dec = [[None] * cps for _ in range(hps)]
            d = [dst[h] for h in range(hps)]
            for c in reversed(range(cps)):
                egl = _heads(q["egl"][c])
                for h in range(hps):
                    dsp[h][c] = d[h]
                    dgl_dec[h][c] = jnp.sum(d[h] * s_in[h][c], axis=0, keepdims=True) * egl[h]
                    d[h] = d[h] * egl[h] + _dot_tn(do_c[h][c], qd_c[h][c])
            for h in range(hps):
                dst[h] = d[h]
                dng_ref[h] += dng_w[:, 128 * h:128 * (h + 1)]
            dsp_b = [[t.astype(BF16) for t in row] for row in dsp]
            dv_ref[rows, :] = jnp.concatenate(
                [_dot_tn(q["am_b"][h], do_h[h]) + cat0([_dot_nt(ke_c[h][c], dsp_b[h][c]) for c in range(cps)])
                 for h in range(hps)], axis=1).astype(BF16)
            dqd = jnp.concatenate(
                [_dot(da_b[h], ki_h[h]) + cat0([_dot(do_c[h][c], st_b[h][c]) for c in range(cps)])
                 for h in range(hps)], axis=1)
            dki = jnp.concatenate([_dot_tn(da_b[h], qd_h[h]) for h in range(hps)], axis=1)
            dke = jnp.concatenate([cat0([_dot(v_c[h][c], dsp_b[h][c]) for c in range(cps)]) for h in range(hps)], axis=1)
            dkk = dke * q["ke"]
            dgl = [jnp.sum(t, axis=0, keepdims=True) + jnp.concatenate([dgl_dec[h][c] for h in range(hps)], axis=1)
                   for c, t in enumerate(_chunks(dkk))]
            dgc = dqd * q["qd"] - dki * q["ki"] - dkk + cat0([jnp.where(last, g, 0.0) for g in dgl])
            dlf = _chunk_sum(trit_ref[...], dgc)
            df = dlf / q["f"] - (dki * q["eng"] + dke * q["elg"])
            sg = q["sg"]
            dlb_ref[...] += jnp.sum(df * (1.0 - sg), axis=0, keepdims=True)
            dfl_ref[rows, :] = (df * (1.0 - lb) * sg * (1.0 - sg)).astype(BF16)
            sq = q["sq"]
            dqh_ref[rows, :] = (dqd * q["eg"] * (sq * (1.0 + qh * (1.0 - sq)))).astype(BF16)
            return carry

        lax.fori_loop(0, nsub, sub, 0)
    col = lambda j: pl.BlockSpec((th, wd), lambda h, t: (nt - 1 - t, (8 // hps) * j + h))
    out = pl.BlockSpec((th, wd), lambda h, t: (nt - 1 - t, h))
    tri_spec = pl.BlockSpec((sub_rows, sub_rows), lambda h, t: (0, 0))
    return _call(
        body, name=name, grid=(HGRN_HEADS // hps, nt),
        in_specs=[col(0), col(1), col(2), col(3), pl.BlockSpec((2, wd), lambda h, t: (0, h)),
                  pl.BlockSpec((1, 128), lambda h, t: (0, 0)), tri_spec, tri_spec,
                  pl.BlockSpec((hps, nc, 128, 128), lambda h, t: (h, nt - 1 - t, 0, 0)), out],
        out_specs=[out, out, out, out, pl.BlockSpec((1, wd), lambda h, t: (0, h)),
                   pl.BlockSpec((hps, 1, 128), lambda h, t: (h, 0, 0))],
        out_shape=[jax.ShapeDtypeStruct((T, D), BF16)] * 4 + [jax.ShapeDtypeStruct((1, D), F32),
                                                              jax.ShapeDtypeStruct((HGRN_HEADS, 1, 128), F32)],
        scratch_shapes=[pltpu.VMEM((hps, 128, 128), F32)],
        ins=[hg, hg, hg, hg, lb_logits, norm_g, tri, tri_t, sstart, dyh], carry=carry)


def _mix_fwd(name, ya, yh, gates, xh1, g1, b1, wall, tm):
    T = ya.shape[0]

    def body(ya_ref, yh_ref, ga_ref, gh_ref, xh_ref, g_ref, b_ref, wall_ref,
             xo_ref, r_ref, pa_ref, ph_ref, mg_ref, wpa, wph, wo, sem):
        @pl.when(pl.program_id(0) == 0)
        def _():
            _load_rows(wall_ref, [(wpa, RIN, RP), (wph, RIN + RP, RP), (wo, RIN + 2 * RP, RP)], sem)

        parts = [slice(p, min(p + PART_ROWS, tm)) for p in range(0, tm, PART_ROWS)]
        pas = [_dot(ya_ref[rs, :], wpa[...]).astype(BF16) for rs in parts]
        phs = [_dot(yh_ref[rs, :], wph[...]).astype(BF16) for rs in parts]
        mgs = []
        for rs, pa, ph in zip(parts, pas, phs):
            pa_ref[rs, :] = pa
            ph_ref[rs, :] = ph
            mgs.append((_sigmoid(ga_ref[rs, :]) * pa.astype(F32) + _sigmoid(gh_ref[rs, :]) * ph.astype(F32)).astype(BF16))
            mg_ref[rs, :] = mgs[-1]
        mixes = [_dot(mg, wo[...]) for mg in mgs]
        for rs, mix in zip(parts, mixes):
            y1 = xh_ref[rs, :] * g_ref[...] + b_ref[...]
            xh2, r = _ln_fwd(ALPHA * y1 + mix)
            xo_ref[rs, :] = xh2
            r_ref[rs, :] = jnp.broadcast_to(r, (xh2.shape[0], 128))

    row = lambda w: pl.BlockSpec((tm, w), lambda i: (i, 0))
    vec = pl.BlockSpec((1, D), lambda i: (0, 0))
    return pl.pallas_call(
        body, name=name, grid=(T // tm,),
        in_specs=[row(D), row(D), pl.BlockSpec((tm, D), lambda i: (i, 0)), pl.BlockSpec((tm, D), lambda i: (i, 1)),
                  row(D), vec, vec, HBM_SPEC],
        out_specs=[row(D), row(128), row(D), row(D), row(D)],
        out_shape=[jax.ShapeDtypeStruct((T, D), F32), jax.ShapeDtypeStruct((T, 128), F32)]
        + [jax.ShapeDtypeStruct((T, D), BF16)] * 3,
        scratch_shapes=[pltpu.VMEM((D, D), BF16)] * 3 + [pltpu.SemaphoreType.DMA((3 * NDEV,))],
        compiler_params=_params("arbitrary"),
    )(ya, yh, gates, gates, xh1, g1, b1, wall)


def _mix_bwd(name, dy2, xh2, r2, g2, gates, pa, ph, wall, tm, carry=None):
    T = dy2.shape[0]

    def body(dy_ref, xh_ref, r_ref, g_ref, ga_ref, gh_ref, pa_ref, ph_ref, wall_ref,
             dz_ref, dmix_ref, dpa_ref, dph_ref, dga_ref, dgh_ref, dya_ref, dyh_ref, dg_ref, db_ref,
             wpa, wph, wo, sem):
        @pl.when(pl.program_id(0) == 0)
        def _():
            _load_rows(wall_ref, [(wpa, RIN, RP), (wph, RIN + RP, RP), (wo, RIN + 2 * RP, RP)], sem)
            dg_ref[...] = jnp.zeros_like(dg_ref)
            db_ref[...] = jnp.zeros_like(db_ref)

        parts = [slice(p, min(p + PART_ROWS, tm)) for p in range(0, tm, PART_ROWS)]
        dmix = []
        for rs in parts:
            dz, dgp, dbp = _ln_bwd(dy_ref[rs, :], xh_ref[rs, :], r_ref[rs, :1], g_ref[...])
            dg_ref[...] += dgp
            db_ref[...] += dbp
            dz_ref[rs, :] = dz
            dmix.append(dz.astype(BF16))
            dmix_ref[rs, :] = dmix[-1]
        dmgs = [_dot_nt(d, wo[...]) for d in dmix]
        dpas, dphs = [], []
        for rs, dmg in zip(parts, dmgs):
            sa, sh = _sigmoid(ga_ref[rs, :]), _sigmoid(gh_ref[rs, :])
            dga_ref[rs, :] = (dmg * pa_ref[rs, :].astype(F32) * sa * (1.0 - sa)).astype(BF16)
            dgh_ref[rs, :] = (dmg * ph_ref[rs, :].astype(F32) * sh * (1.0 - sh)).astype(BF16)
            dpas.append((dmg * sa).astype(BF16))
            dphs.append((dmg * sh).astype(BF16))
            dpa_ref[rs, :] = dpas[-1]
            dph_ref[rs, :] = dphs[-1]
        for rs, dpa, dph in zip(parts, dpas, dphs):
            dya_ref[rs, :] = _dot_nt(dpa, wpa[...]).astype(BF16)
            dyh_ref[rs, :] = _dot_nt(dph, wph[...])

    row = lambda w: pl.BlockSpec((tm, w), lambda i: (i, 0))
    vec = pl.BlockSpec((1, D), lambda i: (0, 0))
    return _call(
        body, name=name, grid=(T // tm,),
        in_specs=[row(D), row(D), row(128), vec, pl.BlockSpec((tm, D), lambda i: (i, 0)),
                  pl.BlockSpec((tm, D), lambda i: (i, 1)), row(D), row(D), HBM_SPEC],
        out_specs=[row(D)] * 8 + [vec, vec],
        out_shape=[jax.ShapeDtypeStruct((T, D), F32)] + [jax.ShapeDtypeStruct((T, D), BF16)] * 6
        + [jax.ShapeDtypeStruct((T, D), F32)] + [jax.ShapeDtypeStruct((1, D), F32)] * 2,
        scratch_shapes=[pltpu.VMEM((D, D), BF16)] * 3 + [pltpu.SemaphoreType.DMA((3 * NDEV,))],
        ins=[dy2, xh2, r2, g2, gates, gates, pa, ph, wall], carry=carry)


def _adam(w, g, m, v):
    m = ADAM_B1 * m + (1.0 - ADAM_B1) * g
    v = ADAM_B2 * v + (1.0 - ADAM_B2) * (g * g)
    m_hat = m / (1.0 - ADAM_B1 ** ADAM_STEP)
    v_hat = v / (1.0 - ADAM_B2 ** ADAM_STEP)
    delta = -ADAM_LR * (m_hat / (jnp.sqrt(v_hat) + ADAM_EPS) + ADAM_WD * w)
    return delta, m, v


def _grad_steps(name, items, cb, carry=None):
    n = len(items)
    pieces = [len(it[0]) for it in items]

    def body(*refs):
        it = iter(refs[:sum(pieces) + 3 * n])
        outs = refs[sum(pieces) + 3 * n:]
        for i in range(n):
            accs = []
            for _ in range(pieces[i]):
                r_ref = next(it)
                acc = r_ref[0].astype(F32)
                for k in range(1, NDEV):
                    acc = acc + r_ref[k].astype(F32)
                accs.append(acc)
            acc = accs[0] if len(accs) == 1 else jnp.concatenate(accs, axis=0)
            w_ref, m_ref, v_ref = next(it), next(it), next(it)
            g_ref, d_ref, mo_ref, vo_ref = outs[4 * i:4 * i + 4]
            g_ref[...] = acc
            d, mm, vv = _adam(w_ref[...], acc, m_ref[...], v_ref[...])
            d_ref[...] = d
            mo_ref[...] = mm
            vo_ref[...] = vv

    in_specs, out_specs, out_shape, ins = [], [], [], []
    for recv, w, m, v in items:
        blk = pl.BlockSpec((w.shape[0], cb), lambda j: (0, j))
        in_specs += [pl.BlockSpec((NDEV, rp.shape[1], cb), lambda j: (0, 0, j)) for rp in recv] + [blk, blk, blk]
        out_specs += [blk] * 4
        out_shape += [jax.ShapeDtypeStruct(w.shape, F32)] * 4
        ins += list(recv) + [w, m, v]
    outs, ex = _call(body, name=name, grid=(D // cb,), in_specs=in_specs, out_specs=out_specs, out_shape=out_shape,
                     scratch_shapes=[], ins=ins, carry=carry)
    return [tuple(outs[4 * i:4 * i + 4]) for i in range(n)], ex


_SMALL = [("ln1_g", D), ("ln1_b", D), ("ln2_g", D), ("ln2_b", D), ("ln3_g", D), ("ln3_b", D),
          ("b_in", DIN), ("lb", D), ("attn_sinks", 128), ("hgrn_norm_g", 128), ("loss", 128)]
_SMALL_OFF = {}
_o = 0
for _n, _w in _SMALL:
    _SMALL_OFF[_n] = (_o, _w)
    _o += _w
PACK = _o


def _small_reduce(name, packed):
    def body(p_ref, o_ref, buf, send_sems, recv_sems):
        x, y, c = lax.axis_index("x"), lax.axis_index("y"), lax.axis_index("c")
        me = 4 * x + 2 * y + c
        buf[me] = p_ref[...]
        copies = []
        for j in range(1, NDEV):
            px, py, pc = x ^ (j >> 2), y ^ ((j >> 1) & 1), c ^ (j & 1)
            cp = pltpu.make_async_remote_copy(
                src_ref=p_ref, dst_ref=buf.at[me], send_sem=send_sems.at[j - 1], recv_sem=recv_sems.at[j - 1],
                device_id=(px, py, pc), device_id_type=MESH)
            cp.start()
            copies.append(cp)
        for cp in copies:
            cp.wait()
        tot = buf[0]
        for k in range(1, NDEV):
            tot = tot + buf[k]
        o_ref[...] = tot

    return pl.pallas_call(
        body, name=name, out_shape=jax.ShapeDtypeStruct((1, PACK), F32),
        scratch_shapes=[pltpu.VMEM((NDEV, 1, PACK), F32), pltpu.SemaphoreType.DMA((NDEV - 1,)),
                        pltpu.SemaphoreType.DMA((NDEV - 1,))],
        compiler_params=pltpu.CompilerParams(vmem_limit_bytes=VMEM_LIMIT),
    )(packed)


def _small_step(name, total, small_w, small_m, small_v):
    names = ["ln1_g", "ln1_b", "ln2_g", "ln2_b", "ln3_g", "ln3_b", "b_in", "attn_sinks", "hgrn_lb_logits", "hgrn_norm_g"]
    np_ = len(names)

    def body(*refs):
        t_ref = refs[0]
        w_refs = refs[1:1 + np_]
        m_refs = refs[1 + np_:1 + 2 * np_]
        v_refs = refs[1 + 2 * np_:1 + 3 * np_]
        loss_ref, outs = refs[1 + 3 * np_], refs[2 + 3 * np_:]

        def part(n):
            o, w = _SMALL_OFF[n]
            return t_ref[:, o:o + w]

        loss_ref[...] = part("loss")
        for i, n in enumerate(names):
            w = w_refs[i][...]
            if n == "hgrn_lb_logits":
                m_ = jnp.maximum(w[0:1, :], w[1:2, :])
                e0, e1 = jnp.exp(w[0:1, :] - m_), jnp.exp(w[1:2, :] - m_)
                p0 = e0 / (e0 + e1)
                t = p0 * (1.0 - p0) * part("lb")
                g = jnp.concatenate([t, -t], axis=0)
            elif n == "attn_sinks":
                g = part(n)[:, :N_Q_HEADS]
            else:
                g = part(n)
            d, mm, vv = _adam(w, g, m_refs[i][...], v_refs[i][...])
            outs[4 * i][...] = g
            outs[4 * i + 1][...] = d
            outs[4 * i + 2][...] = mm
            outs[4 * i + 3][...] = vv

    out_shape = [jax.ShapeDtypeStruct((1, 128), F32)]
    for n in names:
        out_shape += [jax.ShapeDtypeStruct(small_w[n].shape, F32)] * 4
    return pl.pallas_call(
        body, name=name, out_shape=out_shape,
        compiler_params=pltpu.CompilerParams(vmem_limit_bytes=VMEM_LIMIT),
    )(total, *[small_w[n] for n in names], *[small_m[n] for n in names], *[small_v[n] for n in names]), names


def _tile(T, pref):
    return min(T, pref)


def kernel(x, ln1_g, ln1_b, ffn1_w1, ffn1_w3, ffn1_w2, ln2_g, ln2_b, w_in, b_in, attn_sinks, hgrn_lb_logits, hgrn_norm_g, w_proj_attn, w_proj_hgrn, w_out, ln3_g, ln3_b, ffn2_w1, ffn2_w3, ffn2_w2, loss_target, m_ln1_g, m_ln1_b, m_ffn1_w1, m_ffn1_w3, m_ffn1_w2, m_ln2_g, m_ln2_b, m_w_in, m_b_in, m_attn_sinks, m_hgrn_lb_logits, m_hgrn_norm_g, m_w_proj_attn, m_w_proj_hgrn, m_w_out, m_ln3_g, m_ln3_b, m_ffn2_w1, m_ffn2_w3, m_ffn2_w2, v_ln1_g, v_ln1_b, v_ffn1_w1, v_ffn1_w3, v_ffn1_w2, v_ln2_g, v_ln2_b, v_w_in, v_b_in, v_attn_sinks, v_hgrn_lb_logits, v_hgrn_norm_g, v_w_proj_attn, v_w_proj_hgrn, v_w_out, v_ln3_g, v_ln3_b, v_ffn2_w1, v_ffn2_w3, v_ffn2_w2):
    T = x.shape[1]
    xs = x[0]
    tgt = loss_target[0]
    tm = _tile(T, 256)
    tm2 = _tile(T, 512)
    tk = _tile(T, 2048)

    t_bf = lambda w: w[0].T.astype(BF16)
    n_bf = lambda w: w[0].astype(BF16)
    ffn_shard = lambda w1, w3, w2: jnp.concatenate([t_bf(w1), t_bf(w3), n_bf(w2)], axis=0)
    mix_shard = jnp.concatenate([t_bf(w_in), n_bf(w_proj_attn), n_bf(w_proj_hgrn), n_bf(w_out)], axis=0)
    (ffn1_all,) = _exchange_call("gather_ffn1", _gather_exchange(ffn_shard(ffn1_w1, ffn1_w3, ffn1_w2)))
    ffn_offs = (0, RF, 2 * RF)
    ropes = _rope_tables(T)

    (xh1, r1, a1, b1, xb0), (mix_all,) = _ffn_fwd("ffn1_fwd", xs, None, ffn1_all, ffn_offs, tm2,
                                                  carry=_gather_exchange(mix_shard))
    (qkv, kt, vt, hg, gates, y1b), (ffn2_all,) = _inproj_fwd(
        "inproj_fwd", xh1, ln1_g, ln1_b, mix_all, b_in, ropes, tm2,
        carry=_gather_exchange(ffn_shard(ffn2_w1, ffn2_w3, ffn2_w2)))
    ya = _attn_fwd("attn_fwd", qkv, vt, attn_sinks)
    yh, sstart = _hgrn_fwd("hgrn_fwd", hg, hgrn_lb_logits, hgrn_norm_g, HG_FWD)
    xh2, r2, pa, ph, merged = _mix_fwd("mix_fwd", ya, yh, gates, xh1, ln1_g, ln1_b, mix_all, tm2)
    (xh3, r3, a2, b2, y2b, loss_part), _ = _ffn_fwd("ffn2_fwd", xh2, (ln2_g, ln2_b), ffn2_all, ffn_offs, tm2,
                                                    loss=(ln3_g, ln3_b, tgt))

    (dy2, dab2, u2, df2, dg3, db3), _ = _ffn_bwd("ffn2_bwd", ("loss", ln3_b, tgt), xh3, r3, ln3_g, a2, b2, ffn2_all,
                                                 ffn_offs, tm)
    recv = {}
    g_ffn2_13, _ = _wgrad("wgrad_ffn2_w13", dab2, y2b, DFF // 2, tk)
    g_ffn2_2, _ = _wgrad("wgrad_ffn2_w2", u2, df2, DFF // 2, tk)
    (dz2, dmix, dpa, dph, dga, dgh, dya, dyh, dg2, db2), (recv["ffn2_w1"],) = _mix_bwd(
        "mix_bwd", dy2, xh2, r2, ln2_g, gates, pa, ph, mix_all, tm2, carry=_grad_exchange([g_ffn2_13], [0], [RF]))
    g_wo, _ = _wgrad("wgrad_w_out", merged, dmix, D, tk)
    g_pa, _ = _wgrad("wgrad_w_pa", ya, dpa, D, tk)
    g_ph, _ = _wgrad("wgrad_w_ph", yh, dph, D, tk)
    (dq, dk, dv, dsink), (recv["ffn2_w3"],) = _attn_bwd(
        "attn_bwd", qkv, kt, attn_sinks, dya, carry=_grad_exchange([g_ffn2_13], [DFF], [RF]))
    (dfl, dqh, dih, dog, dlb, dng), (recv["ffn2_w2"], recv["w_proj_attn"], recv["w_proj_hgrn"], recv["w_out"]) = \
        _hgrn_bwd("hgrn_bwd", hg, hgrn_lb_logits, hgrn_norm_g, sstart, dyh, HG_BWD,
                  carry=_grad_exchange([g_ffn2_2, g_pa, g_ph, g_wo], [0, 0, 0, 0], [RF, RP, RP, RP]))
    dy1, dproj, dbin = _inproj_bwd("inproj_bwd", dq, dk, dv, (dfl, dqh, dih, dog), (dga, dgh), dz2, mix_all, ropes, tm)
    g_win, _ = _wgrad("wgrad_w_in", dproj, y1b, DIN // 4, tk)
    win_rows = (544, 320, 96)
    win_base = (0, 544, 864)
    (dz1, dab1, u1, df1, dg1, db1), (rw0,) = _ffn_bwd(
        "ffn1_bwd", ("dy", dy1), xh1, r1, ln1_g, a1, b1, ffn1_all, ffn_offs, tm, with_dx=False,
        carry=_grad_exchange([g_win], win_base[:1], win_rows[:1], [RIN]))
    g_ffn1_2, (rw1,) = _wgrad("wgrad_ffn1_w2", u1, df1, DFF // 2, tk,
                              carry=_grad_exchange([g_win], win_base[1:2], win_rows[1:2], [RIN]))
    g_ffn1_1, (recv["ffn1_w2"],) = _wgrad("wgrad_ffn1_w1", dab1, xb0, DFF // 2, tk, 0, DFF,
                                          carry=_grad_exchange([g_ffn1_2], [0], [RF]))
    g_ffn1_3, (recv["ffn1_w1"],) = _wgrad("wgrad_ffn1_w3", dab1, xb0, DFF // 2, tk, DFF, DFF,
                                          carry=_grad_exchange([g_ffn1_1], [0], [RF]))
    gx, (recv["ffn1_w3"], rw2) = _ffn_dx(
        "ffn1_dx", dab1, dz1, ffn1_all, ffn_offs, tm2,
        carry=_grad_exchange([g_ffn1_3, g_win], [0, win_base[2]], [RF, win_rows[2]], [RF, RIN]))
    recv["w_in"] = [rw0, rw1, rw2]

    big = [("ffn1_w1", ffn1_w1, m_ffn1_w1, v_ffn1_w1, True), ("ffn1_w3", ffn1_w3, m_ffn1_w3, v_ffn1_w3, True),
           ("ffn1_w2", ffn1_w2, m_ffn1_w2, v_ffn1_w2, False), ("w_in", w_in, m_w_in, v_w_in, True),
           ("w_proj_attn", w_proj_attn, m_w_proj_attn, v_w_proj_attn, False),
           ("w_proj_hgrn", w_proj_hgrn, m_w_proj_hgrn, v_w_proj_hgrn, False),
           ("w_out", w_out, m_w_out, v_w_out, False),
           ("ffn2_w1", ffn2_w1, m_ffn2_w1, v_ffn2_w1, True), ("ffn2_w3", ffn2_w3, m_ffn2_w3, v_ffn2_w3, True),
           ("ffn2_w2", ffn2_w2, m_ffn2_w2, v_ffn2_w2, False)]
    view = lambda t, transposed: t[0].T if transposed else t[0]
    back = lambda t, transposed: t.T[None] if transposed else t[None]
    slots = lambda n: recv[n] if isinstance(recv[n], list) else [recv[n]]
    stepped, _ = _grad_steps("steps", [(slots(n), view(w, tr), view(m, tr), view(v, tr)) for n, w, m, v, tr in big], 128)
    res = {n: tuple(back(t, tr) for t in outs) for (n, _, _, _, tr), outs in zip(big, stepped)}

    parts = {"ln1_g": dg1, "ln1_b": db1, "ln2_g": dg2, "ln2_b": db2, "ln3_g": dg3, "ln3_b": db3, "b_in": dbin,
             "lb": dlb, "attn_sinks": dsink, "hgrn_norm_g": jnp.sum(dng, axis=0), "loss": loss_part[0:1, :]}
    packed = jnp.concatenate([parts[n] for n, _ in _SMALL], axis=1)
    small_w = dict(ln1_g=ln1_g, ln1_b=ln1_b, ln2_g=ln2_g, ln2_b=ln2_b, ln3_g=ln3_g, ln3_b=ln3_b, b_in=b_in,
                   attn_sinks=attn_sinks, hgrn_lb_logits=hgrn_lb_logits, hgrn_norm_g=hgrn_norm_g)
    small_m = dict(ln1_g=m_ln1_g, ln1_b=m_ln1_b, ln2_g=m_ln2_g, ln2_b=m_ln2_b, ln3_g=m_ln3_g, ln3_b=m_ln3_b,
                   b_in=m_b_in, attn_sinks=m_attn_sinks, hgrn_lb_logits=m_hgrn_lb_logits, hgrn_norm_g=m_hgrn_norm_g)
    small_v = dict(ln1_g=v_ln1_g, ln1_b=v_ln1_b, ln2_g=v_ln2_g, ln2_b=v_ln2_b, ln3_g=v_ln3_g, ln3_b=v_ln3_b,
                   b_in=v_b_in, attn_sinks=v_attn_sinks, hgrn_lb_logits=v_hgrn_lb_logits, hgrn_norm_g=v_hgrn_norm_g)
    outs, names = _small_step("small_step", _small_reduce("small_reduce", packed), small_w, small_m, small_v)
    loss = outs[0][0, 0]
    for i, n in enumerate(names):
        res[n] = tuple(outs[1 + 4 * i:5 + 4 * i])

    order = ["ln1_g", "ln1_b", "ffn1_w1", "ffn1_w3", "ffn1_w2", "ln2_g", "ln2_b", "w_in", "b_in", "attn_sinks",
             "hgrn_lb_logits", "hgrn_norm_g", "w_proj_attn", "w_proj_hgrn", "w_out", "ln3_g", "ln3_b",
             "ffn2_w1", "ffn2_w3", "ffn2_w2"]
    return (loss, gx[None], *[res[n][0] for n in order], *[res[n][1] for n in order],
            *[res[n][2] for n in order], *[res[n][3] for n in order])
```

```python
import jax
import jax.numpy as jnp
from jax import lax
from jax.experimental import pallas as pl
from jax.experimental.pallas import tpu as pltpu

F32 = jnp.float32
BF16 = jnp.bfloat16

NDEV = 8
D = 1024
DFF = 2816
RF = DFF // NDEV
DIN = 7680
RIN = DIN // NDEV
RP = D // NDEV
N_Q_HEADS = 16
N_KV_HEADS = 4
HEAD_DIM = 64
ATTN_BLOCK = 128
ROPE_THETA = 500000.0
ROPE_DIM = HEAD_DIM // 4
HGRN_HEADS = 8
HGRN_DK = 128
HGRN_CHUNK = 64
ALPHA = 2.0 ** 0.25
LN_EPS = 1e-5
RMS_EPS = 1e-6
NEG_INF = -1e30
ADAM_LR = 0.001
ADAM_B1 = 0.9
ADAM_B2 = 0.999
ADAM_EPS = 1e-08
ADAM_WD = 0.01
ADAM_STEP = 10

QKV_W = 1536
HG_W = 4096
GATE_W = 2048
VMEM_LIMIT = 60 * 2 ** 20
PART_ROWS = 256
MESH = pl.DeviceIdType.MESH
HBM_SPEC = pl.BlockSpec(memory_space=pltpu.HBM)


def _params(*sem):
    return pltpu.CompilerParams(dimension_semantics=sem, vmem_limit_bytes=VMEM_LIMIT)


def _dot(a, b):
    return jnp.dot(a, b, preferred_element_type=F32)


def _dot_nt(a, b):
    return lax.dot_general(a, b, (((1,), (1,)), ((), ())), preferred_element_type=F32)


def _dot_tn(a, b):
    return lax.dot_general(a, b, (((0,), (0,)), ((), ())), preferred_element_type=F32)


def _sigmoid(x):
    return 0.5 * jnp.tanh(0.5 * x) + 0.5


def _ln_fwd(z):
    mu = jnp.mean(z, axis=-1, keepdims=True)
    zc = z - mu
    var = jnp.mean(zc * zc, axis=-1, keepdims=True)
    r = lax.rsqrt(var + LN_EPS)
    return zc * r, r


def _ln_bwd(dy, xh, r, g):
    dxh = dy * g
    m1 = jnp.mean(dxh, axis=-1, keepdims=True)
    m2 = jnp.mean(dxh * xh, axis=-1, keepdims=True)
    dz = r * (dxh - m1 - xh * m2)
    return dz, jnp.sum(dy * xh, axis=0, keepdims=True), jnp.sum(dy, axis=0, keepdims=True)


def _load_rows(wall_ref, pieces, sem):
    copies = []
    for dst, off, r in pieces:
        for k in range(NDEV):
            c = pltpu.make_async_copy(wall_ref.at[k, pl.ds(off, r), :], dst.at[pl.ds(k * r, r), :], sem.at[len(copies)])
            c.start()
            copies.append(c)
    for c in copies:
        c.wait()


class _Exchange:
    def __init__(self, inputs, out_shape, scratch, begin, middle, end):
        self.inputs, self.out_shape, self.scratch = inputs, out_shape, scratch
        self.begin, self.middle, self.end = begin, middle, end


def _gather_exchange(shard):
    rows, cols = shard.shape

    def ops(ins, outs, scr):
        (x_ref,), (out_ref,), (send_sems, recv_sems, local_sem) = ins, outs, scr
        x, y, c = lax.axis_index("x"), lax.axis_index("y"), lax.axis_index("c")
        me, sibling = (x, y, c), (x, y, 1 - c)
        chips = [(1 - x, y), (x, 1 - y), (1 - x, 1 - y)]

        def slot(px, py, pc):
            return out_ref.at[4 * px + 2 * py + pc]

        def copy(k, block, to, src=None):
            return pltpu.make_async_remote_copy(
                src_ref=slot(*block) if src is None else src, dst_ref=slot(*block),
                send_sem=send_sems.at[k], recv_sem=recv_sems.at[k], device_id=to, device_id_type=MESH)

        mine = lambda: pltpu.make_async_copy(x_ref, slot(*me), local_sem)
        first = lambda: [copy(0, me, sibling, src=x_ref)] + [
            copy(1 + j, me, (*chip, c), src=x_ref) for j, chip in enumerate(chips)]
        passed = lambda: [copy(4 + j, (*chip, c), sibling) for j, chip in enumerate(chips)]
        return c, me, sibling, chips, copy, mine, first, passed

    def begin(*refs):
        _, _, _, _, _, mine, first, _ = ops(*refs)
        mine().start()
        for cp in first():
            cp.start()

    def middle(*refs):
        c, me, _, chips, copy, _, _, passed = ops(*refs)
        for (j, chip), fwd in zip(enumerate(chips), passed()):
            copy(1 + j, (*chip, c), me).wait_recv()
            fwd.start()

    def end(*refs):
        c, me, sibling, chips, copy, mine, first, passed = ops(*refs)
        copy(0, sibling, me).wait_recv()
        for j, chip in enumerate(chips):
            copy(4 + j, (*chip, 1 - c), me).wait_recv()
        for cp in first() + passed():
            cp.wait_send()
        mine().wait()

    return _Exchange([shard], [jax.ShapeDtypeStruct((NDEV, rows, cols), shard.dtype)],
                     [pltpu.SemaphoreType.DMA((7,)), pltpu.SemaphoreType.DMA((7,)), pltpu.SemaphoreType.DMA],
                     begin, middle, end)


def _grad_exchange(grads, bases, rows, strides=None):
    n = len(grads)
    strides = rows if strides is None else strides

    def copies(g_refs, out_refs, scr):
        send_sems, recv_sems, local_sems = scr
        x, y, c = lax.axis_index("x"), lax.axis_index("y"), lax.axis_index("c")
        me = 4 * x + 2 * y + c
        out = []
        for i in range(n):
            r = rows[i]
            src = lambda k: g_refs[i].at[pl.ds(pl.multiple_of(bases[i] + k * strides[i], 16), r), :]
            out.append(pltpu.make_async_copy(src(me), out_refs[i].at[me], local_sems.at[i]))
            for j in range(1, NDEV):
                px, py, pc = x ^ (j >> 2), y ^ ((j >> 1) & 1), c ^ (j & 1)
                out.append(pltpu.make_async_remote_copy(
                    src_ref=src(4 * px + 2 * py + pc), dst_ref=out_refs[i].at[me],
                    send_sem=send_sems.at[i, j - 1], recv_sem=recv_sems.at[i, j - 1],
                    device_id=(px, py, pc), device_id_type=MESH))
        return out

    def begin(*refs):
        for cp in copies(*refs):
            cp.start()

    def end(*refs):
        for cp in copies(*refs):
            cp.wait()

    return _Exchange(list(grads), [jax.ShapeDtypeStruct((NDEV, r, g.shape[1]), g.dtype) for g, r in zip(grads, rows)],
                     [pltpu.SemaphoreType.DMA((n, NDEV - 1)), pltpu.SemaphoreType.DMA((n, NDEV - 1)),
                      pltpu.SemaphoreType.DMA((n,))], begin, None, end)


def _exchange_call(name, ex):
    ni, no = len(ex.inputs), len(ex.out_shape)

    def body(*refs):
        parts = (refs[:ni], refs[ni:ni + no], refs[ni + no:])
        ex.begin(*parts)
        if ex.middle is not None:
            ex.middle(*parts)
        ex.end(*parts)

    return pl.pallas_call(body, name=name, out_shape=ex.out_shape, in_specs=[HBM_SPEC] * ni, out_specs=[HBM_SPEC] * no,
                          scratch_shapes=ex.scratch)(*ex.inputs)


def _call(body, *, name, grid, in_specs, out_specs, out_shape, scratch_shapes, ins, carry=None):
    sem = ("arbitrary",) * len(grid)
    if carry is None:
        outs = pl.pallas_call(body, name=name, grid=grid, in_specs=in_specs, out_specs=out_specs, out_shape=out_shape,
                              scratch_shapes=scratch_shapes, compiler_params=_params(*sem))(*ins)
        return outs, None
    n_in, n_out, n_scr = len(ins), len(out_shape), len(scratch_shapes)
    ci, co = len(carry.inputs), len(carry.out_shape)
    total = 1
    for g in grid:
        total *= g

    def wrapped(*refs):
        own_in, ex_in = refs[:n_in], refs[n_in:n_in + ci]
        o0 = n_in + ci
        own_out, ex_out = refs[o0:o0 + n_out], refs[o0 + n_out:o0 + n_out + co]
        s0 = o0 + n_out + co
        own_scr, ex_scr = refs[s0:s0 + n_scr], refs[s0 + n_scr:]
        step = pl.program_id(0)
        for d in range(1, len(grid)):
            step = step * grid[d] + pl.program_id(d)
        parts = (ex_in, ex_out, ex_scr)
        pl.when(step == 0)(lambda: carry.begin(*parts))
        body(*own_in, *own_out, *own_scr)
        if carry.middle is not None:
            pl.when(step == (3 * total) // 4)(lambda: carry.middle(*parts))
        pl.when(step == total - 1)(lambda: carry.end(*parts))

    outs = pl.pallas_call(
        wrapped, name=name, grid=grid, in_specs=list(in_specs) + [HBM_SPEC] * ci,
        out_specs=list(out_specs) + [HBM_SPEC] * co, out_shape=list(out_shape) + list(carry.out_shape),
        scratch_shapes=list(scratch_shapes) + list(carry.scratch), compiler_params=_params(*sem),
    )(*ins, *carry.inputs)
    return outs[:n_out], outs[n_out:]


def _ffn_fwd(name, xin, affine, wall, offs, tm, loss=None, carry=None):
    T = xin.shape[0]
    nt = T // tm

    def body(*refs):
        it = iter(refs)
        x_ref = next(it)
        if affine is not None:
            g_ref, b_ref = next(it), next(it)
        wall_ref = next(it)
        if loss is not None:
            go_ref, bo_ref, tgt_ref = next(it), next(it), next(it)
        xh_ref, r_ref, a_ref, b2_ref, yb_ref = (next(it) for _ in range(5))
        if loss is not None:
            loss_ref = next(it)
        w1, w3, w2, sem = (next(it) for _ in range(4))

        @pl.when(pl.program_id(0) == 0)
        def _():
            _load_rows(wall_ref, [(w1, offs[0], RF), (w3, offs[1], RF), (w2, offs[2], RF)], sem)
            if loss is not None:
                loss_ref[...] = jnp.zeros_like(loss_ref)

        parts = [slice(p, min(p + PART_ROWS, tm)) for p in range(0, tm, PART_ROWS)]
        ys, ybs = [], []
        for rs in parts:
            y = x_ref[rs, :]
            if affine is not None:
                y = y * g_ref[...] + b_ref[...]
            ys.append(y)
            ybs.append(y.astype(BF16))
            yb_ref[rs, :] = ybs[-1]
        ab = [(_dot_nt(yb, w1[...]).astype(BF16), _dot_nt(yb, w3[...]).astype(BF16)) for yb in ybs]
        us = []
        for rs, (a, b) in zip(parts, ab):
            a_ref[rs, :] = a
            b2_ref[rs, :] = b
            af, bf = a.astype(F32), b.astype(F32)
            us.append((af * _sigmoid(af) * bf).astype(BF16))
        fs = [_dot(u, w2[...]) for u in us]
        for rs, y, f in zip(parts, ys, fs):
            xh, r = _ln_fwd(ALPHA * y + 0.5 * f)
            xh_ref[rs, :] = xh
            r_ref[rs, :] = jnp.broadcast_to(r, (xh.shape[0], 128))
            if loss is not None:
                e = xh * go_ref[...] + bo_ref[...] - tgt_ref[rs, :]
                loss_ref[...] += jnp.sum(e * e) * (0.5 / D)

    row = lambda w: pl.BlockSpec((tm, w), lambda i: (i, 0))
    vec = pl.BlockSpec((1, D), lambda i: (0, 0))
    ins, in_specs = [xin], [row(D)]
    if affine is not None:
        ins += list(affine)
        in_specs += [vec, vec]
    ins.append(wall)
    in_specs.append(HBM_SPEC)
    if loss is not None:
        ins += list(loss)
        in_specs += [vec, vec, row(D)]
    out_shape = [jax.ShapeDtypeStruct((T, D), F32), jax.ShapeDtypeStruct((T, 128), F32),
                 jax.ShapeDtypeStruct((T, DFF), BF16), jax.ShapeDtypeStruct((T, DFF), BF16),
                 jax.ShapeDtypeStruct((T, D), BF16)]
    out_specs = [row(D), row(128), row(DFF), row(DFF), row(D)]
    if loss is not None:
        out_shape.append(jax.ShapeDtypeStruct((8, 128), F32))
        out_specs.append(pl.BlockSpec((8, 128), lambda i: (0, 0)))
    return _call(body, name=name, grid=(nt,), in_specs=in_specs, out_specs=out_specs, out_shape=out_shape,
                 scratch_shapes=[pltpu.VMEM((DFF, D), BF16)] * 3 + [pltpu.SemaphoreType.DMA((3 * NDEV,))],
                 ins=ins, carry=carry)


def _ffn_bwd(name, dy_src, xh, r, g, a, b, wall, offs, tm, with_dx=True, carry=None):
    T = xh.shape[0]
    nt = T // tm
    from_loss = dy_src[0] == "loss"

    def body(*refs):
        it = iter(refs)
        if from_loss:
            bo_ref, tgt_ref = next(it), next(it)
        else:
            dy_ref = next(it)
        xh_ref, r_ref, g_ref, a_ref, b_ref, wall_ref = (next(it) for _ in range(6))
        dyin_ref, dab_ref, u_ref, df_ref, dg_ref, db_ref = (next(it) for _ in range(6))
        if with_dx:
            w1, w3 = next(it), next(it)
        w2, sem = next(it), next(it)

        @pl.when(pl.program_id(0) == 0)
        def _():
            _load_rows(wall_ref, ([(w1, offs[0], RF), (w3, offs[1], RF)] if with_dx else []) + [(w2, offs[2], RF)], sem)
            dg_ref[...] = jnp.zeros_like(dg_ref)
            db_ref[...] = jnp.zeros_like(db_ref)

        parts = [slice(p, min(p + PART_ROWS, tm)) for p in range(0, tm, PART_ROWS)]
        gv = g_ref[...]
        dz, df = [], []
        for rs in parts:
            xhv = xh_ref[rs, :]
            if from_loss:
                dy = (xhv * gv + bo_ref[...] - tgt_ref[rs, :]) * (1.0 / D)
            else:
                dy = dy_ref[rs, :]
            dzp, dgp, dbp = _ln_bwd(dy, xhv, r_ref[rs, :1], gv)
            dg_ref[...] += dgp
            db_ref[...] += dbp
            dz.append(dzp)
            df.append((0.5 * dzp).astype(BF16))
            df_ref[rs, :] = df[-1]
        du = [_dot_nt(d, w2[...]) for d in df]
        da, dbb = [], []
        for rs, dup in zip(parts, du):
            af, bf = a_ref[rs, :].astype(F32), b_ref[rs, :].astype(F32)
            s = _sigmoid(af)
            sl = af * s
            u_ref[rs, :] = (sl * bf).astype(BF16)
            da.append((dup * bf * (s * (1.0 + af * (1.0 - s)))).astype(BF16))
            dbb.append((dup * sl).astype(BF16))
            dab_ref[rs, :DFF] = da[-1]
            dab_ref[rs, DFF:] = dbb[-1]
        for rs, dzp, dap, dbp in zip(parts, dz, da, dbb):
            dyin_ref[rs, :] = ALPHA * dzp + _dot(dap, w1[...]) + _dot(dbp, w3[...]) if with_dx else dzp

    row = lambda w: pl.BlockSpec((tm, w), lambda i: (i, 0))
    vec = pl.BlockSpec((1, D), lambda i: (0, 0))
    if from_loss:
        ins, in_specs = [dy_src[1], dy_src[2]], [vec, row(D)]
    else:
        ins, in_specs = [dy_src[1]], [row(D)]
    ins += [xh, r, g, a, b, wall]
    in_specs += [row(D), row(128), vec, row(DFF), row(DFF), HBM_SPEC]
    return _call(
        body, name=name, grid=(nt,), in_specs=in_specs,
        out_specs=[row(D), row(2 * DFF), row(DFF), row(D), vec, vec],
        out_shape=[jax.ShapeDtypeStruct((T, D), F32), jax.ShapeDtypeStruct((T, 2 * DFF), BF16),
                   jax.ShapeDtypeStruct((T, DFF), BF16), jax.ShapeDtypeStruct((T, D), BF16),
                   jax.ShapeDtypeStruct((1, D), F32), jax.ShapeDtypeStruct((1, D), F32)],
        scratch_shapes=[pltpu.VMEM((DFF, D), BF16)] * (3 if with_dx else 1) + [pltpu.SemaphoreType.DMA((3 * NDEV,))],
        ins=ins, carry=carry)


def _ffn_dx(name, dab, dz, wall, offs, tm, carry=None):
    T = dz.shape[0]

    def body(dab_ref, dz_ref, wall_ref, o_ref, w13, sem):
        @pl.when(pl.program_id(0) == 0)
        def _():
            _load_rows(wall_ref, [(w13.at[pl.ds(0, DFF), :], offs[0], RF), (w13.at[pl.ds(DFF, DFF), :], offs[1], RF)], sem)

        for p in range(0, tm, PART_ROWS):
            rs = slice(p, min(p + PART_ROWS, tm))
            o_ref[rs, :] = ALPHA * dz_ref[rs, :] + _dot(dab_ref[rs, :], w13[...])

    row = lambda w: pl.BlockSpec((tm, w), lambda i: (i, 0))
    (out,), ex = _call(
        body, name=name, grid=(T // tm,), in_specs=[row(2 * DFF), row(D), HBM_SPEC], out_specs=[row(D)],
        out_shape=[jax.ShapeDtypeStruct((T, D), F32)],
        scratch_shapes=[pltpu.VMEM((2 * DFF, D), BF16), pltpu.SemaphoreType.DMA((2 * NDEV,))],
        ins=[dab, dz, wall], carry=carry)
    return out, ex


def _wgrad(name, a, b, bn, tk, col0=0, ncols=None, carry=None):
    T = a.shape[0]
    N = a.shape[1] if ncols is None else ncols
    nk = T // tk
    c0 = col0 // bn

    def body(a_ref, b_ref, o_ref, acc):
        k = pl.program_id(1)

        @pl.when(k == 0)
        def _():
            acc[...] = jnp.zeros_like(acc)

        acc[...] += _dot_tn(a_ref[...], b_ref[...])

        @pl.when(k == nk - 1)
        def _():
            o_ref[...] = acc[...].astype(BF16)

    (out,), ex = _call(
        body, name=name, grid=(N // bn, nk),
        in_specs=[pl.BlockSpec((tk, bn), lambda n, k: (k, n + c0)), pl.BlockSpec((tk, D), lambda n, k: (k, 0))],
        out_specs=[pl.BlockSpec((bn, D), lambda n, k: (n, 0))],
        out_shape=[jax.ShapeDtypeStruct((N, D), BF16)],
        scratch_shapes=[pltpu.VMEM((bn, D), F32)], ins=[a, b], carry=carry)
    return out, ex


def _rope_tables(T):
    pos = jnp.arange(T, dtype=F32)
    inv_freq = ROPE_THETA ** (-jnp.arange(0, ROPE_DIM, 2, dtype=F32) / ROPE_DIM)
    half = ROPE_DIM // 2
    ch = jnp.arange(128) % HEAD_DIM
    ang = pos[:, None] * inv_freq[ch % half][None, :]
    cos, sin = jnp.cos(ang), jnp.sin(ang)
    first, second = (ch < half)[None, :], ((ch >= half) & (ch < ROPE_DIM))[None, :]
    return (jnp.where(first | second, cos, 1.0), jnp.where(first, -sin, 0.0), jnp.where(second, sin, 0.0))


def _rope(t, c, s1, s2):
    n = t.shape[1] // 128
    ct, s1t, s2t = (jnp.tile(v, (1, n)) for v in (c, s1, s2))
    w = t.shape[1]
    return t * ct + pltpu.roll(t, w - 8, 1) * s1t + pltpu.roll(t, 8, 1) * s2t


def _rope_t(dr, c, s1, s2):
    n = dr.shape[1] // 128
    ct, s1t, s2t = (jnp.tile(v, (1, n)) for v in (c, s1, s2))
    w = dr.shape[1]
    return dr * ct + pltpu.roll(dr * s1t, 8, 1) + pltpu.roll(dr * s2t, w - 8, 1)


_Q, _K, _V = (0, 1024), (1024, 256), (1280, 256)
_HG = (1536, HG_W)
_GATES = (5632, GATE_W)


def _inproj_fwd(name, xh, g, b, wall, b_in, ropes, tm, carry=None):
    T = xh.shape[0]

    def body(xh_ref, g_ref, b_ref, wall_ref, bin_ref, c_ref, s1_ref, s2_ref,
             qkv_ref, kt_ref, vt_ref, hg_ref, gate_ref, yb_ref, w, sem):
        @pl.when(pl.program_id(0) == 0)
        def _():
            _load_rows(wall_ref, [(w, 0, RIN)], sem)

        yb = (xh_ref[...] * g_ref[...] + b_ref[...]).astype(BF16)
        yb_ref[...] = yb
        c, s1, s2 = c_ref[...], s1_ref[...], s2_ref[...]

        def piece(start, width):
            return _dot_nt(yb, w[start:start + width, :]) + bin_ref[:, start:start + width]

        k = _rope(piece(*_K), c, s1, s2)
        v = piece(*_V)
        qkv_ref[:, 0:1024] = _rope(piece(*_Q), c, s1, s2).astype(BF16)
        qkv_ref[:, 1024:1280] = k.astype(BF16)
        qkv_ref[:, 1280:1536] = v.astype(BF16)
        kt_ref[...] = k.T.astype(BF16)
        vt_ref[...] = v.T.astype(BF16)
        for j in range(4):
            hg_ref[:, 1024 * j:1024 * (j + 1)] = piece(_HG[0] + 1024 * j, 1024)
        for j in range(2):
            gate_ref[:, 1024 * j:1024 * (j + 1)] = piece(_GATES[0] + 1024 * j, 1024)

    row = lambda wd: pl.BlockSpec((tm, wd), lambda i: (i, 0))
    vec = lambda wd: pl.BlockSpec((1, wd), lambda i: (0, 0))
    colt = pl.BlockSpec((256, tm), lambda i: (0, i))
    return _call(
        body, name=name, grid=(T // tm,),
        in_specs=[row(D), vec(D), vec(D), HBM_SPEC, vec(DIN), row(128), row(128), row(128)],
        out_specs=[row(QKV_W), colt, colt, row(HG_W), row(GATE_W), row(D)],
        out_shape=[jax.ShapeDtypeStruct((T, QKV_W), BF16), jax.ShapeDtypeStruct((256, T), BF16),
                   jax.ShapeDtypeStruct((256, T), BF16), jax.ShapeDtypeStruct((T, HG_W), F32),
                   jax.ShapeDtypeStruct((T, GATE_W), F32), jax.ShapeDtypeStruct((T, D), BF16)],
        scratch_shapes=[pltpu.VMEM((DIN, D), BF16), pltpu.SemaphoreType.DMA((NDEV,))],
        ins=[xh, g, b, wall, b_in, *ropes], carry=carry)


def _inproj_bwd(name, dq, dk, dv, dhg, dgates, dz2, wall, ropes, tm):
    T = dq.shape[0]

    def body(dq_ref, dk_ref, dv_ref, d0, d1, d2, d3, dga_ref, dgh_ref, dz_ref, wall_ref, c_ref, s1_ref, s2_ref,
             dy_ref, dproj_ref, dbin_ref, w, sem):
        @pl.when(pl.program_id(0) == 0)
        def _():
            _load_rows(wall_ref, [(w, 0, RIN)], sem)
            dbin_ref[...] = jnp.zeros_like(dbin_ref)

        c, s1, s2 = c_ref[...], s1_ref[...], s2_ref[...]
        acc = ALPHA * dz_ref[...]
        pieces = [(_Q[0], _rope_t(dq_ref[...], c, s1, s2)), (_K[0], _rope_t(dk_ref[...], c, s1, s2)),
                  (_V[0], dv_ref[...])]
        pieces += [(_HG[0] + 1024 * j, r[...]) for j, r in enumerate((d0, d1, d2, d3))]
        pieces += [(_GATES[0], dga_ref[...]), (_GATES[0] + 1024, dgh_ref[...])]
        for start, val in pieces:
            width = val.shape[1]
            dbin_ref[:, start:start + width] += jnp.sum(val.astype(F32), axis=0, keepdims=True)
            vb = val.astype(BF16)
            dproj_ref[:, start:start + width] = vb
            acc = acc + _dot(vb, w[start:start + width, :])
        dy_ref[...] = acc

    row = lambda wd: pl.BlockSpec((tm, wd), lambda i: (i, 0))
    return pl.pallas_call(
        body, name=name, grid=(T // tm,),
        in_specs=[row(D), row(256), row(256)] + [row(D)] * 4 + [row(D), row(D), row(D), HBM_SPEC] + [row(128)] * 3,
        out_specs=[row(D), row(DIN), pl.BlockSpec((1, DIN), lambda i: (0, 0))],
        out_shape=[jax.ShapeDtypeStruct((T, D), F32), jax.ShapeDtypeStruct((T, DIN), BF16),
                   jax.ShapeDtypeStruct((1, DIN), F32)],
        scratch_shapes=[pltpu.VMEM((DIN, D), BF16), pltpu.SemaphoreType.DMA((NDEV,))],
        compiler_params=_params("arbitrary"),
    )(dq, dk, dv, *dhg, *dgates, dz2, wall, *ropes)


def _halves(t):
    lane = lax.broadcasted_iota(jnp.int32, t.shape, 1)
    low = lane < HEAD_DIM
    sw = pltpu.roll(t, HEAD_DIM, 1)
    zero = jnp.zeros_like(t)
    h0 = (jnp.where(low, t, zero), jnp.where(low, zero, sw))
    h1 = (jnp.where(low, sw, zero), jnp.where(low, zero, t))
    return h0, h1


def _lane_stack(p_ref, c_ref, gp):
    sl = slice(128 * gp, 128 * (gp + 1))
    hp, hc = _halves(p_ref[:, sl].astype(F32)), _halves(c_ref[:, sl].astype(F32))
    return [jnp.concatenate([hp[gg][0], hc[gg][0], hp[gg][1], hc[gg][1]], axis=0).astype(BF16) for gg in range(2)]


def _row_stack(tp_ref, tc_ref, g):
    band = jnp.concatenate([tp_ref[64 * g:64 * (g + 1), :], tc_ref[64 * g:64 * (g + 1), :]], axis=1)
    z = jnp.zeros_like(band)
    return [jnp.concatenate([band, z], axis=0), jnp.concatenate([z, band], axis=0)]


def _fold(prevm, t4, hh):
    return jnp.where(prevm, t4[256 * hh:256 * hh + 128, :], t4[256 * hh + 128:256 * hh + 256, :])


def _unfold(prevm, t):
    return jnp.concatenate([jnp.where(prevm, t, 0.0), jnp.where(prevm, 0.0, t)], axis=0).astype(BF16)


def _attn_softmax(s, kill, sink):
    s = jnp.where(kill, NEG_INF, s)
    m = jnp.maximum(jnp.max(s, axis=0, keepdims=True), sink)
    p = jnp.exp(s - m)
    es = jnp.exp(sink - m)
    inv = 1.0 / (jnp.sum(p, axis=0, keepdims=True) + es)
    return p * inv, es * inv


def _attn_masks(first):
    row = lax.broadcasted_iota(jnp.int32, (ATTN_BLOCK, 2 * ATTN_BLOCK), 0)
    lane = lax.broadcasted_iota(jnp.int32, (ATTN_BLOCK, 2 * ATTN_BLOCK), 1)
    prevm = row > lane % ATTN_BLOCK
    return prevm, jnp.logical_and(first, prevm)


def _pair_rows(ref, g):
    return jnp.concatenate([ref[:, 256 * g:256 * g + 128], ref[:, 256 * g + 128:256 * (g + 1)]], axis=0)


def _pair_sinks(sink_ref, g, hh):
    h0, h1 = 4 * g + hh, 4 * g + 2 + hh
    return jnp.concatenate([jnp.broadcast_to(sink_ref[:, h0:h0 + 1], (1, ATTN_BLOCK)),
                            jnp.broadcast_to(sink_ref[:, h1:h1 + 1], (1, ATTN_BLOCK))], axis=1)


def _attn_fwd(name, qkv, vt, sinks):
    T = qkv.shape[0]
    nb = T // ATTN_BLOCK
    scale = HEAD_DIM ** -0.5

    def body(q_ref, kp_ref, kc_ref, vtp_ref, vtc_ref, sink_ref, o_ref):
        prevm, kill = _attn_masks(pl.program_id(0) == 0)
        kst = _lane_stack(kp_ref, kc_ref, 0) + _lane_stack(kp_ref, kc_ref, 1)
        vts = [_row_stack(vtp_ref, vtc_ref, g) for g in range(N_KV_HEADS)]
        s8 = [_dot_nt(kst[g], _pair_rows(q_ref, g)) * scale for g in range(N_KV_HEADS)]
        pu = [[_unfold(prevm, _attn_softmax(_fold(prevm, s8[g], hh), kill, _pair_sinks(sink_ref, g, hh))[0])
               for hh in range(2)] for g in range(N_KV_HEADS)]
        for g in range(N_KV_HEADS):
            ot = _dot(vts[g][0], pu[g][0]) + _dot(vts[g][1], pu[g][1])
            o_ref[:, 256 * g:256 * g + 128] = ot[:, :ATTN_BLOCK].T.astype(BF16)
            o_ref[:, 256 * g + 128:256 * (g + 1)] = ot[:, ATTN_BLOCK:].T.astype(BF16)

    prev = lambda i: jnp.maximum(i - 1, 0)
    return pl.pallas_call(
        body, name=name, grid=(nb,),
        in_specs=[pl.BlockSpec((ATTN_BLOCK, D), lambda i: (i, 0)),
                  pl.BlockSpec((ATTN_BLOCK, 256), lambda i: (prev(i), 4)),
                  pl.BlockSpec((ATTN_BLOCK, 256), lambda i: (i, 4)),
                  pl.BlockSpec((256, ATTN_BLOCK), lambda i: (0, prev(i))),
                  pl.BlockSpec((256, ATTN_BLOCK), lambda i: (0, i)),
                  pl.BlockSpec((1, N_Q_HEADS), lambda i: (0, 0))],
        out_specs=pl.BlockSpec((ATTN_BLOCK, D), lambda i: (i, 0)),
        out_shape=jax.ShapeDtypeStruct((T, D), BF16),
        compiler_params=_params("arbitrary"),
    )(qkv, qkv, qkv, vt, vt, sinks)


ATTN_BWD_ORDER = ((0, 0), (0, 1), (1, 0), (0, 2), (1, 1), (2, 0), (0, 3), (1, 2), (2, 1), (1, 3), (2, 2), (2, 3))


def _attn_bwd(name, qkv, kt, sinks, dya, carry=None):
    T = qkv.shape[0]
    nb = T // ATTN_BLOCK
    scale = HEAD_DIM ** -0.5

    def body(q_ref, kp_ref, kc_ref, vp_ref, vc_ref, ktp_ref, ktc_ref, sink_ref, do_ref,
             dq_ref, dk_ref, dv_ref, ds_ref):
        i = pl.program_id(0)

        @pl.when(i == 0)
        def _():
            ds_ref[...] = jnp.zeros_like(ds_ref)

        prevm, kill = _attn_masks(i == 0)
        kst = _lane_stack(kp_ref, kc_ref, 0) + _lane_stack(kp_ref, kc_ref, 1)
        vst = _lane_stack(vp_ref, vc_ref, 0) + _lane_stack(vp_ref, vc_ref, 1)
        kts = [_row_stack(ktp_ref, ktc_ref, g) for g in range(N_KV_HEADS)]
        lane = lax.broadcasted_iota(jnp.int32, (2 * ATTN_BLOCK, 128), 1)
        low = lane < HEAD_DIM
        slane = lax.broadcasted_iota(jnp.int32, (1, 128), 1)
        dkz = [jnp.zeros((2 * ATTN_BLOCK, 128), F32) for _ in range(N_KV_HEADS)]
        dvz = [jnp.zeros((2 * ATTN_BLOCK, 128), F32) for _ in range(N_KV_HEADS)]
        dsink = jnp.zeros((1, 128), F32)
        groups = range(N_KV_HEADS)
        qcat, docat, s8, dp8 = {}, {}, {}, {}
        ds_u, p_u, dsinks = {}, {}, []

        def scores(g):
            qcat[g], docat[g] = _pair_rows(q_ref, g), _pair_rows(do_ref, g)
            s8[g] = _dot_nt(kst[g], qcat[g]) * scale
            dp8[g] = _dot_nt(vst[g], docat[g])

        def algebra(g):
            for hh in range(2):
                pn, ps = _attn_softmax(_fold(prevm, s8[g], hh), kill, _pair_sinks(sink_ref, g, hh))
                dp = _fold(prevm, dp8[g], hh)
                delta = jnp.sum(pn * dp, axis=0, keepdims=True)
                sd = ps * delta
                dsinks.append(jnp.where(slane == 4 * g + hh, -jnp.sum(sd[:, :ATTN_BLOCK]), 0.0)
                              + jnp.where(slane == 4 * g + 2 + hh, -jnp.sum(sd[:, ATTN_BLOCK:]), 0.0))
                ds_u[g, hh] = _unfold(prevm, pn * (dp - delta) * scale)
                p_u[g, hh] = _unfold(prevm, pn)

        def grads(g):
            dqt = _dot(kts[g][0], ds_u[g, 0]) + _dot(kts[g][1], ds_u[g, 1])
            dq_ref[:, 256 * g:256 * g + 128] = dqt[:, :ATTN_BLOCK].T
            dq_ref[:, 256 * g + 128:256 * (g + 1)] = dqt[:, ATTN_BLOCK:].T
            for hh in range(2):
                own = low if hh == 0 else jnp.logical_not(low)
                dk_h = jnp.where(own, _dot(ds_u[g, hh], qcat[g]), 0.0)
                dv_h = jnp.where(own, _dot(p_u[g, hh], docat[g]), 0.0)
                if hh != g % 2:
                    dk_h = pltpu.roll(dk_h, HEAD_DIM, 1)
                    dv_h = pltpu.roll(dv_h, HEAD_DIM, 1)
                dkz[g] = dkz[g] + dk_h
                dvz[g] = dvz[g] + dv_h

        for stage, g in ATTN_BWD_ORDER:
            (scores, algebra, grads)[stage](g)
        for t in dsinks:
            dsink = dsink + t
        ds_ref[...] += dsink
        cur = pl.ds(pl.multiple_of(i * ATTN_BLOCK, ATTN_BLOCK), ATTN_BLOCK)
        prv = pl.ds(pl.multiple_of(jnp.maximum(i - 1, 0) * ATTN_BLOCK, ATTN_BLOCK), ATTN_BLOCK)
        for gp in range(N_KV_HEADS // 2):
            cols = slice(128 * gp, 128 * (gp + 1))
            dkb = dkz[2 * gp] + dkz[2 * gp + 1]
            dvb = dvz[2 * gp] + dvz[2 * gp + 1]
            dk_ref[cur, cols] = dkb[ATTN_BLOCK:, :]
            dv_ref[cur, cols] = dvb[ATTN_BLOCK:, :]

            @pl.when(i > 0)
            def _():
                dk_ref[prv, cols] += dkb[:ATTN_BLOCK, :]
                dv_ref[prv, cols] += dvb[:ATTN_BLOCK, :]

    prev = lambda i: jnp.maximum(i - 1, 0)
    whole = lambda w: pl.BlockSpec((T, w), lambda i: (0, 0))
    return _call(
        body, name=name, grid=(nb,), scratch_shapes=[], ins=[qkv, qkv, qkv, qkv, qkv, kt, kt, sinks, dya], carry=carry,
        in_specs=[pl.BlockSpec((ATTN_BLOCK, D), lambda i: (i, 0)),
                  pl.BlockSpec((ATTN_BLOCK, 256), lambda i: (prev(i), 4)),
                  pl.BlockSpec((ATTN_BLOCK, 256), lambda i: (i, 4)),
                  pl.BlockSpec((ATTN_BLOCK, 256), lambda i: (prev(i), 5)),
                  pl.BlockSpec((ATTN_BLOCK, 256), lambda i: (i, 5)),
                  pl.BlockSpec((256, ATTN_BLOCK), lambda i: (0, prev(i))),
                  pl.BlockSpec((256, ATTN_BLOCK), lambda i: (0, i)),
                  pl.BlockSpec((1, N_Q_HEADS), lambda i: (0, 0)),
                  pl.BlockSpec((ATTN_BLOCK, D), lambda i: (i, 0))],
        out_specs=[pl.BlockSpec((ATTN_BLOCK, D), lambda i: (i, 0)), whole(256), whole(256),
                   pl.BlockSpec((1, 128), lambda i: (0, 0))],
        out_shape=[jax.ShapeDtypeStruct((T, D), F32), jax.ShapeDtypeStruct((T, 256), F32),
                   jax.ShapeDtypeStruct((T, 256), F32), jax.ShapeDtypeStruct((1, 128), F32)])


HG_FWD = (256, 8, 512)
HG_BWD = (128, 8, 512)


def _chunk_sum(tri, x):
    w = x.shape[1]
    h1 = x.astype(BF16)
    h2 = (x - h1.astype(F32)).astype(BF16)
    r = _dot(tri, jnp.concatenate([h1, h2], axis=1))
    return r[:, :w] + r[:, w:]


def _chunk_tri(n):
    ri = lax.broadcasted_iota(jnp.int32, (n, n), 0)
    ci = lax.broadcasted_iota(jnp.int32, (n, n), 1)
    return jnp.where((ri >= ci) & (ri // HGRN_CHUNK == ci // HGRN_CHUNK), 1.0, 0.0).astype(BF16)


def _chunks(t):
    return [t[HGRN_CHUNK * c:HGRN_CHUNK * (c + 1), :] for c in range(t.shape[0] // HGRN_CHUNK)]


def _lower_bound(lbl_ref):
    l0, l1 = lbl_ref[0:1, :], lbl_ref[1:2, :]
    m = jnp.maximum(l0, l1)
    e0, e1 = jnp.exp(l0 - m), jnp.exp(l1 - m)
    return e0 / (e0 + e1)


def _heads(t):
    return [t[:, 128 * h:128 * (h + 1)] for h in range(t.shape[1] // 128)]


def _per_head(fn, *wide):
    return jnp.concatenate([fn(*parts) for parts in zip(*[_heads(t) for t in wide])], axis=1)


def _hgrn_sub(fl, qh, vv, lb, tril_b):
    w = fl.shape[1]
    sg = _sigmoid(fl)
    f = lb + (1.0 - lb) * sg
    k = 1.0 - f
    gc = _chunk_sum(tril_b, jnp.log(f))
    gl_c = [t[HGRN_CHUNK - 1:HGRN_CHUNK, :] for t in _chunks(gc)]
    gl = jnp.concatenate([jnp.broadcast_to(g, (HGRN_CHUNK, w)) for g in gl_c], axis=0)
    sq = _sigmoid(qh)
    eg = jnp.exp(gc)
    eng = jnp.exp(-gc)
    elg = jnp.exp(gl - gc)
    qd = qh * sq * eg
    ki = k * eng
    ke = k * elg
    qd_b, ki_b, ke_b, v_b = (t.astype(BF16) for t in (qd, ki, ke, vv))
    am_b = [jnp.where(tril_b > 0, _dot_nt(qq, kk), 0.0).astype(BF16) for qq, kk in zip(_heads(qd_b), _heads(ki_b))]
    return dict(sg=sg, f=f, sq=sq, eg=eg, eng=eng, elg=elg, qd=qd, ki=ki, ke=ke, egl=[jnp.exp(g) for g in gl_c],
                am_b=am_b, qd_b=qd_b, ki_b=ki_b, ke_b=ke_b, v_b=v_b)


def _hgrn_out(q, st_b):
    outs = []
    for h, (qd_h, v_h) in enumerate(zip(_heads(q["qd_b"]), _heads(q["v_b"]))):
        inter = [_dot_nt(qc, s) for qc, s in zip(_chunks(qd_h), st_b[h])]
        outs.append(_dot(q["am_b"][h], v_h) + jnp.concatenate(inter, axis=0))
    return jnp.concatenate(outs, axis=1)


def _head_rms(o):
    return _per_head(lambda t: jnp.broadcast_to(
        lax.rsqrt(jnp.mean(t * t, axis=-1, keepdims=True) + RMS_EPS), t.shape), o)


def _hgrn_fwd(name, hg, lb_logits, norm_g, cfg):
    T = hg.shape[0]
    sub_rows, hps, th = min(cfg[0], T), cfg[1], min(cfg[2], T)
    cps = sub_rows // HGRN_CHUNK
    nc = th // HGRN_CHUNK
    tri = _chunk_tri(sub_rows)

    def body(fl_ref, qh_ref, v_ref, og_ref, lbl_ref, ng_ref, tri_ref, y_ref, ss_ref, st):
        @pl.when(pl.program_id(1) == 0)
        def _():
            st[...] = jnp.zeros_like(st)

        lbs = _lower_bound(lbl_ref)
        ng = jnp.tile(ng_ref[...], (1, hps))

        def sub(si, carry):
            rows = pl.ds(pl.multiple_of(si * sub_rows, sub_rows), sub_rows)
            q = _hgrn_sub(fl_ref[rows, :], qh_ref[rows, :], v_ref[rows, :], lbs, tri_ref[...])
            v_hc = [_chunks(t) for t in _heads(q["v_b"])]
            k_hc = [_chunks(t) for t in _heads(q["ke_b"])]
            s = [st[h] for h in range(hps)]
            st_b = [[] for _ in range(hps)]
            for c in range(cps):
                egl = _heads(q["egl"][c])
                for h in range(hps):
                    ss_ref[h, si * cps + c] = s[h]
                    st_b[h].append(s[h].astype(BF16))
                    s[h] = s[h] * egl[h] + _dot_tn(v_hc[h][c], k_hc[h][c])
            for h in range(hps):
                st[h] = s[h]
            o = _hgrn_out(q, st_b)
            og = og_ref[rows, :]
            y_ref[rows, :] = (o * _head_rms(o) * ng * (og * _sigmoid(og))).astype(BF16)
            return carry

        lax.fori_loop(0, th // sub_rows, sub, 0)

    wd = 128 * hps
    col = lambda j: pl.BlockSpec((th, wd), lambda h, t: (t, (8 // hps) * j + h))
    return pl.pallas_call(
        body, name=name, grid=(HGRN_HEADS // hps, T // th),
        in_specs=[col(0), col(1), col(2), col(3), pl.BlockSpec((2, wd), lambda h, t: (0, h)),
                  pl.BlockSpec((1, 128), lambda h, t: (0, 0)), pl.BlockSpec((sub_rows, sub_rows), lambda h, t: (0, 0))],
        out_specs=[pl.BlockSpec((th, wd), lambda h, t: (t, h)),
                   pl.BlockSpec((hps, nc, 128, 128), lambda h, t: (h, t, 0, 0))],
        out_shape=[jax.ShapeDtypeStruct((T, D), BF16),
                   jax.ShapeDtypeStruct((HGRN_HEADS, T // HGRN_CHUNK, 128, 128), F32)],
        scratch_shapes=[pltpu.VMEM((hps, 128, 128), F32)],
        compiler_params=_params("parallel", "arbitrary"),
    )(hg, hg, hg, hg, lb_logits, norm_g, tri)


def _hgrn_bwd(name, hg, lb_logits, norm_g, sstart, dyh, cfg, carry=None):
    T = hg.shape[0]
    sub_rows, hps, th = min(cfg[0], T), cfg[1], min(cfg[2], T)
    cps = sub_rows // HGRN_CHUNK
    nc = th // HGRN_CHUNK
    nt = T // th
    wd = 128 * hps
    tri = _chunk_tri(sub_rows)
    tri_t = tri.T

    def body(fl_ref, qh_ref, v_ref, og_ref, lbl_ref, ng_ref, tri_ref, trit_ref, ss_ref, dy_ref,
             dfl_ref, dqh_ref, dv_ref, dog_ref, dlb_ref, dng_ref, dst):
        @pl.when(pl.program_id(1) == 0)
        def _():
            dst[...] = jnp.zeros_like(dst)
            dlb_ref[...] = jnp.zeros_like(dlb_ref)
            dng_ref[...] = jnp.zeros_like(dng_ref)

        lb = _lower_bound(lbl_ref)
        ng = jnp.tile(ng_ref[...], (1, hps))
        last = lax.broadcasted_iota(jnp.int32, (HGRN_CHUNK, wd), 0) == HGRN_CHUNK - 1
        nsub = th // sub_rows
        cat0 = lambda parts: jnp.concatenate(parts, axis=0)

        def sub(step, carry):
            si = nsub - 1 - step
            rows = pl.ds(pl.multiple_of(si * sub_rows, sub_rows), sub_rows)
            qh, og, dy = qh_ref[rows, :], og_ref[rows, :], dy_ref[rows, :]
            tril_b = tri_ref[...]
            q = _hgrn_sub(fl_ref[rows, :], qh, v_ref[rows, :], lb, tril_b)
            s_in = [[ss_ref[h, si * cps + c] for c in range(cps)] for h in range(hps)]
            st_b = [[s.astype(BF16) for s in row] for row in s_in]
            o = _hgrn_out(q, st_b)
            rr = _head_rms(o)
            oh = o * rr
            sog = _sigmoid(og)
            dog_ref[rows, :] = (dy * oh * ng * (sog * (1.0 + og * (1.0 - sog)))).astype(BF16)
            don = dy * (og * sog)
            dng_w = jnp.sum(don * oh, axis=0, keepdims=True)
            dd = don * ng
            mean_h = _per_head(lambda t: jnp.broadcast_to(jnp.mean(t, axis=-1, keepdims=True), t.shape), dd * oh)
            do_b = (rr * (dd - oh * mean_h)).astype(BF16)
            do_h, qd_h, ki_h, ke_h, v_h = (_heads(t) for t in (do_b, q["qd_b"], q["ki_b"], q["ke_b"], q["v_b"]))
            da_b = [jnp.where(tril_b > 0, _dot_nt(do_h[h], v_h[h]), 0.0).astype(BF16) for h in range(hps)]
            do_c, qd_c, ke_c, v_c = ([_chunks(t) for t in hs] for hs in (do_h, qd_h, ke_h, v_h))
            dsp = [[None] * cps for _ in range(hps)]
            dgl_dec = [[None] * cps for _ in range(hps)]
            d = [dst[h] for h in range(hps)]
            for c in reversed(range(cps)):
                egl = _heads(q["egl"][c])
                for h in range(hps):
                    dsp[h][c] = d[h]
                    dgl_dec[h][c] = jnp.sum(d[h] * s_in[h][c], axis=0, keepdims=True) * egl[h]
                    d[h] = d[h] * egl[h] + _dot_tn(do_c[h][c], qd_c[h][c])
            for h in range(hps):
                dst[h] = d[h]
                dng_ref[h] += dng_w[:, 128 * h:128 * (h + 1)]
            dsp_b = [[t.astype(BF16) for t in row] for row in dsp]
            dv_ref[rows, :] = jnp.concatenate(
                [_dot_tn(q["am_b"][h], do_h[h]) + cat0([_dot_nt(ke_c[h][c], dsp_b[h][c]) for c in range(cps)])
                 for h in range(hps)], axis=1).astype(BF16)
            dqd = jnp.concatenate(
                [_dot(da_b[h], ki_h[h]) + cat0([_dot(do_c[h][c], st_b[h][c]) for c in range(cps)])
                 for h in range(hps)], axis=1)
            dki = jnp.concatenate([_dot_tn(da_b[h], qd_h[h]) for h in range(hps)], axis=1)
            dke = jnp.concatenate([cat0([_dot(v_c[h][c], dsp_b[h][c]) for c in range(cps)]) for h in range(hps)], axis=1)
            dkk = dke * q["ke"]
            dgl = [jnp.sum(t, axis=0, keepdims=True) + jnp.concatenate([dgl_dec[h][c] for h in range(hps)], axis=1)
                   for c, t in enumerate(_chunks(dkk))]
            dgc = dqd * q["qd"] - dki * q["ki"] - dkk + cat0([jnp.where(last, g, 0.0) for g in dgl])
            dlf = _chunk_sum(trit_ref[...], dgc)
            df = dlf / q["f"] - (dki * q["eng"] + dke * q["elg"])
            sg = q["sg"]
            dlb_ref[...] += jnp.sum(df * (1.0 - sg), axis=0, keepdims=True)
            dfl_ref[rows, :] = (df * (1.0 - lb) * sg * (1.0 - sg)).astype(BF16)
            sq = q["sq"]
            dqh_ref[rows, :] = (dqd * q["eg"] * (sq * (1.0 + qh * (1.0 - sq)))).astype(BF16)
            return carry

        lax.fori_loop(0, nsub, sub, 0)
    col = lambda j: pl.BlockSpec((th, wd), lambda h, t: (nt - 1 - t, (8 // hps) * j + h))
    out = pl.BlockSpec((th, wd), lambda h, t: (nt - 1 - t, h))
    tri_spec = pl.BlockSpec((sub_rows, sub_rows), lambda h, t: (0, 0))
    return _call(
        body, name=name, grid=(HGRN_HEADS // hps, nt),
        in_specs=[col(0), col(1), col(2), col(3), pl.BlockSpec((2, wd), lambda h, t: (0, h)),
                  pl.BlockSpec((1, 128), lambda h, t: (0, 0)), tri_spec, tri_spec,
                  pl.BlockSpec((hps, nc, 128, 128), lambda h, t: (h, nt - 1 - t, 0, 0)), out],
        out_specs=[out, out, out, out, pl.BlockSpec((1, wd), lambda h, t: (0, h)),
                   pl.BlockSpec((hps, 1, 128), lambda h, t: (h, 0, 0))],
        out_shape=[jax.ShapeDtypeStruct((T, D), BF16)] * 4 + [jax.ShapeDtypeStruct((1, D), F32),
                                                              jax.ShapeDtypeStruct((HGRN_HEADS, 1, 128), F32)],
        scratch_shapes=[pltpu.VMEM((hps, 128, 128), F32)],
        ins=[hg, hg, hg, hg, lb_logits, norm_g, tri, tri_t, sstart, dyh], carry=carry)


def _mix_fwd(name, ya, yh, gates, xh1, g1, b1, wall, tm):
    T = ya.shape[0]

    def body(ya_ref, yh_ref, ga_ref, gh_ref, xh_ref, g_ref, b_ref, wall_ref,
             xo_ref, r_ref, pa_ref, ph_ref, mg_ref, wpa, wph, wo, sem):
        @pl.when(pl.program_id(0) == 0)
        def _():
            _load_rows(wall_ref, [(wpa, RIN, RP), (wph, RIN + RP, RP), (wo, RIN + 2 * RP, RP)], sem)

        parts = [slice(p, min(p + PART_ROWS, tm)) for p in range(0, tm, PART_ROWS)]
        pas = [_dot(ya_ref[rs, :], wpa[...]).astype(BF16) for rs in parts]
        phs = [_dot(yh_ref[rs, :], wph[...]).astype(BF16) for rs in parts]
        mgs = []
        for rs, pa, ph in zip(parts, pas, phs):
            pa_ref[rs, :] = pa
            ph_ref[rs, :] = ph
            mgs.append((_sigmoid(ga_ref[rs, :]) * pa.astype(F32) + _sigmoid(gh_ref[rs, :]) * ph.astype(F32)).astype(BF16))
            mg_ref[rs, :] = mgs[-1]
        mixes = [_dot(mg, wo[...]) for mg in mgs]
        for rs, mix in zip(parts, mixes):
            y1 = xh_ref[rs, :] * g_ref[...] + b_ref[...]
            xh2, r = _ln_fwd(ALPHA * y1 + mix)
            xo_ref[rs, :] = xh2
            r_ref[rs, :] = jnp.broadcast_to(r, (xh2.shape[0], 128))

    row = lambda w: pl.BlockSpec((tm, w), lambda i: (i, 0))
    vec = pl.BlockSpec((1, D), lambda i: (0, 0))
    return pl.pallas_call(
        body, name=name, grid=(T // tm,),
        in_specs=[row(D), row(D), pl.BlockSpec((tm, D), lambda i: (i, 0)), pl.BlockSpec((tm, D), lambda i: (i, 1)),
                  row(D), vec, vec, HBM_SPEC],
        out_specs=[row(D), row(128), row(D), row(D), row(D)],
        out_shape=[jax.ShapeDtypeStruct((T, D), F32), jax.ShapeDtypeStruct((T, 128), F32)]
        + [jax.ShapeDtypeStruct((T, D), BF16)] * 3,
        scratch_shapes=[pltpu.VMEM((D, D), BF16)] * 3 + [pltpu.SemaphoreType.DMA((3 * NDEV,))],
        compiler_params=_params("arbitrary"),
    )(ya, yh, gates, gates, xh1, g1, b1, wall)


def _mix_bwd(name, dy2, xh2, r2, g2, gates, pa, ph, wall, tm, carry=None):
    T = dy2.shape[0]

    def body(dy_ref, xh_ref, r_ref, g_ref, ga_ref, gh_ref, pa_ref, ph_ref, wall_ref,
             dz_ref, dmix_ref, dpa_ref, dph_ref, dga_ref, dgh_ref, dya_ref, dyh_ref, dg_ref, db_ref,
             wpa, wph, wo, sem):
        @pl.when(pl.program_id(0) == 0)
        def _():
            _load_rows(wall_ref, [(wpa, RIN, RP), (wph, RIN + RP, RP), (wo, RIN + 2 * RP, RP)], sem)
            dg_ref[...] = jnp.zeros_like(dg_ref)
            db_ref[...] = jnp.zeros_like(db_ref)

        parts = [slice(p, min(p + PART_ROWS, tm)) for p in range(0, tm, PART_ROWS)]
        dmix = []
        for rs in parts:
            dz, dgp, dbp = _ln_bwd(dy_ref[rs, :], xh_ref[rs, :], r_ref[rs, :1], g_ref[...])
            dg_ref[...] += dgp
            db_ref[...] += dbp
            dz_ref[rs, :] = dz
            dmix.append(dz.astype(BF16))
            dmix_ref[rs, :] = dmix[-1]
        dmgs = [_dot_nt(d, wo[...]) for d in dmix]
        dpas, dphs = [], []
        for rs, dmg in zip(parts, dmgs):
            sa, sh = _sigmoid(ga_ref[rs, :]), _sigmoid(gh_ref[rs, :])
            dga_ref[rs, :] = (dmg * pa_ref[rs, :].astype(F32) * sa * (1.0 - sa)).astype(BF16)
            dgh_ref[rs, :] = (dmg * ph_ref[rs, :].astype(F32) * sh * (1.0 - sh)).astype(BF16)
            dpas.append((dmg * sa).astype(BF16))
            dphs.append((dmg * sh).astype(BF16))
            dpa_ref[rs, :] = dpas[-1]
            dph_ref[rs, :] = dphs[-1]
        for rs, dpa, dph in zip(parts, dpas, dphs):
            dya_ref[rs, :] = _dot_nt(dpa, wpa[...]).astype(BF16)
            dyh_ref[rs, :] = _dot_nt(dph, wph[...])

    row = lambda w: pl.BlockSpec((tm, w), lambda i: (i, 0))
    vec = pl.BlockSpec((1, D), lambda i: (0, 0))
    return _call(
        body, name=name, grid=(T // tm,),
        in_specs=[row(D), row(D), row(128), vec, pl.BlockSpec((tm, D), lambda i: (i, 0)),
                  pl.BlockSpec((tm, D), lambda i: (i, 1)), row(D), row(D), HBM_SPEC],
        out_specs=[row(D)] * 8 + [vec, vec],
        out_shape=[jax.ShapeDtypeStruct((T, D), F32)] + [jax.ShapeDtypeStruct((T, D), BF16)] * 6
        + [jax.ShapeDtypeStruct((T, D), F32)] + [jax.ShapeDtypeStruct((1, D), F32)] * 2,
        scratch_shapes=[pltpu.VMEM((D, D), BF16)] * 3 + [pltpu.SemaphoreType.DMA((3 * NDEV,))],
        ins=[dy2, xh2, r2, g2, gates, gates, pa, ph, wall], carry=carry)


def _adam(w, g, m, v):
    m = ADAM_B1 * m + (1.0 - ADAM_B1) * g
    v = ADAM_B2 * v + (1.0 - ADAM_B2) * (g * g)
    m_hat = m / (1.0 - ADAM_B1 ** ADAM_STEP)
    v_hat = v / (1.0 - ADAM_B2 ** ADAM_STEP)
    delta = -ADAM_LR * (m_hat / (jnp.sqrt(v_hat) + ADAM_EPS) + ADAM_WD * w)
    return delta, m, v


def _grad_steps(name, items, cb, carry=None):
    n = len(items)
    pieces = [len(it[0]) for it in items]

    def body(*refs):
        it = iter(refs[:sum(pieces) + 3 * n])
        outs = refs[sum(pieces) + 3 * n:]
        for i in range(n):
            accs = []
            for _ in range(pieces[i]):
                r_ref = next(it)
                acc = r_ref[0].astype(F32)
                for k in range(1, NDEV):
                    acc = acc + r_ref[k].astype(F32)
                accs.append(acc)
            acc = accs[0] if len(accs) == 1 else jnp.concatenate(accs, axis=0)
            w_ref, m_ref, v_ref = next(it), next(it), next(it)
            g_ref, d_ref, mo_ref, vo_ref = outs[4 * i:4 * i + 4]
            g_ref[...] = acc
            d, mm, vv = _adam(w_ref[...], acc, m_ref[...], v_ref[...])
            d_ref[...] = d
            mo_ref[...] = mm
            vo_ref[...] = vv

    in_specs, out_specs, out_shape, ins = [], [], [], []
    for recv, w, m, v in items:
        blk = pl.BlockSpec((w.shape[0], cb), lambda j: (0, j))
        in_specs += [pl.BlockSpec((NDEV, rp.shape[1], cb), lambda j: (0, 0, j)) for rp in recv] + [blk, blk, blk]
        out_specs += [blk] * 4
        out_shape += [jax.ShapeDtypeStruct(w.shape, F32)] * 4
        ins += list(recv) + [w, m, v]
    outs, ex = _call(body, name=name, grid=(D // cb,), in_specs=in_specs, out_specs=out_specs, out_shape=out_shape,
                     scratch_shapes=[], ins=ins, carry=carry)
    return [tuple(outs[4 * i:4 * i + 4]) for i in range(n)], ex


_SMALL = [("ln1_g", D), ("ln1_b", D), ("ln2_g", D), ("ln2_b", D), ("ln3_g", D), ("ln3_b", D),
          ("b_in", DIN), ("lb", D), ("attn_sinks", 128), ("hgrn_norm_g", 128), ("loss", 128)]
_SMALL_OFF = {}
_o = 0
for _n, _w in _SMALL:
    _SMALL_OFF[_n] = (_o, _w)
    _o += _w
PACK = _o


def _small_reduce(name, packed):
    def body(p_ref, o_ref, buf, send_sems, recv_sems):
        x, y, c = lax.axis_index("x"), lax.axis_index("y"), lax.axis_index("c")
        me = 4 * x + 2 * y + c
        buf[me] = p_ref[...]
        copies = []
        for j in range(1, NDEV):
            px, py, pc = x ^ (j >> 2), y ^ ((j >> 1) & 1), c ^ (j & 1)
            cp = pltpu.make_async_remote_copy(
                src_ref=p_ref, dst_ref=buf.at[me], send_sem=send_sems.at[j - 1], recv_sem=recv_sems.at[j - 1],
                device_id=(px, py, pc), device_id_type=MESH)
            cp.start()
            copies.append(cp)
        for cp in copies:
            cp.wait()
        tot = buf[0]
        for k in range(1, NDEV):
            tot = tot + buf[k]
        o_ref[...] = tot

    return pl.pallas_call(
        body, name=name, out_shape=jax.ShapeDtypeStruct((1, PACK), F32),
        scratch_shapes=[pltpu.VMEM((NDEV, 1, PACK), F32), pltpu.SemaphoreType.DMA((NDEV - 1,)),
                        pltpu.SemaphoreType.DMA((NDEV - 1,))],
        compiler_params=pltpu.CompilerParams(vmem_limit_bytes=VMEM_LIMIT),
    )(packed)


def _small_step(name, total, small_w, small_m, small_v):
    names = ["ln1_g", "ln1_b", "ln2_g", "ln2_b", "ln3_g", "ln3_b", "b_in", "attn_sinks", "hgrn_lb_logits", "hgrn_norm_g"]
    np_ = len(names)

    def body(*refs):
        t_ref = refs[0]
        w_refs = refs[1:1 + np_]
        m_refs = refs[1 + np_:1 + 2 * np_]
        v_refs = refs[1 + 2 * np_:1 + 3 * np_]
        loss_ref, outs = refs[1 + 3 * np_], refs[2 + 3 * np_:]

        def part(n):
            o, w = _SMALL_OFF[n]
            return t_ref[:, o:o + w]

        loss_ref[...] = part("loss")
        for i, n in enumerate(names):
            w = w_refs[i][...]
            if n == "hgrn_lb_logits":
                m_ = jnp.maximum(w[0:1, :], w[1:2, :])
                e0, e1 = jnp.exp(w[0:1, :] - m_), jnp.exp(w[1:2, :] - m_)
                p0 = e0 / (e0 + e1)
                t = p0 * (1.0 - p0) * part("lb")
                g = jnp.concatenate([t, -t], axis=0)
            elif n == "attn_sinks":
                g = part(n)[:, :N_Q_HEADS]
            else:
                g = part(n)
            d, mm, vv = _adam(w, g, m_refs[i][...], v_refs[i][...])
            outs[4 * i][...] = g
            outs[4 * i + 1][...] = d
            outs[4 * i + 2][...] = mm
            outs[4 * i + 3][...] = vv

    out_shape = [jax.ShapeDtypeStruct((1, 128), F32)]
    for n in names:
        out_shape += [jax.ShapeDtypeStruct(small_w[n].shape, F32)] * 4
    return pl.pallas_call(
        body, name=name, out_shape=out_shape,
        compiler_params=pltpu.CompilerParams(vmem_limit_bytes=VMEM_LIMIT),
    )(total, *[small_w[n] for n in names], *[small_m[n] for n in names], *[small_v[n] for n in names]), names


def _tile(T, pref):
    return min(T, pref)


def kernel(x, ln1_g, ln1_b, ffn1_w1, ffn1_w3, ffn1_w2, ln2_g, ln2_b, w_in, b_in, attn_sinks, hgrn_lb_logits, hgrn_norm_g, w_proj_attn, w_proj_hgrn, w_out, ln3_g, ln3_b, ffn2_w1, ffn2_w3, ffn2_w2, loss_target, m_ln1_g, m_ln1_b, m_ffn1_w1, m_ffn1_w3, m_ffn1_w2, m_ln2_g, m_ln2_b, m_w_in, m_b_in, m_attn_sinks, m_hgrn_lb_logits, m_hgrn_norm_g, m_w_proj_attn, m_w_proj_hgrn, m_w_out, m_ln3_g, m_ln3_b, m_ffn2_w1, m_ffn2_w3, m_ffn2_w2, v_ln1_g, v_ln1_b, v_ffn1_w1, v_ffn1_w3, v_ffn1_w2, v_ln2_g, v_ln2_b, v_w_in, v_b_in, v_attn_sinks, v_hgrn_lb_logits, v_hgrn_norm_g, v_w_proj_attn, v_w_proj_hgrn, v_w_out, v_ln3_g, v_ln3_b, v_ffn2_w1, v_ffn2_w3, v_ffn2_w2):
    T = x.shape[1]
    xs = x[0]
    tgt = loss_target[0]
    tm = _tile(T, 256)
    tm2 = _tile(T, 512)
    tk = _tile(T, 2048)

    t_bf = lambda w: w[0].T.astype(BF16)
    n_bf = lambda w: w[0].astype(BF16)
    ffn_shard = lambda w1, w3, w2: jnp.concatenate([t_bf(w1), t_bf(w3), n_bf(w2)], axis=0)
    mix_shard = jnp.concatenate([t_bf(w_in), n_bf(w_proj_attn), n_bf(w_proj_hgrn), n_bf(w_out)], axis=0)
    (ffn1_all,) = _exchange_call("gather_ffn1", _gather_exchange(ffn_shard(ffn1_w1, ffn1_w3, ffn1_w2)))
    ffn_offs = (0, RF, 2 * RF)
    ropes = _rope_tables(T)

    (xh1, r1, a1, b1, xb0), (mix_all,) = _ffn_fwd("ffn1_fwd", xs, None, ffn1_all, ffn_offs, tm2,
                                                  carry=_gather_exchange(mix_shard))
    (qkv, kt, vt, hg, gates, y1b), (ffn2_all,) = _inproj_fwd(
        "inproj_fwd", xh1, ln1_g, ln1_b, mix_all, b_in, ropes, tm2,
        carry=_gather_exchange(ffn_shard(ffn2_w1, ffn2_w3, ffn2_w2)))
    ya = _attn_fwd("attn_fwd", qkv, vt, attn_sinks)
    yh, sstart = _hgrn_fwd("hgrn_fwd", hg, hgrn_lb_logits, hgrn_norm_g, HG_FWD)
    xh2, r2, pa, ph, merged = _mix_fwd("mix_fwd", ya, yh, gates, xh1, ln1_g, ln1_b, mix_all, tm2)
    (xh3, r3, a2, b2, y2b, loss_part), _ = _ffn_fwd("ffn2_fwd", xh2, (ln2_g, ln2_b), ffn2_all, ffn_offs, tm2,
                                                    loss=(ln3_g, ln3_b, tgt))

    (dy2, dab2, u2, df2, dg3, db3), _ = _ffn_bwd("ffn2_bwd", ("loss", ln3_b, tgt), xh3, r3, ln3_g, a2, b2, ffn2_all,
                                                 ffn_offs, tm)
    recv = {}
    g_ffn2_13, _ = _wgrad("wgrad_ffn2_w13", dab2, y2b, DFF // 2, tk)
    g_ffn2_2, _ = _wgrad("wgrad_ffn2_w2", u2, df2, DFF // 2, tk)
    (dz2, dmix, dpa, dph, dga, dgh, dya, dyh, dg2, db2), (recv["ffn2_w1"],) = _mix_bwd(
        "mix_bwd", dy2, xh2, r2, ln2_g, gates, pa, ph, mix_all, tm2, carry=_grad_exchange([g_ffn2_13], [0], [RF]))
    g_wo, _ = _wgrad("wgrad_w_out", merged, dmix, D, tk)
    g_pa, _ = _wgrad("wgrad_w_pa", ya, dpa, D, tk)
    g_ph, _ = _wgrad("wgrad_w_ph", yh, dph, D, tk)
    (dq, dk, dv, dsink), (recv["ffn2_w3"],) = _attn_bwd(
        "attn_bwd", qkv, kt, attn_sinks, dya, carry=_grad_exchange([g_ffn2_13], [DFF], [RF]))
    (dfl, dqh, dih, dog, dlb, dng), (recv["ffn2_w2"], recv["w_proj_attn"], recv["w_proj_hgrn"], recv["w_out"]) = \
        _hgrn_bwd("hgrn_bwd", hg, hgrn_lb_logits, hgrn_norm_g, sstart, dyh, HG_BWD,
                  carry=_grad_exchange([g_ffn2_2, g_pa, g_ph, g_wo], [0, 0, 0, 0], [RF, RP, RP, RP]))
    dy1, dproj, dbin = _inproj_bwd("inproj_bwd", dq, dk, dv, (dfl, dqh, dih, dog), (dga, dgh), dz2, mix_all, ropes, tm)
    g_win, _ = _wgrad("wgrad_w_in", dproj, y1b, DIN // 4, tk)
    win_rows = (400, 288, 272)
    win_base = (0, 400, 688)
    (dz1, dab1, u1, df1, dg1, db1), (rw0,) = _ffn_bwd(
        "ffn1_bwd", ("dy", dy1), xh1, r1, ln1_g, a1, b1, ffn1_all, ffn_offs, tm, with_dx=False,
        carry=_grad_exchange([g_win], win_base[:1], win_rows[:1], [RIN]))
    g_ffn1_2, (rw1,) = _wgrad("wgrad_ffn1_w2", u1, df1, DFF // 2, tk,
                              carry=_grad_exchange([g_win], win_base[1:2], win_rows[1:2], [RIN]))
    g_ffn1_1, (recv["ffn1_w2"],) = _wgrad("wgrad_ffn1_w1", dab1, xb0, DFF // 2, tk, 0, DFF,
                                          carry=_grad_exchange([g_ffn1_2], [0], [RF]))
    g_ffn1_3, (recv["ffn1_w1"],) = _wgrad("wgrad_ffn1_w3", dab1, xb0, DFF // 2, tk, DFF, DFF,
                                          carry=_grad_exchange([g_ffn1_1], [0], [RF]))
    gx, (recv["ffn1_w3"], rw2) = _ffn_dx(
        "ffn1_dx", dab1, dz1, ffn1_all, ffn_offs, tm2,
        carry=_grad_exchange([g_ffn1_3, g_win], [0, win_base[2]], [RF, win_rows[2]], [RF, RIN]))
    recv["w_in"] = [rw0, rw1, rw2]

    big = [("ffn1_w1", ffn1_w1, m_ffn1_w1, v_ffn1_w1, True), ("ffn1_w3", ffn1_w3, m_ffn1_w3, v_ffn1_w3, True),
           ("ffn1_w2", ffn1_w2, m_ffn1_w2, v_ffn1_w2, False), ("w_in", w_in, m_w_in, v_w_in, True),
           ("w_proj_attn", w_proj_attn, m_w_proj_attn, v_w_proj_attn, False),
           ("w_proj_hgrn", w_proj_hgrn, m_w_proj_hgrn, v_w_proj_hgrn, False),
           ("w_out", w_out, m_w_out, v_w_out, False),
           ("ffn2_w1", ffn2_w1, m_ffn2_w1, v_ffn2_w1, True), ("ffn2_w3", ffn2_w3, m_ffn2_w3, v_ffn2_w3, True),
           ("ffn2_w2", ffn2_w2, m_ffn2_w2, v_ffn2_w2, False)]
    view = lambda t, transposed: t[0].T if transposed else t[0]
    back = lambda t, transposed: t.T[None] if transposed else t[None]
    slots = lambda n: recv[n] if isinstance(recv[n], list) else [recv[n]]
    stepped, _ = _grad_steps("steps", [(slots(n), view(w, tr), view(m, tr), view(v, tr)) for n, w, m, v, tr in big], 128)
    res = {n: tuple(back(t, tr) for t in outs) for (n, _, _, _, tr), outs in zip(big, stepped)}

    parts = {"ln1_g": dg1, "ln1_b": db1, "ln2_g": dg2, "ln2_b": db2, "ln3_g": dg3, "ln3_b": db3, "b_in": dbin,
             "lb": dlb, "attn_sinks": dsink, "hgrn_norm_g": jnp.sum(dng, axis=0), "loss": loss_part[0:1, :]}
    packed = jnp.concatenate([parts[n] for n, _ in _SMALL], axis=1)
    small_w = dict(ln1_g=ln1_g, ln1_b=ln1_b, ln2_g=ln2_g, ln2_b=ln2_b, ln3_g=ln3_g, ln3_b=ln3_b, b_in=b_in,
                   attn_sinks=attn_sinks, hgrn_lb_logits=hgrn_lb_logits, hgrn_norm_g=hgrn_norm_g)
    small_m = dict(ln1_g=m_ln1_g, ln1_b=m_ln1_b, ln2_g=m_ln2_g, ln2_b=m_ln2_b, ln3_g=m_ln3_g, ln3_b=m_ln3_b,
                   b_in=m_b_in, attn_sinks=m_attn_sinks, hgrn_lb_logits=m_hgrn_lb_logits, hgrn_norm_g=m_hgrn_norm_g)
    small_v = dict(ln1_g=v_ln1_g, ln1_b=v_ln1_b, ln2_g=v_ln2_g, ln2_b=v_ln2_b, ln3_g=v_ln3_g, ln3_b=v_ln3_b,
                   b_in=v_b_in, attn_sinks=v_attn_sinks, hgrn_lb_logits=v_hgrn_lb_logits, hgrn_norm_g=v_hgrn_norm_g)
    outs, names = _small_step("small_step", _small_reduce("small_reduce", packed), small_w, small_m, small_v)
    loss = outs[0][0, 0]
    for i, n in enumerate(names):
        res[n] = tuple(outs[1 + 4 * i:5 + 4 * i])

    order = ["ln1_g", "ln1_b", "ffn1_w1", "ffn1_w3", "ffn1_w2", "ln2_g", "ln2_b", "w_in", "b_in", "attn_sinks",
             "hgrn_lb_logits", "hgrn_norm_g", "w_proj_attn", "w_proj_hgrn", "w_out", "ln3_g", "ln3_b",
             "ffn2_w1", "ffn2_w3", "ffn2_w2"]
    return (loss, gx[None], *[res[n][0] for n in order], *[res[n][1] for n in order],
            *[res[n][2] for n in order], *[res[n][3] for n in order])
```

```python
import jax
import jax.numpy as jnp
from jax import lax
from jax.experimental import pallas as pl
from jax.experimental.pallas import tpu as pltpu

F32 = jnp.float32
BF16 = jnp.bfloat16

NDEV = 8
D = 1024
DFF = 2816
RF = DFF // NDEV
DIN = 7680
RIN = DIN // NDEV
RP = D // NDEV
N_Q_HEADS = 16
N_KV_HEADS = 4
HEAD_DIM = 64
ATTN_BLOCK = 128
ROPE_THETA = 500000.0
ROPE_DIM = HEAD_DIM // 4
HGRN_HEADS = 8
HGRN_CHUNK = 64
ALPHA = 2.0 ** 0.25
LN_EPS = 1e-5
RMS_EPS = 1e-6
NEG_INF = -1e30
ADAM_LR = 0.001
ADAM_B1 = 0.9
ADAM_B2 = 0.999
ADAM_EPS = 1e-08
ADAM_WD = 0.01
ADAM_STEP = 10

QKV_W = 1536
HG_W = 4096
GATE_W = 2048
VMEM_LIMIT = 60 * 2 ** 20
PART_ROWS = 256
MESH = pl.DeviceIdType.MESH
HBM_SPEC = pl.BlockSpec(memory_space=pltpu.HBM)


def _params(*sem):
    return pltpu.CompilerParams(dimension_semantics=sem, vmem_limit_bytes=VMEM_LIMIT)


def _dot(a, b):
    return jnp.dot(a, b, preferred_element_type=F32)


def _dot_nt(a, b):
    return lax.dot_general(a, b, (((1,), (1,)), ((), ())), preferred_element_type=F32)


def _dot_tn(a, b):
    return lax.dot_general(a, b, (((0,), (0,)), ((), ())), preferred_element_type=F32)


def _sigmoid(x):
    return 0.5 * jnp.tanh(0.5 * x) + 0.5


def _ln_fwd(z):
    mu = jnp.mean(z, axis=-1, keepdims=True)
    zc = z - mu
    var = jnp.mean(zc * zc, axis=-1, keepdims=True)
    r = lax.rsqrt(var + LN_EPS)
    return zc * r, r


def _ln_bwd(dy, xh, r, g):
    dxh = dy * g
    m1 = jnp.mean(dxh, axis=-1, keepdims=True)
    m2 = jnp.mean(dxh * xh, axis=-1, keepdims=True)
    dz = r * (dxh - m1 - xh * m2)
    return dz, jnp.sum(dy * xh, axis=0, keepdims=True), jnp.sum(dy, axis=0, keepdims=True)


def _load_rows(wall_ref, pieces, sem):
    copies = []
    for dst, off, r in pieces:
        for k in range(NDEV):
            c = pltpu.make_async_copy(wall_ref.at[k, pl.ds(off, r), :], dst.at[pl.ds(k * r, r), :], sem.at[len(copies)])
            c.start()
            copies.append(c)
    for c in copies:
        c.wait()


class _Exchange:
    def __init__(self, inputs, out_shape, scratch, begin, middle, end):
        self.inputs, self.out_shape, self.scratch = inputs, out_shape, scratch
        self.begin, self.middle, self.end = begin, middle, end


def _gather_exchange(shard):
    rows, cols = shard.shape

    def ops(ins, outs, scr):
        (x_ref,), (out_ref,), (send_sems, recv_sems, local_sem) = ins, outs, scr
        x, y, c = lax.axis_index("x"), lax.axis_index("y"), lax.axis_index("c")
        me, sibling = (x, y, c), (x, y, 1 - c)
        chips = [(1 - x, y), (x, 1 - y), (1 - x, 1 - y)]

        def slot(px, py, pc):
            return out_ref.at[4 * px + 2 * py + pc]

        def copy(k, block, to, src=None):
            return pltpu.make_async_remote_copy(
                src_ref=slot(*block) if src is None else src, dst_ref=slot(*block),
                send_sem=send_sems.at[k], recv_sem=recv_sems.at[k], device_id=to, device_id_type=MESH)

        mine = lambda: pltpu.make_async_copy(x_ref, slot(*me), local_sem)
        first = lambda: [copy(0, me, sibling, src=x_ref)] + [
            copy(1 + j, me, (*chip, c), src=x_ref) for j, chip in enumerate(chips)]
        passed = lambda: [copy(4 + j, (*chip, c), sibling) for j, chip in enumerate(chips)]
        return c, me, sibling, chips, copy, mine, first, passed

    def begin(*refs):
        _, _, _, _, _, mine, first, _ = ops(*refs)
        mine().start()
        for cp in first():
            cp.start()

    def middle(*refs):
        c, me, _, chips, copy, _, _, passed = ops(*refs)
        for (j, chip), fwd in zip(enumerate(chips), passed()):
            copy(1 + j, (*chip, c), me).wait_recv()
            fwd.start()

    def end(*refs):
        c, me, sibling, chips, copy, mine, first, passed = ops(*refs)
        copy(0, sibling, me).wait_recv()
        for j, chip in enumerate(chips):
            copy(4 + j, (*chip, 1 - c), me).wait_recv()
        for cp in first() + passed():
            cp.wait_send()
        mine().wait()

    return _Exchange([shard], [jax.ShapeDtypeStruct((NDEV, rows, cols), shard.dtype)],
                     [pltpu.SemaphoreType.DMA((7,)), pltpu.SemaphoreType.DMA((7,)), pltpu.SemaphoreType.DMA],
                     begin, middle, end)


def _grad_exchange(grads, bases, rows, strides=None):
    n = len(grads)
    strides = rows if strides is None else strides

    def copies(g_refs, out_refs, scr):
        send_sems, recv_sems, local_sems = scr
        x, y, c = lax.axis_index("x"), lax.axis_index("y"), lax.axis_index("c")
        me = 4 * x + 2 * y + c
        out = []
        for i in range(n):
            r = rows[i]
            src = lambda k: g_refs[i].at[pl.ds(pl.multiple_of(bases[i] + k * strides[i], 16), r), :]
            out.append(pltpu.make_async_copy(src(me), out_refs[i].at[me], local_sems.at[i]))
            for j in range(1, NDEV):
                px, py, pc = x ^ (j >> 2), y ^ ((j >> 1) & 1), c ^ (j & 1)
                out.append(pltpu.make_async_remote_copy(
                    src_ref=src(4 * px + 2 * py + pc), dst_ref=out_refs[i].at[me],
                    send_sem=send_sems.at[i, j - 1], recv_sem=recv_sems.at[i, j - 1],
                    device_id=(px, py, pc), device_id_type=MESH))
        return out

    def begin(*refs):
        for cp in copies(*refs):
            cp.start()

    def end(*refs):
        for cp in copies(*refs):
            cp.wait()

    return _Exchange(list(grads), [jax.ShapeDtypeStruct((NDEV, r, g.shape[1]), g.dtype) for g, r in zip(grads, rows)],
                     [pltpu.SemaphoreType.DMA((n, NDEV - 1)), pltpu.SemaphoreType.DMA((n, NDEV - 1)),
                      pltpu.SemaphoreType.DMA((n,))], begin, None, end)


def _row_gather_exchange(row):
    def copies(ins, outs, scr):
        (r_ref,), (o_ref,), (send_sems, recv_sems, local_sem) = ins, outs, scr
        x, y, c = lax.axis_index("x"), lax.axis_index("y"), lax.axis_index("c")
        me = 4 * x + 2 * y + c
        out = [pltpu.make_async_copy(r_ref, o_ref.at[me], local_sem)]
        for j in range(1, NDEV):
            out.append(pltpu.make_async_remote_copy(
                src_ref=r_ref, dst_ref=o_ref.at[me], send_sem=send_sems.at[j - 1], recv_sem=recv_sems.at[j - 1],
                device_id=(x ^ (j >> 2), y ^ ((j >> 1) & 1), c ^ (j & 1)), device_id_type=MESH))
        return out

    def begin(*refs):
        for cp in copies(*refs):
            cp.start()

    def end(*refs):
        for cp in copies(*refs):
            cp.wait()

    return _Exchange([row], [jax.ShapeDtypeStruct((NDEV,) + row.shape, row.dtype)],
                     [pltpu.SemaphoreType.DMA((NDEV - 1,)), pltpu.SemaphoreType.DMA((NDEV - 1,)), pltpu.SemaphoreType.DMA],
                     begin, None, end)


def _join(a, b):
    na, oa, sa = len(a.inputs), len(a.out_shape), len(a.scratch)
    split = lambda ins, outs, scr: ((ins[:na], outs[:oa], scr[:sa]), (ins[na:], outs[oa:], scr[sa:]))

    def begin(*refs):
        pa, pb = split(*refs)
        a.begin(*pa)
        b.begin(*pb)

    def end(*refs):
        pa, pb = split(*refs)
        a.end(*pa)
        b.end(*pb)

    return _Exchange(a.inputs + b.inputs, a.out_shape + b.out_shape, a.scratch + b.scratch, begin, None, end)


def _exchange_call(name, ex):
    ni, no = len(ex.inputs), len(ex.out_shape)

    def body(*refs):
        parts = (refs[:ni], refs[ni:ni + no], refs[ni + no:])
        ex.begin(*parts)
        if ex.middle is not None:
            ex.middle(*parts)
        ex.end(*parts)

    return pl.pallas_call(body, name=name, out_shape=ex.out_shape, in_specs=[HBM_SPEC] * ni, out_specs=[HBM_SPEC] * no,
                          scratch_shapes=ex.scratch)(*ex.inputs)


def _call(body, *, name, grid, in_specs, out_specs, out_shape, scratch_shapes, ins, carry=None):
    sem = ("arbitrary",) * len(grid)
    if carry is None:
        outs = pl.pallas_call(body, name=name, grid=grid, in_specs=in_specs, out_specs=out_specs, out_shape=out_shape,
                              scratch_shapes=scratch_shapes, compiler_params=_params(*sem))(*ins)
        return outs, None
    n_in, n_out, n_scr = len(ins), len(out_shape), len(scratch_shapes)
    ci, co = len(carry.inputs), len(carry.out_shape)
    total = 1
    for g in grid:
        total *= g

    def wrapped(*refs):
        own_in, ex_in = refs[:n_in], refs[n_in:n_in + ci]
        o0 = n_in + ci
        own_out, ex_out = refs[o0:o0 + n_out], refs[o0 + n_out:o0 + n_out + co]
        s0 = o0 + n_out + co
        own_scr, ex_scr = refs[s0:s0 + n_scr], refs[s0 + n_scr:]
        step = pl.program_id(0)
        for d in range(1, len(grid)):
            step = step * grid[d] + pl.program_id(d)
        parts = (ex_in, ex_out, ex_scr)
        pl.when(step == 0)(lambda: carry.begin(*parts))
        body(*own_in, *own_out, *own_scr)
        if carry.middle is not None:
            pl.when(step == (3 * total) // 4)(lambda: carry.middle(*parts))
        pl.when(step == total - 1)(lambda: carry.end(*parts))

    outs = pl.pallas_call(
        wrapped, name=name, grid=grid, in_specs=list(in_specs) + [HBM_SPEC] * ci,
        out_specs=list(out_specs) + [HBM_SPEC] * co, out_shape=list(out_shape) + list(carry.out_shape),
        scratch_shapes=list(scratch_shapes) + list(carry.scratch), compiler_params=_params(*sem),
    )(*ins, *carry.inputs)
    return outs[:n_out], outs[n_out:]


def _ffn_fwd(name, xin, affine, wall, offs, tm, loss=None, carry=None):
    T = xin.shape[0]
    nt = T // tm

    def body(*refs):
        it = iter(refs)
        x_ref = next(it)
        if affine is not None:
            g_ref, b_ref = next(it), next(it)
        wall_ref = next(it)
        if loss is not None:
            go_ref, bo_ref, tgt_ref = next(it), next(it), next(it)
        xh_ref, r_ref, a_ref, b2_ref, yb_ref = (next(it) for _ in range(5))
        if loss is not None:
            loss_ref = next(it)
        w1, w3, w2, sem = (next(it) for _ in range(4))

        @pl.when(pl.program_id(0) == 0)
        def _():
            _load_rows(wall_ref, [(w1, offs[0], RF), (w3, offs[1], RF), (w2, offs[2], RF)], sem)
            if loss is not None:
                loss_ref[...] = jnp.zeros_like(loss_ref)

        parts = [slice(p, min(p + PART_ROWS, tm)) for p in range(0, tm, PART_ROWS)]
        ys, ybs = [], []
        for rs in parts:
            y = x_ref[rs, :]
            if affine is not None:
                y = y * g_ref[...] + b_ref[...]
            ys.append(y)
            ybs.append(y.astype(BF16))
            yb_ref[rs, :] = ybs[-1]
        ab = [(_dot_nt(yb, w1[...]).astype(BF16), _dot_nt(yb, w3[...]).astype(BF16)) for yb in ybs]
        us = []
        for rs, (a, b) in zip(parts, ab):
            a_ref[rs, :] = a
            b2_ref[rs, :] = b
            af, bf = a.astype(F32), b.astype(F32)
            us.append((af * _sigmoid(af) * bf).astype(BF16))
        fs = [_dot(u, w2[...]) for u in us]
        for rs, y, f in zip(parts, ys, fs):
            xh, r = _ln_fwd(ALPHA * y + 0.5 * f)
            xh_ref[rs, :] = xh
            r_ref[rs, :] = jnp.broadcast_to(r, (xh.shape[0], 128))
            if loss is not None:
                e = xh * go_ref[...] + bo_ref[...] - tgt_ref[rs, :]
                loss_ref[...] += jnp.sum(e * e) * (0.5 / D)

    row = lambda w: pl.BlockSpec((tm, w), lambda i: (i, 0))
    vec = pl.BlockSpec((1, D), lambda i: (0, 0))
    ins, in_specs = [xin], [row(D)]
    if affine is not None:
        ins += list(affine)
        in_specs += [vec, vec]
    ins.append(wall)
    in_specs.append(HBM_SPEC)
    if loss is not None:
        ins += list(loss)
        in_specs += [vec, vec, row(D)]
    out_shape = [jax.ShapeDtypeStruct((T, D), F32), jax.ShapeDtypeStruct((T, 128), F32),
                 jax.ShapeDtypeStruct((T, DFF), BF16), jax.ShapeDtypeStruct((T, DFF), BF16),
                 jax.ShapeDtypeStruct((T, D), BF16)]
    out_specs = [row(D), row(128), row(DFF), row(DFF), row(D)]
    if loss is not None:
        out_shape.append(jax.ShapeDtypeStruct((8, 128), F32))
        out_specs.append(pl.BlockSpec((8, 128), lambda i: (0, 0)))
    return _call(body, name=name, grid=(nt,), in_specs=in_specs, out_specs=out_specs, out_shape=out_shape,
                 scratch_shapes=[pltpu.VMEM((DFF, D), BF16)] * 3 + [pltpu.SemaphoreType.DMA((3 * NDEV,))],
                 ins=ins, carry=carry)


def _ffn_bwd(name, dy_src, xh, r, g, a, b, wall, offs, tm, with_dx=True, carry=None):
    T = xh.shape[0]
    nt = T // tm
    from_loss = dy_src[0] == "loss"

    def body(*refs):
        it = iter(refs)
        if from_loss:
            bo_ref, tgt_ref = next(it), next(it)
        else:
            dy_ref = next(it)
        xh_ref, r_ref, g_ref, a_ref, b_ref, wall_ref = (next(it) for _ in range(6))
        dyin_ref, dab_ref, u_ref, df_ref, dg_ref, db_ref = (next(it) for _ in range(6))
        if with_dx:
            w1, w3 = next(it), next(it)
        w2, sem = next(it), next(it)

        @pl.when(pl.program_id(0) == 0)
        def _():
            _load_rows(wall_ref, ([(w1, offs[0], RF), (w3, offs[1], RF)] if with_dx else []) + [(w2, offs[2], RF)], sem)
            dg_ref[...] = jnp.zeros_like(dg_ref)
            db_ref[...] = jnp.zeros_like(db_ref)

        parts = [slice(p, min(p + PART_ROWS, tm)) for p in range(0, tm, PART_ROWS)]
        gv = g_ref[...]
        dz, df = [], []
        for rs in parts:
            xhv = xh_ref[rs, :]
            if from_loss:
                dy = (xhv * gv + bo_ref[...] - tgt_ref[rs, :]) * (1.0 / D)
            else:
                dy = dy_ref[rs, :]
            dzp, dgp, dbp = _ln_bwd(dy, xhv, r_ref[rs, :1], gv)
            dg_ref[...] += dgp
            db_ref[...] += dbp
            dz.append(dzp)
            df.append((0.5 * dzp).astype(BF16))
            df_ref[rs, :] = df[-1]
        du = [_dot_nt(d, w2[...]) for d in df]
        da, dbb = [], []
        for rs, dup in zip(parts, du):
            af, bf = a_ref[rs, :].astype(F32), b_ref[rs, :].astype(F32)
            s = _sigmoid(af)
            sl = af * s
            u_ref[rs, :] = (sl * bf).astype(BF16)
            da.append((dup * bf * (s * (1.0 + af * (1.0 - s)))).astype(BF16))
            dbb.append((dup * sl).astype(BF16))
            dab_ref[rs, :DFF] = da[-1]
            dab_ref[rs, DFF:] = dbb[-1]
        for rs, dzp, dap, dbp in zip(parts, dz, da, dbb):
            dyin_ref[rs, :] = ALPHA * dzp + _dot(dap, w1[...]) + _dot(dbp, w3[...]) if with_dx else dzp

    row = lambda w: pl.BlockSpec((tm, w), lambda i: (i, 0))
    vec = pl.BlockSpec((1, D), lambda i: (0, 0))
    if from_loss:
        ins, in_specs = [dy_src[1], dy_src[2]], [vec, row(D)]
    else:
        ins, in_specs = [dy_src[1]], [row(D)]
    ins += [xh, r, g, a, b, wall]
    in_specs += [row(D), row(128), vec, row(DFF), row(DFF), HBM_SPEC]
    return _call(
        body, name=name, grid=(nt,), in_specs=in_specs,
        out_specs=[row(D), row(2 * DFF), row(DFF), row(D), vec, vec],
        out_shape=[jax.ShapeDtypeStruct((T, D), F32), jax.ShapeDtypeStruct((T, 2 * DFF), BF16),
                   jax.ShapeDtypeStruct((T, DFF), BF16), jax.ShapeDtypeStruct((T, D), BF16),
                   jax.ShapeDtypeStruct((1, D), F32), jax.ShapeDtypeStruct((1, D), F32)],
        scratch_shapes=[pltpu.VMEM((DFF, D), BF16)] * (3 if with_dx else 1) + [pltpu.SemaphoreType.DMA((3 * NDEV,))],
        ins=ins, carry=carry)


def _ffn_dx(name, dab, dz, wall, offs, tm, carry=None):
    T = dz.shape[0]

    def body(dab_ref, dz_ref, wall_ref, o_ref, w13, sem):
        @pl.when(pl.program_id(0) == 0)
        def _():
            _load_rows(wall_ref, [(w13.at[pl.ds(0, DFF), :], offs[0], RF), (w13.at[pl.ds(DFF, DFF), :], offs[1], RF)], sem)

        for p in range(0, tm, PART_ROWS):
            rs = slice(p, min(p + PART_ROWS, tm))
            o_ref[rs, :] = ALPHA * dz_ref[rs, :] + _dot(dab_ref[rs, :], w13[...])

    row = lambda w: pl.BlockSpec((tm, w), lambda i: (i, 0))
    (out,), ex = _call(
        body, name=name, grid=(T // tm,), in_specs=[row(2 * DFF), row(D), HBM_SPEC], out_specs=[row(D)],
        out_shape=[jax.ShapeDtypeStruct((T, D), F32)],
        scratch_shapes=[pltpu.VMEM((2 * DFF, D), BF16), pltpu.SemaphoreType.DMA((2 * NDEV,))],
        ins=[dab, dz, wall], carry=carry)
    return out, ex


def _wgrad(name, a, b, bn, tk, col0=0, ncols=None, carry=None):
    T = a.shape[0]
    N = a.shape[1] if ncols is None else ncols
    nk = T // tk
    c0 = col0 // bn

    def body(a_ref, b_ref, o_ref, acc):
        k = pl.program_id(1)

        @pl.when(k == 0)
        def _():
            acc[...] = jnp.zeros_like(acc)

        acc[...] += _dot_tn(a_ref[...], b_ref[...])

        @pl.when(k == nk - 1)
        def _():
            o_ref[...] = acc[...].astype(BF16)

    (out,), ex = _call(
        body, name=name, grid=(N // bn, nk),
        in_specs=[pl.BlockSpec((tk, bn), lambda n, k: (k, n + c0)), pl.BlockSpec((tk, D), lambda n, k: (k, 0))],
        out_specs=[pl.BlockSpec((bn, D), lambda n, k: (n, 0))],
        out_shape=[jax.ShapeDtypeStruct((N, D), BF16)],
        scratch_shapes=[pltpu.VMEM((bn, D), F32)], ins=[a, b], carry=carry)
    return out, ex


def _rope_tables(T):
    pos = jnp.arange(T, dtype=F32)
    inv_freq = ROPE_THETA ** (-jnp.arange(0, ROPE_DIM, 2, dtype=F32) / ROPE_DIM)
    half = ROPE_DIM // 2
    ch = jnp.arange(128) % HEAD_DIM
    ang = pos[:, None] * inv_freq[ch % half][None, :]
    cos, sin = jnp.cos(ang), jnp.sin(ang)
    first, second = (ch < half)[None, :], ((ch >= half) & (ch < ROPE_DIM))[None, :]
    return (jnp.where(first | second, cos, 1.0), jnp.where(first, -sin, 0.0), jnp.where(second, sin, 0.0))


def _rope(t, c, s1, s2):
    n = t.shape[1] // 128
    ct, s1t, s2t = (jnp.tile(v, (1, n)) for v in (c, s1, s2))
    w = t.shape[1]
    return t * ct + pltpu.roll(t, w - 8, 1) * s1t + pltpu.roll(t, 8, 1) * s2t


def _rope_t(dr, c, s1, s2):
    n = dr.shape[1] // 128
    ct, s1t, s2t = (jnp.tile(v, (1, n)) for v in (c, s1, s2))
    w = dr.shape[1]
    return dr * ct + pltpu.roll(dr * s1t, 8, 1) + pltpu.roll(dr * s2t, w - 8, 1)


_Q, _K, _V = (0, 1024), (1024, 256), (1280, 256)
_HG = (1536, HG_W)
_GATES = (5632, GATE_W)


def _inproj_fwd(name, xh, g, b, wall, b_in, ropes, tm, carry=None):
    T = xh.shape[0]

    def body(xh_ref, g_ref, b_ref, wall_ref, bin_ref, c_ref, s1_ref, s2_ref,
             qkv_ref, kt_ref, vt_ref, hg_ref, gate_ref, yb_ref, w, sem):
        @pl.when(pl.program_id(0) == 0)
        def _():
            _load_rows(wall_ref, [(w, 0, RIN)], sem)

        yb = (xh_ref[...] * g_ref[...] + b_ref[...]).astype(BF16)
        yb_ref[...] = yb
        c, s1, s2 = c_ref[...], s1_ref[...], s2_ref[...]

        def piece(start, width):
            return _dot_nt(yb, w[start:start + width, :]) + bin_ref[:, start:start + width]

        k = _rope(piece(*_K), c, s1, s2)
        v = piece(*_V)
        qkv_ref[:, 0:1024] = _rope(piece(*_Q), c, s1, s2).astype(BF16)
        qkv_ref[:, 1024:1280] = k.astype(BF16)
        qkv_ref[:, 1280:1536] = v.astype(BF16)
        kt_ref[...] = k.T.astype(BF16)
        vt_ref[...] = v.T.astype(BF16)
        for j in range(4):
            hg_ref[:, 1024 * j:1024 * (j + 1)] = piece(_HG[0] + 1024 * j, 1024)
        for j in range(2):
            gate_ref[:, 1024 * j:1024 * (j + 1)] = piece(_GATES[0] + 1024 * j, 1024)

    row = lambda wd: pl.BlockSpec((tm, wd), lambda i: (i, 0))
    vec = lambda wd: pl.BlockSpec((1, wd), lambda i: (0, 0))
    colt = pl.BlockSpec((256, tm), lambda i: (0, i))
    return _call(
        body, name=name, grid=(T // tm,),
        in_specs=[row(D), vec(D), vec(D), HBM_SPEC, vec(DIN), row(128), row(128), row(128)],
        out_specs=[row(QKV_W), colt, colt, row(HG_W), row(GATE_W), row(D)],
        out_shape=[jax.ShapeDtypeStruct((T, QKV_W), BF16), jax.ShapeDtypeStruct((256, T), BF16),
                   jax.ShapeDtypeStruct((256, T), BF16), jax.ShapeDtypeStruct((T, HG_W), F32),
                   jax.ShapeDtypeStruct((T, GATE_W), F32), jax.ShapeDtypeStruct((T, D), BF16)],
        scratch_shapes=[pltpu.VMEM((DIN, D), BF16), pltpu.SemaphoreType.DMA((NDEV,))],
        ins=[xh, g, b, wall, b_in, *ropes], carry=carry)


def _inproj_bwd(name, dq, dk, dv, dhg, dgates, dz2, wall, ropes, tm):
    T = dq.shape[0]

    def body(dq_ref, dk_ref, dv_ref, d0, d1, d2, d3, dga_ref, dgh_ref, dz_ref, wall_ref, c_ref, s1_ref, s2_ref,
             dy_ref, dproj_ref, dbin_ref, w, sem):
        @pl.when(pl.program_id(0) == 0)
        def _():
            _load_rows(wall_ref, [(w, 0, RIN)], sem)
            dbin_ref[...] = jnp.zeros_like(dbin_ref)

        c, s1, s2 = c_ref[...], s1_ref[...], s2_ref[...]
        acc = ALPHA * dz_ref[...]
        pieces = [(_Q[0], _rope_t(dq_ref[...], c, s1, s2)), (_K[0], _rope_t(dk_ref[...], c, s1, s2)),
                  (_V[0], dv_ref[...])]
        pieces += [(_HG[0] + 1024 * j, r[...]) for j, r in enumerate((d0, d1, d2, d3))]
        pieces += [(_GATES[0], dga_ref[...]), (_GATES[0] + 1024, dgh_ref[...])]
        for start, val in pieces:
            width = val.shape[1]
            dbin_ref[:, start:start + width] += jnp.sum(val.astype(F32), axis=0, keepdims=True)
            vb = val.astype(BF16)
            dproj_ref[:, start:start + width] = vb
            acc = acc + _dot(vb, w[start:start + width, :])
        dy_ref[...] = acc

    row = lambda wd: pl.BlockSpec((tm, wd), lambda i: (i, 0))
    return pl.pallas_call(
        body, name=name, grid=(T // tm,),
        in_specs=[row(D), row(256), row(256)] + [row(D)] * 4 + [row(D), row(D), row(D), HBM_SPEC] + [row(128)] * 3,
        out_specs=[row(D), row(DIN), pl.BlockSpec((1, DIN), lambda i: (0, 0))],
        out_shape=[jax.ShapeDtypeStruct((T, D), F32), jax.ShapeDtypeStruct((T, DIN), BF16),
                   jax.ShapeDtypeStruct((1, DIN), F32)],
        scratch_shapes=[pltpu.VMEM((DIN, D), BF16), pltpu.SemaphoreType.DMA((NDEV,))],
        compiler_params=_params("arbitrary"),
    )(dq, dk, dv, *dhg, *dgates, dz2, wall, *ropes)


def _halves(t):
    lane = lax.broadcasted_iota(jnp.int32, t.shape, 1)
    low = lane < HEAD_DIM
    sw = pltpu.roll(t, HEAD_DIM, 1)
    zero = jnp.zeros_like(t)
    h0 = (jnp.where(low, t, zero), jnp.where(low, zero, sw))
    h1 = (jnp.where(low, sw, zero), jnp.where(low, zero, t))
    return h0, h1


def _lane_stack(p_ref, c_ref, gp):
    sl = slice(128 * gp, 128 * (gp + 1))
    hp, hc = _halves(p_ref[:, sl].astype(F32)), _halves(c_ref[:, sl].astype(F32))
    return [jnp.concatenate([hp[gg][0], hc[gg][0], hp[gg][1], hc[gg][1]], axis=0).astype(BF16) for gg in range(2)]


def _row_stack(tp_ref, tc_ref, g):
    band = jnp.concatenate([tp_ref[64 * g:64 * (g + 1), :], tc_ref[64 * g:64 * (g + 1), :]], axis=1)
    z = jnp.zeros_like(band)
    return [jnp.concatenate([band, z], axis=0), jnp.concatenate([z, band], axis=0)]


def _fold(prevm, t4, hh):
    return jnp.where(prevm, t4[256 * hh:256 * hh + 128, :], t4[256 * hh + 128:256 * hh + 256, :])


def _unfold(prevm, t):
    return jnp.concatenate([jnp.where(prevm, t, 0.0), jnp.where(prevm, 0.0, t)], axis=0).astype(BF16)


def _attn_softmax(s, kill, sink):
    s = jnp.where(kill, NEG_INF, s)
    m = jnp.maximum(jnp.max(s, axis=0, keepdims=True), sink)
    p = jnp.exp(s - m)
    es = jnp.exp(sink - m)
    inv = 1.0 / (jnp.sum(p, axis=0, keepdims=True) + es)
    return p * inv, es * inv


def _attn_masks(first):
    row = lax.broadcasted_iota(jnp.int32, (ATTN_BLOCK, 2 * ATTN_BLOCK), 0)
    lane = lax.broadcasted_iota(jnp.int32, (ATTN_BLOCK, 2 * ATTN_BLOCK), 1)
    prevm = row > lane % ATTN_BLOCK
    return prevm, jnp.logical_and(first, prevm)


def _pair_rows(ref, g):
    return jnp.concatenate([ref[:, 256 * g:256 * g + 128], ref[:, 256 * g + 128:256 * (g + 1)]], axis=0)


def _pair_sinks(sink_ref, g, hh):
    h0, h1 = 4 * g + hh, 4 * g + 2 + hh
    return jnp.concatenate([jnp.broadcast_to(sink_ref[:, h0:h0 + 1], (1, ATTN_BLOCK)),
                            jnp.broadcast_to(sink_ref[:, h1:h1 + 1], (1, ATTN_BLOCK))], axis=1)


def _attn_fwd(name, qkv, vt, sinks):
    T = qkv.shape[0]
    nb = T // ATTN_BLOCK
    scale = HEAD_DIM ** -0.5

    def body(q_ref, kp_ref, kc_ref, vtp_ref, vtc_ref, sink_ref, o_ref):
        prevm, kill = _attn_masks(pl.program_id(0) == 0)
        kst = _lane_stack(kp_ref, kc_ref, 0) + _lane_stack(kp_ref, kc_ref, 1)
        vts = [_row_stack(vtp_ref, vtc_ref, g) for g in range(N_KV_HEADS)]
        s8 = [_dot_nt(kst[g], _pair_rows(q_ref, g)) * scale for g in range(N_KV_HEADS)]
        pu = [[_unfold(prevm, _attn_softmax(_fold(prevm, s8[g], hh), kill, _pair_sinks(sink_ref, g, hh))[0])
               for hh in range(2)] for g in range(N_KV_HEADS)]
        for g in range(N_KV_HEADS):
            ot = _dot(vts[g][0], pu[g][0]) + _dot(vts[g][1], pu[g][1])
            o_ref[:, 256 * g:256 * g + 128] = ot[:, :ATTN_BLOCK].T.astype(BF16)
            o_ref[:, 256 * g + 128:256 * (g + 1)] = ot[:, ATTN_BLOCK:].T.astype(BF16)

    prev = lambda i: jnp.maximum(i - 1, 0)
    return pl.pallas_call(
        body, name=name, grid=(nb,),
        in_specs=[pl.BlockSpec((ATTN_BLOCK, D), lambda i: (i, 0)),
                  pl.BlockSpec((ATTN_BLOCK, 256), lambda i: (prev(i), 4)),
                  pl.BlockSpec((ATTN_BLOCK, 256), lambda i: (i, 4)),
                  pl.BlockSpec((256, ATTN_BLOCK), lambda i: (0, prev(i))),
                  pl.BlockSpec((256, ATTN_BLOCK), lambda i: (0, i)),
                  pl.BlockSpec((1, N_Q_HEADS), lambda i: (0, 0))],
        out_specs=pl.BlockSpec((ATTN_BLOCK, D), lambda i: (i, 0)),
        out_shape=jax.ShapeDtypeStruct((T, D), BF16),
        compiler_params=_params("arbitrary"),
    )(qkv, qkv, qkv, vt, vt, sinks)


ATTN_BWD_ORDER = ((0, 0), (0, 1), (1, 0), (0, 2), (1, 1), (2, 0), (0, 3), (1, 2), (2, 1), (1, 3), (2, 2), (2, 3))


def _attn_bwd(name, qkv, kt, sinks, dya, carry=None):
    T = qkv.shape[0]
    nb = T // ATTN_BLOCK
    scale = HEAD_DIM ** -0.5

    def body(q_ref, kp_ref, kc_ref, vp_ref, vc_ref, ktp_ref, ktc_ref, sink_ref, do_ref,
             dq_ref, dk_ref, dv_ref, ds_ref):
        i = pl.program_id(0)

        @pl.when(i == 0)
        def _():
            ds_ref[...] = jnp.zeros_like(ds_ref)

        prevm, kill = _attn_masks(i == 0)
        kst = _lane_stack(kp_ref, kc_ref, 0) + _lane_stack(kp_ref, kc_ref, 1)
        vst = _lane_stack(vp_ref, vc_ref, 0) + _lane_stack(vp_ref, vc_ref, 1)
        kts = [_row_stack(ktp_ref, ktc_ref, g) for g in range(N_KV_HEADS)]
        lane = lax.broadcasted_iota(jnp.int32, (2 * ATTN_BLOCK, 128), 1)
        low = lane < HEAD_DIM
        slane = lax.broadcasted_iota(jnp.int32, (1, 128), 1)
        dkz = [jnp.zeros((2 * ATTN_BLOCK, 128), F32) for _ in range(N_KV_HEADS)]
        dvz = [jnp.zeros((2 * ATTN_BLOCK, 128), F32) for _ in range(N_KV_HEADS)]
        dsink = jnp.zeros((1, 128), F32)
        groups = range(N_KV_HEADS)
        qcat, docat, s8, dp8 = {}, {}, {}, {}
        ds_u, p_u, dsinks = {}, {}, []

        def scores(g):
            qcat[g], docat[g] = _pair_rows(q_ref, g), _pair_rows(do_ref, g)
            s8[g] = _dot_nt(kst[g], qcat[g]) * scale
            dp8[g] = _dot_nt(vst[g], docat[g])

        def algebra(g):
            for hh in range(2):
                pn, ps = _attn_softmax(_fold(prevm, s8[g], hh), kill, _pair_sinks(sink_ref, g, hh))
                dp = _fold(prevm, dp8[g], hh)
                delta = jnp.sum(pn * dp, axis=0, keepdims=True)
                sd = ps * delta
                dsinks.append(jnp.where(slane == 4 * g + hh, -jnp.sum(sd[:, :ATTN_BLOCK]), 0.0)
                              + jnp.where(slane == 4 * g + 2 + hh, -jnp.sum(sd[:, ATTN_BLOCK:]), 0.0))
                ds_u[g, hh] = _unfold(prevm, pn * (dp - delta) * scale)
                p_u[g, hh] = _unfold(prevm, pn)

        def grads(g):
            dqt = _dot(kts[g][0], ds_u[g, 0]) + _dot(kts[g][1], ds_u[g, 1])
            dq_ref[:, 256 * g:256 * g + 128] = dqt[:, :ATTN_BLOCK].T
            dq_ref[:, 256 * g + 128:256 * (g + 1)] = dqt[:, ATTN_BLOCK:].T
            for hh in range(2):
                own = low if hh == 0 else jnp.logical_not(low)
                dk_h = jnp.where(own, _dot(ds_u[g, hh], qcat[g]), 0.0)
                dv_h = jnp.where(own, _dot(p_u[g, hh], docat[g]), 0.0)
                if hh != g % 2:
                    dk_h = pltpu.roll(dk_h, HEAD_DIM, 1)
                    dv_h = pltpu.roll(dv_h, HEAD_DIM, 1)
                dkz[g] = dkz[g] + dk_h
                dvz[g] = dvz[g] + dv_h

        for stage, g in ATTN_BWD_ORDER:
            (scores, algebra, grads)[stage](g)
        for t in dsinks:
            dsink = dsink + t
        ds_ref[...] += dsink
        cur = pl.ds(pl.multiple_of(i * ATTN_BLOCK, ATTN_BLOCK), ATTN_BLOCK)
        prv = pl.ds(pl.multiple_of(jnp.maximum(i - 1, 0) * ATTN_BLOCK, ATTN_BLOCK), ATTN_BLOCK)
        for gp in range(N_KV_HEADS // 2):
            cols = slice(128 * gp, 128 * (gp + 1))
            dkb = dkz[2 * gp] + dkz[2 * gp + 1]
            dvb = dvz[2 * gp] + dvz[2 * gp + 1]
            dk_ref[cur, cols] = dkb[ATTN_BLOCK:, :]
            dv_ref[cur, cols] = dvb[ATTN_BLOCK:, :]

            @pl.when(i > 0)
            def _():
                dk_ref[prv, cols] += dkb[:ATTN_BLOCK, :]
                dv_ref[prv, cols] += dvb[:ATTN_BLOCK, :]

    prev = lambda i: jnp.maximum(i - 1, 0)
    whole = lambda w: pl.BlockSpec((T, w), lambda i: (0, 0))
    return _call(
        body, name=name, grid=(nb,), scratch_shapes=[], ins=[qkv, qkv, qkv, qkv, qkv, kt, kt, sinks, dya], carry=carry,
        in_specs=[pl.BlockSpec((ATTN_BLOCK, D), lambda i: (i, 0)),
                  pl.BlockSpec((ATTN_BLOCK, 256), lambda i: (prev(i), 4)),
                  pl.BlockSpec((ATTN_BLOCK, 256), lambda i: (i, 4)),
                  pl.BlockSpec((ATTN_BLOCK, 256), lambda i: (prev(i), 5)),
                  pl.BlockSpec((ATTN_BLOCK, 256), lambda i: (i, 5)),
                  pl.BlockSpec((256, ATTN_BLOCK), lambda i: (0, prev(i))),
                  pl.BlockSpec((256, ATTN_BLOCK), lambda i: (0, i)),
                  pl.BlockSpec((1, N_Q_HEADS), lambda i: (0, 0)),
                  pl.BlockSpec((ATTN_BLOCK, D), lambda i: (i, 0))],
        out_specs=[pl.BlockSpec((ATTN_BLOCK, D), lambda i: (i, 0)), whole(256), whole(256),
                   pl.BlockSpec((1, 128), lambda i: (0, 0))],
        out_shape=[jax.ShapeDtypeStruct((T, D), F32), jax.ShapeDtypeStruct((T, 256), F32),
                   jax.ShapeDtypeStruct((T, 256), F32), jax.ShapeDtypeStruct((1, 128), F32)])


HG_FWD = (256, 8, 512)
HG_BWD = (128, 8, 512)


def _chunk_sum(tri, x):
    w = x.shape[1]
    h1 = x.astype(BF16)
    h2 = (x - h1.astype(F32)).astype(BF16)
    r = _dot(tri, jnp.concatenate([h1, h2], axis=1))
    return r[:, :w] + r[:, w:]


def _chunk_tri(n):
    ri = lax.broadcasted_iota(jnp.int32, (n, n), 0)
    ci = lax.broadcasted_iota(jnp.int32, (n, n), 1)
    return jnp.where((ri >= ci) & (ri // HGRN_CHUNK == ci // HGRN_CHUNK), 1.0, 0.0).astype(BF16)


def _chunks(t):
    return [t[HGRN_CHUNK * c:HGRN_CHUNK * (c + 1), :] for c in range(t.shape[0] // HGRN_CHUNK)]


def _lower_bound(lbl_ref):
    l0, l1 = lbl_ref[0:1, :], lbl_ref[1:2, :]
    m = jnp.maximum(l0, l1)
    e0, e1 = jnp.exp(l0 - m), jnp.exp(l1 - m)
    return e0 / (e0 + e1)


def _heads(t):
    return [t[:, 128 * h:128 * (h + 1)] for h in range(t.shape[1] // 128)]


def _per_head(fn, *wide):
    return jnp.concatenate([fn(*parts) for parts in zip(*[_heads(t) for t in wide])], axis=1)


def _hgrn_sub(fl, qh, vv, lb, tril_b):
    w = fl.shape[1]
    sg = _sigmoid(fl)
    f = lb + (1.0 - lb) * sg
    k = 1.0 - f
    gc = _chunk_sum(tril_b, jnp.log(f))
    gl_c = [t[HGRN_CHUNK - 1:HGRN_CHUNK, :] for t in _chunks(gc)]
    gl = jnp.concatenate([jnp.broadcast_to(g, (HGRN_CHUNK, w)) for g in gl_c], axis=0)
    sq = _sigmoid(qh)
    eg = jnp.exp(gc)
    eng = jnp.exp(-gc)
    elg = jnp.exp(gl - gc)
    qd = qh * sq * eg
    ki = k * eng
    ke = k * elg
    qd_b, ki_b, ke_b, v_b = (t.astype(BF16) for t in (qd, ki, ke, vv))
    am_b = [jnp.where(tril_b > 0, _dot_nt(qq, kk), 0.0).astype(BF16) for qq, kk in zip(_heads(qd_b), _heads(ki_b))]
    return dict(sg=sg, f=f, sq=sq, eg=eg, eng=eng, elg=elg, qd=qd, ki=ki, ke=ke, egl=[jnp.exp(g) for g in gl_c],
                am_b=am_b, qd_b=qd_b, ki_b=ki_b, ke_b=ke_b, v_b=v_b)


def _hgrn_out(q, st_b):
    outs = []
    for h, (qd_h, v_h) in enumerate(zip(_heads(q["qd_b"]), _heads(q["v_b"]))):
        inter = [_dot_nt(qc, s) for qc, s in zip(_chunks(qd_h), st_b[h])]
        outs.append(_dot(q["am_b"][h], v_h) + jnp.concatenate(inter, axis=0))
    return jnp.concatenate(outs, axis=1)


def _head_rms(o):
    return _per_head(lambda t: jnp.broadcast_to(
        lax.rsqrt(jnp.mean(t * t, axis=-1, keepdims=True) + RMS_EPS), t.shape), o)


def _hgrn_fwd(name, hg, lb_logits, norm_g, cfg):
    T = hg.shape[0]
    sub_rows, hps, th = min(cfg[0], T), cfg[1], min(cfg[2], T)
    cps = sub_rows // HGRN_CHUNK
    nc = th // HGRN_CHUNK
    tri = _chunk_tri(sub_rows)

    def body(fl_ref, qh_ref, v_ref, og_ref, lbl_ref, ng_ref, tri_ref, y_ref, ss_ref, st):
        @pl.when(pl.program_id(1) == 0)
        def _():
            st[...] = jnp.zeros_like(st)

        lbs = _lower_bound(lbl_ref)
        ng = jnp.tile(ng_ref[...], (1, hps))

        def sub(si, carry):
            rows = pl.ds(pl.multiple_of(si * sub_rows, sub_rows), sub_rows)
            q = _hgrn_sub(fl_ref[rows, :], qh_ref[rows, :], v_ref[rows, :], lbs, tri_ref[...])
            v_hc = [_chunks(t) for t in _heads(q["v_b"])]
            k_hc = [_chunks(t) for t in _heads(q["ke_b"])]
            s = [st[h] for h in range(hps)]
            st_b = [[] for _ in range(hps)]
            for c in range(cps):
                egl = _heads(q["egl"][c])
                for h in range(hps):
                    ss_ref[h, si * cps + c] = s[h]
                    st_b[h].append(s[h].astype(BF16))
                    s[h] = s[h] * egl[h] + _dot_tn(v_hc[h][c], k_hc[h][c])
            for h in range(hps):
                st[h] = s[h]
            o = _hgrn_out(q, st_b)
            og = og_ref[rows, :]
            y_ref[rows, :] = (o * _head_rms(o) * ng * (og * _sigmoid(og))).astype(BF16)
            return carry

        lax.fori_loop(0, th // sub_rows, sub, 0)

    wd = 128 * hps
    col = lambda j: pl.BlockSpec((th, wd), lambda h, t: (t, (8 // hps) * j + h))
    return pl.pallas_call(
        body, name=name, grid=(HGRN_HEADS // hps, T // th),
        in_specs=[col(0), col(1), col(2), col(3), pl.BlockSpec((2, wd), lambda h, t: (0, h)),
                  pl.BlockSpec((1, 128), lambda h, t: (0, 0)), pl.BlockSpec((sub_rows, sub_rows), lambda h, t: (0, 0))],
        out_specs=[pl.BlockSpec((th, wd), lambda h, t: (t, h)),
                   pl.BlockSpec((hps, nc, 128, 128), lambda h, t: (h, t, 0, 0))],
        out_shape=[jax.ShapeDtypeStruct((T, D), BF16),
                   jax.ShapeDtypeStruct((HGRN_HEADS, T // HGRN_CHUNK, 128, 128), F32)],
        scratch_shapes=[pltpu.VMEM((hps, 128, 128), F32)],
        compiler_params=_params("parallel", "arbitrary"),
    )(hg, hg, hg, hg, lb_logits, norm_g, tri)


def _hgrn_bwd(name, hg, lb_logits, norm_g, sstart, dyh, cfg, carry=None):
    T = hg.shape[0]
    sub_rows, hps, th = min(cfg[0], T), cfg[1], min(cfg[2], T)
    cps = sub_rows // HGRN_CHUNK
    nc = th // HGRN_CHUNK
    nt = T // th
    wd = 128 * hps
    tri = _chunk_tri(sub_rows)
    tri_t = tri.T

    def body(fl_ref, qh_ref, v_ref, og_ref, lbl_ref, ng_ref, tri_ref, trit_ref, ss_ref, dy_ref,
             dfl_ref, dqh_ref, dv_ref, dog_ref, dlb_ref, dng_ref, dst):
        @pl.when(pl.program_id(1) == 0)
        def _():
            dst[...] = jnp.zeros_like(dst)
            dlb_ref[...] = jnp.zeros_like(dlb_ref)
            dng_ref[...] = jnp.zeros_like(dng_ref)

        lb = _lower_bound(lbl_ref)
        ng = jnp.tile(ng_ref[...], (1, hps))
        last = lax.broadcasted_iota(jnp.int32, (HGRN_CHUNK, wd), 0) == HGRN_CHUNK - 1
        nsub = th // sub_rows
        cat0 = lambda parts: jnp.concatenate(parts, axis=0)

        def sub(step, carry):
            si = nsub - 1 - step
            rows = pl.ds(pl.multiple_of(si * sub_rows, sub_rows), sub_rows)
            qh, og, dy = qh_ref[rows, :], og_ref[rows, :], dy_ref[rows, :]
            tril_b = tri_ref[...]
            q = _hgrn_sub(fl_ref[rows, :], qh, v_ref[rows, :], lb, tril_b)
            s_in = [[ss_ref[h, si * cps + c] for c in range(cps)] for h in range(hps)]
            st_b = [[s.astype(BF16) for s in row] for row in s_in]
            o = _hgrn_out(q, st_b)
            rr = _head_rms(o)
            oh = o * rr
            sog = _sigmoid(og)
            dog_ref[rows, :] = (dy * oh * ng * (sog * (1.0 + og * (1.0 - sog)))).astype(BF16)
            don = dy * (og * sog)
            dng_w = jnp.sum(don * oh, axis=0, keepdims=True)
            dd = don * ng
            mean_h = _per_head(lambda t: jnp.broadcast_to(jnp.mean(t, axis=-1, keepdims=True), t.shape), dd * oh)
            do_b = (rr * (dd - oh * mean_h)).astype(BF16)
            do_h, qd_h, ki_h, ke_h, v_h = (_heads(t) for t in (do_b, q["qd_b"], q["ki_b"], q["ke_b"], q["v_b"]))
            da_b = [jnp.where(tril_b > 0, _dot_nt(do_h[h], v_h[h]), 0.0).astype(BF16) for h in range(hps)]
            do_c, qd_c, ke_c, v_c = ([_chunks(t) for t in hs] for hs in (do_h, qd_h, ke_h, v_h))
            dsp = [[None] * cps for _ in range(hps)]
            dgl_dec = [[None] * cps for _ in range(hps)]
            d = [dst[h] for h in range(hps)]
            for c in reversed(range(cps)):
                egl = _heads(q["egl"][c])
                for h in range(hps):
                    dsp[h][c] = d[h]
                    dgl_dec[h][c] = jnp.sum(d[h] * s_in[h][c], axis=0, keepdims=True) * egl[h]
                    d[h] = d[h] * egl[h] + _dot_tn(do_c[h][c], qd_c[h][c])
            for h in range(hps):
                dst[h] = d[h]
                dng_ref[h] += dng_w[:, 128 * h:128 * (h + 1)]
            dsp_b = [[t.astype(BF16) for t in row] for row in dsp]
            dv_ref[rows, :] = jnp.concatenate(
                [_dot_tn(q["am_b"][h], do_h[h]) + cat0([_dot_nt(ke_c[h][c], dsp_b[h][c]) for c in range(cps)])
                 for h in range(hps)], axis=1).astype(BF16)
            dqd = jnp.concatenate(
                [_dot(da_b[h], ki_h[h]) + cat0([_dot(do_c[h][c], st_b[h][c]) for c in range(cps)])
                 for h in range(hps)], axis=1)
            dki = jnp.concatenate([_dot_tn(da_b[h], qd_h[h]) for h in range(hps)], axis=1)
            dke = jnp.concatenate([cat0([_dot(v_c[h][c], dsp_b[h][c]) for c in range(cps)]) for h in range(hps)], axis=1)
            dkk = dke * q["ke"]
            dgl = [jnp.sum(t, axis=0, keepdims=True) + jnp.concatenate([dgl_dec[h][c] for h in range(hps)], axis=1)
                   for c, t in enumerate(_chunks(dkk))]
            dgc = dqd * q["qd"] - dki * q["ki"] - dkk + cat0([jnp.where(last, g, 0.0) for g in dgl])
            dlf = _chunk_sum(trit_ref[...], dgc)
            df = dlf / q["f"] - (dki * q["eng"] + dke * q["elg"])
            sg = q["sg"]
            dlb_ref[...] += jnp.sum(df * (1.0 - sg), axis=0, keepdims=True)
            dfl_ref[rows, :] = (df * (1.0 - lb) * sg * (1.0 - sg)).astype(BF16)
            sq = q["sq"]
            dqh_ref[rows, :] = (dqd * q["eg"] * (sq * (1.0 + qh * (1.0 - sq)))).astype(BF16)
            return carry

        lax.fori_loop(0, nsub, sub, 0)
    col = lambda j: pl.BlockSpec((th, wd), lambda h, t: (nt - 1 - t, (8 // hps) * j + h))
    out = pl.BlockSpec((th, wd), lambda h, t: (nt - 1 - t, h))
    tri_spec = pl.BlockSpec((sub_rows, sub_rows), lambda h, t: (0, 0))
    return _call(
        body, name=name, grid=(HGRN_HEADS // hps, nt),
        in_specs=[col(0), col(1), col(2), col(3), pl.BlockSpec((2, wd), lambda h, t: (0, h)),
                  pl.BlockSpec((1, 128), lambda h, t: (0, 0)), tri_spec, tri_spec,
                  pl.BlockSpec((hps, nc, 128, 128), lambda h, t: (h, nt - 1 - t, 0, 0)), out],
        out_specs=[out, out, out, out, pl.BlockSpec((1, wd), lambda h, t: (0, h)),
                   pl.BlockSpec((hps, 1, 128), lambda h, t: (h, 0, 0))],
        out_shape=[jax.ShapeDtypeStruct((T, D), BF16)] * 4 + [jax.ShapeDtypeStruct((1, D), F32),
                                                              jax.ShapeDtypeStruct((HGRN_HEADS, 1, 128), F32)],
        scratch_shapes=[pltpu.VMEM((hps, 128, 128), F32)],
        ins=[hg, hg, hg, hg, lb_logits, norm_g, tri, tri_t, sstart, dyh], carry=carry)


def _mix_fwd(name, ya, yh, gates, xh1, g1, b1, wall, tm):
    T = ya.shape[0]

    def body(ya_ref, yh_ref, ga_ref, gh_ref, xh_ref, g_ref, b_ref, wall_ref,
             xo_ref, r_ref, pa_ref, ph_ref, mg_ref, wpa, wph, wo, sem):
        @pl.when(pl.program_id(0) == 0)
        def _():
            _load_rows(wall_ref, [(wpa, RIN, RP), (wph, RIN + RP, RP), (wo, RIN + 2 * RP, RP)], sem)

        parts = [slice(p, min(p + PART_ROWS, tm)) for p in range(0, tm, PART_ROWS)]
        pas = [_dot(ya_ref[rs, :], wpa[...]).astype(BF16) for rs in parts]
        phs = [_dot(yh_ref[rs, :], wph[...]).astype(BF16) for rs in parts]
        mgs = []
        for rs, pa, ph in zip(parts, pas, phs):
            pa_ref[rs, :] = pa
            ph_ref[rs, :] = ph
            mgs.append((_sigmoid(ga_ref[rs, :]) * pa.astype(F32) + _sigmoid(gh_ref[rs, :]) * ph.astype(F32)).astype(BF16))
            mg_ref[rs, :] = mgs[-1]
        mixes = [_dot(mg, wo[...]) for mg in mgs]
        for rs, mix in zip(parts, mixes):
            y1 = xh_ref[rs, :] * g_ref[...] + b_ref[...]
            xh2, r = _ln_fwd(ALPHA * y1 + mix)
            xo_ref[rs, :] = xh2
            r_ref[rs, :] = jnp.broadcast_to(r, (xh2.shape[0], 128))

    row = lambda w: pl.BlockSpec((tm, w), lambda i: (i, 0))
    vec = pl.BlockSpec((1, D), lambda i: (0, 0))
    return pl.pallas_call(
        body, name=name, grid=(T // tm,),
        in_specs=[row(D), row(D), pl.BlockSpec((tm, D), lambda i: (i, 0)), pl.BlockSpec((tm, D), lambda i: (i, 1)),
                  row(D), vec, vec, HBM_SPEC],
        out_specs=[row(D), row(128), row(D), row(D), row(D)],
        out_shape=[jax.ShapeDtypeStruct((T, D), F32), jax.ShapeDtypeStruct((T, 128), F32)]
        + [jax.ShapeDtypeStruct((T, D), BF16)] * 3,
        scratch_shapes=[pltpu.VMEM((D, D), BF16)] * 3 + [pltpu.SemaphoreType.DMA((3 * NDEV,))],
        compiler_params=_params("arbitrary"),
    )(ya, yh, gates, gates, xh1, g1, b1, wall)


def _mix_bwd(name, dy2, xh2, r2, g2, gates, pa, ph, wall, tm, carry=None):
    T = dy2.shape[0]

    def body(dy_ref, xh_ref, r_ref, g_ref, ga_ref, gh_ref, pa_ref, ph_ref, wall_ref,
             dz_ref, dmix_ref, dpa_ref, dph_ref, dga_ref, dgh_ref, dya_ref, dyh_ref, dg_ref, db_ref,
             wpa, wph, wo, sem):
        @pl.when(pl.program_id(0) == 0)
        def _():
            _load_rows(wall_ref, [(wpa, RIN, RP), (wph, RIN + RP, RP), (wo, RIN + 2 * RP, RP)], sem)
            dg_ref[...] = jnp.zeros_like(dg_ref)
            db_ref[...] = jnp.zeros_like(db_ref)

        parts = [slice(p, min(p + PART_ROWS, tm)) for p in range(0, tm, PART_ROWS)]
        dmix = []
        for rs in parts:
            dz, dgp, dbp = _ln_bwd(dy_ref[rs, :], xh_ref[rs, :], r_ref[rs, :1], g_ref[...])
            dg_ref[...] += dgp
            db_ref[...] += dbp
            dz_ref[rs, :] = dz
            dmix.append(dz.astype(BF16))
            dmix_ref[rs, :] = dmix[-1]
        dmgs = [_dot_nt(d, wo[...]) for d in dmix]
        dpas, dphs = [], []
        for rs, dmg in zip(parts, dmgs):
            sa, sh = _sigmoid(ga_ref[rs, :]), _sigmoid(gh_ref[rs, :])
            dga_ref[rs, :] = (dmg * pa_ref[rs, :].astype(F32) * sa * (1.0 - sa)).astype(BF16)
            dgh_ref[rs, :] = (dmg * ph_ref[rs, :].astype(F32) * sh * (1.0 - sh)).astype(BF16)
            dpas.append((dmg * sa).astype(BF16))
            dphs.append((dmg * sh).astype(BF16))
            dpa_ref[rs, :] = dpas[-1]
            dph_ref[rs, :] = dphs[-1]
        for rs, dpa, dph in zip(parts, dpas, dphs):
            dya_ref[rs, :] = _dot_nt(dpa, wpa[...]).astype(BF16)
            dyh_ref[rs, :] = _dot_nt(dph, wph[...])

    row = lambda w: pl.BlockSpec((tm, w), lambda i: (i, 0))
    vec = pl.BlockSpec((1, D), lambda i: (0, 0))
    return _call(
        body, name=name, grid=(T // tm,),
        in_specs=[row(D), row(D), row(128), vec, pl.BlockSpec((tm, D), lambda i: (i, 0)),
                  pl.BlockSpec((tm, D), lambda i: (i, 1)), row(D), row(D), HBM_SPEC],
        out_specs=[row(D)] * 8 + [vec, vec],
        out_shape=[jax.ShapeDtypeStruct((T, D), F32)] + [jax.ShapeDtypeStruct((T, D), BF16)] * 6
        + [jax.ShapeDtypeStruct((T, D), F32)] + [jax.ShapeDtypeStruct((1, D), F32)] * 2,
        scratch_shapes=[pltpu.VMEM((D, D), BF16)] * 3 + [pltpu.SemaphoreType.DMA((3 * NDEV,))],
        ins=[dy2, xh2, r2, g2, gates, gates, pa, ph, wall], carry=carry)


def _adam(w, g, m, v):
    m = ADAM_B1 * m + (1.0 - ADAM_B1) * g
    v = ADAM_B2 * v + (1.0 - ADAM_B2) * (g * g)
    m_hat = m / (1.0 - ADAM_B1 ** ADAM_STEP)
    v_hat = v / (1.0 - ADAM_B2 ** ADAM_STEP)
    delta = -ADAM_LR * (m_hat / (jnp.sqrt(v_hat) + ADAM_EPS) + ADAM_WD * w)
    return delta, m, v


def _grad_steps(name, items, cb, carry=None):
    n = len(items)
    pieces = [len(it[0]) for it in items]

    def body(*refs):
        it = iter(refs[:sum(pieces) + 3 * n])
        outs = refs[sum(pieces) + 3 * n:]
        for i in range(n):
            accs = []
            for _ in range(pieces[i]):
                r_ref = next(it)
                acc = r_ref[0].astype(F32)
                for k in range(1, NDEV):
                    acc = acc + r_ref[k].astype(F32)
                accs.append(acc)
            acc = accs[0] if len(accs) == 1 else jnp.concatenate(accs, axis=0)
            w_ref, m_ref, v_ref = next(it), next(it), next(it)
            g_ref, d_ref, mo_ref, vo_ref = outs[4 * i:4 * i + 4]
            g_ref[...] = acc
            d, mm, vv = _adam(w_ref[...], acc, m_ref[...], v_ref[...])
            d_ref[...] = d
            mo_ref[...] = mm
            vo_ref[...] = vv

    in_specs, out_specs, out_shape, ins = [], [], [], []
    for recv, w, m, v in items:
        blk = pl.BlockSpec((w.shape[0], cb), lambda j: (0, j))
        in_specs += [pl.BlockSpec((NDEV, rp.shape[1], cb), lambda j: (0, 0, j)) for rp in recv] + [blk, blk, blk]
        out_specs += [blk] * 4
        out_shape += [jax.ShapeDtypeStruct(w.shape, F32)] * 4
        ins += list(recv) + [w, m, v]
    outs, ex = _call(body, name=name, grid=(D // cb,), in_specs=in_specs, out_specs=out_specs, out_shape=out_shape,
                     scratch_shapes=[], ins=ins, carry=carry)
    return [tuple(outs[4 * i:4 * i + 4]) for i in range(n)], ex


_SMALL = [("ln1_g", D), ("ln1_b", D), ("ln2_g", D), ("ln2_b", D), ("ln3_g", D), ("ln3_b", D),
          ("b_in", DIN), ("lb", D), ("attn_sinks", 128), ("hgrn_norm_g", 128), ("loss", 128)]
_SMALL_OFF = {}
_o = 0
for _n, _w in _SMALL:
    _SMALL_OFF[_n] = (_o, _w)
    _o += _w
PACK = _o


def _small_step(name, slots, small_w, small_m, small_v):
    names = ["ln1_g", "ln1_b", "ln2_g", "ln2_b", "ln3_g", "ln3_b", "b_in", "attn_sinks", "hgrn_lb_logits", "hgrn_norm_g"]
    np_ = len(names)

    def body(*refs):
        s_ref = refs[0]
        w_refs = refs[1:1 + np_]
        m_refs = refs[1 + np_:1 + 2 * np_]
        v_refs = refs[1 + 2 * np_:1 + 3 * np_]
        loss_ref, outs = refs[1 + 3 * np_], refs[2 + 3 * np_:]
        tot = s_ref[0]
        for k in range(1, NDEV):
            tot = tot + s_ref[k]

        def part(n):
            o, w = _SMALL_OFF[n]
            return tot[:, o:o + w]

        loss_ref[...] = part("loss")
        for i, n in enumerate(names):
            w = w_refs[i][...]
            if n == "hgrn_lb_logits":
                m_ = jnp.maximum(w[0:1, :], w[1:2, :])
                e0, e1 = jnp.exp(w[0:1, :] - m_), jnp.exp(w[1:2, :] - m_)
                p0 = e0 / (e0 + e1)
                t = p0 * (1.0 - p0) * part("lb")
                g = jnp.concatenate([t, -t], axis=0)
            elif n == "attn_sinks":
                g = part(n)[:, :N_Q_HEADS]
            else:
                g = part(n)
            d, mm, vv = _adam(w, g, m_refs[i][...], v_refs[i][...])
            outs[4 * i][...] = g
            outs[4 * i + 1][...] = d
            outs[4 * i + 2][...] = mm
            outs[4 * i + 3][...] = vv

    out_shape = [jax.ShapeDtypeStruct((1, 128), F32)]
    for n in names:
        out_shape += [jax.ShapeDtypeStruct(small_w[n].shape, F32)] * 4
    return pl.pallas_call(
        body, name=name, out_shape=out_shape,
        compiler_params=pltpu.CompilerParams(vmem_limit_bytes=VMEM_LIMIT),
    )(slots, *[small_w[n] for n in names], *[small_m[n] for n in names], *[small_v[n] for n in names]), names


def _tile(T, pref):
    return min(T, pref)


def kernel(x, ln1_g, ln1_b, ffn1_w1, ffn1_w3, ffn1_w2, ln2_g, ln2_b, w_in, b_in, attn_sinks, hgrn_lb_logits, hgrn_norm_g, w_proj_attn, w_proj_hgrn, w_out, ln3_g, ln3_b, ffn2_w1, ffn2_w3, ffn2_w2, loss_target, m_ln1_g, m_ln1_b, m_ffn1_w1, m_ffn1_w3, m_ffn1_w2, m_ln2_g, m_ln2_b, m_w_in, m_b_in, m_attn_sinks, m_hgrn_lb_logits, m_hgrn_norm_g, m_w_proj_attn, m_w_proj_hgrn, m_w_out, m_ln3_g, m_ln3_b, m_ffn2_w1, m_ffn2_w3, m_ffn2_w2, v_ln1_g, v_ln1_b, v_ffn1_w1, v_ffn1_w3, v_ffn1_w2, v_ln2_g, v_ln2_b, v_w_in, v_b_in, v_attn_sinks, v_hgrn_lb_logits, v_hgrn_norm_g, v_w_proj_attn, v_w_proj_hgrn, v_w_out, v_ln3_g, v_ln3_b, v_ffn2_w1, v_ffn2_w3, v_ffn2_w2):
    T = x.shape[1]
    xs = x[0]
    tgt = loss_target[0]
    tm = _tile(T, 256)
    tm2 = _tile(T, 512)
    tk = _tile(T, 2048)

    t_bf = lambda w: w[0].T.astype(BF16)
    n_bf = lambda w: w[0].astype(BF16)
    ffn_shard = lambda w1, w3, w2: jnp.concatenate([t_bf(w1), t_bf(w3), n_bf(w2)], axis=0)
    mix_shard = jnp.concatenate([t_bf(w_in), n_bf(w_proj_attn), n_bf(w_proj_hgrn), n_bf(w_out)], axis=0)
    (ffn1_all,) = _exchange_call("gather_ffn1", _gather_exchange(ffn_shard(ffn1_w1, ffn1_w3, ffn1_w2)))
    ffn_offs = (0, RF, 2 * RF)
    ropes = _rope_tables(T)

    (xh1, r1, a1, b1, xb0), (mix_all,) = _ffn_fwd("ffn1_fwd", xs, None, ffn1_all, ffn_offs, tm2,
                                                  carry=_gather_exchange(mix_shard))
    (qkv, kt, vt, hg, gates, y1b), (ffn2_all,) = _inproj_fwd(
        "inproj_fwd", xh1, ln1_g, ln1_b, mix_all, b_in, ropes, tm2,
        carry=_gather_exchange(ffn_shard(ffn2_w1, ffn2_w3, ffn2_w2)))
    ya = _attn_fwd("attn_fwd", qkv, vt, attn_sinks)
    yh, sstart = _hgrn_fwd("hgrn_fwd", hg, hgrn_lb_logits, hgrn_norm_g, HG_FWD)
    xh2, r2, pa, ph, merged = _mix_fwd("mix_fwd", ya, yh, gates, xh1, ln1_g, ln1_b, mix_all, tm2)
    (xh3, r3, a2, b2, y2b, loss_part), _ = _ffn_fwd("ffn2_fwd", xh2, (ln2_g, ln2_b), ffn2_all, ffn_offs, tm2,
                                                    loss=(ln3_g, ln3_b, tgt))

    (dy2, dab2, u2, df2, dg3, db3), _ = _ffn_bwd("ffn2_bwd", ("loss", ln3_b, tgt), xh3, r3, ln3_g, a2, b2, ffn2_all,
                                                 ffn_offs, tm)
    recv = {}
    g_ffn2_13, _ = _wgrad("wgrad_ffn2_w13", dab2, y2b, DFF // 2, tk)
    g_ffn2_2, _ = _wgrad("wgrad_ffn2_w2", u2, df2, DFF // 2, tk)
    (dz2, dmix, dpa, dph, dga, dgh, dya, dyh, dg2, db2), (recv["ffn2_w1"],) = _mix_bwd(
        "mix_bwd", dy2, xh2, r2, ln2_g, gates, pa, ph, mix_all, tm2, carry=_grad_exchange([g_ffn2_13], [0], [RF]))
    g_wo, _ = _wgrad("wgrad_w_out", merged, dmix, D, tk)
    g_pa, _ = _wgrad("wgrad_w_pa", ya, dpa, D, tk)
    g_ph, _ = _wgrad("wgrad_w_ph", yh, dph, D, tk)
    (dq, dk, dv, dsink), (recv["ffn2_w3"],) = _attn_bwd(
        "attn_bwd", qkv, kt, attn_sinks, dya, carry=_grad_exchange([g_ffn2_13], [DFF], [RF]))
    (dfl, dqh, dih, dog, dlb, dng), (recv["ffn2_w2"], recv["w_proj_attn"], recv["w_proj_hgrn"], recv["w_out"]) = \
        _hgrn_bwd("hgrn_bwd", hg, hgrn_lb_logits, hgrn_norm_g, sstart, dyh, HG_BWD,
                  carry=_grad_exchange([g_ffn2_2, g_pa, g_ph, g_wo], [0, 0, 0, 0], [RF, RP, RP, RP]))
    dy1, dproj, dbin = _inproj_bwd("inproj_bwd", dq, dk, dv, (dfl, dqh, dih, dog), (dga, dgh), dz2, mix_all, ropes, tm)
    g_win, _ = _wgrad("wgrad_w_in", dproj, y1b, DIN // 4, tk)
    win_rows = (400, 288, 272)
    win_base = (0, 400, 688)
    (dz1, dab1, u1, df1, dg1, db1), (rw0,) = _ffn_bwd(
        "ffn1_bwd", ("dy", dy1), xh1, r1, ln1_g, a1, b1, ffn1_all, ffn_offs, tm, with_dx=False,
        carry=_grad_exchange([g_win], win_base[:1], win_rows[:1], [RIN]))
    g_ffn1_2, (rw1,) = _wgrad("wgrad_ffn1_w2", u1, df1, DFF // 2, tk,
                              carry=_grad_exchange([g_win], win_base[1:2], win_rows[1:2], [RIN]))
    g_ffn1_1, (recv["ffn1_w2"],) = _wgrad("wgrad_ffn1_w1", dab1, xb0, DFF // 2, tk, 0, DFF,
                                          carry=_grad_exchange([g_ffn1_2], [0], [RF]))
    g_ffn1_3, (recv["ffn1_w1"],) = _wgrad("wgrad_ffn1_w3", dab1, xb0, DFF // 2, tk, DFF, DFF,
                                          carry=_grad_exchange([g_ffn1_1], [0], [RF]))
    parts = {"ln1_g": dg1, "ln1_b": db1, "ln2_g": dg2, "ln2_b": db2, "ln3_g": dg3, "ln3_b": db3, "b_in": dbin,
             "lb": dlb, "attn_sinks": dsink, "hgrn_norm_g": jnp.sum(dng, axis=0), "loss": loss_part[0:1, :]}
    packed = jnp.concatenate([parts[n] for n, _ in _SMALL], axis=1)
    gx, (recv["ffn1_w3"], rw2, small_slots) = _ffn_dx(
        "ffn1_dx", dab1, dz1, ffn1_all, ffn_offs, tm2,
        carry=_join(_grad_exchange([g_ffn1_3, g_win], [0, win_base[2]], [RF, win_rows[2]], [RF, RIN]),
                    _row_gather_exchange(packed)))
    recv["w_in"] = [rw0, rw1, rw2]

    big = [("ffn1_w1", ffn1_w1, m_ffn1_w1, v_ffn1_w1, True), ("ffn1_w3", ffn1_w3, m_ffn1_w3, v_ffn1_w3, True),
           ("ffn1_w2", ffn1_w2, m_ffn1_w2, v_ffn1_w2, False), ("w_in", w_in, m_w_in, v_w_in, True),
           ("w_proj_attn", w_proj_attn, m_w_proj_attn, v_w_proj_attn, False),
           ("w_proj_hgrn", w_proj_hgrn, m_w_proj_hgrn, v_w_proj_hgrn, False),
           ("w_out", w_out, m_w_out, v_w_out, False),
           ("ffn2_w1", ffn2_w1, m_ffn2_w1, v_ffn2_w1, True), ("ffn2_w3", ffn2_w3, m_ffn2_w3, v_ffn2_w3, True),
           ("ffn2_w2", ffn2_w2, m_ffn2_w2, v_ffn2_w2, False)]
    view = lambda t, transposed: t[0].T if transposed else t[0]
    back = lambda t, transposed: t.T[None] if transposed else t[None]
    slots = lambda n: recv[n] if isinstance(recv[n], list) else [recv[n]]
    stepped, _ = _grad_steps("steps", [(slots(n), view(w, tr), view(m, tr), view(v, tr)) for n, w, m, v, tr in big], 128)
    res = {n: tuple(back(t, tr) for t in outs) for (n, _, _, _, tr), outs in zip(big, stepped)}

    small_w = dict(ln1_g=ln1_g, ln1_b=ln1_b, ln2_g=ln2_g, ln2_b=ln2_b, ln3_g=ln3_g, ln3_b=ln3_b, b_in=b_in,
                   attn_sinks=attn_sinks, hgrn_lb_logits=hgrn_lb_logits, hgrn_norm_g=hgrn_norm_g)
    small_m = dict(ln1_g=m_ln1_g, ln1_b=m_ln1_b, ln2_g=m_ln2_g, ln2_b=m_ln2_b, ln3_g=m_ln3_g, ln3_b=m_ln3_b,
                   b_in=m_b_in, attn_sinks=m_attn_sinks, hgrn_lb_logits=m_hgrn_lb_logits, hgrn_norm_g=m_hgrn_norm_g)
    small_v = dict(ln1_g=v_ln1_g, ln1_b=v_ln1_b, ln2_g=v_ln2_g, ln2_b=v_ln2_b, ln3_g=v_ln3_g, ln3_b=v_ln3_b,
                   b_in=v_b_in, attn_sinks=v_attn_sinks, hgrn_lb_logits=v_hgrn_lb_logits, hgrn_norm_g=v_hgrn_norm_g)
    outs, names = _small_step("small_step", small_slots, small_w, small_m, small_v)
    loss = outs[0][0, 0]
    for i, n in enumerate(names):
        res[n] = tuple(outs[1 + 4 * i:5 + 4 * i])

    order = ["ln1_g", "ln1_b", "ffn1_w1", "ffn1_w3", "ffn1_w2", "ln2_g", "ln2_b", "w_in", "b_in", "attn_sinks",
             "hgrn_lb_logits", "hgrn_norm_g", "w_proj_attn", "w_proj_hgrn", "w_out", "ln3_g", "ln3_b",
             "ffn2_w1", "ffn2_w3", "ffn2_w2"]
    return (loss, gx[None], *[res[n][0] for n in order], *[res[n][1] for n in order],
            *[res[n][2] for n in order], *[res[n][3] for n in order])
```

```python
import jax
import jax.numpy as jnp
from jax import lax
from jax.experimental import pallas as pl
from jax.experimental.pallas import tpu as pltpu

F32 = jnp.float32
BF16 = jnp.bfloat16

NDEV = 8
D = 1024
DFF = 2816
RF = DFF // NDEV
DIN = 7680
RIN = DIN // NDEV
RP = D // NDEV
N_Q_HEADS = 16
N_KV_HEADS = 4
HEAD_DIM = 64
ATTN_BLOCK = 128
ROPE_THETA = 500000.0
ROPE_DIM = HEAD_DIM // 4
HGRN_HEADS = 8
HGRN_CHUNK = 64
ALPHA = 2.0 ** 0.25
LN_EPS = 1e-5
RMS_EPS = 1e-6
NEG_INF = -1e30
ADAM_LR = 0.001
ADAM_B1 = 0.9
ADAM_B2 = 0.999
ADAM_EPS = 1e-08
ADAM_WD = 0.01
ADAM_STEP = 10

QKV_W = 1536
HG_W = 4096
GATE_W = 2048
VMEM_LIMIT = 60 * 2 ** 20
PART_ROWS = 256
MESH = pl.DeviceIdType.MESH
HBM_SPEC = pl.BlockSpec(memory_space=pltpu.HBM)


def _params(*sem):
    return pltpu.CompilerParams(dimension_semantics=sem, vmem_limit_bytes=VMEM_LIMIT)


def _dot(a, b):
    return jnp.dot(a, b, preferred_element_type=F32)


def _dot_nt(a, b):
    return lax.dot_general(a, b, (((1,), (1,)), ((), ())), preferred_element_type=F32)


def _dot_tn(a, b):
    return lax.dot_general(a, b, (((0,), (0,)), ((), ())), preferred_element_type=F32)


def _sigmoid(x):
    return 0.5 * jnp.tanh(0.5 * x) + 0.5


def _ln_fwd(z):
    mu = jnp.mean(z, axis=-1, keepdims=True)
    zc = z - mu
    var = jnp.mean(zc * zc, axis=-1, keepdims=True)
    r = lax.rsqrt(var + LN_EPS)
    return zc * r, r


def _ln_bwd(dy, xh, r, g):
    dxh = dy * g
    m1 = jnp.mean(dxh, axis=-1, keepdims=True)
    m2 = jnp.mean(dxh * xh, axis=-1, keepdims=True)
    dz = r * (dxh - m1 - xh * m2)
    return dz, jnp.sum(dy * xh, axis=0, keepdims=True), jnp.sum(dy, axis=0, keepdims=True)


def _load_rows(wall_ref, pieces, sem):
    copies = []
    for dst, off, r in pieces:
        for k in range(NDEV):
            c = pltpu.make_async_copy(wall_ref.at[k, pl.ds(off, r), :], dst.at[pl.ds(k * r, r), :], sem.at[len(copies)])
            c.start()
            copies.append(c)
    for c in copies:
        c.wait()


class _Exchange:
    def __init__(self, inputs, out_shape, scratch, begin, middle, end):
        self.inputs, self.out_shape, self.scratch = inputs, out_shape, scratch
        self.begin, self.middle, self.end = begin, middle, end


def _gather_exchange(shard):
    rows, cols = shard.shape

    def ops(ins, outs, scr):
        (x_ref,), (out_ref,), (send_sems, recv_sems, local_sem) = ins, outs, scr
        x, y, c = lax.axis_index("x"), lax.axis_index("y"), lax.axis_index("c")
        me, sibling = (x, y, c), (x, y, 1 - c)
        chips = [(1 - x, y), (x, 1 - y), (1 - x, 1 - y)]

        def slot(px, py, pc):
            return out_ref.at[4 * px + 2 * py + pc]

        def copy(k, block, to, src=None):
            return pltpu.make_async_remote_copy(
                src_ref=slot(*block) if src is None else src, dst_ref=slot(*block),
                send_sem=send_sems.at[k], recv_sem=recv_sems.at[k], device_id=to, device_id_type=MESH)

        mine = lambda: pltpu.make_async_copy(x_ref, slot(*me), local_sem)
        first = lambda: [copy(0, me, sibling, src=x_ref)] + [
            copy(1 + j, me, (*chip, c), src=x_ref) for j, chip in enumerate(chips)]
        passed = lambda: [copy(4 + j, (*chip, c), sibling) for j, chip in enumerate(chips)]
        return c, me, sibling, chips, copy, mine, first, passed

    def begin(*refs):
        _, _, _, _, _, mine, first, _ = ops(*refs)
        mine().start()
        for cp in first():
            cp.start()

    def middle(*refs):
        c, me, _, chips, copy, _, _, passed = ops(*refs)
        for (j, chip), fwd in zip(enumerate(chips), passed()):
            copy(1 + j, (*chip, c), me).wait_recv()
            fwd.start()

    def end(*refs):
        c, me, sibling, chips, copy, mine, first, passed = ops(*refs)
        copy(0, sibling, me).wait_recv()
        for j, chip in enumerate(chips):
            copy(4 + j, (*chip, 1 - c), me).wait_recv()
        for cp in first() + passed():
            cp.wait_send()
        mine().wait()

    return _Exchange([shard], [jax.ShapeDtypeStruct((NDEV, rows, cols), shard.dtype)],
                     [pltpu.SemaphoreType.DMA((7,)), pltpu.SemaphoreType.DMA((7,)), pltpu.SemaphoreType.DMA],
                     begin, middle, end)


def _grad_exchange(grads, bases, rows, strides=None):
    n = len(grads)
    strides = rows if strides is None else strides

    def copies(g_refs, out_refs, scr):
        send_sems, recv_sems, local_sems = scr
        x, y, c = lax.axis_index("x"), lax.axis_index("y"), lax.axis_index("c")
        me = 4 * x + 2 * y + c
        out = []
        for i in range(n):
            r = rows[i]
            src = lambda k: g_refs[i].at[pl.ds(pl.multiple_of(bases[i] + k * strides[i], 16), r), :]
            out.append(pltpu.make_async_copy(src(me), out_refs[i].at[me], local_sems.at[i]))
            for j in range(1, NDEV):
                px, py, pc = x ^ (j >> 2), y ^ ((j >> 1) & 1), c ^ (j & 1)
                out.append(pltpu.make_async_remote_copy(
                    src_ref=src(4 * px + 2 * py + pc), dst_ref=out_refs[i].at[me],
                    send_sem=send_sems.at[i, j - 1], recv_sem=recv_sems.at[i, j - 1],
                    device_id=(px, py, pc), device_id_type=MESH))
        return out

    def begin(*refs):
        for cp in copies(*refs):
            cp.start()

    def end(*refs):
        for cp in copies(*refs):
            cp.wait()

    return _Exchange(list(grads), [jax.ShapeDtypeStruct((NDEV, r, g.shape[1]), g.dtype) for g, r in zip(grads, rows)],
                     [pltpu.SemaphoreType.DMA((n, NDEV - 1)), pltpu.SemaphoreType.DMA((n, NDEV - 1)),
                      pltpu.SemaphoreType.DMA((n,))], begin, None, end)


def _row_gather_exchange(row):
    def copies(ins, outs, scr):
        (r_ref,), (o_ref,), (send_sems, recv_sems, local_sem) = ins, outs, scr
        x, y, c = lax.axis_index("x"), lax.axis_index("y"), lax.axis_index("c")
        me = 4 * x + 2 * y + c
        out = [pltpu.make_async_copy(r_ref, o_ref.at[me], local_sem)]
        for j in range(1, NDEV):
            out.append(pltpu.make_async_remote_copy(
                src_ref=r_ref, dst_ref=o_ref.at[me], send_sem=send_sems.at[j - 1], recv_sem=recv_sems.at[j - 1],
                device_id=(x ^ (j >> 2), y ^ ((j >> 1) & 1), c ^ (j & 1)), device_id_type=MESH))
        return out

    def begin(*refs):
        for cp in copies(*refs):
            cp.start()

    def end(*refs):
        for cp in copies(*refs):
            cp.wait()

    return _Exchange([row], [jax.ShapeDtypeStruct((NDEV,) + row.shape, row.dtype)],
                     [pltpu.SemaphoreType.DMA((NDEV - 1,)), pltpu.SemaphoreType.DMA((NDEV - 1,)), pltpu.SemaphoreType.DMA],
                     begin, None, end)


def _join(a, b):
    na, oa, sa = len(a.inputs), len(a.out_shape), len(a.scratch)
    split = lambda ins, outs, scr: ((ins[:na], outs[:oa], scr[:sa]), (ins[na:], outs[oa:], scr[sa:]))

    def begin(*refs):
        pa, pb = split(*refs)
        a.begin(*pa)
        b.begin(*pb)

    def end(*refs):
        pa, pb = split(*refs)
        a.end(*pa)
        b.end(*pb)

    return _Exchange(a.inputs + b.inputs, a.out_shape + b.out_shape, a.scratch + b.scratch, begin, None, end)


def _exchange_call(name, ex):
    ni, no = len(ex.inputs), len(ex.out_shape)

    def body(*refs):
        parts = (refs[:ni], refs[ni:ni + no], refs[ni + no:])
        ex.begin(*parts)
        if ex.middle is not None:
            ex.middle(*parts)
        ex.end(*parts)

    return pl.pallas_call(body, name=name, out_shape=ex.out_shape, in_specs=[HBM_SPEC] * ni, out_specs=[HBM_SPEC] * no,
                          scratch_shapes=ex.scratch)(*ex.inputs)


def _call(body, *, name, grid, in_specs, out_specs, out_shape, scratch_shapes, ins, carry=None):
    sem = ("arbitrary",) * len(grid)
    if carry is None:
        outs = pl.pallas_call(body, name=name, grid=grid, in_specs=in_specs, out_specs=out_specs, out_shape=out_shape,
                              scratch_shapes=scratch_shapes, compiler_params=_params(*sem))(*ins)
        return outs, None
    n_in, n_out, n_scr = len(ins), len(out_shape), len(scratch_shapes)
    ci, co = len(carry.inputs), len(carry.out_shape)
    total = 1
    for g in grid:
        total *= g

    def wrapped(*refs):
        own_in, ex_in = refs[:n_in], refs[n_in:n_in + ci]
        o0 = n_in + ci
        own_out, ex_out = refs[o0:o0 + n_out], refs[o0 + n_out:o0 + n_out + co]
        s0 = o0 + n_out + co
        own_scr, ex_scr = refs[s0:s0 + n_scr], refs[s0 + n_scr:]
        step = pl.program_id(0)
        for d in range(1, len(grid)):
            step = step * grid[d] + pl.program_id(d)
        parts = (ex_in, ex_out, ex_scr)
        pl.when(step == 0)(lambda: carry.begin(*parts))
        body(*own_in, *own_out, *own_scr)
        if carry.middle is not None:
            pl.when(step == (3 * total) // 4)(lambda: carry.middle(*parts))
        pl.when(step == total - 1)(lambda: carry.end(*parts))

    outs = pl.pallas_call(
        wrapped, name=name, grid=grid, in_specs=list(in_specs) + [HBM_SPEC] * ci,
        out_specs=list(out_specs) + [HBM_SPEC] * co, out_shape=list(out_shape) + list(carry.out_shape),
        scratch_shapes=list(scratch_shapes) + list(carry.scratch), compiler_params=_params(*sem),
    )(*ins, *carry.inputs)
    return outs[:n_out], outs[n_out:]


def _ffn_fwd(name, xin, affine, wall, offs, tm, loss=None, carry=None):
    T = xin.shape[0]
    nt = T // tm

    def body(*refs):
        it = iter(refs)
        x_ref = next(it)
        if affine is not None:
            g_ref, b_ref = next(it), next(it)
        wall_ref = next(it)
        if loss is not None:
            go_ref, bo_ref, tgt_ref = next(it), next(it), next(it)
        xh_ref, r_ref, a_ref, b2_ref, yb_ref = (next(it) for _ in range(5))
        if loss is not None:
            loss_ref = next(it)
        w1, w3, w2, sem = (next(it) for _ in range(4))

        @pl.when(pl.program_id(0) == 0)
        def _():
            _load_rows(wall_ref, [(w1, offs[0], RF), (w3, offs[1], RF), (w2, offs[2], RF)], sem)
            if loss is not None:
                loss_ref[...] = jnp.zeros_like(loss_ref)

        parts = [slice(p, min(p + PART_ROWS, tm)) for p in range(0, tm, PART_ROWS)]
        ys, ybs = [], []
        for rs in parts:
            y = x_ref[rs, :]
            if affine is not None:
                y = y * g_ref[...] + b_ref[...]
            ys.append(y)
            ybs.append(y.astype(BF16))
            yb_ref[rs, :] = ybs[-1]
        ab = [(_dot_nt(yb, w1[...]).astype(BF16), _dot_nt(yb, w3[...]).astype(BF16)) for yb in ybs]
        us = []
        for rs, (a, b) in zip(parts, ab):
            a_ref[rs, :] = a
            b2_ref[rs, :] = b
            af, bf = a.astype(F32), b.astype(F32)
            us.append((af * _sigmoid(af) * bf).astype(BF16))
        fs = [_dot(u, w2[...]) for u in us]
        for rs, y, f in zip(parts, ys, fs):
            xh, r = _ln_fwd(ALPHA * y + 0.5 * f)
            xh_ref[rs, :] = xh
            r_ref[rs, :] = jnp.broadcast_to(r, (xh.shape[0], 128))
            if loss is not None:
                e = xh * go_ref[...] + bo_ref[...] - tgt_ref[rs, :]
                loss_ref[...] += jnp.sum(e * e) * (0.5 / D)

    row = lambda w: pl.BlockSpec((tm, w), lambda i: (i, 0))
    vec = pl.BlockSpec((1, D), lambda i: (0, 0))
    ins, in_specs = [xin], [row(D)]
    if affine is not None:
        ins += list(affine)
        in_specs += [vec, vec]
    ins.append(wall)
    in_specs.append(HBM_SPEC)
    if loss is not None:
        ins += list(loss)
        in_specs += [vec, vec, row(D)]
    out_shape = [jax.ShapeDtypeStruct((T, D), F32), jax.ShapeDtypeStruct((T, 128), F32),
                 jax.ShapeDtypeStruct((T, DFF), BF16), jax.ShapeDtypeStruct((T, DFF), BF16),
                 jax.ShapeDtypeStruct((T, D), BF16)]
    out_specs = [row(D), row(128), row(DFF), row(DFF), row(D)]
    if loss is not None:
        out_shape.append(jax.ShapeDtypeStruct((8, 128), F32))
        out_specs.append(pl.BlockSpec((8, 128), lambda i: (0, 0)))
    return _call(body, name=name, grid=(nt,), in_specs=in_specs, out_specs=out_specs, out_shape=out_shape,
                 scratch_shapes=[pltpu.VMEM((DFF, D), BF16)] * 3 + [pltpu.SemaphoreType.DMA((3 * NDEV,))],
                 ins=ins, carry=carry)


def _ffn_bwd(name, dy_src, xh, r, g, a, b, wall, offs, tm, with_dx=True, carry=None):
    T = xh.shape[0]
    nt = T // tm
    from_loss = dy_src[0] == "loss"

    def body(*refs):
        it = iter(refs)
        if from_loss:
            bo_ref, tgt_ref = next(it), next(it)
        else:
            dy_ref = next(it)
        xh_ref, r_ref, g_ref, a_ref, b_ref, wall_ref = (next(it) for _ in range(6))
        dyin_ref, dab_ref, u_ref, df_ref, dg_ref, db_ref = (next(it) for _ in range(6))
        if with_dx:
            w1, w3 = next(it), next(it)
        w2, sem = next(it), next(it)

        @pl.when(pl.program_id(0) == 0)
        def _():
            _load_rows(wall_ref, ([(w1, offs[0], RF), (w3, offs[1], RF)] if with_dx else []) + [(w2, offs[2], RF)], sem)
            dg_ref[...] = jnp.zeros_like(dg_ref)
            db_ref[...] = jnp.zeros_like(db_ref)

        parts = [slice(p, min(p + PART_ROWS, tm)) for p in range(0, tm, PART_ROWS)]
        gv = g_ref[...]
        dz, df = [], []
        for rs in parts:
            xhv = xh_ref[rs, :]
            if from_loss:
                dy = (xhv * gv + bo_ref[...] - tgt_ref[rs, :]) * (1.0 / D)
            else:
                dy = dy_ref[rs, :]
            dzp, dgp, dbp = _ln_bwd(dy, xhv, r_ref[rs, :1], gv)
            dg_ref[...] += dgp
            db_ref[...] += dbp
            dz.append(dzp)
            df.append((0.5 * dzp).astype(BF16))
            df_ref[rs, :] = df[-1]
        du = [_dot_nt(d, w2[...]) for d in df]
        da, dbb = [], []
        for rs, dup in zip(parts, du):
            af, bf = a_ref[rs, :].astype(F32), b_ref[rs, :].astype(F32)
            s = _sigmoid(af)
            sl = af * s
            u_ref[rs, :] = (sl * bf).astype(BF16)
            da.append((dup * bf * (s * (1.0 + af * (1.0 - s)))).astype(BF16))
            dbb.append((dup * sl).astype(BF16))
            dab_ref[rs, :DFF] = da[-1]
            dab_ref[rs, DFF:] = dbb[-1]
        for rs, dzp, dap, dbp in zip(parts, dz, da, dbb):
            dyin_ref[rs, :] = ALPHA * dzp + _dot(dap, w1[...]) + _dot(dbp, w3[...]) if with_dx else dzp

    row = lambda w: pl.BlockSpec((tm, w), lambda i: (i, 0))
    vec = pl.BlockSpec((1, D), lambda i: (0, 0))
    if from_loss:
        ins, in_specs = [dy_src[1], dy_src[2]], [vec, row(D)]
    else:
        ins, in_specs = [dy_src[1]], [row(D)]
    ins += [xh, r, g, a, b, wall]
    in_specs += [row(D), row(128), vec, row(DFF), row(DFF), HBM_SPEC]
    return _call(
        body, name=name, grid=(nt,), in_specs=in_specs,
        out_specs=[row(D), row(2 * DFF), row(DFF), row(D), vec, vec],
        out_shape=[jax.ShapeDtypeStruct((T, D), F32), jax.ShapeDtypeStruct((T, 2 * DFF), BF16),
                   jax.ShapeDtypeStruct((T, DFF), BF16), jax.ShapeDtypeStruct((T, D), BF16),
                   jax.ShapeDtypeStruct((1, D), F32), jax.ShapeDtypeStruct((1, D), F32)],
        scratch_shapes=[pltpu.VMEM((DFF, D), BF16)] * (3 if with_dx else 1) + [pltpu.SemaphoreType.DMA((3 * NDEV,))],
        ins=ins, carry=carry)


def _ffn_dx(name, dab, dz, wall, offs, tm, carry=None):
    T = dz.shape[0]

    def body(dab_ref, dz_ref, wall_ref, o_ref, w13, sem):
        @pl.when(pl.program_id(0) == 0)
        def _():
            _load_rows(wall_ref, [(w13.at[pl.ds(0, DFF), :], offs[0], RF), (w13.at[pl.ds(DFF, DFF), :], offs[1], RF)], sem)

        for p in range(0, tm, PART_ROWS):
            rs = slice(p, min(p + PART_ROWS, tm))
            o_ref[rs, :] = ALPHA * dz_ref[rs, :] + _dot(dab_ref[rs, :], w13[...])

    row = lambda w: pl.BlockSpec((tm, w), lambda i: (i, 0))
    (out,), ex = _call(
        body, name=name, grid=(T // tm,), in_specs=[row(2 * DFF), row(D), HBM_SPEC], out_specs=[row(D)],
        out_shape=[jax.ShapeDtypeStruct((T, D), F32)],
        scratch_shapes=[pltpu.VMEM((2 * DFF, D), BF16), pltpu.SemaphoreType.DMA((2 * NDEV,))],
        ins=[dab, dz, wall], carry=carry)
    return out, ex


def _wgrad(name, a, b, bn, tk, col0=0, ncols=None, carry=None):
    T = a.shape[0]
    N = a.shape[1] if ncols is None else ncols
    nk = T // tk
    c0 = col0 // bn

    def body(a_ref, b_ref, o_ref, acc):
        k = pl.program_id(1)

        @pl.when(k == 0)
        def _():
            acc[...] = jnp.zeros_like(acc)

        acc[...] += _dot_tn(a_ref[...], b_ref[...])

        @pl.when(k == nk - 1)
        def _():
            o_ref[...] = acc[...].astype(BF16)

    (out,), ex = _call(
        body, name=name, grid=(N // bn, nk),
        in_specs=[pl.BlockSpec((tk, bn), lambda n, k: (k, n + c0)), pl.BlockSpec((tk, D), lambda n, k: (k, 0))],
        out_specs=[pl.BlockSpec((bn, D), lambda n, k: (n, 0))],
        out_shape=[jax.ShapeDtypeStruct((N, D), BF16)],
        scratch_shapes=[pltpu.VMEM((bn, D), F32)], ins=[a, b], carry=carry)
    return out, ex


def _rope_table(T):
    pos = jnp.arange(T, dtype=F32)
    inv_freq = ROPE_THETA ** (-jnp.arange(0, ROPE_DIM, 2, dtype=F32) / ROPE_DIM)
    ang = pos[:, None] * inv_freq[None, :]
    return jnp.pad(jnp.concatenate([jnp.cos(ang), jnp.sin(ang)], axis=1), ((0, 0), (0, 128 - ROPE_DIM)))


def _rope_expand(cs):
    half = ROPE_DIM // 2
    lane = lax.broadcasted_iota(jnp.int32, cs.shape, 1)
    cos = jnp.where(lane < half, cs, 0.0)
    sin = jnp.where((lane >= half) & (lane < ROPE_DIM), cs, 0.0)
    both = lambda t: t + pltpu.roll(t, HEAD_DIM, 1)
    c = jnp.where(jnp.bitwise_and(lane, HEAD_DIM - 1) < ROPE_DIM, both(cos + pltpu.roll(cos, half, 1)), 1.0)
    return c, -both(pltpu.roll(sin, 128 - half, 1)), both(sin)


def _rope(t, c, s1, s2):
    n = t.shape[1] // 128
    ct, s1t, s2t = (jnp.tile(v, (1, n)) for v in (c, s1, s2))
    w = t.shape[1]
    return t * ct + pltpu.roll(t, w - 8, 1) * s1t + pltpu.roll(t, 8, 1) * s2t


def _rope_t(dr, c, s1, s2):
    n = dr.shape[1] // 128
    ct, s1t, s2t = (jnp.tile(v, (1, n)) for v in (c, s1, s2))
    w = dr.shape[1]
    return dr * ct + pltpu.roll(dr * s1t, 8, 1) + pltpu.roll(dr * s2t, w - 8, 1)


_Q, _K, _V = (0, 1024), (1024, 256), (1280, 256)
_HG = (1536, HG_W)
_GATES = (5632, GATE_W)


def _inproj_fwd(name, xh, g, b, wall, b_in, rope, tm, carry=None):
    T = xh.shape[0]

    def body(xh_ref, g_ref, b_ref, wall_ref, bin_ref, cs_ref,
             qkv_ref, kt_ref, vt_ref, hg_ref, gate_ref, yb_ref, w, sem):
        @pl.when(pl.program_id(0) == 0)
        def _():
            _load_rows(wall_ref, [(w, 0, RIN)], sem)

        yb = (xh_ref[...] * g_ref[...] + b_ref[...]).astype(BF16)
        yb_ref[...] = yb
        c, s1, s2 = _rope_expand(cs_ref[...])

        def piece(start, width):
            return _dot_nt(yb, w[start:start + width, :]) + bin_ref[:, start:start + width]

        k = _rope(piece(*_K), c, s1, s2)
        v = piece(*_V)
        qkv_ref[:, 0:1024] = _rope(piece(*_Q), c, s1, s2).astype(BF16)
        qkv_ref[:, 1024:1280] = k.astype(BF16)
        qkv_ref[:, 1280:1536] = v.astype(BF16)
        kt_ref[...] = k.T.astype(BF16)
        vt_ref[...] = v.T.astype(BF16)
        for j in range(4):
            hg_ref[:, 1024 * j:1024 * (j + 1)] = piece(_HG[0] + 1024 * j, 1024)
        for j in range(2):
            gate_ref[:, 1024 * j:1024 * (j + 1)] = piece(_GATES[0] + 1024 * j, 1024)

    row = lambda wd: pl.BlockSpec((tm, wd), lambda i: (i, 0))
    vec = lambda wd: pl.BlockSpec((1, wd), lambda i: (0, 0))
    colt = pl.BlockSpec((256, tm), lambda i: (0, i))
    return _call(
        body, name=name, grid=(T // tm,),
        in_specs=[row(D), vec(D), vec(D), HBM_SPEC, vec(DIN), row(128)],
        out_specs=[row(QKV_W), colt, colt, row(HG_W), row(GATE_W), row(D)],
        out_shape=[jax.ShapeDtypeStruct((T, QKV_W), BF16), jax.ShapeDtypeStruct((256, T), BF16),
                   jax.ShapeDtypeStruct((256, T), BF16), jax.ShapeDtypeStruct((T, HG_W), F32),
                   jax.ShapeDtypeStruct((T, GATE_W), F32), jax.ShapeDtypeStruct((T, D), BF16)],
        scratch_shapes=[pltpu.VMEM((DIN, D), BF16), pltpu.SemaphoreType.DMA((NDEV,))],
        ins=[xh, g, b, wall, b_in, rope], carry=carry)


def _inproj_bwd(name, dq, dk, dv, dhg, dgates, dz2, wall, rope, tm):
    T = dq.shape[0]

    def body(dq_ref, dk_ref, dv_ref, d0, d1, d2, d3, dga_ref, dgh_ref, dz_ref, wall_ref, cs_ref,
             dy_ref, dproj_ref, dbin_ref, w, sem):
        @pl.when(pl.program_id(0) == 0)
        def _():
            _load_rows(wall_ref, [(w, 0, RIN)], sem)
            dbin_ref[...] = jnp.zeros_like(dbin_ref)

        c, s1, s2 = _rope_expand(cs_ref[...])
        acc = ALPHA * dz_ref[...]
        pieces = [(_Q[0], _rope_t(dq_ref[...], c, s1, s2)), (_K[0], _rope_t(dk_ref[...], c, s1, s2)),
                  (_V[0], dv_ref[...])]
        pieces += [(_HG[0] + 1024 * j, r[...]) for j, r in enumerate((d0, d1, d2, d3))]
        pieces += [(_GATES[0], dga_ref[...]), (_GATES[0] + 1024, dgh_ref[...])]
        for start, val in pieces:
            width = val.shape[1]
            dbin_ref[:, start:start + width] += jnp.sum(val.astype(F32), axis=0, keepdims=True)
            vb = val.astype(BF16)
            dproj_ref[:, start:start + width] = vb
            acc = acc + _dot(vb, w[start:start + width, :])
        dy_ref[...] = acc

    row = lambda wd: pl.BlockSpec((tm, wd), lambda i: (i, 0))
    return pl.pallas_call(
        body, name=name, grid=(T // tm,),
        in_specs=[row(D), row(256), row(256)] + [row(D)] * 4 + [row(D), row(D), row(D), HBM_SPEC, row(128)],
        out_specs=[row(D), row(DIN), pl.BlockSpec((1, DIN), lambda i: (0, 0))],
        out_shape=[jax.ShapeDtypeStruct((T, D), F32), jax.ShapeDtypeStruct((T, DIN), BF16),
                   jax.ShapeDtypeStruct((1, DIN), F32)],
        scratch_shapes=[pltpu.VMEM((DIN, D), BF16), pltpu.SemaphoreType.DMA((NDEV,))],
        compiler_params=_params("arbitrary"),
    )(dq, dk, dv, *dhg, *dgates, dz2, wall, rope)


def _halves(t):
    lane = lax.broadcasted_iota(jnp.int32, t.shape, 1)
    low = lane < HEAD_DIM
    sw = pltpu.roll(t, HEAD_DIM, 1)
    zero = jnp.zeros_like(t)
    h0 = (jnp.where(low, t, zero), jnp.where(low, zero, sw))
    h1 = (jnp.where(low, sw, zero), jnp.where(low, zero, t))
    return h0, h1


def _lane_stack(p_ref, c_ref, gp):
    sl = slice(128 * gp, 128 * (gp + 1))
    hp, hc = _halves(p_ref[:, sl].astype(F32)), _halves(c_ref[:, sl].astype(F32))
    return [jnp.concatenate([hp[gg][0], hc[gg][0], hp[gg][1], hc[gg][1]], axis=0).astype(BF16) for gg in range(2)]


def _row_stack(tp_ref, tc_ref, g):
    band = jnp.concatenate([tp_ref[64 * g:64 * (g + 1), :], tc_ref[64 * g:64 * (g + 1), :]], axis=1)
    z = jnp.zeros_like(band)
    return [jnp.concatenate([band, z], axis=0), jnp.concatenate([z, band], axis=0)]


def _fold(prevm, t4, hh):
    return jnp.where(prevm, t4[256 * hh:256 * hh + 128, :], t4[256 * hh + 128:256 * hh + 256, :])


def _unfold(prevm, t):
    return jnp.concatenate([jnp.where(prevm, t, 0.0), jnp.where(prevm, 0.0, t)], axis=0).astype(BF16)


def _attn_softmax(s, kill, sink):
    s = jnp.where(kill, NEG_INF, s)
    m = jnp.maximum(jnp.max(s, axis=0, keepdims=True), sink)
    p = jnp.exp(s - m)
    es = jnp.exp(sink - m)
    inv = 1.0 / (jnp.sum(p, axis=0, keepdims=True) + es)
    return p * inv, es * inv


def _attn_masks(first):
    row = lax.broadcasted_iota(jnp.int32, (ATTN_BLOCK, 2 * ATTN_BLOCK), 0)
    lane = lax.broadcasted_iota(jnp.int32, (ATTN_BLOCK, 2 * ATTN_BLOCK), 1)
    prevm = row > lane % ATTN_BLOCK
    return prevm, jnp.logical_and(first, prevm)


def _pair_rows(ref, g):
    return jnp.concatenate([ref[:, 256 * g:256 * g + 128], ref[:, 256 * g + 128:256 * (g + 1)]], axis=0)


def _pair_sinks(sink_ref, g, hh):
    h0, h1 = 4 * g + hh, 4 * g + 2 + hh
    return jnp.concatenate([jnp.broadcast_to(sink_ref[:, h0:h0 + 1], (1, ATTN_BLOCK)),
                            jnp.broadcast_to(sink_ref[:, h1:h1 + 1], (1, ATTN_BLOCK))], axis=1)


def _attn_fwd(name, qkv, vt, sinks):
    T = qkv.shape[0]
    nb = T // ATTN_BLOCK
    scale = HEAD_DIM ** -0.5

    def body(q_ref, kp_ref, kc_ref, vtp_ref, vtc_ref, sink_ref, o_ref):
        prevm, kill = _attn_masks(pl.program_id(0) == 0)
        kst = _lane_stack(kp_ref, kc_ref, 0) + _lane_stack(kp_ref, kc_ref, 1)
        vts = [_row_stack(vtp_ref, vtc_ref, g) for g in range(N_KV_HEADS)]
        s8 = [_dot_nt(kst[g], _pair_rows(q_ref, g)) * scale for g in range(N_KV_HEADS)]
        pu = [[_unfold(prevm, _attn_softmax(_fold(prevm, s8[g], hh), kill, _pair_sinks(sink_ref, g, hh))[0])
               for hh in range(2)] for g in range(N_KV_HEADS)]
        for g in range(N_KV_HEADS):
            ot = _dot(vts[g][0], pu[g][0]) + _dot(vts[g][1], pu[g][1])
            o_ref[:, 256 * g:256 * g + 128] = ot[:, :ATTN_BLOCK].T.astype(BF16)
            o_ref[:, 256 * g + 128:256 * (g + 1)] = ot[:, ATTN_BLOCK:].T.astype(BF16)

    prev = lambda i: jnp.maximum(i - 1, 0)
    return pl.pallas_call(
        body, name=name, grid=(nb,),
        in_specs=[pl.BlockSpec((ATTN_BLOCK, D), lambda i: (i, 0)),
                  pl.BlockSpec((ATTN_BLOCK, 256), lambda i: (prev(i), 4)),
                  pl.BlockSpec((ATTN_BLOCK, 256), lambda i: (i, 4)),
                  pl.BlockSpec((256, ATTN_BLOCK), lambda i: (0, prev(i))),
                  pl.BlockSpec((256, ATTN_BLOCK), lambda i: (0, i)),
                  pl.BlockSpec((1, N_Q_HEADS), lambda i: (0, 0))],
        out_specs=pl.BlockSpec((ATTN_BLOCK, D), lambda i: (i, 0)),
        out_shape=jax.ShapeDtypeStruct((T, D), BF16),
        compiler_params=_params("arbitrary"),
    )(qkv, qkv, qkv, vt, vt, sinks)


ATTN_BWD_ORDER = ((0, 0), (0, 1), (1, 0), (0, 2), (1, 1), (2, 0), (0, 3), (1, 2), (2, 1), (1, 3), (2, 2), (2, 3))


def _attn_bwd(name, qkv, kt, sinks, dya, carry=None):
    T = qkv.shape[0]
    nb = T // ATTN_BLOCK
    scale = HEAD_DIM ** -0.5

    def body(q_ref, kp_ref, kc_ref, vp_ref, vc_ref, ktp_ref, ktc_ref, sink_ref, do_ref,
             dq_ref, dk_ref, dv_ref, ds_ref):
        i = pl.program_id(0)

        @pl.when(i == 0)
        def _():
            ds_ref[...] = jnp.zeros_like(ds_ref)

        prevm, kill = _attn_masks(i == 0)
        kst = _lane_stack(kp_ref, kc_ref, 0) + _lane_stack(kp_ref, kc_ref, 1)
        vst = _lane_stack(vp_ref, vc_ref, 0) + _lane_stack(vp_ref, vc_ref, 1)
        kts = [_row_stack(ktp_ref, ktc_ref, g) for g in range(N_KV_HEADS)]
        lane = lax.broadcasted_iota(jnp.int32, (2 * ATTN_BLOCK, 128), 1)
        low = lane < HEAD_DIM
        slane = lax.broadcasted_iota(jnp.int32, (1, 128), 1)
        dkz = [jnp.zeros((2 * ATTN_BLOCK, 128), F32) for _ in range(N_KV_HEADS)]
        dvz = [jnp.zeros((2 * ATTN_BLOCK, 128), F32) for _ in range(N_KV_HEADS)]
        dsink = jnp.zeros((1, 128), F32)
        groups = range(N_KV_HEADS)
        qcat, docat, s8, dp8 = {}, {}, {}, {}
        ds_u, p_u, dsinks = {}, {}, []

        def scores(g):
            qcat[g], docat[g] = _pair_rows(q_ref, g), _pair_rows(do_ref, g)
            s8[g] = _dot_nt(kst[g], qcat[g]) * scale
            dp8[g] = _dot_nt(vst[g], docat[g])

        def algebra(g):
            for hh in range(2):
                pn, ps = _attn_softmax(_fold(prevm, s8[g], hh), kill, _pair_sinks(sink_ref, g, hh))
                dp = _fold(prevm, dp8[g], hh)
                delta = jnp.sum(pn * dp, axis=0, keepdims=True)
                sd = ps * delta
                dsinks.append(jnp.where(slane == 4 * g + hh, -jnp.sum(sd[:, :ATTN_BLOCK]), 0.0)
                              + jnp.where(slane == 4 * g + 2 + hh, -jnp.sum(sd[:, ATTN_BLOCK:]), 0.0))
                ds_u[g, hh] = _unfold(prevm, pn * (dp - delta) * scale)
                p_u[g, hh] = _unfold(prevm, pn)

        def grads(g):
            dqt = _dot(kts[g][0], ds_u[g, 0]) + _dot(kts[g][1], ds_u[g, 1])
            dq_ref[:, 256 * g:256 * g + 128] = dqt[:, :ATTN_BLOCK].T
            dq_ref[:, 256 * g + 128:256 * (g + 1)] = dqt[:, ATTN_BLOCK:].T
            for hh in range(2):
                own = low if hh == 0 else jnp.logical_not(low)
                dk_h = jnp.where(own, _dot(ds_u[g, hh], qcat[g]), 0.0)
                dv_h = jnp.where(own, _dot(p_u[g, hh], docat[g]), 0.0)
                if hh != g % 2:
                    dk_h = pltpu.roll(dk_h, HEAD_DIM, 1)
                    dv_h = pltpu.roll(dv_h, HEAD_DIM, 1)
                dkz[g] = dkz[g] + dk_h
                dvz[g] = dvz[g] + dv_h

        for stage, g in ATTN_BWD_ORDER:
            (scores, algebra, grads)[stage](g)
        for t in dsinks:
            dsink = dsink + t
        ds_ref[...] += dsink
        cur = pl.ds(pl.multiple_of(i * ATTN_BLOCK, ATTN_BLOCK), ATTN_BLOCK)
        prv = pl.ds(pl.multiple_of(jnp.maximum(i - 1, 0) * ATTN_BLOCK, ATTN_BLOCK), ATTN_BLOCK)
        for gp in range(N_KV_HEADS // 2):
            cols = slice(128 * gp, 128 * (gp + 1))
            dkb = dkz[2 * gp] + dkz[2 * gp + 1]
            dvb = dvz[2 * gp] + dvz[2 * gp + 1]
            dk_ref[cur, cols] = dkb[ATTN_BLOCK:, :]
            dv_ref[cur, cols] = dvb[ATTN_BLOCK:, :]

            @pl.when(i > 0)
            def _():
                dk_ref[prv, cols] += dkb[:ATTN_BLOCK, :]
                dv_ref[prv, cols] += dvb[:ATTN_BLOCK, :]

    prev = lambda i: jnp.maximum(i - 1, 0)
    whole = lambda w: pl.BlockSpec((T, w), lambda i: (0, 0))
    return _call(
        body, name=name, grid=(nb,), scratch_shapes=[], ins=[qkv, qkv, qkv, qkv, qkv, kt, kt, sinks, dya], carry=carry,
        in_specs=[pl.BlockSpec((ATTN_BLOCK, D), lambda i: (i, 0)),
                  pl.BlockSpec((ATTN_BLOCK, 256), lambda i: (prev(i), 4)),
                  pl.BlockSpec((ATTN_BLOCK, 256), lambda i: (i, 4)),
                  pl.BlockSpec((ATTN_BLOCK, 256), lambda i: (prev(i), 5)),
                  pl.BlockSpec((ATTN_BLOCK, 256), lambda i: (i, 5)),
                  pl.BlockSpec((256, ATTN_BLOCK), lambda i: (0, prev(i))),
                  pl.BlockSpec((256, ATTN_BLOCK), lambda i: (0, i)),
                  pl.BlockSpec((1, N_Q_HEADS), lambda i: (0, 0)),
                  pl.BlockSpec((ATTN_BLOCK, D), lambda i: (i, 0))],
        out_specs=[pl.BlockSpec((ATTN_BLOCK, D), lambda i: (i, 0)), whole(256), whole(256),
                   pl.BlockSpec((1, 128), lambda i: (0, 0))],
        out_shape=[jax.ShapeDtypeStruct((T, D), F32), jax.ShapeDtypeStruct((T, 256), F32),
                   jax.ShapeDtypeStruct((T, 256), F32), jax.ShapeDtypeStruct((1, 128), F32)])


HG_FWD = (256, 8, 512)
HG_BWD = (128, 8, 512)


def _chunk_sum(tri, x):
    w = x.shape[1]
    h1 = x.astype(BF16)
    h2 = (x - h1.astype(F32)).astype(BF16)
    r = _dot(tri, jnp.concatenate([h1, h2], axis=1))
    return r[:, :w] + r[:, w:]


def _chunk_tri(n, upper=False):
    ri = lax.broadcasted_iota(jnp.int32, (n, n), 0)
    ci = lax.broadcasted_iota(jnp.int32, (n, n), 1)
    same_chunk = jnp.bitwise_xor(ri, ci) < HGRN_CHUNK
    return jnp.where(((ri <= ci) if upper else (ri >= ci)) & same_chunk, 1.0, 0.0).astype(BF16)


def _chunks(t):
    return [t[HGRN_CHUNK * c:HGRN_CHUNK * (c + 1), :] for c in range(t.shape[0] // HGRN_CHUNK)]


def _lower_bound(lbl_ref):
    l0, l1 = lbl_ref[0:1, :], lbl_ref[1:2, :]
    m = jnp.maximum(l0, l1)
    e0, e1 = jnp.exp(l0 - m), jnp.exp(l1 - m)
    return e0 / (e0 + e1)


def _heads(t):
    return [t[:, 128 * h:128 * (h + 1)] for h in range(t.shape[1] // 128)]


def _per_head(fn, *wide):
    return jnp.concatenate([fn(*parts) for parts in zip(*[_heads(t) for t in wide])], axis=1)


def _hgrn_sub(fl, qh, vv, lb, tril_b):
    w = fl.shape[1]
    sg = _sigmoid(fl)
    f = lb + (1.0 - lb) * sg
    k = 1.0 - f
    gc = _chunk_sum(tril_b, jnp.log(f))
    gl_c = [t[HGRN_CHUNK - 1:HGRN_CHUNK, :] for t in _chunks(gc)]
    gl = jnp.concatenate([jnp.broadcast_to(g, (HGRN_CHUNK, w)) for g in gl_c], axis=0)
    sq = _sigmoid(qh)
    eg = jnp.exp(gc)
    eng = jnp.exp(-gc)
    elg = jnp.exp(gl - gc)
    qd = qh * sq * eg
    ki = k * eng
    ke = k * elg
    qd_b, ki_b, ke_b, v_b = (t.astype(BF16) for t in (qd, ki, ke, vv))
    am_b = [jnp.where(tril_b > 0, _dot_nt(qq, kk), 0.0).astype(BF16) for qq, kk in zip(_heads(qd_b), _heads(ki_b))]
    return dict(sg=sg, f=f, sq=sq, eg=eg, eng=eng, elg=elg, qd=qd, ki=ki, ke=ke, egl=[jnp.exp(g) for g in gl_c],
                am_b=am_b, qd_b=qd_b, ki_b=ki_b, ke_b=ke_b, v_b=v_b)


def _hgrn_out(q, st_b):
    outs = []
    for h, (qd_h, v_h) in enumerate(zip(_heads(q["qd_b"]), _heads(q["v_b"]))):
        inter = [_dot_nt(qc, s) for qc, s in zip(_chunks(qd_h), st_b[h])]
        outs.append(_dot(q["am_b"][h], v_h) + jnp.concatenate(inter, axis=0))
    return jnp.concatenate(outs, axis=1)


def _head_rms(o):
    return _per_head(lambda t: jnp.broadcast_to(
        lax.rsqrt(jnp.mean(t * t, axis=-1, keepdims=True) + RMS_EPS), t.shape), o)


def _hgrn_fwd(name, hg, lb_logits, norm_g, cfg):
    T = hg.shape[0]
    sub_rows, hps, th = min(cfg[0], T), cfg[1], min(cfg[2], T)
    cps = sub_rows // HGRN_CHUNK
    nc = th // HGRN_CHUNK

    def body(fl_ref, qh_ref, v_ref, og_ref, lbl_ref, ng_ref, y_ref, ss_ref, st):
        @pl.when(pl.program_id(1) == 0)
        def _():
            st[...] = jnp.zeros_like(st)

        lbs = _lower_bound(lbl_ref)
        ng = jnp.tile(ng_ref[...], (1, hps))
        tril_b = _chunk_tri(sub_rows)

        def sub(si, carry):
            rows = pl.ds(pl.multiple_of(si * sub_rows, sub_rows), sub_rows)
            q = _hgrn_sub(fl_ref[rows, :], qh_ref[rows, :], v_ref[rows, :], lbs, tril_b)
            v_hc = [_chunks(t) for t in _heads(q["v_b"])]
            k_hc = [_chunks(t) for t in _heads(q["ke_b"])]
            s = [st[h] for h in range(hps)]
            st_b = [[] for _ in range(hps)]
            for c in range(cps):
                egl = _heads(q["egl"][c])
                for h in range(hps):
                    ss_ref[h, si * cps + c] = s[h]
                    st_b[h].append(s[h].astype(BF16))
                    s[h] = s[h] * egl[h] + _dot_tn(v_hc[h][c], k_hc[h][c])
            for h in range(hps):
                st[h] = s[h]
            o = _hgrn_out(q, st_b)
            og = og_ref[rows, :]
            y_ref[rows, :] = (o * _head_rms(o) * ng * (og * _sigmoid(og))).astype(BF16)
            return carry

        lax.fori_loop(0, th // sub_rows, sub, 0)

    wd = 128 * hps
    col = lambda j: pl.BlockSpec((th, wd), lambda h, t: (t, (8 // hps) * j + h))
    return pl.pallas_call(
        body, name=name, grid=(HGRN_HEADS // hps, T // th),
        in_specs=[col(0), col(1), col(2), col(3), pl.BlockSpec((2, wd), lambda h, t: (0, h)),
                  pl.BlockSpec((1, 128), lambda h, t: (0, 0))],
        out_specs=[pl.BlockSpec((th, wd), lambda h, t: (t, h)),
                   pl.BlockSpec((hps, nc, 128, 128), lambda h, t: (h, t, 0, 0))],
        out_shape=[jax.ShapeDtypeStruct((T, D), BF16),
                   jax.ShapeDtypeStruct((HGRN_HEADS, T // HGRN_CHUNK, 128, 128), F32)],
        scratch_shapes=[pltpu.VMEM((hps, 128, 128), F32)],
        compiler_params=_params("parallel", "arbitrary"),
    )(hg, hg, hg, hg, lb_logits, norm_g)


def _hgrn_bwd(name, hg, lb_logits, norm_g, sstart, dyh, cfg, carry=None):
    T = hg.shape[0]
    sub_rows, hps, th = min(cfg[0], T), cfg[1], min(cfg[2], T)
    cps = sub_rows // HGRN_CHUNK
    nc = th // HGRN_CHUNK
    nt = T // th
    wd = 128 * hps

    def body(fl_ref, qh_ref, v_ref, og_ref, lbl_ref, ng_ref, ss_ref, dy_ref,
             dfl_ref, dqh_ref, dv_ref, dog_ref, dlb_ref, dng_ref, dst):
        @pl.when(pl.program_id(1) == 0)
        def _():
            dst[...] = jnp.zeros_like(dst)
            dlb_ref[...] = jnp.zeros_like(dlb_ref)
            dng_ref[...] = jnp.zeros_like(dng_ref)

        lb = _lower_bound(lbl_ref)
        ng = jnp.tile(ng_ref[...], (1, hps))
        last = lax.broadcasted_iota(jnp.int32, (HGRN_CHUNK, wd), 0) == HGRN_CHUNK - 1
        nsub = th // sub_rows
        cat0 = lambda parts: jnp.concatenate(parts, axis=0)
        tril_b, triu_b = _chunk_tri(sub_rows), _chunk_tri(sub_rows, upper=True)

        def sub(step, carry):
            si = nsub - 1 - step
            rows = pl.ds(pl.multiple_of(si * sub_rows, sub_rows), sub_rows)
            qh, og, dy = qh_ref[rows, :], og_ref[rows, :], dy_ref[rows, :]
            q = _hgrn_sub(fl_ref[rows, :], qh, v_ref[rows, :], lb, tril_b)
            s_in = [[ss_ref[h, si * cps + c] for c in range(cps)] for h in range(hps)]
            st_b = [[s.astype(BF16) for s in row] for row in s_in]
            o = _hgrn_out(q, st_b)
            rr = _head_rms(o)
            oh = o * rr
            sog = _sigmoid(og)
            dog_ref[rows, :] = (dy * oh * ng * (sog * (1.0 + og * (1.0 - sog)))).astype(BF16)
            don = dy * (og * sog)
            dng_w = jnp.sum(don * oh, axis=0, keepdims=True)
            dd = don * ng
            mean_h = _per_head(lambda t: jnp.broadcast_to(jnp.mean(t, axis=-1, keepdims=True), t.shape), dd * oh)
            do_b = (rr * (dd - oh * mean_h)).astype(BF16)
            do_h, qd_h, ki_h, ke_h, v_h = (_heads(t) for t in (do_b, q["qd_b"], q["ki_b"], q["ke_b"], q["v_b"]))
            da_b = [jnp.where(tril_b > 0, _dot_nt(do_h[h], v_h[h]), 0.0).astype(BF16) for h in range(hps)]
            do_c, qd_c, ke_c, v_c = ([_chunks(t) for t in hs] for hs in (do_h, qd_h, ke_h, v_h))
            dsp = [[None] * cps for _ in range(hps)]
            dgl_dec = [[None] * cps for _ in range(hps)]
            d = [dst[h] for h in range(hps)]
            for c in reversed(range(cps)):
                egl = _heads(q["egl"][c])
                for h in range(hps):
                    dsp[h][c] = d[h]
                    dgl_dec[h][c] = jnp.sum(d[h] * s_in[h][c], axis=0, keepdims=True) * egl[h]
                    d[h] = d[h] * egl[h] + _dot_tn(do_c[h][c], qd_c[h][c])
            for h in range(hps):
                dst[h] = d[h]
                dng_ref[h] += dng_w[:, 128 * h:128 * (h + 1)]
            dsp_b = [[t.astype(BF16) for t in row] for row in dsp]
            dv_ref[rows, :] = jnp.concatenate(
                [_dot_tn(q["am_b"][h], do_h[h]) + cat0([_dot_nt(ke_c[h][c], dsp_b[h][c]) for c in range(cps)])
                 for h in range(hps)], axis=1).astype(BF16)
            dqd = jnp.concatenate(
                [_dot(da_b[h], ki_h[h]) + cat0([_dot(do_c[h][c], st_b[h][c]) for c in range(cps)])
                 for h in range(hps)], axis=1)
            dki = jnp.concatenate([_dot_tn(da_b[h], qd_h[h]) for h in range(hps)], axis=1)
            dke = jnp.concatenate([cat0([_dot(v_c[h][c], dsp_b[h][c]) for c in range(cps)]) for h in range(hps)], axis=1)
            dkk = dke * q["ke"]
            dgl = [jnp.sum(t, axis=0, keepdims=True) + jnp.concatenate([dgl_dec[h][c] for h in range(hps)], axis=1)
                   for c, t in enumerate(_chunks(dkk))]
            dgc = dqd * q["qd"] - dki * q["ki"] - dkk + cat0([jnp.where(last, g, 0.0) for g in dgl])
            dlf = _chunk_sum(triu_b, dgc)
            df = dlf / q["f"] - (dki * q["eng"] + dke * q["elg"])
            sg = q["sg"]
            dlb_ref[...] += jnp.sum(df * (1.0 - sg), axis=0, keepdims=True)
            dfl_ref[rows, :] = (df * (1.0 - lb) * sg * (1.0 - sg)).astype(BF16)
            sq = q["sq"]
            dqh_ref[rows, :] = (dqd * q["eg"] * (sq * (1.0 + qh * (1.0 - sq)))).astype(BF16)
            return carry

        lax.fori_loop(0, nsub, sub, 0)
    col = lambda j: pl.BlockSpec((th, wd), lambda h, t: (nt - 1 - t, (8 // hps) * j + h))
    out = pl.BlockSpec((th, wd), lambda h, t: (nt - 1 - t, h))
    return _call(
        body, name=name, grid=(HGRN_HEADS // hps, nt),
        in_specs=[col(0), col(1), col(2), col(3), pl.BlockSpec((2, wd), lambda h, t: (0, h)),
                  pl.BlockSpec((1, 128), lambda h, t: (0, 0)),
                  pl.BlockSpec((hps, nc, 128, 128), lambda h, t: (h, nt - 1 - t, 0, 0)), out],
        out_specs=[out, out, out, out, pl.BlockSpec((1, wd), lambda h, t: (0, h)),
                   pl.BlockSpec((hps, 1, 128), lambda h, t: (h, 0, 0))],
        out_shape=[jax.ShapeDtypeStruct((T, D), BF16)] * 4 + [jax.ShapeDtypeStruct((1, D), F32),
                                                              jax.ShapeDtypeStruct((HGRN_HEADS, 1, 128), F32)],
        scratch_shapes=[pltpu.VMEM((hps, 128, 128), F32)],
        ins=[hg, hg, hg, hg, lb_logits, norm_g, sstart, dyh], carry=carry)


def _mix_fwd(name, ya, yh, gates, xh1, g1, b1, wall, tm):
    T = ya.shape[0]

    def body(ya_ref, yh_ref, ga_ref, gh_ref, xh_ref, g_ref, b_ref, wall_ref,
             xo_ref, r_ref, pa_ref, ph_ref, mg_ref, wpa, wph, wo, sem):
        @pl.when(pl.program_id(0) == 0)
        def _():
            _load_rows(wall_ref, [(wpa, RIN, RP), (wph, RIN + RP, RP), (wo, RIN + 2 * RP, RP)], sem)

        parts = [slice(p, min(p + PART_ROWS, tm)) for p in range(0, tm, PART_ROWS)]
        pas = [_dot(ya_ref[rs, :], wpa[...]).astype(BF16) for rs in parts]
        phs = [_dot(yh_ref[rs, :], wph[...]).astype(BF16) for rs in parts]
        mgs = []
        for rs, pa, ph in zip(parts, pas, phs):
            pa_ref[rs, :] = pa
            ph_ref[rs, :] = ph
            mgs.append((_sigmoid(ga_ref[rs, :]) * pa.astype(F32) + _sigmoid(gh_ref[rs, :]) * ph.astype(F32)).astype(BF16))
            mg_ref[rs, :] = mgs[-1]
        mixes = [_dot(mg, wo[...]) for mg in mgs]
        for rs, mix in zip(parts, mixes):
            y1 = xh_ref[rs, :] * g_ref[...] + b_ref[...]
            xh2, r = _ln_fwd(ALPHA * y1 + mix)
            xo_ref[rs, :] = xh2
            r_ref[rs, :] = jnp.broadcast_to(r, (xh2.shape[0], 128))

    row = lambda w: pl.BlockSpec((tm, w), lambda i: (i, 0))
    vec = pl.BlockSpec((1, D), lambda i: (0, 0))
    return pl.pallas_call(
        body, name=name, grid=(T // tm,),
        in_specs=[row(D), row(D), pl.BlockSpec((tm, D), lambda i: (i, 0)), pl.BlockSpec((tm, D), lambda i: (i, 1)),
                  row(D), vec, vec, HBM_SPEC],
        out_specs=[row(D), row(128), row(D), row(D), row(D)],
        out_shape=[jax.ShapeDtypeStruct((T, D), F32), jax.ShapeDtypeStruct((T, 128), F32)]
        + [jax.ShapeDtypeStruct((T, D), BF16)] * 3,
        scratch_shapes=[pltpu.VMEM((D, D), BF16)] * 3 + [pltpu.SemaphoreType.DMA((3 * NDEV,))],
        compiler_params=_params("arbitrary"),
    )(ya, yh, gates, gates, xh1, g1, b1, wall)


def _mix_bwd(name, dy2, xh2, r2, g2, gates, pa, ph, wall, tm, carry=None):
    T = dy2.shape[0]

    def body(dy_ref, xh_ref, r_ref, g_ref, ga_ref, gh_ref, pa_ref, ph_ref, wall_ref,
             dz_ref, dmix_ref, dpa_ref, dph_ref, dga_ref, dgh_ref, dya_ref, dyh_ref, dg_ref, db_ref,
             wpa, wph, wo, sem):
        @pl.when(pl.program_id(0) == 0)
        def _():
            _load_rows(wall_ref, [(wpa, RIN, RP), (wph, RIN + RP, RP), (wo, RIN + 2 * RP, RP)], sem)
            dg_ref[...] = jnp.zeros_like(dg_ref)
            db_ref[...] = jnp.zeros_like(db_ref)

        parts = [slice(p, min(p + PART_ROWS, tm)) for p in range(0, tm, PART_ROWS)]
        dmix = []
        for rs in parts:
            dz, dgp, dbp = _ln_bwd(dy_ref[rs, :], xh_ref[rs, :], r_ref[rs, :1], g_ref[...])
            dg_ref[...] += dgp
            db_ref[...] += dbp
            dz_ref[rs, :] = dz
            dmix.append(dz.astype(BF16))
            dmix_ref[rs, :] = dmix[-1]
        dmgs = [_dot_nt(d, wo[...]) for d in dmix]
        dpas, dphs = [], []
        for rs, dmg in zip(parts, dmgs):
            sa, sh = _sigmoid(ga_ref[rs, :]), _sigmoid(gh_ref[rs, :])
            dga_ref[rs, :] = (dmg * pa_ref[rs, :].astype(F32) * sa * (1.0 - sa)).astype(BF16)
            dgh_ref[rs, :] = (dmg * ph_ref[rs, :].astype(F32) * sh * (1.0 - sh)).astype(BF16)
            dpas.append((dmg * sa).astype(BF16))
            dphs.append((dmg * sh).astype(BF16))
            dpa_ref[rs, :] = dpas[-1]
            dph_ref[rs, :] = dphs[-1]
        for rs, dpa, dph in zip(parts, dpas, dphs):
            dya_ref[rs, :] = _dot_nt(dpa, wpa[...]).astype(BF16)
            dyh_ref[rs, :] = _dot_nt(dph, wph[...])

    row = lambda w: pl.BlockSpec((tm, w), lambda i: (i, 0))
    vec = pl.BlockSpec((1, D), lambda i: (0, 0))
    return _call(
        body, name=name, grid=(T // tm,),
        in_specs=[row(D), row(D), row(128), vec, pl.BlockSpec((tm, D), lambda i: (i, 0)),
                  pl.BlockSpec((tm, D), lambda i: (i, 1)), row(D), row(D), HBM_SPEC],
        out_specs=[row(D)] * 8 + [vec, vec],
        out_shape=[jax.ShapeDtypeStruct((T, D), F32)] + [jax.ShapeDtypeStruct((T, D), BF16)] * 6
        + [jax.ShapeDtypeStruct((T, D), F32)] + [jax.ShapeDtypeStruct((1, D), F32)] * 2,
        scratch_shapes=[pltpu.VMEM((D, D), BF16)] * 3 + [pltpu.SemaphoreType.DMA((3 * NDEV,))],
        ins=[dy2, xh2, r2, g2, gates, gates, pa, ph, wall], carry=carry)


def _adam(w, g, m, v):
    m = ADAM_B1 * m + (1.0 - ADAM_B1) * g
    v = ADAM_B2 * v + (1.0 - ADAM_B2) * (g * g)
    m_hat = m / (1.0 - ADAM_B1 ** ADAM_STEP)
    v_hat = v / (1.0 - ADAM_B2 ** ADAM_STEP)
    delta = -ADAM_LR * (m_hat / (jnp.sqrt(v_hat) + ADAM_EPS) + ADAM_WD * w)
    return delta, m, v


def _grad_steps(name, items, cb, carry=None):
    n = len(items)
    pieces = [len(it[0]) for it in items]

    def body(*refs):
        it = iter(refs[:sum(pieces) + 3 * n])
        outs = refs[sum(pieces) + 3 * n:]
        for i in range(n):
            accs = []
            for _ in range(pieces[i]):
                r_ref = next(it)
                acc = r_ref[0].astype(F32)
                for k in range(1, NDEV):
                    acc = acc + r_ref[k].astype(F32)
                accs.append(acc)
            acc = accs[0] if len(accs) == 1 else jnp.concatenate(accs, axis=0)
            w_ref, m_ref, v_ref = next(it), next(it), next(it)
            g_ref, d_ref, mo_ref, vo_ref = outs[4 * i:4 * i + 4]
            g_ref[...] = acc
            d, mm, vv = _adam(w_ref[...], acc, m_ref[...], v_ref[...])
            d_ref[...] = d
            mo_ref[...] = mm
            vo_ref[...] = vv

    in_specs, out_specs, out_shape, ins = [], [], [], []
    for recv, w, m, v in items:
        blk = pl.BlockSpec((w.shape[0], cb), lambda j: (0, j))
        in_specs += [pl.BlockSpec((NDEV, rp.shape[1], cb), lambda j: (0, 0, j)) for rp in recv] + [blk, blk, blk]
        out_specs += [blk] * 4
        out_shape += [jax.ShapeDtypeStruct(w.shape, F32)] * 4
        ins += list(recv) + [w, m, v]
    outs, ex = _call(body, name=name, grid=(D // cb,), in_specs=in_specs, out_specs=out_specs, out_shape=out_shape,
                     scratch_shapes=[], ins=ins, carry=carry)
    return [tuple(outs[4 * i:4 * i + 4]) for i in range(n)], ex


_SMALL = [("ln1_g", D), ("ln1_b", D), ("ln2_g", D), ("ln2_b", D), ("ln3_g", D), ("ln3_b", D),
          ("b_in", DIN), ("lb", D), ("attn_sinks", 128), ("hgrn_norm_g", 128), ("loss", 128)]
_SMALL_OFF = {}
_o = 0
for _n, _w in _SMALL:
    _SMALL_OFF[_n] = (_o, _w)
    _o += _w
PACK = _o


def _small_step(name, slots, small_w, small_m, small_v):
    names = ["ln1_g", "ln1_b", "ln2_g", "ln2_b", "ln3_g", "ln3_b", "b_in", "attn_sinks", "hgrn_lb_logits", "hgrn_norm_g"]
    np_ = len(names)

    def body(*refs):
        s_ref = refs[0]
        w_refs = refs[1:1 + np_]
        m_refs = refs[1 + np_:1 + 2 * np_]
        v_refs = refs[1 + 2 * np_:1 + 3 * np_]
        loss_ref, outs = refs[1 + 3 * np_], refs[2 + 3 * np_:]
        tot = s_ref[0]
        for k in range(1, NDEV):
            tot = tot + s_ref[k]

        def part(n):
            o, w = _SMALL_OFF[n]
            return tot[:, o:o + w]

        loss_ref[...] = part("loss")
        for i, n in enumerate(names):
            w = w_refs[i][...]
            if n == "hgrn_lb_logits":
                m_ = jnp.maximum(w[0:1, :], w[1:2, :])
                e0, e1 = jnp.exp(w[0:1, :] - m_), jnp.exp(w[1:2, :] - m_)
                p0 = e0 / (e0 + e1)
                t = p0 * (1.0 - p0) * part("lb")
                g = jnp.concatenate([t, -t], axis=0)
            elif n == "attn_sinks":
                g = part(n)[:, :N_Q_HEADS]
            else:
                g = part(n)
            d, mm, vv = _adam(w, g, m_refs[i][...], v_refs[i][...])
            outs[4 * i][...] = g
            outs[4 * i + 1][...] = d
            outs[4 * i + 2][...] = mm
            outs[4 * i + 3][...] = vv

    out_shape = [jax.ShapeDtypeStruct((1, 128), F32)]
    for n in names:
        out_shape += [jax.ShapeDtypeStruct(small_w[n].shape, F32)] * 4
    return pl.pallas_call(
        body, name=name, out_shape=out_shape,
        compiler_params=pltpu.CompilerParams(vmem_limit_bytes=VMEM_LIMIT),
    )(slots, *[small_w[n] for n in names], *[small_m[n] for n in names], *[small_v[n] for n in names]), names


def _tile(T, pref):
    return min(T, pref)


def kernel(x, ln1_g, ln1_b, ffn1_w1, ffn1_w3, ffn1_w2, ln2_g, ln2_b, w_in, b_in, attn_sinks, hgrn_lb_logits, hgrn_norm_g, w_proj_attn, w_proj_hgrn, w_out, ln3_g, ln3_b, ffn2_w1, ffn2_w3, ffn2_w2, loss_target, m_ln1_g, m_ln1_b, m_ffn1_w1, m_ffn1_w3, m_ffn1_w2, m_ln2_g, m_ln2_b, m_w_in, m_b_in, m_attn_sinks, m_hgrn_lb_logits, m_hgrn_norm_g, m_w_proj_attn, m_w_proj_hgrn, m_w_out, m_ln3_g, m_ln3_b, m_ffn2_w1, m_ffn2_w3, m_ffn2_w2, v_ln1_g, v_ln1_b, v_ffn1_w1, v_ffn1_w3, v_ffn1_w2, v_ln2_g, v_ln2_b, v_w_in, v_b_in, v_attn_sinks, v_hgrn_lb_logits, v_hgrn_norm_g, v_w_proj_attn, v_w_proj_hgrn, v_w_out, v_ln3_g, v_ln3_b, v_ffn2_w1, v_ffn2_w3, v_ffn2_w2):
    T = x.shape[1]
    xs = x[0]
    tgt = loss_target[0]
    tm = _tile(T, 256)
    tm2 = _tile(T, 512)
    tk = _tile(T, 2048)

    t_bf = lambda w: w[0].T.astype(BF16)
    n_bf = lambda w: w[0].astype(BF16)
    ffn_shard = lambda w1, w3, w2: jnp.concatenate([t_bf(w1), t_bf(w3), n_bf(w2)], axis=0)
    mix_shard = jnp.concatenate([t_bf(w_in), n_bf(w_proj_attn), n_bf(w_proj_hgrn), n_bf(w_out)], axis=0)
    (ffn1_all,) = _exchange_call("gather_ffn1", _gather_exchange(ffn_shard(ffn1_w1, ffn1_w3, ffn1_w2)))
    ffn_offs = (0, RF, 2 * RF)
    rope = _rope_table(T)

    (xh1, r1, a1, b1, xb0), (mix_all,) = _ffn_fwd("ffn1_fwd", xs, None, ffn1_all, ffn_offs, tm2,
                                                  carry=_gather_exchange(mix_shard))
    (qkv, kt, vt, hg, gates, y1b), (ffn2_all,) = _inproj_fwd(
        "inproj_fwd", xh1, ln1_g, ln1_b, mix_all, b_in, rope, tm2,
        carry=_gather_exchange(ffn_shard(ffn2_w1, ffn2_w3, ffn2_w2)))
    ya = _attn_fwd("attn_fwd", qkv, vt, attn_sinks)
    yh, sstart = _hgrn_fwd("hgrn_fwd", hg, hgrn_lb_logits, hgrn_norm_g, HG_FWD)
    xh2, r2, pa, ph, merged = _mix_fwd("mix_fwd", ya, yh, gates, xh1, ln1_g, ln1_b, mix_all, tm2)
    (xh3, r3, a2, b2, y2b, loss_part), _ = _ffn_fwd("ffn2_fwd", xh2, (ln2_g, ln2_b), ffn2_all, ffn_offs, tm2,
                                                    loss=(ln3_g, ln3_b, tgt))

    (dy2, dab2, u2, df2, dg3, db3), _ = _ffn_bwd("ffn2_bwd", ("loss", ln3_b, tgt), xh3, r3, ln3_g, a2, b2, ffn2_all,
                                                 ffn_offs, tm)
    recv = {}
    g_ffn2_13, _ = _wgrad("wgrad_ffn2_w13", dab2, y2b, DFF // 2, tk)
    g_ffn2_2, _ = _wgrad("wgrad_ffn2_w2", u2, df2, DFF // 2, tk)
    (dz2, dmix, dpa, dph, dga, dgh, dya, dyh, dg2, db2), (recv["ffn2_w1"],) = _mix_bwd(
        "mix_bwd", dy2, xh2, r2, ln2_g, gates, pa, ph, mix_all, tm2, carry=_grad_exchange([g_ffn2_13], [0], [RF]))
    g_wo, _ = _wgrad("wgrad_w_out", merged, dmix, D, tk)
    g_pa, _ = _wgrad("wgrad_w_pa", ya, dpa, D, tk)
    g_ph, _ = _wgrad("wgrad_w_ph", yh, dph, D, tk)
    (dq, dk, dv, dsink), (recv["ffn2_w3"],) = _attn_bwd(
        "attn_bwd", qkv, kt, attn_sinks, dya, carry=_grad_exchange([g_ffn2_13], [DFF], [RF]))
    (dfl, dqh, dih, dog, dlb, dng), (recv["ffn2_w2"], recv["w_proj_attn"], recv["w_proj_hgrn"], recv["w_out"]) = \
        _hgrn_bwd("hgrn_bwd", hg, hgrn_lb_logits, hgrn_norm_g, sstart, dyh, HG_BWD,
                  carry=_grad_exchange([g_ffn2_2, g_pa, g_ph, g_wo], [0, 0, 0, 0], [RF, RP, RP, RP]))
    dy1, dproj, dbin = _inproj_bwd("inproj_bwd", dq, dk, dv, (dfl, dqh, dih, dog), (dga, dgh), dz2, mix_all, rope, tm)
    g_win, _ = _wgrad("wgrad_w_in", dproj, y1b, DIN // 4, tk)
    win_rows = (400, 288, 272)
    win_base = (0, 400, 688)
    (dz1, dab1, u1, df1, dg1, db1), (rw0,) = _ffn_bwd(
        "ffn1_bwd", ("dy", dy1), xh1, r1, ln1_g, a1, b1, ffn1_all, ffn_offs, tm, with_dx=False,
        carry=_grad_exchange([g_win], win_base[:1], win_rows[:1], [RIN]))
    g_ffn1_2, (rw1,) = _wgrad("wgrad_ffn1_w2", u1, df1, DFF // 2, tk,
                              carry=_grad_exchange([g_win], win_base[1:2], win_rows[1:2], [RIN]))
    g_ffn1_1, (recv["ffn1_w2"],) = _wgrad("wgrad_ffn1_w1", dab1, xb0, DFF // 2, tk, 0, DFF,
                                          carry=_grad_exchange([g_ffn1_2], [0], [RF]))
    g_ffn1_3, (recv["ffn1_w1"],) = _wgrad("wgrad_ffn1_w3", dab1, xb0, DFF // 2, tk, DFF, DFF,
                                          carry=_grad_exchange([g_ffn1_1], [0], [RF]))
    parts = {"ln1_g": dg1, "ln1_b": db1, "ln2_g": dg2, "ln2_b": db2, "ln3_g": dg3, "ln3_b": db3, "b_in": dbin,
             "lb": dlb, "attn_sinks": dsink, "hgrn_norm_g": jnp.sum(dng, axis=0), "loss": loss_part[0:1, :]}
    packed = jnp.concatenate([parts[n] for n, _ in _SMALL], axis=1)
    gx, (recv["ffn1_w3"], rw2, small_slots) = _ffn_dx(
        "ffn1_dx", dab1, dz1, ffn1_all, ffn_offs, tm2,
        carry=_join(_grad_exchange([g_ffn1_3, g_win], [0, win_base[2]], [RF, win_rows[2]], [RF, RIN]),
                    _row_gather_exchange(packed)))
    recv["w_in"] = [rw0, rw1, rw2]

    big = [("ffn1_w1", ffn1_w1, m_ffn1_w1, v_ffn1_w1, True), ("ffn1_w3", ffn1_w3, m_ffn1_w3, v_ffn1_w3, True),
           ("ffn1_w2", ffn1_w2, m_ffn1_w2, v_ffn1_w2, False), ("w_in", w_in, m_w_in, v_w_in, True),
           ("w_proj_attn", w_proj_attn, m_w_proj_attn, v_w_proj_attn, False),
           ("w_proj_hgrn", w_proj_hgrn, m_w_proj_hgrn, v_w_proj_hgrn, False),
           ("w_out", w_out, m_w_out, v_w_out, False),
           ("ffn2_w1", ffn2_w1, m_ffn2_w1, v_ffn2_w1, True), ("ffn2_w3", ffn2_w3, m_ffn2_w3, v_ffn2_w3, True),
           ("ffn2_w2", ffn2_w2, m_ffn2_w2, v_ffn2_w2, False)]
    view = lambda t, transposed: t[0].T if transposed else t[0]
    back = lambda t, transposed: t.T[None] if transposed else t[None]
    slots = lambda n: recv[n] if isinstance(recv[n], list) else [recv[n]]
    stepped, _ = _grad_steps("steps", [(slots(n), view(w, tr), view(m, tr), view(v, tr)) for n, w, m, v, tr in big], 128)
    res = {n: tuple(back(t, tr) for t in outs) for (n, _, _, _, tr), outs in zip(big, stepped)}

    small_w = dict(ln1_g=ln1_g, ln1_b=ln1_b, ln2_g=ln2_g, ln2_b=ln2_b, ln3_g=ln3_g, ln3_b=ln3_b, b_in=b_in,
                   attn_sinks=attn_sinks, hgrn_lb_logits=hgrn_lb_logits, hgrn_norm_g=hgrn_norm_g)
    small_m = dict(ln1_g=m_ln1_g, ln1_b=m_ln1_b, ln2_g=m_ln2_g, ln2_b=m_ln2_b, ln3_g=m_ln3_g, ln3_b=m_ln3_b,
                   b_in=m_b_in, attn_sinks=m_attn_sinks, hgrn_lb_logits=m_hgrn_lb_logits, hgrn_norm_g=m_hgrn_norm_g)
    small_v = dict(ln1_g=v_ln1_g, ln1_b=v_ln1_b, ln2_g=v_ln2_g, ln2_b=v_ln2_b, ln3_g=v_ln3_g, ln3_b=v_ln3_b,
                   b_in=v_b_in, attn_sinks=v_attn_sinks, hgrn_lb_logits=v_hgrn_lb_logits, hgrn_norm_g=v_hgrn_norm_g)
    outs, names = _small_step("small_step", small_slots, small_w, small_m, small_v)
    loss = outs[0][0, 0]
    for i, n in enumerate(names):
        res[n] = tuple(outs[1 + 4 * i:5 + 4 * i])

    order = ["ln1_g", "ln1_b", "ffn1_w1", "ffn1_w3", "ffn1_w2", "ln2_g", "ln2_b", "w_in", "b_in", "attn_sinks",
             "hgrn_lb_logits", "hgrn_norm_g", "w_proj_attn", "w_proj_hgrn", "w_out", "ln3_g", "ln3_b",
             "ffn2_w1", "ffn2_w3", "ffn2_w2"]
    return (loss, gx[None], *[res[n][0] for n in order], *[res[n][1] for n in order],
            *[res[n][2] for n in order], *[res[n][3] for n in order])
```

```python
import jax
import jax.numpy as jnp
from jax import lax
from jax.experimental import pallas as pl
from jax.experimental.pallas import tpu as pltpu

F32 = jnp.float32
BF16 = jnp.bfloat16

NDEV = 8
D = 1024
DFF = 2816
RF = DFF // NDEV
DIN = 7680
RIN = DIN // NDEV
RP = D // NDEV
N_Q_HEADS = 16
N_KV_HEADS = 4
HEAD_DIM = 64
ATTN_BLOCK = 128
ROPE_THETA = 500000.0
ROPE_DIM = HEAD_DIM // 4
HGRN_HEADS = 8
HGRN_CHUNK = 64
ALPHA = 2.0 ** 0.25
LN_EPS = 1e-5
RMS_EPS = 1e-6
NEG_INF = -1e30
ADAM_LR = 0.001
ADAM_B1 = 0.9
ADAM_B2 = 0.999
ADAM_EPS = 1e-08
ADAM_WD = 0.01
ADAM_STEP = 10

QKV_W = 1536
HG_W = 4096
GATE_W = 2048
VMEM_LIMIT = 60 * 2 ** 20
PART_ROWS = 256
MESH = pl.DeviceIdType.MESH
HBM_SPEC = pl.BlockSpec(memory_space=pltpu.HBM)


def _params(*sem):
    return pltpu.CompilerParams(dimension_semantics=sem, vmem_limit_bytes=VMEM_LIMIT)


def _dot(a, b):
    return jnp.dot(a, b, preferred_element_type=F32)


def _dot_nt(a, b):
    return lax.dot_general(a, b, (((1,), (1,)), ((), ())), preferred_element_type=F32)


def _dot_tn(a, b):
    return lax.dot_general(a, b, (((0,), (0,)), ((), ())), preferred_element_type=F32)


def _sigmoid(x):
    return 0.5 * jnp.tanh(0.5 * x) + 0.5


def _ln_fwd(z):
    mu = jnp.mean(z, axis=-1, keepdims=True)
    zc = z - mu
    var = jnp.mean(zc * zc, axis=-1, keepdims=True)
    r = lax.rsqrt(var + LN_EPS)
    return zc * r, r


def _ln_bwd(dy, xh, r, g):
    dxh = dy * g
    m1 = jnp.mean(dxh, axis=-1, keepdims=True)
    m2 = jnp.mean(dxh * xh, axis=-1, keepdims=True)
    dz = r * (dxh - m1 - xh * m2)
    return dz, jnp.sum(dy * xh, axis=0, keepdims=True), jnp.sum(dy, axis=0, keepdims=True)


def _load_rows(wall_ref, pieces, sem):
    copies = []
    for dst, off, r in pieces:
        for k in range(NDEV):
            c = pltpu.make_async_copy(wall_ref.at[k, pl.ds(off, r), :], dst.at[pl.ds(k * r, r), :], sem.at[len(copies)])
            c.start()
            copies.append(c)
    for c in copies:
        c.wait()


class _Exchange:
    def __init__(self, inputs, out_shape, scratch, begin, middle, end):
        self.inputs, self.out_shape, self.scratch = inputs, out_shape, scratch
        self.begin, self.middle, self.end = begin, middle, end


def _gather_exchange(shard):
    rows, cols = shard.shape

    def ops(ins, outs, scr):
        (x_ref,), (out_ref,), (send_sems, recv_sems, local_sem) = ins, outs, scr
        x, y, c = lax.axis_index("x"), lax.axis_index("y"), lax.axis_index("c")
        me, sibling = (x, y, c), (x, y, 1 - c)
        chips = [(1 - x, y), (x, 1 - y), (1 - x, 1 - y)]

        def slot(px, py, pc):
            return out_ref.at[4 * px + 2 * py + pc]

        def copy(k, block, to, src=None):
            return pltpu.make_async_remote_copy(
                src_ref=slot(*block) if src is None else src, dst_ref=slot(*block),
                send_sem=send_sems.at[k], recv_sem=recv_sems.at[k], device_id=to, device_id_type=MESH)

        mine = lambda: pltpu.make_async_copy(x_ref, slot(*me), local_sem)
        first = lambda: [copy(0, me, sibling, src=x_ref)] + [
            copy(1 + j, me, (*chip, c), src=x_ref) for j, chip in enumerate(chips)]
        passed = lambda: [copy(4 + j, (*chip, c), sibling) for j, chip in enumerate(chips)]
        return c, me, sibling, chips, copy, mine, first, passed

    def begin(*refs):
        _, _, _, _, _, mine, first, _ = ops(*refs)
        mine().start()
        for cp in first():
            cp.start()

    def middle(*refs):
        c, me, _, chips, copy, _, _, passed = ops(*refs)
        for (j, chip), fwd in zip(enumerate(chips), passed()):
            copy(1 + j, (*chip, c), me).wait_recv()
            fwd.start()

    def end(*refs):
        c, me, sibling, chips, copy, mine, first, passed = ops(*refs)
        copy(0, sibling, me).wait_recv()
        for j, chip in enumerate(chips):
            copy(4 + j, (*chip, 1 - c), me).wait_recv()
        for cp in first() + passed():
            cp.wait_send()
        mine().wait()

    return _Exchange([shard], [jax.ShapeDtypeStruct((NDEV, rows, cols), shard.dtype)],
                     [pltpu.SemaphoreType.DMA((7,)), pltpu.SemaphoreType.DMA((7,)), pltpu.SemaphoreType.DMA],
                     begin, middle, end)


def _grad_exchange(grads, bases, rows, strides=None):
    n = len(grads)
    strides = rows if strides is None else strides

    def copies(g_refs, out_refs, scr):
        send_sems, recv_sems, local_sems = scr
        x, y, c = lax.axis_index("x"), lax.axis_index("y"), lax.axis_index("c")
        me = 4 * x + 2 * y + c
        out = []
        for i in range(n):
            r = rows[i]
            src = lambda k: g_refs[i].at[pl.ds(pl.multiple_of(bases[i] + k * strides[i], 16), r), :]
            out.append(pltpu.make_async_copy(src(me), out_refs[i].at[me], local_sems.at[i]))
            for j in range(1, NDEV):
                px, py, pc = x ^ (j >> 2), y ^ ((j >> 1) & 1), c ^ (j & 1)
                out.append(pltpu.make_async_remote_copy(
                    src_ref=src(4 * px + 2 * py + pc), dst_ref=out_refs[i].at[me],
                    send_sem=send_sems.at[i, j - 1], recv_sem=recv_sems.at[i, j - 1],
                    device_id=(px, py, pc), device_id_type=MESH))
        return out

    def begin(*refs):
        for cp in copies(*refs):
            cp.start()

    def end(*refs):
        for cp in copies(*refs):
            cp.wait()

    return _Exchange(list(grads), [jax.ShapeDtypeStruct((NDEV, r, g.shape[1]), g.dtype) for g, r in zip(grads, rows)],
                     [pltpu.SemaphoreType.DMA((n, NDEV - 1)), pltpu.SemaphoreType.DMA((n, NDEV - 1)),
                      pltpu.SemaphoreType.DMA((n,))], begin, None, end)


def _row_gather_exchange(row):
    def copies(ins, outs, scr):
        (r_ref,), (o_ref,), (send_sems, recv_sems, local_sem) = ins, outs, scr
        x, y, c = lax.axis_index("x"), lax.axis_index("y"), lax.axis_index("c")
        me = 4 * x + 2 * y + c
        out = [pltpu.make_async_copy(r_ref, o_ref.at[me], local_sem)]
        for j in range(1, NDEV):
            out.append(pltpu.make_async_remote_copy(
                src_ref=r_ref, dst_ref=o_ref.at[me], send_sem=send_sems.at[j - 1], recv_sem=recv_sems.at[j - 1],
                device_id=(x ^ (j >> 2), y ^ ((j >> 1) & 1), c ^ (j & 1)), device_id_type=MESH))
        return out

    def begin(*refs):
        for cp in copies(*refs):
            cp.start()

    def end(*refs):
        for cp in copies(*refs):
            cp.wait()

    return _Exchange([row], [jax.ShapeDtypeStruct((NDEV,) + row.shape, row.dtype)],
                     [pltpu.SemaphoreType.DMA((NDEV - 1,)), pltpu.SemaphoreType.DMA((NDEV - 1,)), pltpu.SemaphoreType.DMA],
                     begin, None, end)


def _join(a, b):
    na, oa, sa = len(a.inputs), len(a.out_shape), len(a.scratch)
    split = lambda ins, outs, scr: ((ins[:na], outs[:oa], scr[:sa]), (ins[na:], outs[oa:], scr[sa:]))

    def begin(*refs):
        pa, pb = split(*refs)
        a.begin(*pa)
        b.begin(*pb)

    def end(*refs):
        pa, pb = split(*refs)
        a.end(*pa)
        b.end(*pb)

    return _Exchange(a.inputs + b.inputs, a.out_shape + b.out_shape, a.scratch + b.scratch, begin, None, end)


def _exchange_call(name, ex):
    ni, no = len(ex.inputs), len(ex.out_shape)

    def body(*refs):
        parts = (refs[:ni], refs[ni:ni + no], refs[ni + no:])
        ex.begin(*parts)
        if ex.middle is not None:
            ex.middle(*parts)
        ex.end(*parts)

    return pl.pallas_call(body, name=name, out_shape=ex.out_shape, in_specs=[HBM_SPEC] * ni, out_specs=[HBM_SPEC] * no,
                          scratch_shapes=ex.scratch)(*ex.inputs)


def _call(body, *, name, grid, in_specs, out_specs, out_shape, scratch_shapes, ins, carry=None):
    sem = ("arbitrary",) * len(grid)
    if carry is None:
        outs = pl.pallas_call(body, name=name, grid=grid, in_specs=in_specs, out_specs=out_specs, out_shape=out_shape,
                              scratch_shapes=scratch_shapes, compiler_params=_params(*sem))(*ins)
        return outs, None
    n_in, n_out, n_scr = len(ins), len(out_shape), len(scratch_shapes)
    ci, co = len(carry.inputs), len(carry.out_shape)
    total = 1
    for g in grid:
        total *= g

    def wrapped(*refs):
        own_in, ex_in = refs[:n_in], refs[n_in:n_in + ci]
        o0 = n_in + ci
        own_out, ex_out = refs[o0:o0 + n_out], refs[o0 + n_out:o0 + n_out + co]
        s0 = o0 + n_out + co
        own_scr, ex_scr = refs[s0:s0 + n_scr], refs[s0 + n_scr:]
        step = pl.program_id(0)
        for d in range(1, len(grid)):
            step = step * grid[d] + pl.program_id(d)
        parts = (ex_in, ex_out, ex_scr)
        pl.when(step == 0)(lambda: carry.begin(*parts))
        body(*own_in, *own_out, *own_scr)
        if carry.middle is not None:
            pl.when(step == (3 * total) // 4)(lambda: carry.middle(*parts))
        pl.when(step == total - 1)(lambda: carry.end(*parts))

    outs = pl.pallas_call(
        wrapped, name=name, grid=grid, in_specs=list(in_specs) + [HBM_SPEC] * ci,
        out_specs=list(out_specs) + [HBM_SPEC] * co, out_shape=list(out_shape) + list(carry.out_shape),
        scratch_shapes=list(scratch_shapes) + list(carry.scratch), compiler_params=_params(*sem),
    )(*ins, *carry.inputs)
    return outs[:n_out], outs[n_out:]


def _ffn_fwd(name, xin, affine, wall, offs, tm, loss=None, carry=None):
    T = xin.shape[0]
    nt = T // tm

    def body(*refs):
        it = iter(refs)
        x_ref = next(it)
        if affine is not None:
            g_ref, b_ref = next(it), next(it)
        wall_ref = next(it)
        if loss is not None:
            go_ref, bo_ref, tgt_ref = next(it), next(it), next(it)
        xh_ref, r_ref, a_ref, b2_ref, yb_ref = (next(it) for _ in range(5))
        if loss is not None:
            loss_ref = next(it)
        w1, w3, w2, sem = (next(it) for _ in range(4))

        @pl.when(pl.program_id(0) == 0)
        def _():
            _load_rows(wall_ref, [(w1, offs[0], RF), (w3, offs[1], RF), (w2, offs[2], RF)], sem)
            if loss is not None:
                loss_ref[...] = jnp.zeros_like(loss_ref)

        parts = [slice(p, min(p + PART_ROWS, tm)) for p in range(0, tm, PART_ROWS)]
        ys, ybs = [], []
        for rs in parts:
            y = x_ref[rs, :]
            if affine is not None:
                y = y * g_ref[...] + b_ref[...]
            ys.append(y)
            ybs.append(y.astype(BF16))
            yb_ref[rs, :] = ybs[-1]
        ab = [(_dot_nt(yb, w1[...]).astype(BF16), _dot_nt(yb, w3[...]).astype(BF16)) for yb in ybs]
        us = []
        for rs, (a, b) in zip(parts, ab):
            a_ref[rs, :] = a
            b2_ref[rs, :] = b
            af, bf = a.astype(F32), b.astype(F32)
            us.append((af * _sigmoid(af) * bf).astype(BF16))
        fs = [_dot(u, w2[...]) for u in us]
        for rs, y, f in zip(parts, ys, fs):
            xh, r = _ln_fwd(ALPHA * y + 0.5 * f)
            xh_ref[rs, :] = xh
            r_ref[rs, :] = jnp.broadcast_to(r, (xh.shape[0], 128))
            if loss is not None:
                e = xh * go_ref[...] + bo_ref[...] - tgt_ref[rs, :]
                loss_ref[...] += jnp.sum(e * e) * (0.5 / D)

    row = lambda w: pl.BlockSpec((tm, w), lambda i: (i, 0))
    vec = pl.BlockSpec((1, D), lambda i: (0, 0))
    ins, in_specs = [xin], [row(D)]
    if affine is not None:
        ins += list(affine)
        in_specs += [vec, vec]
    ins.append(wall)
    in_specs.append(HBM_SPEC)
    if loss is not None:
        ins += list(loss)
        in_specs += [vec, vec, row(D)]
    out_shape = [jax.ShapeDtypeStruct((T, D), F32), jax.ShapeDtypeStruct((T, 128), F32),
                 jax.ShapeDtypeStruct((T, DFF), BF16), jax.ShapeDtypeStruct((T, DFF), BF16),
                 jax.ShapeDtypeStruct((T, D), BF16)]
    out_specs = [row(D), row(128), row(DFF), row(DFF), row(D)]
    if loss is not None:
        out_shape.append(jax.ShapeDtypeStruct((8, 128), F32))
        out_specs.append(pl.BlockSpec((8, 128), lambda i: (0, 0)))
    return _call(body, name=name, grid=(nt,), in_specs=in_specs, out_specs=out_specs, out_shape=out_shape,
                 scratch_shapes=[pltpu.VMEM((DFF, D), BF16)] * 3 + [pltpu.SemaphoreType.DMA((3 * NDEV,))],
                 ins=ins, carry=carry)


def _ffn_bwd(name, dy_src, xh, r, g, a, b, wall, offs, tm, with_dx=True, carry=None):
    T = xh.shape[0]
    nt = T // tm
    from_loss = dy_src[0] == "loss"

    def body(*refs):
        it = iter(refs)
        if from_loss:
            bo_ref, tgt_ref = next(it), next(it)
        else:
            dy_ref = next(it)
        xh_ref, r_ref, g_ref, a_ref, b_ref, wall_ref = (next(it) for _ in range(6))
        dyin_ref, dab_ref, u_ref, df_ref, dg_ref, db_ref = (next(it) for _ in range(6))
        if with_dx:
            w1, w3 = next(it), next(it)
        w2, sem = next(it), next(it)

        @pl.when(pl.program_id(0) == 0)
        def _():
            _load_rows(wall_ref, ([(w1, offs[0], RF), (w3, offs[1], RF)] if with_dx else []) + [(w2, offs[2], RF)], sem)
            dg_ref[...] = jnp.zeros_like(dg_ref)
            db_ref[...] = jnp.zeros_like(db_ref)

        parts = [slice(p, min(p + PART_ROWS, tm)) for p in range(0, tm, PART_ROWS)]
        gv = g_ref[...]
        dz, df = [], []
        for rs in parts:
            xhv = xh_ref[rs, :]
            if from_loss:
                dy = (xhv * gv + bo_ref[...] - tgt_ref[rs, :]) * (1.0 / D)
            else:
                dy = dy_ref[rs, :]
            dzp, dgp, dbp = _ln_bwd(dy, xhv, r_ref[rs, :1], gv)
            dg_ref[...] += dgp
            db_ref[...] += dbp
            dz.append(dzp)
            df.append((0.5 * dzp).astype(BF16))
            df_ref[rs, :] = df[-1]
        du = [_dot_nt(d, w2[...]) for d in df]
        da, dbb = [], []
        for rs, dup in zip(parts, du):
            af, bf = a_ref[rs, :].astype(F32), b_ref[rs, :].astype(F32)
            s = _sigmoid(af)
            sl = af * s
            u_ref[rs, :] = (sl * bf).astype(BF16)
            da.append((dup * bf * (s * (1.0 + af * (1.0 - s)))).astype(BF16))
            dbb.append((dup * sl).astype(BF16))
            dab_ref[rs, :DFF] = da[-1]
            dab_ref[rs, DFF:] = dbb[-1]
        for rs, dzp, dap, dbp in zip(parts, dz, da, dbb):
            dyin_ref[rs, :] = ALPHA * dzp + _dot(dap, w1[...]) + _dot(dbp, w3[...]) if with_dx else dzp

    row = lambda w: pl.BlockSpec((tm, w), lambda i: (i, 0))
    vec = pl.BlockSpec((1, D), lambda i: (0, 0))
    if from_loss:
        ins, in_specs = [dy_src[1], dy_src[2]], [vec, row(D)]
    else:
        ins, in_specs = [dy_src[1]], [row(D)]
    ins += [xh, r, g, a, b, wall]
    in_specs += [row(D), row(128), vec, row(DFF), row(DFF), HBM_SPEC]
    return _call(
        body, name=name, grid=(nt,), in_specs=in_specs,
        out_specs=[row(D), row(2 * DFF), row(DFF), row(D), vec, vec],
        out_shape=[jax.ShapeDtypeStruct((T, D), F32), jax.ShapeDtypeStruct((T, 2 * DFF), BF16),
                   jax.ShapeDtypeStruct((T, DFF), BF16), jax.ShapeDtypeStruct((T, D), BF16),
                   jax.ShapeDtypeStruct((1, D), F32), jax.ShapeDtypeStruct((1, D), F32)],
        scratch_shapes=[pltpu.VMEM((DFF, D), BF16)] * (3 if with_dx else 1) + [pltpu.SemaphoreType.DMA((3 * NDEV,))],
        ins=ins, carry=carry)


def _ffn_dx(name, dab, dz, wall, offs, tm, carry=None):
    T = dz.shape[0]

    def body(dab_ref, dz_ref, wall_ref, o_ref, w13, sem):
        @pl.when(pl.program_id(0) == 0)
        def _():
            _load_rows(wall_ref, [(w13.at[pl.ds(0, DFF), :], offs[0], RF), (w13.at[pl.ds(DFF, DFF), :], offs[1], RF)], sem)

        for p in range(0, tm, PART_ROWS):
            rs = slice(p, min(p + PART_ROWS, tm))
            o_ref[rs, :] = ALPHA * dz_ref[rs, :] + _dot(dab_ref[rs, :], w13[...])

    row = lambda w: pl.BlockSpec((tm, w), lambda i: (i, 0))
    (out,), ex = _call(
        body, name=name, grid=(T // tm,), in_specs=[row(2 * DFF), row(D), HBM_SPEC], out_specs=[row(D)],
        out_shape=[jax.ShapeDtypeStruct((T, D), F32)],
        scratch_shapes=[pltpu.VMEM((2 * DFF, D), BF16), pltpu.SemaphoreType.DMA((2 * NDEV,))],
        ins=[dab, dz, wall], carry=carry)
    return out, ex


def _wgrad(name, a, b, bn, tk, col0=0, ncols=None, carry=None):
    T = a.shape[0]
    N = a.shape[1] if ncols is None else ncols
    nk = T // tk
    c0 = col0 // bn

    def body(a_ref, b_ref, o_ref, acc):
        k = pl.program_id(1)

        @pl.when(k == 0)
        def _():
            acc[...] = jnp.zeros_like(acc)

        acc[...] += _dot_tn(a_ref[...], b_ref[...])

        @pl.when(k == nk - 1)
        def _():
            o_ref[...] = acc[...].astype(BF16)

    (out,), ex = _call(
        body, name=name, grid=(N // bn, nk),
        in_specs=[pl.BlockSpec((tk, bn), lambda n, k: (k, n + c0)), pl.BlockSpec((tk, D), lambda n, k: (k, 0))],
        out_specs=[pl.BlockSpec((bn, D), lambda n, k: (n, 0))],
        out_shape=[jax.ShapeDtypeStruct((N, D), BF16)],
        scratch_shapes=[pltpu.VMEM((bn, D), F32)], ins=[a, b], carry=carry)
    return out, ex


def _rope_table(T):
    pos = jnp.arange(T, dtype=F32)
    inv_freq = ROPE_THETA ** (-jnp.arange(0, ROPE_DIM, 2, dtype=F32) / ROPE_DIM)
    ang = pos[:, None] * inv_freq[None, :]
    return jnp.pad(jnp.concatenate([jnp.cos(ang), jnp.sin(ang)], axis=1), ((0, 0), (0, 128 - ROPE_DIM)))


def _rope_expand(cs):
    half = ROPE_DIM // 2
    lane = lax.broadcasted_iota(jnp.int32, cs.shape, 1)
    cos = jnp.where(lane < half, cs, 0.0)
    sin = jnp.where((lane >= half) & (lane < ROPE_DIM), cs, 0.0)
    both = lambda t: t + pltpu.roll(t, HEAD_DIM, 1)
    c = jnp.where(jnp.bitwise_and(lane, HEAD_DIM - 1) < ROPE_DIM, both(cos + pltpu.roll(cos, half, 1)), 1.0)
    return c, -both(pltpu.roll(sin, 128 - half, 1)), both(sin)


def _rope(t, c, s1, s2):
    n = t.shape[1] // 128
    ct, s1t, s2t = (jnp.tile(v, (1, n)) for v in (c, s1, s2))
    w = t.shape[1]
    return t * ct + pltpu.roll(t, w - 8, 1) * s1t + pltpu.roll(t, 8, 1) * s2t


def _rope_t(dr, c, s1, s2):
    n = dr.shape[1] // 128
    ct, s1t, s2t = (jnp.tile(v, (1, n)) for v in (c, s1, s2))
    w = dr.shape[1]
    return dr * ct + pltpu.roll(dr * s1t, 8, 1) + pltpu.roll(dr * s2t, w - 8, 1)


_Q, _K, _V = (0, 1024), (1024, 256), (1280, 256)
_HG = (1536, HG_W)
_GATES = (5632, GATE_W)


def _inproj_fwd(name, xh, g, b, wall, b_in, rope, tm, carry=None):
    T = xh.shape[0]

    def body(xh_ref, g_ref, b_ref, wall_ref, bin_ref, cs_ref,
             qkv_ref, kt_ref, vt_ref, hg_ref, gate_ref, yb_ref, w, sem):
        @pl.when(pl.program_id(0) == 0)
        def _():
            _load_rows(wall_ref, [(w, 0, RIN)], sem)

        yb = (xh_ref[...] * g_ref[...] + b_ref[...]).astype(BF16)
        yb_ref[...] = yb
        c, s1, s2 = _rope_expand(cs_ref[...])

        def piece(start, width):
            return _dot_nt(yb, w[start:start + width, :]) + bin_ref[:, start:start + width]

        k = _rope(piece(*_K), c, s1, s2)
        v = piece(*_V)
        qkv_ref[:, 0:1024] = _rope(piece(*_Q), c, s1, s2).astype(BF16)
        qkv_ref[:, 1024:1280] = k.astype(BF16)
        qkv_ref[:, 1280:1536] = v.astype(BF16)
        kt_ref[...] = k.T.astype(BF16)
        vt_ref[...] = v.T.astype(BF16)
        for j in range(4):
            hg_ref[:, 1024 * j:1024 * (j + 1)] = piece(_HG[0] + 1024 * j, 1024)
        for j in range(2):
            gate_ref[:, 1024 * j:1024 * (j + 1)] = piece(_GATES[0] + 1024 * j, 1024)

    row = lambda wd: pl.BlockSpec((tm, wd), lambda i: (i, 0))
    vec = lambda wd: pl.BlockSpec((1, wd), lambda i: (0, 0))
    colt = pl.BlockSpec((256, tm), lambda i: (0, i))
    return _call(
        body, name=name, grid=(T // tm,),
        in_specs=[row(D), vec(D), vec(D), HBM_SPEC, vec(DIN), row(128)],
        out_specs=[row(QKV_W), colt, colt, row(HG_W), row(GATE_W), row(D)],
        out_shape=[jax.ShapeDtypeStruct((T, QKV_W), BF16), jax.ShapeDtypeStruct((256, T), BF16),
                   jax.ShapeDtypeStruct((256, T), BF16), jax.ShapeDtypeStruct((T, HG_W), F32),
                   jax.ShapeDtypeStruct((T, GATE_W), F32), jax.ShapeDtypeStruct((T, D), BF16)],
        scratch_shapes=[pltpu.VMEM((DIN, D), BF16), pltpu.SemaphoreType.DMA((NDEV,))],
        ins=[xh, g, b, wall, b_in, rope], carry=carry)


def _inproj_bwd(name, dq, dk, dv, dhg, dgates, dz2, wall, rope, tm):
    T = dq.shape[0]

    def body(dq_ref, dk_ref, dv_ref, d0, d1, d2, d3, dga_ref, dgh_ref, dz_ref, wall_ref, cs_ref,
             dy_ref, dproj_ref, dbin_ref, w, sem):
        @pl.when(pl.program_id(0) == 0)
        def _():
            _load_rows(wall_ref, [(w, 0, RIN)], sem)
            dbin_ref[...] = jnp.zeros_like(dbin_ref)

        c, s1, s2 = _rope_expand(cs_ref[...])
        acc = ALPHA * dz_ref[...]
        pieces = [(_Q[0], _rope_t(dq_ref[...], c, s1, s2)), (_K[0], _rope_t(dk_ref[...], c, s1, s2)),
                  (_V[0], dv_ref[...])]
        pieces += [(_HG[0] + 1024 * j, r[...]) for j, r in enumerate((d0, d1, d2, d3))]
        pieces += [(_GATES[0], dga_ref[...]), (_GATES[0] + 1024, dgh_ref[...])]
        for start, val in pieces:
            width = val.shape[1]
            dbin_ref[:, start:start + width] += jnp.sum(val.astype(F32), axis=0, keepdims=True)
            vb = val.astype(BF16)
            dproj_ref[:, start:start + width] = vb
            acc = acc + _dot(vb, w[start:start + width, :])
        dy_ref[...] = acc

    row = lambda wd: pl.BlockSpec((tm, wd), lambda i: (i, 0))
    return pl.pallas_call(
        body, name=name, grid=(T // tm,),
        in_specs=[row(D), row(256), row(256)] + [row(D)] * 4 + [row(D), row(D), row(D), HBM_SPEC, row(128)],
        out_specs=[row(D), row(DIN), pl.BlockSpec((1, DIN), lambda i: (0, 0))],
        out_shape=[jax.ShapeDtypeStruct((T, D), F32), jax.ShapeDtypeStruct((T, DIN), BF16),
                   jax.ShapeDtypeStruct((1, DIN), F32)],
        scratch_shapes=[pltpu.VMEM((DIN, D), BF16), pltpu.SemaphoreType.DMA((NDEV,))],
        compiler_params=_params("arbitrary"),
    )(dq, dk, dv, *dhg, *dgates, dz2, wall, rope)


def _halves(t):
    lane = lax.broadcasted_iota(jnp.int32, t.shape, 1)
    low = lane < HEAD_DIM
    sw = pltpu.roll(t, HEAD_DIM, 1)
    zero = jnp.zeros_like(t)
    h0 = (jnp.where(low, t, zero), jnp.where(low, zero, sw))
    h1 = (jnp.where(low, sw, zero), jnp.where(low, zero, t))
    return h0, h1


def _lane_stack(p_ref, c_ref, gp):
    sl = slice(128 * gp, 128 * (gp + 1))
    hp, hc = _halves(p_ref[:, sl].astype(F32)), _halves(c_ref[:, sl].astype(F32))
    return [jnp.concatenate([hp[gg][0], hc[gg][0], hp[gg][1], hc[gg][1]], axis=0).astype(BF16) for gg in range(2)]


def _row_stack(tp_ref, tc_ref, g):
    band = jnp.concatenate([tp_ref[64 * g:64 * (g + 1), :], tc_ref[64 * g:64 * (g + 1), :]], axis=1)
    z = jnp.zeros_like(band)
    return [jnp.concatenate([band, z], axis=0), jnp.concatenate([z, band], axis=0)]


def _fold(prevm, t4, hh):
    return jnp.where(prevm, t4[256 * hh:256 * hh + 128, :], t4[256 * hh + 128:256 * hh + 256, :])


def _unfold(prevm, t):
    return jnp.concatenate([jnp.where(prevm, t, 0.0), jnp.where(prevm, 0.0, t)], axis=0).astype(BF16)


def _attn_softmax(s, kill, sink):
    s = jnp.where(kill, NEG_INF, s)
    m = jnp.maximum(jnp.max(s, axis=0, keepdims=True), sink)
    p = jnp.exp(s - m)
    es = jnp.exp(sink - m)
    inv = 1.0 / (jnp.sum(p, axis=0, keepdims=True) + es)
    return p * inv, es * inv


def _attn_masks(first):
    row = lax.broadcasted_iota(jnp.int32, (ATTN_BLOCK, 2 * ATTN_BLOCK), 0)
    lane = lax.broadcasted_iota(jnp.int32, (ATTN_BLOCK, 2 * ATTN_BLOCK), 1)
    prevm = row > lane % ATTN_BLOCK
    return prevm, jnp.logical_and(first, prevm)


def _pair_rows(ref, g):
    return jnp.concatenate([ref[:, 256 * g:256 * g + 128], ref[:, 256 * g + 128:256 * (g + 1)]], axis=0)


def _pair_sinks(sink_ref, g, hh):
    h0, h1 = 4 * g + hh, 4 * g + 2 + hh
    return jnp.concatenate([jnp.broadcast_to(sink_ref[:, h0:h0 + 1], (1, ATTN_BLOCK)),
                            jnp.broadcast_to(sink_ref[:, h1:h1 + 1], (1, ATTN_BLOCK))], axis=1)


def _attn_fwd(name, qkv, vt, sinks):
    T = qkv.shape[0]
    nbk = min(ATTN_TILE, T // ATTN_BLOCK)
    rows = nbk * ATTN_BLOCK
    scale = HEAD_DIM ** -0.5

    def block(q_ref, kp_ref, kc_ref, vtp_ref, vtc_ref, sink_ref, o_ref, first):
        prevm, kill = _attn_masks(first)
        kst = _lane_stack(kp_ref, kc_ref, 0) + _lane_stack(kp_ref, kc_ref, 1)
        vts = [_row_stack(vtp_ref, vtc_ref, g) for g in range(N_KV_HEADS)]
        s8 = [_dot_nt(kst[g], _pair_rows(q_ref, g)) * scale for g in range(N_KV_HEADS)]
        pu = [[_unfold(prevm, _attn_softmax(_fold(prevm, s8[g], hh), kill, _pair_sinks(sink_ref, g, hh))[0])
               for hh in range(2)] for g in range(N_KV_HEADS)]
        for g in range(N_KV_HEADS):
            ot = _dot(vts[g][0], pu[g][0]) + _dot(vts[g][1], pu[g][1])
            o_ref[:, 256 * g:256 * g + 128] = ot[:, :ATTN_BLOCK].T.astype(BF16)
            o_ref[:, 256 * g + 128:256 * (g + 1)] = ot[:, ATTN_BLOCK:].T.astype(BF16)

    def body(q_ref, kp_ref, kc_ref, vtp_ref, vtc_ref, sink_ref, o_ref):
        for b in range(nbk):
            rs = slice(ATTN_BLOCK * b, ATTN_BLOCK * (b + 1))
            ps = slice(ATTN_BLOCK * (b - 1), ATTN_BLOCK * b)
            block(q_ref.at[rs, :], kp_ref if b == 0 else kc_ref.at[ps, :], kc_ref.at[rs, :],
                  vtp_ref if b == 0 else vtc_ref.at[:, ps], vtc_ref.at[:, rs], sink_ref, o_ref.at[rs, :],
                  pl.program_id(0) == 0 if b == 0 else False)

    prev = lambda i: jnp.maximum(nbk * i - 1, 0)
    return pl.pallas_call(
        body, name=name, grid=(T // rows,),
        in_specs=[pl.BlockSpec((rows, D), lambda i: (i, 0)),
                  pl.BlockSpec((ATTN_BLOCK, 256), lambda i: (prev(i), 4)),
                  pl.BlockSpec((rows, 256), lambda i: (i, 4)),
                  pl.BlockSpec((256, ATTN_BLOCK), lambda i: (0, prev(i))),
                  pl.BlockSpec((256, rows), lambda i: (0, i)),
                  pl.BlockSpec((1, N_Q_HEADS), lambda i: (0, 0))],
        out_specs=pl.BlockSpec((rows, D), lambda i: (i, 0)),
        out_shape=jax.ShapeDtypeStruct((T, D), BF16),
        compiler_params=_params("arbitrary"),
    )(qkv, qkv, qkv, vt, vt, sinks)


ATTN_TILE = 4
ATTN_BWD_ORDER = ((0, 0), (0, 1), (1, 0), (0, 2), (1, 1), (2, 0), (0, 3), (1, 2), (2, 1), (1, 3), (2, 2), (2, 3))


def _attn_bwd(name, qkv, kt, sinks, dya, carry=None):
    T = qkv.shape[0]
    nbk = min(ATTN_TILE, T // ATTN_BLOCK)
    rows = nbk * ATTN_BLOCK
    scale = HEAD_DIM ** -0.5

    def block(q_ref, kp_ref, kc_ref, vp_ref, vc_ref, ktp_ref, ktc_ref, sink_ref, do_ref, dq_ref, first):
        prevm, kill = _attn_masks(first)
        kst = _lane_stack(kp_ref, kc_ref, 0) + _lane_stack(kp_ref, kc_ref, 1)
        vst = _lane_stack(vp_ref, vc_ref, 0) + _lane_stack(vp_ref, vc_ref, 1)
        kts = [_row_stack(ktp_ref, ktc_ref, g) for g in range(N_KV_HEADS)]
        lane = lax.broadcasted_iota(jnp.int32, (2 * ATTN_BLOCK, 128), 1)
        low = lane < HEAD_DIM
        slane = lax.broadcasted_iota(jnp.int32, (1, 128), 1)
        dkz = [jnp.zeros((2 * ATTN_BLOCK, 128), F32) for _ in range(N_KV_HEADS)]
        dvz = [jnp.zeros((2 * ATTN_BLOCK, 128), F32) for _ in range(N_KV_HEADS)]
        dsink = jnp.zeros((1, 128), F32)
        groups = range(N_KV_HEADS)
        qcat, docat, s8, dp8 = {}, {}, {}, {}
        ds_u, p_u, dsinks = {}, {}, []

        def scores(g):
            qcat[g], docat[g] = _pair_rows(q_ref, g), _pair_rows(do_ref, g)
            s8[g] = _dot_nt(kst[g], qcat[g]) * scale
            dp8[g] = _dot_nt(vst[g], docat[g])

        def algebra(g):
            for hh in range(2):
                pn, ps = _attn_softmax(_fold(prevm, s8[g], hh), kill, _pair_sinks(sink_ref, g, hh))
                dp = _fold(prevm, dp8[g], hh)
                delta = jnp.sum(pn * dp, axis=0, keepdims=True)
                sd = ps * delta
                dsinks.append(jnp.where(slane == 4 * g + hh, -jnp.sum(sd[:, :ATTN_BLOCK]), 0.0)
                              + jnp.where(slane == 4 * g + 2 + hh, -jnp.sum(sd[:, ATTN_BLOCK:]), 0.0))
                ds_u[g, hh] = _unfold(prevm, pn * (dp - delta) * scale)
                p_u[g, hh] = _unfold(prevm, pn)

        def grads(g):
            dqt = _dot(kts[g][0], ds_u[g, 0]) + _dot(kts[g][1], ds_u[g, 1])
            dq_ref[:, 256 * g:256 * g + 128] = dqt[:, :ATTN_BLOCK].T
            dq_ref[:, 256 * g + 128:256 * (g + 1)] = dqt[:, ATTN_BLOCK:].T
            for hh in range(2):
                own = low if hh == 0 else jnp.logical_not(low)
                dk_h = jnp.where(own, _dot(ds_u[g, hh], qcat[g]), 0.0)
                dv_h = jnp.where(own, _dot(p_u[g, hh], docat[g]), 0.0)
                if hh != g % 2:
                    dk_h = pltpu.roll(dk_h, HEAD_DIM, 1)
                    dv_h = pltpu.roll(dv_h, HEAD_DIM, 1)
                dkz[g] = dkz[g] + dk_h
                dvz[g] = dvz[g] + dv_h

        for stage, g in ATTN_BWD_ORDER:
            (scores, algebra, grads)[stage](g)
        for t in dsinks:
            dsink = dsink + t
        pairs = range(N_KV_HEADS // 2)
        return dsink, [dkz[2 * gp] + dkz[2 * gp + 1] for gp in pairs], [dvz[2 * gp] + dvz[2 * gp + 1] for gp in pairs]

    def body(q_ref, kp_ref, kc_ref, vp_ref, vc_ref, ktp_ref, ktc_ref, sink_ref, do_ref,
             dq_ref, dk_ref, dv_ref, ds_ref):
        i = pl.program_id(0)

        @pl.when(i == 0)
        def _():
            ds_ref[...] = jnp.zeros_like(ds_ref)

        res = []
        for b in range(nbk):
            rs = slice(ATTN_BLOCK * b, ATTN_BLOCK * (b + 1))
            ps = slice(ATTN_BLOCK * (b - 1), ATTN_BLOCK * b)
            res.append(block(
                q_ref.at[rs, :], kp_ref if b == 0 else kc_ref.at[ps, :], kc_ref.at[rs, :],
                vp_ref if b == 0 else vc_ref.at[ps, :], vc_ref.at[rs, :],
                ktp_ref if b == 0 else ktc_ref.at[:, ps], ktc_ref.at[:, rs], sink_ref, do_ref.at[rs, :],
                dq_ref.at[rs, :], i == 0 if b == 0 else False))
        dsink = res[0][0]
        for r in res[1:]:
            dsink = dsink + r[0]
        ds_ref[...] += dsink
        for gp in range(N_KV_HEADS // 2):
            cols = slice(128 * gp, 128 * (gp + 1))
            for b in range(nbk):
                cur = pl.ds(pl.multiple_of(i * rows + ATTN_BLOCK * b, ATTN_BLOCK), ATTN_BLOCK)
                dkb, dvb = res[b][1][gp][ATTN_BLOCK:, :], res[b][2][gp][ATTN_BLOCK:, :]
                if b + 1 < nbk:
                    dkb, dvb = dkb + res[b + 1][1][gp][:ATTN_BLOCK, :], dvb + res[b + 1][2][gp][:ATTN_BLOCK, :]
                dk_ref[cur, cols] = dkb
                dv_ref[cur, cols] = dvb

            @pl.when(i > 0)
            def _():
                prv = pl.ds(pl.multiple_of(jnp.maximum(i * rows - ATTN_BLOCK, 0), ATTN_BLOCK), ATTN_BLOCK)
                dk_ref[prv, cols] += res[0][1][gp][:ATTN_BLOCK, :]
                dv_ref[prv, cols] += res[0][2][gp][:ATTN_BLOCK, :]

    prev = lambda i: jnp.maximum(nbk * i - 1, 0)
    whole = lambda w: pl.BlockSpec((T, w), lambda i: (0, 0))
    return _call(
        body, name=name, grid=(T // rows,), scratch_shapes=[], ins=[qkv, qkv, qkv, qkv, qkv, kt, kt, sinks, dya],
        carry=carry,
        in_specs=[pl.BlockSpec((rows, D), lambda i: (i, 0)),
                  pl.BlockSpec((ATTN_BLOCK, 256), lambda i: (prev(i), 4)),
                  pl.BlockSpec((rows, 256), lambda i: (i, 4)),
                  pl.BlockSpec((ATTN_BLOCK, 256), lambda i: (prev(i), 5)),
                  pl.BlockSpec((rows, 256), lambda i: (i, 5)),
                  pl.BlockSpec((256, ATTN_BLOCK), lambda i: (0, prev(i))),
                  pl.BlockSpec((256, rows), lambda i: (0, i)),
                  pl.BlockSpec((1, N_Q_HEADS), lambda i: (0, 0)),
                  pl.BlockSpec((rows, D), lambda i: (i, 0))],
        out_specs=[pl.BlockSpec((rows, D), lambda i: (i, 0)), whole(256), whole(256),
                   pl.BlockSpec((1, 128), lambda i: (0, 0))],
        out_shape=[jax.ShapeDtypeStruct((T, D), F32), jax.ShapeDtypeStruct((T, 256), F32),
                   jax.ShapeDtypeStruct((T, 256), F32), jax.ShapeDtypeStruct((1, 128), F32)])


HG_FWD = (256, 8, 512)
HG_BWD = (128, 8, 512)


def _chunk_sum(tri, x):
    w = x.shape[1]
    h1 = x.astype(BF16)
    h2 = (x - h1.astype(F32)).astype(BF16)
    r = _dot(tri, jnp.concatenate([h1, h2], axis=1))
    return r[:, :w] + r[:, w:]


def _chunk_tri(n, upper=False):
    ri = lax.broadcasted_iota(jnp.int32, (n, n), 0)
    ci = lax.broadcasted_iota(jnp.int32, (n, n), 1)
    same_chunk = jnp.bitwise_xor(ri, ci) < HGRN_CHUNK
    return jnp.where(((ri <= ci) if upper else (ri >= ci)) & same_chunk, 1.0, 0.0).astype(BF16)


def _chunks(t):
    return [t[HGRN_CHUNK * c:HGRN_CHUNK * (c + 1), :] for c in range(t.shape[0] // HGRN_CHUNK)]


def _lower_bound(lbl_ref):
    l0, l1 = lbl_ref[0:1, :], lbl_ref[1:2, :]
    m = jnp.maximum(l0, l1)
    e0, e1 = jnp.exp(l0 - m), jnp.exp(l1 - m)
    return e0 / (e0 + e1)


def _heads(t):
    return [t[:, 128 * h:128 * (h + 1)] for h in range(t.shape[1] // 128)]


def _per_head(fn, *wide):
    return jnp.concatenate([fn(*parts) for parts in zip(*[_heads(t) for t in wide])], axis=1)


def _hgrn_sub(fl, qh, vv, lb, tril_b):
    w = fl.shape[1]
    sg = _sigmoid(fl)
    f = lb + (1.0 - lb) * sg
    k = 1.0 - f
    gc = _chunk_sum(tril_b, jnp.log(f))
    gl_c = [t[HGRN_CHUNK - 1:HGRN_CHUNK, :] for t in _chunks(gc)]
    gl = jnp.concatenate([jnp.broadcast_to(g, (HGRN_CHUNK, w)) for g in gl_c], axis=0)
    sq = _sigmoid(qh)
    eg = jnp.exp(gc)
    eng = jnp.exp(-gc)
    elg = jnp.exp(gl - gc)
    qd = qh * sq * eg
    ki = k * eng
    ke = k * elg
    qd_b, ki_b, ke_b, v_b = (t.astype(BF16) for t in (qd, ki, ke, vv))
    am_b = [jnp.where(tril_b > 0, _dot_nt(qq, kk), 0.0).astype(BF16) for qq, kk in zip(_heads(qd_b), _heads(ki_b))]
    return dict(sg=sg, f=f, sq=sq, eg=eg, eng=eng, elg=elg, qd=qd, ki=ki, ke=ke, egl=[jnp.exp(g) for g in gl_c],
                am_b=am_b, qd_b=qd_b, ki_b=ki_b, ke_b=ke_b, v_b=v_b)


def _hgrn_out(q, st_b):
    outs = []
    for h, (qd_h, v_h) in enumerate(zip(_heads(q["qd_b"]), _heads(q["v_b"]))):
        inter = [_dot_nt(qc, s) for qc, s in zip(_chunks(qd_h), st_b[h])]
        outs.append(_dot(q["am_b"][h], v_h) + jnp.concatenate(inter, axis=0))
    return jnp.concatenate(outs, axis=1)


def _head_rms(o):
    return _per_head(lambda t: jnp.broadcast_to(
        lax.rsqrt(jnp.mean(t * t, axis=-1, keepdims=True) + RMS_EPS), t.shape), o)


def _hgrn_fwd(name, hg, lb_logits, norm_g, cfg):
    T = hg.shape[0]
    sub_rows, hps, th = min(cfg[0], T), cfg[1], min(cfg[2], T)
    cps = sub_rows // HGRN_CHUNK
    nc = th // HGRN_CHUNK

    def body(fl_ref, qh_ref, v_ref, og_ref, lbl_ref, ng_ref, y_ref, ss_ref, st):
        @pl.when(pl.program_id(1) == 0)
        def _():
            st[...] = jnp.zeros_like(st)

        lbs = _lower_bound(lbl_ref)
        ng = jnp.tile(ng_ref[...], (1, hps))
        tril_b = _chunk_tri(sub_rows)

        def sub(si, carry):
            rows = pl.ds(pl.multiple_of(si * sub_rows, sub_rows), sub_rows)
            q = _hgrn_sub(fl_ref[rows, :], qh_ref[rows, :], v_ref[rows, :], lbs, tril_b)
            v_hc = [_chunks(t) for t in _heads(q["v_b"])]
            k_hc = [_chunks(t) for t in _heads(q["ke_b"])]
            s = [st[h] for h in range(hps)]
            st_b = [[] for _ in range(hps)]
            for c in range(cps):
                egl = _heads(q["egl"][c])
                for h in range(hps):
                    ss_ref[h, si * cps + c] = s[h]
                    st_b[h].append(s[h].astype(BF16))
                    s[h] = s[h] * egl[h] + _dot_tn(v_hc[h][c], k_hc[h][c])
            for h in range(hps):
                st[h] = s[h]
            o = _hgrn_out(q, st_b)
            og = og_ref[rows, :]
            y_ref[rows, :] = (o * _head_rms(o) * ng * (og * _sigmoid(og))).astype(BF16)
            return carry

        lax.fori_loop(0, th // sub_rows, sub, 0)

    wd = 128 * hps
    col = lambda j: pl.BlockSpec((th, wd), lambda h, t: (t, (8 // hps) * j + h))
    return pl.pallas_call(
        body, name=name, grid=(HGRN_HEADS // hps, T // th),
        in_specs=[col(0), col(1), col(2), col(3), pl.BlockSpec((2, wd), lambda h, t: (0, h)),
                  pl.BlockSpec((1, 128), lambda h, t: (0, 0))],
        out_specs=[pl.BlockSpec((th, wd), lambda h, t: (t, h)),
                   pl.BlockSpec((hps, nc, 128, 128), lambda h, t: (h, t, 0, 0))],
        out_shape=[jax.ShapeDtypeStruct((T, D), BF16),
                   jax.ShapeDtypeStruct((HGRN_HEADS, T // HGRN_CHUNK, 128, 128), F32)],
        scratch_shapes=[pltpu.VMEM((hps, 128, 128), F32)],
        compiler_params=_params("parallel", "arbitrary"),
    )(hg, hg, hg, hg, lb_logits, norm_g)


def _hgrn_bwd(name, hg, lb_logits, norm_g, sstart, dyh, cfg, carry=None):
    T = hg.shape[0]
    sub_rows, hps, th = min(cfg[0], T), cfg[1], min(cfg[2], T)
    cps = sub_rows // HGRN_CHUNK
    nc = th // HGRN_CHUNK
    nt = T // th
    wd = 128 * hps

    def body(fl_ref, qh_ref, v_ref, og_ref, lbl_ref, ng_ref, ss_ref, dy_ref,
             dfl_ref, dqh_ref, dv_ref, dog_ref, dlb_ref, dng_ref, dst):
        @pl.when(pl.program_id(1) == 0)
        def _():
            dst[...] = jnp.zeros_like(dst)
            dlb_ref[...] = jnp.zeros_like(dlb_ref)
            dng_ref[...] = jnp.zeros_like(dng_ref)

        lb = _lower_bound(lbl_ref)
        ng = jnp.tile(ng_ref[...], (1, hps))
        last = lax.broadcasted_iota(jnp.int32, (HGRN_CHUNK, wd), 0) == HGRN_CHUNK - 1
        nsub = th // sub_rows
        cat0 = lambda parts: jnp.concatenate(parts, axis=0)
        tril_b, triu_b = _chunk_tri(sub_rows), _chunk_tri(sub_rows, upper=True)

        def sub(step, carry):
            si = nsub - 1 - step
            rows = pl.ds(pl.multiple_of(si * sub_rows, sub_rows), sub_rows)
            qh, og, dy = qh_ref[rows, :], og_ref[rows, :], dy_ref[rows, :]
            q = _hgrn_sub(fl_ref[rows, :], qh, v_ref[rows, :], lb, tril_b)
            s_in = [[ss_ref[h, si * cps + c] for c in range(cps)] for h in range(hps)]
            st_b = [[s.astype(BF16) for s in row] for row in s_in]
            o = _hgrn_out(q, st_b)
            rr = _head_rms(o)
            oh = o * rr
            sog = _sigmoid(og)
            dog_ref[rows, :] = (dy * oh * ng * (sog * (1.0 + og * (1.0 - sog)))).astype(BF16)
            don = dy * (og * sog)
            dng_w = jnp.sum(don * oh, axis=0, keepdims=True)
            dd = don * ng
            mean_h = _per_head(lambda t: jnp.broadcast_to(jnp.mean(t, axis=-1, keepdims=True), t.shape), dd * oh)
            do_b = (rr * (dd - oh * mean_h)).astype(BF16)
            do_h, qd_h, ki_h, ke_h, v_h = (_heads(t) for t in (do_b, q["qd_b"], q["ki_b"], q["ke_b"], q["v_b"]))
            da_b = [jnp.where(tril_b > 0, _dot_nt(do_h[h], v_h[h]), 0.0).astype(BF16) for h in range(hps)]
            do_c, qd_c, ke_c, v_c = ([_chunks(t) for t in hs] for hs in (do_h, qd_h, ke_h, v_h))
            dsp = [[None] * cps for _ in range(hps)]
            dgl_dec = [[None] * cps for _ in range(hps)]
            d = [dst[h] for h in range(hps)]
            for c in reversed(range(cps)):
                egl = _heads(q["egl"][c])
                for h in range(hps):
                    dsp[h][c] = d[h]
                    dgl_dec[h][c] = jnp.sum(d[h] * s_in[h][c], axis=0, keepdims=True) * egl[h]
                    d[h] = d[h] * egl[h] + _dot_tn(do_c[h][c], qd_c[h][c])
            for h in range(hps):
                dst[h] = d[h]
                dng_ref[h] += dng_w[:, 128 * h:128 * (h + 1)]
            dsp_b = [[t.astype(BF16) for t in row] for row in dsp]
            dv_ref[rows, :] = jnp.concatenate(
                [_dot_tn(q["am_b"][h], do_h[h]) + cat0([_dot_nt(ke_c[h][c], dsp_b[h][c]) for c in range(cps)])
                 for h in range(hps)], axis=1).astype(BF16)
            dqd = jnp.concatenate(
                [_dot(da_b[h], ki_h[h]) + cat0([_dot(do_c[h][c], st_b[h][c]) for c in range(cps)])
                 for h in range(hps)], axis=1)
            dki = jnp.concatenate([_dot_tn(da_b[h], qd_h[h]) for h in range(hps)], axis=1)
            dke = jnp.concatenate([cat0([_dot(v_c[h][c], dsp_b[h][c]) for c in range(cps)]) for h in range(hps)], axis=1)
            dkk = dke * q["ke"]
            dgl = [jnp.sum(t, axis=0, keepdims=True) + jnp.concatenate([dgl_dec[h][c] for h in range(hps)], axis=1)
                   for c, t in enumerate(_chunks(dkk))]
            dgc = dqd * q["qd"] - dki * q["ki"] - dkk + cat0([jnp.where(last, g, 0.0) for g in dgl])
            dlf = _chunk_sum(triu_b, dgc)
            df = dlf / q["f"] - (dki * q["eng"] + dke * q["elg"])
            sg = q["sg"]
            dlb_ref[...] += jnp.sum(df * (1.0 - sg), axis=0, keepdims=True)
            dfl_ref[rows, :] = (df * (1.0 - lb) * sg * (1.0 - sg)).astype(BF16)
            sq = q["sq"]
            dqh_ref[rows, :] = (dqd * q["eg"] * (sq * (1.0 + qh * (1.0 - sq)))).astype(BF16)
            return carry

        lax.fori_loop(0, nsub, sub, 0)
    col = lambda j: pl.BlockSpec((th, wd), lambda h, t: (nt - 1 - t, (8 // hps) * j + h))
    out = pl.BlockSpec((th, wd), lambda h, t: (nt - 1 - t, h))
    return _call(
        body, name=name, grid=(HGRN_HEADS // hps, nt),
        in_specs=[col(0), col(1), col(2), col(3), pl.BlockSpec((2, wd), lambda h, t: (0, h)),
                  pl.BlockSpec((1, 128), lambda h, t: (0, 0)),
                  pl.BlockSpec((hps, nc, 128, 128), lambda h, t: (h, nt - 1 - t, 0, 0)), out],
        out_specs=[out, out, out, out, pl.BlockSpec((1, wd), lambda h, t: (0, h)),
                   pl.BlockSpec((hps, 1, 128), lambda h, t: (h, 0, 0))],
        out_shape=[jax.ShapeDtypeStruct((T, D), BF16)] * 4 + [jax.ShapeDtypeStruct((1, D), F32),
                                                              jax.ShapeDtypeStruct((HGRN_HEADS, 1, 128), F32)],
        scratch_shapes=[pltpu.VMEM((hps, 128, 128), F32)],
        ins=[hg, hg, hg, hg, lb_logits, norm_g, sstart, dyh], carry=carry)


def _mix_fwd(name, ya, yh, gates, xh1, g1, b1, wall, tm):
    T = ya.shape[0]

    def body(ya_ref, yh_ref, ga_ref, gh_ref, xh_ref, g_ref, b_ref, wall_ref,
             xo_ref, r_ref, pa_ref, ph_ref, mg_ref, wpa, wph, wo, sem):
        @pl.when(pl.program_id(0) == 0)
        def _():
            _load_rows(wall_ref, [(wpa, RIN, RP), (wph, RIN + RP, RP), (wo, RIN + 2 * RP, RP)], sem)

        parts = [slice(p, min(p + PART_ROWS, tm)) for p in range(0, tm, PART_ROWS)]
        pas = [_dot(ya_ref[rs, :], wpa[...]).astype(BF16) for rs in parts]
        phs = [_dot(yh_ref[rs, :], wph[...]).astype(BF16) for rs in parts]
        mgs = []
        for rs, pa, ph in zip(parts, pas, phs):
            pa_ref[rs, :] = pa
            ph_ref[rs, :] = ph
            mgs.append((_sigmoid(ga_ref[rs, :]) * pa.astype(F32) + _sigmoid(gh_ref[rs, :]) * ph.astype(F32)).astype(BF16))
            mg_ref[rs, :] = mgs[-1]
        mixes = [_dot(mg, wo[...]) for mg in mgs]
        for rs, mix in zip(parts, mixes):
            y1 = xh_ref[rs, :] * g_ref[...] + b_ref[...]
            xh2, r = _ln_fwd(ALPHA * y1 + mix)
            xo_ref[rs, :] = xh2
            r_ref[rs, :] = jnp.broadcast_to(r, (xh2.shape[0], 128))

    row = lambda w: pl.BlockSpec((tm, w), lambda i: (i, 0))
    vec = pl.BlockSpec((1, D), lambda i: (0, 0))
    return pl.pallas_call(
        body, name=name, grid=(T // tm,),
        in_specs=[row(D), row(D), pl.BlockSpec((tm, D), lambda i: (i, 0)), pl.BlockSpec((tm, D), lambda i: (i, 1)),
                  row(D), vec, vec, HBM_SPEC],
        out_specs=[row(D), row(128), row(D), row(D), row(D)],
        out_shape=[jax.ShapeDtypeStruct((T, D), F32), jax.ShapeDtypeStruct((T, 128), F32)]
        + [jax.ShapeDtypeStruct((T, D), BF16)] * 3,
        scratch_shapes=[pltpu.VMEM((D, D), BF16)] * 3 + [pltpu.SemaphoreType.DMA((3 * NDEV,))],
        compiler_params=_params("arbitrary"),
    )(ya, yh, gates, gates, xh1, g1, b1, wall)


def _mix_bwd(name, dy2, xh2, r2, g2, gates, pa, ph, wall, tm, carry=None):
    T = dy2.shape[0]

    def body(dy_ref, xh_ref, r_ref, g_ref, ga_ref, gh_ref, pa_ref, ph_ref, wall_ref,
             dz_ref, dmix_ref, dpa_ref, dph_ref, dga_ref, dgh_ref, dya_ref, dyh_ref, dg_ref, db_ref,
             wpa, wph, wo, sem):
        @pl.when(pl.program_id(0) == 0)
        def _():
            _load_rows(wall_ref, [(wpa, RIN, RP), (wph, RIN + RP, RP), (wo, RIN + 2 * RP, RP)], sem)
            dg_ref[...] = jnp.zeros_like(dg_ref)
            db_ref[...] = jnp.zeros_like(db_ref)

        parts = [slice(p, min(p + PART_ROWS, tm)) for p in range(0, tm, PART_ROWS)]
        dmix = []
        for rs in parts:
            dz, dgp, dbp = _ln_bwd(dy_ref[rs, :], xh_ref[rs, :], r_ref[rs, :1], g_ref[...])
            dg_ref[...] += dgp
            db_ref[...] += dbp
            dz_ref[rs, :] = dz
            dmix.append(dz.astype(BF16))
            dmix_ref[rs, :] = dmix[-1]
        dmgs = [_dot_nt(d, wo[...]) for d in dmix]
        dpas, dphs = [], []
        for rs, dmg in zip(parts, dmgs):
            sa, sh = _sigmoid(ga_ref[rs, :]), _sigmoid(gh_ref[rs, :])
            dga_ref[rs, :] = (dmg * pa_ref[rs, :].astype(F32) * sa * (1.0 - sa)).astype(BF16)
            dgh_ref[rs, :] = (dmg * ph_ref[rs, :].astype(F32) * sh * (1.0 - sh)).astype(BF16)
            dpas.append((dmg * sa).astype(BF16))
            dphs.append((dmg * sh).astype(BF16))
            dpa_ref[rs, :] = dpas[-1]
            dph_ref[rs, :] = dphs[-1]
        for rs, dpa, dph in zip(parts, dpas, dphs):
            dya_ref[rs, :] = _dot_nt(dpa, wpa[...]).astype(BF16)
            dyh_ref[rs, :] = _dot_nt(dph, wph[...])

    row = lambda w: pl.BlockSpec((tm, w), lambda i: (i, 0))
    vec = pl.BlockSpec((1, D), lambda i: (0, 0))
    return _call(
        body, name=name, grid=(T // tm,),
        in_specs=[row(D), row(D), row(128), vec, pl.BlockSpec((tm, D), lambda i: (i, 0)),
                  pl.BlockSpec((tm, D), lambda i: (i, 1)), row(D), row(D), HBM_SPEC],
        out_specs=[row(D)] * 8 + [vec, vec],
        out_shape=[jax.ShapeDtypeStruct((T, D), F32)] + [jax.ShapeDtypeStruct((T, D), BF16)] * 6
        + [jax.ShapeDtypeStruct((T, D), F32)] + [jax.ShapeDtypeStruct((1, D), F32)] * 2,
        scratch_shapes=[pltpu.VMEM((D, D), BF16)] * 3 + [pltpu.SemaphoreType.DMA((3 * NDEV,))],
        ins=[dy2, xh2, r2, g2, gates, gates, pa, ph, wall], carry=carry)


def _adam(w, g, m, v):
    m = ADAM_B1 * m + (1.0 - ADAM_B1) * g
    v = ADAM_B2 * v + (1.0 - ADAM_B2) * (g * g)
    m_hat = m / (1.0 - ADAM_B1 ** ADAM_STEP)
    v_hat = v / (1.0 - ADAM_B2 ** ADAM_STEP)
    delta = -ADAM_LR * (m_hat / (jnp.sqrt(v_hat) + ADAM_EPS) + ADAM_WD * w)
    return delta, m, v


def _grad_steps(name, items, cb, carry=None):
    n = len(items)
    pieces = [len(it[0]) for it in items]

    def body(*refs):
        it = iter(refs[:sum(pieces) + 3 * n])
        outs = refs[sum(pieces) + 3 * n:]
        for i in range(n):
            accs = []
            for _ in range(pieces[i]):
                r_ref = next(it)
                acc = r_ref[0].astype(F32)
                for k in range(1, NDEV):
                    acc = acc + r_ref[k].astype(F32)
                accs.append(acc)
            acc = accs[0] if len(accs) == 1 else jnp.concatenate(accs, axis=0)
            w_ref, m_ref, v_ref = next(it), next(it), next(it)
            g_ref, d_ref, mo_ref, vo_ref = outs[4 * i:4 * i + 4]
            g_ref[...] = acc
            d, mm, vv = _adam(w_ref[...], acc, m_ref[...], v_ref[...])
            d_ref[...] = d
            mo_ref[...] = mm
            vo_ref[...] = vv

    in_specs, out_specs, out_shape, ins = [], [], [], []
    for recv, w, m, v in items:
        blk = pl.BlockSpec((w.shape[0], cb), lambda j: (0, j))
        in_specs += [pl.BlockSpec((NDEV, rp.shape[1], cb), lambda j: (0, 0, j)) for rp in recv] + [blk, blk, blk]
        out_specs += [blk] * 4
        out_shape += [jax.ShapeDtypeStruct(w.shape, F32)] * 4
        ins += list(recv) + [w, m, v]
    outs, ex = _call(body, name=name, grid=(D // cb,), in_specs=in_specs, out_specs=out_specs, out_shape=out_shape,
                     scratch_shapes=[], ins=ins, carry=carry)
    return [tuple(outs[4 * i:4 * i + 4]) for i in range(n)], ex


_SMALL = [("ln1_g", D), ("ln1_b", D), ("ln2_g", D), ("ln2_b", D), ("ln3_g", D), ("ln3_b", D),
          ("b_in", DIN), ("lb", D), ("attn_sinks", 128), ("hgrn_norm_g", 128), ("loss", 128)]
_SMALL_OFF = {}
_o = 0
for _n, _w in _SMALL:
    _SMALL_OFF[_n] = (_o, _w)
    _o += _w
PACK = _o


def _small_step(name, slots, small_w, small_m, small_v):
    names = ["ln1_g", "ln1_b", "ln2_g", "ln2_b", "ln3_g", "ln3_b", "b_in", "attn_sinks", "hgrn_lb_logits", "hgrn_norm_g"]
    np_ = len(names)

    def body(*refs):
        s_ref = refs[0]
        w_refs = refs[1:1 + np_]
        m_refs = refs[1 + np_:1 + 2 * np_]
        v_refs = refs[1 + 2 * np_:1 + 3 * np_]
        loss_ref, outs = refs[1 + 3 * np_], refs[2 + 3 * np_:]
        tot = s_ref[0]
        for k in range(1, NDEV):
            tot = tot + s_ref[k]

        def part(n):
            o, w = _SMALL_OFF[n]
            return tot[:, o:o + w]

        loss_ref[...] = part("loss")
        for i, n in enumerate(names):
            w = w_refs[i][...]
            if n == "hgrn_lb_logits":
                m_ = jnp.maximum(w[0:1, :], w[1:2, :])
                e0, e1 = jnp.exp(w[0:1, :] - m_), jnp.exp(w[1:2, :] - m_)
                p0 = e0 / (e0 + e1)
                t = p0 * (1.0 - p0) * part("lb")
                g = jnp.concatenate([t, -t], axis=0)
            elif n == "attn_sinks":
                g = part(n)[:, :N_Q_HEADS]
            else:
                g = part(n)
            d, mm, vv = _adam(w, g, m_refs[i][...], v_refs[i][...])
            outs[4 * i][...] = g
            outs[4 * i + 1][...] = d
            outs[4 * i + 2][...] = mm
            outs[4 * i + 3][...] = vv

    out_shape = [jax.ShapeDtypeStruct((1, 128), F32)]
    for n in names:
        out_shape += [jax.ShapeDtypeStruct(small_w[n].shape, F32)] * 4
    return pl.pallas_call(
        body, name=name, out_shape=out_shape,
        compiler_params=pltpu.CompilerParams(vmem_limit_bytes=VMEM_LIMIT),
    )(slots, *[small_w[n] for n in names], *[small_m[n] for n in names], *[small_v[n] for n in names]), names


def _tile(T, pref):
    return min(T, pref)


def kernel(x, ln1_g, ln1_b, ffn1_w1, ffn1_w3, ffn1_w2, ln2_g, ln2_b, w_in, b_in, attn_sinks, hgrn_lb_logits, hgrn_norm_g, w_proj_attn, w_proj_hgrn, w_out, ln3_g, ln3_b, ffn2_w1, ffn2_w3, ffn2_w2, loss_target, m_ln1_g, m_ln1_b, m_ffn1_w1, m_ffn1_w3, m_ffn1_w2, m_ln2_g, m_ln2_b, m_w_in, m_b_in, m_attn_sinks, m_hgrn_lb_logits, m_hgrn_norm_g, m_w_proj_attn, m_w_proj_hgrn, m_w_out, m_ln3_g, m_ln3_b, m_ffn2_w1, m_ffn2_w3, m_ffn2_w2, v_ln1_g, v_ln1_b, v_ffn1_w1, v_ffn1_w3, v_ffn1_w2, v_ln2_g, v_ln2_b, v_w_in, v_b_in, v_attn_sinks, v_hgrn_lb_logits, v_hgrn_norm_g, v_w_proj_attn, v_w_proj_hgrn, v_w_out, v_ln3_g, v_ln3_b, v_ffn2_w1, v_ffn2_w3, v_ffn2_w2):
    T = x.shape[1]
    xs = x[0]
    tgt = loss_target[0]
    tm = _tile(T, 256)
    tm2 = _tile(T, 512)
    tk = _tile(T, 2048)

    t_bf = lambda w: w[0].T.astype(BF16)
    n_bf = lambda w: w[0].astype(BF16)
    ffn_shard = lambda w1, w3, w2: jnp.concatenate([t_bf(w1), t_bf(w3), n_bf(w2)], axis=0)
    mix_shard = jnp.concatenate([t_bf(w_in), n_bf(w_proj_attn), n_bf(w_proj_hgrn), n_bf(w_out)], axis=0)
    (ffn1_all,) = _exchange_call("gather_ffn1", _gather_exchange(ffn_shard(ffn1_w1, ffn1_w3, ffn1_w2)))
    ffn_offs = (0, RF, 2 * RF)
    rope = _rope_table(T)

    (xh1, r1, a1, b1, xb0), (mix_all,) = _ffn_fwd("ffn1_fwd", xs, None, ffn1_all, ffn_offs, tm2,
                                                  carry=_gather_exchange(mix_shard))
    (qkv, kt, vt, hg, gates, y1b), (ffn2_all,) = _inproj_fwd(
        "inproj_fwd", xh1, ln1_g, ln1_b, mix_all, b_in, rope, tm2,
        carry=_gather_exchange(ffn_shard(ffn2_w1, ffn2_w3, ffn2_w2)))
    ya = _attn_fwd("attn_fwd", qkv, vt, attn_sinks)
    yh, sstart = _hgrn_fwd("hgrn_fwd", hg, hgrn_lb_logits, hgrn_norm_g, HG_FWD)
    xh2, r2, pa, ph, merged = _mix_fwd("mix_fwd", ya, yh, gates, xh1, ln1_g, ln1_b, mix_all, tm2)
    (xh3, r3, a2, b2, y2b, loss_part), _ = _ffn_fwd("ffn2_fwd", xh2, (ln2_g, ln2_b), ffn2_all, ffn_offs, tm2,
                                                    loss=(ln3_g, ln3_b, tgt))

    (dy2, dab2, u2, df2, dg3, db3), _ = _ffn_bwd("ffn2_bwd", ("loss", ln3_b, tgt), xh3, r3, ln3_g, a2, b2, ffn2_all,
                                                 ffn_offs, tm)
    recv = {}
    g_ffn2_13, _ = _wgrad("wgrad_ffn2_w13", dab2, y2b, DFF // 2, tk)
    g_ffn2_2, _ = _wgrad("wgrad_ffn2_w2", u2, df2, DFF // 2, tk)
    (dz2, dmix, dpa, dph, dga, dgh, dya, dyh, dg2, db2), (recv["ffn2_w1"],) = _mix_bwd(
        "mix_bwd", dy2, xh2, r2, ln2_g, gates, pa, ph, mix_all, tm2, carry=_grad_exchange([g_ffn2_13], [0], [RF]))
    g_wo, _ = _wgrad("wgrad_w_out", merged, dmix, D, tk)
    g_pa, _ = _wgrad("wgrad_w_pa", ya, dpa, D, tk)
    g_ph, _ = _wgrad("wgrad_w_ph", yh, dph, D, tk)
    (dq, dk, dv, dsink), (recv["ffn2_w3"],) = _attn_bwd(
        "attn_bwd", qkv, kt, attn_sinks, dya, carry=_grad_exchange([g_ffn2_13], [DFF], [RF]))
    (dfl, dqh, dih, dog, dlb, dng), (recv["ffn2_w2"], recv["w_proj_attn"], recv["w_proj_hgrn"], recv["w_out"]) = \
        _hgrn_bwd("hgrn_bwd", hg, hgrn_lb_logits, hgrn_norm_g, sstart, dyh, HG_BWD,
                  carry=_grad_exchange([g_ffn2_2, g_pa, g_ph, g_wo], [0, 0, 0, 0], [RF, RP, RP, RP]))
    dy1, dproj, dbin = _inproj_bwd("inproj_bwd", dq, dk, dv, (dfl, dqh, dih, dog), (dga, dgh), dz2, mix_all, rope, tm)
    g_win, _ = _wgrad("wgrad_w_in", dproj, y1b, DIN // 4, tk)
    win_rows = (400, 288, 272)
    win_base = (0, 400, 688)
    (dz1, dab1, u1, df1, dg1, db1), (rw0,) = _ffn_bwd(
        "ffn1_bwd", ("dy", dy1), xh1, r1, ln1_g, a1, b1, ffn1_all, ffn_offs, tm, with_dx=False,
        carry=_grad_exchange([g_win], win_base[:1], win_rows[:1], [RIN]))
    g_ffn1_2, (rw1,) = _wgrad("wgrad_ffn1_w2", u1, df1, DFF // 2, tk,
                              carry=_grad_exchange([g_win], win_base[1:2], win_rows[1:2], [RIN]))
    g_ffn1_1, (recv["ffn1_w2"],) = _wgrad("wgrad_ffn1_w1", dab1, xb0, DFF // 2, tk, 0, DFF,
                                          carry=_grad_exchange([g_ffn1_2], [0], [RF]))
    g_ffn1_3, (recv["ffn1_w1"],) = _wgrad("wgrad_ffn1_w3", dab1, xb0, DFF // 2, tk, DFF, DFF,
                                          carry=_grad_exchange([g_ffn1_1], [0], [RF]))
    parts = {"ln1_g": dg1, "ln1_b": db1, "ln2_g": dg2, "ln2_b": db2, "ln3_g": dg3, "ln3_b": db3, "b_in": dbin,
             "lb": dlb, "attn_sinks": dsink, "hgrn_norm_g": jnp.sum(dng, axis=0), "loss": loss_part[0:1, :]}
    packed = jnp.concatenate([parts[n] for n, _ in _SMALL], axis=1)
    gx, (recv["ffn1_w3"], rw2, small_slots) = _ffn_dx(
        "ffn1_dx", dab1, dz1, ffn1_all, ffn_offs, tm2,
        carry=_join(_grad_exchange([g_ffn1_3, g_win], [0, win_base[2]], [RF, win_rows[2]], [RF, RIN]),
                    _row_gather_exchange(packed)))
    recv["w_in"] = [rw0, rw1, rw2]

    big = [("ffn1_w1", ffn1_w1, m_ffn1_w1, v_ffn1_w1, True), ("ffn1_w3", ffn1_w3, m_ffn1_w3, v_ffn1_w3, True),
           ("ffn1_w2", ffn1_w2, m_ffn1_w2, v_ffn1_w2, False), ("w_in", w_in, m_w_in, v_w_in, True),
           ("w_proj_attn", w_proj_attn, m_w_proj_attn, v_w_proj_attn, False),
           ("w_proj_hgrn", w_proj_hgrn, m_w_proj_hgrn, v_w_proj_hgrn, False),
           ("w_out", w_out, m_w_out, v_w_out, False),
           ("ffn2_w1", ffn2_w1, m_ffn2_w1, v_ffn2_w1, True), ("ffn2_w3", ffn2_w3, m_ffn2_w3, v_ffn2_w3, True),
           ("ffn2_w2", ffn2_w2, m_ffn2_w2, v_ffn2_w2, False)]
    view = lambda t, transposed: t[0].T if transposed else t[0]
    back = lambda t, transposed: t.T[None] if transposed else t[None]
    slots = lambda n: recv[n] if isinstance(recv[n], list) else [recv[n]]
    stepped, _ = _grad_steps("steps", [(slots(n), view(w, tr), view(m, tr), view(v, tr)) for n, w, m, v, tr in big], 128)
    res = {n: tuple(back(t, tr) for t in outs) for (n, _, _, _, tr), outs in zip(big, stepped)}

    small_w = dict(ln1_g=ln1_g, ln1_b=ln1_b, ln2_g=ln2_g, ln2_b=ln2_b, ln3_g=ln3_g, ln3_b=ln3_b, b_in=b_in,
                   attn_sinks=attn_sinks, hgrn_lb_logits=hgrn_lb_logits, hgrn_norm_g=hgrn_norm_g)
    small_m = dict(ln1_g=m_ln1_g, ln1_b=m_ln1_b, ln2_g=m_ln2_g, ln2_b=m_ln2_b, ln3_g=m_ln3_g, ln3_b=m_ln3_b,
                   b_in=m_b_in, attn_sinks=m_attn_sinks, hgrn_lb_logits=m_hgrn_lb_logits, hgrn_norm_g=m_hgrn_norm_g)
    small_v = dict(ln1_g=v_ln1_g, ln1_b=v_ln1_b, ln2_g=v_ln2_g, ln2_b=v_ln2_b, ln3_g=v_ln3_g, ln3_b=v_ln3_b,
                   b_in=v_b_in, attn_sinks=v_attn_sinks, hgrn_lb_logits=v_hgrn_lb_logits, hgrn_norm_g=v_hgrn_norm_g)
    outs, names = _small_step("small_step", small_slots, small_w, small_m, small_v)
    loss = outs[0][0, 0]
    for i, n in enumerate(names):
        res[n] = tuple(outs[1 + 4 * i:5 + 4 * i])

    order = ["ln1_g", "ln1_b", "ffn1_w1", "ffn1_w3", "ffn1_w2", "ln2_g", "ln2_b", "w_in", "b_in", "attn_sinks",
             "hgrn_lb_logits", "hgrn_norm_g", "w_proj_attn", "w_proj_hgrn", "w_out", "ln3_g", "ln3_b",
             "ffn2_w1", "ffn2_w3", "ffn2_w2"]
    return (loss, gx[None], *[res[n][0] for n in order], *[res[n][1] for n in order],
            *[res[n][2] for n in order], *[res[n][3] for n in order])
```

```python
import jax
import jax.numpy as jnp
from jax import lax
from jax.experimental import pallas as pl
from jax.experimental.pallas import tpu as pltpu

F32 = jnp.float32
BF16 = jnp.bfloat16

NDEV = 8
D = 1024
DFF = 2816
RF = DFF // NDEV
DIN = 7680
RIN = DIN // NDEV
RP = D // NDEV
N_Q_HEADS = 16
N_KV_HEADS = 4
HEAD_DIM = 64
ATTN_BLOCK = 128
ROPE_THETA = 500000.0
ROPE_DIM = HEAD_DIM // 4
HGRN_HEADS = 8
HGRN_CHUNK = 64
ALPHA = 2.0 ** 0.25
LN_EPS = 1e-5
RMS_EPS = 1e-6
NEG_INF = -1e30
ADAM_LR = 0.001
ADAM_B1 = 0.9
ADAM_B2 = 0.999
ADAM_EPS = 1e-08
ADAM_WD = 0.01
ADAM_STEP = 10

QKV_W = 1536
HG_W = 4096
GATE_W = 2048
VMEM_LIMIT = 60 * 2 ** 20
PART_ROWS = 256
MESH = pl.DeviceIdType.MESH
HBM_SPEC = pl.BlockSpec(memory_space=pltpu.HBM)


def _params(*sem):
    return pltpu.CompilerParams(dimension_semantics=sem, vmem_limit_bytes=VMEM_LIMIT)


def _dot(a, b):
    return jnp.dot(a, b, preferred_element_type=F32)


def _dot_nt(a, b):
    return lax.dot_general(a, b, (((1,), (1,)), ((), ())), preferred_element_type=F32)


def _dot_tn(a, b):
    return lax.dot_general(a, b, (((0,), (0,)), ((), ())), preferred_element_type=F32)


def _sigmoid(x):
    return 0.5 * jnp.tanh(0.5 * x) + 0.5


def _ln_fwd(z):
    mu = jnp.mean(z, axis=-1, keepdims=True)
    zc = z - mu
    var = jnp.mean(zc * zc, axis=-1, keepdims=True)
    r = lax.rsqrt(var + LN_EPS)
    return zc * r, r


def _ln_bwd(dy, xh, r, g):
    dxh = dy * g
    m1 = jnp.mean(dxh, axis=-1, keepdims=True)
    m2 = jnp.mean(dxh * xh, axis=-1, keepdims=True)
    dz = r * (dxh - m1 - xh * m2)
    return dz, jnp.sum(dy * xh, axis=0, keepdims=True), jnp.sum(dy, axis=0, keepdims=True)


def _load_rows(wall_ref, pieces, sem):
    copies = []
    for dst, off, r in pieces:
        for k in range(NDEV):
            c = pltpu.make_async_copy(wall_ref.at[k, pl.ds(off, r), :], dst.at[pl.ds(k * r, r), :], sem.at[len(copies)])
            c.start()
            copies.append(c)
    for c in copies:
        c.wait()


class _Exchange:
    def __init__(self, inputs, out_shape, scratch, begin, middle, end):
        self.inputs, self.out_shape, self.scratch = inputs, out_shape, scratch
        self.begin, self.middle, self.end = begin, middle, end


def _gather_exchange(shard):
    rows, cols = shard.shape

    def ops(ins, outs, scr):
        (x_ref,), (out_ref,), (send_sems, recv_sems, local_sem) = ins, outs, scr
        x, y, c = lax.axis_index("x"), lax.axis_index("y"), lax.axis_index("c")
        me, sibling = (x, y, c), (x, y, 1 - c)
        chips = [(1 - x, y), (x, 1 - y), (1 - x, 1 - y)]

        def slot(px, py, pc):
            return out_ref.at[4 * px + 2 * py + pc]

        def copy(k, block, to, src=None):
            return pltpu.make_async_remote_copy(
                src_ref=slot(*block) if src is None else src, dst_ref=slot(*block),
                send_sem=send_sems.at[k], recv_sem=recv_sems.at[k], device_id=to, device_id_type=MESH)

        mine = lambda: pltpu.make_async_copy(x_ref, slot(*me), local_sem)
        first = lambda: [copy(0, me, sibling, src=x_ref)] + [
            copy(1 + j, me, (*chip, c), src=x_ref) for j, chip in enumerate(chips)]
        passed = lambda: [copy(4 + j, (*chip, c), sibling) for j, chip in enumerate(chips)]
        return c, me, sibling, chips, copy, mine, first, passed

    def begin(*refs):
        _, _, _, _, _, mine, first, _ = ops(*refs)
        mine().start()
        for cp in first():
            cp.start()

    def middle(*refs):
        c, me, _, chips, copy, _, _, passed = ops(*refs)
        for (j, chip), fwd in zip(enumerate(chips), passed()):
            copy(1 + j, (*chip, c), me).wait_recv()
            fwd.start()

    def end(*refs):
        c, me, sibling, chips, copy, mine, first, passed = ops(*refs)
        copy(0, sibling, me).wait_recv()
        for j, chip in enumerate(chips):
            copy(4 + j, (*chip, 1 - c), me).wait_recv()
        for cp in first() + passed():
            cp.wait_send()
        mine().wait()

    return _Exchange([shard], [jax.ShapeDtypeStruct((NDEV, rows, cols), shard.dtype)],
                     [pltpu.SemaphoreType.DMA((7,)), pltpu.SemaphoreType.DMA((7,)), pltpu.SemaphoreType.DMA],
                     begin, middle, end)


def _grad_exchange(grads, bases, rows, strides=None):
    n = len(grads)
    strides = rows if strides is None else strides

    def copies(g_refs, out_refs, scr):
        send_sems, recv_sems, local_sems = scr
        x, y, c = lax.axis_index("x"), lax.axis_index("y"), lax.axis_index("c")
        me = 4 * x + 2 * y + c
        out = []
        for i in range(n):
            r = rows[i]
            src = lambda k: g_refs[i].at[pl.ds(pl.multiple_of(bases[i] + k * strides[i], 16), r), :]
            out.append(pltpu.make_async_copy(src(me), out_refs[i].at[me], local_sems.at[i]))
            for j in range(1, NDEV):
                px, py, pc = x ^ (j >> 2), y ^ ((j >> 1) & 1), c ^ (j & 1)
                out.append(pltpu.make_async_remote_copy(
                    src_ref=src(4 * px + 2 * py + pc), dst_ref=out_refs[i].at[me],
                    send_sem=send_sems.at[i, j - 1], recv_sem=recv_sems.at[i, j - 1],
                    device_id=(px, py, pc), device_id_type=MESH))
        return out

    def begin(*refs):
        for cp in copies(*refs):
            cp.start()

    def end(*refs):
        for cp in copies(*refs):
            cp.wait()

    return _Exchange(list(grads), [jax.ShapeDtypeStruct((NDEV, r, g.shape[1]), g.dtype) for g, r in zip(grads, rows)],
                     [pltpu.SemaphoreType.DMA((n, NDEV - 1)), pltpu.SemaphoreType.DMA((n, NDEV - 1)),
                      pltpu.SemaphoreType.DMA((n,))], begin, None, end)


def _row_gather_exchange(row):
    def copies(ins, outs, scr):
        (r_ref,), (o_ref,), (send_sems, recv_sems, local_sem) = ins, outs, scr
        x, y, c = lax.axis_index("x"), lax.axis_index("y"), lax.axis_index("c")
        me = 4 * x + 2 * y + c
        out = [pltpu.make_async_copy(r_ref, o_ref.at[me], local_sem)]
        for j in range(1, NDEV):
            out.append(pltpu.make_async_remote_copy(
                src_ref=r_ref, dst_ref=o_ref.at[me], send_sem=send_sems.at[j - 1], recv_sem=recv_sems.at[j - 1],
                device_id=(x ^ (j >> 2), y ^ ((j >> 1) & 1), c ^ (j & 1)), device_id_type=MESH))
        return out

    def begin(*refs):
        for cp in copies(*refs):
            cp.start()

    def end(*refs):
        for cp in copies(*refs):
            cp.wait()

    return _Exchange([row], [jax.ShapeDtypeStruct((NDEV,) + row.shape, row.dtype)],
                     [pltpu.SemaphoreType.DMA((NDEV - 1,)), pltpu.SemaphoreType.DMA((NDEV - 1,)), pltpu.SemaphoreType.DMA],
                     begin, None, end)


def _join(a, b):
    na, oa, sa = len(a.inputs), len(a.out_shape), len(a.scratch)
    split = lambda ins, outs, scr: ((ins[:na], outs[:oa], scr[:sa]), (ins[na:], outs[oa:], scr[sa:]))

    def begin(*refs):
        pa, pb = split(*refs)
        a.begin(*pa)
        b.begin(*pb)

    def end(*refs):
        pa, pb = split(*refs)
        a.end(*pa)
        b.end(*pb)

    return _Exchange(a.inputs + b.inputs, a.out_shape + b.out_shape, a.scratch + b.scratch, begin, None, end)


def _exchange_call(name, ex):
    ni, no = len(ex.inputs), len(ex.out_shape)

    def body(*refs):
        parts = (refs[:ni], refs[ni:ni + no], refs[ni + no:])
        ex.begin(*parts)
        if ex.middle is not None:
            ex.middle(*parts)
        ex.end(*parts)

    return pl.pallas_call(body, name=name, out_shape=ex.out_shape, in_specs=[HBM_SPEC] * ni, out_specs=[HBM_SPEC] * no,
                          scratch_shapes=ex.scratch)(*ex.inputs)


def _call(body, *, name, grid, in_specs, out_specs, out_shape, scratch_shapes, ins, carry=None):
    sem = ("arbitrary",) * len(grid)
    if carry is None:
        outs = pl.pallas_call(body, name=name, grid=grid, in_specs=in_specs, out_specs=out_specs, out_shape=out_shape,
                              scratch_shapes=scratch_shapes, compiler_params=_params(*sem))(*ins)
        return outs, None
    n_in, n_out, n_scr = len(ins), len(out_shape), len(scratch_shapes)
    ci, co = len(carry.inputs), len(carry.out_shape)
    total = 1
    for g in grid:
        total *= g

    def wrapped(*refs):
        own_in, ex_in = refs[:n_in], refs[n_in:n_in + ci]
        o0 = n_in + ci
        own_out, ex_out = refs[o0:o0 + n_out], refs[o0 + n_out:o0 + n_out + co]
        s0 = o0 + n_out + co
        own_scr, ex_scr = refs[s0:s0 + n_scr], refs[s0 + n_scr:]
        step = pl.program_id(0)
        for d in range(1, len(grid)):
            step = step * grid[d] + pl.program_id(d)
        parts = (ex_in, ex_out, ex_scr)
        pl.when(step == 0)(lambda: carry.begin(*parts))
        body(*own_in, *own_out, *own_scr)
        if carry.middle is not None:
            pl.when(step == (3 * total) // 4)(lambda: carry.middle(*parts))
        pl.when(step == total - 1)(lambda: carry.end(*parts))

    outs = pl.pallas_call(
        wrapped, name=name, grid=grid, in_specs=list(in_specs) + [HBM_SPEC] * ci,
        out_specs=list(out_specs) + [HBM_SPEC] * co, out_shape=list(out_shape) + list(carry.out_shape),
        scratch_shapes=list(scratch_shapes) + list(carry.scratch), compiler_params=_params(*sem),
    )(*ins, *carry.inputs)
    return outs[:n_out], outs[n_out:]


def _ffn_fwd(name, xin, affine, wall, offs, tm, loss=None, carry=None):
    T = xin.shape[0]
    nt = T // tm

    def body(*refs):
        it = iter(refs)
        x_ref = next(it)
        if affine is not None:
            g_ref, b_ref = next(it), next(it)
        wall_ref = next(it)
        if loss is not None:
            go_ref, bo_ref, tgt_ref = next(it), next(it), next(it)
        xh_ref, r_ref, a_ref, b2_ref, yb_ref = (next(it) for _ in range(5))
        if loss is not None:
            loss_ref = next(it)
        w1, w3, w2, sem = (next(it) for _ in range(4))

        @pl.when(pl.program_id(0) == 0)
        def _():
            _load_rows(wall_ref, [(w1, offs[0], RF), (w3, offs[1], RF), (w2, offs[2], RF)], sem)
            if loss is not None:
                loss_ref[...] = jnp.zeros_like(loss_ref)

        parts = [slice(p, min(p + PART_ROWS, tm)) for p in range(0, tm, PART_ROWS)]
        ys, ybs = [], []
        for rs in parts:
            y = x_ref[rs, :]
            if affine is not None:
                y = y * g_ref[...] + b_ref[...]
            ys.append(y)
            ybs.append(y.astype(BF16))
            yb_ref[rs, :] = ybs[-1]
        ab = [(_dot_nt(yb, w1[...]).astype(BF16), _dot_nt(yb, w3[...]).astype(BF16)) for yb in ybs]
        us = []
        for rs, (a, b) in zip(parts, ab):
            a_ref[rs, :] = a
            b2_ref[rs, :] = b
            af, bf = a.astype(F32), b.astype(F32)
            us.append((af * _sigmoid(af) * bf).astype(BF16))
        fs = [_dot(u, w2[...]) for u in us]
        for rs, y, f in zip(parts, ys, fs):
            xh, r = _ln_fwd(ALPHA * y + 0.5 * f)
            xh_ref[rs, :] = xh
            r_ref[rs, :] = jnp.broadcast_to(r, (xh.shape[0], 128))
            if loss is not None:
                e = xh * go_ref[...] + bo_ref[...] - tgt_ref[rs, :]
                loss_ref[...] += jnp.sum(e * e) * (0.5 / D)

    row = lambda w: pl.BlockSpec((tm, w), lambda i: (i, 0))
    vec = pl.BlockSpec((1, D), lambda i: (0, 0))
    ins, in_specs = [xin], [row(D)]
    if affine is not None:
        ins += list(affine)
        in_specs += [vec, vec]
    ins.append(wall)
    in_specs.append(HBM_SPEC)
    if loss is not None:
        ins += list(loss)
        in_specs += [vec, vec, row(D)]
    out_shape = [jax.ShapeDtypeStruct((T, D), F32), jax.ShapeDtypeStruct((T, 128), F32),
                 jax.ShapeDtypeStruct((T, DFF), BF16), jax.ShapeDtypeStruct((T, DFF), BF16),
                 jax.ShapeDtypeStruct((T, D), BF16)]
    out_specs = [row(D), row(128), row(DFF), row(DFF), row(D)]
    if loss is not None:
        out_shape.append(jax.ShapeDtypeStruct((8, 128), F32))
        out_specs.append(pl.BlockSpec((8, 128), lambda i: (0, 0)))
    return _call(body, name=name, grid=(nt,), in_specs=in_specs, out_specs=out_specs, out_shape=out_shape,
                 scratch_shapes=[pltpu.VMEM((DFF, D), BF16)] * 3 + [pltpu.SemaphoreType.DMA((3 * NDEV,))],
                 ins=ins, carry=carry)


def _ffn_bwd(name, dy_src, xh, r, g, a, b, wall, offs, tm, with_dx=True, carry=None):
    T = xh.shape[0]
    nt = T // tm
    from_loss = dy_src[0] == "loss"

    def body(*refs):
        it = iter(refs)
        if from_loss:
            bo_ref, tgt_ref = next(it), next(it)
        else:
            dy_ref = next(it)
        xh_ref, r_ref, g_ref, a_ref, b_ref, wall_ref = (next(it) for _ in range(6))
        dyin_ref, dab_ref, u_ref, df_ref, dg_ref, db_ref = (next(it) for _ in range(6))
        if with_dx:
            w1, w3 = next(it), next(it)
        w2, sem = next(it), next(it)

        @pl.when(pl.program_id(0) == 0)
        def _():
            _load_rows(wall_ref, ([(w1, offs[0], RF), (w3, offs[1], RF)] if with_dx else []) + [(w2, offs[2], RF)], sem)
            dg_ref[...] = jnp.zeros_like(dg_ref)
            db_ref[...] = jnp.zeros_like(db_ref)

        parts = [slice(p, min(p + PART_ROWS, tm)) for p in range(0, tm, PART_ROWS)]
        gv = g_ref[...]
        dz, df = [], []
        for rs in parts:
            xhv = xh_ref[rs, :]
            if from_loss:
                dy = (xhv * gv + bo_ref[...] - tgt_ref[rs, :]) * (1.0 / D)
            else:
                dy = dy_ref[rs, :]
            dzp, dgp, dbp = _ln_bwd(dy, xhv, r_ref[rs, :1], gv)
            dg_ref[...] += dgp
            db_ref[...] += dbp
            dz.append(dzp)
            df.append((0.5 * dzp).astype(BF16))
            df_ref[rs, :] = df[-1]
        du = [_dot_nt(d, w2[...]) for d in df]
        da, dbb = [], []
        for rs, dup in zip(parts, du):
            af, bf = a_ref[rs, :].astype(F32), b_ref[rs, :].astype(F32)
            s = _sigmoid(af)
            sl = af * s
            u_ref[rs, :] = (sl * bf).astype(BF16)
            da.append((dup * bf * (s * (1.0 + af * (1.0 - s)))).astype(BF16))
            dbb.append((dup * sl).astype(BF16))
            dab_ref[rs, :DFF] = da[-1]
            dab_ref[rs, DFF:] = dbb[-1]
        for rs, dzp, dap, dbp in zip(parts, dz, da, dbb):
            dyin_ref[rs, :] = ALPHA * dzp + _dot(dap, w1[...]) + _dot(dbp, w3[...]) if with_dx else dzp

    row = lambda w: pl.BlockSpec((tm, w), lambda i: (i, 0))
    vec = pl.BlockSpec((1, D), lambda i: (0, 0))
    if from_loss:
        ins, in_specs = [dy_src[1], dy_src[2]], [vec, row(D)]
    else:
        ins, in_specs = [dy_src[1]], [row(D)]
    ins += [xh, r, g, a, b, wall]
    in_specs += [row(D), row(128), vec, row(DFF), row(DFF), HBM_SPEC]
    return _call(
        body, name=name, grid=(nt,), in_specs=in_specs,
        out_specs=[row(D), row(2 * DFF), row(DFF), row(D), vec, vec],
        out_shape=[jax.ShapeDtypeStruct((T, D), F32), jax.ShapeDtypeStruct((T, 2 * DFF), BF16),
                   jax.ShapeDtypeStruct((T, DFF), BF16), jax.ShapeDtypeStruct((T, D), BF16),
                   jax.ShapeDtypeStruct((1, D), F32), jax.ShapeDtypeStruct((1, D), F32)],
        scratch_shapes=[pltpu.VMEM((DFF, D), BF16)] * (3 if with_dx else 1) + [pltpu.SemaphoreType.DMA((3 * NDEV,))],
        ins=ins, carry=carry)


def _ffn_dx(name, dab, dz, wall, offs, tm, carry=None):
    T = dz.shape[0]

    def body(dab_ref, dz_ref, wall_ref, o_ref, w13, sem):
        @pl.when(pl.program_id(0) == 0)
        def _():
            _load_rows(wall_ref, [(w13.at[pl.ds(0, DFF), :], offs[0], RF), (w13.at[pl.ds(DFF, DFF), :], offs[1], RF)], sem)

        for p in range(0, tm, PART_ROWS):
            rs = slice(p, min(p + PART_ROWS, tm))
            o_ref[rs, :] = ALPHA * dz_ref[rs, :] + _dot(dab_ref[rs, :], w13[...])

    row = lambda w: pl.BlockSpec((tm, w), lambda i: (i, 0))
    (out,), ex = _call(
        body, name=name, grid=(T // tm,), in_specs=[row(2 * DFF), row(D), HBM_SPEC], out_specs=[row(D)],
        out_shape=[jax.ShapeDtypeStruct((T, D), F32)],
        scratch_shapes=[pltpu.VMEM((2 * DFF, D), BF16), pltpu.SemaphoreType.DMA((2 * NDEV,))],
        ins=[dab, dz, wall], carry=carry)
    return out, ex


def _wgrad(name, a, b, bn, tk, col0=0, ncols=None, carry=None):
    T = a.shape[0]
    N = a.shape[1] if ncols is None else ncols
    nk = T // tk
    c0 = col0 // bn

    def body(a_ref, b_ref, o_ref, acc):
        k = pl.program_id(1)

        @pl.when(k == 0)
        def _():
            acc[...] = jnp.zeros_like(acc)

        acc[...] += _dot_tn(a_ref[...], b_ref[...])

        @pl.when(k == nk - 1)
        def _():
            o_ref[...] = acc[...].astype(BF16)

    (out,), ex = _call(
        body, name=name, grid=(N // bn, nk),
        in_specs=[pl.BlockSpec((tk, bn), lambda n, k: (k, n + c0)), pl.BlockSpec((tk, D), lambda n, k: (k, 0))],
        out_specs=[pl.BlockSpec((bn, D), lambda n, k: (n, 0))],
        out_shape=[jax.ShapeDtypeStruct((N, D), BF16)],
        scratch_shapes=[pltpu.VMEM((bn, D), F32)], ins=[a, b], carry=carry)
    return out, ex


def _rope_table(T):
    pos = jnp.arange(T, dtype=F32)
    inv_freq = ROPE_THETA ** (-jnp.arange(0, ROPE_DIM, 2, dtype=F32) / ROPE_DIM)
    ang = pos[:, None] * inv_freq[None, :]
    return jnp.pad(jnp.concatenate([jnp.cos(ang), jnp.sin(ang)], axis=1), ((0, 0), (0, 128 - ROPE_DIM)))


def _rope_expand(cs):
    half = ROPE_DIM // 2
    lane = lax.broadcasted_iota(jnp.int32, cs.shape, 1)
    cos = jnp.where(lane < half, cs, 0.0)
    sin = jnp.where((lane >= half) & (lane < ROPE_DIM), cs, 0.0)
    both = lambda t: t + pltpu.roll(t, HEAD_DIM, 1)
    c = jnp.where(jnp.bitwise_and(lane, HEAD_DIM - 1) < ROPE_DIM, both(cos + pltpu.roll(cos, half, 1)), 1.0)
    return c, -both(pltpu.roll(sin, 128 - half, 1)), both(sin)


def _rope(t, c, s1, s2):
    n = t.shape[1] // 128
    ct, s1t, s2t = (jnp.tile(v, (1, n)) for v in (c, s1, s2))
    w = t.shape[1]
    return t * ct + pltpu.roll(t, w - 8, 1) * s1t + pltpu.roll(t, 8, 1) * s2t


def _rope_t(dr, c, s1, s2):
    n = dr.shape[1] // 128
    ct, s1t, s2t = (jnp.tile(v, (1, n)) for v in (c, s1, s2))
    w = dr.shape[1]
    return dr * ct + pltpu.roll(dr * s1t, 8, 1) + pltpu.roll(dr * s2t, w - 8, 1)


_Q, _K, _V = (0, 1024), (1024, 256), (1280, 256)
_HG = (1536, HG_W)
_GATES = (5632, GATE_W)


def _inproj_fwd(name, xh, g, b, wall, b_in, rope, tm, carry=None):
    T = xh.shape[0]

    def body(xh_ref, g_ref, b_ref, wall_ref, bin_ref, cs_ref,
             qkv_ref, kt_ref, vt_ref, hg_ref, gate_ref, yb_ref, w, sem):
        @pl.when(pl.program_id(0) == 0)
        def _():
            _load_rows(wall_ref, [(w, 0, RIN)], sem)

        yb = (xh_ref[...] * g_ref[...] + b_ref[...]).astype(BF16)
        yb_ref[...] = yb
        c, s1, s2 = _rope_expand(cs_ref[...])

        def piece(start, width):
            return _dot_nt(yb, w[start:start + width, :]) + bin_ref[:, start:start + width]

        k = _rope(piece(*_K), c, s1, s2)
        v = piece(*_V)
        qkv_ref[:, 0:1024] = _rope(piece(*_Q), c, s1, s2).astype(BF16)
        qkv_ref[:, 1024:1280] = k.astype(BF16)
        qkv_ref[:, 1280:1536] = v.astype(BF16)
        kt_ref[...] = k.T.astype(BF16)
        vt_ref[...] = v.T.astype(BF16)
        for j in range(4):
            hg_ref[:, 1024 * j:1024 * (j + 1)] = piece(_HG[0] + 1024 * j, 1024)
        for j in range(2):
            gate_ref[:, 1024 * j:1024 * (j + 1)] = piece(_GATES[0] + 1024 * j, 1024)

    row = lambda wd: pl.BlockSpec((tm, wd), lambda i: (i, 0))
    vec = lambda wd: pl.BlockSpec((1, wd), lambda i: (0, 0))
    colt = pl.BlockSpec((256, tm), lambda i: (0, i))
    return _call(
        body, name=name, grid=(T // tm,),
        in_specs=[row(D), vec(D), vec(D), HBM_SPEC, vec(DIN), row(128)],
        out_specs=[row(QKV_W), colt, colt, row(HG_W), row(GATE_W), row(D)],
        out_shape=[jax.ShapeDtypeStruct((T, QKV_W), BF16), jax.ShapeDtypeStruct((256, T), BF16),
                   jax.ShapeDtypeStruct((256, T), BF16), jax.ShapeDtypeStruct((T, HG_W), F32),
                   jax.ShapeDtypeStruct((T, GATE_W), F32), jax.ShapeDtypeStruct((T, D), BF16)],
        scratch_shapes=[pltpu.VMEM((DIN, D), BF16), pltpu.SemaphoreType.DMA((NDEV,))],
        ins=[xh, g, b, wall, b_in, rope], carry=carry)


def _inproj_bwd(name, dq, dk, dv, dhg, dgates, dz2, wall, rope, tm):
    T = dq.shape[0]

    def body(dq_ref, dk_ref, dv_ref, d0, d1, d2, d3, dga_ref, dgh_ref, dz_ref, wall_ref, cs_ref,
             dy_ref, dproj_ref, dbin_ref, w, sem):
        @pl.when(pl.program_id(0) == 0)
        def _():
            _load_rows(wall_ref, [(w, 0, RIN)], sem)
            dbin_ref[...] = jnp.zeros_like(dbin_ref)

        c, s1, s2 = _rope_expand(cs_ref[...])
        acc = ALPHA * dz_ref[...]
        pieces = [(_Q[0], _rope_t(dq_ref[...], c, s1, s2)), (_K[0], _rope_t(dk_ref[...], c, s1, s2)),
                  (_V[0], dv_ref[...])]
        pieces += [(_HG[0] + 1024 * j, r[...]) for j, r in enumerate((d0, d1, d2, d3))]
        pieces += [(_GATES[0], dga_ref[...]), (_GATES[0] + 1024, dgh_ref[...])]
        for start, val in pieces:
            width = val.shape[1]
            dbin_ref[:, start:start + width] += jnp.sum(val.astype(F32), axis=0, keepdims=True)
            vb = val.astype(BF16)
            dproj_ref[:, start:start + width] = vb
            acc = acc + _dot(vb, w[start:start + width, :])
        dy_ref[...] = acc

    row = lambda wd: pl.BlockSpec((tm, wd), lambda i: (i, 0))
    return pl.pallas_call(
        body, name=name, grid=(T // tm,),
        in_specs=[row(D), row(256), row(256)] + [row(D)] * 4 + [row(D), row(D), row(D), HBM_SPEC, row(128)],
        out_specs=[row(D), row(DIN), pl.BlockSpec((1, DIN), lambda i: (0, 0))],
        out_shape=[jax.ShapeDtypeStruct((T, D), F32), jax.ShapeDtypeStruct((T, DIN), BF16),
                   jax.ShapeDtypeStruct((1, DIN), F32)],
        scratch_shapes=[pltpu.VMEM((DIN, D), BF16), pltpu.SemaphoreType.DMA((NDEV,))],
        compiler_params=_params("arbitrary"),
    )(dq, dk, dv, *dhg, *dgates, dz2, wall, rope)


def _halves(t):
    lane = lax.broadcasted_iota(jnp.int32, t.shape, 1)
    low = lane < HEAD_DIM
    sw = pltpu.roll(t, HEAD_DIM, 1)
    zero = jnp.zeros_like(t)
    h0 = (jnp.where(low, t, zero), jnp.where(low, zero, sw))
    h1 = (jnp.where(low, sw, zero), jnp.where(low, zero, t))
    return h0, h1


def _lane_stack(p_ref, c_ref, gp):
    sl = slice(128 * gp, 128 * (gp + 1))
    hp, hc = _halves(p_ref[:, sl].astype(F32)), _halves(c_ref[:, sl].astype(F32))
    return [jnp.concatenate([hp[gg][0], hc[gg][0], hp[gg][1], hc[gg][1]], axis=0).astype(BF16) for gg in range(2)]


def _row_stack(tp_ref, tc_ref, g):
    band = jnp.concatenate([tp_ref[64 * g:64 * (g + 1), :], tc_ref[64 * g:64 * (g + 1), :]], axis=1)
    z = jnp.zeros_like(band)
    return [jnp.concatenate([band, z], axis=0), jnp.concatenate([z, band], axis=0)]


def _fold(prevm, t4, hh):
    return jnp.where(prevm, t4[256 * hh:256 * hh + 128, :], t4[256 * hh + 128:256 * hh + 256, :])


def _unfold(prevm, t):
    return jnp.concatenate([jnp.where(prevm, t, 0.0), jnp.where(prevm, 0.0, t)], axis=0).astype(BF16)


def _attn_softmax(s, kill, sink):
    s = jnp.where(kill, NEG_INF, s)
    m = jnp.maximum(jnp.max(s, axis=0, keepdims=True), sink)
    p = jnp.exp(s - m)
    es = jnp.exp(sink - m)
    inv = 1.0 / (jnp.sum(p, axis=0, keepdims=True) + es)
    return p * inv, es * inv


def _attn_masks(first):
    row = lax.broadcasted_iota(jnp.int32, (ATTN_BLOCK, 2 * ATTN_BLOCK), 0)
    lane = lax.broadcasted_iota(jnp.int32, (ATTN_BLOCK, 2 * ATTN_BLOCK), 1)
    prevm = row > lane % ATTN_BLOCK
    return prevm, jnp.logical_and(first, prevm)


def _pair_rows(ref, g):
    return jnp.concatenate([ref[:, 256 * g:256 * g + 128], ref[:, 256 * g + 128:256 * (g + 1)]], axis=0)


def _pair_sinks(sink_ref, g, hh):
    h0, h1 = 4 * g + hh, 4 * g + 2 + hh
    return jnp.concatenate([jnp.broadcast_to(sink_ref[:, h0:h0 + 1], (1, ATTN_BLOCK)),
                            jnp.broadcast_to(sink_ref[:, h1:h1 + 1], (1, ATTN_BLOCK))], axis=1)


def _attn_fwd(name, qkv, vt, sinks):
    T = qkv.shape[0]
    nbk = min(ATTN_TILE, T // ATTN_BLOCK)
    rows = nbk * ATTN_BLOCK
    scale = HEAD_DIM ** -0.5

    def block(q_ref, kp_ref, kc_ref, vtp_ref, vtc_ref, sink_ref, o_ref, first):
        prevm, kill = _attn_masks(first)
        kst = _lane_stack(kp_ref, kc_ref, 0) + _lane_stack(kp_ref, kc_ref, 1)
        vts = [_row_stack(vtp_ref, vtc_ref, g) for g in range(N_KV_HEADS)]
        s8 = [_dot_nt(kst[g], _pair_rows(q_ref, g)) * scale for g in range(N_KV_HEADS)]
        pu = [[_unfold(prevm, _attn_softmax(_fold(prevm, s8[g], hh), kill, _pair_sinks(sink_ref, g, hh))[0])
               for hh in range(2)] for g in range(N_KV_HEADS)]
        for g in range(N_KV_HEADS):
            ot = _dot(vts[g][0], pu[g][0]) + _dot(vts[g][1], pu[g][1])
            o_ref[:, 256 * g:256 * g + 128] = ot[:, :ATTN_BLOCK].T.astype(BF16)
            o_ref[:, 256 * g + 128:256 * (g + 1)] = ot[:, ATTN_BLOCK:].T.astype(BF16)

    def body(q_ref, kp_ref, kc_ref, vtp_ref, vtc_ref, sink_ref, o_ref):
        for b in range(nbk):
            rs = slice(ATTN_BLOCK * b, ATTN_BLOCK * (b + 1))
            ps = slice(ATTN_BLOCK * (b - 1), ATTN_BLOCK * b)
            block(q_ref.at[rs, :], kp_ref if b == 0 else kc_ref.at[ps, :], kc_ref.at[rs, :],
                  vtp_ref if b == 0 else vtc_ref.at[:, ps], vtc_ref.at[:, rs], sink_ref, o_ref.at[rs, :],
                  pl.program_id(0) == 0 if b == 0 else False)

    prev = lambda i: jnp.maximum(nbk * i - 1, 0)
    return pl.pallas_call(
        body, name=name, grid=(T // rows,),
        in_specs=[pl.BlockSpec((rows, D), lambda i: (i, 0)),
                  pl.BlockSpec((ATTN_BLOCK, 256), lambda i: (prev(i), 4)),
                  pl.BlockSpec((rows, 256), lambda i: (i, 4)),
                  pl.BlockSpec((256, ATTN_BLOCK), lambda i: (0, prev(i))),
                  pl.BlockSpec((256, rows), lambda i: (0, i)),
                  pl.BlockSpec((1, N_Q_HEADS), lambda i: (0, 0))],
        out_specs=pl.BlockSpec((rows, D), lambda i: (i, 0)),
        out_shape=jax.ShapeDtypeStruct((T, D), BF16),
        compiler_params=_params("arbitrary"),
    )(qkv, qkv, qkv, vt, vt, sinks)


ATTN_TILE = 4
ATTN_BWD_ORDER = ((0, 0), (0, 1), (1, 0), (0, 2), (1, 1), (2, 0), (0, 3), (1, 2), (2, 1), (1, 3), (2, 2), (2, 3))


def _attn_bwd(name, qkv, kt, sinks, dya, carry=None):
    T = qkv.shape[0]
    nbk = min(ATTN_TILE, T // ATTN_BLOCK)
    rows = nbk * ATTN_BLOCK
    scale = HEAD_DIM ** -0.5

    def block(q_ref, kp_ref, kc_ref, vp_ref, vc_ref, ktp_ref, ktc_ref, sink_ref, do_ref, dq_ref, first):
        prevm, kill = _attn_masks(first)
        kst = _lane_stack(kp_ref, kc_ref, 0) + _lane_stack(kp_ref, kc_ref, 1)
        vst = _lane_stack(vp_ref, vc_ref, 0) + _lane_stack(vp_ref, vc_ref, 1)
        kts = [_row_stack(ktp_ref, ktc_ref, g) for g in range(N_KV_HEADS)]
        lane = lax.broadcasted_iota(jnp.int32, (2 * ATTN_BLOCK, 128), 1)
        low = lane < HEAD_DIM
        slane = lax.broadcasted_iota(jnp.int32, (1, 128), 1)
        dkz = [jnp.zeros((2 * ATTN_BLOCK, 128), F32) for _ in range(N_KV_HEADS)]
        dvz = [jnp.zeros((2 * ATTN_BLOCK, 128), F32) for _ in range(N_KV_HEADS)]
        dsink = jnp.zeros((1, 128), F32)
        groups = range(N_KV_HEADS)
        qcat, docat, s8, dp8 = {}, {}, {}, {}
        ds_u, p_u, dsinks = {}, {}, []

        def scores(g):
            qcat[g], docat[g] = _pair_rows(q_ref, g), _pair_rows(do_ref, g)
            s8[g] = _dot_nt(kst[g], qcat[g]) * scale
            dp8[g] = _dot_nt(vst[g], docat[g])

        def algebra(g):
            for hh in range(2):
                pn, ps = _attn_softmax(_fold(prevm, s8[g], hh), kill, _pair_sinks(sink_ref, g, hh))
                dp = _fold(prevm, dp8[g], hh)
                delta = jnp.sum(pn * dp, axis=0, keepdims=True)
                sd = ps * delta
                dsinks.append(jnp.where(slane == 4 * g + hh, -jnp.sum(sd[:, :ATTN_BLOCK]), 0.0)
                              + jnp.where(slane == 4 * g + 2 + hh, -jnp.sum(sd[:, ATTN_BLOCK:]), 0.0))
                ds_u[g, hh] = _unfold(prevm, pn * (dp - delta) * scale)
                p_u[g, hh] = _unfold(prevm, pn)

        def grads(g):
            dqt = _dot(kts[g][0], ds_u[g, 0]) + _dot(kts[g][1], ds_u[g, 1])
            dq_ref[:, 256 * g:256 * g + 128] = dqt[:, :ATTN_BLOCK].T
            dq_ref[:, 256 * g + 128:256 * (g + 1)] = dqt[:, ATTN_BLOCK:].T
            for hh in range(2):
                own = low if hh == 0 else jnp.logical_not(low)
                dk_h = jnp.where(own, _dot(ds_u[g, hh], qcat[g]), 0.0)
                dv_h = jnp.where(own, _dot(p_u[g, hh], docat[g]), 0.0)
                if hh != g % 2:
                    dk_h = pltpu.roll(dk_h, HEAD_DIM, 1)
                    dv_h = pltpu.roll(dv_h, HEAD_DIM, 1)
                dkz[g] = dkz[g] + dk_h
                dvz[g] = dvz[g] + dv_h

        for stage, g in ATTN_BWD_ORDER:
            (scores, algebra, grads)[stage](g)
        for t in dsinks:
            dsink = dsink + t
        pairs = range(N_KV_HEADS // 2)
        return dsink, [dkz[2 * gp] + dkz[2 * gp + 1] for gp in pairs], [dvz[2 * gp] + dvz[2 * gp + 1] for gp in pairs]

    def body(q_ref, kp_ref, kc_ref, vp_ref, vc_ref, ktp_ref, ktc_ref, sink_ref, do_ref,
             dq_ref, dk_ref, dv_ref, ds_ref):
        i = pl.program_id(0)

        @pl.when(i == 0)
        def _():
            ds_ref[...] = jnp.zeros_like(ds_ref)

        res = []
        for b in range(nbk):
            rs = slice(ATTN_BLOCK * b, ATTN_BLOCK * (b + 1))
            ps = slice(ATTN_BLOCK * (b - 1), ATTN_BLOCK * b)
            res.append(block(
                q_ref.at[rs, :], kp_ref if b == 0 else kc_ref.at[ps, :], kc_ref.at[rs, :],
                vp_ref if b == 0 else vc_ref.at[ps, :], vc_ref.at[rs, :],
                ktp_ref if b == 0 else ktc_ref.at[:, ps], ktc_ref.at[:, rs], sink_ref, do_ref.at[rs, :],
                dq_ref.at[rs, :], i == 0 if b == 0 else False))
        dsink = res[0][0]
        for r in res[1:]:
            dsink = dsink + r[0]
        ds_ref[...] += dsink
        for gp in range(N_KV_HEADS // 2):
            cols = slice(128 * gp, 128 * (gp + 1))
            for b in range(nbk):
                cur = pl.ds(pl.multiple_of(i * rows + ATTN_BLOCK * b, ATTN_BLOCK), ATTN_BLOCK)
                dkb, dvb = res[b][1][gp][ATTN_BLOCK:, :], res[b][2][gp][ATTN_BLOCK:, :]
                if b + 1 < nbk:
                    dkb, dvb = dkb + res[b + 1][1][gp][:ATTN_BLOCK, :], dvb + res[b + 1][2][gp][:ATTN_BLOCK, :]
                dk_ref[cur, cols] = dkb
                dv_ref[cur, cols] = dvb

            @pl.when(i > 0)
            def _():
                prv = pl.ds(pl.multiple_of(jnp.maximum(i * rows - ATTN_BLOCK, 0), ATTN_BLOCK), ATTN_BLOCK)
                dk_ref[prv, cols] += res[0][1][gp][:ATTN_BLOCK, :]
                dv_ref[prv, cols] += res[0][2][gp][:ATTN_BLOCK, :]

    prev = lambda i: jnp.maximum(nbk * i - 1, 0)
    whole = lambda w: pl.BlockSpec((T, w), lambda i: (0, 0))
    return _call(
        body, name=name, grid=(T // rows,), scratch_shapes=[], ins=[qkv, qkv, qkv, qkv, qkv, kt, kt, sinks, dya],
        carry=carry,
        in_specs=[pl.BlockSpec((rows, D), lambda i: (i, 0)),
                  pl.BlockSpec((ATTN_BLOCK, 256), lambda i: (prev(i), 4)),
                  pl.BlockSpec((rows, 256), lambda i: (i, 4)),
                  pl.BlockSpec((ATTN_BLOCK, 256), lambda i: (prev(i), 5)),
                  pl.BlockSpec((rows, 256), lambda i: (i, 5)),
                  pl.BlockSpec((256, ATTN_BLOCK), lambda i: (0, prev(i))),
                  pl.BlockSpec((256, rows), lambda i: (0, i)),
                  pl.BlockSpec((1, N_Q_HEADS), lambda i: (0, 0)),
                  pl.BlockSpec((rows, D), lambda i: (i, 0))],
        out_specs=[pl.BlockSpec((rows, D), lambda i: (i, 0)), whole(256), whole(256),
                   pl.BlockSpec((1, 128), lambda i: (0, 0))],
        out_shape=[jax.ShapeDtypeStruct((T, D), F32), jax.ShapeDtypeStruct((T, 256), F32),
                   jax.ShapeDtypeStruct((T, 256), F32), jax.ShapeDtypeStruct((1, 128), F32)])


HG_FWD = (256, 8, 512)
HG_BWD = (128, 8, 512)


def _chunk_sum(tri, x):
    w = x.shape[1]
    h1 = x.astype(BF16)
    h2 = (x - h1.astype(F32)).astype(BF16)
    r = _dot(tri, jnp.concatenate([h1, h2], axis=1))
    return r[:, :w] + r[:, w:]


def _chunk_tri(n, upper=False):
    ri = lax.broadcasted_iota(jnp.int32, (n, n), 0)
    ci = lax.broadcasted_iota(jnp.int32, (n, n), 1)
    same_chunk = jnp.bitwise_xor(ri, ci) < HGRN_CHUNK
    return jnp.where(((ri <= ci) if upper else (ri >= ci)) & same_chunk, 1.0, 0.0).astype(BF16)


def _chunks(t):
    return [t[HGRN_CHUNK * c:HGRN_CHUNK * (c + 1), :] for c in range(t.shape[0] // HGRN_CHUNK)]


def _lower_bound(lbl_ref):
    l0, l1 = lbl_ref[0:1, :], lbl_ref[1:2, :]
    m = jnp.maximum(l0, l1)
    e0, e1 = jnp.exp(l0 - m), jnp.exp(l1 - m)
    return e0 / (e0 + e1)


def _heads(t):
    return [t[:, 128 * h:128 * (h + 1)] for h in range(t.shape[1] // 128)]


def _per_head(fn, *wide):
    return jnp.concatenate([fn(*parts) for parts in zip(*[_heads(t) for t in wide])], axis=1)


def _hgrn_sub(fl, qh, vv, lb, tril_b):
    w = fl.shape[1]
    sg = _sigmoid(fl)
    f = lb + (1.0 - lb) * sg
    k = 1.0 - f
    gc = _chunk_sum(tril_b, jnp.log(f))
    gl_c = [t[HGRN_CHUNK - 1:HGRN_CHUNK, :] for t in _chunks(gc)]
    gl = jnp.concatenate([jnp.broadcast_to(g, (HGRN_CHUNK, w)) for g in gl_c], axis=0)
    sq = _sigmoid(qh)
    eg = jnp.exp(gc)
    eng = jnp.exp(-gc)
    elg = jnp.exp(gl - gc)
    qd = qh * sq * eg
    ki = k * eng
    ke = k * elg
    qd_b, ki_b, ke_b, v_b = (t.astype(BF16) for t in (qd, ki, ke, vv))
    am_b = [jnp.where(tril_b > 0, _dot_nt(qq, kk), 0.0).astype(BF16) for qq, kk in zip(_heads(qd_b), _heads(ki_b))]
    return dict(sg=sg, f=f, sq=sq, eg=eg, eng=eng, elg=elg, qd=qd, ki=ki, ke=ke, egl=[jnp.exp(g) for g in gl_c],
                am_b=am_b, qd_b=qd_b, ki_b=ki_b, ke_b=ke_b, v_b=v_b)


def _hgrn_out(q, st_b):
    outs = []
    for h, (qd_h, v_h) in enumerate(zip(_heads(q["qd_b"]), _heads(q["v_b"]))):
        inter = [_dot_nt(qc, s) for qc, s in zip(_chunks(qd_h), st_b[h])]
        outs.append(_dot(q["am_b"][h], v_h) + jnp.concatenate(inter, axis=0))
    return jnp.concatenate(outs, axis=1)


def _head_rms(o):
    return _per_head(lambda t: jnp.broadcast_to(
        lax.rsqrt(jnp.mean(t * t, axis=-1, keepdims=True) + RMS_EPS), t.shape), o)


def _hgrn_fwd(name, hg, lb_logits, norm_g, cfg):
    T = hg.shape[0]
    sub_rows, hps, th = min(cfg[0], T), cfg[1], min(cfg[2], T)
    cps = sub_rows // HGRN_CHUNK
    nc = th // HGRN_CHUNK

    def body(fl_ref, qh_ref, v_ref, og_ref, lbl_ref, ng_ref, y_ref, ss_ref, st):
        @pl.when(pl.program_id(1) == 0)
        def _():
            st[...] = jnp.zeros_like(st)

        lbs = _lower_bound(lbl_ref)
        ng = jnp.tile(ng_ref[...], (1, hps))
        tril_b = _chunk_tri(sub_rows)

        def sub(si, carry):
            rows = pl.ds(pl.multiple_of(si * sub_rows, sub_rows), sub_rows)
            q = _hgrn_sub(fl_ref[rows, :], qh_ref[rows, :], v_ref[rows, :], lbs, tril_b)
            v_hc = [_chunks(t) for t in _heads(q["v_b"])]
            k_hc = [_chunks(t) for t in _heads(q["ke_b"])]
            s = [st[h] for h in range(hps)]
            st_b = [[] for _ in range(hps)]
            for c in range(cps):
                egl = _heads(q["egl"][c])
                for h in range(hps):
                    ss_ref[h, si * cps + c] = s[h]
                    st_b[h].append(s[h].astype(BF16))
                    s[h] = s[h] * egl[h] + _dot_tn(v_hc[h][c], k_hc[h][c])
            for h in range(hps):
                st[h] = s[h]
            o = _hgrn_out(q, st_b)
            og = og_ref[rows, :]
            y_ref[rows, :] = (o * _head_rms(o) * ng * (og * _sigmoid(og))).astype(BF16)
            return carry

        lax.fori_loop(0, th // sub_rows, sub, 0)

    wd = 128 * hps
    col = lambda j: pl.BlockSpec((th, wd), lambda h, t: (t, (8 // hps) * j + h))
    return pl.pallas_call(
        body, name=name, grid=(HGRN_HEADS // hps, T // th),
        in_specs=[col(0), col(1), col(2), col(3), pl.BlockSpec((2, wd), lambda h, t: (0, h)),
                  pl.BlockSpec((1, 128), lambda h, t: (0, 0))],
        out_specs=[pl.BlockSpec((th, wd), lambda h, t: (t, h)),
                   pl.BlockSpec((hps, nc, 128, 128), lambda h, t: (h, t, 0, 0))],
        out_shape=[jax.ShapeDtypeStruct((T, D), BF16),
                   jax.ShapeDtypeStruct((HGRN_HEADS, T // HGRN_CHUNK, 128, 128), F32)],
        scratch_shapes=[pltpu.VMEM((hps, 128, 128), F32)],
        compiler_params=_params("parallel", "arbitrary"),
    )(hg, hg, hg, hg, lb_logits, norm_g)


def _hgrn_bwd(name, hg, lb_logits, norm_g, sstart, dyh, cfg, carry=None):
    T = hg.shape[0]
    sub_rows, hps, th = min(cfg[0], T), cfg[1], min(cfg[2], T)
    cps = sub_rows // HGRN_CHUNK
    nc = th // HGRN_CHUNK
    nt = T // th
    wd = 128 * hps

    def body(fl_ref, qh_ref, v_ref, og_ref, lbl_ref, ng_ref, ss_ref, dy_ref,
             dfl_ref, dqh_ref, dv_ref, dog_ref, dlb_ref, dng_ref, dst):
        @pl.when(pl.program_id(1) == 0)
        def _():
            dst[...] = jnp.zeros_like(dst)
            dlb_ref[...] = jnp.zeros_like(dlb_ref)
            dng_ref[...] = jnp.zeros_like(dng_ref)

        lb = _lower_bound(lbl_ref)
        ng = jnp.tile(ng_ref[...], (1, hps))
        last = lax.broadcasted_iota(jnp.int32, (HGRN_CHUNK, wd), 0) == HGRN_CHUNK - 1
        nsub = th // sub_rows
        cat0 = lambda parts: jnp.concatenate(parts, axis=0)
        tril_b, triu_b = _chunk_tri(sub_rows), _chunk_tri(sub_rows, upper=True)

        def sub(step, carry):
            si = nsub - 1 - step
            rows = pl.ds(pl.multiple_of(si * sub_rows, sub_rows), sub_rows)
            qh, og, dy = qh_ref[rows, :], og_ref[rows, :], dy_ref[rows, :]
            q = _hgrn_sub(fl_ref[rows, :], qh, v_ref[rows, :], lb, tril_b)
            s_in = [[ss_ref[h, si * cps + c] for c in range(cps)] for h in range(hps)]
            st_b = [[s.astype(BF16) for s in row] for row in s_in]
            o = _hgrn_out(q, st_b)
            rr = _head_rms(o)
            oh = o * rr
            sog = _sigmoid(og)
            dog_ref[rows, :] = (dy * oh * ng * (sog * (1.0 + og * (1.0 - sog)))).astype(BF16)
            don = dy * (og * sog)
            dng_w = jnp.sum(don * oh, axis=0, keepdims=True)
            dd = don * ng
            mean_h = _per_head(lambda t: jnp.broadcast_to(jnp.mean(t, axis=-1, keepdims=True), t.shape), dd * oh)
            do_b = (rr * (dd - oh * mean_h)).astype(BF16)
            do_h, qd_h, ki_h, ke_h, v_h = (_heads(t) for t in (do_b, q["qd_b"], q["ki_b"], q["ke_b"], q["v_b"]))
            da_b = [jnp.where(tril_b > 0, _dot_nt(do_h[h], v_h[h]), 0.0).astype(BF16) for h in range(hps)]
            do_c, qd_c, ke_c, v_c = ([_chunks(t) for t in hs] for hs in (do_h, qd_h, ke_h, v_h))
            dsp = [[None] * cps for _ in range(hps)]
            dgl_dec = [[None] * cps for _ in range(hps)]
            d = [dst[h] for h in range(hps)]
            for c in reversed(range(cps)):
                egl = _heads(q["egl"][c])
                for h in range(hps):
                    dsp[h][c] = d[h]
                    dgl_dec[h][c] = jnp.sum(d[h] * s_in[h][c], axis=0, keepdims=True) * egl[h]
                    d[h] = d[h] * egl[h] + _dot_tn(do_c[h][c], qd_c[h][c])
            for h in range(hps):
                dst[h] = d[h]
                dng_ref[h] += dng_w[:, 128 * h:128 * (h + 1)]
            dsp_b = [[t.astype(BF16) for t in row] for row in dsp]
            dv_ref[rows, :] = jnp.concatenate(
                [_dot_tn(q["am_b"][h], do_h[h]) + cat0([_dot_nt(ke_c[h][c], dsp_b[h][c]) for c in range(cps)])
                 for h in range(hps)], axis=1).astype(BF16)
            dqd = jnp.concatenate(
                [_dot(da_b[h], ki_h[h]) + cat0([_dot(do_c[h][c], st_b[h][c]) for c in range(cps)])
                 for h in range(hps)], axis=1)
            dki = jnp.concatenate([_dot_tn(da_b[h], qd_h[h]) for h in range(hps)], axis=1)
            dke = jnp.concatenate([cat0([_dot(v_c[h][c], dsp_b[h][c]) for c in range(cps)]) for h in range(hps)], axis=1)
            dkk = dke * q["ke"]
            dgl = [jnp.sum(t, axis=0, keepdims=True) + jnp.concatenate([dgl_dec[h][c] for h in range(hps)], axis=1)
                   for c, t in enumerate(_chunks(dkk))]
            dgc = dqd * q["qd"] - dki * q["ki"] - dkk + cat0([jnp.where(last, g, 0.0) for g in dgl])
            dlf = _chunk_sum(triu_b, dgc)
            df = dlf / q["f"] - (dki * q["eng"] + dke * q["elg"])
            sg = q["sg"]
            dlb_ref[...] += jnp.sum(df * (1.0 - sg), axis=0, keepdims=True)
            dfl_ref[rows, :] = (df * (1.0 - lb) * sg * (1.0 - sg)).astype(BF16)
            sq = q["sq"]
            dqh_ref[rows, :] = (dqd * q["eg"] * (sq * (1.0 + qh * (1.0 - sq)))).astype(BF16)
            return carry

        lax.fori_loop(0, nsub, sub, 0)
    col = lambda j: pl.BlockSpec((th, wd), lambda h, t: (nt - 1 - t, (8 // hps) * j + h))
    out = pl.BlockSpec((th, wd), lambda h, t: (nt - 1 - t, h))
    return _call(
        body, name=name, grid=(HGRN_HEADS // hps, nt),
        in_specs=[col(0), col(1), col(2), col(3), pl.BlockSpec((2, wd), lambda h, t: (0, h)),
                  pl.BlockSpec((1, 128), lambda h, t: (0, 0)),
                  pl.BlockSpec((hps, nc, 128, 128), lambda h, t: (h, nt - 1 - t, 0, 0)), out],
        out_specs=[out, out, out, out, pl.BlockSpec((1, wd), lambda h, t: (0, h)),
                   pl.BlockSpec((hps, 1, 128), lambda h, t: (h, 0, 0))],
        out_shape=[jax.ShapeDtypeStruct((T, D), BF16)] * 4 + [jax.ShapeDtypeStruct((1, D), F32),
                                                              jax.ShapeDtypeStruct((HGRN_HEADS, 1, 128), F32)],
        scratch_shapes=[pltpu.VMEM((hps, 128, 128), F32)],
        ins=[hg, hg, hg, hg, lb_logits, norm_g, sstart, dyh], carry=carry)


def _mix_fwd(name, ya, yh, gates, xh1, g1, b1, wall, tm):
    T = ya.shape[0]

    def body(ya_ref, yh_ref, ga_ref, gh_ref, xh_ref, g_ref, b_ref, wall_ref,
             xo_ref, r_ref, pa_ref, ph_ref, mg_ref, wpa, wph, wo, sem):
        @pl.when(pl.program_id(0) == 0)
        def _():
            _load_rows(wall_ref, [(wpa, RIN, RP), (wph, RIN + RP, RP), (wo, RIN + 2 * RP, RP)], sem)

        parts = [slice(p, min(p + PART_ROWS, tm)) for p in range(0, tm, PART_ROWS)]
        pas = [_dot(ya_ref[rs, :], wpa[...]).astype(BF16) for rs in parts]
        phs = [_dot(yh_ref[rs, :], wph[...]).astype(BF16) for rs in parts]
        mgs = []
        for rs, pa, ph in zip(parts, pas, phs):
            pa_ref[rs, :] = pa
            ph_ref[rs, :] = ph
            mgs.append((_sigmoid(ga_ref[rs, :]) * pa.astype(F32) + _sigmoid(gh_ref[rs, :]) * ph.astype(F32)).astype(BF16))
            mg_ref[rs, :] = mgs[-1]
        mixes = [_dot(mg, wo[...]) for mg in mgs]
        for rs, mix in zip(parts, mixes):
            y1 = xh_ref[rs, :] * g_ref[...] + b_ref[...]
            xh2, r = _ln_fwd(ALPHA * y1 + mix)
            xo_ref[rs, :] = xh2
            r_ref[rs, :] = jnp.broadcast_to(r, (xh2.shape[0], 128))

    row = lambda w: pl.BlockSpec((tm, w), lambda i: (i, 0))
    vec = pl.BlockSpec((1, D), lambda i: (0, 0))
    return pl.pallas_call(
        body, name=name, grid=(T // tm,),
        in_specs=[row(D), row(D), pl.BlockSpec((tm, D), lambda i: (i, 0)), pl.BlockSpec((tm, D), lambda i: (i, 1)),
                  row(D), vec, vec, HBM_SPEC],
        out_specs=[row(D), row(128), row(D), row(D), row(D)],
        out_shape=[jax.ShapeDtypeStruct((T, D), F32), jax.ShapeDtypeStruct((T, 128), F32)]
        + [jax.ShapeDtypeStruct((T, D), BF16)] * 3,
        scratch_shapes=[pltpu.VMEM((D, D), BF16)] * 3 + [pltpu.SemaphoreType.DMA((3 * NDEV,))],
        compiler_params=_params("arbitrary"),
    )(ya, yh, gates, gates, xh1, g1, b1, wall)


def _mix_bwd(name, dy2, xh2, r2, g2, gates, pa, ph, wall, tm, carry=None):
    T = dy2.shape[0]

    def body(dy_ref, xh_ref, r_ref, g_ref, ga_ref, gh_ref, pa_ref, ph_ref, wall_ref,
             dz_ref, dmix_ref, dpa_ref, dph_ref, dga_ref, dgh_ref, dya_ref, dyh_ref, dg_ref, db_ref,
             wpa, wph, wo, sem):
        @pl.when(pl.program_id(0) == 0)
        def _():
            _load_rows(wall_ref, [(wpa, RIN, RP), (wph, RIN + RP, RP), (wo, RIN + 2 * RP, RP)], sem)
            dg_ref[...] = jnp.zeros_like(dg_ref)
            db_ref[...] = jnp.zeros_like(db_ref)

        parts = [slice(p, min(p + PART_ROWS, tm)) for p in range(0, tm, PART_ROWS)]
        dmix = []
        for rs in parts:
            dz, dgp, dbp = _ln_bwd(dy_ref[rs, :], xh_ref[rs, :], r_ref[rs, :1], g_ref[...])
            dg_ref[...] += dgp
            db_ref[...] += dbp
            dz_ref[rs, :] = dz
            dmix.append(dz.astype(BF16))
            dmix_ref[rs, :] = dmix[-1]
        dmgs = [_dot_nt(d, wo[...]) for d in dmix]
        dpas, dphs = [], []
        for rs, dmg in zip(parts, dmgs):
            sa, sh = _sigmoid(ga_ref[rs, :]), _sigmoid(gh_ref[rs, :])
            dga_ref[rs, :] = (dmg * pa_ref[rs, :].astype(F32) * sa * (1.0 - sa)).astype(BF16)
            dgh_ref[rs, :] = (dmg * ph_ref[rs, :].astype(F32) * sh * (1.0 - sh)).astype(BF16)
            dpas.append((dmg * sa).astype(BF16))
            dphs.append((dmg * sh).astype(BF16))
            dpa_ref[rs, :] = dpas[-1]
            dph_ref[rs, :] = dphs[-1]
        for rs, dpa, dph in zip(parts, dpas, dphs):
            dya_ref[rs, :] = _dot_nt(dpa, wpa[...]).astype(BF16)
            dyh_ref[rs, :] = _dot_nt(dph, wph[...])

    row = lambda w: pl.BlockSpec((tm, w), lambda i: (i, 0))
    vec = pl.BlockSpec((1, D), lambda i: (0, 0))
    return _call(
        body, name=name, grid=(T // tm,),
        in_specs=[row(D), row(D), row(128), vec, pl.BlockSpec((tm, D), lambda i: (i, 0)),
                  pl.BlockSpec((tm, D), lambda i: (i, 1)), row(D), row(D), HBM_SPEC],
        out_specs=[row(D)] * 8 + [vec, vec],
        out_shape=[jax.ShapeDtypeStruct((T, D), F32)] + [jax.ShapeDtypeStruct((T, D), BF16)] * 6
        + [jax.ShapeDtypeStruct((T, D), F32)] + [jax.ShapeDtypeStruct((1, D), F32)] * 2,
        scratch_shapes=[pltpu.VMEM((D, D), BF16)] * 3 + [pltpu.SemaphoreType.DMA((3 * NDEV,))],
        ins=[dy2, xh2, r2, g2, gates, gates, pa, ph, wall], carry=carry)


def _adam(w, g, m, v):
    m = ADAM_B1 * m + (1.0 - ADAM_B1) * g
    v = ADAM_B2 * v + (1.0 - ADAM_B2) * (g * g)
    m_hat = m / (1.0 - ADAM_B1 ** ADAM_STEP)
    v_hat = v / (1.0 - ADAM_B2 ** ADAM_STEP)
    delta = -ADAM_LR * (m_hat / (jnp.sqrt(v_hat) + ADAM_EPS) + ADAM_WD * w)
    return delta, m, v


def _grad_steps(name, items, cb, carry=None):
    n = len(items)
    pieces = [len(it[0]) for it in items]

    def body(*refs):
        it = iter(refs[:sum(pieces) + 3 * n])
        outs = refs[sum(pieces) + 3 * n:]
        for i in range(n):
            accs = []
            for _ in range(pieces[i]):
                r_ref = next(it)
                acc = r_ref[0].astype(F32)
                for k in range(1, NDEV):
                    acc = acc + r_ref[k].astype(F32)
                accs.append(acc)
            acc = accs[0] if len(accs) == 1 else jnp.concatenate(accs, axis=0)
            w_ref, m_ref, v_ref = next(it), next(it), next(it)
            g_ref, d_ref, mo_ref, vo_ref = outs[4 * i:4 * i + 4]
            g_ref[...] = acc
            d, mm, vv = _adam(w_ref[...], acc, m_ref[...], v_ref[...])
            d_ref[...] = d
            mo_ref[...] = mm
            vo_ref[...] = vv

    in_specs, out_specs, out_shape, ins = [], [], [], []
    for recv, w, m, v in items:
        blk = pl.BlockSpec((w.shape[0], cb), lambda j: (0, j))
        in_specs += [pl.BlockSpec((NDEV, rp.shape[1], cb), lambda j: (0, 0, j)) for rp in recv] + [blk, blk, blk]
        out_specs += [blk] * 4
        out_shape += [jax.ShapeDtypeStruct(w.shape, F32)] * 4
        ins += list(recv) + [w, m, v]
    outs, ex = _call(body, name=name, grid=(D // cb,), in_specs=in_specs, out_specs=out_specs, out_shape=out_shape,
                     scratch_shapes=[], ins=ins, carry=carry)
    return [tuple(outs[4 * i:4 * i + 4]) for i in range(n)], ex


_SMALL = [("ln1_g", D), ("ln1_b", D), ("ln2_g", D), ("ln2_b", D), ("ln3_g", D), ("ln3_b", D),
          ("b_in", DIN), ("lb", D), ("attn_sinks", 128), ("hgrn_norm_g", 128), ("loss", 128)]
_SMALL_OFF = {}
_o = 0
for _n, _w in _SMALL:
    _SMALL_OFF[_n] = (_o, _w)
    _o += _w
PACK = _o


def _small_step(name, slots, small_w, small_m, small_v):
    names = ["ln1_g", "ln1_b", "ln2_g", "ln2_b", "ln3_g", "ln3_b", "b_in", "attn_sinks", "hgrn_lb_logits", "hgrn_norm_g"]
    np_ = len(names)

    def body(*refs):
        s_ref = refs[0]
        w_refs = refs[1:1 + np_]
        m_refs = refs[1 + np_:1 + 2 * np_]
        v_refs = refs[1 + 2 * np_:1 + 3 * np_]
        loss_ref, outs = refs[1 + 3 * np_], refs[2 + 3 * np_:]
        tot = s_ref[0]
        for k in range(1, NDEV):
            tot = tot + s_ref[k]

        def part(n):
            o, w = _SMALL_OFF[n]
            return tot[:, o:o + w]

        loss_ref[...] = part("loss")
        for i, n in enumerate(names):
            w = w_refs[i][...]
            if n == "hgrn_lb_logits":
                m_ = jnp.maximum(w[0:1, :], w[1:2, :])
                e0, e1 = jnp.exp(w[0:1, :] - m_), jnp.exp(w[1:2, :] - m_)
                p0 = e0 / (e0 + e1)
                t = p0 * (1.0 - p0) * part("lb")
                g = jnp.concatenate([t, -t], axis=0)
            elif n == "attn_sinks":
                g = part(n)[:, :N_Q_HEADS]
            else:
                g = part(n)
            d, mm, vv = _adam(w, g, m_refs[i][...], v_refs[i][...])
            outs[4 * i][...] = g
            outs[4 * i + 1][...] = d
            outs[4 * i + 2][...] = mm
            outs[4 * i + 3][...] = vv

    out_shape = [jax.ShapeDtypeStruct((1, 128), F32)]
    for n in names:
        out_shape += [jax.ShapeDtypeStruct(small_w[n].shape, F32)] * 4
    return pl.pallas_call(
        body, name=name, out_shape=out_shape,
        compiler_params=pltpu.CompilerParams(vmem_limit_bytes=VMEM_LIMIT),
    )(slots, *[small_w[n] for n in names], *[small_m[n] for n in names], *[small_v[n] for n in names]), names


def _tile(T, pref):
    return min(T, pref)


def kernel(x, ln1_g, ln1_b, ffn1_w1, ffn1_w3, ffn1_w2, ln2_g, ln2_b, w_in, b_in, attn_sinks, hgrn_lb_logits, hgrn_norm_g, w_proj_attn, w_proj_hgrn, w_out, ln3_g, ln3_b, ffn2_w1, ffn2_w3, ffn2_w2, loss_target, m_ln1_g, m_ln1_b, m_ffn1_w1, m_ffn1_w3, m_ffn1_w2, m_ln2_g, m_ln2_b, m_w_in, m_b_in, m_attn_sinks, m_hgrn_lb_logits, m_hgrn_norm_g, m_w_proj_attn, m_w_proj_hgrn, m_w_out, m_ln3_g, m_ln3_b, m_ffn2_w1, m_ffn2_w3, m_ffn2_w2, v_ln1_g, v_ln1_b, v_ffn1_w1, v_ffn1_w3, v_ffn1_w2, v_ln2_g, v_ln2_b, v_w_in, v_b_in, v_attn_sinks, v_hgrn_lb_logits, v_hgrn_norm_g, v_w_proj_attn, v_w_proj_hgrn, v_w_out, v_ln3_g, v_ln3_b, v_ffn2_w1, v_ffn2_w3, v_ffn2_w2):
    T = x.shape[1]
    xs = x[0]
    tgt = loss_target[0]
    tm = _tile(T, 256)
    tm2 = _tile(T, 512)
    tk = _tile(T, 2048)

    t_bf = lambda w: w[0].T.astype(BF16)
    n_bf = lambda w: w[0].astype(BF16)
    ffn_shard = lambda w1, w3, w2: jnp.concatenate([t_bf(w1), t_bf(w3), n_bf(w2)], axis=0)
    mix_shard = jnp.concatenate([t_bf(w_in), n_bf(w_proj_attn), n_bf(w_proj_hgrn), n_bf(w_out)], axis=0)
    (ffn1_all,) = _exchange_call("gather_ffn1", _gather_exchange(ffn_shard(ffn1_w1, ffn1_w3, ffn1_w2)))
    ffn_offs = (0, RF, 2 * RF)
    rope = _rope_table(T)

    (xh1, r1, a1, b1, xb0), (mix_all,) = _ffn_fwd("ffn1_fwd", xs, None, ffn1_all, ffn_offs, tm2,
                                                  carry=_gather_exchange(mix_shard))
    (qkv, kt, vt, hg, gates, y1b), (ffn2_all,) = _inproj_fwd(
        "inproj_fwd", xh1, ln1_g, ln1_b, mix_all, b_in, rope, tm2,
        carry=_gather_exchange(ffn_shard(ffn2_w1, ffn2_w3, ffn2_w2)))
    ya = _attn_fwd("attn_fwd", qkv, vt, attn_sinks)
    yh, sstart = _hgrn_fwd("hgrn_fwd", hg, hgrn_lb_logits, hgrn_norm_g, HG_FWD)
    xh2, r2, pa, ph, merged = _mix_fwd("mix_fwd", ya, yh, gates, xh1, ln1_g, ln1_b, mix_all, tm2)
    (xh3, r3, a2, b2, y2b, loss_part), _ = _ffn_fwd("ffn2_fwd", xh2, (ln2_g, ln2_b), ffn2_all, ffn_offs, tm2,
                                                    loss=(ln3_g, ln3_b, tgt))

    (dy2, dab2, u2, df2, dg3, db3), _ = _ffn_bwd("ffn2_bwd", ("loss", ln3_b, tgt), xh3, r3, ln3_g, a2, b2, ffn2_all,
                                                 ffn_offs, tm)
    recv = {}
    g_ffn2_13, _ = _wgrad("wgrad_ffn2_w13", dab2, y2b, DFF // 2, tk)
    g_ffn2_2, _ = _wgrad("wgrad_ffn2_w2", u2, df2, DFF // 2, tk)
    (dz2, dmix, dpa, dph, dga, dgh, dya, dyh, dg2, db2), (recv["ffn2_w1"],) = _mix_bwd(
        "mix_bwd", dy2, xh2, r2, ln2_g, gates, pa, ph, mix_all, tm2, carry=_grad_exchange([g_ffn2_13], [0], [RF]))
    g_wo, _ = _wgrad("wgrad_w_out", merged, dmix, D, tk)
    g_pa, _ = _wgrad("wgrad_w_pa", ya, dpa, D, tk)
    g_ph, _ = _wgrad("wgrad_w_ph", yh, dph, D, tk)
    (dq, dk, dv, dsink), (recv["ffn2_w3"],) = _attn_bwd(
        "attn_bwd", qkv, kt, attn_sinks, dya, carry=_grad_exchange([g_ffn2_13], [DFF], [RF]))
    (dfl, dqh, dih, dog, dlb, dng), (recv["ffn2_w2"], recv["w_proj_attn"], recv["w_proj_hgrn"], recv["w_out"]) = \
        _hgrn_bwd("hgrn_bwd", hg, hgrn_lb_logits, hgrn_norm_g, sstart, dyh, HG_BWD,
                  carry=_grad_exchange([g_ffn2_2, g_pa, g_ph, g_wo], [0, 0, 0, 0], [RF, RP, RP, RP]))
    dy1, dproj, dbin = _inproj_bwd("inproj_bwd", dq, dk, dv, (dfl, dqh, dih, dog), (dga, dgh), dz2, mix_all, rope, tm)
    g_win, _ = _wgrad("wgrad_w_in", dproj, y1b, DIN // 4, tk)
    win_rows = (400, 288, 272)
    win_base = (0, 400, 688)
    (dz1, dab1, u1, df1, dg1, db1), (rw0,) = _ffn_bwd(
        "ffn1_bwd", ("dy", dy1), xh1, r1, ln1_g, a1, b1, ffn1_all, ffn_offs, tm2, with_dx=False,
        carry=_grad_exchange([g_win], win_base[:1], win_rows[:1], [RIN]))
    g_ffn1_2, (rw1,) = _wgrad("wgrad_ffn1_w2", u1, df1, DFF // 2, tk,
                              carry=_grad_exchange([g_win], win_base[1:2], win_rows[1:2], [RIN]))
    g_ffn1_1, (recv["ffn1_w2"],) = _wgrad("wgrad_ffn1_w1", dab1, xb0, DFF // 2, tk, 0, DFF,
                                          carry=_grad_exchange([g_ffn1_2], [0], [RF]))
    g_ffn1_3, (recv["ffn1_w1"],) = _wgrad("wgrad_ffn1_w3", dab1, xb0, DFF // 2, tk, DFF, DFF,
                                          carry=_grad_exchange([g_ffn1_1], [0], [RF]))
    parts = {"ln1_g": dg1, "ln1_b": db1, "ln2_g": dg2, "ln2_b": db2, "ln3_g": dg3, "ln3_b": db3, "b_in": dbin,
             "lb": dlb, "attn_sinks": dsink, "hgrn_norm_g": jnp.sum(dng, axis=0), "loss": loss_part[0:1, :]}
    packed = jnp.concatenate([parts[n] for n, _ in _SMALL], axis=1)
    gx, (recv["ffn1_w3"], rw2, small_slots) = _ffn_dx(
        "ffn1_dx", dab1, dz1, ffn1_all, ffn_offs, tm2,
        carry=_join(_grad_exchange([g_ffn1_3, g_win], [0, win_base[2]], [RF, win_rows[2]], [RF, RIN]),
                    _row_gather_exchange(packed)))
    recv["w_in"] = [rw0, rw1, rw2]

    big = [("ffn1_w1", ffn1_w1, m_ffn1_w1, v_ffn1_w1, True), ("ffn1_w3", ffn1_w3, m_ffn1_w3, v_ffn1_w3, True),
           ("ffn1_w2", ffn1_w2, m_ffn1_w2, v_ffn1_w2, False), ("w_in", w_in, m_w_in, v_w_in, True),
           ("w_proj_attn", w_proj_attn, m_w_proj_attn, v_w_proj_attn, False),
           ("w_proj_hgrn", w_proj_hgrn, m_w_proj_hgrn, v_w_proj_hgrn, False),
           ("w_out", w_out, m_w_out, v_w_out, False),
           ("ffn2_w1", ffn2_w1, m_ffn2_w1, v_ffn2_w1, True), ("ffn2_w3", ffn2_w3, m_ffn2_w3, v_ffn2_w3, True),
           ("ffn2_w2", ffn2_w2, m_ffn2_w2, v_ffn2_w2, False)]
    view = lambda t, transposed: t[0].T if transposed else t[0]
    back = lambda t, transposed: t.T[None] if transposed else t[None]
    slots = lambda n: recv[n] if isinstance(recv[n], list) else [recv[n]]
    stepped, _ = _grad_steps("steps", [(slots(n), view(w, tr), view(m, tr), view(v, tr)) for n, w, m, v, tr in big], 128)
    res = {n: tuple(back(t, tr) for t in outs) for (n, _, _, _, tr), outs in zip(big, stepped)}

    small_w = dict(ln1_g=ln1_g, ln1_b=ln1_b, ln2_g=ln2_g, ln2_b=ln2_b, ln3_g=ln3_g, ln3_b=ln3_b, b_in=b_in,
                   attn_sinks=attn_sinks, hgrn_lb_logits=hgrn_lb_logits, hgrn_norm_g=hgrn_norm_g)
    small_m = dict(ln1_g=m_ln1_g, ln1_b=m_ln1_b, ln2_g=m_ln2_g, ln2_b=m_ln2_b, ln3_g=m_ln3_g, ln3_b=m_ln3_b,
                   b_in=m_b_in, attn_sinks=m_attn_sinks, hgrn_lb_logits=m_hgrn_lb_logits, hgrn_norm_g=m_hgrn_norm_g)
    small_v = dict(ln1_g=v_ln1_g, ln1_b=v_ln1_b, ln2_g=v_ln2_g, ln2_b=v_ln2_b, ln3_g=v_ln3_g, ln3_b=v_ln3_b,
                   b_in=v_b_in, attn_sinks=v_attn_sinks, hgrn_lb_logits=v_hgrn_lb_logits, hgrn_norm_g=v_hgrn_norm_g)
    outs, names = _small_step("small_step", small_slots, small_w, small_m, small_v)
    loss = outs[0][0, 0]
    for i, n in enumerate(names):
        res[n] = tuple(outs[1 + 4 * i:5 + 4 * i])

    order = ["ln1_g", "ln1_b", "ffn1_w1", "ffn1_w3", "ffn1_w2", "ln2_g", "ln2_b", "w_in", "b_in", "attn_sinks",
             "hgrn_lb_logits", "hgrn_norm_g", "w_proj_attn", "w_proj_hgrn", "w_out", "ln3_g", "ln3_b",
             "ffn2_w1", "ffn2_w3", "ffn2_w2"]
    return (loss, gx[None], *[res[n][0] for n in order], *[res[n][1] for n in order],
            *[res[n][2] for n in order], *[res[n][3] for n in order])
```

```python
import jax
import jax.numpy as jnp
from jax import lax
from jax.experimental import pallas as pl
from jax.experimental.pallas import tpu as pltpu

F32 = jnp.float32
BF16 = jnp.bfloat16

NDEV = 8
D = 1024
DFF = 2816
RF = DFF // NDEV
DIN = 7680
RIN = DIN // NDEV
RP = D // NDEV
N_Q_HEADS = 16
N_KV_HEADS = 4
HEAD_DIM = 64
ATTN_BLOCK = 128
ROPE_THETA = 500000.0
ROPE_DIM = HEAD_DIM // 4
HGRN_HEADS = 8
HGRN_CHUNK = 64
ALPHA = 2.0 ** 0.25
LN_EPS = 1e-5
RMS_EPS = 1e-6
NEG_INF = -1e30
ADAM_LR = 0.001
ADAM_B1 = 0.9
ADAM_B2 = 0.999
ADAM_EPS = 1e-08
ADAM_WD = 0.01
ADAM_STEP = 10

QKV_W = 1536
HG_W = 4096
GATE_W = 2048
VMEM_LIMIT = 60 * 2 ** 20
PART_ROWS = 256
MESH = pl.DeviceIdType.MESH
HBM_SPEC = pl.BlockSpec(memory_space=pltpu.HBM)


def _params(*sem):
    return pltpu.CompilerParams(dimension_semantics=sem, vmem_limit_bytes=VMEM_LIMIT)


def _dot(a, b):
    return jnp.dot(a, b, preferred_element_type=F32)


def _dot_nt(a, b):
    return lax.dot_general(a, b, (((1,), (1,)), ((), ())), preferred_element_type=F32)


def _dot_tn(a, b):
    return lax.dot_general(a, b, (((0,), (0,)), ((), ())), preferred_element_type=F32)


def _sigmoid(x):
    return 0.5 * jnp.tanh(0.5 * x) + 0.5


def _ln_fwd(z):
    mu = jnp.mean(z, axis=-1, keepdims=True)
    zc = z - mu
    var = jnp.mean(zc * zc, axis=-1, keepdims=True)
    r = lax.rsqrt(var + LN_EPS)
    return zc * r, r


def _ln_bwd(dy, xh, r, g):
    dxh = dy * g
    m1 = jnp.mean(dxh, axis=-1, keepdims=True)
    m2 = jnp.mean(dxh * xh, axis=-1, keepdims=True)
    dz = r * (dxh - m1 - xh * m2)
    return dz, jnp.sum(dy * xh, axis=0, keepdims=True), jnp.sum(dy, axis=0, keepdims=True)


def _load_rows(wall_ref, pieces, sem):
    copies = []
    for dst, off, r in pieces:
        for k in range(NDEV):
            c = pltpu.make_async_copy(wall_ref.at[k, pl.ds(off, r), :], dst.at[pl.ds(k * r, r), :], sem.at[len(copies)])
            c.start()
            copies.append(c)
    for c in copies:
        c.wait()


class _Exchange:
    def __init__(self, inputs, out_shape, scratch, begin, middle, end):
        self.inputs, self.out_shape, self.scratch = inputs, out_shape, scratch
        self.begin, self.middle, self.end = begin, middle, end


def _gather_exchange(shard):
    rows, cols = shard.shape

    def ops(ins, outs, scr):
        (x_ref,), (out_ref,), (send_sems, recv_sems, local_sem) = ins, outs, scr
        x, y, c = lax.axis_index("x"), lax.axis_index("y"), lax.axis_index("c")
        me, sibling = (x, y, c), (x, y, 1 - c)
        chips = [(1 - x, y), (x, 1 - y), (1 - x, 1 - y)]

        def slot(px, py, pc):
            return out_ref.at[4 * px + 2 * py + pc]

        def copy(k, block, to, src=None):
            return pltpu.make_async_remote_copy(
                src_ref=slot(*block) if src is None else src, dst_ref=slot(*block),
                send_sem=send_sems.at[k], recv_sem=recv_sems.at[k], device_id=to, device_id_type=MESH)

        mine = lambda: pltpu.make_async_copy(x_ref, slot(*me), local_sem)
        first = lambda: [copy(0, me, sibling, src=x_ref)] + [
            copy(1 + j, me, (*chip, c), src=x_ref) for j, chip in enumerate(chips)]
        passed = lambda: [copy(4 + j, (*chip, c), sibling) for j, chip in enumerate(chips)]
        return c, me, sibling, chips, copy, mine, first, passed

    def begin(*refs):
        _, _, _, _, _, mine, first, _ = ops(*refs)
        mine().start()
        for cp in first():
            cp.start()

    def middle(*refs):
        c, me, _, chips, copy, _, _, passed = ops(*refs)
        for (j, chip), fwd in zip(enumerate(chips), passed()):
            copy(1 + j, (*chip, c), me).wait_recv()
            fwd.start()

    def end(*refs):
        c, me, sibling, chips, copy, mine, first, passed = ops(*refs)
        copy(0, sibling, me).wait_recv()
        for j, chip in enumerate(chips):
            copy(4 + j, (*chip, 1 - c), me).wait_recv()
        for cp in first() + passed():
            cp.wait_send()
        mine().wait()

    return _Exchange([shard], [jax.ShapeDtypeStruct((NDEV, rows, cols), shard.dtype)],
                     [pltpu.SemaphoreType.DMA((7,)), pltpu.SemaphoreType.DMA((7,)), pltpu.SemaphoreType.DMA],
                     begin, middle, end)


def _grad_exchange(grads, bases, rows, strides=None):
    n = len(grads)
    strides = rows if strides is None else strides

    def copies(g_refs, out_refs, scr):
        send_sems, recv_sems, local_sems = scr
        x, y, c = lax.axis_index("x"), lax.axis_index("y"), lax.axis_index("c")
        me = 4 * x + 2 * y + c
        out = []
        for i in range(n):
            r = rows[i]
            src = lambda k: g_refs[i].at[pl.ds(pl.multiple_of(bases[i] + k * strides[i], 16), r), :]
            out.append(pltpu.make_async_copy(src(me), out_refs[i].at[me], local_sems.at[i]))
            for j in range(1, NDEV):
                px, py, pc = x ^ (j >> 2), y ^ ((j >> 1) & 1), c ^ (j & 1)
                out.append(pltpu.make_async_remote_copy(
                    src_ref=src(4 * px + 2 * py + pc), dst_ref=out_refs[i].at[me],
                    send_sem=send_sems.at[i, j - 1], recv_sem=recv_sems.at[i, j - 1],
                    device_id=(px, py, pc), device_id_type=MESH))
        return out

    def begin(*refs):
        for cp in copies(*refs):
            cp.start()

    def end(*refs):
        for cp in copies(*refs):
            cp.wait()

    return _Exchange(list(grads), [jax.ShapeDtypeStruct((NDEV, r, g.shape[1]), g.dtype) for g, r in zip(grads, rows)],
                     [pltpu.SemaphoreType.DMA((n, NDEV - 1)), pltpu.SemaphoreType.DMA((n, NDEV - 1)),
                      pltpu.SemaphoreType.DMA((n,))], begin, None, end)


def _row_gather_exchange(row):
    def copies(ins, outs, scr):
        (r_ref,), (o_ref,), (send_sems, recv_sems, local_sem) = ins, outs, scr
        x, y, c = lax.axis_index("x"), lax.axis_index("y"), lax.axis_index("c")
        me = 4 * x + 2 * y + c
        out = [pltpu.make_async_copy(r_ref, o_ref.at[me], local_sem)]
        for j in range(1, NDEV):
            out.append(pltpu.make_async_remote_copy(
                src_ref=r_ref, dst_ref=o_ref.at[me], send_sem=send_sems.at[j - 1], recv_sem=recv_sems.at[j - 1],
                device_id=(x ^ (j >> 2), y ^ ((j >> 1) & 1), c ^ (j & 1)), device_id_type=MESH))
        return out

    def begin(*refs):
        for cp in copies(*refs):
            cp.start()

    def end(*refs):
        for cp in copies(*refs):
            cp.wait()

    return _Exchange([row], [jax.ShapeDtypeStruct((NDEV,) + row.shape, row.dtype)],
                     [pltpu.SemaphoreType.DMA((NDEV - 1,)), pltpu.SemaphoreType.DMA((NDEV - 1,)), pltpu.SemaphoreType.DMA],
                     begin, None, end)


def _join(a, b):
    na, oa, sa = len(a.inputs), len(a.out_shape), len(a.scratch)
    split = lambda ins, outs, scr: ((ins[:na], outs[:oa], scr[:sa]), (ins[na:], outs[oa:], scr[sa:]))

    def begin(*refs):
        pa, pb = split(*refs)
        a.begin(*pa)
        b.begin(*pb)

    def end(*refs):
        pa, pb = split(*refs)
        a.end(*pa)
        b.end(*pb)

    return _Exchange(a.inputs + b.inputs, a.out_shape + b.out_shape, a.scratch + b.scratch, begin, None, end)


def _exchange_call(name, ex):
    ni, no = len(ex.inputs), len(ex.out_shape)

    def body(*refs):
        parts = (refs[:ni], refs[ni:ni + no], refs[ni + no:])
        ex.begin(*parts)
        if ex.middle is not None:
            ex.middle(*parts)
        ex.end(*parts)

    return pl.pallas_call(body, name=name, out_shape=ex.out_shape, in_specs=[HBM_SPEC] * ni, out_specs=[HBM_SPEC] * no,
                          scratch_shapes=ex.scratch)(*ex.inputs)


def _call(body, *, name, grid, in_specs, out_specs, out_shape, scratch_shapes, ins, carry=None):
    sem = ("arbitrary",) * len(grid)
    if carry is None:
        outs = pl.pallas_call(body, name=name, grid=grid, in_specs=in_specs, out_specs=out_specs, out_shape=out_shape,
                              scratch_shapes=scratch_shapes, compiler_params=_params(*sem))(*ins)
        return outs, None
    n_in, n_out, n_scr = len(ins), len(out_shape), len(scratch_shapes)
    ci, co = len(carry.inputs), len(carry.out_shape)
    total = 1
    for g in grid:
        total *= g

    def wrapped(*refs):
        own_in, ex_in = refs[:n_in], refs[n_in:n_in + ci]
        o0 = n_in + ci
        own_out, ex_out = refs[o0:o0 + n_out], refs[o0 + n_out:o0 + n_out + co]
        s0 = o0 + n_out + co
        own_scr, ex_scr = refs[s0:s0 + n_scr], refs[s0 + n_scr:]
        step = pl.program_id(0)
        for d in range(1, len(grid)):
            step = step * grid[d] + pl.program_id(d)
        parts = (ex_in, ex_out, ex_scr)
        pl.when(step == 0)(lambda: carry.begin(*parts))
        body(*own_in, *own_out, *own_scr)
        if carry.middle is not None:
            pl.when(step == (3 * total) // 4)(lambda: carry.middle(*parts))
        pl.when(step == total - 1)(lambda: carry.end(*parts))

    outs = pl.pallas_call(
        wrapped, name=name, grid=grid, in_specs=list(in_specs) + [HBM_SPEC] * ci,
        out_specs=list(out_specs) + [HBM_SPEC] * co, out_shape=list(out_shape) + list(carry.out_shape),
        scratch_shapes=list(scratch_shapes) + list(carry.scratch), compiler_params=_params(*sem),
    )(*ins, *carry.inputs)
    return outs[:n_out], outs[n_out:]


def _ffn_fwd(name, xin, affine, wall, offs, tm, loss=None, carry=None):
    T = xin.shape[0]
    nt = T // tm

    def body(*refs):
        it = iter(refs)
        x_ref = next(it)
        if affine is not None:
            g_ref, b_ref = next(it), next(it)
        wall_ref = next(it)
        if loss is not None:
            go_ref, bo_ref, tgt_ref = next(it), next(it), next(it)
        xh_ref, r_ref, a_ref, b2_ref, yb_ref = (next(it) for _ in range(5))
        if loss is not None:
            loss_ref = next(it)
        w1, w3, w2, sem = (next(it) for _ in range(4))

        @pl.when(pl.program_id(0) == 0)
        def _():
            _load_rows(wall_ref, [(w1, offs[0], RF), (w3, offs[1], RF), (w2, offs[2], RF)], sem)
            if loss is not None:
                loss_ref[...] = jnp.zeros_like(loss_ref)

        parts = [slice(p, min(p + PART_ROWS, tm)) for p in range(0, tm, PART_ROWS)]
        ys, ybs = [], []
        for rs in parts:
            y = x_ref[rs, :]
            if affine is not None:
                y = y * g_ref[...] + b_ref[...]
            ys.append(y)
            ybs.append(y.astype(BF16))
            yb_ref[rs, :] = ybs[-1]
        ab = [(_dot_nt(yb, w1[...]).astype(BF16), _dot_nt(yb, w3[...]).astype(BF16)) for yb in ybs]
        us = []
        for rs, (a, b) in zip(parts, ab):
            a_ref[rs, :] = a
            b2_ref[rs, :] = b
            af, bf = a.astype(F32), b.astype(F32)
            us.append((af * _sigmoid(af) * bf).astype(BF16))
        fs = [_dot(u, w2[...]) for u in us]
        for rs, y, f in zip(parts, ys, fs):
            xh, r = _ln_fwd(ALPHA * y + 0.5 * f)
            xh_ref[rs, :] = xh
            r_ref[rs, :] = jnp.broadcast_to(r, (xh.shape[0], 128))
            if loss is not None:
                e = xh * go_ref[...] + bo_ref[...] - tgt_ref[rs, :]
                loss_ref[...] += jnp.sum(e * e) * (0.5 / D)

    row = lambda w: pl.BlockSpec((tm, w), lambda i: (i, 0))
    vec = pl.BlockSpec((1, D), lambda i: (0, 0))
    ins, in_specs = [xin], [row(D)]
    if affine is not None:
        ins += list(affine)
        in_specs += [vec, vec]
    ins.append(wall)
    in_specs.append(HBM_SPEC)
    if loss is not None:
        ins += list(loss)
        in_specs += [vec, vec, row(D)]
    out_shape = [jax.ShapeDtypeStruct((T, D), F32), jax.ShapeDtypeStruct((T, 128), F32),
                 jax.ShapeDtypeStruct((T, DFF), BF16), jax.ShapeDtypeStruct((T, DFF), BF16),
                 jax.ShapeDtypeStruct((T, D), BF16)]
    out_specs = [row(D), row(128), row(DFF), row(DFF), row(D)]
    if loss is not None:
        out_shape.append(jax.ShapeDtypeStruct((8, 128), F32))
        out_specs.append(pl.BlockSpec((8, 128), lambda i: (0, 0)))
    return _call(body, name=name, grid=(nt,), in_specs=in_specs, out_specs=out_specs, out_shape=out_shape,
                 scratch_shapes=[pltpu.VMEM((DFF, D), BF16)] * 3 + [pltpu.SemaphoreType.DMA((3 * NDEV,))],
                 ins=ins, carry=carry)


def _ffn_bwd(name, dy_src, xh, r, g, a, b, wall, offs, tm, with_dx=True, carry=None):
    T = xh.shape[0]
    nt = T // tm
    from_loss = dy_src[0] == "loss"

    def body(*refs):
        it = iter(refs)
        if from_loss:
            bo_ref, tgt_ref = next(it), next(it)
        else:
            dy_ref = next(it)
        xh_ref, r_ref, g_ref, a_ref, b_ref, wall_ref = (next(it) for _ in range(6))
        dyin_ref, dab_ref, u_ref, df_ref, dg_ref, db_ref = (next(it) for _ in range(6))
        if with_dx:
            w1, w3 = next(it), next(it)
        w2, sem = next(it), next(it)

        @pl.when(pl.program_id(0) == 0)
        def _():
            _load_rows(wall_ref, ([(w1, offs[0], RF), (w3, offs[1], RF)] if with_dx else []) + [(w2, offs[2], RF)], sem)
            dg_ref[...] = jnp.zeros_like(dg_ref)
            db_ref[...] = jnp.zeros_like(db_ref)

        parts = [slice(p, min(p + PART_ROWS, tm)) for p in range(0, tm, PART_ROWS)]
        gv = g_ref[...]
        dz, df = [], []
        for rs in parts:
            xhv = xh_ref[rs, :]
            if from_loss:
                dy = (xhv * gv + bo_ref[...] - tgt_ref[rs, :]) * (1.0 / D)
            else:
                dy = dy_ref[rs, :]
            dzp, dgp, dbp = _ln_bwd(dy, xhv, r_ref[rs, :1], gv)
            dg_ref[...] += dgp
            db_ref[...] += dbp
            dz.append(dzp)
            df.append((0.5 * dzp).astype(BF16))
            df_ref[rs, :] = df[-1]
        du = [_dot_nt(d, w2[...]) for d in df]
        da, dbb = [], []
        for rs, dup in zip(parts, du):
            af, bf = a_ref[rs, :].astype(F32), b_ref[rs, :].astype(F32)
            s = _sigmoid(af)
            sl = af * s
            u_ref[rs, :] = (sl * bf).astype(BF16)
            da.append((dup * bf * (s * (1.0 + af * (1.0 - s)))).astype(BF16))
            dbb.append((dup * sl).astype(BF16))
            dab_ref[rs, :DFF] = da[-1]
            dab_ref[rs, DFF:] = dbb[-1]
        for rs, dzp, dap, dbp in zip(parts, dz, da, dbb):
            dyin_ref[rs, :] = ALPHA * dzp + _dot(dap, w1[...]) + _dot(dbp, w3[...]) if with_dx else dzp

    row = lambda w: pl.BlockSpec((tm, w), lambda i: (i, 0))
    vec = pl.BlockSpec((1, D), lambda i: (0, 0))
    if from_loss:
        ins, in_specs = [dy_src[1], dy_src[2]], [vec, row(D)]
    else:
        ins, in_specs = [dy_src[1]], [row(D)]
    ins += [xh, r, g, a, b, wall]
    in_specs += [row(D), row(128), vec, row(DFF), row(DFF), HBM_SPEC]
    return _call(
        body, name=name, grid=(nt,), in_specs=in_specs,
        out_specs=[row(D), row(2 * DFF), row(DFF), row(D), vec, vec],
        out_shape=[jax.ShapeDtypeStruct((T, D), F32), jax.ShapeDtypeStruct((T, 2 * DFF), BF16),
                   jax.ShapeDtypeStruct((T, DFF), BF16), jax.ShapeDtypeStruct((T, D), BF16),
                   jax.ShapeDtypeStruct((1, D), F32), jax.ShapeDtypeStruct((1, D), F32)],
        scratch_shapes=[pltpu.VMEM((DFF, D), BF16)] * (3 if with_dx else 1) + [pltpu.SemaphoreType.DMA((3 * NDEV,))],
        ins=ins, carry=carry)


def _ffn_dx(name, dab, dz, wall, offs, tm, carry=None):
    T = dz.shape[0]

    def body(dab_ref, dz_ref, wall_ref, o_ref, w13, sem):
        @pl.when(pl.program_id(0) == 0)
        def _():
            _load_rows(wall_ref, [(w13.at[pl.ds(0, DFF), :], offs[0], RF), (w13.at[pl.ds(DFF, DFF), :], offs[1], RF)], sem)

        for p in range(0, tm, PART_ROWS):
            rs = slice(p, min(p + PART_ROWS, tm))
            o_ref[rs, :] = ALPHA * dz_ref[rs, :] + _dot(dab_ref[rs, :], w13[...])

    row = lambda w: pl.BlockSpec((tm, w), lambda i: (i, 0))
    (out,), ex = _call(
        body, name=name, grid=(T // tm,), in_specs=[row(2 * DFF), row(D), HBM_SPEC], out_specs=[row(D)],
        out_shape=[jax.ShapeDtypeStruct((T, D), F32)],
        scratch_shapes=[pltpu.VMEM((2 * DFF, D), BF16), pltpu.SemaphoreType.DMA((2 * NDEV,))],
        ins=[dab, dz, wall], carry=carry)
    return out, ex


def _wgrad(name, a, b, bn, tk, col0=0, ncols=None, carry=None):
    T = a.shape[0]
    N = a.shape[1] if ncols is None else ncols
    nk = T // tk
    c0 = col0 // bn

    def body(a_ref, b_ref, o_ref, acc):
        k = pl.program_id(1)

        @pl.when(k == 0)
        def _():
            acc[...] = jnp.zeros_like(acc)

        acc[...] += _dot_tn(a_ref[...], b_ref[...])

        @pl.when(k == nk - 1)
        def _():
            o_ref[...] = acc[...].astype(BF16)

    (out,), ex = _call(
        body, name=name, grid=(N // bn, nk),
        in_specs=[pl.BlockSpec((tk, bn), lambda n, k: (k, n + c0)), pl.BlockSpec((tk, D), lambda n, k: (k, 0))],
        out_specs=[pl.BlockSpec((bn, D), lambda n, k: (n, 0))],
        out_shape=[jax.ShapeDtypeStruct((N, D), BF16)],
        scratch_shapes=[pltpu.VMEM((bn, D), F32)], ins=[a, b], carry=carry)
    return out, ex


def _rope_table(T):
    pos = jnp.arange(T, dtype=F32)
    inv_freq = ROPE_THETA ** (-jnp.arange(0, ROPE_DIM, 2, dtype=F32) / ROPE_DIM)
    ang = pos[:, None] * inv_freq[None, :]
    return jnp.pad(jnp.concatenate([jnp.cos(ang), jnp.sin(ang)], axis=1), ((0, 0), (0, 128 - ROPE_DIM)))


def _rope_expand(cs):
    half = ROPE_DIM // 2
    lane = lax.broadcasted_iota(jnp.int32, cs.shape, 1)
    cos = jnp.where(lane < half, cs, 0.0)
    sin = jnp.where((lane >= half) & (lane < ROPE_DIM), cs, 0.0)
    both = lambda t: t + pltpu.roll(t, HEAD_DIM, 1)
    c = jnp.where(jnp.bitwise_and(lane, HEAD_DIM - 1) < ROPE_DIM, both(cos + pltpu.roll(cos, half, 1)), 1.0)
    return c, -both(pltpu.roll(sin, 128 - half, 1)), both(sin)


def _rope(t, c, s1, s2):
    n = t.shape[1] // 128
    ct, s1t, s2t = (jnp.tile(v, (1, n)) for v in (c, s1, s2))
    w = t.shape[1]
    return t * ct + pltpu.roll(t, w - 8, 1) * s1t + pltpu.roll(t, 8, 1) * s2t


def _rope_t(dr, c, s1, s2):
    n = dr.shape[1] // 128
    ct, s1t, s2t = (jnp.tile(v, (1, n)) for v in (c, s1, s2))
    w = dr.shape[1]
    return dr * ct + pltpu.roll(dr * s1t, 8, 1) + pltpu.roll(dr * s2t, w - 8, 1)


_Q, _K, _V = (0, 1024), (1024, 256), (1280, 256)
_HG = (1536, HG_W)
_GATES = (5632, GATE_W)


def _inproj_fwd(name, xh, g, b, wall, b_in, rope, tm, carry=None):
    T = xh.shape[0]

    def body(xh_ref, g_ref, b_ref, wall_ref, bin_ref, cs_ref,
             qkv_ref, kt_ref, vt_ref, hg_ref, gate_ref, yb_ref, w, sem):
        @pl.when(pl.program_id(0) == 0)
        def _():
            _load_rows(wall_ref, [(w, 0, RIN)], sem)

        yb = (xh_ref[...] * g_ref[...] + b_ref[...]).astype(BF16)
        yb_ref[...] = yb
        c, s1, s2 = _rope_expand(cs_ref[...])

        def piece(start, width):
            return _dot_nt(yb, w[start:start + width, :]) + bin_ref[:, start:start + width]

        k = _rope(piece(*_K), c, s1, s2)
        v = piece(*_V)
        qkv_ref[:, 0:1024] = _rope(piece(*_Q), c, s1, s2).astype(BF16)
        qkv_ref[:, 1024:1280] = k.astype(BF16)
        qkv_ref[:, 1280:1536] = v.astype(BF16)
        kt_ref[...] = k.T.astype(BF16)
        vt_ref[...] = v.T.astype(BF16)
        for j in range(4):
            hg_ref[:, 1024 * j:1024 * (j + 1)] = piece(_HG[0] + 1024 * j, 1024)
        for j in range(2):
            gate_ref[:, 1024 * j:1024 * (j + 1)] = piece(_GATES[0] + 1024 * j, 1024)

    row = lambda wd: pl.BlockSpec((tm, wd), lambda i: (i, 0))
    vec = lambda wd: pl.BlockSpec((1, wd), lambda i: (0, 0))
    colt = pl.BlockSpec((256, tm), lambda i: (0, i))
    return _call(
        body, name=name, grid=(T // tm,),
        in_specs=[row(D), vec(D), vec(D), HBM_SPEC, vec(DIN), row(128)],
        out_specs=[row(QKV_W), colt, colt, row(HG_W), row(GATE_W), row(D)],
        out_shape=[jax.ShapeDtypeStruct((T, QKV_W), BF16), jax.ShapeDtypeStruct((256, T), BF16),
                   jax.ShapeDtypeStruct((256, T), BF16), jax.ShapeDtypeStruct((T, HG_W), F32),
                   jax.ShapeDtypeStruct((T, GATE_W), F32), jax.ShapeDtypeStruct((T, D), BF16)],
        scratch_shapes=[pltpu.VMEM((DIN, D), BF16), pltpu.SemaphoreType.DMA((NDEV,))],
        ins=[xh, g, b, wall, b_in, rope], carry=carry)


def _inproj_bwd(name, dq, dk, dv, dhg, dgates, dz2, wall, rope, tm):
    T = dq.shape[0]

    def body(dq_ref, dk_ref, dv_ref, d0, d1, d2, d3, dga_ref, dgh_ref, dz_ref, wall_ref, cs_ref,
             dy_ref, dproj_ref, dbin_ref, w, sem):
        @pl.when(pl.program_id(0) == 0)
        def _():
            _load_rows(wall_ref, [(w, 0, RIN)], sem)
            dbin_ref[...] = jnp.zeros_like(dbin_ref)

        c, s1, s2 = _rope_expand(cs_ref[...])
        acc = ALPHA * dz_ref[...]
        pieces = [(_Q[0], _rope_t(dq_ref[...], c, s1, s2)), (_K[0], _rope_t(dk_ref[...], c, s1, s2)),
                  (_V[0], dv_ref[...])]
        pieces += [(_HG[0] + 1024 * j, r[...]) for j, r in enumerate((d0, d1, d2, d3))]
        pieces += [(_GATES[0], dga_ref[...]), (_GATES[0] + 1024, dgh_ref[...])]
        for start, val in pieces:
            width = val.shape[1]
            dbin_ref[:, start:start + width] += jnp.sum(val.astype(F32), axis=0, keepdims=True)
            vb = val.astype(BF16)
            dproj_ref[:, start:start + width] = vb
            acc = acc + _dot(vb, w[start:start + width, :])
        dy_ref[...] = acc

    row = lambda wd: pl.BlockSpec((tm, wd), lambda i: (i, 0))
    return pl.pallas_call(
        body, name=name, grid=(T // tm,),
        in_specs=[row(D), row(256), row(256)] + [row(D)] * 4 + [row(D), row(D), row(D), HBM_SPEC, row(128)],
        out_specs=[row(D), row(DIN), pl.BlockSpec((1, DIN), lambda i: (0, 0))],
        out_shape=[jax.ShapeDtypeStruct((T, D), F32), jax.ShapeDtypeStruct((T, DIN), BF16),
                   jax.ShapeDtypeStruct((1, DIN), F32)],
        scratch_shapes=[pltpu.VMEM((DIN, D), BF16), pltpu.SemaphoreType.DMA((NDEV,))],
        compiler_params=_params("arbitrary"),
    )(dq, dk, dv, *dhg, *dgates, dz2, wall, rope)


def _halves(t):
    lane = lax.broadcasted_iota(jnp.int32, t.shape, 1)
    low = lane < HEAD_DIM
    sw = pltpu.roll(t, HEAD_DIM, 1)
    zero = jnp.zeros_like(t)
    h0 = (jnp.where(low, t, zero), jnp.where(low, zero, sw))
    h1 = (jnp.where(low, sw, zero), jnp.where(low, zero, t))
    return h0, h1


def _lane_stack(p_ref, c_ref, gp):
    sl = slice(128 * gp, 128 * (gp + 1))
    hp, hc = _halves(p_ref[:, sl].astype(F32)), _halves(c_ref[:, sl].astype(F32))
    return [jnp.concatenate([hp[gg][0], hc[gg][0], hp[gg][1], hc[gg][1]], axis=0).astype(BF16) for gg in range(2)]


def _row_stack(tp_ref, tc_ref, g):
    band = jnp.concatenate([tp_ref[64 * g:64 * (g + 1), :], tc_ref[64 * g:64 * (g + 1), :]], axis=1)
    z = jnp.zeros_like(band)
    return [jnp.concatenate([band, z], axis=0), jnp.concatenate([z, band], axis=0)]


def _fold(prevm, t4, hh):
    return jnp.where(prevm, t4[256 * hh:256 * hh + 128, :], t4[256 * hh + 128:256 * hh + 256, :])


def _unfold(prevm, t):
    return jnp.concatenate([jnp.where(prevm, t, 0.0), jnp.where(prevm, 0.0, t)], axis=0).astype(BF16)


def _attn_softmax(s, kill, sink):
    s = jnp.where(kill, NEG_INF, s)
    m = jnp.maximum(jnp.max(s, axis=0, keepdims=True), sink)
    p = jnp.exp(s - m)
    es = jnp.exp(sink - m)
    inv = 1.0 / (jnp.sum(p, axis=0, keepdims=True) + es)
    return p * inv, es * inv


def _attn_masks(first):
    row = lax.broadcasted_iota(jnp.int32, (ATTN_BLOCK, 2 * ATTN_BLOCK), 0)
    lane = lax.broadcasted_iota(jnp.int32, (ATTN_BLOCK, 2 * ATTN_BLOCK), 1)
    prevm = row > lane % ATTN_BLOCK
    return prevm, jnp.logical_and(first, prevm)


def _pair_rows(ref, g):
    return jnp.concatenate([ref[:, 256 * g:256 * g + 128], ref[:, 256 * g + 128:256 * (g + 1)]], axis=0)


def _pair_sinks(sink_ref, g, hh):
    h0, h1 = 4 * g + hh, 4 * g + 2 + hh
    return jnp.concatenate([jnp.broadcast_to(sink_ref[:, h0:h0 + 1], (1, ATTN_BLOCK)),
                            jnp.broadcast_to(sink_ref[:, h1:h1 + 1], (1, ATTN_BLOCK))], axis=1)


def _attn_fwd(name, qkv, vt, sinks):
    T = qkv.shape[0]
    nbk = min(ATTN_TILE, T // ATTN_BLOCK)
    rows = nbk * ATTN_BLOCK
    scale = HEAD_DIM ** -0.5

    def block(q_ref, kp_ref, kc_ref, vtp_ref, vtc_ref, sink_ref, o_ref, first):
        prevm, kill = _attn_masks(first)
        kst = _lane_stack(kp_ref, kc_ref, 0) + _lane_stack(kp_ref, kc_ref, 1)
        vts = [_row_stack(vtp_ref, vtc_ref, g) for g in range(N_KV_HEADS)]
        s8 = [_dot_nt(kst[g], _pair_rows(q_ref, g)) * scale for g in range(N_KV_HEADS)]
        pu = [[_unfold(prevm, _attn_softmax(_fold(prevm, s8[g], hh), kill, _pair_sinks(sink_ref, g, hh))[0])
               for hh in range(2)] for g in range(N_KV_HEADS)]
        for g in range(N_KV_HEADS):
            ot = _dot(vts[g][0], pu[g][0]) + _dot(vts[g][1], pu[g][1])
            o_ref[:, 256 * g:256 * g + 128] = ot[:, :ATTN_BLOCK].T.astype(BF16)
            o_ref[:, 256 * g + 128:256 * (g + 1)] = ot[:, ATTN_BLOCK:].T.astype(BF16)

    def body(q_ref, kp_ref, kc_ref, vtp_ref, vtc_ref, sink_ref, o_ref):
        for b in range(nbk):
            rs = slice(ATTN_BLOCK * b, ATTN_BLOCK * (b + 1))
            ps = slice(ATTN_BLOCK * (b - 1), ATTN_BLOCK * b)
            block(q_ref.at[rs, :], kp_ref if b == 0 else kc_ref.at[ps, :], kc_ref.at[rs, :],
                  vtp_ref if b == 0 else vtc_ref.at[:, ps], vtc_ref.at[:, rs], sink_ref, o_ref.at[rs, :],
                  pl.program_id(0) == 0 if b == 0 else False)

    prev = lambda i: jnp.maximum(nbk * i - 1, 0)
    return pl.pallas_call(
        body, name=name, grid=(T // rows,),
        in_specs=[pl.BlockSpec((rows, D), lambda i: (i, 0)),
                  pl.BlockSpec((ATTN_BLOCK, 256), lambda i: (prev(i), 4)),
                  pl.BlockSpec((rows, 256), lambda i: (i, 4)),
                  pl.BlockSpec((256, ATTN_BLOCK), lambda i: (0, prev(i))),
                  pl.BlockSpec((256, rows), lambda i: (0, i)),
                  pl.BlockSpec((1, N_Q_HEADS), lambda i: (0, 0))],
        out_specs=pl.BlockSpec((rows, D), lambda i: (i, 0)),
        out_shape=jax.ShapeDtypeStruct((T, D), BF16),
        compiler_params=_params("arbitrary"),
    )(qkv, qkv, qkv, vt, vt, sinks)


ATTN_TILE = 4
ATTN_BWD_ORDER = ((0, 0), (0, 1), (1, 0), (0, 2), (1, 1), (2, 0), (0, 3), (1, 2), (2, 1), (1, 3), (2, 2), (2, 3))


def _attn_bwd(name, qkv, kt, sinks, dya, carry=None):
    T = qkv.shape[0]
    nbk = min(ATTN_TILE, T // ATTN_BLOCK)
    rows = nbk * ATTN_BLOCK
    scale = HEAD_DIM ** -0.5

    def block(q_ref, kp_ref, kc_ref, vp_ref, vc_ref, ktp_ref, ktc_ref, sink_ref, do_ref, dq_ref, first):
        prevm, kill = _attn_masks(first)
        kst = _lane_stack(kp_ref, kc_ref, 0) + _lane_stack(kp_ref, kc_ref, 1)
        vst = _lane_stack(vp_ref, vc_ref, 0) + _lane_stack(vp_ref, vc_ref, 1)
        kts = [_row_stack(ktp_ref, ktc_ref, g) for g in range(N_KV_HEADS)]
        lane = lax.broadcasted_iota(jnp.int32, (2 * ATTN_BLOCK, 128), 1)
        low = lane < HEAD_DIM
        slane = lax.broadcasted_iota(jnp.int32, (1, 128), 1)
        dkz = [jnp.zeros((2 * ATTN_BLOCK, 128), F32) for _ in range(N_KV_HEADS)]
        dvz = [jnp.zeros((2 * ATTN_BLOCK, 128), F32) for _ in range(N_KV_HEADS)]
        dsink = jnp.zeros((1, 128), F32)
        groups = range(N_KV_HEADS)
        qcat, docat, s8, dp8 = {}, {}, {}, {}
        ds_u, p_u, dsinks = {}, {}, []

        def scores(g):
            qcat[g], docat[g] = _pair_rows(q_ref, g), _pair_rows(do_ref, g)
            s8[g] = _dot_nt(kst[g], qcat[g]) * scale
            dp8[g] = _dot_nt(vst[g], docat[g])

        def algebra(g):
            for hh in range(2):
                pn, ps = _attn_softmax(_fold(prevm, s8[g], hh), kill, _pair_sinks(sink_ref, g, hh))
                dp = _fold(prevm, dp8[g], hh)
                delta = jnp.sum(pn * dp, axis=0, keepdims=True)
                sd = ps * delta
                dsinks.append(jnp.where(slane == 4 * g + hh, -jnp.sum(sd[:, :ATTN_BLOCK]), 0.0)
                              + jnp.where(slane == 4 * g + 2 + hh, -jnp.sum(sd[:, ATTN_BLOCK:]), 0.0))
                ds_u[g, hh] = _unfold(prevm, pn * (dp - delta) * scale)
                p_u[g, hh] = _unfold(prevm, pn)

        def grads(g):
            dqt = _dot(kts[g][0], ds_u[g, 0]) + _dot(kts[g][1], ds_u[g, 1])
            dq_ref[:, 256 * g:256 * g + 128] = dqt[:, :ATTN_BLOCK].T
            dq_ref[:, 256 * g + 128:256 * (g + 1)] = dqt[:, ATTN_BLOCK:].T
            for hh in range(2):
                own = low if hh == 0 else jnp.logical_not(low)
                dk_h = jnp.where(own, _dot(ds_u[g, hh], qcat[g]), 0.0)
                dv_h = jnp.where(own, _dot(p_u[g, hh], docat[g]), 0.0)
                if hh != g % 2:
                    dk_h = pltpu.roll(dk_h, HEAD_DIM, 1)
                    dv_h = pltpu.roll(dv_h, HEAD_DIM, 1)
                dkz[g] = dkz[g] + dk_h
                dvz[g] = dvz[g] + dv_h

        for stage, g in ATTN_BWD_ORDER:
            (scores, algebra, grads)[stage](g)
        for t in dsinks:
            dsink = dsink + t
        pairs = range(N_KV_HEADS // 2)
        return dsink, [dkz[2 * gp] + dkz[2 * gp + 1] for gp in pairs], [dvz[2 * gp] + dvz[2 * gp + 1] for gp in pairs]

    def body(q_ref, kp_ref, kc_ref, vp_ref, vc_ref, ktp_ref, ktc_ref, sink_ref, do_ref,
             dq_ref, dk_ref, dv_ref, ds_ref):
        i = pl.program_id(0)

        @pl.when(i == 0)
        def _():
            ds_ref[...] = jnp.zeros_like(ds_ref)

        res = []
        for b in range(nbk):
            rs = slice(ATTN_BLOCK * b, ATTN_BLOCK * (b + 1))
            ps = slice(ATTN_BLOCK * (b - 1), ATTN_BLOCK * b)
            res.append(block(
                q_ref.at[rs, :], kp_ref if b == 0 else kc_ref.at[ps, :], kc_ref.at[rs, :],
                vp_ref if b == 0 else vc_ref.at[ps, :], vc_ref.at[rs, :],
                ktp_ref if b == 0 else ktc_ref.at[:, ps], ktc_ref.at[:, rs], sink_ref, do_ref.at[rs, :],
                dq_ref.at[rs, :], i == 0 if b == 0 else False))
        dsink = res[0][0]
        for r in res[1:]:
            dsink = dsink + r[0]
        ds_ref[...] += dsink
        for gp in range(N_KV_HEADS // 2):
            cols = slice(128 * gp, 128 * (gp + 1))
            for b in range(nbk):
                cur = pl.ds(pl.multiple_of(i * rows + ATTN_BLOCK * b, ATTN_BLOCK), ATTN_BLOCK)
                dkb, dvb = res[b][1][gp][ATTN_BLOCK:, :], res[b][2][gp][ATTN_BLOCK:, :]
                if b + 1 < nbk:
                    dkb, dvb = dkb + res[b + 1][1][gp][:ATTN_BLOCK, :], dvb + res[b + 1][2][gp][:ATTN_BLOCK, :]
                dk_ref[cur, cols] = dkb
                dv_ref[cur, cols] = dvb

            @pl.when(i > 0)
            def _():
                prv = pl.ds(pl.multiple_of(jnp.maximum(i * rows - ATTN_BLOCK, 0), ATTN_BLOCK), ATTN_BLOCK)
                dk_ref[prv, cols] += res[0][1][gp][:ATTN_BLOCK, :]
                dv_ref[prv, cols] += res[0][2][gp][:ATTN_BLOCK, :]

    prev = lambda i: jnp.maximum(nbk * i - 1, 0)
    whole = lambda w: pl.BlockSpec((T, w), lambda i: (0, 0))
    return _call(
        body, name=name, grid=(T // rows,), scratch_shapes=[], ins=[qkv, qkv, qkv, qkv, qkv, kt, kt, sinks, dya],
        carry=carry,
        in_specs=[pl.BlockSpec((rows, D), lambda i: (i, 0)),
                  pl.BlockSpec((ATTN_BLOCK, 256), lambda i: (prev(i), 4)),
                  pl.BlockSpec((rows, 256), lambda i: (i, 4)),
                  pl.BlockSpec((ATTN_BLOCK, 256), lambda i: (prev(i), 5)),
                  pl.BlockSpec((rows, 256), lambda i: (i, 5)),
                  pl.BlockSpec((256, ATTN_BLOCK), lambda i: (0, prev(i))),
                  pl.BlockSpec((256, rows), lambda i: (0, i)),
                  pl.BlockSpec((1, N_Q_HEADS), lambda i: (0, 0)),
                  pl.BlockSpec((rows, D), lambda i: (i, 0))],
        out_specs=[pl.BlockSpec((rows, D), lambda i: (i, 0)), whole(256), whole(256),
                   pl.BlockSpec((1, 128), lambda i: (0, 0))],
        out_shape=[jax.ShapeDtypeStruct((T, D), F32), jax.ShapeDtypeStruct((T, 256), F32),
                   jax.ShapeDtypeStruct((T, 256), F32), jax.ShapeDtypeStruct((1, 128), F32)])


HG_FWD = (256, 8, 512)
HG_BWD = (128, 8, 512)


def _chunk_sum(tri, x):
    w = x.shape[1]
    h1 = x.astype(BF16)
    h2 = (x - h1.astype(F32)).astype(BF16)
    r = _dot(tri, jnp.concatenate([h1, h2], axis=1))
    return r[:, :w] + r[:, w:]


def _chunk_tri(n, upper=False):
    ri = lax.broadcasted_iota(jnp.int32, (n, n), 0)
    ci = lax.broadcasted_iota(jnp.int32, (n, n), 1)
    same_chunk = jnp.bitwise_xor(ri, ci) < HGRN_CHUNK
    return jnp.where(((ri <= ci) if upper else (ri >= ci)) & same_chunk, 1.0, 0.0).astype(BF16)


def _chunks(t):
    return [t[HGRN_CHUNK * c:HGRN_CHUNK * (c + 1), :] for c in range(t.shape[0] // HGRN_CHUNK)]


def _lower_bound(lbl_ref):
    l0, l1 = lbl_ref[0:1, :], lbl_ref[1:2, :]
    m = jnp.maximum(l0, l1)
    e0, e1 = jnp.exp(l0 - m), jnp.exp(l1 - m)
    return e0 / (e0 + e1)


def _heads(t):
    return [t[:, 128 * h:128 * (h + 1)] for h in range(t.shape[1] // 128)]


def _per_head(fn, *wide):
    return jnp.concatenate([fn(*parts) for parts in zip(*[_heads(t) for t in wide])], axis=1)


def _hgrn_sub(fl, qh, vv, lb, tril_b):
    w = fl.shape[1]
    sg = _sigmoid(fl)
    f = lb + (1.0 - lb) * sg
    k = 1.0 - f
    gc = _chunk_sum(tril_b, jnp.log(f))
    gl_c = [t[HGRN_CHUNK - 1:HGRN_CHUNK, :] for t in _chunks(gc)]
    gl = jnp.concatenate([jnp.broadcast_to(g, (HGRN_CHUNK, w)) for g in gl_c], axis=0)
    sq = _sigmoid(qh)
    eg = jnp.exp(gc)
    eng = jnp.exp(-gc)
    elg = jnp.exp(gl - gc)
    qd = qh * sq * eg
    ki = k * eng
    ke = k * elg
    qd_b, ki_b, ke_b, v_b = (t.astype(BF16) for t in (qd, ki, ke, vv))
    am_b = [jnp.where(tril_b > 0, _dot_nt(qq, kk), 0.0).astype(BF16) for qq, kk in zip(_heads(qd_b), _heads(ki_b))]
    return dict(sg=sg, f=f, sq=sq, eg=eg, eng=eng, elg=elg, qd=qd, ki=ki, ke=ke, egl=[jnp.exp(g) for g in gl_c],
                am_b=am_b, qd_b=qd_b, ki_b=ki_b, ke_b=ke_b, v_b=v_b)


def _hgrn_out(q, st_b):
    outs = []
    for h, (qd_h, v_h) in enumerate(zip(_heads(q["qd_b"]), _heads(q["v_b"]))):
        inter = [_dot_nt(qc, s) for qc, s in zip(_chunks(qd_h), st_b[h])]
        outs.append(_dot(q["am_b"][h], v_h) + jnp.concatenate(inter, axis=0))
    return jnp.concatenate(outs, axis=1)


def _head_rms(o):
    return _per_head(lambda t: jnp.broadcast_to(
        lax.rsqrt(jnp.mean(t * t, axis=-1, keepdims=True) + RMS_EPS), t.shape), o)


def _hgrn_fwd(name, hg, lb_logits, norm_g, cfg):
    T = hg.shape[0]
    sub_rows, hps, th = min(cfg[0], T), cfg[1], min(cfg[2], T)
    cps = sub_rows // HGRN_CHUNK
    nc = th // HGRN_CHUNK

    def body(fl_ref, qh_ref, v_ref, og_ref, lbl_ref, ng_ref, y_ref, ss_ref, st):
        @pl.when(pl.program_id(1) == 0)
        def _():
            st[...] = jnp.zeros_like(st)

        lbs = _lower_bound(lbl_ref)
        ng = jnp.tile(ng_ref[...], (1, hps))
        tril_b = _chunk_tri(sub_rows)

        def sub(si, carry):
            rows = pl.ds(pl.multiple_of(si * sub_rows, sub_rows), sub_rows)
            q = _hgrn_sub(fl_ref[rows, :], qh_ref[rows, :], v_ref[rows, :], lbs, tril_b)
            v_hc = [_chunks(t) for t in _heads(q["v_b"])]
            k_hc = [_chunks(t) for t in _heads(q["ke_b"])]
            s = [st[h] for h in range(hps)]
            st_b = [[] for _ in range(hps)]
            for c in range(cps):
                egl = _heads(q["egl"][c])
                for h in range(hps):
                    ss_ref[h, si * cps + c] = s[h]
                    st_b[h].append(s[h].astype(BF16))
                    s[h] = s[h] * egl[h] + _dot_tn(v_hc[h][c], k_hc[h][c])
            for h in range(hps):
                st[h] = s[h]
            o = _hgrn_out(q, st_b)
            og = og_ref[rows, :]
            y_ref[rows, :] = (o * _head_rms(o) * ng * (og * _sigmoid(og))).astype(BF16)
            return carry

        lax.fori_loop(0, th // sub_rows, sub, 0)

    wd = 128 * hps
    col = lambda j: pl.BlockSpec((th, wd), lambda h, t: (t, (8 // hps) * j + h))
    return pl.pallas_call(
        body, name=name, grid=(HGRN_HEADS // hps, T // th),
        in_specs=[col(0), col(1), col(2), col(3), pl.BlockSpec((2, wd), lambda h, t: (0, h)),
                  pl.BlockSpec((1, 128), lambda h, t: (0, 0))],
        out_specs=[pl.BlockSpec((th, wd), lambda h, t: (t, h)),
                   pl.BlockSpec((hps, nc, 128, 128), lambda h, t: (h, t, 0, 0))],
        out_shape=[jax.ShapeDtypeStruct((T, D), BF16),
                   jax.ShapeDtypeStruct((HGRN_HEADS, T // HGRN_CHUNK, 128, 128), F32)],
        scratch_shapes=[pltpu.VMEM((hps, 128, 128), F32)],
        compiler_params=_params("parallel", "arbitrary"),
    )(hg, hg, hg, hg, lb_logits, norm_g)


def _hgrn_bwd(name, hg, lb_logits, norm_g, sstart, dyh, cfg, carry=None):
    T = hg.shape[0]
    sub_rows, hps, th = min(cfg[0], T), cfg[1], min(cfg[2], T)
    cps = sub_rows // HGRN_CHUNK
    nc = th // HGRN_CHUNK
    nt = T // th
    wd = 128 * hps

    def body(fl_ref, qh_ref, v_ref, og_ref, lbl_ref, ng_ref, ss_ref, dy_ref,
             dfl_ref, dqh_ref, dv_ref, dog_ref, dlb_ref, dng_ref, dst):
        @pl.when(pl.program_id(1) == 0)
        def _():
            dst[...] = jnp.zeros_like(dst)
            dlb_ref[...] = jnp.zeros_like(dlb_ref)
            dng_ref[...] = jnp.zeros_like(dng_ref)

        lb = _lower_bound(lbl_ref)
        ng = jnp.tile(ng_ref[...], (1, hps))
        last = lax.broadcasted_iota(jnp.int32, (HGRN_CHUNK, wd), 0) == HGRN_CHUNK - 1
        nsub = th // sub_rows
        cat0 = lambda parts: jnp.concatenate(parts, axis=0)
        tril_b, triu_b = _chunk_tri(sub_rows), _chunk_tri(sub_rows, upper=True)

        def sub(step, carry):
            si = nsub - 1 - step
            rows = pl.ds(pl.multiple_of(si * sub_rows, sub_rows), sub_rows)
            qh, og, dy = qh_ref[rows, :], og_ref[rows, :], dy_ref[rows, :]
            q = _hgrn_sub(fl_ref[rows, :], qh, v_ref[rows, :], lb, tril_b)
            s_in = [[ss_ref[h, si * cps + c] for c in range(cps)] for h in range(hps)]
            st_b = [[s.astype(BF16) for s in row] for row in s_in]
            o = _hgrn_out(q, st_b)
            rr = _head_rms(o)
            oh = o * rr
            sog = _sigmoid(og)
            dog_ref[rows, :] = (dy * oh * ng * (sog * (1.0 + og * (1.0 - sog)))).astype(BF16)
            don = dy * (og * sog)
            dng_w = jnp.sum(don * oh, axis=0, keepdims=True)
            dd = don * ng
            mean_h = _per_head(lambda t: jnp.broadcast_to(jnp.mean(t, axis=-1, keepdims=True), t.shape), dd * oh)
            do_b = (rr * (dd - oh * mean_h)).astype(BF16)
            do_h, qd_h, ki_h, ke_h, v_h = (_heads(t) for t in (do_b, q["qd_b"], q["ki_b"], q["ke_b"], q["v_b"]))
            da_b = [jnp.where(tril_b > 0, _dot_nt(do_h[h], v_h[h]), 0.0).astype(BF16) for h in range(hps)]
            do_c, qd_c, ke_c, v_c = ([_chunks(t) for t in hs] for hs in (do_h, qd_h, ke_h, v_h))
            dsp = [[None] * cps for _ in range(hps)]
            dgl_dec = [[None] * cps for _ in range(hps)]
            d = [dst[h] for h in range(hps)]
            for c in reversed(range(cps)):
                egl = _heads(q["egl"][c])
                for h in range(hps):
                    dsp[h][c] = d[h]
                    dgl_dec[h][c] = jnp.sum(d[h] * s_in[h][c], axis=0, keepdims=True) * egl[h]
                    d[h] = d[h] * egl[h] + _dot_tn(do_c[h][c], qd_c[h][c])
            for h in range(hps):
                dst[h] = d[h]
                dng_ref[h] += dng_w[:, 128 * h:128 * (h + 1)]
            dsp_b = [[t.astype(BF16) for t in row] for row in dsp]
            dv_ref[rows, :] = jnp.concatenate(
                [_dot_tn(q["am_b"][h], do_h[h]) + cat0([_dot_nt(ke_c[h][c], dsp_b[h][c]) for c in range(cps)])
                 for h in range(hps)], axis=1).astype(BF16)
            dqd = jnp.concatenate(
                [_dot(da_b[h], ki_h[h]) + cat0([_dot(do_c[h][c], st_b[h][c]) for c in range(cps)])
                 for h in range(hps)], axis=1)
            dki = jnp.concatenate([_dot_tn(da_b[h], qd_h[h]) for h in range(hps)], axis=1)
            dke = jnp.concatenate([cat0([_dot(v_c[h][c], dsp_b[h][c]) for c in range(cps)]) for h in range(hps)], axis=1)
            dkk = dke * q["ke"]
            dgl = [jnp.sum(t, axis=0, keepdims=True) + jnp.concatenate([dgl_dec[h][c] for h in range(hps)], axis=1)
                   for c, t in enumerate(_chunks(dkk))]
            dgc = dqd * q["qd"] - dki * q["ki"] - dkk + cat0([jnp.where(last, g, 0.0) for g in dgl])
            dlf = _chunk_sum(triu_b, dgc)
            df = dlf / q["f"] - (dki * q["eng"] + dke * q["elg"])
            sg = q["sg"]
            dlb_ref[...] += jnp.sum(df * (1.0 - sg), axis=0, keepdims=True)
            dfl_ref[rows, :] = (df * (1.0 - lb) * sg * (1.0 - sg)).astype(BF16)
            sq = q["sq"]
            dqh_ref[rows, :] = (dqd * q["eg"] * (sq * (1.0 + qh * (1.0 - sq)))).astype(BF16)
            return carry

        lax.fori_loop(0, nsub, sub, 0)
    col = lambda j: pl.BlockSpec((th, wd), lambda h, t: (nt - 1 - t, (8 // hps) * j + h))
    out = pl.BlockSpec((th, wd), lambda h, t: (nt - 1 - t, h))
    return _call(
        body, name=name, grid=(HGRN_HEADS // hps, nt),
        in_specs=[col(0), col(1), col(2), col(3), pl.BlockSpec((2, wd), lambda h, t: (0, h)),
                  pl.BlockSpec((1, 128), lambda h, t: (0, 0)),
                  pl.BlockSpec((hps, nc, 128, 128), lambda h, t: (h, nt - 1 - t, 0, 0)), out],
        out_specs=[out, out, out, out, pl.BlockSpec((1, wd), lambda h, t: (0, h)),
                   pl.BlockSpec((hps, 1, 128), lambda h, t: (h, 0, 0))],
        out_shape=[jax.ShapeDtypeStruct((T, D), BF16)] * 4 + [jax.ShapeDtypeStruct((1, D), F32),
                                                              jax.ShapeDtypeStruct((HGRN_HEADS, 1, 128), F32)],
        scratch_shapes=[pltpu.VMEM((hps, 128, 128), F32)],
        ins=[hg, hg, hg, hg, lb_logits, norm_g, sstart, dyh], carry=carry)


def _mix_fwd(name, ya, yh, gates, xh1, g1, b1, wall, tm):
    T = ya.shape[0]

    def body(ya_ref, yh_ref, ga_ref, gh_ref, xh_ref, g_ref, b_ref, wall_ref,
             xo_ref, r_ref, pa_ref, ph_ref, mg_ref, wpa, wph, wo, sem):
        @pl.when(pl.program_id(0) == 0)
        def _():
            _load_rows(wall_ref, [(wpa, RIN, RP), (wph, RIN + RP, RP), (wo, RIN + 2 * RP, RP)], sem)

        parts = [slice(p, min(p + PART_ROWS, tm)) for p in range(0, tm, PART_ROWS)]
        pas = [_dot(ya_ref[rs, :], wpa[...]).astype(BF16) for rs in parts]
        phs = [_dot(yh_ref[rs, :], wph[...]).astype(BF16) for rs in parts]
        mgs = []
        for rs, pa, ph in zip(parts, pas, phs):
            pa_ref[rs, :] = pa
            ph_ref[rs, :] = ph
            mgs.append((_sigmoid(ga_ref[rs, :]) * pa.astype(F32) + _sigmoid(gh_ref[rs, :]) * ph.astype(F32)).astype(BF16))
            mg_ref[rs, :] = mgs[-1]
        mixes = [_dot(mg, wo[...]) for mg in mgs]
        for rs, mix in zip(parts, mixes):
            y1 = xh_ref[rs, :] * g_ref[...] + b_ref[...]
            xh2, r = _ln_fwd(ALPHA * y1 + mix)
            xo_ref[rs, :] = xh2
            r_ref[rs, :] = jnp.broadcast_to(r, (xh2.shape[0], 128))

    row = lambda w: pl.BlockSpec((tm, w), lambda i: (i, 0))
    vec = pl.BlockSpec((1, D), lambda i: (0, 0))
    return pl.pallas_call(
        body, name=name, grid=(T // tm,),
        in_specs=[row(D), row(D), pl.BlockSpec((tm, D), lambda i: (i, 0)), pl.BlockSpec((tm, D), lambda i: (i, 1)),
                  row(D), vec, vec, HBM_SPEC],
        out_specs=[row(D), row(128), row(D), row(D), row(D)],
        out_shape=[jax.ShapeDtypeStruct((T, D), F32), jax.ShapeDtypeStruct((T, 128), F32)]
        + [jax.ShapeDtypeStruct((T, D), BF16)] * 3,
        scratch_shapes=[pltpu.VMEM((D, D), BF16)] * 3 + [pltpu.SemaphoreType.DMA((3 * NDEV,))],
        compiler_params=_params("arbitrary"),
    )(ya, yh, gates, gates, xh1, g1, b1, wall)


def _mix_bwd(name, dy2, xh2, r2, g2, gates, pa, ph, wall, tm, carry=None):
    T = dy2.shape[0]

    def body(dy_ref, xh_ref, r_ref, g_ref, ga_ref, gh_ref, pa_ref, ph_ref, wall_ref,
             dz_ref, dmix_ref, dpa_ref, dph_ref, dga_ref, dgh_ref, dya_ref, dyh_ref, dg_ref, db_ref,
             wpa, wph, wo, sem):
        @pl.when(pl.program_id(0) == 0)
        def _():
            _load_rows(wall_ref, [(wpa, RIN, RP), (wph, RIN + RP, RP), (wo, RIN + 2 * RP, RP)], sem)
            dg_ref[...] = jnp.zeros_like(dg_ref)
            db_ref[...] = jnp.zeros_like(db_ref)

        parts = [slice(p, min(p + PART_ROWS, tm)) for p in range(0, tm, PART_ROWS)]
        dmix = []
        for rs in parts:
            dz, dgp, dbp = _ln_bwd(dy_ref[rs, :], xh_ref[rs, :], r_ref[rs, :1], g_ref[...])
            dg_ref[...] += dgp
            db_ref[...] += dbp
            dz_ref[rs, :] = dz
            dmix.append(dz.astype(BF16))
            dmix_ref[rs, :] = dmix[-1]
        dmgs = [_dot_nt(d, wo[...]) for d in dmix]
        dpas, dphs = [], []
        for rs, dmg in zip(parts, dmgs):
            sa, sh = _sigmoid(ga_ref[rs, :]), _sigmoid(gh_ref[rs, :])
            dga_ref[rs, :] = (dmg * pa_ref[rs, :].astype(F32) * sa * (1.0 - sa)).astype(BF16)
            dgh_ref[rs, :] = (dmg * ph_ref[rs, :].astype(F32) * sh * (1.0 - sh)).astype(BF16)
            dpas.append((dmg * sa).astype(BF16))
            dphs.append((dmg * sh).astype(BF16))
            dpa_ref[rs, :] = dpas[-1]
            dph_ref[rs, :] = dphs[-1]
        for rs, dpa, dph in zip(parts, dpas, dphs):
            dya_ref[rs, :] = _dot_nt(dpa, wpa[...]).astype(BF16)
            dyh_ref[rs, :] = _dot_nt(dph, wph[...])

    row = lambda w: pl.BlockSpec((tm, w), lambda i: (i, 0))
    vec = pl.BlockSpec((1, D), lambda i: (0, 0))
    return _call(
        body, name=name, grid=(T // tm,),
        in_specs=[row(D), row(D), row(128), vec, pl.BlockSpec((tm, D), lambda i: (i, 0)),
                  pl.BlockSpec((tm, D), lambda i: (i, 1)), row(D), row(D), HBM_SPEC],
        out_specs=[row(D)] * 8 + [vec, vec],
        out_shape=[jax.ShapeDtypeStruct((T, D), F32)] + [jax.ShapeDtypeStruct((T, D), BF16)] * 6
        + [jax.ShapeDtypeStruct((T, D), F32)] + [jax.ShapeDtypeStruct((1, D), F32)] * 2,
        scratch_shapes=[pltpu.VMEM((D, D), BF16)] * 3 + [pltpu.SemaphoreType.DMA((3 * NDEV,))],
        ins=[dy2, xh2, r2, g2, gates, gates, pa, ph, wall], carry=carry)


def _adam(w, g, m, v):
    m = ADAM_B1 * m + (1.0 - ADAM_B1) * g
    v = ADAM_B2 * v + (1.0 - ADAM_B2) * (g * g)
    m_hat = m / (1.0 - ADAM_B1 ** ADAM_STEP)
    v_hat = v / (1.0 - ADAM_B2 ** ADAM_STEP)
    delta = -ADAM_LR * (m_hat / (jnp.sqrt(v_hat) + ADAM_EPS) + ADAM_WD * w)
    return delta, m, v


def _grad_steps(name, items, cb, carry=None):
    n = len(items)
    pieces = [len(it[0]) for it in items]

    def body(*refs):
        it = iter(refs[:sum(pieces) + 3 * n])
        outs = refs[sum(pieces) + 3 * n:]
        for i in range(n):
            accs = []
            for _ in range(pieces[i]):
                r_ref = next(it)
                acc = r_ref[0].astype(F32)
                for k in range(1, NDEV):
                    acc = acc + r_ref[k].astype(F32)
                accs.append(acc)
            acc = accs[0] if len(accs) == 1 else jnp.concatenate(accs, axis=0)
            w_ref, m_ref, v_ref = next(it), next(it), next(it)
            g_ref, d_ref, mo_ref, vo_ref = outs[4 * i:4 * i + 4]
            g_ref[...] = acc
            d, mm, vv = _adam(w_ref[...], acc, m_ref[...], v_ref[...])
            d_ref[...] = d
            mo_ref[...] = mm
            vo_ref[...] = vv

    in_specs, out_specs, out_shape, ins = [], [], [], []
    for recv, w, m, v in items:
        blk = pl.BlockSpec((w.shape[0], cb), lambda j: (0, j))
        in_specs += [pl.BlockSpec((NDEV, rp.shape[1], cb), lambda j: (0, 0, j)) for rp in recv] + [blk, blk, blk]
        out_specs += [blk] * 4
        out_shape += [jax.ShapeDtypeStruct(w.shape, F32)] * 4
        ins += list(recv) + [w, m, v]
    outs, ex = _call(body, name=name, grid=(D // cb,), in_specs=in_specs, out_specs=out_specs, out_shape=out_shape,
                     scratch_shapes=[], ins=ins, carry=carry)
    return [tuple(outs[4 * i:4 * i + 4]) for i in range(n)], ex


_SMALL = [("ln1_g", D), ("ln1_b", D), ("ln2_g", D), ("ln2_b", D), ("ln3_g", D), ("ln3_b", D),
          ("b_in", DIN), ("lb", D), ("attn_sinks", 128), ("hgrn_norm_g", 128), ("loss", 128)]
_SMALL_OFF = {}
_o = 0
for _n, _w in _SMALL:
    _SMALL_OFF[_n] = (_o, _w)
    _o += _w
PACK = _o


def _small_step(name, slots, small_w, small_m, small_v):
    names = ["ln1_g", "ln1_b", "ln2_g", "ln2_b", "ln3_g", "ln3_b", "b_in", "attn_sinks", "hgrn_lb_logits", "hgrn_norm_g"]
    np_ = len(names)

    def body(*refs):
        s_ref = refs[0]
        w_refs = refs[1:1 + np_]
        m_refs = refs[1 + np_:1 + 2 * np_]
        v_refs = refs[1 + 2 * np_:1 + 3 * np_]
        loss_ref, outs = refs[1 + 3 * np_], refs[2 + 3 * np_:]
        tot = s_ref[0]
        for k in range(1, NDEV):
            tot = tot + s_ref[k]

        def part(n):
            o, w = _SMALL_OFF[n]
            return tot[:, o:o + w]

        loss_ref[...] = part("loss")
        for i, n in enumerate(names):
            w = w_refs[i][...]
            if n == "hgrn_lb_logits":
                m_ = jnp.maximum(w[0:1, :], w[1:2, :])
                e0, e1 = jnp.exp(w[0:1, :] - m_), jnp.exp(w[1:2, :] - m_)
                p0 = e0 / (e0 + e1)
                t = p0 * (1.0 - p0) * part("lb")
                g = jnp.concatenate([t, -t], axis=0)
            elif n == "attn_sinks":
                g = part(n)[:, :N_Q_HEADS]
            else:
                g = part(n)
            d, mm, vv = _adam(w, g, m_refs[i][...], v_refs[i][...])
            outs[4 * i][...] = g
            outs[4 * i + 1][...] = d
            outs[4 * i + 2][...] = mm
            outs[4 * i + 3][...] = vv

    out_shape = [jax.ShapeDtypeStruct((1, 128), F32)]
    for n in names:
        out_shape += [jax.ShapeDtypeStruct(small_w[n].shape, F32)] * 4
    return pl.pallas_call(
        body, name=name, out_shape=out_shape,
        compiler_params=pltpu.CompilerParams(vmem_limit_bytes=VMEM_LIMIT),
    )(slots, *[small_w[n] for n in names], *[small_m[n] for n in names], *[small_v[n] for n in names]), names


def _tile(T, pref):
    return min(T, pref)


def kernel(x, ln1_g, ln1_b, ffn1_w1, ffn1_w3, ffn1_w2, ln2_g, ln2_b, w_in, b_in, attn_sinks, hgrn_lb_logits, hgrn_norm_g, w_proj_attn, w_proj_hgrn, w_out, ln3_g, ln3_b, ffn2_w1, ffn2_w3, ffn2_w2, loss_target, m_ln1_g, m_ln1_b, m_ffn1_w1, m_ffn1_w3, m_ffn1_w2, m_ln2_g, m_ln2_b, m_w_in, m_b_in, m_attn_sinks, m_hgrn_lb_logits, m_hgrn_norm_g, m_w_proj_attn, m_w_proj_hgrn, m_w_out, m_ln3_g, m_ln3_b, m_ffn2_w1, m_ffn2_w3, m_ffn2_w2, v_ln1_g, v_ln1_b, v_ffn1_w1, v_ffn1_w3, v_ffn1_w2, v_ln2_g, v_ln2_b, v_w_in, v_b_in, v_attn_sinks, v_hgrn_lb_logits, v_hgrn_norm_g, v_w_proj_attn, v_w_proj_hgrn, v_w_out, v_ln3_g, v_ln3_b, v_ffn2_w1, v_ffn2_w3, v_ffn2_w2):
    T = x.shape[1]
    xs = x[0]
    tgt = loss_target[0]
    tm = _tile(T, 256)
    tm2 = _tile(T, 512)
    tk = _tile(T, 2048)

    t_bf = lambda w: w[0].T.astype(BF16)
    n_bf = lambda w: w[0].astype(BF16)
    ffn_shard = lambda w1, w3, w2: jnp.concatenate([t_bf(w1), t_bf(w3), n_bf(w2)], axis=0)
    mix_shard = jnp.concatenate([t_bf(w_in), n_bf(w_proj_attn), n_bf(w_proj_hgrn), n_bf(w_out)], axis=0)
    (ffn1_all,) = _exchange_call("gather_ffn1", _gather_exchange(ffn_shard(ffn1_w1, ffn1_w3, ffn1_w2)))
    ffn_offs = (0, RF, 2 * RF)
    rope = _rope_table(T)

    (xh1, r1, a1, b1, xb0), (mix_all,) = _ffn_fwd("ffn1_fwd", xs, None, ffn1_all, ffn_offs, tm2,
                                                  carry=_gather_exchange(mix_shard))
    (qkv, kt, vt, hg, gates, y1b), (ffn2_all,) = _inproj_fwd(
        "inproj_fwd", xh1, ln1_g, ln1_b, mix_all, b_in, rope, tm2,
        carry=_gather_exchange(ffn_shard(ffn2_w1, ffn2_w3, ffn2_w2)))
    ya = _attn_fwd("attn_fwd", qkv, vt, attn_sinks)
    yh, sstart = _hgrn_fwd("hgrn_fwd", hg, hgrn_lb_logits, hgrn_norm_g, HG_FWD)
    xh2, r2, pa, ph, merged = _mix_fwd("mix_fwd", ya, yh, gates, xh1, ln1_g, ln1_b, mix_all, tm2)
    (xh3, r3, a2, b2, y2b, loss_part), _ = _ffn_fwd("ffn2_fwd", xh2, (ln2_g, ln2_b), ffn2_all, ffn_offs, tm2,
                                                    loss=(ln3_g, ln3_b, tgt))

    (dy2, dab2, u2, df2, dg3, db3), _ = _ffn_bwd("ffn2_bwd", ("loss", ln3_b, tgt), xh3, r3, ln3_g, a2, b2, ffn2_all,
                                                 ffn_offs, tm)
    recv = {}
    g_ffn2_13, _ = _wgrad("wgrad_ffn2_w13", dab2, y2b, DFF // 2, tk)
    g_ffn2_2, _ = _wgrad("wgrad_ffn2_w2", u2, df2, DFF // 2, tk)
    (dz2, dmix, dpa, dph, dga, dgh, dya, dyh, dg2, db2), (recv["ffn2_w1"],) = _mix_bwd(
        "mix_bwd", dy2, xh2, r2, ln2_g, gates, pa, ph, mix_all, tm2, carry=_grad_exchange([g_ffn2_13], [0], [RF]))
    g_wo, _ = _wgrad("wgrad_w_out", merged, dmix, D, tk)
    g_pa, _ = _wgrad("wgrad_w_pa", ya, dpa, D, tk)
    g_ph, _ = _wgrad("wgrad_w_ph", yh, dph, D, tk)
    (dq, dk, dv, dsink), (recv["ffn2_w3"],) = _attn_bwd(
        "attn_bwd", qkv, kt, attn_sinks, dya, carry=_grad_exchange([g_ffn2_13], [DFF], [RF]))
    (dfl, dqh, dih, dog, dlb, dng), (recv["ffn2_w2"], recv["w_proj_attn"], recv["w_proj_hgrn"], recv["w_out"]) = \
        _hgrn_bwd("hgrn_bwd", hg, hgrn_lb_logits, hgrn_norm_g, sstart, dyh, HG_BWD,
                  carry=_grad_exchange([g_ffn2_2, g_pa, g_ph, g_wo], [0, 0, 0, 0], [RF, RP, RP, RP]))
    dy1, dproj, dbin = _inproj_bwd("inproj_bwd", dq, dk, dv, (dfl, dqh, dih, dog), (dga, dgh), dz2, mix_all, rope, tm2)
    g_win, _ = _wgrad("wgrad_w_in", dproj, y1b, DIN // 4, tk)
    win_rows = (400, 288, 272)
    win_base = (0, 400, 688)
    (dz1, dab1, u1, df1, dg1, db1), (rw0,) = _ffn_bwd(
        "ffn1_bwd", ("dy", dy1), xh1, r1, ln1_g, a1, b1, ffn1_all, ffn_offs, tm2, with_dx=False,
        carry=_grad_exchange([g_win], win_base[:1], win_rows[:1], [RIN]))
    g_ffn1_2, (rw1,) = _wgrad("wgrad_ffn1_w2", u1, df1, DFF // 2, tk,
                              carry=_grad_exchange([g_win], win_base[1:2], win_rows[1:2], [RIN]))
    g_ffn1_1, (recv["ffn1_w2"],) = _wgrad("wgrad_ffn1_w1", dab1, xb0, DFF // 2, tk, 0, DFF,
                                          carry=_grad_exchange([g_ffn1_2], [0], [RF]))
    g_ffn1_3, (recv["ffn1_w1"],) = _wgrad("wgrad_ffn1_w3", dab1, xb0, DFF // 2, tk, DFF, DFF,
                                          carry=_grad_exchange([g_ffn1_1], [0], [RF]))
    parts = {"ln1_g": dg1, "ln1_b": db1, "ln2_g": dg2, "ln2_b": db2, "ln3_g": dg3, "ln3_b": db3, "b_in": dbin,
             "lb": dlb, "attn_sinks": dsink, "hgrn_norm_g": jnp.sum(dng, axis=0), "loss": loss_part[0:1, :]}
    packed = jnp.concatenate([parts[n] for n, _ in _SMALL], axis=1)
    gx, (recv["ffn1_w3"], rw2, small_slots) = _ffn_dx(
        "ffn1_dx", dab1, dz1, ffn1_all, ffn_offs, tm2,
        carry=_join(_grad_exchange([g_ffn1_3, g_win], [0, win_base[2]], [RF, win_rows[2]], [RF, RIN]),
                    _row_gather_exchange(packed)))
    recv["w_in"] = [rw0, rw1, rw2]

    big = [("ffn1_w1", ffn1_w1, m_ffn1_w1, v_ffn1_w1, True), ("ffn1_w3", ffn1_w3, m_ffn1_w3, v_ffn1_w3, True),
           ("ffn1_w2", ffn1_w2, m_ffn1_w2, v_ffn1_w2, False), ("w_in", w_in, m_w_in, v_w_in, True),
           ("w_proj_attn", w_proj_attn, m_w_proj_attn, v_w_proj_attn, False),
           ("w_proj_hgrn", w_proj_hgrn, m_w_proj_hgrn, v_w_proj_hgrn, False),
           ("w_out", w_out, m_w_out, v_w_out, False),
           ("ffn2_w1", ffn2_w1, m_ffn2_w1, v_ffn2_w1, True), ("ffn2_w3", ffn2_w3, m_ffn2_w3, v_ffn2_w3, True),
           ("ffn2_w2", ffn2_w2, m_ffn2_w2, v_ffn2_w2, False)]
    view = lambda t, transposed: t[0].T if transposed else t[0]
    back = lambda t, transposed: t.T[None] if transposed else t[None]
    slots = lambda n: recv[n] if isinstance(recv[n], list) else [recv[n]]
    stepped, _ = _grad_steps("steps", [(slots(n), view(w, tr), view(m, tr), view(v, tr)) for n, w, m, v, tr in big], 128)
    res = {n: tuple(back(t, tr) for t in outs) for (n, _, _, _, tr), outs in zip(big, stepped)}

    small_w = dict(ln1_g=ln1_g, ln1_b=ln1_b, ln2_g=ln2_g, ln2_b=ln2_b, ln3_g=ln3_g, ln3_b=ln3_b, b_in=b_in,
                   attn_sinks=attn_sinks, hgrn_lb_logits=hgrn_lb_logits, hgrn_norm_g=hgrn_norm_g)
    small_m = dict(ln1_g=m_ln1_g, ln1_b=m_ln1_b, ln2_g=m_ln2_g, ln2_b=m_ln2_b, ln3_g=m_ln3_g, ln3_b=m_ln3_b,
                   b_in=m_b_in, attn_sinks=m_attn_sinks, hgrn_lb_logits=m_hgrn_lb_logits, hgrn_norm_g=m_hgrn_norm_g)
    small_v = dict(ln1_g=v_ln1_g, ln1_b=v_ln1_b, ln2_g=v_ln2_g, ln2_b=v_ln2_b, ln3_g=v_ln3_g, ln3_b=v_ln3_b,
                   b_in=v_b_in, attn_sinks=v_attn_sinks, hgrn_lb_logits=v_hgrn_lb_logits, hgrn_norm_g=v_hgrn_norm_g)
    outs, names = _small_step("small_step", small_slots, small_w, small_m, small_v)
    loss = outs[0][0, 0]
    for i, n in enumerate(names):
        res[n] = tuple(outs[1 + 4 * i:5 + 4 * i])

    order = ["ln1_g", "ln1_b", "ffn1_w1", "ffn1_w3", "ffn1_w2", "ln2_g", "ln2_b", "w_in", "b_in", "attn_sinks",
             "hgrn_lb_logits", "hgrn_norm_g", "w_proj_attn", "w_proj_hgrn", "w_out", "ln3_g", "ln3_b",
             "ffn2_w1", "ffn2_w3", "ffn2_w2"]
    return (loss, gx[None], *[res[n][0] for n in order], *[res[n][1] for n in order],
            *[res[n][2] for n in order], *[res[n][3] for n in order])
```

```python
import jax
import jax.numpy as jnp
from jax import lax
from jax.experimental import pallas as pl
from jax.experimental.pallas import tpu as pltpu

F32 = jnp.float32
BF16 = jnp.bfloat16

NDEV = 8
D = 1024
DFF = 2816
RF = DFF // NDEV
DIN = 7680
RIN = DIN // NDEV
RP = D // NDEV
N_Q_HEADS = 16
N_KV_HEADS = 4
HEAD_DIM = 64
ATTN_BLOCK = 128
ROPE_THETA = 500000.0
ROPE_DIM = HEAD_DIM // 4
HGRN_HEADS = 8
HGRN_CHUNK = 64
ALPHA = 2.0 ** 0.25
LN_EPS = 1e-5
RMS_EPS = 1e-6
NEG_INF = -1e30
ADAM_LR = 0.001
ADAM_B1 = 0.9
ADAM_B2 = 0.999
ADAM_EPS = 1e-08
ADAM_WD = 0.01
ADAM_STEP = 10

QKV_W = 1536
HG_W = 4096
GATE_W = 2048
VMEM_LIMIT = 60 * 2 ** 20
PART_ROWS = 256
MESH = pl.DeviceIdType.MESH
HBM_SPEC = pl.BlockSpec(memory_space=pltpu.HBM)


def _params(*sem):
    return pltpu.CompilerParams(dimension_semantics=sem, vmem_limit_bytes=VMEM_LIMIT)


def _dot(a, b):
    return jnp.dot(a, b, preferred_element_type=F32)


def _dot_nt(a, b):
    return lax.dot_general(a, b, (((1,), (1,)), ((), ())), preferred_element_type=F32)


def _dot_tn(a, b):
    return lax.dot_general(a, b, (((0,), (0,)), ((), ())), preferred_element_type=F32)


def _sigmoid(x):
    return 0.5 * jnp.tanh(0.5 * x) + 0.5


def _ln_fwd(z):
    mu = jnp.mean(z, axis=-1, keepdims=True)
    zc = z - mu
    var = jnp.mean(zc * zc, axis=-1, keepdims=True)
    r = lax.rsqrt(var + LN_EPS)
    return zc * r, r


def _ln_bwd(dy, xh, r, g):
    dxh = dy * g
    m1 = jnp.mean(dxh, axis=-1, keepdims=True)
    m2 = jnp.mean(dxh * xh, axis=-1, keepdims=True)
    dz = r * (dxh - m1 - xh * m2)
    return dz, jnp.sum(dy * xh, axis=0, keepdims=True), jnp.sum(dy, axis=0, keepdims=True)


def _load_rows(wall_ref, pieces, sem):
    copies = []
    for dst, off, r in pieces:
        for k in range(NDEV):
            c = pltpu.make_async_copy(wall_ref.at[k, pl.ds(off, r), :], dst.at[pl.ds(k * r, r), :], sem.at[len(copies)])
            c.start()
            copies.append(c)
    for c in copies:
        c.wait()


class _Exchange:
    def __init__(self, inputs, out_shape, scratch, begin, middle, end):
        self.inputs, self.out_shape, self.scratch = inputs, out_shape, scratch
        self.begin, self.middle, self.end = begin, middle, end


def _gather_exchange(shard):
    rows, cols = shard.shape

    def ops(ins, outs, scr):
        (x_ref,), (out_ref,), (send_sems, recv_sems, local_sem) = ins, outs, scr
        x, y, c = lax.axis_index("x"), lax.axis_index("y"), lax.axis_index("c")
        me, sibling = (x, y, c), (x, y, 1 - c)
        chips = [(1 - x, y), (x, 1 - y), (1 - x, 1 - y)]

        def slot(px, py, pc):
            return out_ref.at[4 * px + 2 * py + pc]

        def copy(k, block, to, src=None):
            return pltpu.make_async_remote_copy(
                src_ref=slot(*block) if src is None else src, dst_ref=slot(*block),
                send_sem=send_sems.at[k], recv_sem=recv_sems.at[k], device_id=to, device_id_type=MESH)

        mine = lambda: pltpu.make_async_copy(x_ref, slot(*me), local_sem)
        first = lambda: [copy(0, me, sibling, src=x_ref)] + [
            copy(1 + j, me, (*chip, c), src=x_ref) for j, chip in enumerate(chips)]
        passed = lambda: [copy(4 + j, (*chip, c), sibling) for j, chip in enumerate(chips)]
        return c, me, sibling, chips, copy, mine, first, passed

    def begin(*refs):
        _, _, _, _, _, mine, first, _ = ops(*refs)
        mine().start()
        for cp in first():
            cp.start()

    def middle(*refs):
        c, me, _, chips, copy, _, _, passed = ops(*refs)
        for (j, chip), fwd in zip(enumerate(chips), passed()):
            copy(1 + j, (*chip, c), me).wait_recv()
            fwd.start()

    def end(*refs):
        c, me, sibling, chips, copy, mine, first, passed = ops(*refs)
        copy(0, sibling, me).wait_recv()
        for j, chip in enumerate(chips):
            copy(4 + j, (*chip, 1 - c), me).wait_recv()
        for cp in first() + passed():
            cp.wait_send()
        mine().wait()

    return _Exchange([shard], [jax.ShapeDtypeStruct((NDEV, rows, cols), shard.dtype)],
                     [pltpu.SemaphoreType.DMA((7,)), pltpu.SemaphoreType.DMA((7,)), pltpu.SemaphoreType.DMA],
                     begin, middle, end)


def _grad_exchange(grads, bases, rows, strides=None):
    n = len(grads)
    strides = rows if strides is None else strides

    def copies(g_refs, out_refs, scr):
        send_sems, recv_sems, local_sems = scr
        x, y, c = lax.axis_index("x"), lax.axis_index("y"), lax.axis_index("c")
        me = 4 * x + 2 * y + c
        out = []
        for i in range(n):
            r = rows[i]
            src = lambda k: g_refs[i].at[pl.ds(pl.multiple_of(bases[i] + k * strides[i], 16), r), :]
            out.append(pltpu.make_async_copy(src(me), out_refs[i].at[me], local_sems.at[i]))
            for j in range(1, NDEV):
                px, py, pc = x ^ (j >> 2), y ^ ((j >> 1) & 1), c ^ (j & 1)
                out.append(pltpu.make_async_remote_copy(
                    src_ref=src(4 * px + 2 * py + pc), dst_ref=out_refs[i].at[me],
                    send_sem=send_sems.at[i, j - 1], recv_sem=recv_sems.at[i, j - 1],
                    device_id=(px, py, pc), device_id_type=MESH))
        return out

    def begin(*refs):
        for cp in copies(*refs):
            cp.start()

    def end(*refs):
        for cp in copies(*refs):
            cp.wait()

    return _Exchange(list(grads), [jax.ShapeDtypeStruct((NDEV, r, g.shape[1]), g.dtype) for g, r in zip(grads, rows)],
                     [pltpu.SemaphoreType.DMA((n, NDEV - 1)), pltpu.SemaphoreType.DMA((n, NDEV - 1)),
                      pltpu.SemaphoreType.DMA((n,))], begin, None, end)


def _row_gather_exchange(row):
    def copies(ins, outs, scr):
        (r_ref,), (o_ref,), (send_sems, recv_sems, local_sem) = ins, outs, scr
        x, y, c = lax.axis_index("x"), lax.axis_index("y"), lax.axis_index("c")
        me = 4 * x + 2 * y + c
        out = [pltpu.make_async_copy(r_ref, o_ref.at[me], local_sem)]
        for j in range(1, NDEV):
            out.append(pltpu.make_async_remote_copy(
                src_ref=r_ref, dst_ref=o_ref.at[me], send_sem=send_sems.at[j - 1], recv_sem=recv_sems.at[j - 1],
                device_id=(x ^ (j >> 2), y ^ ((j >> 1) & 1), c ^ (j & 1)), device_id_type=MESH))
        return out

    def begin(*refs):
        for cp in copies(*refs):
            cp.start()

    def end(*refs):
        for cp in copies(*refs):
            cp.wait()

    return _Exchange([row], [jax.ShapeDtypeStruct((NDEV,) + row.shape, row.dtype)],
                     [pltpu.SemaphoreType.DMA((NDEV - 1,)), pltpu.SemaphoreType.DMA((NDEV - 1,)), pltpu.SemaphoreType.DMA],
                     begin, None, end)


def _join(a, b):
    na, oa, sa = len(a.inputs), len(a.out_shape), len(a.scratch)
    split = lambda ins, outs, scr: ((ins[:na], outs[:oa], scr[:sa]), (ins[na:], outs[oa:], scr[sa:]))

    def begin(*refs):
        pa, pb = split(*refs)
        a.begin(*pa)
        b.begin(*pb)

    def end(*refs):
        pa, pb = split(*refs)
        a.end(*pa)
        b.end(*pb)

    return _Exchange(a.inputs + b.inputs, a.out_shape + b.out_shape, a.scratch + b.scratch, begin, None, end)


def _exchange_call(name, ex):
    ni, no = len(ex.inputs), len(ex.out_shape)

    def body(*refs):
        parts = (refs[:ni], refs[ni:ni + no], refs[ni + no:])
        ex.begin(*parts)
        if ex.middle is not None:
            ex.middle(*parts)
        ex.end(*parts)

    return pl.pallas_call(body, name=name, out_shape=ex.out_shape, in_specs=[HBM_SPEC] * ni, out_specs=[HBM_SPEC] * no,
                          scratch_shapes=ex.scratch)(*ex.inputs)


def _call(body, *, name, grid, in_specs, out_specs, out_shape, scratch_shapes, ins, carry=None):
    sem = ("arbitrary",) * len(grid)
    if carry is None:
        outs = pl.pallas_call(body, name=name, grid=grid, in_specs=in_specs, out_specs=out_specs, out_shape=out_shape,
                              scratch_shapes=scratch_shapes, compiler_params=_params(*sem))(*ins)
        return outs, None
    n_in, n_out, n_scr = len(ins), len(out_shape), len(scratch_shapes)
    ci, co = len(carry.inputs), len(carry.out_shape)
    total = 1
    for g in grid:
        total *= g

    def wrapped(*refs):
        own_in, ex_in = refs[:n_in], refs[n_in:n_in + ci]
        o0 = n_in + ci
        own_out, ex_out = refs[o0:o0 + n_out], refs[o0 + n_out:o0 + n_out + co]
        s0 = o0 + n_out + co
        own_scr, ex_scr = refs[s0:s0 + n_scr], refs[s0 + n_scr:]
        step = pl.program_id(0)
        for d in range(1, len(grid)):
            step = step * grid[d] + pl.program_id(d)
        parts = (ex_in, ex_out, ex_scr)
        pl.when(step == 0)(lambda: carry.begin(*parts))
        body(*own_in, *own_out, *own_scr)
        if carry.middle is not None:
            pl.when(step == (3 * total) // 4)(lambda: carry.middle(*parts))
        pl.when(step == total - 1)(lambda: carry.end(*parts))

    outs = pl.pallas_call(
        wrapped, name=name, grid=grid, in_specs=list(in_specs) + [HBM_SPEC] * ci,
        out_specs=list(out_specs) + [HBM_SPEC] * co, out_shape=list(out_shape) + list(carry.out_shape),
        scratch_shapes=list(scratch_shapes) + list(carry.scratch), compiler_params=_params(*sem),
    )(*ins, *carry.inputs)
    return outs[:n_out], outs[n_out:]


def _ffn_fwd(name, xin, affine, wall, offs, tm, loss=None, carry=None):
    T = xin.shape[0]
    nt = T // tm

    def body(*refs):
        it = iter(refs)
        x_ref = next(it)
        if affine is not None:
            g_ref, b_ref = next(it), next(it)
        wall_ref = next(it)
        if loss is not None:
            go_ref, bo_ref, tgt_ref = next(it), next(it), next(it)
        xh_ref, r_ref, a_ref, b2_ref, yb_ref = (next(it) for _ in range(5))
        if loss is not None:
            loss_ref = next(it)
        w1, w3, w2, sem = (next(it) for _ in range(4))

        @pl.when(pl.program_id(0) == 0)
        def _():
            _load_rows(wall_ref, [(w1, offs[0], RF), (w3, offs[1], RF), (w2, offs[2], RF)], sem)
            if loss is not None:
                loss_ref[...] = jnp.zeros_like(loss_ref)

        parts = [slice(p, min(p + PART_ROWS, tm)) for p in range(0, tm, PART_ROWS)]
        ys, ybs = [], []
        for rs in parts:
            y = x_ref[rs, :]
            if affine is not None:
                y = y * g_ref[...] + b_ref[...]
            ys.append(y)
            ybs.append(y.astype(BF16))
            yb_ref[rs, :] = ybs[-1]
        ab = [(_dot_nt(yb, w1[...]).astype(BF16), _dot_nt(yb, w3[...]).astype(BF16)) for yb in ybs]
        us = []
        for rs, (a, b) in zip(parts, ab):
            a_ref[rs, :] = a
            b2_ref[rs, :] = b
            af, bf = a.astype(F32), b.astype(F32)
            us.append((af * _sigmoid(af) * bf).astype(BF16))
        fs = [_dot(u, w2[...]) for u in us]
        for rs, y, f in zip(parts, ys, fs):
            xh, r = _ln_fwd(ALPHA * y + 0.5 * f)
            xh_ref[rs, :] = xh
            r_ref[rs, :] = jnp.broadcast_to(r, (xh.shape[0], 128))
            if loss is not None:
                e = xh * go_ref[...] + bo_ref[...] - tgt_ref[rs, :]
                loss_ref[...] += jnp.sum(e * e) * (0.5 / D)

    row = lambda w: pl.BlockSpec((tm, w), lambda i: (i, 0))
    vec = pl.BlockSpec((1, D), lambda i: (0, 0))
    ins, in_specs = [xin], [row(D)]
    if affine is not None:
        ins += list(affine)
        in_specs += [vec, vec]
    ins.append(wall)
    in_specs.append(HBM_SPEC)
    if loss is not None:
        ins += list(loss)
        in_specs += [vec, vec, row(D)]
    out_shape = [jax.ShapeDtypeStruct((T, D), F32), jax.ShapeDtypeStruct((T, 128), F32),
                 jax.ShapeDtypeStruct((T, DFF), BF16), jax.ShapeDtypeStruct((T, DFF), BF16),
                 jax.ShapeDtypeStruct((T, D), BF16)]
    out_specs = [row(D), row(128), row(DFF), row(DFF), row(D)]
    if loss is not None:
        out_shape.append(jax.ShapeDtypeStruct((8, 128), F32))
        out_specs.append(pl.BlockSpec((8, 128), lambda i: (0, 0)))
    return _call(body, name=name, grid=(nt,), in_specs=in_specs, out_specs=out_specs, out_shape=out_shape,
                 scratch_shapes=[pltpu.VMEM((DFF, D), BF16)] * 3 + [pltpu.SemaphoreType.DMA((3 * NDEV,))],
                 ins=ins, carry=carry)


def _ffn_bwd(name, dy_src, xh, r, g, a, b, wall, offs, tm, with_dx=True, carry=None):
    T = xh.shape[0]
    nt = T // tm
    from_loss = dy_src[0] == "loss"

    def body(*refs):
        it = iter(refs)
        if from_loss:
            bo_ref, tgt_ref = next(it), next(it)
        else:
            dy_ref = next(it)
        xh_ref, r_ref, g_ref, a_ref, b_ref, wall_ref = (next(it) for _ in range(6))
        dyin_ref, dab_ref, u_ref, df_ref, dg_ref, db_ref = (next(it) for _ in range(6))
        if with_dx:
            w1, w3 = next(it), next(it)
        w2, sem = next(it), next(it)

        @pl.when(pl.program_id(0) == 0)
        def _():
            _load_rows(wall_ref, ([(w1, offs[0], RF), (w3, offs[1], RF)] if with_dx else []) + [(w2, offs[2], RF)], sem)
            dg_ref[...] = jnp.zeros_like(dg_ref)
            db_ref[...] = jnp.zeros_like(db_ref)

        parts = [slice(p, min(p + PART_ROWS, tm)) for p in range(0, tm, PART_ROWS)]
        gv = g_ref[...]
        dz, df = [], []
        for rs in parts:
            xhv = xh_ref[rs, :]
            if from_loss:
                dy = (xhv * gv + bo_ref[...] - tgt_ref[rs, :]) * (1.0 / D)
            else:
                dy = dy_ref[rs, :]
            dzp, dgp, dbp = _ln_bwd(dy, xhv, r_ref[rs, :1], gv)
            dg_ref[...] += dgp
            db_ref[...] += dbp
            dz.append(dzp)
            df.append((0.5 * dzp).astype(BF16))
            df_ref[rs, :] = df[-1]
        du = [_dot_nt(d, w2[...]) for d in df]
        da, dbb = [], []
        for rs, dup in zip(parts, du):
            af, bf = a_ref[rs, :].astype(F32), b_ref[rs, :].astype(F32)
            s = _sigmoid(af)
            sl = af * s
            u_ref[rs, :] = (sl * bf).astype(BF16)
            da.append((dup * bf * (s * (1.0 + af * (1.0 - s)))).astype(BF16))
            dbb.append((dup * sl).astype(BF16))
            dab_ref[rs, :DFF] = da[-1]
            dab_ref[rs, DFF:] = dbb[-1]
        for rs, dzp, dap, dbp in zip(parts, dz, da, dbb):
            dyin_ref[rs, :] = ALPHA * dzp + _dot(dap, w1[...]) + _dot(dbp, w3[...]) if with_dx else dzp

    row = lambda w: pl.BlockSpec((tm, w), lambda i: (i, 0))
    vec = pl.BlockSpec((1, D), lambda i: (0, 0))
    if from_loss:
        ins, in_specs = [dy_src[1], dy_src[2]], [vec, row(D)]
    else:
        ins, in_specs = [dy_src[1]], [row(D)]
    ins += [xh, r, g, a, b, wall]
    in_specs += [row(D), row(128), vec, row(DFF), row(DFF), HBM_SPEC]
    return _call(
        body, name=name, grid=(nt,), in_specs=in_specs,
        out_specs=[row(D), row(2 * DFF), row(DFF), row(D), vec, vec],
        out_shape=[jax.ShapeDtypeStruct((T, D), F32), jax.ShapeDtypeStruct((T, 2 * DFF), BF16),
                   jax.ShapeDtypeStruct((T, DFF), BF16), jax.ShapeDtypeStruct((T, D), BF16),
                   jax.ShapeDtypeStruct((1, D), F32), jax.ShapeDtypeStruct((1, D), F32)],
        scratch_shapes=[pltpu.VMEM((DFF, D), BF16)] * (3 if with_dx else 1) + [pltpu.SemaphoreType.DMA((3 * NDEV,))],
        ins=ins, carry=carry)


def _ffn_dx(name, dab, dz, wall, offs, tm, carry=None):
    T = dz.shape[0]

    def body(dab_ref, dz_ref, wall_ref, o_ref, w13, sem):
        @pl.when(pl.program_id(0) == 0)
        def _():
            _load_rows(wall_ref, [(w13.at[pl.ds(0, DFF), :], offs[0], RF), (w13.at[pl.ds(DFF, DFF), :], offs[1], RF)], sem)

        for p in range(0, tm, PART_ROWS):
            rs = slice(p, min(p + PART_ROWS, tm))
            o_ref[rs, :] = ALPHA * dz_ref[rs, :] + _dot(dab_ref[rs, :], w13[...])

    row = lambda w: pl.BlockSpec((tm, w), lambda i: (i, 0))
    (out,), ex = _call(
        body, name=name, grid=(T // tm,), in_specs=[row(2 * DFF), row(D), HBM_SPEC], out_specs=[row(D)],
        out_shape=[jax.ShapeDtypeStruct((T, D), F32)],
        scratch_shapes=[pltpu.VMEM((2 * DFF, D), BF16), pltpu.SemaphoreType.DMA((2 * NDEV,))],
        ins=[dab, dz, wall], carry=carry)
    return out, ex


def _wgrad(name, a, b, bn, tk, col0=0, ncols=None, carry=None):
    T = a.shape[0]
    N = a.shape[1] if ncols is None else ncols
    nk = T // tk
    c0 = col0 // bn

    def body(a_ref, b_ref, o_ref, acc):
        k = pl.program_id(1)

        @pl.when(k == 0)
        def _():
            acc[...] = jnp.zeros_like(acc)

        acc[...] += _dot_tn(a_ref[...], b_ref[...])

        @pl.when(k == nk - 1)
        def _():
            o_ref[...] = acc[...].astype(BF16)

    (out,), ex = _call(
        body, name=name, grid=(N // bn, nk),
        in_specs=[pl.BlockSpec((tk, bn), lambda n, k: (k, n + c0)), pl.BlockSpec((tk, D), lambda n, k: (k, 0))],
        out_specs=[pl.BlockSpec((bn, D), lambda n, k: (n, 0))],
        out_shape=[jax.ShapeDtypeStruct((N, D), BF16)],
        scratch_shapes=[pltpu.VMEM((bn, D), F32)], ins=[a, b], carry=carry)
    return out, ex


def _rope_table(T):
    pos = jnp.arange(T, dtype=F32)
    inv_freq = ROPE_THETA ** (-jnp.arange(0, ROPE_DIM, 2, dtype=F32) / ROPE_DIM)
    ang = pos[:, None] * inv_freq[None, :]
    return jnp.pad(jnp.concatenate([jnp.cos(ang), jnp.sin(ang)], axis=1), ((0, 0), (0, 128 - ROPE_DIM)))


def _rope_expand(cs):
    half = ROPE_DIM // 2
    lane = lax.broadcasted_iota(jnp.int32, cs.shape, 1)
    cos = jnp.where(lane < half, cs, 0.0)
    sin = jnp.where((lane >= half) & (lane < ROPE_DIM), cs, 0.0)
    both = lambda t: t + pltpu.roll(t, HEAD_DIM, 1)
    c = jnp.where(jnp.bitwise_and(lane, HEAD_DIM - 1) < ROPE_DIM, both(cos + pltpu.roll(cos, half, 1)), 1.0)
    return c, -both(pltpu.roll(sin, 128 - half, 1)), both(sin)


def _rope(t, c, s1, s2):
    n = t.shape[1] // 128
    ct, s1t, s2t = (jnp.tile(v, (1, n)) for v in (c, s1, s2))
    w = t.shape[1]
    return t * ct + pltpu.roll(t, w - 8, 1) * s1t + pltpu.roll(t, 8, 1) * s2t


def _rope_t(dr, c, s1, s2):
    n = dr.shape[1] // 128
    ct, s1t, s2t = (jnp.tile(v, (1, n)) for v in (c, s1, s2))
    w = dr.shape[1]
    return dr * ct + pltpu.roll(dr * s1t, 8, 1) + pltpu.roll(dr * s2t, w - 8, 1)


_Q, _K, _V = (0, 1024), (1024, 256), (1280, 256)
_HG = (1536, HG_W)
_GATES = (5632, GATE_W)


def _inproj_fwd(name, xh, g, b, wall, b_in, rope, tm, carry=None):
    T = xh.shape[0]

    def body(xh_ref, g_ref, b_ref, wall_ref, bin_ref, cs_ref,
             qkv_ref, kt_ref, vt_ref, hg_ref, gate_ref, yb_ref, w, sem):
        @pl.when(pl.program_id(0) == 0)
        def _():
            _load_rows(wall_ref, [(w, 0, RIN)], sem)

        yb = (xh_ref[...] * g_ref[...] + b_ref[...]).astype(BF16)
        yb_ref[...] = yb
        c, s1, s2 = _rope_expand(cs_ref[...])

        def piece(start, width):
            return _dot_nt(yb, w[start:start + width, :]) + bin_ref[:, start:start + width]

        k = _rope(piece(*_K), c, s1, s2)
        v = piece(*_V)
        qkv_ref[:, 0:1024] = _rope(piece(*_Q), c, s1, s2).astype(BF16)
        qkv_ref[:, 1024:1280] = k.astype(BF16)
        qkv_ref[:, 1280:1536] = v.astype(BF16)
        kt_ref[...] = k.T.astype(BF16)
        vt_ref[...] = v.T.astype(BF16)
        for j in range(4):
            hg_ref[:, 1024 * j:1024 * (j + 1)] = piece(_HG[0] + 1024 * j, 1024)
        for j in range(2):
            gate_ref[:, 1024 * j:1024 * (j + 1)] = piece(_GATES[0] + 1024 * j, 1024)

    row = lambda wd: pl.BlockSpec((tm, wd), lambda i: (i, 0))
    vec = lambda wd: pl.BlockSpec((1, wd), lambda i: (0, 0))
    colt = pl.BlockSpec((256, tm), lambda i: (0, i))
    return _call(
        body, name=name, grid=(T // tm,),
        in_specs=[row(D), vec(D), vec(D), HBM_SPEC, vec(DIN), row(128)],
        out_specs=[row(QKV_W), colt, colt, row(HG_W), row(GATE_W), row(D)],
        out_shape=[jax.ShapeDtypeStruct((T, QKV_W), BF16), jax.ShapeDtypeStruct((256, T), BF16),
                   jax.ShapeDtypeStruct((256, T), BF16), jax.ShapeDtypeStruct((T, HG_W), F32),
                   jax.ShapeDtypeStruct((T, GATE_W), F32), jax.ShapeDtypeStruct((T, D), BF16)],
        scratch_shapes=[pltpu.VMEM((DIN, D), BF16), pltpu.SemaphoreType.DMA((NDEV,))],
        ins=[xh, g, b, wall, b_in, rope], carry=carry)


def _inproj_bwd(name, dq, dk, dv, dhg, dgates, dz2, wall, rope, tm, carry=None):
    T = dq.shape[0]

    def body(dq_ref, dk_ref, dv_ref, d0, d1, d2, d3, dga_ref, dgh_ref, dz_ref, wall_ref, cs_ref,
             dy_ref, dproj_ref, dbin_ref, w, sem):
        @pl.when(pl.program_id(0) == 0)
        def _():
            _load_rows(wall_ref, [(w, 0, RIN)], sem)
            dbin_ref[...] = jnp.zeros_like(dbin_ref)

        c, s1, s2 = _rope_expand(cs_ref[...])
        acc = ALPHA * dz_ref[...]
        pieces = [(_Q[0], _rope_t(dq_ref[...], c, s1, s2)), (_K[0], _rope_t(dk_ref[...], c, s1, s2)),
                  (_V[0], dv_ref[...])]
        pieces += [(_HG[0] + 1024 * j, r[...]) for j, r in enumerate((d0, d1, d2, d3))]
        pieces += [(_GATES[0], dga_ref[...]), (_GATES[0] + 1024, dgh_ref[...])]
        for start, val in pieces:
            width = val.shape[1]
            dbin_ref[:, start:start + width] += jnp.sum(val.astype(F32), axis=0, keepdims=True)
            vb = val.astype(BF16)
            dproj_ref[:, start:start + width] = vb
            acc = acc + _dot(vb, w[start:start + width, :])
        dy_ref[...] = acc

    row = lambda wd: pl.BlockSpec((tm, wd), lambda i: (i, 0))
    return _call(
        body, name=name, grid=(T // tm,),
        in_specs=[row(D), row(256), row(256)] + [row(D)] * 4 + [row(D), row(D), row(D), HBM_SPEC, row(128)],
        out_specs=[row(D), row(DIN), pl.BlockSpec((1, DIN), lambda i: (0, 0))],
        out_shape=[jax.ShapeDtypeStruct((T, D), F32), jax.ShapeDtypeStruct((T, DIN), BF16),
                   jax.ShapeDtypeStruct((1, DIN), F32)],
        scratch_shapes=[pltpu.VMEM((DIN, D), BF16), pltpu.SemaphoreType.DMA((NDEV,))],
        ins=[dq, dk, dv, *dhg, *dgates, dz2, wall, rope], carry=carry)


def _halves(t):
    lane = lax.broadcasted_iota(jnp.int32, t.shape, 1)
    low = lane < HEAD_DIM
    sw = pltpu.roll(t, HEAD_DIM, 1)
    zero = jnp.zeros_like(t)
    h0 = (jnp.where(low, t, zero), jnp.where(low, zero, sw))
    h1 = (jnp.where(low, sw, zero), jnp.where(low, zero, t))
    return h0, h1


def _lane_stack(p_ref, c_ref, gp):
    sl = slice(128 * gp, 128 * (gp + 1))
    hp, hc = _halves(p_ref[:, sl].astype(F32)), _halves(c_ref[:, sl].astype(F32))
    return [jnp.concatenate([hp[gg][0], hc[gg][0], hp[gg][1], hc[gg][1]], axis=0).astype(BF16) for gg in range(2)]


def _row_stack(tp_ref, tc_ref, g):
    band = jnp.concatenate([tp_ref[64 * g:64 * (g + 1), :], tc_ref[64 * g:64 * (g + 1), :]], axis=1)
    z = jnp.zeros_like(band)
    return [jnp.concatenate([band, z], axis=0), jnp.concatenate([z, band], axis=0)]


def _fold(prevm, t4, hh):
    return jnp.where(prevm, t4[256 * hh:256 * hh + 128, :], t4[256 * hh + 128:256 * hh + 256, :])


def _unfold(prevm, t):
    return jnp.concatenate([jnp.where(prevm, t, 0.0), jnp.where(prevm, 0.0, t)], axis=0).astype(BF16)


def _attn_softmax(s, kill, sink):
    s = jnp.where(kill, NEG_INF, s)
    m = jnp.maximum(jnp.max(s, axis=0, keepdims=True), sink)
    p = jnp.exp(s - m)
    es = jnp.exp(sink - m)
    inv = 1.0 / (jnp.sum(p, axis=0, keepdims=True) + es)
    return p * inv, es * inv


def _attn_masks(first):
    row = lax.broadcasted_iota(jnp.int32, (ATTN_BLOCK, 2 * ATTN_BLOCK), 0)
    lane = lax.broadcasted_iota(jnp.int32, (ATTN_BLOCK, 2 * ATTN_BLOCK), 1)
    prevm = row > lane % ATTN_BLOCK
    return prevm, jnp.logical_and(first, prevm)


def _pair_rows(ref, g):
    return jnp.concatenate([ref[:, 256 * g:256 * g + 128], ref[:, 256 * g + 128:256 * (g + 1)]], axis=0)


def _pair_sinks(sink_ref, g, hh):
    h0, h1 = 4 * g + hh, 4 * g + 2 + hh
    return jnp.concatenate([jnp.broadcast_to(sink_ref[:, h0:h0 + 1], (1, ATTN_BLOCK)),
                            jnp.broadcast_to(sink_ref[:, h1:h1 + 1], (1, ATTN_BLOCK))], axis=1)


def _attn_fwd(name, qkv, vt, sinks):
    T = qkv.shape[0]
    nbk = min(ATTN_TILE, T // ATTN_BLOCK)
    rows = nbk * ATTN_BLOCK
    scale = HEAD_DIM ** -0.5

    def block(q_ref, kp_ref, kc_ref, vtp_ref, vtc_ref, sink_ref, o_ref, first):
        prevm, kill = _attn_masks(first)
        kst = _lane_stack(kp_ref, kc_ref, 0) + _lane_stack(kp_ref, kc_ref, 1)
        vts = [_row_stack(vtp_ref, vtc_ref, g) for g in range(N_KV_HEADS)]
        s8 = [_dot_nt(kst[g], _pair_rows(q_ref, g)) * scale for g in range(N_KV_HEADS)]
        pu = [[_unfold(prevm, _attn_softmax(_fold(prevm, s8[g], hh), kill, _pair_sinks(sink_ref, g, hh))[0])
               for hh in range(2)] for g in range(N_KV_HEADS)]
        for g in range(N_KV_HEADS):
            ot = _dot(vts[g][0], pu[g][0]) + _dot(vts[g][1], pu[g][1])
            o_ref[:, 256 * g:256 * g + 128] = ot[:, :ATTN_BLOCK].T.astype(BF16)
            o_ref[:, 256 * g + 128:256 * (g + 1)] = ot[:, ATTN_BLOCK:].T.astype(BF16)

    def body(q_ref, kp_ref, kc_ref, vtp_ref, vtc_ref, sink_ref, o_ref):
        for b in range(nbk):
            rs = slice(ATTN_BLOCK * b, ATTN_BLOCK * (b + 1))
            ps = slice(ATTN_BLOCK * (b - 1), ATTN_BLOCK * b)
            block(q_ref.at[rs, :], kp_ref if b == 0 else kc_ref.at[ps, :], kc_ref.at[rs, :],
                  vtp_ref if b == 0 else vtc_ref.at[:, ps], vtc_ref.at[:, rs], sink_ref, o_ref.at[rs, :],
                  pl.program_id(0) == 0 if b == 0 else False)

    prev = lambda i: jnp.maximum(nbk * i - 1, 0)
    return pl.pallas_call(
        body, name=name, grid=(T // rows,),
        in_specs=[pl.BlockSpec((rows, D), lambda i: (i, 0)),
                  pl.BlockSpec((ATTN_BLOCK, 256), lambda i: (prev(i), 4)),
                  pl.BlockSpec((rows, 256), lambda i: (i, 4)),
                  pl.BlockSpec((256, ATTN_BLOCK), lambda i: (0, prev(i))),
                  pl.BlockSpec((256, rows), lambda i: (0, i)),
                  pl.BlockSpec((1, N_Q_HEADS), lambda i: (0, 0))],
        out_specs=pl.BlockSpec((rows, D), lambda i: (i, 0)),
        out_shape=jax.ShapeDtypeStruct((T, D), BF16),
        compiler_params=_params("arbitrary"),
    )(qkv, qkv, qkv, vt, vt, sinks)


ATTN_TILE = 4
ATTN_BWD_ORDER = ((0, 0), (0, 1), (1, 0), (0, 2), (1, 1), (2, 0), (0, 3), (1, 2), (2, 1), (1, 3), (2, 2), (2, 3))


def _attn_bwd(name, qkv, kt, sinks, dya, carry=None):
    T = qkv.shape[0]
    nbk = min(ATTN_TILE, T // ATTN_BLOCK)
    rows = nbk * ATTN_BLOCK
    scale = HEAD_DIM ** -0.5

    def block(q_ref, kp_ref, kc_ref, vp_ref, vc_ref, ktp_ref, ktc_ref, sink_ref, do_ref, dq_ref, first):
        prevm, kill = _attn_masks(first)
        kst = _lane_stack(kp_ref, kc_ref, 0) + _lane_stack(kp_ref, kc_ref, 1)
        vst = _lane_stack(vp_ref, vc_ref, 0) + _lane_stack(vp_ref, vc_ref, 1)
        kts = [_row_stack(ktp_ref, ktc_ref, g) for g in range(N_KV_HEADS)]
        lane = lax.broadcasted_iota(jnp.int32, (2 * ATTN_BLOCK, 128), 1)
        low = lane < HEAD_DIM
        slane = lax.broadcasted_iota(jnp.int32, (1, 128), 1)
        dkz = [jnp.zeros((2 * ATTN_BLOCK, 128), F32) for _ in range(N_KV_HEADS)]
        dvz = [jnp.zeros((2 * ATTN_BLOCK, 128), F32) for _ in range(N_KV_HEADS)]
        dsink = jnp.zeros((1, 128), F32)
        groups = range(N_KV_HEADS)
        qcat, docat, s8, dp8 = {}, {}, {}, {}
        ds_u, p_u, dsinks = {}, {}, []

        def scores(g):
            qcat[g], docat[g] = _pair_rows(q_ref, g), _pair_rows(do_ref, g)
            s8[g] = _dot_nt(kst[g], qcat[g]) * scale
            dp8[g] = _dot_nt(vst[g], docat[g])

        def algebra(g):
            for hh in range(2):
                pn, ps = _attn_softmax(_fold(prevm, s8[g], hh), kill, _pair_sinks(sink_ref, g, hh))
                dp = _fold(prevm, dp8[g], hh)
                delta = jnp.sum(pn * dp, axis=0, keepdims=True)
                sd = ps * delta
                dsinks.append(jnp.where(slane == 4 * g + hh, -jnp.sum(sd[:, :ATTN_BLOCK]), 0.0)
                              + jnp.where(slane == 4 * g + 2 + hh, -jnp.sum(sd[:, ATTN_BLOCK:]), 0.0))
                ds_u[g, hh] = _unfold(prevm, pn * (dp - delta) * scale)
                p_u[g, hh] = _unfold(prevm, pn)

        def grads(g):
            dqt = _dot(kts[g][0], ds_u[g, 0]) + _dot(kts[g][1], ds_u[g, 1])
            dq_ref[:, 256 * g:256 * g + 128] = dqt[:, :ATTN_BLOCK].T
            dq_ref[:, 256 * g + 128:256 * (g + 1)] = dqt[:, ATTN_BLOCK:].T
            for hh in range(2):
                own = low if hh == 0 else jnp.logical_not(low)
                dk_h = jnp.where(own, _dot(ds_u[g, hh], qcat[g]), 0.0)
                dv_h = jnp.where(own, _dot(p_u[g, hh], docat[g]), 0.0)
                if hh != g % 2:
                    dk_h = pltpu.roll(dk_h, HEAD_DIM, 1)
                    dv_h = pltpu.roll(dv_h, HEAD_DIM, 1)
                dkz[g] = dkz[g] + dk_h
                dvz[g] = dvz[g] + dv_h

        for stage, g in ATTN_BWD_ORDER:
            (scores, algebra, grads)[stage](g)
        for t in dsinks:
            dsink = dsink + t
        pairs = range(N_KV_HEADS // 2)
        return dsink, [dkz[2 * gp] + dkz[2 * gp + 1] for gp in pairs], [dvz[2 * gp] + dvz[2 * gp + 1] for gp in pairs]

    def body(q_ref, kp_ref, kc_ref, vp_ref, vc_ref, ktp_ref, ktc_ref, sink_ref, do_ref,
             dq_ref, dk_ref, dv_ref, ds_ref):
        i = pl.program_id(0)

        @pl.when(i == 0)
        def _():
            ds_ref[...] = jnp.zeros_like(ds_ref)

        res = []
        for b in range(nbk):
            rs = slice(ATTN_BLOCK * b, ATTN_BLOCK * (b + 1))
            ps = slice(ATTN_BLOCK * (b - 1), ATTN_BLOCK * b)
            res.append(block(
                q_ref.at[rs, :], kp_ref if b == 0 else kc_ref.at[ps, :], kc_ref.at[rs, :],
                vp_ref if b == 0 else vc_ref.at[ps, :], vc_ref.at[rs, :],
                ktp_ref if b == 0 else ktc_ref.at[:, ps], ktc_ref.at[:, rs], sink_ref, do_ref.at[rs, :],
                dq_ref.at[rs, :], i == 0 if b == 0 else False))
        dsink = res[0][0]
        for r in res[1:]:
            dsink = dsink + r[0]
        ds_ref[...] += dsink
        for gp in range(N_KV_HEADS // 2):
            cols = slice(128 * gp, 128 * (gp + 1))
            for b in range(nbk):
                cur = pl.ds(pl.multiple_of(i * rows + ATTN_BLOCK * b, ATTN_BLOCK), ATTN_BLOCK)
                dkb, dvb = res[b][1][gp][ATTN_BLOCK:, :], res[b][2][gp][ATTN_BLOCK:, :]
                if b + 1 < nbk:
                    dkb, dvb = dkb + res[b + 1][1][gp][:ATTN_BLOCK, :], dvb + res[b + 1][2][gp][:ATTN_BLOCK, :]
                dk_ref[cur, cols] = dkb
                dv_ref[cur, cols] = dvb

            @pl.when(i > 0)
            def _():
                prv = pl.ds(pl.multiple_of(jnp.maximum(i * rows - ATTN_BLOCK, 0), ATTN_BLOCK), ATTN_BLOCK)
                dk_ref[prv, cols] += res[0][1][gp][:ATTN_BLOCK, :]
                dv_ref[prv, cols] += res[0][2][gp][:ATTN_BLOCK, :]

    prev = lambda i: jnp.maximum(nbk * i - 1, 0)
    whole = lambda w: pl.BlockSpec((T, w), lambda i: (0, 0))
    return _call(
        body, name=name, grid=(T // rows,), scratch_shapes=[], ins=[qkv, qkv, qkv, qkv, qkv, kt, kt, sinks, dya],
        carry=carry,
        in_specs=[pl.BlockSpec((rows, D), lambda i: (i, 0)),
                  pl.BlockSpec((ATTN_BLOCK, 256), lambda i: (prev(i), 4)),
                  pl.BlockSpec((rows, 256), lambda i: (i, 4)),
                  pl.BlockSpec((ATTN_BLOCK, 256), lambda i: (prev(i), 5)),
                  pl.BlockSpec((rows, 256), lambda i: (i, 5)),
                  pl.BlockSpec((256, ATTN_BLOCK), lambda i: (0, prev(i))),
                  pl.BlockSpec((256, rows), lambda i: (0, i)),
                  pl.BlockSpec((1, N_Q_HEADS), lambda i: (0, 0)),
                  pl.BlockSpec((rows, D), lambda i: (i, 0))],
        out_specs=[pl.BlockSpec((rows, D), lambda i: (i, 0)), whole(256), whole(256),
                   pl.BlockSpec((1, 128), lambda i: (0, 0))],
        out_shape=[jax.ShapeDtypeStruct((T, D), F32), jax.ShapeDtypeStruct((T, 256), F32),
                   jax.ShapeDtypeStruct((T, 256), F32), jax.ShapeDtypeStruct((1, 128), F32)])


HG_FWD = (256, 8, 512)
HG_BWD = (128, 8, 512)


def _chunk_sum(tri, x):
    w = x.shape[1]
    h1 = x.astype(BF16)
    h2 = (x - h1.astype(F32)).astype(BF16)
    r = _dot(tri, jnp.concatenate([h1, h2], axis=1))
    return r[:, :w] + r[:, w:]


def _chunk_tri(n, upper=False):
    ri = lax.broadcasted_iota(jnp.int32, (n, n), 0)
    ci = lax.broadcasted_iota(jnp.int32, (n, n), 1)
    same_chunk = jnp.bitwise_xor(ri, ci) < HGRN_CHUNK
    return jnp.where(((ri <= ci) if upper else (ri >= ci)) & same_chunk, 1.0, 0.0).astype(BF16)


def _chunks(t):
    return [t[HGRN_CHUNK * c:HGRN_CHUNK * (c + 1), :] for c in range(t.shape[0] // HGRN_CHUNK)]


def _lower_bound(lbl_ref):
    l0, l1 = lbl_ref[0:1, :], lbl_ref[1:2, :]
    m = jnp.maximum(l0, l1)
    e0, e1 = jnp.exp(l0 - m), jnp.exp(l1 - m)
    return e0 / (e0 + e1)


def _heads(t):
    return [t[:, 128 * h:128 * (h + 1)] for h in range(t.shape[1] // 128)]


def _per_head(fn, *wide):
    return jnp.concatenate([fn(*parts) for parts in zip(*[_heads(t) for t in wide])], axis=1)


def _hgrn_sub(fl, qh, vv, lb, tril_b):
    w = fl.shape[1]
    sg = _sigmoid(fl)
    f = lb + (1.0 - lb) * sg
    k = 1.0 - f
    gc = _chunk_sum(tril_b, jnp.log(f))
    gl_c = [t[HGRN_CHUNK - 1:HGRN_CHUNK, :] for t in _chunks(gc)]
    gl = jnp.concatenate([jnp.broadcast_to(g, (HGRN_CHUNK, w)) for g in gl_c], axis=0)
    sq = _sigmoid(qh)
    eg = jnp.exp(gc)
    eng = jnp.exp(-gc)
    elg = jnp.exp(gl - gc)
    qd = qh * sq * eg
    ki = k * eng
    ke = k * elg
    qd_b, ki_b, ke_b, v_b = (t.astype(BF16) for t in (qd, ki, ke, vv))
    am_b = [jnp.where(tril_b > 0, _dot_nt(qq, kk), 0.0).astype(BF16) for qq, kk in zip(_heads(qd_b), _heads(ki_b))]
    return dict(sg=sg, f=f, sq=sq, eg=eg, eng=eng, elg=elg, qd=qd, ki=ki, ke=ke, egl=[jnp.exp(g) for g in gl_c],
                am_b=am_b, qd_b=qd_b, ki_b=ki_b, ke_b=ke_b, v_b=v_b)


def _hgrn_out(q, st_b):
    outs = []
    for h, (qd_h, v_h) in enumerate(zip(_heads(q["qd_b"]), _heads(q["v_b"]))):
        inter = [_dot_nt(qc, s) for qc, s in zip(_chunks(qd_h), st_b[h])]
        outs.append(_dot(q["am_b"][h], v_h) + jnp.concatenate(inter, axis=0))
    return jnp.concatenate(outs, axis=1)


def _head_rms(o):
    return _per_head(lambda t: jnp.broadcast_to(
        lax.rsqrt(jnp.mean(t * t, axis=-1, keepdims=True) + RMS_EPS), t.shape), o)


def _hgrn_fwd(name, hg, lb_logits, norm_g, cfg):
    T = hg.shape[0]
    sub_rows, hps, th = min(cfg[0], T), cfg[1], min(cfg[2], T)
    cps = sub_rows // HGRN_CHUNK
    nc = th // HGRN_CHUNK

    def body(fl_ref, qh_ref, v_ref, og_ref, lbl_ref, ng_ref, y_ref, ss_ref, st):
        @pl.when(pl.program_id(1) == 0)
        def _():
            st[...] = jnp.zeros_like(st)

        lbs = _lower_bound(lbl_ref)
        ng = jnp.tile(ng_ref[...], (1, hps))
        tril_b = _chunk_tri(sub_rows)

        def sub(si, carry):
            rows = pl.ds(pl.multiple_of(si * sub_rows, sub_rows), sub_rows)
            q = _hgrn_sub(fl_ref[rows, :], qh_ref[rows, :], v_ref[rows, :], lbs, tril_b)
            v_hc = [_chunks(t) for t in _heads(q["v_b"])]
            k_hc = [_chunks(t) for t in _heads(q["ke_b"])]
            s = [st[h] for h in range(hps)]
            st_b = [[] for _ in range(hps)]
            for c in range(cps):
                egl = _heads(q["egl"][c])
                for h in range(hps):
                    ss_ref[h, si * cps + c] = s[h]
                    st_b[h].append(s[h].astype(BF16))
                    s[h] = s[h] * egl[h] + _dot_tn(v_hc[h][c], k_hc[h][c])
            for h in range(hps):
                st[h] = s[h]
            o = _hgrn_out(q, st_b)
            og = og_ref[rows, :]
            y_ref[rows, :] = (o * _head_rms(o) * ng * (og * _sigmoid(og))).astype(BF16)
            return carry

        lax.fori_loop(0, th // sub_rows, sub, 0)

    wd = 128 * hps
    col = lambda j: pl.BlockSpec((th, wd), lambda h, t: (t, (8 // hps) * j + h))
    return pl.pallas_call(
        body, name=name, grid=(HGRN_HEADS // hps, T // th),
        in_specs=[col(0), col(1), col(2), col(3), pl.BlockSpec((2, wd), lambda h, t: (0, h)),
                  pl.BlockSpec((1, 128), lambda h, t: (0, 0))],
        out_specs=[pl.BlockSpec((th, wd), lambda h, t: (t, h)),
                   pl.BlockSpec((hps, nc, 128, 128), lambda h, t: (h, t, 0, 0))],
        out_shape=[jax.ShapeDtypeStruct((T, D), BF16),
                   jax.ShapeDtypeStruct((HGRN_HEADS, T // HGRN_CHUNK, 128, 128), F32)],
        scratch_shapes=[pltpu.VMEM((hps, 128, 128), F32)],
        compiler_params=_params("parallel", "arbitrary"),
    )(hg, hg, hg, hg, lb_logits, norm_g)


def _hgrn_bwd(name, hg, lb_logits, norm_g, sstart, dyh, cfg, carry=None):
    T = hg.shape[0]
    sub_rows, hps, th = min(cfg[0], T), cfg[1], min(cfg[2], T)
    cps = sub_rows // HGRN_CHUNK
    nc = th // HGRN_CHUNK
    nt = T // th
    wd = 128 * hps

    def body(fl_ref, qh_ref, v_ref, og_ref, lbl_ref, ng_ref, ss_ref, dy_ref,
             dfl_ref, dqh_ref, dv_ref, dog_ref, dlb_ref, dng_ref, dst):
        @pl.when(pl.program_id(1) == 0)
        def _():
            dst[...] = jnp.zeros_like(dst)
            dlb_ref[...] = jnp.zeros_like(dlb_ref)
            dng_ref[...] = jnp.zeros_like(dng_ref)

        lb = _lower_bound(lbl_ref)
        ng = jnp.tile(ng_ref[...], (1, hps))
        last = lax.broadcasted_iota(jnp.int32, (HGRN_CHUNK, wd), 0) == HGRN_CHUNK - 1
        nsub = th // sub_rows
        cat0 = lambda parts: jnp.concatenate(parts, axis=0)
        tril_b, triu_b = _chunk_tri(sub_rows), _chunk_tri(sub_rows, upper=True)

        def sub(step, carry):
            si = nsub - 1 - step
            rows = pl.ds(pl.multiple_of(si * sub_rows, sub_rows), sub_rows)
            qh, og, dy = qh_ref[rows, :], og_ref[rows, :], dy_ref[rows, :]
            q = _hgrn_sub(fl_ref[rows, :], qh, v_ref[rows, :], lb, tril_b)
            s_in = [[ss_ref[h, si * cps + c] for c in range(cps)] for h in range(hps)]
            st_b = [[s.astype(BF16) for s in row] for row in s_in]
            o = _hgrn_out(q, st_b)
            rr = _head_rms(o)
            oh = o * rr
            sog = _sigmoid(og)
            dog_ref[rows, :] = (dy * oh * ng * (sog * (1.0 + og * (1.0 - sog)))).astype(BF16)
            don = dy * (og * sog)
            dng_w = jnp.sum(don * oh, axis=0, keepdims=True)
            dd = don * ng
            mean_h = _per_head(lambda t: jnp.broadcast_to(jnp.mean(t, axis=-1, keepdims=True), t.shape), dd * oh)
            do_b = (rr * (dd - oh * mean_h)).astype(BF16)
            do_h, qd_h, ki_h, ke_h, v_h = (_heads(t) for t in (do_b, q["qd_b"], q["ki_b"], q["ke_b"], q["v_b"]))
            da_b = [jnp.where(tril_b > 0, _dot_nt(do_h[h], v_h[h]), 0.0).astype(BF16) for h in range(hps)]
            do_c, qd_c, ke_c, v_c = ([_chunks(t) for t in hs] for hs in (do_h, qd_h, ke_h, v_h))
            dsp = [[None] * cps for _ in range(hps)]
            dgl_dec = [[None] * cps for _ in range(hps)]
            d = [dst[h] for h in range(hps)]
            for c in reversed(range(cps)):
                egl = _heads(q["egl"][c])
                for h in range(hps):
                    dsp[h][c] = d[h]
                    dgl_dec[h][c] = jnp.sum(d[h] * s_in[h][c], axis=0, keepdims=True) * egl[h]
                    d[h] = d[h] * egl[h] + _dot_tn(do_c[h][c], qd_c[h][c])
            for h in range(hps):
                dst[h] = d[h]
                dng_ref[h] += dng_w[:, 128 * h:128 * (h + 1)]
            dsp_b = [[t.astype(BF16) for t in row] for row in dsp]
            dv_ref[rows, :] = jnp.concatenate(
                [_dot_tn(q["am_b"][h], do_h[h]) + cat0([_dot_nt(ke_c[h][c], dsp_b[h][c]) for c in range(cps)])
                 for h in range(hps)], axis=1).astype(BF16)
            dqd = jnp.concatenate(
                [_dot(da_b[h], ki_h[h]) + cat0([_dot(do_c[h][c], st_b[h][c]) for c in range(cps)])
                 for h in range(hps)], axis=1)
            dki = jnp.concatenate([_dot_tn(da_b[h], qd_h[h]) for h in range(hps)], axis=1)
            dke = jnp.concatenate([cat0([_dot(v_c[h][c], dsp_b[h][c]) for c in range(cps)]) for h in range(hps)], axis=1)
            dkk = dke * q["ke"]
            dgl = [jnp.sum(t, axis=0, keepdims=True) + jnp.concatenate([dgl_dec[h][c] for h in range(hps)], axis=1)
                   for c, t in enumerate(_chunks(dkk))]
            dgc = dqd * q["qd"] - dki * q["ki"] - dkk + cat0([jnp.where(last, g, 0.0) for g in dgl])
            dlf = _chunk_sum(triu_b, dgc)
            df = dlf / q["f"] - (dki * q["eng"] + dke * q["elg"])
            sg = q["sg"]
            dlb_ref[...] += jnp.sum(df * (1.0 - sg), axis=0, keepdims=True)
            dfl_ref[rows, :] = (df * (1.0 - lb) * sg * (1.0 - sg)).astype(BF16)
            sq = q["sq"]
            dqh_ref[rows, :] = (dqd * q["eg"] * (sq * (1.0 + qh * (1.0 - sq)))).astype(BF16)
            return carry

        lax.fori_loop(0, nsub, sub, 0)
    col = lambda j: pl.BlockSpec((th, wd), lambda h, t: (nt - 1 - t, (8 // hps) * j + h))
    out = pl.BlockSpec((th, wd), lambda h, t: (nt - 1 - t, h))
    return _call(
        body, name=name, grid=(HGRN_HEADS // hps, nt),
        in_specs=[col(0), col(1), col(2), col(3), pl.BlockSpec((2, wd), lambda h, t: (0, h)),
                  pl.BlockSpec((1, 128), lambda h, t: (0, 0)),
                  pl.BlockSpec((hps, nc, 128, 128), lambda h, t: (h, nt - 1 - t, 0, 0)), out],
        out_specs=[out, out, out, out, pl.BlockSpec((1, wd), lambda h, t: (0, h)),
                   pl.BlockSpec((hps, 1, 128), lambda h, t: (h, 0, 0))],
        out_shape=[jax.ShapeDtypeStruct((T, D), BF16)] * 4 + [jax.ShapeDtypeStruct((1, D), F32),
                                                              jax.ShapeDtypeStruct((HGRN_HEADS, 1, 128), F32)],
        scratch_shapes=[pltpu.VMEM((hps, 128, 128), F32)],
        ins=[hg, hg, hg, hg, lb_logits, norm_g, sstart, dyh], carry=carry)


def _mix_fwd(name, ya, yh, gates, xh1, g1, b1, wall, tm):
    T = ya.shape[0]

    def body(ya_ref, yh_ref, ga_ref, gh_ref, xh_ref, g_ref, b_ref, wall_ref,
             xo_ref, r_ref, pa_ref, ph_ref, mg_ref, wpa, wph, wo, sem):
        @pl.when(pl.program_id(0) == 0)
        def _():
            _load_rows(wall_ref, [(wpa, RIN, RP), (wph, RIN + RP, RP), (wo, RIN + 2 * RP, RP)], sem)

        parts = [slice(p, min(p + PART_ROWS, tm)) for p in range(0, tm, PART_ROWS)]
        pas = [_dot(ya_ref[rs, :], wpa[...]).astype(BF16) for rs in parts]
        phs = [_dot(yh_ref[rs, :], wph[...]).astype(BF16) for rs in parts]
        mgs = []
        for rs, pa, ph in zip(parts, pas, phs):
            pa_ref[rs, :] = pa
            ph_ref[rs, :] = ph
            mgs.append((_sigmoid(ga_ref[rs, :]) * pa.astype(F32) + _sigmoid(gh_ref[rs, :]) * ph.astype(F32)).astype(BF16))
            mg_ref[rs, :] = mgs[-1]
        mixes = [_dot(mg, wo[...]) for mg in mgs]
        for rs, mix in zip(parts, mixes):
            y1 = xh_ref[rs, :] * g_ref[...] + b_ref[...]
            xh2, r = _ln_fwd(ALPHA * y1 + mix)
            xo_ref[rs, :] = xh2
            r_ref[rs, :] = jnp.broadcast_to(r, (xh2.shape[0], 128))

    row = lambda w: pl.BlockSpec((tm, w), lambda i: (i, 0))
    vec = pl.BlockSpec((1, D), lambda i: (0, 0))
    return pl.pallas_call(
        body, name=name, grid=(T // tm,),
        in_specs=[row(D), row(D), pl.BlockSpec((tm, D), lambda i: (i, 0)), pl.BlockSpec((tm, D), lambda i: (i, 1)),
                  row(D), vec, vec, HBM_SPEC],
        out_specs=[row(D), row(128), row(D), row(D), row(D)],
        out_shape=[jax.ShapeDtypeStruct((T, D), F32), jax.ShapeDtypeStruct((T, 128), F32)]
        + [jax.ShapeDtypeStruct((T, D), BF16)] * 3,
        scratch_shapes=[pltpu.VMEM((D, D), BF16)] * 3 + [pltpu.SemaphoreType.DMA((3 * NDEV,))],
        compiler_params=_params("arbitrary"),
    )(ya, yh, gates, gates, xh1, g1, b1, wall)


def _mix_bwd(name, dy2, xh2, r2, g2, gates, pa, ph, wall, tm, carry=None):
    T = dy2.shape[0]

    def body(dy_ref, xh_ref, r_ref, g_ref, ga_ref, gh_ref, pa_ref, ph_ref, wall_ref,
             dz_ref, dmix_ref, dpa_ref, dph_ref, dga_ref, dgh_ref, dya_ref, dyh_ref, dg_ref, db_ref,
             wpa, wph, wo, sem):
        @pl.when(pl.program_id(0) == 0)
        def _():
            _load_rows(wall_ref, [(wpa, RIN, RP), (wph, RIN + RP, RP), (wo, RIN + 2 * RP, RP)], sem)
            dg_ref[...] = jnp.zeros_like(dg_ref)
            db_ref[...] = jnp.zeros_like(db_ref)

        parts = [slice(p, min(p + PART_ROWS, tm)) for p in range(0, tm, PART_ROWS)]
        dmix = []
        for rs in parts:
            dz, dgp, dbp = _ln_bwd(dy_ref[rs, :], xh_ref[rs, :], r_ref[rs, :1], g_ref[...])
            dg_ref[...] += dgp
            db_ref[...] += dbp
            dz_ref[rs, :] = dz
            dmix.append(dz.astype(BF16))
            dmix_ref[rs, :] = dmix[-1]
        dmgs = [_dot_nt(d, wo[...]) for d in dmix]
        dpas, dphs = [], []
        for rs, dmg in zip(parts, dmgs):
            sa, sh = _sigmoid(ga_ref[rs, :]), _sigmoid(gh_ref[rs, :])
            dga_ref[rs, :] = (dmg * pa_ref[rs, :].astype(F32) * sa * (1.0 - sa)).astype(BF16)
            dgh_ref[rs, :] = (dmg * ph_ref[rs, :].astype(F32) * sh * (1.0 - sh)).astype(BF16)
            dpas.append((dmg * sa).astype(BF16))
            dphs.append((dmg * sh).astype(BF16))
            dpa_ref[rs, :] = dpas[-1]
            dph_ref[rs, :] = dphs[-1]
        for rs, dpa, dph in zip(parts, dpas, dphs):
            dya_ref[rs, :] = _dot_nt(dpa, wpa[...]).astype(BF16)
            dyh_ref[rs, :] = _dot_nt(dph, wph[...])

    row = lambda w: pl.BlockSpec((tm, w), lambda i: (i, 0))
    vec = pl.BlockSpec((1, D), lambda i: (0, 0))
    return _call(
        body, name=name, grid=(T // tm,),
        in_specs=[row(D), row(D), row(128), vec, pl.BlockSpec((tm, D), lambda i: (i, 0)),
                  pl.BlockSpec((tm, D), lambda i: (i, 1)), row(D), row(D), HBM_SPEC],
        out_specs=[row(D)] * 8 + [vec, vec],
        out_shape=[jax.ShapeDtypeStruct((T, D), F32)] + [jax.ShapeDtypeStruct((T, D), BF16)] * 6
        + [jax.ShapeDtypeStruct((T, D), F32)] + [jax.ShapeDtypeStruct((1, D), F32)] * 2,
        scratch_shapes=[pltpu.VMEM((D, D), BF16)] * 3 + [pltpu.SemaphoreType.DMA((3 * NDEV,))],
        ins=[dy2, xh2, r2, g2, gates, gates, pa, ph, wall], carry=carry)


def _adam(w, g, m, v):
    m = ADAM_B1 * m + (1.0 - ADAM_B1) * g
    v = ADAM_B2 * v + (1.0 - ADAM_B2) * (g * g)
    m_hat = m / (1.0 - ADAM_B1 ** ADAM_STEP)
    v_hat = v / (1.0 - ADAM_B2 ** ADAM_STEP)
    delta = -ADAM_LR * (m_hat / (jnp.sqrt(v_hat) + ADAM_EPS) + ADAM_WD * w)
    return delta, m, v


def _grad_steps(name, items, cb, carry=None):
    n = len(items)
    pieces = [len(it[0]) for it in items]

    def body(*refs):
        it = iter(refs[:sum(pieces) + 3 * n])
        outs = refs[sum(pieces) + 3 * n:]
        for i in range(n):
            accs = []
            for _ in range(pieces[i]):
                r_ref = next(it)
                acc = r_ref[0].astype(F32)
                for k in range(1, NDEV):
                    acc = acc + r_ref[k].astype(F32)
                accs.append(acc)
            acc = accs[0] if len(accs) == 1 else jnp.concatenate(accs, axis=0)
            w_ref, m_ref, v_ref = next(it), next(it), next(it)
            g_ref, d_ref, mo_ref, vo_ref = outs[4 * i:4 * i + 4]
            g_ref[...] = acc
            d, mm, vv = _adam(w_ref[...], acc, m_ref[...], v_ref[...])
            d_ref[...] = d
            mo_ref[...] = mm
            vo_ref[...] = vv

    in_specs, out_specs, out_shape, ins = [], [], [], []
    for recv, w, m, v in items:
        blk = pl.BlockSpec((w.shape[0], cb), lambda j: (0, j))
        in_specs += [pl.BlockSpec((NDEV, rp.shape[1], cb), lambda j: (0, 0, j)) for rp in recv] + [blk, blk, blk]
        out_specs += [blk] * 4
        out_shape += [jax.ShapeDtypeStruct(w.shape, F32)] * 4
        ins += list(recv) + [w, m, v]
    outs, ex = _call(body, name=name, grid=(D // cb,), in_specs=in_specs, out_specs=out_specs, out_shape=out_shape,
                     scratch_shapes=[], ins=ins, carry=carry)
    return [tuple(outs[4 * i:4 * i + 4]) for i in range(n)], ex


_SMALL = [("ln1_g", D), ("ln1_b", D), ("ln2_g", D), ("ln2_b", D), ("ln3_g", D), ("ln3_b", D),
          ("b_in", DIN), ("lb", D), ("attn_sinks", 128), ("hgrn_norm_g", 128), ("loss", 128)]
_SMALL_OFF = {}
_o = 0
for _n, _w in _SMALL:
    _SMALL_OFF[_n] = (_o, _w)
    _o += _w
PACK = _o


def _small_step(name, slots, small_w, small_m, small_v):
    names = ["ln1_g", "ln1_b", "ln2_g", "ln2_b", "ln3_g", "ln3_b", "b_in", "attn_sinks", "hgrn_lb_logits", "hgrn_norm_g"]
    np_ = len(names)

    def body(*refs):
        s_ref = refs[0]
        w_refs = refs[1:1 + np_]
        m_refs = refs[1 + np_:1 + 2 * np_]
        v_refs = refs[1 + 2 * np_:1 + 3 * np_]
        loss_ref, outs = refs[1 + 3 * np_], refs[2 + 3 * np_:]
        tot = s_ref[0]
        for k in range(1, NDEV):
            tot = tot + s_ref[k]

        def part(n):
            o, w = _SMALL_OFF[n]
            return tot[:, o:o + w]

        loss_ref[...] = part("loss")
        for i, n in enumerate(names):
            w = w_refs[i][...]
            if n == "hgrn_lb_logits":
                m_ = jnp.maximum(w[0:1, :], w[1:2, :])
                e0, e1 = jnp.exp(w[0:1, :] - m_), jnp.exp(w[1:2, :] - m_)
                p0 = e0 / (e0 + e1)
                t = p0 * (1.0 - p0) * part("lb")
                g = jnp.concatenate([t, -t], axis=0)
            elif n == "attn_sinks":
                g = part(n)[:, :N_Q_HEADS]
            else:
                g = part(n)
            d, mm, vv = _adam(w, g, m_refs[i][...], v_refs[i][...])
            outs[4 * i][...] = g
            outs[4 * i + 1][...] = d
            outs[4 * i + 2][...] = mm
            outs[4 * i + 3][...] = vv

    out_shape = [jax.ShapeDtypeStruct((1, 128), F32)]
    for n in names:
        out_shape += [jax.ShapeDtypeStruct(small_w[n].shape, F32)] * 4
    return pl.pallas_call(
        body, name=name, out_shape=out_shape,
        compiler_params=pltpu.CompilerParams(vmem_limit_bytes=VMEM_LIMIT),
    )(slots, *[small_w[n] for n in names], *[small_m[n] for n in names], *[small_v[n] for n in names]), names


def _tile(T, pref):
    return min(T, pref)


def kernel(x, ln1_g, ln1_b, ffn1_w1, ffn1_w3, ffn1_w2, ln2_g, ln2_b, w_in, b_in, attn_sinks, hgrn_lb_logits, hgrn_norm_g, w_proj_attn, w_proj_hgrn, w_out, ln3_g, ln3_b, ffn2_w1, ffn2_w3, ffn2_w2, loss_target, m_ln1_g, m_ln1_b, m_ffn1_w1, m_ffn1_w3, m_ffn1_w2, m_ln2_g, m_ln2_b, m_w_in, m_b_in, m_attn_sinks, m_hgrn_lb_logits, m_hgrn_norm_g, m_w_proj_attn, m_w_proj_hgrn, m_w_out, m_ln3_g, m_ln3_b, m_ffn2_w1, m_ffn2_w3, m_ffn2_w2, v_ln1_g, v_ln1_b, v_ffn1_w1, v_ffn1_w3, v_ffn1_w2, v_ln2_g, v_ln2_b, v_w_in, v_b_in, v_attn_sinks, v_hgrn_lb_logits, v_hgrn_norm_g, v_w_proj_attn, v_w_proj_hgrn, v_w_out, v_ln3_g, v_ln3_b, v_ffn2_w1, v_ffn2_w3, v_ffn2_w2):
    T = x.shape[1]
    xs = x[0]
    tgt = loss_target[0]
    tm = _tile(T, 256)
    tm2 = _tile(T, 512)
    tk = _tile(T, 2048)

    t_bf = lambda w: w[0].T.astype(BF16)
    n_bf = lambda w: w[0].astype(BF16)
    ffn_shard = lambda w1, w3, w2: jnp.concatenate([t_bf(w1), t_bf(w3), n_bf(w2)], axis=0)
    mix_shard = jnp.concatenate([t_bf(w_in), n_bf(w_proj_attn), n_bf(w_proj_hgrn), n_bf(w_out)], axis=0)
    (ffn1_all,) = _exchange_call("gather_ffn1", _gather_exchange(ffn_shard(ffn1_w1, ffn1_w3, ffn1_w2)))
    ffn_offs = (0, RF, 2 * RF)
    rope = _rope_table(T)

    (xh1, r1, a1, b1, xb0), (mix_all,) = _ffn_fwd("ffn1_fwd", xs, None, ffn1_all, ffn_offs, tm2,
                                                  carry=_gather_exchange(mix_shard))
    (qkv, kt, vt, hg, gates, y1b), (ffn2_all,) = _inproj_fwd(
        "inproj_fwd", xh1, ln1_g, ln1_b, mix_all, b_in, rope, tm2,
        carry=_gather_exchange(ffn_shard(ffn2_w1, ffn2_w3, ffn2_w2)))
    ya = _attn_fwd("attn_fwd", qkv, vt, attn_sinks)
    yh, sstart = _hgrn_fwd("hgrn_fwd", hg, hgrn_lb_logits, hgrn_norm_g, HG_FWD)
    xh2, r2, pa, ph, merged = _mix_fwd("mix_fwd", ya, yh, gates, xh1, ln1_g, ln1_b, mix_all, tm2)
    (xh3, r3, a2, b2, y2b, loss_part), _ = _ffn_fwd("ffn2_fwd", xh2, (ln2_g, ln2_b), ffn2_all, ffn_offs, tm2,
                                                    loss=(ln3_g, ln3_b, tgt))

    (dy2, dab2, u2, df2, dg3, db3), _ = _ffn_bwd("ffn2_bwd", ("loss", ln3_b, tgt), xh3, r3, ln3_g, a2, b2, ffn2_all,
                                                 ffn_offs, tm)
    recv = {}
    g_ffn2_13, _ = _wgrad("wgrad_ffn2_w13", dab2, y2b, DFF // 2, tk)
    g_ffn2_2, _ = _wgrad("wgrad_ffn2_w2", u2, df2, DFF // 2, tk)
    (dz2, dmix, dpa, dph, dga, dgh, dya, dyh, dg2, db2), _ = _mix_bwd(
        "mix_bwd", dy2, xh2, r2, ln2_g, gates, pa, ph, mix_all, tm2)
    g_wo, _ = _wgrad("wgrad_w_out", merged, dmix, D, tk)
    g_pa, _ = _wgrad("wgrad_w_pa", ya, dpa, D, tk)
    g_ph, _ = _wgrad("wgrad_w_ph", yh, dph, D, tk)
    (dq, dk, dv, dsink), _ = _attn_bwd("attn_bwd", qkv, kt, attn_sinks, dya)
    (dfl, dqh, dih, dog, dlb, dng), (recv["ffn2_w2"], recv["w_proj_attn"], recv["w_proj_hgrn"], recv["w_out"]) = \
        _hgrn_bwd("hgrn_bwd", hg, hgrn_lb_logits, hgrn_norm_g, sstart, dyh, HG_BWD,
                  carry=_grad_exchange([g_ffn2_2, g_pa, g_ph, g_wo], [0, 0, 0, 0], [RF, RP, RP, RP]))
    (dy1, dproj, dbin), (recv["ffn2_w1"], recv["ffn2_w3"]) = _inproj_bwd(
        "inproj_bwd", dq, dk, dv, (dfl, dqh, dih, dog), (dga, dgh), dz2, mix_all, rope, tm2,
        carry=_grad_exchange([g_ffn2_13, g_ffn2_13], [0, DFF], [RF, RF]))
    g_win, _ = _wgrad("wgrad_w_in", dproj, y1b, DIN // 4, tk)
    win_rows = (400, 288, 272)
    win_base = (0, 400, 688)
    (dz1, dab1, u1, df1, dg1, db1), (rw0,) = _ffn_bwd(
        "ffn1_bwd", ("dy", dy1), xh1, r1, ln1_g, a1, b1, ffn1_all, ffn_offs, tm2, with_dx=False,
        carry=_grad_exchange([g_win], win_base[:1], win_rows[:1], [RIN]))
    g_ffn1_2, (rw1,) = _wgrad("wgrad_ffn1_w2", u1, df1, DFF // 2, tk,
                              carry=_grad_exchange([g_win], win_base[1:2], win_rows[1:2], [RIN]))
    g_ffn1_1, (recv["ffn1_w2"],) = _wgrad("wgrad_ffn1_w1", dab1, xb0, DFF // 2, tk, 0, DFF,
                                          carry=_grad_exchange([g_ffn1_2], [0], [RF]))
    g_ffn1_3, (recv["ffn1_w1"],) = _wgrad("wgrad_ffn1_w3", dab1, xb0, DFF // 2, tk, DFF, DFF,
                                          carry=_grad_exchange([g_ffn1_1], [0], [RF]))
    parts = {"ln1_g": dg1, "ln1_b": db1, "ln2_g": dg2, "ln2_b": db2, "ln3_g": dg3, "ln3_b": db3, "b_in": dbin,
             "lb": dlb, "attn_sinks": dsink, "hgrn_norm_g": jnp.sum(dng, axis=0), "loss": loss_part[0:1, :]}
    packed = jnp.concatenate([parts[n] for n, _ in _SMALL], axis=1)
    gx, (recv["ffn1_w3"], rw2, small_slots) = _ffn_dx(
        "ffn1_dx", dab1, dz1, ffn1_all, ffn_offs, tm2,
        carry=_join(_grad_exchange([g_ffn1_3, g_win], [0, win_base[2]], [RF, win_rows[2]], [RF, RIN]),
                    _row_gather_exchange(packed)))
    recv["w_in"] = [rw0, rw1, rw2]

    big = [("ffn1_w1", ffn1_w1, m_ffn1_w1, v_ffn1_w1, True), ("ffn1_w3", ffn1_w3, m_ffn1_w3, v_ffn1_w3, True),
           ("ffn1_w2", ffn1_w2, m_ffn1_w2, v_ffn1_w2, False), ("w_in", w_in, m_w_in, v_w_in, True),
           ("w_proj_attn", w_proj_attn, m_w_proj_attn, v_w_proj_attn, False),
           ("w_proj_hgrn", w_proj_hgrn, m_w_proj_hgrn, v_w_proj_hgrn, False),
           ("w_out", w_out, m_w_out, v_w_out, False),
           ("ffn2_w1", ffn2_w1, m_ffn2_w1, v_ffn2_w1, True), ("ffn2_w3", ffn2_w3, m_ffn2_w3, v_ffn2_w3, True),
           ("ffn2_w2", ffn2_w2, m_ffn2_w2, v_ffn2_w2, False)]
    view = lambda t, transposed: t[0].T if transposed else t[0]
    back = lambda t, transposed: t.T[None] if transposed else t[None]
    slots = lambda n: recv[n] if isinstance(recv[n], list) else [recv[n]]
    stepped, _ = _grad_steps("steps", [(slots(n), view(w, tr), view(m, tr), view(v, tr)) for n, w, m, v, tr in big], 128)
    res = {n: tuple(back(t, tr) for t in outs) for (n, _, _, _, tr), outs in zip(big, stepped)}

    small_w = dict(ln1_g=ln1_g, ln1_b=ln1_b, ln2_g=ln2_g, ln2_b=ln2_b, ln3_g=ln3_g, ln3_b=ln3_b, b_in=b_in,
                   attn_sinks=attn_sinks, hgrn_lb_logits=hgrn_lb_logits, hgrn_norm_g=hgrn_norm_g)
    small_m = dict(ln1_g=m_ln1_g, ln1_b=m_ln1_b, ln2_g=m_ln2_g, ln2_b=m_ln2_b, ln3_g=m_ln3_g, ln3_b=m_ln3_b,
                   b_in=m_b_in, attn_sinks=m_attn_sinks, hgrn_lb_logits=m_hgrn_lb_logits, hgrn_norm_g=m_hgrn_norm_g)
    small_v = dict(ln1_g=v_ln1_g, ln1_b=v_ln1_b, ln2_g=v_ln2_g, ln2_b=v_ln2_b, ln3_g=v_ln3_g, ln3_b=v_ln3_b,
                   b_in=v_b_in, attn_sinks=v_attn_sinks, hgrn_lb_logits=v_hgrn_lb_logits, hgrn_norm_g=v_hgrn_norm_g)
    outs, names = _small_step("small_step", small_slots, small_w, small_m, small_v)
    loss = outs[0][0, 0]
    for i, n in enumerate(names):
        res[n] = tuple(outs[1 + 4 * i:5 + 4 * i])

    order = ["ln1_g", "ln1_b", "ffn1_w1", "ffn1_w3", "ffn1_w2", "ln2_g", "ln2_b", "w_in", "b_in", "attn_sinks",
             "hgrn_lb_logits", "hgrn_norm_g", "w_proj_attn", "w_proj_hgrn", "w_out", "ln3_g", "ln3_b",
             "ffn2_w1", "ffn2_w3", "ffn2_w2"]
    return (loss, gx[None], *[res[n][0] for n in order], *[res[n][1] for n in order],
            *[res[n][2] for n in order], *[res[n][3] for n in order])
```

```python
import jax
import jax.numpy as jnp
from jax import lax
from jax.experimental import pallas as pl
from jax.experimental.pallas import tpu as pltpu

F32 = jnp.float32
BF16 = jnp.bfloat16

NDEV = 8
D = 1024
DFF = 2816
RF = DFF // NDEV
DIN = 7680
RIN = DIN // NDEV
RP = D // NDEV
N_Q_HEADS = 16
N_KV_HEADS = 4
HEAD_DIM = 64
ATTN_BLOCK = 128
ROPE_THETA = 500000.0
ROPE_DIM = HEAD_DIM // 4
HGRN_HEADS = 8
HGRN_CHUNK = 64
ALPHA = 2.0 ** 0.25
LN_EPS = 1e-5
RMS_EPS = 1e-6
NEG_INF = -1e30
ADAM_LR = 0.001
ADAM_B1 = 0.9
ADAM_B2 = 0.999
ADAM_EPS = 1e-08
ADAM_WD = 0.01
ADAM_STEP = 10

QKV_W = 1536
HG_W = 4096
GATE_W = 2048
VMEM_LIMIT = 60 * 2 ** 20
PART_ROWS = 256
MESH = pl.DeviceIdType.MESH
HBM_SPEC = pl.BlockSpec(memory_space=pltpu.HBM)


def _params(*sem):
    return pltpu.CompilerParams(dimension_semantics=sem, vmem_limit_bytes=VMEM_LIMIT)


def _dot(a, b):
    return jnp.dot(a, b, preferred_element_type=F32)


def _dot_nt(a, b):
    return lax.dot_general(a, b, (((1,), (1,)), ((), ())), preferred_element_type=F32)


def _dot_tn(a, b):
    return lax.dot_general(a, b, (((0,), (0,)), ((), ())), preferred_element_type=F32)


def _sigmoid(x):
    return 0.5 * jnp.tanh(0.5 * x) + 0.5


def _ln_fwd(z):
    mu = jnp.mean(z, axis=-1, keepdims=True)
    zc = z - mu
    var = jnp.mean(zc * zc, axis=-1, keepdims=True)
    r = lax.rsqrt(var + LN_EPS)
    return zc * r, r


def _ln_bwd(dy, xh, r, g):
    dxh = dy * g
    m1 = jnp.mean(dxh, axis=-1, keepdims=True)
    m2 = jnp.mean(dxh * xh, axis=-1, keepdims=True)
    dz = r * (dxh - m1 - xh * m2)
    return dz, jnp.sum(dy * xh, axis=0, keepdims=True), jnp.sum(dy, axis=0, keepdims=True)


def _load_rows(wall_ref, pieces, sem):
    copies = []
    for dst, off, r in pieces:
        for k in range(NDEV):
            c = pltpu.make_async_copy(wall_ref.at[k, pl.ds(off, r), :], dst.at[pl.ds(k * r, r), :], sem.at[len(copies)])
            c.start()
            copies.append(c)
    for c in copies:
        c.wait()


class _Exchange:
    def __init__(self, inputs, out_shape, scratch, begin, middle, end):
        self.inputs, self.out_shape, self.scratch = inputs, out_shape, scratch
        self.begin, self.middle, self.end = begin, middle, end


def _gather_exchange(shard):
    rows, cols = shard.shape

    def ops(ins, outs, scr):
        (x_ref,), (out_ref,), (send_sems, recv_sems, local_sem) = ins, outs, scr
        x, y, c = lax.axis_index("x"), lax.axis_index("y"), lax.axis_index("c")
        me, sibling = (x, y, c), (x, y, 1 - c)
        chips = [(1 - x, y), (x, 1 - y), (1 - x, 1 - y)]

        def slot(px, py, pc):
            return out_ref.at[4 * px + 2 * py + pc]

        def copy(k, block, to, src=None):
            return pltpu.make_async_remote_copy(
                src_ref=slot(*block) if src is None else src, dst_ref=slot(*block),
                send_sem=send_sems.at[k], recv_sem=recv_sems.at[k], device_id=to, device_id_type=MESH)

        mine = lambda: pltpu.make_async_copy(x_ref, slot(*me), local_sem)
        first = lambda: [copy(0, me, sibling, src=x_ref)] + [
            copy(1 + j, me, (*chip, c), src=x_ref) for j, chip in enumerate(chips)]
        passed = lambda: [copy(4 + j, (*chip, c), sibling) for j, chip in enumerate(chips)]
        return c, me, sibling, chips, copy, mine, first, passed

    def begin(*refs):
        _, _, _, _, _, mine, first, _ = ops(*refs)
        mine().start()
        for cp in first():
            cp.start()

    def middle(*refs):
        c, me, _, chips, copy, _, _, passed = ops(*refs)
        for (j, chip), fwd in zip(enumerate(chips), passed()):
            copy(1 + j, (*chip, c), me).wait_recv()
            fwd.start()

    def end(*refs):
        c, me, sibling, chips, copy, mine, first, passed = ops(*refs)
        copy(0, sibling, me).wait_recv()
        for j, chip in enumerate(chips):
            copy(4 + j, (*chip, 1 - c), me).wait_recv()
        for cp in first() + passed():
            cp.wait_send()
        mine().wait()

    return _Exchange([shard], [jax.ShapeDtypeStruct((NDEV, rows, cols), shard.dtype)],
                     [pltpu.SemaphoreType.DMA((7,)), pltpu.SemaphoreType.DMA((7,)), pltpu.SemaphoreType.DMA],
                     begin, middle, end)


def _grad_exchange(grads, bases, rows, strides=None):
    n = len(grads)
    strides = rows if strides is None else strides

    def copies(g_refs, out_refs, scr):
        send_sems, recv_sems, local_sems = scr
        x, y, c = lax.axis_index("x"), lax.axis_index("y"), lax.axis_index("c")
        me = 4 * x + 2 * y + c
        out = []
        for i in range(n):
            r = rows[i]
            src = lambda k: g_refs[i].at[pl.ds(pl.multiple_of(bases[i] + k * strides[i], 16), r), :]
            out.append(pltpu.make_async_copy(src(me), out_refs[i].at[me], local_sems.at[i]))
            for j in range(1, NDEV):
                px, py, pc = x ^ (j >> 2), y ^ ((j >> 1) & 1), c ^ (j & 1)
                out.append(pltpu.make_async_remote_copy(
                    src_ref=src(4 * px + 2 * py + pc), dst_ref=out_refs[i].at[me],
                    send_sem=send_sems.at[i, j - 1], recv_sem=recv_sems.at[i, j - 1],
                    device_id=(px, py, pc), device_id_type=MESH))
        return out

    def begin(*refs):
        for cp in copies(*refs):
            cp.start()

    def end(*refs):
        for cp in copies(*refs):
            cp.wait()

    return _Exchange(list(grads), [jax.ShapeDtypeStruct((NDEV, r, g.shape[1]), g.dtype) for g, r in zip(grads, rows)],
                     [pltpu.SemaphoreType.DMA((n, NDEV - 1)), pltpu.SemaphoreType.DMA((n, NDEV - 1)),
                      pltpu.SemaphoreType.DMA((n,))], begin, None, end)


def _row_gather_exchange(row):
    def copies(ins, outs, scr):
        (r_ref,), (o_ref,), (send_sems, recv_sems, local_sem) = ins, outs, scr
        x, y, c = lax.axis_index("x"), lax.axis_index("y"), lax.axis_index("c")
        me = 4 * x + 2 * y + c
        out = [pltpu.make_async_copy(r_ref, o_ref.at[me], local_sem)]
        for j in range(1, NDEV):
            out.append(pltpu.make_async_remote_copy(
                src_ref=r_ref, dst_ref=o_ref.at[me], send_sem=send_sems.at[j - 1], recv_sem=recv_sems.at[j - 1],
                device_id=(x ^ (j >> 2), y ^ ((j >> 1) & 1), c ^ (j & 1)), device_id_type=MESH))
        return out

    def begin(*refs):
        for cp in copies(*refs):
            cp.start()

    def end(*refs):
        for cp in copies(*refs):
            cp.wait()

    return _Exchange([row], [jax.ShapeDtypeStruct((NDEV,) + row.shape, row.dtype)],
                     [pltpu.SemaphoreType.DMA((NDEV - 1,)), pltpu.SemaphoreType.DMA((NDEV - 1,)), pltpu.SemaphoreType.DMA],
                     begin, None, end)


def _join(a, b):
    na, oa, sa = len(a.inputs), len(a.out_shape), len(a.scratch)
    split = lambda ins, outs, scr: ((ins[:na], outs[:oa], scr[:sa]), (ins[na:], outs[oa:], scr[sa:]))

    def begin(*refs):
        pa, pb = split(*refs)
        a.begin(*pa)
        b.begin(*pb)

    def end(*refs):
        pa, pb = split(*refs)
        a.end(*pa)
        b.end(*pb)

    return _Exchange(a.inputs + b.inputs, a.out_shape + b.out_shape, a.scratch + b.scratch, begin, None, end)


def _exchange_call(name, ex):
    ni, no = len(ex.inputs), len(ex.out_shape)

    def body(*refs):
        parts = (refs[:ni], refs[ni:ni + no], refs[ni + no:])
        ex.begin(*parts)
        if ex.middle is not None:
            ex.middle(*parts)
        ex.end(*parts)

    return pl.pallas_call(body, name=name, out_shape=ex.out_shape, in_specs=[HBM_SPEC] * ni, out_specs=[HBM_SPEC] * no,
                          scratch_shapes=ex.scratch)(*ex.inputs)


def _call(body, *, name, grid, in_specs, out_specs, out_shape, scratch_shapes, ins, carry=None):
    sem = ("arbitrary",) * len(grid)
    if carry is None:
        outs = pl.pallas_call(body, name=name, grid=grid, in_specs=in_specs, out_specs=out_specs, out_shape=out_shape,
                              scratch_shapes=scratch_shapes, compiler_params=_params(*sem))(*ins)
        return outs, None
    n_in, n_out, n_scr = len(ins), len(out_shape), len(scratch_shapes)
    ci, co = len(carry.inputs), len(carry.out_shape)
    total = 1
    for g in grid:
        total *= g

    def wrapped(*refs):
        own_in, ex_in = refs[:n_in], refs[n_in:n_in + ci]
        o0 = n_in + ci
        own_out, ex_out = refs[o0:o0 + n_out], refs[o0 + n_out:o0 + n_out + co]
        s0 = o0 + n_out + co
        own_scr, ex_scr = refs[s0:s0 + n_scr], refs[s0 + n_scr:]
        step = pl.program_id(0)
        for d in range(1, len(grid)):
            step = step * grid[d] + pl.program_id(d)
        parts = (ex_in, ex_out, ex_scr)
        pl.when(step == 0)(lambda: carry.begin(*parts))
        body(*own_in, *own_out, *own_scr)
        if carry.middle is not None:
            pl.when(step == (3 * total) // 4)(lambda: carry.middle(*parts))
        pl.when(step == total - 1)(lambda: carry.end(*parts))

    outs = pl.pallas_call(
        wrapped, name=name, grid=grid, in_specs=list(in_specs) + [HBM_SPEC] * ci,
        out_specs=list(out_specs) + [HBM_SPEC] * co, out_shape=list(out_shape) + list(carry.out_shape),
        scratch_shapes=list(scratch_shapes) + list(carry.scratch), compiler_params=_params(*sem),
    )(*ins, *carry.inputs)
    return outs[:n_out], outs[n_out:]


def _ffn_fwd(name, xin, affine, wall, offs, tm, loss=None, carry=None):
    T = xin.shape[0]
    nt = T // tm

    def body(*refs):
        it = iter(refs)
        x_ref = next(it)
        if affine is not None:
            g_ref, b_ref = next(it), next(it)
        wall_ref = next(it)
        if loss is not None:
            go_ref, bo_ref, tgt_ref = next(it), next(it), next(it)
        xh_ref, r_ref, a_ref, b2_ref, yb_ref = (next(it) for _ in range(5))
        if loss is not None:
            loss_ref = next(it)
        w1, w3, w2, sem = (next(it) for _ in range(4))

        @pl.when(pl.program_id(0) == 0)
        def _():
            _load_rows(wall_ref, [(w1, offs[0], RF), (w3, offs[1], RF), (w2, offs[2], RF)], sem)
            if loss is not None:
                loss_ref[...] = jnp.zeros_like(loss_ref)

        parts = [slice(p, min(p + PART_ROWS, tm)) for p in range(0, tm, PART_ROWS)]
        ys, ybs = [], []
        for rs in parts:
            y = x_ref[rs, :]
            if affine is not None:
                y = y * g_ref[...] + b_ref[...]
            ys.append(y)
            ybs.append(y.astype(BF16))
            yb_ref[rs, :] = ybs[-1]
        ab = [(_dot_nt(yb, w1[...]).astype(BF16), _dot_nt(yb, w3[...]).astype(BF16)) for yb in ybs]
        us = []
        for rs, (a, b) in zip(parts, ab):
            a_ref[rs, :] = a
            b2_ref[rs, :] = b
            af, bf = a.astype(F32), b.astype(F32)
            us.append((af * _sigmoid(af) * bf).astype(BF16))
        fs = [_dot(u, w2[...]) for u in us]
        for rs, y, f in zip(parts, ys, fs):
            xh, r = _ln_fwd(ALPHA * y + 0.5 * f)
            xh_ref[rs, :] = xh
            r_ref[rs, :] = jnp.broadcast_to(r, (xh.shape[0], 128))
            if loss is not None:
                e = xh * go_ref[...] + bo_ref[...] - tgt_ref[rs, :]
                loss_ref[...] += jnp.sum(e * e) * (0.5 / D)

    row = lambda w: pl.BlockSpec((tm, w), lambda i: (i, 0))
    vec = pl.BlockSpec((1, D), lambda i: (0, 0))
    ins, in_specs = [xin], [row(D)]
    if affine is not None:
        ins += list(affine)
        in_specs += [vec, vec]
    ins.append(wall)
    in_specs.append(HBM_SPEC)
    if loss is not None:
        ins += list(loss)
        in_specs += [vec, vec, row(D)]
    out_shape = [jax.ShapeDtypeStruct((T, D), F32), jax.ShapeDtypeStruct((T, 128), F32),
                 jax.ShapeDtypeStruct((T, DFF), BF16), jax.ShapeDtypeStruct((T, DFF), BF16),
                 jax.ShapeDtypeStruct((T, D), BF16)]
    out_specs = [row(D), row(128), row(DFF), row(DFF), row(D)]
    if loss is not None:
        out_shape.append(jax.ShapeDtypeStruct((8, 128), F32))
        out_specs.append(pl.BlockSpec((8, 128), lambda i: (0, 0)))
    return _call(body, name=name, grid=(nt,), in_specs=in_specs, out_specs=out_specs, out_shape=out_shape,
                 scratch_shapes=[pltpu.VMEM((DFF, D), BF16)] * 3 + [pltpu.SemaphoreType.DMA((3 * NDEV,))],
                 ins=ins, carry=carry)


def _ffn_bwd(name, dy_src, xh, r, g, a, b, wall, offs, tm, with_dx=True, carry=None):
    T = xh.shape[0]
    nt = T // tm
    from_loss = dy_src[0] == "loss"

    def body(*refs):
        it = iter(refs)
        if from_loss:
            bo_ref, tgt_ref = next(it), next(it)
        else:
            dy_ref = next(it)
        xh_ref, r_ref, g_ref, a_ref, b_ref, wall_ref = (next(it) for _ in range(6))
        dyin_ref, dab_ref, u_ref, df_ref, dg_ref, db_ref = (next(it) for _ in range(6))
        if with_dx:
            w1, w3 = next(it), next(it)
        w2, sem = next(it), next(it)

        @pl.when(pl.program_id(0) == 0)
        def _():
            _load_rows(wall_ref, ([(w1, offs[0], RF), (w3, offs[1], RF)] if with_dx else []) + [(w2, offs[2], RF)], sem)
            dg_ref[...] = jnp.zeros_like(dg_ref)
            db_ref[...] = jnp.zeros_like(db_ref)

        parts = [slice(p, min(p + PART_ROWS, tm)) for p in range(0, tm, PART_ROWS)]
        gv = g_ref[...]
        dz, df = [], []
        for rs in parts:
            xhv = xh_ref[rs, :]
            if from_loss:
                dy = (xhv * gv + bo_ref[...] - tgt_ref[rs, :]) * (1.0 / D)
            else:
                dy = dy_ref[rs, :]
            dzp, dgp, dbp = _ln_bwd(dy, xhv, r_ref[rs, :1], gv)
            dg_ref[...] += dgp
            db_ref[...] += dbp
            dz.append(dzp)
            df.append((0.5 * dzp).astype(BF16))
            df_ref[rs, :] = df[-1]
        du = [_dot_nt(d, w2[...]) for d in df]
        da, dbb = [], []
        for rs, dup in zip(parts, du):
            af, bf = a_ref[rs, :].astype(F32), b_ref[rs, :].astype(F32)
            s = _sigmoid(af)
            sl = af * s
            u_ref[rs, :] = (sl * bf).astype(BF16)
            da.append((dup * bf * (s * (1.0 + af * (1.0 - s)))).astype(BF16))
            dbb.append((dup * sl).astype(BF16))
            dab_ref[rs, :DFF] = da[-1]
            dab_ref[rs, DFF:] = dbb[-1]
        for rs, dzp, dap, dbp in zip(parts, dz, da, dbb):
            dyin_ref[rs, :] = ALPHA * dzp + _dot(dap, w1[...]) + _dot(dbp, w3[...]) if with_dx else dzp

    row = lambda w: pl.BlockSpec((tm, w), lambda i: (i, 0))
    vec = pl.BlockSpec((1, D), lambda i: (0, 0))
    if from_loss:
        ins, in_specs = [dy_src[1], dy_src[2]], [vec, row(D)]
    else:
        ins, in_specs = [dy_src[1]], [row(D)]
    ins += [xh, r, g, a, b, wall]
    in_specs += [row(D), row(128), vec, row(DFF), row(DFF), HBM_SPEC]
    return _call(
        body, name=name, grid=(nt,), in_specs=in_specs,
        out_specs=[row(D), row(2 * DFF), row(DFF), row(D), vec, vec],
        out_shape=[jax.ShapeDtypeStruct((T, D), F32), jax.ShapeDtypeStruct((T, 2 * DFF), BF16),
                   jax.ShapeDtypeStruct((T, DFF), BF16), jax.ShapeDtypeStruct((T, D), BF16),
                   jax.ShapeDtypeStruct((1, D), F32), jax.ShapeDtypeStruct((1, D), F32)],
        scratch_shapes=[pltpu.VMEM((DFF, D), BF16)] * (3 if with_dx else 1) + [pltpu.SemaphoreType.DMA((3 * NDEV,))],
        ins=ins, carry=carry)


def _ffn_dx(name, dab, dz, wall, offs, tm, carry=None):
    T = dz.shape[0]

    def body(dab_ref, dz_ref, wall_ref, o_ref, w13, sem):
        @pl.when(pl.program_id(0) == 0)
        def _():
            _load_rows(wall_ref, [(w13.at[pl.ds(0, DFF), :], offs[0], RF), (w13.at[pl.ds(DFF, DFF), :], offs[1], RF)], sem)

        for p in range(0, tm, PART_ROWS):
            rs = slice(p, min(p + PART_ROWS, tm))
            o_ref[rs, :] = ALPHA * dz_ref[rs, :] + _dot(dab_ref[rs, :], w13[...])

    row = lambda w: pl.BlockSpec((tm, w), lambda i: (i, 0))
    (out,), ex = _call(
        body, name=name, grid=(T // tm,), in_specs=[row(2 * DFF), row(D), HBM_SPEC], out_specs=[row(D)],
        out_shape=[jax.ShapeDtypeStruct((T, D), F32)],
        scratch_shapes=[pltpu.VMEM((2 * DFF, D), BF16), pltpu.SemaphoreType.DMA((2 * NDEV,))],
        ins=[dab, dz, wall], carry=carry)
    return out, ex


def _wgrad(name, a, b, bn, tk, col0=0, ncols=None, carry=None):
    T = a.shape[0]
    N = a.shape[1] if ncols is None else ncols
    nk = T // tk
    c0 = col0 // bn

    def body(a_ref, b_ref, o_ref, acc):
        k = pl.program_id(1)

        @pl.when(k == 0)
        def _():
            acc[...] = jnp.zeros_like(acc)

        acc[...] += _dot_tn(a_ref[...], b_ref[...])

        @pl.when(k == nk - 1)
        def _():
            o_ref[...] = acc[...].astype(BF16)

    (out,), ex = _call(
        body, name=name, grid=(N // bn, nk),
        in_specs=[pl.BlockSpec((tk, bn), lambda n, k: (k, n + c0)), pl.BlockSpec((tk, D), lambda n, k: (k, 0))],
        out_specs=[pl.BlockSpec((bn, D), lambda n, k: (n, 0))],
        out_shape=[jax.ShapeDtypeStruct((N, D), BF16)],
        scratch_shapes=[pltpu.VMEM((bn, D), F32)], ins=[a, b], carry=carry)
    return out, ex


def _rope_table(T):
    pos = jnp.arange(T, dtype=F32)
    inv_freq = ROPE_THETA ** (-jnp.arange(0, ROPE_DIM, 2, dtype=F32) / ROPE_DIM)
    ang = pos[:, None] * inv_freq[None, :]
    return jnp.pad(jnp.concatenate([jnp.cos(ang), jnp.sin(ang)], axis=1), ((0, 0), (0, 128 - ROPE_DIM)))


def _rope_expand(cs):
    half = ROPE_DIM // 2
    lane = lax.broadcasted_iota(jnp.int32, cs.shape, 1)
    cos = jnp.where(lane < half, cs, 0.0)
    sin = jnp.where((lane >= half) & (lane < ROPE_DIM), cs, 0.0)
    both = lambda t: t + pltpu.roll(t, HEAD_DIM, 1)
    c = jnp.where(jnp.bitwise_and(lane, HEAD_DIM - 1) < ROPE_DIM, both(cos + pltpu.roll(cos, half, 1)), 1.0)
    return c, -both(pltpu.roll(sin, 128 - half, 1)), both(sin)


def _rope(t, c, s1, s2):
    n = t.shape[1] // 128
    ct, s1t, s2t = (jnp.tile(v, (1, n)) for v in (c, s1, s2))
    w = t.shape[1]
    return t * ct + pltpu.roll(t, w - 8, 1) * s1t + pltpu.roll(t, 8, 1) * s2t


def _rope_t(dr, c, s1, s2):
    n = dr.shape[1] // 128
    ct, s1t, s2t = (jnp.tile(v, (1, n)) for v in (c, s1, s2))
    w = dr.shape[1]
    return dr * ct + pltpu.roll(dr * s1t, 8, 1) + pltpu.roll(dr * s2t, w - 8, 1)


_Q, _K, _V = (0, 1024), (1024, 256), (1280, 256)
_HG = (1536, HG_W)
_GATES = (5632, GATE_W)


def _inproj_fwd(name, xh, g, b, wall, b_in, rope, tm, carry=None):
    T = xh.shape[0]

    def body(xh_ref, g_ref, b_ref, wall_ref, bin_ref, cs_ref,
             qkv_ref, kt_ref, vt_ref, hg_ref, gate_ref, yb_ref, w, sem):
        @pl.when(pl.program_id(0) == 0)
        def _():
            _load_rows(wall_ref, [(w, 0, RIN)], sem)

        yb = (xh_ref[...] * g_ref[...] + b_ref[...]).astype(BF16)
        yb_ref[...] = yb
        c, s1, s2 = _rope_expand(cs_ref[...])

        def piece(start, width):
            return _dot_nt(yb, w[start:start + width, :]) + bin_ref[:, start:start + width]

        k = _rope(piece(*_K), c, s1, s2)
        v = piece(*_V)
        qkv_ref[:, 0:1024] = _rope(piece(*_Q), c, s1, s2).astype(BF16)
        qkv_ref[:, 1024:1280] = k.astype(BF16)
        qkv_ref[:, 1280:1536] = v.astype(BF16)
        kt_ref[...] = k.T.astype(BF16)
        vt_ref[...] = v.T.astype(BF16)
        for j in range(4):
            hg_ref[:, 1024 * j:1024 * (j + 1)] = piece(_HG[0] + 1024 * j, 1024)
        for j in range(2):
            gate_ref[:, 1024 * j:1024 * (j + 1)] = piece(_GATES[0] + 1024 * j, 1024).astype(BF16)

    row = lambda wd: pl.BlockSpec((tm, wd), lambda i: (i, 0))
    vec = lambda wd: pl.BlockSpec((1, wd), lambda i: (0, 0))
    colt = pl.BlockSpec((256, tm), lambda i: (0, i))
    return _call(
        body, name=name, grid=(T // tm,),
        in_specs=[row(D), vec(D), vec(D), HBM_SPEC, vec(DIN), row(128)],
        out_specs=[row(QKV_W), colt, colt, row(HG_W), row(GATE_W), row(D)],
        out_shape=[jax.ShapeDtypeStruct((T, QKV_W), BF16), jax.ShapeDtypeStruct((256, T), BF16),
                   jax.ShapeDtypeStruct((256, T), BF16), jax.ShapeDtypeStruct((T, HG_W), F32),
                   jax.ShapeDtypeStruct((T, GATE_W), BF16), jax.ShapeDtypeStruct((T, D), BF16)],
        scratch_shapes=[pltpu.VMEM((DIN, D), BF16), pltpu.SemaphoreType.DMA((NDEV,))],
        ins=[xh, g, b, wall, b_in, rope], carry=carry)


def _inproj_bwd(name, dq, dk, dv, dhg, dgates, dz2, wall, rope, tm, carry=None):
    T = dq.shape[0]

    def body(dq_ref, dk_ref, dv_ref, d0, d1, d2, d3, dga_ref, dgh_ref, dz_ref, wall_ref, cs_ref,
             dy_ref, dproj_ref, dbin_ref, w, sem):
        @pl.when(pl.program_id(0) == 0)
        def _():
            _load_rows(wall_ref, [(w, 0, RIN)], sem)
            dbin_ref[...] = jnp.zeros_like(dbin_ref)

        c, s1, s2 = _rope_expand(cs_ref[...])
        acc = ALPHA * dz_ref[...]
        pieces = [(_Q[0], _rope_t(dq_ref[...], c, s1, s2)), (_K[0], _rope_t(dk_ref[...], c, s1, s2)),
                  (_V[0], dv_ref[...])]
        pieces += [(_HG[0] + 1024 * j, r[...]) for j, r in enumerate((d0, d1, d2, d3))]
        pieces += [(_GATES[0], dga_ref[...]), (_GATES[0] + 1024, dgh_ref[...])]
        for start, val in pieces:
            width = val.shape[1]
            dbin_ref[:, start:start + width] += jnp.sum(val.astype(F32), axis=0, keepdims=True)
            vb = val.astype(BF16)
            dproj_ref[:, start:start + width] = vb
            acc = acc + _dot(vb, w[start:start + width, :])
        dy_ref[...] = acc

    row = lambda wd: pl.BlockSpec((tm, wd), lambda i: (i, 0))
    return _call(
        body, name=name, grid=(T // tm,),
        in_specs=[row(D), row(256), row(256)] + [row(D)] * 4 + [row(D), row(D), row(D), HBM_SPEC, row(128)],
        out_specs=[row(D), row(DIN), pl.BlockSpec((1, DIN), lambda i: (0, 0))],
        out_shape=[jax.ShapeDtypeStruct((T, D), F32), jax.ShapeDtypeStruct((T, DIN), BF16),
                   jax.ShapeDtypeStruct((1, DIN), F32)],
        scratch_shapes=[pltpu.VMEM((DIN, D), BF16), pltpu.SemaphoreType.DMA((NDEV,))],
        ins=[dq, dk, dv, *dhg, *dgates, dz2, wall, rope], carry=carry)


def _halves(t):
    lane = lax.broadcasted_iota(jnp.int32, t.shape, 1)
    low = lane < HEAD_DIM
    sw = pltpu.roll(t, HEAD_DIM, 1)
    zero = jnp.zeros_like(t)
    h0 = (jnp.where(low, t, zero), jnp.where(low, zero, sw))
    h1 = (jnp.where(low, sw, zero), jnp.where(low, zero, t))
    return h0, h1


def _lane_stack(p_ref, c_ref, gp):
    sl = slice(128 * gp, 128 * (gp + 1))
    hp, hc = _halves(p_ref[:, sl].astype(F32)), _halves(c_ref[:, sl].astype(F32))
    return [jnp.concatenate([hp[gg][0], hc[gg][0], hp[gg][1], hc[gg][1]], axis=0).astype(BF16) for gg in range(2)]


def _row_stack(tp_ref, tc_ref, g):
    band = jnp.concatenate([tp_ref[64 * g:64 * (g + 1), :], tc_ref[64 * g:64 * (g + 1), :]], axis=1)
    z = jnp.zeros_like(band)
    return [jnp.concatenate([band, z], axis=0), jnp.concatenate([z, band], axis=0)]


def _fold(prevm, t4, hh):
    return jnp.where(prevm, t4[256 * hh:256 * hh + 128, :], t4[256 * hh + 128:256 * hh + 256, :])


def _unfold(prevm, t):
    return jnp.concatenate([jnp.where(prevm, t, 0.0), jnp.where(prevm, 0.0, t)], axis=0).astype(BF16)


def _attn_softmax(s, kill, sink):
    s = jnp.where(kill, NEG_INF, s)
    m = jnp.maximum(jnp.max(s, axis=0, keepdims=True), sink)
    p = jnp.exp(s - m)
    es = jnp.exp(sink - m)
    inv = 1.0 / (jnp.sum(p, axis=0, keepdims=True) + es)
    return p * inv, es * inv


def _attn_masks(first):
    row = lax.broadcasted_iota(jnp.int32, (ATTN_BLOCK, 2 * ATTN_BLOCK), 0)
    lane = lax.broadcasted_iota(jnp.int32, (ATTN_BLOCK, 2 * ATTN_BLOCK), 1)
    prevm = row > lane % ATTN_BLOCK
    return prevm, jnp.logical_and(first, prevm)


def _pair_rows(ref, g):
    return jnp.concatenate([ref[:, 256 * g:256 * g + 128], ref[:, 256 * g + 128:256 * (g + 1)]], axis=0)


def _pair_sinks(sink_ref, g, hh):
    h0, h1 = 4 * g + hh, 4 * g + 2 + hh
    return jnp.concatenate([jnp.broadcast_to(sink_ref[:, h0:h0 + 1], (1, ATTN_BLOCK)),
                            jnp.broadcast_to(sink_ref[:, h1:h1 + 1], (1, ATTN_BLOCK))], axis=1)


def _attn_fwd(name, qkv, vt, sinks):
    T = qkv.shape[0]
    nbk = min(ATTN_TILE, T // ATTN_BLOCK)
    rows = nbk * ATTN_BLOCK
    scale = HEAD_DIM ** -0.5

    def block(q_ref, kp_ref, kc_ref, vtp_ref, vtc_ref, sink_ref, o_ref, first):
        prevm, kill = _attn_masks(first)
        kst = _lane_stack(kp_ref, kc_ref, 0) + _lane_stack(kp_ref, kc_ref, 1)
        vts = [_row_stack(vtp_ref, vtc_ref, g) for g in range(N_KV_HEADS)]
        s8 = [_dot_nt(kst[g], _pair_rows(q_ref, g)) * scale for g in range(N_KV_HEADS)]
        pu = [[_unfold(prevm, _attn_softmax(_fold(prevm, s8[g], hh), kill, _pair_sinks(sink_ref, g, hh))[0])
               for hh in range(2)] for g in range(N_KV_HEADS)]
        for g in range(N_KV_HEADS):
            ot = _dot(vts[g][0], pu[g][0]) + _dot(vts[g][1], pu[g][1])
            o_ref[:, 256 * g:256 * g + 128] = ot[:, :ATTN_BLOCK].T.astype(BF16)
            o_ref[:, 256 * g + 128:256 * (g + 1)] = ot[:, ATTN_BLOCK:].T.astype(BF16)

    def body(q_ref, kp_ref, kc_ref, vtp_ref, vtc_ref, sink_ref, o_ref):
        for b in range(nbk):
            rs = slice(ATTN_BLOCK * b, ATTN_BLOCK * (b + 1))
            ps = slice(ATTN_BLOCK * (b - 1), ATTN_BLOCK * b)
            block(q_ref.at[rs, :], kp_ref if b == 0 else kc_ref.at[ps, :], kc_ref.at[rs, :],
                  vtp_ref if b == 0 else vtc_ref.at[:, ps], vtc_ref.at[:, rs], sink_ref, o_ref.at[rs, :],
                  pl.program_id(0) == 0 if b == 0 else False)

    prev = lambda i: jnp.maximum(nbk * i - 1, 0)
    return pl.pallas_call(
        body, name=name, grid=(T // rows,),
        in_specs=[pl.BlockSpec((rows, D), lambda i: (i, 0)),
                  pl.BlockSpec((ATTN_BLOCK, 256), lambda i: (prev(i), 4)),
                  pl.BlockSpec((rows, 256), lambda i: (i, 4)),
                  pl.BlockSpec((256, ATTN_BLOCK), lambda i: (0, prev(i))),
                  pl.BlockSpec((256, rows), lambda i: (0, i)),
                  pl.BlockSpec((1, N_Q_HEADS), lambda i: (0, 0))],
        out_specs=pl.BlockSpec((rows, D), lambda i: (i, 0)),
        out_shape=jax.ShapeDtypeStruct((T, D), BF16),
        compiler_params=_params("arbitrary"),
    )(qkv, qkv, qkv, vt, vt, sinks)


ATTN_TILE = 4
ATTN_BWD_ORDER = ((0, 0), (0, 1), (1, 0), (0, 2), (1, 1), (2, 0), (0, 3), (1, 2), (2, 1), (1, 3), (2, 2), (2, 3))


def _attn_bwd(name, qkv, kt, sinks, dya, carry=None):
    T = qkv.shape[0]
    nbk = min(ATTN_TILE, T // ATTN_BLOCK)
    rows = nbk * ATTN_BLOCK
    scale = HEAD_DIM ** -0.5

    def block(q_ref, kp_ref, kc_ref, vp_ref, vc_ref, ktp_ref, ktc_ref, sink_ref, do_ref, dq_ref, first):
        prevm, kill = _attn_masks(first)
        kst = _lane_stack(kp_ref, kc_ref, 0) + _lane_stack(kp_ref, kc_ref, 1)
        vst = _lane_stack(vp_ref, vc_ref, 0) + _lane_stack(vp_ref, vc_ref, 1)
        kts = [_row_stack(ktp_ref, ktc_ref, g) for g in range(N_KV_HEADS)]
        lane = lax.broadcasted_iota(jnp.int32, (2 * ATTN_BLOCK, 128), 1)
        low = lane < HEAD_DIM
        slane = lax.broadcasted_iota(jnp.int32, (1, 128), 1)
        dkz = [jnp.zeros((2 * ATTN_BLOCK, 128), F32) for _ in range(N_KV_HEADS)]
        dvz = [jnp.zeros((2 * ATTN_BLOCK, 128), F32) for _ in range(N_KV_HEADS)]
        dsink = jnp.zeros((1, 128), F32)
        groups = range(N_KV_HEADS)
        qcat, docat, s8, dp8 = {}, {}, {}, {}
        ds_u, p_u, dsinks = {}, {}, []

        def scores(g):
            qcat[g], docat[g] = _pair_rows(q_ref, g), _pair_rows(do_ref, g)
            s8[g] = _dot_nt(kst[g], qcat[g]) * scale
            dp8[g] = _dot_nt(vst[g], docat[g])

        def algebra(g):
            for hh in range(2):
                pn, ps = _attn_softmax(_fold(prevm, s8[g], hh), kill, _pair_sinks(sink_ref, g, hh))
                dp = _fold(prevm, dp8[g], hh)
                delta = jnp.sum(pn * dp, axis=0, keepdims=True)
                sd = ps * delta
                dsinks.append(jnp.where(slane == 4 * g + hh, -jnp.sum(sd[:, :ATTN_BLOCK]), 0.0)
                              + jnp.where(slane == 4 * g + 2 + hh, -jnp.sum(sd[:, ATTN_BLOCK:]), 0.0))
                ds_u[g, hh] = _unfold(prevm, pn * (dp - delta) * scale)
                p_u[g, hh] = _unfold(prevm, pn)

        def grads(g):
            dqt = _dot(kts[g][0], ds_u[g, 0]) + _dot(kts[g][1], ds_u[g, 1])
            dq_ref[:, 256 * g:256 * g + 128] = dqt[:, :ATTN_BLOCK].T
            dq_ref[:, 256 * g + 128:256 * (g + 1)] = dqt[:, ATTN_BLOCK:].T
            for hh in range(2):
                own = low if hh == 0 else jnp.logical_not(low)
                dk_h = jnp.where(own, _dot(ds_u[g, hh], qcat[g]), 0.0)
                dv_h = jnp.where(own, _dot(p_u[g, hh], docat[g]), 0.0)
                if hh != g % 2:
                    dk_h = pltpu.roll(dk_h, HEAD_DIM, 1)
                    dv_h = pltpu.roll(dv_h, HEAD_DIM, 1)
                dkz[g] = dkz[g] + dk_h
                dvz[g] = dvz[g] + dv_h

        for stage, g in ATTN_BWD_ORDER:
            (scores, algebra, grads)[stage](g)
        for t in dsinks:
            dsink = dsink + t
        pairs = range(N_KV_HEADS // 2)
        return dsink, [dkz[2 * gp] + dkz[2 * gp + 1] for gp in pairs], [dvz[2 * gp] + dvz[2 * gp + 1] for gp in pairs]

    def body(q_ref, kp_ref, kc_ref, vp_ref, vc_ref, ktp_ref, ktc_ref, sink_ref, do_ref,
             dq_ref, dk_ref, dv_ref, ds_ref):
        i = pl.program_id(0)

        @pl.when(i == 0)
        def _():
            ds_ref[...] = jnp.zeros_like(ds_ref)

        res = []
        for b in range(nbk):
            rs = slice(ATTN_BLOCK * b, ATTN_BLOCK * (b + 1))
            ps = slice(ATTN_BLOCK * (b - 1), ATTN_BLOCK * b)
            res.append(block(
                q_ref.at[rs, :], kp_ref if b == 0 else kc_ref.at[ps, :], kc_ref.at[rs, :],
                vp_ref if b == 0 else vc_ref.at[ps, :], vc_ref.at[rs, :],
                ktp_ref if b == 0 else ktc_ref.at[:, ps], ktc_ref.at[:, rs], sink_ref, do_ref.at[rs, :],
                dq_ref.at[rs, :], i == 0 if b == 0 else False))
        dsink = res[0][0]
        for r in res[1:]:
            dsink = dsink + r[0]
        ds_ref[...] += dsink
        for gp in range(N_KV_HEADS // 2):
            cols = slice(128 * gp, 128 * (gp + 1))
            for b in range(nbk):
                cur = pl.ds(pl.multiple_of(i * rows + ATTN_BLOCK * b, ATTN_BLOCK), ATTN_BLOCK)
                dkb, dvb = res[b][1][gp][ATTN_BLOCK:, :], res[b][2][gp][ATTN_BLOCK:, :]
                if b + 1 < nbk:
                    dkb, dvb = dkb + res[b + 1][1][gp][:ATTN_BLOCK, :], dvb + res[b + 1][2][gp][:ATTN_BLOCK, :]
                dk_ref[cur, cols] = dkb
                dv_ref[cur, cols] = dvb

            @pl.when(i > 0)
            def _():
                prv = pl.ds(pl.multiple_of(jnp.maximum(i * rows - ATTN_BLOCK, 0), ATTN_BLOCK), ATTN_BLOCK)
                dk_ref[prv, cols] += res[0][1][gp][:ATTN_BLOCK, :]
                dv_ref[prv, cols] += res[0][2][gp][:ATTN_BLOCK, :]

    prev = lambda i: jnp.maximum(nbk * i - 1, 0)
    whole = lambda w: pl.BlockSpec((T, w), lambda i: (0, 0))
    return _call(
        body, name=name, grid=(T // rows,), scratch_shapes=[], ins=[qkv, qkv, qkv, qkv, qkv, kt, kt, sinks, dya],
        carry=carry,
        in_specs=[pl.BlockSpec((rows, D), lambda i: (i, 0)),
                  pl.BlockSpec((ATTN_BLOCK, 256), lambda i: (prev(i), 4)),
                  pl.BlockSpec((rows, 256), lambda i: (i, 4)),
                  pl.BlockSpec((ATTN_BLOCK, 256), lambda i: (prev(i), 5)),
                  pl.BlockSpec((rows, 256), lambda i: (i, 5)),
                  pl.BlockSpec((256, ATTN_BLOCK), lambda i: (0, prev(i))),
                  pl.BlockSpec((256, rows), lambda i: (0, i)),
                  pl.BlockSpec((1, N_Q_HEADS), lambda i: (0, 0)),
                  pl.BlockSpec((rows, D), lambda i: (i, 0))],
        out_specs=[pl.BlockSpec((rows, D), lambda i: (i, 0)), whole(256), whole(256),
                   pl.BlockSpec((1, 128), lambda i: (0, 0))],
        out_shape=[jax.ShapeDtypeStruct((T, D), F32), jax.ShapeDtypeStruct((T, 256), F32),
                   jax.ShapeDtypeStruct((T, 256), F32), jax.ShapeDtypeStruct((1, 128), F32)])


HG_FWD = (256, 8, 512)
HG_BWD = (128, 8, 512)


def _chunk_sum(tri, x):
    w = x.shape[1]
    h1 = x.astype(BF16)
    h2 = (x - h1.astype(F32)).astype(BF16)
    r = _dot(tri, jnp.concatenate([h1, h2], axis=1))
    return r[:, :w] + r[:, w:]


def _chunk_tri(n, upper=False):
    ri = lax.broadcasted_iota(jnp.int32, (n, n), 0)
    ci = lax.broadcasted_iota(jnp.int32, (n, n), 1)
    same_chunk = jnp.bitwise_xor(ri, ci) < HGRN_CHUNK
    return jnp.where(((ri <= ci) if upper else (ri >= ci)) & same_chunk, 1.0, 0.0).astype(BF16)


def _chunks(t):
    return [t[HGRN_CHUNK * c:HGRN_CHUNK * (c + 1), :] for c in range(t.shape[0] // HGRN_CHUNK)]


def _lower_bound(lbl_ref):
    l0, l1 = lbl_ref[0:1, :], lbl_ref[1:2, :]
    m = jnp.maximum(l0, l1)
    e0, e1 = jnp.exp(l0 - m), jnp.exp(l1 - m)
    return e0 / (e0 + e1)


def _heads(t):
    return [t[:, 128 * h:128 * (h + 1)] for h in range(t.shape[1] // 128)]


def _per_head(fn, *wide):
    return jnp.concatenate([fn(*parts) for parts in zip(*[_heads(t) for t in wide])], axis=1)


def _hgrn_sub(fl, qh, vv, lb, tril_b):
    w = fl.shape[1]
    sg = _sigmoid(fl)
    f = lb + (1.0 - lb) * sg
    k = 1.0 - f
    gc = _chunk_sum(tril_b, jnp.log(f))
    gl_c = [t[HGRN_CHUNK - 1:HGRN_CHUNK, :] for t in _chunks(gc)]
    gl = jnp.concatenate([jnp.broadcast_to(g, (HGRN_CHUNK, w)) for g in gl_c], axis=0)
    sq = _sigmoid(qh)
    eg = jnp.exp(gc)
    eng = jnp.exp(-gc)
    elg = jnp.exp(gl - gc)
    qd = qh * sq * eg
    ki = k * eng
    ke = k * elg
    qd_b, ki_b, ke_b, v_b = (t.astype(BF16) for t in (qd, ki, ke, vv))
    am_b = [jnp.where(tril_b > 0, _dot_nt(qq, kk), 0.0).astype(BF16) for qq, kk in zip(_heads(qd_b), _heads(ki_b))]
    return dict(sg=sg, f=f, sq=sq, eg=eg, eng=eng, elg=elg, qd=qd, ki=ki, ke=ke, egl=[jnp.exp(g) for g in gl_c],
                am_b=am_b, qd_b=qd_b, ki_b=ki_b, ke_b=ke_b, v_b=v_b)


def _hgrn_out(q, st_b):
    outs = []
    for h, (qd_h, v_h) in enumerate(zip(_heads(q["qd_b"]), _heads(q["v_b"]))):
        inter = [_dot_nt(qc, s) for qc, s in zip(_chunks(qd_h), st_b[h])]
        outs.append(_dot(q["am_b"][h], v_h) + jnp.concatenate(inter, axis=0))
    return jnp.concatenate(outs, axis=1)


def _head_rms(o):
    return _per_head(lambda t: jnp.broadcast_to(
        lax.rsqrt(jnp.mean(t * t, axis=-1, keepdims=True) + RMS_EPS), t.shape), o)


def _hgrn_fwd(name, hg, lb_logits, norm_g, cfg):
    T = hg.shape[0]
    sub_rows, hps, th = min(cfg[0], T), cfg[1], min(cfg[2], T)
    cps = sub_rows // HGRN_CHUNK
    nc = th // HGRN_CHUNK

    def body(fl_ref, qh_ref, v_ref, og_ref, lbl_ref, ng_ref, y_ref, ss_ref, st):
        @pl.when(pl.program_id(1) == 0)
        def _():
            st[...] = jnp.zeros_like(st)

        lbs = _lower_bound(lbl_ref)
        ng = jnp.tile(ng_ref[...], (1, hps))
        tril_b = _chunk_tri(sub_rows)

        def sub(si, carry):
            rows = pl.ds(pl.multiple_of(si * sub_rows, sub_rows), sub_rows)
            q = _hgrn_sub(fl_ref[rows, :], qh_ref[rows, :], v_ref[rows, :], lbs, tril_b)
            v_hc = [_chunks(t) for t in _heads(q["v_b"])]
            k_hc = [_chunks(t) for t in _heads(q["ke_b"])]
            s = [st[h] for h in range(hps)]
            st_b = [[] for _ in range(hps)]
            for c in range(cps):
                egl = _heads(q["egl"][c])
                for h in range(hps):
                    ss_ref[h, si * cps + c] = s[h]
                    st_b[h].append(s[h].astype(BF16))
                    s[h] = s[h] * egl[h] + _dot_tn(v_hc[h][c], k_hc[h][c])
            for h in range(hps):
                st[h] = s[h]
            o = _hgrn_out(q, st_b)
            og = og_ref[rows, :]
            y_ref[rows, :] = (o * _head_rms(o) * ng * (og * _sigmoid(og))).astype(BF16)
            return carry

        lax.fori_loop(0, th // sub_rows, sub, 0)

    wd = 128 * hps
    col = lambda j: pl.BlockSpec((th, wd), lambda h, t: (t, (8 // hps) * j + h))
    return pl.pallas_call(
        body, name=name, grid=(HGRN_HEADS // hps, T // th),
        in_specs=[col(0), col(1), col(2), col(3), pl.BlockSpec((2, wd), lambda h, t: (0, h)),
                  pl.BlockSpec((1, 128), lambda h, t: (0, 0))],
        out_specs=[pl.BlockSpec((th, wd), lambda h, t: (t, h)),
                   pl.BlockSpec((hps, nc, 128, 128), lambda h, t: (h, t, 0, 0))],
        out_shape=[jax.ShapeDtypeStruct((T, D), BF16),
                   jax.ShapeDtypeStruct((HGRN_HEADS, T // HGRN_CHUNK, 128, 128), F32)],
        scratch_shapes=[pltpu.VMEM((hps, 128, 128), F32)],
        compiler_params=_params("parallel", "arbitrary"),
    )(hg, hg, hg, hg, lb_logits, norm_g)


def _hgrn_bwd(name, hg, lb_logits, norm_g, sstart, dyh, cfg, carry=None):
    T = hg.shape[0]
    sub_rows, hps, th = min(cfg[0], T), cfg[1], min(cfg[2], T)
    cps = sub_rows // HGRN_CHUNK
    nc = th // HGRN_CHUNK
    nt = T // th
    wd = 128 * hps

    def body(fl_ref, qh_ref, v_ref, og_ref, lbl_ref, ng_ref, ss_ref, dy_ref,
             dfl_ref, dqh_ref, dv_ref, dog_ref, dlb_ref, dng_ref, dst):
        @pl.when(pl.program_id(1) == 0)
        def _():
            dst[...] = jnp.zeros_like(dst)
            dlb_ref[...] = jnp.zeros_like(dlb_ref)
            dng_ref[...] = jnp.zeros_like(dng_ref)

        lb = _lower_bound(lbl_ref)
        ng = jnp.tile(ng_ref[...], (1, hps))
        last = lax.broadcasted_iota(jnp.int32, (HGRN_CHUNK, wd), 0) == HGRN_CHUNK - 1
        nsub = th // sub_rows
        cat0 = lambda parts: jnp.concatenate(parts, axis=0)
        tril_b, triu_b = _chunk_tri(sub_rows), _chunk_tri(sub_rows, upper=True)

        def sub(step, carry):
            si = nsub - 1 - step
            rows = pl.ds(pl.multiple_of(si * sub_rows, sub_rows), sub_rows)
            qh, og, dy = qh_ref[rows, :], og_ref[rows, :], dy_ref[rows, :]
            q = _hgrn_sub(fl_ref[rows, :], qh, v_ref[rows, :], lb, tril_b)
            s_in = [[ss_ref[h, si * cps + c] for c in range(cps)] for h in range(hps)]
            st_b = [[s.astype(BF16) for s in row] for row in s_in]
            o = _hgrn_out(q, st_b)
            rr = _head_rms(o)
            oh = o * rr
            sog = _sigmoid(og)
            dog_ref[rows, :] = (dy * oh * ng * (sog * (1.0 + og * (1.0 - sog)))).astype(BF16)
            don = dy * (og * sog)
            dng_w = jnp.sum(don * oh, axis=0, keepdims=True)
            dd = don * ng
            mean_h = _per_head(lambda t: jnp.broadcast_to(jnp.mean(t, axis=-1, keepdims=True), t.shape), dd * oh)
            do_b = (rr * (dd - oh * mean_h)).astype(BF16)
            do_h, qd_h, ki_h, ke_h, v_h = (_heads(t) for t in (do_b, q["qd_b"], q["ki_b"], q["ke_b"], q["v_b"]))
            da_b = [jnp.where(tril_b > 0, _dot_nt(do_h[h], v_h[h]), 0.0).astype(BF16) for h in range(hps)]
            do_c, qd_c, ke_c, v_c = ([_chunks(t) for t in hs] for hs in (do_h, qd_h, ke_h, v_h))
            dsp = [[None] * cps for _ in range(hps)]
            dgl_dec = [[None] * cps for _ in range(hps)]
            d = [dst[h] for h in range(hps)]
            for c in reversed(range(cps)):
                egl = _heads(q["egl"][c])
                for h in range(hps):
                    dsp[h][c] = d[h]
                    dgl_dec[h][c] = jnp.sum(d[h] * s_in[h][c], axis=0, keepdims=True) * egl[h]
                    d[h] = d[h] * egl[h] + _dot_tn(do_c[h][c], qd_c[h][c])
            for h in range(hps):
                dst[h] = d[h]
                dng_ref[h] += dng_w[:, 128 * h:128 * (h + 1)]
            dsp_b = [[t.astype(BF16) for t in row] for row in dsp]
            dv_ref[rows, :] = jnp.concatenate(
                [_dot_tn(q["am_b"][h], do_h[h]) + cat0([_dot_nt(ke_c[h][c], dsp_b[h][c]) for c in range(cps)])
                 for h in range(hps)], axis=1).astype(BF16)
            dqd = jnp.concatenate(
                [_dot(da_b[h], ki_h[h]) + cat0([_dot(do_c[h][c], st_b[h][c]) for c in range(cps)])
                 for h in range(hps)], axis=1)
            dki = jnp.concatenate([_dot_tn(da_b[h], qd_h[h]) for h in range(hps)], axis=1)
            dke = jnp.concatenate([cat0([_dot(v_c[h][c], dsp_b[h][c]) for c in range(cps)]) for h in range(hps)], axis=1)
            dkk = dke * q["ke"]
            dgl = [jnp.sum(t, axis=0, keepdims=True) + jnp.concatenate([dgl_dec[h][c] for h in range(hps)], axis=1)
                   for c, t in enumerate(_chunks(dkk))]
            dgc = dqd * q["qd"] - dki * q["ki"] - dkk + cat0([jnp.where(last, g, 0.0) for g in dgl])
            dlf = _chunk_sum(triu_b, dgc)
            df = dlf / q["f"] - (dki * q["eng"] + dke * q["elg"])
            sg = q["sg"]
            dlb_ref[...] += jnp.sum(df * (1.0 - sg), axis=0, keepdims=True)
            dfl_ref[rows, :] = (df * (1.0 - lb) * sg * (1.0 - sg)).astype(BF16)
            sq = q["sq"]
            dqh_ref[rows, :] = (dqd * q["eg"] * (sq * (1.0 + qh * (1.0 - sq)))).astype(BF16)
            return carry

        lax.fori_loop(0, nsub, sub, 0)
    col = lambda j: pl.BlockSpec((th, wd), lambda h, t: (nt - 1 - t, (8 // hps) * j + h))
    out = pl.BlockSpec((th, wd), lambda h, t: (nt - 1 - t, h))
    return _call(
        body, name=name, grid=(HGRN_HEADS // hps, nt),
        in_specs=[col(0), col(1), col(2), col(3), pl.BlockSpec((2, wd), lambda h, t: (0, h)),
                  pl.BlockSpec((1, 128), lambda h, t: (0, 0)),
                  pl.BlockSpec((hps, nc, 128, 128), lambda h, t: (h, nt - 1 - t, 0, 0)), out],
        out_specs=[out, out, out, out, pl.BlockSpec((1, wd), lambda h, t: (0, h)),
                   pl.BlockSpec((hps, 1, 128), lambda h, t: (h, 0, 0))],
        out_shape=[jax.ShapeDtypeStruct((T, D), BF16)] * 4 + [jax.ShapeDtypeStruct((1, D), F32),
                                                              jax.ShapeDtypeStruct((HGRN_HEADS, 1, 128), F32)],
        scratch_shapes=[pltpu.VMEM((hps, 128, 128), F32)],
        ins=[hg, hg, hg, hg, lb_logits, norm_g, sstart, dyh], carry=carry)


def _mix_fwd(name, ya, yh, gates, xh1, g1, b1, wall, tm):
    T = ya.shape[0]

    def body(ya_ref, yh_ref, ga_ref, gh_ref, xh_ref, g_ref, b_ref, wall_ref,
             xo_ref, r_ref, pa_ref, ph_ref, mg_ref, wpa, wph, wo, sem):
        @pl.when(pl.program_id(0) == 0)
        def _():
            _load_rows(wall_ref, [(wpa, RIN, RP), (wph, RIN + RP, RP), (wo, RIN + 2 * RP, RP)], sem)

        parts = [slice(p, min(p + PART_ROWS, tm)) for p in range(0, tm, PART_ROWS)]
        pas = [_dot(ya_ref[rs, :], wpa[...]).astype(BF16) for rs in parts]
        phs = [_dot(yh_ref[rs, :], wph[...]).astype(BF16) for rs in parts]
        mgs = []
        for rs, pa, ph in zip(parts, pas, phs):
            pa_ref[rs, :] = pa
            ph_ref[rs, :] = ph
            sa, sh = _sigmoid(ga_ref[rs, :].astype(F32)), _sigmoid(gh_ref[rs, :].astype(F32))
            mgs.append((sa * pa.astype(F32) + sh * ph.astype(F32)).astype(BF16))
            mg_ref[rs, :] = mgs[-1]
        mixes = [_dot(mg, wo[...]) for mg in mgs]
        for rs, mix in zip(parts, mixes):
            y1 = xh_ref[rs, :] * g_ref[...] + b_ref[...]
            xh2, r = _ln_fwd(ALPHA * y1 + mix)
            xo_ref[rs, :] = xh2
            r_ref[rs, :] = jnp.broadcast_to(r, (xh2.shape[0], 128))

    row = lambda w: pl.BlockSpec((tm, w), lambda i: (i, 0))
    vec = pl.BlockSpec((1, D), lambda i: (0, 0))
    return pl.pallas_call(
        body, name=name, grid=(T // tm,),
        in_specs=[row(D), row(D), pl.BlockSpec((tm, D), lambda i: (i, 0)), pl.BlockSpec((tm, D), lambda i: (i, 1)),
                  row(D), vec, vec, HBM_SPEC],
        out_specs=[row(D), row(128), row(D), row(D), row(D)],
        out_shape=[jax.ShapeDtypeStruct((T, D), F32), jax.ShapeDtypeStruct((T, 128), F32)]
        + [jax.ShapeDtypeStruct((T, D), BF16)] * 3,
        scratch_shapes=[pltpu.VMEM((D, D), BF16)] * 3 + [pltpu.SemaphoreType.DMA((3 * NDEV,))],
        compiler_params=_params("arbitrary"),
    )(ya, yh, gates, gates, xh1, g1, b1, wall)


def _mix_bwd(name, dy2, xh2, r2, g2, gates, pa, ph, wall, tm, carry=None):
    T = dy2.shape[0]

    def body(dy_ref, xh_ref, r_ref, g_ref, ga_ref, gh_ref, pa_ref, ph_ref, wall_ref,
             dz_ref, dmix_ref, dpa_ref, dph_ref, dga_ref, dgh_ref, dya_ref, dyh_ref, dg_ref, db_ref,
             wpa, wph, wo, sem):
        @pl.when(pl.program_id(0) == 0)
        def _():
            _load_rows(wall_ref, [(wpa, RIN, RP), (wph, RIN + RP, RP), (wo, RIN + 2 * RP, RP)], sem)
            dg_ref[...] = jnp.zeros_like(dg_ref)
            db_ref[...] = jnp.zeros_like(db_ref)

        parts = [slice(p, min(p + PART_ROWS, tm)) for p in range(0, tm, PART_ROWS)]
        dmix = []
        for rs in parts:
            dz, dgp, dbp = _ln_bwd(dy_ref[rs, :], xh_ref[rs, :], r_ref[rs, :1], g_ref[...])
            dg_ref[...] += dgp
            db_ref[...] += dbp
            dz_ref[rs, :] = dz
            dmix.append(dz.astype(BF16))
            dmix_ref[rs, :] = dmix[-1]
        dmgs = [_dot_nt(d, wo[...]) for d in dmix]
        dpas, dphs = [], []
        for rs, dmg in zip(parts, dmgs):
            sa, sh = _sigmoid(ga_ref[rs, :].astype(F32)), _sigmoid(gh_ref[rs, :].astype(F32))
            dga_ref[rs, :] = (dmg * pa_ref[rs, :].astype(F32) * sa * (1.0 - sa)).astype(BF16)
            dgh_ref[rs, :] = (dmg * ph_ref[rs, :].astype(F32) * sh * (1.0 - sh)).astype(BF16)
            dpas.append((dmg * sa).astype(BF16))
            dphs.append((dmg * sh).astype(BF16))
            dpa_ref[rs, :] = dpas[-1]
            dph_ref[rs, :] = dphs[-1]
        for rs, dpa, dph in zip(parts, dpas, dphs):
            dya_ref[rs, :] = _dot_nt(dpa, wpa[...]).astype(BF16)
            dyh_ref[rs, :] = _dot_nt(dph, wph[...])

    row = lambda w: pl.BlockSpec((tm, w), lambda i: (i, 0))
    vec = pl.BlockSpec((1, D), lambda i: (0, 0))
    return _call(
        body, name=name, grid=(T // tm,),
        in_specs=[row(D), row(D), row(128), vec, pl.BlockSpec((tm, D), lambda i: (i, 0)),
                  pl.BlockSpec((tm, D), lambda i: (i, 1)), row(D), row(D), HBM_SPEC],
        out_specs=[row(D)] * 8 + [vec, vec],
        out_shape=[jax.ShapeDtypeStruct((T, D), F32)] + [jax.ShapeDtypeStruct((T, D), BF16)] * 6
        + [jax.ShapeDtypeStruct((T, D), F32)] + [jax.ShapeDtypeStruct((1, D), F32)] * 2,
        scratch_shapes=[pltpu.VMEM((D, D), BF16)] * 3 + [pltpu.SemaphoreType.DMA((3 * NDEV,))],
        ins=[dy2, xh2, r2, g2, gates, gates, pa, ph, wall], carry=carry)


def _adam(w, g, m, v):
    m = ADAM_B1 * m + (1.0 - ADAM_B1) * g
    v = ADAM_B2 * v + (1.0 - ADAM_B2) * (g * g)
    m_hat = m / (1.0 - ADAM_B1 ** ADAM_STEP)
    v_hat = v / (1.0 - ADAM_B2 ** ADAM_STEP)
    delta = -ADAM_LR * (m_hat / (jnp.sqrt(v_hat) + ADAM_EPS) + ADAM_WD * w)
    return delta, m, v


def _grad_steps(name, items, cb, carry=None):
    n = len(items)
    pieces = [len(it[0]) for it in items]

    def body(*refs):
        it = iter(refs[:sum(pieces) + 3 * n])
        outs = refs[sum(pieces) + 3 * n:]
        for i in range(n):
            accs = []
            for _ in range(pieces[i]):
                r_ref = next(it)
                acc = r_ref[0].astype(F32)
                for k in range(1, NDEV):
                    acc = acc + r_ref[k].astype(F32)
                accs.append(acc)
            acc = accs[0] if len(accs) == 1 else jnp.concatenate(accs, axis=0)
            w_ref, m_ref, v_ref = next(it), next(it), next(it)
            g_ref, d_ref, mo_ref, vo_ref = outs[4 * i:4 * i + 4]
            g_ref[...] = acc
            d, mm, vv = _adam(w_ref[...], acc, m_ref[...], v_ref[...])
            d_ref[...] = d
            mo_ref[...] = mm
            vo_ref[...] = vv

    in_specs, out_specs, out_shape, ins = [], [], [], []
    for recv, w, m, v in items:
        blk = pl.BlockSpec((w.shape[0], cb), lambda j: (0, j))
        in_specs += [pl.BlockSpec((NDEV, rp.shape[1], cb), lambda j: (0, 0, j)) for rp in recv] + [blk, blk, blk]
        out_specs += [blk] * 4
        out_shape += [jax.ShapeDtypeStruct(w.shape, F32)] * 4
        ins += list(recv) + [w, m, v]
    outs, ex = _call(body, name=name, grid=(D // cb,), in_specs=in_specs, out_specs=out_specs, out_shape=out_shape,
                     scratch_shapes=[], ins=ins, carry=carry)
    return [tuple(outs[4 * i:4 * i + 4]) for i in range(n)], ex


_SMALL = [("ln1_g", D), ("ln1_b", D), ("ln2_g", D), ("ln2_b", D), ("ln3_g", D), ("ln3_b", D),
          ("b_in", DIN), ("lb", D), ("attn_sinks", 128), ("hgrn_norm_g", 128), ("loss", 128)]
_SMALL_OFF = {}
_o = 0
for _n, _w in _SMALL:
    _SMALL_OFF[_n] = (_o, _w)
    _o += _w
PACK = _o


def _small_step(name, slots, small_w, small_m, small_v):
    names = ["ln1_g", "ln1_b", "ln2_g", "ln2_b", "ln3_g", "ln3_b", "b_in", "attn_sinks", "hgrn_lb_logits", "hgrn_norm_g"]
    np_ = len(names)

    def body(*refs):
        s_ref = refs[0]
        w_refs = refs[1:1 + np_]
        m_refs = refs[1 + np_:1 + 2 * np_]
        v_refs = refs[1 + 2 * np_:1 + 3 * np_]
        loss_ref, outs = refs[1 + 3 * np_], refs[2 + 3 * np_:]
        tot = s_ref[0]
        for k in range(1, NDEV):
            tot = tot + s_ref[k]

        def part(n):
            o, w = _SMALL_OFF[n]
            return tot[:, o:o + w]

        loss_ref[...] = part("loss")
        for i, n in enumerate(names):
            w = w_refs[i][...]
            if n == "hgrn_lb_logits":
                m_ = jnp.maximum(w[0:1, :], w[1:2, :])
                e0, e1 = jnp.exp(w[0:1, :] - m_), jnp.exp(w[1:2, :] - m_)
                p0 = e0 / (e0 + e1)
                t = p0 * (1.0 - p0) * part("lb")
                g = jnp.concatenate([t, -t], axis=0)
            elif n == "attn_sinks":
                g = part(n)[:, :N_Q_HEADS]
            else:
                g = part(n)
            d, mm, vv = _adam(w, g, m_refs[i][...], v_refs[i][...])
            outs[4 * i][...] = g
            outs[4 * i + 1][...] = d
            outs[4 * i + 2][...] = mm
            outs[4 * i + 3][...] = vv

    out_shape = [jax.ShapeDtypeStruct((1, 128), F32)]
    for n in names:
        out_shape += [jax.ShapeDtypeStruct(small_w[n].shape, F32)] * 4
    return pl.pallas_call(
        body, name=name, out_shape=out_shape,
        compiler_params=pltpu.CompilerParams(vmem_limit_bytes=VMEM_LIMIT),
    )(slots, *[small_w[n] for n in names], *[small_m[n] for n in names], *[small_v[n] for n in names]), names


def _tile(T, pref):
    return min(T, pref)


def kernel(x, ln1_g, ln1_b, ffn1_w1, ffn1_w3, ffn1_w2, ln2_g, ln2_b, w_in, b_in, attn_sinks, hgrn_lb_logits, hgrn_norm_g, w_proj_attn, w_proj_hgrn, w_out, ln3_g, ln3_b, ffn2_w1, ffn2_w3, ffn2_w2, loss_target, m_ln1_g, m_ln1_b, m_ffn1_w1, m_ffn1_w3, m_ffn1_w2, m_ln2_g, m_ln2_b, m_w_in, m_b_in, m_attn_sinks, m_hgrn_lb_logits, m_hgrn_norm_g, m_w_proj_attn, m_w_proj_hgrn, m_w_out, m_ln3_g, m_ln3_b, m_ffn2_w1, m_ffn2_w3, m_ffn2_w2, v_ln1_g, v_ln1_b, v_ffn1_w1, v_ffn1_w3, v_ffn1_w2, v_ln2_g, v_ln2_b, v_w_in, v_b_in, v_attn_sinks, v_hgrn_lb_logits, v_hgrn_norm_g, v_w_proj_attn, v_w_proj_hgrn, v_w_out, v_ln3_g, v_ln3_b, v_ffn2_w1, v_ffn2_w3, v_ffn2_w2):
    T = x.shape[1]
    xs = x[0]
    tgt = loss_target[0]
    tm = _tile(T, 256)
    tm2 = _tile(T, 512)
    tk = _tile(T, 2048)

    t_bf = lambda w: w[0].T.astype(BF16)
    n_bf = lambda w: w[0].astype(BF16)
    ffn_shard = lambda w1, w3, w2: jnp.concatenate([t_bf(w1), t_bf(w3), n_bf(w2)], axis=0)
    mix_shard = jnp.concatenate([t_bf(w_in), n_bf(w_proj_attn), n_bf(w_proj_hgrn), n_bf(w_out)], axis=0)
    (ffn1_all,) = _exchange_call("gather_ffn1", _gather_exchange(ffn_shard(ffn1_w1, ffn1_w3, ffn1_w2)))
    ffn_offs = (0, RF, 2 * RF)
    rope = _rope_table(T)

    (xh1, r1, a1, b1, xb0), (mix_all,) = _ffn_fwd("ffn1_fwd", xs, None, ffn1_all, ffn_offs, tm2,
                                                  carry=_gather_exchange(mix_shard))
    (qkv, kt, vt, hg, gates, y1b), (ffn2_all,) = _inproj_fwd(
        "inproj_fwd", xh1, ln1_g, ln1_b, mix_all, b_in, rope, tm2,
        carry=_gather_exchange(ffn_shard(ffn2_w1, ffn2_w3, ffn2_w2)))
    ya = _attn_fwd("attn_fwd", qkv, vt, attn_sinks)
    yh, sstart = _hgrn_fwd("hgrn_fwd", hg, hgrn_lb_logits, hgrn_norm_g, HG_FWD)
    xh2, r2, pa, ph, merged = _mix_fwd("mix_fwd", ya, yh, gates, xh1, ln1_g, ln1_b, mix_all, tm2)
    (xh3, r3, a2, b2, y2b, loss_part), _ = _ffn_fwd("ffn2_fwd", xh2, (ln2_g, ln2_b), ffn2_all, ffn_offs, tm2,
                                                    loss=(ln3_g, ln3_b, tgt))

    (dy2, dab2, u2, df2, dg3, db3), _ = _ffn_bwd("ffn2_bwd", ("loss", ln3_b, tgt), xh3, r3, ln3_g, a2, b2, ffn2_all,
                                                 ffn_offs, tm)
    recv = {}
    g_ffn2_13, _ = _wgrad("wgrad_ffn2_w13", dab2, y2b, DFF // 2, tk)
    g_ffn2_2, _ = _wgrad("wgrad_ffn2_w2", u2, df2, DFF // 2, tk)
    (dz2, dmix, dpa, dph, dga, dgh, dya, dyh, dg2, db2), _ = _mix_bwd(
        "mix_bwd", dy2, xh2, r2, ln2_g, gates, pa, ph, mix_all, tm2)
    g_wo, _ = _wgrad("wgrad_w_out", merged, dmix, D, tk)
    g_pa, _ = _wgrad("wgrad_w_pa", ya, dpa, D, tk)
    g_ph, _ = _wgrad("wgrad_w_ph", yh, dph, D, tk)
    (dq, dk, dv, dsink), _ = _attn_bwd("attn_bwd", qkv, kt, attn_sinks, dya)
    (dfl, dqh, dih, dog, dlb, dng), (recv["ffn2_w2"], recv["w_proj_attn"], recv["w_proj_hgrn"], recv["w_out"]) = \
        _hgrn_bwd("hgrn_bwd", hg, hgrn_lb_logits, hgrn_norm_g, sstart, dyh, HG_BWD,
                  carry=_grad_exchange([g_ffn2_2, g_pa, g_ph, g_wo], [0, 0, 0, 0], [RF, RP, RP, RP]))
    (dy1, dproj, dbin), (recv["ffn2_w1"], recv["ffn2_w3"]) = _inproj_bwd(
        "inproj_bwd", dq, dk, dv, (dfl, dqh, dih, dog), (dga, dgh), dz2, mix_all, rope, tm2,
        carry=_grad_exchange([g_ffn2_13, g_ffn2_13], [0, DFF], [RF, RF]))
    g_win, _ = _wgrad("wgrad_w_in", dproj, y1b, DIN // 4, tk)
    win_rows = (400, 288, 272)
    win_base = (0, 400, 688)
    (dz1, dab1, u1, df1, dg1, db1), (rw0,) = _ffn_bwd(
        "ffn1_bwd", ("dy", dy1), xh1, r1, ln1_g, a1, b1, ffn1_all, ffn_offs, tm2, with_dx=False,
        carry=_grad_exchange([g_win], win_base[:1], win_rows[:1], [RIN]))
    g_ffn1_2, (rw1,) = _wgrad("wgrad_ffn1_w2", u1, df1, DFF // 2, tk,
                              carry=_grad_exchange([g_win], win_base[1:2], win_rows[1:2], [RIN]))
    g_ffn1_1, (recv["ffn1_w2"],) = _wgrad("wgrad_ffn1_w1", dab1, xb0, DFF // 2, tk, 0, DFF,
                                          carry=_grad_exchange([g_ffn1_2], [0], [RF]))
    g_ffn1_3, (recv["ffn1_w1"],) = _wgrad("wgrad_ffn1_w3", dab1, xb0, DFF // 2, tk, DFF, DFF,
                                          carry=_grad_exchange([g_ffn1_1], [0], [RF]))
    parts = {"ln1_g": dg1, "ln1_b": db1, "ln2_g": dg2, "ln2_b": db2, "ln3_g": dg3, "ln3_b": db3, "b_in": dbin,
             "lb": dlb, "attn_sinks": dsink, "hgrn_norm_g": jnp.sum(dng, axis=0), "loss": loss_part[0:1, :]}
    packed = jnp.concatenate([parts[n] for n, _ in _SMALL], axis=1)
    gx, (recv["ffn1_w3"], rw2, small_slots) = _ffn_dx(
        "ffn1_dx", dab1, dz1, ffn1_all, ffn_offs, tm2,
        carry=_join(_grad_exchange([g_ffn1_3, g_win], [0, win_base[2]], [RF, win_rows[2]], [RF, RIN]),
                    _row_gather_exchange(packed)))
    recv["w_in"] = [rw0, rw1, rw2]

    big = [("ffn1_w1", ffn1_w1, m_ffn1_w1, v_ffn1_w1, True), ("ffn1_w3", ffn1_w3, m_ffn1_w3, v_ffn1_w3, True),
           ("ffn1_w2", ffn1_w2, m_ffn1_w2, v_ffn1_w2, False), ("w_in", w_in, m_w_in, v_w_in, True),
           ("w_proj_attn", w_proj_attn, m_w_proj_attn, v_w_proj_attn, False),
           ("w_proj_hgrn", w_proj_hgrn, m_w_proj_hgrn, v_w_proj_hgrn, False),
           ("w_out", w_out, m_w_out, v_w_out, False),
           ("ffn2_w1", ffn2_w1, m_ffn2_w1, v_ffn2_w1, True), ("ffn2_w3", ffn2_w3, m_ffn2_w3, v_ffn2_w3, True),
           ("ffn2_w2", ffn2_w2, m_ffn2_w2, v_ffn2_w2, False)]
    view = lambda t, transposed: t[0].T if transposed else t[0]
    back = lambda t, transposed: t.T[None] if transposed else t[None]
    slots = lambda n: recv[n] if isinstance(recv[n], list) else [recv[n]]
    stepped, _ = _grad_steps("steps", [(slots(n), view(w, tr), view(m, tr), view(v, tr)) for n, w, m, v, tr in big], 128)
    res = {n: tuple(back(t, tr) for t in outs) for (n, _, _, _, tr), outs in zip(big, stepped)}

    small_w = dict(ln1_g=ln1_g, ln1_b=ln1_b, ln2_g=ln2_g, ln2_b=ln2_b, ln3_g=ln3_g, ln3_b=ln3_b, b_in=b_in,
                   attn_sinks=attn_sinks, hgrn_lb_logits=hgrn_lb_logits, hgrn_norm_g=hgrn_norm_g)
    small_m = dict(ln1_g=m_ln1_g, ln1_b=m_ln1_b, ln2_g=m_ln2_g, ln2_b=m_ln2_b, ln3_g=m_ln3_g, ln3_b=m_ln3_b,
                   b_in=m_b_in, attn_sinks=m_attn_sinks, hgrn_lb_logits=m_hgrn_lb_logits, hgrn_norm_g=m_hgrn_norm_g)
    small_v = dict(ln1_g=v_ln1_g, ln1_b=v_ln1_b, ln2_g=v_ln2_g, ln2_b=v_ln2_b, ln3_g=v_ln3_g, ln3_b=v_ln3_b,
                   b_in=v_b_in, attn_sinks=v_attn_sinks, hgrn_lb_logits=v_hgrn_lb_logits, hgrn_norm_g=v_hgrn_norm_g)
    outs, names = _small_step("small_step", small_slots, small_w, small_m, small_v)
    loss = outs[0][0, 0]
    for i, n in enumerate(names):
        res[n] = tuple(outs[1 + 4 * i:5 + 4 * i])

    order = ["ln1_g", "ln1_b", "ffn1_w1", "ffn1_w3", "ffn1_w2", "ln2_g", "ln2_b", "w_in", "b_in", "attn_sinks",
             "hgrn_lb_logits", "hgrn_norm_g", "w_proj_attn", "w_proj_hgrn", "w_out", "ln3_g", "ln3_b",
             "ffn2_w1", "ffn2_w3", "ffn2_w2"]
    return (loss, gx[None], *[res[n][0] for n in order], *[res[n][1] for n in order],
            *[res[n][2] for n in order], *[res[n][3] for n in order])
```

```python
import jax
import jax.numpy as jnp
from jax import lax
from jax.experimental import pallas as pl
from jax.experimental.pallas import tpu as pltpu

F32 = jnp.float32
BF16 = jnp.bfloat16

NDEV = 8
D = 1024
DFF = 2816
RF = DFF // NDEV
DIN = 7680
RIN = DIN // NDEV
RP = D // NDEV
N_Q_HEADS = 16
N_KV_HEADS = 4
HEAD_DIM = 64
ATTN_BLOCK = 128
ROPE_THETA = 500000.0
ROPE_DIM = HEAD_DIM // 4
HGRN_HEADS = 8
HGRN_CHUNK = 64
ALPHA = 2.0 ** 0.25
LN_EPS = 1e-5
RMS_EPS = 1e-6
NEG_INF = -1e30
ADAM_LR = 0.001
ADAM_B1 = 0.9
ADAM_B2 = 0.999
ADAM_EPS = 1e-08
ADAM_WD = 0.01
ADAM_STEP = 10

QKV_W = 1536
HG_W = 4096
GATE_W = 2048
VMEM_LIMIT = 60 * 2 ** 20
PART_ROWS = 256
MESH = pl.DeviceIdType.MESH
HBM_SPEC = pl.BlockSpec(memory_space=pltpu.HBM)


def _params(*sem):
    return pltpu.CompilerParams(dimension_semantics=sem, vmem_limit_bytes=VMEM_LIMIT)


def _dot(a, b):
    return jnp.dot(a, b, preferred_element_type=F32)


def _dot_nt(a, b):
    return lax.dot_general(a, b, (((1,), (1,)), ((), ())), preferred_element_type=F32)


def _dot_tn(a, b):
    return lax.dot_general(a, b, (((0,), (0,)), ((), ())), preferred_element_type=F32)


def _sigmoid(x):
    return 0.5 * jnp.tanh(0.5 * x) + 0.5


def _ln_fwd(z):
    mu = jnp.mean(z, axis=-1, keepdims=True)
    zc = z - mu
    var = jnp.mean(zc * zc, axis=-1, keepdims=True)
    r = lax.rsqrt(var + LN_EPS)
    return zc * r, r


def _ln_bwd(dy, xh, r, g):
    dxh = dy * g
    m1 = jnp.mean(dxh, axis=-1, keepdims=True)
    m2 = jnp.mean(dxh * xh, axis=-1, keepdims=True)
    dz = r * (dxh - m1 - xh * m2)
    return dz, jnp.sum(dy * xh, axis=0, keepdims=True), jnp.sum(dy, axis=0, keepdims=True)


def _load_rows(wall_ref, pieces, sem):
    copies = []
    for dst, off, r in pieces:
        for k in range(NDEV):
            c = pltpu.make_async_copy(wall_ref.at[k, pl.ds(off, r), :], dst.at[pl.ds(k * r, r), :], sem.at[len(copies)])
            c.start()
            copies.append(c)
    for c in copies:
        c.wait()


class _Exchange:
    def __init__(self, inputs, out_shape, scratch, begin, middle, end):
        self.inputs, self.out_shape, self.scratch = inputs, out_shape, scratch
        self.begin, self.middle, self.end = begin, middle, end


def _gather_exchange(shard):
    rows, cols = shard.shape

    def ops(ins, outs, scr):
        (x_ref,), (out_ref,), (send_sems, recv_sems, local_sem) = ins, outs, scr
        x, y, c = lax.axis_index("x"), lax.axis_index("y"), lax.axis_index("c")
        me, sibling = (x, y, c), (x, y, 1 - c)
        chips = [(1 - x, y), (x, 1 - y), (1 - x, 1 - y)]

        def slot(px, py, pc):
            return out_ref.at[4 * px + 2 * py + pc]

        def copy(k, block, to, src=None):
            return pltpu.make_async_remote_copy(
                src_ref=slot(*block) if src is None else src, dst_ref=slot(*block),
                send_sem=send_sems.at[k], recv_sem=recv_sems.at[k], device_id=to, device_id_type=MESH)

        mine = lambda: pltpu.make_async_copy(x_ref, slot(*me), local_sem)
        first = lambda: [copy(0, me, sibling, src=x_ref)] + [
            copy(1 + j, me, (*chip, c), src=x_ref) for j, chip in enumerate(chips)]
        passed = lambda: [copy(4 + j, (*chip, c), sibling) for j, chip in enumerate(chips)]
        return c, me, sibling, chips, copy, mine, first, passed

    def begin(*refs):
        _, _, _, _, _, mine, first, _ = ops(*refs)
        mine().start()
        for cp in first():
            cp.start()

    def middle(*refs):
        c, me, _, chips, copy, _, _, passed = ops(*refs)
        for (j, chip), fwd in zip(enumerate(chips), passed()):
            copy(1 + j, (*chip, c), me).wait_recv()
            fwd.start()

    def end(*refs):
        c, me, sibling, chips, copy, mine, first, passed = ops(*refs)
        copy(0, sibling, me).wait_recv()
        for j, chip in enumerate(chips):
            copy(4 + j, (*chip, 1 - c), me).wait_recv()
        for cp in first() + passed():
            cp.wait_send()
        mine().wait()

    return _Exchange([shard], [jax.ShapeDtypeStruct((NDEV, rows, cols), shard.dtype)],
                     [pltpu.SemaphoreType.DMA((7,)), pltpu.SemaphoreType.DMA((7,)), pltpu.SemaphoreType.DMA],
                     begin, middle, end)


def _grad_exchange(grads, bases, rows, strides=None):
    n = len(grads)
    strides = rows if strides is None else strides

    def copies(g_refs, out_refs, scr):
        send_sems, recv_sems, local_sems = scr
        x, y, c = lax.axis_index("x"), lax.axis_index("y"), lax.axis_index("c")
        me = 4 * x + 2 * y + c
        out = []
        for i in range(n):
            r = rows[i]
            src = lambda k: g_refs[i].at[pl.ds(pl.multiple_of(bases[i] + k * strides[i], 16), r), :]
            out.append(pltpu.make_async_copy(src(me), out_refs[i].at[me], local_sems.at[i]))
            for j in range(1, NDEV):
                px, py, pc = x ^ (j >> 2), y ^ ((j >> 1) & 1), c ^ (j & 1)
                out.append(pltpu.make_async_remote_copy(
                    src_ref=src(4 * px + 2 * py + pc), dst_ref=out_refs[i].at[me],
                    send_sem=send_sems.at[i, j - 1], recv_sem=recv_sems.at[i, j - 1],
                    device_id=(px, py, pc), device_id_type=MESH))
        return out

    def begin(*refs):
        for cp in copies(*refs):
            cp.start()

    def end(*refs):
        for cp in copies(*refs):
            cp.wait()

    return _Exchange(list(grads), [jax.ShapeDtypeStruct((NDEV, r, g.shape[1]), g.dtype) for g, r in zip(grads, rows)],
                     [pltpu.SemaphoreType.DMA((n, NDEV - 1)), pltpu.SemaphoreType.DMA((n, NDEV - 1)),
                      pltpu.SemaphoreType.DMA((n,))], begin, None, end)


def _row_gather_exchange(row):
    def copies(ins, outs, scr):
        (r_ref,), (o_ref,), (send_sems, recv_sems, local_sem) = ins, outs, scr
        x, y, c = lax.axis_index("x"), lax.axis_index("y"), lax.axis_index("c")
        me = 4 * x + 2 * y + c
        out = [pltpu.make_async_copy(r_ref, o_ref.at[me], local_sem)]
        for j in range(1, NDEV):
            out.append(pltpu.make_async_remote_copy(
                src_ref=r_ref, dst_ref=o_ref.at[me], send_sem=send_sems.at[j - 1], recv_sem=recv_sems.at[j - 1],
                device_id=(x ^ (j >> 2), y ^ ((j >> 1) & 1), c ^ (j & 1)), device_id_type=MESH))
        return out

    def begin(*refs):
        for cp in copies(*refs):
            cp.start()

    def end(*refs):
        for cp in copies(*refs):
            cp.wait()

    return _Exchange([row], [jax.ShapeDtypeStruct((NDEV,) + row.shape, row.dtype)],
                     [pltpu.SemaphoreType.DMA((NDEV - 1,)), pltpu.SemaphoreType.DMA((NDEV - 1,)), pltpu.SemaphoreType.DMA],
                     begin, None, end)


def _join(a, b):
    na, oa, sa = len(a.inputs), len(a.out_shape), len(a.scratch)
    split = lambda ins, outs, scr: ((ins[:na], outs[:oa], scr[:sa]), (ins[na:], outs[oa:], scr[sa:]))

    def begin(*refs):
        pa, pb = split(*refs)
        a.begin(*pa)
        b.begin(*pb)

    def end(*refs):
        pa, pb = split(*refs)
        a.end(*pa)
        b.end(*pb)

    return _Exchange(a.inputs + b.inputs, a.out_shape + b.out_shape, a.scratch + b.scratch, begin, None, end)


def _exchange_call(name, ex):
    ni, no = len(ex.inputs), len(ex.out_shape)

    def body(*refs):
        parts = (refs[:ni], refs[ni:ni + no], refs[ni + no:])
        ex.begin(*parts)
        if ex.middle is not None:
            ex.middle(*parts)
        ex.end(*parts)

    return pl.pallas_call(body, name=name, out_shape=ex.out_shape, in_specs=[HBM_SPEC] * ni, out_specs=[HBM_SPEC] * no,
                          scratch_shapes=ex.scratch)(*ex.inputs)


def _call(body, *, name, grid, in_specs, out_specs, out_shape, scratch_shapes, ins, carry=None):
    sem = ("arbitrary",) * len(grid)
    if carry is None:
        outs = pl.pallas_call(body, name=name, grid=grid, in_specs=in_specs, out_specs=out_specs, out_shape=out_shape,
                              scratch_shapes=scratch_shapes, compiler_params=_params(*sem))(*ins)
        return outs, None
    n_in, n_out, n_scr = len(ins), len(out_shape), len(scratch_shapes)
    ci, co = len(carry.inputs), len(carry.out_shape)
    total = 1
    for g in grid:
        total *= g

    def wrapped(*refs):
        own_in, ex_in = refs[:n_in], refs[n_in:n_in + ci]
        o0 = n_in + ci
        own_out, ex_out = refs[o0:o0 + n_out], refs[o0 + n_out:o0 + n_out + co]
        s0 = o0 + n_out + co
        own_scr, ex_scr = refs[s0:s0 + n_scr], refs[s0 + n_scr:]
        step = pl.program_id(0)
        for d in range(1, len(grid)):
            step = step * grid[d] + pl.program_id(d)
        parts = (ex_in, ex_out, ex_scr)
        pl.when(step == 0)(lambda: carry.begin(*parts))
        body(*own_in, *own_out, *own_scr)
        if carry.middle is not None:
            pl.when(step == (3 * total) // 4)(lambda: carry.middle(*parts))
        pl.when(step == total - 1)(lambda: carry.end(*parts))

    outs = pl.pallas_call(
        wrapped, name=name, grid=grid, in_specs=list(in_specs) + [HBM_SPEC] * ci,
        out_specs=list(out_specs) + [HBM_SPEC] * co, out_shape=list(out_shape) + list(carry.out_shape),
        scratch_shapes=list(scratch_shapes) + list(carry.scratch), compiler_params=_params(*sem),
    )(*ins, *carry.inputs)
    return outs[:n_out], outs[n_out:]


def _ffn_fwd(name, xin, affine, wall, offs, tm, loss=None, carry=None):
    T = xin.shape[0]
    nt = T // tm

    def body(*refs):
        it = iter(refs)
        x_ref = next(it)
        if affine is not None:
            g_ref, b_ref = next(it), next(it)
        wall_ref = next(it)
        if loss is not None:
            go_ref, bo_ref, tgt_ref = next(it), next(it), next(it)
        xh_ref, r_ref, a_ref, b2_ref, yb_ref = (next(it) for _ in range(5))
        if loss is not None:
            loss_ref = next(it)
        w1, w3, w2, sem = (next(it) for _ in range(4))

        @pl.when(pl.program_id(0) == 0)
        def _():
            _load_rows(wall_ref, [(w1, offs[0], RF), (w3, offs[1], RF), (w2, offs[2], RF)], sem)
            if loss is not None:
                loss_ref[...] = jnp.zeros_like(loss_ref)

        parts = [slice(p, min(p + PART_ROWS, tm)) for p in range(0, tm, PART_ROWS)]
        ys, ybs = [], []
        for rs in parts:
            y = x_ref[rs, :]
            if affine is not None:
                y = y * g_ref[...] + b_ref[...]
            ys.append(y)
            ybs.append(y.astype(BF16))
            yb_ref[rs, :] = ybs[-1]
        ab = [(_dot_nt(yb, w1[...]).astype(BF16), _dot_nt(yb, w3[...]).astype(BF16)) for yb in ybs]
        us = []
        for rs, (a, b) in zip(parts, ab):
            a_ref[rs, :] = a
            b2_ref[rs, :] = b
            af, bf = a.astype(F32), b.astype(F32)
            us.append((af * _sigmoid(af) * bf).astype(BF16))
        fs = [_dot(u, w2[...]) for u in us]
        for rs, y, f in zip(parts, ys, fs):
            xh, r = _ln_fwd(ALPHA * y + 0.5 * f)
            xh_ref[rs, :] = xh
            r_ref[rs, :] = jnp.broadcast_to(r, (xh.shape[0], 128))
            if loss is not None:
                e = xh * go_ref[...] + bo_ref[...] - tgt_ref[rs, :]
                loss_ref[...] += jnp.sum(e * e) * (0.5 / D)

    row = lambda w: pl.BlockSpec((tm, w), lambda i: (i, 0))
    vec = pl.BlockSpec((1, D), lambda i: (0, 0))
    ins, in_specs = [xin], [row(D)]
    if affine is not None:
        ins += list(affine)
        in_specs += [vec, vec]
    ins.append(wall)
    in_specs.append(HBM_SPEC)
    if loss is not None:
        ins += list(loss)
        in_specs += [vec, vec, row(D)]
    out_shape = [jax.ShapeDtypeStruct((T, D), F32), jax.ShapeDtypeStruct((T, 128), F32),
                 jax.ShapeDtypeStruct((T, DFF), BF16), jax.ShapeDtypeStruct((T, DFF), BF16),
                 jax.ShapeDtypeStruct((T, D), BF16)]
    out_specs = [row(D), row(128), row(DFF), row(DFF), row(D)]
    if loss is not None:
        out_shape.append(jax.ShapeDtypeStruct((8, 128), F32))
        out_specs.append(pl.BlockSpec((8, 128), lambda i: (0, 0)))
    return _call(body, name=name, grid=(nt,), in_specs=in_specs, out_specs=out_specs, out_shape=out_shape,
                 scratch_shapes=[pltpu.VMEM((DFF, D), BF16)] * 3 + [pltpu.SemaphoreType.DMA((3 * NDEV,))],
                 ins=ins, carry=carry)


def _ffn_bwd(name, dy_src, xh, r, g, a, b, wall, offs, tm, with_dx=True, carry=None):
    T = xh.shape[0]
    nt = T // tm
    from_loss = dy_src[0] == "loss"

    def body(*refs):
        it = iter(refs)
        if from_loss:
            bo_ref, tgt_ref = next(it), next(it)
        else:
            dy_ref = next(it)
        xh_ref, r_ref, g_ref, a_ref, b_ref, wall_ref = (next(it) for _ in range(6))
        dyin_ref, dab_ref, u_ref, df_ref, dg_ref, db_ref = (next(it) for _ in range(6))
        if with_dx:
            w1, w3 = next(it), next(it)
        w2, sem = next(it), next(it)

        @pl.when(pl.program_id(0) == 0)
        def _():
            _load_rows(wall_ref, ([(w1, offs[0], RF), (w3, offs[1], RF)] if with_dx else []) + [(w2, offs[2], RF)], sem)
            dg_ref[...] = jnp.zeros_like(dg_ref)
            db_ref[...] = jnp.zeros_like(db_ref)

        parts = [slice(p, min(p + PART_ROWS, tm)) for p in range(0, tm, PART_ROWS)]
        gv = g_ref[...]
        dz, df = [], []
        for rs in parts:
            xhv = xh_ref[rs, :]
            if from_loss:
                dy = (xhv * gv + bo_ref[...] - tgt_ref[rs, :]) * (1.0 / D)
            else:
                dy = dy_ref[rs, :]
            dzp, dgp, dbp = _ln_bwd(dy, xhv, r_ref[rs, :1], gv)
            dg_ref[...] += dgp
            db_ref[...] += dbp
            dz.append(dzp)
            df.append((0.5 * dzp).astype(BF16))
            df_ref[rs, :] = df[-1]
        du = [_dot_nt(d, w2[...]) for d in df]
        da, dbb = [], []
        for rs, dup in zip(parts, du):
            af, bf = a_ref[rs, :].astype(F32), b_ref[rs, :].astype(F32)
            s = _sigmoid(af)
            sl = af * s
            u_ref[rs, :] = (sl * bf).astype(BF16)
            da.append((dup * bf * (s * (1.0 + af * (1.0 - s)))).astype(BF16))
            dbb.append((dup * sl).astype(BF16))
            dab_ref[rs, :DFF] = da[-1]
            dab_ref[rs, DFF:] = dbb[-1]
        for rs, dzp, dap, dbp in zip(parts, dz, da, dbb):
            dyin_ref[rs, :] = ALPHA * dzp + _dot(dap, w1[...]) + _dot(dbp, w3[...]) if with_dx else dzp

    row = lambda w: pl.BlockSpec((tm, w), lambda i: (i, 0))
    vec = pl.BlockSpec((1, D), lambda i: (0, 0))
    if from_loss:
        ins, in_specs = [dy_src[1], dy_src[2]], [vec, row(D)]
    else:
        ins, in_specs = [dy_src[1]], [row(D)]
    ins += [xh, r, g, a, b, wall]
    in_specs += [row(D), row(128), vec, row(DFF), row(DFF), HBM_SPEC]
    return _call(
        body, name=name, grid=(nt,), in_specs=in_specs,
        out_specs=[row(D), row(2 * DFF), row(DFF), row(D), vec, vec],
        out_shape=[jax.ShapeDtypeStruct((T, D), F32), jax.ShapeDtypeStruct((T, 2 * DFF), BF16),
                   jax.ShapeDtypeStruct((T, DFF), BF16), jax.ShapeDtypeStruct((T, D), BF16),
                   jax.ShapeDtypeStruct((1, D), F32), jax.ShapeDtypeStruct((1, D), F32)],
        scratch_shapes=[pltpu.VMEM((DFF, D), BF16)] * (3 if with_dx else 1) + [pltpu.SemaphoreType.DMA((3 * NDEV,))],
        ins=ins, carry=carry)


def _ffn_dx(name, dab, dz, wall, offs, tm, carry=None):
    T = dz.shape[0]

    def body(dab_ref, dz_ref, wall_ref, o_ref, w13, sem):
        @pl.when(pl.program_id(0) == 0)
        def _():
            _load_rows(wall_ref, [(w13.at[pl.ds(0, DFF), :], offs[0], RF), (w13.at[pl.ds(DFF, DFF), :], offs[1], RF)], sem)

        for p in range(0, tm, PART_ROWS):
            rs = slice(p, min(p + PART_ROWS, tm))
            o_ref[rs, :] = ALPHA * dz_ref[rs, :] + _dot(dab_ref[rs, :], w13[...])

    row = lambda w: pl.BlockSpec((tm, w), lambda i: (i, 0))
    (out,), ex = _call(
        body, name=name, grid=(T // tm,), in_specs=[row(2 * DFF), row(D), HBM_SPEC], out_specs=[row(D)],
        out_shape=[jax.ShapeDtypeStruct((T, D), F32)],
        scratch_shapes=[pltpu.VMEM((2 * DFF, D), BF16), pltpu.SemaphoreType.DMA((2 * NDEV,))],
        ins=[dab, dz, wall], carry=carry)
    return out, ex


def _wgrad(name, a, b, bn, tk, col0=0, ncols=None, carry=None):
    T = a.shape[0]
    N = a.shape[1] if ncols is None else ncols
    nk = T // tk
    c0 = col0 // bn

    def body(a_ref, b_ref, o_ref, acc):
        k = pl.program_id(1)

        @pl.when(k == 0)
        def _():
            acc[...] = jnp.zeros_like(acc)

        acc[...] += _dot_tn(a_ref[...], b_ref[...])

        @pl.when(k == nk - 1)
        def _():
            o_ref[...] = acc[...].astype(BF16)

    (out,), ex = _call(
        body, name=name, grid=(N // bn, nk),
        in_specs=[pl.BlockSpec((tk, bn), lambda n, k: (k, n + c0)), pl.BlockSpec((tk, D), lambda n, k: (k, 0))],
        out_specs=[pl.BlockSpec((bn, D), lambda n, k: (n, 0))],
        out_shape=[jax.ShapeDtypeStruct((N, D), BF16)],
        scratch_shapes=[pltpu.VMEM((bn, D), F32)], ins=[a, b], carry=carry)
    return out, ex


def _rope_table(T):
    pos = jnp.arange(T, dtype=F32)
    inv_freq = ROPE_THETA ** (-jnp.arange(0, ROPE_DIM, 2, dtype=F32) / ROPE_DIM)
    ang = pos[:, None] * inv_freq[None, :]
    return jnp.pad(jnp.concatenate([jnp.cos(ang), jnp.sin(ang)], axis=1), ((0, 0), (0, 128 - ROPE_DIM)))


def _rope_expand(cs):
    half = ROPE_DIM // 2
    lane = lax.broadcasted_iota(jnp.int32, cs.shape, 1)
    cos = jnp.where(lane < half, cs, 0.0)
    sin = jnp.where((lane >= half) & (lane < ROPE_DIM), cs, 0.0)
    both = lambda t: t + pltpu.roll(t, HEAD_DIM, 1)
    c = jnp.where(jnp.bitwise_and(lane, HEAD_DIM - 1) < ROPE_DIM, both(cos + pltpu.roll(cos, half, 1)), 1.0)
    return c, -both(pltpu.roll(sin, 128 - half, 1)), both(sin)


def _rope(t, c, s1, s2):
    n = t.shape[1] // 128
    ct, s1t, s2t = (jnp.tile(v, (1, n)) for v in (c, s1, s2))
    w = t.shape[1]
    return t * ct + pltpu.roll(t, w - 8, 1) * s1t + pltpu.roll(t, 8, 1) * s2t


def _rope_t(dr, c, s1, s2):
    n = dr.shape[1] // 128
    ct, s1t, s2t = (jnp.tile(v, (1, n)) for v in (c, s1, s2))
    w = dr.shape[1]
    return dr * ct + pltpu.roll(dr * s1t, 8, 1) + pltpu.roll(dr * s2t, w - 8, 1)


_Q, _K, _V = (0, 1024), (1024, 256), (1280, 256)
_HG = (1536, HG_W)
_GATES = (5632, GATE_W)


def _inproj_fwd(name, xh, g, b, wall, b_in, rope, tm, carry=None):
    T = xh.shape[0]

    def body(xh_ref, g_ref, b_ref, wall_ref, bin_ref, cs_ref,
             qkv_ref, kt_ref, vt_ref, hg_ref, gate_ref, yb_ref, w, sem):
        @pl.when(pl.program_id(0) == 0)
        def _():
            _load_rows(wall_ref, [(w, 0, RIN)], sem)

        yb = (xh_ref[...] * g_ref[...] + b_ref[...]).astype(BF16)
        yb_ref[...] = yb
        c, s1, s2 = _rope_expand(cs_ref[...])

        def piece(start, width):
            return _dot_nt(yb, w[start:start + width, :]) + bin_ref[:, start:start + width]

        k = _rope(piece(*_K), c, s1, s2)
        v = piece(*_V)
        qkv_ref[:, 0:1024] = _rope(piece(*_Q), c, s1, s2).astype(BF16)
        qkv_ref[:, 1024:1280] = k.astype(BF16)
        qkv_ref[:, 1280:1536] = v.astype(BF16)
        kt_ref[...] = k.T.astype(BF16)
        vt_ref[...] = v.T.astype(BF16)
        for j in range(4):
            hg_ref[:, 1024 * j:1024 * (j + 1)] = piece(_HG[0] + 1024 * j, 1024)
        for j in range(2):
            gate_ref[:, 1024 * j:1024 * (j + 1)] = piece(_GATES[0] + 1024 * j, 1024).astype(BF16)

    row = lambda wd: pl.BlockSpec((tm, wd), lambda i: (i, 0))
    vec = lambda wd: pl.BlockSpec((1, wd), lambda i: (0, 0))
    colt = pl.BlockSpec((256, tm), lambda i: (0, i))
    return _call(
        body, name=name, grid=(T // tm,),
        in_specs=[row(D), vec(D), vec(D), HBM_SPEC, vec(DIN), row(128)],
        out_specs=[row(QKV_W), colt, colt, row(HG_W), row(GATE_W), row(D)],
        out_shape=[jax.ShapeDtypeStruct((T, QKV_W), BF16), jax.ShapeDtypeStruct((256, T), BF16),
                   jax.ShapeDtypeStruct((256, T), BF16), jax.ShapeDtypeStruct((T, HG_W), F32),
                   jax.ShapeDtypeStruct((T, GATE_W), BF16), jax.ShapeDtypeStruct((T, D), BF16)],
        scratch_shapes=[pltpu.VMEM((DIN, D), BF16), pltpu.SemaphoreType.DMA((NDEV,))],
        ins=[xh, g, b, wall, b_in, rope], carry=carry)


def _inproj_bwd(name, dq, dk, dv, dhg, dgates, dz2, wall, rope, tm, carry=None):
    T = dq.shape[0]

    def body(dq_ref, dk_ref, dv_ref, d0, d1, d2, d3, dga_ref, dgh_ref, dz_ref, wall_ref, cs_ref,
             dy_ref, dproj_ref, dbin_ref, w, sem):
        @pl.when(pl.program_id(0) == 0)
        def _():
            _load_rows(wall_ref, [(w, 0, RIN)], sem)
            dbin_ref[...] = jnp.zeros_like(dbin_ref)

        c, s1, s2 = _rope_expand(cs_ref[...])
        acc = ALPHA * dz_ref[...]
        pieces = [(_Q[0], _rope_t(dq_ref[...], c, s1, s2)), (_K[0], _rope_t(dk_ref[...], c, s1, s2)),
                  (_V[0], dv_ref[...])]
        pieces += [(_HG[0] + 1024 * j, r[...]) for j, r in enumerate((d0, d1, d2, d3))]
        pieces += [(_GATES[0], dga_ref[...]), (_GATES[0] + 1024, dgh_ref[...])]
        for start, val in pieces:
            width = val.shape[1]
            dbin_ref[:, start:start + width] += jnp.sum(val.astype(F32), axis=0, keepdims=True)
            vb = val.astype(BF16)
            dproj_ref[:, start:start + width] = vb
            acc = acc + _dot(vb, w[start:start + width, :])
        dy_ref[...] = acc

    row = lambda wd: pl.BlockSpec((tm, wd), lambda i: (i, 0))
    return _call(
        body, name=name, grid=(T // tm,),
        in_specs=[row(D), row(256), row(256)] + [row(D)] * 4 + [row(D), row(D), row(D), HBM_SPEC, row(128)],
        out_specs=[row(D), row(DIN), pl.BlockSpec((1, DIN), lambda i: (0, 0))],
        out_shape=[jax.ShapeDtypeStruct((T, D), F32), jax.ShapeDtypeStruct((T, DIN), BF16),
                   jax.ShapeDtypeStruct((1, DIN), F32)],
        scratch_shapes=[pltpu.VMEM((DIN, D), BF16), pltpu.SemaphoreType.DMA((NDEV,))],
        ins=[dq, dk, dv, *dhg, *dgates, dz2, wall, rope], carry=carry)


def _halves(t):
    lane = lax.broadcasted_iota(jnp.int32, t.shape, 1)
    low = lane < HEAD_DIM
    sw = pltpu.roll(t, HEAD_DIM, 1)
    zero = jnp.zeros_like(t)
    h0 = (jnp.where(low, t, zero), jnp.where(low, zero, sw))
    h1 = (jnp.where(low, sw, zero), jnp.where(low, zero, t))
    return h0, h1


def _lane_stack(p_ref, c_ref, gp):
    sl = slice(128 * gp, 128 * (gp + 1))
    hp, hc = _halves(p_ref[:, sl].astype(F32)), _halves(c_ref[:, sl].astype(F32))
    return [jnp.concatenate([hp[gg][0], hc[gg][0], hp[gg][1], hc[gg][1]], axis=0).astype(BF16) for gg in range(2)]


def _row_stack(tp_ref, tc_ref, g):
    band = jnp.concatenate([tp_ref[64 * g:64 * (g + 1), :], tc_ref[64 * g:64 * (g + 1), :]], axis=1)
    z = jnp.zeros_like(band)
    return [jnp.concatenate([band, z], axis=0), jnp.concatenate([z, band], axis=0)]


def _fold(prevm, t4, hh):
    return jnp.where(prevm, t4[256 * hh:256 * hh + 128, :], t4[256 * hh + 128:256 * hh + 256, :])


def _unfold(prevm, t):
    return jnp.concatenate([jnp.where(prevm, t, 0.0), jnp.where(prevm, 0.0, t)], axis=0).astype(BF16)


def _attn_softmax(s, kill, sink):
    s = jnp.where(kill, NEG_INF, s)
    m = jnp.maximum(jnp.max(s, axis=0, keepdims=True), sink)
    p = jnp.exp(s - m)
    es = jnp.exp(sink - m)
    inv = 1.0 / (jnp.sum(p, axis=0, keepdims=True) + es)
    return p * inv, es * inv


def _attn_masks(first):
    row = lax.broadcasted_iota(jnp.int32, (ATTN_BLOCK, 2 * ATTN_BLOCK), 0)
    lane = lax.broadcasted_iota(jnp.int32, (ATTN_BLOCK, 2 * ATTN_BLOCK), 1)
    prevm = row > lane % ATTN_BLOCK
    return prevm, jnp.logical_and(first, prevm)


def _pair_rows(ref, g):
    return jnp.concatenate([ref[:, 256 * g:256 * g + 128], ref[:, 256 * g + 128:256 * (g + 1)]], axis=0)


def _pair_sinks(sink_ref, g, hh):
    h0, h1 = 4 * g + hh, 4 * g + 2 + hh
    return jnp.concatenate([jnp.broadcast_to(sink_ref[:, h0:h0 + 1], (1, ATTN_BLOCK)),
                            jnp.broadcast_to(sink_ref[:, h1:h1 + 1], (1, ATTN_BLOCK))], axis=1)


def _attn_fwd(name, qkv, vt, sinks):
    T = qkv.shape[0]
    nbk = min(ATTN_TILE[0], T // ATTN_BLOCK)
    rows = nbk * ATTN_BLOCK
    scale = HEAD_DIM ** -0.5

    def block(q_ref, kp_ref, kc_ref, vtp_ref, vtc_ref, sink_ref, o_ref, first):
        prevm, kill = _attn_masks(first)
        kst = _lane_stack(kp_ref, kc_ref, 0) + _lane_stack(kp_ref, kc_ref, 1)
        vts = [_row_stack(vtp_ref, vtc_ref, g) for g in range(N_KV_HEADS)]
        s8 = [_dot_nt(kst[g], _pair_rows(q_ref, g)) * scale for g in range(N_KV_HEADS)]
        pu = [[_unfold(prevm, _attn_softmax(_fold(prevm, s8[g], hh), kill, _pair_sinks(sink_ref, g, hh))[0])
               for hh in range(2)] for g in range(N_KV_HEADS)]
        for g in range(N_KV_HEADS):
            ot = _dot(vts[g][0], pu[g][0]) + _dot(vts[g][1], pu[g][1])
            o_ref[:, 256 * g:256 * g + 128] = ot[:, :ATTN_BLOCK].T.astype(BF16)
            o_ref[:, 256 * g + 128:256 * (g + 1)] = ot[:, ATTN_BLOCK:].T.astype(BF16)

    def body(q_ref, kp_ref, kc_ref, vtp_ref, vtc_ref, sink_ref, o_ref):
        for b in range(nbk):
            rs = slice(ATTN_BLOCK * b, ATTN_BLOCK * (b + 1))
            ps = slice(ATTN_BLOCK * (b - 1), ATTN_BLOCK * b)
            block(q_ref.at[rs, :], kp_ref if b == 0 else kc_ref.at[ps, :], kc_ref.at[rs, :],
                  vtp_ref if b == 0 else vtc_ref.at[:, ps], vtc_ref.at[:, rs], sink_ref, o_ref.at[rs, :],
                  pl.program_id(0) == 0 if b == 0 else False)

    prev = lambda i: jnp.maximum(nbk * i - 1, 0)
    return pl.pallas_call(
        body, name=name, grid=(T // rows,),
        in_specs=[pl.BlockSpec((rows, D), lambda i: (i, 0)),
                  pl.BlockSpec((ATTN_BLOCK, 256), lambda i: (prev(i), 4)),
                  pl.BlockSpec((rows, 256), lambda i: (i, 4)),
                  pl.BlockSpec((256, ATTN_BLOCK), lambda i: (0, prev(i))),
                  pl.BlockSpec((256, rows), lambda i: (0, i)),
                  pl.BlockSpec((1, N_Q_HEADS), lambda i: (0, 0))],
        out_specs=pl.BlockSpec((rows, D), lambda i: (i, 0)),
        out_shape=jax.ShapeDtypeStruct((T, D), BF16),
        compiler_params=_params("arbitrary"),
    )(qkv, qkv, qkv, vt, vt, sinks)


ATTN_TILE = (8, 4)
ATTN_BWD_ORDER = ((0, 0), (0, 1), (1, 0), (0, 2), (1, 1), (2, 0), (0, 3), (1, 2), (2, 1), (1, 3), (2, 2), (2, 3))


def _attn_bwd(name, qkv, kt, sinks, dya, carry=None):
    T = qkv.shape[0]
    nbk = min(ATTN_TILE[1], T // ATTN_BLOCK)
    rows = nbk * ATTN_BLOCK
    scale = HEAD_DIM ** -0.5

    def block(q_ref, kp_ref, kc_ref, vp_ref, vc_ref, ktp_ref, ktc_ref, sink_ref, do_ref, dq_ref, first):
        prevm, kill = _attn_masks(first)
        kst = _lane_stack(kp_ref, kc_ref, 0) + _lane_stack(kp_ref, kc_ref, 1)
        vst = _lane_stack(vp_ref, vc_ref, 0) + _lane_stack(vp_ref, vc_ref, 1)
        kts = [_row_stack(ktp_ref, ktc_ref, g) for g in range(N_KV_HEADS)]
        lane = lax.broadcasted_iota(jnp.int32, (2 * ATTN_BLOCK, 128), 1)
        low = lane < HEAD_DIM
        slane = lax.broadcasted_iota(jnp.int32, (1, 128), 1)
        dkz = [jnp.zeros((2 * ATTN_BLOCK, 128), F32) for _ in range(N_KV_HEADS)]
        dvz = [jnp.zeros((2 * ATTN_BLOCK, 128), F32) for _ in range(N_KV_HEADS)]
        dsink = jnp.zeros((1, 128), F32)
        groups = range(N_KV_HEADS)
        qcat, docat, s8, dp8 = {}, {}, {}, {}
        ds_u, p_u, dsinks = {}, {}, []

        def scores(g):
            qcat[g], docat[g] = _pair_rows(q_ref, g), _pair_rows(do_ref, g)
            s8[g] = _dot_nt(kst[g], qcat[g]) * scale
            dp8[g] = _dot_nt(vst[g], docat[g])

        def algebra(g):
            for hh in range(2):
                pn, ps = _attn_softmax(_fold(prevm, s8[g], hh), kill, _pair_sinks(sink_ref, g, hh))
                dp = _fold(prevm, dp8[g], hh)
                delta = jnp.sum(pn * dp, axis=0, keepdims=True)
                sd = ps * delta
                dsinks.append(jnp.where(slane == 4 * g + hh, -jnp.sum(sd[:, :ATTN_BLOCK]), 0.0)
                              + jnp.where(slane == 4 * g + 2 + hh, -jnp.sum(sd[:, ATTN_BLOCK:]), 0.0))
                ds_u[g, hh] = _unfold(prevm, pn * (dp - delta) * scale)
                p_u[g, hh] = _unfold(prevm, pn)

        def grads(g):
            dqt = _dot(kts[g][0], ds_u[g, 0]) + _dot(kts[g][1], ds_u[g, 1])
            dq_ref[:, 256 * g:256 * g + 128] = dqt[:, :ATTN_BLOCK].T
            dq_ref[:, 256 * g + 128:256 * (g + 1)] = dqt[:, ATTN_BLOCK:].T
            for hh in range(2):
                own = low if hh == 0 else jnp.logical_not(low)
                dk_h = jnp.where(own, _dot(ds_u[g, hh], qcat[g]), 0.0)
                dv_h = jnp.where(own, _dot(p_u[g, hh], docat[g]), 0.0)
                if hh != g % 2:
                    dk_h = pltpu.roll(dk_h, HEAD_DIM, 1)
                    dv_h = pltpu.roll(dv_h, HEAD_DIM, 1)
                dkz[g] = dkz[g] + dk_h
                dvz[g] = dvz[g] + dv_h

        for stage, g in ATTN_BWD_ORDER:
            (scores, algebra, grads)[stage](g)
        for t in dsinks:
            dsink = dsink + t
        pairs = range(N_KV_HEADS // 2)
        return dsink, [dkz[2 * gp] + dkz[2 * gp + 1] for gp in pairs], [dvz[2 * gp] + dvz[2 * gp + 1] for gp in pairs]

    def body(q_ref, kp_ref, kc_ref, vp_ref, vc_ref, ktp_ref, ktc_ref, sink_ref, do_ref,
             dq_ref, dk_ref, dv_ref, ds_ref):
        i = pl.program_id(0)

        @pl.when(i == 0)
        def _():
            ds_ref[...] = jnp.zeros_like(ds_ref)

        res = []
        for b in range(nbk):
            rs = slice(ATTN_BLOCK * b, ATTN_BLOCK * (b + 1))
            ps = slice(ATTN_BLOCK * (b - 1), ATTN_BLOCK * b)
            res.append(block(
                q_ref.at[rs, :], kp_ref if b == 0 else kc_ref.at[ps, :], kc_ref.at[rs, :],
                vp_ref if b == 0 else vc_ref.at[ps, :], vc_ref.at[rs, :],
                ktp_ref if b == 0 else ktc_ref.at[:, ps], ktc_ref.at[:, rs], sink_ref, do_ref.at[rs, :],
                dq_ref.at[rs, :], i == 0 if b == 0 else False))
        dsink = res[0][0]
        for r in res[1:]:
            dsink = dsink + r[0]
        ds_ref[...] += dsink
        for gp in range(N_KV_HEADS // 2):
            cols = slice(128 * gp, 128 * (gp + 1))
            for b in range(nbk):
                cur = pl.ds(pl.multiple_of(i * rows + ATTN_BLOCK * b, ATTN_BLOCK), ATTN_BLOCK)
                dkb, dvb = res[b][1][gp][ATTN_BLOCK:, :], res[b][2][gp][ATTN_BLOCK:, :]
                if b + 1 < nbk:
                    dkb, dvb = dkb + res[b + 1][1][gp][:ATTN_BLOCK, :], dvb + res[b + 1][2][gp][:ATTN_BLOCK, :]
                dk_ref[cur, cols] = dkb
                dv_ref[cur, cols] = dvb

            @pl.when(i > 0)
            def _():
                prv = pl.ds(pl.multiple_of(jnp.maximum(i * rows - ATTN_BLOCK, 0), ATTN_BLOCK), ATTN_BLOCK)
                dk_ref[prv, cols] += res[0][1][gp][:ATTN_BLOCK, :]
                dv_ref[prv, cols] += res[0][2][gp][:ATTN_BLOCK, :]

    prev = lambda i: jnp.maximum(nbk * i - 1, 0)
    whole = lambda w: pl.BlockSpec((T, w), lambda i: (0, 0))
    return _call(
        body, name=name, grid=(T // rows,), scratch_shapes=[], ins=[qkv, qkv, qkv, qkv, qkv, kt, kt, sinks, dya],
        carry=carry,
        in_specs=[pl.BlockSpec((rows, D), lambda i: (i, 0)),
                  pl.BlockSpec((ATTN_BLOCK, 256), lambda i: (prev(i), 4)),
                  pl.BlockSpec((rows, 256), lambda i: (i, 4)),
                  pl.BlockSpec((ATTN_BLOCK, 256), lambda i: (prev(i), 5)),
                  pl.BlockSpec((rows, 256), lambda i: (i, 5)),
                  pl.BlockSpec((256, ATTN_BLOCK), lambda i: (0, prev(i))),
                  pl.BlockSpec((256, rows), lambda i: (0, i)),
                  pl.BlockSpec((1, N_Q_HEADS), lambda i: (0, 0)),
                  pl.BlockSpec((rows, D), lambda i: (i, 0))],
        out_specs=[pl.BlockSpec((rows, D), lambda i: (i, 0)), whole(256), whole(256),
                   pl.BlockSpec((1, 128), lambda i: (0, 0))],
        out_shape=[jax.ShapeDtypeStruct((T, D), F32), jax.ShapeDtypeStruct((T, 256), F32),
                   jax.ShapeDtypeStruct((T, 256), F32), jax.ShapeDtypeStruct((1, 128), F32)])


HG_FWD = (256, 8, 512)
HG_BWD = (128, 8, 512)


def _chunk_sum(tri, x):
    w = x.shape[1]
    h1 = x.astype(BF16)
    h2 = (x - h1.astype(F32)).astype(BF16)
    r = _dot(tri, jnp.concatenate([h1, h2], axis=1))
    return r[:, :w] + r[:, w:]


def _chunk_tri(n, upper=False):
    ri = lax.broadcasted_iota(jnp.int32, (n, n), 0)
    ci = lax.broadcasted_iota(jnp.int32, (n, n), 1)
    same_chunk = jnp.bitwise_xor(ri, ci) < HGRN_CHUNK
    return jnp.where(((ri <= ci) if upper else (ri >= ci)) & same_chunk, 1.0, 0.0).astype(BF16)


def _chunks(t):
    return [t[HGRN_CHUNK * c:HGRN_CHUNK * (c + 1), :] for c in range(t.shape[0] // HGRN_CHUNK)]


def _lower_bound(lbl_ref):
    l0, l1 = lbl_ref[0:1, :], lbl_ref[1:2, :]
    m = jnp.maximum(l0, l1)
    e0, e1 = jnp.exp(l0 - m), jnp.exp(l1 - m)
    return e0 / (e0 + e1)


def _heads(t):
    return [t[:, 128 * h:128 * (h + 1)] for h in range(t.shape[1] // 128)]


def _per_head(fn, *wide):
    return jnp.concatenate([fn(*parts) for parts in zip(*[_heads(t) for t in wide])], axis=1)


def _hgrn_sub(fl, qh, vv, lb, tril_b):
    w = fl.shape[1]
    sg = _sigmoid(fl)
    f = lb + (1.0 - lb) * sg
    k = 1.0 - f
    gc = _chunk_sum(tril_b, jnp.log(f))
    gl_c = [t[HGRN_CHUNK - 1:HGRN_CHUNK, :] for t in _chunks(gc)]
    gl = jnp.concatenate([jnp.broadcast_to(g, (HGRN_CHUNK, w)) for g in gl_c], axis=0)
    sq = _sigmoid(qh)
    eg = jnp.exp(gc)
    eng = jnp.exp(-gc)
    elg = jnp.exp(gl - gc)
    qd = qh * sq * eg
    ki = k * eng
    ke = k * elg
    qd_b, ki_b, ke_b, v_b = (t.astype(BF16) for t in (qd, ki, ke, vv))
    am_b = [jnp.where(tril_b > 0, _dot_nt(qq, kk), 0.0).astype(BF16) for qq, kk in zip(_heads(qd_b), _heads(ki_b))]
    return dict(sg=sg, f=f, sq=sq, eg=eg, eng=eng, elg=elg, qd=qd, ki=ki, ke=ke, egl=[jnp.exp(g) for g in gl_c],
                am_b=am_b, qd_b=qd_b, ki_b=ki_b, ke_b=ke_b, v_b=v_b)


def _hgrn_out(q, st_b):
    outs = []
    for h, (qd_h, v_h) in enumerate(zip(_heads(q["qd_b"]), _heads(q["v_b"]))):
        inter = [_dot_nt(qc, s) for qc, s in zip(_chunks(qd_h), st_b[h])]
        outs.append(_dot(q["am_b"][h], v_h) + jnp.concatenate(inter, axis=0))
    return jnp.concatenate(outs, axis=1)


def _head_rms(o):
    return _per_head(lambda t: jnp.broadcast_to(
        lax.rsqrt(jnp.mean(t * t, axis=-1, keepdims=True) + RMS_EPS), t.shape), o)


def _hgrn_fwd(name, hg, lb_logits, norm_g, cfg):
    T = hg.shape[0]
    sub_rows, hps, th = min(cfg[0], T), cfg[1], min(cfg[2], T)
    cps = sub_rows // HGRN_CHUNK
    nc = th // HGRN_CHUNK

    def body(fl_ref, qh_ref, v_ref, og_ref, lbl_ref, ng_ref, y_ref, ss_ref, st):
        @pl.when(pl.program_id(1) == 0)
        def _():
            st[...] = jnp.zeros_like(st)

        lbs = _lower_bound(lbl_ref)
        ng = jnp.tile(ng_ref[...], (1, hps))
        tril_b = _chunk_tri(sub_rows)

        def sub(si, carry):
            rows = pl.ds(pl.multiple_of(si * sub_rows, sub_rows), sub_rows)
            q = _hgrn_sub(fl_ref[rows, :], qh_ref[rows, :], v_ref[rows, :], lbs, tril_b)
            v_hc = [_chunks(t) for t in _heads(q["v_b"])]
            k_hc = [_chunks(t) for t in _heads(q["ke_b"])]
            s = [st[h] for h in range(hps)]
            st_b = [[] for _ in range(hps)]
            for c in range(cps):
                egl = _heads(q["egl"][c])
                for h in range(hps):
                    ss_ref[h, si * cps + c] = s[h]
                    st_b[h].append(s[h].astype(BF16))
                    s[h] = s[h] * egl[h] + _dot_tn(v_hc[h][c], k_hc[h][c])
            for h in range(hps):
                st[h] = s[h]
            o = _hgrn_out(q, st_b)
            og = og_ref[rows, :]
            y_ref[rows, :] = (o * _head_rms(o) * ng * (og * _sigmoid(og))).astype(BF16)
            return carry

        lax.fori_loop(0, th // sub_rows, sub, 0)

    wd = 128 * hps
    col = lambda j: pl.BlockSpec((th, wd), lambda h, t: (t, (8 // hps) * j + h))
    return pl.pallas_call(
        body, name=name, grid=(HGRN_HEADS // hps, T // th),
        in_specs=[col(0), col(1), col(2), col(3), pl.BlockSpec((2, wd), lambda h, t: (0, h)),
                  pl.BlockSpec((1, 128), lambda h, t: (0, 0))],
        out_specs=[pl.BlockSpec((th, wd), lambda h, t: (t, h)),
                   pl.BlockSpec((hps, nc, 128, 128), lambda h, t: (h, t, 0, 0))],
        out_shape=[jax.ShapeDtypeStruct((T, D), BF16),
                   jax.ShapeDtypeStruct((HGRN_HEADS, T // HGRN_CHUNK, 128, 128), F32)],
        scratch_shapes=[pltpu.VMEM((hps, 128, 128), F32)],
        compiler_params=_params("parallel", "arbitrary"),
    )(hg, hg, hg, hg, lb_logits, norm_g)


def _hgrn_bwd(name, hg, lb_logits, norm_g, sstart, dyh, cfg, carry=None):
    T = hg.shape[0]
    sub_rows, hps, th = min(cfg[0], T), cfg[1], min(cfg[2], T)
    cps = sub_rows // HGRN_CHUNK
    nc = th // HGRN_CHUNK
    nt = T // th
    wd = 128 * hps

    def body(fl_ref, qh_ref, v_ref, og_ref, lbl_ref, ng_ref, ss_ref, dy_ref,
             dfl_ref, dqh_ref, dv_ref, dog_ref, dlb_ref, dng_ref, dst):
        @pl.when(pl.program_id(1) == 0)
        def _():
            dst[...] = jnp.zeros_like(dst)
            dlb_ref[...] = jnp.zeros_like(dlb_ref)
            dng_ref[...] = jnp.zeros_like(dng_ref)

        lb = _lower_bound(lbl_ref)
        ng = jnp.tile(ng_ref[...], (1, hps))
        last = lax.broadcasted_iota(jnp.int32, (HGRN_CHUNK, wd), 0) == HGRN_CHUNK - 1
        nsub = th // sub_rows
        cat0 = lambda parts: jnp.concatenate(parts, axis=0)
        tril_b, triu_b = _chunk_tri(sub_rows), _chunk_tri(sub_rows, upper=True)

        def sub(step, carry):
            si = nsub - 1 - step
            rows = pl.ds(pl.multiple_of(si * sub_rows, sub_rows), sub_rows)
            qh, og, dy = qh_ref[rows, :], og_ref[rows, :], dy_ref[rows, :]
            q = _hgrn_sub(fl_ref[rows, :], qh, v_ref[rows, :], lb, tril_b)
            s_in = [[ss_ref[h, si * cps + c] for c in range(cps)] for h in range(hps)]
            st_b = [[s.astype(BF16) for s in row] for row in s_in]
            o = _hgrn_out(q, st_b)
            rr = _head_rms(o)
            oh = o * rr
            sog = _sigmoid(og)
            dog_ref[rows, :] = (dy * oh * ng * (sog * (1.0 + og * (1.0 - sog)))).astype(BF16)
            don = dy * (og * sog)
            dng_w = jnp.sum(don * oh, axis=0, keepdims=True)
            dd = don * ng
            mean_h = _per_head(lambda t: jnp.broadcast_to(jnp.mean(t, axis=-1, keepdims=True), t.shape), dd * oh)
            do_b = (rr * (dd - oh * mean_h)).astype(BF16)
            do_h, qd_h, ki_h, ke_h, v_h = (_heads(t) for t in (do_b, q["qd_b"], q["ki_b"], q["ke_b"], q["v_b"]))
            da_b = [jnp.where(tril_b > 0, _dot_nt(do_h[h], v_h[h]), 0.0).astype(BF16) for h in range(hps)]
            do_c, qd_c, ke_c, v_c = ([_chunks(t) for t in hs] for hs in (do_h, qd_h, ke_h, v_h))
            dsp = [[None] * cps for _ in range(hps)]
            dgl_dec = [[None] * cps for _ in range(hps)]
            d = [dst[h] for h in range(hps)]
            for c in reversed(range(cps)):
                egl = _heads(q["egl"][c])
                for h in range(hps):
                    dsp[h][c] = d[h]
                    dgl_dec[h][c] = jnp.sum(d[h] * s_in[h][c], axis=0, keepdims=True) * egl[h]
                    d[h] = d[h] * egl[h] + _dot_tn(do_c[h][c], qd_c[h][c])
            for h in range(hps):
                dst[h] = d[h]
                dng_ref[h] += dng_w[:, 128 * h:128 * (h + 1)]
            dsp_b = [[t.astype(BF16) for t in row] for row in dsp]
            dv_ref[rows, :] = jnp.concatenate(
                [_dot_tn(q["am_b"][h], do_h[h]) + cat0([_dot_nt(ke_c[h][c], dsp_b[h][c]) for c in range(cps)])
                 for h in range(hps)], axis=1).astype(BF16)
            dqd = jnp.concatenate(
                [_dot(da_b[h], ki_h[h]) + cat0([_dot(do_c[h][c], st_b[h][c]) for c in range(cps)])
                 for h in range(hps)], axis=1)
            dki = jnp.concatenate([_dot_tn(da_b[h], qd_h[h]) for h in range(hps)], axis=1)
            dke = jnp.concatenate([cat0([_dot(v_c[h][c], dsp_b[h][c]) for c in range(cps)]) for h in range(hps)], axis=1)
            dkk = dke * q["ke"]
            dgl = [jnp.sum(t, axis=0, keepdims=True) + jnp.concatenate([dgl_dec[h][c] for h in range(hps)], axis=1)
                   for c, t in enumerate(_chunks(dkk))]
            dgc = dqd * q["qd"] - dki * q["ki"] - dkk + cat0([jnp.where(last, g, 0.0) for g in dgl])
            dlf = _chunk_sum(triu_b, dgc)
            df = dlf / q["f"] - (dki * q["eng"] + dke * q["elg"])
            sg = q["sg"]
            dlb_ref[...] += jnp.sum(df * (1.0 - sg), axis=0, keepdims=True)
            dfl_ref[rows, :] = (df * (1.0 - lb) * sg * (1.0 - sg)).astype(BF16)
            sq = q["sq"]
            dqh_ref[rows, :] = (dqd * q["eg"] * (sq * (1.0 + qh * (1.0 - sq)))).astype(BF16)
            return carry

        lax.fori_loop(0, nsub, sub, 0)
    col = lambda j: pl.BlockSpec((th, wd), lambda h, t: (nt - 1 - t, (8 // hps) * j + h))
    out = pl.BlockSpec((th, wd), lambda h, t: (nt - 1 - t, h))
    return _call(
        body, name=name, grid=(HGRN_HEADS // hps, nt),
        in_specs=[col(0), col(1), col(2), col(3), pl.BlockSpec((2, wd), lambda h, t: (0, h)),
                  pl.BlockSpec((1, 128), lambda h, t: (0, 0)),
                  pl.BlockSpec((hps, nc, 128, 128), lambda h, t: (h, nt - 1 - t, 0, 0)), out],
        out_specs=[out, out, out, out, pl.BlockSpec((1, wd), lambda h, t: (0, h)),
                   pl.BlockSpec((hps, 1, 128), lambda h, t: (h, 0, 0))],
        out_shape=[jax.ShapeDtypeStruct((T, D), BF16)] * 4 + [jax.ShapeDtypeStruct((1, D), F32),
                                                              jax.ShapeDtypeStruct((HGRN_HEADS, 1, 128), F32)],
        scratch_shapes=[pltpu.VMEM((hps, 128, 128), F32)],
        ins=[hg, hg, hg, hg, lb_logits, norm_g, sstart, dyh], carry=carry)


def _mix_fwd(name, ya, yh, gates, xh1, g1, b1, wall, tm):
    T = ya.shape[0]

    def body(ya_ref, yh_ref, ga_ref, gh_ref, xh_ref, g_ref, b_ref, wall_ref,
             xo_ref, r_ref, pa_ref, ph_ref, mg_ref, wpa, wph, wo, sem):
        @pl.when(pl.program_id(0) == 0)
        def _():
            _load_rows(wall_ref, [(wpa, RIN, RP), (wph, RIN + RP, RP), (wo, RIN + 2 * RP, RP)], sem)

        parts = [slice(p, min(p + PART_ROWS, tm)) for p in range(0, tm, PART_ROWS)]
        pas = [_dot(ya_ref[rs, :], wpa[...]).astype(BF16) for rs in parts]
        phs = [_dot(yh_ref[rs, :], wph[...]).astype(BF16) for rs in parts]
        mgs = []
        for rs, pa, ph in zip(parts, pas, phs):
            pa_ref[rs, :] = pa
            ph_ref[rs, :] = ph
            sa, sh = _sigmoid(ga_ref[rs, :].astype(F32)), _sigmoid(gh_ref[rs, :].astype(F32))
            mgs.append((sa * pa.astype(F32) + sh * ph.astype(F32)).astype(BF16))
            mg_ref[rs, :] = mgs[-1]
        mixes = [_dot(mg, wo[...]) for mg in mgs]
        for rs, mix in zip(parts, mixes):
            y1 = xh_ref[rs, :] * g_ref[...] + b_ref[...]
            xh2, r = _ln_fwd(ALPHA * y1 + mix)
            xo_ref[rs, :] = xh2
            r_ref[rs, :] = jnp.broadcast_to(r, (xh2.shape[0], 128))

    row = lambda w: pl.BlockSpec((tm, w), lambda i: (i, 0))
    vec = pl.BlockSpec((1, D), lambda i: (0, 0))
    return pl.pallas_call(
        body, name=name, grid=(T // tm,),
        in_specs=[row(D), row(D), pl.BlockSpec((tm, D), lambda i: (i, 0)), pl.BlockSpec((tm, D), lambda i: (i, 1)),
                  row(D), vec, vec, HBM_SPEC],
        out_specs=[row(D), row(128), row(D), row(D), row(D)],
        out_shape=[jax.ShapeDtypeStruct((T, D), F32), jax.ShapeDtypeStruct((T, 128), F32)]
        + [jax.ShapeDtypeStruct((T, D), BF16)] * 3,
        scratch_shapes=[pltpu.VMEM((D, D), BF16)] * 3 + [pltpu.SemaphoreType.DMA((3 * NDEV,))],
        compiler_params=_params("arbitrary"),
    )(ya, yh, gates, gates, xh1, g1, b1, wall)


def _mix_bwd(name, dy2, xh2, r2, g2, gates, pa, ph, wall, tm, carry=None):
    T = dy2.shape[0]

    def body(dy_ref, xh_ref, r_ref, g_ref, ga_ref, gh_ref, pa_ref, ph_ref, wall_ref,
             dz_ref, dmix_ref, dpa_ref, dph_ref, dga_ref, dgh_ref, dya_ref, dyh_ref, dg_ref, db_ref,
             wpa, wph, wo, sem):
        @pl.when(pl.program_id(0) == 0)
        def _():
            _load_rows(wall_ref, [(wpa, RIN, RP), (wph, RIN + RP, RP), (wo, RIN + 2 * RP, RP)], sem)
            dg_ref[...] = jnp.zeros_like(dg_ref)
            db_ref[...] = jnp.zeros_like(db_ref)

        parts = [slice(p, min(p + PART_ROWS, tm)) for p in range(0, tm, PART_ROWS)]
        dmix = []
        for rs in parts:
            dz, dgp, dbp = _ln_bwd(dy_ref[rs, :], xh_ref[rs, :], r_ref[rs, :1], g_ref[...])
            dg_ref[...] += dgp
            db_ref[...] += dbp
            dz_ref[rs, :] = dz
            dmix.append(dz.astype(BF16))
            dmix_ref[rs, :] = dmix[-1]
        dmgs = [_dot_nt(d, wo[...]) for d in dmix]
        dpas, dphs = [], []
        for rs, dmg in zip(parts, dmgs):
            sa, sh = _sigmoid(ga_ref[rs, :].astype(F32)), _sigmoid(gh_ref[rs, :].astype(F32))
            dga_ref[rs, :] = (dmg * pa_ref[rs, :].astype(F32) * sa * (1.0 - sa)).astype(BF16)
            dgh_ref[rs, :] = (dmg * ph_ref[rs, :].astype(F32) * sh * (1.0 - sh)).astype(BF16)
            dpas.append((dmg * sa).astype(BF16))
            dphs.append((dmg * sh).astype(BF16))
            dpa_ref[rs, :] = dpas[-1]
            dph_ref[rs, :] = dphs[-1]
        for rs, dpa, dph in zip(parts, dpas, dphs):
            dya_ref[rs, :] = _dot_nt(dpa, wpa[...]).astype(BF16)
            dyh_ref[rs, :] = _dot_nt(dph, wph[...])

    row = lambda w: pl.BlockSpec((tm, w), lambda i: (i, 0))
    vec = pl.BlockSpec((1, D), lambda i: (0, 0))
    return _call(
        body, name=name, grid=(T // tm,),
        in_specs=[row(D), row(D), row(128), vec, pl.BlockSpec((tm, D), lambda i: (i, 0)),
                  pl.BlockSpec((tm, D), lambda i: (i, 1)), row(D), row(D), HBM_SPEC],
        out_specs=[row(D)] * 8 + [vec, vec],
        out_shape=[jax.ShapeDtypeStruct((T, D), F32)] + [jax.ShapeDtypeStruct((T, D), BF16)] * 6
        + [jax.ShapeDtypeStruct((T, D), F32)] + [jax.ShapeDtypeStruct((1, D), F32)] * 2,
        scratch_shapes=[pltpu.VMEM((D, D), BF16)] * 3 + [pltpu.SemaphoreType.DMA((3 * NDEV,))],
        ins=[dy2, xh2, r2, g2, gates, gates, pa, ph, wall], carry=carry)


def _adam(w, g, m, v):
    m = ADAM_B1 * m + (1.0 - ADAM_B1) * g
    v = ADAM_B2 * v + (1.0 - ADAM_B2) * (g * g)
    m_hat = m / (1.0 - ADAM_B1 ** ADAM_STEP)
    v_hat = v / (1.0 - ADAM_B2 ** ADAM_STEP)
    delta = -ADAM_LR * (m_hat / (jnp.sqrt(v_hat) + ADAM_EPS) + ADAM_WD * w)
    return delta, m, v


def _grad_steps(name, items, cb, carry=None):
    n = len(items)
    pieces = [len(it[0]) for it in items]

    def body(*refs):
        it = iter(refs[:sum(pieces) + 3 * n])
        outs = refs[sum(pieces) + 3 * n:]
        for i in range(n):
            accs = []
            for _ in range(pieces[i]):
                r_ref = next(it)
                acc = r_ref[0].astype(F32)
                for k in range(1, NDEV):
                    acc = acc + r_ref[k].astype(F32)
                accs.append(acc)
            acc = accs[0] if len(accs) == 1 else jnp.concatenate(accs, axis=0)
            w_ref, m_ref, v_ref = next(it), next(it), next(it)
            g_ref, d_ref, mo_ref, vo_ref = outs[4 * i:4 * i + 4]
            g_ref[...] = acc
            d, mm, vv = _adam(w_ref[...], acc, m_ref[...], v_ref[...])
            d_ref[...] = d
            mo_ref[...] = mm
            vo_ref[...] = vv

    in_specs, out_specs, out_shape, ins = [], [], [], []
    for recv, w, m, v in items:
        blk = pl.BlockSpec((w.shape[0], cb), lambda j: (0, j))
        in_specs += [pl.BlockSpec((NDEV, rp.shape[1], cb), lambda j: (0, 0, j)) for rp in recv] + [blk, blk, blk]
        out_specs += [blk] * 4
        out_shape += [jax.ShapeDtypeStruct(w.shape, F32)] * 4
        ins += list(recv) + [w, m, v]
    outs, ex = _call(body, name=name, grid=(D // cb,), in_specs=in_specs, out_specs=out_specs, out_shape=out_shape,
                     scratch_shapes=[], ins=ins, carry=carry)
    return [tuple(outs[4 * i:4 * i + 4]) for i in range(n)], ex


_SMALL = [("ln1_g", D), ("ln1_b", D), ("ln2_g", D), ("ln2_b", D), ("ln3_g", D), ("ln3_b", D),
          ("b_in", DIN), ("lb", D), ("attn_sinks", 128), ("hgrn_norm_g", 128), ("loss", 128)]
_SMALL_OFF = {}
_o = 0
for _n, _w in _SMALL:
    _SMALL_OFF[_n] = (_o, _w)
    _o += _w
PACK = _o


def _small_step(name, slots, small_w, small_m, small_v):
    names = ["ln1_g", "ln1_b", "ln2_g", "ln2_b", "ln3_g", "ln3_b", "b_in", "attn_sinks", "hgrn_lb_logits", "hgrn_norm_g"]
    np_ = len(names)

    def body(*refs):
        s_ref = refs[0]
        w_refs = refs[1:1 + np_]
        m_refs = refs[1 + np_:1 + 2 * np_]
        v_refs = refs[1 + 2 * np_:1 + 3 * np_]
        loss_ref, outs = refs[1 + 3 * np_], refs[2 + 3 * np_:]
        tot = s_ref[0]
        for k in range(1, NDEV):
            tot = tot + s_ref[k]

        def part(n):
            o, w = _SMALL_OFF[n]
            return tot[:, o:o + w]

        loss_ref[...] = part("loss")
        for i, n in enumerate(names):
            w = w_refs[i][...]
            if n == "hgrn_lb_logits":
                m_ = jnp.maximum(w[0:1, :], w[1:2, :])
                e0, e1 = jnp.exp(w[0:1, :] - m_), jnp.exp(w[1:2, :] - m_)
                p0 = e0 / (e0 + e1)
                t = p0 * (1.0 - p0) * part("lb")
                g = jnp.concatenate([t, -t], axis=0)
            elif n == "attn_sinks":
                g = part(n)[:, :N_Q_HEADS]
            else:
                g = part(n)
            d, mm, vv = _adam(w, g, m_refs[i][...], v_refs[i][...])
            outs[4 * i][...] = g
            outs[4 * i + 1][...] = d
            outs[4 * i + 2][...] = mm
            outs[4 * i + 3][...] = vv

    out_shape = [jax.ShapeDtypeStruct((1, 128), F32)]
    for n in names:
        out_shape += [jax.ShapeDtypeStruct(small_w[n].shape, F32)] * 4
    return pl.pallas_call(
        body, name=name, out_shape=out_shape,
        compiler_params=pltpu.CompilerParams(vmem_limit_bytes=VMEM_LIMIT),
    )(slots, *[small_w[n] for n in names], *[small_m[n] for n in names], *[small_v[n] for n in names]), names


def _tile(T, pref):
    return min(T, pref)


def kernel(x, ln1_g, ln1_b, ffn1_w1, ffn1_w3, ffn1_w2, ln2_g, ln2_b, w_in, b_in, attn_sinks, hgrn_lb_logits, hgrn_norm_g, w_proj_attn, w_proj_hgrn, w_out, ln3_g, ln3_b, ffn2_w1, ffn2_w3, ffn2_w2, loss_target, m_ln1_g, m_ln1_b, m_ffn1_w1, m_ffn1_w3, m_ffn1_w2, m_ln2_g, m_ln2_b, m_w_in, m_b_in, m_attn_sinks, m_hgrn_lb_logits, m_hgrn_norm_g, m_w_proj_attn, m_w_proj_hgrn, m_w_out, m_ln3_g, m_ln3_b, m_ffn2_w1, m_ffn2_w3, m_ffn2_w2, v_ln1_g, v_ln1_b, v_ffn1_w1, v_ffn1_w3, v_ffn1_w2, v_ln2_g, v_ln2_b, v_w_in, v_b_in, v_attn_sinks, v_hgrn_lb_logits, v_hgrn_norm_g, v_w_proj_attn, v_w_proj_hgrn, v_w_out, v_ln3_g, v_ln3_b, v_ffn2_w1, v_ffn2_w3, v_ffn2_w2):
    T = x.shape[1]
    xs = x[0]
    tgt = loss_target[0]
    tm = _tile(T, 256)
    tm2 = _tile(T, 512)
    tk = _tile(T, 2048)
    tk4 = _tile(T, 4096)

    t_bf = lambda w: w[0].T.astype(BF16)
    n_bf = lambda w: w[0].astype(BF16)
    ffn_shard = lambda w1, w3, w2: jnp.concatenate([t_bf(w1), t_bf(w3), n_bf(w2)], axis=0)
    mix_shard = jnp.concatenate([t_bf(w_in), n_bf(w_proj_attn), n_bf(w_proj_hgrn), n_bf(w_out)], axis=0)
    (ffn1_all,) = _exchange_call("gather_ffn1", _gather_exchange(ffn_shard(ffn1_w1, ffn1_w3, ffn1_w2)))
    ffn_offs = (0, RF, 2 * RF)
    rope = _rope_table(T)

    (xh1, r1, a1, b1, xb0), (mix_all,) = _ffn_fwd("ffn1_fwd", xs, None, ffn1_all, ffn_offs, tm2,
                                                  carry=_gather_exchange(mix_shard))
    (qkv, kt, vt, hg, gates, y1b), (ffn2_all,) = _inproj_fwd(
        "inproj_fwd", xh1, ln1_g, ln1_b, mix_all, b_in, rope, tm2,
        carry=_gather_exchange(ffn_shard(ffn2_w1, ffn2_w3, ffn2_w2)))
    ya = _attn_fwd("attn_fwd", qkv, vt, attn_sinks)
    yh, sstart = _hgrn_fwd("hgrn_fwd", hg, hgrn_lb_logits, hgrn_norm_g, HG_FWD)
    xh2, r2, pa, ph, merged = _mix_fwd("mix_fwd", ya, yh, gates, xh1, ln1_g, ln1_b, mix_all, tm2)
    (xh3, r3, a2, b2, y2b, loss_part), _ = _ffn_fwd("ffn2_fwd", xh2, (ln2_g, ln2_b), ffn2_all, ffn_offs, tm2,
                                                    loss=(ln3_g, ln3_b, tgt))

    (dy2, dab2, u2, df2, dg3, db3), _ = _ffn_bwd("ffn2_bwd", ("loss", ln3_b, tgt), xh3, r3, ln3_g, a2, b2, ffn2_all,
                                                 ffn_offs, tm)
    recv = {}
    g_ffn2_13, _ = _wgrad("wgrad_ffn2_w13", dab2, y2b, DFF // 2, tk)
    g_ffn2_2, _ = _wgrad("wgrad_ffn2_w2", u2, df2, DFF // 2, tk)
    (dz2, dmix, dpa, dph, dga, dgh, dya, dyh, dg2, db2), _ = _mix_bwd(
        "mix_bwd", dy2, xh2, r2, ln2_g, gates, pa, ph, mix_all, tm2)
    g_wo, _ = _wgrad("wgrad_w_out", merged, dmix, D, tk4)
    g_pa, _ = _wgrad("wgrad_w_pa", ya, dpa, D, tk4)
    g_ph, _ = _wgrad("wgrad_w_ph", yh, dph, D, tk4)
    (dq, dk, dv, dsink), _ = _attn_bwd("attn_bwd", qkv, kt, attn_sinks, dya)
    (dfl, dqh, dih, dog, dlb, dng), (recv["ffn2_w2"], recv["w_proj_attn"], recv["w_proj_hgrn"], recv["w_out"]) = \
        _hgrn_bwd("hgrn_bwd", hg, hgrn_lb_logits, hgrn_norm_g, sstart, dyh, HG_BWD,
                  carry=_grad_exchange([g_ffn2_2, g_pa, g_ph, g_wo], [0, 0, 0, 0], [RF, RP, RP, RP]))
    (dy1, dproj, dbin), (recv["ffn2_w1"], recv["ffn2_w3"]) = _inproj_bwd(
        "inproj_bwd", dq, dk, dv, (dfl, dqh, dih, dog), (dga, dgh), dz2, mix_all, rope, tm2,
        carry=_grad_exchange([g_ffn2_13, g_ffn2_13], [0, DFF], [RF, RF]))
    g_win, _ = _wgrad("wgrad_w_in", dproj, y1b, DIN // 4, tk)
    win_rows = (400, 288, 272)
    win_base = (0, 400, 688)
    (dz1, dab1, u1, df1, dg1, db1), (rw0,) = _ffn_bwd(
        "ffn1_bwd", ("dy", dy1), xh1, r1, ln1_g, a1, b1, ffn1_all, ffn_offs, tm2, with_dx=False,
        carry=_grad_exchange([g_win], win_base[:1], win_rows[:1], [RIN]))
    g_ffn1_2, (rw1,) = _wgrad("wgrad_ffn1_w2", u1, df1, DFF // 2, tk,
                              carry=_grad_exchange([g_win], win_base[1:2], win_rows[1:2], [RIN]))
    g_ffn1_1, (recv["ffn1_w2"],) = _wgrad("wgrad_ffn1_w1", dab1, xb0, DFF // 2, tk, 0, DFF,
                                          carry=_grad_exchange([g_ffn1_2], [0], [RF]))
    g_ffn1_3, (recv["ffn1_w1"],) = _wgrad("wgrad_ffn1_w3", dab1, xb0, DFF // 2, tk, DFF, DFF,
                                          carry=_grad_exchange([g_ffn1_1], [0], [RF]))
    parts = {"ln1_g": dg1, "ln1_b": db1, "ln2_g": dg2, "ln2_b": db2, "ln3_g": dg3, "ln3_b": db3, "b_in": dbin,
             "lb": dlb, "attn_sinks": dsink, "hgrn_norm_g": jnp.sum(dng, axis=0), "loss": loss_part[0:1, :]}
    packed = jnp.concatenate([parts[n] for n, _ in _SMALL], axis=1)
    gx, (recv["ffn1_w3"], rw2, small_slots) = _ffn_dx(
        "ffn1_dx", dab1, dz1, ffn1_all, ffn_offs, tm2,
        carry=_join(_grad_exchange([g_ffn1_3, g_win], [0, win_base[2]], [RF, win_rows[2]], [RF, RIN]),
                    _row_gather_exchange(packed)))
    recv["w_in"] = [rw0, rw1, rw2]

    big = [("ffn1_w1", ffn1_w1, m_ffn1_w1, v_ffn1_w1, True), ("ffn1_w3", ffn1_w3, m_ffn1_w3, v_ffn1_w3, True),
           ("ffn1_w2", ffn1_w2, m_ffn1_w2, v_ffn1_w2, False), ("w_in", w_in, m_w_in, v_w_in, True),
           ("w_proj_attn", w_proj_attn, m_w_proj_attn, v_w_proj_attn, False),
           ("w_proj_hgrn", w_proj_hgrn, m_w_proj_hgrn, v_w_proj_hgrn, False),
           ("w_out", w_out, m_w_out, v_w_out, False),
           ("ffn2_w1", ffn2_w1, m_ffn2_w1, v_ffn2_w1, True), ("ffn2_w3", ffn2_w3, m_ffn2_w3, v_ffn2_w3, True),
           ("ffn2_w2", ffn2_w2, m_ffn2_w2, v_ffn2_w2, False)]
    view = lambda t, transposed: t[0].T if transposed else t[0]
    back = lambda t, transposed: t.T[None] if transposed else t[None]
    slots = lambda n: recv[n] if isinstance(recv[n], list) else [recv[n]]
    stepped, _ = _grad_steps("steps", [(slots(n), view(w, tr), view(m, tr), view(v, tr)) for n, w, m, v, tr in big], 128)
    res = {n: tuple(back(t, tr) for t in outs) for (n, _, _, _, tr), outs in zip(big, stepped)}

    small_w = dict(ln1_g=ln1_g, ln1_b=ln1_b, ln2_g=ln2_g, ln2_b=ln2_b, ln3_g=ln3_g, ln3_b=ln3_b, b_in=b_in,
                   attn_sinks=attn_sinks, hgrn_lb_logits=hgrn_lb_logits, hgrn_norm_g=hgrn_norm_g)
    small_m = dict(ln1_g=m_ln1_g, ln1_b=m_ln1_b, ln2_g=m_ln2_g, ln2_b=m_ln2_b, ln3_g=m_ln3_g, ln3_b=m_ln3_b,
                   b_in=m_b_in, attn_sinks=m_attn_sinks, hgrn_lb_logits=m_hgrn_lb_logits, hgrn_norm_g=m_hgrn_norm_g)
    small_v = dict(ln1_g=v_ln1_g, ln1_b=v_ln1_b, ln2_g=v_ln2_g, ln2_b=v_ln2_b, ln3_g=v_ln3_g, ln3_b=v_ln3_b,
                   b_in=v_b_in, attn_sinks=v_attn_sinks, hgrn_lb_logits=v_hgrn_lb_logits, hgrn_norm_g=v_hgrn_norm_g)
    outs, names = _small_step("small_step", small_slots, small_w, small_m, small_v)
    loss = outs[0][0, 0]
    for i, n in enumerate(names):
        res[n] = tuple(outs[1 + 4 * i:5 + 4 * i])

    order = ["ln1_g", "ln1_b", "ffn1_w1", "ffn1_w3", "ffn1_w2", "ln2_g", "ln2_b", "w_in", "b_in", "attn_sinks",
             "hgrn_lb_logits", "hgrn_norm_g", "w_proj_attn", "w_proj_hgrn", "w_out", "ln3_g", "ln3_b",
             "ffn2_w1", "ffn2_w3", "ffn2_w2"]
    return (loss, gx[None], *[res[n][0] for n in order], *[res[n][1] for n in order],
            *[res[n][2] for n in order], *[res[n][3] for n in order])
```

```python
import jax
import jax.numpy as jnp
from jax import lax
from jax.experimental import pallas as pl
from jax.experimental.pallas import tpu as pltpu

F32 = jnp.float32
BF16 = jnp.bfloat16

NDEV = 8
D = 1024
DFF = 2816
RF = DFF // NDEV
DIN = 7680
RIN = DIN // NDEV
RP = D // NDEV
N_Q_HEADS = 16
N_KV_HEADS = 4
HEAD_DIM = 64
ATTN_BLOCK = 128
ROPE_THETA = 500000.0
ROPE_DIM = HEAD_DIM // 4
HGRN_HEADS = 8
HGRN_CHUNK = 64
ALPHA = 2.0 ** 0.25
LN_EPS = 1e-5
RMS_EPS = 1e-6
NEG_INF = -1e30
ADAM_LR = 0.001
ADAM_B1 = 0.9
ADAM_B2 = 0.999
ADAM_EPS = 1e-08
ADAM_WD = 0.01
ADAM_STEP = 10

QKV_W = 1536
HG_W = 4096
GATE_W = 2048
VMEM_LIMIT = 60 * 2 ** 20
PART_ROWS = 256
MESH = pl.DeviceIdType.MESH
HBM_SPEC = pl.BlockSpec(memory_space=pltpu.HBM)


def _params(*sem):
    return pltpu.CompilerParams(dimension_semantics=sem, vmem_limit_bytes=VMEM_LIMIT)


def _dot(a, b):
    return jnp.dot(a, b, preferred_element_type=F32)


def _dot_nt(a, b):
    return lax.dot_general(a, b, (((1,), (1,)), ((), ())), preferred_element_type=F32)


def _dot_tn(a, b):
    return lax.dot_general(a, b, (((0,), (0,)), ((), ())), preferred_element_type=F32)


def _sigmoid(x):
    return 0.5 * jnp.tanh(0.5 * x) + 0.5


def _ln_fwd(z):
    mu = jnp.mean(z, axis=-1, keepdims=True)
    zc = z - mu
    var = jnp.mean(zc * zc, axis=-1, keepdims=True)
    r = lax.rsqrt(var + LN_EPS)
    return zc * r, r


def _ln_bwd(dy, xh, r, g):
    dxh = dy * g
    m1 = jnp.mean(dxh, axis=-1, keepdims=True)
    m2 = jnp.mean(dxh * xh, axis=-1, keepdims=True)
    dz = r * (dxh - m1 - xh * m2)
    return dz, jnp.sum(dy * xh, axis=0, keepdims=True), jnp.sum(dy, axis=0, keepdims=True)


def _load_rows(wall_ref, pieces, sem):
    copies = []
    for dst, off, r in pieces:
        for k in range(NDEV):
            c = pltpu.make_async_copy(wall_ref.at[k, pl.ds(off, r), :], dst.at[pl.ds(k * r, r), :], sem.at[len(copies)])
            c.start()
            copies.append(c)
    for c in copies:
        c.wait()


class _Exchange:
    def __init__(self, inputs, out_shape, scratch, begin, middle, end):
        self.inputs, self.out_shape, self.scratch = inputs, out_shape, scratch
        self.begin, self.middle, self.end = begin, middle, end


def _gather_exchange(shard):
    rows, cols = shard.shape

    def ops(ins, outs, scr):
        (x_ref,), (out_ref,), (send_sems, recv_sems, local_sem) = ins, outs, scr
        x, y, c = lax.axis_index("x"), lax.axis_index("y"), lax.axis_index("c")
        me, sibling = (x, y, c), (x, y, 1 - c)
        chips = [(1 - x, y), (x, 1 - y), (1 - x, 1 - y)]

        def slot(px, py, pc):
            return out_ref.at[4 * px + 2 * py + pc]

        def copy(k, block, to, src=None):
            return pltpu.make_async_remote_copy(
                src_ref=slot(*block) if src is None else src, dst_ref=slot(*block),
                send_sem=send_sems.at[k], recv_sem=recv_sems.at[k], device_id=to, device_id_type=MESH)

        mine = lambda: pltpu.make_async_copy(x_ref, slot(*me), local_sem)
        first = lambda: [copy(0, me, sibling, src=x_ref)] + [
            copy(1 + j, me, (*chip, c), src=x_ref) for j, chip in enumerate(chips)]
        passed = lambda: [copy(4 + j, (*chip, c), sibling) for j, chip in enumerate(chips)]
        return c, me, sibling, chips, copy, mine, first, passed

    def begin(*refs):
        _, _, _, _, _, mine, first, _ = ops(*refs)
        mine().start()
        for cp in first():
            cp.start()

    def middle(*refs):
        c, me, _, chips, copy, _, _, passed = ops(*refs)
        for (j, chip), fwd in zip(enumerate(chips), passed()):
            copy(1 + j, (*chip, c), me).wait_recv()
            fwd.start()

    def end(*refs):
        c, me, sibling, chips, copy, mine, first, passed = ops(*refs)
        copy(0, sibling, me).wait_recv()
        for j, chip in enumerate(chips):
            copy(4 + j, (*chip, 1 - c), me).wait_recv()
        for cp in first() + passed():
            cp.wait_send()
        mine().wait()

    return _Exchange([shard], [jax.ShapeDtypeStruct((NDEV, rows, cols), shard.dtype)],
                     [pltpu.SemaphoreType.DMA((7,)), pltpu.SemaphoreType.DMA((7,)), pltpu.SemaphoreType.DMA],
                     begin, middle, end)


def _grad_exchange(grads, bases, rows, strides=None):
    n = len(grads)
    strides = rows if strides is None else strides

    def copies(g_refs, out_refs, scr):
        send_sems, recv_sems, local_sems = scr
        x, y, c = lax.axis_index("x"), lax.axis_index("y"), lax.axis_index("c")
        me = 4 * x + 2 * y + c
        out = []
        for i in range(n):
            r = rows[i]
            src = lambda k: g_refs[i].at[pl.ds(pl.multiple_of(bases[i] + k * strides[i], 16), r), :]
            out.append(pltpu.make_async_copy(src(me), out_refs[i].at[me], local_sems.at[i]))
            for j in range(1, NDEV):
                px, py, pc = x ^ (j >> 2), y ^ ((j >> 1) & 1), c ^ (j & 1)
                out.append(pltpu.make_async_remote_copy(
                    src_ref=src(4 * px + 2 * py + pc), dst_ref=out_refs[i].at[me],
                    send_sem=send_sems.at[i, j - 1], recv_sem=recv_sems.at[i, j - 1],
                    device_id=(px, py, pc), device_id_type=MESH))
        return out

    def begin(*refs):
        for cp in copies(*refs):
            cp.start()

    def end(*refs):
        for cp in copies(*refs):
            cp.wait()

    return _Exchange(list(grads), [jax.ShapeDtypeStruct((NDEV, r, g.shape[1]), g.dtype) for g, r in zip(grads, rows)],
                     [pltpu.SemaphoreType.DMA((n, NDEV - 1)), pltpu.SemaphoreType.DMA((n, NDEV - 1)),
                      pltpu.SemaphoreType.DMA((n,))], begin, None, end)


def _row_gather_exchange(row):
    def copies(ins, outs, scr):
        (r_ref,), (o_ref,), (send_sems, recv_sems, local_sem) = ins, outs, scr
        x, y, c = lax.axis_index("x"), lax.axis_index("y"), lax.axis_index("c")
        me = 4 * x + 2 * y + c
        out = [pltpu.make_async_copy(r_ref, o_ref.at[me], local_sem)]
        for j in range(1, NDEV):
            out.append(pltpu.make_async_remote_copy(
                src_ref=r_ref, dst_ref=o_ref.at[me], send_sem=send_sems.at[j - 1], recv_sem=recv_sems.at[j - 1],
                device_id=(x ^ (j >> 2), y ^ ((j >> 1) & 1), c ^ (j & 1)), device_id_type=MESH))
        return out

    def begin(*refs):
        for cp in copies(*refs):
            cp.start()

    def end(*refs):
        for cp in copies(*refs):
            cp.wait()

    return _Exchange([row], [jax.ShapeDtypeStruct((NDEV,) + row.shape, row.dtype)],
                     [pltpu.SemaphoreType.DMA((NDEV - 1,)), pltpu.SemaphoreType.DMA((NDEV - 1,)), pltpu.SemaphoreType.DMA],
                     begin, None, end)


def _join(a, b):
    na, oa, sa = len(a.inputs), len(a.out_shape), len(a.scratch)
    split = lambda ins, outs, scr: ((ins[:na], outs[:oa], scr[:sa]), (ins[na:], outs[oa:], scr[sa:]))

    def begin(*refs):
        pa, pb = split(*refs)
        a.begin(*pa)
        b.begin(*pb)

    def end(*refs):
        pa, pb = split(*refs)
        a.end(*pa)
        b.end(*pb)

    return _Exchange(a.inputs + b.inputs, a.out_shape + b.out_shape, a.scratch + b.scratch, begin, None, end)


def _exchange_call(name, ex):
    ni, no = len(ex.inputs), len(ex.out_shape)

    def body(*refs):
        parts = (refs[:ni], refs[ni:ni + no], refs[ni + no:])
        ex.begin(*parts)
        if ex.middle is not None:
            ex.middle(*parts)
        ex.end(*parts)

    return pl.pallas_call(body, name=name, out_shape=ex.out_shape, in_specs=[HBM_SPEC] * ni, out_specs=[HBM_SPEC] * no,
                          scratch_shapes=ex.scratch)(*ex.inputs)


def _call(body, *, name, grid, in_specs, out_specs, out_shape, scratch_shapes, ins, carry=None):
    sem = ("arbitrary",) * len(grid)
    if carry is None:
        outs = pl.pallas_call(body, name=name, grid=grid, in_specs=in_specs, out_specs=out_specs, out_shape=out_shape,
                              scratch_shapes=scratch_shapes, compiler_params=_params(*sem))(*ins)
        return outs, None
    n_in, n_out, n_scr = len(ins), len(out_shape), len(scratch_shapes)
    ci, co = len(carry.inputs), len(carry.out_shape)
    total = 1
    for g in grid:
        total *= g

    def wrapped(*refs):
        own_in, ex_in = refs[:n_in], refs[n_in:n_in + ci]
        o0 = n_in + ci
        own_out, ex_out = refs[o0:o0 + n_out], refs[o0 + n_out:o0 + n_out + co]
        s0 = o0 + n_out + co
        own_scr, ex_scr = refs[s0:s0 + n_scr], refs[s0 + n_scr:]
        step = pl.program_id(0)
        for d in range(1, len(grid)):
            step = step * grid[d] + pl.program_id(d)
        parts = (ex_in, ex_out, ex_scr)
        pl.when(step == 0)(lambda: carry.begin(*parts))
        body(*own_in, *own_out, *own_scr)
        if carry.middle is not None:
            pl.when(step == (3 * total) // 4)(lambda: carry.middle(*parts))
        pl.when(step == total - 1)(lambda: carry.end(*parts))

    outs = pl.pallas_call(
        wrapped, name=name, grid=grid, in_specs=list(in_specs) + [HBM_SPEC] * ci,
        out_specs=list(out_specs) + [HBM_SPEC] * co, out_shape=list(out_shape) + list(carry.out_shape),
        scratch_shapes=list(scratch_shapes) + list(carry.scratch), compiler_params=_params(*sem),
    )(*ins, *carry.inputs)
    return outs[:n_out], outs[n_out:]


def _ffn_fwd(name, xin, affine, wall, offs, tm, loss=None, carry=None):
    T = xin.shape[0]
    nt = T // tm

    def body(*refs):
        it = iter(refs)
        x_ref = next(it)
        if affine is not None:
            g_ref, b_ref = next(it), next(it)
        wall_ref = next(it)
        if loss is not None:
            go_ref, bo_ref, tgt_ref = next(it), next(it), next(it)
        xh_ref, r_ref, a_ref, b2_ref, yb_ref = (next(it) for _ in range(5))
        if loss is not None:
            loss_ref = next(it)
        w1, w3, w2, sem = (next(it) for _ in range(4))

        @pl.when(pl.program_id(0) == 0)
        def _():
            _load_rows(wall_ref, [(w1, offs[0], RF), (w3, offs[1], RF), (w2, offs[2], RF)], sem)
            if loss is not None:
                loss_ref[...] = jnp.zeros_like(loss_ref)

        parts = [slice(p, min(p + PART_ROWS, tm)) for p in range(0, tm, PART_ROWS)]
        ys, ybs = [], []
        for rs in parts:
            y = x_ref[rs, :]
            if affine is not None:
                y = y * g_ref[...] + b_ref[...]
            ys.append(y)
            ybs.append(y.astype(BF16))
            yb_ref[rs, :] = ybs[-1]
        ab = [(_dot_nt(yb, w1[...]).astype(BF16), _dot_nt(yb, w3[...]).astype(BF16)) for yb in ybs]
        us = []
        for rs, (a, b) in zip(parts, ab):
            a_ref[rs, :] = a
            b2_ref[rs, :] = b
            af, bf = a.astype(F32), b.astype(F32)
            us.append((af * _sigmoid(af) * bf).astype(BF16))
        fs = [_dot(u, w2[...]) for u in us]
        for rs, y, f in zip(parts, ys, fs):
            xh, r = _ln_fwd(ALPHA * y + 0.5 * f)
            xh_ref[rs, :] = xh
            r_ref[rs, :] = jnp.broadcast_to(r, (xh.shape[0], 128))
            if loss is not None:
                e = xh * go_ref[...] + bo_ref[...] - tgt_ref[rs, :]
                loss_ref[...] += jnp.sum(e * e) * (0.5 / D)

    row = lambda w: pl.BlockSpec((tm, w), lambda i: (i, 0))
    vec = pl.BlockSpec((1, D), lambda i: (0, 0))
    ins, in_specs = [xin], [row(D)]
    if affine is not None:
        ins += list(affine)
        in_specs += [vec, vec]
    ins.append(wall)
    in_specs.append(HBM_SPEC)
    if loss is not None:
        ins += list(loss)
        in_specs += [vec, vec, row(D)]
    out_shape = [jax.ShapeDtypeStruct((T, D), F32), jax.ShapeDtypeStruct((T, 128), F32),
                 jax.ShapeDtypeStruct((T, DFF), BF16), jax.ShapeDtypeStruct((T, DFF), BF16),
                 jax.ShapeDtypeStruct((T, D), BF16)]
    out_specs = [row(D), row(128), row(DFF), row(DFF), row(D)]
    if loss is not None:
        out_shape.append(jax.ShapeDtypeStruct((8, 128), F32))
        out_specs.append(pl.BlockSpec((8, 128), lambda i: (0, 0)))
    return _call(body, name=name, grid=(nt,), in_specs=in_specs, out_specs=out_specs, out_shape=out_shape,
                 scratch_shapes=[pltpu.VMEM((DFF, D), BF16)] * 3 + [pltpu.SemaphoreType.DMA((3 * NDEV,))],
                 ins=ins, carry=carry)


RING_AHEAD = 2


def _ffn_bwd(name, dy_src, xh, r, g, a, b, wall, offs, tm, with_dx=True, carry=None):
    T = xh.shape[0]
    nt = T // tm
    from_loss = dy_src[0] == "loss"
    ring = not with_dx

    def body(*refs):
        it = iter(refs)
        if from_loss:
            bo_ref, tgt_ref = next(it), next(it)
        else:
            dy_ref = next(it)
        xh_ref, r_ref, g_ref, a_ref, b_ref, wall_ref = (next(it) for _ in range(6))
        dyin_ref, dab_ref, u_ref, df_ref, dg_ref, db_ref = (next(it) for _ in range(6))
        if with_dx:
            w1, w3 = next(it), next(it)
        w2, sem = next(it), next(it)
        step = pl.program_id(0)
        if ring:
            srcs, bufs, rsem = (a_ref, b_ref), (next(it), next(it)), next(it)
            fetch = lambda s, slot: [
                pltpu.make_async_copy(src.at[pl.ds(s * tm, tm), :], buf.at[slot], rsem.at[j, slot])
                for j, (src, buf) in enumerate(zip(srcs, bufs))]

            @pl.when(step == 0)
            def _():
                for s in range(min(RING_AHEAD, nt)):
                    for c in fetch(s, s):
                        c.start()

            @pl.when(step + RING_AHEAD < nt)
            def _():
                for c in fetch(step + RING_AHEAD, lax.rem(step + RING_AHEAD, RING_AHEAD + 1)):
                    c.start()

        @pl.when(step == 0)
        def _():
            _load_rows(wall_ref, ([(w1, offs[0], RF), (w3, offs[1], RF)] if with_dx else []) + [(w2, offs[2], RF)], sem)
            dg_ref[...] = jnp.zeros_like(dg_ref)
            db_ref[...] = jnp.zeros_like(db_ref)

        if ring:
            slot = lax.rem(step, RING_AHEAD + 1)
            for c in fetch(step, slot):
                c.wait()
            a_ref, b_ref = bufs[0].at[slot], bufs[1].at[slot]

        parts = [slice(p, min(p + PART_ROWS, tm)) for p in range(0, tm, PART_ROWS)]
        gv = g_ref[...]
        dz, df = [], []
        for rs in parts:
            xhv = xh_ref[rs, :]
            if from_loss:
                dy = (xhv * gv + bo_ref[...] - tgt_ref[rs, :]) * (1.0 / D)
            else:
                dy = dy_ref[rs, :]
            dzp, dgp, dbp = _ln_bwd(dy, xhv, r_ref[rs, :1], gv)
            dg_ref[...] += dgp
            db_ref[...] += dbp
            dz.append(dzp)
            df.append((0.5 * dzp).astype(BF16))
            df_ref[rs, :] = df[-1]
        du = [_dot_nt(d, w2[...]) for d in df]
        da, dbb = [], []
        for rs, dup in zip(parts, du):
            af, bf = a_ref[rs, :].astype(F32), b_ref[rs, :].astype(F32)
            s = _sigmoid(af)
            sl = af * s
            u_ref[rs, :] = (sl * bf).astype(BF16)
            da.append((dup * bf * (s * (1.0 + af * (1.0 - s)))).astype(BF16))
            dbb.append((dup * sl).astype(BF16))
            dab_ref[rs, :DFF] = da[-1]
            dab_ref[rs, DFF:] = dbb[-1]
        for rs, dzp, dap, dbp in zip(parts, dz, da, dbb):
            dyin_ref[rs, :] = ALPHA * dzp + _dot(dap, w1[...]) + _dot(dbp, w3[...]) if with_dx else dzp

    row = lambda w: pl.BlockSpec((tm, w), lambda i: (i, 0))
    vec = pl.BlockSpec((1, D), lambda i: (0, 0))
    if from_loss:
        ins, in_specs = [dy_src[1], dy_src[2]], [vec, row(D)]
    else:
        ins, in_specs = [dy_src[1]], [row(D)]
    ins += [xh, r, g, a, b, wall]
    ab_spec = HBM_SPEC if ring else row(DFF)
    in_specs += [row(D), row(128), vec, ab_spec, ab_spec, HBM_SPEC]
    ring_scratch = ([pltpu.VMEM((RING_AHEAD + 1, tm, DFF), BF16)] * 2 +
                    [pltpu.SemaphoreType.DMA((2, RING_AHEAD + 1))]) if ring else []
    return _call(
        body, name=name, grid=(nt,), in_specs=in_specs,
        out_specs=[row(D), row(2 * DFF), row(DFF), row(D), vec, vec],
        out_shape=[jax.ShapeDtypeStruct((T, D), F32), jax.ShapeDtypeStruct((T, 2 * DFF), BF16),
                   jax.ShapeDtypeStruct((T, DFF), BF16), jax.ShapeDtypeStruct((T, D), BF16),
                   jax.ShapeDtypeStruct((1, D), F32), jax.ShapeDtypeStruct((1, D), F32)],
        scratch_shapes=[pltpu.VMEM((DFF, D), BF16)] * (3 if with_dx else 1) + [pltpu.SemaphoreType.DMA((3 * NDEV,))] +
                       ring_scratch,
        ins=ins, carry=carry)


def _ffn_dx(name, dab, dz, wall, offs, tm, carry=None):
    T = dz.shape[0]

    def body(dab_ref, dz_ref, wall_ref, o_ref, w13, sem):
        @pl.when(pl.program_id(0) == 0)
        def _():
            _load_rows(wall_ref, [(w13.at[pl.ds(0, DFF), :], offs[0], RF), (w13.at[pl.ds(DFF, DFF), :], offs[1], RF)], sem)

        for p in range(0, tm, PART_ROWS):
            rs = slice(p, min(p + PART_ROWS, tm))
            o_ref[rs, :] = ALPHA * dz_ref[rs, :] + _dot(dab_ref[rs, :], w13[...])

    row = lambda w: pl.BlockSpec((tm, w), lambda i: (i, 0))
    (out,), ex = _call(
        body, name=name, grid=(T // tm,), in_specs=[row(2 * DFF), row(D), HBM_SPEC], out_specs=[row(D)],
        out_shape=[jax.ShapeDtypeStruct((T, D), F32)],
        scratch_shapes=[pltpu.VMEM((2 * DFF, D), BF16), pltpu.SemaphoreType.DMA((2 * NDEV,))],
        ins=[dab, dz, wall], carry=carry)
    return out, ex


def _wgrad(name, a, b, bn, tk, col0=0, ncols=None, carry=None):
    T = a.shape[0]
    N = a.shape[1] if ncols is None else ncols
    nk = T // tk
    c0 = col0 // bn

    def body(a_ref, b_ref, o_ref, acc):
        k = pl.program_id(1)

        @pl.when(k == 0)
        def _():
            acc[...] = jnp.zeros_like(acc)

        acc[...] += _dot_tn(a_ref[...], b_ref[...])

        @pl.when(k == nk - 1)
        def _():
            o_ref[...] = acc[...].astype(BF16)

    (out,), ex = _call(
        body, name=name, grid=(N // bn, nk),
        in_specs=[pl.BlockSpec((tk, bn), lambda n, k: (k, n + c0)), pl.BlockSpec((tk, D), lambda n, k: (k, 0))],
        out_specs=[pl.BlockSpec((bn, D), lambda n, k: (n, 0))],
        out_shape=[jax.ShapeDtypeStruct((N, D), BF16)],
        scratch_shapes=[pltpu.VMEM((bn, D), F32)], ins=[a, b], carry=carry)
    return out, ex


def _rope_table(T):
    pos = jnp.arange(T, dtype=F32)
    inv_freq = ROPE_THETA ** (-jnp.arange(0, ROPE_DIM, 2, dtype=F32) / ROPE_DIM)
    ang = pos[:, None] * inv_freq[None, :]
    return jnp.pad(jnp.concatenate([jnp.cos(ang), jnp.sin(ang)], axis=1), ((0, 0), (0, 128 - ROPE_DIM)))


def _rope_expand(cs):
    half = ROPE_DIM // 2
    lane = lax.broadcasted_iota(jnp.int32, cs.shape, 1)
    cos = jnp.where(lane < half, cs, 0.0)
    sin = jnp.where((lane >= half) & (lane < ROPE_DIM), cs, 0.0)
    both = lambda t: t + pltpu.roll(t, HEAD_DIM, 1)
    c = jnp.where(jnp.bitwise_and(lane, HEAD_DIM - 1) < ROPE_DIM, both(cos + pltpu.roll(cos, half, 1)), 1.0)
    return c, -both(pltpu.roll(sin, 128 - half, 1)), both(sin)


def _rope(t, c, s1, s2):
    n = t.shape[1] // 128
    ct, s1t, s2t = (jnp.tile(v, (1, n)) for v in (c, s1, s2))
    w = t.shape[1]
    return t * ct + pltpu.roll(t, w - 8, 1) * s1t + pltpu.roll(t, 8, 1) * s2t


def _rope_t(dr, c, s1, s2):
    n = dr.shape[1] // 128
    ct, s1t, s2t = (jnp.tile(v, (1, n)) for v in (c, s1, s2))
    w = dr.shape[1]
    return dr * ct + pltpu.roll(dr * s1t, 8, 1) + pltpu.roll(dr * s2t, w - 8, 1)


_Q, _K, _V = (0, 1024), (1024, 256), (1280, 256)
_HG = (1536, HG_W)
_GATES = (5632, GATE_W)


def _inproj_fwd(name, xh, g, b, wall, b_in, rope, tm, carry=None):
    T = xh.shape[0]

    def body(xh_ref, g_ref, b_ref, wall_ref, bin_ref, cs_ref,
             qkv_ref, kt_ref, vt_ref, hg_ref, gate_ref, yb_ref, w, sem):
        @pl.when(pl.program_id(0) == 0)
        def _():
            _load_rows(wall_ref, [(w, 0, RIN)], sem)

        yb = (xh_ref[...] * g_ref[...] + b_ref[...]).astype(BF16)
        yb_ref[...] = yb
        c, s1, s2 = _rope_expand(cs_ref[...])

        def piece(start, width):
            return _dot_nt(yb, w[start:start + width, :]) + bin_ref[:, start:start + width]

        k = _rope(piece(*_K), c, s1, s2)
        v = piece(*_V)
        qkv_ref[:, 0:1024] = _rope(piece(*_Q), c, s1, s2).astype(BF16)
        qkv_ref[:, 1024:1280] = k.astype(BF16)
        qkv_ref[:, 1280:1536] = v.astype(BF16)
        kt_ref[...] = k.T.astype(BF16)
        vt_ref[...] = v.T.astype(BF16)
        for j in range(4):
            hg_ref[:, 1024 * j:1024 * (j + 1)] = piece(_HG[0] + 1024 * j, 1024)
        for j in range(2):
            gate_ref[:, 1024 * j:1024 * (j + 1)] = piece(_GATES[0] + 1024 * j, 1024).astype(BF16)

    row = lambda wd: pl.BlockSpec((tm, wd), lambda i: (i, 0))
    vec = lambda wd: pl.BlockSpec((1, wd), lambda i: (0, 0))
    colt = pl.BlockSpec((256, tm), lambda i: (0, i))
    return _call(
        body, name=name, grid=(T // tm,),
        in_specs=[row(D), vec(D), vec(D), HBM_SPEC, vec(DIN), row(128)],
        out_specs=[row(QKV_W), colt, colt, row(HG_W), row(GATE_W), row(D)],
        out_shape=[jax.ShapeDtypeStruct((T, QKV_W), BF16), jax.ShapeDtypeStruct((256, T), BF16),
                   jax.ShapeDtypeStruct((256, T), BF16), jax.ShapeDtypeStruct((T, HG_W), F32),
                   jax.ShapeDtypeStruct((T, GATE_W), BF16), jax.ShapeDtypeStruct((T, D), BF16)],
        scratch_shapes=[pltpu.VMEM((DIN, D), BF16), pltpu.SemaphoreType.DMA((NDEV,))],
        ins=[xh, g, b, wall, b_in, rope], carry=carry)


def _inproj_bwd(name, dq, dk, dv, dhg, dgates, dz2, wall, rope, tm, carry=None):
    T = dq.shape[0]

    def body(dq_ref, dk_ref, dv_ref, d0, d1, d2, d3, dga_ref, dgh_ref, dz_ref, wall_ref, cs_ref,
             dy_ref, dproj_ref, dbin_ref, w, sem):
        @pl.when(pl.program_id(0) == 0)
        def _():
            _load_rows(wall_ref, [(w, 0, RIN)], sem)
            dbin_ref[...] = jnp.zeros_like(dbin_ref)

        c, s1, s2 = _rope_expand(cs_ref[...])
        acc = ALPHA * dz_ref[...]
        pieces = [(_Q[0], _rope_t(dq_ref[...], c, s1, s2)), (_K[0], _rope_t(dk_ref[...], c, s1, s2)),
                  (_V[0], dv_ref[...])]
        pieces += [(_HG[0] + 1024 * j, r[...]) for j, r in enumerate((d0, d1, d2, d3))]
        pieces += [(_GATES[0], dga_ref[...]), (_GATES[0] + 1024, dgh_ref[...])]
        for start, val in pieces:
            width = val.shape[1]
            dbin_ref[:, start:start + width] += jnp.sum(val.astype(F32), axis=0, keepdims=True)
            vb = val.astype(BF16)
            dproj_ref[:, start:start + width] = vb
            acc = acc + _dot(vb, w[start:start + width, :])
        dy_ref[...] = acc

    row = lambda wd: pl.BlockSpec((tm, wd), lambda i: (i, 0))
    return _call(
        body, name=name, grid=(T // tm,),
        in_specs=[row(D), row(256), row(256)] + [row(D)] * 4 + [row(D), row(D), row(D), HBM_SPEC, row(128)],
        out_specs=[row(D), row(DIN), pl.BlockSpec((1, DIN), lambda i: (0, 0))],
        out_shape=[jax.ShapeDtypeStruct((T, D), F32), jax.ShapeDtypeStruct((T, DIN), BF16),
                   jax.ShapeDtypeStruct((1, DIN), F32)],
        scratch_shapes=[pltpu.VMEM((DIN, D), BF16), pltpu.SemaphoreType.DMA((NDEV,))],
        ins=[dq, dk, dv, *dhg, *dgates, dz2, wall, rope], carry=carry)


def _halves(t):
    lane = lax.broadcasted_iota(jnp.int32, t.shape, 1)
    low = lane < HEAD_DIM
    sw = pltpu.roll(t, HEAD_DIM, 1)
    zero = jnp.zeros_like(t)
    h0 = (jnp.where(low, t, zero), jnp.where(low, zero, sw))
    h1 = (jnp.where(low, sw, zero), jnp.where(low, zero, t))
    return h0, h1


def _lane_stack(p_ref, c_ref, gp):
    sl = slice(128 * gp, 128 * (gp + 1))
    hp, hc = _halves(p_ref[:, sl].astype(F32)), _halves(c_ref[:, sl].astype(F32))
    return [jnp.concatenate([hp[gg][0], hc[gg][0], hp[gg][1], hc[gg][1]], axis=0).astype(BF16) for gg in range(2)]


def _row_stack(tp_ref, tc_ref, g):
    band = jnp.concatenate([tp_ref[64 * g:64 * (g + 1), :], tc_ref[64 * g:64 * (g + 1), :]], axis=1)
    z = jnp.zeros_like(band)
    return [jnp.concatenate([band, z], axis=0), jnp.concatenate([z, band], axis=0)]


def _fold(prevm, t4, hh):
    return jnp.where(prevm, t4[256 * hh:256 * hh + 128, :], t4[256 * hh + 128:256 * hh + 256, :])


def _unfold(prevm, t):
    return jnp.concatenate([jnp.where(prevm, t, 0.0), jnp.where(prevm, 0.0, t)], axis=0).astype(BF16)


def _attn_softmax(s, kill, sink):
    s = jnp.where(kill, NEG_INF, s)
    m = jnp.maximum(jnp.max(s, axis=0, keepdims=True), sink)
    p = jnp.exp(s - m)
    es = jnp.exp(sink - m)
    inv = 1.0 / (jnp.sum(p, axis=0, keepdims=True) + es)
    return p * inv, es * inv


def _attn_masks(first):
    row = lax.broadcasted_iota(jnp.int32, (ATTN_BLOCK, 2 * ATTN_BLOCK), 0)
    lane = lax.broadcasted_iota(jnp.int32, (ATTN_BLOCK, 2 * ATTN_BLOCK), 1)
    prevm = row > lane % ATTN_BLOCK
    return prevm, jnp.logical_and(first, prevm)


def _pair_rows(ref, g):
    return jnp.concatenate([ref[:, 256 * g:256 * g + 128], ref[:, 256 * g + 128:256 * (g + 1)]], axis=0)


def _pair_sinks(sink_ref, g, hh):
    h0, h1 = 4 * g + hh, 4 * g + 2 + hh
    return jnp.concatenate([jnp.broadcast_to(sink_ref[:, h0:h0 + 1], (1, ATTN_BLOCK)),
                            jnp.broadcast_to(sink_ref[:, h1:h1 + 1], (1, ATTN_BLOCK))], axis=1)


def _attn_fwd(name, qkv, vt, sinks):
    T = qkv.shape[0]
    nbk = min(ATTN_TILE, T // ATTN_BLOCK)
    rows = nbk * ATTN_BLOCK
    scale = HEAD_DIM ** -0.5

    def block(q_ref, kp_ref, kc_ref, vtp_ref, vtc_ref, sink_ref, o_ref, first):
        prevm, kill = _attn_masks(first)
        kst = _lane_stack(kp_ref, kc_ref, 0) + _lane_stack(kp_ref, kc_ref, 1)
        vts = [_row_stack(vtp_ref, vtc_ref, g) for g in range(N_KV_HEADS)]
        s8 = [_dot_nt(kst[g], _pair_rows(q_ref, g)) * scale for g in range(N_KV_HEADS)]
        pu = [[_unfold(prevm, _attn_softmax(_fold(prevm, s8[g], hh), kill, _pair_sinks(sink_ref, g, hh))[0])
               for hh in range(2)] for g in range(N_KV_HEADS)]
        for g in range(N_KV_HEADS):
            ot = _dot(vts[g][0], pu[g][0]) + _dot(vts[g][1], pu[g][1])
            o_ref[:, 256 * g:256 * g + 128] = ot[:, :ATTN_BLOCK].T.astype(BF16)
            o_ref[:, 256 * g + 128:256 * (g + 1)] = ot[:, ATTN_BLOCK:].T.astype(BF16)

    def body(q_ref, kp_ref, kc_ref, vtp_ref, vtc_ref, sink_ref, o_ref):
        for b in range(nbk):
            rs = slice(ATTN_BLOCK * b, ATTN_BLOCK * (b + 1))
            ps = slice(ATTN_BLOCK * (b - 1), ATTN_BLOCK * b)
            block(q_ref.at[rs, :], kp_ref if b == 0 else kc_ref.at[ps, :], kc_ref.at[rs, :],
                  vtp_ref if b == 0 else vtc_ref.at[:, ps], vtc_ref.at[:, rs], sink_ref, o_ref.at[rs, :],
                  pl.program_id(0) == 0 if b == 0 else False)

    prev = lambda i: jnp.maximum(nbk * i - 1, 0)
    return pl.pallas_call(
        body, name=name, grid=(T // rows,),
        in_specs=[pl.BlockSpec((rows, D), lambda i: (i, 0)),
                  pl.BlockSpec((ATTN_BLOCK, 256), lambda i: (prev(i), 4)),
                  pl.BlockSpec((rows, 256), lambda i: (i, 4)),
                  pl.BlockSpec((256, ATTN_BLOCK), lambda i: (0, prev(i))),
                  pl.BlockSpec((256, rows), lambda i: (0, i)),
                  pl.BlockSpec((1, N_Q_HEADS), lambda i: (0, 0))],
        out_specs=pl.BlockSpec((rows, D), lambda i: (i, 0)),
        out_shape=jax.ShapeDtypeStruct((T, D), BF16),
        compiler_params=_params("arbitrary"),
    )(qkv, qkv, qkv, vt, vt, sinks)


ATTN_TILE = 4
ATTN_BWD_ORDER = ((0, 0), (0, 1), (1, 0), (0, 2), (1, 1), (2, 0), (0, 3), (1, 2), (2, 1), (1, 3), (2, 2), (2, 3))


def _attn_bwd(name, qkv, kt, sinks, dya, carry=None):
    T = qkv.shape[0]
    nbk = min(ATTN_TILE, T // ATTN_BLOCK)
    rows = nbk * ATTN_BLOCK
    scale = HEAD_DIM ** -0.5

    def block(q_ref, kp_ref, kc_ref, vp_ref, vc_ref, ktp_ref, ktc_ref, sink_ref, do_ref, dq_ref, first):
        prevm, kill = _attn_masks(first)
        kst = _lane_stack(kp_ref, kc_ref, 0) + _lane_stack(kp_ref, kc_ref, 1)
        vst = _lane_stack(vp_ref, vc_ref, 0) + _lane_stack(vp_ref, vc_ref, 1)
        kts = [_row_stack(ktp_ref, ktc_ref, g) for g in range(N_KV_HEADS)]
        lane = lax.broadcasted_iota(jnp.int32, (2 * ATTN_BLOCK, 128), 1)
        low = lane < HEAD_DIM
        slane = lax.broadcasted_iota(jnp.int32, (1, 128), 1)
        dkz = [jnp.zeros((2 * ATTN_BLOCK, 128), F32) for _ in range(N_KV_HEADS)]
        dvz = [jnp.zeros((2 * ATTN_BLOCK, 128), F32) for _ in range(N_KV_HEADS)]
        dsink = jnp.zeros((1, 128), F32)
        groups = range(N_KV_HEADS)
        qcat, docat, s8, dp8 = {}, {}, {}, {}
        ds_u, p_u, dsinks = {}, {}, []

        def scores(g):
            qcat[g], docat[g] = _pair_rows(q_ref, g), _pair_rows(do_ref, g)
            s8[g] = _dot_nt(kst[g], qcat[g]) * scale
            dp8[g] = _dot_nt(vst[g], docat[g])

        def algebra(g):
            for hh in range(2):
                pn, ps = _attn_softmax(_fold(prevm, s8[g], hh), kill, _pair_sinks(sink_ref, g, hh))
                dp = _fold(prevm, dp8[g], hh)
                delta = jnp.sum(pn * dp, axis=0, keepdims=True)
                sd = ps * delta
                dsinks.append(jnp.where(slane == 4 * g + hh, -jnp.sum(sd[:, :ATTN_BLOCK]), 0.0)
                              + jnp.where(slane == 4 * g + 2 + hh, -jnp.sum(sd[:, ATTN_BLOCK:]), 0.0))
                ds_u[g, hh] = _unfold(prevm, pn * (dp - delta) * scale)
                p_u[g, hh] = _unfold(prevm, pn)

        def grads(g):
            dqt = _dot(kts[g][0], ds_u[g, 0]) + _dot(kts[g][1], ds_u[g, 1])
            dq_ref[:, 256 * g:256 * g + 128] = dqt[:, :ATTN_BLOCK].T
            dq_ref[:, 256 * g + 128:256 * (g + 1)] = dqt[:, ATTN_BLOCK:].T
            for hh in range(2):
                own = low if hh == 0 else jnp.logical_not(low)
                dk_h = jnp.where(own, _dot(ds_u[g, hh], qcat[g]), 0.0)
                dv_h = jnp.where(own, _dot(p_u[g, hh], docat[g]), 0.0)
                if hh != g % 2:
                    dk_h = pltpu.roll(dk_h, HEAD_DIM, 1)
                    dv_h = pltpu.roll(dv_h, HEAD_DIM, 1)
                dkz[g] = dkz[g] + dk_h
                dvz[g] = dvz[g] + dv_h

        for stage, g in ATTN_BWD_ORDER:
            (scores, algebra, grads)[stage](g)
        for t in dsinks:
            dsink = dsink + t
        pairs = range(N_KV_HEADS // 2)
        return dsink, [dkz[2 * gp] + dkz[2 * gp + 1] for gp in pairs], [dvz[2 * gp] + dvz[2 * gp + 1] for gp in pairs]

    def body(q_ref, kp_ref, kc_ref, vp_ref, vc_ref, ktp_ref, ktc_ref, sink_ref, do_ref,
             dq_ref, dk_ref, dv_ref, ds_ref):
        i = pl.program_id(0)

        @pl.when(i == 0)
        def _():
            ds_ref[...] = jnp.zeros_like(ds_ref)

        res = []
        for b in range(nbk):
            rs = slice(ATTN_BLOCK * b, ATTN_BLOCK * (b + 1))
            ps = slice(ATTN_BLOCK * (b - 1), ATTN_BLOCK * b)
            res.append(block(
                q_ref.at[rs, :], kp_ref if b == 0 else kc_ref.at[ps, :], kc_ref.at[rs, :],
                vp_ref if b == 0 else vc_ref.at[ps, :], vc_ref.at[rs, :],
                ktp_ref if b == 0 else ktc_ref.at[:, ps], ktc_ref.at[:, rs], sink_ref, do_ref.at[rs, :],
                dq_ref.at[rs, :], i == 0 if b == 0 else False))
        dsink = res[0][0]
        for r in res[1:]:
            dsink = dsink + r[0]
        ds_ref[...] += dsink
        for gp in range(N_KV_HEADS // 2):
            cols = slice(128 * gp, 128 * (gp + 1))
            for b in range(nbk):
                cur = pl.ds(pl.multiple_of(i * rows + ATTN_BLOCK * b, ATTN_BLOCK), ATTN_BLOCK)
                dkb, dvb = res[b][1][gp][ATTN_BLOCK:, :], res[b][2][gp][ATTN_BLOCK:, :]
                if b + 1 < nbk:
                    dkb, dvb = dkb + res[b + 1][1][gp][:ATTN_BLOCK, :], dvb + res[b + 1][2][gp][:ATTN_BLOCK, :]
                dk_ref[cur, cols] = dkb
                dv_ref[cur, cols] = dvb

            @pl.when(i > 0)
            def _():
                prv = pl.ds(pl.multiple_of(jnp.maximum(i * rows - ATTN_BLOCK, 0), ATTN_BLOCK), ATTN_BLOCK)
                dk_ref[prv, cols] += res[0][1][gp][:ATTN_BLOCK, :]
                dv_ref[prv, cols] += res[0][2][gp][:ATTN_BLOCK, :]

    prev = lambda i: jnp.maximum(nbk * i - 1, 0)
    whole = lambda w: pl.BlockSpec((T, w), lambda i: (0, 0))
    return _call(
        body, name=name, grid=(T // rows,), scratch_shapes=[], ins=[qkv, qkv, qkv, qkv, qkv, kt, kt, sinks, dya],
        carry=carry,
        in_specs=[pl.BlockSpec((rows, D), lambda i: (i, 0)),
                  pl.BlockSpec((ATTN_BLOCK, 256), lambda i: (prev(i), 4)),
                  pl.BlockSpec((rows, 256), lambda i: (i, 4)),
                  pl.BlockSpec((ATTN_BLOCK, 256), lambda i: (prev(i), 5)),
                  pl.BlockSpec((rows, 256), lambda i: (i, 5)),
                  pl.BlockSpec((256, ATTN_BLOCK), lambda i: (0, prev(i))),
                  pl.BlockSpec((256, rows), lambda i: (0, i)),
                  pl.BlockSpec((1, N_Q_HEADS), lambda i: (0, 0)),
                  pl.BlockSpec((rows, D), lambda i: (i, 0))],
        out_specs=[pl.BlockSpec((rows, D), lambda i: (i, 0)), whole(256), whole(256),
                   pl.BlockSpec((1, 128), lambda i: (0, 0))],
        out_shape=[jax.ShapeDtypeStruct((T, D), F32), jax.ShapeDtypeStruct((T, 256), F32),
                   jax.ShapeDtypeStruct((T, 256), F32), jax.ShapeDtypeStruct((1, 128), F32)])


HG_FWD = (256, 8, 512)
HG_BWD = (128, 8, 512)


def _chunk_sum(tri, x):
    w = x.shape[1]
    h1 = x.astype(BF16)
    h2 = (x - h1.astype(F32)).astype(BF16)
    r = _dot(tri, jnp.concatenate([h1, h2], axis=1))
    return r[:, :w] + r[:, w:]


def _chunk_tri(n, upper=False):
    ri = lax.broadcasted_iota(jnp.int32, (n, n), 0)
    ci = lax.broadcasted_iota(jnp.int32, (n, n), 1)
    same_chunk = jnp.bitwise_xor(ri, ci) < HGRN_CHUNK
    return jnp.where(((ri <= ci) if upper else (ri >= ci)) & same_chunk, 1.0, 0.0).astype(BF16)


def _chunks(t):
    return [t[HGRN_CHUNK * c:HGRN_CHUNK * (c + 1), :] for c in range(t.shape[0] // HGRN_CHUNK)]


def _lower_bound(lbl_ref):
    l0, l1 = lbl_ref[0:1, :], lbl_ref[1:2, :]
    m = jnp.maximum(l0, l1)
    e0, e1 = jnp.exp(l0 - m), jnp.exp(l1 - m)
    return e0 / (e0 + e1)


def _heads(t):
    return [t[:, 128 * h:128 * (h + 1)] for h in range(t.shape[1] // 128)]


def _per_head(fn, *wide):
    return jnp.concatenate([fn(*parts) for parts in zip(*[_heads(t) for t in wide])], axis=1)


def _hgrn_sub(fl, qh, vv, lb, tril_b):
    w = fl.shape[1]
    sg = _sigmoid(fl)
    f = lb + (1.0 - lb) * sg
    k = 1.0 - f
    gc = _chunk_sum(tril_b, jnp.log(f))
    gl_c = [t[HGRN_CHUNK - 1:HGRN_CHUNK, :] for t in _chunks(gc)]
    gl = jnp.concatenate([jnp.broadcast_to(g, (HGRN_CHUNK, w)) for g in gl_c], axis=0)
    sq = _sigmoid(qh)
    eg = jnp.exp(gc)
    eng = jnp.exp(-gc)
    elg = jnp.exp(gl - gc)
    qd = qh * sq * eg
    ki = k * eng
    ke = k * elg
    qd_b, ki_b, ke_b, v_b = (t.astype(BF16) for t in (qd, ki, ke, vv))
    am_b = [jnp.where(tril_b > 0, _dot_nt(qq, kk), 0.0).astype(BF16) for qq, kk in zip(_heads(qd_b), _heads(ki_b))]
    return dict(sg=sg, f=f, sq=sq, eg=eg, eng=eng, elg=elg, qd=qd, ki=ki, ke=ke, egl=[jnp.exp(g) for g in gl_c],
                am_b=am_b, qd_b=qd_b, ki_b=ki_b, ke_b=ke_b, v_b=v_b)


def _hgrn_out(q, st_b):
    outs = []
    for h, (qd_h, v_h) in enumerate(zip(_heads(q["qd_b"]), _heads(q["v_b"]))):
        inter = [_dot_nt(qc, s) for qc, s in zip(_chunks(qd_h), st_b[h])]
        outs.append(_dot(q["am_b"][h], v_h) + jnp.concatenate(inter, axis=0))
    return jnp.concatenate(outs, axis=1)


def _head_rms(o):
    return _per_head(lambda t: jnp.broadcast_to(
        lax.rsqrt(jnp.mean(t * t, axis=-1, keepdims=True) + RMS_EPS), t.shape), o)


def _hgrn_fwd(name, hg, lb_logits, norm_g, cfg):
    T = hg.shape[0]
    sub_rows, hps, th = min(cfg[0], T), cfg[1], min(cfg[2], T)
    cps = sub_rows // HGRN_CHUNK
    nc = th // HGRN_CHUNK

    def body(fl_ref, qh_ref, v_ref, og_ref, lbl_ref, ng_ref, y_ref, ss_ref, st):
        @pl.when(pl.program_id(1) == 0)
        def _():
            st[...] = jnp.zeros_like(st)

        lbs = _lower_bound(lbl_ref)
        ng = jnp.tile(ng_ref[...], (1, hps))
        tril_b = _chunk_tri(sub_rows)

        def sub(si, carry):
            rows = pl.ds(pl.multiple_of(si * sub_rows, sub_rows), sub_rows)
            q = _hgrn_sub(fl_ref[rows, :], qh_ref[rows, :], v_ref[rows, :], lbs, tril_b)
            v_hc = [_chunks(t) for t in _heads(q["v_b"])]
            k_hc = [_chunks(t) for t in _heads(q["ke_b"])]
            s = [st[h] for h in range(hps)]
            st_b = [[] for _ in range(hps)]
            for c in range(cps):
                egl = _heads(q["egl"][c])
                for h in range(hps):
                    ss_ref[h, si * cps + c] = s[h]
                    st_b[h].append(s[h].astype(BF16))
                    s[h] = s[h] * egl[h] + _dot_tn(v_hc[h][c], k_hc[h][c])
            for h in range(hps):
                st[h] = s[h]
            o = _hgrn_out(q, st_b)
            og = og_ref[rows, :]
            y_ref[rows, :] = (o * _head_rms(o) * ng * (og * _sigmoid(og))).astype(BF16)
            return carry

        lax.fori_loop(0, th // sub_rows, sub, 0)

    wd = 128 * hps
    col = lambda j: pl.BlockSpec((th, wd), lambda h, t: (t, (8 // hps) * j + h))
    return pl.pallas_call(
        body, name=name, grid=(HGRN_HEADS // hps, T // th),
        in_specs=[col(0), col(1), col(2), col(3), pl.BlockSpec((2, wd), lambda h, t: (0, h)),
                  pl.BlockSpec((1, 128), lambda h, t: (0, 0))],
        out_specs=[pl.BlockSpec((th, wd), lambda h, t: (t, h)),
                   pl.BlockSpec((hps, nc, 128, 128), lambda h, t: (h, t, 0, 0))],
        out_shape=[jax.ShapeDtypeStruct((T, D), BF16),
                   jax.ShapeDtypeStruct((HGRN_HEADS, T // HGRN_CHUNK, 128, 128), F32)],
        scratch_shapes=[pltpu.VMEM((hps, 128, 128), F32)],
        compiler_params=_params("parallel", "arbitrary"),
    )(hg, hg, hg, hg, lb_logits, norm_g)


def _hgrn_bwd(name, hg, lb_logits, norm_g, sstart, dyh, cfg, carry=None):
    T = hg.shape[0]
    sub_rows, hps, th = min(cfg[0], T), cfg[1], min(cfg[2], T)
    cps = sub_rows // HGRN_CHUNK
    nc = th // HGRN_CHUNK
    nt = T // th
    wd = 128 * hps

    def body(fl_ref, qh_ref, v_ref, og_ref, lbl_ref, ng_ref, ss_ref, dy_ref,
             dfl_ref, dqh_ref, dv_ref, dog_ref, dlb_ref, dng_ref, dst):
        @pl.when(pl.program_id(1) == 0)
        def _():
            dst[...] = jnp.zeros_like(dst)
            dlb_ref[...] = jnp.zeros_like(dlb_ref)
            dng_ref[...] = jnp.zeros_like(dng_ref)

        lb = _lower_bound(lbl_ref)
        ng = jnp.tile(ng_ref[...], (1, hps))
        last = lax.broadcasted_iota(jnp.int32, (HGRN_CHUNK, wd), 0) == HGRN_CHUNK - 1
        nsub = th // sub_rows
        cat0 = lambda parts: jnp.concatenate(parts, axis=0)
        tril_b, triu_b = _chunk_tri(sub_rows), _chunk_tri(sub_rows, upper=True)

        def sub(step, carry):
            si = nsub - 1 - step
            rows = pl.ds(pl.multiple_of(si * sub_rows, sub_rows), sub_rows)
            qh, og, dy = qh_ref[rows, :], og_ref[rows, :], dy_ref[rows, :]
            q = _hgrn_sub(fl_ref[rows, :], qh, v_ref[rows, :], lb, tril_b)
            s_in = [[ss_ref[h, si * cps + c] for c in range(cps)] for h in range(hps)]
            st_b = [[s.astype(BF16) for s in row] for row in s_in]
            o = _hgrn_out(q, st_b)
            rr = _head_rms(o)
            oh = o * rr
            sog = _sigmoid(og)
            dog_ref[rows, :] = (dy * oh * ng * (sog * (1.0 + og * (1.0 - sog)))).astype(BF16)
            don = dy * (og * sog)
            dng_w = jnp.sum(don * oh, axis=0, keepdims=True)
            dd = don * ng
            mean_h = _per_head(lambda t: jnp.broadcast_to(jnp.mean(t, axis=-1, keepdims=True), t.shape), dd * oh)
            do_b = (rr * (dd - oh * mean_h)).astype(BF16)
            do_h, qd_h, ki_h, ke_h, v_h = (_heads(t) for t in (do_b, q["qd_b"], q["ki_b"], q["ke_b"], q["v_b"]))
            da_b = [jnp.where(tril_b > 0, _dot_nt(do_h[h], v_h[h]), 0.0).astype(BF16) for h in range(hps)]
            do_c, qd_c, ke_c, v_c = ([_chunks(t) for t in hs] for hs in (do_h, qd_h, ke_h, v_h))
            dsp = [[None] * cps for _ in range(hps)]
            dgl_dec = [[None] * cps for _ in range(hps)]
            d = [dst[h] for h in range(hps)]
            for c in reversed(range(cps)):
                egl = _heads(q["egl"][c])
                for h in range(hps):
                    dsp[h][c] = d[h]
                    dgl_dec[h][c] = jnp.sum(d[h] * s_in[h][c], axis=0, keepdims=True) * egl[h]
                    d[h] = d[h] * egl[h] + _dot_tn(do_c[h][c], qd_c[h][c])
            for h in range(hps):
                dst[h] = d[h]
                dng_ref[h] += dng_w[:, 128 * h:128 * (h + 1)]
            dsp_b = [[t.astype(BF16) for t in row] for row in dsp]
            dv_ref[rows, :] = jnp.concatenate(
                [_dot_tn(q["am_b"][h], do_h[h]) + cat0([_dot_nt(ke_c[h][c], dsp_b[h][c]) for c in range(cps)])
                 for h in range(hps)], axis=1).astype(BF16)
            dqd = jnp.concatenate(
                [_dot(da_b[h], ki_h[h]) + cat0([_dot(do_c[h][c], st_b[h][c]) for c in range(cps)])
                 for h in range(hps)], axis=1)
            dki = jnp.concatenate([_dot_tn(da_b[h], qd_h[h]) for h in range(hps)], axis=1)
            dke = jnp.concatenate([cat0([_dot(v_c[h][c], dsp_b[h][c]) for c in range(cps)]) for h in range(hps)], axis=1)
            dkk = dke * q["ke"]
            dgl = [jnp.sum(t, axis=0, keepdims=True) + jnp.concatenate([dgl_dec[h][c] for h in range(hps)], axis=1)
                   for c, t in enumerate(_chunks(dkk))]
            dgc = dqd * q["qd"] - dki * q["ki"] - dkk + cat0([jnp.where(last, g, 0.0) for g in dgl])
            dlf = _chunk_sum(triu_b, dgc)
            df = dlf / q["f"] - (dki * q["eng"] + dke * q["elg"])
            sg = q["sg"]
            dlb_ref[...] += jnp.sum(df * (1.0 - sg), axis=0, keepdims=True)
            dfl_ref[rows, :] = (df * (1.0 - lb) * sg * (1.0 - sg)).astype(BF16)
            sq = q["sq"]
            dqh_ref[rows, :] = (dqd * q["eg"] * (sq * (1.0 + qh * (1.0 - sq)))).astype(BF16)
            return carry

        lax.fori_loop(0, nsub, sub, 0)
    col = lambda j: pl.BlockSpec((th, wd), lambda h, t: (nt - 1 - t, (8 // hps) * j + h))
    out = pl.BlockSpec((th, wd), lambda h, t: (nt - 1 - t, h))
    return _call(
        body, name=name, grid=(HGRN_HEADS // hps, nt),
        in_specs=[col(0), col(1), col(2), col(3), pl.BlockSpec((2, wd), lambda h, t: (0, h)),
                  pl.BlockSpec((1, 128), lambda h, t: (0, 0)),
                  pl.BlockSpec((hps, nc, 128, 128), lambda h, t: (h, nt - 1 - t, 0, 0)), out],
        out_specs=[out, out, out, out, pl.BlockSpec((1, wd), lambda h, t: (0, h)),
                   pl.BlockSpec((hps, 1, 128), lambda h, t: (h, 0, 0))],
        out_shape=[jax.ShapeDtypeStruct((T, D), BF16)] * 4 + [jax.ShapeDtypeStruct((1, D), F32),
                                                              jax.ShapeDtypeStruct((HGRN_HEADS, 1, 128), F32)],
        scratch_shapes=[pltpu.VMEM((hps, 128, 128), F32)],
        ins=[hg, hg, hg, hg, lb_logits, norm_g, sstart, dyh], carry=carry)


def _mix_fwd(name, ya, yh, gates, xh1, g1, b1, wall, tm):
    T = ya.shape[0]

    def body(ya_ref, yh_ref, ga_ref, gh_ref, xh_ref, g_ref, b_ref, wall_ref,
             xo_ref, r_ref, pa_ref, ph_ref, mg_ref, wpa, wph, wo, sem):
        @pl.when(pl.program_id(0) == 0)
        def _():
            _load_rows(wall_ref, [(wpa, RIN, RP), (wph, RIN + RP, RP), (wo, RIN + 2 * RP, RP)], sem)

        parts = [slice(p, min(p + PART_ROWS, tm)) for p in range(0, tm, PART_ROWS)]
        pas = [_dot(ya_ref[rs, :], wpa[...]).astype(BF16) for rs in parts]
        phs = [_dot(yh_ref[rs, :], wph[...]).astype(BF16) for rs in parts]
        mgs = []
        for rs, pa, ph in zip(parts, pas, phs):
            pa_ref[rs, :] = pa
            ph_ref[rs, :] = ph
            sa, sh = _sigmoid(ga_ref[rs, :].astype(F32)), _sigmoid(gh_ref[rs, :].astype(F32))
            mgs.append((sa * pa.astype(F32) + sh * ph.astype(F32)).astype(BF16))
            mg_ref[rs, :] = mgs[-1]
        mixes = [_dot(mg, wo[...]) for mg in mgs]
        for rs, mix in zip(parts, mixes):
            y1 = xh_ref[rs, :] * g_ref[...] + b_ref[...]
            xh2, r = _ln_fwd(ALPHA * y1 + mix)
            xo_ref[rs, :] = xh2
            r_ref[rs, :] = jnp.broadcast_to(r, (xh2.shape[0], 128))

    row = lambda w: pl.BlockSpec((tm, w), lambda i: (i, 0))
    vec = pl.BlockSpec((1, D), lambda i: (0, 0))
    return pl.pallas_call(
        body, name=name, grid=(T // tm,),
        in_specs=[row(D), row(D), pl.BlockSpec((tm, D), lambda i: (i, 0)), pl.BlockSpec((tm, D), lambda i: (i, 1)),
                  row(D), vec, vec, HBM_SPEC],
        out_specs=[row(D), row(128), row(D), row(D), row(D)],
        out_shape=[jax.ShapeDtypeStruct((T, D), F32), jax.ShapeDtypeStruct((T, 128), F32)]
        + [jax.ShapeDtypeStruct((T, D), BF16)] * 3,
        scratch_shapes=[pltpu.VMEM((D, D), BF16)] * 3 + [pltpu.SemaphoreType.DMA((3 * NDEV,))],
        compiler_params=_params("arbitrary"),
    )(ya, yh, gates, gates, xh1, g1, b1, wall)


def _mix_bwd(name, dy2, xh2, r2, g2, gates, pa, ph, wall, tm, carry=None):
    T = dy2.shape[0]

    def body(dy_ref, xh_ref, r_ref, g_ref, ga_ref, gh_ref, pa_ref, ph_ref, wall_ref,
             dz_ref, dmix_ref, dpa_ref, dph_ref, dga_ref, dgh_ref, dya_ref, dyh_ref, dg_ref, db_ref,
             wpa, wph, wo, sem):
        @pl.when(pl.program_id(0) == 0)
        def _():
            _load_rows(wall_ref, [(wpa, RIN, RP), (wph, RIN + RP, RP), (wo, RIN + 2 * RP, RP)], sem)
            dg_ref[...] = jnp.zeros_like(dg_ref)
            db_ref[...] = jnp.zeros_like(db_ref)

        parts = [slice(p, min(p + PART_ROWS, tm)) for p in range(0, tm, PART_ROWS)]
        dmix = []
        for rs in parts:
            dz, dgp, dbp = _ln_bwd(dy_ref[rs, :], xh_ref[rs, :], r_ref[rs, :1], g_ref[...])
            dg_ref[...] += dgp
            db_ref[...] += dbp
            dz_ref[rs, :] = dz
            dmix.append(dz.astype(BF16))
            dmix_ref[rs, :] = dmix[-1]
        dmgs = [_dot_nt(d, wo[...]) for d in dmix]
        dpas, dphs = [], []
        for rs, dmg in zip(parts, dmgs):
            sa, sh = _sigmoid(ga_ref[rs, :].astype(F32)), _sigmoid(gh_ref[rs, :].astype(F32))
            dga_ref[rs, :] = (dmg * pa_ref[rs, :].astype(F32) * sa * (1.0 - sa)).astype(BF16)
            dgh_ref[rs, :] = (dmg * ph_ref[rs, :].astype(F32) * sh * (1.0 - sh)).astype(BF16)
            dpas.append((dmg * sa).astype(BF16))
            dphs.append((dmg * sh).astype(BF16))
            dpa_ref[rs, :] = dpas[-1]
            dph_ref[rs, :] = dphs[-1]
        for rs, dpa, dph in zip(parts, dpas, dphs):
            dya_ref[rs, :] = _dot_nt(dpa, wpa[...]).astype(BF16)
            dyh_ref[rs, :] = _dot_nt(dph, wph[...])

    row = lambda w: pl.BlockSpec((tm, w), lambda i: (i, 0))
    vec = pl.BlockSpec((1, D), lambda i: (0, 0))
    return _call(
        body, name=name, grid=(T // tm,),
        in_specs=[row(D), row(D), row(128), vec, pl.BlockSpec((tm, D), lambda i: (i, 0)),
                  pl.BlockSpec((tm, D), lambda i: (i, 1)), row(D), row(D), HBM_SPEC],
        out_specs=[row(D)] * 8 + [vec, vec],
        out_shape=[jax.ShapeDtypeStruct((T, D), F32)] + [jax.ShapeDtypeStruct((T, D), BF16)] * 6
        + [jax.ShapeDtypeStruct((T, D), F32)] + [jax.ShapeDtypeStruct((1, D), F32)] * 2,
        scratch_shapes=[pltpu.VMEM((D, D), BF16)] * 3 + [pltpu.SemaphoreType.DMA((3 * NDEV,))],
        ins=[dy2, xh2, r2, g2, gates, gates, pa, ph, wall], carry=carry)


def _adam(w, g, m, v):
    m = ADAM_B1 * m + (1.0 - ADAM_B1) * g
    v = ADAM_B2 * v + (1.0 - ADAM_B2) * (g * g)
    m_hat = m / (1.0 - ADAM_B1 ** ADAM_STEP)
    v_hat = v / (1.0 - ADAM_B2 ** ADAM_STEP)
    delta = -ADAM_LR * (m_hat / (jnp.sqrt(v_hat) + ADAM_EPS) + ADAM_WD * w)
    return delta, m, v


def _grad_steps(name, items, cb, carry=None):
    n = len(items)
    pieces = [len(it[0]) for it in items]

    def body(*refs):
        it = iter(refs[:sum(pieces) + 3 * n])
        outs = refs[sum(pieces) + 3 * n:]
        for i in range(n):
            accs = []
            for _ in range(pieces[i]):
                r_ref = next(it)
                acc = r_ref[0].astype(F32)
                for k in range(1, NDEV):
                    acc = acc + r_ref[k].astype(F32)
                accs.append(acc)
            acc = accs[0] if len(accs) == 1 else jnp.concatenate(accs, axis=0)
            w_ref, m_ref, v_ref = next(it), next(it), next(it)
            g_ref, d_ref, mo_ref, vo_ref = outs[4 * i:4 * i + 4]
            g_ref[...] = acc
            d, mm, vv = _adam(w_ref[...], acc, m_ref[...], v_ref[...])
            d_ref[...] = d
            mo_ref[...] = mm
            vo_ref[...] = vv

    in_specs, out_specs, out_shape, ins = [], [], [], []
    for recv, w, m, v in items:
        blk = pl.BlockSpec((w.shape[0], cb), lambda j: (0, j))
        in_specs += [pl.BlockSpec((NDEV, rp.shape[1], cb), lambda j: (0, 0, j)) for rp in recv] + [blk, blk, blk]
        out_specs += [blk] * 4
        out_shape += [jax.ShapeDtypeStruct(w.shape, F32)] * 4
        ins += list(recv) + [w, m, v]
    outs, ex = _call(body, name=name, grid=(D // cb,), in_specs=in_specs, out_specs=out_specs, out_shape=out_shape,
                     scratch_shapes=[], ins=ins, carry=carry)
    return [tuple(outs[4 * i:4 * i + 4]) for i in range(n)], ex


_SMALL = [("ln1_g", D), ("ln1_b", D), ("ln2_g", D), ("ln2_b", D), ("ln3_g", D), ("ln3_b", D),
          ("b_in", DIN), ("lb", D), ("attn_sinks", 128), ("hgrn_norm_g", 128), ("loss", 128)]
_SMALL_OFF = {}
_o = 0
for _n, _w in _SMALL:
    _SMALL_OFF[_n] = (_o, _w)
    _o += _w
PACK = _o


def _small_step(name, slots, small_w, small_m, small_v):
    names = ["ln1_g", "ln1_b", "ln2_g", "ln2_b", "ln3_g", "ln3_b", "b_in", "attn_sinks", "hgrn_lb_logits", "hgrn_norm_g"]
    np_ = len(names)

    def body(*refs):
        s_ref = refs[0]
        w_refs = refs[1:1 + np_]
        m_refs = refs[1 + np_:1 + 2 * np_]
        v_refs = refs[1 + 2 * np_:1 + 3 * np_]
        loss_ref, outs = refs[1 + 3 * np_], refs[2 + 3 * np_:]
        tot = s_ref[0]
        for k in range(1, NDEV):
            tot = tot + s_ref[k]

        def part(n):
            o, w = _SMALL_OFF[n]
            return tot[:, o:o + w]

        loss_ref[...] = part("loss")
        for i, n in enumerate(names):
            w = w_refs[i][...]
            if n == "hgrn_lb_logits":
                m_ = jnp.maximum(w[0:1, :], w[1:2, :])
                e0, e1 = jnp.exp(w[0:1, :] - m_), jnp.exp(w[1:2, :] - m_)
                p0 = e0 / (e0 + e1)
                t = p0 * (1.0 - p0) * part("lb")
                g = jnp.concatenate([t, -t], axis=0)
            elif n == "attn_sinks":
                g = part(n)[:, :N_Q_HEADS]
            else:
                g = part(n)
            d, mm, vv = _adam(w, g, m_refs[i][...], v_refs[i][...])
            outs[4 * i][...] = g
            outs[4 * i + 1][...] = d
            outs[4 * i + 2][...] = mm
            outs[4 * i + 3][...] = vv

    out_shape = [jax.ShapeDtypeStruct((1, 128), F32)]
    for n in names:
        out_shape += [jax.ShapeDtypeStruct(small_w[n].shape, F32)] * 4
    return pl.pallas_call(
        body, name=name, out_shape=out_shape,
        compiler_params=pltpu.CompilerParams(vmem_limit_bytes=VMEM_LIMIT),
    )(slots, *[small_w[n] for n in names], *[small_m[n] for n in names], *[small_v[n] for n in names]), names


def _tile(T, pref):
    return min(T, pref)


def kernel(x, ln1_g, ln1_b, ffn1_w1, ffn1_w3, ffn1_w2, ln2_g, ln2_b, w_in, b_in, attn_sinks, hgrn_lb_logits, hgrn_norm_g, w_proj_attn, w_proj_hgrn, w_out, ln3_g, ln3_b, ffn2_w1, ffn2_w3, ffn2_w2, loss_target, m_ln1_g, m_ln1_b, m_ffn1_w1, m_ffn1_w3, m_ffn1_w2, m_ln2_g, m_ln2_b, m_w_in, m_b_in, m_attn_sinks, m_hgrn_lb_logits, m_hgrn_norm_g, m_w_proj_attn, m_w_proj_hgrn, m_w_out, m_ln3_g, m_ln3_b, m_ffn2_w1, m_ffn2_w3, m_ffn2_w2, v_ln1_g, v_ln1_b, v_ffn1_w1, v_ffn1_w3, v_ffn1_w2, v_ln2_g, v_ln2_b, v_w_in, v_b_in, v_attn_sinks, v_hgrn_lb_logits, v_hgrn_norm_g, v_w_proj_attn, v_w_proj_hgrn, v_w_out, v_ln3_g, v_ln3_b, v_ffn2_w1, v_ffn2_w3, v_ffn2_w2):
    T = x.shape[1]
    xs = x[0]
    tgt = loss_target[0]
    tm = _tile(T, 256)
    tm2 = _tile(T, 512)
    tk = _tile(T, 2048)

    t_bf = lambda w: w[0].T.astype(BF16)
    n_bf = lambda w: w[0].astype(BF16)
    ffn_shard = lambda w1, w3, w2: jnp.concatenate([t_bf(w1), t_bf(w3), n_bf(w2)], axis=0)
    mix_shard = jnp.concatenate([t_bf(w_in), n_bf(w_proj_attn), n_bf(w_proj_hgrn), n_bf(w_out)], axis=0)
    (ffn1_all,) = _exchange_call("gather_ffn1", _gather_exchange(ffn_shard(ffn1_w1, ffn1_w3, ffn1_w2)))
    ffn_offs = (0, RF, 2 * RF)
    rope = _rope_table(T)

    (xh1, r1, a1, b1, xb0), (mix_all,) = _ffn_fwd("ffn1_fwd", xs, None, ffn1_all, ffn_offs, tm2,
                                                  carry=_gather_exchange(mix_shard))
    (qkv, kt, vt, hg, gates, y1b), (ffn2_all,) = _inproj_fwd(
        "inproj_fwd", xh1, ln1_g, ln1_b, mix_all, b_in, rope, tm2,
        carry=_gather_exchange(ffn_shard(ffn2_w1, ffn2_w3, ffn2_w2)))
    ya = _attn_fwd("attn_fwd", qkv, vt, attn_sinks)
    yh, sstart = _hgrn_fwd("hgrn_fwd", hg, hgrn_lb_logits, hgrn_norm_g, HG_FWD)
    xh2, r2, pa, ph, merged = _mix_fwd("mix_fwd", ya, yh, gates, xh1, ln1_g, ln1_b, mix_all, tm2)
    (xh3, r3, a2, b2, y2b, loss_part), _ = _ffn_fwd("ffn2_fwd", xh2, (ln2_g, ln2_b), ffn2_all, ffn_offs, tm2,
                                                    loss=(ln3_g, ln3_b, tgt))

    (dy2, dab2, u2, df2, dg3, db3), _ = _ffn_bwd("ffn2_bwd", ("loss", ln3_b, tgt), xh3, r3, ln3_g, a2, b2, ffn2_all,
                                                 ffn_offs, tm)
    recv = {}
    g_ffn2_13, _ = _wgrad("wgrad_ffn2_w13", dab2, y2b, DFF // 2, tk)
    g_ffn2_2, _ = _wgrad("wgrad_ffn2_w2", u2, df2, DFF // 2, tk)
    (dz2, dmix, dpa, dph, dga, dgh, dya, dyh, dg2, db2), _ = _mix_bwd(
        "mix_bwd", dy2, xh2, r2, ln2_g, gates, pa, ph, mix_all, tm2)
    g_wo, _ = _wgrad("wgrad_w_out", merged, dmix, D, tk)
    g_pa, _ = _wgrad("wgrad_w_pa", ya, dpa, D, tk)
    g_ph, _ = _wgrad("wgrad_w_ph", yh, dph, D, tk)
    (dq, dk, dv, dsink), _ = _attn_bwd("attn_bwd", qkv, kt, attn_sinks, dya)
    (dfl, dqh, dih, dog, dlb, dng), (recv["ffn2_w2"], recv["w_proj_attn"], recv["w_proj_hgrn"], recv["w_out"]) = \
        _hgrn_bwd("hgrn_bwd", hg, hgrn_lb_logits, hgrn_norm_g, sstart, dyh, HG_BWD,
                  carry=_grad_exchange([g_ffn2_2, g_pa, g_ph, g_wo], [0, 0, 0, 0], [RF, RP, RP, RP]))
    (dy1, dproj, dbin), (recv["ffn2_w1"], recv["ffn2_w3"]) = _inproj_bwd(
        "inproj_bwd", dq, dk, dv, (dfl, dqh, dih, dog), (dga, dgh), dz2, mix_all, rope, tm2,
        carry=_grad_exchange([g_ffn2_13, g_ffn2_13], [0, DFF], [RF, RF]))
    g_win, _ = _wgrad("wgrad_w_in", dproj, y1b, DIN // 4, tk)
    win_rows = (400, 288, 272)
    win_base = (0, 400, 688)
    (dz1, dab1, u1, df1, dg1, db1), (rw0,) = _ffn_bwd(
        "ffn1_bwd", ("dy", dy1), xh1, r1, ln1_g, a1, b1, ffn1_all, ffn_offs, tm2, with_dx=False,
        carry=_grad_exchange([g_win], win_base[:1], win_rows[:1], [RIN]))
    g_ffn1_2, (rw1,) = _wgrad("wgrad_ffn1_w2", u1, df1, DFF // 2, tk,
                              carry=_grad_exchange([g_win], win_base[1:2], win_rows[1:2], [RIN]))
    g_ffn1_1, (recv["ffn1_w2"],) = _wgrad("wgrad_ffn1_w1", dab1, xb0, DFF // 2, tk, 0, DFF,
                                          carry=_grad_exchange([g_ffn1_2], [0], [RF]))
    g_ffn1_3, (recv["ffn1_w1"],) = _wgrad("wgrad_ffn1_w3", dab1, xb0, DFF // 2, tk, DFF, DFF,
                                          carry=_grad_exchange([g_ffn1_1], [0], [RF]))
    parts = {"ln1_g": dg1, "ln1_b": db1, "ln2_g": dg2, "ln2_b": db2, "ln3_g": dg3, "ln3_b": db3, "b_in": dbin,
             "lb": dlb, "attn_sinks": dsink, "hgrn_norm_g": jnp.sum(dng, axis=0), "loss": loss_part[0:1, :]}
    packed = jnp.concatenate([parts[n] for n, _ in _SMALL], axis=1)
    gx, (recv["ffn1_w3"], rw2, small_slots) = _ffn_dx(
        "ffn1_dx", dab1, dz1, ffn1_all, ffn_offs, tm2,
        carry=_join(_grad_exchange([g_ffn1_3, g_win], [0, win_base[2]], [RF, win_rows[2]], [RF, RIN]),
                    _row_gather_exchange(packed)))
    recv["w_in"] = [rw0, rw1, rw2]

    big = [("ffn1_w1", ffn1_w1, m_ffn1_w1, v_ffn1_w1, True), ("ffn1_w3", ffn1_w3, m_ffn1_w3, v_ffn1_w3, True),
           ("ffn1_w2", ffn1_w2, m_ffn1_w2, v_ffn1_w2, False), ("w_in", w_in, m_w_in, v_w_in, True),
           ("w_proj_attn", w_proj_attn, m_w_proj_attn, v_w_proj_attn, False),
           ("w_proj_hgrn", w_proj_hgrn, m_w_proj_hgrn, v_w_proj_hgrn, False),
           ("w_out", w_out, m_w_out, v_w_out, False),
           ("ffn2_w1", ffn2_w1, m_ffn2_w1, v_ffn2_w1, True), ("ffn2_w3", ffn2_w3, m_ffn2_w3, v_ffn2_w3, True),
           ("ffn2_w2", ffn2_w2, m_ffn2_w2, v_ffn2_w2, False)]
    view = lambda t, transposed: t[0].T if transposed else t[0]
    back = lambda t, transposed: t.T[None] if transposed else t[None]
    slots = lambda n: recv[n] if isinstance(recv[n], list) else [recv[n]]
    stepped, _ = _grad_steps("steps", [(slots(n), view(w, tr), view(m, tr), view(v, tr)) for n, w, m, v, tr in big], 128)
    res = {n: tuple(back(t, tr) for t in outs) for (n, _, _, _, tr), outs in zip(big, stepped)}

    small_w = dict(ln1_g=ln1_g, ln1_b=ln1_b, ln2_g=ln2_g, ln2_b=ln2_b, ln3_g=ln3_g, ln3_b=ln3_b, b_in=b_in,
                   attn_sinks=attn_sinks, hgrn_lb_logits=hgrn_lb_logits, hgrn_norm_g=hgrn_norm_g)
    small_m = dict(ln1_g=m_ln1_g, ln1_b=m_ln1_b, ln2_g=m_ln2_g, ln2_b=m_ln2_b, ln3_g=m_ln3_g, ln3_b=m_ln3_b,
                   b_in=m_b_in, attn_sinks=m_attn_sinks, hgrn_lb_logits=m_hgrn_lb_logits, hgrn_norm_g=m_hgrn_norm_g)
    small_v = dict(ln1_g=v_ln1_g, ln1_b=v_ln1_b, ln2_g=v_ln2_g, ln2_b=v_ln2_b, ln3_g=v_ln3_g, ln3_b=v_ln3_b,
                   b_in=v_b_in, attn_sinks=v_attn_sinks, hgrn_lb_logits=v_hgrn_lb_logits, hgrn_norm_g=v_hgrn_norm_g)
    outs, names = _small_step("small_step", small_slots, small_w, small_m, small_v)
    loss = outs[0][0, 0]
    for i, n in enumerate(names):
        res[n] = tuple(outs[1 + 4 * i:5 + 4 * i])

    order = ["ln1_g", "ln1_b", "ffn1_w1", "ffn1_w3", "ffn1_w2", "ln2_g", "ln2_b", "w_in", "b_in", "attn_sinks",
             "hgrn_lb_logits", "hgrn_norm_g", "w_proj_attn", "w_proj_hgrn", "w_out", "ln3_g", "ln3_b",
             "ffn2_w1", "ffn2_w3", "ffn2_w2"]
    return (loss, gx[None], *[res[n][0] for n in order], *[res[n][1] for n in order],
            *[res[n][2] for n in order], *[res[n][3] for n in order])
```

```python
import jax
import jax.numpy as jnp
from jax import lax
from jax.experimental import pallas as pl
from jax.experimental.pallas import tpu as pltpu

F32 = jnp.float32
BF16 = jnp.bfloat16

NDEV = 8
D = 1024
DFF = 2816
RF = DFF // NDEV
DIN = 7680
RIN = DIN // NDEV
RP = D // NDEV
N_Q_HEADS = 16
N_KV_HEADS = 4
HEAD_DIM = 64
ATTN_BLOCK = 128
ROPE_THETA = 500000.0
ROPE_DIM = HEAD_DIM // 4
HGRN_HEADS = 8
HGRN_CHUNK = 64
ALPHA = 2.0 ** 0.25
LN_EPS = 1e-5
RMS_EPS = 1e-6
NEG_INF = -1e30
ADAM_LR = 0.001
ADAM_B1 = 0.9
ADAM_B2 = 0.999
ADAM_EPS = 1e-08
ADAM_WD = 0.01
ADAM_STEP = 10

QKV_W = 1536
HG_W = 4096
GATE_W = 2048
VMEM_LIMIT = 60 * 2 ** 20
PART_ROWS = 256
MESH = pl.DeviceIdType.MESH
HBM_SPEC = pl.BlockSpec(memory_space=pltpu.HBM)


def _params(*sem):
    return pltpu.CompilerParams(dimension_semantics=sem, vmem_limit_bytes=VMEM_LIMIT)


def _dot(a, b):
    return jnp.dot(a, b, preferred_element_type=F32)


def _dot_nt(a, b):
    return lax.dot_general(a, b, (((1,), (1,)), ((), ())), preferred_element_type=F32)


def _dot_tn(a, b):
    return lax.dot_general(a, b, (((0,), (0,)), ((), ())), preferred_element_type=F32)


def _sigmoid(x):
    return 0.5 * jnp.tanh(0.5 * x) + 0.5


def _ln_fwd(z):
    mu = jnp.mean(z, axis=-1, keepdims=True)
    zc = z - mu
    var = jnp.mean(zc * zc, axis=-1, keepdims=True)
    r = lax.rsqrt(var + LN_EPS)
    return zc * r, r


def _ln_bwd(dy, xh, r, g):
    dxh = dy * g
    m1 = jnp.mean(dxh, axis=-1, keepdims=True)
    m2 = jnp.mean(dxh * xh, axis=-1, keepdims=True)
    dz = r * (dxh - m1 - xh * m2)
    return dz, jnp.sum(dy * xh, axis=0, keepdims=True), jnp.sum(dy, axis=0, keepdims=True)


def _load_rows(wall_ref, pieces, sem):
    copies = []
    for dst, off, r in pieces:
        for k in range(NDEV):
            c = pltpu.make_async_copy(wall_ref.at[k, pl.ds(off, r), :], dst.at[pl.ds(k * r, r), :], sem.at[len(copies)])
            c.start()
            copies.append(c)
    for c in copies:
        c.wait()


class _Exchange:
    def __init__(self, inputs, out_shape, scratch, begin, middle, end):
        self.inputs, self.out_shape, self.scratch = inputs, out_shape, scratch
        self.begin, self.middle, self.end = begin, middle, end


def _gather_exchange(shard):
    rows, cols = shard.shape

    def ops(ins, outs, scr):
        (x_ref,), (out_ref,), (send_sems, recv_sems, local_sem) = ins, outs, scr
        x, y, c = lax.axis_index("x"), lax.axis_index("y"), lax.axis_index("c")
        me, sibling = (x, y, c), (x, y, 1 - c)
        chips = [(1 - x, y), (x, 1 - y), (1 - x, 1 - y)]

        def slot(px, py, pc):
            return out_ref.at[4 * px + 2 * py + pc]

        def copy(k, block, to, src=None):
            return pltpu.make_async_remote_copy(
                src_ref=slot(*block) if src is None else src, dst_ref=slot(*block),
                send_sem=send_sems.at[k], recv_sem=recv_sems.at[k], device_id=to, device_id_type=MESH)

        mine = lambda: pltpu.make_async_copy(x_ref, slot(*me), local_sem)
        first = lambda: [copy(0, me, sibling, src=x_ref)] + [
            copy(1 + j, me, (*chip, c), src=x_ref) for j, chip in enumerate(chips)]
        passed = lambda: [copy(4 + j, (*chip, c), sibling) for j, chip in enumerate(chips)]
        return c, me, sibling, chips, copy, mine, first, passed

    def begin(*refs):
        _, _, _, _, _, mine, first, _ = ops(*refs)
        mine().start()
        for cp in first():
            cp.start()

    def middle(*refs):
        c, me, _, chips, copy, _, _, passed = ops(*refs)
        for (j, chip), fwd in zip(enumerate(chips), passed()):
            copy(1 + j, (*chip, c), me).wait_recv()
            fwd.start()

    def end(*refs):
        c, me, sibling, chips, copy, mine, first, passed = ops(*refs)
        copy(0, sibling, me).wait_recv()
        for j, chip in enumerate(chips):
            copy(4 + j, (*chip, 1 - c), me).wait_recv()
        for cp in first() + passed():
            cp.wait_send()
        mine().wait()

    return _Exchange([shard], [jax.ShapeDtypeStruct((NDEV, rows, cols), shard.dtype)],
                     [pltpu.SemaphoreType.DMA((7,)), pltpu.SemaphoreType.DMA((7,)), pltpu.SemaphoreType.DMA],
                     begin, middle, end)


def _relay_gather_exchange(shard):
    rows, cols = shard.shape

    def ops(ins, outs, scr):
        (x_ref,), (out_ref,), (send_sems, recv_sems, local_sem) = ins, outs, scr
        x, y, c = lax.axis_index("x"), lax.axis_index("y"), lax.axis_index("c")
        me, sibling = (x, y, c), (x, y, 1 - c)
        xchip, ychip, diag = (1 - x, y), (x, 1 - y), (1 - x, 1 - y)
        relay_from = (x ^ (1 - c), y ^ c)
        relay_to = (x ^ c, y ^ (1 - c))

        def slot(px, py, pc):
            return out_ref.at[4 * px + 2 * py + pc]

        def copy(k, block, to, src=None):
            return pltpu.make_async_remote_copy(
                src_ref=slot(*block) if src is None else src, dst_ref=slot(*block),
                send_sem=send_sems.at[k], recv_sem=recv_sems.at[k], device_id=to, device_id_type=MESH)

        mine = lambda: pltpu.make_async_copy(x_ref, slot(*me), local_sem)
        first = lambda: [copy(0, me, sibling, src=x_ref), copy(1, me, (*xchip, c), src=x_ref),
                         copy(2, me, (*ychip, c), src=x_ref)]
        relay = lambda: copy(3, (*relay_from, c), (*relay_to, c))
        arrivals = lambda: [copy(1, (*xchip, c), me), copy(2, (*ychip, c), me), copy(3, (*diag, c), me)]
        passed = lambda: [copy(4 + j, (*chip, c), sibling) for j, chip in enumerate((xchip, ychip, diag))]
        from_sibling = lambda: [copy(0, sibling, me)] + [
            copy(4 + j, (*chip, 1 - c), me) for j, chip in enumerate((xchip, ychip, diag))]
        return mine, first, relay, arrivals, passed, from_sibling

    def begin(*refs):
        mine, first, _, _, _, _ = ops(*refs)
        mine().start()
        for cp in first():
            cp.start()

    def middle(*refs):
        _, _, relay, arrivals, passed, _ = ops(*refs)
        for cp in arrivals()[:2]:
            cp.wait_recv()
        relay().start()
        for cp in passed()[:2]:
            cp.start()

    def end(*refs):
        mine, first, relay, arrivals, passed, from_sibling = ops(*refs)
        arrivals()[2].wait_recv()
        passed()[2].start()
        for cp in from_sibling():
            cp.wait_recv()
        for cp in first() + [relay()] + passed():
            cp.wait_send()
        mine().wait()

    return _Exchange([shard], [jax.ShapeDtypeStruct((NDEV, rows, cols), shard.dtype)],
                     [pltpu.SemaphoreType.DMA((7,)), pltpu.SemaphoreType.DMA((7,)), pltpu.SemaphoreType.DMA],
                     begin, middle, end)


def _grad_exchange(grads, bases, rows, strides=None):
    n = len(grads)
    strides = rows if strides is None else strides

    def copies(g_refs, out_refs, scr):
        send_sems, recv_sems, local_sems = scr
        x, y, c = lax.axis_index("x"), lax.axis_index("y"), lax.axis_index("c")
        me = 4 * x + 2 * y + c
        out = []
        for i in range(n):
            r = rows[i]
            src = lambda k: g_refs[i].at[pl.ds(pl.multiple_of(bases[i] + k * strides[i], 16), r), :]
            out.append(pltpu.make_async_copy(src(me), out_refs[i].at[me], local_sems.at[i]))
            for j in range(1, NDEV):
                px, py, pc = x ^ (j >> 2), y ^ ((j >> 1) & 1), c ^ (j & 1)
                out.append(pltpu.make_async_remote_copy(
                    src_ref=src(4 * px + 2 * py + pc), dst_ref=out_refs[i].at[me],
                    send_sem=send_sems.at[i, j - 1], recv_sem=recv_sems.at[i, j - 1],
                    device_id=(px, py, pc), device_id_type=MESH))
        return out

    def begin(*refs):
        for cp in copies(*refs):
            cp.start()

    def end(*refs):
        for cp in copies(*refs):
            cp.wait()

    return _Exchange(list(grads), [jax.ShapeDtypeStruct((NDEV, r, g.shape[1]), g.dtype) for g, r in zip(grads, rows)],
                     [pltpu.SemaphoreType.DMA((n, NDEV - 1)), pltpu.SemaphoreType.DMA((n, NDEV - 1)),
                      pltpu.SemaphoreType.DMA((n,))], begin, None, end)


def _row_gather_exchange(row):
    def copies(ins, outs, scr):
        (r_ref,), (o_ref,), (send_sems, recv_sems, local_sem) = ins, outs, scr
        x, y, c = lax.axis_index("x"), lax.axis_index("y"), lax.axis_index("c")
        me = 4 * x + 2 * y + c
        out = [pltpu.make_async_copy(r_ref, o_ref.at[me], local_sem)]
        for j in range(1, NDEV):
            out.append(pltpu.make_async_remote_copy(
                src_ref=r_ref, dst_ref=o_ref.at[me], send_sem=send_sems.at[j - 1], recv_sem=recv_sems.at[j - 1],
                device_id=(x ^ (j >> 2), y ^ ((j >> 1) & 1), c ^ (j & 1)), device_id_type=MESH))
        return out

    def begin(*refs):
        for cp in copies(*refs):
            cp.start()

    def end(*refs):
        for cp in copies(*refs):
            cp.wait()

    return _Exchange([row], [jax.ShapeDtypeStruct((NDEV,) + row.shape, row.dtype)],
                     [pltpu.SemaphoreType.DMA((NDEV - 1,)), pltpu.SemaphoreType.DMA((NDEV - 1,)), pltpu.SemaphoreType.DMA],
                     begin, None, end)


def _join(a, b):
    na, oa, sa = len(a.inputs), len(a.out_shape), len(a.scratch)
    split = lambda ins, outs, scr: ((ins[:na], outs[:oa], scr[:sa]), (ins[na:], outs[oa:], scr[sa:]))

    def begin(*refs):
        pa, pb = split(*refs)
        a.begin(*pa)
        b.begin(*pb)

    def end(*refs):
        pa, pb = split(*refs)
        a.end(*pa)
        b.end(*pb)

    return _Exchange(a.inputs + b.inputs, a.out_shape + b.out_shape, a.scratch + b.scratch, begin, None, end)


def _exchange_call(name, ex):
    ni, no = len(ex.inputs), len(ex.out_shape)

    def body(*refs):
        parts = (refs[:ni], refs[ni:ni + no], refs[ni + no:])
        ex.begin(*parts)
        if ex.middle is not None:
            ex.middle(*parts)
        ex.end(*parts)

    return pl.pallas_call(body, name=name, out_shape=ex.out_shape, in_specs=[HBM_SPEC] * ni, out_specs=[HBM_SPEC] * no,
                          scratch_shapes=ex.scratch)(*ex.inputs)


def _call(body, *, name, grid, in_specs, out_specs, out_shape, scratch_shapes, ins, carry=None):
    sem = ("arbitrary",) * len(grid)
    if carry is None:
        outs = pl.pallas_call(body, name=name, grid=grid, in_specs=in_specs, out_specs=out_specs, out_shape=out_shape,
                              scratch_shapes=scratch_shapes, compiler_params=_params(*sem))(*ins)
        return outs, None
    n_in, n_out, n_scr = len(ins), len(out_shape), len(scratch_shapes)
    ci, co = len(carry.inputs), len(carry.out_shape)
    total = 1
    for g in grid:
        total *= g

    def wrapped(*refs):
        own_in, ex_in = refs[:n_in], refs[n_in:n_in + ci]
        o0 = n_in + ci
        own_out, ex_out = refs[o0:o0 + n_out], refs[o0 + n_out:o0 + n_out + co]
        s0 = o0 + n_out + co
        own_scr, ex_scr = refs[s0:s0 + n_scr], refs[s0 + n_scr:]
        step = pl.program_id(0)
        for d in range(1, len(grid)):
            step = step * grid[d] + pl.program_id(d)
        parts = (ex_in, ex_out, ex_scr)
        pl.when(step == 0)(lambda: carry.begin(*parts))
        body(*own_in, *own_out, *own_scr)
        if carry.middle is not None:
            pl.when(step == (3 * total) // 4)(lambda: carry.middle(*parts))
        pl.when(step == total - 1)(lambda: carry.end(*parts))

    outs = pl.pallas_call(
        wrapped, name=name, grid=grid, in_specs=list(in_specs) + [HBM_SPEC] * ci,
        out_specs=list(out_specs) + [HBM_SPEC] * co, out_shape=list(out_shape) + list(carry.out_shape),
        scratch_shapes=list(scratch_shapes) + list(carry.scratch), compiler_params=_params(*sem),
    )(*ins, *carry.inputs)
    return outs[:n_out], outs[n_out:]


def _ffn_fwd(name, xin, affine, wall, offs, tm, loss=None, carry=None):
    T = xin.shape[0]
    nt = T // tm

    def body(*refs):
        it = iter(refs)
        x_ref = next(it)
        if affine is not None:
            g_ref, b_ref = next(it), next(it)
        wall_ref = next(it)
        if loss is not None:
            go_ref, bo_ref, tgt_ref = next(it), next(it), next(it)
        xh_ref, r_ref, a_ref, b2_ref, yb_ref = (next(it) for _ in range(5))
        if loss is not None:
            loss_ref = next(it)
        w1, w3, w2, sem = (next(it) for _ in range(4))

        @pl.when(pl.program_id(0) == 0)
        def _():
            _load_rows(wall_ref, [(w1, offs[0], RF), (w3, offs[1], RF), (w2, offs[2], RF)], sem)
            if loss is not None:
                loss_ref[...] = jnp.zeros_like(loss_ref)

        parts = [slice(p, min(p + PART_ROWS, tm)) for p in range(0, tm, PART_ROWS)]
        ys, ybs = [], []
        for rs in parts:
            y = x_ref[rs, :]
            if affine is not None:
                y = y * g_ref[...] + b_ref[...]
            ys.append(y)
            ybs.append(y.astype(BF16))
            yb_ref[rs, :] = ybs[-1]
        ab = [(_dot_nt(yb, w1[...]).astype(BF16), _dot_nt(yb, w3[...]).astype(BF16)) for yb in ybs]
        us = []
        for rs, (a, b) in zip(parts, ab):
            a_ref[rs, :] = a
            b2_ref[rs, :] = b
            af, bf = a.astype(F32), b.astype(F32)
            us.append((af * _sigmoid(af) * bf).astype(BF16))
        fs = [_dot(u, w2[...]) for u in us]
        for rs, y, f in zip(parts, ys, fs):
            xh, r = _ln_fwd(ALPHA * y + 0.5 * f)
            xh_ref[rs, :] = xh
            r_ref[rs, :] = jnp.broadcast_to(r, (xh.shape[0], 128))
            if loss is not None:
                e = xh * go_ref[...] + bo_ref[...] - tgt_ref[rs, :]
                loss_ref[...] += jnp.sum(e * e) * (0.5 / D)

    row = lambda w: pl.BlockSpec((tm, w), lambda i: (i, 0))
    vec = pl.BlockSpec((1, D), lambda i: (0, 0))
    ins, in_specs = [xin], [row(D)]
    if affine is not None:
        ins += list(affine)
        in_specs += [vec, vec]
    ins.append(wall)
    in_specs.append(HBM_SPEC)
    if loss is not None:
        ins += list(loss)
        in_specs += [vec, vec, row(D)]
    out_shape = [jax.ShapeDtypeStruct((T, D), F32), jax.ShapeDtypeStruct((T, 128), F32),
                 jax.ShapeDtypeStruct((T, DFF), BF16), jax.ShapeDtypeStruct((T, DFF), BF16),
                 jax.ShapeDtypeStruct((T, D), BF16)]
    out_specs = [row(D), row(128), row(DFF), row(DFF), row(D)]
    if loss is not None:
        out_shape.append(jax.ShapeDtypeStruct((8, 128), F32))
        out_specs.append(pl.BlockSpec((8, 128), lambda i: (0, 0)))
    return _call(body, name=name, grid=(nt,), in_specs=in_specs, out_specs=out_specs, out_shape=out_shape,
                 scratch_shapes=[pltpu.VMEM((DFF, D), BF16)] * 3 + [pltpu.SemaphoreType.DMA((3 * NDEV,))],
                 ins=ins, carry=carry)


def _ffn_bwd(name, dy_src, xh, r, g, a, b, wall, offs, tm, with_dx=True, carry=None):
    T = xh.shape[0]
    nt = T // tm
    from_loss = dy_src[0] == "loss"

    def body(*refs):
        it = iter(refs)
        if from_loss:
            bo_ref, tgt_ref = next(it), next(it)
        else:
            dy_ref = next(it)
        xh_ref, r_ref, g_ref, a_ref, b_ref, wall_ref = (next(it) for _ in range(6))
        dyin_ref, dab_ref, u_ref, df_ref, dg_ref, db_ref = (next(it) for _ in range(6))
        if with_dx:
            w1, w3 = next(it), next(it)
        w2, sem = next(it), next(it)

        @pl.when(pl.program_id(0) == 0)
        def _():
            _load_rows(wall_ref, ([(w1, offs[0], RF), (w3, offs[1], RF)] if with_dx else []) + [(w2, offs[2], RF)], sem)
            dg_ref[...] = jnp.zeros_like(dg_ref)
            db_ref[...] = jnp.zeros_like(db_ref)

        parts = [slice(p, min(p + PART_ROWS, tm)) for p in range(0, tm, PART_ROWS)]
        gv = g_ref[...]
        dz, df = [], []
        for rs in parts:
            xhv = xh_ref[rs, :]
            if from_loss:
                dy = (xhv * gv + bo_ref[...] - tgt_ref[rs, :]) * (1.0 / D)
            else:
                dy = dy_ref[rs, :]
            dzp, dgp, dbp = _ln_bwd(dy, xhv, r_ref[rs, :1], gv)
            dg_ref[...] += dgp
            db_ref[...] += dbp
            dz.append(dzp)
            df.append((0.5 * dzp).astype(BF16))
            df_ref[rs, :] = df[-1]
        du = [_dot_nt(d, w2[...]) for d in df]
        da, dbb = [], []
        for rs, dup in zip(parts, du):
            af, bf = a_ref[rs, :].astype(F32), b_ref[rs, :].astype(F32)
            s = _sigmoid(af)
            sl = af * s
            u_ref[rs, :] = (sl * bf).astype(BF16)
            da.append((dup * bf * (s * (1.0 + af * (1.0 - s)))).astype(BF16))
            dbb.append((dup * sl).astype(BF16))
            dab_ref[rs, :DFF] = da[-1]
            dab_ref[rs, DFF:] = dbb[-1]
        for rs, dzp, dap, dbp in zip(parts, dz, da, dbb):
            dyin_ref[rs, :] = ALPHA * dzp + _dot(dap, w1[...]) + _dot(dbp, w3[...]) if with_dx else dzp

    row = lambda w: pl.BlockSpec((tm, w), lambda i: (i, 0))
    vec = pl.BlockSpec((1, D), lambda i: (0, 0))
    if from_loss:
        ins, in_specs = [dy_src[1], dy_src[2]], [vec, row(D)]
    else:
        ins, in_specs = [dy_src[1]], [row(D)]
    ins += [xh, r, g, a, b, wall]
    in_specs += [row(D), row(128), vec, row(DFF), row(DFF), HBM_SPEC]
    return _call(
        body, name=name, grid=(nt,), in_specs=in_specs,
        out_specs=[row(D), row(2 * DFF), row(DFF), row(D), vec, vec],
        out_shape=[jax.ShapeDtypeStruct((T, D), F32), jax.ShapeDtypeStruct((T, 2 * DFF), BF16),
                   jax.ShapeDtypeStruct((T, DFF), BF16), jax.ShapeDtypeStruct((T, D), BF16),
                   jax.ShapeDtypeStruct((1, D), F32), jax.ShapeDtypeStruct((1, D), F32)],
        scratch_shapes=[pltpu.VMEM((DFF, D), BF16)] * (3 if with_dx else 1) + [pltpu.SemaphoreType.DMA((3 * NDEV,))],
        ins=ins, carry=carry)


def _ffn_dx(name, dab, dz, wall, offs, tm, carry=None):
    T = dz.shape[0]

    def body(dab_ref, dz_ref, wall_ref, o_ref, w13, sem):
        @pl.when(pl.program_id(0) == 0)
        def _():
            _load_rows(wall_ref, [(w13.at[pl.ds(0, DFF), :], offs[0], RF), (w13.at[pl.ds(DFF, DFF), :], offs[1], RF)], sem)

        for p in range(0, tm, PART_ROWS):
            rs = slice(p, min(p + PART_ROWS, tm))
            o_ref[rs, :] = ALPHA * dz_ref[rs, :] + _dot(dab_ref[rs, :], w13[...])

    row = lambda w: pl.BlockSpec((tm, w), lambda i: (i, 0))
    (out,), ex = _call(
        body, name=name, grid=(T // tm,), in_specs=[row(2 * DFF), row(D), HBM_SPEC], out_specs=[row(D)],
        out_shape=[jax.ShapeDtypeStruct((T, D), F32)],
        scratch_shapes=[pltpu.VMEM((2 * DFF, D), BF16), pltpu.SemaphoreType.DMA((2 * NDEV,))],
        ins=[dab, dz, wall], carry=carry)
    return out, ex


def _wgrad(name, a, b, bn, tk, col0=0, ncols=None, carry=None):
    T = a.shape[0]
    N = a.shape[1] if ncols is None else ncols
    nk = T // tk
    c0 = col0 // bn

    def body(a_ref, b_ref, o_ref, acc):
        k = pl.program_id(1)

        @pl.when(k == 0)
        def _():
            acc[...] = jnp.zeros_like(acc)

        acc[...] += _dot_tn(a_ref[...], b_ref[...])

        @pl.when(k == nk - 1)
        def _():
            o_ref[...] = acc[...].astype(BF16)

    (out,), ex = _call(
        body, name=name, grid=(N // bn, nk),
        in_specs=[pl.BlockSpec((tk, bn), lambda n, k: (k, n + c0)), pl.BlockSpec((tk, D), lambda n, k: (k, 0))],
        out_specs=[pl.BlockSpec((bn, D), lambda n, k: (n, 0))],
        out_shape=[jax.ShapeDtypeStruct((N, D), BF16)],
        scratch_shapes=[pltpu.VMEM((bn, D), F32)], ins=[a, b], carry=carry)
    return out, ex


def _rope_table(T):
    pos = jnp.arange(T, dtype=F32)
    inv_freq = ROPE_THETA ** (-jnp.arange(0, ROPE_DIM, 2, dtype=F32) / ROPE_DIM)
    ang = pos[:, None] * inv_freq[None, :]
    return jnp.pad(jnp.concatenate([jnp.cos(ang), jnp.sin(ang)], axis=1), ((0, 0), (0, 128 - ROPE_DIM)))


def _rope_expand(cs):
    half = ROPE_DIM // 2
    lane = lax.broadcasted_iota(jnp.int32, cs.shape, 1)
    cos = jnp.where(lane < half, cs, 0.0)
    sin = jnp.where((lane >= half) & (lane < ROPE_DIM), cs, 0.0)
    both = lambda t: t + pltpu.roll(t, HEAD_DIM, 1)
    c = jnp.where(jnp.bitwise_and(lane, HEAD_DIM - 1) < ROPE_DIM, both(cos + pltpu.roll(cos, half, 1)), 1.0)
    return c, -both(pltpu.roll(sin, 128 - half, 1)), both(sin)


def _rope(t, c, s1, s2):
    n = t.shape[1] // 128
    ct, s1t, s2t = (jnp.tile(v, (1, n)) for v in (c, s1, s2))
    w = t.shape[1]
    return t * ct + pltpu.roll(t, w - 8, 1) * s1t + pltpu.roll(t, 8, 1) * s2t


def _rope_t(dr, c, s1, s2):
    n = dr.shape[1] // 128
    ct, s1t, s2t = (jnp.tile(v, (1, n)) for v in (c, s1, s2))
    w = dr.shape[1]
    return dr * ct + pltpu.roll(dr * s1t, 8, 1) + pltpu.roll(dr * s2t, w - 8, 1)


_Q, _K, _V = (0, 1024), (1024, 256), (1280, 256)
_HG = (1536, HG_W)
_GATES = (5632, GATE_W)


def _inproj_fwd(name, xh, g, b, wall, b_in, rope, tm, carry=None):
    T = xh.shape[0]

    def body(xh_ref, g_ref, b_ref, wall_ref, bin_ref, cs_ref,
             qkv_ref, kt_ref, vt_ref, hg_ref, gate_ref, yb_ref, w, sem):
        @pl.when(pl.program_id(0) == 0)
        def _():
            _load_rows(wall_ref, [(w, 0, RIN)], sem)

        yb = (xh_ref[...] * g_ref[...] + b_ref[...]).astype(BF16)
        yb_ref[...] = yb
        c, s1, s2 = _rope_expand(cs_ref[...])

        def piece(start, width):
            return _dot_nt(yb, w[start:start + width, :]) + bin_ref[:, start:start + width]

        k = _rope(piece(*_K), c, s1, s2)
        v = piece(*_V)
        qkv_ref[:, 0:1024] = _rope(piece(*_Q), c, s1, s2).astype(BF16)
        qkv_ref[:, 1024:1280] = k.astype(BF16)
        qkv_ref[:, 1280:1536] = v.astype(BF16)
        kt_ref[...] = k.T.astype(BF16)
        vt_ref[...] = v.T.astype(BF16)
        for j in range(4):
            hg_ref[:, 1024 * j:1024 * (j + 1)] = piece(_HG[0] + 1024 * j, 1024)
        for j in range(2):
            gate_ref[:, 1024 * j:1024 * (j + 1)] = piece(_GATES[0] + 1024 * j, 1024).astype(BF16)

    row = lambda wd: pl.BlockSpec((tm, wd), lambda i: (i, 0))
    vec = lambda wd: pl.BlockSpec((1, wd), lambda i: (0, 0))
    colt = pl.BlockSpec((256, tm), lambda i: (0, i))
    return _call(
        body, name=name, grid=(T // tm,),
        in_specs=[row(D), vec(D), vec(D), HBM_SPEC, vec(DIN), row(128)],
        out_specs=[row(QKV_W), colt, colt, row(HG_W), row(GATE_W), row(D)],
        out_shape=[jax.ShapeDtypeStruct((T, QKV_W), BF16), jax.ShapeDtypeStruct((256, T), BF16),
                   jax.ShapeDtypeStruct((256, T), BF16), jax.ShapeDtypeStruct((T, HG_W), F32),
                   jax.ShapeDtypeStruct((T, GATE_W), BF16), jax.ShapeDtypeStruct((T, D), BF16)],
        scratch_shapes=[pltpu.VMEM((DIN, D), BF16), pltpu.SemaphoreType.DMA((NDEV,))],
        ins=[xh, g, b, wall, b_in, rope], carry=carry)


def _inproj_bwd(name, dq, dk, dv, dhg, dgates, dz2, wall, rope, tm, carry=None):
    T = dq.shape[0]

    def body(dq_ref, dk_ref, dv_ref, d0, d1, d2, d3, dga_ref, dgh_ref, dz_ref, wall_ref, cs_ref,
             dy_ref, dproj_ref, dbin_ref, w, sem):
        @pl.when(pl.program_id(0) == 0)
        def _():
            _load_rows(wall_ref, [(w, 0, RIN)], sem)
            dbin_ref[...] = jnp.zeros_like(dbin_ref)

        c, s1, s2 = _rope_expand(cs_ref[...])
        acc = ALPHA * dz_ref[...]
        pieces = [(_Q[0], _rope_t(dq_ref[...], c, s1, s2)), (_K[0], _rope_t(dk_ref[...], c, s1, s2)),
                  (_V[0], dv_ref[...])]
        pieces += [(_HG[0] + 1024 * j, r[...]) for j, r in enumerate((d0, d1, d2, d3))]
        pieces += [(_GATES[0], dga_ref[...]), (_GATES[0] + 1024, dgh_ref[...])]
        for start, val in pieces:
            width = val.shape[1]
            dbin_ref[:, start:start + width] += jnp.sum(val.astype(F32), axis=0, keepdims=True)
            vb = val.astype(BF16)
            dproj_ref[:, start:start + width] = vb
            acc = acc + _dot(vb, w[start:start + width, :])
        dy_ref[...] = acc

    row = lambda wd: pl.BlockSpec((tm, wd), lambda i: (i, 0))
    return _call(
        body, name=name, grid=(T // tm,),
        in_specs=[row(D), row(256), row(256)] + [row(D)] * 4 + [row(D), row(D), row(D), HBM_SPEC, row(128)],
        out_specs=[row(D), row(DIN), pl.BlockSpec((1, DIN), lambda i: (0, 0))],
        out_shape=[jax.ShapeDtypeStruct((T, D), F32), jax.ShapeDtypeStruct((T, DIN), BF16),
                   jax.ShapeDtypeStruct((1, DIN), F32)],
        scratch_shapes=[pltpu.VMEM((DIN, D), BF16), pltpu.SemaphoreType.DMA((NDEV,))],
        ins=[dq, dk, dv, *dhg, *dgates, dz2, wall, rope], carry=carry)


def _halves(t):
    lane = lax.broadcasted_iota(jnp.int32, t.shape, 1)
    low = lane < HEAD_DIM
    sw = pltpu.roll(t, HEAD_DIM, 1)
    zero = jnp.zeros_like(t)
    h0 = (jnp.where(low, t, zero), jnp.where(low, zero, sw))
    h1 = (jnp.where(low, sw, zero), jnp.where(low, zero, t))
    return h0, h1


def _lane_stack(p_ref, c_ref, gp):
    sl = slice(128 * gp, 128 * (gp + 1))
    hp, hc = _halves(p_ref[:, sl].astype(F32)), _halves(c_ref[:, sl].astype(F32))
    return [jnp.concatenate([hp[gg][0], hc[gg][0], hp[gg][1], hc[gg][1]], axis=0).astype(BF16) for gg in range(2)]


def _row_stack(tp_ref, tc_ref, g):
    band = jnp.concatenate([tp_ref[64 * g:64 * (g + 1), :], tc_ref[64 * g:64 * (g + 1), :]], axis=1)
    z = jnp.zeros_like(band)
    return [jnp.concatenate([band, z], axis=0), jnp.concatenate([z, band], axis=0)]


def _fold(prevm, t4, hh):
    return jnp.where(prevm, t4[256 * hh:256 * hh + 128, :], t4[256 * hh + 128:256 * hh + 256, :])


def _unfold(prevm, t):
    return jnp.concatenate([jnp.where(prevm, t, 0.0), jnp.where(prevm, 0.0, t)], axis=0).astype(BF16)


def _attn_softmax(s, kill, sink):
    s = jnp.where(kill, NEG_INF, s)
    m = jnp.maximum(jnp.max(s, axis=0, keepdims=True), sink)
    p = jnp.exp(s - m)
    es = jnp.exp(sink - m)
    inv = 1.0 / (jnp.sum(p, axis=0, keepdims=True) + es)
    return p * inv, es * inv


def _attn_masks(first):
    row = lax.broadcasted_iota(jnp.int32, (ATTN_BLOCK, 2 * ATTN_BLOCK), 0)
    lane = lax.broadcasted_iota(jnp.int32, (ATTN_BLOCK, 2 * ATTN_BLOCK), 1)
    prevm = row > lane % ATTN_BLOCK
    return prevm, jnp.logical_and(first, prevm)


def _pair_rows(ref, g):
    return jnp.concatenate([ref[:, 256 * g:256 * g + 128], ref[:, 256 * g + 128:256 * (g + 1)]], axis=0)


def _pair_sinks(sink_ref, g, hh):
    h0, h1 = 4 * g + hh, 4 * g + 2 + hh
    return jnp.concatenate([jnp.broadcast_to(sink_ref[:, h0:h0 + 1], (1, ATTN_BLOCK)),
                            jnp.broadcast_to(sink_ref[:, h1:h1 + 1], (1, ATTN_BLOCK))], axis=1)


def _attn_fwd(name, qkv, vt, sinks):
    T = qkv.shape[0]
    nbk = min(ATTN_TILE, T // ATTN_BLOCK)
    rows = nbk * ATTN_BLOCK
    scale = HEAD_DIM ** -0.5

    def block(q_ref, kp_ref, kc_ref, vtp_ref, vtc_ref, sink_ref, o_ref, first):
        prevm, kill = _attn_masks(first)
        kst = _lane_stack(kp_ref, kc_ref, 0) + _lane_stack(kp_ref, kc_ref, 1)
        vts = [_row_stack(vtp_ref, vtc_ref, g) for g in range(N_KV_HEADS)]
        s8 = [_dot_nt(kst[g], _pair_rows(q_ref, g)) * scale for g in range(N_KV_HEADS)]
        pu = [[_unfold(prevm, _attn_softmax(_fold(prevm, s8[g], hh), kill, _pair_sinks(sink_ref, g, hh))[0])
               for hh in range(2)] for g in range(N_KV_HEADS)]
        for g in range(N_KV_HEADS):
            ot = _dot(vts[g][0], pu[g][0]) + _dot(vts[g][1], pu[g][1])
            o_ref[:, 256 * g:256 * g + 128] = ot[:, :ATTN_BLOCK].T.astype(BF16)
            o_ref[:, 256 * g + 128:256 * (g + 1)] = ot[:, ATTN_BLOCK:].T.astype(BF16)

    def body(q_ref, kp_ref, kc_ref, vtp_ref, vtc_ref, sink_ref, o_ref):
        for b in range(nbk):
            rs = slice(ATTN_BLOCK * b, ATTN_BLOCK * (b + 1))
            ps = slice(ATTN_BLOCK * (b - 1), ATTN_BLOCK * b)
            block(q_ref.at[rs, :], kp_ref if b == 0 else kc_ref.at[ps, :], kc_ref.at[rs, :],
                  vtp_ref if b == 0 else vtc_ref.at[:, ps], vtc_ref.at[:, rs], sink_ref, o_ref.at[rs, :],
                  pl.program_id(0) == 0 if b == 0 else False)

    prev = lambda i: jnp.maximum(nbk * i - 1, 0)
    return pl.pallas_call(
        body, name=name, grid=(T // rows,),
        in_specs=[pl.BlockSpec((rows, D), lambda i: (i, 0)),
                  pl.BlockSpec((ATTN_BLOCK, 256), lambda i: (prev(i), 4)),
                  pl.BlockSpec((rows, 256), lambda i: (i, 4)),
                  pl.BlockSpec((256, ATTN_BLOCK), lambda i: (0, prev(i))),
                  pl.BlockSpec((256, rows), lambda i: (0, i)),
                  pl.BlockSpec((1, N_Q_HEADS), lambda i: (0, 0))],
        out_specs=pl.BlockSpec((rows, D), lambda i: (i, 0)),
        out_shape=jax.ShapeDtypeStruct((T, D), BF16),
        compiler_params=_params("arbitrary"),
    )(qkv, qkv, qkv, vt, vt, sinks)


ATTN_TILE = 4
ATTN_BWD_ORDER = ((0, 0), (0, 1), (1, 0), (0, 2), (1, 1), (2, 0), (0, 3), (1, 2), (2, 1), (1, 3), (2, 2), (2, 3))


def _attn_bwd(name, qkv, kt, sinks, dya, carry=None):
    T = qkv.shape[0]
    nbk = min(ATTN_TILE, T // ATTN_BLOCK)
    rows = nbk * ATTN_BLOCK
    scale = HEAD_DIM ** -0.5

    def block(q_ref, kp_ref, kc_ref, vp_ref, vc_ref, ktp_ref, ktc_ref, sink_ref, do_ref, dq_ref, first):
        prevm, kill = _attn_masks(first)
        kst = _lane_stack(kp_ref, kc_ref, 0) + _lane_stack(kp_ref, kc_ref, 1)
        vst = _lane_stack(vp_ref, vc_ref, 0) + _lane_stack(vp_ref, vc_ref, 1)
        kts = [_row_stack(ktp_ref, ktc_ref, g) for g in range(N_KV_HEADS)]
        lane = lax.broadcasted_iota(jnp.int32, (2 * ATTN_BLOCK, 128), 1)
        low = lane < HEAD_DIM
        slane = lax.broadcasted_iota(jnp.int32, (1, 128), 1)
        dkz = [jnp.zeros((2 * ATTN_BLOCK, 128), F32) for _ in range(N_KV_HEADS)]
        dvz = [jnp.zeros((2 * ATTN_BLOCK, 128), F32) for _ in range(N_KV_HEADS)]
        dsink = jnp.zeros((1, 128), F32)
        groups = range(N_KV_HEADS)
        qcat, docat, s8, dp8 = {}, {}, {}, {}
        ds_u, p_u, dsinks = {}, {}, []

        def scores(g):
            qcat[g], docat[g] = _pair_rows(q_ref, g), _pair_rows(do_ref, g)
            s8[g] = _dot_nt(kst[g], qcat[g]) * scale
            dp8[g] = _dot_nt(vst[g], docat[g])

        def algebra(g):
            for hh in range(2):
                pn, ps = _attn_softmax(_fold(prevm, s8[g], hh), kill, _pair_sinks(sink_ref, g, hh))
                dp = _fold(prevm, dp8[g], hh)
                delta = jnp.sum(pn * dp, axis=0, keepdims=True)
                sd = ps * delta
                dsinks.append(jnp.where(slane == 4 * g + hh, -jnp.sum(sd[:, :ATTN_BLOCK]), 0.0)
                              + jnp.where(slane == 4 * g + 2 + hh, -jnp.sum(sd[:, ATTN_BLOCK:]), 0.0))
                ds_u[g, hh] = _unfold(prevm, pn * (dp - delta) * scale)
                p_u[g, hh] = _unfold(prevm, pn)

        def grads(g):
            dqt = _dot(kts[g][0], ds_u[g, 0]) + _dot(kts[g][1], ds_u[g, 1])
            dq_ref[:, 256 * g:256 * g + 128] = dqt[:, :ATTN_BLOCK].T
            dq_ref[:, 256 * g + 128:256 * (g + 1)] = dqt[:, ATTN_BLOCK:].T
            for hh in range(2):
                own = low if hh == 0 else jnp.logical_not(low)
                dk_h = jnp.where(own, _dot(ds_u[g, hh], qcat[g]), 0.0)
                dv_h = jnp.where(own, _dot(p_u[g, hh], docat[g]), 0.0)
                if hh != g % 2:
                    dk_h = pltpu.roll(dk_h, HEAD_DIM, 1)
                    dv_h = pltpu.roll(dv_h, HEAD_DIM, 1)
                dkz[g] = dkz[g] + dk_h
                dvz[g] = dvz[g] + dv_h

        for stage, g in ATTN_BWD_ORDER:
            (scores, algebra, grads)[stage](g)
        for t in dsinks:
            dsink = dsink + t
        pairs = range(N_KV_HEADS // 2)
        return dsink, [dkz[2 * gp] + dkz[2 * gp + 1] for gp in pairs], [dvz[2 * gp] + dvz[2 * gp + 1] for gp in pairs]

    def body(q_ref, kp_ref, kc_ref, vp_ref, vc_ref, ktp_ref, ktc_ref, sink_ref, do_ref,
             dq_ref, dk_ref, dv_ref, ds_ref):
        i = pl.program_id(0)

        @pl.when(i == 0)
        def _():
            ds_ref[...] = jnp.zeros_like(ds_ref)

        res = []
        for b in range(nbk):
            rs = slice(ATTN_BLOCK * b, ATTN_BLOCK * (b + 1))
            ps = slice(ATTN_BLOCK * (b - 1), ATTN_BLOCK * b)
            res.append(block(
                q_ref.at[rs, :], kp_ref if b == 0 else kc_ref.at[ps, :], kc_ref.at[rs, :],
                vp_ref if b == 0 else vc_ref.at[ps, :], vc_ref.at[rs, :],
                ktp_ref if b == 0 else ktc_ref.at[:, ps], ktc_ref.at[:, rs], sink_ref, do_ref.at[rs, :],
                dq_ref.at[rs, :], i == 0 if b == 0 else False))
        dsink = res[0][0]
        for r in res[1:]:
            dsink = dsink + r[0]
        ds_ref[...] += dsink
        for gp in range(N_KV_HEADS // 2):
            cols = slice(128 * gp, 128 * (gp + 1))
            for b in range(nbk):
                cur = pl.ds(pl.multiple_of(i * rows + ATTN_BLOCK * b, ATTN_BLOCK), ATTN_BLOCK)
                dkb, dvb = res[b][1][gp][ATTN_BLOCK:, :], res[b][2][gp][ATTN_BLOCK:, :]
                if b + 1 < nbk:
                    dkb, dvb = dkb + res[b + 1][1][gp][:ATTN_BLOCK, :], dvb + res[b + 1][2][gp][:ATTN_BLOCK, :]
                dk_ref[cur, cols] = dkb
                dv_ref[cur, cols] = dvb

            @pl.when(i > 0)
            def _():
                prv = pl.ds(pl.multiple_of(jnp.maximum(i * rows - ATTN_BLOCK, 0), ATTN_BLOCK), ATTN_BLOCK)
                dk_ref[prv, cols] += res[0][1][gp][:ATTN_BLOCK, :]
                dv_ref[prv, cols] += res[0][2][gp][:ATTN_BLOCK, :]

    prev = lambda i: jnp.maximum(nbk * i - 1, 0)
    whole = lambda w: pl.BlockSpec((T, w), lambda i: (0, 0))
    return _call(
        body, name=name, grid=(T // rows,), scratch_shapes=[], ins=[qkv, qkv, qkv, qkv, qkv, kt, kt, sinks, dya],
        carry=carry,
        in_specs=[pl.BlockSpec((rows, D), lambda i: (i, 0)),
                  pl.BlockSpec((ATTN_BLOCK, 256), lambda i: (prev(i), 4)),
                  pl.BlockSpec((rows, 256), lambda i: (i, 4)),
                  pl.BlockSpec((ATTN_BLOCK, 256), lambda i: (prev(i), 5)),
                  pl.BlockSpec((rows, 256), lambda i: (i, 5)),
                  pl.BlockSpec((256, ATTN_BLOCK), lambda i: (0, prev(i))),
                  pl.BlockSpec((256, rows), lambda i: (0, i)),
                  pl.BlockSpec((1, N_Q_HEADS), lambda i: (0, 0)),
                  pl.BlockSpec((rows, D), lambda i: (i, 0))],
        out_specs=[pl.BlockSpec((rows, D), lambda i: (i, 0)), whole(256), whole(256),
                   pl.BlockSpec((1, 128), lambda i: (0, 0))],
        out_shape=[jax.ShapeDtypeStruct((T, D), F32), jax.ShapeDtypeStruct((T, 256), F32),
                   jax.ShapeDtypeStruct((T, 256), F32), jax.ShapeDtypeStruct((1, 128), F32)])


HG_FWD = (256, 8, 512)
HG_BWD = (128, 8, 512)


def _chunk_sum(tri, x):
    w = x.shape[1]
    h1 = x.astype(BF16)
    h2 = (x - h1.astype(F32)).astype(BF16)
    r = _dot(tri, jnp.concatenate([h1, h2], axis=1))
    return r[:, :w] + r[:, w:]


def _chunk_tri(n, upper=False):
    ri = lax.broadcasted_iota(jnp.int32, (n, n), 0)
    ci = lax.broadcasted_iota(jnp.int32, (n, n), 1)
    same_chunk = jnp.bitwise_xor(ri, ci) < HGRN_CHUNK
    return jnp.where(((ri <= ci) if upper else (ri >= ci)) & same_chunk, 1.0, 0.0).astype(BF16)


def _chunks(t):
    return [t[HGRN_CHUNK * c:HGRN_CHUNK * (c + 1), :] for c in range(t.shape[0] // HGRN_CHUNK)]


def _lower_bound(lbl_ref):
    l0, l1 = lbl_ref[0:1, :], lbl_ref[1:2, :]
    m = jnp.maximum(l0, l1)
    e0, e1 = jnp.exp(l0 - m), jnp.exp(l1 - m)
    return e0 / (e0 + e1)


def _heads(t):
    return [t[:, 128 * h:128 * (h + 1)] for h in range(t.shape[1] // 128)]


def _per_head(fn, *wide):
    return jnp.concatenate([fn(*parts) for parts in zip(*[_heads(t) for t in wide])], axis=1)


def _hgrn_sub(fl, qh, vv, lb, tril_b):
    w = fl.shape[1]
    sg = _sigmoid(fl)
    f = lb + (1.0 - lb) * sg
    k = 1.0 - f
    gc = _chunk_sum(tril_b, jnp.log(f))
    gl_c = [t[HGRN_CHUNK - 1:HGRN_CHUNK, :] for t in _chunks(gc)]
    gl = jnp.concatenate([jnp.broadcast_to(g, (HGRN_CHUNK, w)) for g in gl_c], axis=0)
    sq = _sigmoid(qh)
    eg = jnp.exp(gc)
    eng = jnp.exp(-gc)
    elg = jnp.exp(gl - gc)
    qd = qh * sq * eg
    ki = k * eng
    ke = k * elg
    qd_b, ki_b, ke_b, v_b = (t.astype(BF16) for t in (qd, ki, ke, vv))
    am_b = [jnp.where(tril_b > 0, _dot_nt(qq, kk), 0.0).astype(BF16) for qq, kk in zip(_heads(qd_b), _heads(ki_b))]
    return dict(sg=sg, f=f, sq=sq, eg=eg, eng=eng, elg=elg, qd=qd, ki=ki, ke=ke, egl=[jnp.exp(g) for g in gl_c],
                am_b=am_b, qd_b=qd_b, ki_b=ki_b, ke_b=ke_b, v_b=v_b)


def _hgrn_out(q, st_b):
    outs = []
    for h, (qd_h, v_h) in enumerate(zip(_heads(q["qd_b"]), _heads(q["v_b"]))):
        inter = [_dot_nt(qc, s) for qc, s in zip(_chunks(qd_h), st_b[h])]
        outs.append(_dot(q["am_b"][h], v_h) + jnp.concatenate(inter, axis=0))
    return jnp.concatenate(outs, axis=1)


def _head_rms(o):
    return _per_head(lambda t: jnp.broadcast_to(
        lax.rsqrt(jnp.mean(t * t, axis=-1, keepdims=True) + RMS_EPS), t.shape), o)


def _hgrn_fwd(name, hg, lb_logits, norm_g, cfg):
    T = hg.shape[0]
    sub_rows, hps, th = min(cfg[0], T), cfg[1], min(cfg[2], T)
    cps = sub_rows // HGRN_CHUNK
    nc = th // HGRN_CHUNK

    def body(fl_ref, qh_ref, v_ref, og_ref, lbl_ref, ng_ref, y_ref, ss_ref, st):
        @pl.when(pl.program_id(1) == 0)
        def _():
            st[...] = jnp.zeros_like(st)

        lbs = _lower_bound(lbl_ref)
        ng = jnp.tile(ng_ref[...], (1, hps))
        tril_b = _chunk_tri(sub_rows)

        def sub(si, carry):
            rows = pl.ds(pl.multiple_of(si * sub_rows, sub_rows), sub_rows)
            q = _hgrn_sub(fl_ref[rows, :], qh_ref[rows, :], v_ref[rows, :], lbs, tril_b)
            v_hc = [_chunks(t) for t in _heads(q["v_b"])]
            k_hc = [_chunks(t) for t in _heads(q["ke_b"])]
            s = [st[h] for h in range(hps)]
            st_b = [[] for _ in range(hps)]
            for c in range(cps):
                egl = _heads(q["egl"][c])
                for h in range(hps):
                    ss_ref[h, si * cps + c] = s[h]
                    st_b[h].append(s[h].astype(BF16))
                    s[h] = s[h] * egl[h] + _dot_tn(v_hc[h][c], k_hc[h][c])
            for h in range(hps):
                st[h] = s[h]
            o = _hgrn_out(q, st_b)
            og = og_ref[rows, :]
            y_ref[rows, :] = (o * _head_rms(o) * ng * (og * _sigmoid(og))).astype(BF16)
            return carry

        lax.fori_loop(0, th // sub_rows, sub, 0)

    wd = 128 * hps
    col = lambda j: pl.BlockSpec((th, wd), lambda h, t: (t, (8 // hps) * j + h))
    return pl.pallas_call(
        body, name=name, grid=(HGRN_HEADS // hps, T // th),
        in_specs=[col(0), col(1), col(2), col(3), pl.BlockSpec((2, wd), lambda h, t: (0, h)),
                  pl.BlockSpec((1, 128), lambda h, t: (0, 0))],
        out_specs=[pl.BlockSpec((th, wd), lambda h, t: (t, h)),
                   pl.BlockSpec((hps, nc, 128, 128), lambda h, t: (h, t, 0, 0))],
        out_shape=[jax.ShapeDtypeStruct((T, D), BF16),
                   jax.ShapeDtypeStruct((HGRN_HEADS, T // HGRN_CHUNK, 128, 128), F32)],
        scratch_shapes=[pltpu.VMEM((hps, 128, 128), F32)],
        compiler_params=_params("parallel", "arbitrary"),
    )(hg, hg, hg, hg, lb_logits, norm_g)


def _hgrn_bwd(name, hg, lb_logits, norm_g, sstart, dyh, cfg, carry=None):
    T = hg.shape[0]
    sub_rows, hps, th = min(cfg[0], T), cfg[1], min(cfg[2], T)
    cps = sub_rows // HGRN_CHUNK
    nc = th // HGRN_CHUNK
    nt = T // th
    wd = 128 * hps

    def body(fl_ref, qh_ref, v_ref, og_ref, lbl_ref, ng_ref, ss_ref, dy_ref,
             dfl_ref, dqh_ref, dv_ref, dog_ref, dlb_ref, dng_ref, dst):
        @pl.when(pl.program_id(1) == 0)
        def _():
            dst[...] = jnp.zeros_like(dst)
            dlb_ref[...] = jnp.zeros_like(dlb_ref)
            dng_ref[...] = jnp.zeros_like(dng_ref)

        lb = _lower_bound(lbl_ref)
        ng = jnp.tile(ng_ref[...], (1, hps))
        last = lax.broadcasted_iota(jnp.int32, (HGRN_CHUNK, wd), 0) == HGRN_CHUNK - 1
        nsub = th // sub_rows
        cat0 = lambda parts: jnp.concatenate(parts, axis=0)
        tril_b, triu_b = _chunk_tri(sub_rows), _chunk_tri(sub_rows, upper=True)

        def sub(step, carry):
            si = nsub - 1 - step
            rows = pl.ds(pl.multiple_of(si * sub_rows, sub_rows), sub_rows)
            qh, og, dy = qh_ref[rows, :], og_ref[rows, :], dy_ref[rows, :]
            q = _hgrn_sub(fl_ref[rows, :], qh, v_ref[rows, :], lb, tril_b)
            s_in = [[ss_ref[h, si * cps + c] for c in range(cps)] for h in range(hps)]
            st_b = [[s.astype(BF16) for s in row] for row in s_in]
            o = _hgrn_out(q, st_b)
            rr = _head_rms(o)
            oh = o * rr
            sog = _sigmoid(og)
            dog_ref[rows, :] = (dy * oh * ng * (sog * (1.0 + og * (1.0 - sog)))).astype(BF16)
            don = dy * (og * sog)
            dng_w = jnp.sum(don * oh, axis=0, keepdims=True)
            dd = don * ng
            mean_h = _per_head(lambda t: jnp.broadcast_to(jnp.mean(t, axis=-1, keepdims=True), t.shape), dd * oh)
            do_b = (rr * (dd - oh * mean_h)).astype(BF16)
            do_h, qd_h, ki_h, ke_h, v_h = (_heads(t) for t in (do_b, q["qd_b"], q["ki_b"], q["ke_b"], q["v_b"]))
            da_b = [jnp.where(tril_b > 0, _dot_nt(do_h[h], v_h[h]), 0.0).astype(BF16) for h in range(hps)]
            do_c, qd_c, ke_c, v_c = ([_chunks(t) for t in hs] for hs in (do_h, qd_h, ke_h, v_h))
            dsp = [[None] * cps for _ in range(hps)]
            dgl_dec = [[None] * cps for _ in range(hps)]
            d = [dst[h] for h in range(hps)]
            for c in reversed(range(cps)):
                egl = _heads(q["egl"][c])
                for h in range(hps):
                    dsp[h][c] = d[h]
                    dgl_dec[h][c] = jnp.sum(d[h] * s_in[h][c], axis=0, keepdims=True) * egl[h]
                    d[h] = d[h] * egl[h] + _dot_tn(do_c[h][c], qd_c[h][c])
            for h in range(hps):
                dst[h] = d[h]
                dng_ref[h] += dng_w[:, 128 * h:128 * (h + 1)]
            dsp_b = [[t.astype(BF16) for t in row] for row in dsp]
            dv_ref[rows, :] = jnp.concatenate(
                [_dot_tn(q["am_b"][h], do_h[h]) + cat0([_dot_nt(ke_c[h][c], dsp_b[h][c]) for c in range(cps)])
                 for h in range(hps)], axis=1).astype(BF16)
            dqd = jnp.concatenate(
                [_dot(da_b[h], ki_h[h]) + cat0([_dot(do_c[h][c], st_b[h][c]) for c in range(cps)])
                 for h in range(hps)], axis=1)
            dki = jnp.concatenate([_dot_tn(da_b[h], qd_h[h]) for h in range(hps)], axis=1)
            dke = jnp.concatenate([cat0([_dot(v_c[h][c], dsp_b[h][c]) for c in range(cps)]) for h in range(hps)], axis=1)
            dkk = dke * q["ke"]
            dgl = [jnp.sum(t, axis=0, keepdims=True) + jnp.concatenate([dgl_dec[h][c] for h in range(hps)], axis=1)
                   for c, t in enumerate(_chunks(dkk))]
            dgc = dqd * q["qd"] - dki * q["ki"] - dkk + cat0([jnp.where(last, g, 0.0) for g in dgl])
            dlf = _chunk_sum(triu_b, dgc)
            df = dlf / q["f"] - (dki * q["eng"] + dke * q["elg"])
            sg = q["sg"]
            dlb_ref[...] += jnp.sum(df * (1.0 - sg), axis=0, keepdims=True)
            dfl_ref[rows, :] = (df * (1.0 - lb) * sg * (1.0 - sg)).astype(BF16)
            sq = q["sq"]
            dqh_ref[rows, :] = (dqd * q["eg"] * (sq * (1.0 + qh * (1.0 - sq)))).astype(BF16)
            return carry

        lax.fori_loop(0, nsub, sub, 0)
    col = lambda j: pl.BlockSpec((th, wd), lambda h, t: (nt - 1 - t, (8 // hps) * j + h))
    out = pl.BlockSpec((th, wd), lambda h, t: (nt - 1 - t, h))
    return _call(
        body, name=name, grid=(HGRN_HEADS // hps, nt),
        in_specs=[col(0), col(1), col(2), col(3), pl.BlockSpec((2, wd), lambda h, t: (0, h)),
                  pl.BlockSpec((1, 128), lambda h, t: (0, 0)),
                  pl.BlockSpec((hps, nc, 128, 128), lambda h, t: (h, nt - 1 - t, 0, 0)), out],
        out_specs=[out, out, out, out, pl.BlockSpec((1, wd), lambda h, t: (0, h)),
                   pl.BlockSpec((hps, 1, 128), lambda h, t: (h, 0, 0))],
        out_shape=[jax.ShapeDtypeStruct((T, D), BF16)] * 4 + [jax.ShapeDtypeStruct((1, D), F32),
                                                              jax.ShapeDtypeStruct((HGRN_HEADS, 1, 128), F32)],
        scratch_shapes=[pltpu.VMEM((hps, 128, 128), F32)],
        ins=[hg, hg, hg, hg, lb_logits, norm_g, sstart, dyh], carry=carry)


def _mix_fwd(name, ya, yh, gates, xh1, g1, b1, wall, tm):
    T = ya.shape[0]

    def body(ya_ref, yh_ref, ga_ref, gh_ref, xh_ref, g_ref, b_ref, wall_ref,
             xo_ref, r_ref, pa_ref, ph_ref, mg_ref, wpa, wph, wo, sem):
        @pl.when(pl.program_id(0) == 0)
        def _():
            _load_rows(wall_ref, [(wpa, RIN, RP), (wph, RIN + RP, RP), (wo, RIN + 2 * RP, RP)], sem)

        parts = [slice(p, min(p + PART_ROWS, tm)) for p in range(0, tm, PART_ROWS)]
        pas = [_dot(ya_ref[rs, :], wpa[...]).astype(BF16) for rs in parts]
        phs = [_dot(yh_ref[rs, :], wph[...]).astype(BF16) for rs in parts]
        mgs = []
        for rs, pa, ph in zip(parts, pas, phs):
            pa_ref[rs, :] = pa
            ph_ref[rs, :] = ph
            sa, sh = _sigmoid(ga_ref[rs, :].astype(F32)), _sigmoid(gh_ref[rs, :].astype(F32))
            mgs.append((sa * pa.astype(F32) + sh * ph.astype(F32)).astype(BF16))
            mg_ref[rs, :] = mgs[-1]
        mixes = [_dot(mg, wo[...]) for mg in mgs]
        for rs, mix in zip(parts, mixes):
            y1 = xh_ref[rs, :] * g_ref[...] + b_ref[...]
            xh2, r = _ln_fwd(ALPHA * y1 + mix)
            xo_ref[rs, :] = xh2
            r_ref[rs, :] = jnp.broadcast_to(r, (xh2.shape[0], 128))

    row = lambda w: pl.BlockSpec((tm, w), lambda i: (i, 0))
    vec = pl.BlockSpec((1, D), lambda i: (0, 0))
    return pl.pallas_call(
        body, name=name, grid=(T // tm,),
        in_specs=[row(D), row(D), pl.BlockSpec((tm, D), lambda i: (i, 0)), pl.BlockSpec((tm, D), lambda i: (i, 1)),
                  row(D), vec, vec, HBM_SPEC],
        out_specs=[row(D), row(128), row(D), row(D), row(D)],
        out_shape=[jax.ShapeDtypeStruct((T, D), F32), jax.ShapeDtypeStruct((T, 128), F32)]
        + [jax.ShapeDtypeStruct((T, D), BF16)] * 3,
        scratch_shapes=[pltpu.VMEM((D, D), BF16)] * 3 + [pltpu.SemaphoreType.DMA((3 * NDEV,))],
        compiler_params=_params("arbitrary"),
    )(ya, yh, gates, gates, xh1, g1, b1, wall)


def _mix_bwd(name, dy2, xh2, r2, g2, gates, pa, ph, wall, tm, carry=None):
    T = dy2.shape[0]

    def body(dy_ref, xh_ref, r_ref, g_ref, ga_ref, gh_ref, pa_ref, ph_ref, wall_ref,
             dz_ref, dmix_ref, dpa_ref, dph_ref, dga_ref, dgh_ref, dya_ref, dyh_ref, dg_ref, db_ref,
             wpa, wph, wo, sem):
        @pl.when(pl.program_id(0) == 0)
        def _():
            _load_rows(wall_ref, [(wpa, RIN, RP), (wph, RIN + RP, RP), (wo, RIN + 2 * RP, RP)], sem)
            dg_ref[...] = jnp.zeros_like(dg_ref)
            db_ref[...] = jnp.zeros_like(db_ref)

        parts = [slice(p, min(p + PART_ROWS, tm)) for p in range(0, tm, PART_ROWS)]
        dmix = []
        for rs in parts:
            dz, dgp, dbp = _ln_bwd(dy_ref[rs, :], xh_ref[rs, :], r_ref[rs, :1], g_ref[...])
            dg_ref[...] += dgp
            db_ref[...] += dbp
            dz_ref[rs, :] = dz
            dmix.append(dz.astype(BF16))
            dmix_ref[rs, :] = dmix[-1]
        dmgs = [_dot_nt(d, wo[...]) for d in dmix]
        dpas, dphs = [], []
        for rs, dmg in zip(parts, dmgs):
            sa, sh = _sigmoid(ga_ref[rs, :].astype(F32)), _sigmoid(gh_ref[rs, :].astype(F32))
            dga_ref[rs, :] = (dmg * pa_ref[rs, :].astype(F32) * sa * (1.0 - sa)).astype(BF16)
            dgh_ref[rs, :] = (dmg * ph_ref[rs, :].astype(F32) * sh * (1.0 - sh)).astype(BF16)
            dpas.append((dmg * sa).astype(BF16))
            dphs.append((dmg * sh).astype(BF16))
            dpa_ref[rs, :] = dpas[-1]
            dph_ref[rs, :] = dphs[-1]
        for rs, dpa, dph in zip(parts, dpas, dphs):
            dya_ref[rs, :] = _dot_nt(dpa, wpa[...]).astype(BF16)
            dyh_ref[rs, :] = _dot_nt(dph, wph[...])

    row = lambda w: pl.BlockSpec((tm, w), lambda i: (i, 0))
    vec = pl.BlockSpec((1, D), lambda i: (0, 0))
    return _call(
        body, name=name, grid=(T // tm,),
        in_specs=[row(D), row(D), row(128), vec, pl.BlockSpec((tm, D), lambda i: (i, 0)),
                  pl.BlockSpec((tm, D), lambda i: (i, 1)), row(D), row(D), HBM_SPEC],
        out_specs=[row(D)] * 8 + [vec, vec],
        out_shape=[jax.ShapeDtypeStruct((T, D), F32)] + [jax.ShapeDtypeStruct((T, D), BF16)] * 6
        + [jax.ShapeDtypeStruct((T, D), F32)] + [jax.ShapeDtypeStruct((1, D), F32)] * 2,
        scratch_shapes=[pltpu.VMEM((D, D), BF16)] * 3 + [pltpu.SemaphoreType.DMA((3 * NDEV,))],
        ins=[dy2, xh2, r2, g2, gates, gates, pa, ph, wall], carry=carry)


def _adam(w, g, m, v):
    m = ADAM_B1 * m + (1.0 - ADAM_B1) * g
    v = ADAM_B2 * v + (1.0 - ADAM_B2) * (g * g)
    m_hat = m / (1.0 - ADAM_B1 ** ADAM_STEP)
    v_hat = v / (1.0 - ADAM_B2 ** ADAM_STEP)
    delta = -ADAM_LR * (m_hat / (jnp.sqrt(v_hat) + ADAM_EPS) + ADAM_WD * w)
    return delta, m, v


def _grad_steps(name, items, cb, carry=None):
    n = len(items)
    pieces = [len(it[0]) for it in items]

    def body(*refs):
        it = iter(refs[:sum(pieces) + 3 * n])
        outs = refs[sum(pieces) + 3 * n:]
        for i in range(n):
            accs = []
            for _ in range(pieces[i]):
                r_ref = next(it)
                acc = r_ref[0].astype(F32)
                for k in range(1, NDEV):
                    acc = acc + r_ref[k].astype(F32)
                accs.append(acc)
            acc = accs[0] if len(accs) == 1 else jnp.concatenate(accs, axis=0)
            w_ref, m_ref, v_ref = next(it), next(it), next(it)
            g_ref, d_ref, mo_ref, vo_ref = outs[4 * i:4 * i + 4]
            g_ref[...] = acc
            d, mm, vv = _adam(w_ref[...], acc, m_ref[...], v_ref[...])
            d_ref[...] = d
            mo_ref[...] = mm
            vo_ref[...] = vv

    in_specs, out_specs, out_shape, ins = [], [], [], []
    for recv, w, m, v in items:
        blk = pl.BlockSpec((w.shape[0], cb), lambda j: (0, j))
        in_specs += [pl.BlockSpec((NDEV, rp.shape[1], cb), lambda j: (0, 0, j)) for rp in recv] + [blk, blk, blk]
        out_specs += [blk] * 4
        out_shape += [jax.ShapeDtypeStruct(w.shape, F32)] * 4
        ins += list(recv) + [w, m, v]
    outs, ex = _call(body, name=name, grid=(D // cb,), in_specs=in_specs, out_specs=out_specs, out_shape=out_shape,
                     scratch_shapes=[], ins=ins, carry=carry)
    return [tuple(outs[4 * i:4 * i + 4]) for i in range(n)], ex


_SMALL = [("ln1_g", D), ("ln1_b", D), ("ln2_g", D), ("ln2_b", D), ("ln3_g", D), ("ln3_b", D),
          ("b_in", DIN), ("lb", D), ("attn_sinks", 128), ("hgrn_norm_g", 128), ("loss", 128)]
_SMALL_OFF = {}
_o = 0
for _n, _w in _SMALL:
    _SMALL_OFF[_n] = (_o, _w)
    _o += _w
PACK = _o


def _small_step(name, slots, small_w, small_m, small_v):
    names = ["ln1_g", "ln1_b", "ln2_g", "ln2_b", "ln3_g", "ln3_b", "b_in", "attn_sinks", "hgrn_lb_logits", "hgrn_norm_g"]
    np_ = len(names)

    def body(*refs):
        s_ref = refs[0]
        w_refs = refs[1:1 + np_]
        m_refs = refs[1 + np_:1 + 2 * np_]
        v_refs = refs[1 + 2 * np_:1 + 3 * np_]
        loss_ref, outs = refs[1 + 3 * np_], refs[2 + 3 * np_:]
        tot = s_ref[0]
        for k in range(1, NDEV):
            tot = tot + s_ref[k]

        def part(n):
            o, w = _SMALL_OFF[n]
            return tot[:, o:o + w]

        loss_ref[...] = part("loss")
        for i, n in enumerate(names):
            w = w_refs[i][...]
            if n == "hgrn_lb_logits":
                m_ = jnp.maximum(w[0:1, :], w[1:2, :])
                e0, e1 = jnp.exp(w[0:1, :] - m_), jnp.exp(w[1:2, :] - m_)
                p0 = e0 / (e0 + e1)
                t = p0 * (1.0 - p0) * part("lb")
                g = jnp.concatenate([t, -t], axis=0)
            elif n == "attn_sinks":
                g = part(n)[:, :N_Q_HEADS]
            else:
                g = part(n)
            d, mm, vv = _adam(w, g, m_refs[i][...], v_refs[i][...])
            outs[4 * i][...] = g
            outs[4 * i + 1][...] = d
            outs[4 * i + 2][...] = mm
            outs[4 * i + 3][...] = vv

    out_shape = [jax.ShapeDtypeStruct((1, 128), F32)]
    for n in names:
        out_shape += [jax.ShapeDtypeStruct(small_w[n].shape, F32)] * 4
    return pl.pallas_call(
        body, name=name, out_shape=out_shape,
        compiler_params=pltpu.CompilerParams(vmem_limit_bytes=VMEM_LIMIT),
    )(slots, *[small_w[n] for n in names], *[small_m[n] for n in names], *[small_v[n] for n in names]), names


def _tile(T, pref):
    return min(T, pref)


def kernel(x, ln1_g, ln1_b, ffn1_w1, ffn1_w3, ffn1_w2, ln2_g, ln2_b, w_in, b_in, attn_sinks, hgrn_lb_logits, hgrn_norm_g, w_proj_attn, w_proj_hgrn, w_out, ln3_g, ln3_b, ffn2_w1, ffn2_w3, ffn2_w2, loss_target, m_ln1_g, m_ln1_b, m_ffn1_w1, m_ffn1_w3, m_ffn1_w2, m_ln2_g, m_ln2_b, m_w_in, m_b_in, m_attn_sinks, m_hgrn_lb_logits, m_hgrn_norm_g, m_w_proj_attn, m_w_proj_hgrn, m_w_out, m_ln3_g, m_ln3_b, m_ffn2_w1, m_ffn2_w3, m_ffn2_w2, v_ln1_g, v_ln1_b, v_ffn1_w1, v_ffn1_w3, v_ffn1_w2, v_ln2_g, v_ln2_b, v_w_in, v_b_in, v_attn_sinks, v_hgrn_lb_logits, v_hgrn_norm_g, v_w_proj_attn, v_w_proj_hgrn, v_w_out, v_ln3_g, v_ln3_b, v_ffn2_w1, v_ffn2_w3, v_ffn2_w2):
    T = x.shape[1]
    xs = x[0]
    tgt = loss_target[0]
    tm = _tile(T, 256)
    tm2 = _tile(T, 512)
    tk = _tile(T, 2048)

    t_bf = lambda w: w[0].T.astype(BF16)
    n_bf = lambda w: w[0].astype(BF16)
    ffn_shard = lambda w1, w3, w2: jnp.concatenate([t_bf(w1), t_bf(w3), n_bf(w2)], axis=0)
    mix_shard = jnp.concatenate([t_bf(w_in), n_bf(w_proj_attn), n_bf(w_proj_hgrn), n_bf(w_out)], axis=0)
    (ffn1_all,) = _exchange_call("gather_ffn1", _relay_gather_exchange(ffn_shard(ffn1_w1, ffn1_w3, ffn1_w2)))
    ffn_offs = (0, RF, 2 * RF)
    rope = _rope_table(T)

    (xh1, r1, a1, b1, xb0), (mix_all,) = _ffn_fwd("ffn1_fwd", xs, None, ffn1_all, ffn_offs, tm2,
                                                  carry=_gather_exchange(mix_shard))
    (qkv, kt, vt, hg, gates, y1b), (ffn2_all,) = _inproj_fwd(
        "inproj_fwd", xh1, ln1_g, ln1_b, mix_all, b_in, rope, tm2,
        carry=_gather_exchange(ffn_shard(ffn2_w1, ffn2_w3, ffn2_w2)))
    ya = _attn_fwd("attn_fwd", qkv, vt, attn_sinks)
    yh, sstart = _hgrn_fwd("hgrn_fwd", hg, hgrn_lb_logits, hgrn_norm_g, HG_FWD)
    xh2, r2, pa, ph, merged = _mix_fwd("mix_fwd", ya, yh, gates, xh1, ln1_g, ln1_b, mix_all, tm2)
    (xh3, r3, a2, b2, y2b, loss_part), _ = _ffn_fwd("ffn2_fwd", xh2, (ln2_g, ln2_b), ffn2_all, ffn_offs, tm2,
                                                    loss=(ln3_g, ln3_b, tgt))

    (dy2, dab2, u2, df2, dg3, db3), _ = _ffn_bwd("ffn2_bwd", ("loss", ln3_b, tgt), xh3, r3, ln3_g, a2, b2, ffn2_all,
                                                 ffn_offs, tm)
    recv = {}
    g_ffn2_13, _ = _wgrad("wgrad_ffn2_w13", dab2, y2b, DFF // 2, tk)
    g_ffn2_2, _ = _wgrad("wgrad_ffn2_w2", u2, df2, DFF // 2, tk)
    (dz2, dmix, dpa, dph, dga, dgh, dya, dyh, dg2, db2), _ = _mix_bwd(
        "mix_bwd", dy2, xh2, r2, ln2_g, gates, pa, ph, mix_all, tm2)
    g_wo, _ = _wgrad("wgrad_w_out", merged, dmix, D, tk)
    g_pa, _ = _wgrad("wgrad_w_pa", ya, dpa, D, tk)
    g_ph, _ = _wgrad("wgrad_w_ph", yh, dph, D, tk)
    (dq, dk, dv, dsink), _ = _attn_bwd("attn_bwd", qkv, kt, attn_sinks, dya)
    (dfl, dqh, dih, dog, dlb, dng), (recv["ffn2_w2"], recv["w_proj_attn"], recv["w_proj_hgrn"], recv["w_out"]) = \
        _hgrn_bwd("hgrn_bwd", hg, hgrn_lb_logits, hgrn_norm_g, sstart, dyh, HG_BWD,
                  carry=_grad_exchange([g_ffn2_2, g_pa, g_ph, g_wo], [0, 0, 0, 0], [RF, RP, RP, RP]))
    (dy1, dproj, dbin), (recv["ffn2_w1"], recv["ffn2_w3"]) = _inproj_bwd(
        "inproj_bwd", dq, dk, dv, (dfl, dqh, dih, dog), (dga, dgh), dz2, mix_all, rope, tm2,
        carry=_grad_exchange([g_ffn2_13, g_ffn2_13], [0, DFF], [RF, RF]))
    g_win, _ = _wgrad("wgrad_w_in", dproj, y1b, DIN // 4, tk)
    win_rows = (400, 288, 272)
    win_base = (0, 400, 688)
    (dz1, dab1, u1, df1, dg1, db1), (rw0,) = _ffn_bwd(
        "ffn1_bwd", ("dy", dy1), xh1, r1, ln1_g, a1, b1, ffn1_all, ffn_offs, tm2, with_dx=False,
        carry=_grad_exchange([g_win], win_base[:1], win_rows[:1], [RIN]))
    g_ffn1_2, (rw1,) = _wgrad("wgrad_ffn1_w2", u1, df1, DFF // 2, tk,
                              carry=_grad_exchange([g_win], win_base[1:2], win_rows[1:2], [RIN]))
    g_ffn1_1, (recv["ffn1_w2"],) = _wgrad("wgrad_ffn1_w1", dab1, xb0, DFF // 2, tk, 0, DFF,
                                          carry=_grad_exchange([g_ffn1_2], [0], [RF]))
    g_ffn1_3, (recv["ffn1_w1"],) = _wgrad("wgrad_ffn1_w3", dab1, xb0, DFF // 2, tk, DFF, DFF,
                                          carry=_grad_exchange([g_ffn1_1], [0], [RF]))
    parts = {"ln1_g": dg1, "ln1_b": db1, "ln2_g": dg2, "ln2_b": db2, "ln3_g": dg3, "ln3_b": db3, "b_in": dbin,
             "lb": dlb, "attn_sinks": dsink, "hgrn_norm_g": jnp.sum(dng, axis=0), "loss": loss_part[0:1, :]}
    packed = jnp.concatenate([parts[n] for n, _ in _SMALL], axis=1)
    gx, (recv["ffn1_w3"], rw2, small_slots) = _ffn_dx(
        "ffn1_dx", dab1, dz1, ffn1_all, ffn_offs, tm2,
        carry=_join(_grad_exchange([g_ffn1_3, g_win], [0, win_base[2]], [RF, win_rows[2]], [RF, RIN]),
                    _row_gather_exchange(packed)))
    recv["w_in"] = [rw0, rw1, rw2]

    big = [("ffn1_w1", ffn1_w1, m_ffn1_w1, v_ffn1_w1, True), ("ffn1_w3", ffn1_w3, m_ffn1_w3, v_ffn1_w3, True),
           ("ffn1_w2", ffn1_w2, m_ffn1_w2, v_ffn1_w2, False), ("w_in", w_in, m_w_in, v_w_in, True),
           ("w_proj_attn", w_proj_attn, m_w_proj_attn, v_w_proj_attn, False),
           ("w_proj_hgrn", w_proj_hgrn, m_w_proj_hgrn, v_w_proj_hgrn, False),
           ("w_out", w_out, m_w_out, v_w_out, False),
           ("ffn2_w1", ffn2_w1, m_ffn2_w1, v_ffn2_w1, True), ("ffn2_w3", ffn2_w3, m_ffn2_w3, v_ffn2_w3, True),
           ("ffn2_w2", ffn2_w2, m_ffn2_w2, v_ffn2_w2, False)]
    view = lambda t, transposed: t[0].T if transposed else t[0]
    back = lambda t, transposed: t.T[None] if transposed else t[None]
    slots = lambda n: recv[n] if isinstance(recv[n], list) else [recv[n]]
    stepped, _ = _grad_steps("steps", [(slots(n), view(w, tr), view(m, tr), view(v, tr)) for n, w, m, v, tr in big], 128)
    res = {n: tuple(back(t, tr) for t in outs) for (n, _, _, _, tr), outs in zip(big, stepped)}

    small_w = dict(ln1_g=ln1_g, ln1_b=ln1_b, ln2_g=ln2_g, ln2_b=ln2_b, ln3_g=ln3_g, ln3_b=ln3_b, b_in=b_in,
                   attn_sinks=attn_sinks, hgrn_lb_logits=hgrn_lb_logits, hgrn_norm_g=hgrn_norm_g)
    small_m = dict(ln1_g=m_ln1_g, ln1_b=m_ln1_b, ln2_g=m_ln2_g, ln2_b=m_ln2_b, ln3_g=m_ln3_g, ln3_b=m_ln3_b,
                   b_in=m_b_in, attn_sinks=m_attn_sinks, hgrn_lb_logits=m_hgrn_lb_logits, hgrn_norm_g=m_hgrn_norm_g)
    small_v = dict(ln1_g=v_ln1_g, ln1_b=v_ln1_b, ln2_g=v_ln2_g, ln2_b=v_ln2_b, ln3_g=v_ln3_g, ln3_b=v_ln3_b,
                   b_in=v_b_in, attn_sinks=v_attn_sinks, hgrn_lb_logits=v_hgrn_lb_logits, hgrn_norm_g=v_hgrn_norm_g)
    outs, names = _small_step("small_step", small_slots, small_w, small_m, small_v)
    loss = outs[0][0, 0]
    for i, n in enumerate(names):
        res[n] = tuple(outs[1 + 4 * i:5 + 4 * i])

    order = ["ln1_g", "ln1_b", "ffn1_w1", "ffn1_w3", "ffn1_w2", "ln2_g", "ln2_b", "w_in", "b_in", "attn_sinks",
             "hgrn_lb_logits", "hgrn_norm_g", "w_proj_attn", "w_proj_hgrn", "w_out", "ln3_g", "ln3_b",
             "ffn2_w1", "ffn2_w3", "ffn2_w2"]
    return (loss, gx[None], *[res[n][0] for n in order], *[res[n][1] for n in order],
            *[res[n][2] for n in order], *[res[n][3] for n in order])
```

```python
import jax
import jax.numpy as jnp
from jax import lax
from jax.experimental import pallas as pl
from jax.experimental.pallas import tpu as pltpu

F32 = jnp.float32
BF16 = jnp.bfloat16

NDEV = 8
D = 1024
DFF = 2816
RF = DFF // NDEV
DIN = 7680
RIN = DIN // NDEV
RP = D // NDEV
N_Q_HEADS = 16
N_KV_HEADS = 4
HEAD_DIM = 64
ATTN_BLOCK = 128
ROPE_THETA = 500000.0
ROPE_DIM = HEAD_DIM // 4
HGRN_HEADS = 8
HGRN_CHUNK = 64
ALPHA = 2.0 ** 0.25
LN_EPS = 1e-5
RMS_EPS = 1e-6
NEG_INF = -1e30
ADAM_LR = 0.001
ADAM_B1 = 0.9
ADAM_B2 = 0.999
ADAM_EPS = 1e-08
ADAM_WD = 0.01
ADAM_STEP = 10

QKV_W = 1536
HG_W = 4096
GATE_W = 2048
VMEM_LIMIT = 60 * 2 ** 20
PART_ROWS = 256
MESH = pl.DeviceIdType.MESH
HBM_SPEC = pl.BlockSpec(memory_space=pltpu.HBM)


def _params(*sem):
    return pltpu.CompilerParams(dimension_semantics=sem, vmem_limit_bytes=VMEM_LIMIT)


def _dot(a, b):
    return jnp.dot(a, b, preferred_element_type=F32)


def _dot_nt(a, b):
    return lax.dot_general(a, b, (((1,), (1,)), ((), ())), preferred_element_type=F32)


def _dot_tn(a, b):
    return lax.dot_general(a, b, (((0,), (0,)), ((), ())), preferred_element_type=F32)


def _sigmoid(x):
    return 0.5 * jnp.tanh(0.5 * x) + 0.5


def _ln_fwd(z):
    mu = jnp.mean(z, axis=-1, keepdims=True)
    zc = z - mu
    var = jnp.mean(zc * zc, axis=-1, keepdims=True)
    r = lax.rsqrt(var + LN_EPS)
    return zc * r, r


def _ln_bwd(dy, xh, r, g):
    dxh = dy * g
    m1 = jnp.mean(dxh, axis=-1, keepdims=True)
    m2 = jnp.mean(dxh * xh, axis=-1, keepdims=True)
    dz = r * (dxh - m1 - xh * m2)
    return dz, jnp.sum(dy * xh, axis=0, keepdims=True), jnp.sum(dy, axis=0, keepdims=True)


def _load_rows(wall_ref, pieces, sem):
    copies = []
    for dst, off, r in pieces:
        for k in range(NDEV):
            c = pltpu.make_async_copy(wall_ref.at[k, pl.ds(off, r), :], dst.at[pl.ds(k * r, r), :], sem.at[len(copies)])
            c.start()
            copies.append(c)
    for c in copies:
        c.wait()


class _Exchange:
    def __init__(self, inputs, out_shape, scratch, begin, middle, end):
        self.inputs, self.out_shape, self.scratch = inputs, out_shape, scratch
        self.begin, self.middle, self.end = begin, middle, end


def _gather_exchange(shard):
    rows, cols = shard.shape

    def ops(ins, outs, scr):
        (x_ref,), (out_ref,), (send_sems, recv_sems, local_sem) = ins, outs, scr
        x, y, c = lax.axis_index("x"), lax.axis_index("y"), lax.axis_index("c")
        me, sibling = (x, y, c), (x, y, 1 - c)
        chips = [(1 - x, y), (x, 1 - y), (1 - x, 1 - y)]

        def slot(px, py, pc):
            return out_ref.at[4 * px + 2 * py + pc]

        def copy(k, block, to, src=None):
            return pltpu.make_async_remote_copy(
                src_ref=slot(*block) if src is None else src, dst_ref=slot(*block),
                send_sem=send_sems.at[k], recv_sem=recv_sems.at[k], device_id=to, device_id_type=MESH)

        mine = lambda: pltpu.make_async_copy(x_ref, slot(*me), local_sem)
        first = lambda: [copy(0, me, sibling, src=x_ref)] + [
            copy(1 + j, me, (*chip, c), src=x_ref) for j, chip in enumerate(chips)]
        passed = lambda: [copy(4 + j, (*chip, c), sibling) for j, chip in enumerate(chips)]
        return c, me, sibling, chips, copy, mine, first, passed

    def begin(*refs):
        _, _, _, _, _, mine, first, _ = ops(*refs)
        mine().start()
        for cp in first():
            cp.start()

    def middle(*refs):
        c, me, _, chips, copy, _, _, passed = ops(*refs)
        for (j, chip), fwd in zip(enumerate(chips), passed()):
            copy(1 + j, (*chip, c), me).wait_recv()
            fwd.start()

    def end(*refs):
        c, me, sibling, chips, copy, mine, first, passed = ops(*refs)
        copy(0, sibling, me).wait_recv()
        for j, chip in enumerate(chips):
            copy(4 + j, (*chip, 1 - c), me).wait_recv()
        for cp in first() + passed():
            cp.wait_send()
        mine().wait()

    return _Exchange([shard], [jax.ShapeDtypeStruct((NDEV, rows, cols), shard.dtype)],
                     [pltpu.SemaphoreType.DMA((7,)), pltpu.SemaphoreType.DMA((7,)), pltpu.SemaphoreType.DMA],
                     begin, middle, end)


def _relay_gather_exchange(shard):
    rows, cols = shard.shape

    def ops(ins, outs, scr):
        (x_ref,), (out_ref,), (send_sems, recv_sems, local_sem) = ins, outs, scr
        x, y, c = lax.axis_index("x"), lax.axis_index("y"), lax.axis_index("c")
        me, sibling = (x, y, c), (x, y, 1 - c)
        xchip, ychip, diag = (1 - x, y), (x, 1 - y), (1 - x, 1 - y)
        relay_from = (x ^ (1 - c), y ^ c)
        relay_to = (x ^ c, y ^ (1 - c))

        def slot(px, py, pc):
            return out_ref.at[4 * px + 2 * py + pc]

        def copy(k, block, to, src=None):
            return pltpu.make_async_remote_copy(
                src_ref=slot(*block) if src is None else src, dst_ref=slot(*block),
                send_sem=send_sems.at[k], recv_sem=recv_sems.at[k], device_id=to, device_id_type=MESH)

        mine = lambda: pltpu.make_async_copy(x_ref, slot(*me), local_sem)
        first = lambda: [copy(0, me, sibling, src=x_ref), copy(1, me, (*xchip, c), src=x_ref),
                         copy(2, me, (*ychip, c), src=x_ref)]
        relay = lambda: copy(3, (*relay_from, c), (*relay_to, c))
        chips = (xchip, ychip, diag)
        arrival = lambda j: copy(1 + j, (*chips[j], c), me)
        passed = lambda j: copy(4 + j, (*chips[j], c), sibling)
        from_sibling = lambda: [copy(0, sibling, me)] + [copy(4 + j, (*chips[j], 1 - c), me) for j in range(3)]
        return mine, first, relay, arrival, passed, from_sibling

    def begin(*refs):
        mine, first, _, _, _, _ = ops(*refs)
        mine().start()
        for cp in first():
            cp.start()

    def middle(*refs):
        _, _, relay, arrival, passed, _ = ops(*refs)
        for j in range(2):
            arrival(j).wait_recv()
        relay().start()
        for j in range(2):
            passed(j).start()

    def end(*refs):
        mine, first, relay, arrival, passed, from_sibling = ops(*refs)
        arrival(2).wait_recv()
        passed(2).start()
        for cp in from_sibling():
            cp.wait_recv()
        for cp in first() + [relay()] + [passed(j) for j in range(3)]:
            cp.wait_send()
        mine().wait()

    return _Exchange([shard], [jax.ShapeDtypeStruct((NDEV, rows, cols), shard.dtype)],
                     [pltpu.SemaphoreType.DMA((7,)), pltpu.SemaphoreType.DMA((7,)), pltpu.SemaphoreType.DMA],
                     begin, middle, end)


def _grad_exchange(grads, bases, rows, strides=None):
    n = len(grads)
    strides = rows if strides is None else strides

    def copies(g_refs, out_refs, scr):
        send_sems, recv_sems, local_sems = scr
        x, y, c = lax.axis_index("x"), lax.axis_index("y"), lax.axis_index("c")
        me = 4 * x + 2 * y + c
        out = []
        for i in range(n):
            r = rows[i]
            src = lambda k: g_refs[i].at[pl.ds(pl.multiple_of(bases[i] + k * strides[i], 16), r), :]
            out.append(pltpu.make_async_copy(src(me), out_refs[i].at[me], local_sems.at[i]))
            for j in range(1, NDEV):
                px, py, pc = x ^ (j >> 2), y ^ ((j >> 1) & 1), c ^ (j & 1)
                out.append(pltpu.make_async_remote_copy(
                    src_ref=src(4 * px + 2 * py + pc), dst_ref=out_refs[i].at[me],
                    send_sem=send_sems.at[i, j - 1], recv_sem=recv_sems.at[i, j - 1],
                    device_id=(px, py, pc), device_id_type=MESH))
        return out

    def begin(*refs):
        for cp in copies(*refs):
            cp.start()

    def end(*refs):
        for cp in copies(*refs):
            cp.wait()

    return _Exchange(list(grads), [jax.ShapeDtypeStruct((NDEV, r, g.shape[1]), g.dtype) for g, r in zip(grads, rows)],
                     [pltpu.SemaphoreType.DMA((n, NDEV - 1)), pltpu.SemaphoreType.DMA((n, NDEV - 1)),
                      pltpu.SemaphoreType.DMA((n,))], begin, None, end)


def _row_gather_exchange(row):
    def copies(ins, outs, scr):
        (r_ref,), (o_ref,), (send_sems, recv_sems, local_sem) = ins, outs, scr
        x, y, c = lax.axis_index("x"), lax.axis_index("y"), lax.axis_index("c")
        me = 4 * x + 2 * y + c
        out = [pltpu.make_async_copy(r_ref, o_ref.at[me], local_sem)]
        for j in range(1, NDEV):
            out.append(pltpu.make_async_remote_copy(
                src_ref=r_ref, dst_ref=o_ref.at[me], send_sem=send_sems.at[j - 1], recv_sem=recv_sems.at[j - 1],
                device_id=(x ^ (j >> 2), y ^ ((j >> 1) & 1), c ^ (j & 1)), device_id_type=MESH))
        return out

    def begin(*refs):
        for cp in copies(*refs):
            cp.start()

    def end(*refs):
        for cp in copies(*refs):
            cp.wait()

    return _Exchange([row], [jax.ShapeDtypeStruct((NDEV,) + row.shape, row.dtype)],
                     [pltpu.SemaphoreType.DMA((NDEV - 1,)), pltpu.SemaphoreType.DMA((NDEV - 1,)), pltpu.SemaphoreType.DMA],
                     begin, None, end)


def _join(a, b):
    na, oa, sa = len(a.inputs), len(a.out_shape), len(a.scratch)
    split = lambda ins, outs, scr: ((ins[:na], outs[:oa], scr[:sa]), (ins[na:], outs[oa:], scr[sa:]))

    def begin(*refs):
        pa, pb = split(*refs)
        a.begin(*pa)
        b.begin(*pb)

    def end(*refs):
        pa, pb = split(*refs)
        a.end(*pa)
        b.end(*pb)

    return _Exchange(a.inputs + b.inputs, a.out_shape + b.out_shape, a.scratch + b.scratch, begin, None, end)


def _exchange_call(name, ex):
    ni, no = len(ex.inputs), len(ex.out_shape)

    def body(*refs):
        parts = (refs[:ni], refs[ni:ni + no], refs[ni + no:])
        ex.begin(*parts)
        if ex.middle is not None:
            ex.middle(*parts)
        ex.end(*parts)

    return pl.pallas_call(body, name=name, out_shape=ex.out_shape, in_specs=[HBM_SPEC] * ni, out_specs=[HBM_SPEC] * no,
                          scratch_shapes=ex.scratch)(*ex.inputs)


def _call(body, *, name, grid, in_specs, out_specs, out_shape, scratch_shapes, ins, carry=None):
    sem = ("arbitrary",) * len(grid)
    if carry is None:
        outs = pl.pallas_call(body, name=name, grid=grid, in_specs=in_specs, out_specs=out_specs, out_shape=out_shape,
                              scratch_shapes=scratch_shapes, compiler_params=_params(*sem))(*ins)
        return outs, None
    n_in, n_out, n_scr = len(ins), len(out_shape), len(scratch_shapes)
    ci, co = len(carry.inputs), len(carry.out_shape)
    total = 1
    for g in grid:
        total *= g

    def wrapped(*refs):
        own_in, ex_in = refs[:n_in], refs[n_in:n_in + ci]
        o0 = n_in + ci
        own_out, ex_out = refs[o0:o0 + n_out], refs[o0 + n_out:o0 + n_out + co]
        s0 = o0 + n_out + co
        own_scr, ex_scr = refs[s0:s0 + n_scr], refs[s0 + n_scr:]
        step = pl.program_id(0)
        for d in range(1, len(grid)):
            step = step * grid[d] + pl.program_id(d)
        parts = (ex_in, ex_out, ex_scr)
        pl.when(step == 0)(lambda: carry.begin(*parts))
        body(*own_in, *own_out, *own_scr)
        if carry.middle is not None:
            pl.when(step == (3 * total) // 4)(lambda: carry.middle(*parts))
        pl.when(step == total - 1)(lambda: carry.end(*parts))

    outs = pl.pallas_call(
        wrapped, name=name, grid=grid, in_specs=list(in_specs) + [HBM_SPEC] * ci,
        out_specs=list(out_specs) + [HBM_SPEC] * co, out_shape=list(out_shape) + list(carry.out_shape),
        scratch_shapes=list(scratch_shapes) + list(carry.scratch), compiler_params=_params(*sem),
    )(*ins, *carry.inputs)
    return outs[:n_out], outs[n_out:]


def _ffn_fwd(name, xin, affine, wall, offs, tm, loss=None, carry=None):
    T = xin.shape[0]
    nt = T // tm

    def body(*refs):
        it = iter(refs)
        x_ref = next(it)
        if affine is not None:
            g_ref, b_ref = next(it), next(it)
        wall_ref = next(it)
        if loss is not None:
            go_ref, bo_ref, tgt_ref = next(it), next(it), next(it)
        xh_ref, r_ref, a_ref, b2_ref, yb_ref = (next(it) for _ in range(5))
        if loss is not None:
            loss_ref = next(it)
        w1, w3, w2, sem = (next(it) for _ in range(4))

        @pl.when(pl.program_id(0) == 0)
        def _():
            _load_rows(wall_ref, [(w1, offs[0], RF), (w3, offs[1], RF), (w2, offs[2], RF)], sem)
            if loss is not None:
                loss_ref[...] = jnp.zeros_like(loss_ref)

        parts = [slice(p, min(p + PART_ROWS, tm)) for p in range(0, tm, PART_ROWS)]
        ys, ybs = [], []
        for rs in parts:
            y = x_ref[rs, :]
            if affine is not None:
                y = y * g_ref[...] + b_ref[...]
            ys.append(y)
            ybs.append(y.astype(BF16))
            yb_ref[rs, :] = ybs[-1]
        ab = [(_dot_nt(yb, w1[...]).astype(BF16), _dot_nt(yb, w3[...]).astype(BF16)) for yb in ybs]
        us = []
        for rs, (a, b) in zip(parts, ab):
            a_ref[rs, :] = a
            b2_ref[rs, :] = b
            af, bf = a.astype(F32), b.astype(F32)
            us.append((af * _sigmoid(af) * bf).astype(BF16))
        fs = [_dot(u, w2[...]) for u in us]
        for rs, y, f in zip(parts, ys, fs):
            xh, r = _ln_fwd(ALPHA * y + 0.5 * f)
            xh_ref[rs, :] = xh
            r_ref[rs, :] = jnp.broadcast_to(r, (xh.shape[0], 128))
            if loss is not None:
                e = xh * go_ref[...] + bo_ref[...] - tgt_ref[rs, :]
                loss_ref[...] += jnp.sum(e * e) * (0.5 / D)

    row = lambda w: pl.BlockSpec((tm, w), lambda i: (i, 0))
    vec = pl.BlockSpec((1, D), lambda i: (0, 0))
    ins, in_specs = [xin], [row(D)]
    if affine is not None:
        ins += list(affine)
        in_specs += [vec, vec]
    ins.append(wall)
    in_specs.append(HBM_SPEC)
    if loss is not None:
        ins += list(loss)
        in_specs += [vec, vec, row(D)]
    out_shape = [jax.ShapeDtypeStruct((T, D), F32), jax.ShapeDtypeStruct((T, 128), F32),
                 jax.ShapeDtypeStruct((T, DFF), BF16), jax.ShapeDtypeStruct((T, DFF), BF16),
                 jax.ShapeDtypeStruct((T, D), BF16)]
    out_specs = [row(D), row(128), row(DFF), row(DFF), row(D)]
    if loss is not None:
        out_shape.append(jax.ShapeDtypeStruct((8, 128), F32))
        out_specs.append(pl.BlockSpec((8, 128), lambda i: (0, 0)))
    return _call(body, name=name, grid=(nt,), in_specs=in_specs, out_specs=out_specs, out_shape=out_shape,
                 scratch_shapes=[pltpu.VMEM((DFF, D), BF16)] * 3 + [pltpu.SemaphoreType.DMA((3 * NDEV,))],
                 ins=ins, carry=carry)


def _ffn_bwd(name, dy_src, xh, r, g, a, b, wall, offs, tm, with_dx=True, carry=None):
    T = xh.shape[0]
    nt = T // tm
    from_loss = dy_src[0] == "loss"

    def body(*refs):
        it = iter(refs)
        if from_loss:
            bo_ref, tgt_ref = next(it), next(it)
        else:
            dy_ref = next(it)
        xh_ref, r_ref, g_ref, a_ref, b_ref, wall_ref = (next(it) for _ in range(6))
        dyin_ref, dab_ref, u_ref, df_ref, dg_ref, db_ref = (next(it) for _ in range(6))
        if with_dx:
            w1, w3 = next(it), next(it)
        w2, sem = next(it), next(it)

        @pl.when(pl.program_id(0) == 0)
        def _():
            _load_rows(wall_ref, ([(w1, offs[0], RF), (w3, offs[1], RF)] if with_dx else []) + [(w2, offs[2], RF)], sem)
            dg_ref[...] = jnp.zeros_like(dg_ref)
            db_ref[...] = jnp.zeros_like(db_ref)

        parts = [slice(p, min(p + PART_ROWS, tm)) for p in range(0, tm, PART_ROWS)]
        gv = g_ref[...]
        dz, df = [], []
        for rs in parts:
            xhv = xh_ref[rs, :]
            if from_loss:
                dy = (xhv * gv + bo_ref[...] - tgt_ref[rs, :]) * (1.0 / D)
            else:
                dy = dy_ref[rs, :]
            dzp, dgp, dbp = _ln_bwd(dy, xhv, r_ref[rs, :1], gv)
            dg_ref[...] += dgp
            db_ref[...] += dbp
            dz.append(dzp)
            df.append((0.5 * dzp).astype(BF16))
            df_ref[rs, :] = df[-1]
        du = [_dot_nt(d, w2[...]) for d in df]
        da, dbb = [], []
        for rs, dup in zip(parts, du):
            af, bf = a_ref[rs, :].astype(F32), b_ref[rs, :].astype(F32)
            s = _sigmoid(af)
            sl = af * s
            u_ref[rs, :] = (sl * bf).astype(BF16)
            da.append((dup * bf * (s * (1.0 + af * (1.0 - s)))).astype(BF16))
            dbb.append((dup * sl).astype(BF16))
            dab_ref[rs, :DFF] = da[-1]
            dab_ref[rs, DFF:] = dbb[-1]
        for rs, dzp, dap, dbp in zip(parts, dz, da, dbb):
            dyin_ref[rs, :] = ALPHA * dzp + _dot(dap, w1[...]) + _dot(dbp, w3[...]) if with_dx else dzp

    row = lambda w: pl.BlockSpec((tm, w), lambda i: (i, 0))
    vec = pl.BlockSpec((1, D), lambda i: (0, 0))
    if from_loss:
        ins, in_specs = [dy_src[1], dy_src[2]], [vec, row(D)]
    else:
        ins, in_specs = [dy_src[1]], [row(D)]
    ins += [xh, r, g, a, b, wall]
    in_specs += [row(D), row(128), vec, row(DFF), row(DFF), HBM_SPEC]
    return _call(
        body, name=name, grid=(nt,), in_specs=in_specs,
        out_specs=[row(D), row(2 * DFF), row(DFF), row(D), vec, vec],
        out_shape=[jax.ShapeDtypeStruct((T, D), F32), jax.ShapeDtypeStruct((T, 2 * DFF), BF16),
                   jax.ShapeDtypeStruct((T, DFF), BF16), jax.ShapeDtypeStruct((T, D), BF16),
                   jax.ShapeDtypeStruct((1, D), F32), jax.ShapeDtypeStruct((1, D), F32)],
        scratch_shapes=[pltpu.VMEM((DFF, D), BF16)] * (3 if with_dx else 1) + [pltpu.SemaphoreType.DMA((3 * NDEV,))],
        ins=ins, carry=carry)


def _ffn_dx(name, dab, dz, wall, offs, tm, carry=None):
    T = dz.shape[0]

    def body(dab_ref, dz_ref, wall_ref, o_ref, w13, sem):
        @pl.when(pl.program_id(0) == 0)
        def _():
            _load_rows(wall_ref, [(w13.at[pl.ds(0, DFF), :], offs[0], RF), (w13.at[pl.ds(DFF, DFF), :], offs[1], RF)], sem)

        for p in range(0, tm, PART_ROWS):
            rs = slice(p, min(p + PART_ROWS, tm))
            o_ref[rs, :] = ALPHA * dz_ref[rs, :] + _dot(dab_ref[rs, :], w13[...])

    row = lambda w: pl.BlockSpec((tm, w), lambda i: (i, 0))
    (out,), ex = _call(
        body, name=name, grid=(T // tm,), in_specs=[row(2 * DFF), row(D), HBM_SPEC], out_specs=[row(D)],
        out_shape=[jax.ShapeDtypeStruct((T, D), F32)],
        scratch_shapes=[pltpu.VMEM((2 * DFF, D), BF16), pltpu.SemaphoreType.DMA((2 * NDEV,))],
        ins=[dab, dz, wall], carry=carry)
    return out, ex


def _wgrad(name, a, b, bn, tk, col0=0, ncols=None, carry=None):
    T = a.shape[0]
    N = a.shape[1] if ncols is None else ncols
    nk = T // tk
    c0 = col0 // bn

    def body(a_ref, b_ref, o_ref, acc):
        k = pl.program_id(1)

        @pl.when(k == 0)
        def _():
            acc[...] = jnp.zeros_like(acc)

        acc[...] += _dot_tn(a_ref[...], b_ref[...])

        @pl.when(k == nk - 1)
        def _():
            o_ref[...] = acc[...].astype(BF16)

    (out,), ex = _call(
        body, name=name, grid=(N // bn, nk),
        in_specs=[pl.BlockSpec((tk, bn), lambda n, k: (k, n + c0)), pl.BlockSpec((tk, D), lambda n, k: (k, 0))],
        out_specs=[pl.BlockSpec((bn, D), lambda n, k: (n, 0))],
        out_shape=[jax.ShapeDtypeStruct((N, D), BF16)],
        scratch_shapes=[pltpu.VMEM((bn, D), F32)], ins=[a, b], carry=carry)
    return out, ex


def _rope_table(T):
    pos = jnp.arange(T, dtype=F32)
    inv_freq = ROPE_THETA ** (-jnp.arange(0, ROPE_DIM, 2, dtype=F32) / ROPE_DIM)
    ang = pos[:, None] * inv_freq[None, :]
    return jnp.pad(jnp.concatenate([jnp.cos(ang), jnp.sin(ang)], axis=1), ((0, 0), (0, 128 - ROPE_DIM)))


def _rope_expand(cs):
    half = ROPE_DIM // 2
    lane = lax.broadcasted_iota(jnp.int32, cs.shape, 1)
    cos = jnp.where(lane < half, cs, 0.0)
    sin = jnp.where((lane >= half) & (lane < ROPE_DIM), cs, 0.0)
    both = lambda t: t + pltpu.roll(t, HEAD_DIM, 1)
    c = jnp.where(jnp.bitwise_and(lane, HEAD_DIM - 1) < ROPE_DIM, both(cos + pltpu.roll(cos, half, 1)), 1.0)
    return c, -both(pltpu.roll(sin, 128 - half, 1)), both(sin)


def _rope(t, c, s1, s2):
    n = t.shape[1] // 128
    ct, s1t, s2t = (jnp.tile(v, (1, n)) for v in (c, s1, s2))
    w = t.shape[1]
    return t * ct + pltpu.roll(t, w - 8, 1) * s1t + pltpu.roll(t, 8, 1) * s2t


def _rope_t(dr, c, s1, s2):
    n = dr.shape[1] // 128
    ct, s1t, s2t = (jnp.tile(v, (1, n)) for v in (c, s1, s2))
    w = dr.shape[1]
    return dr * ct + pltpu.roll(dr * s1t, 8, 1) + pltpu.roll(dr * s2t, w - 8, 1)


_Q, _K, _V = (0, 1024), (1024, 256), (1280, 256)
_HG = (1536, HG_W)
_GATES = (5632, GATE_W)


def _inproj_fwd(name, xh, g, b, wall, b_in, rope, tm, carry=None):
    T = xh.shape[0]

    def body(xh_ref, g_ref, b_ref, wall_ref, bin_ref, cs_ref,
             qkv_ref, kt_ref, vt_ref, hg_ref, gate_ref, yb_ref, w, sem):
        @pl.when(pl.program_id(0) == 0)
        def _():
            _load_rows(wall_ref, [(w, 0, RIN)], sem)

        yb = (xh_ref[...] * g_ref[...] + b_ref[...]).astype(BF16)
        yb_ref[...] = yb
        c, s1, s2 = _rope_expand(cs_ref[...])

        def piece(start, width):
            return _dot_nt(yb, w[start:start + width, :]) + bin_ref[:, start:start + width]

        k = _rope(piece(*_K), c, s1, s2)
        v = piece(*_V)
        qkv_ref[:, 0:1024] = _rope(piece(*_Q), c, s1, s2).astype(BF16)
        qkv_ref[:, 1024:1280] = k.astype(BF16)
        qkv_ref[:, 1280:1536] = v.astype(BF16)
        kt_ref[...] = k.T.astype(BF16)
        vt_ref[...] = v.T.astype(BF16)
        for j in range(4):
            hg_ref[:, 1024 * j:1024 * (j + 1)] = piece(_HG[0] + 1024 * j, 1024)
        for j in range(2):
            gate_ref[:, 1024 * j:1024 * (j + 1)] = piece(_GATES[0] + 1024 * j, 1024).astype(BF16)

    row = lambda wd: pl.BlockSpec((tm, wd), lambda i: (i, 0))
    vec = lambda wd: pl.BlockSpec((1, wd), lambda i: (0, 0))
    colt = pl.BlockSpec((256, tm), lambda i: (0, i))
    return _call(
        body, name=name, grid=(T // tm,),
        in_specs=[row(D), vec(D), vec(D), HBM_SPEC, vec(DIN), row(128)],
        out_specs=[row(QKV_W), colt, colt, row(HG_W), row(GATE_W), row(D)],
        out_shape=[jax.ShapeDtypeStruct((T, QKV_W), BF16), jax.ShapeDtypeStruct((256, T), BF16),
                   jax.ShapeDtypeStruct((256, T), BF16), jax.ShapeDtypeStruct((T, HG_W), F32),
                   jax.ShapeDtypeStruct((T, GATE_W), BF16), jax.ShapeDtypeStruct((T, D), BF16)],
        scratch_shapes=[pltpu.VMEM((DIN, D), BF16), pltpu.SemaphoreType.DMA((NDEV,))],
        ins=[xh, g, b, wall, b_in, rope], carry=carry)


def _inproj_bwd(name, dq, dk, dv, dhg, dgates, dz2, wall, rope, tm, carry=None):
    T = dq.shape[0]

    def body(dq_ref, dk_ref, dv_ref, d0, d1, d2, d3, dga_ref, dgh_ref, dz_ref, wall_ref, cs_ref,
             dy_ref, dproj_ref, dbin_ref, w, sem):
        @pl.when(pl.program_id(0) == 0)
        def _():
            _load_rows(wall_ref, [(w, 0, RIN)], sem)
            dbin_ref[...] = jnp.zeros_like(dbin_ref)

        c, s1, s2 = _rope_expand(cs_ref[...])
        acc = ALPHA * dz_ref[...]
        pieces = [(_Q[0], _rope_t(dq_ref[...], c, s1, s2)), (_K[0], _rope_t(dk_ref[...], c, s1, s2)),
                  (_V[0], dv_ref[...])]
        pieces += [(_HG[0] + 1024 * j, r[...]) for j, r in enumerate((d0, d1, d2, d3))]
        pieces += [(_GATES[0], dga_ref[...]), (_GATES[0] + 1024, dgh_ref[...])]
        for start, val in pieces:
            width = val.shape[1]
            dbin_ref[:, start:start + width] += jnp.sum(val.astype(F32), axis=0, keepdims=True)
            vb = val.astype(BF16)
            dproj_ref[:, start:start + width] = vb
            acc = acc + _dot(vb, w[start:start + width, :])
        dy_ref[...] = acc

    row = lambda wd: pl.BlockSpec((tm, wd), lambda i: (i, 0))
    return _call(
        body, name=name, grid=(T // tm,),
        in_specs=[row(D), row(256), row(256)] + [row(D)] * 4 + [row(D), row(D), row(D), HBM_SPEC, row(128)],
        out_specs=[row(D), row(DIN), pl.BlockSpec((1, DIN), lambda i: (0, 0))],
        out_shape=[jax.ShapeDtypeStruct((T, D), F32), jax.ShapeDtypeStruct((T, DIN), BF16),
                   jax.ShapeDtypeStruct((1, DIN), F32)],
        scratch_shapes=[pltpu.VMEM((DIN, D), BF16), pltpu.SemaphoreType.DMA((NDEV,))],
        ins=[dq, dk, dv, *dhg, *dgates, dz2, wall, rope], carry=carry)


def _halves(t):
    lane = lax.broadcasted_iota(jnp.int32, t.shape, 1)
    low = lane < HEAD_DIM
    sw = pltpu.roll(t, HEAD_DIM, 1)
    zero = jnp.zeros_like(t)
    h0 = (jnp.where(low, t, zero), jnp.where(low, zero, sw))
    h1 = (jnp.where(low, sw, zero), jnp.where(low, zero, t))
    return h0, h1


def _lane_stack(p_ref, c_ref, gp):
    sl = slice(128 * gp, 128 * (gp + 1))
    hp, hc = _halves(p_ref[:, sl].astype(F32)), _halves(c_ref[:, sl].astype(F32))
    return [jnp.concatenate([hp[gg][0], hc[gg][0], hp[gg][1], hc[gg][1]], axis=0).astype(BF16) for gg in range(2)]


def _row_stack(tp_ref, tc_ref, g):
    band = jnp.concatenate([tp_ref[64 * g:64 * (g + 1), :], tc_ref[64 * g:64 * (g + 1), :]], axis=1)
    z = jnp.zeros_like(band)
    return [jnp.concatenate([band, z], axis=0), jnp.concatenate([z, band], axis=0)]


def _fold(prevm, t4, hh):
    return jnp.where(prevm, t4[256 * hh:256 * hh + 128, :], t4[256 * hh + 128:256 * hh + 256, :])


def _unfold(prevm, t):
    return jnp.concatenate([jnp.where(prevm, t, 0.0), jnp.where(prevm, 0.0, t)], axis=0).astype(BF16)


def _attn_softmax(s, kill, sink):
    s = jnp.where(kill, NEG_INF, s)
    m = jnp.maximum(jnp.max(s, axis=0, keepdims=True), sink)
    p = jnp.exp(s - m)
    es = jnp.exp(sink - m)
    inv = 1.0 / (jnp.sum(p, axis=0, keepdims=True) + es)
    return p * inv, es * inv


def _attn_masks(first):
    row = lax.broadcasted_iota(jnp.int32, (ATTN_BLOCK, 2 * ATTN_BLOCK), 0)
    lane = lax.broadcasted_iota(jnp.int32, (ATTN_BLOCK, 2 * ATTN_BLOCK), 1)
    prevm = row > lane % ATTN_BLOCK
    return prevm, jnp.logical_and(first, prevm)


def _pair_rows(ref, g):
    return jnp.concatenate([ref[:, 256 * g:256 * g + 128], ref[:, 256 * g + 128:256 * (g + 1)]], axis=0)


def _pair_sinks(sink_ref, g, hh):
    h0, h1 = 4 * g + hh, 4 * g + 2 + hh
    return jnp.concatenate([jnp.broadcast_to(sink_ref[:, h0:h0 + 1], (1, ATTN_BLOCK)),
                            jnp.broadcast_to(sink_ref[:, h1:h1 + 1], (1, ATTN_BLOCK))], axis=1)


def _attn_fwd(name, qkv, vt, sinks):
    T = qkv.shape[0]
    nbk = min(ATTN_TILE, T // ATTN_BLOCK)
    rows = nbk * ATTN_BLOCK
    scale = HEAD_DIM ** -0.5

    def block(q_ref, kp_ref, kc_ref, vtp_ref, vtc_ref, sink_ref, o_ref, first):
        prevm, kill = _attn_masks(first)
        kst = _lane_stack(kp_ref, kc_ref, 0) + _lane_stack(kp_ref, kc_ref, 1)
        vts = [_row_stack(vtp_ref, vtc_ref, g) for g in range(N_KV_HEADS)]
        s8 = [_dot_nt(kst[g], _pair_rows(q_ref, g)) * scale for g in range(N_KV_HEADS)]
        pu = [[_unfold(prevm, _attn_softmax(_fold(prevm, s8[g], hh), kill, _pair_sinks(sink_ref, g, hh))[0])
               for hh in range(2)] for g in range(N_KV_HEADS)]
        for g in range(N_KV_HEADS):
            ot = _dot(vts[g][0], pu[g][0]) + _dot(vts[g][1], pu[g][1])
            o_ref[:, 256 * g:256 * g + 128] = ot[:, :ATTN_BLOCK].T.astype(BF16)
            o_ref[:, 256 * g + 128:256 * (g + 1)] = ot[:, ATTN_BLOCK:].T.astype(BF16)

    def body(q_ref, kp_ref, kc_ref, vtp_ref, vtc_ref, sink_ref, o_ref):
        for b in range(nbk):
            rs = slice(ATTN_BLOCK * b, ATTN_BLOCK * (b + 1))
            ps = slice(ATTN_BLOCK * (b - 1), ATTN_BLOCK * b)
            block(q_ref.at[rs, :], kp_ref if b == 0 else kc_ref.at[ps, :], kc_ref.at[rs, :],
                  vtp_ref if b == 0 else vtc_ref.at[:, ps], vtc_ref.at[:, rs], sink_ref, o_ref.at[rs, :],
                  pl.program_id(0) == 0 if b == 0 else False)

    prev = lambda i: jnp.maximum(nbk * i - 1, 0)
    return pl.pallas_call(
        body, name=name, grid=(T // rows,),
        in_specs=[pl.BlockSpec((rows, D), lambda i: (i, 0)),
                  pl.BlockSpec((ATTN_BLOCK, 256), lambda i: (prev(i), 4)),
                  pl.BlockSpec((rows, 256), lambda i: (i, 4)),
                  pl.BlockSpec((256, ATTN_BLOCK), lambda i: (0, prev(i))),
                  pl.BlockSpec((256, rows), lambda i: (0, i)),
                  pl.BlockSpec((1, N_Q_HEADS), lambda i: (0, 0))],
        out_specs=pl.BlockSpec((rows, D), lambda i: (i, 0)),
        out_shape=jax.ShapeDtypeStruct((T, D), BF16),
        compiler_params=_params("arbitrary"),
    )(qkv, qkv, qkv, vt, vt, sinks)


ATTN_TILE = 4
ATTN_BWD_ORDER = ((0, 0), (0, 1), (1, 0), (0, 2), (1, 1), (2, 0), (0, 3), (1, 2), (2, 1), (1, 3), (2, 2), (2, 3))


def _attn_bwd(name, qkv, kt, sinks, dya, carry=None):
    T = qkv.shape[0]
    nbk = min(ATTN_TILE, T // ATTN_BLOCK)
    rows = nbk * ATTN_BLOCK
    scale = HEAD_DIM ** -0.5

    def block(q_ref, kp_ref, kc_ref, vp_ref, vc_ref, ktp_ref, ktc_ref, sink_ref, do_ref, dq_ref, first):
        prevm, kill = _attn_masks(first)
        kst = _lane_stack(kp_ref, kc_ref, 0) + _lane_stack(kp_ref, kc_ref, 1)
        vst = _lane_stack(vp_ref, vc_ref, 0) + _lane_stack(vp_ref, vc_ref, 1)
        kts = [_row_stack(ktp_ref, ktc_ref, g) for g in range(N_KV_HEADS)]
        lane = lax.broadcasted_iota(jnp.int32, (2 * ATTN_BLOCK, 128), 1)
        low = lane < HEAD_DIM
        slane = lax.broadcasted_iota(jnp.int32, (1, 128), 1)
        dkz = [jnp.zeros((2 * ATTN_BLOCK, 128), F32) for _ in range(N_KV_HEADS)]
        dvz = [jnp.zeros((2 * ATTN_BLOCK, 128), F32) for _ in range(N_KV_HEADS)]
        dsink = jnp.zeros((1, 128), F32)
        groups = range(N_KV_HEADS)
        qcat, docat, s8, dp8 = {}, {}, {}, {}
        ds_u, p_u, dsinks = {}, {}, []

        def scores(g):
            qcat[g], docat[g] = _pair_rows(q_ref, g), _pair_rows(do_ref, g)
            s8[g] = _dot_nt(kst[g], qcat[g]) * scale
            dp8[g] = _dot_nt(vst[g], docat[g])

        def algebra(g):
            for hh in range(2):
                pn, ps = _attn_softmax(_fold(prevm, s8[g], hh), kill, _pair_sinks(sink_ref, g, hh))
                dp = _fold(prevm, dp8[g], hh)
                delta = jnp.sum(pn * dp, axis=0, keepdims=True)
                sd = ps * delta
                dsinks.append(jnp.where(slane == 4 * g + hh, -jnp.sum(sd[:, :ATTN_BLOCK]), 0.0)
                              + jnp.where(slane == 4 * g + 2 + hh, -jnp.sum(sd[:, ATTN_BLOCK:]), 0.0))
                ds_u[g, hh] = _unfold(prevm, pn * (dp - delta) * scale)
                p_u[g, hh] = _unfold(prevm, pn)

        def grads(g):
            dqt = _dot(kts[g][0], ds_u[g, 0]) + _dot(kts[g][1], ds_u[g, 1])
            dq_ref[:, 256 * g:256 * g + 128] = dqt[:, :ATTN_BLOCK].T
            dq_ref[:, 256 * g + 128:256 * (g + 1)] = dqt[:, ATTN_BLOCK:].T
            for hh in range(2):
                own = low if hh == 0 else jnp.logical_not(low)
                dk_h = jnp.where(own, _dot(ds_u[g, hh], qcat[g]), 0.0)
                dv_h = jnp.where(own, _dot(p_u[g, hh], docat[g]), 0.0)
                if hh != g % 2:
                    dk_h = pltpu.roll(dk_h, HEAD_DIM, 1)
                    dv_h = pltpu.roll(dv_h, HEAD_DIM, 1)
                dkz[g] = dkz[g] + dk_h
                dvz[g] = dvz[g] + dv_h

        for stage, g in ATTN_BWD_ORDER:
            (scores, algebra, grads)[stage](g)
        for t in dsinks:
            dsink = dsink + t
        pairs = range(N_KV_HEADS // 2)
        return dsink, [dkz[2 * gp] + dkz[2 * gp + 1] for gp in pairs], [dvz[2 * gp] + dvz[2 * gp + 1] for gp in pairs]

    def body(q_ref, kp_ref, kc_ref, vp_ref, vc_ref, ktp_ref, ktc_ref, sink_ref, do_ref,
             dq_ref, dk_ref, dv_ref, ds_ref):
        i = pl.program_id(0)

        @pl.when(i == 0)
        def _():
            ds_ref[...] = jnp.zeros_like(ds_ref)

        res = []
        for b in range(nbk):
            rs = slice(ATTN_BLOCK * b, ATTN_BLOCK * (b + 1))
            ps = slice(ATTN_BLOCK * (b - 1), ATTN_BLOCK * b)
            res.append(block(
                q_ref.at[rs, :], kp_ref if b == 0 else kc_ref.at[ps, :], kc_ref.at[rs, :],
                vp_ref if b == 0 else vc_ref.at[ps, :], vc_ref.at[rs, :],
                ktp_ref if b == 0 else ktc_ref.at[:, ps], ktc_ref.at[:, rs], sink_ref, do_ref.at[rs, :],
                dq_ref.at[rs, :], i == 0 if b == 0 else False))
        dsink = res[0][0]
        for r in res[1:]:
            dsink = dsink + r[0]
        ds_ref[...] += dsink
        for gp in range(N_KV_HEADS // 2):
            cols = slice(128 * gp, 128 * (gp + 1))
            for b in range(nbk):
                cur = pl.ds(pl.multiple_of(i * rows + ATTN_BLOCK * b, ATTN_BLOCK), ATTN_BLOCK)
                dkb, dvb = res[b][1][gp][ATTN_BLOCK:, :], res[b][2][gp][ATTN_BLOCK:, :]
                if b + 1 < nbk:
                    dkb, dvb = dkb + res[b + 1][1][gp][:ATTN_BLOCK, :], dvb + res[b + 1][2][gp][:ATTN_BLOCK, :]
                dk_ref[cur, cols] = dkb
                dv_ref[cur, cols] = dvb

            @pl.when(i > 0)
            def _():
                prv = pl.ds(pl.multiple_of(jnp.maximum(i * rows - ATTN_BLOCK, 0), ATTN_BLOCK), ATTN_BLOCK)
                dk_ref[prv, cols] += res[0][1][gp][:ATTN_BLOCK, :]
                dv_ref[prv, cols] += res[0][2][gp][:ATTN_BLOCK, :]

    prev = lambda i: jnp.maximum(nbk * i - 1, 0)
    whole = lambda w: pl.BlockSpec((T, w), lambda i: (0, 0))
    return _call(
        body, name=name, grid=(T // rows,), scratch_shapes=[], ins=[qkv, qkv, qkv, qkv, qkv, kt, kt, sinks, dya],
        carry=carry,
        in_specs=[pl.BlockSpec((rows, D), lambda i: (i, 0)),
                  pl.BlockSpec((ATTN_BLOCK, 256), lambda i: (prev(i), 4)),
                  pl.BlockSpec((rows, 256), lambda i: (i, 4)),
                  pl.BlockSpec((ATTN_BLOCK, 256), lambda i: (prev(i), 5)),
                  pl.BlockSpec((rows, 256), lambda i: (i, 5)),
                  pl.BlockSpec((256, ATTN_BLOCK), lambda i: (0, prev(i))),
                  pl.BlockSpec((256, rows), lambda i: (0, i)),
                  pl.BlockSpec((1, N_Q_HEADS), lambda i: (0, 0)),
                  pl.BlockSpec((rows, D), lambda i: (i, 0))],
        out_specs=[pl.BlockSpec((rows, D), lambda i: (i, 0)), whole(256), whole(256),
                   pl.BlockSpec((1, 128), lambda i: (0, 0))],
        out_shape=[jax.ShapeDtypeStruct((T, D), F32), jax.ShapeDtypeStruct((T, 256), F32),
                   jax.ShapeDtypeStruct((T, 256), F32), jax.ShapeDtypeStruct((1, 128), F32)])


HG_FWD = (256, 8, 512)
HG_BWD = (128, 8, 512)


def _chunk_sum(tri, x):
    w = x.shape[1]
    h1 = x.astype(BF16)
    h2 = (x - h1.astype(F32)).astype(BF16)
    r = _dot(tri, jnp.concatenate([h1, h2], axis=1))
    return r[:, :w] + r[:, w:]


def _chunk_tri(n, upper=False):
    ri = lax.broadcasted_iota(jnp.int32, (n, n), 0)
    ci = lax.broadcasted_iota(jnp.int32, (n, n), 1)
    same_chunk = jnp.bitwise_xor(ri, ci) < HGRN_CHUNK
    return jnp.where(((ri <= ci) if upper else (ri >= ci)) & same_chunk, 1.0, 0.0).astype(BF16)


def _chunks(t):
    return [t[HGRN_CHUNK * c:HGRN_CHUNK * (c + 1), :] for c in range(t.shape[0] // HGRN_CHUNK)]


def _lower_bound(lbl_ref):
    l0, l1 = lbl_ref[0:1, :], lbl_ref[1:2, :]
    m = jnp.maximum(l0, l1)
    e0, e1 = jnp.exp(l0 - m), jnp.exp(l1 - m)
    return e0 / (e0 + e1)


def _heads(t):
    return [t[:, 128 * h:128 * (h + 1)] for h in range(t.shape[1] // 128)]


def _per_head(fn, *wide):
    return jnp.concatenate([fn(*parts) for parts in zip(*[_heads(t) for t in wide])], axis=1)


def _hgrn_sub(fl, qh, vv, lb, tril_b):
    w = fl.shape[1]
    sg = _sigmoid(fl)
    f = lb + (1.0 - lb) * sg
    k = 1.0 - f
    gc = _chunk_sum(tril_b, jnp.log(f))
    gl_c = [t[HGRN_CHUNK - 1:HGRN_CHUNK, :] for t in _chunks(gc)]
    gl = jnp.concatenate([jnp.broadcast_to(g, (HGRN_CHUNK, w)) for g in gl_c], axis=0)
    sq = _sigmoid(qh)
    eg = jnp.exp(gc)
    eng = jnp.exp(-gc)
    elg = jnp.exp(gl - gc)
    qd = qh * sq * eg
    ki = k * eng
    ke = k * elg
    qd_b, ki_b, ke_b, v_b = (t.astype(BF16) for t in (qd, ki, ke, vv))
    am_b = [jnp.where(tril_b > 0, _dot_nt(qq, kk), 0.0).astype(BF16) for qq, kk in zip(_heads(qd_b), _heads(ki_b))]
    return dict(sg=sg, f=f, sq=sq, eg=eg, eng=eng, elg=elg, qd=qd, ki=ki, ke=ke, egl=[jnp.exp(g) for g in gl_c],
                am_b=am_b, qd_b=qd_b, ki_b=ki_b, ke_b=ke_b, v_b=v_b)


def _hgrn_out(q, st_b):
    outs = []
    for h, (qd_h, v_h) in enumerate(zip(_heads(q["qd_b"]), _heads(q["v_b"]))):
        inter = [_dot_nt(qc, s) for qc, s in zip(_chunks(qd_h), st_b[h])]
        outs.append(_dot(q["am_b"][h], v_h) + jnp.concatenate(inter, axis=0))
    return jnp.concatenate(outs, axis=1)


def _head_rms(o):
    return _per_head(lambda t: jnp.broadcast_to(
        lax.rsqrt(jnp.mean(t * t, axis=-1, keepdims=True) + RMS_EPS), t.shape), o)


def _hgrn_fwd(name, hg, lb_logits, norm_g, cfg):
    T = hg.shape[0]
    sub_rows, hps, th = min(cfg[0], T), cfg[1], min(cfg[2], T)
    cps = sub_rows // HGRN_CHUNK
    nc = th // HGRN_CHUNK

    def body(fl_ref, qh_ref, v_ref, og_ref, lbl_ref, ng_ref, y_ref, ss_ref, st):
        @pl.when(pl.program_id(1) == 0)
        def _():
            st[...] = jnp.zeros_like(st)

        lbs = _lower_bound(lbl_ref)
        ng = jnp.tile(ng_ref[...], (1, hps))
        tril_b = _chunk_tri(sub_rows)

        def sub(si, carry):
            rows = pl.ds(pl.multiple_of(si * sub_rows, sub_rows), sub_rows)
            q = _hgrn_sub(fl_ref[rows, :], qh_ref[rows, :], v_ref[rows, :], lbs, tril_b)
            v_hc = [_chunks(t) for t in _heads(q["v_b"])]
            k_hc = [_chunks(t) for t in _heads(q["ke_b"])]
            s = [st[h] for h in range(hps)]
            st_b = [[] for _ in range(hps)]
            for c in range(cps):
                egl = _heads(q["egl"][c])
                for h in range(hps):
                    ss_ref[h, si * cps + c] = s[h]
                    st_b[h].append(s[h].astype(BF16))
                    s[h] = s[h] * egl[h] + _dot_tn(v_hc[h][c], k_hc[h][c])
            for h in range(hps):
                st[h] = s[h]
            o = _hgrn_out(q, st_b)
            og = og_ref[rows, :]
            y_ref[rows, :] = (o * _head_rms(o) * ng * (og * _sigmoid(og))).astype(BF16)
            return carry

        lax.fori_loop(0, th // sub_rows, sub, 0)

    wd = 128 * hps
    col = lambda j: pl.BlockSpec((th, wd), lambda h, t: (t, (8 // hps) * j + h))
    return pl.pallas_call(
        body, name=name, grid=(HGRN_HEADS // hps, T // th),
        in_specs=[col(0), col(1), col(2), col(3), pl.BlockSpec((2, wd), lambda h, t: (0, h)),
                  pl.BlockSpec((1, 128), lambda h, t: (0, 0))],
        out_specs=[pl.BlockSpec((th, wd), lambda h, t: (t, h)),
                   pl.BlockSpec((hps, nc, 128, 128), lambda h, t: (h, t, 0, 0))],
        out_shape=[jax.ShapeDtypeStruct((T, D), BF16),
                   jax.ShapeDtypeStruct((HGRN_HEADS, T // HGRN_CHUNK, 128, 128), F32)],
        scratch_shapes=[pltpu.VMEM((hps, 128, 128), F32)],
        compiler_params=_params("parallel", "arbitrary"),
    )(hg, hg, hg, hg, lb_logits, norm_g)


def _hgrn_bwd(name, hg, lb_logits, norm_g, sstart, dyh, cfg, carry=None):
    T = hg.shape[0]
    sub_rows, hps, th = min(cfg[0], T), cfg[1], min(cfg[2], T)
    cps = sub_rows // HGRN_CHUNK
    nc = th // HGRN_CHUNK
    nt = T // th
    wd = 128 * hps

    def body(fl_ref, qh_ref, v_ref, og_ref, lbl_ref, ng_ref, ss_ref, dy_ref,
             dfl_ref, dqh_ref, dv_ref, dog_ref, dlb_ref, dng_ref, dst):
        @pl.when(pl.program_id(1) == 0)
        def _():
            dst[...] = jnp.zeros_like(dst)
            dlb_ref[...] = jnp.zeros_like(dlb_ref)
            dng_ref[...] = jnp.zeros_like(dng_ref)

        lb = _lower_bound(lbl_ref)
        ng = jnp.tile(ng_ref[...], (1, hps))
        last = lax.broadcasted_iota(jnp.int32, (HGRN_CHUNK, wd), 0) == HGRN_CHUNK - 1
        nsub = th // sub_rows
        cat0 = lambda parts: jnp.concatenate(parts, axis=0)
        tril_b, triu_b = _chunk_tri(sub_rows), _chunk_tri(sub_rows, upper=True)

        def sub(step, carry):
            si = nsub - 1 - step
            rows = pl.ds(pl.multiple_of(si * sub_rows, sub_rows), sub_rows)
            qh, og, dy = qh_ref[rows, :], og_ref[rows, :], dy_ref[rows, :]
            q = _hgrn_sub(fl_ref[rows, :], qh, v_ref[rows, :], lb, tril_b)
            s_in = [[ss_ref[h, si * cps + c] for c in range(cps)] for h in range(hps)]
            st_b = [[s.astype(BF16) for s in row] for row in s_in]
            o = _hgrn_out(q, st_b)
            rr = _head_rms(o)
            oh = o * rr
            sog = _sigmoid(og)
            dog_ref[rows, :] = (dy * oh * ng * (sog * (1.0 + og * (1.0 - sog)))).astype(BF16)
            don = dy * (og * sog)
            dng_w = jnp.sum(don * oh, axis=0, keepdims=True)
            dd = don * ng
            mean_h = _per_head(lambda t: jnp.broadcast_to(jnp.mean(t, axis=-1, keepdims=True), t.shape), dd * oh)
            do_b = (rr * (dd - oh * mean_h)).astype(BF16)
            do_h, qd_h, ki_h, ke_h, v_h = (_heads(t) for t in (do_b, q["qd_b"], q["ki_b"], q["ke_b"], q["v_b"]))
            da_b = [jnp.where(tril_b > 0, _dot_nt(do_h[h], v_h[h]), 0.0).astype(BF16) for h in range(hps)]
            do_c, qd_c, ke_c, v_c = ([_chunks(t) for t in hs] for hs in (do_h, qd_h, ke_h, v_h))
            dsp = [[None] * cps for _ in range(hps)]
            dgl_dec = [[None] * cps for _ in range(hps)]
            d = [dst[h] for h in range(hps)]
            for c in reversed(range(cps)):
                egl = _heads(q["egl"][c])
                for h in range(hps):
                    dsp[h][c] = d[h]
                    dgl_dec[h][c] = jnp.sum(d[h] * s_in[h][c], axis=0, keepdims=True) * egl[h]
                    d[h] = d[h] * egl[h] + _dot_tn(do_c[h][c], qd_c[h][c])
            for h in range(hps):
                dst[h] = d[h]
                dng_ref[h] += dng_w[:, 128 * h:128 * (h + 1)]
            dsp_b = [[t.astype(BF16) for t in row] for row in dsp]
            dv_ref[rows, :] = jnp.concatenate(
                [_dot_tn(q["am_b"][h], do_h[h]) + cat0([_dot_nt(ke_c[h][c], dsp_b[h][c]) for c in range(cps)])
                 for h in range(hps)], axis=1).astype(BF16)
            dqd = jnp.concatenate(
                [_dot(da_b[h], ki_h[h]) + cat0([_dot(do_c[h][c], st_b[h][c]) for c in range(cps)])
                 for h in range(hps)], axis=1)
            dki = jnp.concatenate([_dot_tn(da_b[h], qd_h[h]) for h in range(hps)], axis=1)
            dke = jnp.concatenate([cat0([_dot(v_c[h][c], dsp_b[h][c]) for c in range(cps)]) for h in range(hps)], axis=1)
            dkk = dke * q["ke"]
            dgl = [jnp.sum(t, axis=0, keepdims=True) + jnp.concatenate([dgl_dec[h][c] for h in range(hps)], axis=1)
                   for c, t in enumerate(_chunks(dkk))]
            dgc = dqd * q["qd"] - dki * q["ki"] - dkk + cat0([jnp.where(last, g, 0.0) for g in dgl])
            dlf = _chunk_sum(triu_b, dgc)
            df = dlf / q["f"] - (dki * q["eng"] + dke * q["elg"])
            sg = q["sg"]
            dlb_ref[...] += jnp.sum(df * (1.0 - sg), axis=0, keepdims=True)
            dfl_ref[rows, :] = (df * (1.0 - lb) * sg * (1.0 - sg)).astype(BF16)
            sq = q["sq"]
            dqh_ref[rows, :] = (dqd * q["eg"] * (sq * (1.0 + qh * (1.0 - sq)))).astype(BF16)
            return carry

        lax.fori_loop(0, nsub, sub, 0)
    col = lambda j: pl.BlockSpec((th, wd), lambda h, t: (nt - 1 - t, (8 // hps) * j + h))
    out = pl.BlockSpec((th, wd), lambda h, t: (nt - 1 - t, h))
    return _call(
        body, name=name, grid=(HGRN_HEADS // hps, nt),
        in_specs=[col(0), col(1), col(2), col(3), pl.BlockSpec((2, wd), lambda h, t: (0, h)),
                  pl.BlockSpec((1, 128), lambda h, t: (0, 0)),
                  pl.BlockSpec((hps, nc, 128, 128), lambda h, t: (h, nt - 1 - t, 0, 0)), out],
        out_specs=[out, out, out, out, pl.BlockSpec((1, wd), lambda h, t: (0, h)),
                   pl.BlockSpec((hps, 1, 128), lambda h, t: (h, 0, 0))],
        out_shape=[jax.ShapeDtypeStruct((T, D), BF16)] * 4 + [jax.ShapeDtypeStruct((1, D), F32),
                                                              jax.ShapeDtypeStruct((HGRN_HEADS, 1, 128), F32)],
        scratch_shapes=[pltpu.VMEM((hps, 128, 128), F32)],
        ins=[hg, hg, hg, hg, lb_logits, norm_g, sstart, dyh], carry=carry)


def _mix_fwd(name, ya, yh, gates, xh1, g1, b1, wall, tm):
    T = ya.shape[0]

    def body(ya_ref, yh_ref, ga_ref, gh_ref, xh_ref, g_ref, b_ref, wall_ref,
             xo_ref, r_ref, pa_ref, ph_ref, mg_ref, wpa, wph, wo, sem):
        @pl.when(pl.program_id(0) == 0)
        def _():
            _load_rows(wall_ref, [(wpa, RIN, RP), (wph, RIN + RP, RP), (wo, RIN + 2 * RP, RP)], sem)

        parts = [slice(p, min(p + PART_ROWS, tm)) for p in range(0, tm, PART_ROWS)]
        pas = [_dot(ya_ref[rs, :], wpa[...]).astype(BF16) for rs in parts]
        phs = [_dot(yh_ref[rs, :], wph[...]).astype(BF16) for rs in parts]
        mgs = []
        for rs, pa, ph in zip(parts, pas, phs):
            pa_ref[rs, :] = pa
            ph_ref[rs, :] = ph
            sa, sh = _sigmoid(ga_ref[rs, :].astype(F32)), _sigmoid(gh_ref[rs, :].astype(F32))
            mgs.append((sa * pa.astype(F32) + sh * ph.astype(F32)).astype(BF16))
            mg_ref[rs, :] = mgs[-1]
        mixes = [_dot(mg, wo[...]) for mg in mgs]
        for rs, mix in zip(parts, mixes):
            y1 = xh_ref[rs, :] * g_ref[...] + b_ref[...]
            xh2, r = _ln_fwd(ALPHA * y1 + mix)
            xo_ref[rs, :] = xh2
            r_ref[rs, :] = jnp.broadcast_to(r, (xh2.shape[0], 128))

    row = lambda w: pl.BlockSpec((tm, w), lambda i: (i, 0))
    vec = pl.BlockSpec((1, D), lambda i: (0, 0))
    return pl.pallas_call(
        body, name=name, grid=(T // tm,),
        in_specs=[row(D), row(D), pl.BlockSpec((tm, D), lambda i: (i, 0)), pl.BlockSpec((tm, D), lambda i: (i, 1)),
                  row(D), vec, vec, HBM_SPEC],
        out_specs=[row(D), row(128), row(D), row(D), row(D)],
        out_shape=[jax.ShapeDtypeStruct((T, D), F32), jax.ShapeDtypeStruct((T, 128), F32)]
        + [jax.ShapeDtypeStruct((T, D), BF16)] * 3,
        scratch_shapes=[pltpu.VMEM((D, D), BF16)] * 3 + [pltpu.SemaphoreType.DMA((3 * NDEV,))],
        compiler_params=_params("arbitrary"),
    )(ya, yh, gates, gates, xh1, g1, b1, wall)


def _mix_bwd(name, dy2, xh2, r2, g2, gates, pa, ph, wall, tm, carry=None):
    T = dy2.shape[0]

    def body(dy_ref, xh_ref, r_ref, g_ref, ga_ref, gh_ref, pa_ref, ph_ref, wall_ref,
             dz_ref, dmix_ref, dpa_ref, dph_ref, dga_ref, dgh_ref, dya_ref, dyh_ref, dg_ref, db_ref,
             wpa, wph, wo, sem):
        @pl.when(pl.program_id(0) == 0)
        def _():
            _load_rows(wall_ref, [(wpa, RIN, RP), (wph, RIN + RP, RP), (wo, RIN + 2 * RP, RP)], sem)
            dg_ref[...] = jnp.zeros_like(dg_ref)
            db_ref[...] = jnp.zeros_like(db_ref)

        parts = [slice(p, min(p + PART_ROWS, tm)) for p in range(0, tm, PART_ROWS)]
        dmix = []
        for rs in parts:
            dz, dgp, dbp = _ln_bwd(dy_ref[rs, :], xh_ref[rs, :], r_ref[rs, :1], g_ref[...])
            dg_ref[...] += dgp
            db_ref[...] += dbp
            dz_ref[rs, :] = dz
            dmix.append(dz.astype(BF16))
            dmix_ref[rs, :] = dmix[-1]
        dmgs = [_dot_nt(d, wo[...]) for d in dmix]
        dpas, dphs = [], []
        for rs, dmg in zip(parts, dmgs):
            sa, sh = _sigmoid(ga_ref[rs, :].astype(F32)), _sigmoid(gh_ref[rs, :].astype(F32))
            dga_ref[rs, :] = (dmg * pa_ref[rs, :].astype(F32) * sa * (1.0 - sa)).astype(BF16)
            dgh_ref[rs, :] = (dmg * ph_ref[rs, :].astype(F32) * sh * (1.0 - sh)).astype(BF16)
            dpas.append((dmg * sa).astype(BF16))
            dphs.append((dmg * sh).astype(BF16))
            dpa_ref[rs, :] = dpas[-1]
            dph_ref[rs, :] = dphs[-1]
        for rs, dpa, dph in zip(parts, dpas, dphs):
            dya_ref[rs, :] = _dot_nt(dpa, wpa[...]).astype(BF16)
            dyh_ref[rs, :] = _dot_nt(dph, wph[...])

    row = lambda w: pl.BlockSpec((tm, w), lambda i: (i, 0))
    vec = pl.BlockSpec((1, D), lambda i: (0, 0))
    return _call(
        body, name=name, grid=(T // tm,),
        in_specs=[row(D), row(D), row(128), vec, pl.BlockSpec((tm, D), lambda i: (i, 0)),
                  pl.BlockSpec((tm, D), lambda i: (i, 1)), row(D), row(D), HBM_SPEC],
        out_specs=[row(D)] * 8 + [vec, vec],
        out_shape=[jax.ShapeDtypeStruct((T, D), F32)] + [jax.ShapeDtypeStruct((T, D), BF16)] * 6
        + [jax.ShapeDtypeStruct((T, D), F32)] + [jax.ShapeDtypeStruct((1, D), F32)] * 2,
        scratch_shapes=[pltpu.VMEM((D, D), BF16)] * 3 + [pltpu.SemaphoreType.DMA((3 * NDEV,))],
        ins=[dy2, xh2, r2, g2, gates, gates, pa, ph, wall], carry=carry)


def _adam(w, g, m, v):
    m = ADAM_B1 * m + (1.0 - ADAM_B1) * g
    v = ADAM_B2 * v + (1.0 - ADAM_B2) * (g * g)
    m_hat = m / (1.0 - ADAM_B1 ** ADAM_STEP)
    v_hat = v / (1.0 - ADAM_B2 ** ADAM_STEP)
    delta = -ADAM_LR * (m_hat / (jnp.sqrt(v_hat) + ADAM_EPS) + ADAM_WD * w)
    return delta, m, v


def _grad_steps(name, items, cb, carry=None):
    n = len(items)
    pieces = [len(it[0]) for it in items]

    def body(*refs):
        it = iter(refs[:sum(pieces) + 3 * n])
        outs = refs[sum(pieces) + 3 * n:]
        for i in range(n):
            accs = []
            for _ in range(pieces[i]):
                r_ref = next(it)
                acc = r_ref[0].astype(F32)
                for k in range(1, NDEV):
                    acc = acc + r_ref[k].astype(F32)
                accs.append(acc)
            acc = accs[0] if len(accs) == 1 else jnp.concatenate(accs, axis=0)
            w_ref, m_ref, v_ref = next(it), next(it), next(it)
            g_ref, d_ref, mo_ref, vo_ref = outs[4 * i:4 * i + 4]
            g_ref[...] = acc
            d, mm, vv = _adam(w_ref[...], acc, m_ref[...], v_ref[...])
            d_ref[...] = d
            mo_ref[...] = mm
            vo_ref[...] = vv

    in_specs, out_specs, out_shape, ins = [], [], [], []
    for recv, w, m, v in items:
        blk = pl.BlockSpec((w.shape[0], cb), lambda j: (0, j))
        in_specs += [pl.BlockSpec((NDEV, rp.shape[1], cb), lambda j: (0, 0, j)) for rp in recv] + [blk, blk, blk]
        out_specs += [blk] * 4
        out_shape += [jax.ShapeDtypeStruct(w.shape, F32)] * 4
        ins += list(recv) + [w, m, v]
    outs, ex = _call(body, name=name, grid=(D // cb,), in_specs=in_specs, out_specs=out_specs, out_shape=out_shape,
                     scratch_shapes=[], ins=ins, carry=carry)
    return [tuple(outs[4 * i:4 * i + 4]) for i in range(n)], ex


_SMALL = [("ln1_g", D), ("ln1_b", D), ("ln2_g", D), ("ln2_b", D), ("ln3_g", D), ("ln3_b", D),
          ("b_in", DIN), ("lb", D), ("attn_sinks", 128), ("hgrn_norm_g", 128), ("loss", 128)]
_SMALL_OFF = {}
_o = 0
for _n, _w in _SMALL:
    _SMALL_OFF[_n] = (_o, _w)
    _o += _w
PACK = _o


def _small_step(name, slots, small_w, small_m, small_v):
    names = ["ln1_g", "ln1_b", "ln2_g", "ln2_b", "ln3_g", "ln3_b", "b_in", "attn_sinks", "hgrn_lb_logits", "hgrn_norm_g"]
    np_ = len(names)

    def body(*refs):
        s_ref = refs[0]
        w_refs = refs[1:1 + np_]
        m_refs = refs[1 + np_:1 + 2 * np_]
        v_refs = refs[1 + 2 * np_:1 + 3 * np_]
        loss_ref, outs = refs[1 + 3 * np_], refs[2 + 3 * np_:]
        tot = s_ref[0]
        for k in range(1, NDEV):
            tot = tot + s_ref[k]

        def part(n):
            o, w = _SMALL_OFF[n]
            return tot[:, o:o + w]

        loss_ref[...] = part("loss")
        for i, n in enumerate(names):
            w = w_refs[i][...]
            if n == "hgrn_lb_logits":
                m_ = jnp.maximum(w[0:1, :], w[1:2, :])
                e0, e1 = jnp.exp(w[0:1, :] - m_), jnp.exp(w[1:2, :] - m_)
                p0 = e0 / (e0 + e1)
                t = p0 * (1.0 - p0) * part("lb")
                g = jnp.concatenate([t, -t], axis=0)
            elif n == "attn_sinks":
                g = part(n)[:, :N_Q_HEADS]
            else:
                g = part(n)
            d, mm, vv = _adam(w, g, m_refs[i][...], v_refs[i][...])
            outs[4 * i][...] = g
            outs[4 * i + 1][...] = d
            outs[4 * i + 2][...] = mm
            outs[4 * i + 3][...] = vv

    out_shape = [jax.ShapeDtypeStruct((1, 128), F32)]
    for n in names:
        out_shape += [jax.ShapeDtypeStruct(small_w[n].shape, F32)] * 4
    return pl.pallas_call(
        body, name=name, out_shape=out_shape,
        compiler_params=pltpu.CompilerParams(vmem_limit_bytes=VMEM_LIMIT),
    )(slots, *[small_w[n] for n in names], *[small_m[n] for n in names], *[small_v[n] for n in names]), names


def _tile(T, pref):
    return min(T, pref)


def kernel(x, ln1_g, ln1_b, ffn1_w1, ffn1_w3, ffn1_w2, ln2_g, ln2_b, w_in, b_in, attn_sinks, hgrn_lb_logits, hgrn_norm_g, w_proj_attn, w_proj_hgrn, w_out, ln3_g, ln3_b, ffn2_w1, ffn2_w3, ffn2_w2, loss_target, m_ln1_g, m_ln1_b, m_ffn1_w1, m_ffn1_w3, m_ffn1_w2, m_ln2_g, m_ln2_b, m_w_in, m_b_in, m_attn_sinks, m_hgrn_lb_logits, m_hgrn_norm_g, m_w_proj_attn, m_w_proj_hgrn, m_w_out, m_ln3_g, m_ln3_b, m_ffn2_w1, m_ffn2_w3, m_ffn2_w2, v_ln1_g, v_ln1_b, v_ffn1_w1, v_ffn1_w3, v_ffn1_w2, v_ln2_g, v_ln2_b, v_w_in, v_b_in, v_attn_sinks, v_hgrn_lb_logits, v_hgrn_norm_g, v_w_proj_attn, v_w_proj_hgrn, v_w_out, v_ln3_g, v_ln3_b, v_ffn2_w1, v_ffn2_w3, v_ffn2_w2):
    T = x.shape[1]
    xs = x[0]
    tgt = loss_target[0]
    tm = _tile(T, 256)
    tm2 = _tile(T, 512)
    tk = _tile(T, 2048)

    t_bf = lambda w: w[0].T.astype(BF16)
    n_bf = lambda w: w[0].astype(BF16)
    ffn_shard = lambda w1, w3, w2: jnp.concatenate([t_bf(w1), t_bf(w3), n_bf(w2)], axis=0)
    mix_shard = jnp.concatenate([t_bf(w_in), n_bf(w_proj_attn), n_bf(w_proj_hgrn), n_bf(w_out)], axis=0)
    (ffn1_all,) = _exchange_call("gather_ffn1", _relay_gather_exchange(ffn_shard(ffn1_w1, ffn1_w3, ffn1_w2)))
    ffn_offs = (0, RF, 2 * RF)
    rope = _rope_table(T)

    (xh1, r1, a1, b1, xb0), (mix_all,) = _ffn_fwd("ffn1_fwd", xs, None, ffn1_all, ffn_offs, tm2,
                                                  carry=_gather_exchange(mix_shard))
    (qkv, kt, vt, hg, gates, y1b), (ffn2_all,) = _inproj_fwd(
        "inproj_fwd", xh1, ln1_g, ln1_b, mix_all, b_in, rope, tm2,
        carry=_gather_exchange(ffn_shard(ffn2_w1, ffn2_w3, ffn2_w2)))
    ya = _attn_fwd("attn_fwd", qkv, vt, attn_sinks)
    yh, sstart = _hgrn_fwd("hgrn_fwd", hg, hgrn_lb_logits, hgrn_norm_g, HG_FWD)
    xh2, r2, pa, ph, merged = _mix_fwd("mix_fwd", ya, yh, gates, xh1, ln1_g, ln1_b, mix_all, tm2)
    (xh3, r3, a2, b2, y2b, loss_part), _ = _ffn_fwd("ffn2_fwd", xh2, (ln2_g, ln2_b), ffn2_all, ffn_offs, tm2,
                                                    loss=(ln3_g, ln3_b, tgt))

    (dy2, dab2, u2, df2, dg3, db3), _ = _ffn_bwd("ffn2_bwd", ("loss", ln3_b, tgt), xh3, r3, ln3_g, a2, b2, ffn2_all,
                                                 ffn_offs, tm)
    recv = {}
    g_ffn2_13, _ = _wgrad("wgrad_ffn2_w13", dab2, y2b, DFF // 2, tk)
    g_ffn2_2, _ = _wgrad("wgrad_ffn2_w2", u2, df2, DFF // 2, tk)
    (dz2, dmix, dpa, dph, dga, dgh, dya, dyh, dg2, db2), _ = _mix_bwd(
        "mix_bwd", dy2, xh2, r2, ln2_g, gates, pa, ph, mix_all, tm2)
    g_wo, _ = _wgrad("wgrad_w_out", merged, dmix, D, tk)
    g_pa, _ = _wgrad("wgrad_w_pa", ya, dpa, D, tk)
    g_ph, _ = _wgrad("wgrad_w_ph", yh, dph, D, tk)
    (dq, dk, dv, dsink), _ = _attn_bwd("attn_bwd", qkv, kt, attn_sinks, dya)
    (dfl, dqh, dih, dog, dlb, dng), (recv["ffn2_w2"], recv["w_proj_attn"], recv["w_proj_hgrn"], recv["w_out"]) = \
        _hgrn_bwd("hgrn_bwd", hg, hgrn_lb_logits, hgrn_norm_g, sstart, dyh, HG_BWD,
                  carry=_grad_exchange([g_ffn2_2, g_pa, g_ph, g_wo], [0, 0, 0, 0], [RF, RP, RP, RP]))
    (dy1, dproj, dbin), (recv["ffn2_w1"], recv["ffn2_w3"]) = _inproj_bwd(
        "inproj_bwd", dq, dk, dv, (dfl, dqh, dih, dog), (dga, dgh), dz2, mix_all, rope, tm2,
        carry=_grad_exchange([g_ffn2_13, g_ffn2_13], [0, DFF], [RF, RF]))
    g_win, _ = _wgrad("wgrad_w_in", dproj, y1b, DIN // 4, tk)
    win_rows = (400, 288, 272)
    win_base = (0, 400, 688)
    (dz1, dab1, u1, df1, dg1, db1), (rw0,) = _ffn_bwd(
        "ffn1_bwd", ("dy", dy1), xh1, r1, ln1_g, a1, b1, ffn1_all, ffn_offs, tm2, with_dx=False,
        carry=_grad_exchange([g_win], win_base[:1], win_rows[:1], [RIN]))
    g_ffn1_2, (rw1,) = _wgrad("wgrad_ffn1_w2", u1, df1, DFF // 2, tk,
                              carry=_grad_exchange([g_win], win_base[1:2], win_rows[1:2], [RIN]))
    g_ffn1_1, (recv["ffn1_w2"],) = _wgrad("wgrad_ffn1_w1", dab1, xb0, DFF // 2, tk, 0, DFF,
                                          carry=_grad_exchange([g_ffn1_2], [0], [RF]))
    g_ffn1_3, (recv["ffn1_w1"],) = _wgrad("wgrad_ffn1_w3", dab1, xb0, DFF // 2, tk, DFF, DFF,
                                          carry=_grad_exchange([g_ffn1_1], [0], [RF]))
    parts = {"ln1_g": dg1, "ln1_b": db1, "ln2_g": dg2, "ln2_b": db2, "ln3_g": dg3, "ln3_b": db3, "b_in": dbin,
             "lb": dlb, "attn_sinks": dsink, "hgrn_norm_g": jnp.sum(dng, axis=0), "loss": loss_part[0:1, :]}
    packed = jnp.concatenate([parts[n] for n, _ in _SMALL], axis=1)
    gx, (recv["ffn1_w3"], rw2, small_slots) = _ffn_dx(
        "ffn1_dx", dab1, dz1, ffn1_all, ffn_offs, tm2,
        carry=_join(_grad_exchange([g_ffn1_3, g_win], [0, win_base[2]], [RF, win_rows[2]], [RF, RIN]),
                    _row_gather_exchange(packed)))
    recv["w_in"] = [rw0, rw1, rw2]

    big = [("ffn1_w1", ffn1_w1, m_ffn1_w1, v_ffn1_w1, True), ("ffn1_w3", ffn1_w3, m_ffn1_w3, v_ffn1_w3, True),
           ("ffn1_w2", ffn1_w2, m_ffn1_w2, v_ffn1_w2, False), ("w_in", w_in, m_w_in, v_w_in, True),
           ("w_proj_attn", w_proj_attn, m_w_proj_attn, v_w_proj_attn, False),
           ("w_proj_hgrn", w_proj_hgrn, m_w_proj_hgrn, v_w_proj_hgrn, False),
           ("w_out", w_out, m_w_out, v_w_out, False),
           ("ffn2_w1", ffn2_w1, m_ffn2_w1, v_ffn2_w1, True), ("ffn2_w3", ffn2_w3, m_ffn2_w3, v_ffn2_w3, True),
           ("ffn2_w2", ffn2_w2, m_ffn2_w2, v_ffn2_w2, False)]
    view = lambda t, transposed: t[0].T if transposed else t[0]
    back = lambda t, transposed: t.T[None] if transposed else t[None]
    slots = lambda n: recv[n] if isinstance(recv[n], list) else [recv[n]]
    stepped, _ = _grad_steps("steps", [(slots(n), view(w, tr), view(m, tr), view(v, tr)) for n, w, m, v, tr in big], 128)
    res = {n: tuple(back(t, tr) for t in outs) for (n, _, _, _, tr), outs in zip(big, stepped)}

    small_w = dict(ln1_g=ln1_g, ln1_b=ln1_b, ln2_g=ln2_g, ln2_b=ln2_b, ln3_g=ln3_g, ln3_b=ln3_b, b_in=b_in,
                   attn_sinks=attn_sinks, hgrn_lb_logits=hgrn_lb_logits, hgrn_norm_g=hgrn_norm_g)
    small_m = dict(ln1_g=m_ln1_g, ln1_b=m_ln1_b, ln2_g=m_ln2_g, ln2_b=m_ln2_b, ln3_g=m_ln3_g, ln3_b=m_ln3_b,
                   b_in=m_b_in, attn_sinks=m_attn_sinks, hgrn_lb_logits=m_hgrn_lb_logits, hgrn_norm_g=m_hgrn_norm_g)
    small_v = dict(ln1_g=v_ln1_g, ln1_b=v_ln1_b, ln2_g=v_ln2_g, ln2_b=v_ln2_b, ln3_g=v_ln3_g, ln3_b=v_ln3_b,
                   b_in=v_b_in, attn_sinks=v_attn_sinks, hgrn_lb_logits=v_hgrn_lb_logits, hgrn_norm_g=v_hgrn_norm_g)
    outs, names = _small_step("small_step", small_slots, small_w, small_m, small_v)
    loss = outs[0][0, 0]
    for i, n in enumerate(names):
        res[n] = tuple(outs[1 + 4 * i:5 + 4 * i])

    order = ["ln1_g", "ln1_b", "ffn1_w1", "ffn1_w3", "ffn1_w2", "ln2_g", "ln2_b", "w_in", "b_in", "attn_sinks",
             "hgrn_lb_logits", "hgrn_norm_g", "w_proj_attn", "w_proj_hgrn", "w_out", "ln3_g", "ln3_b",
             "ffn2_w1", "ffn2_w3", "ffn2_w2"]
    return (loss, gx[None], *[res[n][0] for n in order], *[res[n][1] for n in order],
            *[res[n][2] for n in order], *[res[n][3] for n in order])
```

```python
import jax
import jax.numpy as jnp
from jax import lax
from jax.experimental import pallas as pl
from jax.experimental.pallas import tpu as pltpu

F32 = jnp.float32
BF16 = jnp.bfloat16

NDEV = 8
D = 1024
DFF = 2816
RF = DFF // NDEV
DIN = 7680
RIN = DIN // NDEV
RP = D // NDEV
N_Q_HEADS = 16
N_KV_HEADS = 4
HEAD_DIM = 64
ATTN_BLOCK = 128
ROPE_THETA = 500000.0
ROPE_DIM = HEAD_DIM // 4
HGRN_HEADS = 8
HGRN_CHUNK = 64
ALPHA = 2.0 ** 0.25
LN_EPS = 1e-5
RMS_EPS = 1e-6
NEG_INF = -1e30
ADAM_LR = 0.001
ADAM_B1 = 0.9
ADAM_B2 = 0.999
ADAM_EPS = 1e-08
ADAM_WD = 0.01
ADAM_STEP = 10

QKV_W = 1536
HG_W = 4096
GATE_W = 2048
VMEM_LIMIT = 60 * 2 ** 20
PART_ROWS = 256
MESH = pl.DeviceIdType.MESH
HBM_SPEC = pl.BlockSpec(memory_space=pltpu.HBM)


def _params(*sem):
    return pltpu.CompilerParams(dimension_semantics=sem, vmem_limit_bytes=VMEM_LIMIT)


def _dot(a, b):
    return jnp.dot(a, b, preferred_element_type=F32)


def _dot_nt(a, b):
    return lax.dot_general(a, b, (((1,), (1,)), ((), ())), preferred_element_type=F32)


def _dot_tn(a, b):
    return lax.dot_general(a, b, (((0,), (0,)), ((), ())), preferred_element_type=F32)


def _sigmoid(x):
    return 0.5 * jnp.tanh(0.5 * x) + 0.5


def _ln_fwd(z):
    mu = jnp.mean(z, axis=-1, keepdims=True)
    zc = z - mu
    var = jnp.mean(zc * zc, axis=-1, keepdims=True)
    r = lax.rsqrt(var + LN_EPS)
    return zc * r, r


def _ln_bwd(dy, xh, r, g):
    dxh = dy * g
    m1 = jnp.mean(dxh, axis=-1, keepdims=True)
    m2 = jnp.mean(dxh * xh, axis=-1, keepdims=True)
    dz = r * (dxh - m1 - xh * m2)
    return dz, jnp.sum(dy * xh, axis=0, keepdims=True), jnp.sum(dy, axis=0, keepdims=True)


def _load_rows(wall_ref, pieces, sem):
    copies = []
    for dst, off, r in pieces:
        for k in range(NDEV):
            c = pltpu.make_async_copy(wall_ref.at[k, pl.ds(off, r), :], dst.at[pl.ds(k * r, r), :], sem.at[len(copies)])
            c.start()
            copies.append(c)
    for c in copies:
        c.wait()


class _Exchange:
    def __init__(self, inputs, out_shape, scratch, begin, middle, end):
        self.inputs, self.out_shape, self.scratch = inputs, out_shape, scratch
        self.begin, self.middle, self.end = begin, middle, end


def _gather_exchange(shard):
    rows, cols = shard.shape

    def ops(ins, outs, scr):
        (x_ref,), (out_ref,), (send_sems, recv_sems, local_sem) = ins, outs, scr
        x, y, c = lax.axis_index("x"), lax.axis_index("y"), lax.axis_index("c")
        me, sibling = (x, y, c), (x, y, 1 - c)
        chips = [(1 - x, y), (x, 1 - y), (1 - x, 1 - y)]

        def slot(px, py, pc):
            return out_ref.at[4 * px + 2 * py + pc]

        def copy(k, block, to, src=None):
            return pltpu.make_async_remote_copy(
                src_ref=slot(*block) if src is None else src, dst_ref=slot(*block),
                send_sem=send_sems.at[k], recv_sem=recv_sems.at[k], device_id=to, device_id_type=MESH)

        mine = lambda: pltpu.make_async_copy(x_ref, slot(*me), local_sem)
        first = lambda: [copy(0, me, sibling, src=x_ref)] + [
            copy(1 + j, me, (*chip, c), src=x_ref) for j, chip in enumerate(chips)]
        passed = lambda: [copy(4 + j, (*chip, c), sibling) for j, chip in enumerate(chips)]
        return c, me, sibling, chips, copy, mine, first, passed

    def begin(*refs):
        _, _, _, _, _, mine, first, _ = ops(*refs)
        mine().start()
        for cp in first():
            cp.start()

    def middle(*refs):
        c, me, _, chips, copy, _, _, passed = ops(*refs)
        for (j, chip), fwd in zip(enumerate(chips), passed()):
            copy(1 + j, (*chip, c), me).wait_recv()
            fwd.start()

    def end(*refs):
        c, me, sibling, chips, copy, mine, first, passed = ops(*refs)
        copy(0, sibling, me).wait_recv()
        for j, chip in enumerate(chips):
            copy(4 + j, (*chip, 1 - c), me).wait_recv()
        for cp in first() + passed():
            cp.wait_send()
        mine().wait()

    return _Exchange([shard], [jax.ShapeDtypeStruct((NDEV, rows, cols), shard.dtype)],
                     [pltpu.SemaphoreType.DMA((7,)), pltpu.SemaphoreType.DMA((7,)), pltpu.SemaphoreType.DMA],
                     begin, middle, end)


def _relay_gather_exchange(shard):
    rows, cols = shard.shape

    def ops(ins, outs, scr):
        (x_ref,), (out_ref,), (send_sems, recv_sems, local_sem) = ins, outs, scr
        x, y, c = lax.axis_index("x"), lax.axis_index("y"), lax.axis_index("c")
        me, sibling = (x, y, c), (x, y, 1 - c)
        xchip, ychip, diag = (1 - x, y), (x, 1 - y), (1 - x, 1 - y)
        relay_from = (x ^ (1 - c), y ^ c)
        relay_to = (x ^ c, y ^ (1 - c))

        def slot(px, py, pc):
            return out_ref.at[4 * px + 2 * py + pc]

        def copy(k, block, to, src=None):
            return pltpu.make_async_remote_copy(
                src_ref=slot(*block) if src is None else src, dst_ref=slot(*block),
                send_sem=send_sems.at[k], recv_sem=recv_sems.at[k], device_id=to, device_id_type=MESH)

        mine = lambda: pltpu.make_async_copy(x_ref, slot(*me), local_sem)
        first = lambda: [copy(0, me, sibling, src=x_ref), copy(1, me, (*xchip, c), src=x_ref),
                         copy(2, me, (*ychip, c), src=x_ref)]
        relay = lambda: copy(3, (*relay_from, c), (*relay_to, c))
        arrivals = lambda: [copy(1, (*xchip, c), me), copy(2, (*ychip, c), me), copy(3, (*diag, c), me)]
        passed = lambda: [copy(4 + j, (*chip, c), sibling) for j, chip in enumerate((xchip, ychip, diag))]
        from_sibling = lambda: [copy(0, sibling, me)] + [
            copy(4 + j, (*chip, 1 - c), me) for j, chip in enumerate((xchip, ychip, diag))]
        return mine, first, relay, arrivals, passed, from_sibling

    def begin(*refs):
        mine, first, _, _, _, _ = ops(*refs)
        mine().start()
        for cp in first():
            cp.start()

    def middle(*refs):
        _, _, relay, arrivals, passed, _ = ops(*refs)
        for cp in arrivals()[:2]:
            cp.wait_recv()
        relay().start()
        for cp in passed()[:2]:
            cp.start()

    def end(*refs):
        mine, first, relay, arrivals, passed, from_sibling = ops(*refs)
        arrivals()[2].wait_recv()
        passed()[2].start()
        for cp in from_sibling():
            cp.wait_recv()
        for cp in first() + [relay()] + passed():
            cp.wait_send()
        mine().wait()

    return _Exchange([shard], [jax.ShapeDtypeStruct((NDEV, rows, cols), shard.dtype)],
                     [pltpu.SemaphoreType.DMA((7,)), pltpu.SemaphoreType.DMA((7,)), pltpu.SemaphoreType.DMA],
                     begin, middle, end)


def _grad_exchange(grads, bases, rows, strides=None):
    n = len(grads)
    strides = rows if strides is None else strides

    def copies(g_refs, out_refs, scr):
        send_sems, recv_sems, local_sems = scr
        x, y, c = lax.axis_index("x"), lax.axis_index("y"), lax.axis_index("c")
        me = 4 * x + 2 * y + c
        out = []
        for i in range(n):
            r = rows[i]
            src = lambda k: g_refs[i].at[pl.ds(pl.multiple_of(bases[i] + k * strides[i], 16), r), :]
            out.append(pltpu.make_async_copy(src(me), out_refs[i].at[me], local_sems.at[i]))
            for j in range(1, NDEV):
                px, py, pc = x ^ (j >> 2), y ^ ((j >> 1) & 1), c ^ (j & 1)
                out.append(pltpu.make_async_remote_copy(
                    src_ref=src(4 * px + 2 * py + pc), dst_ref=out_refs[i].at[me],
                    send_sem=send_sems.at[i, j - 1], recv_sem=recv_sems.at[i, j - 1],
                    device_id=(px, py, pc), device_id_type=MESH))
        return out

    def begin(*refs):
        for cp in copies(*refs):
            cp.start()

    def end(*refs):
        for cp in copies(*refs):
            cp.wait()

    return _Exchange(list(grads), [jax.ShapeDtypeStruct((NDEV, r, g.shape[1]), g.dtype) for g, r in zip(grads, rows)],
                     [pltpu.SemaphoreType.DMA((n, NDEV - 1)), pltpu.SemaphoreType.DMA((n, NDEV - 1)),
                      pltpu.SemaphoreType.DMA((n,))], begin, None, end)


def _row_gather_exchange(row):
    def copies(ins, outs, scr):
        (r_ref,), (o_ref,), (send_sems, recv_sems, local_sem) = ins, outs, scr
        x, y, c = lax.axis_index("x"), lax.axis_index("y"), lax.axis_index("c")
        me = 4 * x + 2 * y + c
        out = [pltpu.make_async_copy(r_ref, o_ref.at[me], local_sem)]
        for j in range(1, NDEV):
            out.append(pltpu.make_async_remote_copy(
                src_ref=r_ref, dst_ref=o_ref.at[me], send_sem=send_sems.at[j - 1], recv_sem=recv_sems.at[j - 1],
                device_id=(x ^ (j >> 2), y ^ ((j >> 1) & 1), c ^ (j & 1)), device_id_type=MESH))
        return out

    def begin(*refs):
        for cp in copies(*refs):
            cp.start()

    def end(*refs):
        for cp in copies(*refs):
            cp.wait()

    return _Exchange([row], [jax.ShapeDtypeStruct((NDEV,) + row.shape, row.dtype)],
                     [pltpu.SemaphoreType.DMA((NDEV - 1,)), pltpu.SemaphoreType.DMA((NDEV - 1,)), pltpu.SemaphoreType.DMA],
                     begin, None, end)


def _join(a, b):
    na, oa, sa = len(a.inputs), len(a.out_shape), len(a.scratch)
    split = lambda ins, outs, scr: ((ins[:na], outs[:oa], scr[:sa]), (ins[na:], outs[oa:], scr[sa:]))

    def begin(*refs):
        pa, pb = split(*refs)
        a.begin(*pa)
        b.begin(*pb)

    def end(*refs):
        pa, pb = split(*refs)
        a.end(*pa)
        b.end(*pb)

    return _Exchange(a.inputs + b.inputs, a.out_shape + b.out_shape, a.scratch + b.scratch, begin, None, end)


def _exchange_call(name, ex):
    ni, no = len(ex.inputs), len(ex.out_shape)

    def body(*refs):
        parts = (refs[:ni], refs[ni:ni + no], refs[ni + no:])
        ex.begin(*parts)
        if ex.middle is not None:
            ex.middle(*parts)
        ex.end(*parts)

    return pl.pallas_call(body, name=name, out_shape=ex.out_shape, in_specs=[HBM_SPEC] * ni, out_specs=[HBM_SPEC] * no,
                          scratch_shapes=ex.scratch)(*ex.inputs)


def _call(body, *, name, grid, in_specs, out_specs, out_shape, scratch_shapes, ins, carry=None):
    sem = ("arbitrary",) * len(grid)
    if carry is None:
        outs = pl.pallas_call(body, name=name, grid=grid, in_specs=in_specs, out_specs=out_specs, out_shape=out_shape,
                              scratch_shapes=scratch_shapes, compiler_params=_params(*sem))(*ins)
        return outs, None
    n_in, n_out, n_scr = len(ins), len(out_shape), len(scratch_shapes)
    ci, co = len(carry.inputs), len(carry.out_shape)
    total = 1
    for g in grid:
        total *= g

    def wrapped(*refs):
        own_in, ex_in = refs[:n_in], refs[n_in:n_in + ci]
        o0 = n_in + ci
        own_out, ex_out = refs[o0:o0 + n_out], refs[o0 + n_out:o0 + n_out + co]
        s0 = o0 + n_out + co
        own_scr, ex_scr = refs[s0:s0 + n_scr], refs[s0 + n_scr:]
        step = pl.program_id(0)
        for d in range(1, len(grid)):
            step = step * grid[d] + pl.program_id(d)
        parts = (ex_in, ex_out, ex_scr)
        pl.when(step == 0)(lambda: carry.begin(*parts))
        body(*own_in, *own_out, *own_scr)
        if carry.middle is not None:
            pl.when(step == (3 * total) // 4)(lambda: carry.middle(*parts))
        pl.when(step == total - 1)(lambda: carry.end(*parts))

    outs = pl.pallas_call(
        wrapped, name=name, grid=grid, in_specs=list(in_specs) + [HBM_SPEC] * ci,
        out_specs=list(out_specs) + [HBM_SPEC] * co, out_shape=list(out_shape) + list(carry.out_shape),
        scratch_shapes=list(scratch_shapes) + list(carry.scratch), compiler_params=_params(*sem),
    )(*ins, *carry.inputs)
    return outs[:n_out], outs[n_out:]


def _ffn_fwd(name, xin, affine, wall, offs, tm, loss=None, carry=None):
    T = xin.shape[0]
    nt = T // tm

    def body(*refs):
        it = iter(refs)
        x_ref = next(it)
        if affine is not None:
            g_ref, b_ref = next(it), next(it)
        wall_ref = next(it)
        if loss is not None:
            go_ref, bo_ref, tgt_ref = next(it), next(it), next(it)
        xh_ref, r_ref, a_ref, b2_ref, yb_ref = (next(it) for _ in range(5))
        if loss is not None:
            loss_ref = next(it)
        w1, w3, w2, sem = (next(it) for _ in range(4))

        @pl.when(pl.program_id(0) == 0)
        def _():
            _load_rows(wall_ref, [(w1, offs[0], RF), (w3, offs[1], RF), (w2, offs[2], RF)], sem)
            if loss is not None:
                loss_ref[...] = jnp.zeros_like(loss_ref)

        parts = [slice(p, min(p + PART_ROWS, tm)) for p in range(0, tm, PART_ROWS)]
        ys, ybs = [], []
        for rs in parts:
            y = x_ref[rs, :]
            if affine is not None:
                y = y * g_ref[...] + b_ref[...]
            ys.append(y)
            ybs.append(y.astype(BF16))
            yb_ref[rs, :] = ybs[-1]
        ab = [(_dot_nt(yb, w1[...]).astype(BF16), _dot_nt(yb, w3[...]).astype(BF16)) for yb in ybs]
        us = []
        for rs, (a, b) in zip(parts, ab):
            a_ref[rs, :] = a
            b2_ref[rs, :] = b
            af, bf = a.astype(F32), b.astype(F32)
            us.append((af * _sigmoid(af) * bf).astype(BF16))
        fs = [_dot(u, w2[...]) for u in us]
        for rs, y, f in zip(parts, ys, fs):
            xh, r = _ln_fwd(ALPHA * y + 0.5 * f)
            xh_ref[rs, :] = xh
            r_ref[rs, :] = jnp.broadcast_to(r, (xh.shape[0], 128))
            if loss is not None:
                e = xh * go_ref[...] + bo_ref[...] - tgt_ref[rs, :]
                loss_ref[...] += jnp.sum(e * e) * (0.5 / D)

    row = lambda w: pl.BlockSpec((tm, w), lambda i: (i, 0))
    vec = pl.BlockSpec((1, D), lambda i: (0, 0))
    ins, in_specs = [xin], [row(D)]
    if affine is not None:
        ins += list(affine)
        in_specs += [vec, vec]
    ins.append(wall)
    in_specs.append(HBM_SPEC)
    if loss is not None:
        ins += list(loss)
        in_specs += [vec, vec, row(D)]
    out_shape = [jax.ShapeDtypeStruct((T, D), F32), jax.ShapeDtypeStruct((T, 128), F32),
                 jax.ShapeDtypeStruct((T, DFF), BF16), jax.ShapeDtypeStruct((T, DFF), BF16),
                 jax.ShapeDtypeStruct((T, D), BF16)]
    out_specs = [row(D), row(128), row(DFF), row(DFF), row(D)]
    if loss is not None:
        out_shape.append(jax.ShapeDtypeStruct((8, 128), F32))
        out_specs.append(pl.BlockSpec((8, 128), lambda i: (0, 0)))
    return _call(body, name=name, grid=(nt,), in_specs=in_specs, out_specs=out_specs, out_shape=out_shape,
                 scratch_shapes=[pltpu.VMEM((DFF, D), BF16)] * 3 + [pltpu.SemaphoreType.DMA((3 * NDEV,))],
                 ins=ins, carry=carry)


def _ffn_bwd(name, dy_src, xh, r, g, a, b, wall, offs, tm, with_dx=True, carry=None):
    T = xh.shape[0]
    nt = T // tm
    from_loss = dy_src[0] == "loss"

    def body(*refs):
        it = iter(refs)
        if from_loss:
            bo_ref, tgt_ref = next(it), next(it)
        else:
            dy_ref = next(it)
        xh_ref, r_ref, g_ref, a_ref, b_ref, wall_ref = (next(it) for _ in range(6))
        dyin_ref, dab_ref, u_ref, df_ref, dg_ref, db_ref = (next(it) for _ in range(6))
        if with_dx:
            w1, w3 = next(it), next(it)
        w2, sem = next(it), next(it)

        @pl.when(pl.program_id(0) == 0)
        def _():
            _load_rows(wall_ref, ([(w1, offs[0], RF), (w3, offs[1], RF)] if with_dx else []) + [(w2, offs[2], RF)], sem)
            dg_ref[...] = jnp.zeros_like(dg_ref)
            db_ref[...] = jnp.zeros_like(db_ref)

        parts = [slice(p, min(p + PART_ROWS, tm)) for p in range(0, tm, PART_ROWS)]
        gv = g_ref[...]
        dz, df = [], []
        for rs in parts:
            xhv = xh_ref[rs, :]
            if from_loss:
                dy = (xhv * gv + bo_ref[...] - tgt_ref[rs, :]) * (1.0 / D)
            else:
                dy = dy_ref[rs, :]
            dzp, dgp, dbp = _ln_bwd(dy, xhv, r_ref[rs, :1], gv)
            dg_ref[...] += dgp
            db_ref[...] += dbp
            dz.append(dzp)
            df.append((0.5 * dzp).astype(BF16))
            df_ref[rs, :] = df[-1]
        du = [_dot_nt(d, w2[...]) for d in df]
        da, dbb = [], []
        for rs, dup in zip(parts, du):
            af, bf = a_ref[rs, :].astype(F32), b_ref[rs, :].astype(F32)
            s = _sigmoid(af)
            sl = af * s
            u_ref[rs, :] = (sl * bf).astype(BF16)
            da.append((dup * bf * (s * (1.0 + af * (1.0 - s)))).astype(BF16))
            dbb.append((dup * sl).astype(BF16))
            dab_ref[rs, :DFF] = da[-1]
            dab_ref[rs, DFF:] = dbb[-1]
        for rs, dzp, dap, dbp in zip(parts, dz, da, dbb):
            dyin_ref[rs, :] = ALPHA * dzp + _dot(dap, w1[...]) + _dot(dbp, w3[...]) if with_dx else dzp

    row = lambda w: pl.BlockSpec((tm, w), lambda i: (i, 0))
    vec = pl.BlockSpec((1, D), lambda i: (0, 0))
    if from_loss:
        ins, in_specs = [dy_src[1], dy_src[2]], [vec, row(D)]
    else:
        ins, in_specs = [dy_src[1]], [row(D)]
    ins += [xh, r, g, a, b, wall]
    in_specs += [row(D), row(128), vec, row(DFF), row(DFF), HBM_SPEC]
    return _call(
        body, name=name, grid=(nt,), in_specs=in_specs,
        out_specs=[row(D), row(2 * DFF), row(DFF), row(D), vec, vec],
        out_shape=[jax.ShapeDtypeStruct((T, D), F32), jax.ShapeDtypeStruct((T, 2 * DFF), BF16),
                   jax.ShapeDtypeStruct((T, DFF), BF16), jax.ShapeDtypeStruct((T, D), BF16),
                   jax.ShapeDtypeStruct((1, D), F32), jax.ShapeDtypeStruct((1, D), F32)],
        scratch_shapes=[pltpu.VMEM((DFF, D), BF16)] * (3 if with_dx else 1) + [pltpu.SemaphoreType.DMA((3 * NDEV,))],
        ins=ins, carry=carry)


def _ffn_dx(name, dab, dz, wall, offs, tm, carry=None):
    T = dz.shape[0]

    def body(dab_ref, dz_ref, wall_ref, o_ref, w13, sem):
        @pl.when(pl.program_id(0) == 0)
        def _():
            _load_rows(wall_ref, [(w13.at[pl.ds(0, DFF), :], offs[0], RF), (w13.at[pl.ds(DFF, DFF), :], offs[1], RF)], sem)

        for p in range(0, tm, PART_ROWS):
            rs = slice(p, min(p + PART_ROWS, tm))
            o_ref[rs, :] = ALPHA * dz_ref[rs, :] + _dot(dab_ref[rs, :], w13[...])

    row = lambda w: pl.BlockSpec((tm, w), lambda i: (i, 0))
    (out,), ex = _call(
        body, name=name, grid=(T // tm,), in_specs=[row(2 * DFF), row(D), HBM_SPEC], out_specs=[row(D)],
        out_shape=[jax.ShapeDtypeStruct((T, D), F32)],
        scratch_shapes=[pltpu.VMEM((2 * DFF, D), BF16), pltpu.SemaphoreType.DMA((2 * NDEV,))],
        ins=[dab, dz, wall], carry=carry)
    return out, ex


def _wgrad(name, a, b, bn, tk, col0=0, ncols=None, carry=None):
    T = a.shape[0]
    N = a.shape[1] if ncols is None else ncols
    nk = T // tk
    c0 = col0 // bn

    def body(a_ref, b_ref, o_ref, acc):
        k = pl.program_id(1)

        @pl.when(k == 0)
        def _():
            acc[...] = jnp.zeros_like(acc)

        acc[...] += _dot_tn(a_ref[...], b_ref[...])

        @pl.when(k == nk - 1)
        def _():
            o_ref[...] = acc[...].astype(BF16)

    (out,), ex = _call(
        body, name=name, grid=(N // bn, nk),
        in_specs=[pl.BlockSpec((tk, bn), lambda n, k: (k, n + c0)), pl.BlockSpec((tk, D), lambda n, k: (k, 0))],
        out_specs=[pl.BlockSpec((bn, D), lambda n, k: (n, 0))],
        out_shape=[jax.ShapeDtypeStruct((N, D), BF16)],
        scratch_shapes=[pltpu.VMEM((bn, D), F32)], ins=[a, b], carry=carry)
    return out, ex


def _rope_table(T):
    pos = jnp.arange(T, dtype=F32)
    inv_freq = ROPE_THETA ** (-jnp.arange(0, ROPE_DIM, 2, dtype=F32) / ROPE_DIM)
    ang = pos[:, None] * inv_freq[None, :]
    return jnp.pad(jnp.concatenate([jnp.cos(ang), jnp.sin(ang)], axis=1), ((0, 0), (0, 128 - ROPE_DIM)))


def _rope_expand(cs):
    half = ROPE_DIM // 2
    lane = lax.broadcasted_iota(jnp.int32, cs.shape, 1)
    cos = jnp.where(lane < half, cs, 0.0)
    sin = jnp.where((lane >= half) & (lane < ROPE_DIM), cs, 0.0)
    both = lambda t: t + pltpu.roll(t, HEAD_DIM, 1)
    c = jnp.where(jnp.bitwise_and(lane, HEAD_DIM - 1) < ROPE_DIM, both(cos + pltpu.roll(cos, half, 1)), 1.0)
    return c, -both(pltpu.roll(sin, 128 - half, 1)), both(sin)


def _rope(t, c, s1, s2):
    n = t.shape[1] // 128
    ct, s1t, s2t = (jnp.tile(v, (1, n)) for v in (c, s1, s2))
    w = t.shape[1]
    return t * ct + pltpu.roll(t, w - 8, 1) * s1t + pltpu.roll(t, 8, 1) * s2t


def _rope_t(dr, c, s1, s2):
    n = dr.shape[1] // 128
    ct, s1t, s2t = (jnp.tile(v, (1, n)) for v in (c, s1, s2))
    w = dr.shape[1]
    return dr * ct + pltpu.roll(dr * s1t, 8, 1) + pltpu.roll(dr * s2t, w - 8, 1)


_Q, _K, _V = (0, 1024), (1024, 256), (1280, 256)
_HG = (1536, HG_W)
_GATES = (5632, GATE_W)


def _inproj_fwd(name, xh, g, b, wall, b_in, rope, tm, carry=None):
    T = xh.shape[0]

    def body(xh_ref, g_ref, b_ref, wall_ref, bin_ref, cs_ref,
             qkv_ref, kt_ref, vt_ref, hg_ref, gate_ref, yb_ref, w, sem):
        @pl.when(pl.program_id(0) == 0)
        def _():
            _load_rows(wall_ref, [(w, 0, RIN)], sem)

        yb = (xh_ref[...] * g_ref[...] + b_ref[...]).astype(BF16)
        yb_ref[...] = yb
        c, s1, s2 = _rope_expand(cs_ref[...])

        def piece(start, width):
            return _dot_nt(yb, w[start:start + width, :]) + bin_ref[:, start:start + width]

        k = _rope(piece(*_K), c, s1, s2)
        v = piece(*_V)
        qkv_ref[:, 0:1024] = _rope(piece(*_Q), c, s1, s2).astype(BF16)
        qkv_ref[:, 1024:1280] = k.astype(BF16)
        qkv_ref[:, 1280:1536] = v.astype(BF16)
        kt_ref[...] = k.T.astype(BF16)
        vt_ref[...] = v.T.astype(BF16)
        for j in range(4):
            hg_ref[:, 1024 * j:1024 * (j + 1)] = piece(_HG[0] + 1024 * j, 1024)
        for j in range(2):
            gate_ref[:, 1024 * j:1024 * (j + 1)] = piece(_GATES[0] + 1024 * j, 1024).astype(BF16)

    row = lambda wd: pl.BlockSpec((tm, wd), lambda i: (i, 0))
    vec = lambda wd: pl.BlockSpec((1, wd), lambda i: (0, 0))
    colt = pl.BlockSpec((256, tm), lambda i: (0, i))
    return _call(
        body, name=name, grid=(T // tm,),
        in_specs=[row(D), vec(D), vec(D), HBM_SPEC, vec(DIN), row(128)],
        out_specs=[row(QKV_W), colt, colt, row(HG_W), row(GATE_W), row(D)],
        out_shape=[jax.ShapeDtypeStruct((T, QKV_W), BF16), jax.ShapeDtypeStruct((256, T), BF16),
                   jax.ShapeDtypeStruct((256, T), BF16), jax.ShapeDtypeStruct((T, HG_W), F32),
                   jax.ShapeDtypeStruct((T, GATE_W), BF16), jax.ShapeDtypeStruct((T, D), BF16)],
        scratch_shapes=[pltpu.VMEM((DIN, D), BF16), pltpu.SemaphoreType.DMA((NDEV,))],
        ins=[xh, g, b, wall, b_in, rope], carry=carry)


def _inproj_bwd(name, dq, dk, dv, dhg, dgates, dz2, wall, rope, tm, carry=None):
    T = dq.shape[0]

    def body(dq_ref, dk_ref, dv_ref, d0, d1, d2, d3, dga_ref, dgh_ref, dz_ref, wall_ref, cs_ref,
             dy_ref, dproj_ref, dbin_ref, w, sem):
        @pl.when(pl.program_id(0) == 0)
        def _():
            _load_rows(wall_ref, [(w, 0, RIN)], sem)
            dbin_ref[...] = jnp.zeros_like(dbin_ref)

        c, s1, s2 = _rope_expand(cs_ref[...])
        acc = ALPHA * dz_ref[...]
        pieces = [(_Q[0], _rope_t(dq_ref[...], c, s1, s2)), (_K[0], _rope_t(dk_ref[...], c, s1, s2)),
                  (_V[0], dv_ref[...])]
        pieces += [(_HG[0] + 1024 * j, r[...]) for j, r in enumerate((d0, d1, d2, d3))]
        pieces += [(_GATES[0], dga_ref[...]), (_GATES[0] + 1024, dgh_ref[...])]
        for start, val in pieces:
            width = val.shape[1]
            dbin_ref[:, start:start + width] += jnp.sum(val.astype(F32), axis=0, keepdims=True)
            vb = val.astype(BF16)
            dproj_ref[:, start:start + width] = vb
            acc = acc + _dot(vb, w[start:start + width, :])
        dy_ref[...] = acc

    row = lambda wd: pl.BlockSpec((tm, wd), lambda i: (i, 0))
    return _call(
        body, name=name, grid=(T // tm,),
        in_specs=[row(D), row(256), row(256)] + [row(D)] * 4 + [row(D), row(D), row(D), HBM_SPEC, row(128)],
        out_specs=[row(D), row(DIN), pl.BlockSpec((1, DIN), lambda i: (0, 0))],
        out_shape=[jax.ShapeDtypeStruct((T, D), F32), jax.ShapeDtypeStruct((T, DIN), BF16),
                   jax.ShapeDtypeStruct((1, DIN), F32)],
        scratch_shapes=[pltpu.VMEM((DIN, D), BF16), pltpu.SemaphoreType.DMA((NDEV,))],
        ins=[dq, dk, dv, *dhg, *dgates, dz2, wall, rope], carry=carry)


def _halves(t):
    lane = lax.broadcasted_iota(jnp.int32, t.shape, 1)
    low = lane < HEAD_DIM
    sw = pltpu.roll(t, HEAD_DIM, 1)
    zero = jnp.zeros_like(t)
    h0 = (jnp.where(low, t, zero), jnp.where(low, zero, sw))
    h1 = (jnp.where(low, sw, zero), jnp.where(low, zero, t))
    return h0, h1


def _lane_stack(p_ref, c_ref, gp):
    sl = slice(128 * gp, 128 * (gp + 1))
    hp, hc = _halves(p_ref[:, sl].astype(F32)), _halves(c_ref[:, sl].astype(F32))
    return [jnp.concatenate([hp[gg][0], hc[gg][0], hp[gg][1], hc[gg][1]], axis=0).astype(BF16) for gg in range(2)]


def _row_stack(tp_ref, tc_ref, g):
    band = jnp.concatenate([tp_ref[64 * g:64 * (g + 1), :], tc_ref[64 * g:64 * (g + 1), :]], axis=1)
    z = jnp.zeros_like(band)
    return [jnp.concatenate([band, z], axis=0), jnp.concatenate([z, band], axis=0)]


def _fold(prevm, t4, hh):
    return jnp.where(prevm, t4[256 * hh:256 * hh + 128, :], t4[256 * hh + 128:256 * hh + 256, :])


def _unfold(prevm, t):
    return jnp.concatenate([jnp.where(prevm, t, 0.0), jnp.where(prevm, 0.0, t)], axis=0).astype(BF16)


def _attn_softmax(s, kill, sink):
    s = jnp.where(kill, NEG_INF, s)
    m = jnp.maximum(jnp.max(s, axis=0, keepdims=True), sink)
    p = jnp.exp(s - m)
    es = jnp.exp(sink - m)
    inv = 1.0 / (jnp.sum(p, axis=0, keepdims=True) + es)
    return p * inv, es * inv


def _attn_masks(first):
    row = lax.broadcasted_iota(jnp.int32, (ATTN_BLOCK, 2 * ATTN_BLOCK), 0)
    lane = lax.broadcasted_iota(jnp.int32, (ATTN_BLOCK, 2 * ATTN_BLOCK), 1)
    prevm = row > lane % ATTN_BLOCK
    return prevm, jnp.logical_and(first, prevm)


def _pair_rows(ref, g):
    return jnp.concatenate([ref[:, 256 * g:256 * g + 128], ref[:, 256 * g + 128:256 * (g + 1)]], axis=0)


def _pair_sinks(sink_ref, g, hh):
    h0, h1 = 4 * g + hh, 4 * g + 2 + hh
    return jnp.concatenate([jnp.broadcast_to(sink_ref[:, h0:h0 + 1], (1, ATTN_BLOCK)),
                            jnp.broadcast_to(sink_ref[:, h1:h1 + 1], (1, ATTN_BLOCK))], axis=1)


def _attn_fwd(name, qkv, vt, sinks):
    T = qkv.shape[0]
    nbk = min(ATTN_TILE, T // ATTN_BLOCK)
    rows = nbk * ATTN_BLOCK
    scale = HEAD_DIM ** -0.5

    def block(q_ref, kp_ref, kc_ref, vtp_ref, vtc_ref, sink_ref, o_ref, first):
        prevm, kill = _attn_masks(first)
        kst = _lane_stack(kp_ref, kc_ref, 0) + _lane_stack(kp_ref, kc_ref, 1)
        vts = [_row_stack(vtp_ref, vtc_ref, g) for g in range(N_KV_HEADS)]
        s8 = [_dot_nt(kst[g], _pair_rows(q_ref, g)) * scale for g in range(N_KV_HEADS)]
        pu = [[_unfold(prevm, _attn_softmax(_fold(prevm, s8[g], hh), kill, _pair_sinks(sink_ref, g, hh))[0])
               for hh in range(2)] for g in range(N_KV_HEADS)]
        for g in range(N_KV_HEADS):
            ot = _dot(vts[g][0], pu[g][0]) + _dot(vts[g][1], pu[g][1])
            o_ref[:, 256 * g:256 * g + 128] = ot[:, :ATTN_BLOCK].T.astype(BF16)
            o_ref[:, 256 * g + 128:256 * (g + 1)] = ot[:, ATTN_BLOCK:].T.astype(BF16)

    def body(q_ref, kp_ref, kc_ref, vtp_ref, vtc_ref, sink_ref, o_ref):
        for b in range(nbk):
            rs = slice(ATTN_BLOCK * b, ATTN_BLOCK * (b + 1))
            ps = slice(ATTN_BLOCK * (b - 1), ATTN_BLOCK * b)
            block(q_ref.at[rs, :], kp_ref if b == 0 else kc_ref.at[ps, :], kc_ref.at[rs, :],
                  vtp_ref if b == 0 else vtc_ref.at[:, ps], vtc_ref.at[:, rs], sink_ref, o_ref.at[rs, :],
                  pl.program_id(0) == 0 if b == 0 else False)

    prev = lambda i: jnp.maximum(nbk * i - 1, 0)
    return pl.pallas_call(
        body, name=name, grid=(T // rows,),
        in_specs=[pl.BlockSpec((rows, D), lambda i: (i, 0)),
                  pl.BlockSpec((ATTN_BLOCK, 256), lambda i: (prev(i), 4)),
                  pl.BlockSpec((rows, 256), lambda i: (i, 4)),
                  pl.BlockSpec((256, ATTN_BLOCK), lambda i: (0, prev(i))),
                  pl.BlockSpec((256, rows), lambda i: (0, i)),
                  pl.BlockSpec((1, N_Q_HEADS), lambda i: (0, 0))],
        out_specs=pl.BlockSpec((rows, D), lambda i: (i, 0)),
        out_shape=jax.ShapeDtypeStruct((T, D), BF16),
        compiler_params=_params("arbitrary"),
    )(qkv, qkv, qkv, vt, vt, sinks)


ATTN_TILE = 4
ATTN_BWD_ORDER = ((0, 0), (0, 1), (1, 0), (0, 2), (1, 1), (2, 0), (0, 3), (1, 2), (2, 1), (1, 3), (2, 2), (2, 3))


def _attn_bwd(name, qkv, kt, sinks, dya, carry=None):
    T = qkv.shape[0]
    nbk = min(ATTN_TILE, T // ATTN_BLOCK)
    rows = nbk * ATTN_BLOCK
    scale = HEAD_DIM ** -0.5

    def block(q_ref, kp_ref, kc_ref, vp_ref, vc_ref, ktp_ref, ktc_ref, sink_ref, do_ref, dq_ref, first):
        prevm, kill = _attn_masks(first)
        kst = _lane_stack(kp_ref, kc_ref, 0) + _lane_stack(kp_ref, kc_ref, 1)
        vst = _lane_stack(vp_ref, vc_ref, 0) + _lane_stack(vp_ref, vc_ref, 1)
        kts = [_row_stack(ktp_ref, ktc_ref, g) for g in range(N_KV_HEADS)]
        lane = lax.broadcasted_iota(jnp.int32, (2 * ATTN_BLOCK, 128), 1)
        low = lane < HEAD_DIM
        slane = lax.broadcasted_iota(jnp.int32, (1, 128), 1)
        dkz = [jnp.zeros((2 * ATTN_BLOCK, 128), F32) for _ in range(N_KV_HEADS)]
        dvz = [jnp.zeros((2 * ATTN_BLOCK, 128), F32) for _ in range(N_KV_HEADS)]
        dsink = jnp.zeros((1, 128), F32)
        groups = range(N_KV_HEADS)
        qcat, docat, s8, dp8 = {}, {}, {}, {}
        ds_u, p_u, dsinks = {}, {}, []

        def scores(g):
            qcat[g], docat[g] = _pair_rows(q_ref, g), _pair_rows(do_ref, g)
            s8[g] = _dot_nt(kst[g], qcat[g]) * scale
            dp8[g] = _dot_nt(vst[g], docat[g])

        def algebra(g):
            for hh in range(2):
                pn, ps = _attn_softmax(_fold(prevm, s8[g], hh), kill, _pair_sinks(sink_ref, g, hh))
                dp = _fold(prevm, dp8[g], hh)
                delta = jnp.sum(pn * dp, axis=0, keepdims=True)
                sd = ps * delta
                dsinks.append(jnp.where(slane == 4 * g + hh, -jnp.sum(sd[:, :ATTN_BLOCK]), 0.0)
                              + jnp.where(slane == 4 * g + 2 + hh, -jnp.sum(sd[:, ATTN_BLOCK:]), 0.0))
                ds_u[g, hh] = _unfold(prevm, pn * (dp - delta) * scale)
                p_u[g, hh] = _unfold(prevm, pn)

        def grads(g):
            dqt = _dot(kts[g][0], ds_u[g, 0]) + _dot(kts[g][1], ds_u[g, 1])
            dq_ref[:, 256 * g:256 * g + 128] = dqt[:, :ATTN_BLOCK].T
            dq_ref[:, 256 * g + 128:256 * (g + 1)] = dqt[:, ATTN_BLOCK:].T
            for hh in range(2):
                own = low if hh == 0 else jnp.logical_not(low)
                dk_h = jnp.where(own, _dot(ds_u[g, hh], qcat[g]), 0.0)
                dv_h = jnp.where(own, _dot(p_u[g, hh], docat[g]), 0.0)
                if hh != g % 2:
                    dk_h = pltpu.roll(dk_h, HEAD_DIM, 1)
                    dv_h = pltpu.roll(dv_h, HEAD_DIM, 1)
                dkz[g] = dkz[g] + dk_h
                dvz[g] = dvz[g] + dv_h

        for stage, g in ATTN_BWD_ORDER:
            (scores, algebra, grads)[stage](g)
        for t in dsinks:
            dsink = dsink + t
        pairs = range(N_KV_HEADS // 2)
        return dsink, [dkz[2 * gp] + dkz[2 * gp + 1] for gp in pairs], [dvz[2 * gp] + dvz[2 * gp + 1] for gp in pairs]

    def body(q_ref, kp_ref, kc_ref, vp_ref, vc_ref, ktp_ref, ktc_ref, sink_ref, do_ref,
             dq_ref, dk_ref, dv_ref, ds_ref):
        i = pl.program_id(0)

        @pl.when(i == 0)
        def _():
            ds_ref[...] = jnp.zeros_like(ds_ref)

        res = []
        for b in range(nbk):
            rs = slice(ATTN_BLOCK * b, ATTN_BLOCK * (b + 1))
            ps = slice(ATTN_BLOCK * (b - 1), ATTN_BLOCK * b)
            res.append(block(
                q_ref.at[rs, :], kp_ref if b == 0 else kc_ref.at[ps, :], kc_ref.at[rs, :],
                vp_ref if b == 0 else vc_ref.at[ps, :], vc_ref.at[rs, :],
                ktp_ref if b == 0 else ktc_ref.at[:, ps], ktc_ref.at[:, rs], sink_ref, do_ref.at[rs, :],
                dq_ref.at[rs, :], i == 0 if b == 0 else False))
        dsink = res[0][0]
        for r in res[1:]:
            dsink = dsink + r[0]
        ds_ref[...] += dsink
        for gp in range(N_KV_HEADS // 2):
            cols = slice(128 * gp, 128 * (gp + 1))
            for b in range(nbk):
                cur = pl.ds(pl.multiple_of(i * rows + ATTN_BLOCK * b, ATTN_BLOCK), ATTN_BLOCK)
                dkb, dvb = res[b][1][gp][ATTN_BLOCK:, :], res[b][2][gp][ATTN_BLOCK:, :]
                if b + 1 < nbk:
                    dkb, dvb = dkb + res[b + 1][1][gp][:ATTN_BLOCK, :], dvb + res[b + 1][2][gp][:ATTN_BLOCK, :]
                dk_ref[cur, cols] = dkb
                dv_ref[cur, cols] = dvb

            @pl.when(i > 0)
            def _():
                prv = pl.ds(pl.multiple_of(jnp.maximum(i * rows - ATTN_BLOCK, 0), ATTN_BLOCK), ATTN_BLOCK)
                dk_ref[prv, cols] += res[0][1][gp][:ATTN_BLOCK, :]
                dv_ref[prv, cols] += res[0][2][gp][:ATTN_BLOCK, :]

    prev = lambda i: jnp.maximum(nbk * i - 1, 0)
    whole = lambda w: pl.BlockSpec((T, w), lambda i: (0, 0))
    return _call(
        body, name=name, grid=(T // rows,), scratch_shapes=[], ins=[qkv, qkv, qkv, qkv, qkv, kt, kt, sinks, dya],
        carry=carry,
        in_specs=[pl.BlockSpec((rows, D), lambda i: (i, 0)),
                  pl.BlockSpec((ATTN_BLOCK, 256), lambda i: (prev(i), 4)),
                  pl.BlockSpec((rows, 256), lambda i: (i, 4)),
                  pl.BlockSpec((ATTN_BLOCK, 256), lambda i: (prev(i), 5)),
                  pl.BlockSpec((rows, 256), lambda i: (i, 5)),
                  pl.BlockSpec((256, ATTN_BLOCK), lambda i: (0, prev(i))),
                  pl.BlockSpec((256, rows), lambda i: (0, i)),
                  pl.BlockSpec((1, N_Q_HEADS), lambda i: (0, 0)),
                  pl.BlockSpec((rows, D), lambda i: (i, 0))],
        out_specs=[pl.BlockSpec((rows, D), lambda i: (i, 0)), whole(256), whole(256),
                   pl.BlockSpec((1, 128), lambda i: (0, 0))],
        out_shape=[jax.ShapeDtypeStruct((T, D), F32), jax.ShapeDtypeStruct((T, 256), F32),
                   jax.ShapeDtypeStruct((T, 256), F32), jax.ShapeDtypeStruct((1, 128), F32)])


HG_FWD = (256, 8, 512)
HG_BWD = (128, 8, 512)


def _chunk_sum(tri, x):
    w = x.shape[1]
    h1 = x.astype(BF16)
    h2 = (x - h1.astype(F32)).astype(BF16)
    r = _dot(tri, jnp.concatenate([h1, h2], axis=1))
    return r[:, :w] + r[:, w:]


def _chunk_tri(n, upper=False):
    ri = lax.broadcasted_iota(jnp.int32, (n, n), 0)
    ci = lax.broadcasted_iota(jnp.int32, (n, n), 1)
    same_chunk = jnp.bitwise_xor(ri, ci) < HGRN_CHUNK
    return jnp.where(((ri <= ci) if upper else (ri >= ci)) & same_chunk, 1.0, 0.0).astype(BF16)


def _chunks(t):
    return [t[HGRN_CHUNK * c:HGRN_CHUNK * (c + 1), :] for c in range(t.shape[0] // HGRN_CHUNK)]


def _lower_bound(lbl_ref):
    l0, l1 = lbl_ref[0:1, :], lbl_ref[1:2, :]
    m = jnp.maximum(l0, l1)
    e0, e1 = jnp.exp(l0 - m), jnp.exp(l1 - m)
    return e0 / (e0 + e1)


def _heads(t):
    return [t[:, 128 * h:128 * (h + 1)] for h in range(t.shape[1] // 128)]


def _per_head(fn, *wide):
    return jnp.concatenate([fn(*parts) for parts in zip(*[_heads(t) for t in wide])], axis=1)


def _hgrn_sub(fl, qh, vv, lb, tril_b):
    w = fl.shape[1]
    sg = _sigmoid(fl)
    f = lb + (1.0 - lb) * sg
    k = 1.0 - f
    gc = _chunk_sum(tril_b, jnp.log(f))
    gl_c = [t[HGRN_CHUNK - 1:HGRN_CHUNK, :] for t in _chunks(gc)]
    gl = jnp.concatenate([jnp.broadcast_to(g, (HGRN_CHUNK, w)) for g in gl_c], axis=0)
    sq = _sigmoid(qh)
    eg = jnp.exp(gc)
    eng = jnp.exp(-gc)
    elg = jnp.exp(gl - gc)
    qd = qh * sq * eg
    ki = k * eng
    ke = k * elg
    qd_b, ki_b, ke_b, v_b = (t.astype(BF16) for t in (qd, ki, ke, vv))
    am_b = [jnp.where(tril_b > 0, _dot_nt(qq, kk), 0.0).astype(BF16) for qq, kk in zip(_heads(qd_b), _heads(ki_b))]
    return dict(sg=sg, f=f, sq=sq, eg=eg, eng=eng, elg=elg, qd=qd, ki=ki, ke=ke, egl=[jnp.exp(g) for g in gl_c],
                am_b=am_b, qd_b=qd_b, ki_b=ki_b, ke_b=ke_b, v_b=v_b)


def _hgrn_out(q, st_b):
    outs = []
    for h, (qd_h, v_h) in enumerate(zip(_heads(q["qd_b"]), _heads(q["v_b"]))):
        inter = [_dot_nt(qc, s) for qc, s in zip(_chunks(qd_h), st_b[h])]
        outs.append(_dot(q["am_b"][h], v_h) + jnp.concatenate(inter, axis=0))
    return jnp.concatenate(outs, axis=1)


def _head_rms(o):
    return _per_head(lambda t: jnp.broadcast_to(
        lax.rsqrt(jnp.mean(t * t, axis=-1, keepdims=True) + RMS_EPS), t.shape), o)


def _hgrn_fwd(name, hg, lb_logits, norm_g, cfg):
    T = hg.shape[0]
    sub_rows, hps, th = min(cfg[0], T), cfg[1], min(cfg[2], T)
    cps = sub_rows // HGRN_CHUNK
    nc = th // HGRN_CHUNK

    def body(fl_ref, qh_ref, v_ref, og_ref, lbl_ref, ng_ref, y_ref, ss_ref, st):
        @pl.when(pl.program_id(1) == 0)
        def _():
            st[...] = jnp.zeros_like(st)

        lbs = _lower_bound(lbl_ref)
        ng = jnp.tile(ng_ref[...], (1, hps))
        tril_b = _chunk_tri(sub_rows)

        def sub(si, carry):
            rows = pl.ds(pl.multiple_of(si * sub_rows, sub_rows), sub_rows)
            q = _hgrn_sub(fl_ref[rows, :], qh_ref[rows, :], v_ref[rows, :], lbs, tril_b)
            v_hc = [_chunks(t) for t in _heads(q["v_b"])]
            k_hc = [_chunks(t) for t in _heads(q["ke_b"])]
            s = [st[h] for h in range(hps)]
            st_b = [[] for _ in range(hps)]
            for c in range(cps):
                egl = _heads(q["egl"][c])
                for h in range(hps):
                    ss_ref[h, si * cps + c] = s[h]
                    st_b[h].append(s[h].astype(BF16))
                    s[h] = s[h] * egl[h] + _dot_tn(v_hc[h][c], k_hc[h][c])
            for h in range(hps):
                st[h] = s[h]
            o = _hgrn_out(q, st_b)
            og = og_ref[rows, :]
            y_ref[rows, :] = (o * _head_rms(o) * ng * (og * _sigmoid(og))).astype(BF16)
            return carry

        lax.fori_loop(0, th // sub_rows, sub, 0)

    wd = 128 * hps
    col = lambda j: pl.BlockSpec((th, wd), lambda h, t: (t, (8 // hps) * j + h))
    return pl.pallas_call(
        body, name=name, grid=(HGRN_HEADS // hps, T // th),
        in_specs=[col(0), col(1), col(2), col(3), pl.BlockSpec((2, wd), lambda h, t: (0, h)),
                  pl.BlockSpec((1, 128), lambda h, t: (0, 0))],
        out_specs=[pl.BlockSpec((th, wd), lambda h, t: (t, h)),
                   pl.BlockSpec((hps, nc, 128, 128), lambda h, t: (h, t, 0, 0))],
        out_shape=[jax.ShapeDtypeStruct((T, D), BF16),
                   jax.ShapeDtypeStruct((HGRN_HEADS, T // HGRN_CHUNK, 128, 128), F32)],
        scratch_shapes=[pltpu.VMEM((hps, 128, 128), F32)],
        compiler_params=_params("parallel", "arbitrary"),
    )(hg, hg, hg, hg, lb_logits, norm_g)


def _hgrn_bwd(name, hg, lb_logits, norm_g, sstart, dyh, cfg, carry=None):
    T = hg.shape[0]
    sub_rows, hps, th = min(cfg[0], T), cfg[1], min(cfg[2], T)
    cps = sub_rows // HGRN_CHUNK
    nc = th // HGRN_CHUNK
    nt = T // th
    wd = 128 * hps

    def body(fl_ref, qh_ref, v_ref, og_ref, lbl_ref, ng_ref, ss_ref, dy_ref,
             dfl_ref, dqh_ref, dv_ref, dog_ref, dlb_ref, dng_ref, dst):
        @pl.when(pl.program_id(1) == 0)
        def _():
            dst[...] = jnp.zeros_like(dst)
            dlb_ref[...] = jnp.zeros_like(dlb_ref)
            dng_ref[...] = jnp.zeros_like(dng_ref)

        lb = _lower_bound(lbl_ref)
        ng = jnp.tile(ng_ref[...], (1, hps))
        last = lax.broadcasted_iota(jnp.int32, (HGRN_CHUNK, wd), 0) == HGRN_CHUNK - 1
        nsub = th // sub_rows
        cat0 = lambda parts: jnp.concatenate(parts, axis=0)
        tril_b, triu_b = _chunk_tri(sub_rows), _chunk_tri(sub_rows, upper=True)

        def sub(step, carry):
            si = nsub - 1 - step
            rows = pl.ds(pl.multiple_of(si * sub_rows, sub_rows), sub_rows)
            qh, og, dy = qh_ref[rows, :], og_ref[rows, :], dy_ref[rows, :]
            q = _hgrn_sub(fl_ref[rows, :], qh, v_ref[rows, :], lb, tril_b)
            s_in = [[ss_ref[h, si * cps + c] for c in range(cps)] for h in range(hps)]
            st_b = [[s.astype(BF16) for s in row] for row in s_in]
            o = _hgrn_out(q, st_b)
            rr = _head_rms(o)
            oh = o * rr
            sog = _sigmoid(og)
            dog_ref[rows, :] = (dy * oh * ng * (sog * (1.0 + og * (1.0 - sog)))).astype(BF16)
            don = dy * (og * sog)
            dng_w = jnp.sum(don * oh, axis=0, keepdims=True)
            dd = don * ng
            mean_h = _per_head(lambda t: jnp.broadcast_to(jnp.mean(t, axis=-1, keepdims=True), t.shape), dd * oh)
            do_b = (rr * (dd - oh * mean_h)).astype(BF16)
            do_h, qd_h, ki_h, ke_h, v_h = (_heads(t) for t in (do_b, q["qd_b"], q["ki_b"], q["ke_b"], q["v_b"]))
            da_b = [jnp.where(tril_b > 0, _dot_nt(do_h[h], v_h[h]), 0.0).astype(BF16) for h in range(hps)]
            do_c, qd_c, ke_c, v_c = ([_chunks(t) for t in hs] for hs in (do_h, qd_h, ke_h, v_h))
            dsp = [[None] * cps for _ in range(hps)]
            dgl_dec = [[None] * cps for _ in range(hps)]
            d = [dst[h] for h in range(hps)]
            for c in reversed(range(cps)):
                egl = _heads(q["egl"][c])
                for h in range(hps):
                    dsp[h][c] = d[h]
                    dgl_dec[h][c] = jnp.sum(d[h] * s_in[h][c], axis=0, keepdims=True) * egl[h]
                    d[h] = d[h] * egl[h] + _dot_tn(do_c[h][c], qd_c[h][c])
            for h in range(hps):
                dst[h] = d[h]
                dng_ref[h] += dng_w[:, 128 * h:128 * (h + 1)]
            dsp_b = [[t.astype(BF16) for t in row] for row in dsp]
            dv_ref[rows, :] = jnp.concatenate(
                [_dot_tn(q["am_b"][h], do_h[h]) + cat0([_dot_nt(ke_c[h][c], dsp_b[h][c]) for c in range(cps)])
                 for h in range(hps)], axis=1).astype(BF16)
            dqd = jnp.concatenate(
                [_dot(da_b[h], ki_h[h]) + cat0([_dot(do_c[h][c], st_b[h][c]) for c in range(cps)])
                 for h in range(hps)], axis=1)
            dki = jnp.concatenate([_dot_tn(da_b[h], qd_h[h]) for h in range(hps)], axis=1)
            dke = jnp.concatenate([cat0([_dot(v_c[h][c], dsp_b[h][c]) for c in range(cps)]) for h in range(hps)], axis=1)
            dkk = dke * q["ke"]
            dgl = [jnp.sum(t, axis=0, keepdims=True) + jnp.concatenate([dgl_dec[h][c] for h in range(hps)], axis=1)
                   for c, t in enumerate(_chunks(dkk))]
            dgc = dqd * q["qd"] - dki * q["ki"] - dkk + cat0([jnp.where(last, g, 0.0) for g in dgl])
            dlf = _chunk_sum(triu_b, dgc)
            df = dlf / q["f"] - (dki * q["eng"] + dke * q["elg"])
            sg = q["sg"]
            dlb_ref[...] += jnp.sum(df * (1.0 - sg), axis=0, keepdims=True)
            dfl_ref[rows, :] = (df * (1.0 - lb) * sg * (1.0 - sg)).astype(BF16)
            sq = q["sq"]
            dqh_ref[rows, :] = (dqd * q["eg"] * (sq * (1.0 + qh * (1.0 - sq)))).astype(BF16)
            return carry

        lax.fori_loop(0, nsub, sub, 0)
    col = lambda j: pl.BlockSpec((th, wd), lambda h, t: (nt - 1 - t, (8 // hps) * j + h))
    out = pl.BlockSpec((th, wd), lambda h, t: (nt - 1 - t, h))
    return _call(
        body, name=name, grid=(HGRN_HEADS // hps, nt),
        in_specs=[col(0), col(1), col(2), col(3), pl.BlockSpec((2, wd), lambda h, t: (0, h)),
                  pl.BlockSpec((1, 128), lambda h, t: (0, 0)),
                  pl.BlockSpec((hps, nc, 128, 128), lambda h, t: (h, nt - 1 - t, 0, 0)), out],
        out_specs=[out, out, out, out, pl.BlockSpec((1, wd), lambda h, t: (0, h)),
                   pl.BlockSpec((hps, 1, 128), lambda h, t: (h, 0, 0))],
        out_shape=[jax.ShapeDtypeStruct((T, D), BF16)] * 4 + [jax.ShapeDtypeStruct((1, D), F32),
                                                              jax.ShapeDtypeStruct((HGRN_HEADS, 1, 128), F32)],
        scratch_shapes=[pltpu.VMEM((hps, 128, 128), F32)],
        ins=[hg, hg, hg, hg, lb_logits, norm_g, sstart, dyh], carry=carry)


def _mix_fwd(name, ya, yh, gates, xh1, g1, b1, wall, tm):
    T = ya.shape[0]

    def body(ya_ref, yh_ref, ga_ref, gh_ref, xh_ref, g_ref, b_ref, wall_ref,
             xo_ref, r_ref, pa_ref, ph_ref, mg_ref, wpa, wph, wo, sem):
        @pl.when(pl.program_id(0) == 0)
        def _():
            _load_rows(wall_ref, [(wpa, RIN, RP), (wph, RIN + RP, RP), (wo, RIN + 2 * RP, RP)], sem)

        parts = [slice(p, min(p + PART_ROWS, tm)) for p in range(0, tm, PART_ROWS)]
        pas = [_dot(ya_ref[rs, :], wpa[...]).astype(BF16) for rs in parts]
        phs = [_dot(yh_ref[rs, :], wph[...]).astype(BF16) for rs in parts]
        mgs = []
        for rs, pa, ph in zip(parts, pas, phs):
            pa_ref[rs, :] = pa
            ph_ref[rs, :] = ph
            sa, sh = _sigmoid(ga_ref[rs, :].astype(F32)), _sigmoid(gh_ref[rs, :].astype(F32))
            mgs.append((sa * pa.astype(F32) + sh * ph.astype(F32)).astype(BF16))
            mg_ref[rs, :] = mgs[-1]
        mixes = [_dot(mg, wo[...]) for mg in mgs]
        for rs, mix in zip(parts, mixes):
            y1 = xh_ref[rs, :] * g_ref[...] + b_ref[...]
            xh2, r = _ln_fwd(ALPHA * y1 + mix)
            xo_ref[rs, :] = xh2
            r_ref[rs, :] = jnp.broadcast_to(r, (xh2.shape[0], 128))

    row = lambda w: pl.BlockSpec((tm, w), lambda i: (i, 0))
    vec = pl.BlockSpec((1, D), lambda i: (0, 0))
    return pl.pallas_call(
        body, name=name, grid=(T // tm,),
        in_specs=[row(D), row(D), pl.BlockSpec((tm, D), lambda i: (i, 0)), pl.BlockSpec((tm, D), lambda i: (i, 1)),
                  row(D), vec, vec, HBM_SPEC],
        out_specs=[row(D), row(128), row(D), row(D), row(D)],
        out_shape=[jax.ShapeDtypeStruct((T, D), F32), jax.ShapeDtypeStruct((T, 128), F32)]
        + [jax.ShapeDtypeStruct((T, D), BF16)] * 3,
        scratch_shapes=[pltpu.VMEM((D, D), BF16)] * 3 + [pltpu.SemaphoreType.DMA((3 * NDEV,))],
        compiler_params=_params("arbitrary"),
    )(ya, yh, gates, gates, xh1, g1, b1, wall)


def _mix_bwd(name, dy2, xh2, r2, g2, gates, pa, ph, wall, tm, carry=None):
    T = dy2.shape[0]

    def body(dy_ref, xh_ref, r_ref, g_ref, ga_ref, gh_ref, pa_ref, ph_ref, wall_ref,
             dz_ref, dmix_ref, dpa_ref, dph_ref, dga_ref, dgh_ref, dya_ref, dyh_ref, dg_ref, db_ref,
             wpa, wph, wo, sem):
        @pl.when(pl.program_id(0) == 0)
        def _():
            _load_rows(wall_ref, [(wpa, RIN, RP), (wph, RIN + RP, RP), (wo, RIN + 2 * RP, RP)], sem)
            dg_ref[...] = jnp.zeros_like(dg_ref)
            db_ref[...] = jnp.zeros_like(db_ref)

        parts = [slice(p, min(p + PART_ROWS, tm)) for p in range(0, tm, PART_ROWS)]
        dmix = []
        for rs in parts:
            dz, dgp, dbp = _ln_bwd(dy_ref[rs, :], xh_ref[rs, :], r_ref[rs, :1], g_ref[...])
            dg_ref[...] += dgp
            db_ref[...] += dbp
            dz_ref[rs, :] = dz
            dmix.append(dz.astype(BF16))
            dmix_ref[rs, :] = dmix[-1]
        dmgs = [_dot_nt(d, wo[...]) for d in dmix]
        dpas, dphs = [], []
        for rs, dmg in zip(parts, dmgs):
            sa, sh = _sigmoid(ga_ref[rs, :].astype(F32)), _sigmoid(gh_ref[rs, :].astype(F32))
            dga_ref[rs, :] = (dmg * pa_ref[rs, :].astype(F32) * sa * (1.0 - sa)).astype(BF16)
            dgh_ref[rs, :] = (dmg * ph_ref[rs, :].astype(F32) * sh * (1.0 - sh)).astype(BF16)
            dpas.append((dmg * sa).astype(BF16))
            dphs.append((dmg * sh).astype(BF16))
            dpa_ref[rs, :] = dpas[-1]
            dph_ref[rs, :] = dphs[-1]
        for rs, dpa, dph in zip(parts, dpas, dphs):
            dya_ref[rs, :] = _dot_nt(dpa, wpa[...]).astype(BF16)
            dyh_ref[rs, :] = _dot_nt(dph, wph[...])

    row = lambda w: pl.BlockSpec((tm, w), lambda i: (i, 0))
    vec = pl.BlockSpec((1, D), lambda i: (0, 0))
    return _call(
        body, name=name, grid=(T // tm,),
        in_specs=[row(D), row(D), row(128), vec, pl.BlockSpec((tm, D), lambda i: (i, 0)),
                  pl.BlockSpec((tm, D), lambda i: (i, 1)), row(D), row(D), HBM_SPEC],
        out_specs=[row(D)] * 8 + [vec, vec],
        out_shape=[jax.ShapeDtypeStruct((T, D), F32)] + [jax.ShapeDtypeStruct((T, D), BF16)] * 6
        + [jax.ShapeDtypeStruct((T, D), F32)] + [jax.ShapeDtypeStruct((1, D), F32)] * 2,
        scratch_shapes=[pltpu.VMEM((D, D), BF16)] * 3 + [pltpu.SemaphoreType.DMA((3 * NDEV,))],
        ins=[dy2, xh2, r2, g2, gates, gates, pa, ph, wall], carry=carry)


def _adam(w, g, m, v):
    m = ADAM_B1 * m + (1.0 - ADAM_B1) * g
    v = ADAM_B2 * v + (1.0 - ADAM_B2) * (g * g)
    m_hat = m / (1.0 - ADAM_B1 ** ADAM_STEP)
    v_hat = v / (1.0 - ADAM_B2 ** ADAM_STEP)
    delta = -ADAM_LR * (m_hat / (jnp.sqrt(v_hat) + ADAM_EPS) + ADAM_WD * w)
    return delta, m, v


def _grad_steps(name, items, cb, carry=None):
    n = len(items)
    pieces = [len(it[0]) for it in items]

    def body(*refs):
        it = iter(refs[:sum(pieces) + 3 * n])
        outs = refs[sum(pieces) + 3 * n:]
        for i in range(n):
            accs = []
            for _ in range(pieces[i]):
                r_ref = next(it)
                acc = r_ref[0].astype(F32)
                for k in range(1, NDEV):
                    acc = acc + r_ref[k].astype(F32)
                accs.append(acc)
            acc = accs[0] if len(accs) == 1 else jnp.concatenate(accs, axis=0)
            w_ref, m_ref, v_ref = next(it), next(it), next(it)
            g_ref, d_ref, mo_ref, vo_ref = outs[4 * i:4 * i + 4]
            g_ref[...] = acc
            d, mm, vv = _adam(w_ref[...], acc, m_ref[...], v_ref[...])
            d_ref[...] = d
            mo_ref[...] = mm
            vo_ref[...] = vv

    in_specs, out_specs, out_shape, ins = [], [], [], []
    for recv, w, m, v in items:
        blk = pl.BlockSpec((w.shape[0], cb), lambda j: (0, j))
        in_specs += [pl.BlockSpec((NDEV, rp.shape[1], cb), lambda j: (0, 0, j)) for rp in recv] + [blk, blk, blk]
        out_specs += [blk] * 4
        out_shape += [jax.ShapeDtypeStruct(w.shape, F32)] * 4
        ins += list(recv) + [w, m, v]
    outs, ex = _call(body, name=name, grid=(D // cb,), in_specs=in_specs, out_specs=out_specs, out_shape=out_shape,
                     scratch_shapes=[], ins=ins, carry=carry)
    return [tuple(outs[4 * i:4 * i + 4]) for i in range(n)], ex


_SMALL = [("ln1_g", D), ("ln1_b", D), ("ln2_g", D), ("ln2_b", D), ("ln3_g", D), ("ln3_b", D),
          ("b_in", DIN), ("lb", D), ("attn_sinks", 128), ("hgrn_norm_g", 128), ("loss", 128)]
_SMALL_OFF = {}
_o = 0
for _n, _w in _SMALL:
    _SMALL_OFF[_n] = (_o, _w)
    _o += _w
PACK = _o


def _small_step(name, slots, small_w, small_m, small_v):
    names = ["ln1_g", "ln1_b", "ln2_g", "ln2_b", "ln3_g", "ln3_b", "b_in", "attn_sinks", "hgrn_lb_logits", "hgrn_norm_g"]
    np_ = len(names)

    def body(*refs):
        s_ref = refs[0]
        w_refs = refs[1:1 + np_]
        m_refs = refs[1 + np_:1 + 2 * np_]
        v_refs = refs[1 + 2 * np_:1 + 3 * np_]
        loss_ref, outs = refs[1 + 3 * np_], refs[2 + 3 * np_:]
        tot = s_ref[0]
        for k in range(1, NDEV):
            tot = tot + s_ref[k]

        def part(n):
            o, w = _SMALL_OFF[n]
            return tot[:, o:o + w]

        loss_ref[...] = part("loss")
        for i, n in enumerate(names):
            w = w_refs[i][...]
            if n == "hgrn_lb_logits":
                m_ = jnp.maximum(w[0:1, :], w[1:2, :])
                e0, e1 = jnp.exp(w[0:1, :] - m_), jnp.exp(w[1:2, :] - m_)
                p0 = e0 / (e0 + e1)
                t = p0 * (1.0 - p0) * part("lb")
                g = jnp.concatenate([t, -t], axis=0)
            elif n == "attn_sinks":
                g = part(n)[:, :N_Q_HEADS]
            else:
                g = part(n)
            d, mm, vv = _adam(w, g, m_refs[i][...], v_refs[i][...])
            outs[4 * i][...] = g
            outs[4 * i + 1][...] = d
            outs[4 * i + 2][...] = mm
            outs[4 * i + 3][...] = vv

    out_shape = [jax.ShapeDtypeStruct((1, 128), F32)]
    for n in names:
        out_shape += [jax.ShapeDtypeStruct(small_w[n].shape, F32)] * 4
    return pl.pallas_call(
        body, name=name, out_shape=out_shape,
        compiler_params=pltpu.CompilerParams(vmem_limit_bytes=VMEM_LIMIT),
    )(slots, *[small_w[n] for n in names], *[small_m[n] for n in names], *[small_v[n] for n in names]), names


def _tile(T, pref):
    return min(T, pref)


def kernel(x, ln1_g, ln1_b, ffn1_w1, ffn1_w3, ffn1_w2, ln2_g, ln2_b, w_in, b_in, attn_sinks, hgrn_lb_logits, hgrn_norm_g, w_proj_attn, w_proj_hgrn, w_out, ln3_g, ln3_b, ffn2_w1, ffn2_w3, ffn2_w2, loss_target, m_ln1_g, m_ln1_b, m_ffn1_w1, m_ffn1_w3, m_ffn1_w2, m_ln2_g, m_ln2_b, m_w_in, m_b_in, m_attn_sinks, m_hgrn_lb_logits, m_hgrn_norm_g, m_w_proj_attn, m_w_proj_hgrn, m_w_out, m_ln3_g, m_ln3_b, m_ffn2_w1, m_ffn2_w3, m_ffn2_w2, v_ln1_g, v_ln1_b, v_ffn1_w1, v_ffn1_w3, v_ffn1_w2, v_ln2_g, v_ln2_b, v_w_in, v_b_in, v_attn_sinks, v_hgrn_lb_logits, v_hgrn_norm_g, v_w_proj_attn, v_w_proj_hgrn, v_w_out, v_ln3_g, v_ln3_b, v_ffn2_w1, v_ffn2_w3, v_ffn2_w2):
    T = x.shape[1]
    xs = x[0]
    tgt = loss_target[0]
    tm = _tile(T, 256)
    tm2 = _tile(T, 512)
    tk = _tile(T, 2048)

    t_bf = lambda w: w[0].T.astype(BF16)
    n_bf = lambda w: w[0].astype(BF16)
    ffn_shard = lambda w1, w3, w2: jnp.concatenate([t_bf(w1), t_bf(w3), n_bf(w2)], axis=0)
    mix_shard = jnp.concatenate([t_bf(w_in), n_bf(w_proj_attn), n_bf(w_proj_hgrn), n_bf(w_out)], axis=0)
    (ffn1_all,) = _exchange_call("gather_ffn1", _relay_gather_exchange(ffn_shard(ffn1_w1, ffn1_w3, ffn1_w2)))
    ffn_offs = (0, RF, 2 * RF)
    rope = _rope_table(T)

    (xh1, r1, a1, b1, xb0), (mix_all,) = _ffn_fwd("ffn1_fwd", xs, None, ffn1_all, ffn_offs, tm2,
                                                  carry=_relay_gather_exchange(mix_shard))
    (qkv, kt, vt, hg, gates, y1b), (ffn2_all,) = _inproj_fwd(
        "inproj_fwd", xh1, ln1_g, ln1_b, mix_all, b_in, rope, tm2,
        carry=_relay_gather_exchange(ffn_shard(ffn2_w1, ffn2_w3, ffn2_w2)))
    ya = _attn_fwd("attn_fwd", qkv, vt, attn_sinks)
    yh, sstart = _hgrn_fwd("hgrn_fwd", hg, hgrn_lb_logits, hgrn_norm_g, HG_FWD)
    xh2, r2, pa, ph, merged = _mix_fwd("mix_fwd", ya, yh, gates, xh1, ln1_g, ln1_b, mix_all, tm2)
    (xh3, r3, a2, b2, y2b, loss_part), _ = _ffn_fwd("ffn2_fwd", xh2, (ln2_g, ln2_b), ffn2_all, ffn_offs, tm2,
                                                    loss=(ln3_g, ln3_b, tgt))

    (dy2, dab2, u2, df2, dg3, db3), _ = _ffn_bwd("ffn2_bwd", ("loss", ln3_b, tgt), xh3, r3, ln3_g, a2, b2, ffn2_all,
                                                 ffn_offs, tm)
    recv = {}
    g_ffn2_13, _ = _wgrad("wgrad_ffn2_w13", dab2, y2b, DFF // 2, tk)
    g_ffn2_2, _ = _wgrad("wgrad_ffn2_w2", u2, df2, DFF // 2, tk)
    (dz2, dmix, dpa, dph, dga, dgh, dya, dyh, dg2, db2), _ = _mix_bwd(
        "mix_bwd", dy2, xh2, r2, ln2_g, gates, pa, ph, mix_all, tm2)
    g_wo, _ = _wgrad("wgrad_w_out", merged, dmix, D, tk)
    g_pa, _ = _wgrad("wgrad_w_pa", ya, dpa, D, tk)
    g_ph, _ = _wgrad("wgrad_w_ph", yh, dph, D, tk)
    (dq, dk, dv, dsink), _ = _attn_bwd("attn_bwd", qkv, kt, attn_sinks, dya)
    (dfl, dqh, dih, dog, dlb, dng), (recv["ffn2_w2"], recv["w_proj_attn"], recv["w_proj_hgrn"], recv["w_out"]) = \
        _hgrn_bwd("hgrn_bwd", hg, hgrn_lb_logits, hgrn_norm_g, sstart, dyh, HG_BWD,
                  carry=_grad_exchange([g_ffn2_2, g_pa, g_ph, g_wo], [0, 0, 0, 0], [RF, RP, RP, RP]))
    (dy1, dproj, dbin), (recv["ffn2_w1"], recv["ffn2_w3"]) = _inproj_bwd(
        "inproj_bwd", dq, dk, dv, (dfl, dqh, dih, dog), (dga, dgh), dz2, mix_all, rope, tm2,
        carry=_grad_exchange([g_ffn2_13, g_ffn2_13], [0, DFF], [RF, RF]))
    g_win, _ = _wgrad("wgrad_w_in", dproj, y1b, DIN // 4, tk)
    win_rows = (400, 288, 272)
    win_base = (0, 400, 688)
    (dz1, dab1, u1, df1, dg1, db1), (rw0,) = _ffn_bwd(
        "ffn1_bwd", ("dy", dy1), xh1, r1, ln1_g, a1, b1, ffn1_all, ffn_offs, tm2, with_dx=False,
        carry=_grad_exchange([g_win], win_base[:1], win_rows[:1], [RIN]))
    g_ffn1_2, (rw1,) = _wgrad("wgrad_ffn1_w2", u1, df1, DFF // 2, tk,
                              carry=_grad_exchange([g_win], win_base[1:2], win_rows[1:2], [RIN]))
    g_ffn1_1, (recv["ffn1_w2"],) = _wgrad("wgrad_ffn1_w1", dab1, xb0, DFF // 2, tk, 0, DFF,
                                          carry=_grad_exchange([g_ffn1_2], [0], [RF]))
    g_ffn1_3, (recv["ffn1_w1"],) = _wgrad("wgrad_ffn1_w3", dab1, xb0, DFF // 2, tk, DFF, DFF,
                                          carry=_grad_exchange([g_ffn1_1], [0], [RF]))
    parts = {"ln1_g": dg1, "ln1_b": db1, "ln2_g": dg2, "ln2_b": db2, "ln3_g": dg3, "ln3_b": db3, "b_in": dbin,
             "lb": dlb, "attn_sinks": dsink, "hgrn_norm_g": jnp.sum(dng, axis=0), "loss": loss_part[0:1, :]}
    packed = jnp.concatenate([parts[n] for n, _ in _SMALL], axis=1)
    gx, (recv["ffn1_w3"], rw2, small_slots) = _ffn_dx(
        "ffn1_dx", dab1, dz1, ffn1_all, ffn_offs, tm2,
        carry=_join(_grad_exchange([g_ffn1_3, g_win], [0, win_base[2]], [RF, win_rows[2]], [RF, RIN]),
                    _row_gather_exchange(packed)))
    recv["w_in"] = [rw0, rw1, rw2]

    big = [("ffn1_w1", ffn1_w1, m_ffn1_w1, v_ffn1_w1, True), ("ffn1_w3", ffn1_w3, m_ffn1_w3, v_ffn1_w3, True),
           ("ffn1_w2", ffn1_w2, m_ffn1_w2, v_ffn1_w2, False), ("w_in", w_in, m_w_in, v_w_in, True),
           ("w_proj_attn", w_proj_attn, m_w_proj_attn, v_w_proj_attn, False),
           ("w_proj_hgrn", w_proj_hgrn, m_w_proj_hgrn, v_w_proj_hgrn, False),
           ("w_out", w_out, m_w_out, v_w_out, False),
           ("ffn2_w1", ffn2_w1, m_ffn2_w1, v_ffn2_w1, True), ("ffn2_w3", ffn2_w3, m_ffn2_w3, v_ffn2_w3, True),
           ("ffn2_w2", ffn2_w2, m_ffn2_w2, v_ffn2_w2, False)]
    view = lambda t, transposed: t[0].T if transposed else t[0]
    back = lambda t, transposed: t.T[None] if transposed else t[None]
    slots = lambda n: recv[n] if isinstance(recv[n], list) else [recv[n]]
    stepped, _ = _grad_steps("steps", [(slots(n), view(w, tr), view(m, tr), view(v, tr)) for n, w, m, v, tr in big], 128)
    res = {n: tuple(back(t, tr) for t in outs) for (n, _, _, _, tr), outs in zip(big, stepped)}

    small_w = dict(ln1_g=ln1_g, ln1_b=ln1_b, ln2_g=ln2_g, ln2_b=ln2_b, ln3_g=ln3_g, ln3_b=ln3_b, b_in=b_in,
                   attn_sinks=attn_sinks, hgrn_lb_logits=hgrn_lb_logits, hgrn_norm_g=hgrn_norm_g)
    small_m = dict(ln1_g=m_ln1_g, ln1_b=m_ln1_b, ln2_g=m_ln2_g, ln2_b=m_ln2_b, ln3_g=m_ln3_g, ln3_b=m_ln3_b,
                   b_in=m_b_in, attn_sinks=m_attn_sinks, hgrn_lb_logits=m_hgrn_lb_logits, hgrn_norm_g=m_hgrn_norm_g)
    small_v = dict(ln1_g=v_ln1_g, ln1_b=v_ln1_b, ln2_g=v_ln2_g, ln2_b=v_ln2_b, ln3_g=v_ln3_g, ln3_b=v_ln3_b,
                   b_in=v_b_in, attn_sinks=v_attn_sinks, hgrn_lb_logits=v_hgrn_lb_logits, hgrn_norm_g=v_hgrn_norm_g)
    outs, names = _small_step("small_step", small_slots, small_w, small_m, small_v)
    loss = outs[0][0, 0]
    for i, n in enumerate(names):
        res[n] = tuple(outs[1 + 4 * i:5 + 4 * i])

    order = ["ln1_g", "ln1_b", "ffn1_w1", "ffn1_w3", "ffn1_w2", "ln2_g", "ln2_b", "w_in", "b_in", "attn_sinks",
             "hgrn_lb_logits", "hgrn_norm_g", "w_proj_attn", "w_proj_hgrn", "w_out", "ln3_g", "ln3_b",
             "ffn2_w1", "ffn2_w3", "ffn2_w2"]
    return (loss, gx[None], *[res[n][0] for n in order], *[res[n][1] for n in order],
            *[res[n][2] for n in order], *[res[n][3] for n in order])
```
